```python
import jax, jax.numpy as jnp
from jax import lax
import numpy as np

D_MODEL = 1024
BATCH = 8
SEQ = 2048
DEPTH = 1

CONV_GROUPS = 16
CONV_GROUP_DIM = 64
D_CONV = CONV_GROUPS * CONV_GROUP_DIM
CONV_WIDTH = 3
N_Q_HEADS = 16
N_KV_HEADS = 2
HEAD_DIM = 64
Q_PER_KV = N_Q_HEADS // N_KV_HEADS
D_ATTN = N_Q_HEADS * HEAD_DIM
D_KV = N_KV_HEADS * HEAD_DIM
WINDOW = 128
BLOCK = 128
D_MIX = D_CONV + D_ATTN
D_IN_PROJ = 4 * D_CONV + 2 * D_ATTN + 2 * D_KV
SPLIT_POINTS = (D_CONV, 2 * D_CONV, 3 * D_CONV, 4 * D_CONV,
                4 * D_CONV + D_ATTN,
                4 * D_CONV + D_ATTN + D_KV,
                4 * D_CONV + D_ATTN + 2 * D_KV)
RMS_EPS = 1e-5

kernel_name = "hybrid_shortconv_swa_sink_alibi_parallel"


def rms_norm(x, gain):
    xf = x.astype(jnp.float32)
    y = xf * lax.rsqrt(jnp.mean(xf * xf, axis=-1, keepdims=True) + RMS_EPS)
    return (y * gain.astype(jnp.float32)).astype(x.dtype)


def alibi_slopes(n_heads):
    return jnp.exp2(-8.0 * jnp.arange(1, n_heads + 1, dtype=jnp.float32) / n_heads)


def short_conv(u, w):
    k = w[:, None, :].astype(u.dtype)
    return lax.conv_general_dilated(
        u, k, window_strides=(1,), padding=[(CONV_WIDTH - 1, 0)],
        dimension_numbers=("NWC", "WIO", "NWC"), feature_group_count=u.shape[-1])


def sliding_window_attention(q, k, v, sinks):
    bsz, seq, _ = q.shape
    nb = seq // BLOCK
    q = q.reshape(bsz, nb, BLOCK, N_KV_HEADS, Q_PER_KV, HEAD_DIM)
    k = k.reshape(bsz, nb, BLOCK, N_KV_HEADS, HEAD_DIM)
    v = v.reshape(bsz, nb, BLOCK, N_KV_HEADS, HEAD_DIM)
    pad = ((0, 0), (1, 0), (0, 0), (0, 0), (0, 0))
    k_band = jnp.concatenate([jnp.pad(k, pad)[:, :-1], k], axis=2)
    v_band = jnp.concatenate([jnp.pad(v, pad)[:, :-1], v], axis=2)
    scores = jnp.einsum("bnqhgd,bnkhd->bnhgqk", q, k_band).astype(jnp.float32)
    scores = scores * (HEAD_DIM ** -0.5)
    qi = jnp.arange(BLOCK)
    kj = jnp.arange(2 * BLOCK)
    dist = BLOCK + qi[:, None] - kj[None, :]
    key_pos = jnp.arange(nb)[:, None] * BLOCK - BLOCK + kj[None, :]
    valid = ((dist >= 0) & (dist < WINDOW))[None, :, :] & (key_pos >= 0)[:, None, :]
    slopes = alibi_slopes(N_Q_HEADS).reshape(N_KV_HEADS, Q_PER_KV)
    alibi = -slopes[:, :, None, None] * dist.astype(jnp.float32)
    scores = jnp.where(valid[None, :, None, None], scores + alibi[None, None], -jnp.inf)
    sink = sinks.astype(jnp.float32).reshape(N_KV_HEADS, Q_PER_KV)[None, None, :, :, None, None]
    m = jnp.maximum(jnp.max(scores, axis=-1, keepdims=True), sink)
    p = jnp.exp(scores - m)
    denom = jnp.sum(p, axis=-1, keepdims=True) + jnp.exp(sink - m)
    probs = (p / denom).astype(v.dtype)
    out = jnp.einsum("bnhgqk,bnkhd->bnqhgd", probs, v_band)
    return out.reshape(bsz, seq, D_ATTN)


def _fwd_setup_inputs(seed: int = 0) -> dict:
    key = jax.random.key(seed)
    ks = jax.random.split(key, 10)
    f32 = jnp.float32
    x = jax.random.normal(ks[0], (BATCH, SEQ, D_MODEL), f32)
    norm_in = 1.0 + 0.1 * jax.random.normal(ks[1], (DEPTH, D_MODEL), f32)
    w_in = jax.random.normal(ks[2], (DEPTH, D_MODEL, D_IN_PROJ), f32) * D_MODEL ** -0.5
    conv_w = jax.random.normal(ks[3], (DEPTH, CONV_WIDTH, D_CONV), f32) * CONV_WIDTH ** -0.5
    attn_sinks = jax.random.normal(ks[4], (DEPTH, N_Q_HEADS), f32)
    norm_conv_out = 1.0 + 0.1 * jax.random.normal(ks[5], (DEPTH, D_CONV), f32)
    norm_attn_out = 1.0 + 0.1 * jax.random.normal(ks[6], (DEPTH, D_ATTN), f32)
    w_out = jax.random.normal(ks[7], (DEPTH, D_MIX, D_MODEL), f32) * D_MIX ** -0.5
    norm_final = 1.0 + 0.1 * jax.random.normal(ks[8], (D_MODEL,), f32)
    return {"x": x, "norm_in": norm_in, "w_in": w_in, "conv_w": conv_w,
            "attn_sinks": attn_sinks, "norm_conv_out": norm_conv_out,
            "norm_attn_out": norm_attn_out, "w_out": w_out, "norm_final": norm_final}


def _fwd_reference(x, norm_in, w_in, conv_w, attn_sinks, norm_conv_out, norm_attn_out, w_out, norm_final):
    for layer in range(DEPTH):
        h = rms_norm(x, norm_in[layer])
        proj = jnp.einsum("bsd,de->bse", h, w_in[layer])
        cb, cc, cu, gate_c, q, k, v, gate_a = jnp.split(proj, SPLIT_POINTS, axis=-1)
        conv_y = cb * short_conv(cc * cu, conv_w[layer])
        conv_y = rms_norm(conv_y, norm_conv_out[layer]) * jax.nn.silu(gate_c)
        attn_y = sliding_window_attention(q, k, v, attn_sinks[layer])
        attn_y = rms_norm(attn_y, norm_attn_out[layer]) * jax.nn.silu(gate_a)
        mixed = jnp.concatenate([conv_y, attn_y], axis=-1)
        x = x + jnp.einsum("bse,ed->bsd", mixed, w_out[layer])
    return rms_norm(x, norm_final)


import jax as _jax
import jax.numpy as _jnp

TWIN_FORMAT = 'train_step'
FWD_PARAMS = ['x', 'norm_in', 'w_in', 'conv_w', 'attn_sinks', 'norm_conv_out', 'norm_attn_out', 'w_out', 'norm_final']
TWIN_WEIGHTS = ['norm_in', 'w_in', 'conv_w', 'attn_sinks', 'norm_conv_out', 'norm_attn_out', 'w_out', 'norm_final']
TWIN_DIFF_INPUT = 'x'
TWIN_INPUTS = ['x', 'norm_in', 'w_in', 'conv_w', 'attn_sinks', 'norm_conv_out', 'norm_attn_out', 'w_out', 'norm_final', 'loss_target', 'm_norm_in', 'm_w_in', 'm_conv_w', 'm_attn_sinks', 'm_norm_conv_out', 'm_norm_attn_out', 'm_w_out', 'm_norm_final', 'v_norm_in', 'v_w_in', 'v_conv_w', 'v_attn_sinks', 'v_norm_conv_out', 'v_norm_attn_out', 'v_w_out', 'v_norm_final']
TWIN_OUTPUTS = ['loss', 'grad_x', 'grad_norm_in', 'grad_w_in', 'grad_conv_w', 'grad_attn_sinks', 'grad_norm_conv_out', 'grad_norm_attn_out', 'grad_w_out', 'grad_norm_final', 'delta_norm_in', 'delta_w_in', 'delta_conv_w', 'delta_attn_sinks', 'delta_norm_conv_out', 'delta_norm_attn_out', 'delta_w_out', 'delta_norm_final', 'new_m_norm_in', 'new_m_w_in', 'new_m_conv_w', 'new_m_attn_sinks', 'new_m_norm_conv_out', 'new_m_norm_attn_out', 'new_m_w_out', 'new_m_norm_final', 'new_v_norm_in', 'new_v_w_in', 'new_v_conv_w', 'new_v_attn_sinks', 'new_v_norm_conv_out', 'new_v_norm_attn_out', 'new_v_w_out', 'new_v_norm_final']
TWIN_LEAF_KINDS = {'loss': 'loss', 'grad_x': 'grad_x', 'grad_norm_in': 'grad_w', 'grad_w_in': 'grad_w', 'grad_conv_w': 'grad_w', 'grad_attn_sinks': 'grad_w', 'grad_norm_conv_out': 'grad_w', 'grad_norm_attn_out': 'grad_w', 'grad_w_out': 'grad_w', 'grad_norm_final': 'grad_w', 'delta_norm_in': 'delta_w', 'delta_w_in': 'delta_w', 'delta_conv_w': 'delta_w', 'delta_attn_sinks': 'delta_w', 'delta_norm_conv_out': 'delta_w', 'delta_norm_attn_out': 'delta_w', 'delta_w_out': 'delta_w', 'delta_norm_final': 'delta_w', 'new_m_norm_in': 'new_m', 'new_m_w_in': 'new_m', 'new_m_conv_w': 'new_m', 'new_m_attn_sinks': 'new_m', 'new_m_norm_conv_out': 'new_m', 'new_m_norm_attn_out': 'new_m', 'new_m_w_out': 'new_m', 'new_m_norm_final': 'new_m', 'new_v_norm_in': 'new_v', 'new_v_w_in': 'new_v', 'new_v_conv_w': 'new_v', 'new_v_attn_sinks': 'new_v', 'new_v_norm_conv_out': 'new_v', 'new_v_norm_attn_out': 'new_v', 'new_v_w_out': 'new_v', 'new_v_norm_final': 'new_v'}


def _forward(args):
    return _fwd_reference(*[args[k] for k in FWD_PARAMS])


def _output_shape():
    out = _jax.eval_shape(lambda: _forward(_fwd_setup_inputs(0)))
    return out.shape, out.dtype

N_MICROBATCH = 1
ADAM_LR = 0.001
ADAM_B1 = 0.9
ADAM_B2 = 0.999
ADAM_EPS = 1e-08
ADAM_WD = 0.01
ADAM_STEP = 10
PER_EXAMPLE_BATCH_AXIS = {'x': 0, 'loss_target': 0}
SHARED_INPUTS = []
_WEIGHT_DTYPES = {'norm_in': _jnp.float32, 'w_in': _jnp.float32, 'conv_w': _jnp.float32, 'attn_sinks': _jnp.float32, 'norm_conv_out': _jnp.float32, 'norm_attn_out': _jnp.float32, 'w_out': _jnp.float32, 'norm_final': _jnp.float32}
MOMENT_SCALE = {'norm_in': 1.276706e-01, 'w_in': 5.060461e-02, 'conv_w': 4.904292e-02, 'attn_sinks': 5.282393e-02, 'norm_conv_out': 4.892967e-02, 'norm_attn_out': 4.977396e-02, 'w_out': 6.791244e-02, 'norm_final': 1.611923e+01}


def _to_microbatches(a, axis):
    t = _jnp.moveaxis(a, axis, 0)
    t = t.reshape((N_MICROBATCH, t.shape[0] // N_MICROBATCH) + t.shape[1:])
    return _jnp.moveaxis(t, 1, axis + 1)


def setup_inputs(seed: int = 0) -> dict:
    inp = _fwd_setup_inputs(seed)
    key = _jax.random.fold_in(_jax.random.key(seed), 7919)
    shape, _ = _output_shape()
    out = dict(inp)
    out["loss_target"] = _jax.random.normal(_jax.random.fold_in(key, 0), shape, _jnp.float32)
    for i, name in enumerate(TWIN_WEIGHTS):
        w = inp[name].astype(_jnp.float32)
        if MOMENT_SCALE is None:
            s = _jnp.sqrt(_jnp.mean(_jnp.square(w)) + 1e-30)
        else:
            s = MOMENT_SCALE[name]
        km, kv = _jax.random.split(_jax.random.fold_in(key, i + 1))
        out[name] = w
        out["m_" + name] = s * _jax.random.normal(km, w.shape, _jnp.float32)
        out["v_" + name] = (s * s) * _jax.random.uniform(kv, w.shape, _jnp.float32, 0.5, 1.5)
    if N_MICROBATCH > 1:
        for name, axis in PER_EXAMPLE_BATCH_AXIS.items():
            out[name] = _to_microbatches(out[name], axis)
    return {'x': out['x'], 'norm_in': out['norm_in'], 'w_in': out['w_in'], 'conv_w': out['conv_w'], 'attn_sinks': out['attn_sinks'], 'norm_conv_out': out['norm_conv_out'], 'norm_attn_out': out['norm_attn_out'], 'w_out': out['w_out'], 'norm_final': out['norm_final'], 'loss_target': out['loss_target'], 'm_norm_in': out['m_norm_in'], 'm_w_in': out['m_w_in'], 'm_conv_w': out['m_conv_w'], 'm_attn_sinks': out['m_attn_sinks'], 'm_norm_conv_out': out['m_norm_conv_out'], 'm_norm_attn_out': out['m_norm_attn_out'], 'm_w_out': out['m_w_out'], 'm_norm_final': out['m_norm_final'], 'v_norm_in': out['v_norm_in'], 'v_w_in': out['v_w_in'], 'v_conv_w': out['v_conv_w'], 'v_attn_sinks': out['v_attn_sinks'], 'v_norm_conv_out': out['v_norm_conv_out'], 'v_norm_attn_out': out['v_norm_attn_out'], 'v_w_out': out['v_w_out'], 'v_norm_final': out['v_norm_final']}


def _loss(weights, diff, rest, loss_target):
    with _jax.named_scope("forward"):
        args = {**rest, TWIN_DIFF_INPUT: diff, **{k: w.astype(_WEIGHT_DTYPES[k]) for k, w in weights.items()}}
        y = _forward(args)
    with _jax.named_scope("loss_head"):
        err = _jnp.square(y.astype(_jnp.float32) - loss_target)
        return 0.5 * _jnp.sum(_jnp.mean(err, axis=-1)) if err.ndim else 0.5 * err


def _adamw(w, g, m, v):
    m = ADAM_B1 * m + (1.0 - ADAM_B1) * g
    v = ADAM_B2 * v + (1.0 - ADAM_B2) * _jnp.square(g)
    m_hat = m / (1.0 - ADAM_B1 ** ADAM_STEP)
    v_hat = v / (1.0 - ADAM_B2 ** ADAM_STEP)
    delta = -ADAM_LR * (m_hat / (_jnp.sqrt(v_hat) + ADAM_EPS) + ADAM_WD * w)
    return delta, m, v


def reference(x, norm_in, w_in, conv_w, attn_sinks, norm_conv_out, norm_attn_out, w_out, norm_final, loss_target, m_norm_in, m_w_in, m_conv_w, m_attn_sinks, m_norm_conv_out, m_norm_attn_out, m_w_out, m_norm_final, v_norm_in, v_w_in, v_conv_w, v_attn_sinks, v_norm_conv_out, v_norm_attn_out, v_w_out, v_norm_final):
    given = dict(x=x, norm_in=norm_in, w_in=w_in, conv_w=conv_w, attn_sinks=attn_sinks, norm_conv_out=norm_conv_out, norm_attn_out=norm_attn_out, w_out=w_out, norm_final=norm_final, loss_target=loss_target, m_norm_in=m_norm_in, m_w_in=m_w_in, m_conv_w=m_conv_w, m_attn_sinks=m_attn_sinks, m_norm_conv_out=m_norm_conv_out, m_norm_attn_out=m_norm_attn_out, m_w_out=m_w_out, m_norm_final=m_norm_final, v_norm_in=v_norm_in, v_w_in=v_w_in, v_conv_w=v_conv_w, v_attn_sinks=v_attn_sinks, v_norm_conv_out=v_norm_conv_out, v_norm_attn_out=v_norm_attn_out, v_w_out=v_w_out, v_norm_final=v_norm_final)
    weights = {n: given[n] for n in TWIN_WEIGHTS}
    shared = {n: given[n] for n in SHARED_INPUTS}
    per_example = {n: given[n] for n in ['x']}
    grad_fn = _jax.value_and_grad(_loss, argnums=(0, 1))

    def one_microbatch(ex, loss_target):
        ex = dict(ex)
        diff = ex.pop(TWIN_DIFF_INPUT)
        return grad_fn(weights, diff, {**shared, **ex}, loss_target)

    if N_MICROBATCH == 1:
        loss, (grad_w, grad_x) = one_microbatch(per_example, given["loss_target"])
    else:
        def body(carry, xs):
            loss_sum, grad_sum = carry
            l_k, (gw_k, gx_k) = one_microbatch(xs[0], xs[1])
            with _jax.named_scope("update"):
                return (loss_sum + l_k, _jax.tree.map(_jnp.add, grad_sum, gw_k)), gx_k

        init = (_jnp.zeros((), _jnp.float32), _jax.tree.map(_jnp.zeros_like, weights))
        (loss, grad_w), grad_x = _jax.lax.scan(body, init, (per_example, given["loss_target"]))
    with _jax.named_scope("update"):
        delta_w, new_m, new_v = {}, {}, {}
        for n in TWIN_WEIGHTS:
            delta_w[n], new_m[n], new_v[n] = _adamw(weights[n], grad_w[n], given["m_" + n], given["v_" + n])
    return (loss, grad_x, *[grad_w[n] for n in TWIN_WEIGHTS], *[delta_w[n] for n in TWIN_WEIGHTS],
            *[new_m[n] for n in TWIN_WEIGHTS], *[new_v[n] for n in TWIN_WEIGHTS])
```

```python
import functools
import math

import jax
import jax.numpy as jnp
from jax import lax
from jax.experimental import pallas as pl
from jax.experimental.pallas import tpu as pltpu

F32 = jnp.float32
BF16 = jnp.bfloat16
MESH = pl.DeviceIdType.MESH

N_DEV = 8
SEQ = 2048
D_MODEL = 1024
D_CONV = 1024
D_ATTN = 1024
D_KV = 128
HEAD_DIM = 64
N_Q_HEADS = 16
N_PAIRS = N_Q_HEADS // 2
PAIRS_PER_KV = N_PAIRS // 2
D_MIX = D_CONV + D_ATTN
D_PROJ = 6400
SHARD_IN = D_PROJ // N_DEV
SHARD_OUT = D_MIX // N_DEV
SHARD_CONV = D_CONV // N_DEV
OFF_CB, OFF_CC, OFF_CU, OFF_GC, OFF_Q, OFF_K, OFF_V, OFF_GA = 0, 1024, 2048, 3072, 4096, 5120, 5248, 5376
BLOCK = 128
N_BLOCKS = SEQ // BLOCK
HALO = 8
RMS_EPS = 1e-5
NEG = -1e30
SCALE = HEAD_DIM ** -0.5
SLOPES = tuple(2.0 ** (-8.0 * (h + 1) / N_Q_HEADS) for h in range(N_Q_HEADS))

ADAM_LR = 0.001
ADAM_B1 = 0.9
ADAM_B2 = 0.999
ADAM_EPS = 1e-08
ADAM_WD = 0.01
ADAM_STEP = 10

ROW_NORM_IN, ROW_NORM_CONV, ROW_NORM_ATTN, ROW_NORM_FINAL, ROW_CONV0, ROW_SINKS = 0, 1, 2, 3, 4, 7

VMEM_LIMIT = 56 * 1024 * 1024

_NT = (((1,), (1,)), ((), ()))
_TN = (((0,), (0,)), ((), ()))


def _params(**kw):
    return pltpu.CompilerParams(vmem_limit_bytes=VMEM_LIMIT, **kw)


def _adamw(w, g, m, v):
    m = ADAM_B1 * m + (1.0 - ADAM_B1) * g
    v = ADAM_B2 * v + (1.0 - ADAM_B2) * (g * g)
    m_hat = m / (1.0 - ADAM_B1 ** ADAM_STEP)
    v_hat = v / (1.0 - ADAM_B2 ** ADAM_STEP)
    delta = -ADAM_LR * (m_hat / (jnp.sqrt(v_hat) + ADAM_EPS) + ADAM_WD * w)
    return delta, m, v


def _sigmoid(t):
    return 1.0 / (1.0 + jnp.exp(-t))


def _slot(px, py, pc):
    return 4 * px + 2 * py + pc


def _all_gather(w_in_sh, w_out_sh, conv_sh):
    def body(win_ref, wout_ref, cv_ref, gin_ref, gout_ref, gcv_ref, send_sems, recv_sems):
        x, y, c = lax.axis_index("x"), lax.axis_index("y"), lax.axis_index("c")
        me, sibling = (x, y, c), (x, y, 1 - c)
        chips = [(1 - x, y), (x, 1 - y), (1 - x, 1 - y)]
        bufs = (gin_ref, gout_ref, gcv_ref)

        gin_ref[_slot(*me)] = win_ref[...].astype(BF16)
        gout_ref[_slot(*me)] = wout_ref[...].astype(BF16)
        gcv_ref[_slot(*me)] = cv_ref[...]

        def copy(b, k, block, to):
            s = _slot(*block)
            return pltpu.make_async_remote_copy(
                src_ref=bufs[b].at[s], dst_ref=bufs[b].at[s],
                send_sem=send_sems.at[7 * b + k], recv_sem=recv_sems.at[7 * b + k],
                device_id=to, device_id_type=MESH)

        started = []
        for b in range(3):
            first = [copy(b, 0, me, sibling)]
            first += [copy(b, 1 + j, me, (*chip, c)) for j, chip in enumerate(chips)]
            for cp in first:
                cp.start()
            started += first
        for b in range(3):
            for j, chip in enumerate(chips):
                copy(b, 1 + j, (*chip, c), me).wait_recv()
                fwd = copy(b, 4 + j, (*chip, c), sibling)
                fwd.start()
                started.append(fwd)
        for b in range(3):
            copy(b, 0, sibling, me).wait_recv()
            for j, chip in enumerate(chips):
                copy(b, 4 + j, (*chip, 1 - c), me).wait_recv()
        for cp in started:
            cp.wait_send()

    vmem = pl.BlockSpec(memory_space=pltpu.VMEM)
    return pl.pallas_call(
        body, name="all_gather",
        out_shape=(jax.ShapeDtypeStruct((N_DEV, D_MODEL, SHARD_IN), BF16),
                   jax.ShapeDtypeStruct((N_DEV, SHARD_OUT, D_MODEL), BF16),
                   jax.ShapeDtypeStruct((N_DEV, 8, SHARD_CONV), F32)),
        in_specs=[vmem, vmem, vmem], out_specs=(vmem, vmem, vmem),
        scratch_shapes=[pltpu.SemaphoreType.DMA((21,)), pltpu.SemaphoreType.DMA((21,))],
        compiler_params=_params(),
    )(w_in_sh, w_out_sh, conv_sh)


SEM_D2D_A, SEM_D2D_B, SEM_ICI_A, SEM_ICI_B, SEM_SMALL, N_RS_SEMS = 0, 4, 8, 11, 14, 21


def _reduce_scatter(dw_in, dw_out, gslab, wslab, mslab, vslab, cw, cm, cv):
    def body(a_ref, b_ref, gs_ref, ws_ref, ms_ref, vs_ref, cw_ref, cm_ref, cv_ref,
             ga_ref, gb_ref, gsum_ref, dsl_ref, msl_ref, vsl_ref, cg_ref, cd_ref, cmo_ref, cvo_ref,
             d2d_a, d2d_b, st_a, st_b, ici_a, ici_b, slabs, send_sems, recv_sems):
        x, y, c = lax.axis_index("x"), lax.axis_index("y"), lax.axis_index("c")
        me, sibling = (x, y, c), (x, y, 1 - c)
        chips = [(x, y), (1 - x, y), (x, 1 - y), (1 - x, 1 - y)]
        peers = [(x, y, 1 - c), (1 - x, y, c), (x, 1 - y, c), (1 - x, 1 - y, c),
                 (1 - x, y, 1 - c), (x, 1 - y, 1 - c), (1 - x, 1 - y, 1 - c)]

        def rcopy(src, dst, k, to):
            return pltpu.make_async_remote_copy(src_ref=src, dst_ref=dst, send_sem=send_sems.at[k],
                                                recv_sem=recv_sems.at[k], device_id=to, device_id_type=MESH)

        started = []
        slabs[_slot(*me)] = gs_ref[...]
        for k, peer in enumerate(peers):
            cp = rcopy(gs_ref, slabs.at[_slot(*me)], SEM_SMALL + k, peer)
            cp.start()
            started.append(cp)
        for k, chip in enumerate(chips):
            s = _slot(*chip, 1 - c)
            for src, dst, base in ((a_ref, d2d_a, SEM_D2D_A), (b_ref, d2d_b, SEM_D2D_B)):
                cp = rcopy(src.at[s], dst.at[k], base + k, sibling)
                cp.start()
                started.append(cp)
        for k in range(1, 4):
            s = _slot(*chips[k], c)
            for src, d2d, st, ici, b_d2d, b_ici in ((a_ref, d2d_a, st_a, ici_a, SEM_D2D_A, SEM_ICI_A),
                                                    (b_ref, d2d_b, st_b, ici_b, SEM_D2D_B, SEM_ICI_B)):
                rcopy(src.at[0], d2d.at[k], b_d2d + k, me).wait_recv()
                st[k - 1] = (src[s].astype(F32) + d2d[k].astype(F32)).astype(BF16)
                cp = rcopy(st.at[k - 1], ici.at[k - 1], b_ici + k - 1, (*chips[k], c))
                cp.start()
                started.append(cp)
        for src, d2d, ici, out, b_d2d, b_ici in ((a_ref, d2d_a, ici_a, ga_ref, SEM_D2D_A, SEM_ICI_A),
                                                 (b_ref, d2d_b, ici_b, gb_ref, SEM_D2D_B, SEM_ICI_B)):
            rcopy(src.at[0], d2d.at[0], b_d2d, me).wait_recv()
            acc = src[_slot(*me)].astype(F32) + d2d[0].astype(F32)
            for k in range(1, 4):
                rcopy(src.at[0], ici.at[k - 1], b_ici + k - 1, me).wait_recv()
                acc = acc + ici[k - 1].astype(F32)
            out[...] = acc
        for k in range(7):
            rcopy(gs_ref, slabs.at[0], SEM_SMALL + k, me).wait_recv()
        gsum = slabs[0]
        for d in range(1, N_DEV):
            gsum = gsum + slabs[d]
        gsum_ref[...] = gsum
        delta, m, v = _adamw(ws_ref[...], gsum, ms_ref[...], vs_ref[...])
        dsl_ref[...] = delta
        msl_ref[...] = m
        vsl_ref[...] = v
        idx = _slot(*me)
        cg = jnp.zeros((8, SHARD_CONV), F32)
        for d in range(N_DEV):
            cg = jnp.where(idx == d, gsum[:, d * SHARD_CONV:(d + 1) * SHARD_CONV], cg)
        cg = pltpu.roll(cg, 8 - ROW_CONV0, 0)
        row = lax.broadcasted_iota(jnp.int32, (8, SHARD_CONV), 0)
        cg = jnp.where(row < 3, cg, 0.0)
        cg_ref[...] = cg
        delta, m, v = _adamw(cw_ref[...], cg, cm_ref[...], cv_ref[...])
        cd_ref[...] = delta
        cmo_ref[...] = m
        cvo_ref[...] = v
        for cp in started:
            cp.wait_send()

    vmem = pl.BlockSpec(memory_space=pltpu.VMEM)
    slab = jax.ShapeDtypeStruct((8, D_MODEL), F32)
    cslab = jax.ShapeDtypeStruct((8, SHARD_CONV), F32)
    return pl.pallas_call(
        body, name="reduce_scatter",
        out_shape=(jax.ShapeDtypeStruct((D_MODEL, SHARD_IN), F32), jax.ShapeDtypeStruct((SHARD_OUT, D_MODEL), F32),
                   slab, slab, slab, slab, cslab, cslab, cslab, cslab),
        in_specs=[vmem] * 9, out_specs=tuple([vmem] * 10),
        scratch_shapes=[
            pltpu.VMEM((4, D_MODEL, SHARD_IN), BF16), pltpu.VMEM((4, SHARD_OUT, D_MODEL), BF16),
            pltpu.VMEM((3, D_MODEL, SHARD_IN), BF16), pltpu.VMEM((3, SHARD_OUT, D_MODEL), BF16),
            pltpu.VMEM((3, D_MODEL, SHARD_IN), BF16), pltpu.VMEM((3, SHARD_OUT, D_MODEL), BF16),
            pltpu.VMEM((N_DEV, 8, D_MODEL), F32),
            pltpu.SemaphoreType.DMA((N_RS_SEMS,)), pltpu.SemaphoreType.DMA((N_RS_SEMS,))],
        compiler_params=_params(),
    )(dw_in, dw_out, gslab, wslab, mslab, vslab, cw, cm, cv)


def _in_proj(x, norm_in, w_full):
    tm = 256

    def body(x_ref, g_ref, w_ref, h_ref, proj_ref):
        xv = x_ref[...]
        r = lax.rsqrt(jnp.mean(xv * xv, axis=-1, keepdims=True) + RMS_EPS)
        h = (xv * r * g_ref[...]).astype(BF16)
        h_ref[...] = h
        proj_ref[...] = jnp.dot(h, w_ref[...], preferred_element_type=F32)

    return pl.pallas_call(
        body, name="in_proj", grid=(SEQ // tm,),
        in_specs=[pl.BlockSpec((tm, D_MODEL), lambda i: (i, 0)), pl.BlockSpec((1, D_MODEL), lambda i: (0, 0)),
                  pl.BlockSpec(memory_space=pltpu.VMEM)],
        out_specs=(pl.BlockSpec((tm, D_MODEL), lambda i: (i, 0)), pl.BlockSpec((tm, D_PROJ), lambda i: (i, 0))),
        out_shape=(jax.ShapeDtypeStruct((SEQ, D_MODEL), BF16), jax.ShapeDtypeStruct((SEQ, D_PROJ), F32)),
        compiler_params=_params(dimension_semantics=("arbitrary",)),
    )(x, norm_in, w_full)


def _conv_fwd(pj, cch_ref, cuh_ref, cw_ref, n):
    cc = pj[:, OFF_CC:OFF_CC + D_CONV]
    cu = pj[:, OFF_CU:OFF_CU + D_CONV]
    z = cc * cu
    zh = jnp.where(n > 0, cch_ref[...] * cuh_ref[...], 0.0)
    row = lax.broadcasted_iota(jnp.int32, (BLOCK, D_CONV), 0)
    z1 = jnp.where(row == 0, zh[HALO - 1:HALO, :], pltpu.roll(z, 1, 0))
    z2 = jnp.where(row == 0, zh[HALO - 2:HALO - 1, :], jnp.where(row == 1, zh[HALO - 1:HALO, :], pltpu.roll(z, 2, 0)))
    co = cw_ref[0:1, :] * z2 + cw_ref[1:2, :] * z1 + cw_ref[2:3, :] * z
    return cc, cu, z, z1, z2, co


def _kv_bands(pj, kvp_ref):
    lane = lax.broadcasted_iota(jnp.int32, (2 * BLOCK, D_KV), 1)
    lo = lane < HEAD_DIM

    def bands(prev, cur):
        b = jnp.concatenate([prev, cur], axis=0)
        br = pltpu.roll(b, HEAD_DIM, 1)
        zero = jnp.zeros_like(b)
        return ((jnp.where(lo, b, zero).astype(BF16), jnp.where(lo, zero, br).astype(BF16)),
                (jnp.where(lo, br, zero).astype(BF16), jnp.where(lo, zero, b).astype(BF16)))

    ks = bands(kvp_ref[:, 0:D_KV], pj[:, OFF_K:OFF_K + D_KV])
    vs = bands(kvp_ref[:, D_KV:2 * D_KV], pj[:, OFF_V:OFF_V + D_KV])
    return ks, vs


def _attn_bias(n):
    qi = lax.broadcasted_iota(jnp.int32, (BLOCK, 2 * BLOCK), 0)
    kj = lax.broadcasted_iota(jnp.int32, (BLOCK, 2 * BLOCK), 1)
    dist = BLOCK + qi - kj
    valid = (dist >= 0) & (dist < BLOCK) & ((kj >= BLOCK) | (n > 0))
    return dist.astype(F32), valid


def _attn_probs(qp, kband, h, distf, valid, sink):
    s = lax.dot_general(qp, kband, _NT, preferred_element_type=F32) * SCALE
    s = jnp.where(valid, s - SLOPES[h] * distf, NEG)
    m = jnp.maximum(jnp.max(s, axis=-1, keepdims=True), sink)
    p = jnp.exp(s - m)
    es = jnp.exp(sink - m)
    inv = 1.0 / (jnp.sum(p, axis=-1, keepdims=True) + es)
    return p * inv, es * inv


def _mix_fwd(proj, conv_full, sinks, norm_conv, norm_attn):
    def body(pj_ref, kvp_ref, cch_ref, cuh_ref, cw_ref, sink_ref, gc_ref, ga_ref, mixed_ref, attn_scr):
        n = pl.program_id(0)
        pj = pj_ref
        _, _, _, _, _, co = _conv_fwd(pj, cch_ref, cuh_ref, cw_ref, n)
        a_c = pj[:, OFF_CB:OFF_CB + D_CONV] * co
        r_c = lax.rsqrt(jnp.mean(a_c * a_c, axis=-1, keepdims=True) + RMS_EPS)
        t_c = pj[:, OFF_GC:OFF_GC + D_CONV]
        mixed_ref[:, 0:D_CONV] = (a_c * r_c * gc_ref[...] * (t_c * _sigmoid(t_c))).astype(BF16)

        ks, vs = _kv_bands(pj, kvp_ref)
        distf, valid = _attn_bias(n)
        for i in range(N_PAIRS):
            j = i // PAIRS_PER_KV
            qp = pj[:, OFF_Q + 128 * i:OFF_Q + 128 * (i + 1)].astype(BF16)
            out = None
            for e in range(2):
                h = 2 * i + e
                p, _ = _attn_probs(qp, ks[j][e], h, distf, valid, sink_ref[h])
                o = jnp.dot(p.astype(BF16), vs[j][e], preferred_element_type=F32)
                out = o if out is None else out + o
            attn_scr[:, 128 * i:128 * (i + 1)] = out
        a_a = attn_scr[...]
        r_a = lax.rsqrt(jnp.mean(a_a * a_a, axis=-1, keepdims=True) + RMS_EPS)
        t_a = pj[:, OFF_GA:OFF_GA + D_ATTN]
        mixed_ref[:, D_CONV:D_MIX] = (a_a * r_a * ga_ref[...] * (t_a * _sigmoid(t_a))).astype(BF16)

    per_block = BLOCK // HALO
    return pl.pallas_call(
        body, name="mix_fwd", grid=(N_BLOCKS,),
        in_specs=[
            pl.BlockSpec((BLOCK, D_PROJ), lambda n: (n, 0)),
            pl.BlockSpec((BLOCK, 2 * D_KV), lambda n: (jnp.maximum(n - 1, 0), OFF_K // (2 * D_KV))),
            pl.BlockSpec((HALO, D_CONV), lambda n: (jnp.maximum(n * per_block - 1, 0), OFF_CC // D_CONV)),
            pl.BlockSpec((HALO, D_CONV), lambda n: (jnp.maximum(n * per_block - 1, 0), OFF_CU // D_CONV)),
            pl.BlockSpec((8, D_CONV), lambda n: (0, 0)),
            pl.BlockSpec(memory_space=pltpu.SMEM),
            pl.BlockSpec((1, D_CONV), lambda n: (0, 0)),
            pl.BlockSpec((1, D_ATTN), lambda n: (0, 0)),
        ],
        out_specs=pl.BlockSpec((BLOCK, D_MIX), lambda n: (n, 0)),
        out_shape=jax.ShapeDtypeStruct((SEQ, D_MIX), BF16),
        scratch_shapes=[pltpu.VMEM((BLOCK, D_ATTN), F32)],
        compiler_params=_params(dimension_semantics=("arbitrary",)),
    )(proj, proj, proj, proj, conv_full, sinks, norm_conv, norm_attn)


def _out_proj_loss(mixed, x, target, w_out_full, norm_final):
    tm = 256

    def body(mx_ref, x_ref, t_ref, w_ref, g_ref, dx2_ref, dmix_ref, gnf_ref, loss_ref):
        i = pl.program_id(0)
        w = w_ref[...]
        x2 = x_ref[...] + jnp.dot(mx_ref[...], w, preferred_element_type=F32)
        r = lax.rsqrt(jnp.mean(x2 * x2, axis=-1, keepdims=True) + RMS_EPS)
        xn = x2 * r
        g = g_ref[...]
        err = xn * g - t_ref[...]
        part = 0.5 * jnp.sum(jnp.mean(err * err, axis=-1, keepdims=True), axis=0, keepdims=True)
        dy = err * (1.0 / D_MODEL)
        gnf = jnp.sum(dy * xn, axis=0, keepdims=True)
        u = dy * g
        dx2 = r * (u - xn * jnp.mean(u * xn, axis=-1, keepdims=True))
        dx2_ref[...] = dx2
        dmix_ref[...] = lax.dot_general(dx2.astype(BF16), w, _NT, preferred_element_type=F32)

        @pl.when(i == 0)
        def _():
            gnf_ref[...] = jnp.zeros_like(gnf_ref)
            loss_ref[...] = jnp.zeros_like(loss_ref)

        gnf_ref[...] += gnf
        loss_ref[...] += jnp.broadcast_to(part, loss_ref.shape)

    return pl.pallas_call(
        body, name="out_proj_loss", grid=(SEQ // tm,),
        in_specs=[pl.BlockSpec((tm, D_MIX), lambda i: (i, 0)), pl.BlockSpec((tm, D_MODEL), lambda i: (i, 0)),
                  pl.BlockSpec((tm, D_MODEL), lambda i: (i, 0)), pl.BlockSpec(memory_space=pltpu.VMEM),
                  pl.BlockSpec((1, D_MODEL), lambda i: (0, 0))],
        out_specs=(pl.BlockSpec((tm, D_MODEL), lambda i: (i, 0)), pl.BlockSpec((tm, D_MIX), lambda i: (i, 0)),
                   pl.BlockSpec((1, D_MODEL), lambda i: (0, 0)), pl.BlockSpec((8, 128), lambda i: (0, 0))),
        out_shape=(jax.ShapeDtypeStruct((SEQ, D_MODEL), F32), jax.ShapeDtypeStruct((SEQ, D_MIX), F32),
                   jax.ShapeDtypeStruct((1, D_MODEL), F32), jax.ShapeDtypeStruct((8, 128), F32)),
        compiler_params=_params(dimension_semantics=("arbitrary",)),
    )(mixed, x, target, w_out_full, norm_final)


def _gated_norm_bwd(a, gain, t, dy):
    r = lax.rsqrt(jnp.mean(a * a, axis=-1, keepdims=True) + RMS_EPS)
    an = a * r
    sg = _sigmoid(t)
    dn = dy * (t * sg)
    dt = dy * (an * gain) * (sg * (1.0 + t * (1.0 - sg)))
    dgain = jnp.sum(dn * an, axis=0, keepdims=True)
    u = dn * gain
    da = r * (u - an * jnp.mean(u * an, axis=-1, keepdims=True))
    return da, dt, dgain


def _mix_bwd(proj, dmixed, conv_full, sinks, norm_conv, norm_attn):
    def body(pj_ref, kvp_ref, cch_ref, cuh_ref, dmx_ref, cw_ref, sink_ref, gc_ref, ga_ref,
             dpj_ref, gslab_ref, attn_scr, dattn_scr, p_scr, ps_scr, nxt_scr, dkv_scr):
        step = pl.program_id(0)
        n = N_BLOCKS - 1 - step
        pj = pj_ref

        @pl.when(step == 0)
        def _():
            gslab_ref[...] = jnp.zeros_like(gslab_ref)
            nxt_scr[...] = jnp.zeros_like(nxt_scr)
            dkv_scr[...] = jnp.zeros_like(dkv_scr)

        cc, cu, z, z1, z2, co = _conv_fwd(pj, cch_ref, cuh_ref, cw_ref, n)
        cb = pj[:, OFF_CB:OFF_CB + D_CONV]
        da_c, dgate_c, dgain_c = _gated_norm_bwd(cb * co, gc_ref[...], pj[:, OFF_GC:OFF_GC + D_CONV],
                                                 dmx_ref[:, 0:D_CONV])
        dpj_ref[:, OFF_GC:OFF_GC + D_CONV] = dgate_c.astype(BF16)
        dpj_ref[:, OFF_CB:OFF_CB + D_CONV] = (da_c * co).astype(BF16)
        dco = da_c * cb
        nxt = nxt_scr[...]
        row = lax.broadcasted_iota(jnp.int32, (BLOCK, D_CONV), 0)
        dco1 = jnp.where(row == BLOCK - 1, nxt[0:1, :], pltpu.roll(dco, BLOCK - 1, 0))
        dco2 = jnp.where(row == BLOCK - 2, nxt[0:1, :],
                         jnp.where(row == BLOCK - 1, nxt[1:2, :], pltpu.roll(dco, BLOCK - 2, 0)))
        dz = cw_ref[2:3, :] * dco + cw_ref[1:2, :] * dco1 + cw_ref[0:1, :] * dco2
        dpj_ref[:, OFF_CC:OFF_CC + D_CONV] = (dz * cu).astype(BF16)
        dpj_ref[:, OFF_CU:OFF_CU + D_CONV] = (dz * cc).astype(BF16)
        nxt_scr[...] = dco[0:HALO, :]
        gslab_ref[ROW_NORM_CONV:ROW_NORM_CONV + 1, :] += dgain_c
        gslab_ref[ROW_CONV0:ROW_CONV0 + 1, :] += jnp.sum(dco * z2, axis=0, keepdims=True)
        gslab_ref[ROW_CONV0 + 1:ROW_CONV0 + 2, :] += jnp.sum(dco * z1, axis=0, keepdims=True)
        gslab_ref[ROW_CONV0 + 2:ROW_CONV0 + 3, :] += jnp.sum(dco * z, axis=0, keepdims=True)

        ks, vs = _kv_bands(pj, kvp_ref)
        distf, valid = _attn_bias(n)
        for i in range(N_PAIRS):
            j = i // PAIRS_PER_KV
            qp = pj[:, OFF_Q + 128 * i:OFF_Q + 128 * (i + 1)].astype(BF16)
            out = None
            for e in range(2):
                h = 2 * i + e
                p, psink = _attn_probs(qp, ks[j][e], h, distf, valid, sink_ref[h])
                p_scr[h] = p
                ps_scr[h] = psink
                o = jnp.dot(p.astype(BF16), vs[j][e], preferred_element_type=F32)
                out = o if out is None else out + o
            attn_scr[:, 128 * i:128 * (i + 1)] = out
        da_a, dgate_a, dgain_a = _gated_norm_bwd(attn_scr[...], ga_ref[...], pj[:, OFF_GA:OFF_GA + D_ATTN],
                                                 dmx_ref[:, D_CONV:D_MIX])
        dpj_ref[:, OFF_GA:OFF_GA + D_ATTN] = dgate_a.astype(BF16)
        dattn_scr[...] = da_a
        gslab_ref[ROW_NORM_ATTN:ROW_NORM_ATTN + 1, :] += dgain_a

        lane = lax.broadcasted_iota(jnp.int32, (BLOCK, 128), 1)
        lo = lane < HEAD_DIM
        lane_s = lax.broadcasted_iota(jnp.int32, (1, D_MODEL), 1)
        gsink = jnp.zeros((1, D_MODEL), F32)
        dkv_full = []
        for j in range(2):
            dk_f = jnp.zeros((2 * BLOCK, 128), F32)
            dv_f = jnp.zeros((2 * BLOCK, 128), F32)
            for i in range(j * PAIRS_PER_KV, (j + 1) * PAIRS_PER_KV):
                q_f = pj[:, OFF_Q + 128 * i:OFF_Q + 128 * (i + 1)]
                do_f = dattn_scr[:, 128 * i:128 * (i + 1)]
                prod = do_f * attn_scr[:, 128 * i:128 * (i + 1)]
                tot = jnp.sum(prod, axis=-1, keepdims=True)
                d_lo = jnp.sum(jnp.where(lo, prod, 0.0), axis=-1, keepdims=True)
                deltas = (d_lo, tot - d_lo)
                do_b = do_f.astype(BF16)
                dq = None
                for e in range(2):
                    h = 2 * i + e
                    half = lo if e == 0 else jnp.logical_not(lo)
                    p = p_scr[h]
                    dp = lax.dot_general(do_b, vs[j][e], _NT, preferred_element_type=F32)
                    ds = (p * (dp - deltas[e]) * SCALE).astype(BF16)
                    gs = -jnp.sum(ps_scr[h] * deltas[e], axis=0, keepdims=True)
                    gsink = gsink + jnp.where(lane_s == h, gs, 0.0)
                    t = jnp.dot(ds, ks[j][e], preferred_element_type=F32)
                    dq = t if dq is None else dq + t
                    dk_f = dk_f + lax.dot_general(ds, jnp.where(half, q_f, 0.0).astype(BF16), _TN,
                                                  preferred_element_type=F32)
                    dv_f = dv_f + lax.dot_general(p.astype(BF16), jnp.where(half, do_f, 0.0).astype(BF16), _TN,
                                                  preferred_element_type=F32)
                dpj_ref[:, OFF_Q + 128 * i:OFF_Q + 128 * (i + 1)] = dq.astype(BF16)
            dkv_full.append((dk_f + pltpu.roll(dk_f, HEAD_DIM, 1), dv_f + pltpu.roll(dv_f, HEAD_DIM, 1)))
        lane2 = lax.broadcasted_iota(jnp.int32, (2 * BLOCK, 128), 1)
        lo2 = lane2 < HEAD_DIM
        dk = jnp.where(lo2, dkv_full[0][0], dkv_full[1][0])
        dv = jnp.where(lo2, dkv_full[0][1], dkv_full[1][1])
        dpj_ref[:, OFF_K:OFF_K + D_KV] = (dk[BLOCK:, :] + dkv_scr[:, 0:D_KV]).astype(BF16)
        dpj_ref[:, OFF_V:OFF_V + D_KV] = (dv[BLOCK:, :] + dkv_scr[:, D_KV:2 * D_KV]).astype(BF16)
        dkv_scr[:, 0:D_KV] = dk[:BLOCK, :]
        dkv_scr[:, D_KV:2 * D_KV] = dv[:BLOCK, :]
        gslab_ref[ROW_SINKS:ROW_SINKS + 1, :] += gsink

    per_block = BLOCK // HALO
    last = N_BLOCKS - 1
    return pl.pallas_call(
        body, name="mix_bwd", grid=(N_BLOCKS,),
        in_specs=[
            pl.BlockSpec((BLOCK, D_PROJ), lambda s: (last - s, 0)),
            pl.BlockSpec((BLOCK, 2 * D_KV), lambda s: (jnp.maximum(last - s - 1, 0), OFF_K // (2 * D_KV))),
            pl.BlockSpec((HALO, D_CONV), lambda s: (jnp.maximum((last - s) * per_block - 1, 0), OFF_CC // D_CONV)),
            pl.BlockSpec((HALO, D_CONV), lambda s: (jnp.maximum((last - s) * per_block - 1, 0), OFF_CU // D_CONV)),
            pl.BlockSpec((BLOCK, D_MIX), lambda s: (last - s, 0)),
            pl.BlockSpec((8, D_CONV), lambda s: (0, 0)),
            pl.BlockSpec(memory_space=pltpu.SMEM),
            pl.BlockSpec((1, D_CONV), lambda s: (0, 0)),
            pl.BlockSpec((1, D_ATTN), lambda s: (0, 0)),
        ],
        out_specs=(pl.BlockSpec((BLOCK, D_PROJ), lambda s: (last - s, 0)),
                   pl.BlockSpec((8, D_MODEL), lambda s: (0, 0))),
        out_shape=(jax.ShapeDtypeStruct((SEQ, D_PROJ), BF16), jax.ShapeDtypeStruct((8, D_MODEL), F32)),
        scratch_shapes=[pltpu.VMEM((BLOCK, D_ATTN), F32), pltpu.VMEM((BLOCK, D_ATTN), F32),
                        pltpu.VMEM((N_Q_HEADS, BLOCK, 2 * BLOCK), F32), pltpu.VMEM((N_Q_HEADS, BLOCK, 1), F32),
                        pltpu.VMEM((HALO, D_CONV), F32), pltpu.VMEM((BLOCK, 2 * D_KV), F32)],
        compiler_params=_params(dimension_semantics=("arbitrary",)),
    )(proj, proj, proj, proj, dmixed, conv_full, sinks, norm_conv, norm_attn)


def _in_bwd(dproj, w_full, x, dx2, norm_in):
    tm = 256

    def body(dp_ref, w_ref, x_ref, dx2_ref, g_ref, gx_ref, gni_ref):
        i = pl.program_id(0)
        dh = lax.dot_general(dp_ref[...], w_ref[...], _NT, preferred_element_type=F32)
        xv = x_ref[...]
        r = lax.rsqrt(jnp.mean(xv * xv, axis=-1, keepdims=True) + RMS_EPS)
        xn = xv * r
        u = dh * g_ref[...]
        gx_ref[...] = dx2_ref[...] + r * (u - xn * jnp.mean(u * xn, axis=-1, keepdims=True))

        @pl.when(i == 0)
        def _():
            gni_ref[...] = jnp.zeros_like(gni_ref)

        gni_ref[...] += jnp.sum(dh * xn, axis=0, keepdims=True)

    return pl.pallas_call(
        body, name="in_bwd", grid=(SEQ // tm,),
        in_specs=[pl.BlockSpec((tm, D_PROJ), lambda i: (i, 0)), pl.BlockSpec(memory_space=pltpu.VMEM),
                  pl.BlockSpec((tm, D_MODEL), lambda i: (i, 0)), pl.BlockSpec((tm, D_MODEL), lambda i: (i, 0)),
                  pl.BlockSpec((1, D_MODEL), lambda i: (0, 0))],
        out_specs=(pl.BlockSpec((tm, D_MODEL), lambda i: (i, 0)), pl.BlockSpec((1, D_MODEL), lambda i: (0, 0))),
        out_shape=(jax.ShapeDtypeStruct((SEQ, D_MODEL), F32), jax.ShapeDtypeStruct((1, D_MODEL), F32)),
        compiler_params=_params(dimension_semantics=("arbitrary",)),
    )(dproj, w_full, x, dx2, norm_in)


def _matmul_tn(a, b, tn, name):
    k, m = a.shape
    _, n = b.shape

    def body(a_ref, b_ref, o_ref):
        o_ref[...] = lax.dot_general(a_ref[...].astype(BF16), b_ref[...].astype(BF16), _TN,
                                     preferred_element_type=F32).astype(BF16)

    return pl.pallas_call(
        body, name=name, grid=(n // tn,),
        in_specs=[pl.BlockSpec(memory_space=pltpu.VMEM), pl.BlockSpec((k, tn), lambda i: (0, i))],
        out_specs=pl.BlockSpec((m, tn), lambda i: (0, i)),
        out_shape=jax.ShapeDtypeStruct((m, n), BF16),
        compiler_params=_params(dimension_semantics=("arbitrary",)),
    )(a, b)


def _adam_shard(w, g, m, v, tr, name):
    rows, cols = w.shape

    def body(w_ref, g_ref, m_ref, v_ref, d_ref, mo_ref, vo_ref):
        delta, mn, vn = _adamw(w_ref[...], g_ref[...], m_ref[...], v_ref[...])
        d_ref[...] = delta
        mo_ref[...] = mn
        vo_ref[...] = vn

    spec = pl.BlockSpec((tr, cols), lambda i: (i, 0))
    out = jax.ShapeDtypeStruct((rows, cols), F32)
    return pl.pallas_call(
        body, name=name, grid=(rows // tr,),
        in_specs=[spec] * 4, out_specs=(spec, spec, spec), out_shape=(out, out, out),
        compiler_params=_params(dimension_semantics=("arbitrary",)),
    )(w, g, m, v)


def _pad_rows(a, rows=8):
    return jnp.pad(a, ((0, rows - a.shape[0]), (0, 0)))


def _small_slab(norm_in, norm_conv, norm_attn, norm_final, sinks):
    return jnp.concatenate([norm_in, norm_conv, norm_attn, norm_final.reshape(1, D_MODEL),
                            jnp.zeros((3, D_MODEL), F32), jnp.pad(sinks, ((0, 0), (0, D_MODEL - N_Q_HEADS)))], axis=0)


def kernel(x, norm_in, w_in, conv_w, attn_sinks, norm_conv_out, norm_attn_out, w_out, norm_final, loss_target, m_norm_in, m_w_in, m_conv_w, m_attn_sinks, m_norm_conv_out, m_norm_attn_out, m_w_out, m_norm_final, v_norm_in, v_w_in, v_conv_w, v_attn_sinks, v_norm_conv_out, v_norm_attn_out, v_w_out, v_norm_final):
    x2d = x.reshape(SEQ, D_MODEL)
    target = loss_target.reshape(SEQ, D_MODEL)
    nf = norm_final.reshape(1, D_MODEL)

    g_in, g_out, g_cv = _all_gather(w_in[0], w_out[0], _pad_rows(conv_w[0]))
    w_in_full = jnp.transpose(g_in, (1, 0, 2)).reshape(D_MODEL, D_PROJ)
    w_out_full = g_out.reshape(D_MIX, D_MODEL)
    conv_full = jnp.transpose(g_cv, (1, 0, 2)).reshape(8, D_CONV)
    sinks = attn_sinks.reshape(N_Q_HEADS)

    h, proj = _in_proj(x2d, norm_in, w_in_full)
    mixed = _mix_fwd(proj, conv_full, sinks, norm_conv_out, norm_attn_out)
    dx2, dmixed, gnf, loss_part = _out_proj_loss(mixed, x2d, target, w_out_full, nf)
    dproj, gslab = _mix_bwd(proj, dmixed, conv_full, sinks, norm_conv_out, norm_attn_out)
    grad_x, gni = _in_bwd(dproj, w_in_full, x2d, dx2, norm_in)
    dw_in = _matmul_tn(h, dproj, 640, "dw_in")
    dw_out = _matmul_tn(mixed, dx2, 512, "dw_out")

    row = lax.broadcasted_iota(jnp.int32, (8, D_MODEL), 0)
    gslab = jnp.where(row == ROW_NORM_IN, gni, jnp.where(row == ROW_NORM_FINAL, gnf, gslab))
    dw_in_sh = jnp.transpose(dw_in.reshape(D_MODEL, N_DEV, SHARD_IN), (1, 0, 2))
    dw_out_sh = dw_out.reshape(N_DEV, SHARD_OUT, D_MODEL)
    wslab = _small_slab(norm_in, norm_conv_out, norm_attn_out, norm_final, attn_sinks)
    mslab = _small_slab(m_norm_in, m_norm_conv_out, m_norm_attn_out, m_norm_final, m_attn_sinks)
    vslab = _small_slab(v_norm_in, v_norm_conv_out, v_norm_attn_out, v_norm_final, v_attn_sinks)
    (g_w_in, g_w_out, gsum, dsl, msl, vsl, cg, cd, cmo, cvo) = _reduce_scatter(
        dw_in_sh, dw_out_sh, gslab, wslab, mslab, vslab,
        _pad_rows(conv_w[0]), _pad_rows(m_conv_w[0]), _pad_rows(v_conv_w[0]))
    d_w_in, nm_w_in, nv_w_in = _adam_shard(w_in[0], g_w_in, m_w_in[0], v_w_in[0], 128, "adam_in")
    d_w_out, nm_w_out, nv_w_out = _adam_shard(w_out[0], g_w_out, m_w_out[0], v_w_out[0], 64, "adam_out")

    loss = lax.psum(loss_part[0, 0], ("x", "y", "c"))

    def small(s, c3):
        return (s[ROW_NORM_IN:ROW_NORM_IN + 1], None, c3[None, 0:3, :], s[ROW_SINKS:ROW_SINKS + 1, 0:N_Q_HEADS],
                s[ROW_NORM_CONV:ROW_NORM_CONV + 1], s[ROW_NORM_ATTN:ROW_NORM_ATTN + 1], None, s[ROW_NORM_FINAL])

    def leaves(s, c3, big_in, big_out):
        t = list(small(s, c3))
        t[1] = big_in[None]
        t[6] = big_out[None]
        return tuple(t)

    return (loss, grad_x.reshape(1, SEQ, D_MODEL),
            *leaves(gsum, cg, g_w_in, g_w_out),
            *leaves(dsl, cd, d_w_in, d_w_out),
            *leaves(msl, cmo, nm_w_in, nm_w_out),
            *leaves(vsl, cvo, nv_w_in, nv_w_out))
```

```python
import functools
import math

import jax
import jax.numpy as jnp
from jax import lax
from jax.experimental import pallas as pl
from jax.experimental.pallas import tpu as pltpu

F32 = jnp.float32
BF16 = jnp.bfloat16
MESH = pl.DeviceIdType.MESH

N_DEV = 8
SEQ = 2048
D_MODEL = 1024
D_CONV = 1024
D_ATTN = 1024
D_KV = 128
HEAD_DIM = 64
N_Q_HEADS = 16
N_PAIRS = N_Q_HEADS // 2
PAIRS_PER_KV = N_PAIRS // 2
D_MIX = D_CONV + D_ATTN
D_PROJ = 6400
SHARD_IN = D_PROJ // N_DEV
SHARD_OUT = D_MIX // N_DEV
SHARD_CONV = D_CONV // N_DEV
OFF_CB, OFF_CC, OFF_CU, OFF_GC, OFF_Q, OFF_K, OFF_V, OFF_GA = 0, 1024, 2048, 3072, 4096, 5120, 5248, 5376
BLOCK = 128
N_BLOCKS = SEQ // BLOCK
HALO = 8
RMS_EPS = 1e-5
NEG = -1e30
SCALE = HEAD_DIM ** -0.5
SLOPES = tuple(2.0 ** (-8.0 * (h + 1) / N_Q_HEADS) for h in range(N_Q_HEADS))

ADAM_LR = 0.001
ADAM_B1 = 0.9
ADAM_B2 = 0.999
ADAM_EPS = 1e-08
ADAM_WD = 0.01
ADAM_STEP = 10

ROW_NORM_IN, ROW_NORM_CONV, ROW_NORM_ATTN, ROW_NORM_FINAL, ROW_CONV0, ROW_SINKS = 0, 1, 2, 3, 4, 7
LOSS_LANE = N_Q_HEADS

VMEM_LIMIT = 56 * 1024 * 1024

_NT = (((1,), (1,)), ((), ()))
_TN = (((0,), (0,)), ((), ()))


def _params(**kw):
    return pltpu.CompilerParams(vmem_limit_bytes=VMEM_LIMIT, **kw)


def _adamw(w, g, m, v):
    m = ADAM_B1 * m + (1.0 - ADAM_B1) * g
    v = ADAM_B2 * v + (1.0 - ADAM_B2) * (g * g)
    m_hat = m / (1.0 - ADAM_B1 ** ADAM_STEP)
    v_hat = v / (1.0 - ADAM_B2 ** ADAM_STEP)
    delta = -ADAM_LR * (m_hat / (jnp.sqrt(v_hat) + ADAM_EPS) + ADAM_WD * w)
    return delta, m, v


def _sigmoid(t):
    return 1.0 / (1.0 + jnp.exp(-t))


def _slot(px, py, pc):
    return 4 * px + 2 * py + pc


def _all_gather(w_in_sh, w_out_sh, conv_sh):
    def body(win_ref, wout_ref, cv_ref, gin_ref, gout_ref, gcv_ref, send_sems, recv_sems):
        x, y, c = lax.axis_index("x"), lax.axis_index("y"), lax.axis_index("c")
        me, sibling = (x, y, c), (x, y, 1 - c)
        chips = [(1 - x, y), (x, 1 - y), (1 - x, 1 - y)]
        bufs = (gin_ref, gout_ref, gcv_ref)

        gin_ref[_slot(*me)] = win_ref[...].astype(BF16)
        gout_ref[_slot(*me)] = wout_ref[...].astype(BF16)
        gcv_ref[_slot(*me)] = cv_ref[...]

        def copy(b, k, block, to):
            s = _slot(*block)
            return pltpu.make_async_remote_copy(
                src_ref=bufs[b].at[s], dst_ref=bufs[b].at[s],
                send_sem=send_sems.at[7 * b + k], recv_sem=recv_sems.at[7 * b + k],
                device_id=to, device_id_type=MESH)

        started = []
        for b in range(3):
            first = [copy(b, 0, me, sibling)]
            first += [copy(b, 1 + j, me, (*chip, c)) for j, chip in enumerate(chips)]
            for cp in first:
                cp.start()
            started += first
        for b in range(3):
            for j, chip in enumerate(chips):
                copy(b, 1 + j, (*chip, c), me).wait_recv()
                fwd = copy(b, 4 + j, (*chip, c), sibling)
                fwd.start()
                started.append(fwd)
        for b in range(3):
            copy(b, 0, sibling, me).wait_recv()
            for j, chip in enumerate(chips):
                copy(b, 4 + j, (*chip, 1 - c), me).wait_recv()
        for cp in started:
            cp.wait_send()

    vmem = pl.BlockSpec(memory_space=pltpu.VMEM)
    return pl.pallas_call(
        body, name="all_gather",
        out_shape=(jax.ShapeDtypeStruct((N_DEV, SHARD_IN, D_MODEL), BF16),
                   jax.ShapeDtypeStruct((N_DEV, SHARD_OUT, D_MODEL), BF16),
                   jax.ShapeDtypeStruct((N_DEV, 8, SHARD_CONV), F32)),
        in_specs=[vmem, vmem, vmem], out_specs=(vmem, vmem, vmem),
        scratch_shapes=[pltpu.SemaphoreType.DMA((21,)), pltpu.SemaphoreType.DMA((21,))],
        compiler_params=_params(),
    )(w_in_sh, w_out_sh, conv_sh)


SEM_D2D_A, SEM_D2D_B, SEM_ICI_A, SEM_ICI_B, SEM_SMALL, N_RS_SEMS = 0, 4, 8, 11, 14, 21


def _reduce_scatter(dw_in, dw_out, gslab, wslab, mslab, vslab, cw, cm, cv):
    def body(a_ref, b_ref, gs_ref, ws_ref, ms_ref, vs_ref, cw_ref, cm_ref, cv_ref,
             ga_ref, gb_ref, gsum_ref, dsl_ref, msl_ref, vsl_ref, cg_ref, cd_ref, cmo_ref, cvo_ref,
             d2d_a, d2d_b, st_a, st_b, ici_a, ici_b, slabs, send_sems, recv_sems):
        x, y, c = lax.axis_index("x"), lax.axis_index("y"), lax.axis_index("c")
        me, sibling = (x, y, c), (x, y, 1 - c)
        chips = [(x, y), (1 - x, y), (x, 1 - y), (1 - x, 1 - y)]
        peers = [(x, y, 1 - c), (1 - x, y, c), (x, 1 - y, c), (1 - x, 1 - y, c),
                 (1 - x, y, 1 - c), (x, 1 - y, 1 - c), (1 - x, 1 - y, 1 - c)]

        def rcopy(src, dst, k, to):
            return pltpu.make_async_remote_copy(src_ref=src, dst_ref=dst, send_sem=send_sems.at[k],
                                                recv_sem=recv_sems.at[k], device_id=to, device_id_type=MESH)

        started = []
        slabs[_slot(*me)] = gs_ref[...]
        for k, peer in enumerate(peers):
            cp = rcopy(gs_ref, slabs.at[_slot(*me)], SEM_SMALL + k, peer)
            cp.start()
            started.append(cp)
        for k, chip in enumerate(chips):
            s = _slot(*chip, 1 - c)
            for src, dst, base in ((a_ref, d2d_a, SEM_D2D_A), (b_ref, d2d_b, SEM_D2D_B)):
                cp = rcopy(src.at[s], dst.at[k], base + k, sibling)
                cp.start()
                started.append(cp)
        for k in range(1, 4):
            s = _slot(*chips[k], c)
            for src, d2d, st, ici, b_d2d, b_ici in ((a_ref, d2d_a, st_a, ici_a, SEM_D2D_A, SEM_ICI_A),
                                                    (b_ref, d2d_b, st_b, ici_b, SEM_D2D_B, SEM_ICI_B)):
                rcopy(src.at[0], d2d.at[k], b_d2d + k, me).wait_recv()
                st[k - 1] = (src[s].astype(F32) + d2d[k].astype(F32)).astype(BF16)
                cp = rcopy(st.at[k - 1], ici.at[k - 1], b_ici + k - 1, (*chips[k], c))
                cp.start()
                started.append(cp)
        for src, d2d, ici, out, b_d2d, b_ici in ((a_ref, d2d_a, ici_a, ga_ref, SEM_D2D_A, SEM_ICI_A),
                                                 (b_ref, d2d_b, ici_b, gb_ref, SEM_D2D_B, SEM_ICI_B)):
            rcopy(src.at[0], d2d.at[0], b_d2d, me).wait_recv()
            acc = src[_slot(*me)].astype(F32) + d2d[0].astype(F32)
            for k in range(1, 4):
                rcopy(src.at[0], ici.at[k - 1], b_ici + k - 1, me).wait_recv()
                acc = acc + ici[k - 1].astype(F32)
            out[...] = acc
        for k in range(7):
            rcopy(gs_ref, slabs.at[0], SEM_SMALL + k, me).wait_recv()
        gsum = slabs[0]
        for d in range(1, N_DEV):
            gsum = gsum + slabs[d]
        gsum_ref[...] = gsum
        delta, m, v = _adamw(ws_ref[...], gsum, ms_ref[...], vs_ref[...])
        dsl_ref[...] = delta
        msl_ref[...] = m
        vsl_ref[...] = v
        idx = _slot(*me)
        cg = jnp.zeros((8, SHARD_CONV), F32)
        for d in range(N_DEV):
            cg = jnp.where(idx == d, gsum[:, d * SHARD_CONV:(d + 1) * SHARD_CONV], cg)
        cg = pltpu.roll(cg, 8 - ROW_CONV0, 0)
        row = lax.broadcasted_iota(jnp.int32, (8, SHARD_CONV), 0)
        cg = jnp.where(row < 3, cg, 0.0)
        cg_ref[...] = cg
        delta, m, v = _adamw(cw_ref[...], cg, cm_ref[...], cv_ref[...])
        cd_ref[...] = delta
        cmo_ref[...] = m
        cvo_ref[...] = v
        for cp in started:
            cp.wait_send()

    vmem = pl.BlockSpec(memory_space=pltpu.VMEM)
    slab = jax.ShapeDtypeStruct((8, D_MODEL), F32)
    cslab = jax.ShapeDtypeStruct((8, SHARD_CONV), F32)
    return pl.pallas_call(
        body, name="reduce_scatter",
        out_shape=(jax.ShapeDtypeStruct((SHARD_IN, D_MODEL), F32), jax.ShapeDtypeStruct((SHARD_OUT, D_MODEL), F32),
                   slab, slab, slab, slab, cslab, cslab, cslab, cslab),
        in_specs=[vmem] * 9, out_specs=tuple([vmem] * 10),
        scratch_shapes=[
            pltpu.VMEM((4, SHARD_IN, D_MODEL), BF16), pltpu.VMEM((4, SHARD_OUT, D_MODEL), BF16),
            pltpu.VMEM((3, SHARD_IN, D_MODEL), BF16), pltpu.VMEM((3, SHARD_OUT, D_MODEL), BF16),
            pltpu.VMEM((3, SHARD_IN, D_MODEL), BF16), pltpu.VMEM((3, SHARD_OUT, D_MODEL), BF16),
            pltpu.VMEM((N_DEV, 8, D_MODEL), F32),
            pltpu.SemaphoreType.DMA((N_RS_SEMS,)), pltpu.SemaphoreType.DMA((N_RS_SEMS,))],
        compiler_params=_params(),
    )(dw_in, dw_out, gslab, wslab, mslab, vslab, cw, cm, cv)


def _in_proj(x, norm_in, w_full):
    tm = 256

    def body(x_ref, g_ref, w_ref, h_ref, proj_ref):
        xv = x_ref[...]
        r = lax.rsqrt(jnp.mean(xv * xv, axis=-1, keepdims=True) + RMS_EPS)
        h = (xv * r * g_ref[...]).astype(BF16)
        h_ref[...] = h
        proj_ref[...] = lax.dot_general(h, w_ref[...], _NT, preferred_element_type=F32)

    return pl.pallas_call(
        body, name="in_proj", grid=(SEQ // tm,),
        in_specs=[pl.BlockSpec((tm, D_MODEL), lambda i: (i, 0)), pl.BlockSpec((1, D_MODEL), lambda i: (0, 0)),
                  pl.BlockSpec(memory_space=pltpu.VMEM)],
        out_specs=(pl.BlockSpec((tm, D_MODEL), lambda i: (i, 0)), pl.BlockSpec((tm, D_PROJ), lambda i: (i, 0))),
        out_shape=(jax.ShapeDtypeStruct((SEQ, D_MODEL), BF16), jax.ShapeDtypeStruct((SEQ, D_PROJ), F32)),
        compiler_params=_params(dimension_semantics=("arbitrary",)),
    )(x, norm_in, w_full)


def _conv_fwd(pj, cch_ref, cuh_ref, cw_ref, n):
    cc = pj[:, OFF_CC:OFF_CC + D_CONV]
    cu = pj[:, OFF_CU:OFF_CU + D_CONV]
    z = cc * cu
    zh = jnp.where(n > 0, cch_ref[...] * cuh_ref[...], 0.0)
    row = lax.broadcasted_iota(jnp.int32, (BLOCK, D_CONV), 0)
    z1 = jnp.where(row == 0, zh[HALO - 1:HALO, :], pltpu.roll(z, 1, 0))
    z2 = jnp.where(row == 0, zh[HALO - 2:HALO - 1, :], jnp.where(row == 1, zh[HALO - 1:HALO, :], pltpu.roll(z, 2, 0)))
    co = cw_ref[0:1, :] * z2 + cw_ref[1:2, :] * z1 + cw_ref[2:3, :] * z
    return cc, cu, z, z1, z2, co


def _kv_bands(pj, kvp_ref):
    lane = lax.broadcasted_iota(jnp.int32, (2 * BLOCK, D_KV), 1)
    lo = lane < HEAD_DIM

    def bands(prev, cur):
        b = jnp.concatenate([prev, cur], axis=0)
        br = pltpu.roll(b, HEAD_DIM, 1)
        zero = jnp.zeros_like(b)
        return ((jnp.where(lo, b, zero).astype(BF16), jnp.where(lo, zero, br).astype(BF16)),
                (jnp.where(lo, br, zero).astype(BF16), jnp.where(lo, zero, b).astype(BF16)))

    ks = bands(kvp_ref[:, 0:D_KV], pj[:, OFF_K:OFF_K + D_KV])
    vs = bands(kvp_ref[:, D_KV:2 * D_KV], pj[:, OFF_V:OFF_V + D_KV])
    return ks, vs


def _attn_bias(n):
    qi = lax.broadcasted_iota(jnp.int32, (BLOCK, 2 * BLOCK), 0)
    kj = lax.broadcasted_iota(jnp.int32, (BLOCK, 2 * BLOCK), 1)
    dist = BLOCK + qi - kj
    valid = (dist >= 0) & (dist < BLOCK) & ((kj >= BLOCK) | (n > 0))
    return dist.astype(F32), valid


def _attn_probs(qp, kband, h, distf, valid, sink):
    s = lax.dot_general(qp, kband, _NT, preferred_element_type=F32) * SCALE
    s = jnp.where(valid, s - SLOPES[h] * distf, NEG)
    m = jnp.maximum(jnp.max(s, axis=-1, keepdims=True), sink)
    p = jnp.exp(s - m)
    es = jnp.exp(sink - m)
    inv = 1.0 / (jnp.sum(p, axis=-1, keepdims=True) + es)
    return p * inv, es * inv


def _mix_fwd(proj, conv_full, sinks, norm_conv, norm_attn):
    def body(pj_ref, kvp_ref, cch_ref, cuh_ref, cw_ref, sink_ref, gc_ref, ga_ref, mixed_ref, attn_scr):
        n = pl.program_id(0)
        pj = pj_ref
        _, _, _, _, _, co = _conv_fwd(pj, cch_ref, cuh_ref, cw_ref, n)
        a_c = pj[:, OFF_CB:OFF_CB + D_CONV] * co
        r_c = lax.rsqrt(jnp.mean(a_c * a_c, axis=-1, keepdims=True) + RMS_EPS)
        t_c = pj[:, OFF_GC:OFF_GC + D_CONV]
        mixed_ref[:, 0:D_CONV] = (a_c * r_c * gc_ref[...] * (t_c * _sigmoid(t_c))).astype(BF16)

        ks, vs = _kv_bands(pj, kvp_ref)
        distf, valid = _attn_bias(n)
        for i in range(N_PAIRS):
            j = i // PAIRS_PER_KV
            qp = pj[:, OFF_Q + 128 * i:OFF_Q + 128 * (i + 1)].astype(BF16)
            out = None
            for e in range(2):
                h = 2 * i + e
                p, _ = _attn_probs(qp, ks[j][e], h, distf, valid, sink_ref[h])
                o = jnp.dot(p.astype(BF16), vs[j][e], preferred_element_type=F32)
                out = o if out is None else out + o
            attn_scr[:, 128 * i:128 * (i + 1)] = out
        a_a = attn_scr[...]
        r_a = lax.rsqrt(jnp.mean(a_a * a_a, axis=-1, keepdims=True) + RMS_EPS)
        t_a = pj[:, OFF_GA:OFF_GA + D_ATTN]
        mixed_ref[:, D_CONV:D_MIX] = (a_a * r_a * ga_ref[...] * (t_a * _sigmoid(t_a))).astype(BF16)

    per_block = BLOCK // HALO
    return pl.pallas_call(
        body, name="mix_fwd", grid=(N_BLOCKS,),
        in_specs=[
            pl.BlockSpec((BLOCK, D_PROJ), lambda n: (n, 0)),
            pl.BlockSpec((BLOCK, 2 * D_KV), lambda n: (jnp.maximum(n - 1, 0), OFF_K // (2 * D_KV))),
            pl.BlockSpec((HALO, D_CONV), lambda n: (jnp.maximum(n * per_block - 1, 0), OFF_CC // D_CONV)),
            pl.BlockSpec((HALO, D_CONV), lambda n: (jnp.maximum(n * per_block - 1, 0), OFF_CU // D_CONV)),
            pl.BlockSpec((8, D_CONV), lambda n: (0, 0)),
            pl.BlockSpec(memory_space=pltpu.SMEM),
            pl.BlockSpec((1, D_CONV), lambda n: (0, 0)),
            pl.BlockSpec((1, D_ATTN), lambda n: (0, 0)),
        ],
        out_specs=pl.BlockSpec((BLOCK, D_MIX), lambda n: (n, 0)),
        out_shape=jax.ShapeDtypeStruct((SEQ, D_MIX), BF16),
        scratch_shapes=[pltpu.VMEM((BLOCK, D_ATTN), F32)],
        compiler_params=_params(dimension_semantics=("arbitrary",)),
    )(proj, proj, proj, proj, conv_full, sinks, norm_conv, norm_attn)


def _out_proj_loss(mixed, x, target, w_out_full, norm_final):
    tm = 256

    def body(mx_ref, x_ref, t_ref, w_ref, g_ref, dx2_ref, dx2b_ref, dmix_ref, gnf_ref, loss_ref):
        i = pl.program_id(0)
        w = w_ref[...]
        x2 = x_ref[...] + jnp.dot(mx_ref[...], w, preferred_element_type=F32)
        r = lax.rsqrt(jnp.mean(x2 * x2, axis=-1, keepdims=True) + RMS_EPS)
        xn = x2 * r
        g = g_ref[...]
        err = xn * g - t_ref[...]
        part = 0.5 * jnp.sum(jnp.mean(err * err, axis=-1, keepdims=True), axis=0, keepdims=True)
        dy = err * (1.0 / D_MODEL)
        gnf = jnp.sum(dy * xn, axis=0, keepdims=True)
        u = dy * g
        dx2 = r * (u - xn * jnp.mean(u * xn, axis=-1, keepdims=True))
        dx2_ref[...] = dx2
        dx2b = dx2.astype(BF16)
        dx2b_ref[...] = dx2b
        dmix_ref[...] = lax.dot_general(dx2b, w, _NT, preferred_element_type=F32)

        @pl.when(i == 0)
        def _():
            gnf_ref[...] = jnp.zeros_like(gnf_ref)
            loss_ref[...] = jnp.zeros_like(loss_ref)

        gnf_ref[...] += gnf
        loss_ref[...] += jnp.broadcast_to(part, loss_ref.shape)

    return pl.pallas_call(
        body, name="out_proj_loss", grid=(SEQ // tm,),
        in_specs=[pl.BlockSpec((tm, D_MIX), lambda i: (i, 0)), pl.BlockSpec((tm, D_MODEL), lambda i: (i, 0)),
                  pl.BlockSpec((tm, D_MODEL), lambda i: (i, 0)), pl.BlockSpec(memory_space=pltpu.VMEM),
                  pl.BlockSpec((1, D_MODEL), lambda i: (0, 0))],
        out_specs=(pl.BlockSpec((tm, D_MODEL), lambda i: (i, 0)), pl.BlockSpec((tm, D_MODEL), lambda i: (i, 0)),
                   pl.BlockSpec((tm, D_MIX), lambda i: (i, 0)),
                   pl.BlockSpec((1, D_MODEL), lambda i: (0, 0)), pl.BlockSpec((8, 128), lambda i: (0, 0))),
        out_shape=(jax.ShapeDtypeStruct((SEQ, D_MODEL), F32), jax.ShapeDtypeStruct((SEQ, D_MODEL), BF16),
                   jax.ShapeDtypeStruct((SEQ, D_MIX), F32),
                   jax.ShapeDtypeStruct((1, D_MODEL), F32), jax.ShapeDtypeStruct((8, 128), F32)),
        compiler_params=_params(dimension_semantics=("arbitrary",)),
    )(mixed, x, target, w_out_full, norm_final)


def _gated_norm_bwd(a, gain, t, dy):
    r = lax.rsqrt(jnp.mean(a * a, axis=-1, keepdims=True) + RMS_EPS)
    an = a * r
    sg = _sigmoid(t)
    dn = dy * (t * sg)
    dt = dy * (an * gain) * (sg * (1.0 + t * (1.0 - sg)))
    dgain = jnp.sum(dn * an, axis=0, keepdims=True)
    u = dn * gain
    da = r * (u - an * jnp.mean(u * an, axis=-1, keepdims=True))
    return da, dt, dgain


def _mix_bwd(proj, dmixed, conv_full, sinks, norm_conv, norm_attn):
    def body(pj_ref, kvp_ref, cch_ref, cuh_ref, dmx_ref, cw_ref, sink_ref, gc_ref, ga_ref,
             dpj_ref, gslab_ref, attn_scr, dattn_scr, p_scr, ps_scr, nxt_scr, dkv_scr):
        step = pl.program_id(0)
        n = N_BLOCKS - 1 - step
        pj = pj_ref

        @pl.when(step == 0)
        def _():
            gslab_ref[...] = jnp.zeros_like(gslab_ref)
            nxt_scr[...] = jnp.zeros_like(nxt_scr)
            dkv_scr[...] = jnp.zeros_like(dkv_scr)

        cc, cu, z, z1, z2, co = _conv_fwd(pj, cch_ref, cuh_ref, cw_ref, n)
        cb = pj[:, OFF_CB:OFF_CB + D_CONV]
        da_c, dgate_c, dgain_c = _gated_norm_bwd(cb * co, gc_ref[...], pj[:, OFF_GC:OFF_GC + D_CONV],
                                                 dmx_ref[:, 0:D_CONV])
        dpj_ref[:, OFF_GC:OFF_GC + D_CONV] = dgate_c.astype(BF16)
        dpj_ref[:, OFF_CB:OFF_CB + D_CONV] = (da_c * co).astype(BF16)
        dco = da_c * cb
        nxt = nxt_scr[...]
        row = lax.broadcasted_iota(jnp.int32, (BLOCK, D_CONV), 0)
        dco1 = jnp.where(row == BLOCK - 1, nxt[0:1, :], pltpu.roll(dco, BLOCK - 1, 0))
        dco2 = jnp.where(row == BLOCK - 2, nxt[0:1, :],
                         jnp.where(row == BLOCK - 1, nxt[1:2, :], pltpu.roll(dco, BLOCK - 2, 0)))
        dz = cw_ref[2:3, :] * dco + cw_ref[1:2, :] * dco1 + cw_ref[0:1, :] * dco2
        dpj_ref[:, OFF_CC:OFF_CC + D_CONV] = (dz * cu).astype(BF16)
        dpj_ref[:, OFF_CU:OFF_CU + D_CONV] = (dz * cc).astype(BF16)
        nxt_scr[...] = dco[0:HALO, :]
        gslab_ref[ROW_NORM_CONV:ROW_NORM_CONV + 1, :] += dgain_c
        gslab_ref[ROW_CONV0:ROW_CONV0 + 1, :] += jnp.sum(dco * z2, axis=0, keepdims=True)
        gslab_ref[ROW_CONV0 + 1:ROW_CONV0 + 2, :] += jnp.sum(dco * z1, axis=0, keepdims=True)
        gslab_ref[ROW_CONV0 + 2:ROW_CONV0 + 3, :] += jnp.sum(dco * z, axis=0, keepdims=True)

        ks, vs = _kv_bands(pj, kvp_ref)
        distf, valid = _attn_bias(n)
        for i in range(N_PAIRS):
            j = i // PAIRS_PER_KV
            qp = pj[:, OFF_Q + 128 * i:OFF_Q + 128 * (i + 1)].astype(BF16)
            out = None
            for e in range(2):
                h = 2 * i + e
                p, psink = _attn_probs(qp, ks[j][e], h, distf, valid, sink_ref[h])
                p_scr[h] = p
                ps_scr[h] = psink
                o = jnp.dot(p.astype(BF16), vs[j][e], preferred_element_type=F32)
                out = o if out is None else out + o
            attn_scr[:, 128 * i:128 * (i + 1)] = out
        da_a, dgate_a, dgain_a = _gated_norm_bwd(attn_scr[...], ga_ref[...], pj[:, OFF_GA:OFF_GA + D_ATTN],
                                                 dmx_ref[:, D_CONV:D_MIX])
        dpj_ref[:, OFF_GA:OFF_GA + D_ATTN] = dgate_a.astype(BF16)
        dattn_scr[...] = da_a
        gslab_ref[ROW_NORM_ATTN:ROW_NORM_ATTN + 1, :] += dgain_a

        lane = lax.broadcasted_iota(jnp.int32, (BLOCK, 128), 1)
        lo = lane < HEAD_DIM
        lane_s = lax.broadcasted_iota(jnp.int32, (1, D_MODEL), 1)
        gsink = jnp.zeros((1, D_MODEL), F32)
        dkv_full = []
        for j in range(2):
            dk_f = jnp.zeros((2 * BLOCK, 128), F32)
            dv_f = jnp.zeros((2 * BLOCK, 128), F32)
            for i in range(j * PAIRS_PER_KV, (j + 1) * PAIRS_PER_KV):
                q_f = pj[:, OFF_Q + 128 * i:OFF_Q + 128 * (i + 1)]
                do_f = dattn_scr[:, 128 * i:128 * (i + 1)]
                prod = do_f * attn_scr[:, 128 * i:128 * (i + 1)]
                tot = jnp.sum(prod, axis=-1, keepdims=True)
                d_lo = jnp.sum(jnp.where(lo, prod, 0.0), axis=-1, keepdims=True)
                deltas = (d_lo, tot - d_lo)
                do_b = do_f.astype(BF16)
                dq = None
                for e in range(2):
                    h = 2 * i + e
                    half = lo if e == 0 else jnp.logical_not(lo)
                    p = p_scr[h]
                    dp = lax.dot_general(do_b, vs[j][e], _NT, preferred_element_type=F32)
                    ds = (p * (dp - deltas[e]) * SCALE).astype(BF16)
                    gs = -jnp.sum(ps_scr[h] * deltas[e], axis=0, keepdims=True)
                    gsink = gsink + jnp.where(lane_s == h, gs, 0.0)
                    t = jnp.dot(ds, ks[j][e], preferred_element_type=F32)
                    dq = t if dq is None else dq + t
                    dk_f = dk_f + lax.dot_general(ds, jnp.where(half, q_f, 0.0).astype(BF16), _TN,
                                                  preferred_element_type=F32)
                    dv_f = dv_f + lax.dot_general(p.astype(BF16), jnp.where(half, do_f, 0.0).astype(BF16), _TN,
                                                  preferred_element_type=F32)
                dpj_ref[:, OFF_Q + 128 * i:OFF_Q + 128 * (i + 1)] = dq.astype(BF16)
            dkv_full.append((dk_f + pltpu.roll(dk_f, HEAD_DIM, 1), dv_f + pltpu.roll(dv_f, HEAD_DIM, 1)))
        lane2 = lax.broadcasted_iota(jnp.int32, (2 * BLOCK, 128), 1)
        lo2 = lane2 < HEAD_DIM
        dk = jnp.where(lo2, dkv_full[0][0], dkv_full[1][0])
        dv = jnp.where(lo2, dkv_full[0][1], dkv_full[1][1])
        dpj_ref[:, OFF_K:OFF_K + D_KV] = (dk[BLOCK:, :] + dkv_scr[:, 0:D_KV]).astype(BF16)
        dpj_ref[:, OFF_V:OFF_V + D_KV] = (dv[BLOCK:, :] + dkv_scr[:, D_KV:2 * D_KV]).astype(BF16)
        dkv_scr[:, 0:D_KV] = dk[:BLOCK, :]
        dkv_scr[:, D_KV:2 * D_KV] = dv[:BLOCK, :]
        gslab_ref[ROW_SINKS:ROW_SINKS + 1, :] += gsink

    per_block = BLOCK // HALO
    last = N_BLOCKS - 1
    return pl.pallas_call(
        body, name="mix_bwd", grid=(N_BLOCKS,),
        in_specs=[
            pl.BlockSpec((BLOCK, D_PROJ), lambda s: (last - s, 0)),
            pl.BlockSpec((BLOCK, 2 * D_KV), lambda s: (jnp.maximum(last - s - 1, 0), OFF_K // (2 * D_KV))),
            pl.BlockSpec((HALO, D_CONV), lambda s: (jnp.maximum((last - s) * per_block - 1, 0), OFF_CC // D_CONV)),
            pl.BlockSpec((HALO, D_CONV), lambda s: (jnp.maximum((last - s) * per_block - 1, 0), OFF_CU // D_CONV)),
            pl.BlockSpec((BLOCK, D_MIX), lambda s: (last - s, 0)),
            pl.BlockSpec((8, D_CONV), lambda s: (0, 0)),
            pl.BlockSpec(memory_space=pltpu.SMEM),
            pl.BlockSpec((1, D_CONV), lambda s: (0, 0)),
            pl.BlockSpec((1, D_ATTN), lambda s: (0, 0)),
        ],
        out_specs=(pl.BlockSpec((BLOCK, D_PROJ), lambda s: (last - s, 0)),
                   pl.BlockSpec((8, D_MODEL), lambda s: (0, 0))),
        out_shape=(jax.ShapeDtypeStruct((SEQ, D_PROJ), BF16), jax.ShapeDtypeStruct((8, D_MODEL), F32)),
        scratch_shapes=[pltpu.VMEM((BLOCK, D_ATTN), F32), pltpu.VMEM((BLOCK, D_ATTN), F32),
                        pltpu.VMEM((N_Q_HEADS, BLOCK, 2 * BLOCK), F32), pltpu.VMEM((N_Q_HEADS, BLOCK, 1), F32),
                        pltpu.VMEM((HALO, D_CONV), F32), pltpu.VMEM((BLOCK, 2 * D_KV), F32)],
        compiler_params=_params(dimension_semantics=("arbitrary",)),
    )(proj, proj, proj, proj, dmixed, conv_full, sinks, norm_conv, norm_attn)


def _in_bwd(dproj, w_full, x, dx2, norm_in):
    tm = 256

    def body(dp_ref, w_ref, x_ref, dx2_ref, g_ref, gx_ref, gni_ref):
        i = pl.program_id(0)
        dh = jnp.dot(dp_ref[...], w_ref[...], preferred_element_type=F32)
        xv = x_ref[...]
        r = lax.rsqrt(jnp.mean(xv * xv, axis=-1, keepdims=True) + RMS_EPS)
        xn = xv * r
        u = dh * g_ref[...]
        gx_ref[...] = dx2_ref[...] + r * (u - xn * jnp.mean(u * xn, axis=-1, keepdims=True))

        @pl.when(i == 0)
        def _():
            gni_ref[...] = jnp.zeros_like(gni_ref)

        gni_ref[...] += jnp.sum(dh * xn, axis=0, keepdims=True)

    return pl.pallas_call(
        body, name="in_bwd", grid=(SEQ // tm,),
        in_specs=[pl.BlockSpec((tm, D_PROJ), lambda i: (i, 0)), pl.BlockSpec(memory_space=pltpu.VMEM),
                  pl.BlockSpec((tm, D_MODEL), lambda i: (i, 0)), pl.BlockSpec((tm, D_MODEL), lambda i: (i, 0)),
                  pl.BlockSpec((1, D_MODEL), lambda i: (0, 0))],
        out_specs=(pl.BlockSpec((tm, D_MODEL), lambda i: (i, 0)), pl.BlockSpec((1, D_MODEL), lambda i: (0, 0))),
        out_shape=(jax.ShapeDtypeStruct((SEQ, D_MODEL), F32), jax.ShapeDtypeStruct((1, D_MODEL), F32)),
        compiler_params=_params(dimension_semantics=("arbitrary",)),
    )(dproj, w_full, x, dx2, norm_in)


def _matmul_tn(a, b, tn, name):
    k, n = a.shape
    _, m = b.shape

    def body(a_ref, b_ref, o_ref):
        o_ref[...] = lax.dot_general(a_ref[...], b_ref[...], _TN, preferred_element_type=F32).astype(BF16)

    return pl.pallas_call(
        body, name=name, grid=(n // tn,),
        in_specs=[pl.BlockSpec((k, tn), lambda i: (0, i)), pl.BlockSpec(memory_space=pltpu.VMEM)],
        out_specs=pl.BlockSpec((tn, m), lambda i: (i, 0)),
        out_shape=jax.ShapeDtypeStruct((n, m), BF16),
        compiler_params=_params(dimension_semantics=("arbitrary",)),
    )(a, b)


def _adam_shard(w, g, m, v, tr, name):
    rows, cols = w.shape

    def body(w_ref, g_ref, m_ref, v_ref, d_ref, mo_ref, vo_ref):
        delta, mn, vn = _adamw(w_ref[...], g_ref[...], m_ref[...], v_ref[...])
        d_ref[...] = delta
        mo_ref[...] = mn
        vo_ref[...] = vn

    spec = pl.BlockSpec((tr, cols), lambda i: (i, 0))
    out = jax.ShapeDtypeStruct((rows, cols), F32)
    return pl.pallas_call(
        body, name=name, grid=(rows // tr,),
        in_specs=[spec] * 4, out_specs=(spec, spec, spec), out_shape=(out, out, out),
        compiler_params=_params(dimension_semantics=("arbitrary",)),
    )(w, g, m, v)


def _pad_rows(a, rows=8):
    return jnp.pad(a, ((0, rows - a.shape[0]), (0, 0)))


def _small_slab(norm_in, norm_conv, norm_attn, norm_final, sinks):
    return jnp.concatenate([norm_in, norm_conv, norm_attn, norm_final.reshape(1, D_MODEL),
                            jnp.zeros((3, D_MODEL), F32), jnp.pad(sinks, ((0, 0), (0, D_MODEL - N_Q_HEADS)))], axis=0)


def kernel(x, norm_in, w_in, conv_w, attn_sinks, norm_conv_out, norm_attn_out, w_out, norm_final, loss_target, m_norm_in, m_w_in, m_conv_w, m_attn_sinks, m_norm_conv_out, m_norm_attn_out, m_w_out, m_norm_final, v_norm_in, v_w_in, v_conv_w, v_attn_sinks, v_norm_conv_out, v_norm_attn_out, v_w_out, v_norm_final):
    x2d = x.reshape(SEQ, D_MODEL)
    target = loss_target.reshape(SEQ, D_MODEL)
    nf = norm_final.reshape(1, D_MODEL)

    w_in_t, m_w_in_t, v_w_in_t = w_in[0].T, m_w_in[0].T, v_w_in[0].T
    g_in, g_out, g_cv = _all_gather(w_in_t, w_out[0], _pad_rows(conv_w[0]))
    w_in_full = g_in.reshape(D_PROJ, D_MODEL)
    w_out_full = g_out.reshape(D_MIX, D_MODEL)
    conv_full = jnp.transpose(g_cv, (1, 0, 2)).reshape(8, D_CONV)
    sinks = attn_sinks.reshape(N_Q_HEADS)

    h, proj = _in_proj(x2d, norm_in, w_in_full)
    mixed = _mix_fwd(proj, conv_full, sinks, norm_conv_out, norm_attn_out)
    dx2, dx2b, dmixed, gnf, loss_part = _out_proj_loss(mixed, x2d, target, w_out_full, nf)
    dproj, gslab = _mix_bwd(proj, dmixed, conv_full, sinks, norm_conv_out, norm_attn_out)
    grad_x, gni = _in_bwd(dproj, w_in_full, x2d, dx2, norm_in)
    dw_in = _matmul_tn(dproj, h, 640, "dw_in")
    dw_out = _matmul_tn(mixed, dx2b, 512, "dw_out")

    row = lax.broadcasted_iota(jnp.int32, (8, D_MODEL), 0)
    lane = lax.broadcasted_iota(jnp.int32, (8, D_MODEL), 1)
    gslab = jnp.where(row == ROW_NORM_IN, gni, jnp.where(row == ROW_NORM_FINAL, gnf, gslab))
    gslab = jnp.where((row == ROW_SINKS) & (lane == LOSS_LANE), loss_part[0, 0], gslab)
    dw_in_sh = dw_in.reshape(N_DEV, SHARD_IN, D_MODEL)
    dw_out_sh = dw_out.reshape(N_DEV, SHARD_OUT, D_MODEL)
    wslab = _small_slab(norm_in, norm_conv_out, norm_attn_out, norm_final, attn_sinks)
    mslab = _small_slab(m_norm_in, m_norm_conv_out, m_norm_attn_out, m_norm_final, m_attn_sinks)
    vslab = _small_slab(v_norm_in, v_norm_conv_out, v_norm_attn_out, v_norm_final, v_attn_sinks)
    (g_w_in, g_w_out, gsum, dsl, msl, vsl, cg, cd, cmo, cvo) = _reduce_scatter(
        dw_in_sh, dw_out_sh, gslab, wslab, mslab, vslab,
        _pad_rows(conv_w[0]), _pad_rows(m_conv_w[0]), _pad_rows(v_conv_w[0]))
    d_w_in, nm_w_in, nv_w_in = _adam_shard(w_in_t, g_w_in, m_w_in_t, v_w_in_t, 200, "adam_in")
    d_w_out, nm_w_out, nv_w_out = _adam_shard(w_out[0], g_w_out, m_w_out[0], v_w_out[0], 64, "adam_out")
    g_w_in, d_w_in, nm_w_in, nv_w_in = g_w_in.T, d_w_in.T, nm_w_in.T, nv_w_in.T

    loss = gsum[ROW_SINKS, LOSS_LANE]

    def small(s, c3):
        return (s[ROW_NORM_IN:ROW_NORM_IN + 1], None, c3[None, 0:3, :], s[ROW_SINKS:ROW_SINKS + 1, 0:N_Q_HEADS],
                s[ROW_NORM_CONV:ROW_NORM_CONV + 1], s[ROW_NORM_ATTN:ROW_NORM_ATTN + 1], None, s[ROW_NORM_FINAL])

    def leaves(s, c3, big_in, big_out):
        t = list(small(s, c3))
        t[1] = big_in[None]
        t[6] = big_out[None]
        return tuple(t)

    return (loss, grad_x.reshape(1, SEQ, D_MODEL),
            *leaves(gsum, cg, g_w_in, g_w_out),
            *leaves(dsl, cd, d_w_in, d_w_out),
            *leaves(msl, cmo, nm_w_in, nm_w_out),
            *leaves(vsl, cvo, nv_w_in, nv_w_out))
```

```python
import functools
import math

import jax
import jax.numpy as jnp
from jax import lax
from jax.experimental import pallas as pl
from jax.experimental.pallas import tpu as pltpu

F32 = jnp.float32
BF16 = jnp.bfloat16
MESH = pl.DeviceIdType.MESH

N_DEV = 8
SEQ = 2048
D_MODEL = 1024
D_CONV = 1024
D_ATTN = 1024
D_KV = 128
HEAD_DIM = 64
N_Q_HEADS = 16
N_PAIRS = N_Q_HEADS // 2
PAIRS_PER_KV = N_PAIRS // 2
D_MIX = D_CONV + D_ATTN
D_PROJ = 6400
SHARD_IN = D_PROJ // N_DEV
SHARD_OUT = D_MIX // N_DEV
SHARD_CONV = D_CONV // N_DEV
OFF_CB, OFF_CC, OFF_CU, OFF_GC, OFF_Q, OFF_K, OFF_V, OFF_GA = 0, 1024, 2048, 3072, 4096, 5120, 5248, 5376
BLOCK = 128
N_BLOCKS = SEQ // BLOCK
HALO = 8
RMS_EPS = 1e-5
NEG = -1e30
SCALE = HEAD_DIM ** -0.5
SLOPES = tuple(2.0 ** (-8.0 * (h + 1) / N_Q_HEADS) for h in range(N_Q_HEADS))

ADAM_LR = 0.001
ADAM_B1 = 0.9
ADAM_B2 = 0.999
ADAM_EPS = 1e-08
ADAM_WD = 0.01
ADAM_STEP = 10

ROW_NORM_IN, ROW_NORM_CONV, ROW_NORM_ATTN, ROW_NORM_FINAL, ROW_CONV0, ROW_SINKS = 0, 1, 2, 3, 4, 7
LOSS_LANE = N_Q_HEADS

VMEM_LIMIT = 56 * 1024 * 1024

_NT = (((1,), (1,)), ((), ()))
_TN = (((0,), (0,)), ((), ()))


def _params(**kw):
    return pltpu.CompilerParams(vmem_limit_bytes=VMEM_LIMIT, **kw)


def _adamw(w, g, m, v):
    m = ADAM_B1 * m + (1.0 - ADAM_B1) * g
    v = ADAM_B2 * v + (1.0 - ADAM_B2) * (g * g)
    m_hat = m / (1.0 - ADAM_B1 ** ADAM_STEP)
    v_hat = v / (1.0 - ADAM_B2 ** ADAM_STEP)
    delta = -ADAM_LR * (m_hat / (jnp.sqrt(v_hat) + ADAM_EPS) + ADAM_WD * w)
    return delta, m, v


def _sigmoid(t):
    return 1.0 / (1.0 + jnp.exp(-t))


def _slot(px, py, pc):
    return 4 * px + 2 * py + pc


def _all_gather(w_in_sh, w_out_sh, conv_sh):
    def body(win_ref, wout_ref, cv_ref, gin_ref, woutb_ref, conv_ref, gcv_ref, send_sems, recv_sems):
        x, y, c = lax.axis_index("x"), lax.axis_index("y"), lax.axis_index("c")
        me, sibling = (x, y, c), (x, y, 1 - c)
        chips = [(1 - x, y), (x, 1 - y), (1 - x, 1 - y)]
        bufs = (gin_ref, gcv_ref)

        gin_ref[_slot(*me)] = win_ref[...].astype(BF16)
        gcv_ref[_slot(*me)] = cv_ref[...]

        def copy(b, k, block, to):
            s = _slot(*block)
            return pltpu.make_async_remote_copy(
                src_ref=bufs[b].at[s], dst_ref=bufs[b].at[s],
                send_sem=send_sems.at[7 * b + k], recv_sem=recv_sems.at[7 * b + k],
                device_id=to, device_id_type=MESH)

        started = []
        for b in range(2):
            first = [copy(b, 0, me, sibling)]
            first += [copy(b, 1 + j, me, (*chip, c)) for j, chip in enumerate(chips)]
            for cp in first:
                cp.start()
            started += first
        woutb_ref[...] = wout_ref[...].astype(BF16)
        for b in range(2):
            for j, chip in enumerate(chips):
                copy(b, 1 + j, (*chip, c), me).wait_recv()
                fwd = copy(b, 4 + j, (*chip, c), sibling)
                fwd.start()
                started.append(fwd)
        for b in range(2):
            copy(b, 0, sibling, me).wait_recv()
            for j, chip in enumerate(chips):
                copy(b, 4 + j, (*chip, 1 - c), me).wait_recv()
        for d in range(N_DEV):
            conv_ref[:, d * SHARD_CONV:(d + 1) * SHARD_CONV] = gcv_ref[d]
        for cp in started:
            cp.wait_send()

    vmem = pl.BlockSpec(memory_space=pltpu.VMEM)
    return pl.pallas_call(
        body, name="all_gather",
        out_shape=(jax.ShapeDtypeStruct((N_DEV, SHARD_IN, D_MODEL), BF16),
                   jax.ShapeDtypeStruct((SHARD_OUT, D_MODEL), BF16),
                   jax.ShapeDtypeStruct((8, D_CONV), F32)),
        in_specs=[vmem, vmem, vmem], out_specs=(vmem, vmem, vmem),
        scratch_shapes=[pltpu.VMEM((N_DEV, 8, SHARD_CONV), F32),
                        pltpu.SemaphoreType.DMA((14,)), pltpu.SemaphoreType.DMA((14,))],
        compiler_params=_params(),
    )(w_in_sh, w_out_sh, conv_sh)


def _shard_sum(src, own, d2d, ici, send_sems, recv_sems, local_sems, base=0):
    x, y, c = lax.axis_index("x"), lax.axis_index("y"), lax.axis_index("c")
    sibling = (x, y, 1 - c)
    chips = [(x, y), (1 - x, y), (x, 1 - y), (1 - x, 1 - y)]

    def rcopy(s, d, k, to):
        return pltpu.make_async_remote_copy(src_ref=s, dst_ref=d, send_sem=send_sems.at[base + k],
                                            recv_sem=recv_sems.at[base + k], device_id=to, device_id_type=MESH)

    def mine(k):
        return pltpu.make_async_copy(src.at[_slot(*chips[k], c)], own.at[k], local_sems.at[k])

    def to_sibling(k):
        return rcopy(src.at[_slot(*chips[k], 1 - c)], d2d.at[k], k, sibling)

    def to_chip(k):
        return rcopy(own.at[k], ici.at[k - 1], 3 + k, (*chips[k], c))

    def start():
        for k in range(4):
            mine(k).start()
            to_sibling(k).start()

    def forward():
        for k in range(1, 4):
            mine(k).wait()
            to_sibling(k).wait_recv()
            own[k] = (own[k].astype(F32) + d2d[k].astype(F32)).astype(BF16)
            to_chip(k).start()

    def finish():
        mine(0).wait()
        to_sibling(0).wait_recv()
        acc = own[0].astype(F32) + d2d[0].astype(F32)
        for k in range(1, 4):
            to_chip(k).wait_recv()
            acc = acc + ici[k - 1].astype(F32)
        for k in range(4):
            to_sibling(k).wait_send()
        for k in range(1, 4):
            to_chip(k).wait_send()
        return acc

    return start, forward, finish


def _shard_sum_scratch(rows):
    return [pltpu.VMEM((4, rows, D_MODEL), BF16), pltpu.VMEM((4, rows, D_MODEL), BF16),
            pltpu.VMEM((3, rows, D_MODEL), BF16)]


N_SHARD_SUM_SEMS = 7


def _slab_sum(myslab, slabs, send_sems, recv_sems, base):
    x, y, c = lax.axis_index("x"), lax.axis_index("y"), lax.axis_index("c")
    me = _slot(x, y, c)
    peers = [(x, y, 1 - c), (1 - x, y, c), (x, 1 - y, c), (1 - x, 1 - y, c),
             (1 - x, y, 1 - c), (x, 1 - y, 1 - c), (1 - x, 1 - y, 1 - c)]

    def cp(k):
        return pltpu.make_async_remote_copy(src_ref=myslab, dst_ref=slabs.at[me], send_sem=send_sems.at[base + k],
                                            recv_sem=recv_sems.at[base + k], device_id=peers[k], device_id_type=MESH)

    def start():
        slabs[me] = myslab[...]
        for k in range(7):
            cp(k).start()

    def finish():
        for k in range(7):
            cp(k).wait_recv()
        total = slabs[0]
        for d in range(1, N_DEV):
            total = total + slabs[d]
        for k in range(7):
            cp(k).wait_send()
        return total

    return start, finish


def _in_proj(x, norm_in, w_full, w_out_b):
    tm = 256
    steps = SEQ // tm
    forward_step = 5

    def body(x_ref, g_ref, w_ref, wo_ref, h_ref, proj_ref, gout_ref, send_sems, recv_sems, local_sem):
        i = pl.program_id(0)
        x, y, c = lax.axis_index("x"), lax.axis_index("y"), lax.axis_index("c")
        me, sibling = (x, y, c), (x, y, 1 - c)
        chips = [(1 - x, y), (x, 1 - y), (1 - x, 1 - y)]

        def copy(k, block, to, src=None):
            rows = gout_ref.at[_slot(*block)]
            return pltpu.make_async_remote_copy(src_ref=rows if src is None else src, dst_ref=rows,
                                                send_sem=send_sems.at[k], recv_sem=recv_sems.at[k],
                                                device_id=to, device_id_type=MESH)

        mine = pltpu.make_async_copy(wo_ref, gout_ref.at[_slot(*me)], local_sem)

        @pl.when(i == 0)
        def _():
            mine.start()
            copy(0, me, sibling, src=wo_ref).start()
            for j, chip in enumerate(chips):
                copy(1 + j, me, (*chip, c), src=wo_ref).start()

        xv = x_ref[...]
        r = lax.rsqrt(jnp.mean(xv * xv, axis=-1, keepdims=True) + RMS_EPS)
        h = (xv * r * g_ref[...]).astype(BF16)
        h_ref[...] = h
        proj_ref[...] = lax.dot_general(h, w_ref[...], _NT, preferred_element_type=F32)

        @pl.when(i == forward_step)
        def _():
            for j, chip in enumerate(chips):
                copy(1 + j, (*chip, c), me).wait_recv()
                copy(4 + j, (*chip, c), sibling).start()

        @pl.when(i == steps - 1)
        def _():
            copy(0, sibling, me).wait_recv()
            for j, chip in enumerate(chips):
                copy(4 + j, (*chip, 1 - c), me).wait_recv()
            copy(0, me, sibling, src=wo_ref).wait_send()
            for j, chip in enumerate(chips):
                copy(1 + j, me, (*chip, c), src=wo_ref).wait_send()
                copy(4 + j, (*chip, c), sibling).wait_send()
            mine.wait()

    return pl.pallas_call(
        body, name="in_proj", grid=(steps,),
        in_specs=[pl.BlockSpec((tm, D_MODEL), lambda i: (i, 0)), pl.BlockSpec((1, D_MODEL), lambda i: (0, 0)),
                  pl.BlockSpec(memory_space=pltpu.VMEM), pl.BlockSpec(memory_space=pl.ANY)],
        out_specs=(pl.BlockSpec((tm, D_MODEL), lambda i: (i, 0)), pl.BlockSpec((tm, D_PROJ), lambda i: (i, 0)),
                   pl.BlockSpec(memory_space=pl.ANY)),
        out_shape=(jax.ShapeDtypeStruct((SEQ, D_MODEL), BF16), jax.ShapeDtypeStruct((SEQ, D_PROJ), F32),
                   jax.ShapeDtypeStruct((N_DEV, SHARD_OUT, D_MODEL), BF16)),
        scratch_shapes=[pltpu.SemaphoreType.DMA((7,)), pltpu.SemaphoreType.DMA((7,)), pltpu.SemaphoreType.DMA],
        compiler_params=_params(dimension_semantics=("arbitrary",)),
    )(x, norm_in, w_full, w_out_b)


def _conv_fwd(pj, cch_ref, cuh_ref, cw_ref, n):
    cc = pj[:, OFF_CC:OFF_CC + D_CONV]
    cu = pj[:, OFF_CU:OFF_CU + D_CONV]
    z = cc * cu
    zh = jnp.where(n > 0, cch_ref[...] * cuh_ref[...], 0.0)
    row = lax.broadcasted_iota(jnp.int32, (BLOCK, D_CONV), 0)
    z1 = jnp.where(row == 0, zh[HALO - 1:HALO, :], pltpu.roll(z, 1, 0))
    z2 = jnp.where(row == 0, zh[HALO - 2:HALO - 1, :], jnp.where(row == 1, zh[HALO - 1:HALO, :], pltpu.roll(z, 2, 0)))
    co = cw_ref[0:1, :] * z2 + cw_ref[1:2, :] * z1 + cw_ref[2:3, :] * z
    return cc, cu, z, z1, z2, co


def _kv_bands(pj, kvp_ref):
    lane = lax.broadcasted_iota(jnp.int32, (2 * BLOCK, D_KV), 1)
    lo = lane < HEAD_DIM

    def bands(prev, cur):
        b = jnp.concatenate([prev, cur], axis=0)
        br = pltpu.roll(b, HEAD_DIM, 1)
        zero = jnp.zeros_like(b)
        return ((jnp.where(lo, b, zero).astype(BF16), jnp.where(lo, zero, br).astype(BF16)),
                (jnp.where(lo, br, zero).astype(BF16), jnp.where(lo, zero, b).astype(BF16)))

    ks = bands(kvp_ref[:, 0:D_KV], pj[:, OFF_K:OFF_K + D_KV])
    vs = bands(kvp_ref[:, D_KV:2 * D_KV], pj[:, OFF_V:OFF_V + D_KV])
    return ks, vs


def _attn_bias(n):
    qi = lax.broadcasted_iota(jnp.int32, (BLOCK, 2 * BLOCK), 0)
    kj = lax.broadcasted_iota(jnp.int32, (BLOCK, 2 * BLOCK), 1)
    dist = BLOCK + qi - kj
    valid = (dist >= 0) & (dist < BLOCK) & ((kj >= BLOCK) | (n > 0))
    return dist.astype(F32), valid


def _attn_probs(qp, kband, h, distf, valid, sink):
    s = lax.dot_general(qp, kband, _NT, preferred_element_type=F32) * SCALE
    s = jnp.where(valid, s - SLOPES[h] * distf, NEG)
    m = jnp.maximum(jnp.max(s, axis=-1, keepdims=True), sink)
    p = jnp.exp(s - m)
    es = jnp.exp(sink - m)
    inv = 1.0 / (jnp.sum(p, axis=-1, keepdims=True) + es)
    return p * inv, es * inv


def _mix_fwd(proj, conv_full, sinks, norm_conv, norm_attn):
    def body(pj_ref, kvp_ref, cch_ref, cuh_ref, cw_ref, sink_ref, gc_ref, ga_ref, mixed_ref, attn_scr):
        n = pl.program_id(0)
        pj = pj_ref
        _, _, _, _, _, co = _conv_fwd(pj, cch_ref, cuh_ref, cw_ref, n)
        a_c = pj[:, OFF_CB:OFF_CB + D_CONV] * co
        r_c = lax.rsqrt(jnp.mean(a_c * a_c, axis=-1, keepdims=True) + RMS_EPS)
        t_c = pj[:, OFF_GC:OFF_GC + D_CONV]
        mixed_ref[:, 0:D_CONV] = (a_c * r_c * gc_ref[...] * (t_c * _sigmoid(t_c))).astype(BF16)

        ks, vs = _kv_bands(pj, kvp_ref)
        distf, valid = _attn_bias(n)
        for i in range(N_PAIRS):
            j = i // PAIRS_PER_KV
            qp = pj[:, OFF_Q + 128 * i:OFF_Q + 128 * (i + 1)].astype(BF16)
            out = None
            for e in range(2):
                h = 2 * i + e
                p, _ = _attn_probs(qp, ks[j][e], h, distf, valid, sink_ref[h])
                o = jnp.dot(p.astype(BF16), vs[j][e], preferred_element_type=F32)
                out = o if out is None else out + o
            attn_scr[:, 128 * i:128 * (i + 1)] = out
        a_a = attn_scr[...]
        r_a = lax.rsqrt(jnp.mean(a_a * a_a, axis=-1, keepdims=True) + RMS_EPS)
        t_a = pj[:, OFF_GA:OFF_GA + D_ATTN]
        mixed_ref[:, D_CONV:D_MIX] = (a_a * r_a * ga_ref[...] * (t_a * _sigmoid(t_a))).astype(BF16)

    per_block = BLOCK // HALO
    return pl.pallas_call(
        body, name="mix_fwd", grid=(N_BLOCKS,),
        in_specs=[
            pl.BlockSpec((BLOCK, D_PROJ), lambda n: (n, 0)),
            pl.BlockSpec((BLOCK, 2 * D_KV), lambda n: (jnp.maximum(n - 1, 0), OFF_K // (2 * D_KV))),
            pl.BlockSpec((HALO, D_CONV), lambda n: (jnp.maximum(n * per_block - 1, 0), OFF_CC // D_CONV)),
            pl.BlockSpec((HALO, D_CONV), lambda n: (jnp.maximum(n * per_block - 1, 0), OFF_CU // D_CONV)),
            pl.BlockSpec((8, D_CONV), lambda n: (0, 0)),
            pl.BlockSpec(memory_space=pltpu.SMEM),
            pl.BlockSpec((1, D_CONV), lambda n: (0, 0)),
            pl.BlockSpec((1, D_ATTN), lambda n: (0, 0)),
        ],
        out_specs=pl.BlockSpec((BLOCK, D_MIX), lambda n: (n, 0)),
        out_shape=jax.ShapeDtypeStruct((SEQ, D_MIX), BF16),
        scratch_shapes=[pltpu.VMEM((BLOCK, D_ATTN), F32)],
        compiler_params=_params(dimension_semantics=("arbitrary",)),
    )(proj, proj, proj, proj, conv_full, sinks, norm_conv, norm_attn)


def _out_proj_loss(mixed, x, target, w_out_full, norm_final):
    tm = 256

    def body(mx_ref, x_ref, t_ref, w_ref, g_ref, dx2_ref, dx2b_ref, dmix_ref, gnf_ref, loss_ref):
        i = pl.program_id(0)
        w = w_ref[...]
        x2 = x_ref[...] + jnp.dot(mx_ref[...], w, preferred_element_type=F32)
        r = lax.rsqrt(jnp.mean(x2 * x2, axis=-1, keepdims=True) + RMS_EPS)
        xn = x2 * r
        g = g_ref[...]
        err = xn * g - t_ref[...]
        part = 0.5 * jnp.sum(jnp.mean(err * err, axis=-1, keepdims=True), axis=0, keepdims=True)
        dy = err * (1.0 / D_MODEL)
        gnf = jnp.sum(dy * xn, axis=0, keepdims=True)
        u = dy * g
        dx2 = r * (u - xn * jnp.mean(u * xn, axis=-1, keepdims=True))
        dx2_ref[...] = dx2
        dx2b = dx2.astype(BF16)
        dx2b_ref[...] = dx2b
        dmix_ref[...] = lax.dot_general(dx2b, w, _NT, preferred_element_type=F32)

        @pl.when(i == 0)
        def _():
            gnf_ref[...] = jnp.zeros_like(gnf_ref)
            loss_ref[...] = jnp.zeros_like(loss_ref)

        gnf_ref[...] += gnf
        loss_ref[...] += jnp.broadcast_to(part, loss_ref.shape)

    return pl.pallas_call(
        body, name="out_proj_loss", grid=(SEQ // tm,),
        in_specs=[pl.BlockSpec((tm, D_MIX), lambda i: (i, 0)), pl.BlockSpec((tm, D_MODEL), lambda i: (i, 0)),
                  pl.BlockSpec((tm, D_MODEL), lambda i: (i, 0)), pl.BlockSpec(memory_space=pltpu.VMEM),
                  pl.BlockSpec((1, D_MODEL), lambda i: (0, 0))],
        out_specs=(pl.BlockSpec((tm, D_MODEL), lambda i: (i, 0)), pl.BlockSpec((tm, D_MODEL), lambda i: (i, 0)),
                   pl.BlockSpec((tm, D_MIX), lambda i: (i, 0)),
                   pl.BlockSpec((1, D_MODEL), lambda i: (0, 0)), pl.BlockSpec((8, 128), lambda i: (0, 0))),
        out_shape=(jax.ShapeDtypeStruct((SEQ, D_MODEL), F32), jax.ShapeDtypeStruct((SEQ, D_MODEL), BF16),
                   jax.ShapeDtypeStruct((SEQ, D_MIX), F32),
                   jax.ShapeDtypeStruct((1, D_MODEL), F32), jax.ShapeDtypeStruct((8, 128), F32)),
        compiler_params=_params(dimension_semantics=("arbitrary",)),
    )(mixed, x, target, w_out_full, norm_final)


def _gated_norm_bwd(a, gain, t, dy):
    r = lax.rsqrt(jnp.mean(a * a, axis=-1, keepdims=True) + RMS_EPS)
    an = a * r
    sg = _sigmoid(t)
    dn = dy * (t * sg)
    dt = dy * (an * gain) * (sg * (1.0 + t * (1.0 - sg)))
    dgain = jnp.sum(dn * an, axis=0, keepdims=True)
    u = dn * gain
    da = r * (u - an * jnp.mean(u * an, axis=-1, keepdims=True))
    return da, dt, dgain


def _mix_bwd(proj, dmixed, conv_full, sinks, norm_conv, norm_attn):
    def body(pj_ref, kvp_ref, cch_ref, cuh_ref, dmx_ref, cw_ref, sink_ref, gc_ref, ga_ref,
             dpj_ref, gslab_ref, attn_scr, dattn_scr, p_scr, ps_scr, nxt_scr, dkv_scr):
        step = pl.program_id(0)
        n = N_BLOCKS - 1 - step
        pj = pj_ref

        @pl.when(step == 0)
        def _():
            gslab_ref[...] = jnp.zeros_like(gslab_ref)
            nxt_scr[...] = jnp.zeros_like(nxt_scr)
            dkv_scr[...] = jnp.zeros_like(dkv_scr)

        cc, cu, z, z1, z2, co = _conv_fwd(pj, cch_ref, cuh_ref, cw_ref, n)
        cb = pj[:, OFF_CB:OFF_CB + D_CONV]
        da_c, dgate_c, dgain_c = _gated_norm_bwd(cb * co, gc_ref[...], pj[:, OFF_GC:OFF_GC + D_CONV],
                                                 dmx_ref[:, 0:D_CONV])
        dpj_ref[:, OFF_GC:OFF_GC + D_CONV] = dgate_c.astype(BF16)
        dpj_ref[:, OFF_CB:OFF_CB + D_CONV] = (da_c * co).astype(BF16)
        dco = da_c * cb
        nxt = nxt_scr[...]
        row = lax.broadcasted_iota(jnp.int32, (BLOCK, D_CONV), 0)
        dco1 = jnp.where(row == BLOCK - 1, nxt[0:1, :], pltpu.roll(dco, BLOCK - 1, 0))
        dco2 = jnp.where(row == BLOCK - 2, nxt[0:1, :],
                         jnp.where(row == BLOCK - 1, nxt[1:2, :], pltpu.roll(dco, BLOCK - 2, 0)))
        dz = cw_ref[2:3, :] * dco + cw_ref[1:2, :] * dco1 + cw_ref[0:1, :] * dco2
        dpj_ref[:, OFF_CC:OFF_CC + D_CONV] = (dz * cu).astype(BF16)
        dpj_ref[:, OFF_CU:OFF_CU + D_CONV] = (dz * cc).astype(BF16)
        nxt_scr[...] = dco[0:HALO, :]
        gslab_ref[ROW_NORM_CONV:ROW_NORM_CONV + 1, :] += dgain_c
        gslab_ref[ROW_CONV0:ROW_CONV0 + 1, :] += jnp.sum(dco * z2, axis=0, keepdims=True)
        gslab_ref[ROW_CONV0 + 1:ROW_CONV0 + 2, :] += jnp.sum(dco * z1, axis=0, keepdims=True)
        gslab_ref[ROW_CONV0 + 2:ROW_CONV0 + 3, :] += jnp.sum(dco * z, axis=0, keepdims=True)

        ks, vs = _kv_bands(pj, kvp_ref)
        distf, valid = _attn_bias(n)
        for i in range(N_PAIRS):
            j = i // PAIRS_PER_KV
            qp = pj[:, OFF_Q + 128 * i:OFF_Q + 128 * (i + 1)].astype(BF16)
            out = None
            for e in range(2):
                h = 2 * i + e
                p, psink = _attn_probs(qp, ks[j][e], h, distf, valid, sink_ref[h])
                p_scr[h] = p
                ps_scr[h] = psink
                o = jnp.dot(p.astype(BF16), vs[j][e], preferred_element_type=F32)
                out = o if out is None else out + o
            attn_scr[:, 128 * i:128 * (i + 1)] = out
        da_a, dgate_a, dgain_a = _gated_norm_bwd(attn_scr[...], ga_ref[...], pj[:, OFF_GA:OFF_GA + D_ATTN],
                                                 dmx_ref[:, D_CONV:D_MIX])
        dpj_ref[:, OFF_GA:OFF_GA + D_ATTN] = dgate_a.astype(BF16)
        dattn_scr[...] = da_a
        gslab_ref[ROW_NORM_ATTN:ROW_NORM_ATTN + 1, :] += dgain_a

        lane = lax.broadcasted_iota(jnp.int32, (BLOCK, 128), 1)
        lo = lane < HEAD_DIM
        lane_s = lax.broadcasted_iota(jnp.int32, (1, D_MODEL), 1)
        gsink = jnp.zeros((1, D_MODEL), F32)
        dkv_full = []
        for j in range(2):
            dk_f = jnp.zeros((2 * BLOCK, 128), F32)
            dv_f = jnp.zeros((2 * BLOCK, 128), F32)
            for i in range(j * PAIRS_PER_KV, (j + 1) * PAIRS_PER_KV):
                q_f = pj[:, OFF_Q + 128 * i:OFF_Q + 128 * (i + 1)]
                do_f = dattn_scr[:, 128 * i:128 * (i + 1)]
                prod = do_f * attn_scr[:, 128 * i:128 * (i + 1)]
                tot = jnp.sum(prod, axis=-1, keepdims=True)
                d_lo = jnp.sum(jnp.where(lo, prod, 0.0), axis=-1, keepdims=True)
                deltas = (d_lo, tot - d_lo)
                do_b = do_f.astype(BF16)
                dq = None
                for e in range(2):
                    h = 2 * i + e
                    half = lo if e == 0 else jnp.logical_not(lo)
                    p = p_scr[h]
                    dp = lax.dot_general(do_b, vs[j][e], _NT, preferred_element_type=F32)
                    ds = (p * (dp - deltas[e]) * SCALE).astype(BF16)
                    gs = -jnp.sum(ps_scr[h] * deltas[e], axis=0, keepdims=True)
                    gsink = gsink + jnp.where(lane_s == h, gs, 0.0)
                    t = jnp.dot(ds, ks[j][e], preferred_element_type=F32)
                    dq = t if dq is None else dq + t
                    dk_f = dk_f + lax.dot_general(ds, jnp.where(half, q_f, 0.0).astype(BF16), _TN,
                                                  preferred_element_type=F32)
                    dv_f = dv_f + lax.dot_general(p.astype(BF16), jnp.where(half, do_f, 0.0).astype(BF16), _TN,
                                                  preferred_element_type=F32)
                dpj_ref[:, OFF_Q + 128 * i:OFF_Q + 128 * (i + 1)] = dq.astype(BF16)
            dkv_full.append((dk_f + pltpu.roll(dk_f, HEAD_DIM, 1), dv_f + pltpu.roll(dv_f, HEAD_DIM, 1)))
        lane2 = lax.broadcasted_iota(jnp.int32, (2 * BLOCK, 128), 1)
        lo2 = lane2 < HEAD_DIM
        dk = jnp.where(lo2, dkv_full[0][0], dkv_full[1][0])
        dv = jnp.where(lo2, dkv_full[0][1], dkv_full[1][1])
        dpj_ref[:, OFF_K:OFF_K + D_KV] = (dk[BLOCK:, :] + dkv_scr[:, 0:D_KV]).astype(BF16)
        dpj_ref[:, OFF_V:OFF_V + D_KV] = (dv[BLOCK:, :] + dkv_scr[:, D_KV:2 * D_KV]).astype(BF16)
        dkv_scr[:, 0:D_KV] = dk[:BLOCK, :]
        dkv_scr[:, D_KV:2 * D_KV] = dv[:BLOCK, :]
        gslab_ref[ROW_SINKS:ROW_SINKS + 1, :] += gsink

    per_block = BLOCK // HALO
    last = N_BLOCKS - 1
    return pl.pallas_call(
        body, name="mix_bwd", grid=(N_BLOCKS,),
        in_specs=[
            pl.BlockSpec((BLOCK, D_PROJ), lambda s: (last - s, 0)),
            pl.BlockSpec((BLOCK, 2 * D_KV), lambda s: (jnp.maximum(last - s - 1, 0), OFF_K // (2 * D_KV))),
            pl.BlockSpec((HALO, D_CONV), lambda s: (jnp.maximum((last - s) * per_block - 1, 0), OFF_CC // D_CONV)),
            pl.BlockSpec((HALO, D_CONV), lambda s: (jnp.maximum((last - s) * per_block - 1, 0), OFF_CU // D_CONV)),
            pl.BlockSpec((BLOCK, D_MIX), lambda s: (last - s, 0)),
            pl.BlockSpec((8, D_CONV), lambda s: (0, 0)),
            pl.BlockSpec(memory_space=pltpu.SMEM),
            pl.BlockSpec((1, D_CONV), lambda s: (0, 0)),
            pl.BlockSpec((1, D_ATTN), lambda s: (0, 0)),
        ],
        out_specs=(pl.BlockSpec((BLOCK, D_PROJ), lambda s: (last - s, 0)),
                   pl.BlockSpec((8, D_MODEL), lambda s: (0, 0))),
        out_shape=(jax.ShapeDtypeStruct((SEQ, D_PROJ), BF16), jax.ShapeDtypeStruct((8, D_MODEL), F32)),
        scratch_shapes=[pltpu.VMEM((BLOCK, D_ATTN), F32), pltpu.VMEM((BLOCK, D_ATTN), F32),
                        pltpu.VMEM((N_Q_HEADS, BLOCK, 2 * BLOCK), F32), pltpu.VMEM((N_Q_HEADS, BLOCK, 1), F32),
                        pltpu.VMEM((HALO, D_CONV), F32), pltpu.VMEM((BLOCK, 2 * D_KV), F32)],
        compiler_params=_params(dimension_semantics=("arbitrary",)),
    )(proj, proj, proj, proj, dmixed, conv_full, sinks, norm_conv, norm_attn)


def _in_bwd_rs(dproj, w_full, x, dx2, norm_in, dw_in_sh, gslab, gnf, loss_part):
    tm = 256
    steps = SEQ // tm
    forward_step = 2

    def body(dp_ref, w_ref, x_ref, dx2_ref, g_ref, dwi_ref, gs_ref, gnf_ref, lp_ref, gx_ref, gwin_ref, gsum_ref,
             gni_scr, own, d2d, ici, myslab, slabs, send_sems, recv_sems, local_sems):
        i = pl.program_id(0)
        rs_start, rs_forward, rs_finish = _shard_sum(dwi_ref, own, d2d, ici, send_sems, recv_sems, local_sems)
        slab_start, slab_finish = _slab_sum(myslab, slabs, send_sems, recv_sems, N_SHARD_SUM_SEMS)

        @pl.when(i == 0)
        def _():
            gni_scr[...] = jnp.zeros_like(gni_scr)
            rs_start()

        dh = jnp.dot(dp_ref[...], w_ref[...], preferred_element_type=F32)
        xv = x_ref[...]
        r = lax.rsqrt(jnp.mean(xv * xv, axis=-1, keepdims=True) + RMS_EPS)
        xn = xv * r
        u = dh * g_ref[...]
        gx_ref[...] = dx2_ref[...] + r * (u - xn * jnp.mean(u * xn, axis=-1, keepdims=True))
        gni_scr[...] += jnp.sum(dh * xn, axis=0, keepdims=True)

        @pl.when(i == forward_step)
        def _():
            rs_forward()

        @pl.when(i == steps - 1)
        def _():
            row = lax.broadcasted_iota(jnp.int32, (8, D_MODEL), 0)
            lane = lax.broadcasted_iota(jnp.int32, (8, D_MODEL), 1)
            slab = jnp.where(row == ROW_NORM_IN, gni_scr[...], jnp.where(row == ROW_NORM_FINAL, gnf_ref[...], gs_ref[...]))
            myslab[...] = jnp.where((row == ROW_SINKS) & (lane == LOSS_LANE), lp_ref[0:1, 0:1], slab)
            slab_start()
            gwin_ref[...] = rs_finish()
            gsum_ref[...] = slab_finish()

    const = lambda i: (0, 0)
    return pl.pallas_call(
        body, name="in_bwd", grid=(steps,),
        in_specs=[pl.BlockSpec((tm, D_PROJ), lambda i: (i, 0)), pl.BlockSpec(memory_space=pltpu.VMEM),
                  pl.BlockSpec((tm, D_MODEL), lambda i: (i, 0)), pl.BlockSpec((tm, D_MODEL), lambda i: (i, 0)),
                  pl.BlockSpec((1, D_MODEL), const), pl.BlockSpec(memory_space=pl.ANY),
                  pl.BlockSpec((8, D_MODEL), const), pl.BlockSpec((1, D_MODEL), const), pl.BlockSpec((8, 128), const)],
        out_specs=(pl.BlockSpec((tm, D_MODEL), lambda i: (i, 0)), pl.BlockSpec((SHARD_IN, D_MODEL), const),
                   pl.BlockSpec((8, D_MODEL), const)),
        out_shape=(jax.ShapeDtypeStruct((SEQ, D_MODEL), F32), jax.ShapeDtypeStruct((SHARD_IN, D_MODEL), F32),
                   jax.ShapeDtypeStruct((8, D_MODEL), F32)),
        scratch_shapes=[pltpu.VMEM((1, D_MODEL), F32), *_shard_sum_scratch(SHARD_IN),
                        pltpu.VMEM((8, D_MODEL), F32), pltpu.VMEM((N_DEV, 8, D_MODEL), F32),
                        pltpu.SemaphoreType.DMA((N_SHARD_SUM_SEMS + 7,)), pltpu.SemaphoreType.DMA((N_SHARD_SUM_SEMS + 7,)),
                        pltpu.SemaphoreType.DMA((4,))],
        compiler_params=_params(dimension_semantics=("arbitrary",)),
    )(dproj, w_full, x, dx2, norm_in, dw_in_sh, gslab, gnf, loss_part)


def _dw_in_rs(dproj, h, dw_out_sh):
    tn = 640
    steps = D_PROJ // tn
    forward_step = 2

    def body(a_ref, b_ref, dwo_ref, o_ref, gwo_ref, own, d2d, ici, send_sems, recv_sems, local_sems):
        i = pl.program_id(0)
        rs_start, rs_forward, rs_finish = _shard_sum(dwo_ref, own, d2d, ici, send_sems, recv_sems, local_sems)

        @pl.when(i == 0)
        def _():
            rs_start()

        o_ref[...] = lax.dot_general(a_ref[...], b_ref[...], _TN, preferred_element_type=F32).astype(BF16)

        @pl.when(i == forward_step)
        def _():
            rs_forward()

        @pl.when(i == steps - 1)
        def _():
            gwo_ref[...] = rs_finish()

    return pl.pallas_call(
        body, name="dw_in", grid=(steps,),
        in_specs=[pl.BlockSpec((SEQ, tn), lambda i: (0, i)), pl.BlockSpec(memory_space=pltpu.VMEM),
                  pl.BlockSpec(memory_space=pl.ANY)],
        out_specs=(pl.BlockSpec((tn, D_MODEL), lambda i: (i, 0)), pl.BlockSpec((SHARD_OUT, D_MODEL), lambda i: (0, 0))),
        out_shape=(jax.ShapeDtypeStruct((D_PROJ, D_MODEL), BF16), jax.ShapeDtypeStruct((SHARD_OUT, D_MODEL), F32)),
        scratch_shapes=[*_shard_sum_scratch(SHARD_OUT),
                        pltpu.SemaphoreType.DMA((N_SHARD_SUM_SEMS,)), pltpu.SemaphoreType.DMA((N_SHARD_SUM_SEMS,)),
                        pltpu.SemaphoreType.DMA((4,))],
        compiler_params=_params(dimension_semantics=("arbitrary",)),
    )(dproj, h, dw_out_sh)


def _matmul_tn(a, b, tn, name):
    k, n = a.shape
    _, m = b.shape

    def body(a_ref, b_ref, o_ref):
        o_ref[...] = lax.dot_general(a_ref[...], b_ref[...], _TN, preferred_element_type=F32).astype(BF16)

    return pl.pallas_call(
        body, name=name, grid=(n // tn,),
        in_specs=[pl.BlockSpec((k, tn), lambda i: (0, i)), pl.BlockSpec(memory_space=pltpu.VMEM)],
        out_specs=pl.BlockSpec((tn, m), lambda i: (i, 0)),
        out_shape=jax.ShapeDtypeStruct((n, m), BF16),
        compiler_params=_params(dimension_semantics=("arbitrary",)),
    )(a, b)


def _adam_all(big_in, big_out, gsum, small):
    steps = 4
    tr_in, tr_out = SHARD_IN // steps, SHARD_OUT // steps

    def body(*refs):
        ins, outs = refs[:8 + 1 + 18], refs[8 + 1 + 18:]
        i = pl.program_id(0)
        for b in range(2):
            w_ref, g_ref, m_ref, v_ref = ins[4 * b:4 * b + 4]
            g = g_ref[...]
            delta, mn, vn = _adamw(w_ref[...], g, m_ref[...], v_ref[...])
            for ref, val in zip(outs[4 * b:4 * b + 4], (g, delta, mn, vn)):
                ref[...] = val

        @pl.when(i == 0)
        def _():
            gsum = ins[8][...]
            idx = _slot(lax.axis_index("x"), lax.axis_index("y"), lax.axis_index("c"))
            cg = jnp.zeros((3, SHARD_CONV), F32)
            for d in range(N_DEV):
                cg = jnp.where(idx == d, gsum[ROW_CONV0:ROW_CONV0 + 3, d * SHARD_CONV:(d + 1) * SHARD_CONV], cg)
            grads = (gsum[ROW_NORM_IN:ROW_NORM_IN + 1], gsum[ROW_SINKS:ROW_SINKS + 1, 0:N_Q_HEADS],
                     gsum[ROW_NORM_CONV:ROW_NORM_CONV + 1], gsum[ROW_NORM_ATTN:ROW_NORM_ATTN + 1],
                     gsum[ROW_NORM_FINAL:ROW_NORM_FINAL + 1], cg)
            for s, g in enumerate(grads):
                w_ref, m_ref, v_ref = ins[9 + 3 * s:12 + 3 * s]
                delta, mn, vn = _adamw(w_ref[...], g, m_ref[...], v_ref[...])
                for ref, val in zip(outs[8 + 4 * s:12 + 4 * s], (g, delta, mn, vn)):
                    ref[...] = val
            outs[32][...] = gsum[ROW_SINKS:ROW_SINKS + 1, LOSS_LANE:LOSS_LANE + 1]

    const = lambda i: (0, 0)
    rows = lambda i: (i, 0)
    small_shapes = [a.shape for a in small[::3]]
    in_specs = ([pl.BlockSpec((tr_in, D_MODEL), rows)] * 4 + [pl.BlockSpec((tr_out, D_MODEL), rows)] * 4
                + [pl.BlockSpec((8, D_MODEL), const)] + [pl.BlockSpec(a.shape, const) for a in small])
    out_specs = ([pl.BlockSpec((tr_in, D_MODEL), rows)] * 4 + [pl.BlockSpec((tr_out, D_MODEL), rows)] * 4
                 + [pl.BlockSpec(s, const) for s in small_shapes for _ in range(4)] + [pl.BlockSpec((1, 1), const)])
    out_shape = ([jax.ShapeDtypeStruct((SHARD_IN, D_MODEL), F32)] * 4 + [jax.ShapeDtypeStruct((SHARD_OUT, D_MODEL), F32)] * 4
                 + [jax.ShapeDtypeStruct(s, F32) for s in small_shapes for _ in range(4)]
                 + [jax.ShapeDtypeStruct((1, 1), F32)])
    outs = pl.pallas_call(
        body, name="adam", grid=(steps,), in_specs=in_specs, out_specs=tuple(out_specs), out_shape=tuple(out_shape),
        compiler_params=_params(dimension_semantics=("arbitrary",)),
    )(*big_in, *big_out, gsum, *small)
    return outs[0:4], outs[4:8], [outs[8 + 4 * s:12 + 4 * s] for s in range(6)], outs[32]


def _pad_rows(a, rows=8):
    return jnp.pad(a, ((0, rows - a.shape[0]), (0, 0)))


def kernel(x, norm_in, w_in, conv_w, attn_sinks, norm_conv_out, norm_attn_out, w_out, norm_final, loss_target, m_norm_in, m_w_in, m_conv_w, m_attn_sinks, m_norm_conv_out, m_norm_attn_out, m_w_out, m_norm_final, v_norm_in, v_w_in, v_conv_w, v_attn_sinks, v_norm_conv_out, v_norm_attn_out, v_w_out, v_norm_final):
    x2d = x.reshape(SEQ, D_MODEL)
    target = loss_target.reshape(SEQ, D_MODEL)
    nf = norm_final.reshape(1, D_MODEL)

    w_in_t, m_w_in_t, v_w_in_t = w_in[0].T, m_w_in[0].T, v_w_in[0].T
    g_in, w_out_b, conv_full = _all_gather(w_in_t, w_out[0], _pad_rows(conv_w[0]))
    w_in_full = g_in.reshape(D_PROJ, D_MODEL)
    sinks = attn_sinks.reshape(N_Q_HEADS)

    h, proj, g_out = _in_proj(x2d, norm_in, w_in_full, w_out_b)
    w_out_full = g_out.reshape(D_MIX, D_MODEL)
    mixed = _mix_fwd(proj, conv_full, sinks, norm_conv_out, norm_attn_out)
    dx2, dx2b, dmixed, gnf, loss_part = _out_proj_loss(mixed, x2d, target, w_out_full, nf)
    dproj, gslab = _mix_bwd(proj, dmixed, conv_full, sinks, norm_conv_out, norm_attn_out)
    dw_out = _matmul_tn(mixed, dx2b, 512, "dw_out")
    dw_in, g_w_out = _dw_in_rs(dproj, h, dw_out.reshape(N_DEV, SHARD_OUT, D_MODEL))
    grad_x, g_w_in, gsum = _in_bwd_rs(dproj, w_in_full, x2d, dx2, norm_in, dw_in.reshape(N_DEV, SHARD_IN, D_MODEL),
                                      gslab, gnf, loss_part)

    small = (norm_in, m_norm_in, v_norm_in, attn_sinks, m_attn_sinks, v_attn_sinks,
             norm_conv_out, m_norm_conv_out, v_norm_conv_out, norm_attn_out, m_norm_attn_out, v_norm_attn_out,
             nf, m_norm_final.reshape(1, D_MODEL), v_norm_final.reshape(1, D_MODEL),
             conv_w[0], m_conv_w[0], v_conv_w[0])
    big_in, big_out, (s_ni, s_sk, s_nc, s_na, s_nf, s_cv), loss = _adam_all(
        (w_in_t, g_w_in, m_w_in_t, v_w_in_t), (w_out[0], g_w_out, m_w_out[0], v_w_out[0]), gsum, small)

    def leaves(k):
        return (s_ni[k], big_in[k].T[None], s_cv[k][None], s_sk[k], s_nc[k], s_na[k], big_out[k][None],
                s_nf[k].reshape(D_MODEL))

    return (loss.reshape(()), grad_x.reshape(1, SEQ, D_MODEL), *leaves(0), *leaves(1), *leaves(2), *leaves(3))
```

```python
import functools
import math

import jax
import jax.numpy as jnp
from jax import lax
from jax.experimental import pallas as pl
from jax.experimental.pallas import tpu as pltpu

F32 = jnp.float32
BF16 = jnp.bfloat16
MESH = pl.DeviceIdType.MESH

N_DEV = 8
SEQ = 2048
D_MODEL = 1024
D_CONV = 1024
D_ATTN = 1024
D_KV = 128
HEAD_DIM = 64
N_Q_HEADS = 16
N_PAIRS = N_Q_HEADS // 2
PAIRS_PER_KV = N_PAIRS // 2
D_MIX = D_CONV + D_ATTN
D_PROJ = 6400
SHARD_IN = D_PROJ // N_DEV
SHARD_OUT = D_MIX // N_DEV
SHARD_CONV = D_CONV // N_DEV
OFF_CB, OFF_CC, OFF_CU, OFF_GC, OFF_Q, OFF_K, OFF_V, OFF_GA = 0, 1024, 2048, 3072, 4096, 5120, 5248, 5376
BLOCK = 128
N_BLOCKS = SEQ // BLOCK
HALO = 8
CHUNK = 16
N_CHUNKS = BLOCK // CHUNK
RMS_EPS = 1e-5
NEG = -1e30
SCALE = HEAD_DIM ** -0.5
SLOPES = tuple(2.0 ** (-8.0 * (h + 1) / N_Q_HEADS) for h in range(N_Q_HEADS))

ADAM_LR = 0.001
ADAM_B1 = 0.9
ADAM_B2 = 0.999
ADAM_EPS = 1e-08
ADAM_WD = 0.01
ADAM_STEP = 10

ROW_NORM_IN, ROW_NORM_CONV, ROW_NORM_ATTN, ROW_NORM_FINAL, ROW_CONV0, ROW_SINKS = 0, 1, 2, 3, 4, 7
LOSS_LANE = N_Q_HEADS
ACC_NORM_CONV, ACC_NORM_ATTN, ACC_CONV0, N_ACC = 0, 1, 2, 5

VMEM_LIMIT = 56 * 1024 * 1024

_NT = (((1,), (1,)), ((), ()))
_TN = (((0,), (0,)), ((), ()))


def _params(**kw):
    return pltpu.CompilerParams(vmem_limit_bytes=VMEM_LIMIT, **kw)


def _adamw(w, g, m, v):
    m = ADAM_B1 * m + (1.0 - ADAM_B1) * g
    v = ADAM_B2 * v + (1.0 - ADAM_B2) * (g * g)
    m_hat = m / (1.0 - ADAM_B1 ** ADAM_STEP)
    v_hat = v / (1.0 - ADAM_B2 ** ADAM_STEP)
    delta = -ADAM_LR * (m_hat / (jnp.sqrt(v_hat) + ADAM_EPS) + ADAM_WD * w)
    return delta, m, v


def _sigmoid(t):
    return 1.0 / (1.0 + jnp.exp(-t))


def _slot(px, py, pc):
    return 4 * px + 2 * py + pc


def _all_gather(w_in_sh, w_out_sh, conv_sh):
    def body(win_ref, wout_ref, cv_ref, gin_ref, woutb_ref, conv_ref, gcv_ref, send_sems, recv_sems):
        x, y, c = lax.axis_index("x"), lax.axis_index("y"), lax.axis_index("c")
        me, sibling = (x, y, c), (x, y, 1 - c)
        chips = [(1 - x, y), (x, 1 - y), (1 - x, 1 - y)]
        bufs = (gin_ref, gcv_ref)

        gin_ref[_slot(*me)] = win_ref[...].astype(BF16)
        gcv_ref[_slot(*me)] = cv_ref[...]

        def copy(b, k, block, to):
            s = _slot(*block)
            return pltpu.make_async_remote_copy(
                src_ref=bufs[b].at[s], dst_ref=bufs[b].at[s],
                send_sem=send_sems.at[7 * b + k], recv_sem=recv_sems.at[7 * b + k],
                device_id=to, device_id_type=MESH)

        started = []
        for b in range(2):
            first = [copy(b, 0, me, sibling)]
            first += [copy(b, 1 + j, me, (*chip, c)) for j, chip in enumerate(chips)]
            for cp in first:
                cp.start()
            started += first
        woutb_ref[...] = wout_ref[...].astype(BF16)
        for b in range(2):
            for j, chip in enumerate(chips):
                copy(b, 1 + j, (*chip, c), me).wait_recv()
                fwd = copy(b, 4 + j, (*chip, c), sibling)
                fwd.start()
                started.append(fwd)
        for b in range(2):
            copy(b, 0, sibling, me).wait_recv()
            for j, chip in enumerate(chips):
                copy(b, 4 + j, (*chip, 1 - c), me).wait_recv()
        for d in range(N_DEV):
            conv_ref[:, d * SHARD_CONV:(d + 1) * SHARD_CONV] = gcv_ref[d]
        for cp in started:
            cp.wait_send()

    vmem = pl.BlockSpec(memory_space=pltpu.VMEM)
    return pl.pallas_call(
        body, name="all_gather",
        out_shape=(jax.ShapeDtypeStruct((N_DEV, SHARD_IN, D_MODEL), BF16),
                   jax.ShapeDtypeStruct((SHARD_OUT, D_MODEL), BF16),
                   jax.ShapeDtypeStruct((8, D_CONV), F32)),
        in_specs=[vmem, vmem, vmem], out_specs=(vmem, vmem, vmem),
        scratch_shapes=[pltpu.VMEM((N_DEV, 8, SHARD_CONV), F32),
                        pltpu.SemaphoreType.DMA((14,)), pltpu.SemaphoreType.DMA((14,))],
        compiler_params=_params(),
    )(w_in_sh, w_out_sh, conv_sh)


def _shard_sum(src, own, d2d, ici, send_sems, recv_sems, local_sems, base=0):
    x, y, c = lax.axis_index("x"), lax.axis_index("y"), lax.axis_index("c")
    sibling = (x, y, 1 - c)
    chips = [(x, y), (1 - x, y), (x, 1 - y), (1 - x, 1 - y)]

    def rcopy(s, d, k, to):
        return pltpu.make_async_remote_copy(src_ref=s, dst_ref=d, send_sem=send_sems.at[base + k],
                                            recv_sem=recv_sems.at[base + k], device_id=to, device_id_type=MESH)

    def mine(k):
        return pltpu.make_async_copy(src.at[_slot(*chips[k], c)], own.at[k], local_sems.at[k])

    def to_sibling(k):
        return rcopy(src.at[_slot(*chips[k], 1 - c)], d2d.at[k], k, sibling)

    def to_chip(k):
        return rcopy(own.at[k], ici.at[k - 1], 3 + k, (*chips[k], c))

    def start():
        for k in range(4):
            mine(k).start()
            to_sibling(k).start()

    def forward():
        for k in range(1, 4):
            mine(k).wait()
            to_sibling(k).wait_recv()
            own[k] = (own[k].astype(F32) + d2d[k].astype(F32)).astype(BF16)
            to_chip(k).start()

    def finish():
        mine(0).wait()
        to_sibling(0).wait_recv()
        acc = own[0].astype(F32) + d2d[0].astype(F32)
        for k in range(1, 4):
            to_chip(k).wait_recv()
            acc = acc + ici[k - 1].astype(F32)
        for k in range(4):
            to_sibling(k).wait_send()
        for k in range(1, 4):
            to_chip(k).wait_send()
        return acc

    return start, forward, finish


def _shard_sum_scratch(rows):
    return [pltpu.VMEM((4, rows, D_MODEL), BF16), pltpu.VMEM((4, rows, D_MODEL), BF16),
            pltpu.VMEM((3, rows, D_MODEL), BF16)]


N_SHARD_SUM_SEMS = 7


def _slab_sum(myslab, slabs, send_sems, recv_sems, base):
    x, y, c = lax.axis_index("x"), lax.axis_index("y"), lax.axis_index("c")
    me = _slot(x, y, c)
    peers = [(x, y, 1 - c), (1 - x, y, c), (x, 1 - y, c), (1 - x, 1 - y, c),
             (1 - x, y, 1 - c), (x, 1 - y, 1 - c), (1 - x, 1 - y, 1 - c)]

    def cp(k):
        return pltpu.make_async_remote_copy(src_ref=myslab, dst_ref=slabs.at[me], send_sem=send_sems.at[base + k],
                                            recv_sem=recv_sems.at[base + k], device_id=peers[k], device_id_type=MESH)

    def start():
        slabs[me] = myslab[...]
        for k in range(7):
            cp(k).start()

    def finish():
        for k in range(7):
            cp(k).wait_recv()
        total = slabs[0]
        for d in range(1, N_DEV):
            total = total + slabs[d]
        for k in range(7):
            cp(k).wait_send()
        return total

    return start, finish


def _in_proj(x, norm_in, w_full, w_out_b):
    tm = 256
    steps = SEQ // tm
    forward_step = 5

    def body(x_ref, g_ref, w_ref, wo_ref, h_ref, proj_ref, gout_ref, send_sems, recv_sems, local_sem):
        i = pl.program_id(0)
        x, y, c = lax.axis_index("x"), lax.axis_index("y"), lax.axis_index("c")
        me, sibling = (x, y, c), (x, y, 1 - c)
        chips = [(1 - x, y), (x, 1 - y), (1 - x, 1 - y)]

        def copy(k, block, to, src=None):
            rows = gout_ref.at[_slot(*block)]
            return pltpu.make_async_remote_copy(src_ref=rows if src is None else src, dst_ref=rows,
                                                send_sem=send_sems.at[k], recv_sem=recv_sems.at[k],
                                                device_id=to, device_id_type=MESH)

        mine = pltpu.make_async_copy(wo_ref, gout_ref.at[_slot(*me)], local_sem)

        @pl.when(i == 0)
        def _():
            mine.start()
            copy(0, me, sibling, src=wo_ref).start()
            for j, chip in enumerate(chips):
                copy(1 + j, me, (*chip, c), src=wo_ref).start()

        xv = x_ref[...]
        r = lax.rsqrt(jnp.mean(xv * xv, axis=-1, keepdims=True) + RMS_EPS)
        h = (xv * r * g_ref[...]).astype(BF16)
        h_ref[...] = h
        proj_ref[...] = lax.dot_general(h, w_ref[...], _NT, preferred_element_type=F32)

        @pl.when(i == forward_step)
        def _():
            for j, chip in enumerate(chips):
                copy(1 + j, (*chip, c), me).wait_recv()
                copy(4 + j, (*chip, c), sibling).start()

        @pl.when(i == steps - 1)
        def _():
            copy(0, sibling, me).wait_recv()
            for j, chip in enumerate(chips):
                copy(4 + j, (*chip, 1 - c), me).wait_recv()
            copy(0, me, sibling, src=wo_ref).wait_send()
            for j, chip in enumerate(chips):
                copy(1 + j, me, (*chip, c), src=wo_ref).wait_send()
                copy(4 + j, (*chip, c), sibling).wait_send()
            mine.wait()

    return pl.pallas_call(
        body, name="in_proj", grid=(steps,),
        in_specs=[pl.BlockSpec((tm, D_MODEL), lambda i: (i, 0)), pl.BlockSpec((1, D_MODEL), lambda i: (0, 0)),
                  pl.BlockSpec(memory_space=pltpu.VMEM), pl.BlockSpec(memory_space=pl.ANY)],
        out_specs=(pl.BlockSpec((tm, D_MODEL), lambda i: (i, 0)), pl.BlockSpec((tm, D_PROJ), lambda i: (i, 0)),
                   pl.BlockSpec(memory_space=pl.ANY)),
        out_shape=(jax.ShapeDtypeStruct((SEQ, D_MODEL), BF16), jax.ShapeDtypeStruct((SEQ, D_PROJ), F32),
                   jax.ShapeDtypeStruct((N_DEV, SHARD_OUT, D_MODEL), BF16)),
        scratch_shapes=[pltpu.SemaphoreType.DMA((7,)), pltpu.SemaphoreType.DMA((7,)), pltpu.SemaphoreType.DMA],
        compiler_params=_params(dimension_semantics=("arbitrary",)),
    )(x, norm_in, w_full, w_out_b)


def _chunk_rows(r):
    return pl.ds(pl.multiple_of(r * CHUNK, CHUNK), CHUNK)


def _conv_halo(cch_ref, cuh_ref, n):
    zh = jnp.where(n > 0, cch_ref[...] * cuh_ref[...], 0.0)
    return jnp.concatenate([zh] * (CHUNK // HALO), axis=0)


def _conv_chunk(pj_ref, zhalo, cw, r):
    rows = _chunk_rows(r)
    cc = pj_ref[rows, OFF_CC:OFF_CC + D_CONV]
    cu = pj_ref[rows, OFF_CU:OFF_CU + D_CONV]
    z = cc * cu
    before = _chunk_rows(jnp.maximum(r - 1, 0))
    zprev = jnp.where(r > 0, pj_ref[before, OFF_CC:OFF_CC + D_CONV] * pj_ref[before, OFF_CU:OFF_CU + D_CONV], zhalo)
    row = lax.broadcasted_iota(jnp.int32, (CHUNK, D_CONV), 0)
    z1 = jnp.where(row < 1, pltpu.roll(zprev, 1, 0), pltpu.roll(z, 1, 0))
    z2 = jnp.where(row < 2, pltpu.roll(zprev, 2, 0), pltpu.roll(z, 2, 0))
    co = cw[0] * z2 + cw[1] * z1 + cw[2] * z
    return cc, cu, z, z1, z2, co


def _gated_norm(a, gain, t):
    r = lax.rsqrt(jnp.mean(a * a, axis=-1, keepdims=True) + RMS_EPS)
    return a * r * gain * (t * _sigmoid(t))


def _kv_bands(pj, kvp_ref):
    lane = lax.broadcasted_iota(jnp.int32, (2 * BLOCK, D_KV), 1)
    lo = lane < HEAD_DIM

    def bands(prev, cur):
        b = jnp.concatenate([prev, cur], axis=0)
        br = pltpu.roll(b, HEAD_DIM, 1)
        zero = jnp.zeros_like(b)
        return ((jnp.where(lo, b, zero).astype(BF16), jnp.where(lo, zero, br).astype(BF16)),
                (jnp.where(lo, br, zero).astype(BF16), jnp.where(lo, zero, b).astype(BF16)))

    ks = bands(kvp_ref[:, 0:D_KV], pj[:, OFF_K:OFF_K + D_KV])
    vs = bands(kvp_ref[:, D_KV:2 * D_KV], pj[:, OFF_V:OFF_V + D_KV])
    return ks, vs


STACK = PAIRS_PER_KV * BLOCK


def _head(j, i, e):
    return 2 * (PAIRS_PER_KV * j + i) + e


def _pair_cols(j, i, off):
    p = PAIRS_PER_KV * j + i
    return slice(off + 128 * p, off + 128 * (p + 1))


def _fill_attn_bias(bias_scr, first_block):
    qi = lax.broadcasted_iota(jnp.int32, (BLOCK, 2 * BLOCK), 0)
    kj = lax.broadcasted_iota(jnp.int32, (BLOCK, 2 * BLOCK), 1)
    dist = BLOCK + qi - kj
    valid = (dist >= 0) & (dist < BLOCK)
    if first_block:
        valid = valid & (kj >= BLOCK)
    distf = dist.astype(F32)
    for j in range(2):
        for e in range(2):
            for i in range(PAIRS_PER_KV):
                bias_scr[2 * j + e, BLOCK * i:BLOCK * (i + 1), :] = jnp.where(valid, -SLOPES[_head(j, i, e)] * distf, NEG)


def _q_stack(pj, j):
    return jnp.concatenate([(pj[:, _pair_cols(j, i, OFF_Q)] * SCALE).astype(BF16) for i in range(PAIRS_PER_KV)], axis=0)


def _sink_col(sink_ref, j, e):
    return jnp.concatenate([jnp.full((BLOCK, 1), sink_ref[_head(j, i, e)], F32) for i in range(PAIRS_PER_KV)], axis=0)


def _attn_probs(q_stack, kband, bias, sink):
    s = lax.dot_general(q_stack, kband, _NT, preferred_element_type=F32) + bias
    m = jnp.maximum(jnp.max(s, axis=-1, keepdims=True), sink)
    p = jnp.exp(s - m)
    es = jnp.exp(sink - m)
    inv = 1.0 / (jnp.sum(p, axis=-1, keepdims=True) + es)
    return p * inv, es * inv


def _attn_group(pj, ks, vs, bias_scr, sink_ref, j):
    q_stack = _q_stack(pj, j)
    out, probs, shares = None, [], []
    for e in range(2):
        p, ps = _attn_probs(q_stack, ks[j][e], bias_scr[2 * j + e], _sink_col(sink_ref, j, e))
        o = jnp.dot(p.astype(BF16), vs[j][e], preferred_element_type=F32)
        out = o if out is None else out + o
        probs.append(p)
        shares.append(ps)
    return out, probs, shares


def _mix_fwd(proj, conv_full, sinks, norm_conv, norm_attn):
    def body(pj_ref, kvp_ref, cch_ref, cuh_ref, cw_ref, sink_ref, gc_ref, ga_ref, mixed_ref, attn_scr, bias_scr):
        n = pl.program_id(0)
        pj = pj_ref

        @pl.when(n == 0)
        def _():
            _fill_attn_bias(bias_scr, first_block=True)

        @pl.when(n == 1)
        def _():
            _fill_attn_bias(bias_scr, first_block=False)

        zhalo = _conv_halo(cch_ref, cuh_ref, n)
        cw = (cw_ref[0:1, :], cw_ref[1:2, :], cw_ref[2:3, :])
        gain_c = gc_ref[...]

        def conv_chunk(r, carry):
            rows = _chunk_rows(r)
            co = _conv_chunk(pj_ref, zhalo, cw, r)[-1]
            y = _gated_norm(pj_ref[rows, OFF_CB:OFF_CB + D_CONV] * co, gain_c, pj_ref[rows, OFF_GC:OFF_GC + D_CONV])
            mixed_ref[rows, 0:D_CONV] = y.astype(BF16)
            return carry

        lax.fori_loop(0, N_CHUNKS, conv_chunk, 0, unroll=True)

        ks, vs = _kv_bands(pj, kvp_ref)
        for j in range(2):
            out, _, _ = _attn_group(pj, ks, vs, bias_scr, sink_ref, j)
            for i in range(PAIRS_PER_KV):
                attn_scr[:, _pair_cols(j, i, 0)] = out[BLOCK * i:BLOCK * (i + 1), :]
        gain_a = ga_ref[...]

        def norm_chunk(r, carry):
            rows = _chunk_rows(r)
            y = _gated_norm(attn_scr[rows, :], gain_a, pj_ref[rows, OFF_GA:OFF_GA + D_ATTN])
            mixed_ref[rows, D_CONV:D_MIX] = y.astype(BF16)
            return carry

        lax.fori_loop(0, N_CHUNKS, norm_chunk, 0, unroll=True)

    per_block = BLOCK // HALO
    return pl.pallas_call(
        body, name="mix_fwd", grid=(N_BLOCKS,),
        in_specs=[
            pl.BlockSpec((BLOCK, D_PROJ), lambda n: (n, 0)),
            pl.BlockSpec((BLOCK, 2 * D_KV), lambda n: (jnp.maximum(n - 1, 0), OFF_K // (2 * D_KV))),
            pl.BlockSpec((HALO, D_CONV), lambda n: (jnp.maximum(n * per_block - 1, 0), OFF_CC // D_CONV)),
            pl.BlockSpec((HALO, D_CONV), lambda n: (jnp.maximum(n * per_block - 1, 0), OFF_CU // D_CONV)),
            pl.BlockSpec((8, D_CONV), lambda n: (0, 0)),
            pl.BlockSpec(memory_space=pltpu.SMEM),
            pl.BlockSpec((1, D_CONV), lambda n: (0, 0)),
            pl.BlockSpec((1, D_ATTN), lambda n: (0, 0)),
        ],
        out_specs=pl.BlockSpec((BLOCK, D_MIX), lambda n: (n, 0)),
        out_shape=jax.ShapeDtypeStruct((SEQ, D_MIX), BF16),
        scratch_shapes=[pltpu.VMEM((BLOCK, D_ATTN), F32), pltpu.VMEM((4, STACK, 2 * BLOCK), F32)],
        compiler_params=_params(dimension_semantics=("arbitrary",)),
    )(proj, proj, proj, proj, conv_full, sinks, norm_conv, norm_attn)


def _out_proj_loss(mixed, x, target, w_out_full, norm_final):
    tm = 256

    def body(mx_ref, x_ref, t_ref, w_ref, g_ref, dx2_ref, dx2b_ref, dmix_ref, gnf_ref, loss_ref):
        i = pl.program_id(0)
        w = w_ref[...]
        x2 = x_ref[...] + jnp.dot(mx_ref[...], w, preferred_element_type=F32)
        r = lax.rsqrt(jnp.mean(x2 * x2, axis=-1, keepdims=True) + RMS_EPS)
        xn = x2 * r
        g = g_ref[...]
        err = xn * g - t_ref[...]
        part = 0.5 * jnp.sum(jnp.mean(err * err, axis=-1, keepdims=True), axis=0, keepdims=True)
        dy = err * (1.0 / D_MODEL)
        gnf = jnp.sum(dy * xn, axis=0, keepdims=True)
        u = dy * g
        dx2 = r * (u - xn * jnp.mean(u * xn, axis=-1, keepdims=True))
        dx2_ref[...] = dx2
        dx2b = dx2.astype(BF16)
        dx2b_ref[...] = dx2b
        dmix_ref[...] = lax.dot_general(dx2b, w, _NT, preferred_element_type=F32)

        @pl.when(i == 0)
        def _():
            gnf_ref[...] = jnp.zeros_like(gnf_ref)
            loss_ref[...] = jnp.zeros_like(loss_ref)

        gnf_ref[...] += gnf
        loss_ref[...] += jnp.broadcast_to(part, loss_ref.shape)

    return pl.pallas_call(
        body, name="out_proj_loss", grid=(SEQ // tm,),
        in_specs=[pl.BlockSpec((tm, D_MIX), lambda i: (i, 0)), pl.BlockSpec((tm, D_MODEL), lambda i: (i, 0)),
                  pl.BlockSpec((tm, D_MODEL), lambda i: (i, 0)), pl.BlockSpec(memory_space=pltpu.VMEM),
                  pl.BlockSpec((1, D_MODEL), lambda i: (0, 0))],
        out_specs=(pl.BlockSpec((tm, D_MODEL), lambda i: (i, 0)), pl.BlockSpec((tm, D_MODEL), lambda i: (i, 0)),
                   pl.BlockSpec((tm, D_MIX), lambda i: (i, 0)),
                   pl.BlockSpec((1, D_MODEL), lambda i: (0, 0)), pl.BlockSpec((8, 128), lambda i: (0, 0))),
        out_shape=(jax.ShapeDtypeStruct((SEQ, D_MODEL), F32), jax.ShapeDtypeStruct((SEQ, D_MODEL), BF16),
                   jax.ShapeDtypeStruct((SEQ, D_MIX), F32),
                   jax.ShapeDtypeStruct((1, D_MODEL), F32), jax.ShapeDtypeStruct((8, 128), F32)),
        compiler_params=_params(dimension_semantics=("arbitrary",)),
    )(mixed, x, target, w_out_full, norm_final)


def _gated_norm_bwd(a, gain, t, dy):
    r = lax.rsqrt(jnp.mean(a * a, axis=-1, keepdims=True) + RMS_EPS)
    an = a * r
    sg = _sigmoid(t)
    dn = dy * (t * sg)
    dt = dy * (an * gain) * (sg * (1.0 + t * (1.0 - sg)))
    u = dn * gain
    da = r * (u - an * jnp.mean(u * an, axis=-1, keepdims=True))
    return da, dt, dn * an


def _mix_bwd(proj, dmixed, conv_full, sinks, norm_conv, norm_attn):
    def body(pj_ref, kvp_ref, cch_ref, cuh_ref, dmx_ref, cw_ref, sink_ref, gc_ref, ga_ref,
             dpj_ref, gslab_ref, attn_scr, dattn_scr, p_scr, ps_scr, nxt_scr, dkv_scr, bias_scr, acc_scr, ostack_scr):
        step = pl.program_id(0)
        n = N_BLOCKS - 1 - step
        pj = pj_ref

        @pl.when(step == 0)
        def _():
            gslab_ref[...] = jnp.zeros_like(gslab_ref)
            nxt_scr[...] = jnp.zeros_like(nxt_scr)
            dkv_scr[...] = jnp.zeros_like(dkv_scr)
            acc_scr[...] = jnp.zeros_like(acc_scr)
            _fill_attn_bias(bias_scr, first_block=False)

        @pl.when(n == 0)
        def _():
            _fill_attn_bias(bias_scr, first_block=True)

        zhalo = _conv_halo(cch_ref, cuh_ref, n)
        cw = (cw_ref[0:1, :], cw_ref[1:2, :], cw_ref[2:3, :])
        gain_c = gc_ref[...]
        row = lax.broadcasted_iota(jnp.int32, (CHUNK, D_CONV), 0)

        def conv_chunk(t, dco_after):
            r = N_CHUNKS - 1 - t
            rows = _chunk_rows(r)
            cc, cu, z, z1, z2, co = _conv_chunk(pj_ref, zhalo, cw, r)
            cb = pj_ref[rows, OFF_CB:OFF_CB + D_CONV]
            da, dgate, gterm = _gated_norm_bwd(cb * co, gain_c, pj_ref[rows, OFF_GC:OFF_GC + D_CONV],
                                               dmx_ref[rows, 0:D_CONV])
            dpj_ref[rows, OFF_GC:OFF_GC + D_CONV] = dgate.astype(BF16)
            dpj_ref[rows, OFF_CB:OFF_CB + D_CONV] = (da * co).astype(BF16)
            dco = da * cb
            dco1 = jnp.where(row >= CHUNK - 1, pltpu.roll(dco_after, CHUNK - 1, 0), pltpu.roll(dco, CHUNK - 1, 0))
            dco2 = jnp.where(row >= CHUNK - 2, pltpu.roll(dco_after, CHUNK - 2, 0), pltpu.roll(dco, CHUNK - 2, 0))
            dz = cw[2] * dco + cw[1] * dco1 + cw[0] * dco2
            dpj_ref[rows, OFF_CC:OFF_CC + D_CONV] = (dz * cu).astype(BF16)
            dpj_ref[rows, OFF_CU:OFF_CU + D_CONV] = (dz * cc).astype(BF16)
            acc_scr[ACC_NORM_CONV] += gterm
            acc_scr[ACC_CONV0] += dco * z2
            acc_scr[ACC_CONV0 + 1] += dco * z1
            acc_scr[ACC_CONV0 + 2] += dco * z
            return dco

        nxt_scr[...] = lax.fori_loop(0, N_CHUNKS, conv_chunk, nxt_scr[...], unroll=True)

        ks, vs = _kv_bands(pj, kvp_ref)
        for j in range(2):
            out, probs, shares = _attn_group(pj, ks, vs, bias_scr, sink_ref, j)
            ostack_scr[j] = out
            for e in range(2):
                p_scr[2 * j + e] = probs[e]
                ps_scr[2 * j + e] = shares[e]
            for i in range(PAIRS_PER_KV):
                attn_scr[:, _pair_cols(j, i, 0)] = out[BLOCK * i:BLOCK * (i + 1), :]
        gain_a = ga_ref[...]

        def norm_chunk(r, carry):
            rows = _chunk_rows(r)
            da, dgate, gterm = _gated_norm_bwd(attn_scr[rows, :], gain_a, pj_ref[rows, OFF_GA:OFF_GA + D_ATTN],
                                               dmx_ref[rows, D_CONV:D_MIX])
            dpj_ref[rows, OFF_GA:OFF_GA + D_ATTN] = dgate.astype(BF16)
            dattn_scr[rows, :] = da
            acc_scr[ACC_NORM_ATTN] += gterm
            return carry

        lax.fori_loop(0, N_CHUNKS, norm_chunk, 0, unroll=True)

        lo = lax.broadcasted_iota(jnp.int32, (STACK, 128), 1) < HEAD_DIM
        lane_s = lax.broadcasted_iota(jnp.int32, (1, D_MODEL), 1)
        gsink = jnp.zeros((1, D_MODEL), F32)
        dk_t, dv_t = [], []
        for j in range(2):
            q_stack = _q_stack(pj, j)
            do_f = jnp.concatenate([dattn_scr[:, _pair_cols(j, i, 0)] for i in range(PAIRS_PER_KV)], axis=0)
            prod = do_f * ostack_scr[j]
            tot = jnp.sum(prod, axis=-1, keepdims=True)
            d_lo = jnp.sum(jnp.where(lo, prod, 0.0), axis=-1, keepdims=True)
            deltas = (d_lo, tot - d_lo)
            do_b = do_f.astype(BF16)
            dq, dk_j, dv_j = None, None, None
            for e in range(2):
                p = p_scr[2 * j + e]
                dp = lax.dot_general(do_b, vs[j][e], _NT, preferred_element_type=F32)
                ds = (p * (dp - deltas[e])).astype(BF16)
                gs = ps_scr[2 * j + e] * deltas[e]
                for i in range(PAIRS_PER_KV):
                    gs_h = -jnp.sum(gs[BLOCK * i:BLOCK * (i + 1), :], axis=0, keepdims=True)
                    gsink = gsink + jnp.where(lane_s == _head(j, i, e), gs_h, 0.0)
                t = jnp.dot(ds, ks[j][e], preferred_element_type=F32)
                dq = t if dq is None else dq + t
                half = slice(HEAD_DIM * e, HEAD_DIM * (e + 1))
                a = lax.dot_general(q_stack, ds, _TN, preferred_element_type=F32)[half, :]
                b = lax.dot_general(do_b, p.astype(BF16), _TN, preferred_element_type=F32)[half, :]
                dk_j = a if dk_j is None else dk_j + a
                dv_j = b if dv_j is None else dv_j + b
            for i in range(PAIRS_PER_KV):
                dpj_ref[:, _pair_cols(j, i, OFF_Q)] = (dq[BLOCK * i:BLOCK * (i + 1), :] * SCALE).astype(BF16)
            dk_t.append(dk_j)
            dv_t.append(dv_j)
        dk = jnp.concatenate(dk_t, axis=0).T
        dv = jnp.concatenate(dv_t, axis=0).T
        dpj_ref[:, OFF_K:OFF_K + D_KV] = (dk[BLOCK:, :] + dkv_scr[:, 0:D_KV]).astype(BF16)
        dpj_ref[:, OFF_V:OFF_V + D_KV] = (dv[BLOCK:, :] + dkv_scr[:, D_KV:2 * D_KV]).astype(BF16)
        dkv_scr[:, 0:D_KV] = dk[:BLOCK, :]
        dkv_scr[:, D_KV:2 * D_KV] = dv[:BLOCK, :]
        gslab_ref[ROW_SINKS:ROW_SINKS + 1, :] += gsink

        @pl.when(step == N_BLOCKS - 1)
        def _():
            for k, slab_row in ((ACC_NORM_CONV, ROW_NORM_CONV), (ACC_NORM_ATTN, ROW_NORM_ATTN), (ACC_CONV0, ROW_CONV0),
                                (ACC_CONV0 + 1, ROW_CONV0 + 1), (ACC_CONV0 + 2, ROW_CONV0 + 2)):
                gslab_ref[slab_row:slab_row + 1, :] = jnp.sum(acc_scr[k], axis=0, keepdims=True)

    per_block = BLOCK // HALO
    last = N_BLOCKS - 1
    return pl.pallas_call(
        body, name="mix_bwd", grid=(N_BLOCKS,),
        in_specs=[
            pl.BlockSpec((BLOCK, D_PROJ), lambda s: (last - s, 0)),
            pl.BlockSpec((BLOCK, 2 * D_KV), lambda s: (jnp.maximum(last - s - 1, 0), OFF_K // (2 * D_KV))),
            pl.BlockSpec((HALO, D_CONV), lambda s: (jnp.maximum((last - s) * per_block - 1, 0), OFF_CC // D_CONV)),
            pl.BlockSpec((HALO, D_CONV), lambda s: (jnp.maximum((last - s) * per_block - 1, 0), OFF_CU // D_CONV)),
            pl.BlockSpec((BLOCK, D_MIX), lambda s: (last - s, 0)),
            pl.BlockSpec((8, D_CONV), lambda s: (0, 0)),
            pl.BlockSpec(memory_space=pltpu.SMEM),
            pl.BlockSpec((1, D_CONV), lambda s: (0, 0)),
            pl.BlockSpec((1, D_ATTN), lambda s: (0, 0)),
        ],
        out_specs=(pl.BlockSpec((BLOCK, D_PROJ), lambda s: (last - s, 0)),
                   pl.BlockSpec((8, D_MODEL), lambda s: (0, 0))),
        out_shape=(jax.ShapeDtypeStruct((SEQ, D_PROJ), BF16), jax.ShapeDtypeStruct((8, D_MODEL), F32)),
        scratch_shapes=[pltpu.VMEM((BLOCK, D_ATTN), F32), pltpu.VMEM((BLOCK, D_ATTN), F32),
                        pltpu.VMEM((4, STACK, 2 * BLOCK), F32), pltpu.VMEM((4, STACK, 1), F32),
                        pltpu.VMEM((CHUNK, D_CONV), F32), pltpu.VMEM((BLOCK, 2 * D_KV), F32),
                        pltpu.VMEM((4, STACK, 2 * BLOCK), F32), pltpu.VMEM((N_ACC, CHUNK, D_MODEL), F32),
                        pltpu.VMEM((2, STACK, 128), F32)],
        compiler_params=_params(dimension_semantics=("arbitrary",)),
    )(proj, proj, proj, proj, dmixed, conv_full, sinks, norm_conv, norm_attn)


def _in_bwd_rs(dproj, w_full, x, dx2, norm_in, dw_in_sh, gslab, gnf, loss_part):
    tm = 256
    steps = SEQ // tm
    forward_step = 2

    def body(dp_ref, w_ref, x_ref, dx2_ref, g_ref, dwi_ref, gs_ref, gnf_ref, lp_ref, gx_ref, gwin_ref, gsum_ref,
             gni_scr, own, d2d, ici, myslab, slabs, send_sems, recv_sems, local_sems):
        i = pl.program_id(0)
        rs_start, rs_forward, rs_finish = _shard_sum(dwi_ref, own, d2d, ici, send_sems, recv_sems, local_sems)
        slab_start, slab_finish = _slab_sum(myslab, slabs, send_sems, recv_sems, N_SHARD_SUM_SEMS)

        @pl.when(i == 0)
        def _():
            gni_scr[...] = jnp.zeros_like(gni_scr)
            rs_start()

        dh = jnp.dot(dp_ref[...], w_ref[...], preferred_element_type=F32)
        xv = x_ref[...]
        r = lax.rsqrt(jnp.mean(xv * xv, axis=-1, keepdims=True) + RMS_EPS)
        xn = xv * r
        u = dh * g_ref[...]
        gx_ref[...] = dx2_ref[...] + r * (u - xn * jnp.mean(u * xn, axis=-1, keepdims=True))
        gni_scr[...] += jnp.sum(dh * xn, axis=0, keepdims=True)

        @pl.when(i == forward_step)
        def _():
            rs_forward()

        @pl.when(i == steps - 1)
        def _():
            row = lax.broadcasted_iota(jnp.int32, (8, D_MODEL), 0)
            lane = lax.broadcasted_iota(jnp.int32, (8, D_MODEL), 1)
            slab = jnp.where(row == ROW_NORM_IN, gni_scr[...], jnp.where(row == ROW_NORM_FINAL, gnf_ref[...], gs_ref[...]))
            myslab[...] = jnp.where((row == ROW_SINKS) & (lane == LOSS_LANE), lp_ref[0:1, 0:1], slab)
            slab_start()
            gwin_ref[...] = rs_finish()
            gsum_ref[...] = slab_finish()

    const = lambda i: (0, 0)
    return pl.pallas_call(
        body, name="in_bwd", grid=(steps,),
        in_specs=[pl.BlockSpec((tm, D_PROJ), lambda i: (i, 0)), pl.BlockSpec(memory_space=pltpu.VMEM),
                  pl.BlockSpec((tm, D_MODEL), lambda i: (i, 0)), pl.BlockSpec((tm, D_MODEL), lambda i: (i, 0)),
                  pl.BlockSpec((1, D_MODEL), const), pl.BlockSpec(memory_space=pl.ANY),
                  pl.BlockSpec((8, D_MODEL), const), pl.BlockSpec((1, D_MODEL), const), pl.BlockSpec((8, 128), const)],
        out_specs=(pl.BlockSpec((tm, D_MODEL), lambda i: (i, 0)), pl.BlockSpec((SHARD_IN, D_MODEL), const),
                   pl.BlockSpec((8, D_MODEL), const)),
        out_shape=(jax.ShapeDtypeStruct((SEQ, D_MODEL), F32), jax.ShapeDtypeStruct((SHARD_IN, D_MODEL), F32),
                   jax.ShapeDtypeStruct((8, D_MODEL), F32)),
        scratch_shapes=[pltpu.VMEM((1, D_MODEL), F32), *_shard_sum_scratch(SHARD_IN),
                        pltpu.VMEM((8, D_MODEL), F32), pltpu.VMEM((N_DEV, 8, D_MODEL), F32),
                        pltpu.SemaphoreType.DMA((N_SHARD_SUM_SEMS + 7,)), pltpu.SemaphoreType.DMA((N_SHARD_SUM_SEMS + 7,)),
                        pltpu.SemaphoreType.DMA((4,))],
        compiler_params=_params(dimension_semantics=("arbitrary",)),
    )(dproj, w_full, x, dx2, norm_in, dw_in_sh, gslab, gnf, loss_part)


def _dw_in_rs(dproj, h, dw_out_sh):
    tn = 640
    steps = D_PROJ // tn
    forward_step = 2

    def body(a_ref, b_ref, dwo_ref, o_ref, gwo_ref, own, d2d, ici, send_sems, recv_sems, local_sems):
        i = pl.program_id(0)
        rs_start, rs_forward, rs_finish = _shard_sum(dwo_ref, own, d2d, ici, send_sems, recv_sems, local_sems)

        @pl.when(i == 0)
        def _():
            rs_start()

        o_ref[...] = lax.dot_general(a_ref[...], b_ref[...], _TN, preferred_element_type=F32).astype(BF16)

        @pl.when(i == forward_step)
        def _():
            rs_forward()

        @pl.when(i == steps - 1)
        def _():
            gwo_ref[...] = rs_finish()

    return pl.pallas_call(
        body, name="dw_in", grid=(steps,),
        in_specs=[pl.BlockSpec((SEQ, tn), lambda i: (0, i)), pl.BlockSpec(memory_space=pltpu.VMEM),
                  pl.BlockSpec(memory_space=pl.ANY)],
        out_specs=(pl.BlockSpec((tn, D_MODEL), lambda i: (i, 0)), pl.BlockSpec((SHARD_OUT, D_MODEL), lambda i: (0, 0))),
        out_shape=(jax.ShapeDtypeStruct((D_PROJ, D_MODEL), BF16), jax.ShapeDtypeStruct((SHARD_OUT, D_MODEL), F32)),
        scratch_shapes=[*_shard_sum_scratch(SHARD_OUT),
                        pltpu.SemaphoreType.DMA((N_SHARD_SUM_SEMS,)), pltpu.SemaphoreType.DMA((N_SHARD_SUM_SEMS,)),
                        pltpu.SemaphoreType.DMA((4,))],
        compiler_params=_params(dimension_semantics=("arbitrary",)),
    )(dproj, h, dw_out_sh)


def _matmul_tn(a, b, tn, name):
    k, n = a.shape
    _, m = b.shape

    def body(a_ref, b_ref, o_ref):
        o_ref[...] = lax.dot_general(a_ref[...], b_ref[...], _TN, preferred_element_type=F32).astype(BF16)

    return pl.pallas_call(
        body, name=name, grid=(n // tn,),
        in_specs=[pl.BlockSpec((k, tn), lambda i: (0, i)), pl.BlockSpec(memory_space=pltpu.VMEM)],
        out_specs=pl.BlockSpec((tn, m), lambda i: (i, 0)),
        out_shape=jax.ShapeDtypeStruct((n, m), BF16),
        compiler_params=_params(dimension_semantics=("arbitrary",)),
    )(a, b)


def _adam_all(big_in, big_out, gsum, small):
    steps = 4
    tr_in, tr_out = SHARD_IN // steps, SHARD_OUT // steps

    def body(*refs):
        ins, outs = refs[:8 + 1 + 18], refs[8 + 1 + 18:]
        i = pl.program_id(0)
        for b in range(2):
            w_ref, g_ref, m_ref, v_ref = ins[4 * b:4 * b + 4]
            g = g_ref[...]
            delta, mn, vn = _adamw(w_ref[...], g, m_ref[...], v_ref[...])
            for ref, val in zip(outs[4 * b:4 * b + 4], (g, delta, mn, vn)):
                ref[...] = val

        @pl.when(i == 0)
        def _():
            gsum = ins[8][...]
            idx = _slot(lax.axis_index("x"), lax.axis_index("y"), lax.axis_index("c"))
            cg = jnp.zeros((3, SHARD_CONV), F32)
            for d in range(N_DEV):
                cg = jnp.where(idx == d, gsum[ROW_CONV0:ROW_CONV0 + 3, d * SHARD_CONV:(d + 1) * SHARD_CONV], cg)
            grads = (gsum[ROW_NORM_IN:ROW_NORM_IN + 1], gsum[ROW_SINKS:ROW_SINKS + 1, 0:N_Q_HEADS],
                     gsum[ROW_NORM_CONV:ROW_NORM_CONV + 1], gsum[ROW_NORM_ATTN:ROW_NORM_ATTN + 1],
                     gsum[ROW_NORM_FINAL:ROW_NORM_FINAL + 1], cg)
            for s, g in enumerate(grads):
                w_ref, m_ref, v_ref = ins[9 + 3 * s:12 + 3 * s]
                delta, mn, vn = _adamw(w_ref[...], g, m_ref[...], v_ref[...])
                for ref, val in zip(outs[8 + 4 * s:12 + 4 * s], (g, delta, mn, vn)):
                    ref[...] = val
            outs[32][...] = gsum[ROW_SINKS:ROW_SINKS + 1, LOSS_LANE:LOSS_LANE + 1]

    const = lambda i: (0, 0)
    rows = lambda i: (i, 0)
    small_shapes = [a.shape for a in small[::3]]
    in_specs = ([pl.BlockSpec((tr_in, D_MODEL), rows)] * 4 + [pl.BlockSpec((tr_out, D_MODEL), rows)] * 4
                + [pl.BlockSpec((8, D_MODEL), const)] + [pl.BlockSpec(a.shape, const) for a in small])
    out_specs = ([pl.BlockSpec((tr_in, D_MODEL), rows)] * 4 + [pl.BlockSpec((tr_out, D_MODEL), rows)] * 4
                 + [pl.BlockSpec(s, const) for s in small_shapes for _ in range(4)] + [pl.BlockSpec((1, 1), const)])
    out_shape = ([jax.ShapeDtypeStruct((SHARD_IN, D_MODEL), F32)] * 4 + [jax.ShapeDtypeStruct((SHARD_OUT, D_MODEL), F32)] * 4
                 + [jax.ShapeDtypeStruct(s, F32) for s in small_shapes for _ in range(4)]
                 + [jax.ShapeDtypeStruct((1, 1), F32)])
    outs = pl.pallas_call(
        body, name="adam", grid=(steps,), in_specs=in_specs, out_specs=tuple(out_specs), out_shape=tuple(out_shape),
        compiler_params=_params(dimension_semantics=("arbitrary",)),
    )(*big_in, *big_out, gsum, *small)
    return outs[0:4], outs[4:8], [outs[8 + 4 * s:12 + 4 * s] for s in range(6)], outs[32]


def _pad_rows(a, rows=8):
    return jnp.pad(a, ((0, rows - a.shape[0]), (0, 0)))


def kernel(x, norm_in, w_in, conv_w, attn_sinks, norm_conv_out, norm_attn_out, w_out, norm_final, loss_target, m_norm_in, m_w_in, m_conv_w, m_attn_sinks, m_norm_conv_out, m_norm_attn_out, m_w_out, m_norm_final, v_norm_in, v_w_in, v_conv_w, v_attn_sinks, v_norm_conv_out, v_norm_attn_out, v_w_out, v_norm_final):
    x2d = x.reshape(SEQ, D_MODEL)
    target = loss_target.reshape(SEQ, D_MODEL)
    nf = norm_final.reshape(1, D_MODEL)

    w_in_t, m_w_in_t, v_w_in_t = w_in[0].T, m_w_in[0].T, v_w_in[0].T
    g_in, w_out_b, conv_full = _all_gather(w_in_t, w_out[0], _pad_rows(conv_w[0]))
    w_in_full = g_in.reshape(D_PROJ, D_MODEL)
    sinks = attn_sinks.reshape(N_Q_HEADS)

    h, proj, g_out = _in_proj(x2d, norm_in, w_in_full, w_out_b)
    w_out_full = g_out.reshape(D_MIX, D_MODEL)
    mixed = _mix_fwd(proj, conv_full, sinks, norm_conv_out, norm_attn_out)
    dx2, dx2b, dmixed, gnf, loss_part = _out_proj_loss(mixed, x2d, target, w_out_full, nf)
    dproj, gslab = _mix_bwd(proj, dmixed, conv_full, sinks, norm_conv_out, norm_attn_out)
    dw_out = _matmul_tn(mixed, dx2b, 512, "dw_out")
    dw_in, g_w_out = _dw_in_rs(dproj, h, dw_out.reshape(N_DEV, SHARD_OUT, D_MODEL))
    grad_x, g_w_in, gsum = _in_bwd_rs(dproj, w_in_full, x2d, dx2, norm_in, dw_in.reshape(N_DEV, SHARD_IN, D_MODEL),
                                      gslab, gnf, loss_part)

    small = (norm_in, m_norm_in, v_norm_in, attn_sinks, m_attn_sinks, v_attn_sinks,
             norm_conv_out, m_norm_conv_out, v_norm_conv_out, norm_attn_out, m_norm_attn_out, v_norm_attn_out,
             nf, m_norm_final.reshape(1, D_MODEL), v_norm_final.reshape(1, D_MODEL),
             conv_w[0], m_conv_w[0], v_conv_w[0])
    big_in, big_out, (s_ni, s_sk, s_nc, s_na, s_nf, s_cv), loss = _adam_all(
        (w_in_t, g_w_in, m_w_in_t, v_w_in_t), (w_out[0], g_w_out, m_w_out[0], v_w_out[0]), gsum, small)

    def leaves(k):
        return (s_ni[k], big_in[k].T[None], s_cv[k][None], s_sk[k], s_nc[k], s_na[k], big_out[k][None],
                s_nf[k].reshape(D_MODEL))

    return (loss.reshape(()), grad_x.reshape(1, SEQ, D_MODEL), *leaves(0), *leaves(1), *leaves(2), *leaves(3))
```

```python
import functools
import math

import jax
import jax.numpy as jnp
from jax import lax
from jax.experimental import pallas as pl
from jax.experimental.pallas import tpu as pltpu

F32 = jnp.float32
BF16 = jnp.bfloat16
MESH = pl.DeviceIdType.MESH

N_DEV = 8
SEQ = 2048
D_MODEL = 1024
D_CONV = 1024
D_ATTN = 1024
D_KV = 128
HEAD_DIM = 64
N_Q_HEADS = 16
N_PAIRS = N_Q_HEADS // 2
PAIRS_PER_KV = N_PAIRS // 2
D_MIX = D_CONV + D_ATTN
D_PROJ = 6400
SHARD_IN = D_PROJ // N_DEV
SHARD_OUT = D_MIX // N_DEV
SHARD_CONV = D_CONV // N_DEV
OFF_CB, OFF_CC, OFF_CU, OFF_GC, OFF_Q, OFF_K, OFF_V, OFF_GA = 0, 1024, 2048, 3072, 4096, 5120, 5248, 5376
BLOCK = 128
N_BLOCKS = SEQ // BLOCK
HALO = 8
CHUNK = 16
N_CHUNKS = BLOCK // CHUNK
RMS_EPS = 1e-5
NEG = -1e30
SCALE = HEAD_DIM ** -0.5
SLOPES = tuple(2.0 ** (-8.0 * (h + 1) / N_Q_HEADS) for h in range(N_Q_HEADS))

ADAM_LR = 0.001
ADAM_B1 = 0.9
ADAM_B2 = 0.999
ADAM_EPS = 1e-08
ADAM_WD = 0.01
ADAM_STEP = 10

ROW_NORM_IN, ROW_NORM_CONV, ROW_NORM_ATTN, ROW_NORM_FINAL, ROW_CONV0, ROW_SINKS = 0, 1, 2, 3, 4, 7
LOSS_LANE = N_Q_HEADS
ACC_NORM_CONV, ACC_NORM_ATTN, ACC_CONV0, N_ACC = 0, 1, 2, 5

VMEM_LIMIT = 56 * 1024 * 1024

_NT = (((1,), (1,)), ((), ()))
_TN = (((0,), (0,)), ((), ()))


def _params(**kw):
    return pltpu.CompilerParams(vmem_limit_bytes=VMEM_LIMIT, **kw)


def _adamw(w, g, m, v):
    m = ADAM_B1 * m + (1.0 - ADAM_B1) * g
    v = ADAM_B2 * v + (1.0 - ADAM_B2) * (g * g)
    m_hat = m / (1.0 - ADAM_B1 ** ADAM_STEP)
    v_hat = v / (1.0 - ADAM_B2 ** ADAM_STEP)
    delta = -ADAM_LR * (m_hat / (jnp.sqrt(v_hat) + ADAM_EPS) + ADAM_WD * w)
    return delta, m, v


def _sigmoid(t):
    return 1.0 / (1.0 + jnp.exp(-t))


def _slot(px, py, pc):
    return 4 * px + 2 * py + pc


def _all_gather(w_in_sh, w_out_sh, conv_sh):
    def body(win_ref, wout_ref, cv_ref, gin_ref, woutb_ref, conv_ref, gcv_ref, send_sems, recv_sems):
        x, y, c = lax.axis_index("x"), lax.axis_index("y"), lax.axis_index("c")
        me, sibling = (x, y, c), (x, y, 1 - c)
        chips = [(1 - x, y), (x, 1 - y), (1 - x, 1 - y)]
        bufs = (gin_ref, gcv_ref)

        gin_ref[_slot(*me)] = win_ref[...].astype(BF16)
        gcv_ref[_slot(*me)] = cv_ref[...]

        def copy(b, k, block, to):
            s = _slot(*block)
            return pltpu.make_async_remote_copy(
                src_ref=bufs[b].at[s], dst_ref=bufs[b].at[s],
                send_sem=send_sems.at[7 * b + k], recv_sem=recv_sems.at[7 * b + k],
                device_id=to, device_id_type=MESH)

        started = []
        for b in range(2):
            first = [copy(b, 0, me, sibling)]
            first += [copy(b, 1 + j, me, (*chip, c)) for j, chip in enumerate(chips)]
            for cp in first:
                cp.start()
            started += first
        woutb_ref[...] = wout_ref[...].astype(BF16)
        for b in range(2):
            for j, chip in enumerate(chips):
                copy(b, 1 + j, (*chip, c), me).wait_recv()
                fwd = copy(b, 4 + j, (*chip, c), sibling)
                fwd.start()
                started.append(fwd)
        for b in range(2):
            copy(b, 0, sibling, me).wait_recv()
            for j, chip in enumerate(chips):
                copy(b, 4 + j, (*chip, 1 - c), me).wait_recv()
        for d in range(N_DEV):
            conv_ref[:, d * SHARD_CONV:(d + 1) * SHARD_CONV] = gcv_ref[d]
        for cp in started:
            cp.wait_send()

    vmem = pl.BlockSpec(memory_space=pltpu.VMEM)
    return pl.pallas_call(
        body, name="all_gather",
        out_shape=(jax.ShapeDtypeStruct((N_DEV, SHARD_IN, D_MODEL), BF16),
                   jax.ShapeDtypeStruct((SHARD_OUT, D_MODEL), BF16),
                   jax.ShapeDtypeStruct((8, D_CONV), F32)),
        in_specs=[vmem, vmem, vmem], out_specs=(vmem, vmem, vmem),
        scratch_shapes=[pltpu.VMEM((N_DEV, 8, SHARD_CONV), F32),
                        pltpu.SemaphoreType.DMA((14,)), pltpu.SemaphoreType.DMA((14,))],
        compiler_params=_params(),
    )(w_in_sh, w_out_sh, conv_sh)


def _shard_sum(src, own, d2d, ici, send_sems, recv_sems, local_sems, base=0):
    x, y, c = lax.axis_index("x"), lax.axis_index("y"), lax.axis_index("c")
    sibling = (x, y, 1 - c)
    chips = [(x, y), (1 - x, y), (x, 1 - y), (1 - x, 1 - y)]

    def rcopy(s, d, k, to):
        return pltpu.make_async_remote_copy(src_ref=s, dst_ref=d, send_sem=send_sems.at[base + k],
                                            recv_sem=recv_sems.at[base + k], device_id=to, device_id_type=MESH)

    def mine(k):
        return pltpu.make_async_copy(src.at[_slot(*chips[k], c)], own.at[k], local_sems.at[k])

    def to_sibling(k):
        return rcopy(src.at[_slot(*chips[k], 1 - c)], d2d.at[k], k, sibling)

    def to_chip(k):
        return rcopy(own.at[k], ici.at[k - 1], 3 + k, (*chips[k], c))

    def start():
        for k in range(4):
            mine(k).start()
            to_sibling(k).start()

    def forward():
        for k in range(1, 4):
            mine(k).wait()
            to_sibling(k).wait_recv()
            own[k] = (own[k].astype(F32) + d2d[k].astype(F32)).astype(BF16)
            to_chip(k).start()

    def finish():
        mine(0).wait()
        to_sibling(0).wait_recv()
        acc = own[0].astype(F32) + d2d[0].astype(F32)
        for k in range(1, 4):
            to_chip(k).wait_recv()
            acc = acc + ici[k - 1].astype(F32)
        for k in range(4):
            to_sibling(k).wait_send()
        for k in range(1, 4):
            to_chip(k).wait_send()
        return acc

    return start, forward, finish


def _shard_sum_scratch(rows):
    return [pltpu.VMEM((4, rows, D_MODEL), BF16), pltpu.VMEM((4, rows, D_MODEL), BF16),
            pltpu.VMEM((3, rows, D_MODEL), BF16)]


N_SHARD_SUM_SEMS = 7


def _chip_sum(dwt, d2d, out_hbm, send_sems, recv_sems, local_sems, base, local_base):
    x, y, c = lax.axis_index("x"), lax.axis_index("y"), lax.axis_index("c")
    sibling = (x, y, 1 - c)
    chips = [(x, y), (1 - x, y), (x, 1 - y), (1 - x, 1 - y)]

    def shard(s):
        return dwt.at[pl.ds(pl.multiple_of(s * SHARD_IN, 16), SHARD_IN), :]

    def to_sibling(k):
        return pltpu.make_async_remote_copy(src_ref=shard(_slot(*chips[k], 1 - c)), dst_ref=d2d.at[k],
                                            send_sem=send_sems.at[base + k], recv_sem=recv_sems.at[base + k],
                                            device_id=sibling, device_id_type=MESH)

    def save(k):
        return pltpu.make_async_copy(d2d.at[k], out_hbm.at[k], local_sems.at[local_base + k])

    def send(first):
        for k in range(4):
            in_first = _slot(*chips[k], 1 - c) < N_DEV // 2

            @pl.when(in_first if first else jnp.logical_not(in_first))
            def _():
                to_sibling(k).start()

    def finish():
        for k in range(4):
            to_sibling(k).wait_recv()
            d2d[k] = (shard(_slot(*chips[k], c))[...].astype(F32) + d2d[k].astype(F32)).astype(BF16)
            save(k).start()
        for k in range(4):
            save(k).wait()
            to_sibling(k).wait_send()

    return send, finish


def _ici_sum(src, own, ici, send_sems, recv_sems, local_sem, base=0):
    x, y, c = lax.axis_index("x"), lax.axis_index("y"), lax.axis_index("c")
    chips = [(x, y), (1 - x, y), (x, 1 - y), (1 - x, 1 - y)]

    def mine():
        return pltpu.make_async_copy(src.at[0], own, local_sem)

    def to_chip(k):
        return pltpu.make_async_remote_copy(src_ref=src.at[k], dst_ref=ici.at[k - 1], send_sem=send_sems.at[base + k - 1],
                                            recv_sem=recv_sems.at[base + k - 1], device_id=(*chips[k], c),
                                            device_id_type=MESH)

    def start():
        mine().start()
        for k in range(1, 4):
            to_chip(k).start()

    def finish():
        mine().wait()
        acc = own[...].astype(F32)
        for k in range(1, 4):
            to_chip(k).wait_recv()
            acc = acc + ici[k - 1].astype(F32)
        for k in range(1, 4):
            to_chip(k).wait_send()
        return acc

    return start, finish


def _slab_sum(myslab, slabs, send_sems, recv_sems, base):
    x, y, c = lax.axis_index("x"), lax.axis_index("y"), lax.axis_index("c")
    me = _slot(x, y, c)
    peers = [(x, y, 1 - c), (1 - x, y, c), (x, 1 - y, c), (1 - x, 1 - y, c),
             (1 - x, y, 1 - c), (x, 1 - y, 1 - c), (1 - x, 1 - y, 1 - c)]

    def cp(k):
        return pltpu.make_async_remote_copy(src_ref=myslab, dst_ref=slabs.at[me], send_sem=send_sems.at[base + k],
                                            recv_sem=recv_sems.at[base + k], device_id=peers[k], device_id_type=MESH)

    def start():
        slabs[me] = myslab[...]
        for k in range(7):
            cp(k).start()

    def finish():
        for k in range(7):
            cp(k).wait_recv()
        total = slabs[0]
        for d in range(1, N_DEV):
            total = total + slabs[d]
        for k in range(7):
            cp(k).wait_send()
        return total

    return start, finish


def _in_proj(x, norm_in, w_full, w_out_b):
    tm = 256
    steps = SEQ // tm
    forward_step = 5

    def body(x_ref, g_ref, w_ref, wo_ref, h_ref, proj_ref, gout_ref, send_sems, recv_sems, local_sem):
        i = pl.program_id(0)
        x, y, c = lax.axis_index("x"), lax.axis_index("y"), lax.axis_index("c")
        me, sibling = (x, y, c), (x, y, 1 - c)
        chips = [(1 - x, y), (x, 1 - y), (1 - x, 1 - y)]

        def copy(k, block, to, src=None):
            rows = gout_ref.at[_slot(*block)]
            return pltpu.make_async_remote_copy(src_ref=rows if src is None else src, dst_ref=rows,
                                                send_sem=send_sems.at[k], recv_sem=recv_sems.at[k],
                                                device_id=to, device_id_type=MESH)

        mine = pltpu.make_async_copy(wo_ref, gout_ref.at[_slot(*me)], local_sem)

        @pl.when(i == 0)
        def _():
            mine.start()
            copy(0, me, sibling, src=wo_ref).start()
            for j, chip in enumerate(chips):
                copy(1 + j, me, (*chip, c), src=wo_ref).start()

        xv = x_ref[...]
        r = lax.rsqrt(jnp.mean(xv * xv, axis=-1, keepdims=True) + RMS_EPS)
        h = (xv * r * g_ref[...]).astype(BF16)
        h_ref[...] = h
        proj_ref[...] = lax.dot_general(h, w_ref[...], _NT, preferred_element_type=F32)

        @pl.when(i == forward_step)
        def _():
            for j, chip in enumerate(chips):
                copy(1 + j, (*chip, c), me).wait_recv()
                copy(4 + j, (*chip, c), sibling).start()

        @pl.when(i == steps - 1)
        def _():
            copy(0, sibling, me).wait_recv()
            for j, chip in enumerate(chips):
                copy(4 + j, (*chip, 1 - c), me).wait_recv()
            copy(0, me, sibling, src=wo_ref).wait_send()
            for j, chip in enumerate(chips):
                copy(1 + j, me, (*chip, c), src=wo_ref).wait_send()
                copy(4 + j, (*chip, c), sibling).wait_send()
            mine.wait()

    return pl.pallas_call(
        body, name="in_proj", grid=(steps,),
        in_specs=[pl.BlockSpec((tm, D_MODEL), lambda i: (i, 0)), pl.BlockSpec((1, D_MODEL), lambda i: (0, 0)),
                  pl.BlockSpec(memory_space=pltpu.VMEM), pl.BlockSpec(memory_space=pl.ANY)],
        out_specs=(pl.BlockSpec((tm, D_MODEL), lambda i: (i, 0)), pl.BlockSpec((tm, D_PROJ), lambda i: (i, 0)),
                   pl.BlockSpec(memory_space=pl.ANY)),
        out_shape=(jax.ShapeDtypeStruct((SEQ, D_MODEL), BF16), jax.ShapeDtypeStruct((SEQ, D_PROJ), F32),
                   jax.ShapeDtypeStruct((N_DEV, SHARD_OUT, D_MODEL), BF16)),
        scratch_shapes=[pltpu.SemaphoreType.DMA((7,)), pltpu.SemaphoreType.DMA((7,)), pltpu.SemaphoreType.DMA],
        compiler_params=_params(dimension_semantics=("arbitrary",)),
    )(x, norm_in, w_full, w_out_b)


def _chunk_rows(r):
    return pl.ds(pl.multiple_of(r * CHUNK, CHUNK), CHUNK)


def _conv_halo(cch_ref, cuh_ref, n):
    zh = jnp.where(n > 0, cch_ref[...] * cuh_ref[...], 0.0)
    return jnp.concatenate([zh] * (CHUNK // HALO), axis=0)


def _conv_chunk(pj_ref, zhalo, cw, r):
    rows = _chunk_rows(r)
    cc = pj_ref[rows, OFF_CC:OFF_CC + D_CONV]
    cu = pj_ref[rows, OFF_CU:OFF_CU + D_CONV]
    z = cc * cu
    before = _chunk_rows(jnp.maximum(r - 1, 0))
    zprev = jnp.where(r > 0, pj_ref[before, OFF_CC:OFF_CC + D_CONV] * pj_ref[before, OFF_CU:OFF_CU + D_CONV], zhalo)
    row = lax.broadcasted_iota(jnp.int32, (CHUNK, D_CONV), 0)
    z1 = jnp.where(row < 1, pltpu.roll(zprev, 1, 0), pltpu.roll(z, 1, 0))
    z2 = jnp.where(row < 2, pltpu.roll(zprev, 2, 0), pltpu.roll(z, 2, 0))
    co = cw[0] * z2 + cw[1] * z1 + cw[2] * z
    return cc, cu, z, z1, z2, co


def _gated_norm(a, gain, t):
    r = lax.rsqrt(jnp.mean(a * a, axis=-1, keepdims=True) + RMS_EPS)
    return a * r * gain * (t * _sigmoid(t))


def _kv_bands(pj, kvp_ref):
    lane = lax.broadcasted_iota(jnp.int32, (2 * BLOCK, D_KV), 1)
    lo = lane < HEAD_DIM

    def bands(prev, cur):
        b = jnp.concatenate([prev, cur], axis=0)
        br = pltpu.roll(b, HEAD_DIM, 1)
        zero = jnp.zeros_like(b)
        return ((jnp.where(lo, b, zero).astype(BF16), jnp.where(lo, zero, br).astype(BF16)),
                (jnp.where(lo, br, zero).astype(BF16), jnp.where(lo, zero, b).astype(BF16)))

    ks = bands(kvp_ref[:, 0:D_KV], pj[:, OFF_K:OFF_K + D_KV])
    vs = bands(kvp_ref[:, D_KV:2 * D_KV], pj[:, OFF_V:OFF_V + D_KV])
    return ks, vs


STACK = PAIRS_PER_KV * BLOCK


def _head(j, i, e):
    return 2 * (PAIRS_PER_KV * j + i) + e


def _pair_cols(j, i, off):
    p = PAIRS_PER_KV * j + i
    return slice(off + 128 * p, off + 128 * (p + 1))


def _fill_attn_bias(bias_scr, first_block):
    qi = lax.broadcasted_iota(jnp.int32, (BLOCK, 2 * BLOCK), 0)
    kj = lax.broadcasted_iota(jnp.int32, (BLOCK, 2 * BLOCK), 1)
    dist = BLOCK + qi - kj
    valid = (dist >= 0) & (dist < BLOCK)
    if first_block:
        valid = valid & (kj >= BLOCK)
    distf = dist.astype(F32)
    for j in range(2):
        for e in range(2):
            for i in range(PAIRS_PER_KV):
                bias_scr[2 * j + e, BLOCK * i:BLOCK * (i + 1), :] = jnp.where(valid, -SLOPES[_head(j, i, e)] * distf, NEG)


def _q_stack(pj, j):
    return jnp.concatenate([(pj[:, _pair_cols(j, i, OFF_Q)] * SCALE).astype(BF16) for i in range(PAIRS_PER_KV)], axis=0)


def _sink_col(sink_ref, j, e):
    return jnp.concatenate([jnp.full((BLOCK, 1), sink_ref[_head(j, i, e)], F32) for i in range(PAIRS_PER_KV)], axis=0)


def _attn_probs(q_stack, kband, bias, sink):
    s = lax.dot_general(q_stack, kband, _NT, preferred_element_type=F32) + bias
    m = jnp.maximum(jnp.max(s, axis=-1, keepdims=True), sink)
    p = jnp.exp(s - m)
    es = jnp.exp(sink - m)
    inv = 1.0 / (jnp.sum(p, axis=-1, keepdims=True) + es)
    return p * inv, es * inv


def _attn_group(pj, ks, vs, bias_scr, sink_ref, j):
    q_stack = _q_stack(pj, j)
    out, probs, shares = None, [], []
    for e in range(2):
        p, ps = _attn_probs(q_stack, ks[j][e], bias_scr[2 * j + e], _sink_col(sink_ref, j, e))
        o = jnp.dot(p.astype(BF16), vs[j][e], preferred_element_type=F32)
        out = o if out is None else out + o
        probs.append(p)
        shares.append(ps)
    return out, probs, shares


def _mix_fwd(proj, conv_full, sinks, norm_conv, norm_attn):
    def body(pj_ref, kvp_ref, cch_ref, cuh_ref, cw_ref, sink_ref, gc_ref, ga_ref, mixed_ref, attn_scr, bias_scr):
        n = pl.program_id(0)
        pj = pj_ref

        @pl.when(n == 0)
        def _():
            _fill_attn_bias(bias_scr, first_block=True)

        @pl.when(n == 1)
        def _():
            _fill_attn_bias(bias_scr, first_block=False)

        zhalo = _conv_halo(cch_ref, cuh_ref, n)
        cw = (cw_ref[0:1, :], cw_ref[1:2, :], cw_ref[2:3, :])
        gain_c = gc_ref[...]

        def conv_chunk(r, carry):
            rows = _chunk_rows(r)
            co = _conv_chunk(pj_ref, zhalo, cw, r)[-1]
            y = _gated_norm(pj_ref[rows, OFF_CB:OFF_CB + D_CONV] * co, gain_c, pj_ref[rows, OFF_GC:OFF_GC + D_CONV])
            mixed_ref[rows, 0:D_CONV] = y.astype(BF16)
            return carry

        lax.fori_loop(0, N_CHUNKS, conv_chunk, 0, unroll=True)

        ks, vs = _kv_bands(pj, kvp_ref)
        for j in range(2):
            out, _, _ = _attn_group(pj, ks, vs, bias_scr, sink_ref, j)
            for i in range(PAIRS_PER_KV):
                attn_scr[:, _pair_cols(j, i, 0)] = out[BLOCK * i:BLOCK * (i + 1), :]
        gain_a = ga_ref[...]

        def norm_chunk(r, carry):
            rows = _chunk_rows(r)
            y = _gated_norm(attn_scr[rows, :], gain_a, pj_ref[rows, OFF_GA:OFF_GA + D_ATTN])
            mixed_ref[rows, D_CONV:D_MIX] = y.astype(BF16)
            return carry

        lax.fori_loop(0, N_CHUNKS, norm_chunk, 0, unroll=True)

    per_block = BLOCK // HALO
    return pl.pallas_call(
        body, name="mix_fwd", grid=(N_BLOCKS,),
        in_specs=[
            pl.BlockSpec((BLOCK, D_PROJ), lambda n: (n, 0)),
            pl.BlockSpec((BLOCK, 2 * D_KV), lambda n: (jnp.maximum(n - 1, 0), OFF_K // (2 * D_KV))),
            pl.BlockSpec((HALO, D_CONV), lambda n: (jnp.maximum(n * per_block - 1, 0), OFF_CC // D_CONV)),
            pl.BlockSpec((HALO, D_CONV), lambda n: (jnp.maximum(n * per_block - 1, 0), OFF_CU // D_CONV)),
            pl.BlockSpec((8, D_CONV), lambda n: (0, 0)),
            pl.BlockSpec(memory_space=pltpu.SMEM),
            pl.BlockSpec((1, D_CONV), lambda n: (0, 0)),
            pl.BlockSpec((1, D_ATTN), lambda n: (0, 0)),
        ],
        out_specs=pl.BlockSpec((BLOCK, D_MIX), lambda n: (n, 0)),
        out_shape=jax.ShapeDtypeStruct((SEQ, D_MIX), BF16),
        scratch_shapes=[pltpu.VMEM((BLOCK, D_ATTN), F32), pltpu.VMEM((4, STACK, 2 * BLOCK), F32)],
        compiler_params=_params(dimension_semantics=("arbitrary",)),
    )(proj, proj, proj, proj, conv_full, sinks, norm_conv, norm_attn)


def _out_proj_loss(mixed, x, target, w_out_full, norm_final):
    tm = 256

    def body(mx_ref, x_ref, t_ref, w_ref, g_ref, dx2_ref, dx2b_ref, dmix_ref, gnf_ref, loss_ref):
        i = pl.program_id(0)
        w = w_ref[...]
        x2 = x_ref[...] + jnp.dot(mx_ref[...], w, preferred_element_type=F32)
        r = lax.rsqrt(jnp.mean(x2 * x2, axis=-1, keepdims=True) + RMS_EPS)
        xn = x2 * r
        g = g_ref[...]
        err = xn * g - t_ref[...]
        part = 0.5 * jnp.sum(jnp.mean(err * err, axis=-1, keepdims=True), axis=0, keepdims=True)
        dy = err * (1.0 / D_MODEL)
        gnf = jnp.sum(dy * xn, axis=0, keepdims=True)
        u = dy * g
        dx2 = r * (u - xn * jnp.mean(u * xn, axis=-1, keepdims=True))
        dx2_ref[...] = dx2
        dx2b = dx2.astype(BF16)
        dx2b_ref[...] = dx2b
        dmix_ref[...] = lax.dot_general(dx2b, w, _NT, preferred_element_type=F32)

        @pl.when(i == 0)
        def _():
            gnf_ref[...] = jnp.zeros_like(gnf_ref)
            loss_ref[...] = jnp.zeros_like(loss_ref)

        gnf_ref[...] += gnf
        loss_ref[...] += jnp.broadcast_to(part, loss_ref.shape)

    return pl.pallas_call(
        body, name="out_proj_loss", grid=(SEQ // tm,),
        in_specs=[pl.BlockSpec((tm, D_MIX), lambda i: (i, 0)), pl.BlockSpec((tm, D_MODEL), lambda i: (i, 0)),
                  pl.BlockSpec((tm, D_MODEL), lambda i: (i, 0)), pl.BlockSpec(memory_space=pltpu.VMEM),
                  pl.BlockSpec((1, D_MODEL), lambda i: (0, 0))],
        out_specs=(pl.BlockSpec((tm, D_MODEL), lambda i: (i, 0)), pl.BlockSpec((tm, D_MODEL), lambda i: (i, 0)),
                   pl.BlockSpec((tm, D_MIX), lambda i: (i, 0)),
                   pl.BlockSpec((1, D_MODEL), lambda i: (0, 0)), pl.BlockSpec((8, 128), lambda i: (0, 0))),
        out_shape=(jax.ShapeDtypeStruct((SEQ, D_MODEL), F32), jax.ShapeDtypeStruct((SEQ, D_MODEL), BF16),
                   jax.ShapeDtypeStruct((SEQ, D_MIX), F32),
                   jax.ShapeDtypeStruct((1, D_MODEL), F32), jax.ShapeDtypeStruct((8, 128), F32)),
        compiler_params=_params(dimension_semantics=("arbitrary",)),
    )(mixed, x, target, w_out_full, norm_final)


def _gated_norm_bwd(a, gain, t, dy):
    r = lax.rsqrt(jnp.mean(a * a, axis=-1, keepdims=True) + RMS_EPS)
    an = a * r
    sg = _sigmoid(t)
    dn = dy * (t * sg)
    dt = dy * (an * gain) * (sg * (1.0 + t * (1.0 - sg)))
    u = dn * gain
    da = r * (u - an * jnp.mean(u * an, axis=-1, keepdims=True))
    return da, dt, dn * an


def _mix_bwd(proj, dmixed, conv_full, sinks, norm_conv, norm_attn):
    def body(pj_ref, kvp_ref, cch_ref, cuh_ref, dmx_ref, cw_ref, sink_ref, gc_ref, ga_ref,
             dpj_ref, gslab_ref, attn_scr, dattn_scr, p_scr, ps_scr, nxt_scr, dkv_scr, bias_scr, acc_scr, ostack_scr):
        step = pl.program_id(0)
        n = N_BLOCKS - 1 - step
        pj = pj_ref

        @pl.when(step == 0)
        def _():
            gslab_ref[...] = jnp.zeros_like(gslab_ref)
            nxt_scr[...] = jnp.zeros_like(nxt_scr)
            dkv_scr[...] = jnp.zeros_like(dkv_scr)
            acc_scr[...] = jnp.zeros_like(acc_scr)
            _fill_attn_bias(bias_scr, first_block=False)

        @pl.when(n == 0)
        def _():
            _fill_attn_bias(bias_scr, first_block=True)

        zhalo = _conv_halo(cch_ref, cuh_ref, n)
        cw = (cw_ref[0:1, :], cw_ref[1:2, :], cw_ref[2:3, :])
        gain_c = gc_ref[...]
        row = lax.broadcasted_iota(jnp.int32, (CHUNK, D_CONV), 0)

        def conv_chunk(t, dco_after):
            r = N_CHUNKS - 1 - t
            rows = _chunk_rows(r)
            cc, cu, z, z1, z2, co = _conv_chunk(pj_ref, zhalo, cw, r)
            cb = pj_ref[rows, OFF_CB:OFF_CB + D_CONV]
            da, dgate, gterm = _gated_norm_bwd(cb * co, gain_c, pj_ref[rows, OFF_GC:OFF_GC + D_CONV],
                                               dmx_ref[rows, 0:D_CONV])
            dpj_ref[rows, OFF_GC:OFF_GC + D_CONV] = dgate.astype(BF16)
            dpj_ref[rows, OFF_CB:OFF_CB + D_CONV] = (da * co).astype(BF16)
            dco = da * cb
            dco1 = jnp.where(row >= CHUNK - 1, pltpu.roll(dco_after, CHUNK - 1, 0), pltpu.roll(dco, CHUNK - 1, 0))
            dco2 = jnp.where(row >= CHUNK - 2, pltpu.roll(dco_after, CHUNK - 2, 0), pltpu.roll(dco, CHUNK - 2, 0))
            dz = cw[2] * dco + cw[1] * dco1 + cw[0] * dco2
            dpj_ref[rows, OFF_CC:OFF_CC + D_CONV] = (dz * cu).astype(BF16)
            dpj_ref[rows, OFF_CU:OFF_CU + D_CONV] = (dz * cc).astype(BF16)
            acc_scr[ACC_NORM_CONV] += gterm
            acc_scr[ACC_CONV0] += dco * z2
            acc_scr[ACC_CONV0 + 1] += dco * z1
            acc_scr[ACC_CONV0 + 2] += dco * z
            return dco

        nxt_scr[...] = lax.fori_loop(0, N_CHUNKS, conv_chunk, nxt_scr[...], unroll=True)

        ks, vs = _kv_bands(pj, kvp_ref)
        for j in range(2):
            out, probs, shares = _attn_group(pj, ks, vs, bias_scr, sink_ref, j)
            ostack_scr[j] = out
            for e in range(2):
                p_scr[2 * j + e] = probs[e]
                ps_scr[2 * j + e] = shares[e]
            for i in range(PAIRS_PER_KV):
                attn_scr[:, _pair_cols(j, i, 0)] = out[BLOCK * i:BLOCK * (i + 1), :]
        gain_a = ga_ref[...]

        def norm_chunk(r, carry):
            rows = _chunk_rows(r)
            da, dgate, gterm = _gated_norm_bwd(attn_scr[rows, :], gain_a, pj_ref[rows, OFF_GA:OFF_GA + D_ATTN],
                                               dmx_ref[rows, D_CONV:D_MIX])
            dpj_ref[rows, OFF_GA:OFF_GA + D_ATTN] = dgate.astype(BF16)
            dattn_scr[rows, :] = da
            acc_scr[ACC_NORM_ATTN] += gterm
            return carry

        lax.fori_loop(0, N_CHUNKS, norm_chunk, 0, unroll=True)

        lo = lax.broadcasted_iota(jnp.int32, (STACK, 128), 1) < HEAD_DIM
        lane_s = lax.broadcasted_iota(jnp.int32, (1, D_MODEL), 1)
        gsink = jnp.zeros((1, D_MODEL), F32)
        dk_t, dv_t = [], []
        for j in range(2):
            q_stack = _q_stack(pj, j)
            do_f = jnp.concatenate([dattn_scr[:, _pair_cols(j, i, 0)] for i in range(PAIRS_PER_KV)], axis=0)
            prod = do_f * ostack_scr[j]
            tot = jnp.sum(prod, axis=-1, keepdims=True)
            d_lo = jnp.sum(jnp.where(lo, prod, 0.0), axis=-1, keepdims=True)
            deltas = (d_lo, tot - d_lo)
            do_b = do_f.astype(BF16)
            dq, dk_j, dv_j = None, None, None
            for e in range(2):
                p = p_scr[2 * j + e]
                dp = lax.dot_general(do_b, vs[j][e], _NT, preferred_element_type=F32)
                ds = (p * (dp - deltas[e])).astype(BF16)
                gs = ps_scr[2 * j + e] * deltas[e]
                for i in range(PAIRS_PER_KV):
                    gs_h = -jnp.sum(gs[BLOCK * i:BLOCK * (i + 1), :], axis=0, keepdims=True)
                    gsink = gsink + jnp.where(lane_s == _head(j, i, e), gs_h, 0.0)
                t = jnp.dot(ds, ks[j][e], preferred_element_type=F32)
                dq = t if dq is None else dq + t
                half = slice(HEAD_DIM * e, HEAD_DIM * (e + 1))
                a = lax.dot_general(q_stack, ds, _TN, preferred_element_type=F32)[half, :]
                b = lax.dot_general(do_b, p.astype(BF16), _TN, preferred_element_type=F32)[half, :]
                dk_j = a if dk_j is None else dk_j + a
                dv_j = b if dv_j is None else dv_j + b
            for i in range(PAIRS_PER_KV):
                dpj_ref[:, _pair_cols(j, i, OFF_Q)] = (dq[BLOCK * i:BLOCK * (i + 1), :] * SCALE).astype(BF16)
            dk_t.append(dk_j)
            dv_t.append(dv_j)
        dk = jnp.concatenate(dk_t, axis=0).T
        dv = jnp.concatenate(dv_t, axis=0).T
        dpj_ref[:, OFF_K:OFF_K + D_KV] = (dk[BLOCK:, :] + dkv_scr[:, 0:D_KV]).astype(BF16)
        dpj_ref[:, OFF_V:OFF_V + D_KV] = (dv[BLOCK:, :] + dkv_scr[:, D_KV:2 * D_KV]).astype(BF16)
        dkv_scr[:, 0:D_KV] = dk[:BLOCK, :]
        dkv_scr[:, D_KV:2 * D_KV] = dv[:BLOCK, :]
        gslab_ref[ROW_SINKS:ROW_SINKS + 1, :] += gsink

        @pl.when(step == N_BLOCKS - 1)
        def _():
            for k, slab_row in ((ACC_NORM_CONV, ROW_NORM_CONV), (ACC_NORM_ATTN, ROW_NORM_ATTN), (ACC_CONV0, ROW_CONV0),
                                (ACC_CONV0 + 1, ROW_CONV0 + 1), (ACC_CONV0 + 2, ROW_CONV0 + 2)):
                gslab_ref[slab_row:slab_row + 1, :] = jnp.sum(acc_scr[k], axis=0, keepdims=True)

    per_block = BLOCK // HALO
    last = N_BLOCKS - 1
    return pl.pallas_call(
        body, name="mix_bwd", grid=(N_BLOCKS,),
        in_specs=[
            pl.BlockSpec((BLOCK, D_PROJ), lambda s: (last - s, 0)),
            pl.BlockSpec((BLOCK, 2 * D_KV), lambda s: (jnp.maximum(last - s - 1, 0), OFF_K // (2 * D_KV))),
            pl.BlockSpec((HALO, D_CONV), lambda s: (jnp.maximum((last - s) * per_block - 1, 0), OFF_CC // D_CONV)),
            pl.BlockSpec((HALO, D_CONV), lambda s: (jnp.maximum((last - s) * per_block - 1, 0), OFF_CU // D_CONV)),
            pl.BlockSpec((BLOCK, D_MIX), lambda s: (last - s, 0)),
            pl.BlockSpec((8, D_CONV), lambda s: (0, 0)),
            pl.BlockSpec(memory_space=pltpu.SMEM),
            pl.BlockSpec((1, D_CONV), lambda s: (0, 0)),
            pl.BlockSpec((1, D_ATTN), lambda s: (0, 0)),
        ],
        out_specs=(pl.BlockSpec((BLOCK, D_PROJ), lambda s: (last - s, 0)),
                   pl.BlockSpec((8, D_MODEL), lambda s: (0, 0))),
        out_shape=(jax.ShapeDtypeStruct((SEQ, D_PROJ), BF16), jax.ShapeDtypeStruct((8, D_MODEL), F32)),
        scratch_shapes=[pltpu.VMEM((BLOCK, D_ATTN), F32), pltpu.VMEM((BLOCK, D_ATTN), F32),
                        pltpu.VMEM((4, STACK, 2 * BLOCK), F32), pltpu.VMEM((4, STACK, 1), F32),
                        pltpu.VMEM((CHUNK, D_CONV), F32), pltpu.VMEM((BLOCK, 2 * D_KV), F32),
                        pltpu.VMEM((4, STACK, 2 * BLOCK), F32), pltpu.VMEM((N_ACC, CHUNK, D_MODEL), F32),
                        pltpu.VMEM((2, STACK, 128), F32)],
        compiler_params=_params(dimension_semantics=("arbitrary",)),
    )(proj, proj, proj, proj, dmixed, conv_full, sinks, norm_conv, norm_attn)


def _in_bwd_rs(dproj, w_full, x, dx2, norm_in, dw_in_chip, gslab, gnf, loss_part):
    tm = 256
    steps = SEQ // tm

    def body(dp_ref, w_ref, x_ref, dx2_ref, g_ref, dwi_ref, gs_ref, gnf_ref, lp_ref, gx_ref, gwin_ref, gsum_ref,
             gni_scr, own, ici, myslab, slabs, send_sems, recv_sems, local_sem):
        i = pl.program_id(0)
        rs_start, rs_finish = _ici_sum(dwi_ref, own, ici, send_sems, recv_sems, local_sem)
        slab_start, slab_finish = _slab_sum(myslab, slabs, send_sems, recv_sems, 3)

        @pl.when(i == 0)
        def _():
            gni_scr[...] = jnp.zeros_like(gni_scr)
            rs_start()

        dh = jnp.dot(dp_ref[...], w_ref[...], preferred_element_type=F32)
        xv = x_ref[...]
        r = lax.rsqrt(jnp.mean(xv * xv, axis=-1, keepdims=True) + RMS_EPS)
        xn = xv * r
        u = dh * g_ref[...]
        gx_ref[...] = dx2_ref[...] + r * (u - xn * jnp.mean(u * xn, axis=-1, keepdims=True))
        gni_scr[...] += jnp.sum(dh * xn, axis=0, keepdims=True)

        @pl.when(i == steps - 1)
        def _():
            row = lax.broadcasted_iota(jnp.int32, (8, D_MODEL), 0)
            lane = lax.broadcasted_iota(jnp.int32, (8, D_MODEL), 1)
            slab = jnp.where(row == ROW_NORM_IN, gni_scr[...], jnp.where(row == ROW_NORM_FINAL, gnf_ref[...], gs_ref[...]))
            myslab[...] = jnp.where((row == ROW_SINKS) & (lane == LOSS_LANE), lp_ref[0:1, 0:1], slab)
            slab_start()
            gwin_ref[...] = rs_finish()
            gsum_ref[...] = slab_finish()

    const = lambda i: (0, 0)
    return pl.pallas_call(
        body, name="in_bwd", grid=(steps,),
        in_specs=[pl.BlockSpec((tm, D_PROJ), lambda i: (i, 0)), pl.BlockSpec(memory_space=pltpu.VMEM),
                  pl.BlockSpec((tm, D_MODEL), lambda i: (i, 0)), pl.BlockSpec((tm, D_MODEL), lambda i: (i, 0)),
                  pl.BlockSpec((1, D_MODEL), const), pl.BlockSpec(memory_space=pl.ANY),
                  pl.BlockSpec((8, D_MODEL), const), pl.BlockSpec((1, D_MODEL), const), pl.BlockSpec((8, 128), const)],
        out_specs=(pl.BlockSpec((tm, D_MODEL), lambda i: (i, 0)), pl.BlockSpec((SHARD_IN, D_MODEL), const),
                   pl.BlockSpec((8, D_MODEL), const)),
        out_shape=(jax.ShapeDtypeStruct((SEQ, D_MODEL), F32), jax.ShapeDtypeStruct((SHARD_IN, D_MODEL), F32),
                   jax.ShapeDtypeStruct((8, D_MODEL), F32)),
        scratch_shapes=[pltpu.VMEM((1, D_MODEL), F32), pltpu.VMEM((SHARD_IN, D_MODEL), BF16),
                        pltpu.VMEM((3, SHARD_IN, D_MODEL), BF16),
                        pltpu.VMEM((8, D_MODEL), F32), pltpu.VMEM((N_DEV, 8, D_MODEL), F32),
                        pltpu.SemaphoreType.DMA((3 + 7,)), pltpu.SemaphoreType.DMA((3 + 7,)), pltpu.SemaphoreType.DMA],
        compiler_params=_params(dimension_semantics=("arbitrary",)),
    )(dproj, w_full, x, dx2, norm_in, dw_in_chip, gslab, gnf, loss_part)


def _dw_in_rs(dproj, h, dw_out_sh):
    tn = 640
    steps = D_PROJ // tn
    forward_step = 2
    half_step = (D_PROJ // 2) // tn

    def body(a_ref, b_ref, dwo_ref, chip_ref, gwo_ref, dwt, d2d_in, own, d2d, ici, send_sems, recv_sems, local_sems):
        i = pl.program_id(0)
        rs_start, rs_forward, rs_finish = _shard_sum(dwo_ref, own, d2d, ici, send_sems, recv_sems, local_sems)
        pair_send, pair_finish = _chip_sum(dwt, d2d_in, chip_ref, send_sems, recv_sems, local_sems,
                                           N_SHARD_SUM_SEMS, 4)

        @pl.when(i == 0)
        def _():
            rs_start()

        @pl.when(i == half_step)
        def _():
            pair_send(first=True)

        tile = lax.dot_general(a_ref[...], b_ref[...], _TN, preferred_element_type=F32).astype(BF16)
        dwt[pl.ds(pl.multiple_of(i * tn, tn), tn), :] = tile

        @pl.when(i == forward_step)
        def _():
            rs_forward()

        @pl.when(i == steps - 1)
        def _():
            pair_send(first=False)
            gwo_ref[...] = rs_finish()
            pair_finish()

    return pl.pallas_call(
        body, name="dw_in", grid=(steps,),
        in_specs=[pl.BlockSpec((SEQ, tn), lambda i: (0, i)), pl.BlockSpec(memory_space=pltpu.VMEM),
                  pl.BlockSpec(memory_space=pl.ANY)],
        out_specs=(pl.BlockSpec(memory_space=pl.ANY), pl.BlockSpec((SHARD_OUT, D_MODEL), lambda i: (0, 0))),
        out_shape=(jax.ShapeDtypeStruct((4, SHARD_IN, D_MODEL), BF16), jax.ShapeDtypeStruct((SHARD_OUT, D_MODEL), F32)),
        scratch_shapes=[pltpu.VMEM((D_PROJ, D_MODEL), BF16), pltpu.VMEM((4, SHARD_IN, D_MODEL), BF16),
                        *_shard_sum_scratch(SHARD_OUT),
                        pltpu.SemaphoreType.DMA((N_SHARD_SUM_SEMS + 4,)), pltpu.SemaphoreType.DMA((N_SHARD_SUM_SEMS + 4,)),
                        pltpu.SemaphoreType.DMA((8,))],
        compiler_params=_params(dimension_semantics=("arbitrary",)),
    )(dproj, h, dw_out_sh)


def _matmul_tn(a, b, tn, name):
    k, n = a.shape
    _, m = b.shape

    def body(a_ref, b_ref, o_ref):
        o_ref[...] = lax.dot_general(a_ref[...], b_ref[...], _TN, preferred_element_type=F32).astype(BF16)

    return pl.pallas_call(
        body, name=name, grid=(n // tn,),
        in_specs=[pl.BlockSpec((k, tn), lambda i: (0, i)), pl.BlockSpec(memory_space=pltpu.VMEM)],
        out_specs=pl.BlockSpec((tn, m), lambda i: (i, 0)),
        out_shape=jax.ShapeDtypeStruct((n, m), BF16),
        compiler_params=_params(dimension_semantics=("arbitrary",)),
    )(a, b)


def _adam_all(big_in, big_out, gsum, small):
    steps = 4
    tr_in, tr_out = SHARD_IN // steps, SHARD_OUT // steps

    def body(*refs):
        ins, outs = refs[:8 + 1 + 18], refs[8 + 1 + 18:]
        i = pl.program_id(0)
        for b in range(2):
            w_ref, g_ref, m_ref, v_ref = ins[4 * b:4 * b + 4]
            g = g_ref[...]
            delta, mn, vn = _adamw(w_ref[...], g, m_ref[...], v_ref[...])
            for ref, val in zip(outs[4 * b:4 * b + 4], (g, delta, mn, vn)):
                ref[...] = val

        @pl.when(i == 0)
        def _():
            gsum = ins[8][...]
            idx = _slot(lax.axis_index("x"), lax.axis_index("y"), lax.axis_index("c"))
            cg = jnp.zeros((3, SHARD_CONV), F32)
            for d in range(N_DEV):
                cg = jnp.where(idx == d, gsum[ROW_CONV0:ROW_CONV0 + 3, d * SHARD_CONV:(d + 1) * SHARD_CONV], cg)
            grads = (gsum[ROW_NORM_IN:ROW_NORM_IN + 1], gsum[ROW_SINKS:ROW_SINKS + 1, 0:N_Q_HEADS],
                     gsum[ROW_NORM_CONV:ROW_NORM_CONV + 1], gsum[ROW_NORM_ATTN:ROW_NORM_ATTN + 1],
                     gsum[ROW_NORM_FINAL:ROW_NORM_FINAL + 1], cg)
            for s, g in enumerate(grads):
                w_ref, m_ref, v_ref = ins[9 + 3 * s:12 + 3 * s]
                delta, mn, vn = _adamw(w_ref[...], g, m_ref[...], v_ref[...])
                for ref, val in zip(outs[8 + 4 * s:12 + 4 * s], (g, delta, mn, vn)):
                    ref[...] = val
            outs[32][...] = gsum[ROW_SINKS:ROW_SINKS + 1, LOSS_LANE:LOSS_LANE + 1]

    const = lambda i: (0, 0)
    rows = lambda i: (i, 0)
    small_shapes = [a.shape for a in small[::3]]
    in_specs = ([pl.BlockSpec((tr_in, D_MODEL), rows)] * 4 + [pl.BlockSpec((tr_out, D_MODEL), rows)] * 4
                + [pl.BlockSpec((8, D_MODEL), const)] + [pl.BlockSpec(a.shape, const) for a in small])
    out_specs = ([pl.BlockSpec((tr_in, D_MODEL), rows)] * 4 + [pl.BlockSpec((tr_out, D_MODEL), rows)] * 4
                 + [pl.BlockSpec(s, const) for s in small_shapes for _ in range(4)] + [pl.BlockSpec((1, 1), const)])
    out_shape = ([jax.ShapeDtypeStruct((SHARD_IN, D_MODEL), F32)] * 4 + [jax.ShapeDtypeStruct((SHARD_OUT, D_MODEL), F32)] * 4
                 + [jax.ShapeDtypeStruct(s, F32) for s in small_shapes for _ in range(4)]
                 + [jax.ShapeDtypeStruct((1, 1), F32)])
    outs = pl.pallas_call(
        body, name="adam", grid=(steps,), in_specs=in_specs, out_specs=tuple(out_specs), out_shape=tuple(out_shape),
        compiler_params=_params(dimension_semantics=("arbitrary",)),
    )(*big_in, *big_out, gsum, *small)
    return outs[0:4], outs[4:8], [outs[8 + 4 * s:12 + 4 * s] for s in range(6)], outs[32]


def _pad_rows(a, rows=8):
    return jnp.pad(a, ((0, rows - a.shape[0]), (0, 0)))


def kernel(x, norm_in, w_in, conv_w, attn_sinks, norm_conv_out, norm_attn_out, w_out, norm_final, loss_target, m_norm_in, m_w_in, m_conv_w, m_attn_sinks, m_norm_conv_out, m_norm_attn_out, m_w_out, m_norm_final, v_norm_in, v_w_in, v_conv_w, v_attn_sinks, v_norm_conv_out, v_norm_attn_out, v_w_out, v_norm_final):
    x2d = x.reshape(SEQ, D_MODEL)
    target = loss_target.reshape(SEQ, D_MODEL)
    nf = norm_final.reshape(1, D_MODEL)

    w_in_t, m_w_in_t, v_w_in_t = w_in[0].T, m_w_in[0].T, v_w_in[0].T
    g_in, w_out_b, conv_full = _all_gather(w_in_t, w_out[0], _pad_rows(conv_w[0]))
    w_in_full = g_in.reshape(D_PROJ, D_MODEL)
    sinks = attn_sinks.reshape(N_Q_HEADS)

    h, proj, g_out = _in_proj(x2d, norm_in, w_in_full, w_out_b)
    w_out_full = g_out.reshape(D_MIX, D_MODEL)
    mixed = _mix_fwd(proj, conv_full, sinks, norm_conv_out, norm_attn_out)
    dx2, dx2b, dmixed, gnf, loss_part = _out_proj_loss(mixed, x2d, target, w_out_full, nf)
    dproj, gslab = _mix_bwd(proj, dmixed, conv_full, sinks, norm_conv_out, norm_attn_out)
    dw_out = _matmul_tn(mixed, dx2b, 512, "dw_out")
    dw_in_chip, g_w_out = _dw_in_rs(dproj, h, dw_out.reshape(N_DEV, SHARD_OUT, D_MODEL))
    grad_x, g_w_in, gsum = _in_bwd_rs(dproj, w_in_full, x2d, dx2, norm_in, dw_in_chip, gslab, gnf, loss_part)

    small = (norm_in, m_norm_in, v_norm_in, attn_sinks, m_attn_sinks, v_attn_sinks,
             norm_conv_out, m_norm_conv_out, v_norm_conv_out, norm_attn_out, m_norm_attn_out, v_norm_attn_out,
             nf, m_norm_final.reshape(1, D_MODEL), v_norm_final.reshape(1, D_MODEL),
             conv_w[0], m_conv_w[0], v_conv_w[0])
    big_in, big_out, (s_ni, s_sk, s_nc, s_na, s_nf, s_cv), loss = _adam_all(
        (w_in_t, g_w_in, m_w_in_t, v_w_in_t), (w_out[0], g_w_out, m_w_out[0], v_w_out[0]), gsum, small)

    def leaves(k):
        return (s_ni[k], big_in[k].T[None], s_cv[k][None], s_sk[k], s_nc[k], s_na[k], big_out[k][None],
                s_nf[k].reshape(D_MODEL))

    return (loss.reshape(()), grad_x.reshape(1, SEQ, D_MODEL), *leaves(0), *leaves(1), *leaves(2), *leaves(3))
```

```python
import functools
import math

import jax
import jax.numpy as jnp
from jax import lax
from jax.experimental import pallas as pl
from jax.experimental.pallas import tpu as pltpu

F32 = jnp.float32
BF16 = jnp.bfloat16
MESH = pl.DeviceIdType.MESH

N_DEV = 8
SEQ = 2048
D_MODEL = 1024
D_CONV = 1024
D_ATTN = 1024
D_KV = 128
HEAD_DIM = 64
N_Q_HEADS = 16
N_PAIRS = N_Q_HEADS // 2
PAIRS_PER_KV = N_PAIRS // 2
D_MIX = D_CONV + D_ATTN
D_PROJ = 6400
SHARD_IN = D_PROJ // N_DEV
SHARD_OUT = D_MIX // N_DEV
SHARD_CONV = D_CONV // N_DEV
OFF_CB, OFF_CC, OFF_CU, OFF_GC, OFF_Q, OFF_K, OFF_V, OFF_GA = 0, 1024, 2048, 3072, 4096, 5120, 5248, 5376
BLOCK = 128
N_BLOCKS = SEQ // BLOCK
HALO = 8
CHUNK = 16
N_CHUNKS = BLOCK // CHUNK
RMS_EPS = 1e-5
NEG = -1e30
SCALE = HEAD_DIM ** -0.5
SLOPES = tuple(2.0 ** (-8.0 * (h + 1) / N_Q_HEADS) for h in range(N_Q_HEADS))

ADAM_LR = 0.001
ADAM_B1 = 0.9
ADAM_B2 = 0.999
ADAM_EPS = 1e-08
ADAM_WD = 0.01
ADAM_STEP = 10

ROW_NORM_IN, ROW_NORM_CONV, ROW_NORM_ATTN, ROW_NORM_FINAL, ROW_CONV0, ROW_SINKS = 0, 1, 2, 3, 4, 7
LOSS_LANE = N_Q_HEADS
ACC_NORM_CONV, ACC_NORM_ATTN, ACC_CONV0, N_ACC = 0, 1, 2, 5

VMEM_LIMIT = 56 * 1024 * 1024

_NT = (((1,), (1,)), ((), ()))
_TN = (((0,), (0,)), ((), ()))


def _params(**kw):
    return pltpu.CompilerParams(vmem_limit_bytes=VMEM_LIMIT, **kw)


def _adamw(w, g, m, v):
    m = ADAM_B1 * m + (1.0 - ADAM_B1) * g
    v = ADAM_B2 * v + (1.0 - ADAM_B2) * (g * g)
    m_hat = m / (1.0 - ADAM_B1 ** ADAM_STEP)
    v_hat = v / (1.0 - ADAM_B2 ** ADAM_STEP)
    delta = -ADAM_LR * (m_hat / (jnp.sqrt(v_hat) + ADAM_EPS) + ADAM_WD * w)
    return delta, m, v


def _sigmoid(t):
    return 1.0 / (1.0 + jnp.exp(-t))


def _slot(px, py, pc):
    return 4 * px + 2 * py + pc


HALF_IN = SHARD_IN // 2
N_GATHER_KINDS = 13


def _all_gather(w_in_sh, w_out_sh, conv_sh):
    def body(win_ref, wout_ref, cv_ref, gin_ref, woutb_ref, conv_ref, gcv_ref, send_sems, recv_sems):
        x, y, c = lax.axis_index("x"), lax.axis_index("y"), lax.axis_index("c")
        me, sibling = (x, y, c), (x, y, 1 - c)
        nx, ny, dg = (1 - x, y, c), (x, 1 - y, c), (1 - x, 1 - y, c)

        def other(dev):
            return (dev[0], dev[1], 1 - dev[2])

        gin_ref[_slot(*me)] = win_ref[...].astype(BF16)
        gcv_ref[_slot(*me)] = cv_ref[...]

        def half(dev, h):
            return gin_ref.at[_slot(*dev), pl.ds(h * HALF_IN, HALF_IN), :]

        def rc(ref, k, to):
            return pltpu.make_async_remote_copy(src_ref=ref, dst_ref=ref, send_sem=send_sems.at[k],
                                                recv_sem=recv_sems.at[k], device_id=to, device_id_type=MESH)

        own = [rc(gin_ref.at[_slot(*me)], 0, sibling),
               rc(half(me, 0), 1, nx), rc(half(me, 1), 2, nx),
               rc(half(me, 1), 4, ny), rc(half(me, 0), 3, ny)]
        for cp in own:
            cp.start()
        def cv(k, dev, to):
            s = _slot(*dev)
            return pltpu.make_async_remote_copy(src_ref=gcv_ref.at[s], dst_ref=gcv_ref.at[s],
                                                send_sem=send_sems.at[N_GATHER_KINDS + k],
                                                recv_sem=recv_sems.at[N_GATHER_KINDS + k], device_id=to, device_id_type=MESH)

        small = [cv(0, me, sibling)] + [cv(1 + j, me, peer) for j, peer in enumerate((nx, ny, dg))]
        for cp in small:
            cp.start()
        woutb_ref[...] = wout_ref[...].astype(BF16)

        relayed = []
        for (dev, h), k_in, k_ici, k_d2d in (((nx, 0), 1, 5, 7), ((ny, 1), 4, 6, 10), ((nx, 1), 2, None, 8),
                                             ((ny, 0), 3, None, 9), ((dg, 0), 5, None, 11), ((dg, 1), 6, None, 12)):
            rc(half(dev, h), k_in, me).wait_recv()
            if k_ici is not None:
                relayed.append(rc(half(dev, h), k_ici, ny if dev is nx else nx))
                relayed[-1].start()
            relayed.append(rc(half(dev, h), k_d2d, sibling))
            relayed[-1].start()
        for j, peer in enumerate((nx, ny, dg)):
            cv(1 + j, peer, me).wait_recv()
            relayed.append(cv(4 + j, peer, sibling))
            relayed[-1].start()
        rc(gin_ref.at[_slot(*sibling)], 0, me).wait_recv()
        for (dev, h), k in (((nx, 0), 7), ((nx, 1), 8), ((ny, 0), 9), ((ny, 1), 10), ((dg, 0), 11), ((dg, 1), 12)):
            rc(half(other(dev), h), k, me).wait_recv()
        cv(0, sibling, me).wait_recv()
        for j, peer in enumerate((nx, ny, dg)):
            cv(4 + j, other(peer), me).wait_recv()
        for d in range(N_DEV):
            conv_ref[:, d * SHARD_CONV:(d + 1) * SHARD_CONV] = gcv_ref[d]
        for cp in own + small + relayed:
            cp.wait_send()

    vmem = pl.BlockSpec(memory_space=pltpu.VMEM)
    return pl.pallas_call(
        body, name="all_gather",
        out_shape=(jax.ShapeDtypeStruct((N_DEV, SHARD_IN, D_MODEL), BF16),
                   jax.ShapeDtypeStruct((SHARD_OUT, D_MODEL), BF16),
                   jax.ShapeDtypeStruct((8, D_CONV), F32)),
        in_specs=[vmem, vmem, vmem], out_specs=(vmem, vmem, vmem),
        scratch_shapes=[pltpu.VMEM((N_DEV, 8, SHARD_CONV), F32),
                        pltpu.SemaphoreType.DMA((N_GATHER_KINDS + 7,)), pltpu.SemaphoreType.DMA((N_GATHER_KINDS + 7,))],
        compiler_params=_params(),
    )(w_in_sh, w_out_sh, conv_sh)


def _shard_sum(src, own, d2d, ici, send_sems, recv_sems, local_sems, base=0):
    x, y, c = lax.axis_index("x"), lax.axis_index("y"), lax.axis_index("c")
    sibling = (x, y, 1 - c)
    chips = [(x, y), (1 - x, y), (x, 1 - y), (1 - x, 1 - y)]

    def rcopy(s, d, k, to):
        return pltpu.make_async_remote_copy(src_ref=s, dst_ref=d, send_sem=send_sems.at[base + k],
                                            recv_sem=recv_sems.at[base + k], device_id=to, device_id_type=MESH)

    def mine(k):
        return pltpu.make_async_copy(src.at[_slot(*chips[k], c)], own.at[k], local_sems.at[k])

    def to_sibling(k):
        return rcopy(src.at[_slot(*chips[k], 1 - c)], d2d.at[k], k, sibling)

    def to_chip(k):
        return rcopy(own.at[k], ici.at[k - 1], 3 + k, (*chips[k], c))

    def start():
        for k in range(4):
            mine(k).start()
            to_sibling(k).start()

    def forward():
        for k in range(1, 4):
            mine(k).wait()
            to_sibling(k).wait_recv()
            own[k] = (own[k].astype(F32) + d2d[k].astype(F32)).astype(BF16)
            to_chip(k).start()

    def finish():
        mine(0).wait()
        to_sibling(0).wait_recv()
        acc = own[0].astype(F32) + d2d[0].astype(F32)
        for k in range(1, 4):
            to_chip(k).wait_recv()
            acc = acc + ici[k - 1].astype(F32)
        for k in range(4):
            to_sibling(k).wait_send()
        for k in range(1, 4):
            to_chip(k).wait_send()
        return acc

    return start, forward, finish


def _shard_sum_scratch(rows):
    return [pltpu.VMEM((4, rows, D_MODEL), BF16), pltpu.VMEM((4, rows, D_MODEL), BF16),
            pltpu.VMEM((3, rows, D_MODEL), BF16)]


N_SHARD_SUM_SEMS = 7


def _chip_sum(dwt, d2d, out_hbm, send_sems, recv_sems, local_sems, base, local_base):
    x, y, c = lax.axis_index("x"), lax.axis_index("y"), lax.axis_index("c")
    sibling = (x, y, 1 - c)
    chips = [(x, y), (1 - x, y), (x, 1 - y), (1 - x, 1 - y)]

    def shard(s):
        return dwt.at[pl.ds(pl.multiple_of(s * SHARD_IN, 16), SHARD_IN), :]

    def to_sibling(k):
        return pltpu.make_async_remote_copy(src_ref=shard(_slot(*chips[k], 1 - c)), dst_ref=d2d.at[k],
                                            send_sem=send_sems.at[base + k], recv_sem=recv_sems.at[base + k],
                                            device_id=sibling, device_id_type=MESH)

    def save(k):
        return pltpu.make_async_copy(d2d.at[k], out_hbm.at[k], local_sems.at[local_base + k])

    def send(first):
        for k in range(4):
            in_first = _slot(*chips[k], 1 - c) < N_DEV // 2

            @pl.when(in_first if first else jnp.logical_not(in_first))
            def _():
                to_sibling(k).start()

    def finish():
        for k in range(4):
            to_sibling(k).wait_recv()
            d2d[k] = (shard(_slot(*chips[k], c))[...].astype(F32) + d2d[k].astype(F32)).astype(BF16)
            save(k).start()
        for k in range(4):
            save(k).wait()
            to_sibling(k).wait_send()

    return send, finish


N_ICI_SUM_SEMS = 6


def _ici_sum(src, own, ici, via, stage, send_sems, recv_sems, local_sems, base=0):
    x, y, c = lax.axis_index("x"), lax.axis_index("y"), lax.axis_index("c")
    nx, ny = (1 - x, y, c), (x, 1 - y, c)
    OWN, NX, NY, DG = range(4)

    def half(ref, h):
        return ref.at[pl.ds(h * HALF_IN, HALF_IN), :]

    def rc(s, d, k, to):
        return pltpu.make_async_remote_copy(src_ref=s, dst_ref=d, send_sem=send_sems.at[base + k],
                                            recv_sem=recv_sems.at[base + k], device_id=to, device_id_type=MESH)

    for_dg_0 = lambda: rc(half(src.at[DG], 0), via.at[0], 0, nx)
    for_dg_1 = lambda: rc(half(src.at[DG], 1), via.at[1], 1, ny)
    for_nx_0 = lambda: rc(half(src.at[NX], 0), half(ici.at[0], 0), 2, nx)
    for_ny_1 = lambda: rc(half(src.at[NY], 1), half(ici.at[1], 1), 3, ny)
    for_ny_0 = lambda: rc(stage.at[0], half(ici.at[1], 0), 4, ny)
    for_nx_1 = lambda: rc(stage.at[1], half(ici.at[0], 1), 5, nx)
    mine = lambda: pltpu.make_async_copy(src.at[OWN], own, local_sems.at[0])
    stage_0 = lambda: pltpu.make_async_copy(half(src.at[NY], 0), stage.at[0], local_sems.at[1])
    stage_1 = lambda: pltpu.make_async_copy(half(src.at[NX], 1), stage.at[1], local_sems.at[2])

    def start():
        for cp in (for_dg_0, for_dg_1, for_nx_0, for_ny_1, stage_0, stage_1, mine):
            cp().start()

    def relay():
        for h, staged, landed, out in ((0, stage_0, for_dg_0, for_ny_0), (1, stage_1, for_dg_1, for_nx_1)):
            staged().wait()
            landed().wait_recv()
            stage[h] = (stage[h].astype(F32) + via[h].astype(F32)).astype(BF16)
            out().start()

    def finish():
        mine().wait()
        for cp in (for_nx_0, for_nx_1, for_ny_1, for_ny_0):
            cp().wait_recv()
        acc = own[...].astype(F32) + ici[0].astype(F32) + ici[1].astype(F32)
        for cp in (for_dg_0, for_dg_1, for_nx_0, for_ny_1, for_ny_0, for_nx_1):
            cp().wait_send()
        return acc

    return start, relay, finish


def _slab_sum(myslab, slabs, send_sems, recv_sems, base):
    x, y, c = lax.axis_index("x"), lax.axis_index("y"), lax.axis_index("c")
    me = _slot(x, y, c)
    peers = [(x, y, 1 - c), (1 - x, y, c), (x, 1 - y, c), (1 - x, 1 - y, c),
             (1 - x, y, 1 - c), (x, 1 - y, 1 - c), (1 - x, 1 - y, 1 - c)]

    def cp(k):
        return pltpu.make_async_remote_copy(src_ref=myslab, dst_ref=slabs.at[me], send_sem=send_sems.at[base + k],
                                            recv_sem=recv_sems.at[base + k], device_id=peers[k], device_id_type=MESH)

    def start():
        slabs[me] = myslab[...]
        for k in range(7):
            cp(k).start()

    def finish():
        for k in range(7):
            cp(k).wait_recv()
        total = slabs[0]
        for d in range(1, N_DEV):
            total = total + slabs[d]
        for k in range(7):
            cp(k).wait_send()
        return total

    return start, finish


def _in_proj(x, norm_in, w_full, w_out_b):
    tm = 256
    steps = SEQ // tm
    forward_step = 5

    def body(x_ref, g_ref, w_ref, wo_ref, h_ref, proj_ref, gout_ref, send_sems, recv_sems, local_sem):
        i = pl.program_id(0)
        x, y, c = lax.axis_index("x"), lax.axis_index("y"), lax.axis_index("c")
        me, sibling = (x, y, c), (x, y, 1 - c)
        chips = [(1 - x, y), (x, 1 - y), (1 - x, 1 - y)]

        def copy(k, block, to, src=None):
            rows = gout_ref.at[_slot(*block)]
            return pltpu.make_async_remote_copy(src_ref=rows if src is None else src, dst_ref=rows,
                                                send_sem=send_sems.at[k], recv_sem=recv_sems.at[k],
                                                device_id=to, device_id_type=MESH)

        mine = pltpu.make_async_copy(wo_ref, gout_ref.at[_slot(*me)], local_sem)

        @pl.when(i == 0)
        def _():
            mine.start()
            copy(0, me, sibling, src=wo_ref).start()
            for j, chip in enumerate(chips):
                copy(1 + j, me, (*chip, c), src=wo_ref).start()

        xv = x_ref[...]
        r = lax.rsqrt(jnp.mean(xv * xv, axis=-1, keepdims=True) + RMS_EPS)
        h = (xv * r * g_ref[...]).astype(BF16)
        h_ref[...] = h
        proj_ref[...] = lax.dot_general(h, w_ref[...], _NT, preferred_element_type=F32)

        @pl.when(i == forward_step)
        def _():
            for j, chip in enumerate(chips):
                copy(1 + j, (*chip, c), me).wait_recv()
                copy(4 + j, (*chip, c), sibling).start()

        @pl.when(i == steps - 1)
        def _():
            copy(0, sibling, me).wait_recv()
            for j, chip in enumerate(chips):
                copy(4 + j, (*chip, 1 - c), me).wait_recv()
            copy(0, me, sibling, src=wo_ref).wait_send()
            for j, chip in enumerate(chips):
                copy(1 + j, me, (*chip, c), src=wo_ref).wait_send()
                copy(4 + j, (*chip, c), sibling).wait_send()
            mine.wait()

    return pl.pallas_call(
        body, name="in_proj", grid=(steps,),
        in_specs=[pl.BlockSpec((tm, D_MODEL), lambda i: (i, 0)), pl.BlockSpec((1, D_MODEL), lambda i: (0, 0)),
                  pl.BlockSpec(memory_space=pltpu.VMEM), pl.BlockSpec(memory_space=pl.ANY)],
        out_specs=(pl.BlockSpec((tm, D_MODEL), lambda i: (i, 0)), pl.BlockSpec((tm, D_PROJ), lambda i: (i, 0)),
                   pl.BlockSpec(memory_space=pl.ANY)),
        out_shape=(jax.ShapeDtypeStruct((SEQ, D_MODEL), BF16), jax.ShapeDtypeStruct((SEQ, D_PROJ), F32),
                   jax.ShapeDtypeStruct((N_DEV, SHARD_OUT, D_MODEL), BF16)),
        scratch_shapes=[pltpu.SemaphoreType.DMA((7,)), pltpu.SemaphoreType.DMA((7,)), pltpu.SemaphoreType.DMA],
        compiler_params=_params(dimension_semantics=("arbitrary",)),
    )(x, norm_in, w_full, w_out_b)


def _chunk_rows(r):
    return pl.ds(pl.multiple_of(r * CHUNK, CHUNK), CHUNK)


def _conv_halo(cch_ref, cuh_ref, n):
    zh = jnp.where(n > 0, cch_ref[...] * cuh_ref[...], 0.0)
    return jnp.concatenate([zh] * (CHUNK // HALO), axis=0)


def _conv_chunk(pj_ref, zhalo, cw, r):
    rows = _chunk_rows(r)
    cc = pj_ref[rows, OFF_CC:OFF_CC + D_CONV]
    cu = pj_ref[rows, OFF_CU:OFF_CU + D_CONV]
    z = cc * cu
    before = _chunk_rows(jnp.maximum(r - 1, 0))
    zprev = jnp.where(r > 0, pj_ref[before, OFF_CC:OFF_CC + D_CONV] * pj_ref[before, OFF_CU:OFF_CU + D_CONV], zhalo)
    row = lax.broadcasted_iota(jnp.int32, (CHUNK, D_CONV), 0)
    z1 = jnp.where(row < 1, pltpu.roll(zprev, 1, 0), pltpu.roll(z, 1, 0))
    z2 = jnp.where(row < 2, pltpu.roll(zprev, 2, 0), pltpu.roll(z, 2, 0))
    co = cw[0] * z2 + cw[1] * z1 + cw[2] * z
    return cc, cu, z, z1, z2, co


def _gated_norm(a, gain, t):
    r = lax.rsqrt(jnp.mean(a * a, axis=-1, keepdims=True) + RMS_EPS)
    return a * r * gain * (t * _sigmoid(t))


def _kv_bands(pj, kvp_ref):
    lane = lax.broadcasted_iota(jnp.int32, (2 * BLOCK, D_KV), 1)
    lo = lane < HEAD_DIM

    def bands(prev, cur):
        b = jnp.concatenate([prev, cur], axis=0)
        br = pltpu.roll(b, HEAD_DIM, 1)
        zero = jnp.zeros_like(b)
        return ((jnp.where(lo, b, zero).astype(BF16), jnp.where(lo, zero, br).astype(BF16)),
                (jnp.where(lo, br, zero).astype(BF16), jnp.where(lo, zero, b).astype(BF16)))

    ks = bands(kvp_ref[:, 0:D_KV], pj[:, OFF_K:OFF_K + D_KV])
    vs = bands(kvp_ref[:, D_KV:2 * D_KV], pj[:, OFF_V:OFF_V + D_KV])
    return ks, vs


STACK = PAIRS_PER_KV * BLOCK


def _head(j, i, e):
    return 2 * (PAIRS_PER_KV * j + i) + e


def _pair_cols(j, i, off):
    p = PAIRS_PER_KV * j + i
    return slice(off + 128 * p, off + 128 * (p + 1))


def _fill_attn_bias(bias_scr, first_block):
    qi = lax.broadcasted_iota(jnp.int32, (BLOCK, 2 * BLOCK), 0)
    kj = lax.broadcasted_iota(jnp.int32, (BLOCK, 2 * BLOCK), 1)
    dist = BLOCK + qi - kj
    valid = (dist >= 0) & (dist < BLOCK)
    if first_block:
        valid = valid & (kj >= BLOCK)
    distf = dist.astype(F32)
    for j in range(2):
        for e in range(2):
            for i in range(PAIRS_PER_KV):
                bias_scr[2 * j + e, BLOCK * i:BLOCK * (i + 1), :] = jnp.where(valid, -SLOPES[_head(j, i, e)] * distf, NEG)


def _q_stack(pj, j):
    return jnp.concatenate([(pj[:, _pair_cols(j, i, OFF_Q)] * SCALE).astype(BF16) for i in range(PAIRS_PER_KV)], axis=0)


def _sink_col(sink_ref, j, e):
    return jnp.concatenate([jnp.full((BLOCK, 1), sink_ref[_head(j, i, e)], F32) for i in range(PAIRS_PER_KV)], axis=0)


def _attn_probs(q_stack, kband, bias, sink):
    s = lax.dot_general(q_stack, kband, _NT, preferred_element_type=F32) + bias
    m = jnp.maximum(jnp.max(s, axis=-1, keepdims=True), sink)
    p = jnp.exp(s - m)
    es = jnp.exp(sink - m)
    inv = 1.0 / (jnp.sum(p, axis=-1, keepdims=True) + es)
    return p * inv, es * inv


def _attn_group(pj, ks, vs, bias_scr, sink_ref, j):
    q_stack = _q_stack(pj, j)
    out, probs, shares = None, [], []
    for e in range(2):
        p, ps = _attn_probs(q_stack, ks[j][e], bias_scr[2 * j + e], _sink_col(sink_ref, j, e))
        o = jnp.dot(p.astype(BF16), vs[j][e], preferred_element_type=F32)
        out = o if out is None else out + o
        probs.append(p)
        shares.append(ps)
    return out, probs, shares


def _mix_fwd(proj, conv_full, sinks, norm_conv, norm_attn):
    def body(pj_ref, kvp_ref, cch_ref, cuh_ref, cw_ref, sink_ref, gc_ref, ga_ref, mixed_ref, attn_scr, bias_scr):
        n = pl.program_id(0)
        pj = pj_ref

        @pl.when(n == 0)
        def _():
            _fill_attn_bias(bias_scr, first_block=True)

        @pl.when(n == 1)
        def _():
            _fill_attn_bias(bias_scr, first_block=False)

        zhalo = _conv_halo(cch_ref, cuh_ref, n)
        cw = (cw_ref[0:1, :], cw_ref[1:2, :], cw_ref[2:3, :])
        gain_c = gc_ref[...]

        def conv_chunk(r, carry):
            rows = _chunk_rows(r)
            co = _conv_chunk(pj_ref, zhalo, cw, r)[-1]
            y = _gated_norm(pj_ref[rows, OFF_CB:OFF_CB + D_CONV] * co, gain_c, pj_ref[rows, OFF_GC:OFF_GC + D_CONV])
            mixed_ref[rows, 0:D_CONV] = y.astype(BF16)
            return carry

        lax.fori_loop(0, N_CHUNKS, conv_chunk, 0, unroll=True)

        ks, vs = _kv_bands(pj, kvp_ref)
        for j in range(2):
            out, _, _ = _attn_group(pj, ks, vs, bias_scr, sink_ref, j)
            for i in range(PAIRS_PER_KV):
                attn_scr[:, _pair_cols(j, i, 0)] = out[BLOCK * i:BLOCK * (i + 1), :]
        gain_a = ga_ref[...]

        def norm_chunk(r, carry):
            rows = _chunk_rows(r)
            y = _gated_norm(attn_scr[rows, :], gain_a, pj_ref[rows, OFF_GA:OFF_GA + D_ATTN])
            mixed_ref[rows, D_CONV:D_MIX] = y.astype(BF16)
            return carry

        lax.fori_loop(0, N_CHUNKS, norm_chunk, 0, unroll=True)

    per_block = BLOCK // HALO
    return pl.pallas_call(
        body, name="mix_fwd", grid=(N_BLOCKS,),
        in_specs=[
            pl.BlockSpec((BLOCK, D_PROJ), lambda n: (n, 0)),
            pl.BlockSpec((BLOCK, 2 * D_KV), lambda n: (jnp.maximum(n - 1, 0), OFF_K // (2 * D_KV))),
            pl.BlockSpec((HALO, D_CONV), lambda n: (jnp.maximum(n * per_block - 1, 0), OFF_CC // D_CONV)),
            pl.BlockSpec((HALO, D_CONV), lambda n: (jnp.maximum(n * per_block - 1, 0), OFF_CU // D_CONV)),
            pl.BlockSpec((8, D_CONV), lambda n: (0, 0)),
            pl.BlockSpec(memory_space=pltpu.SMEM),
            pl.BlockSpec((1, D_CONV), lambda n: (0, 0)),
            pl.BlockSpec((1, D_ATTN), lambda n: (0, 0)),
        ],
        out_specs=pl.BlockSpec((BLOCK, D_MIX), lambda n: (n, 0)),
        out_shape=jax.ShapeDtypeStruct((SEQ, D_MIX), BF16),
        scratch_shapes=[pltpu.VMEM((BLOCK, D_ATTN), F32), pltpu.VMEM((4, STACK, 2 * BLOCK), F32)],
        compiler_params=_params(dimension_semantics=("arbitrary",)),
    )(proj, proj, proj, proj, conv_full, sinks, norm_conv, norm_attn)


def _out_proj_loss(mixed, x, target, w_out_full, norm_final):
    tm = 256

    def body(mx_ref, x_ref, t_ref, w_ref, g_ref, dx2_ref, dx2b_ref, dmix_ref, gnf_ref, loss_ref):
        i = pl.program_id(0)
        w = w_ref[...]
        x2 = x_ref[...] + jnp.dot(mx_ref[...], w, preferred_element_type=F32)
        r = lax.rsqrt(jnp.mean(x2 * x2, axis=-1, keepdims=True) + RMS_EPS)
        xn = x2 * r
        g = g_ref[...]
        err = xn * g - t_ref[...]
        part = 0.5 * jnp.sum(jnp.mean(err * err, axis=-1, keepdims=True), axis=0, keepdims=True)
        dy = err * (1.0 / D_MODEL)
        gnf = jnp.sum(dy * xn, axis=0, keepdims=True)
        u = dy * g
        dx2 = r * (u - xn * jnp.mean(u * xn, axis=-1, keepdims=True))
        dx2_ref[...] = dx2
        dx2b = dx2.astype(BF16)
        dx2b_ref[...] = dx2b
        dmix_ref[...] = lax.dot_general(dx2b, w, _NT, preferred_element_type=F32)

        @pl.when(i == 0)
        def _():
            gnf_ref[...] = jnp.zeros_like(gnf_ref)
            loss_ref[...] = jnp.zeros_like(loss_ref)

        gnf_ref[...] += gnf
        loss_ref[...] += jnp.broadcast_to(part, loss_ref.shape)

    return pl.pallas_call(
        body, name="out_proj_loss", grid=(SEQ // tm,),
        in_specs=[pl.BlockSpec((tm, D_MIX), lambda i: (i, 0)), pl.BlockSpec((tm, D_MODEL), lambda i: (i, 0)),
                  pl.BlockSpec((tm, D_MODEL), lambda i: (i, 0)), pl.BlockSpec(memory_space=pltpu.VMEM),
                  pl.BlockSpec((1, D_MODEL), lambda i: (0, 0))],
        out_specs=(pl.BlockSpec((tm, D_MODEL), lambda i: (i, 0)), pl.BlockSpec((tm, D_MODEL), lambda i: (i, 0)),
                   pl.BlockSpec((tm, D_MIX), lambda i: (i, 0)),
                   pl.BlockSpec((1, D_MODEL), lambda i: (0, 0)), pl.BlockSpec((8, 128), lambda i: (0, 0))),
        out_shape=(jax.ShapeDtypeStruct((SEQ, D_MODEL), F32), jax.ShapeDtypeStruct((SEQ, D_MODEL), BF16),
                   jax.ShapeDtypeStruct((SEQ, D_MIX), F32),
                   jax.ShapeDtypeStruct((1, D_MODEL), F32), jax.ShapeDtypeStruct((8, 128), F32)),
        compiler_params=_params(dimension_semantics=("arbitrary",)),
    )(mixed, x, target, w_out_full, norm_final)


def _gated_norm_bwd(a, gain, t, dy):
    r = lax.rsqrt(jnp.mean(a * a, axis=-1, keepdims=True) + RMS_EPS)
    an = a * r
    sg = _sigmoid(t)
    dn = dy * (t * sg)
    dt = dy * (an * gain) * (sg * (1.0 + t * (1.0 - sg)))
    u = dn * gain
    da = r * (u - an * jnp.mean(u * an, axis=-1, keepdims=True))
    return da, dt, dn * an


def _mix_bwd(proj, dmixed, conv_full, sinks, norm_conv, norm_attn):
    def body(pj_ref, kvp_ref, cch_ref, cuh_ref, dmx_ref, cw_ref, sink_ref, gc_ref, ga_ref,
             dpj_ref, gslab_ref, attn_scr, dattn_scr, p_scr, ps_scr, nxt_scr, dkv_scr, bias_scr, acc_scr, ostack_scr):
        step = pl.program_id(0)
        n = N_BLOCKS - 1 - step
        pj = pj_ref

        @pl.when(step == 0)
        def _():
            gslab_ref[...] = jnp.zeros_like(gslab_ref)
            nxt_scr[...] = jnp.zeros_like(nxt_scr)
            dkv_scr[...] = jnp.zeros_like(dkv_scr)
            acc_scr[...] = jnp.zeros_like(acc_scr)
            _fill_attn_bias(bias_scr, first_block=False)

        @pl.when(n == 0)
        def _():
            _fill_attn_bias(bias_scr, first_block=True)

        zhalo = _conv_halo(cch_ref, cuh_ref, n)
        cw = (cw_ref[0:1, :], cw_ref[1:2, :], cw_ref[2:3, :])
        gain_c = gc_ref[...]
        row = lax.broadcasted_iota(jnp.int32, (CHUNK, D_CONV), 0)

        def conv_chunk(t, dco_after):
            r = N_CHUNKS - 1 - t
            rows = _chunk_rows(r)
            cc, cu, z, z1, z2, co = _conv_chunk(pj_ref, zhalo, cw, r)
            cb = pj_ref[rows, OFF_CB:OFF_CB + D_CONV]
            da, dgate, gterm = _gated_norm_bwd(cb * co, gain_c, pj_ref[rows, OFF_GC:OFF_GC + D_CONV],
                                               dmx_ref[rows, 0:D_CONV])
            dpj_ref[rows, OFF_GC:OFF_GC + D_CONV] = dgate.astype(BF16)
            dpj_ref[rows, OFF_CB:OFF_CB + D_CONV] = (da * co).astype(BF16)
            dco = da * cb
            dco1 = jnp.where(row >= CHUNK - 1, pltpu.roll(dco_after, CHUNK - 1, 0), pltpu.roll(dco, CHUNK - 1, 0))
            dco2 = jnp.where(row >= CHUNK - 2, pltpu.roll(dco_after, CHUNK - 2, 0), pltpu.roll(dco, CHUNK - 2, 0))
            dz = cw[2] * dco + cw[1] * dco1 + cw[0] * dco2
            dpj_ref[rows, OFF_CC:OFF_CC + D_CONV] = (dz * cu).astype(BF16)
            dpj_ref[rows, OFF_CU:OFF_CU + D_CONV] = (dz * cc).astype(BF16)
            acc_scr[ACC_NORM_CONV] += gterm
            acc_scr[ACC_CONV0] += dco * z2
            acc_scr[ACC_CONV0 + 1] += dco * z1
            acc_scr[ACC_CONV0 + 2] += dco * z
            return dco

        nxt_scr[...] = lax.fori_loop(0, N_CHUNKS, conv_chunk, nxt_scr[...], unroll=True)

        ks, vs = _kv_bands(pj, kvp_ref)
        for j in range(2):
            out, probs, shares = _attn_group(pj, ks, vs, bias_scr, sink_ref, j)
            ostack_scr[j] = out
            for e in range(2):
                p_scr[2 * j + e] = probs[e]
                ps_scr[2 * j + e] = shares[e]
            for i in range(PAIRS_PER_KV):
                attn_scr[:, _pair_cols(j, i, 0)] = out[BLOCK * i:BLOCK * (i + 1), :]
        gain_a = ga_ref[...]

        def norm_chunk(r, carry):
            rows = _chunk_rows(r)
            da, dgate, gterm = _gated_norm_bwd(attn_scr[rows, :], gain_a, pj_ref[rows, OFF_GA:OFF_GA + D_ATTN],
                                               dmx_ref[rows, D_CONV:D_MIX])
            dpj_ref[rows, OFF_GA:OFF_GA + D_ATTN] = dgate.astype(BF16)
            dattn_scr[rows, :] = da
            acc_scr[ACC_NORM_ATTN] += gterm
            return carry

        lax.fori_loop(0, N_CHUNKS, norm_chunk, 0, unroll=True)

        lo = lax.broadcasted_iota(jnp.int32, (STACK, 128), 1) < HEAD_DIM
        lane_s = lax.broadcasted_iota(jnp.int32, (1, D_MODEL), 1)
        gsink = jnp.zeros((1, D_MODEL), F32)
        dk_t, dv_t = [], []
        for j in range(2):
            q_stack = _q_stack(pj, j)
            do_f = jnp.concatenate([dattn_scr[:, _pair_cols(j, i, 0)] for i in range(PAIRS_PER_KV)], axis=0)
            prod = do_f * ostack_scr[j]
            tot = jnp.sum(prod, axis=-1, keepdims=True)
            d_lo = jnp.sum(jnp.where(lo, prod, 0.0), axis=-1, keepdims=True)
            deltas = (d_lo, tot - d_lo)
            do_b = do_f.astype(BF16)
            dq, dk_j, dv_j = None, None, None
            for e in range(2):
                p = p_scr[2 * j + e]
                dp = lax.dot_general(do_b, vs[j][e], _NT, preferred_element_type=F32)
                ds = (p * (dp - deltas[e])).astype(BF16)
                gs = ps_scr[2 * j + e] * deltas[e]
                for i in range(PAIRS_PER_KV):
                    gs_h = -jnp.sum(gs[BLOCK * i:BLOCK * (i + 1), :], axis=0, keepdims=True)
                    gsink = gsink + jnp.where(lane_s == _head(j, i, e), gs_h, 0.0)
                t = jnp.dot(ds, ks[j][e], preferred_element_type=F32)
                dq = t if dq is None else dq + t
                half = slice(HEAD_DIM * e, HEAD_DIM * (e + 1))
                a = lax.dot_general(q_stack, ds, _TN, preferred_element_type=F32)[half, :]
                b = lax.dot_general(do_b, p.astype(BF16), _TN, preferred_element_type=F32)[half, :]
                dk_j = a if dk_j is None else dk_j + a
                dv_j = b if dv_j is None else dv_j + b
            for i in range(PAIRS_PER_KV):
                dpj_ref[:, _pair_cols(j, i, OFF_Q)] = (dq[BLOCK * i:BLOCK * (i + 1), :] * SCALE).astype(BF16)
            dk_t.append(dk_j)
            dv_t.append(dv_j)
        dk = jnp.concatenate(dk_t, axis=0).T
        dv = jnp.concatenate(dv_t, axis=0).T
        dpj_ref[:, OFF_K:OFF_K + D_KV] = (dk[BLOCK:, :] + dkv_scr[:, 0:D_KV]).astype(BF16)
        dpj_ref[:, OFF_V:OFF_V + D_KV] = (dv[BLOCK:, :] + dkv_scr[:, D_KV:2 * D_KV]).astype(BF16)
        dkv_scr[:, 0:D_KV] = dk[:BLOCK, :]
        dkv_scr[:, D_KV:2 * D_KV] = dv[:BLOCK, :]
        gslab_ref[ROW_SINKS:ROW_SINKS + 1, :] += gsink

        @pl.when(step == N_BLOCKS - 1)
        def _():
            for k, slab_row in ((ACC_NORM_CONV, ROW_NORM_CONV), (ACC_NORM_ATTN, ROW_NORM_ATTN), (ACC_CONV0, ROW_CONV0),
                                (ACC_CONV0 + 1, ROW_CONV0 + 1), (ACC_CONV0 + 2, ROW_CONV0 + 2)):
                gslab_ref[slab_row:slab_row + 1, :] = jnp.sum(acc_scr[k], axis=0, keepdims=True)

    per_block = BLOCK // HALO
    last = N_BLOCKS - 1
    return pl.pallas_call(
        body, name="mix_bwd", grid=(N_BLOCKS,),
        in_specs=[
            pl.BlockSpec((BLOCK, D_PROJ), lambda s: (last - s, 0)),
            pl.BlockSpec((BLOCK, 2 * D_KV), lambda s: (jnp.maximum(last - s - 1, 0), OFF_K // (2 * D_KV))),
            pl.BlockSpec((HALO, D_CONV), lambda s: (jnp.maximum((last - s) * per_block - 1, 0), OFF_CC // D_CONV)),
            pl.BlockSpec((HALO, D_CONV), lambda s: (jnp.maximum((last - s) * per_block - 1, 0), OFF_CU // D_CONV)),
            pl.BlockSpec((BLOCK, D_MIX), lambda s: (last - s, 0)),
            pl.BlockSpec((8, D_CONV), lambda s: (0, 0)),
            pl.BlockSpec(memory_space=pltpu.SMEM),
            pl.BlockSpec((1, D_CONV), lambda s: (0, 0)),
            pl.BlockSpec((1, D_ATTN), lambda s: (0, 0)),
        ],
        out_specs=(pl.BlockSpec((BLOCK, D_PROJ), lambda s: (last - s, 0)),
                   pl.BlockSpec((8, D_MODEL), lambda s: (0, 0))),
        out_shape=(jax.ShapeDtypeStruct((SEQ, D_PROJ), BF16), jax.ShapeDtypeStruct((8, D_MODEL), F32)),
        scratch_shapes=[pltpu.VMEM((BLOCK, D_ATTN), F32), pltpu.VMEM((BLOCK, D_ATTN), F32),
                        pltpu.VMEM((4, STACK, 2 * BLOCK), F32), pltpu.VMEM((4, STACK, 1), F32),
                        pltpu.VMEM((CHUNK, D_CONV), F32), pltpu.VMEM((BLOCK, 2 * D_KV), F32),
                        pltpu.VMEM((4, STACK, 2 * BLOCK), F32), pltpu.VMEM((N_ACC, CHUNK, D_MODEL), F32),
                        pltpu.VMEM((2, STACK, 128), F32)],
        compiler_params=_params(dimension_semantics=("arbitrary",)),
    )(proj, proj, proj, proj, dmixed, conv_full, sinks, norm_conv, norm_attn)


def _in_bwd_rs(dproj, w_full, x, dx2, norm_in, dw_in_chip, gslab, gnf, loss_part):
    tm = 256
    steps = SEQ // tm
    relay_step = 4

    def body(dp_ref, w_ref, x_ref, dx2_ref, g_ref, dwi_ref, gs_ref, gnf_ref, lp_ref, gx_ref, gwin_ref, gsum_ref,
             gni_scr, own, ici, via, stage, myslab, slabs, send_sems, recv_sems, local_sems):
        i = pl.program_id(0)
        rs_start, rs_relay, rs_finish = _ici_sum(dwi_ref, own, ici, via, stage, send_sems, recv_sems, local_sems)
        slab_start, slab_finish = _slab_sum(myslab, slabs, send_sems, recv_sems, N_ICI_SUM_SEMS)

        @pl.when(i == 0)
        def _():
            gni_scr[...] = jnp.zeros_like(gni_scr)
            rs_start()

        dh = jnp.dot(dp_ref[...], w_ref[...], preferred_element_type=F32)
        xv = x_ref[...]
        r = lax.rsqrt(jnp.mean(xv * xv, axis=-1, keepdims=True) + RMS_EPS)
        xn = xv * r
        u = dh * g_ref[...]
        gx_ref[...] = dx2_ref[...] + r * (u - xn * jnp.mean(u * xn, axis=-1, keepdims=True))
        gni_scr[...] += jnp.sum(dh * xn, axis=0, keepdims=True)

        @pl.when(i == relay_step)
        def _():
            rs_relay()

        @pl.when(i == steps - 1)
        def _():
            row = lax.broadcasted_iota(jnp.int32, (8, D_MODEL), 0)
            lane = lax.broadcasted_iota(jnp.int32, (8, D_MODEL), 1)
            slab = jnp.where(row == ROW_NORM_IN, gni_scr[...], jnp.where(row == ROW_NORM_FINAL, gnf_ref[...], gs_ref[...]))
            myslab[...] = jnp.where((row == ROW_SINKS) & (lane == LOSS_LANE), lp_ref[0:1, 0:1], slab)
            slab_start()
            gwin_ref[...] = rs_finish()
            gsum_ref[...] = slab_finish()

    const = lambda i: (0, 0)
    return pl.pallas_call(
        body, name="in_bwd", grid=(steps,),
        in_specs=[pl.BlockSpec((tm, D_PROJ), lambda i: (i, 0)), pl.BlockSpec(memory_space=pltpu.VMEM),
                  pl.BlockSpec((tm, D_MODEL), lambda i: (i, 0)), pl.BlockSpec((tm, D_MODEL), lambda i: (i, 0)),
                  pl.BlockSpec((1, D_MODEL), const), pl.BlockSpec(memory_space=pl.ANY),
                  pl.BlockSpec((8, D_MODEL), const), pl.BlockSpec((1, D_MODEL), const), pl.BlockSpec((8, 128), const)],
        out_specs=(pl.BlockSpec((tm, D_MODEL), lambda i: (i, 0)), pl.BlockSpec((SHARD_IN, D_MODEL), const),
                   pl.BlockSpec((8, D_MODEL), const)),
        out_shape=(jax.ShapeDtypeStruct((SEQ, D_MODEL), F32), jax.ShapeDtypeStruct((SHARD_IN, D_MODEL), F32),
                   jax.ShapeDtypeStruct((8, D_MODEL), F32)),
        scratch_shapes=[pltpu.VMEM((1, D_MODEL), F32), pltpu.VMEM((SHARD_IN, D_MODEL), BF16),
                        pltpu.VMEM((2, SHARD_IN, D_MODEL), BF16), pltpu.VMEM((2, HALF_IN, D_MODEL), BF16),
                        pltpu.VMEM((2, HALF_IN, D_MODEL), BF16),
                        pltpu.VMEM((8, D_MODEL), F32), pltpu.VMEM((N_DEV, 8, D_MODEL), F32),
                        pltpu.SemaphoreType.DMA((N_ICI_SUM_SEMS + 7,)), pltpu.SemaphoreType.DMA((N_ICI_SUM_SEMS + 7,)),
                        pltpu.SemaphoreType.DMA((3,))],
        compiler_params=_params(dimension_semantics=("arbitrary",)),
    )(dproj, w_full, x, dx2, norm_in, dw_in_chip, gslab, gnf, loss_part)


def _dw_in_rs(dproj, h, dw_out_sh):
    tn = 640
    steps = D_PROJ // tn
    forward_step = 2
    half_step = (D_PROJ // 2) // tn

    def body(a_ref, b_ref, dwo_ref, chip_ref, gwo_ref, dwt, d2d_in, own, d2d, ici, send_sems, recv_sems, local_sems):
        i = pl.program_id(0)
        rs_start, rs_forward, rs_finish = _shard_sum(dwo_ref, own, d2d, ici, send_sems, recv_sems, local_sems)
        pair_send, pair_finish = _chip_sum(dwt, d2d_in, chip_ref, send_sems, recv_sems, local_sems,
                                           N_SHARD_SUM_SEMS, 4)

        @pl.when(i == 0)
        def _():
            rs_start()

        @pl.when(i == half_step)
        def _():
            pair_send(first=True)

        tile = lax.dot_general(a_ref[...], b_ref[...], _TN, preferred_element_type=F32).astype(BF16)
        dwt[pl.ds(pl.multiple_of(i * tn, tn), tn), :] = tile

        @pl.when(i == forward_step)
        def _():
            rs_forward()

        @pl.when(i == steps - 1)
        def _():
            pair_send(first=False)
            gwo_ref[...] = rs_finish()
            pair_finish()

    return pl.pallas_call(
        body, name="dw_in", grid=(steps,),
        in_specs=[pl.BlockSpec((SEQ, tn), lambda i: (0, i)), pl.BlockSpec(memory_space=pltpu.VMEM),
                  pl.BlockSpec(memory_space=pl.ANY)],
        out_specs=(pl.BlockSpec(memory_space=pl.ANY), pl.BlockSpec((SHARD_OUT, D_MODEL), lambda i: (0, 0))),
        out_shape=(jax.ShapeDtypeStruct((4, SHARD_IN, D_MODEL), BF16), jax.ShapeDtypeStruct((SHARD_OUT, D_MODEL), F32)),
        scratch_shapes=[pltpu.VMEM((D_PROJ, D_MODEL), BF16), pltpu.VMEM((4, SHARD_IN, D_MODEL), BF16),
                        *_shard_sum_scratch(SHARD_OUT),
                        pltpu.SemaphoreType.DMA((N_SHARD_SUM_SEMS + 4,)), pltpu.SemaphoreType.DMA((N_SHARD_SUM_SEMS + 4,)),
                        pltpu.SemaphoreType.DMA((8,))],
        compiler_params=_params(dimension_semantics=("arbitrary",)),
    )(dproj, h, dw_out_sh)


def _matmul_tn(a, b, tn, name):
    k, n = a.shape
    _, m = b.shape

    def body(a_ref, b_ref, o_ref):
        o_ref[...] = lax.dot_general(a_ref[...], b_ref[...], _TN, preferred_element_type=F32).astype(BF16)

    return pl.pallas_call(
        body, name=name, grid=(n // tn,),
        in_specs=[pl.BlockSpec((k, tn), lambda i: (0, i)), pl.BlockSpec(memory_space=pltpu.VMEM)],
        out_specs=pl.BlockSpec((tn, m), lambda i: (i, 0)),
        out_shape=jax.ShapeDtypeStruct((n, m), BF16),
        compiler_params=_params(dimension_semantics=("arbitrary",)),
    )(a, b)


def _adam_all(big_in, big_out, gsum, small):
    steps = 4
    tr_in, tr_out = SHARD_IN // steps, SHARD_OUT // steps

    def body(*refs):
        ins, outs = refs[:8 + 1 + 18], refs[8 + 1 + 18:]
        i = pl.program_id(0)
        for b in range(2):
            w_ref, g_ref, m_ref, v_ref = ins[4 * b:4 * b + 4]
            g = g_ref[...]
            delta, mn, vn = _adamw(w_ref[...], g, m_ref[...], v_ref[...])
            for ref, val in zip(outs[4 * b:4 * b + 4], (g, delta, mn, vn)):
                ref[...] = val

        @pl.when(i == 0)
        def _():
            gsum = ins[8][...]
            idx = _slot(lax.axis_index("x"), lax.axis_index("y"), lax.axis_index("c"))
            cg = jnp.zeros((3, SHARD_CONV), F32)
            for d in range(N_DEV):
                cg = jnp.where(idx == d, gsum[ROW_CONV0:ROW_CONV0 + 3, d * SHARD_CONV:(d + 1) * SHARD_CONV], cg)
            grads = (gsum[ROW_NORM_IN:ROW_NORM_IN + 1], gsum[ROW_SINKS:ROW_SINKS + 1, 0:N_Q_HEADS],
                     gsum[ROW_NORM_CONV:ROW_NORM_CONV + 1], gsum[ROW_NORM_ATTN:ROW_NORM_ATTN + 1],
                     gsum[ROW_NORM_FINAL:ROW_NORM_FINAL + 1], cg)
            for s, g in enumerate(grads):
                w_ref, m_ref, v_ref = ins[9 + 3 * s:12 + 3 * s]
                delta, mn, vn = _adamw(w_ref[...], g, m_ref[...], v_ref[...])
                for ref, val in zip(outs[8 + 4 * s:12 + 4 * s], (g, delta, mn, vn)):
                    ref[...] = val
            outs[32][...] = gsum[ROW_SINKS:ROW_SINKS + 1, LOSS_LANE:LOSS_LANE + 1]

    const = lambda i: (0, 0)
    rows = lambda i: (i, 0)
    small_shapes = [a.shape for a in small[::3]]
    in_specs = ([pl.BlockSpec((tr_in, D_MODEL), rows)] * 4 + [pl.BlockSpec((tr_out, D_MODEL), rows)] * 4
                + [pl.BlockSpec((8, D_MODEL), const)] + [pl.BlockSpec(a.shape, const) for a in small])
    out_specs = ([pl.BlockSpec((tr_in, D_MODEL), rows)] * 4 + [pl.BlockSpec((tr_out, D_MODEL), rows)] * 4
                 + [pl.BlockSpec(s, const) for s in small_shapes for _ in range(4)] + [pl.BlockSpec((1, 1), const)])
    out_shape = ([jax.ShapeDtypeStruct((SHARD_IN, D_MODEL), F32)] * 4 + [jax.ShapeDtypeStruct((SHARD_OUT, D_MODEL), F32)] * 4
                 + [jax.ShapeDtypeStruct(s, F32) for s in small_shapes for _ in range(4)]
                 + [jax.ShapeDtypeStruct((1, 1), F32)])
    outs = pl.pallas_call(
        body, name="adam", grid=(steps,), in_specs=in_specs, out_specs=tuple(out_specs), out_shape=tuple(out_shape),
        compiler_params=_params(dimension_semantics=("arbitrary",)),
    )(*big_in, *big_out, gsum, *small)
    return outs[0:4], outs[4:8], [outs[8 + 4 * s:12 + 4 * s] for s in range(6)], outs[32]


def _pad_rows(a, rows=8):
    return jnp.pad(a, ((0, rows - a.shape[0]), (0, 0)))


def kernel(x, norm_in, w_in, conv_w, attn_sinks, norm_conv_out, norm_attn_out, w_out, norm_final, loss_target, m_norm_in, m_w_in, m_conv_w, m_attn_sinks, m_norm_conv_out, m_norm_attn_out, m_w_out, m_norm_final, v_norm_in, v_w_in, v_conv_w, v_attn_sinks, v_norm_conv_out, v_norm_attn_out, v_w_out, v_norm_final):
    x2d = x.reshape(SEQ, D_MODEL)
    target = loss_target.reshape(SEQ, D_MODEL)
    nf = norm_final.reshape(1, D_MODEL)

    w_in_t, m_w_in_t, v_w_in_t = w_in[0].T, m_w_in[0].T, v_w_in[0].T
    g_in, w_out_b, conv_full = _all_gather(w_in_t, w_out[0], _pad_rows(conv_w[0]))
    w_in_full = g_in.reshape(D_PROJ, D_MODEL)
    sinks = attn_sinks.reshape(N_Q_HEADS)

    h, proj, g_out = _in_proj(x2d, norm_in, w_in_full, w_out_b)
    w_out_full = g_out.reshape(D_MIX, D_MODEL)
    mixed = _mix_fwd(proj, conv_full, sinks, norm_conv_out, norm_attn_out)
    dx2, dx2b, dmixed, gnf, loss_part = _out_proj_loss(mixed, x2d, target, w_out_full, nf)
    dproj, gslab = _mix_bwd(proj, dmixed, conv_full, sinks, norm_conv_out, norm_attn_out)
    dw_out = _matmul_tn(mixed, dx2b, 512, "dw_out")
    dw_in_chip, g_w_out = _dw_in_rs(dproj, h, dw_out.reshape(N_DEV, SHARD_OUT, D_MODEL))
    grad_x, g_w_in, gsum = _in_bwd_rs(dproj, w_in_full, x2d, dx2, norm_in, dw_in_chip, gslab, gnf, loss_part)

    small = (norm_in, m_norm_in, v_norm_in, attn_sinks, m_attn_sinks, v_attn_sinks,
             norm_conv_out, m_norm_conv_out, v_norm_conv_out, norm_attn_out, m_norm_attn_out, v_norm_attn_out,
             nf, m_norm_final.reshape(1, D_MODEL), v_norm_final.reshape(1, D_MODEL),
             conv_w[0], m_conv_w[0], v_conv_w[0])
    big_in, big_out, (s_ni, s_sk, s_nc, s_na, s_nf, s_cv), loss = _adam_all(
        (w_in_t, g_w_in, m_w_in_t, v_w_in_t), (w_out[0], g_w_out, m_w_out[0], v_w_out[0]), gsum, small)

    def leaves(k):
        return (s_ni[k], big_in[k].T[None], s_cv[k][None], s_sk[k], s_nc[k], s_na[k], big_out[k][None],
                s_nf[k].reshape(D_MODEL))

    return (loss.reshape(()), grad_x.reshape(1, SEQ, D_MODEL), *leaves(0), *leaves(1), *leaves(2), *leaves(3))
```

```python
import functools
import math

import jax
import jax.numpy as jnp
from jax import lax
from jax.experimental import pallas as pl
from jax.experimental.pallas import tpu as pltpu

F32 = jnp.float32
BF16 = jnp.bfloat16
MESH = pl.DeviceIdType.MESH

N_DEV = 8
SEQ = 2048
D_MODEL = 1024
D_CONV = 1024
D_ATTN = 1024
D_KV = 128
HEAD_DIM = 64
N_Q_HEADS = 16
N_PAIRS = N_Q_HEADS // 2
PAIRS_PER_KV = N_PAIRS // 2
D_MIX = D_CONV + D_ATTN
D_PROJ = 6400
SHARD_IN = D_PROJ // N_DEV
SHARD_OUT = D_MIX // N_DEV
SHARD_CONV = D_CONV // N_DEV
OFF_CB, OFF_CC, OFF_CU, OFF_GC, OFF_Q, OFF_K, OFF_V, OFF_GA = 0, 1024, 2048, 3072, 4096, 5120, 5248, 5376
BLOCK = 128
N_BLOCKS = SEQ // BLOCK
HALO = 8
CHUNK = 16
N_CHUNKS = BLOCK // CHUNK
RMS_EPS = 1e-5
NEG = -1e30
SCALE = HEAD_DIM ** -0.5
SLOPES = tuple(2.0 ** (-8.0 * (h + 1) / N_Q_HEADS) for h in range(N_Q_HEADS))

ADAM_LR = 0.001
ADAM_B1 = 0.9
ADAM_B2 = 0.999
ADAM_EPS = 1e-08
ADAM_WD = 0.01
ADAM_STEP = 10

ROW_NORM_IN, ROW_NORM_CONV, ROW_NORM_ATTN, ROW_NORM_FINAL, ROW_CONV0, ROW_SINKS = 0, 1, 2, 3, 4, 7
LOSS_LANE = N_Q_HEADS
ACC_NORM_CONV, ACC_NORM_ATTN, ACC_CONV0, N_ACC = 0, 1, 2, 5

VMEM_LIMIT = 56 * 1024 * 1024

_NT = (((1,), (1,)), ((), ()))
_TN = (((0,), (0,)), ((), ()))


def _params(**kw):
    return pltpu.CompilerParams(vmem_limit_bytes=VMEM_LIMIT, **kw)


def _adamw(w, g, m, v):
    m = ADAM_B1 * m + (1.0 - ADAM_B1) * g
    v = ADAM_B2 * v + (1.0 - ADAM_B2) * (g * g)
    m_hat = m / (1.0 - ADAM_B1 ** ADAM_STEP)
    v_hat = v / (1.0 - ADAM_B2 ** ADAM_STEP)
    delta = -ADAM_LR * (m_hat / (jnp.sqrt(v_hat) + ADAM_EPS) + ADAM_WD * w)
    return delta, m, v


def _sigmoid(t):
    return 1.0 / (1.0 + jnp.exp(-t))


def _slot(px, py, pc):
    return 4 * px + 2 * py + pc


HALF_IN = SHARD_IN // 2
N_GATHER_KINDS = 13


def _all_gather(w_in_sh, w_out_sh, conv_sh):
    def body(win_ref, wout_ref, cv_ref, gin_ref, woutb_ref, conv_ref, gcv_ref, send_sems, recv_sems):
        x, y, c = lax.axis_index("x"), lax.axis_index("y"), lax.axis_index("c")
        me, sibling = (x, y, c), (x, y, 1 - c)
        nx, ny, dg = (1 - x, y, c), (x, 1 - y, c), (1 - x, 1 - y, c)

        def other(dev):
            return (dev[0], dev[1], 1 - dev[2])

        gin_ref[_slot(*me)] = win_ref[...].astype(BF16)
        gcv_ref[_slot(*me)] = cv_ref[...]

        def half(dev, h):
            return gin_ref.at[_slot(*dev), pl.ds(h * HALF_IN, HALF_IN), :]

        def rc(ref, k, to):
            return pltpu.make_async_remote_copy(src_ref=ref, dst_ref=ref, send_sem=send_sems.at[k],
                                                recv_sem=recv_sems.at[k], device_id=to, device_id_type=MESH)

        own = [rc(gin_ref.at[_slot(*me)], 0, sibling),
               rc(half(me, 0), 1, nx), rc(half(me, 1), 2, nx),
               rc(half(me, 1), 4, ny), rc(half(me, 0), 3, ny)]
        for cp in own:
            cp.start()
        def cv(k, dev, to):
            s = _slot(*dev)
            return pltpu.make_async_remote_copy(src_ref=gcv_ref.at[s], dst_ref=gcv_ref.at[s],
                                                send_sem=send_sems.at[N_GATHER_KINDS + k],
                                                recv_sem=recv_sems.at[N_GATHER_KINDS + k], device_id=to, device_id_type=MESH)

        small = [cv(0, me, sibling)] + [cv(1 + j, me, peer) for j, peer in enumerate((nx, ny, dg))]
        for cp in small:
            cp.start()
        woutb_ref[...] = wout_ref[...].astype(BF16)

        relayed = []
        for (dev, h), k_in, k_ici, k_d2d in (((nx, 0), 1, 5, 7), ((ny, 1), 4, 6, 10), ((nx, 1), 2, None, 8),
                                             ((ny, 0), 3, None, 9), ((dg, 0), 5, None, 11), ((dg, 1), 6, None, 12)):
            rc(half(dev, h), k_in, me).wait_recv()
            if k_ici is not None:
                relayed.append(rc(half(dev, h), k_ici, ny if dev is nx else nx))
                relayed[-1].start()
            relayed.append(rc(half(dev, h), k_d2d, sibling))
            relayed[-1].start()
        for j, peer in enumerate((nx, ny, dg)):
            cv(1 + j, peer, me).wait_recv()
            relayed.append(cv(4 + j, peer, sibling))
            relayed[-1].start()
        rc(gin_ref.at[_slot(*sibling)], 0, me).wait_recv()
        for (dev, h), k in (((nx, 0), 7), ((nx, 1), 8), ((ny, 0), 9), ((ny, 1), 10), ((dg, 0), 11), ((dg, 1), 12)):
            rc(half(other(dev), h), k, me).wait_recv()
        cv(0, sibling, me).wait_recv()
        for j, peer in enumerate((nx, ny, dg)):
            cv(4 + j, other(peer), me).wait_recv()
        for d in range(N_DEV):
            conv_ref[:, d * SHARD_CONV:(d + 1) * SHARD_CONV] = gcv_ref[d]
        for cp in own + small + relayed:
            cp.wait_send()

    vmem = pl.BlockSpec(memory_space=pltpu.VMEM)
    return pl.pallas_call(
        body, name="all_gather",
        out_shape=(jax.ShapeDtypeStruct((N_DEV, SHARD_IN, D_MODEL), BF16),
                   jax.ShapeDtypeStruct((SHARD_OUT, D_MODEL), BF16),
                   jax.ShapeDtypeStruct((8, D_CONV), F32)),
        in_specs=[vmem, vmem, vmem], out_specs=(vmem, vmem, vmem),
        scratch_shapes=[pltpu.VMEM((N_DEV, 8, SHARD_CONV), F32),
                        pltpu.SemaphoreType.DMA((N_GATHER_KINDS + 7,)), pltpu.SemaphoreType.DMA((N_GATHER_KINDS + 7,))],
        compiler_params=_params(),
    )(w_in_sh, w_out_sh, conv_sh)


def _shard_sum(src, own, d2d, ici, send_sems, recv_sems, local_sems, base=0):
    x, y, c = lax.axis_index("x"), lax.axis_index("y"), lax.axis_index("c")
    sibling = (x, y, 1 - c)
    chips = [(x, y), (1 - x, y), (x, 1 - y), (1 - x, 1 - y)]

    def rcopy(s, d, k, to):
        return pltpu.make_async_remote_copy(src_ref=s, dst_ref=d, send_sem=send_sems.at[base + k],
                                            recv_sem=recv_sems.at[base + k], device_id=to, device_id_type=MESH)

    def mine(k):
        return pltpu.make_async_copy(src.at[_slot(*chips[k], c)], own.at[k], local_sems.at[k])

    def to_sibling(k):
        return rcopy(src.at[_slot(*chips[k], 1 - c)], d2d.at[k], k, sibling)

    def to_chip(k):
        return rcopy(own.at[k], ici.at[k - 1], 3 + k, (*chips[k], c))

    def start():
        for k in range(4):
            mine(k).start()
            to_sibling(k).start()

    def forward():
        for k in range(1, 4):
            mine(k).wait()
            to_sibling(k).wait_recv()
            own[k] = (own[k].astype(F32) + d2d[k].astype(F32)).astype(BF16)
            to_chip(k).start()

    def finish():
        mine(0).wait()
        to_sibling(0).wait_recv()
        acc = own[0].astype(F32) + d2d[0].astype(F32)
        for k in range(1, 4):
            to_chip(k).wait_recv()
            acc = acc + ici[k - 1].astype(F32)
        for k in range(4):
            to_sibling(k).wait_send()
        for k in range(1, 4):
            to_chip(k).wait_send()
        return acc

    return start, forward, finish


def _shard_sum_scratch(rows):
    return [pltpu.VMEM((4, rows, D_MODEL), BF16), pltpu.VMEM((4, rows, D_MODEL), BF16),
            pltpu.VMEM((3, rows, D_MODEL), BF16)]


N_SHARD_SUM_SEMS = 7


def _chip_sum(dwt, d2d, out_hbm, send_sems, recv_sems, local_sems, base, local_base):
    x, y, c = lax.axis_index("x"), lax.axis_index("y"), lax.axis_index("c")
    sibling = (x, y, 1 - c)
    chips = [(x, y), (1 - x, y), (x, 1 - y), (1 - x, 1 - y)]

    def shard(s):
        return dwt.at[pl.ds(pl.multiple_of(s * SHARD_IN, 16), SHARD_IN), :]

    def to_sibling(k):
        return pltpu.make_async_remote_copy(src_ref=shard(_slot(*chips[k], 1 - c)), dst_ref=d2d.at[k],
                                            send_sem=send_sems.at[base + k], recv_sem=recv_sems.at[base + k],
                                            device_id=sibling, device_id_type=MESH)

    def save(k):
        return pltpu.make_async_copy(d2d.at[k], out_hbm.at[k], local_sems.at[local_base + k])

    def send(first):
        for k in range(4):
            in_first = _slot(*chips[k], 1 - c) < N_DEV // 2

            @pl.when(in_first if first else jnp.logical_not(in_first))
            def _():
                to_sibling(k).start()

    def finish():
        for k in range(4):
            to_sibling(k).wait_recv()
            d2d[k] = (shard(_slot(*chips[k], c))[...].astype(F32) + d2d[k].astype(F32)).astype(BF16)
            save(k).start()
        for k in range(4):
            save(k).wait()
            to_sibling(k).wait_send()

    return send, finish


N_ICI_SUM_SEMS = 6


def _ici_sum(src, own, ici, via, stage, send_sems, recv_sems, local_sems, base=0):
    x, y, c = lax.axis_index("x"), lax.axis_index("y"), lax.axis_index("c")
    nx, ny = (1 - x, y, c), (x, 1 - y, c)
    OWN, NX, NY, DG = range(4)

    def half(ref, h):
        return ref.at[pl.ds(h * HALF_IN, HALF_IN), :]

    def rc(s, d, k, to):
        return pltpu.make_async_remote_copy(src_ref=s, dst_ref=d, send_sem=send_sems.at[base + k],
                                            recv_sem=recv_sems.at[base + k], device_id=to, device_id_type=MESH)

    for_dg_0 = lambda: rc(half(src.at[DG], 0), via.at[0], 0, nx)
    for_dg_1 = lambda: rc(half(src.at[DG], 1), via.at[1], 1, ny)
    for_nx_0 = lambda: rc(half(src.at[NX], 0), half(ici.at[0], 0), 2, nx)
    for_ny_1 = lambda: rc(half(src.at[NY], 1), half(ici.at[1], 1), 3, ny)
    for_ny_0 = lambda: rc(stage.at[0], half(ici.at[1], 0), 4, ny)
    for_nx_1 = lambda: rc(stage.at[1], half(ici.at[0], 1), 5, nx)
    mine = lambda: pltpu.make_async_copy(src.at[OWN], own, local_sems.at[0])
    stage_0 = lambda: pltpu.make_async_copy(half(src.at[NY], 0), stage.at[0], local_sems.at[1])
    stage_1 = lambda: pltpu.make_async_copy(half(src.at[NX], 1), stage.at[1], local_sems.at[2])

    def start():
        for cp in (for_dg_0, for_dg_1, for_nx_0, for_ny_1, stage_0, stage_1, mine):
            cp().start()

    def relay():
        for h, staged, landed, out in ((0, stage_0, for_dg_0, for_ny_0), (1, stage_1, for_dg_1, for_nx_1)):
            staged().wait()
            landed().wait_recv()
            stage[h] = (stage[h].astype(F32) + via[h].astype(F32)).astype(BF16)
            out().start()

    def finish():
        mine().wait()
        for cp in (for_nx_0, for_nx_1, for_ny_1, for_ny_0):
            cp().wait_recv()
        acc = own[...].astype(F32) + ici[0].astype(F32) + ici[1].astype(F32)
        for cp in (for_dg_0, for_dg_1, for_nx_0, for_ny_1, for_ny_0, for_nx_1):
            cp().wait_send()
        return acc

    return start, relay, finish


def _slab_sum(myslab, slabs, send_sems, recv_sems, base):
    x, y, c = lax.axis_index("x"), lax.axis_index("y"), lax.axis_index("c")
    me = _slot(x, y, c)
    peers = [(x, y, 1 - c), (1 - x, y, c), (x, 1 - y, c), (1 - x, 1 - y, c),
             (1 - x, y, 1 - c), (x, 1 - y, 1 - c), (1 - x, 1 - y, 1 - c)]

    def cp(k):
        return pltpu.make_async_remote_copy(src_ref=myslab, dst_ref=slabs.at[me], send_sem=send_sems.at[base + k],
                                            recv_sem=recv_sems.at[base + k], device_id=peers[k], device_id_type=MESH)

    def start():
        slabs[me] = myslab[...]
        for k in range(7):
            cp(k).start()

    def finish():
        for k in range(7):
            cp(k).wait_recv()
        total = slabs[0]
        for d in range(1, N_DEV):
            total = total + slabs[d]
        for k in range(7):
            cp(k).wait_send()
        return total

    return start, finish


def _in_proj(x, norm_in, w_full, w_out_b):
    tm = 512
    steps = SEQ // tm
    forward_step = steps // 2

    def body(x_ref, g_ref, w_ref, wo_ref, h_ref, proj_ref, gout_ref, send_sems, recv_sems, local_sem):
        i = pl.program_id(0)
        x, y, c = lax.axis_index("x"), lax.axis_index("y"), lax.axis_index("c")
        me, sibling = (x, y, c), (x, y, 1 - c)
        chips = [(1 - x, y), (x, 1 - y), (1 - x, 1 - y)]

        def copy(k, block, to, src=None):
            rows = gout_ref.at[_slot(*block)]
            return pltpu.make_async_remote_copy(src_ref=rows if src is None else src, dst_ref=rows,
                                                send_sem=send_sems.at[k], recv_sem=recv_sems.at[k],
                                                device_id=to, device_id_type=MESH)

        mine = pltpu.make_async_copy(wo_ref, gout_ref.at[_slot(*me)], local_sem)

        @pl.when(i == 0)
        def _():
            mine.start()
            copy(0, me, sibling, src=wo_ref).start()
            for j, chip in enumerate(chips):
                copy(1 + j, me, (*chip, c), src=wo_ref).start()

        xv = x_ref[...]
        r = lax.rsqrt(jnp.mean(xv * xv, axis=-1, keepdims=True) + RMS_EPS)
        h = (xv * r * g_ref[...]).astype(BF16)
        h_ref[...] = h
        proj_ref[...] = lax.dot_general(h, w_ref[...], _NT, preferred_element_type=F32)

        @pl.when(i == forward_step)
        def _():
            for j, chip in enumerate(chips):
                copy(1 + j, (*chip, c), me).wait_recv()
                copy(4 + j, (*chip, c), sibling).start()

        @pl.when(i == steps - 1)
        def _():
            copy(0, sibling, me).wait_recv()
            for j, chip in enumerate(chips):
                copy(4 + j, (*chip, 1 - c), me).wait_recv()
            copy(0, me, sibling, src=wo_ref).wait_send()
            for j, chip in enumerate(chips):
                copy(1 + j, me, (*chip, c), src=wo_ref).wait_send()
                copy(4 + j, (*chip, c), sibling).wait_send()
            mine.wait()

    return pl.pallas_call(
        body, name="in_proj", grid=(steps,),
        in_specs=[pl.BlockSpec((tm, D_MODEL), lambda i: (i, 0)), pl.BlockSpec((1, D_MODEL), lambda i: (0, 0)),
                  pl.BlockSpec(memory_space=pltpu.VMEM), pl.BlockSpec(memory_space=pl.ANY)],
        out_specs=(pl.BlockSpec((tm, D_MODEL), lambda i: (i, 0)), pl.BlockSpec((tm, D_PROJ), lambda i: (i, 0)),
                   pl.BlockSpec(memory_space=pl.ANY)),
        out_shape=(jax.ShapeDtypeStruct((SEQ, D_MODEL), BF16), jax.ShapeDtypeStruct((SEQ, D_PROJ), F32),
                   jax.ShapeDtypeStruct((N_DEV, SHARD_OUT, D_MODEL), BF16)),
        scratch_shapes=[pltpu.SemaphoreType.DMA((7,)), pltpu.SemaphoreType.DMA((7,)), pltpu.SemaphoreType.DMA],
        compiler_params=_params(dimension_semantics=("arbitrary",)),
    )(x, norm_in, w_full, w_out_b)


def _chunk_rows(r):
    return pl.ds(pl.multiple_of(r * CHUNK, CHUNK), CHUNK)


def _conv_halo(cch_ref, cuh_ref, n):
    zh = jnp.where(n > 0, cch_ref[...] * cuh_ref[...], 0.0)
    return jnp.concatenate([zh] * (CHUNK // HALO), axis=0)


def _conv_chunk(pj_ref, zhalo, cw, r):
    rows = _chunk_rows(r)
    cc = pj_ref[rows, OFF_CC:OFF_CC + D_CONV]
    cu = pj_ref[rows, OFF_CU:OFF_CU + D_CONV]
    z = cc * cu
    before = _chunk_rows(jnp.maximum(r - 1, 0))
    zprev = jnp.where(r > 0, pj_ref[before, OFF_CC:OFF_CC + D_CONV] * pj_ref[before, OFF_CU:OFF_CU + D_CONV], zhalo)
    row = lax.broadcasted_iota(jnp.int32, (CHUNK, D_CONV), 0)
    z1 = jnp.where(row < 1, pltpu.roll(zprev, 1, 0), pltpu.roll(z, 1, 0))
    z2 = jnp.where(row < 2, pltpu.roll(zprev, 2, 0), pltpu.roll(z, 2, 0))
    co = cw[0] * z2 + cw[1] * z1 + cw[2] * z
    return cc, cu, z, z1, z2, co


def _gated_norm(a, gain, t):
    r = lax.rsqrt(jnp.mean(a * a, axis=-1, keepdims=True) + RMS_EPS)
    return a * r * gain * (t * _sigmoid(t))


def _kv_bands(pj, kvp_ref):
    lane = lax.broadcasted_iota(jnp.int32, (2 * BLOCK, D_KV), 1)
    lo = lane < HEAD_DIM

    def bands(prev, cur):
        b = jnp.concatenate([prev, cur], axis=0)
        br = pltpu.roll(b, HEAD_DIM, 1)
        zero = jnp.zeros_like(b)
        return ((jnp.where(lo, b, zero).astype(BF16), jnp.where(lo, zero, br).astype(BF16)),
                (jnp.where(lo, br, zero).astype(BF16), jnp.where(lo, zero, b).astype(BF16)))

    ks = bands(kvp_ref[:, 0:D_KV], pj[:, OFF_K:OFF_K + D_KV])
    vs = bands(kvp_ref[:, D_KV:2 * D_KV], pj[:, OFF_V:OFF_V + D_KV])
    return ks, vs


STACK = PAIRS_PER_KV * BLOCK


def _head(j, i, e):
    return 2 * (PAIRS_PER_KV * j + i) + e


def _pair_cols(j, i, off):
    p = PAIRS_PER_KV * j + i
    return slice(off + 128 * p, off + 128 * (p + 1))


def _fill_attn_bias(bias_scr, first_block):
    qi = lax.broadcasted_iota(jnp.int32, (BLOCK, 2 * BLOCK), 0)
    kj = lax.broadcasted_iota(jnp.int32, (BLOCK, 2 * BLOCK), 1)
    dist = BLOCK + qi - kj
    valid = (dist >= 0) & (dist < BLOCK)
    if first_block:
        valid = valid & (kj >= BLOCK)
    distf = dist.astype(F32)
    for j in range(2):
        for e in range(2):
            for i in range(PAIRS_PER_KV):
                bias_scr[2 * j + e, BLOCK * i:BLOCK * (i + 1), :] = jnp.where(valid, -SLOPES[_head(j, i, e)] * distf, NEG)


def _q_stack(pj, j):
    return jnp.concatenate([(pj[:, _pair_cols(j, i, OFF_Q)] * SCALE).astype(BF16) for i in range(PAIRS_PER_KV)], axis=0)


def _sink_rows(sink_ref, j, e):
    return jnp.concatenate([jnp.full((BLOCK, 128), sink_ref[_head(j, i, e)], F32) for i in range(PAIRS_PER_KV)], axis=0)


def _attn_probs(q_stack, kband, bias, sink):
    s = lax.dot_general(q_stack, kband, _NT, preferred_element_type=F32) + bias
    m = jnp.broadcast_to(jnp.max(s, axis=-1, keepdims=True), (STACK, 128))
    m = jnp.maximum(m, sink)
    p = [jnp.exp(t - m) for t in (s[:, :128], s[:, 128:])]
    es = jnp.exp(sink - m)
    ones = jnp.ones((128, 128), BF16)
    total = (jnp.dot(p[0].astype(BF16), ones, preferred_element_type=F32)
             + jnp.dot(p[1].astype(BF16), ones, preferred_element_type=F32))
    inv = 1.0 / (total + es)
    return jnp.concatenate([p[0] * inv, p[1] * inv], axis=1), es * inv


def _attn_group(pj, ks, vs, bias_scr, sink_ref, j):
    q_stack = _q_stack(pj, j)
    out, probs, shares = None, [], []
    for e in range(2):
        p, ps = _attn_probs(q_stack, ks[j][e], bias_scr[2 * j + e], _sink_rows(sink_ref, j, e))
        o = jnp.dot(p.astype(BF16), vs[j][e], preferred_element_type=F32)
        out = o if out is None else out + o
        probs.append(p)
        shares.append(ps)
    return out, probs, shares


def _mix_fwd(proj, conv_full, sinks, norm_conv, norm_attn):
    def body(pj_ref, kvp_ref, cch_ref, cuh_ref, cw_ref, sink_ref, gc_ref, ga_ref, mixed_ref, attn_scr, bias_scr):
        n = pl.program_id(0)
        pj = pj_ref

        @pl.when(n == 0)
        def _():
            _fill_attn_bias(bias_scr, first_block=True)

        @pl.when(n == 1)
        def _():
            _fill_attn_bias(bias_scr, first_block=False)

        zhalo = _conv_halo(cch_ref, cuh_ref, n)
        cw = (cw_ref[0:1, :], cw_ref[1:2, :], cw_ref[2:3, :])
        gain_c = gc_ref[...]

        def conv_chunk(r, carry):
            rows = _chunk_rows(r)
            co = _conv_chunk(pj_ref, zhalo, cw, r)[-1]
            y = _gated_norm(pj_ref[rows, OFF_CB:OFF_CB + D_CONV] * co, gain_c, pj_ref[rows, OFF_GC:OFF_GC + D_CONV])
            mixed_ref[rows, 0:D_CONV] = y.astype(BF16)
            return carry

        lax.fori_loop(0, N_CHUNKS, conv_chunk, 0, unroll=True)

        ks, vs = _kv_bands(pj, kvp_ref)
        for j in range(2):
            out, _, _ = _attn_group(pj, ks, vs, bias_scr, sink_ref, j)
            for i in range(PAIRS_PER_KV):
                attn_scr[:, _pair_cols(j, i, 0)] = out[BLOCK * i:BLOCK * (i + 1), :]
        gain_a = ga_ref[...]

        def norm_chunk(r, carry):
            rows = _chunk_rows(r)
            y = _gated_norm(attn_scr[rows, :], gain_a, pj_ref[rows, OFF_GA:OFF_GA + D_ATTN])
            mixed_ref[rows, D_CONV:D_MIX] = y.astype(BF16)
            return carry

        lax.fori_loop(0, N_CHUNKS, norm_chunk, 0, unroll=True)

    per_block = BLOCK // HALO
    return pl.pallas_call(
        body, name="mix_fwd", grid=(N_BLOCKS,),
        in_specs=[
            pl.BlockSpec((BLOCK, D_PROJ), lambda n: (n, 0)),
            pl.BlockSpec((BLOCK, 2 * D_KV), lambda n: (jnp.maximum(n - 1, 0), OFF_K // (2 * D_KV))),
            pl.BlockSpec((HALO, D_CONV), lambda n: (jnp.maximum(n * per_block - 1, 0), OFF_CC // D_CONV)),
            pl.BlockSpec((HALO, D_CONV), lambda n: (jnp.maximum(n * per_block - 1, 0), OFF_CU // D_CONV)),
            pl.BlockSpec((8, D_CONV), lambda n: (0, 0)),
            pl.BlockSpec(memory_space=pltpu.SMEM),
            pl.BlockSpec((1, D_CONV), lambda n: (0, 0)),
            pl.BlockSpec((1, D_ATTN), lambda n: (0, 0)),
        ],
        out_specs=pl.BlockSpec((BLOCK, D_MIX), lambda n: (n, 0)),
        out_shape=jax.ShapeDtypeStruct((SEQ, D_MIX), BF16),
        scratch_shapes=[pltpu.VMEM((BLOCK, D_ATTN), F32), pltpu.VMEM((4, STACK, 2 * BLOCK), F32)],
        compiler_params=_params(dimension_semantics=("arbitrary",)),
    )(proj, proj, proj, proj, conv_full, sinks, norm_conv, norm_attn)


def _out_proj_loss(mixed, x, target, w_out_full, norm_final):
    tm = 256

    def body(mx_ref, x_ref, t_ref, w_ref, g_ref, dx2_ref, dx2b_ref, dmix_ref, gnf_ref, loss_ref):
        i = pl.program_id(0)
        w = w_ref[...]
        x2 = x_ref[...] + jnp.dot(mx_ref[...], w, preferred_element_type=F32)
        r = lax.rsqrt(jnp.mean(x2 * x2, axis=-1, keepdims=True) + RMS_EPS)
        xn = x2 * r
        g = g_ref[...]
        err = xn * g - t_ref[...]
        part = 0.5 * jnp.sum(jnp.mean(err * err, axis=-1, keepdims=True), axis=0, keepdims=True)
        dy = err * (1.0 / D_MODEL)
        gnf = jnp.sum(dy * xn, axis=0, keepdims=True)
        u = dy * g
        dx2 = r * (u - xn * jnp.mean(u * xn, axis=-1, keepdims=True))
        dx2_ref[...] = dx2
        dx2b = dx2.astype(BF16)
        dx2b_ref[...] = dx2b
        dmix_ref[...] = lax.dot_general(dx2b, w, _NT, preferred_element_type=F32)

        @pl.when(i == 0)
        def _():
            gnf_ref[...] = jnp.zeros_like(gnf_ref)
            loss_ref[...] = jnp.zeros_like(loss_ref)

        gnf_ref[...] += gnf
        loss_ref[...] += jnp.broadcast_to(part, loss_ref.shape)

    return pl.pallas_call(
        body, name="out_proj_loss", grid=(SEQ // tm,),
        in_specs=[pl.BlockSpec((tm, D_MIX), lambda i: (i, 0)), pl.BlockSpec((tm, D_MODEL), lambda i: (i, 0)),
                  pl.BlockSpec((tm, D_MODEL), lambda i: (i, 0)), pl.BlockSpec(memory_space=pltpu.VMEM),
                  pl.BlockSpec((1, D_MODEL), lambda i: (0, 0))],
        out_specs=(pl.BlockSpec((tm, D_MODEL), lambda i: (i, 0)), pl.BlockSpec((tm, D_MODEL), lambda i: (i, 0)),
                   pl.BlockSpec((tm, D_MIX), lambda i: (i, 0)),
                   pl.BlockSpec((1, D_MODEL), lambda i: (0, 0)), pl.BlockSpec((8, 128), lambda i: (0, 0))),
        out_shape=(jax.ShapeDtypeStruct((SEQ, D_MODEL), F32), jax.ShapeDtypeStruct((SEQ, D_MODEL), BF16),
                   jax.ShapeDtypeStruct((SEQ, D_MIX), F32),
                   jax.ShapeDtypeStruct((1, D_MODEL), F32), jax.ShapeDtypeStruct((8, 128), F32)),
        compiler_params=_params(dimension_semantics=("arbitrary",)),
    )(mixed, x, target, w_out_full, norm_final)


def _gated_norm_bwd(a, gain, t, dy):
    r = lax.rsqrt(jnp.mean(a * a, axis=-1, keepdims=True) + RMS_EPS)
    an = a * r
    sg = _sigmoid(t)
    dn = dy * (t * sg)
    dt = dy * (an * gain) * (sg * (1.0 + t * (1.0 - sg)))
    u = dn * gain
    da = r * (u - an * jnp.mean(u * an, axis=-1, keepdims=True))
    return da, dt, dn * an


def _mix_bwd(proj, dmixed, conv_full, sinks, norm_conv, norm_attn):
    def body(pj_ref, kvp_ref, cch_ref, cuh_ref, dmx_ref, cw_ref, sink_ref, gc_ref, ga_ref,
             dpj_ref, gslab_ref, attn_scr, dattn_scr, p_scr, ps_scr, nxt_scr, dkv_scr, bias_scr, acc_scr, ostack_scr):
        step = pl.program_id(0)
        n = N_BLOCKS - 1 - step
        pj = pj_ref

        @pl.when(step == 0)
        def _():
            gslab_ref[...] = jnp.zeros_like(gslab_ref)
            nxt_scr[...] = jnp.zeros_like(nxt_scr)
            dkv_scr[...] = jnp.zeros_like(dkv_scr)
            acc_scr[...] = jnp.zeros_like(acc_scr)
            _fill_attn_bias(bias_scr, first_block=False)

        @pl.when(n == 0)
        def _():
            _fill_attn_bias(bias_scr, first_block=True)

        zhalo = _conv_halo(cch_ref, cuh_ref, n)
        cw = (cw_ref[0:1, :], cw_ref[1:2, :], cw_ref[2:3, :])
        gain_c = gc_ref[...]
        row = lax.broadcasted_iota(jnp.int32, (CHUNK, D_CONV), 0)

        def conv_chunk(t, dco_after):
            r = N_CHUNKS - 1 - t
            rows = _chunk_rows(r)
            cc, cu, z, z1, z2, co = _conv_chunk(pj_ref, zhalo, cw, r)
            cb = pj_ref[rows, OFF_CB:OFF_CB + D_CONV]
            da, dgate, gterm = _gated_norm_bwd(cb * co, gain_c, pj_ref[rows, OFF_GC:OFF_GC + D_CONV],
                                               dmx_ref[rows, 0:D_CONV])
            dpj_ref[rows, OFF_GC:OFF_GC + D_CONV] = dgate.astype(BF16)
            dpj_ref[rows, OFF_CB:OFF_CB + D_CONV] = (da * co).astype(BF16)
            dco = da * cb
            dco1 = jnp.where(row >= CHUNK - 1, pltpu.roll(dco_after, CHUNK - 1, 0), pltpu.roll(dco, CHUNK - 1, 0))
            dco2 = jnp.where(row >= CHUNK - 2, pltpu.roll(dco_after, CHUNK - 2, 0), pltpu.roll(dco, CHUNK - 2, 0))
            dz = cw[2] * dco + cw[1] * dco1 + cw[0] * dco2
            dpj_ref[rows, OFF_CC:OFF_CC + D_CONV] = (dz * cu).astype(BF16)
            dpj_ref[rows, OFF_CU:OFF_CU + D_CONV] = (dz * cc).astype(BF16)
            acc_scr[ACC_NORM_CONV] += gterm
            acc_scr[ACC_CONV0] += dco * z2
            acc_scr[ACC_CONV0 + 1] += dco * z1
            acc_scr[ACC_CONV0 + 2] += dco * z
            return dco

        nxt_scr[...] = lax.fori_loop(0, N_CHUNKS, conv_chunk, nxt_scr[...], unroll=True)

        ks, vs = _kv_bands(pj, kvp_ref)
        for j in range(2):
            out, probs, shares = _attn_group(pj, ks, vs, bias_scr, sink_ref, j)
            ostack_scr[j] = out
            for e in range(2):
                p_scr[2 * j + e] = probs[e]
                ps_scr[2 * j + e] = shares[e]
            for i in range(PAIRS_PER_KV):
                attn_scr[:, _pair_cols(j, i, 0)] = out[BLOCK * i:BLOCK * (i + 1), :]
        gain_a = ga_ref[...]

        def norm_chunk(r, carry):
            rows = _chunk_rows(r)
            da, dgate, gterm = _gated_norm_bwd(attn_scr[rows, :], gain_a, pj_ref[rows, OFF_GA:OFF_GA + D_ATTN],
                                               dmx_ref[rows, D_CONV:D_MIX])
            dpj_ref[rows, OFF_GA:OFF_GA + D_ATTN] = dgate.astype(BF16)
            dattn_scr[rows, :] = da
            acc_scr[ACC_NORM_ATTN] += gterm
            return carry

        lax.fori_loop(0, N_CHUNKS, norm_chunk, 0, unroll=True)

        in_lo = lax.broadcasted_iota(jnp.int32, (128, 128), 0) < HEAD_DIM
        half_ones = (jnp.where(in_lo, 1.0, 0.0).astype(BF16), jnp.where(in_lo, 0.0, 1.0).astype(BF16))
        lane_s = lax.broadcasted_iota(jnp.int32, (1, D_MODEL), 1)
        gsink = jnp.zeros((1, D_MODEL), F32)
        dk_t, dv_t = [], []
        for j in range(2):
            q_stack = _q_stack(pj, j)
            do_f = jnp.concatenate([dattn_scr[:, _pair_cols(j, i, 0)] for i in range(PAIRS_PER_KV)], axis=0)
            prod = (do_f * ostack_scr[j]).astype(BF16)
            deltas = [jnp.dot(prod, half_ones[e], preferred_element_type=F32) for e in range(2)]
            do_b = do_f.astype(BF16)
            dq, dk_j, dv_j = None, None, None
            for e in range(2):
                p = p_scr[2 * j + e]
                dp = lax.dot_general(do_b, vs[j][e], _NT, preferred_element_type=F32)
                delta = jnp.concatenate([deltas[e], deltas[e]], axis=1)
                ds = (p * (dp - delta)).astype(BF16)
                gs = ps_scr[2 * j + e] * deltas[e]
                for i in range(PAIRS_PER_KV):
                    gs_h = -jnp.sum(gs[BLOCK * i:BLOCK * (i + 1), 0:1], axis=0, keepdims=True)
                    gsink = gsink + jnp.where(lane_s == _head(j, i, e), gs_h, 0.0)
                t = jnp.dot(ds, ks[j][e], preferred_element_type=F32)
                dq = t if dq is None else dq + t
                half = slice(HEAD_DIM * e, HEAD_DIM * (e + 1))
                a = lax.dot_general(q_stack, ds, _TN, preferred_element_type=F32)[half, :]
                b = lax.dot_general(do_b, p.astype(BF16), _TN, preferred_element_type=F32)[half, :]
                dk_j = a if dk_j is None else dk_j + a
                dv_j = b if dv_j is None else dv_j + b
            for i in range(PAIRS_PER_KV):
                dpj_ref[:, _pair_cols(j, i, OFF_Q)] = (dq[BLOCK * i:BLOCK * (i + 1), :] * SCALE).astype(BF16)
            dk_t.append(dk_j)
            dv_t.append(dv_j)
        dk = jnp.concatenate(dk_t, axis=0).T
        dv = jnp.concatenate(dv_t, axis=0).T
        dpj_ref[:, OFF_K:OFF_K + D_KV] = (dk[BLOCK:, :] + dkv_scr[:, 0:D_KV]).astype(BF16)
        dpj_ref[:, OFF_V:OFF_V + D_KV] = (dv[BLOCK:, :] + dkv_scr[:, D_KV:2 * D_KV]).astype(BF16)
        dkv_scr[:, 0:D_KV] = dk[:BLOCK, :]
        dkv_scr[:, D_KV:2 * D_KV] = dv[:BLOCK, :]
        gslab_ref[ROW_SINKS:ROW_SINKS + 1, :] += gsink

        @pl.when(step == N_BLOCKS - 1)
        def _():
            for k, slab_row in ((ACC_NORM_CONV, ROW_NORM_CONV), (ACC_NORM_ATTN, ROW_NORM_ATTN), (ACC_CONV0, ROW_CONV0),
                                (ACC_CONV0 + 1, ROW_CONV0 + 1), (ACC_CONV0 + 2, ROW_CONV0 + 2)):
                gslab_ref[slab_row:slab_row + 1, :] = jnp.sum(acc_scr[k], axis=0, keepdims=True)

    per_block = BLOCK // HALO
    last = N_BLOCKS - 1
    return pl.pallas_call(
        body, name="mix_bwd", grid=(N_BLOCKS,),
        in_specs=[
            pl.BlockSpec((BLOCK, D_PROJ), lambda s: (last - s, 0)),
            pl.BlockSpec((BLOCK, 2 * D_KV), lambda s: (jnp.maximum(last - s - 1, 0), OFF_K // (2 * D_KV))),
            pl.BlockSpec((HALO, D_CONV), lambda s: (jnp.maximum((last - s) * per_block - 1, 0), OFF_CC // D_CONV)),
            pl.BlockSpec((HALO, D_CONV), lambda s: (jnp.maximum((last - s) * per_block - 1, 0), OFF_CU // D_CONV)),
            pl.BlockSpec((BLOCK, D_MIX), lambda s: (last - s, 0)),
            pl.BlockSpec((8, D_CONV), lambda s: (0, 0)),
            pl.BlockSpec(memory_space=pltpu.SMEM),
            pl.BlockSpec((1, D_CONV), lambda s: (0, 0)),
            pl.BlockSpec((1, D_ATTN), lambda s: (0, 0)),
        ],
        out_specs=(pl.BlockSpec((BLOCK, D_PROJ), lambda s: (last - s, 0)),
                   pl.BlockSpec((8, D_MODEL), lambda s: (0, 0))),
        out_shape=(jax.ShapeDtypeStruct((SEQ, D_PROJ), BF16), jax.ShapeDtypeStruct((8, D_MODEL), F32)),
        scratch_shapes=[pltpu.VMEM((BLOCK, D_ATTN), F32), pltpu.VMEM((BLOCK, D_ATTN), F32),
                        pltpu.VMEM((4, STACK, 2 * BLOCK), F32), pltpu.VMEM((4, STACK, 128), F32),
                        pltpu.VMEM((CHUNK, D_CONV), F32), pltpu.VMEM((BLOCK, 2 * D_KV), F32),
                        pltpu.VMEM((4, STACK, 2 * BLOCK), F32), pltpu.VMEM((N_ACC, CHUNK, D_MODEL), F32),
                        pltpu.VMEM((2, STACK, 128), F32)],
        compiler_params=_params(dimension_semantics=("arbitrary",)),
    )(proj, proj, proj, proj, dmixed, conv_full, sinks, norm_conv, norm_attn)


def _in_bwd_rs(dproj, w_full, x, dx2, norm_in, dw_in_chip, gslab, gnf, loss_part):
    tm = 256
    steps = SEQ // tm
    relay_step = 4

    def body(dp_ref, w_ref, x_ref, dx2_ref, g_ref, dwi_ref, gs_ref, gnf_ref, lp_ref, gx_ref, gwin_ref, gsum_ref,
             gni_scr, own, ici, via, stage, myslab, slabs, send_sems, recv_sems, local_sems):
        i = pl.program_id(0)
        rs_start, rs_relay, rs_finish = _ici_sum(dwi_ref, own, ici, via, stage, send_sems, recv_sems, local_sems)
        slab_start, slab_finish = _slab_sum(myslab, slabs, send_sems, recv_sems, N_ICI_SUM_SEMS)

        @pl.when(i == 0)
        def _():
            gni_scr[...] = jnp.zeros_like(gni_scr)
            rs_start()

        dh = jnp.dot(dp_ref[...], w_ref[...], preferred_element_type=F32)
        xv = x_ref[...]
        r = lax.rsqrt(jnp.mean(xv * xv, axis=-1, keepdims=True) + RMS_EPS)
        xn = xv * r
        u = dh * g_ref[...]
        gx_ref[...] = dx2_ref[...] + r * (u - xn * jnp.mean(u * xn, axis=-1, keepdims=True))
        gni_scr[...] += jnp.sum(dh * xn, axis=0, keepdims=True)

        @pl.when(i == relay_step)
        def _():
            rs_relay()

        @pl.when(i == steps - 1)
        def _():
            row = lax.broadcasted_iota(jnp.int32, (8, D_MODEL), 0)
            lane = lax.broadcasted_iota(jnp.int32, (8, D_MODEL), 1)
            slab = jnp.where(row == ROW_NORM_IN, gni_scr[...], jnp.where(row == ROW_NORM_FINAL, gnf_ref[...], gs_ref[...]))
            myslab[...] = jnp.where((row == ROW_SINKS) & (lane == LOSS_LANE), lp_ref[0:1, 0:1], slab)
            slab_start()
            gwin_ref[...] = rs_finish()
            gsum_ref[...] = slab_finish()

    const = lambda i: (0, 0)
    return pl.pallas_call(
        body, name="in_bwd", grid=(steps,),
        in_specs=[pl.BlockSpec((tm, D_PROJ), lambda i: (i, 0)), pl.BlockSpec(memory_space=pltpu.VMEM),
                  pl.BlockSpec((tm, D_MODEL), lambda i: (i, 0)), pl.BlockSpec((tm, D_MODEL), lambda i: (i, 0)),
                  pl.BlockSpec((1, D_MODEL), const), pl.BlockSpec(memory_space=pl.ANY),
                  pl.BlockSpec((8, D_MODEL), const), pl.BlockSpec((1, D_MODEL), const), pl.BlockSpec((8, 128), const)],
        out_specs=(pl.BlockSpec((tm, D_MODEL), lambda i: (i, 0)), pl.BlockSpec((SHARD_IN, D_MODEL), const),
                   pl.BlockSpec((8, D_MODEL), const)),
        out_shape=(jax.ShapeDtypeStruct((SEQ, D_MODEL), F32), jax.ShapeDtypeStruct((SHARD_IN, D_MODEL), F32),
                   jax.ShapeDtypeStruct((8, D_MODEL), F32)),
        scratch_shapes=[pltpu.VMEM((1, D_MODEL), F32), pltpu.VMEM((SHARD_IN, D_MODEL), BF16),
                        pltpu.VMEM((2, SHARD_IN, D_MODEL), BF16), pltpu.VMEM((2, HALF_IN, D_MODEL), BF16),
                        pltpu.VMEM((2, HALF_IN, D_MODEL), BF16),
                        pltpu.VMEM((8, D_MODEL), F32), pltpu.VMEM((N_DEV, 8, D_MODEL), F32),
                        pltpu.SemaphoreType.DMA((N_ICI_SUM_SEMS + 7,)), pltpu.SemaphoreType.DMA((N_ICI_SUM_SEMS + 7,)),
                        pltpu.SemaphoreType.DMA((3,))],
        compiler_params=_params(dimension_semantics=("arbitrary",)),
    )(dproj, w_full, x, dx2, norm_in, dw_in_chip, gslab, gnf, loss_part)


def _dw_in_rs(dproj, h, dw_out_sh):
    tn = 640
    steps = D_PROJ // tn
    forward_step = 2
    half_step = (D_PROJ // 2) // tn

    def body(a_ref, b_ref, dwo_ref, chip_ref, gwo_ref, dwt, d2d_in, own, d2d, ici, send_sems, recv_sems, local_sems):
        i = pl.program_id(0)
        rs_start, rs_forward, rs_finish = _shard_sum(dwo_ref, own, d2d, ici, send_sems, recv_sems, local_sems)
        pair_send, pair_finish = _chip_sum(dwt, d2d_in, chip_ref, send_sems, recv_sems, local_sems,
                                           N_SHARD_SUM_SEMS, 4)

        @pl.when(i == 0)
        def _():
            rs_start()

        @pl.when(i == half_step)
        def _():
            pair_send(first=True)

        tile = lax.dot_general(a_ref[...], b_ref[...], _TN, preferred_element_type=F32).astype(BF16)
        dwt[pl.ds(pl.multiple_of(i * tn, tn), tn), :] = tile

        @pl.when(i == forward_step)
        def _():
            rs_forward()

        @pl.when(i == steps - 1)
        def _():
            pair_send(first=False)
            gwo_ref[...] = rs_finish()
            pair_finish()

    return pl.pallas_call(
        body, name="dw_in", grid=(steps,),
        in_specs=[pl.BlockSpec((SEQ, tn), lambda i: (0, i)), pl.BlockSpec(memory_space=pltpu.VMEM),
                  pl.BlockSpec(memory_space=pl.ANY)],
        out_specs=(pl.BlockSpec(memory_space=pl.ANY), pl.BlockSpec((SHARD_OUT, D_MODEL), lambda i: (0, 0))),
        out_shape=(jax.ShapeDtypeStruct((4, SHARD_IN, D_MODEL), BF16), jax.ShapeDtypeStruct((SHARD_OUT, D_MODEL), F32)),
        scratch_shapes=[pltpu.VMEM((D_PROJ, D_MODEL), BF16), pltpu.VMEM((4, SHARD_IN, D_MODEL), BF16),
                        *_shard_sum_scratch(SHARD_OUT),
                        pltpu.SemaphoreType.DMA((N_SHARD_SUM_SEMS + 4,)), pltpu.SemaphoreType.DMA((N_SHARD_SUM_SEMS + 4,)),
                        pltpu.SemaphoreType.DMA((8,))],
        compiler_params=_params(dimension_semantics=("arbitrary",)),
    )(dproj, h, dw_out_sh)


def _matmul_tn(a, b, tn, name):
    k, n = a.shape
    _, m = b.shape

    def body(a_ref, b_ref, o_ref):
        o_ref[...] = lax.dot_general(a_ref[...], b_ref[...], _TN, preferred_element_type=F32).astype(BF16)

    return pl.pallas_call(
        body, name=name, grid=(n // tn,),
        in_specs=[pl.BlockSpec((k, tn), lambda i: (0, i)), pl.BlockSpec(memory_space=pltpu.VMEM)],
        out_specs=pl.BlockSpec((tn, m), lambda i: (i, 0)),
        out_shape=jax.ShapeDtypeStruct((n, m), BF16),
        compiler_params=_params(dimension_semantics=("arbitrary",)),
    )(a, b)


def _adam_all(big_in, big_out, gsum, small):
    steps = 4
    tr_in, tr_out = SHARD_IN // steps, SHARD_OUT // steps

    def body(*refs):
        ins, outs = refs[:8 + 1 + 18], refs[8 + 1 + 18:]
        i = pl.program_id(0)
        for b in range(2):
            w_ref, g_ref, m_ref, v_ref = ins[4 * b:4 * b + 4]
            g = g_ref[...]
            delta, mn, vn = _adamw(w_ref[...], g, m_ref[...], v_ref[...])
            for ref, val in zip(outs[4 * b:4 * b + 4], (g, delta, mn, vn)):
                ref[...] = val

        @pl.when(i == 0)
        def _():
            gsum = ins[8][...]
            idx = _slot(lax.axis_index("x"), lax.axis_index("y"), lax.axis_index("c"))
            cg = jnp.zeros((3, SHARD_CONV), F32)
            for d in range(N_DEV):
                cg = jnp.where(idx == d, gsum[ROW_CONV0:ROW_CONV0 + 3, d * SHARD_CONV:(d + 1) * SHARD_CONV], cg)
            grads = (gsum[ROW_NORM_IN:ROW_NORM_IN + 1], gsum[ROW_SINKS:ROW_SINKS + 1, 0:N_Q_HEADS],
                     gsum[ROW_NORM_CONV:ROW_NORM_CONV + 1], gsum[ROW_NORM_ATTN:ROW_NORM_ATTN + 1],
                     gsum[ROW_NORM_FINAL:ROW_NORM_FINAL + 1], cg)
            for s, g in enumerate(grads):
                w_ref, m_ref, v_ref = ins[9 + 3 * s:12 + 3 * s]
                delta, mn, vn = _adamw(w_ref[...], g, m_ref[...], v_ref[...])
                for ref, val in zip(outs[8 + 4 * s:12 + 4 * s], (g, delta, mn, vn)):
                    ref[...] = val
            outs[32][...] = gsum[ROW_SINKS:ROW_SINKS + 1, LOSS_LANE:LOSS_LANE + 1]

    const = lambda i: (0, 0)
    rows = lambda i: (i, 0)
    small_shapes = [a.shape for a in small[::3]]
    in_specs = ([pl.BlockSpec((tr_in, D_MODEL), rows)] * 4 + [pl.BlockSpec((tr_out, D_MODEL), rows)] * 4
                + [pl.BlockSpec((8, D_MODEL), const)] + [pl.BlockSpec(a.shape, const) for a in small])
    out_specs = ([pl.BlockSpec((tr_in, D_MODEL), rows)] * 4 + [pl.BlockSpec((tr_out, D_MODEL), rows)] * 4
                 + [pl.BlockSpec(s, const) for s in small_shapes for _ in range(4)] + [pl.BlockSpec((1, 1), const)])
    out_shape = ([jax.ShapeDtypeStruct((SHARD_IN, D_MODEL), F32)] * 4 + [jax.ShapeDtypeStruct((SHARD_OUT, D_MODEL), F32)] * 4
                 + [jax.ShapeDtypeStruct(s, F32) for s in small_shapes for _ in range(4)]
                 + [jax.ShapeDtypeStruct((1, 1), F32)])
    outs = pl.pallas_call(
        body, name="adam", grid=(steps,), in_specs=in_specs, out_specs=tuple(out_specs), out_shape=tuple(out_shape),
        compiler_params=_params(dimension_semantics=("arbitrary",)),
    )(*big_in, *big_out, gsum, *small)
    return outs[0:4], outs[4:8], [outs[8 + 4 * s:12 + 4 * s] for s in range(6)], outs[32]


def _pad_rows(a, rows=8):
    return jnp.pad(a, ((0, rows - a.shape[0]), (0, 0)))


def kernel(x, norm_in, w_in, conv_w, attn_sinks, norm_conv_out, norm_attn_out, w_out, norm_final, loss_target, m_norm_in, m_w_in, m_conv_w, m_attn_sinks, m_norm_conv_out, m_norm_attn_out, m_w_out, m_norm_final, v_norm_in, v_w_in, v_conv_w, v_attn_sinks, v_norm_conv_out, v_norm_attn_out, v_w_out, v_norm_final):
    x2d = x.reshape(SEQ, D_MODEL)
    target = loss_target.reshape(SEQ, D_MODEL)
    nf = norm_final.reshape(1, D_MODEL)

    w_in_t, m_w_in_t, v_w_in_t = w_in[0].T, m_w_in[0].T, v_w_in[0].T
    g_in, w_out_b, conv_full = _all_gather(w_in_t, w_out[0], _pad_rows(conv_w[0]))
    w_in_full = g_in.reshape(D_PROJ, D_MODEL)
    sinks = attn_sinks.reshape(N_Q_HEADS)

    h, proj, g_out = _in_proj(x2d, norm_in, w_in_full, w_out_b)
    w_out_full = g_out.reshape(D_MIX, D_MODEL)
    mixed = _mix_fwd(proj, conv_full, sinks, norm_conv_out, norm_attn_out)
    dx2, dx2b, dmixed, gnf, loss_part = _out_proj_loss(mixed, x2d, target, w_out_full, nf)
    dproj, gslab = _mix_bwd(proj, dmixed, conv_full, sinks, norm_conv_out, norm_attn_out)
    dw_out = _matmul_tn(mixed, dx2b, 512, "dw_out")
    dw_in_chip, g_w_out = _dw_in_rs(dproj, h, dw_out.reshape(N_DEV, SHARD_OUT, D_MODEL))
    grad_x, g_w_in, gsum = _in_bwd_rs(dproj, w_in_full, x2d, dx2, norm_in, dw_in_chip, gslab, gnf, loss_part)

    small = (norm_in, m_norm_in, v_norm_in, attn_sinks, m_attn_sinks, v_attn_sinks,
             norm_conv_out, m_norm_conv_out, v_norm_conv_out, norm_attn_out, m_norm_attn_out, v_norm_attn_out,
             nf, m_norm_final.reshape(1, D_MODEL), v_norm_final.reshape(1, D_MODEL),
             conv_w[0], m_conv_w[0], v_conv_w[0])
    big_in, big_out, (s_ni, s_sk, s_nc, s_na, s_nf, s_cv), loss = _adam_all(
        (w_in_t, g_w_in, m_w_in_t, v_w_in_t), (w_out[0], g_w_out, m_w_out[0], v_w_out[0]), gsum, small)

    def leaves(k):
        return (s_ni[k], big_in[k].T[None], s_cv[k][None], s_sk[k], s_nc[k], s_na[k], big_out[k][None],
                s_nf[k].reshape(D_MODEL))

    return (loss.reshape(()), grad_x.reshape(1, SEQ, D_MODEL), *leaves(0), *leaves(1), *leaves(2), *leaves(3))
```

```python
import functools
import math

import jax
import jax.numpy as jnp
from jax import lax
from jax.experimental import pallas as pl
from jax.experimental.pallas import tpu as pltpu

F32 = jnp.float32
BF16 = jnp.bfloat16
MESH = pl.DeviceIdType.MESH

N_DEV = 8
SEQ = 2048
D_MODEL = 1024
D_CONV = 1024
D_ATTN = 1024
D_KV = 128
HEAD_DIM = 64
N_Q_HEADS = 16
N_PAIRS = N_Q_HEADS // 2
PAIRS_PER_KV = N_PAIRS // 2
D_MIX = D_CONV + D_ATTN
D_PROJ = 6400
SHARD_IN = D_PROJ // N_DEV
SHARD_OUT = D_MIX // N_DEV
SHARD_CONV = D_CONV // N_DEV
OFF_CB, OFF_CC, OFF_CU, OFF_GC, OFF_Q, OFF_K, OFF_V, OFF_GA = 0, 1024, 2048, 3072, 4096, 5120, 5248, 5376
BLOCK = 128
N_BLOCKS = SEQ // BLOCK
HALO = 8
CHUNK = 16
N_CHUNKS = BLOCK // CHUNK
RMS_EPS = 1e-5
NEG = -1e30
SCALE = HEAD_DIM ** -0.5
SLOPES = tuple(2.0 ** (-8.0 * (h + 1) / N_Q_HEADS) for h in range(N_Q_HEADS))

ADAM_LR = 0.001
ADAM_B1 = 0.9
ADAM_B2 = 0.999
ADAM_EPS = 1e-08
ADAM_WD = 0.01
ADAM_STEP = 10

ROW_NORM_IN, ROW_NORM_CONV, ROW_NORM_ATTN, ROW_NORM_FINAL, ROW_CONV0, ROW_SINKS = 0, 1, 2, 3, 4, 7
LOSS_LANE = N_Q_HEADS
ACC_NORM_CONV, ACC_NORM_ATTN, ACC_CONV0, N_ACC = 0, 1, 2, 5

VMEM_LIMIT = 56 * 1024 * 1024

_NT = (((1,), (1,)), ((), ()))
_TN = (((0,), (0,)), ((), ()))


def _params(**kw):
    return pltpu.CompilerParams(vmem_limit_bytes=VMEM_LIMIT, **kw)


def _adamw(w, g, m, v):
    m = ADAM_B1 * m + (1.0 - ADAM_B1) * g
    v = ADAM_B2 * v + (1.0 - ADAM_B2) * (g * g)
    m_hat = m / (1.0 - ADAM_B1 ** ADAM_STEP)
    v_hat = v / (1.0 - ADAM_B2 ** ADAM_STEP)
    delta = -ADAM_LR * (m_hat / (jnp.sqrt(v_hat) + ADAM_EPS) + ADAM_WD * w)
    return delta, m, v


def _sigmoid(t):
    return 1.0 / (1.0 + jnp.exp(-t))


def _slot(px, py, pc):
    return 4 * px + 2 * py + pc


HALF_IN = SHARD_IN // 2
N_GATHER_KINDS = 13


def _all_gather(w_in_sh, w_out_sh, conv_sh):
    def body(win_ref, wout_ref, cv_ref, gin_ref, woutb_ref, conv_ref, gcv_ref, send_sems, recv_sems):
        x, y, c = lax.axis_index("x"), lax.axis_index("y"), lax.axis_index("c")
        me, sibling = (x, y, c), (x, y, 1 - c)
        nx, ny, dg = (1 - x, y, c), (x, 1 - y, c), (1 - x, 1 - y, c)

        def other(dev):
            return (dev[0], dev[1], 1 - dev[2])

        gin_ref[_slot(*me)] = win_ref[...].astype(BF16)
        gcv_ref[_slot(*me)] = cv_ref[...]

        def half(dev, h):
            return gin_ref.at[_slot(*dev), pl.ds(h * HALF_IN, HALF_IN), :]

        def rc(ref, k, to):
            return pltpu.make_async_remote_copy(src_ref=ref, dst_ref=ref, send_sem=send_sems.at[k],
                                                recv_sem=recv_sems.at[k], device_id=to, device_id_type=MESH)

        own = [rc(gin_ref.at[_slot(*me)], 0, sibling),
               rc(half(me, 0), 1, nx), rc(half(me, 1), 2, nx),
               rc(half(me, 1), 4, ny), rc(half(me, 0), 3, ny)]
        for cp in own:
            cp.start()
        def cv(k, dev, to):
            s = _slot(*dev)
            return pltpu.make_async_remote_copy(src_ref=gcv_ref.at[s], dst_ref=gcv_ref.at[s],
                                                send_sem=send_sems.at[N_GATHER_KINDS + k],
                                                recv_sem=recv_sems.at[N_GATHER_KINDS + k], device_id=to, device_id_type=MESH)

        small = [cv(0, me, sibling)] + [cv(1 + j, me, peer) for j, peer in enumerate((nx, ny, dg))]
        for cp in small:
            cp.start()
        woutb_ref[...] = wout_ref[...].astype(BF16)

        relayed = []
        for (dev, h), k_in, k_ici, k_d2d in (((nx, 0), 1, 5, 7), ((ny, 1), 4, 6, 10), ((nx, 1), 2, None, 8),
                                             ((ny, 0), 3, None, 9), ((dg, 0), 5, None, 11), ((dg, 1), 6, None, 12)):
            rc(half(dev, h), k_in, me).wait_recv()
            if k_ici is not None:
                relayed.append(rc(half(dev, h), k_ici, ny if dev is nx else nx))
                relayed[-1].start()
            relayed.append(rc(half(dev, h), k_d2d, sibling))
            relayed[-1].start()
        for j, peer in enumerate((nx, ny, dg)):
            cv(1 + j, peer, me).wait_recv()
            relayed.append(cv(4 + j, peer, sibling))
            relayed[-1].start()
        rc(gin_ref.at[_slot(*sibling)], 0, me).wait_recv()
        for (dev, h), k in (((nx, 0), 7), ((nx, 1), 8), ((ny, 0), 9), ((ny, 1), 10), ((dg, 0), 11), ((dg, 1), 12)):
            rc(half(other(dev), h), k, me).wait_recv()
        cv(0, sibling, me).wait_recv()
        for j, peer in enumerate((nx, ny, dg)):
            cv(4 + j, other(peer), me).wait_recv()
        for d in range(N_DEV):
            conv_ref[:, d * SHARD_CONV:(d + 1) * SHARD_CONV] = gcv_ref[d]
        for cp in own + small + relayed:
            cp.wait_send()

    vmem = pl.BlockSpec(memory_space=pltpu.VMEM)
    return pl.pallas_call(
        body, name="all_gather",
        out_shape=(jax.ShapeDtypeStruct((N_DEV, SHARD_IN, D_MODEL), BF16),
                   jax.ShapeDtypeStruct((SHARD_OUT, D_MODEL), BF16),
                   jax.ShapeDtypeStruct((8, D_CONV), F32)),
        in_specs=[vmem, vmem, vmem], out_specs=(vmem, vmem, vmem),
        scratch_shapes=[pltpu.VMEM((N_DEV, 8, SHARD_CONV), F32),
                        pltpu.SemaphoreType.DMA((N_GATHER_KINDS + 7,)), pltpu.SemaphoreType.DMA((N_GATHER_KINDS + 7,))],
        compiler_params=_params(),
    )(w_in_sh, w_out_sh, conv_sh)


def _shard_sum(src, own, d2d, ici, send_sems, recv_sems, local_sems, base=0):
    x, y, c = lax.axis_index("x"), lax.axis_index("y"), lax.axis_index("c")
    sibling = (x, y, 1 - c)
    chips = [(x, y), (1 - x, y), (x, 1 - y), (1 - x, 1 - y)]

    def rcopy(s, d, k, to):
        return pltpu.make_async_remote_copy(src_ref=s, dst_ref=d, send_sem=send_sems.at[base + k],
                                            recv_sem=recv_sems.at[base + k], device_id=to, device_id_type=MESH)

    def mine(k):
        return pltpu.make_async_copy(src.at[_slot(*chips[k], c)], own.at[k], local_sems.at[k])

    def to_sibling(k):
        return rcopy(src.at[_slot(*chips[k], 1 - c)], d2d.at[k], k, sibling)

    def to_chip(k):
        return rcopy(own.at[k], ici.at[k - 1], 3 + k, (*chips[k], c))

    def start():
        for k in range(4):
            mine(k).start()
            to_sibling(k).start()

    def forward():
        for k in range(1, 4):
            mine(k).wait()
            to_sibling(k).wait_recv()
            own[k] = (own[k].astype(F32) + d2d[k].astype(F32)).astype(BF16)
            to_chip(k).start()

    def finish():
        mine(0).wait()
        to_sibling(0).wait_recv()
        acc = own[0].astype(F32) + d2d[0].astype(F32)
        for k in range(1, 4):
            to_chip(k).wait_recv()
            acc = acc + ici[k - 1].astype(F32)
        for k in range(4):
            to_sibling(k).wait_send()
        for k in range(1, 4):
            to_chip(k).wait_send()
        return acc

    return start, forward, finish


def _shard_sum_scratch(rows):
    return [pltpu.VMEM((4, rows, D_MODEL), BF16), pltpu.VMEM((4, rows, D_MODEL), BF16),
            pltpu.VMEM((3, rows, D_MODEL), BF16)]


N_SHARD_SUM_SEMS = 7


def _chip_sum(dwt, d2d, out_hbm, send_sems, recv_sems, local_sems, base, local_base):
    x, y, c = lax.axis_index("x"), lax.axis_index("y"), lax.axis_index("c")
    sibling = (x, y, 1 - c)
    chips = [(x, y), (1 - x, y), (x, 1 - y), (1 - x, 1 - y)]

    def shard(s):
        return dwt.at[pl.ds(pl.multiple_of(s * SHARD_IN, 16), SHARD_IN), :]

    def to_sibling(k):
        return pltpu.make_async_remote_copy(src_ref=shard(_slot(*chips[k], 1 - c)), dst_ref=d2d.at[k],
                                            send_sem=send_sems.at[base + k], recv_sem=recv_sems.at[base + k],
                                            device_id=sibling, device_id_type=MESH)

    def save(k):
        return pltpu.make_async_copy(d2d.at[k], out_hbm.at[k], local_sems.at[local_base + k])

    def send(first):
        for k in range(4):
            in_first = _slot(*chips[k], 1 - c) < N_DEV // 2

            @pl.when(in_first if first else jnp.logical_not(in_first))
            def _():
                to_sibling(k).start()

    def finish():
        for k in range(4):
            to_sibling(k).wait_recv()
            d2d[k] = (shard(_slot(*chips[k], c))[...].astype(F32) + d2d[k].astype(F32)).astype(BF16)
            save(k).start()
        for k in range(4):
            save(k).wait()
            to_sibling(k).wait_send()

    return send, finish


N_ICI_SUM_SEMS = 6


def _ici_sum(src, own, ici, via, stage, send_sems, recv_sems, local_sems, base=0):
    x, y, c = lax.axis_index("x"), lax.axis_index("y"), lax.axis_index("c")
    nx, ny = (1 - x, y, c), (x, 1 - y, c)
    OWN, NX, NY, DG = range(4)

    def half(ref, h):
        return ref.at[pl.ds(h * HALF_IN, HALF_IN), :]

    def rc(s, d, k, to):
        return pltpu.make_async_remote_copy(src_ref=s, dst_ref=d, send_sem=send_sems.at[base + k],
                                            recv_sem=recv_sems.at[base + k], device_id=to, device_id_type=MESH)

    for_dg_0 = lambda: rc(half(src.at[DG], 0), via.at[0], 0, nx)
    for_dg_1 = lambda: rc(half(src.at[DG], 1), via.at[1], 1, ny)
    for_nx_0 = lambda: rc(half(src.at[NX], 0), half(ici.at[0], 0), 2, nx)
    for_ny_1 = lambda: rc(half(src.at[NY], 1), half(ici.at[1], 1), 3, ny)
    for_ny_0 = lambda: rc(stage.at[0], half(ici.at[1], 0), 4, ny)
    for_nx_1 = lambda: rc(stage.at[1], half(ici.at[0], 1), 5, nx)
    mine = lambda: pltpu.make_async_copy(src.at[OWN], own, local_sems.at[0])
    stage_0 = lambda: pltpu.make_async_copy(half(src.at[NY], 0), stage.at[0], local_sems.at[1])
    stage_1 = lambda: pltpu.make_async_copy(half(src.at[NX], 1), stage.at[1], local_sems.at[2])

    def start():
        for cp in (for_dg_0, for_dg_1, for_nx_0, for_ny_1, stage_0, stage_1, mine):
            cp().start()

    def relay():
        for h, staged, landed, out in ((0, stage_0, for_dg_0, for_ny_0), (1, stage_1, for_dg_1, for_nx_1)):
            staged().wait()
            landed().wait_recv()
            stage[h] = (stage[h].astype(F32) + via[h].astype(F32)).astype(BF16)
            out().start()

    def finish():
        mine().wait()
        for cp in (for_nx_0, for_nx_1, for_ny_1, for_ny_0):
            cp().wait_recv()
        acc = own[...].astype(F32) + ici[0].astype(F32) + ici[1].astype(F32)
        for cp in (for_dg_0, for_dg_1, for_nx_0, for_ny_1, for_ny_0, for_nx_1):
            cp().wait_send()
        return acc

    return start, relay, finish


def _slab_sum(myslab, slabs, send_sems, recv_sems, base):
    x, y, c = lax.axis_index("x"), lax.axis_index("y"), lax.axis_index("c")
    me = _slot(x, y, c)
    peers = [(x, y, 1 - c), (1 - x, y, c), (x, 1 - y, c), (1 - x, 1 - y, c),
             (1 - x, y, 1 - c), (x, 1 - y, 1 - c), (1 - x, 1 - y, 1 - c)]

    def cp(k):
        return pltpu.make_async_remote_copy(src_ref=myslab, dst_ref=slabs.at[me], send_sem=send_sems.at[base + k],
                                            recv_sem=recv_sems.at[base + k], device_id=peers[k], device_id_type=MESH)

    def start():
        slabs[me] = myslab[...]
        for k in range(7):
            cp(k).start()

    def finish():
        for k in range(7):
            cp(k).wait_recv()
        total = slabs[0]
        for d in range(1, N_DEV):
            total = total + slabs[d]
        for k in range(7):
            cp(k).wait_send()
        return total

    return start, finish


def _in_proj(x, norm_in, w_full, w_out_b):
    tm = 256
    steps = SEQ // tm
    forward_step = steps // 2 + 1

    def body(x_ref, g_ref, w_ref, wo_ref, h_ref, proj_ref, gout_ref, send_sems, recv_sems, local_sem):
        i = pl.program_id(0)
        x, y, c = lax.axis_index("x"), lax.axis_index("y"), lax.axis_index("c")
        me, sibling = (x, y, c), (x, y, 1 - c)
        chips = [(1 - x, y), (x, 1 - y), (1 - x, 1 - y)]

        def copy(k, block, to, src=None):
            rows = gout_ref.at[_slot(*block)]
            return pltpu.make_async_remote_copy(src_ref=rows if src is None else src, dst_ref=rows,
                                                send_sem=send_sems.at[k], recv_sem=recv_sems.at[k],
                                                device_id=to, device_id_type=MESH)

        mine = pltpu.make_async_copy(wo_ref, gout_ref.at[_slot(*me)], local_sem)

        @pl.when(i == 0)
        def _():
            mine.start()
            copy(0, me, sibling, src=wo_ref).start()
            for j, chip in enumerate(chips):
                copy(1 + j, me, (*chip, c), src=wo_ref).start()

        xv = x_ref[...]
        r = lax.rsqrt(jnp.mean(xv * xv, axis=-1, keepdims=True) + RMS_EPS)
        h = (xv * r * g_ref[...]).astype(BF16)
        h_ref[...] = h
        proj_ref[...] = lax.dot_general(h, w_ref[...], _NT, preferred_element_type=F32)

        @pl.when(i == forward_step)
        def _():
            for j, chip in enumerate(chips):
                copy(1 + j, (*chip, c), me).wait_recv()
                copy(4 + j, (*chip, c), sibling).start()

        @pl.when(i == steps - 1)
        def _():
            copy(0, sibling, me).wait_recv()
            for j, chip in enumerate(chips):
                copy(4 + j, (*chip, 1 - c), me).wait_recv()
            copy(0, me, sibling, src=wo_ref).wait_send()
            for j, chip in enumerate(chips):
                copy(1 + j, me, (*chip, c), src=wo_ref).wait_send()
                copy(4 + j, (*chip, c), sibling).wait_send()
            mine.wait()

    return pl.pallas_call(
        body, name="in_proj", grid=(steps,),
        in_specs=[pl.BlockSpec((tm, D_MODEL), lambda i: (i, 0)), pl.BlockSpec((1, D_MODEL), lambda i: (0, 0)),
                  pl.BlockSpec(memory_space=pltpu.VMEM), pl.BlockSpec(memory_space=pl.ANY)],
        out_specs=(pl.BlockSpec((tm, D_MODEL), lambda i: (i, 0)), pl.BlockSpec((tm, D_PROJ), lambda i: (i, 0)),
                   pl.BlockSpec(memory_space=pl.ANY)),
        out_shape=(jax.ShapeDtypeStruct((SEQ, D_MODEL), BF16), jax.ShapeDtypeStruct((SEQ, D_PROJ), F32),
                   jax.ShapeDtypeStruct((N_DEV, SHARD_OUT, D_MODEL), BF16)),
        scratch_shapes=[pltpu.SemaphoreType.DMA((7,)), pltpu.SemaphoreType.DMA((7,)), pltpu.SemaphoreType.DMA],
        compiler_params=_params(dimension_semantics=("arbitrary",)),
    )(x, norm_in, w_full, w_out_b)


def _chunk_rows(r):
    return pl.ds(pl.multiple_of(r * CHUNK, CHUNK), CHUNK)


def _conv_halo(cch_ref, cuh_ref, n):
    zh = jnp.where(n > 0, cch_ref[...] * cuh_ref[...], 0.0)
    return jnp.concatenate([zh] * (CHUNK // HALO), axis=0)


def _conv_chunk(pj_ref, zhalo, cw, r):
    rows = _chunk_rows(r)
    cc = pj_ref[rows, OFF_CC:OFF_CC + D_CONV]
    cu = pj_ref[rows, OFF_CU:OFF_CU + D_CONV]
    z = cc * cu
    before = _chunk_rows(jnp.maximum(r - 1, 0))
    zprev = jnp.where(r > 0, pj_ref[before, OFF_CC:OFF_CC + D_CONV] * pj_ref[before, OFF_CU:OFF_CU + D_CONV], zhalo)
    row = lax.broadcasted_iota(jnp.int32, (CHUNK, D_CONV), 0)
    z1 = jnp.where(row < 1, pltpu.roll(zprev, 1, 0), pltpu.roll(z, 1, 0))
    z2 = jnp.where(row < 2, pltpu.roll(zprev, 2, 0), pltpu.roll(z, 2, 0))
    co = cw[0] * z2 + cw[1] * z1 + cw[2] * z
    return cc, cu, z, z1, z2, co


def _gated_norm(a, gain, t):
    r = lax.rsqrt(jnp.mean(a * a, axis=-1, keepdims=True) + RMS_EPS)
    return a * r * gain * (t * _sigmoid(t))


def _kv_bands(pj, kvp_ref):
    lane = lax.broadcasted_iota(jnp.int32, (2 * BLOCK, D_KV), 1)
    lo = lane < HEAD_DIM

    def bands(prev, cur):
        b = jnp.concatenate([prev, cur], axis=0)
        br = pltpu.roll(b, HEAD_DIM, 1)
        zero = jnp.zeros_like(b)
        return ((jnp.where(lo, b, zero).astype(BF16), jnp.where(lo, zero, br).astype(BF16)),
                (jnp.where(lo, br, zero).astype(BF16), jnp.where(lo, zero, b).astype(BF16)))

    ks = bands(kvp_ref[:, 0:D_KV], pj[:, OFF_K:OFF_K + D_KV])
    vs = bands(kvp_ref[:, D_KV:2 * D_KV], pj[:, OFF_V:OFF_V + D_KV])
    return ks, vs


STACK = PAIRS_PER_KV * BLOCK


def _head(j, i, e):
    return 2 * (PAIRS_PER_KV * j + i) + e


def _pair_cols(j, i, off):
    p = PAIRS_PER_KV * j + i
    return slice(off + 128 * p, off + 128 * (p + 1))


def _fill_attn_bias(bias_scr, first_block):
    qi = lax.broadcasted_iota(jnp.int32, (BLOCK, 2 * BLOCK), 0)
    kj = lax.broadcasted_iota(jnp.int32, (BLOCK, 2 * BLOCK), 1)
    dist = BLOCK + qi - kj
    valid = (dist >= 0) & (dist < BLOCK)
    if first_block:
        valid = valid & (kj >= BLOCK)
    distf = dist.astype(F32)
    for j in range(2):
        for e in range(2):
            for i in range(PAIRS_PER_KV):
                bias_scr[2 * j + e, BLOCK * i:BLOCK * (i + 1), :] = jnp.where(valid, -SLOPES[_head(j, i, e)] * distf, NEG)


def _q_stack(pj, j):
    return jnp.concatenate([(pj[:, _pair_cols(j, i, OFF_Q)] * SCALE).astype(BF16) for i in range(PAIRS_PER_KV)], axis=0)


def _sink_rows(sink_ref, j, e):
    return jnp.concatenate([jnp.full((BLOCK, 128), sink_ref[_head(j, i, e)], F32) for i in range(PAIRS_PER_KV)], axis=0)


def _attn_probs(q_stack, kband, bias, sink):
    s = lax.dot_general(q_stack, kband, _NT, preferred_element_type=F32) + bias
    m = jnp.broadcast_to(jnp.max(s, axis=-1, keepdims=True), (STACK, 128))
    m = jnp.maximum(m, sink)
    p = [jnp.exp(t - m) for t in (s[:, :128], s[:, 128:])]
    es = jnp.exp(sink - m)
    ones = jnp.ones((128, 128), BF16)
    total = (jnp.dot(p[0].astype(BF16), ones, preferred_element_type=F32)
             + jnp.dot(p[1].astype(BF16), ones, preferred_element_type=F32))
    inv = 1.0 / (total + es)
    return jnp.concatenate([p[0] * inv, p[1] * inv], axis=1), es * inv


def _attn_group(pj, ks, vs, bias_scr, sink_ref, j):
    q_stack = _q_stack(pj, j)
    out, probs, shares = None, [], []
    for e in range(2):
        p, ps = _attn_probs(q_stack, ks[j][e], bias_scr[2 * j + e], _sink_rows(sink_ref, j, e))
        p = p.astype(BF16)
        o = jnp.dot(p, vs[j][e], preferred_element_type=F32)
        out = o if out is None else out + o
        probs.append(p)
        shares.append(ps)
    return out, probs, shares


def _mix_fwd(proj, conv_full, sinks, norm_conv, norm_attn):
    def body(pj_ref, kvp_ref, cch_ref, cuh_ref, cw_ref, sink_ref, gc_ref, ga_ref, mixed_ref, attn_scr, p_ref, ps_ref,
             bias_scr):
        n = pl.program_id(0)
        pj = pj_ref

        @pl.when(n == 0)
        def _():
            _fill_attn_bias(bias_scr, first_block=True)

        @pl.when(n == 1)
        def _():
            _fill_attn_bias(bias_scr, first_block=False)

        zhalo = _conv_halo(cch_ref, cuh_ref, n)
        cw = (cw_ref[0:1, :], cw_ref[1:2, :], cw_ref[2:3, :])
        gain_c = gc_ref[...]

        def conv_chunk(r, carry):
            rows = _chunk_rows(r)
            co = _conv_chunk(pj_ref, zhalo, cw, r)[-1]
            y = _gated_norm(pj_ref[rows, OFF_CB:OFF_CB + D_CONV] * co, gain_c, pj_ref[rows, OFF_GC:OFF_GC + D_CONV])
            mixed_ref[rows, 0:D_CONV] = y.astype(BF16)
            return carry

        lax.fori_loop(0, N_CHUNKS, conv_chunk, 0, unroll=True)

        ks, vs = _kv_bands(pj, kvp_ref)
        for j in range(2):
            out, probs, shares = _attn_group(pj, ks, vs, bias_scr, sink_ref, j)
            for e in range(2):
                p_ref[0, 2 * j + e] = probs[e]
                ps_ref[0, 2 * j + e] = shares[e]
            for i in range(PAIRS_PER_KV):
                attn_scr[:, _pair_cols(j, i, 0)] = out[BLOCK * i:BLOCK * (i + 1), :]
        gain_a = ga_ref[...]

        def norm_chunk(r, carry):
            rows = _chunk_rows(r)
            y = _gated_norm(attn_scr[rows, :], gain_a, pj_ref[rows, OFF_GA:OFF_GA + D_ATTN])
            mixed_ref[rows, D_CONV:D_MIX] = y.astype(BF16)
            return carry

        lax.fori_loop(0, N_CHUNKS, norm_chunk, 0, unroll=True)

    per_block = BLOCK // HALO
    return pl.pallas_call(
        body, name="mix_fwd", grid=(N_BLOCKS,),
        in_specs=[
            pl.BlockSpec((BLOCK, D_PROJ), lambda n: (n, 0)),
            pl.BlockSpec((BLOCK, 2 * D_KV), lambda n: (jnp.maximum(n - 1, 0), OFF_K // (2 * D_KV))),
            pl.BlockSpec((HALO, D_CONV), lambda n: (jnp.maximum(n * per_block - 1, 0), OFF_CC // D_CONV)),
            pl.BlockSpec((HALO, D_CONV), lambda n: (jnp.maximum(n * per_block - 1, 0), OFF_CU // D_CONV)),
            pl.BlockSpec((8, D_CONV), lambda n: (0, 0)),
            pl.BlockSpec(memory_space=pltpu.SMEM),
            pl.BlockSpec((1, D_CONV), lambda n: (0, 0)),
            pl.BlockSpec((1, D_ATTN), lambda n: (0, 0)),
        ],
        out_specs=(pl.BlockSpec((BLOCK, D_MIX), lambda n: (n, 0)), pl.BlockSpec((BLOCK, D_ATTN), lambda n: (n, 0)),
                   pl.BlockSpec((1, 4, STACK, 2 * BLOCK), lambda n: (n, 0, 0, 0)),
                   pl.BlockSpec((1, 4, STACK, 128), lambda n: (n, 0, 0, 0))),
        out_shape=(jax.ShapeDtypeStruct((SEQ, D_MIX), BF16), jax.ShapeDtypeStruct((SEQ, D_ATTN), F32),
                   jax.ShapeDtypeStruct((N_BLOCKS, 4, STACK, 2 * BLOCK), BF16),
                   jax.ShapeDtypeStruct((N_BLOCKS, 4, STACK, 128), F32)),
        scratch_shapes=[pltpu.VMEM((4, STACK, 2 * BLOCK), F32)],
        compiler_params=_params(dimension_semantics=("arbitrary",)),
    )(proj, proj, proj, proj, conv_full, sinks, norm_conv, norm_attn)


def _out_proj_loss(mixed, x, target, w_out_full, norm_final):
    tm = 256

    def body(mx_ref, x_ref, t_ref, w_ref, g_ref, dx2_ref, dx2b_ref, dmix_ref, gnf_ref, loss_ref):
        i = pl.program_id(0)
        w = w_ref[...]
        x2 = x_ref[...] + jnp.dot(mx_ref[...], w, preferred_element_type=F32)
        r = lax.rsqrt(jnp.mean(x2 * x2, axis=-1, keepdims=True) + RMS_EPS)
        xn = x2 * r
        g = g_ref[...]
        err = xn * g - t_ref[...]
        part = 0.5 * jnp.sum(jnp.mean(err * err, axis=-1, keepdims=True), axis=0, keepdims=True)
        dy = err * (1.0 / D_MODEL)
        gnf = jnp.sum(dy * xn, axis=0, keepdims=True)
        u = dy * g
        dx2 = r * (u - xn * jnp.mean(u * xn, axis=-1, keepdims=True))
        dx2_ref[...] = dx2
        dx2b = dx2.astype(BF16)
        dx2b_ref[...] = dx2b
        dmix_ref[...] = lax.dot_general(dx2b, w, _NT, preferred_element_type=F32)

        @pl.when(i == 0)
        def _():
            gnf_ref[...] = jnp.zeros_like(gnf_ref)
            loss_ref[...] = jnp.zeros_like(loss_ref)

        gnf_ref[...] += gnf
        loss_ref[...] += jnp.broadcast_to(part, loss_ref.shape)

    return pl.pallas_call(
        body, name="out_proj_loss", grid=(SEQ // tm,),
        in_specs=[pl.BlockSpec((tm, D_MIX), lambda i: (i, 0)), pl.BlockSpec((tm, D_MODEL), lambda i: (i, 0)),
                  pl.BlockSpec((tm, D_MODEL), lambda i: (i, 0)), pl.BlockSpec(memory_space=pltpu.VMEM),
                  pl.BlockSpec((1, D_MODEL), lambda i: (0, 0))],
        out_specs=(pl.BlockSpec((tm, D_MODEL), lambda i: (i, 0)), pl.BlockSpec((tm, D_MODEL), lambda i: (i, 0)),
                   pl.BlockSpec((tm, D_MIX), lambda i: (i, 0)),
                   pl.BlockSpec((1, D_MODEL), lambda i: (0, 0)), pl.BlockSpec((8, 128), lambda i: (0, 0))),
        out_shape=(jax.ShapeDtypeStruct((SEQ, D_MODEL), F32), jax.ShapeDtypeStruct((SEQ, D_MODEL), BF16),
                   jax.ShapeDtypeStruct((SEQ, D_MIX), F32),
                   jax.ShapeDtypeStruct((1, D_MODEL), F32), jax.ShapeDtypeStruct((8, 128), F32)),
        compiler_params=_params(dimension_semantics=("arbitrary",)),
    )(mixed, x, target, w_out_full, norm_final)


def _gated_norm_bwd(a, gain, t, dy):
    r = lax.rsqrt(jnp.mean(a * a, axis=-1, keepdims=True) + RMS_EPS)
    an = a * r
    sg = _sigmoid(t)
    dn = dy * (t * sg)
    dt = dy * (an * gain) * (sg * (1.0 + t * (1.0 - sg)))
    u = dn * gain
    da = r * (u - an * jnp.mean(u * an, axis=-1, keepdims=True))
    return da, dt, dn * an


def _mix_bwd(proj, dmixed, attn, probs, shares, conv_full, norm_conv, norm_attn):
    def body(pj_ref, kvp_ref, cch_ref, cuh_ref, dmx_ref, attn_ref, p_ref, ps_ref, cw_ref, gc_ref, ga_ref,
             dpj_ref, gslab_ref, dattn_scr, nxt_scr, dkv_scr, acc_scr):
        step = pl.program_id(0)
        n = N_BLOCKS - 1 - step
        pj = pj_ref

        @pl.when(step == 0)
        def _():
            gslab_ref[...] = jnp.zeros_like(gslab_ref)
            nxt_scr[...] = jnp.zeros_like(nxt_scr)
            dkv_scr[...] = jnp.zeros_like(dkv_scr)
            acc_scr[...] = jnp.zeros_like(acc_scr)

        zhalo = _conv_halo(cch_ref, cuh_ref, n)
        cw = (cw_ref[0:1, :], cw_ref[1:2, :], cw_ref[2:3, :])
        gain_c = gc_ref[...]
        row = lax.broadcasted_iota(jnp.int32, (CHUNK, D_CONV), 0)

        def conv_chunk(t, dco_after):
            r = N_CHUNKS - 1 - t
            rows = _chunk_rows(r)
            cc, cu, z, z1, z2, co = _conv_chunk(pj_ref, zhalo, cw, r)
            cb = pj_ref[rows, OFF_CB:OFF_CB + D_CONV]
            da, dgate, gterm = _gated_norm_bwd(cb * co, gain_c, pj_ref[rows, OFF_GC:OFF_GC + D_CONV],
                                               dmx_ref[rows, 0:D_CONV])
            dpj_ref[rows, OFF_GC:OFF_GC + D_CONV] = dgate.astype(BF16)
            dpj_ref[rows, OFF_CB:OFF_CB + D_CONV] = (da * co).astype(BF16)
            dco = da * cb
            dco1 = jnp.where(row >= CHUNK - 1, pltpu.roll(dco_after, CHUNK - 1, 0), pltpu.roll(dco, CHUNK - 1, 0))
            dco2 = jnp.where(row >= CHUNK - 2, pltpu.roll(dco_after, CHUNK - 2, 0), pltpu.roll(dco, CHUNK - 2, 0))
            dz = cw[2] * dco + cw[1] * dco1 + cw[0] * dco2
            dpj_ref[rows, OFF_CC:OFF_CC + D_CONV] = (dz * cu).astype(BF16)
            dpj_ref[rows, OFF_CU:OFF_CU + D_CONV] = (dz * cc).astype(BF16)
            acc_scr[ACC_NORM_CONV] += gterm
            acc_scr[ACC_CONV0] += dco * z2
            acc_scr[ACC_CONV0 + 1] += dco * z1
            acc_scr[ACC_CONV0 + 2] += dco * z
            return dco

        nxt_scr[...] = lax.fori_loop(0, N_CHUNKS, conv_chunk, nxt_scr[...], unroll=True)

        ks, vs = _kv_bands(pj, kvp_ref)
        gain_a = ga_ref[...]

        def norm_chunk(r, carry):
            rows = _chunk_rows(r)
            da, dgate, gterm = _gated_norm_bwd(attn_ref[rows, :], gain_a, pj_ref[rows, OFF_GA:OFF_GA + D_ATTN],
                                               dmx_ref[rows, D_CONV:D_MIX])
            dpj_ref[rows, OFF_GA:OFF_GA + D_ATTN] = dgate.astype(BF16)
            dattn_scr[rows, :] = da
            acc_scr[ACC_NORM_ATTN] += gterm
            return carry

        lax.fori_loop(0, N_CHUNKS, norm_chunk, 0, unroll=True)

        in_lo = lax.broadcasted_iota(jnp.int32, (128, 128), 0) < HEAD_DIM
        half_ones = (jnp.where(in_lo, 1.0, 0.0).astype(BF16), jnp.where(in_lo, 0.0, 1.0).astype(BF16))
        lane_s = lax.broadcasted_iota(jnp.int32, (1, D_MODEL), 1)
        gsink = jnp.zeros((1, D_MODEL), F32)
        dk_t, dv_t = [], []
        for j in range(2):
            q_stack = _q_stack(pj, j)
            do_f = jnp.concatenate([dattn_scr[:, _pair_cols(j, i, 0)] for i in range(PAIRS_PER_KV)], axis=0)
            o_f = jnp.concatenate([attn_ref[:, _pair_cols(j, i, 0)] for i in range(PAIRS_PER_KV)], axis=0)
            prod = (do_f * o_f).astype(BF16)
            deltas = [jnp.dot(prod, half_ones[e], preferred_element_type=F32) for e in range(2)]
            do_b = do_f.astype(BF16)
            dq, dk_j, dv_j = None, None, None
            for e in range(2):
                p = p_ref[0, 2 * j + e]
                dp = lax.dot_general(do_b, vs[j][e], _NT, preferred_element_type=F32)
                delta = jnp.concatenate([deltas[e], deltas[e]], axis=1)
                ds = (p.astype(F32) * (dp - delta)).astype(BF16)
                gs = ps_ref[0, 2 * j + e] * deltas[e]
                for i in range(PAIRS_PER_KV):
                    gs_h = -jnp.sum(gs[BLOCK * i:BLOCK * (i + 1), 0:1], axis=0, keepdims=True)
                    gsink = gsink + jnp.where(lane_s == _head(j, i, e), gs_h, 0.0)
                t = jnp.dot(ds, ks[j][e], preferred_element_type=F32)
                dq = t if dq is None else dq + t
                half = slice(HEAD_DIM * e, HEAD_DIM * (e + 1))
                a = lax.dot_general(q_stack, ds, _TN, preferred_element_type=F32)[half, :]
                b = lax.dot_general(do_b, p, _TN, preferred_element_type=F32)[half, :]
                dk_j = a if dk_j is None else dk_j + a
                dv_j = b if dv_j is None else dv_j + b
            for i in range(PAIRS_PER_KV):
                dpj_ref[:, _pair_cols(j, i, OFF_Q)] = (dq[BLOCK * i:BLOCK * (i + 1), :] * SCALE).astype(BF16)
            dk_t.append(dk_j)
            dv_t.append(dv_j)
        dk = jnp.concatenate(dk_t, axis=0).T
        dv = jnp.concatenate(dv_t, axis=0).T
        dpj_ref[:, OFF_K:OFF_K + D_KV] = (dk[BLOCK:, :] + dkv_scr[:, 0:D_KV]).astype(BF16)
        dpj_ref[:, OFF_V:OFF_V + D_KV] = (dv[BLOCK:, :] + dkv_scr[:, D_KV:2 * D_KV]).astype(BF16)
        dkv_scr[:, 0:D_KV] = dk[:BLOCK, :]
        dkv_scr[:, D_KV:2 * D_KV] = dv[:BLOCK, :]
        gslab_ref[ROW_SINKS:ROW_SINKS + 1, :] += gsink

        @pl.when(step == N_BLOCKS - 1)
        def _():
            for k, slab_row in ((ACC_NORM_CONV, ROW_NORM_CONV), (ACC_NORM_ATTN, ROW_NORM_ATTN), (ACC_CONV0, ROW_CONV0),
                                (ACC_CONV0 + 1, ROW_CONV0 + 1), (ACC_CONV0 + 2, ROW_CONV0 + 2)):
                gslab_ref[slab_row:slab_row + 1, :] = jnp.sum(acc_scr[k], axis=0, keepdims=True)

    per_block = BLOCK // HALO
    last = N_BLOCKS - 1
    return pl.pallas_call(
        body, name="mix_bwd", grid=(N_BLOCKS,),
        in_specs=[
            pl.BlockSpec((BLOCK, D_PROJ), lambda s: (last - s, 0)),
            pl.BlockSpec((BLOCK, 2 * D_KV), lambda s: (jnp.maximum(last - s - 1, 0), OFF_K // (2 * D_KV))),
            pl.BlockSpec((HALO, D_CONV), lambda s: (jnp.maximum((last - s) * per_block - 1, 0), OFF_CC // D_CONV)),
            pl.BlockSpec((HALO, D_CONV), lambda s: (jnp.maximum((last - s) * per_block - 1, 0), OFF_CU // D_CONV)),
            pl.BlockSpec((BLOCK, D_MIX), lambda s: (last - s, 0)),
            pl.BlockSpec((BLOCK, D_ATTN), lambda s: (last - s, 0)),
            pl.BlockSpec((1, 4, STACK, 2 * BLOCK), lambda s: (last - s, 0, 0, 0)),
            pl.BlockSpec((1, 4, STACK, 128), lambda s: (last - s, 0, 0, 0)),
            pl.BlockSpec((8, D_CONV), lambda s: (0, 0)),
            pl.BlockSpec((1, D_CONV), lambda s: (0, 0)),
            pl.BlockSpec((1, D_ATTN), lambda s: (0, 0)),
        ],
        out_specs=(pl.BlockSpec((BLOCK, D_PROJ), lambda s: (last - s, 0)),
                   pl.BlockSpec((8, D_MODEL), lambda s: (0, 0))),
        out_shape=(jax.ShapeDtypeStruct((SEQ, D_PROJ), BF16), jax.ShapeDtypeStruct((8, D_MODEL), F32)),
        scratch_shapes=[pltpu.VMEM((BLOCK, D_ATTN), F32), pltpu.VMEM((CHUNK, D_CONV), F32),
                        pltpu.VMEM((BLOCK, 2 * D_KV), F32), pltpu.VMEM((N_ACC, CHUNK, D_MODEL), F32)],
        compiler_params=_params(dimension_semantics=("arbitrary",)),
    )(proj, proj, proj, proj, dmixed, attn, probs, shares, conv_full, norm_conv, norm_attn)


def _in_bwd_rs(dproj, w_full, x, dx2, norm_in, dw_in_chip, gslab, gnf, loss_part):
    tm = 256
    steps = SEQ // tm
    relay_step = 4

    def body(dp_ref, w_ref, x_ref, dx2_ref, g_ref, dwi_ref, gs_ref, gnf_ref, lp_ref, gx_ref, gwin_ref, gsum_ref,
             gni_scr, own, ici, via, stage, myslab, slabs, send_sems, recv_sems, local_sems):
        i = pl.program_id(0)
        rs_start, rs_relay, rs_finish = _ici_sum(dwi_ref, own, ici, via, stage, send_sems, recv_sems, local_sems)
        slab_start, slab_finish = _slab_sum(myslab, slabs, send_sems, recv_sems, N_ICI_SUM_SEMS)

        @pl.when(i == 0)
        def _():
            gni_scr[...] = jnp.zeros_like(gni_scr)
            rs_start()

        dh = jnp.dot(dp_ref[...], w_ref[...], preferred_element_type=F32)
        xv = x_ref[...]
        r = lax.rsqrt(jnp.mean(xv * xv, axis=-1, keepdims=True) + RMS_EPS)
        xn = xv * r
        u = dh * g_ref[...]
        gx_ref[...] = dx2_ref[...] + r * (u - xn * jnp.mean(u * xn, axis=-1, keepdims=True))
        gni_scr[...] += jnp.sum(dh * xn, axis=0, keepdims=True)

        @pl.when(i == relay_step)
        def _():
            rs_relay()

        @pl.when(i == steps - 1)
        def _():
            row = lax.broadcasted_iota(jnp.int32, (8, D_MODEL), 0)
            lane = lax.broadcasted_iota(jnp.int32, (8, D_MODEL), 1)
            slab = jnp.where(row == ROW_NORM_IN, gni_scr[...], jnp.where(row == ROW_NORM_FINAL, gnf_ref[...], gs_ref[...]))
            myslab[...] = jnp.where((row == ROW_SINKS) & (lane == LOSS_LANE), lp_ref[0:1, 0:1], slab)
            slab_start()
            gwin_ref[...] = rs_finish()
            gsum_ref[...] = slab_finish()

    const = lambda i: (0, 0)
    return pl.pallas_call(
        body, name="in_bwd", grid=(steps,),
        in_specs=[pl.BlockSpec((tm, D_PROJ), lambda i: (i, 0)), pl.BlockSpec(memory_space=pltpu.VMEM),
                  pl.BlockSpec((tm, D_MODEL), lambda i: (i, 0)), pl.BlockSpec((tm, D_MODEL), lambda i: (i, 0)),
                  pl.BlockSpec((1, D_MODEL), const), pl.BlockSpec(memory_space=pl.ANY),
                  pl.BlockSpec((8, D_MODEL), const), pl.BlockSpec((1, D_MODEL), const), pl.BlockSpec((8, 128), const)],
        out_specs=(pl.BlockSpec((tm, D_MODEL), lambda i: (i, 0)), pl.BlockSpec((SHARD_IN, D_MODEL), const),
                   pl.BlockSpec((8, D_MODEL), const)),
        out_shape=(jax.ShapeDtypeStruct((SEQ, D_MODEL), F32), jax.ShapeDtypeStruct((SHARD_IN, D_MODEL), F32),
                   jax.ShapeDtypeStruct((8, D_MODEL), F32)),
        scratch_shapes=[pltpu.VMEM((1, D_MODEL), F32), pltpu.VMEM((SHARD_IN, D_MODEL), BF16),
                        pltpu.VMEM((2, SHARD_IN, D_MODEL), BF16), pltpu.VMEM((2, HALF_IN, D_MODEL), BF16),
                        pltpu.VMEM((2, HALF_IN, D_MODEL), BF16),
                        pltpu.VMEM((8, D_MODEL), F32), pltpu.VMEM((N_DEV, 8, D_MODEL), F32),
                        pltpu.SemaphoreType.DMA((N_ICI_SUM_SEMS + 7,)), pltpu.SemaphoreType.DMA((N_ICI_SUM_SEMS + 7,)),
                        pltpu.SemaphoreType.DMA((3,))],
        compiler_params=_params(dimension_semantics=("arbitrary",)),
    )(dproj, w_full, x, dx2, norm_in, dw_in_chip, gslab, gnf, loss_part)


def _dw_in_rs(dproj, h, dw_out_sh):
    tn = 640
    steps = D_PROJ // tn
    forward_step = 2
    half_step = (D_PROJ // 2) // tn

    def body(a_ref, b_ref, dwo_ref, chip_ref, gwo_ref, dwt, d2d_in, own, d2d, ici, send_sems, recv_sems, local_sems):
        i = pl.program_id(0)
        rs_start, rs_forward, rs_finish = _shard_sum(dwo_ref, own, d2d, ici, send_sems, recv_sems, local_sems)
        pair_send, pair_finish = _chip_sum(dwt, d2d_in, chip_ref, send_sems, recv_sems, local_sems,
                                           N_SHARD_SUM_SEMS, 4)

        @pl.when(i == 0)
        def _():
            rs_start()

        @pl.when(i == half_step)
        def _():
            pair_send(first=True)

        tile = lax.dot_general(a_ref[...], b_ref[...], _TN, preferred_element_type=F32).astype(BF16)
        dwt[pl.ds(pl.multiple_of(i * tn, tn), tn), :] = tile

        @pl.when(i == forward_step)
        def _():
            rs_forward()

        @pl.when(i == steps - 1)
        def _():
            pair_send(first=False)
            gwo_ref[...] = rs_finish()
            pair_finish()

    return pl.pallas_call(
        body, name="dw_in", grid=(steps,),
        in_specs=[pl.BlockSpec((SEQ, tn), lambda i: (0, i)), pl.BlockSpec(memory_space=pltpu.VMEM),
                  pl.BlockSpec(memory_space=pl.ANY)],
        out_specs=(pl.BlockSpec(memory_space=pl.ANY), pl.BlockSpec((SHARD_OUT, D_MODEL), lambda i: (0, 0))),
        out_shape=(jax.ShapeDtypeStruct((4, SHARD_IN, D_MODEL), BF16), jax.ShapeDtypeStruct((SHARD_OUT, D_MODEL), F32)),
        scratch_shapes=[pltpu.VMEM((D_PROJ, D_MODEL), BF16), pltpu.VMEM((4, SHARD_IN, D_MODEL), BF16),
                        *_shard_sum_scratch(SHARD_OUT),
                        pltpu.SemaphoreType.DMA((N_SHARD_SUM_SEMS + 4,)), pltpu.SemaphoreType.DMA((N_SHARD_SUM_SEMS + 4,)),
                        pltpu.SemaphoreType.DMA((8,))],
        compiler_params=_params(dimension_semantics=("arbitrary",)),
    )(dproj, h, dw_out_sh)


def _matmul_tn(a, b, tn, name):
    k, n = a.shape
    _, m = b.shape

    def body(a_ref, b_ref, o_ref):
        o_ref[...] = lax.dot_general(a_ref[...], b_ref[...], _TN, preferred_element_type=F32).astype(BF16)

    return pl.pallas_call(
        body, name=name, grid=(n // tn,),
        in_specs=[pl.BlockSpec((k, tn), lambda i: (0, i)), pl.BlockSpec(memory_space=pltpu.VMEM)],
        out_specs=pl.BlockSpec((tn, m), lambda i: (i, 0)),
        out_shape=jax.ShapeDtypeStruct((n, m), BF16),
        compiler_params=_params(dimension_semantics=("arbitrary",)),
    )(a, b)


def _adam_all(big_in, big_out, gsum, small):
    steps = 4
    tr_in, tr_out = SHARD_IN // steps, SHARD_OUT // steps

    def body(*refs):
        ins, outs = refs[:8 + 1 + 18], refs[8 + 1 + 18:]
        i = pl.program_id(0)
        for b in range(2):
            w_ref, g_ref, m_ref, v_ref = ins[4 * b:4 * b + 4]
            g = g_ref[...]
            delta, mn, vn = _adamw(w_ref[...], g, m_ref[...], v_ref[...])
            for ref, val in zip(outs[4 * b:4 * b + 4], (g, delta, mn, vn)):
                ref[...] = val

        @pl.when(i == 0)
        def _():
            gsum = ins[8][...]
            idx = _slot(lax.axis_index("x"), lax.axis_index("y"), lax.axis_index("c"))
            cg = jnp.zeros((3, SHARD_CONV), F32)
            for d in range(N_DEV):
                cg = jnp.where(idx == d, gsum[ROW_CONV0:ROW_CONV0 + 3, d * SHARD_CONV:(d + 1) * SHARD_CONV], cg)
            grads = (gsum[ROW_NORM_IN:ROW_NORM_IN + 1], gsum[ROW_SINKS:ROW_SINKS + 1, 0:N_Q_HEADS],
                     gsum[ROW_NORM_CONV:ROW_NORM_CONV + 1], gsum[ROW_NORM_ATTN:ROW_NORM_ATTN + 1],
                     gsum[ROW_NORM_FINAL:ROW_NORM_FINAL + 1], cg)
            for s, g in enumerate(grads):
                w_ref, m_ref, v_ref = ins[9 + 3 * s:12 + 3 * s]
                delta, mn, vn = _adamw(w_ref[...], g, m_ref[...], v_ref[...])
                for ref, val in zip(outs[8 + 4 * s:12 + 4 * s], (g, delta, mn, vn)):
                    ref[...] = val
            outs[32][...] = gsum[ROW_SINKS:ROW_SINKS + 1, LOSS_LANE:LOSS_LANE + 1]

    const = lambda i: (0, 0)
    rows = lambda i: (i, 0)
    small_shapes = [a.shape for a in small[::3]]
    in_specs = ([pl.BlockSpec((tr_in, D_MODEL), rows)] * 4 + [pl.BlockSpec((tr_out, D_MODEL), rows)] * 4
                + [pl.BlockSpec((8, D_MODEL), const)] + [pl.BlockSpec(a.shape, const) for a in small])
    out_specs = ([pl.BlockSpec((tr_in, D_MODEL), rows)] * 4 + [pl.BlockSpec((tr_out, D_MODEL), rows)] * 4
                 + [pl.BlockSpec(s, const) for s in small_shapes for _ in range(4)] + [pl.BlockSpec((1, 1), const)])
    out_shape = ([jax.ShapeDtypeStruct((SHARD_IN, D_MODEL), F32)] * 4 + [jax.ShapeDtypeStruct((SHARD_OUT, D_MODEL), F32)] * 4
                 + [jax.ShapeDtypeStruct(s, F32) for s in small_shapes for _ in range(4)]
                 + [jax.ShapeDtypeStruct((1, 1), F32)])
    outs = pl.pallas_call(
        body, name="adam", grid=(steps,), in_specs=in_specs, out_specs=tuple(out_specs), out_shape=tuple(out_shape),
        compiler_params=_params(dimension_semantics=("arbitrary",)),
    )(*big_in, *big_out, gsum, *small)
    return outs[0:4], outs[4:8], [outs[8 + 4 * s:12 + 4 * s] for s in range(6)], outs[32]


def _pad_rows(a, rows=8):
    return jnp.pad(a, ((0, rows - a.shape[0]), (0, 0)))


def kernel(x, norm_in, w_in, conv_w, attn_sinks, norm_conv_out, norm_attn_out, w_out, norm_final, loss_target, m_norm_in, m_w_in, m_conv_w, m_attn_sinks, m_norm_conv_out, m_norm_attn_out, m_w_out, m_norm_final, v_norm_in, v_w_in, v_conv_w, v_attn_sinks, v_norm_conv_out, v_norm_attn_out, v_w_out, v_norm_final):
    x2d = x.reshape(SEQ, D_MODEL)
    target = loss_target.reshape(SEQ, D_MODEL)
    nf = norm_final.reshape(1, D_MODEL)

    w_in_t, m_w_in_t, v_w_in_t = w_in[0].T, m_w_in[0].T, v_w_in[0].T
    g_in, w_out_b, conv_full = _all_gather(w_in_t, w_out[0], _pad_rows(conv_w[0]))
    w_in_full = g_in.reshape(D_PROJ, D_MODEL)
    sinks = attn_sinks.reshape(N_Q_HEADS)

    h, proj, g_out = _in_proj(x2d, norm_in, w_in_full, w_out_b)
    w_out_full = g_out.reshape(D_MIX, D_MODEL)
    mixed, attn, probs, shares = _mix_fwd(proj, conv_full, sinks, norm_conv_out, norm_attn_out)
    dx2, dx2b, dmixed, gnf, loss_part = _out_proj_loss(mixed, x2d, target, w_out_full, nf)
    dproj, gslab = _mix_bwd(proj, dmixed, attn, probs, shares, conv_full, norm_conv_out, norm_attn_out)
    dw_out = _matmul_tn(mixed, dx2b, 512, "dw_out")
    dw_in_chip, g_w_out = _dw_in_rs(dproj, h, dw_out.reshape(N_DEV, SHARD_OUT, D_MODEL))
    grad_x, g_w_in, gsum = _in_bwd_rs(dproj, w_in_full, x2d, dx2, norm_in, dw_in_chip, gslab, gnf, loss_part)

    small = (norm_in, m_norm_in, v_norm_in, attn_sinks, m_attn_sinks, v_attn_sinks,
             norm_conv_out, m_norm_conv_out, v_norm_conv_out, norm_attn_out, m_norm_attn_out, v_norm_attn_out,
             nf, m_norm_final.reshape(1, D_MODEL), v_norm_final.reshape(1, D_MODEL),
             conv_w[0], m_conv_w[0], v_conv_w[0])
    big_in, big_out, (s_ni, s_sk, s_nc, s_na, s_nf, s_cv), loss = _adam_all(
        (w_in_t, g_w_in, m_w_in_t, v_w_in_t), (w_out[0], g_w_out, m_w_out[0], v_w_out[0]), gsum, small)

    def leaves(k):
        return (s_ni[k], big_in[k].T[None], s_cv[k][None], s_sk[k], s_nc[k], s_na[k], big_out[k][None],
                s_nf[k].reshape(D_MODEL))

    return (loss.reshape(()), grad_x.reshape(1, SEQ, D_MODEL), *leaves(0), *leaves(1), *leaves(2), *leaves(3))
```

```python
import functools
import math

import jax
import jax.numpy as jnp
from jax import lax
from jax.experimental import pallas as pl
from jax.experimental.pallas import tpu as pltpu

F32 = jnp.float32
BF16 = jnp.bfloat16
MESH = pl.DeviceIdType.MESH

N_DEV = 8
SEQ = 2048
D_MODEL = 1024
D_CONV = 1024
D_ATTN = 1024
D_KV = 128
HEAD_DIM = 64
N_Q_HEADS = 16
N_PAIRS = N_Q_HEADS // 2
PAIRS_PER_KV = N_PAIRS // 2
D_MIX = D_CONV + D_ATTN
D_PROJ = 6400
SHARD_IN = D_PROJ // N_DEV
SHARD_OUT = D_MIX // N_DEV
SHARD_CONV = D_CONV // N_DEV
OFF_CB, OFF_CC, OFF_CU, OFF_GC, OFF_Q, OFF_K, OFF_V, OFF_GA = 0, 1024, 2048, 3072, 4096, 5120, 5248, 5376
BLOCK = 128
N_BLOCKS = SEQ // BLOCK
HALO = 8
CHUNK = 16
N_CHUNKS = BLOCK // CHUNK
RMS_EPS = 1e-5
NEG = -1e30
SCALE = HEAD_DIM ** -0.5
SLOPES = tuple(2.0 ** (-8.0 * (h + 1) / N_Q_HEADS) for h in range(N_Q_HEADS))

ADAM_LR = 0.001
ADAM_B1 = 0.9
ADAM_B2 = 0.999
ADAM_EPS = 1e-08
ADAM_WD = 0.01
ADAM_STEP = 10

ROW_NORM_IN, ROW_NORM_CONV, ROW_NORM_ATTN, ROW_NORM_FINAL, ROW_CONV0, ROW_SINKS = 0, 1, 2, 3, 4, 7
LOSS_LANE = N_Q_HEADS
ACC_NORM_CONV, ACC_NORM_ATTN, ACC_CONV0, N_ACC = 0, 1, 2, 5

VMEM_LIMIT = 56 * 1024 * 1024

_NT = (((1,), (1,)), ((), ()))
_TN = (((0,), (0,)), ((), ()))


def _params(**kw):
    return pltpu.CompilerParams(vmem_limit_bytes=VMEM_LIMIT, **kw)


def _adamw(w, g, m, v):
    m = ADAM_B1 * m + (1.0 - ADAM_B1) * g
    v = ADAM_B2 * v + (1.0 - ADAM_B2) * (g * g)
    m_hat = m / (1.0 - ADAM_B1 ** ADAM_STEP)
    v_hat = v / (1.0 - ADAM_B2 ** ADAM_STEP)
    delta = -ADAM_LR * (m_hat / (jnp.sqrt(v_hat) + ADAM_EPS) + ADAM_WD * w)
    return delta, m, v


def _sigmoid(t):
    return 1.0 / (1.0 + jnp.exp(-t))


def _slot(px, py, pc):
    return 4 * px + 2 * py + pc


HALF_IN = SHARD_IN // 2
N_GATHER_KINDS = 13


IN_PROJ_TILE = 640
TILE_ORDER = ((0, 1, 2, 3, 4, 5, 6, 7, 8, 9), (3, 4, 0, 1, 2, 8, 9, 5, 6, 7),
              (5, 6, 0, 1, 7, 8, 9, 2, 3, 4), (8, 9, 3, 4, 5, 6, 7, 0, 1, 2))
TILES_OWN, TILES_NEIGHBOURS = 2, 7


def _gather_in_proj(x, norm_in, w_in_sh, w_out_sh, conv_sh, tiles):
    tn = IN_PROJ_TILE
    steps = D_PROJ // tn
    tm = 256

    def body(tiles_ref, x_ref, g_ref, win_ref, wout_ref, cv_ref, wt_ref, h_ref, proj_ref, woutb_ref, conv_ref,
             gin_ref, gcv_ref, send_sems, recv_sems, local_sem):
        p = pl.program_id(0)
        x, y, c = lax.axis_index("x"), lax.axis_index("y"), lax.axis_index("c")
        me, sibling = (x, y, c), (x, y, 1 - c)
        nx, ny, dg = (1 - x, y, c), (x, 1 - y, c), (1 - x, 1 - y, c)

        def other(dev):
            return (dev[0], dev[1], 1 - dev[2])

        def shard(dev):
            return gin_ref.at[pl.ds(pl.multiple_of(_slot(*dev) * SHARD_IN, 16), SHARD_IN), :]

        def half(dev, h):
            return gin_ref.at[pl.ds(pl.multiple_of(_slot(*dev) * SHARD_IN + h * HALF_IN, 16), HALF_IN), :]

        def rc(ref, k, to):
            return pltpu.make_async_remote_copy(src_ref=ref, dst_ref=ref, send_sem=send_sems.at[k],
                                                recv_sem=recv_sems.at[k], device_id=to, device_id_type=MESH)

        def cv(k, dev, to):
            s = _slot(*dev)
            return pltpu.make_async_remote_copy(src_ref=gcv_ref.at[s], dst_ref=gcv_ref.at[s],
                                                send_sem=send_sems.at[N_GATHER_KINDS + k],
                                                recv_sem=recv_sems.at[N_GATHER_KINDS + k], device_id=to, device_id_type=MESH)

        def own_copies():
            return [rc(shard(me), 0, sibling),
                    rc(half(me, 0), 1, nx), rc(half(me, 1), 2, nx),
                    rc(half(me, 1), 4, ny), rc(half(me, 0), 3, ny),
                    cv(0, me, sibling)] + [cv(1 + j, me, peer) for j, peer in enumerate((nx, ny, dg))]

        def pass_on(dev, h, k_in, k_ici, k_d2d):
            rc(half(dev, h), k_in, me).wait_recv()
            if k_ici is not None:
                rc(half(dev, h), k_ici, ny if dev is nx else nx).start()
            rc(half(dev, h), k_d2d, sibling).start()

        @pl.when(p == 0)
        def _():
            gin_ref[pl.ds(pl.multiple_of(_slot(*me) * SHARD_IN, 16), SHARD_IN), :] = win_ref[...].astype(BF16)
            gcv_ref[_slot(*me)] = cv_ref[...]
            for cp in own_copies():
                cp.start()
            woutb_ref[...] = wout_ref[...].astype(BF16)
            for t in range(SEQ // tm):
                xv = x_ref[tm * t:tm * (t + 1), :]
                r = lax.rsqrt(jnp.mean(xv * xv, axis=-1, keepdims=True) + RMS_EPS)
                h_ref[tm * t:tm * (t + 1), :] = (xv * r * g_ref[...]).astype(BF16)
            rc(shard(sibling), 0, me).wait_recv()

        @pl.when(p == TILES_OWN)
        def _():
            for args in ((nx, 0, 1, 5, 7), (ny, 1, 4, 6, 10), (nx, 1, 2, None, 8), (ny, 0, 3, None, 9)):
                pass_on(*args)
            for j, peer in enumerate((nx, ny, dg)):
                cv(1 + j, peer, me).wait_recv()
                cv(4 + j, peer, sibling).start()
            for (dev, h), k in (((nx, 0), 7), ((nx, 1), 8), ((ny, 0), 9), ((ny, 1), 10)):
                rc(half(other(dev), h), k, me).wait_recv()

        @pl.when(p == TILES_NEIGHBOURS)
        def _():
            pass_on(dg, 0, 5, None, 11)
            pass_on(dg, 1, 6, None, 12)
            for (dev, h), k in (((dg, 0), 11), ((dg, 1), 12)):
                rc(half(other(dev), h), k, me).wait_recv()
            pltpu.make_async_copy(gin_ref, wt_ref, local_sem).start()

        w = gin_ref[pl.ds(pl.multiple_of(tiles_ref[p] * tn, tn), tn), :]
        proj_ref[...] = lax.dot_general(h_ref[...], w, _NT, preferred_element_type=F32)

        @pl.when(p == steps - 1)
        def _():
            cv(0, sibling, me).wait_recv()
            for j, peer in enumerate((nx, ny, dg)):
                cv(4 + j, other(peer), me).wait_recv()
            for d in range(N_DEV):
                conv_ref[:, d * SHARD_CONV:(d + 1) * SHARD_CONV] = gcv_ref[d]
            relayed = [rc(half(nx, 0), 5, ny), rc(half(ny, 1), 6, nx)]
            relayed += [rc(half(dev, h), k, sibling) for (dev, h), k in
                        (((nx, 0), 7), ((nx, 1), 8), ((ny, 0), 9), ((ny, 1), 10), ((dg, 0), 11), ((dg, 1), 12))]
            relayed += [cv(4 + j, peer, sibling) for j, peer in enumerate((nx, ny, dg))]
            for cp in own_copies() + relayed:
                cp.wait_send()
            pltpu.make_async_copy(gin_ref, wt_ref, local_sem).wait()

    vmem = pl.BlockSpec(memory_space=pltpu.VMEM)
    grid_spec = pltpu.PrefetchScalarGridSpec(
        num_scalar_prefetch=1, grid=(steps,),
        in_specs=[vmem, vmem, vmem, vmem, vmem],
        out_specs=(pl.BlockSpec(memory_space=pl.ANY), vmem,
                   pl.BlockSpec((SEQ, tn), lambda p, tiles_ref: (0, tiles_ref[p])), vmem, vmem),
        scratch_shapes=[pltpu.VMEM((D_PROJ, D_MODEL), BF16), pltpu.VMEM((N_DEV, 8, SHARD_CONV), F32),
                        pltpu.SemaphoreType.DMA((N_GATHER_KINDS + 7,)), pltpu.SemaphoreType.DMA((N_GATHER_KINDS + 7,)),
                        pltpu.SemaphoreType.DMA])
    return pl.pallas_call(
        body, name="gather_in_proj", grid_spec=grid_spec,
        out_shape=(jax.ShapeDtypeStruct((D_PROJ, D_MODEL), BF16), jax.ShapeDtypeStruct((SEQ, D_MODEL), BF16),
                   jax.ShapeDtypeStruct((SEQ, D_PROJ), F32), jax.ShapeDtypeStruct((SHARD_OUT, D_MODEL), BF16),
                   jax.ShapeDtypeStruct((8, D_CONV), F32)),
        compiler_params=_params(dimension_semantics=("arbitrary",)),
    )(tiles, x, norm_in, w_in_sh, w_out_sh, conv_sh)


def _wout_gather(wo_ref, gout_ref, send_sems, recv_sems, local_sem):
    x, y, c = lax.axis_index("x"), lax.axis_index("y"), lax.axis_index("c")
    me, sibling = (x, y, c), (x, y, 1 - c)
    chips = [(1 - x, y), (x, 1 - y), (1 - x, 1 - y)]

    def copy(k, block, to, src=None):
        rows = gout_ref.at[_slot(*block)]
        return pltpu.make_async_remote_copy(src_ref=rows if src is None else src, dst_ref=rows,
                                            send_sem=send_sems.at[k], recv_sem=recv_sems.at[k],
                                            device_id=to, device_id_type=MESH)

    def mine():
        return pltpu.make_async_copy(wo_ref, gout_ref.at[_slot(*me)], local_sem)

    def start():
        mine().start()
        copy(0, me, sibling, src=wo_ref).start()
        for j, chip in enumerate(chips):
            copy(1 + j, me, (*chip, c), src=wo_ref).start()

    def forward():
        for j, chip in enumerate(chips):
            copy(1 + j, (*chip, c), me).wait_recv()
            copy(4 + j, (*chip, c), sibling).start()

    def finish():
        copy(0, sibling, me).wait_recv()
        for j, chip in enumerate(chips):
            copy(4 + j, (*chip, 1 - c), me).wait_recv()
        copy(0, me, sibling, src=wo_ref).wait_send()
        for j, chip in enumerate(chips):
            copy(1 + j, me, (*chip, c), src=wo_ref).wait_send()
            copy(4 + j, (*chip, c), sibling).wait_send()
        mine().wait()

    return start, forward, finish


def _shard_sum(src, own, d2d, ici, send_sems, recv_sems, local_sems, base=0):
    x, y, c = lax.axis_index("x"), lax.axis_index("y"), lax.axis_index("c")
    sibling = (x, y, 1 - c)
    chips = [(x, y), (1 - x, y), (x, 1 - y), (1 - x, 1 - y)]

    def rcopy(s, d, k, to):
        return pltpu.make_async_remote_copy(src_ref=s, dst_ref=d, send_sem=send_sems.at[base + k],
                                            recv_sem=recv_sems.at[base + k], device_id=to, device_id_type=MESH)

    def mine(k):
        return pltpu.make_async_copy(src.at[_slot(*chips[k], c)], own.at[k], local_sems.at[k])

    def to_sibling(k):
        return rcopy(src.at[_slot(*chips[k], 1 - c)], d2d.at[k], k, sibling)

    def to_chip(k):
        return rcopy(own.at[k], ici.at[k - 1], 3 + k, (*chips[k], c))

    def start():
        for k in range(4):
            mine(k).start()
            to_sibling(k).start()

    def forward():
        for k in range(1, 4):
            mine(k).wait()
            to_sibling(k).wait_recv()
            own[k] = (own[k].astype(F32) + d2d[k].astype(F32)).astype(BF16)
            to_chip(k).start()

    def finish():
        mine(0).wait()
        to_sibling(0).wait_recv()
        acc = own[0].astype(F32) + d2d[0].astype(F32)
        for k in range(1, 4):
            to_chip(k).wait_recv()
            acc = acc + ici[k - 1].astype(F32)
        for k in range(4):
            to_sibling(k).wait_send()
        for k in range(1, 4):
            to_chip(k).wait_send()
        return acc

    return start, forward, finish


def _shard_sum_scratch(rows):
    return [pltpu.VMEM((4, rows, D_MODEL), BF16), pltpu.VMEM((4, rows, D_MODEL), BF16),
            pltpu.VMEM((3, rows, D_MODEL), BF16)]


N_SHARD_SUM_SEMS = 7


def _chip_sum(dwt, d2d, out_hbm, send_sems, recv_sems, local_sems, base, local_base):
    x, y, c = lax.axis_index("x"), lax.axis_index("y"), lax.axis_index("c")
    sibling = (x, y, 1 - c)
    chips = [(x, y), (1 - x, y), (x, 1 - y), (1 - x, 1 - y)]

    def shard(s):
        return dwt.at[pl.ds(pl.multiple_of(s * SHARD_IN, 16), SHARD_IN), :]

    def to_sibling(k):
        return pltpu.make_async_remote_copy(src_ref=shard(_slot(*chips[k], 1 - c)), dst_ref=d2d.at[k],
                                            send_sem=send_sems.at[base + k], recv_sem=recv_sems.at[base + k],
                                            device_id=sibling, device_id_type=MESH)

    def save(k):
        return pltpu.make_async_copy(d2d.at[k], out_hbm.at[k], local_sems.at[local_base + k])

    def send(first):
        for k in range(4):
            in_first = _slot(*chips[k], 1 - c) < N_DEV // 2

            @pl.when(in_first if first else jnp.logical_not(in_first))
            def _():
                to_sibling(k).start()

    def finish():
        for k in range(4):
            to_sibling(k).wait_recv()
            d2d[k] = (shard(_slot(*chips[k], c))[...].astype(F32) + d2d[k].astype(F32)).astype(BF16)
            save(k).start()
        for k in range(4):
            save(k).wait()
            to_sibling(k).wait_send()

    return send, finish


N_ICI_SUM_SEMS = 6


def _ici_sum(src, own, ici, via, stage, send_sems, recv_sems, local_sems, base=0):
    x, y, c = lax.axis_index("x"), lax.axis_index("y"), lax.axis_index("c")
    nx, ny = (1 - x, y, c), (x, 1 - y, c)
    OWN, NX, NY, DG = range(4)

    def half(ref, h):
        return ref.at[pl.ds(h * HALF_IN, HALF_IN), :]

    def rc(s, d, k, to):
        return pltpu.make_async_remote_copy(src_ref=s, dst_ref=d, send_sem=send_sems.at[base + k],
                                            recv_sem=recv_sems.at[base + k], device_id=to, device_id_type=MESH)

    for_dg_0 = lambda: rc(half(src.at[DG], 0), via.at[0], 0, nx)
    for_dg_1 = lambda: rc(half(src.at[DG], 1), via.at[1], 1, ny)
    for_nx_0 = lambda: rc(half(src.at[NX], 0), half(ici.at[0], 0), 2, nx)
    for_ny_1 = lambda: rc(half(src.at[NY], 1), half(ici.at[1], 1), 3, ny)
    for_ny_0 = lambda: rc(stage.at[0], half(ici.at[1], 0), 4, ny)
    for_nx_1 = lambda: rc(stage.at[1], half(ici.at[0], 1), 5, nx)
    mine = lambda: pltpu.make_async_copy(src.at[OWN], own, local_sems.at[0])
    stage_0 = lambda: pltpu.make_async_copy(half(src.at[NY], 0), stage.at[0], local_sems.at[1])
    stage_1 = lambda: pltpu.make_async_copy(half(src.at[NX], 1), stage.at[1], local_sems.at[2])

    def start():
        for cp in (for_dg_0, for_dg_1, for_nx_0, for_ny_1, stage_0, stage_1, mine):
            cp().start()

    def relay():
        for h, staged, landed, out in ((0, stage_0, for_dg_0, for_ny_0), (1, stage_1, for_dg_1, for_nx_1)):
            staged().wait()
            landed().wait_recv()
            stage[h] = (stage[h].astype(F32) + via[h].astype(F32)).astype(BF16)
            out().start()

    def finish():
        mine().wait()
        for cp in (for_nx_0, for_nx_1, for_ny_1, for_ny_0):
            cp().wait_recv()
        acc = own[...].astype(F32) + ici[0].astype(F32) + ici[1].astype(F32)
        for cp in (for_dg_0, for_dg_1, for_nx_0, for_ny_1, for_ny_0, for_nx_1):
            cp().wait_send()
        return acc

    return start, relay, finish


def _slab_sum(myslab, slabs, send_sems, recv_sems, base):
    x, y, c = lax.axis_index("x"), lax.axis_index("y"), lax.axis_index("c")
    me = _slot(x, y, c)
    peers = [(x, y, 1 - c), (1 - x, y, c), (x, 1 - y, c), (1 - x, 1 - y, c),
             (1 - x, y, 1 - c), (x, 1 - y, 1 - c), (1 - x, 1 - y, 1 - c)]

    def cp(k):
        return pltpu.make_async_remote_copy(src_ref=myslab, dst_ref=slabs.at[me], send_sem=send_sems.at[base + k],
                                            recv_sem=recv_sems.at[base + k], device_id=peers[k], device_id_type=MESH)

    def start():
        slabs[me] = myslab[...]
        for k in range(7):
            cp(k).start()

    def finish():
        for k in range(7):
            cp(k).wait_recv()
        total = slabs[0]
        for d in range(1, N_DEV):
            total = total + slabs[d]
        for k in range(7):
            cp(k).wait_send()
        return total

    return start, finish


def _chunk_rows(r):
    return pl.ds(pl.multiple_of(r * CHUNK, CHUNK), CHUNK)


def _conv_halo(cch_ref, cuh_ref, n):
    zh = jnp.where(n > 0, cch_ref[...] * cuh_ref[...], 0.0)
    return jnp.concatenate([zh] * (CHUNK // HALO), axis=0)


def _conv_chunk(pj_ref, zhalo, cw, r):
    rows = _chunk_rows(r)
    cc = pj_ref[rows, OFF_CC:OFF_CC + D_CONV]
    cu = pj_ref[rows, OFF_CU:OFF_CU + D_CONV]
    z = cc * cu
    before = _chunk_rows(jnp.maximum(r - 1, 0))
    zprev = jnp.where(r > 0, pj_ref[before, OFF_CC:OFF_CC + D_CONV] * pj_ref[before, OFF_CU:OFF_CU + D_CONV], zhalo)
    row = lax.broadcasted_iota(jnp.int32, (CHUNK, D_CONV), 0)
    z1 = jnp.where(row < 1, pltpu.roll(zprev, 1, 0), pltpu.roll(z, 1, 0))
    z2 = jnp.where(row < 2, pltpu.roll(zprev, 2, 0), pltpu.roll(z, 2, 0))
    co = cw[0] * z2 + cw[1] * z1 + cw[2] * z
    return cc, cu, z, z1, z2, co


def _gated_norm(a, gain, t):
    r = lax.rsqrt(jnp.mean(a * a, axis=-1, keepdims=True) + RMS_EPS)
    return a * r * gain * (t * _sigmoid(t))


def _kv_bands(pj, kvp_ref):
    lane = lax.broadcasted_iota(jnp.int32, (2 * BLOCK, D_KV), 1)
    lo = lane < HEAD_DIM

    def bands(prev, cur):
        b = jnp.concatenate([prev, cur], axis=0)
        br = pltpu.roll(b, HEAD_DIM, 1)
        zero = jnp.zeros_like(b)
        return ((jnp.where(lo, b, zero).astype(BF16), jnp.where(lo, zero, br).astype(BF16)),
                (jnp.where(lo, br, zero).astype(BF16), jnp.where(lo, zero, b).astype(BF16)))

    ks = bands(kvp_ref[:, 0:D_KV], pj[:, OFF_K:OFF_K + D_KV])
    vs = bands(kvp_ref[:, D_KV:2 * D_KV], pj[:, OFF_V:OFF_V + D_KV])
    return ks, vs


STACK = PAIRS_PER_KV * BLOCK


def _head(j, i, e):
    return 2 * (PAIRS_PER_KV * j + i) + e


def _pair_cols(j, i, off):
    p = PAIRS_PER_KV * j + i
    return slice(off + 128 * p, off + 128 * (p + 1))


def _fill_attn_bias(bias_scr, first_block):
    qi = lax.broadcasted_iota(jnp.int32, (BLOCK, 2 * BLOCK), 0)
    kj = lax.broadcasted_iota(jnp.int32, (BLOCK, 2 * BLOCK), 1)
    dist = BLOCK + qi - kj
    valid = (dist >= 0) & (dist < BLOCK)
    if first_block:
        valid = valid & (kj >= BLOCK)
    distf = dist.astype(F32)
    for j in range(2):
        for e in range(2):
            for i in range(PAIRS_PER_KV):
                bias_scr[2 * j + e, BLOCK * i:BLOCK * (i + 1), :] = jnp.where(valid, -SLOPES[_head(j, i, e)] * distf, NEG)


def _q_stack(pj, j):
    return jnp.concatenate([(pj[:, _pair_cols(j, i, OFF_Q)] * SCALE).astype(BF16) for i in range(PAIRS_PER_KV)], axis=0)


def _sink_rows(sink_ref, j, e):
    return jnp.concatenate([jnp.full((BLOCK, 128), sink_ref[_head(j, i, e)], F32) for i in range(PAIRS_PER_KV)], axis=0)


def _attn_probs(q_stack, kband, bias, sink):
    s = lax.dot_general(q_stack, kband, _NT, preferred_element_type=F32) + bias
    m = jnp.broadcast_to(jnp.max(s, axis=-1, keepdims=True), (STACK, 128))
    m = jnp.maximum(m, sink)
    p = [jnp.exp(t - m) for t in (s[:, :128], s[:, 128:])]
    es = jnp.exp(sink - m)
    ones = jnp.ones((128, 128), BF16)
    total = (jnp.dot(p[0].astype(BF16), ones, preferred_element_type=F32)
             + jnp.dot(p[1].astype(BF16), ones, preferred_element_type=F32))
    inv = 1.0 / (total + es)
    return jnp.concatenate([p[0] * inv, p[1] * inv], axis=1), es * inv


def _attn_group(pj, ks, vs, bias_scr, sink_ref, j):
    q_stack = _q_stack(pj, j)
    out, probs, shares = None, [], []
    for e in range(2):
        p, ps = _attn_probs(q_stack, ks[j][e], bias_scr[2 * j + e], _sink_rows(sink_ref, j, e))
        p = p.astype(BF16)
        o = jnp.dot(p, vs[j][e], preferred_element_type=F32)
        out = o if out is None else out + o
        probs.append(p)
        shares.append(ps)
    return out, probs, shares


def _mix_fwd(proj, conv_full, sinks, norm_conv, norm_attn, w_out_b):
    forward_step = N_BLOCKS // 2

    def body(pj_ref, kvp_ref, cch_ref, cuh_ref, cw_ref, sink_ref, gc_ref, ga_ref, wo_ref,
             mixed_ref, attn_scr, p_ref, ps_ref, gout_ref, bias_scr, send_sems, recv_sems, local_sem):
        n = pl.program_id(0)
        pj = pj_ref
        wout_start, wout_forward, wout_finish = _wout_gather(wo_ref, gout_ref, send_sems, recv_sems, local_sem)

        @pl.when(n == 0)
        def _():
            wout_start()
            _fill_attn_bias(bias_scr, first_block=True)

        @pl.when(n == 1)
        def _():
            _fill_attn_bias(bias_scr, first_block=False)

        zhalo = _conv_halo(cch_ref, cuh_ref, n)
        cw = (cw_ref[0:1, :], cw_ref[1:2, :], cw_ref[2:3, :])
        gain_c = gc_ref[...]

        def conv_chunk(r, carry):
            rows = _chunk_rows(r)
            co = _conv_chunk(pj_ref, zhalo, cw, r)[-1]
            y = _gated_norm(pj_ref[rows, OFF_CB:OFF_CB + D_CONV] * co, gain_c, pj_ref[rows, OFF_GC:OFF_GC + D_CONV])
            mixed_ref[rows, 0:D_CONV] = y.astype(BF16)
            return carry

        lax.fori_loop(0, N_CHUNKS, conv_chunk, 0, unroll=True)

        ks, vs = _kv_bands(pj, kvp_ref)
        for j in range(2):
            out, probs, shares = _attn_group(pj, ks, vs, bias_scr, sink_ref, j)
            for e in range(2):
                p_ref[0, 2 * j + e] = probs[e]
                ps_ref[0, 2 * j + e] = shares[e]
            for i in range(PAIRS_PER_KV):
                attn_scr[:, _pair_cols(j, i, 0)] = out[BLOCK * i:BLOCK * (i + 1), :]
        gain_a = ga_ref[...]

        def norm_chunk(r, carry):
            rows = _chunk_rows(r)
            y = _gated_norm(attn_scr[rows, :], gain_a, pj_ref[rows, OFF_GA:OFF_GA + D_ATTN])
            mixed_ref[rows, D_CONV:D_MIX] = y.astype(BF16)
            return carry

        lax.fori_loop(0, N_CHUNKS, norm_chunk, 0, unroll=True)

        @pl.when(n == forward_step)
        def _():
            wout_forward()

        @pl.when(n == N_BLOCKS - 1)
        def _():
            wout_finish()

    per_block = BLOCK // HALO
    return pl.pallas_call(
        body, name="mix_fwd", grid=(N_BLOCKS,),
        in_specs=[
            pl.BlockSpec((BLOCK, D_PROJ), lambda n: (n, 0)),
            pl.BlockSpec((BLOCK, 2 * D_KV), lambda n: (jnp.maximum(n - 1, 0), OFF_K // (2 * D_KV))),
            pl.BlockSpec((HALO, D_CONV), lambda n: (jnp.maximum(n * per_block - 1, 0), OFF_CC // D_CONV)),
            pl.BlockSpec((HALO, D_CONV), lambda n: (jnp.maximum(n * per_block - 1, 0), OFF_CU // D_CONV)),
            pl.BlockSpec((8, D_CONV), lambda n: (0, 0)),
            pl.BlockSpec(memory_space=pltpu.SMEM),
            pl.BlockSpec((1, D_CONV), lambda n: (0, 0)),
            pl.BlockSpec((1, D_ATTN), lambda n: (0, 0)),
            pl.BlockSpec(memory_space=pl.ANY),
        ],
        out_specs=(pl.BlockSpec((BLOCK, D_MIX), lambda n: (n, 0)), pl.BlockSpec((BLOCK, D_ATTN), lambda n: (n, 0)),
                   pl.BlockSpec((1, 4, STACK, 2 * BLOCK), lambda n: (n, 0, 0, 0)),
                   pl.BlockSpec((1, 4, STACK, 128), lambda n: (n, 0, 0, 0)),
                   pl.BlockSpec(memory_space=pl.ANY)),
        out_shape=(jax.ShapeDtypeStruct((SEQ, D_MIX), BF16), jax.ShapeDtypeStruct((SEQ, D_ATTN), F32),
                   jax.ShapeDtypeStruct((N_BLOCKS, 4, STACK, 2 * BLOCK), BF16),
                   jax.ShapeDtypeStruct((N_BLOCKS, 4, STACK, 128), F32),
                   jax.ShapeDtypeStruct((N_DEV, SHARD_OUT, D_MODEL), BF16)),
        scratch_shapes=[pltpu.VMEM((4, STACK, 2 * BLOCK), F32),
                        pltpu.SemaphoreType.DMA((7,)), pltpu.SemaphoreType.DMA((7,)), pltpu.SemaphoreType.DMA],
        compiler_params=_params(dimension_semantics=("arbitrary",)),
    )(proj, proj, proj, proj, conv_full, sinks, norm_conv, norm_attn, w_out_b)


def _out_proj_loss(mixed, x, target, w_out_full, norm_final):
    tm = 256

    def body(mx_ref, x_ref, t_ref, w_ref, g_ref, dx2_ref, dx2b_ref, dmix_ref, gnf_ref, loss_ref):
        i = pl.program_id(0)
        w = w_ref[...]
        x2 = x_ref[...] + jnp.dot(mx_ref[...], w, preferred_element_type=F32)
        r = lax.rsqrt(jnp.mean(x2 * x2, axis=-1, keepdims=True) + RMS_EPS)
        xn = x2 * r
        g = g_ref[...]
        err = xn * g - t_ref[...]
        part = 0.5 * jnp.sum(jnp.mean(err * err, axis=-1, keepdims=True), axis=0, keepdims=True)
        dy = err * (1.0 / D_MODEL)
        gnf = jnp.sum(dy * xn, axis=0, keepdims=True)
        u = dy * g
        dx2 = r * (u - xn * jnp.mean(u * xn, axis=-1, keepdims=True))
        dx2_ref[...] = dx2
        dx2b = dx2.astype(BF16)
        dx2b_ref[...] = dx2b
        dmix_ref[...] = lax.dot_general(dx2b, w, _NT, preferred_element_type=F32)

        @pl.when(i == 0)
        def _():
            gnf_ref[...] = jnp.zeros_like(gnf_ref)
            loss_ref[...] = jnp.zeros_like(loss_ref)

        gnf_ref[...] += gnf
        loss_ref[...] += jnp.broadcast_to(part, loss_ref.shape)

    return pl.pallas_call(
        body, name="out_proj_loss", grid=(SEQ // tm,),
        in_specs=[pl.BlockSpec((tm, D_MIX), lambda i: (i, 0)), pl.BlockSpec((tm, D_MODEL), lambda i: (i, 0)),
                  pl.BlockSpec((tm, D_MODEL), lambda i: (i, 0)), pl.BlockSpec(memory_space=pltpu.VMEM),
                  pl.BlockSpec((1, D_MODEL), lambda i: (0, 0))],
        out_specs=(pl.BlockSpec((tm, D_MODEL), lambda i: (i, 0)), pl.BlockSpec((tm, D_MODEL), lambda i: (i, 0)),
                   pl.BlockSpec((tm, D_MIX), lambda i: (i, 0)),
                   pl.BlockSpec((1, D_MODEL), lambda i: (0, 0)), pl.BlockSpec((8, 128), lambda i: (0, 0))),
        out_shape=(jax.ShapeDtypeStruct((SEQ, D_MODEL), F32), jax.ShapeDtypeStruct((SEQ, D_MODEL), BF16),
                   jax.ShapeDtypeStruct((SEQ, D_MIX), F32),
                   jax.ShapeDtypeStruct((1, D_MODEL), F32), jax.ShapeDtypeStruct((8, 128), F32)),
        compiler_params=_params(dimension_semantics=("arbitrary",)),
    )(mixed, x, target, w_out_full, norm_final)


def _gated_norm_bwd(a, gain, t, dy):
    r = lax.rsqrt(jnp.mean(a * a, axis=-1, keepdims=True) + RMS_EPS)
    an = a * r
    sg = _sigmoid(t)
    dn = dy * (t * sg)
    dt = dy * (an * gain) * (sg * (1.0 + t * (1.0 - sg)))
    u = dn * gain
    da = r * (u - an * jnp.mean(u * an, axis=-1, keepdims=True))
    return da, dt, dn * an


def _mix_bwd(proj, dmixed, attn, probs, shares, conv_full, norm_conv, norm_attn):
    def body(pj_ref, kvp_ref, cch_ref, cuh_ref, dmx_ref, attn_ref, p_ref, ps_ref, cw_ref, gc_ref, ga_ref,
             dpj_ref, gslab_ref, dattn_scr, nxt_scr, dkv_scr, acc_scr):
        step = pl.program_id(0)
        n = N_BLOCKS - 1 - step
        pj = pj_ref

        @pl.when(step == 0)
        def _():
            gslab_ref[...] = jnp.zeros_like(gslab_ref)
            nxt_scr[...] = jnp.zeros_like(nxt_scr)
            dkv_scr[...] = jnp.zeros_like(dkv_scr)
            acc_scr[...] = jnp.zeros_like(acc_scr)

        zhalo = _conv_halo(cch_ref, cuh_ref, n)
        cw = (cw_ref[0:1, :], cw_ref[1:2, :], cw_ref[2:3, :])
        gain_c = gc_ref[...]
        row = lax.broadcasted_iota(jnp.int32, (CHUNK, D_CONV), 0)

        def conv_chunk(t, dco_after):
            r = N_CHUNKS - 1 - t
            rows = _chunk_rows(r)
            cc, cu, z, z1, z2, co = _conv_chunk(pj_ref, zhalo, cw, r)
            cb = pj_ref[rows, OFF_CB:OFF_CB + D_CONV]
            da, dgate, gterm = _gated_norm_bwd(cb * co, gain_c, pj_ref[rows, OFF_GC:OFF_GC + D_CONV],
                                               dmx_ref[rows, 0:D_CONV])
            dpj_ref[rows, OFF_GC:OFF_GC + D_CONV] = dgate.astype(BF16)
            dpj_ref[rows, OFF_CB:OFF_CB + D_CONV] = (da * co).astype(BF16)
            dco = da * cb
            dco1 = jnp.where(row >= CHUNK - 1, pltpu.roll(dco_after, CHUNK - 1, 0), pltpu.roll(dco, CHUNK - 1, 0))
            dco2 = jnp.where(row >= CHUNK - 2, pltpu.roll(dco_after, CHUNK - 2, 0), pltpu.roll(dco, CHUNK - 2, 0))
            dz = cw[2] * dco + cw[1] * dco1 + cw[0] * dco2
            dpj_ref[rows, OFF_CC:OFF_CC + D_CONV] = (dz * cu).astype(BF16)
            dpj_ref[rows, OFF_CU:OFF_CU + D_CONV] = (dz * cc).astype(BF16)
            acc_scr[ACC_NORM_CONV] += gterm
            acc_scr[ACC_CONV0] += dco * z2
            acc_scr[ACC_CONV0 + 1] += dco * z1
            acc_scr[ACC_CONV0 + 2] += dco * z
            return dco

        nxt_scr[...] = lax.fori_loop(0, N_CHUNKS, conv_chunk, nxt_scr[...], unroll=True)

        ks, vs = _kv_bands(pj, kvp_ref)
        gain_a = ga_ref[...]

        def norm_chunk(r, carry):
            rows = _chunk_rows(r)
            da, dgate, gterm = _gated_norm_bwd(attn_ref[rows, :], gain_a, pj_ref[rows, OFF_GA:OFF_GA + D_ATTN],
                                               dmx_ref[rows, D_CONV:D_MIX])
            dpj_ref[rows, OFF_GA:OFF_GA + D_ATTN] = dgate.astype(BF16)
            dattn_scr[rows, :] = da
            acc_scr[ACC_NORM_ATTN] += gterm
            return carry

        lax.fori_loop(0, N_CHUNKS, norm_chunk, 0, unroll=True)

        in_lo = lax.broadcasted_iota(jnp.int32, (128, 128), 0) < HEAD_DIM
        half_ones = (jnp.where(in_lo, 1.0, 0.0).astype(BF16), jnp.where(in_lo, 0.0, 1.0).astype(BF16))
        lane_s = lax.broadcasted_iota(jnp.int32, (1, D_MODEL), 1)
        gsink = jnp.zeros((1, D_MODEL), F32)
        dk_t, dv_t = [], []
        for j in range(2):
            q_stack = _q_stack(pj, j)
            do_f = jnp.concatenate([dattn_scr[:, _pair_cols(j, i, 0)] for i in range(PAIRS_PER_KV)], axis=0)
            o_f = jnp.concatenate([attn_ref[:, _pair_cols(j, i, 0)] for i in range(PAIRS_PER_KV)], axis=0)
            prod = (do_f * o_f).astype(BF16)
            deltas = [jnp.dot(prod, half_ones[e], preferred_element_type=F32) for e in range(2)]
            do_b = do_f.astype(BF16)
            dq, dk_j, dv_j = None, None, None
            for e in range(2):
                p = p_ref[0, 2 * j + e]
                dp = lax.dot_general(do_b, vs[j][e], _NT, preferred_element_type=F32)
                delta = jnp.concatenate([deltas[e], deltas[e]], axis=1)
                ds = (p.astype(F32) * (dp - delta)).astype(BF16)
                gs = ps_ref[0, 2 * j + e] * deltas[e]
                for i in range(PAIRS_PER_KV):
                    gs_h = -jnp.sum(gs[BLOCK * i:BLOCK * (i + 1), 0:1], axis=0, keepdims=True)
                    gsink = gsink + jnp.where(lane_s == _head(j, i, e), gs_h, 0.0)
                t = jnp.dot(ds, ks[j][e], preferred_element_type=F32)
                dq = t if dq is None else dq + t
                half = slice(HEAD_DIM * e, HEAD_DIM * (e + 1))
                a = lax.dot_general(q_stack, ds, _TN, preferred_element_type=F32)[half, :]
                b = lax.dot_general(do_b, p, _TN, preferred_element_type=F32)[half, :]
                dk_j = a if dk_j is None else dk_j + a
                dv_j = b if dv_j is None else dv_j + b
            for i in range(PAIRS_PER_KV):
                dpj_ref[:, _pair_cols(j, i, OFF_Q)] = (dq[BLOCK * i:BLOCK * (i + 1), :] * SCALE).astype(BF16)
            dk_t.append(dk_j)
            dv_t.append(dv_j)
        dk = jnp.concatenate(dk_t, axis=0).T
        dv = jnp.concatenate(dv_t, axis=0).T
        dpj_ref[:, OFF_K:OFF_K + D_KV] = (dk[BLOCK:, :] + dkv_scr[:, 0:D_KV]).astype(BF16)
        dpj_ref[:, OFF_V:OFF_V + D_KV] = (dv[BLOCK:, :] + dkv_scr[:, D_KV:2 * D_KV]).astype(BF16)
        dkv_scr[:, 0:D_KV] = dk[:BLOCK, :]
        dkv_scr[:, D_KV:2 * D_KV] = dv[:BLOCK, :]
        gslab_ref[ROW_SINKS:ROW_SINKS + 1, :] += gsink

        @pl.when(step == N_BLOCKS - 1)
        def _():
            for k, slab_row in ((ACC_NORM_CONV, ROW_NORM_CONV), (ACC_NORM_ATTN, ROW_NORM_ATTN), (ACC_CONV0, ROW_CONV0),
                                (ACC_CONV0 + 1, ROW_CONV0 + 1), (ACC_CONV0 + 2, ROW_CONV0 + 2)):
                gslab_ref[slab_row:slab_row + 1, :] = jnp.sum(acc_scr[k], axis=0, keepdims=True)

    per_block = BLOCK // HALO
    last = N_BLOCKS - 1
    return pl.pallas_call(
        body, name="mix_bwd", grid=(N_BLOCKS,),
        in_specs=[
            pl.BlockSpec((BLOCK, D_PROJ), lambda s: (last - s, 0)),
            pl.BlockSpec((BLOCK, 2 * D_KV), lambda s: (jnp.maximum(last - s - 1, 0), OFF_K // (2 * D_KV))),
            pl.BlockSpec((HALO, D_CONV), lambda s: (jnp.maximum((last - s) * per_block - 1, 0), OFF_CC // D_CONV)),
            pl.BlockSpec((HALO, D_CONV), lambda s: (jnp.maximum((last - s) * per_block - 1, 0), OFF_CU // D_CONV)),
            pl.BlockSpec((BLOCK, D_MIX), lambda s: (last - s, 0)),
            pl.BlockSpec((BLOCK, D_ATTN), lambda s: (last - s, 0)),
            pl.BlockSpec((1, 4, STACK, 2 * BLOCK), lambda s: (last - s, 0, 0, 0)),
            pl.BlockSpec((1, 4, STACK, 128), lambda s: (last - s, 0, 0, 0)),
            pl.BlockSpec((8, D_CONV), lambda s: (0, 0)),
            pl.BlockSpec((1, D_CONV), lambda s: (0, 0)),
            pl.BlockSpec((1, D_ATTN), lambda s: (0, 0)),
        ],
        out_specs=(pl.BlockSpec((BLOCK, D_PROJ), lambda s: (last - s, 0)),
                   pl.BlockSpec((8, D_MODEL), lambda s: (0, 0))),
        out_shape=(jax.ShapeDtypeStruct((SEQ, D_PROJ), BF16), jax.ShapeDtypeStruct((8, D_MODEL), F32)),
        scratch_shapes=[pltpu.VMEM((BLOCK, D_ATTN), F32), pltpu.VMEM((CHUNK, D_CONV), F32),
                        pltpu.VMEM((BLOCK, 2 * D_KV), F32), pltpu.VMEM((N_ACC, CHUNK, D_MODEL), F32)],
        compiler_params=_params(dimension_semantics=("arbitrary",)),
    )(proj, proj, proj, proj, dmixed, attn, probs, shares, conv_full, norm_conv, norm_attn)


def _in_bwd_rs(dproj, w_full, x, dx2, norm_in, dw_in_chip, gslab, gnf, loss_part):
    tm = 256
    steps = SEQ // tm
    relay_step = 4

    def body(dp_ref, w_ref, x_ref, dx2_ref, g_ref, dwi_ref, gs_ref, gnf_ref, lp_ref, gx_ref, gwin_ref, gsum_ref,
             gni_scr, own, ici, via, stage, myslab, slabs, send_sems, recv_sems, local_sems):
        i = pl.program_id(0)
        rs_start, rs_relay, rs_finish = _ici_sum(dwi_ref, own, ici, via, stage, send_sems, recv_sems, local_sems)
        slab_start, slab_finish = _slab_sum(myslab, slabs, send_sems, recv_sems, N_ICI_SUM_SEMS)

        @pl.when(i == 0)
        def _():
            gni_scr[...] = jnp.zeros_like(gni_scr)
            rs_start()

        dh = jnp.dot(dp_ref[...], w_ref[...], preferred_element_type=F32)
        xv = x_ref[...]
        r = lax.rsqrt(jnp.mean(xv * xv, axis=-1, keepdims=True) + RMS_EPS)
        xn = xv * r
        u = dh * g_ref[...]
        gx_ref[...] = dx2_ref[...] + r * (u - xn * jnp.mean(u * xn, axis=-1, keepdims=True))
        gni_scr[...] += jnp.sum(dh * xn, axis=0, keepdims=True)

        @pl.when(i == relay_step)
        def _():
            rs_relay()

        @pl.when(i == steps - 1)
        def _():
            row = lax.broadcasted_iota(jnp.int32, (8, D_MODEL), 0)
            lane = lax.broadcasted_iota(jnp.int32, (8, D_MODEL), 1)
            slab = jnp.where(row == ROW_NORM_IN, gni_scr[...], jnp.where(row == ROW_NORM_FINAL, gnf_ref[...], gs_ref[...]))
            myslab[...] = jnp.where((row == ROW_SINKS) & (lane == LOSS_LANE), lp_ref[0:1, 0:1], slab)
            slab_start()
            gwin_ref[...] = rs_finish()
            gsum_ref[...] = slab_finish()

    const = lambda i: (0, 0)
    return pl.pallas_call(
        body, name="in_bwd", grid=(steps,),
        in_specs=[pl.BlockSpec((tm, D_PROJ), lambda i: (i, 0)), pl.BlockSpec(memory_space=pltpu.VMEM),
                  pl.BlockSpec((tm, D_MODEL), lambda i: (i, 0)), pl.BlockSpec((tm, D_MODEL), lambda i: (i, 0)),
                  pl.BlockSpec((1, D_MODEL), const), pl.BlockSpec(memory_space=pl.ANY),
                  pl.BlockSpec((8, D_MODEL), const), pl.BlockSpec((1, D_MODEL), const), pl.BlockSpec((8, 128), const)],
        out_specs=(pl.BlockSpec((tm, D_MODEL), lambda i: (i, 0)), pl.BlockSpec((SHARD_IN, D_MODEL), const),
                   pl.BlockSpec((8, D_MODEL), const)),
        out_shape=(jax.ShapeDtypeStruct((SEQ, D_MODEL), F32), jax.ShapeDtypeStruct((SHARD_IN, D_MODEL), F32),
                   jax.ShapeDtypeStruct((8, D_MODEL), F32)),
        scratch_shapes=[pltpu.VMEM((1, D_MODEL), F32), pltpu.VMEM((SHARD_IN, D_MODEL), BF16),
                        pltpu.VMEM((2, SHARD_IN, D_MODEL), BF16), pltpu.VMEM((2, HALF_IN, D_MODEL), BF16),
                        pltpu.VMEM((2, HALF_IN, D_MODEL), BF16),
                        pltpu.VMEM((8, D_MODEL), F32), pltpu.VMEM((N_DEV, 8, D_MODEL), F32),
                        pltpu.SemaphoreType.DMA((N_ICI_SUM_SEMS + 7,)), pltpu.SemaphoreType.DMA((N_ICI_SUM_SEMS + 7,)),
                        pltpu.SemaphoreType.DMA((3,))],
        compiler_params=_params(dimension_semantics=("arbitrary",)),
    )(dproj, w_full, x, dx2, norm_in, dw_in_chip, gslab, gnf, loss_part)


def _dw_in_rs(dproj, h, dw_out_sh):
    tn = 640
    steps = D_PROJ // tn
    forward_step = 2
    half_step = (D_PROJ // 2) // tn

    def body(a_ref, b_ref, dwo_ref, chip_ref, gwo_ref, dwt, d2d_in, own, d2d, ici, send_sems, recv_sems, local_sems):
        i = pl.program_id(0)
        rs_start, rs_forward, rs_finish = _shard_sum(dwo_ref, own, d2d, ici, send_sems, recv_sems, local_sems)
        pair_send, pair_finish = _chip_sum(dwt, d2d_in, chip_ref, send_sems, recv_sems, local_sems,
                                           N_SHARD_SUM_SEMS, 4)

        @pl.when(i == 0)
        def _():
            rs_start()

        @pl.when(i == half_step)
        def _():
            pair_send(first=True)

        tile = lax.dot_general(a_ref[...], b_ref[...], _TN, preferred_element_type=F32).astype(BF16)
        dwt[pl.ds(pl.multiple_of(i * tn, tn), tn), :] = tile

        @pl.when(i == forward_step)
        def _():
            rs_forward()

        @pl.when(i == steps - 1)
        def _():
            pair_send(first=False)
            gwo_ref[...] = rs_finish()
            pair_finish()

    return pl.pallas_call(
        body, name="dw_in", grid=(steps,),
        in_specs=[pl.BlockSpec((SEQ, tn), lambda i: (0, i)), pl.BlockSpec(memory_space=pltpu.VMEM),
                  pl.BlockSpec(memory_space=pl.ANY)],
        out_specs=(pl.BlockSpec(memory_space=pl.ANY), pl.BlockSpec((SHARD_OUT, D_MODEL), lambda i: (0, 0))),
        out_shape=(jax.ShapeDtypeStruct((4, SHARD_IN, D_MODEL), BF16), jax.ShapeDtypeStruct((SHARD_OUT, D_MODEL), F32)),
        scratch_shapes=[pltpu.VMEM((D_PROJ, D_MODEL), BF16), pltpu.VMEM((4, SHARD_IN, D_MODEL), BF16),
                        *_shard_sum_scratch(SHARD_OUT),
                        pltpu.SemaphoreType.DMA((N_SHARD_SUM_SEMS + 4,)), pltpu.SemaphoreType.DMA((N_SHARD_SUM_SEMS + 4,)),
                        pltpu.SemaphoreType.DMA((8,))],
        compiler_params=_params(dimension_semantics=("arbitrary",)),
    )(dproj, h, dw_out_sh)


def _matmul_tn(a, b, tn, name):
    k, n = a.shape
    _, m = b.shape

    def body(a_ref, b_ref, o_ref):
        o_ref[...] = lax.dot_general(a_ref[...], b_ref[...], _TN, preferred_element_type=F32).astype(BF16)

    return pl.pallas_call(
        body, name=name, grid=(n // tn,),
        in_specs=[pl.BlockSpec((k, tn), lambda i: (0, i)), pl.BlockSpec(memory_space=pltpu.VMEM)],
        out_specs=pl.BlockSpec((tn, m), lambda i: (i, 0)),
        out_shape=jax.ShapeDtypeStruct((n, m), BF16),
        compiler_params=_params(dimension_semantics=("arbitrary",)),
    )(a, b)


def _adam_all(big_in, big_out, gsum, small):
    steps = 4
    tr_in, tr_out = SHARD_IN // steps, SHARD_OUT // steps

    def body(*refs):
        ins, outs = refs[:8 + 1 + 18], refs[8 + 1 + 18:]
        i = pl.program_id(0)
        for b in range(2):
            w_ref, g_ref, m_ref, v_ref = ins[4 * b:4 * b + 4]
            g = g_ref[...]
            delta, mn, vn = _adamw(w_ref[...], g, m_ref[...], v_ref[...])
            for ref, val in zip(outs[4 * b:4 * b + 4], (g, delta, mn, vn)):
                ref[...] = val

        @pl.when(i == 0)
        def _():
            gsum = ins[8][...]
            idx = _slot(lax.axis_index("x"), lax.axis_index("y"), lax.axis_index("c"))
            cg = jnp.zeros((3, SHARD_CONV), F32)
            for d in range(N_DEV):
                cg = jnp.where(idx == d, gsum[ROW_CONV0:ROW_CONV0 + 3, d * SHARD_CONV:(d + 1) * SHARD_CONV], cg)
            grads = (gsum[ROW_NORM_IN:ROW_NORM_IN + 1], gsum[ROW_SINKS:ROW_SINKS + 1, 0:N_Q_HEADS],
                     gsum[ROW_NORM_CONV:ROW_NORM_CONV + 1], gsum[ROW_NORM_ATTN:ROW_NORM_ATTN + 1],
                     gsum[ROW_NORM_FINAL:ROW_NORM_FINAL + 1], cg)
            for s, g in enumerate(grads):
                w_ref, m_ref, v_ref = ins[9 + 3 * s:12 + 3 * s]
                delta, mn, vn = _adamw(w_ref[...], g, m_ref[...], v_ref[...])
                for ref, val in zip(outs[8 + 4 * s:12 + 4 * s], (g, delta, mn, vn)):
                    ref[...] = val
            outs[32][...] = gsum[ROW_SINKS:ROW_SINKS + 1, LOSS_LANE:LOSS_LANE + 1]

    const = lambda i: (0, 0)
    rows = lambda i: (i, 0)
    small_shapes = [a.shape for a in small[::3]]
    in_specs = ([pl.BlockSpec((tr_in, D_MODEL), rows)] * 4 + [pl.BlockSpec((tr_out, D_MODEL), rows)] * 4
                + [pl.BlockSpec((8, D_MODEL), const)] + [pl.BlockSpec(a.shape, const) for a in small])
    out_specs = ([pl.BlockSpec((tr_in, D_MODEL), rows)] * 4 + [pl.BlockSpec((tr_out, D_MODEL), rows)] * 4
                 + [pl.BlockSpec(s, const) for s in small_shapes for _ in range(4)] + [pl.BlockSpec((1, 1), const)])
    out_shape = ([jax.ShapeDtypeStruct((SHARD_IN, D_MODEL), F32)] * 4 + [jax.ShapeDtypeStruct((SHARD_OUT, D_MODEL), F32)] * 4
                 + [jax.ShapeDtypeStruct(s, F32) for s in small_shapes for _ in range(4)]
                 + [jax.ShapeDtypeStruct((1, 1), F32)])
    outs = pl.pallas_call(
        body, name="adam", grid=(steps,), in_specs=in_specs, out_specs=tuple(out_specs), out_shape=tuple(out_shape),
        compiler_params=_params(dimension_semantics=("arbitrary",)),
    )(*big_in, *big_out, gsum, *small)
    return outs[0:4], outs[4:8], [outs[8 + 4 * s:12 + 4 * s] for s in range(6)], outs[32]


def _pad_rows(a, rows=8):
    return jnp.pad(a, ((0, rows - a.shape[0]), (0, 0)))


def kernel(x, norm_in, w_in, conv_w, attn_sinks, norm_conv_out, norm_attn_out, w_out, norm_final, loss_target, m_norm_in, m_w_in, m_conv_w, m_attn_sinks, m_norm_conv_out, m_norm_attn_out, m_w_out, m_norm_final, v_norm_in, v_w_in, v_conv_w, v_attn_sinks, v_norm_conv_out, v_norm_attn_out, v_w_out, v_norm_final):
    x2d = x.reshape(SEQ, D_MODEL)
    target = loss_target.reshape(SEQ, D_MODEL)
    nf = norm_final.reshape(1, D_MODEL)

    w_in_t, m_w_in_t, v_w_in_t = w_in[0].T, m_w_in[0].T, v_w_in[0].T
    tiles = jnp.asarray(TILE_ORDER, jnp.int32)[2 * lax.axis_index("x") + lax.axis_index("y")]
    w_in_full, h, proj, w_out_b, conv_full = _gather_in_proj(x2d, norm_in, w_in_t, w_out[0], _pad_rows(conv_w[0]), tiles)
    sinks = attn_sinks.reshape(N_Q_HEADS)

    mixed, attn, probs, shares, g_out = _mix_fwd(proj, conv_full, sinks, norm_conv_out, norm_attn_out, w_out_b)
    dx2, dx2b, dmixed, gnf, loss_part = _out_proj_loss(mixed, x2d, target, g_out.reshape(D_MIX, D_MODEL), nf)
    dproj, gslab = _mix_bwd(proj, dmixed, attn, probs, shares, conv_full, norm_conv_out, norm_attn_out)
    dw_out = _matmul_tn(mixed, dx2b, 512, "dw_out")
    dw_in_chip, g_w_out = _dw_in_rs(dproj, h, dw_out.reshape(N_DEV, SHARD_OUT, D_MODEL))
    grad_x, g_w_in, gsum = _in_bwd_rs(dproj, w_in_full, x2d, dx2, norm_in, dw_in_chip, gslab, gnf, loss_part)

    small = (norm_in, m_norm_in, v_norm_in, attn_sinks, m_attn_sinks, v_attn_sinks,
             norm_conv_out, m_norm_conv_out, v_norm_conv_out, norm_attn_out, m_norm_attn_out, v_norm_attn_out,
             nf, m_norm_final.reshape(1, D_MODEL), v_norm_final.reshape(1, D_MODEL),
             conv_w[0], m_conv_w[0], v_conv_w[0])
    big_in, big_out, (s_ni, s_sk, s_nc, s_na, s_nf, s_cv), loss = _adam_all(
        (w_in_t, g_w_in, m_w_in_t, v_w_in_t), (w_out[0], g_w_out, m_w_out[0], v_w_out[0]), gsum, small)

    def leaves(k):
        return (s_ni[k], big_in[k].T[None], s_cv[k][None], s_sk[k], s_nc[k], s_na[k], big_out[k][None],
                s_nf[k].reshape(D_MODEL))

    return (loss.reshape(()), grad_x.reshape(1, SEQ, D_MODEL), *leaves(0), *leaves(1), *leaves(2), *leaves(3))
```

```python
import functools
import math

import jax
import jax.numpy as jnp
from jax import lax
from jax.experimental import pallas as pl
from jax.experimental.pallas import tpu as pltpu

F32 = jnp.float32
BF16 = jnp.bfloat16
MESH = pl.DeviceIdType.MESH

N_DEV = 8
SEQ = 2048
D_MODEL = 1024
D_CONV = 1024
D_ATTN = 1024
D_KV = 128
HEAD_DIM = 64
N_Q_HEADS = 16
N_PAIRS = N_Q_HEADS // 2
PAIRS_PER_KV = N_PAIRS // 2
D_MIX = D_CONV + D_ATTN
D_PROJ = 6400
SHARD_IN = D_PROJ // N_DEV
SHARD_OUT = D_MIX // N_DEV
SHARD_CONV = D_CONV // N_DEV
OFF_CB, OFF_CC, OFF_CU, OFF_GC, OFF_Q, OFF_K, OFF_V, OFF_GA = 0, 1024, 2048, 3072, 4096, 5120, 5248, 5376
BLOCK = 128
N_BLOCKS = SEQ // BLOCK
HALO = 8
CHUNK = 16
N_CHUNKS = BLOCK // CHUNK
RMS_EPS = 1e-5
NEG = -1e30
SCALE = HEAD_DIM ** -0.5
SLOPES = tuple(2.0 ** (-8.0 * (h + 1) / N_Q_HEADS) for h in range(N_Q_HEADS))

ADAM_LR = 0.001
ADAM_B1 = 0.9
ADAM_B2 = 0.999
ADAM_EPS = 1e-08
ADAM_WD = 0.01
ADAM_STEP = 10

ROW_NORM_IN, ROW_NORM_CONV, ROW_NORM_ATTN, ROW_NORM_FINAL, ROW_CONV0, ROW_SINKS = 0, 1, 2, 3, 4, 7
LOSS_LANE = N_Q_HEADS
ACC_NORM_CONV, ACC_NORM_ATTN, ACC_CONV0, N_ACC = 0, 1, 2, 5

VMEM_LIMIT = 56 * 1024 * 1024

_NT = (((1,), (1,)), ((), ()))
_TN = (((0,), (0,)), ((), ()))


def _params(**kw):
    return pltpu.CompilerParams(vmem_limit_bytes=VMEM_LIMIT, **kw)


def _adamw(w, g, m, v):
    m = ADAM_B1 * m + (1.0 - ADAM_B1) * g
    v = ADAM_B2 * v + (1.0 - ADAM_B2) * (g * g)
    m_hat = m / (1.0 - ADAM_B1 ** ADAM_STEP)
    v_hat = v / (1.0 - ADAM_B2 ** ADAM_STEP)
    delta = -ADAM_LR * (m_hat / (jnp.sqrt(v_hat) + ADAM_EPS) + ADAM_WD * w)
    return delta, m, v


def _sigmoid(t):
    return 1.0 / (1.0 + jnp.exp(-t))


def _slot(px, py, pc):
    return 4 * px + 2 * py + pc


HALF_IN = SHARD_IN // 2
N_GATHER_KINDS = 13


IN_PROJ_TILE = 640
TILE_ORDER = ((0, 1, 2, 3, 4, 5, 6, 7, 8, 9), (3, 4, 0, 1, 2, 8, 9, 5, 6, 7),
              (5, 6, 0, 1, 7, 8, 9, 2, 3, 4), (8, 9, 3, 4, 5, 6, 7, 0, 1, 2))
TILES_OWN, TILES_NEIGHBOURS = 2, 7


def _gather_in_proj(x, norm_in, w_in_sh, w_out_sh, conv_sh, tiles):
    tn = IN_PROJ_TILE
    steps = D_PROJ // tn
    tm = 256

    def body(tiles_ref, x_ref, g_ref, win_ref, wout_ref, cv_ref, wt_ref, h_ref, proj_ref, gout_ref, conv_ref,
             gin_ref, gcv_ref, wob_ref, send_sems, recv_sems, local_sems):
        p = pl.program_id(0)
        wout_start, wout_forward, wout_finish = _wout_gather(wob_ref, gout_ref, send_sems, recv_sems, local_sems.at[1],
                                                             N_GATHER_KINDS + 7)
        local_sem = local_sems.at[0]
        x, y, c = lax.axis_index("x"), lax.axis_index("y"), lax.axis_index("c")
        me, sibling = (x, y, c), (x, y, 1 - c)
        nx, ny, dg = (1 - x, y, c), (x, 1 - y, c), (1 - x, 1 - y, c)

        def other(dev):
            return (dev[0], dev[1], 1 - dev[2])

        def shard(dev):
            return gin_ref.at[pl.ds(pl.multiple_of(_slot(*dev) * SHARD_IN, 16), SHARD_IN), :]

        def half(dev, h):
            return gin_ref.at[pl.ds(pl.multiple_of(_slot(*dev) * SHARD_IN + h * HALF_IN, 16), HALF_IN), :]

        def rc(ref, k, to):
            return pltpu.make_async_remote_copy(src_ref=ref, dst_ref=ref, send_sem=send_sems.at[k],
                                                recv_sem=recv_sems.at[k], device_id=to, device_id_type=MESH)

        def cv(k, dev, to):
            s = _slot(*dev)
            return pltpu.make_async_remote_copy(src_ref=gcv_ref.at[s], dst_ref=gcv_ref.at[s],
                                                send_sem=send_sems.at[N_GATHER_KINDS + k],
                                                recv_sem=recv_sems.at[N_GATHER_KINDS + k], device_id=to, device_id_type=MESH)

        def own_copies():
            return [rc(shard(me), 0, sibling),
                    rc(half(me, 0), 1, nx), rc(half(me, 1), 2, nx),
                    rc(half(me, 1), 4, ny), rc(half(me, 0), 3, ny),
                    cv(0, me, sibling)] + [cv(1 + j, me, peer) for j, peer in enumerate((nx, ny, dg))]

        def pass_on(dev, h, k_in, k_ici, k_d2d):
            rc(half(dev, h), k_in, me).wait_recv()
            if k_ici is not None:
                rc(half(dev, h), k_ici, ny if dev is nx else nx).start()
            rc(half(dev, h), k_d2d, sibling).start()

        @pl.when(p == 0)
        def _():
            gin_ref[pl.ds(pl.multiple_of(_slot(*me) * SHARD_IN, 16), SHARD_IN), :] = win_ref[...].astype(BF16)
            gcv_ref[_slot(*me)] = cv_ref[...]
            for cp in own_copies():
                cp.start()
            wob_ref[...] = wout_ref[...].astype(BF16)
            for t in range(SEQ // tm):
                xv = x_ref[tm * t:tm * (t + 1), :]
                r = lax.rsqrt(jnp.mean(xv * xv, axis=-1, keepdims=True) + RMS_EPS)
                h_ref[tm * t:tm * (t + 1), :] = (xv * r * g_ref[...]).astype(BF16)
            rc(shard(sibling), 0, me).wait_recv()

        @pl.when(p == TILES_OWN)
        def _():
            for args in ((nx, 0, 1, 5, 7), (ny, 1, 4, 6, 10), (nx, 1, 2, None, 8), (ny, 0, 3, None, 9)):
                pass_on(*args)
            for j, peer in enumerate((nx, ny, dg)):
                cv(1 + j, peer, me).wait_recv()
                cv(4 + j, peer, sibling).start()
            for (dev, h), k in (((nx, 0), 7), ((nx, 1), 8), ((ny, 0), 9), ((ny, 1), 10)):
                rc(half(other(dev), h), k, me).wait_recv()
            wout_start()

        @pl.when(p == TILES_NEIGHBOURS)
        def _():
            pass_on(dg, 0, 5, None, 11)
            pass_on(dg, 1, 6, None, 12)
            for (dev, h), k in (((dg, 0), 11), ((dg, 1), 12)):
                rc(half(other(dev), h), k, me).wait_recv()
            pltpu.make_async_copy(gin_ref, wt_ref, local_sem).start()

        @pl.when(p == steps - 1)
        def _():
            wout_forward()

        w = gin_ref[pl.ds(pl.multiple_of(tiles_ref[p] * tn, tn), tn), :]
        proj_ref[...] = lax.dot_general(h_ref[...], w, _NT, preferred_element_type=F32)

        @pl.when(p == steps - 1)
        def _():
            cv(0, sibling, me).wait_recv()
            for j, peer in enumerate((nx, ny, dg)):
                cv(4 + j, other(peer), me).wait_recv()
            for d in range(N_DEV):
                conv_ref[:, d * SHARD_CONV:(d + 1) * SHARD_CONV] = gcv_ref[d]
            relayed = [rc(half(nx, 0), 5, ny), rc(half(ny, 1), 6, nx)]
            relayed += [rc(half(dev, h), k, sibling) for (dev, h), k in
                        (((nx, 0), 7), ((nx, 1), 8), ((ny, 0), 9), ((ny, 1), 10), ((dg, 0), 11), ((dg, 1), 12))]
            relayed += [cv(4 + j, peer, sibling) for j, peer in enumerate((nx, ny, dg))]
            for cp in own_copies() + relayed:
                cp.wait_send()
            pltpu.make_async_copy(gin_ref, wt_ref, local_sem).wait()
            wout_finish()

    vmem = pl.BlockSpec(memory_space=pltpu.VMEM)
    grid_spec = pltpu.PrefetchScalarGridSpec(
        num_scalar_prefetch=1, grid=(steps,),
        in_specs=[vmem, vmem, vmem, vmem, vmem],
        out_specs=(pl.BlockSpec(memory_space=pl.ANY), vmem,
                   pl.BlockSpec((SEQ, tn), lambda p, tiles_ref: (0, tiles_ref[p])), pl.BlockSpec(memory_space=pl.ANY), vmem),
        scratch_shapes=[pltpu.VMEM((D_PROJ, D_MODEL), BF16), pltpu.VMEM((N_DEV, 8, SHARD_CONV), F32),
                        pltpu.VMEM((SHARD_OUT, D_MODEL), BF16),
                        pltpu.SemaphoreType.DMA((N_GATHER_KINDS + 14,)), pltpu.SemaphoreType.DMA((N_GATHER_KINDS + 14,)),
                        pltpu.SemaphoreType.DMA((2,))])
    return pl.pallas_call(
        body, name="gather_in_proj", grid_spec=grid_spec,
        out_shape=(jax.ShapeDtypeStruct((D_PROJ, D_MODEL), BF16), jax.ShapeDtypeStruct((SEQ, D_MODEL), BF16),
                   jax.ShapeDtypeStruct((SEQ, D_PROJ), F32), jax.ShapeDtypeStruct((N_DEV, SHARD_OUT, D_MODEL), BF16),
                   jax.ShapeDtypeStruct((8, D_CONV), F32)),
        compiler_params=_params(dimension_semantics=("arbitrary",)),
    )(tiles, x, norm_in, w_in_sh, w_out_sh, conv_sh)


def _wout_gather(wo_ref, gout_ref, send_sems, recv_sems, local_sem, base=0):
    x, y, c = lax.axis_index("x"), lax.axis_index("y"), lax.axis_index("c")
    me, sibling = (x, y, c), (x, y, 1 - c)
    chips = [(1 - x, y), (x, 1 - y), (1 - x, 1 - y)]

    def copy(k, block, to, src=None):
        rows = gout_ref.at[_slot(*block)]
        return pltpu.make_async_remote_copy(src_ref=rows if src is None else src, dst_ref=rows,
                                            send_sem=send_sems.at[base + k], recv_sem=recv_sems.at[base + k],
                                            device_id=to, device_id_type=MESH)

    def mine():
        return pltpu.make_async_copy(wo_ref, gout_ref.at[_slot(*me)], local_sem)

    def start():
        mine().start()
        copy(0, me, sibling, src=wo_ref).start()
        for j, chip in enumerate(chips):
            copy(1 + j, me, (*chip, c), src=wo_ref).start()

    def forward():
        for j, chip in enumerate(chips):
            copy(1 + j, (*chip, c), me).wait_recv()
            copy(4 + j, (*chip, c), sibling).start()

    def finish():
        copy(0, sibling, me).wait_recv()
        for j, chip in enumerate(chips):
            copy(4 + j, (*chip, 1 - c), me).wait_recv()
        copy(0, me, sibling, src=wo_ref).wait_send()
        for j, chip in enumerate(chips):
            copy(1 + j, me, (*chip, c), src=wo_ref).wait_send()
            copy(4 + j, (*chip, c), sibling).wait_send()
        mine().wait()

    return start, forward, finish


def _shard_sum(src, own, d2d, ici, send_sems, recv_sems, local_sems, base=0):
    x, y, c = lax.axis_index("x"), lax.axis_index("y"), lax.axis_index("c")
    sibling = (x, y, 1 - c)
    chips = [(x, y), (1 - x, y), (x, 1 - y), (1 - x, 1 - y)]

    def rcopy(s, d, k, to):
        return pltpu.make_async_remote_copy(src_ref=s, dst_ref=d, send_sem=send_sems.at[base + k],
                                            recv_sem=recv_sems.at[base + k], device_id=to, device_id_type=MESH)

    def mine(k):
        return pltpu.make_async_copy(src.at[_slot(*chips[k], c)], own.at[k], local_sems.at[k])

    def to_sibling(k):
        return rcopy(src.at[_slot(*chips[k], 1 - c)], d2d.at[k], k, sibling)

    def to_chip(k):
        return rcopy(own.at[k], ici.at[k - 1], 3 + k, (*chips[k], c))

    def start():
        for k in range(4):
            mine(k).start()
            to_sibling(k).start()

    def forward():
        for k in range(1, 4):
            mine(k).wait()
            to_sibling(k).wait_recv()
            own[k] = (own[k].astype(F32) + d2d[k].astype(F32)).astype(BF16)
            to_chip(k).start()

    def finish():
        mine(0).wait()
        to_sibling(0).wait_recv()
        acc = own[0].astype(F32) + d2d[0].astype(F32)
        for k in range(1, 4):
            to_chip(k).wait_recv()
            acc = acc + ici[k - 1].astype(F32)
        for k in range(4):
            to_sibling(k).wait_send()
        for k in range(1, 4):
            to_chip(k).wait_send()
        return acc

    return start, forward, finish


def _shard_sum_scratch(rows):
    return [pltpu.VMEM((4, rows, D_MODEL), BF16), pltpu.VMEM((4, rows, D_MODEL), BF16),
            pltpu.VMEM((3, rows, D_MODEL), BF16)]


N_SHARD_SUM_SEMS = 7


def _chip_sum(dwt, d2d, out_hbm, send_sems, recv_sems, local_sems, base, local_base):
    x, y, c = lax.axis_index("x"), lax.axis_index("y"), lax.axis_index("c")
    sibling = (x, y, 1 - c)
    chips = [(x, y), (1 - x, y), (x, 1 - y), (1 - x, 1 - y)]

    def shard(s):
        return dwt.at[pl.ds(pl.multiple_of(s * SHARD_IN, 16), SHARD_IN), :]

    def to_sibling(k):
        return pltpu.make_async_remote_copy(src_ref=shard(_slot(*chips[k], 1 - c)), dst_ref=d2d.at[k],
                                            send_sem=send_sems.at[base + k], recv_sem=recv_sems.at[base + k],
                                            device_id=sibling, device_id_type=MESH)

    def save(k):
        return pltpu.make_async_copy(d2d.at[k], out_hbm.at[k], local_sems.at[local_base + k])

    def send(first):
        for k in range(4):
            in_first = _slot(*chips[k], 1 - c) < N_DEV // 2

            @pl.when(in_first if first else jnp.logical_not(in_first))
            def _():
                to_sibling(k).start()

    def finish():
        for k in range(4):
            to_sibling(k).wait_recv()
            d2d[k] = (shard(_slot(*chips[k], c))[...].astype(F32) + d2d[k].astype(F32)).astype(BF16)
            save(k).start()
        for k in range(4):
            save(k).wait()
            to_sibling(k).wait_send()

    return send, finish


N_ICI_SUM_SEMS = 6


def _ici_sum(src, own, ici, via, stage, send_sems, recv_sems, local_sems, base=0):
    x, y, c = lax.axis_index("x"), lax.axis_index("y"), lax.axis_index("c")
    nx, ny = (1 - x, y, c), (x, 1 - y, c)
    OWN, NX, NY, DG = range(4)

    def half(ref, h):
        return ref.at[pl.ds(h * HALF_IN, HALF_IN), :]

    def rc(s, d, k, to):
        return pltpu.make_async_remote_copy(src_ref=s, dst_ref=d, send_sem=send_sems.at[base + k],
                                            recv_sem=recv_sems.at[base + k], device_id=to, device_id_type=MESH)

    for_dg_0 = lambda: rc(half(src.at[DG], 0), via.at[0], 0, nx)
    for_dg_1 = lambda: rc(half(src.at[DG], 1), via.at[1], 1, ny)
    for_nx_0 = lambda: rc(half(src.at[NX], 0), half(ici.at[0], 0), 2, nx)
    for_ny_1 = lambda: rc(half(src.at[NY], 1), half(ici.at[1], 1), 3, ny)
    for_ny_0 = lambda: rc(stage.at[0], half(ici.at[1], 0), 4, ny)
    for_nx_1 = lambda: rc(stage.at[1], half(ici.at[0], 1), 5, nx)
    mine = lambda: pltpu.make_async_copy(src.at[OWN], own, local_sems.at[0])
    stage_0 = lambda: pltpu.make_async_copy(half(src.at[NY], 0), stage.at[0], local_sems.at[1])
    stage_1 = lambda: pltpu.make_async_copy(half(src.at[NX], 1), stage.at[1], local_sems.at[2])

    def start():
        for cp in (for_dg_0, for_dg_1, for_nx_0, for_ny_1, stage_0, stage_1, mine):
            cp().start()

    def relay():
        for h, staged, landed, out in ((0, stage_0, for_dg_0, for_ny_0), (1, stage_1, for_dg_1, for_nx_1)):
            staged().wait()
            landed().wait_recv()
            stage[h] = (stage[h].astype(F32) + via[h].astype(F32)).astype(BF16)
            out().start()

    def finish():
        mine().wait()
        for cp in (for_nx_0, for_nx_1, for_ny_1, for_ny_0):
            cp().wait_recv()
        acc = own[...].astype(F32) + ici[0].astype(F32) + ici[1].astype(F32)
        for cp in (for_dg_0, for_dg_1, for_nx_0, for_ny_1, for_ny_0, for_nx_1):
            cp().wait_send()
        return acc

    return start, relay, finish


def _slab_sum(myslab, slabs, send_sems, recv_sems, base):
    x, y, c = lax.axis_index("x"), lax.axis_index("y"), lax.axis_index("c")
    me = _slot(x, y, c)
    peers = [(x, y, 1 - c), (1 - x, y, c), (x, 1 - y, c), (1 - x, 1 - y, c),
             (1 - x, y, 1 - c), (x, 1 - y, 1 - c), (1 - x, 1 - y, 1 - c)]

    def cp(k):
        return pltpu.make_async_remote_copy(src_ref=myslab, dst_ref=slabs.at[me], send_sem=send_sems.at[base + k],
                                            recv_sem=recv_sems.at[base + k], device_id=peers[k], device_id_type=MESH)

    def start():
        slabs[me] = myslab[...]
        for k in range(7):
            cp(k).start()

    def finish():
        for k in range(7):
            cp(k).wait_recv()
        total = slabs[0]
        for d in range(1, N_DEV):
            total = total + slabs[d]
        for k in range(7):
            cp(k).wait_send()
        return total

    return start, finish


def _chunk_rows(r):
    return pl.ds(pl.multiple_of(r * CHUNK, CHUNK), CHUNK)


def _conv_halo(cch_ref, cuh_ref, n):
    zh = jnp.where(n > 0, cch_ref[...] * cuh_ref[...], 0.0)
    return jnp.concatenate([zh] * (CHUNK // HALO), axis=0)


def _conv_chunk(pj_ref, zhalo, cw, r):
    rows = _chunk_rows(r)
    cc = pj_ref[rows, OFF_CC:OFF_CC + D_CONV]
    cu = pj_ref[rows, OFF_CU:OFF_CU + D_CONV]
    z = cc * cu
    before = _chunk_rows(jnp.maximum(r - 1, 0))
    zprev = jnp.where(r > 0, pj_ref[before, OFF_CC:OFF_CC + D_CONV] * pj_ref[before, OFF_CU:OFF_CU + D_CONV], zhalo)
    row = lax.broadcasted_iota(jnp.int32, (CHUNK, D_CONV), 0)
    z1 = jnp.where(row < 1, pltpu.roll(zprev, 1, 0), pltpu.roll(z, 1, 0))
    z2 = jnp.where(row < 2, pltpu.roll(zprev, 2, 0), pltpu.roll(z, 2, 0))
    co = cw[0] * z2 + cw[1] * z1 + cw[2] * z
    return cc, cu, z, z1, z2, co


def _gated_norm(a, gain, t):
    r = lax.rsqrt(jnp.mean(a * a, axis=-1, keepdims=True) + RMS_EPS)
    return a * r * gain * (t * _sigmoid(t))


def _kv_bands(pj, kvp_ref):
    lane = lax.broadcasted_iota(jnp.int32, (2 * BLOCK, D_KV), 1)
    lo = lane < HEAD_DIM

    def bands(prev, cur):
        b = jnp.concatenate([prev, cur], axis=0)
        br = pltpu.roll(b, HEAD_DIM, 1)
        zero = jnp.zeros_like(b)
        return ((jnp.where(lo, b, zero).astype(BF16), jnp.where(lo, zero, br).astype(BF16)),
                (jnp.where(lo, br, zero).astype(BF16), jnp.where(lo, zero, b).astype(BF16)))

    ks = bands(kvp_ref[:, 0:D_KV], pj[:, OFF_K:OFF_K + D_KV])
    vs = bands(kvp_ref[:, D_KV:2 * D_KV], pj[:, OFF_V:OFF_V + D_KV])
    return ks, vs


STACK = PAIRS_PER_KV * BLOCK


def _head(j, i, e):
    return 2 * (PAIRS_PER_KV * j + i) + e


def _pair_cols(j, i, off):
    p = PAIRS_PER_KV * j + i
    return slice(off + 128 * p, off + 128 * (p + 1))


def _fill_attn_bias(bias_scr, first_block):
    qi = lax.broadcasted_iota(jnp.int32, (BLOCK, 2 * BLOCK), 0)
    kj = lax.broadcasted_iota(jnp.int32, (BLOCK, 2 * BLOCK), 1)
    dist = BLOCK + qi - kj
    valid = (dist >= 0) & (dist < BLOCK)
    if first_block:
        valid = valid & (kj >= BLOCK)
    distf = dist.astype(F32)
    for j in range(2):
        for e in range(2):
            for i in range(PAIRS_PER_KV):
                bias_scr[2 * j + e, BLOCK * i:BLOCK * (i + 1), :] = jnp.where(valid, -SLOPES[_head(j, i, e)] * distf, NEG)


def _q_stack(pj, j):
    return jnp.concatenate([(pj[:, _pair_cols(j, i, OFF_Q)] * SCALE).astype(BF16) for i in range(PAIRS_PER_KV)], axis=0)


def _sink_rows(sink_ref, j, e):
    return jnp.concatenate([jnp.full((BLOCK, 128), sink_ref[_head(j, i, e)], F32) for i in range(PAIRS_PER_KV)], axis=0)


def _attn_probs(q_stack, kband, bias, sink):
    s = lax.dot_general(q_stack, kband, _NT, preferred_element_type=F32) + bias
    m = jnp.broadcast_to(jnp.max(s, axis=-1, keepdims=True), (STACK, 128))
    m = jnp.maximum(m, sink)
    p = [jnp.exp(t - m) for t in (s[:, :128], s[:, 128:])]
    es = jnp.exp(sink - m)
    ones = jnp.ones((128, 128), BF16)
    total = (jnp.dot(p[0].astype(BF16), ones, preferred_element_type=F32)
             + jnp.dot(p[1].astype(BF16), ones, preferred_element_type=F32))
    inv = 1.0 / (total + es)
    return jnp.concatenate([p[0] * inv, p[1] * inv], axis=1), es * inv


def _attn_group(pj, ks, vs, bias_scr, sink_ref, j):
    q_stack = _q_stack(pj, j)
    out, probs, shares = None, [], []
    for e in range(2):
        p, ps = _attn_probs(q_stack, ks[j][e], bias_scr[2 * j + e], _sink_rows(sink_ref, j, e))
        p = p.astype(BF16)
        o = jnp.dot(p, vs[j][e], preferred_element_type=F32)
        out = o if out is None else out + o
        probs.append(p)
        shares.append(ps)
    return out, probs, shares


def _mix_fwd(proj, conv_full, sinks, norm_conv, norm_attn):
    def body(pj_ref, kvp_ref, cch_ref, cuh_ref, cw_ref, sink_ref, gc_ref, ga_ref,
             mixed_ref, attn_scr, p_ref, ps_ref, bias_scr):
        n = pl.program_id(0)
        pj = pj_ref

        @pl.when(n == 0)
        def _():
            _fill_attn_bias(bias_scr, first_block=True)

        @pl.when(n == 1)
        def _():
            _fill_attn_bias(bias_scr, first_block=False)

        zhalo = _conv_halo(cch_ref, cuh_ref, n)
        cw = (cw_ref[0:1, :], cw_ref[1:2, :], cw_ref[2:3, :])
        gain_c = gc_ref[...]

        def conv_chunk(r, carry):
            rows = _chunk_rows(r)
            co = _conv_chunk(pj_ref, zhalo, cw, r)[-1]
            y = _gated_norm(pj_ref[rows, OFF_CB:OFF_CB + D_CONV] * co, gain_c, pj_ref[rows, OFF_GC:OFF_GC + D_CONV])
            mixed_ref[rows, 0:D_CONV] = y.astype(BF16)
            return carry

        lax.fori_loop(0, N_CHUNKS, conv_chunk, 0, unroll=True)

        ks, vs = _kv_bands(pj, kvp_ref)
        for j in range(2):
            out, probs, shares = _attn_group(pj, ks, vs, bias_scr, sink_ref, j)
            for e in range(2):
                p_ref[0, 2 * j + e] = probs[e]
                ps_ref[0, 2 * j + e] = shares[e]
            for i in range(PAIRS_PER_KV):
                attn_scr[:, _pair_cols(j, i, 0)] = out[BLOCK * i:BLOCK * (i + 1), :]
        gain_a = ga_ref[...]

        def norm_chunk(r, carry):
            rows = _chunk_rows(r)
            y = _gated_norm(attn_scr[rows, :], gain_a, pj_ref[rows, OFF_GA:OFF_GA + D_ATTN])
            mixed_ref[rows, D_CONV:D_MIX] = y.astype(BF16)
            return carry

        lax.fori_loop(0, N_CHUNKS, norm_chunk, 0, unroll=True)

    per_block = BLOCK // HALO
    return pl.pallas_call(
        body, name="mix_fwd", grid=(N_BLOCKS,),
        in_specs=[
            pl.BlockSpec((BLOCK, D_PROJ), lambda n: (n, 0)),
            pl.BlockSpec((BLOCK, 2 * D_KV), lambda n: (jnp.maximum(n - 1, 0), OFF_K // (2 * D_KV))),
            pl.BlockSpec((HALO, D_CONV), lambda n: (jnp.maximum(n * per_block - 1, 0), OFF_CC // D_CONV)),
            pl.BlockSpec((HALO, D_CONV), lambda n: (jnp.maximum(n * per_block - 1, 0), OFF_CU // D_CONV)),
            pl.BlockSpec((8, D_CONV), lambda n: (0, 0)),
            pl.BlockSpec(memory_space=pltpu.SMEM),
            pl.BlockSpec((1, D_CONV), lambda n: (0, 0)),
            pl.BlockSpec((1, D_ATTN), lambda n: (0, 0)),
        ],
        out_specs=(pl.BlockSpec((BLOCK, D_MIX), lambda n: (n, 0)), pl.BlockSpec((BLOCK, D_ATTN), lambda n: (n, 0)),
                   pl.BlockSpec((1, 4, STACK, 2 * BLOCK), lambda n: (n, 0, 0, 0)),
                   pl.BlockSpec((1, 4, STACK, 128), lambda n: (n, 0, 0, 0))),
        out_shape=(jax.ShapeDtypeStruct((SEQ, D_MIX), BF16), jax.ShapeDtypeStruct((SEQ, D_ATTN), F32),
                   jax.ShapeDtypeStruct((N_BLOCKS, 4, STACK, 2 * BLOCK), BF16),
                   jax.ShapeDtypeStruct((N_BLOCKS, 4, STACK, 128), F32)),
        scratch_shapes=[pltpu.VMEM((4, STACK, 2 * BLOCK), F32)],
        compiler_params=_params(dimension_semantics=("arbitrary",)),
    )(proj, proj, proj, proj, conv_full, sinks, norm_conv, norm_attn)


def _out_proj_loss(mixed, x, target, w_out_full, norm_final):
    tm = 256

    def body(mx_ref, x_ref, t_ref, w_ref, g_ref, dx2_ref, dx2b_ref, dmix_ref, gnf_ref, loss_ref):
        i = pl.program_id(0)
        w = w_ref[...]
        x2 = x_ref[...] + jnp.dot(mx_ref[...], w, preferred_element_type=F32)
        r = lax.rsqrt(jnp.mean(x2 * x2, axis=-1, keepdims=True) + RMS_EPS)
        xn = x2 * r
        g = g_ref[...]
        err = xn * g - t_ref[...]
        part = 0.5 * jnp.sum(jnp.mean(err * err, axis=-1, keepdims=True), axis=0, keepdims=True)
        dy = err * (1.0 / D_MODEL)
        gnf = jnp.sum(dy * xn, axis=0, keepdims=True)
        u = dy * g
        dx2 = r * (u - xn * jnp.mean(u * xn, axis=-1, keepdims=True))
        dx2_ref[...] = dx2
        dx2b = dx2.astype(BF16)
        dx2b_ref[...] = dx2b
        dmix_ref[...] = lax.dot_general(dx2b, w, _NT, preferred_element_type=F32)

        @pl.when(i == 0)
        def _():
            gnf_ref[...] = jnp.zeros_like(gnf_ref)
            loss_ref[...] = jnp.zeros_like(loss_ref)

        gnf_ref[...] += gnf
        loss_ref[...] += jnp.broadcast_to(part, loss_ref.shape)

    return pl.pallas_call(
        body, name="out_proj_loss", grid=(SEQ // tm,),
        in_specs=[pl.BlockSpec((tm, D_MIX), lambda i: (i, 0)), pl.BlockSpec((tm, D_MODEL), lambda i: (i, 0)),
                  pl.BlockSpec((tm, D_MODEL), lambda i: (i, 0)), pl.BlockSpec(memory_space=pltpu.VMEM),
                  pl.BlockSpec((1, D_MODEL), lambda i: (0, 0))],
        out_specs=(pl.BlockSpec((tm, D_MODEL), lambda i: (i, 0)), pl.BlockSpec((tm, D_MODEL), lambda i: (i, 0)),
                   pl.BlockSpec((tm, D_MIX), lambda i: (i, 0)),
                   pl.BlockSpec((1, D_MODEL), lambda i: (0, 0)), pl.BlockSpec((8, 128), lambda i: (0, 0))),
        out_shape=(jax.ShapeDtypeStruct((SEQ, D_MODEL), F32), jax.ShapeDtypeStruct((SEQ, D_MODEL), BF16),
                   jax.ShapeDtypeStruct((SEQ, D_MIX), F32),
                   jax.ShapeDtypeStruct((1, D_MODEL), F32), jax.ShapeDtypeStruct((8, 128), F32)),
        compiler_params=_params(dimension_semantics=("arbitrary",)),
    )(mixed, x, target, w_out_full, norm_final)


def _gated_norm_bwd(a, gain, t, dy):
    r = lax.rsqrt(jnp.mean(a * a, axis=-1, keepdims=True) + RMS_EPS)
    an = a * r
    sg = _sigmoid(t)
    dn = dy * (t * sg)
    dt = dy * (an * gain) * (sg * (1.0 + t * (1.0 - sg)))
    u = dn * gain
    da = r * (u - an * jnp.mean(u * an, axis=-1, keepdims=True))
    return da, dt, dn * an


def _mix_bwd(proj, dmixed, attn, probs, shares, conv_full, norm_conv, norm_attn):
    def body(pj_ref, kvp_ref, cch_ref, cuh_ref, dmx_ref, attn_ref, p_ref, ps_ref, cw_ref, gc_ref, ga_ref,
             dpj_ref, gslab_ref, dattn_scr, nxt_scr, dkv_scr, acc_scr):
        step = pl.program_id(0)
        n = N_BLOCKS - 1 - step
        pj = pj_ref

        @pl.when(step == 0)
        def _():
            gslab_ref[...] = jnp.zeros_like(gslab_ref)
            nxt_scr[...] = jnp.zeros_like(nxt_scr)
            dkv_scr[...] = jnp.zeros_like(dkv_scr)
            acc_scr[...] = jnp.zeros_like(acc_scr)

        zhalo = _conv_halo(cch_ref, cuh_ref, n)
        cw = (cw_ref[0:1, :], cw_ref[1:2, :], cw_ref[2:3, :])
        gain_c = gc_ref[...]
        row = lax.broadcasted_iota(jnp.int32, (CHUNK, D_CONV), 0)

        def conv_chunk(t, dco_after):
            r = N_CHUNKS - 1 - t
            rows = _chunk_rows(r)
            cc, cu, z, z1, z2, co = _conv_chunk(pj_ref, zhalo, cw, r)
            cb = pj_ref[rows, OFF_CB:OFF_CB + D_CONV]
            da, dgate, gterm = _gated_norm_bwd(cb * co, gain_c, pj_ref[rows, OFF_GC:OFF_GC + D_CONV],
                                               dmx_ref[rows, 0:D_CONV])
            dpj_ref[rows, OFF_GC:OFF_GC + D_CONV] = dgate.astype(BF16)
            dpj_ref[rows, OFF_CB:OFF_CB + D_CONV] = (da * co).astype(BF16)
            dco = da * cb
            dco1 = jnp.where(row >= CHUNK - 1, pltpu.roll(dco_after, CHUNK - 1, 0), pltpu.roll(dco, CHUNK - 1, 0))
            dco2 = jnp.where(row >= CHUNK - 2, pltpu.roll(dco_after, CHUNK - 2, 0), pltpu.roll(dco, CHUNK - 2, 0))
            dz = cw[2] * dco + cw[1] * dco1 + cw[0] * dco2
            dpj_ref[rows, OFF_CC:OFF_CC + D_CONV] = (dz * cu).astype(BF16)
            dpj_ref[rows, OFF_CU:OFF_CU + D_CONV] = (dz * cc).astype(BF16)
            acc_scr[ACC_NORM_CONV] += gterm
            acc_scr[ACC_CONV0] += dco * z2
            acc_scr[ACC_CONV0 + 1] += dco * z1
            acc_scr[ACC_CONV0 + 2] += dco * z
            return dco

        nxt_scr[...] = lax.fori_loop(0, N_CHUNKS, conv_chunk, nxt_scr[...], unroll=True)

        ks, vs = _kv_bands(pj, kvp_ref)
        gain_a = ga_ref[...]

        def norm_chunk(r, carry):
            rows = _chunk_rows(r)
            da, dgate, gterm = _gated_norm_bwd(attn_ref[rows, :], gain_a, pj_ref[rows, OFF_GA:OFF_GA + D_ATTN],
                                               dmx_ref[rows, D_CONV:D_MIX])
            dpj_ref[rows, OFF_GA:OFF_GA + D_ATTN] = dgate.astype(BF16)
            dattn_scr[rows, :] = da
            acc_scr[ACC_NORM_ATTN] += gterm
            return carry

        lax.fori_loop(0, N_CHUNKS, norm_chunk, 0, unroll=True)

        in_lo = lax.broadcasted_iota(jnp.int32, (128, 128), 0) < HEAD_DIM
        half_ones = (jnp.where(in_lo, 1.0, 0.0).astype(BF16), jnp.where(in_lo, 0.0, 1.0).astype(BF16))
        lane_s = lax.broadcasted_iota(jnp.int32, (1, D_MODEL), 1)
        gsink = jnp.zeros((1, D_MODEL), F32)
        dk_t, dv_t = [], []
        for j in range(2):
            q_stack = _q_stack(pj, j)
            do_f = jnp.concatenate([dattn_scr[:, _pair_cols(j, i, 0)] for i in range(PAIRS_PER_KV)], axis=0)
            o_f = jnp.concatenate([attn_ref[:, _pair_cols(j, i, 0)] for i in range(PAIRS_PER_KV)], axis=0)
            prod = (do_f * o_f).astype(BF16)
            deltas = [jnp.dot(prod, half_ones[e], preferred_element_type=F32) for e in range(2)]
            do_b = do_f.astype(BF16)
            dq, dk_j, dv_j = None, None, None
            for e in range(2):
                p = p_ref[0, 2 * j + e]
                dp = lax.dot_general(do_b, vs[j][e], _NT, preferred_element_type=F32)
                delta = jnp.concatenate([deltas[e], deltas[e]], axis=1)
                ds = (p.astype(F32) * (dp - delta)).astype(BF16)
                gs = ps_ref[0, 2 * j + e] * deltas[e]
                for i in range(PAIRS_PER_KV):
                    gs_h = -jnp.sum(gs[BLOCK * i:BLOCK * (i + 1), 0:1], axis=0, keepdims=True)
                    gsink = gsink + jnp.where(lane_s == _head(j, i, e), gs_h, 0.0)
                t = jnp.dot(ds, ks[j][e], preferred_element_type=F32)
                dq = t if dq is None else dq + t
                half = slice(HEAD_DIM * e, HEAD_DIM * (e + 1))
                a = lax.dot_general(q_stack, ds, _TN, preferred_element_type=F32)[half, :]
                b = lax.dot_general(do_b, p, _TN, preferred_element_type=F32)[half, :]
                dk_j = a if dk_j is None else dk_j + a
                dv_j = b if dv_j is None else dv_j + b
            for i in range(PAIRS_PER_KV):
                dpj_ref[:, _pair_cols(j, i, OFF_Q)] = (dq[BLOCK * i:BLOCK * (i + 1), :] * SCALE).astype(BF16)
            dk_t.append(dk_j)
            dv_t.append(dv_j)
        dk = jnp.concatenate(dk_t, axis=0).T
        dv = jnp.concatenate(dv_t, axis=0).T
        dpj_ref[:, OFF_K:OFF_K + D_KV] = (dk[BLOCK:, :] + dkv_scr[:, 0:D_KV]).astype(BF16)
        dpj_ref[:, OFF_V:OFF_V + D_KV] = (dv[BLOCK:, :] + dkv_scr[:, D_KV:2 * D_KV]).astype(BF16)
        dkv_scr[:, 0:D_KV] = dk[:BLOCK, :]
        dkv_scr[:, D_KV:2 * D_KV] = dv[:BLOCK, :]
        gslab_ref[ROW_SINKS:ROW_SINKS + 1, :] += gsink

        @pl.when(step == N_BLOCKS - 1)
        def _():
            for k, slab_row in ((ACC_NORM_CONV, ROW_NORM_CONV), (ACC_NORM_ATTN, ROW_NORM_ATTN), (ACC_CONV0, ROW_CONV0),
                                (ACC_CONV0 + 1, ROW_CONV0 + 1), (ACC_CONV0 + 2, ROW_CONV0 + 2)):
                gslab_ref[slab_row:slab_row + 1, :] = jnp.sum(acc_scr[k], axis=0, keepdims=True)

    per_block = BLOCK // HALO
    last = N_BLOCKS - 1
    return pl.pallas_call(
        body, name="mix_bwd", grid=(N_BLOCKS,),
        in_specs=[
            pl.BlockSpec((BLOCK, D_PROJ), lambda s: (last - s, 0)),
            pl.BlockSpec((BLOCK, 2 * D_KV), lambda s: (jnp.maximum(last - s - 1, 0), OFF_K // (2 * D_KV))),
            pl.BlockSpec((HALO, D_CONV), lambda s: (jnp.maximum((last - s) * per_block - 1, 0), OFF_CC // D_CONV)),
            pl.BlockSpec((HALO, D_CONV), lambda s: (jnp.maximum((last - s) * per_block - 1, 0), OFF_CU // D_CONV)),
            pl.BlockSpec((BLOCK, D_MIX), lambda s: (last - s, 0)),
            pl.BlockSpec((BLOCK, D_ATTN), lambda s: (last - s, 0)),
            pl.BlockSpec((1, 4, STACK, 2 * BLOCK), lambda s: (last - s, 0, 0, 0)),
            pl.BlockSpec((1, 4, STACK, 128), lambda s: (last - s, 0, 0, 0)),
            pl.BlockSpec((8, D_CONV), lambda s: (0, 0)),
            pl.BlockSpec((1, D_CONV), lambda s: (0, 0)),
            pl.BlockSpec((1, D_ATTN), lambda s: (0, 0)),
        ],
        out_specs=(pl.BlockSpec((BLOCK, D_PROJ), lambda s: (last - s, 0)),
                   pl.BlockSpec((8, D_MODEL), lambda s: (0, 0))),
        out_shape=(jax.ShapeDtypeStruct((SEQ, D_PROJ), BF16), jax.ShapeDtypeStruct((8, D_MODEL), F32)),
        scratch_shapes=[pltpu.VMEM((BLOCK, D_ATTN), F32), pltpu.VMEM((CHUNK, D_CONV), F32),
                        pltpu.VMEM((BLOCK, 2 * D_KV), F32), pltpu.VMEM((N_ACC, CHUNK, D_MODEL), F32)],
        compiler_params=_params(dimension_semantics=("arbitrary",)),
    )(proj, proj, proj, proj, dmixed, attn, probs, shares, conv_full, norm_conv, norm_attn)


def _in_bwd_rs(dproj, w_full, x, dx2, norm_in, dw_in_chip, gslab, gnf, loss_part):
    tm = 256
    steps = SEQ // tm
    relay_step = 4

    def body(dp_ref, w_ref, x_ref, dx2_ref, g_ref, dwi_ref, gs_ref, gnf_ref, lp_ref, gx_ref, gwin_ref, gsum_ref,
             gni_scr, own, ici, via, stage, myslab, slabs, send_sems, recv_sems, local_sems):
        i = pl.program_id(0)
        rs_start, rs_relay, rs_finish = _ici_sum(dwi_ref, own, ici, via, stage, send_sems, recv_sems, local_sems)
        slab_start, slab_finish = _slab_sum(myslab, slabs, send_sems, recv_sems, N_ICI_SUM_SEMS)

        @pl.when(i == 0)
        def _():
            gni_scr[...] = jnp.zeros_like(gni_scr)
            rs_start()

        dh = jnp.dot(dp_ref[...], w_ref[...], preferred_element_type=F32)
        xv = x_ref[...]
        r = lax.rsqrt(jnp.mean(xv * xv, axis=-1, keepdims=True) + RMS_EPS)
        xn = xv * r
        u = dh * g_ref[...]
        gx_ref[...] = dx2_ref[...] + r * (u - xn * jnp.mean(u * xn, axis=-1, keepdims=True))
        gni_scr[...] += jnp.sum(dh * xn, axis=0, keepdims=True)

        @pl.when(i == relay_step)
        def _():
            rs_relay()

        @pl.when(i == steps - 1)
        def _():
            row = lax.broadcasted_iota(jnp.int32, (8, D_MODEL), 0)
            lane = lax.broadcasted_iota(jnp.int32, (8, D_MODEL), 1)
            slab = jnp.where(row == ROW_NORM_IN, gni_scr[...], jnp.where(row == ROW_NORM_FINAL, gnf_ref[...], gs_ref[...]))
            myslab[...] = jnp.where((row == ROW_SINKS) & (lane == LOSS_LANE), lp_ref[0:1, 0:1], slab)
            slab_start()
            gwin_ref[...] = rs_finish()
            gsum_ref[...] = slab_finish()

    const = lambda i: (0, 0)
    return pl.pallas_call(
        body, name="in_bwd", grid=(steps,),
        in_specs=[pl.BlockSpec((tm, D_PROJ), lambda i: (i, 0)), pl.BlockSpec(memory_space=pltpu.VMEM),
                  pl.BlockSpec((tm, D_MODEL), lambda i: (i, 0)), pl.BlockSpec((tm, D_MODEL), lambda i: (i, 0)),
                  pl.BlockSpec((1, D_MODEL), const), pl.BlockSpec(memory_space=pl.ANY),
                  pl.BlockSpec((8, D_MODEL), const), pl.BlockSpec((1, D_MODEL), const), pl.BlockSpec((8, 128), const)],
        out_specs=(pl.BlockSpec((tm, D_MODEL), lambda i: (i, 0)), pl.BlockSpec((SHARD_IN, D_MODEL), const),
                   pl.BlockSpec((8, D_MODEL), const)),
        out_shape=(jax.ShapeDtypeStruct((SEQ, D_MODEL), F32), jax.ShapeDtypeStruct((SHARD_IN, D_MODEL), F32),
                   jax.ShapeDtypeStruct((8, D_MODEL), F32)),
        scratch_shapes=[pltpu.VMEM((1, D_MODEL), F32), pltpu.VMEM((SHARD_IN, D_MODEL), BF16),
                        pltpu.VMEM((2, SHARD_IN, D_MODEL), BF16), pltpu.VMEM((2, HALF_IN, D_MODEL), BF16),
                        pltpu.VMEM((2, HALF_IN, D_MODEL), BF16),
                        pltpu.VMEM((8, D_MODEL), F32), pltpu.VMEM((N_DEV, 8, D_MODEL), F32),
                        pltpu.SemaphoreType.DMA((N_ICI_SUM_SEMS + 7,)), pltpu.SemaphoreType.DMA((N_ICI_SUM_SEMS + 7,)),
                        pltpu.SemaphoreType.DMA((3,))],
        compiler_params=_params(dimension_semantics=("arbitrary",)),
    )(dproj, w_full, x, dx2, norm_in, dw_in_chip, gslab, gnf, loss_part)


def _dw_in_rs(dproj, h, dw_out_sh):
    tn = 640
    steps = D_PROJ // tn
    forward_step = 2
    half_step = (D_PROJ // 2) // tn

    def body(a_ref, b_ref, dwo_ref, chip_ref, gwo_ref, dwt, d2d_in, own, d2d, ici, send_sems, recv_sems, local_sems):
        i = pl.program_id(0)
        rs_start, rs_forward, rs_finish = _shard_sum(dwo_ref, own, d2d, ici, send_sems, recv_sems, local_sems)
        pair_send, pair_finish = _chip_sum(dwt, d2d_in, chip_ref, send_sems, recv_sems, local_sems,
                                           N_SHARD_SUM_SEMS, 4)

        @pl.when(i == 0)
        def _():
            rs_start()

        @pl.when(i == half_step)
        def _():
            pair_send(first=True)

        tile = lax.dot_general(a_ref[...], b_ref[...], _TN, preferred_element_type=F32).astype(BF16)
        dwt[pl.ds(pl.multiple_of(i * tn, tn), tn), :] = tile

        @pl.when(i == forward_step)
        def _():
            rs_forward()

        @pl.when(i == steps - 1)
        def _():
            pair_send(first=False)
            gwo_ref[...] = rs_finish()
            pair_finish()

    return pl.pallas_call(
        body, name="dw_in", grid=(steps,),
        in_specs=[pl.BlockSpec((SEQ, tn), lambda i: (0, i)), pl.BlockSpec(memory_space=pltpu.VMEM),
                  pl.BlockSpec(memory_space=pl.ANY)],
        out_specs=(pl.BlockSpec(memory_space=pl.ANY), pl.BlockSpec((SHARD_OUT, D_MODEL), lambda i: (0, 0))),
        out_shape=(jax.ShapeDtypeStruct((4, SHARD_IN, D_MODEL), BF16), jax.ShapeDtypeStruct((SHARD_OUT, D_MODEL), F32)),
        scratch_shapes=[pltpu.VMEM((D_PROJ, D_MODEL), BF16), pltpu.VMEM((4, SHARD_IN, D_MODEL), BF16),
                        *_shard_sum_scratch(SHARD_OUT),
                        pltpu.SemaphoreType.DMA((N_SHARD_SUM_SEMS + 4,)), pltpu.SemaphoreType.DMA((N_SHARD_SUM_SEMS + 4,)),
                        pltpu.SemaphoreType.DMA((8,))],
        compiler_params=_params(dimension_semantics=("arbitrary",)),
    )(dproj, h, dw_out_sh)


def _matmul_tn(a, b, tn, name):
    k, n = a.shape
    _, m = b.shape

    def body(a_ref, b_ref, o_ref):
        o_ref[...] = lax.dot_general(a_ref[...], b_ref[...], _TN, preferred_element_type=F32).astype(BF16)

    return pl.pallas_call(
        body, name=name, grid=(n // tn,),
        in_specs=[pl.BlockSpec((k, tn), lambda i: (0, i)), pl.BlockSpec(memory_space=pltpu.VMEM)],
        out_specs=pl.BlockSpec((tn, m), lambda i: (i, 0)),
        out_shape=jax.ShapeDtypeStruct((n, m), BF16),
        compiler_params=_params(dimension_semantics=("arbitrary",)),
    )(a, b)


def _adam_all(big_in, big_out, gsum, small):
    steps = 4
    tr_in, tr_out = SHARD_IN // steps, SHARD_OUT // steps

    def body(*refs):
        ins, outs = refs[:8 + 1 + 18], refs[8 + 1 + 18:]
        i = pl.program_id(0)
        for b in range(2):
            w_ref, g_ref, m_ref, v_ref = ins[4 * b:4 * b + 4]
            g = g_ref[...]
            delta, mn, vn = _adamw(w_ref[...], g, m_ref[...], v_ref[...])
            for ref, val in zip(outs[4 * b:4 * b + 4], (g, delta, mn, vn)):
                ref[...] = val

        @pl.when(i == 0)
        def _():
            gsum = ins[8][...]
            idx = _slot(lax.axis_index("x"), lax.axis_index("y"), lax.axis_index("c"))
            cg = jnp.zeros((3, SHARD_CONV), F32)
            for d in range(N_DEV):
                cg = jnp.where(idx == d, gsum[ROW_CONV0:ROW_CONV0 + 3, d * SHARD_CONV:(d + 1) * SHARD_CONV], cg)
            grads = (gsum[ROW_NORM_IN:ROW_NORM_IN + 1], gsum[ROW_SINKS:ROW_SINKS + 1, 0:N_Q_HEADS],
                     gsum[ROW_NORM_CONV:ROW_NORM_CONV + 1], gsum[ROW_NORM_ATTN:ROW_NORM_ATTN + 1],
                     gsum[ROW_NORM_FINAL:ROW_NORM_FINAL + 1], cg)
            for s, g in enumerate(grads):
                w_ref, m_ref, v_ref = ins[9 + 3 * s:12 + 3 * s]
                delta, mn, vn = _adamw(w_ref[...], g, m_ref[...], v_ref[...])
                for ref, val in zip(outs[8 + 4 * s:12 + 4 * s], (g, delta, mn, vn)):
                    ref[...] = val
            outs[32][...] = gsum[ROW_SINKS:ROW_SINKS + 1, LOSS_LANE:LOSS_LANE + 1]

    const = lambda i: (0, 0)
    rows = lambda i: (i, 0)
    small_shapes = [a.shape for a in small[::3]]
    in_specs = ([pl.BlockSpec((tr_in, D_MODEL), rows)] * 4 + [pl.BlockSpec((tr_out, D_MODEL), rows)] * 4
                + [pl.BlockSpec((8, D_MODEL), const)] + [pl.BlockSpec(a.shape, const) for a in small])
    out_specs = ([pl.BlockSpec((tr_in, D_MODEL), rows)] * 4 + [pl.BlockSpec((tr_out, D_MODEL), rows)] * 4
                 + [pl.BlockSpec(s, const) for s in small_shapes for _ in range(4)] + [pl.BlockSpec((1, 1), const)])
    out_shape = ([jax.ShapeDtypeStruct((SHARD_IN, D_MODEL), F32)] * 4 + [jax.ShapeDtypeStruct((SHARD_OUT, D_MODEL), F32)] * 4
                 + [jax.ShapeDtypeStruct(s, F32) for s in small_shapes for _ in range(4)]
                 + [jax.ShapeDtypeStruct((1, 1), F32)])
    outs = pl.pallas_call(
        body, name="adam", grid=(steps,), in_specs=in_specs, out_specs=tuple(out_specs), out_shape=tuple(out_shape),
        compiler_params=_params(dimension_semantics=("arbitrary",)),
    )(*big_in, *big_out, gsum, *small)
    return outs[0:4], outs[4:8], [outs[8 + 4 * s:12 + 4 * s] for s in range(6)], outs[32]


def _pad_rows(a, rows=8):
    return jnp.pad(a, ((0, rows - a.shape[0]), (0, 0)))


def kernel(x, norm_in, w_in, conv_w, attn_sinks, norm_conv_out, norm_attn_out, w_out, norm_final, loss_target, m_norm_in, m_w_in, m_conv_w, m_attn_sinks, m_norm_conv_out, m_norm_attn_out, m_w_out, m_norm_final, v_norm_in, v_w_in, v_conv_w, v_attn_sinks, v_norm_conv_out, v_norm_attn_out, v_w_out, v_norm_final):
    x2d = x.reshape(SEQ, D_MODEL)
    target = loss_target.reshape(SEQ, D_MODEL)
    nf = norm_final.reshape(1, D_MODEL)

    w_in_t, m_w_in_t, v_w_in_t = w_in[0].T, m_w_in[0].T, v_w_in[0].T
    tiles = jnp.asarray(TILE_ORDER, jnp.int32)[2 * lax.axis_index("x") + lax.axis_index("y")]
    w_in_full, h, proj, g_out, conv_full = _gather_in_proj(x2d, norm_in, w_in_t, w_out[0], _pad_rows(conv_w[0]), tiles)
    sinks = attn_sinks.reshape(N_Q_HEADS)

    mixed, attn, probs, shares = _mix_fwd(proj, conv_full, sinks, norm_conv_out, norm_attn_out)
    dx2, dx2b, dmixed, gnf, loss_part = _out_proj_loss(mixed, x2d, target, g_out.reshape(D_MIX, D_MODEL), nf)
    dproj, gslab = _mix_bwd(proj, dmixed, attn, probs, shares, conv_full, norm_conv_out, norm_attn_out)
    dw_out = _matmul_tn(mixed, dx2b, 512, "dw_out")
    dw_in_chip, g_w_out = _dw_in_rs(dproj, h, dw_out.reshape(N_DEV, SHARD_OUT, D_MODEL))
    grad_x, g_w_in, gsum = _in_bwd_rs(dproj, w_in_full, x2d, dx2, norm_in, dw_in_chip, gslab, gnf, loss_part)

    small = (norm_in, m_norm_in, v_norm_in, attn_sinks, m_attn_sinks, v_attn_sinks,
             norm_conv_out, m_norm_conv_out, v_norm_conv_out, norm_attn_out, m_norm_attn_out, v_norm_attn_out,
             nf, m_norm_final.reshape(1, D_MODEL), v_norm_final.reshape(1, D_MODEL),
             conv_w[0], m_conv_w[0], v_conv_w[0])
    big_in, big_out, (s_ni, s_sk, s_nc, s_na, s_nf, s_cv), loss = _adam_all(
        (w_in_t, g_w_in, m_w_in_t, v_w_in_t), (w_out[0], g_w_out, m_w_out[0], v_w_out[0]), gsum, small)

    def leaves(k):
        return (s_ni[k], big_in[k].T[None], s_cv[k][None], s_sk[k], s_nc[k], s_na[k], big_out[k][None],
                s_nf[k].reshape(D_MODEL))

    return (loss.reshape(()), grad_x.reshape(1, SEQ, D_MODEL), *leaves(0), *leaves(1), *leaves(2), *leaves(3))
```

```python
import functools
import math

import jax
import jax.numpy as jnp
from jax import lax
from jax.experimental import pallas as pl
from jax.experimental.pallas import tpu as pltpu

F32 = jnp.float32
BF16 = jnp.bfloat16
MESH = pl.DeviceIdType.MESH

N_DEV = 8
SEQ = 2048
D_MODEL = 1024
D_CONV = 1024
D_ATTN = 1024
D_KV = 128
HEAD_DIM = 64
N_Q_HEADS = 16
N_PAIRS = N_Q_HEADS // 2
PAIRS_PER_KV = N_PAIRS // 2
D_MIX = D_CONV + D_ATTN
D_PROJ = 6400
SHARD_IN = D_PROJ // N_DEV
SHARD_OUT = D_MIX // N_DEV
SHARD_CONV = D_CONV // N_DEV
OFF_CB, OFF_CC, OFF_CU, OFF_GC, OFF_Q, OFF_K, OFF_V, OFF_GA = 0, 1024, 2048, 3072, 4096, 5120, 5248, 5376
BLOCK = 128
N_BLOCKS = SEQ // BLOCK
HALO = 8
CHUNK = 16
N_CHUNKS = BLOCK // CHUNK
RMS_EPS = 1e-5
NEG = -1e30
SCALE = HEAD_DIM ** -0.5
SLOPES = tuple(2.0 ** (-8.0 * (h + 1) / N_Q_HEADS) for h in range(N_Q_HEADS))

ADAM_LR = 0.001
ADAM_B1 = 0.9
ADAM_B2 = 0.999
ADAM_EPS = 1e-08
ADAM_WD = 0.01
ADAM_STEP = 10

ROW_NORM_IN, ROW_NORM_CONV, ROW_NORM_ATTN, ROW_NORM_FINAL, ROW_CONV0, ROW_SINKS = 0, 1, 2, 3, 4, 7
LOSS_LANE = N_Q_HEADS
ACC_NORM_CONV, ACC_NORM_ATTN, ACC_CONV0, N_ACC = 0, 1, 2, 5

VMEM_LIMIT = 56 * 1024 * 1024

_NT = (((1,), (1,)), ((), ()))
_TN = (((0,), (0,)), ((), ()))


def _params(**kw):
    return pltpu.CompilerParams(vmem_limit_bytes=VMEM_LIMIT, **kw)


def _adamw(w, g, m, v):
    m = ADAM_B1 * m + (1.0 - ADAM_B1) * g
    v = ADAM_B2 * v + (1.0 - ADAM_B2) * (g * g)
    m_hat = m / (1.0 - ADAM_B1 ** ADAM_STEP)
    v_hat = v / (1.0 - ADAM_B2 ** ADAM_STEP)
    delta = -ADAM_LR * (m_hat / (jnp.sqrt(v_hat) + ADAM_EPS) + ADAM_WD * w)
    return delta, m, v


def _sigmoid(t):
    return 1.0 / (1.0 + jnp.exp(-t))


def _slot(px, py, pc):
    return 4 * px + 2 * py + pc


HALF_IN = SHARD_IN // 2
N_GATHER_KINDS = 13


IN_PROJ_TILE = 640
TILE_ORDER = ((0, 1, 2, 3, 4, 5, 6, 7, 8, 9), (3, 4, 0, 1, 2, 8, 9, 5, 6, 7),
              (5, 6, 0, 1, 7, 8, 9, 2, 3, 4), (8, 9, 3, 4, 5, 6, 7, 0, 1, 2))
TILES_OWN, TILES_NEIGHBOURS = 2, 7


def _gather_in_proj(x, norm_in, w_in_sh, w_out_sh, conv_sh, tiles):
    tn = IN_PROJ_TILE
    steps = D_PROJ // tn
    tm = 256

    def body(tiles_ref, x_ref, g_ref, win_ref, wout_ref, cv_ref, wt_ref, h_ref, proj_ref, gout_ref, conv_ref,
             gin_ref, gcv_ref, wob_ref, send_sems, recv_sems, local_sems):
        p = pl.program_id(0)
        wout_start, wout_forward, wout_finish = _wout_gather(wob_ref, gout_ref, send_sems, recv_sems, local_sems.at[1],
                                                             N_GATHER_KINDS + 7)
        local_sem = local_sems.at[0]
        x, y, c = lax.axis_index("x"), lax.axis_index("y"), lax.axis_index("c")
        me, sibling = (x, y, c), (x, y, 1 - c)
        nx, ny, dg = (1 - x, y, c), (x, 1 - y, c), (1 - x, 1 - y, c)

        def other(dev):
            return (dev[0], dev[1], 1 - dev[2])

        def shard(dev):
            return gin_ref.at[pl.ds(pl.multiple_of(_slot(*dev) * SHARD_IN, 16), SHARD_IN), :]

        def half(dev, h):
            return gin_ref.at[pl.ds(pl.multiple_of(_slot(*dev) * SHARD_IN + h * HALF_IN, 16), HALF_IN), :]

        def rc(ref, k, to):
            return pltpu.make_async_remote_copy(src_ref=ref, dst_ref=ref, send_sem=send_sems.at[k],
                                                recv_sem=recv_sems.at[k], device_id=to, device_id_type=MESH)

        def cv(k, dev, to):
            s = _slot(*dev)
            return pltpu.make_async_remote_copy(src_ref=gcv_ref.at[s], dst_ref=gcv_ref.at[s],
                                                send_sem=send_sems.at[N_GATHER_KINDS + k],
                                                recv_sem=recv_sems.at[N_GATHER_KINDS + k], device_id=to, device_id_type=MESH)

        def own_copies():
            return [rc(shard(me), 0, sibling),
                    rc(half(me, 0), 1, nx), rc(half(me, 1), 2, nx),
                    rc(half(me, 1), 4, ny), rc(half(me, 0), 3, ny),
                    cv(0, me, sibling)] + [cv(1 + j, me, peer) for j, peer in enumerate((nx, ny, dg))]

        def pass_on(dev, h, k_in, k_ici, k_d2d):
            rc(half(dev, h), k_in, me).wait_recv()
            if k_ici is not None:
                rc(half(dev, h), k_ici, ny if dev is nx else nx).start()
            rc(half(dev, h), k_d2d, sibling).start()

        @pl.when(p == 0)
        def _():
            gin_ref[pl.ds(pl.multiple_of(_slot(*me) * SHARD_IN, 16), SHARD_IN), :] = win_ref[...].astype(BF16)
            gcv_ref[_slot(*me)] = cv_ref[...]
            for cp in own_copies():
                cp.start()
            wob_ref[...] = wout_ref[...].astype(BF16)
            for t in range(SEQ // tm):
                xv = x_ref[tm * t:tm * (t + 1), :]
                r = lax.rsqrt(jnp.mean(xv * xv, axis=-1, keepdims=True) + RMS_EPS)
                h_ref[tm * t:tm * (t + 1), :] = (xv * r * g_ref[...]).astype(BF16)
            rc(shard(sibling), 0, me).wait_recv()

        @pl.when(p == TILES_OWN)
        def _():
            for args in ((nx, 0, 1, 5, 7), (ny, 1, 4, 6, 10), (nx, 1, 2, None, 8), (ny, 0, 3, None, 9)):
                pass_on(*args)
            for j, peer in enumerate((nx, ny, dg)):
                cv(1 + j, peer, me).wait_recv()
                cv(4 + j, peer, sibling).start()
            for (dev, h), k in (((nx, 0), 7), ((nx, 1), 8), ((ny, 0), 9), ((ny, 1), 10)):
                rc(half(other(dev), h), k, me).wait_recv()
            wout_start()

        @pl.when(p == TILES_NEIGHBOURS)
        def _():
            pass_on(dg, 0, 5, None, 11)
            pass_on(dg, 1, 6, None, 12)
            for (dev, h), k in (((dg, 0), 11), ((dg, 1), 12)):
                rc(half(other(dev), h), k, me).wait_recv()
            pltpu.make_async_copy(gin_ref, wt_ref, local_sem).start()

        @pl.when(p == steps - 1)
        def _():
            wout_forward()

        w = gin_ref[pl.ds(pl.multiple_of(tiles_ref[p] * tn, tn), tn), :]
        proj_ref[...] = lax.dot_general(h_ref[...], w, _NT, preferred_element_type=F32)

        @pl.when(p == steps - 1)
        def _():
            cv(0, sibling, me).wait_recv()
            for j, peer in enumerate((nx, ny, dg)):
                cv(4 + j, other(peer), me).wait_recv()
            for d in range(N_DEV):
                conv_ref[:, d * SHARD_CONV:(d + 1) * SHARD_CONV] = gcv_ref[d]
            relayed = [rc(half(nx, 0), 5, ny), rc(half(ny, 1), 6, nx)]
            relayed += [rc(half(dev, h), k, sibling) for (dev, h), k in
                        (((nx, 0), 7), ((nx, 1), 8), ((ny, 0), 9), ((ny, 1), 10), ((dg, 0), 11), ((dg, 1), 12))]
            relayed += [cv(4 + j, peer, sibling) for j, peer in enumerate((nx, ny, dg))]
            for cp in own_copies() + relayed:
                cp.wait_send()
            pltpu.make_async_copy(gin_ref, wt_ref, local_sem).wait()
            wout_finish()

    vmem = pl.BlockSpec(memory_space=pltpu.VMEM)
    grid_spec = pltpu.PrefetchScalarGridSpec(
        num_scalar_prefetch=1, grid=(steps,),
        in_specs=[vmem, vmem, vmem, vmem, vmem],
        out_specs=(pl.BlockSpec(memory_space=pl.ANY), vmem,
                   pl.BlockSpec((SEQ, tn), lambda p, tiles_ref: (0, tiles_ref[p])), pl.BlockSpec(memory_space=pl.ANY), vmem),
        scratch_shapes=[pltpu.VMEM((D_PROJ, D_MODEL), BF16), pltpu.VMEM((N_DEV, 8, SHARD_CONV), F32),
                        pltpu.VMEM((SHARD_OUT, D_MODEL), BF16),
                        pltpu.SemaphoreType.DMA((N_GATHER_KINDS + 14,)), pltpu.SemaphoreType.DMA((N_GATHER_KINDS + 14,)),
                        pltpu.SemaphoreType.DMA((2,))])
    return pl.pallas_call(
        body, name="gather_in_proj", grid_spec=grid_spec,
        out_shape=(jax.ShapeDtypeStruct((D_PROJ, D_MODEL), BF16), jax.ShapeDtypeStruct((SEQ, D_MODEL), BF16),
                   jax.ShapeDtypeStruct((SEQ, D_PROJ), F32), jax.ShapeDtypeStruct((N_DEV, SHARD_OUT, D_MODEL), BF16),
                   jax.ShapeDtypeStruct((8, D_CONV), F32)),
        compiler_params=_params(dimension_semantics=("arbitrary",)),
    )(tiles, x, norm_in, w_in_sh, w_out_sh, conv_sh)


def _wout_gather(wo_ref, gout_ref, send_sems, recv_sems, local_sem, base=0):
    x, y, c = lax.axis_index("x"), lax.axis_index("y"), lax.axis_index("c")
    me, sibling = (x, y, c), (x, y, 1 - c)
    chips = [(1 - x, y), (x, 1 - y), (1 - x, 1 - y)]

    def copy(k, block, to, src=None):
        rows = gout_ref.at[_slot(*block)]
        return pltpu.make_async_remote_copy(src_ref=rows if src is None else src, dst_ref=rows,
                                            send_sem=send_sems.at[base + k], recv_sem=recv_sems.at[base + k],
                                            device_id=to, device_id_type=MESH)

    def mine():
        return pltpu.make_async_copy(wo_ref, gout_ref.at[_slot(*me)], local_sem)

    def start():
        mine().start()
        copy(0, me, sibling, src=wo_ref).start()
        for j, chip in enumerate(chips):
            copy(1 + j, me, (*chip, c), src=wo_ref).start()

    def forward():
        for j, chip in enumerate(chips):
            copy(1 + j, (*chip, c), me).wait_recv()
            copy(4 + j, (*chip, c), sibling).start()

    def finish():
        copy(0, sibling, me).wait_recv()
        for j, chip in enumerate(chips):
            copy(4 + j, (*chip, 1 - c), me).wait_recv()
        copy(0, me, sibling, src=wo_ref).wait_send()
        for j, chip in enumerate(chips):
            copy(1 + j, me, (*chip, c), src=wo_ref).wait_send()
            copy(4 + j, (*chip, c), sibling).wait_send()
        mine().wait()

    return start, forward, finish


def _shard_sum(src, own, d2d, ici, send_sems, recv_sems, local_sems, base=0):
    x, y, c = lax.axis_index("x"), lax.axis_index("y"), lax.axis_index("c")
    sibling = (x, y, 1 - c)
    chips = [(x, y), (1 - x, y), (x, 1 - y), (1 - x, 1 - y)]

    def rcopy(s, d, k, to):
        return pltpu.make_async_remote_copy(src_ref=s, dst_ref=d, send_sem=send_sems.at[base + k],
                                            recv_sem=recv_sems.at[base + k], device_id=to, device_id_type=MESH)

    def mine(k):
        return pltpu.make_async_copy(src.at[_slot(*chips[k], c)], own.at[k], local_sems.at[k])

    def to_sibling(k):
        return rcopy(src.at[_slot(*chips[k], 1 - c)], d2d.at[k], k, sibling)

    def to_chip(k):
        return rcopy(own.at[k], ici.at[k - 1], 3 + k, (*chips[k], c))

    def start():
        for k in range(4):
            mine(k).start()
            to_sibling(k).start()

    def forward():
        for k in range(1, 4):
            mine(k).wait()
            to_sibling(k).wait_recv()
            own[k] = (own[k].astype(F32) + d2d[k].astype(F32)).astype(BF16)
            to_chip(k).start()

    def finish():
        mine(0).wait()
        to_sibling(0).wait_recv()
        acc = own[0].astype(F32) + d2d[0].astype(F32)
        for k in range(1, 4):
            to_chip(k).wait_recv()
            acc = acc + ici[k - 1].astype(F32)
        for k in range(4):
            to_sibling(k).wait_send()
        for k in range(1, 4):
            to_chip(k).wait_send()
        return acc

    return start, forward, finish


def _shard_sum_scratch(rows):
    return [pltpu.VMEM((4, rows, D_MODEL), BF16), pltpu.VMEM((4, rows, D_MODEL), BF16),
            pltpu.VMEM((3, rows, D_MODEL), BF16)]


N_SHARD_SUM_SEMS = 7


def _chip_sum(dwt, d2d, out_hbm, send_sems, recv_sems, local_sems, base, local_base):
    x, y, c = lax.axis_index("x"), lax.axis_index("y"), lax.axis_index("c")
    sibling = (x, y, 1 - c)
    chips = [(x, y), (1 - x, y), (x, 1 - y), (1 - x, 1 - y)]

    def shard(s):
        return dwt.at[pl.ds(pl.multiple_of(s * SHARD_IN, 16), SHARD_IN), :]

    def to_sibling(k):
        return pltpu.make_async_remote_copy(src_ref=shard(_slot(*chips[k], 1 - c)), dst_ref=d2d.at[k],
                                            send_sem=send_sems.at[base + k], recv_sem=recv_sems.at[base + k],
                                            device_id=sibling, device_id_type=MESH)

    def save(k):
        return pltpu.make_async_copy(d2d.at[k], out_hbm.at[k], local_sems.at[local_base + k])

    def send(first):
        for k in range(4):
            in_first = _slot(*chips[k], 1 - c) < N_DEV // 2

            @pl.when(in_first if first else jnp.logical_not(in_first))
            def _():
                to_sibling(k).start()

    def finish():
        for k in range(4):
            to_sibling(k).wait_recv()
            d2d[k] = (shard(_slot(*chips[k], c))[...].astype(F32) + d2d[k].astype(F32)).astype(BF16)
            save(k).start()
        for k in range(4):
            save(k).wait()
            to_sibling(k).wait_send()

    return send, finish


N_ICI_SUM_SEMS = 6


def _ici_sum(src, own, ici, via, stage, send_sems, recv_sems, local_sems, base=0):
    x, y, c = lax.axis_index("x"), lax.axis_index("y"), lax.axis_index("c")
    nx, ny = (1 - x, y, c), (x, 1 - y, c)
    OWN, NX, NY, DG = range(4)

    def half(ref, h):
        return ref.at[pl.ds(h * HALF_IN, HALF_IN), :]

    def rc(s, d, k, to):
        return pltpu.make_async_remote_copy(src_ref=s, dst_ref=d, send_sem=send_sems.at[base + k],
                                            recv_sem=recv_sems.at[base + k], device_id=to, device_id_type=MESH)

    for_dg_0 = lambda: rc(half(src.at[DG], 0), via.at[0], 0, nx)
    for_dg_1 = lambda: rc(half(src.at[DG], 1), via.at[1], 1, ny)
    for_nx_0 = lambda: rc(half(src.at[NX], 0), half(ici.at[0], 0), 2, nx)
    for_ny_1 = lambda: rc(half(src.at[NY], 1), half(ici.at[1], 1), 3, ny)
    for_ny_0 = lambda: rc(stage.at[0], half(ici.at[1], 0), 4, ny)
    for_nx_1 = lambda: rc(stage.at[1], half(ici.at[0], 1), 5, nx)
    mine = lambda: pltpu.make_async_copy(src.at[OWN], own, local_sems.at[0])
    stage_0 = lambda: pltpu.make_async_copy(half(src.at[NY], 0), stage.at[0], local_sems.at[1])
    stage_1 = lambda: pltpu.make_async_copy(half(src.at[NX], 1), stage.at[1], local_sems.at[2])

    def start():
        for cp in (for_dg_0, for_dg_1, for_nx_0, for_ny_1, stage_0, stage_1, mine):
            cp().start()

    def relay():
        for h, staged, landed, out in ((0, stage_0, for_dg_0, for_ny_0), (1, stage_1, for_dg_1, for_nx_1)):
            staged().wait()
            landed().wait_recv()
            stage[h] = (stage[h].astype(F32) + via[h].astype(F32)).astype(BF16)
            out().start()

    def finish():
        mine().wait()
        for cp in (for_nx_0, for_nx_1, for_ny_1, for_ny_0):
            cp().wait_recv()
        acc = own[...].astype(F32) + ici[0].astype(F32) + ici[1].astype(F32)
        for cp in (for_dg_0, for_dg_1, for_nx_0, for_ny_1, for_ny_0, for_nx_1):
            cp().wait_send()
        return acc

    return start, relay, finish


def _slab_sum(myslab, slabs, send_sems, recv_sems, base):
    x, y, c = lax.axis_index("x"), lax.axis_index("y"), lax.axis_index("c")
    me = _slot(x, y, c)
    peers = [(x, y, 1 - c), (1 - x, y, c), (x, 1 - y, c), (1 - x, 1 - y, c),
             (1 - x, y, 1 - c), (x, 1 - y, 1 - c), (1 - x, 1 - y, 1 - c)]

    def cp(k):
        return pltpu.make_async_remote_copy(src_ref=myslab, dst_ref=slabs.at[me], send_sem=send_sems.at[base + k],
                                            recv_sem=recv_sems.at[base + k], device_id=peers[k], device_id_type=MESH)

    def start():
        slabs[me] = myslab[...]
        for k in range(7):
            cp(k).start()

    def finish():
        for k in range(7):
            cp(k).wait_recv()
        total = slabs[0]
        for d in range(1, N_DEV):
            total = total + slabs[d]
        for k in range(7):
            cp(k).wait_send()
        return total

    return start, finish


def _chunk_rows(r):
    return pl.ds(pl.multiple_of(r * CHUNK, CHUNK), CHUNK)


def _conv_halo(cch_ref, cuh_ref, n):
    zh = jnp.where(n > 0, cch_ref[...] * cuh_ref[...], 0.0)
    return jnp.concatenate([zh] * (CHUNK // HALO), axis=0)


def _conv_chunk(pj_ref, zhalo, cw, r):
    rows = _chunk_rows(r)
    cc = pj_ref[rows, OFF_CC:OFF_CC + D_CONV]
    cu = pj_ref[rows, OFF_CU:OFF_CU + D_CONV]
    z = cc * cu
    before = _chunk_rows(jnp.maximum(r - 1, 0))
    zprev = jnp.where(r > 0, pj_ref[before, OFF_CC:OFF_CC + D_CONV] * pj_ref[before, OFF_CU:OFF_CU + D_CONV], zhalo)
    row = lax.broadcasted_iota(jnp.int32, (CHUNK, D_CONV), 0)
    z1 = jnp.where(row < 1, pltpu.roll(zprev, 1, 0), pltpu.roll(z, 1, 0))
    z2 = jnp.where(row < 2, pltpu.roll(zprev, 2, 0), pltpu.roll(z, 2, 0))
    co = cw[0] * z2 + cw[1] * z1 + cw[2] * z
    return cc, cu, z, z1, z2, co


def _gated_norm(a, gain, t):
    r = lax.rsqrt(jnp.mean(a * a, axis=-1, keepdims=True) + RMS_EPS)
    return a * r * gain * (t * _sigmoid(t))


def _kv_bands(pj, kvp_ref):
    lane = lax.broadcasted_iota(jnp.int32, (2 * BLOCK, D_KV), 1)
    lo = lane < HEAD_DIM

    def bands(prev, cur):
        b = jnp.concatenate([prev, cur], axis=0)
        br = pltpu.roll(b, HEAD_DIM, 1)
        zero = jnp.zeros_like(b)
        return ((jnp.where(lo, b, zero).astype(BF16), jnp.where(lo, zero, br).astype(BF16)),
                (jnp.where(lo, br, zero).astype(BF16), jnp.where(lo, zero, b).astype(BF16)))

    ks = bands(kvp_ref[:, 0:D_KV], pj[:, OFF_K:OFF_K + D_KV])
    vs = bands(kvp_ref[:, D_KV:2 * D_KV], pj[:, OFF_V:OFF_V + D_KV])
    return ks, vs


STACK = PAIRS_PER_KV * BLOCK


def _head(j, i, e):
    return 2 * (PAIRS_PER_KV * j + i) + e


def _pair_cols(j, i, off):
    p = PAIRS_PER_KV * j + i
    return slice(off + 128 * p, off + 128 * (p + 1))


def _fill_attn_bias(bias_scr, first_block):
    qi = lax.broadcasted_iota(jnp.int32, (BLOCK, 2 * BLOCK), 0)
    kj = lax.broadcasted_iota(jnp.int32, (BLOCK, 2 * BLOCK), 1)
    dist = BLOCK + qi - kj
    valid = (dist >= 0) & (dist < BLOCK)
    if first_block:
        valid = valid & (kj >= BLOCK)
    distf = dist.astype(F32)
    for j in range(2):
        for e in range(2):
            for i in range(PAIRS_PER_KV):
                bias_scr[2 * j + e, BLOCK * i:BLOCK * (i + 1), :] = jnp.where(valid, -SLOPES[_head(j, i, e)] * distf, NEG)


def _q_stack(pj, j):
    return jnp.concatenate([(pj[:, _pair_cols(j, i, OFF_Q)] * SCALE).astype(BF16) for i in range(PAIRS_PER_KV)], axis=0)


def _sink_rows(sink_ref, j, e):
    return jnp.concatenate([jnp.full((BLOCK, 128), sink_ref[_head(j, i, e)], F32) for i in range(PAIRS_PER_KV)], axis=0)


def _attn_probs(q_stack, kband, bias, sink):
    s = lax.dot_general(q_stack, kband, _NT, preferred_element_type=F32) + bias
    m = jnp.broadcast_to(jnp.max(s, axis=-1, keepdims=True), (STACK, 128))
    m = jnp.maximum(m, sink)
    p = [jnp.exp(t - m) for t in (s[:, :128], s[:, 128:])]
    es = jnp.exp(sink - m)
    ones = jnp.ones((128, 128), BF16)
    total = (jnp.dot(p[0].astype(BF16), ones, preferred_element_type=F32)
             + jnp.dot(p[1].astype(BF16), ones, preferred_element_type=F32))
    inv = 1.0 / (total + es)
    return jnp.concatenate([p[0] * inv, p[1] * inv], axis=1), es * inv


def _attn_group(pj, ks, vs, bias_scr, sink_ref, j):
    q_stack = _q_stack(pj, j)
    out, probs, shares = None, [], []
    for e in range(2):
        p, ps = _attn_probs(q_stack, ks[j][e], bias_scr[2 * j + e], _sink_rows(sink_ref, j, e))
        p = p.astype(BF16)
        o = jnp.dot(p, vs[j][e], preferred_element_type=F32)
        out = o if out is None else out + o
        probs.append(p)
        shares.append(ps)
    return out, probs, shares


def _mix_fwd(proj, conv_full, sinks, norm_conv, norm_attn):
    def body(pj_ref, kvp_ref, cch_ref, cuh_ref, cw_ref, sink_ref, gc_ref, ga_ref,
             mixed_ref, attn_scr, p_ref, ps_ref, bias_scr):
        n = pl.program_id(0)
        pj = pj_ref

        @pl.when(n == 0)
        def _():
            _fill_attn_bias(bias_scr, first_block=True)

        @pl.when(n == 1)
        def _():
            _fill_attn_bias(bias_scr, first_block=False)

        zhalo = _conv_halo(cch_ref, cuh_ref, n)
        cw = (cw_ref[0:1, :], cw_ref[1:2, :], cw_ref[2:3, :])
        gain_c = gc_ref[...]

        def conv_chunk(r, carry):
            rows = _chunk_rows(r)
            co = _conv_chunk(pj_ref, zhalo, cw, r)[-1]
            y = _gated_norm(pj_ref[rows, OFF_CB:OFF_CB + D_CONV] * co, gain_c, pj_ref[rows, OFF_GC:OFF_GC + D_CONV])
            mixed_ref[rows, 0:D_CONV] = y.astype(BF16)
            return carry

        lax.fori_loop(0, N_CHUNKS, conv_chunk, 0, unroll=True)

        ks, vs = _kv_bands(pj, kvp_ref)
        for j in range(2):
            out, probs, shares = _attn_group(pj, ks, vs, bias_scr, sink_ref, j)
            for e in range(2):
                p_ref[0, 2 * j + e] = probs[e]
                ps_ref[0, 2 * j + e] = shares[e]
            for i in range(PAIRS_PER_KV):
                attn_scr[:, _pair_cols(j, i, 0)] = out[BLOCK * i:BLOCK * (i + 1), :]
        gain_a = ga_ref[...]

        def norm_chunk(r, carry):
            rows = _chunk_rows(r)
            y = _gated_norm(attn_scr[rows, :], gain_a, pj_ref[rows, OFF_GA:OFF_GA + D_ATTN])
            mixed_ref[rows, D_CONV:D_MIX] = y.astype(BF16)
            return carry

        lax.fori_loop(0, N_CHUNKS, norm_chunk, 0, unroll=True)

    per_block = BLOCK // HALO
    return pl.pallas_call(
        body, name="mix_fwd", grid=(N_BLOCKS,),
        in_specs=[
            pl.BlockSpec((BLOCK, D_PROJ), lambda n: (n, 0)),
            pl.BlockSpec((BLOCK, 2 * D_KV), lambda n: (jnp.maximum(n - 1, 0), OFF_K // (2 * D_KV))),
            pl.BlockSpec((HALO, D_CONV), lambda n: (jnp.maximum(n * per_block - 1, 0), OFF_CC // D_CONV)),
            pl.BlockSpec((HALO, D_CONV), lambda n: (jnp.maximum(n * per_block - 1, 0), OFF_CU // D_CONV)),
            pl.BlockSpec((8, D_CONV), lambda n: (0, 0)),
            pl.BlockSpec(memory_space=pltpu.SMEM),
            pl.BlockSpec((1, D_CONV), lambda n: (0, 0)),
            pl.BlockSpec((1, D_ATTN), lambda n: (0, 0)),
        ],
        out_specs=(pl.BlockSpec((BLOCK, D_MIX), lambda n: (n, 0)), pl.BlockSpec((BLOCK, D_ATTN), lambda n: (n, 0)),
                   pl.BlockSpec((1, 4, STACK, 2 * BLOCK), lambda n: (n, 0, 0, 0)),
                   pl.BlockSpec((1, 4, STACK, 128), lambda n: (n, 0, 0, 0))),
        out_shape=(jax.ShapeDtypeStruct((SEQ, D_MIX), BF16), jax.ShapeDtypeStruct((SEQ, D_ATTN), F32),
                   jax.ShapeDtypeStruct((N_BLOCKS, 4, STACK, 2 * BLOCK), BF16),
                   jax.ShapeDtypeStruct((N_BLOCKS, 4, STACK, 128), F32)),
        scratch_shapes=[pltpu.VMEM((4, STACK, 2 * BLOCK), F32)],
        compiler_params=_params(dimension_semantics=("arbitrary",)),
    )(proj, proj, proj, proj, conv_full, sinks, norm_conv, norm_attn)


def _out_proj_loss(mixed, x, target, w_out_full, norm_final):
    tm = 256

    def body(mx_ref, x_ref, t_ref, w_ref, g_ref, dx2_ref, dx2b_ref, dmix_ref, gnf_ref, loss_ref):
        i = pl.program_id(0)
        w = w_ref[...]
        x2 = x_ref[...] + jnp.dot(mx_ref[...], w, preferred_element_type=F32)
        r = lax.rsqrt(jnp.mean(x2 * x2, axis=-1, keepdims=True) + RMS_EPS)
        xn = x2 * r
        g = g_ref[...]
        err = xn * g - t_ref[...]
        part = 0.5 * jnp.sum(jnp.mean(err * err, axis=-1, keepdims=True), axis=0, keepdims=True)
        dy = err * (1.0 / D_MODEL)
        gnf = jnp.sum(dy * xn, axis=0, keepdims=True)
        u = dy * g
        dx2 = r * (u - xn * jnp.mean(u * xn, axis=-1, keepdims=True))
        dx2_ref[...] = dx2
        dx2b = dx2.astype(BF16)
        dx2b_ref[...] = dx2b
        dmix_ref[...] = lax.dot_general(dx2b, w, _NT, preferred_element_type=F32)

        @pl.when(i == 0)
        def _():
            gnf_ref[...] = jnp.zeros_like(gnf_ref)
            loss_ref[...] = jnp.zeros_like(loss_ref)

        gnf_ref[...] += gnf
        loss_ref[...] += jnp.broadcast_to(part, loss_ref.shape)

    return pl.pallas_call(
        body, name="out_proj_loss", grid=(SEQ // tm,),
        in_specs=[pl.BlockSpec((tm, D_MIX), lambda i: (i, 0)), pl.BlockSpec((tm, D_MODEL), lambda i: (i, 0)),
                  pl.BlockSpec((tm, D_MODEL), lambda i: (i, 0)), pl.BlockSpec(memory_space=pltpu.VMEM),
                  pl.BlockSpec((1, D_MODEL), lambda i: (0, 0))],
        out_specs=(pl.BlockSpec((tm, D_MODEL), lambda i: (i, 0)), pl.BlockSpec((tm, D_MODEL), lambda i: (i, 0)),
                   pl.BlockSpec((tm, D_MIX), lambda i: (i, 0)),
                   pl.BlockSpec((1, D_MODEL), lambda i: (0, 0)), pl.BlockSpec((8, 128), lambda i: (0, 0))),
        out_shape=(jax.ShapeDtypeStruct((SEQ, D_MODEL), F32), jax.ShapeDtypeStruct((SEQ, D_MODEL), BF16),
                   jax.ShapeDtypeStruct((SEQ, D_MIX), F32),
                   jax.ShapeDtypeStruct((1, D_MODEL), F32), jax.ShapeDtypeStruct((8, 128), F32)),
        compiler_params=_params(dimension_semantics=("arbitrary",)),
    )(mixed, x, target, w_out_full, norm_final)


def _gated_norm_bwd(a, gain, t, dy):
    r = lax.rsqrt(jnp.mean(a * a, axis=-1, keepdims=True) + RMS_EPS)
    an = a * r
    sg = _sigmoid(t)
    dn = dy * (t * sg)
    dt = dy * (an * gain) * (sg * (1.0 + t * (1.0 - sg)))
    u = dn * gain
    da = r * (u - an * jnp.mean(u * an, axis=-1, keepdims=True))
    return da, dt, dn * an


def _mix_bwd(proj, dmixed, attn, probs, shares, conv_full, norm_conv, norm_attn):
    def body(pj_ref, kvp_ref, cch_ref, cuh_ref, dmx_ref, attn_ref, p_ref, ps_ref, cw_ref, gc_ref, ga_ref,
             dpj_ref, gslab_ref, dattn_scr, nxt_scr, dkv_scr, acc_scr):
        step = pl.program_id(0)
        n = N_BLOCKS - 1 - step
        pj = pj_ref

        @pl.when(step == 0)
        def _():
            gslab_ref[...] = jnp.zeros_like(gslab_ref)
            nxt_scr[...] = jnp.zeros_like(nxt_scr)
            dkv_scr[...] = jnp.zeros_like(dkv_scr)
            acc_scr[...] = jnp.zeros_like(acc_scr)

        zhalo = _conv_halo(cch_ref, cuh_ref, n)
        cw = (cw_ref[0:1, :], cw_ref[1:2, :], cw_ref[2:3, :])
        gain_c = gc_ref[...]
        row = lax.broadcasted_iota(jnp.int32, (CHUNK, D_CONV), 0)

        def conv_chunk(t, dco_after):
            r = N_CHUNKS - 1 - t
            rows = _chunk_rows(r)
            cc, cu, z, z1, z2, co = _conv_chunk(pj_ref, zhalo, cw, r)
            cb = pj_ref[rows, OFF_CB:OFF_CB + D_CONV]
            da, dgate, gterm = _gated_norm_bwd(cb * co, gain_c, pj_ref[rows, OFF_GC:OFF_GC + D_CONV],
                                               dmx_ref[rows, 0:D_CONV])
            dpj_ref[rows, OFF_GC:OFF_GC + D_CONV] = dgate.astype(BF16)
            dpj_ref[rows, OFF_CB:OFF_CB + D_CONV] = (da * co).astype(BF16)
            dco = da * cb
            dco1 = jnp.where(row >= CHUNK - 1, pltpu.roll(dco_after, CHUNK - 1, 0), pltpu.roll(dco, CHUNK - 1, 0))
            dco2 = jnp.where(row >= CHUNK - 2, pltpu.roll(dco_after, CHUNK - 2, 0), pltpu.roll(dco, CHUNK - 2, 0))
            dz = cw[2] * dco + cw[1] * dco1 + cw[0] * dco2
            dpj_ref[rows, OFF_CC:OFF_CC + D_CONV] = (dz * cu).astype(BF16)
            dpj_ref[rows, OFF_CU:OFF_CU + D_CONV] = (dz * cc).astype(BF16)
            acc_scr[ACC_NORM_CONV] += gterm
            acc_scr[ACC_CONV0] += dco * z2
            acc_scr[ACC_CONV0 + 1] += dco * z1
            acc_scr[ACC_CONV0 + 2] += dco * z
            return dco

        nxt_scr[...] = lax.fori_loop(0, N_CHUNKS, conv_chunk, nxt_scr[...], unroll=True)

        ks, vs = _kv_bands(pj, kvp_ref)
        gain_a = ga_ref[...]

        def norm_chunk(r, carry):
            rows = _chunk_rows(r)
            da, dgate, gterm = _gated_norm_bwd(attn_ref[rows, :], gain_a, pj_ref[rows, OFF_GA:OFF_GA + D_ATTN],
                                               dmx_ref[rows, D_CONV:D_MIX])
            dpj_ref[rows, OFF_GA:OFF_GA + D_ATTN] = dgate.astype(BF16)
            dattn_scr[rows, :] = da
            acc_scr[ACC_NORM_ATTN] += gterm
            return carry

        lax.fori_loop(0, N_CHUNKS, norm_chunk, 0, unroll=True)

        in_lo = lax.broadcasted_iota(jnp.int32, (128, 128), 0) < HEAD_DIM
        half_ones = (jnp.where(in_lo, 1.0, 0.0).astype(BF16), jnp.where(in_lo, 0.0, 1.0).astype(BF16))
        lane_s = lax.broadcasted_iota(jnp.int32, (1, D_MODEL), 1)
        gsink = jnp.zeros((1, D_MODEL), F32)
        dk_t, dv_t = [], []
        for j in range(2):
            q_stack = _q_stack(pj, j)
            do_f = jnp.concatenate([dattn_scr[:, _pair_cols(j, i, 0)] for i in range(PAIRS_PER_KV)], axis=0)
            o_f = jnp.concatenate([attn_ref[:, _pair_cols(j, i, 0)] for i in range(PAIRS_PER_KV)], axis=0)
            prod = (do_f * o_f).astype(BF16)
            deltas = [jnp.dot(prod, half_ones[e], preferred_element_type=F32) for e in range(2)]
            do_b = do_f.astype(BF16)
            q_t, do_t = q_stack.T, do_b.T
            dq, dk_j, dv_j = None, None, None
            for e in range(2):
                p = p_ref[0, 2 * j + e]
                dp = lax.dot_general(do_b, vs[j][e], _NT, preferred_element_type=F32)
                delta = jnp.concatenate([deltas[e], deltas[e]], axis=1)
                ds = (p.astype(F32) * (dp - delta)).astype(BF16)
                gs = ps_ref[0, 2 * j + e] * deltas[e]
                for i in range(PAIRS_PER_KV):
                    gs_h = -jnp.sum(gs[BLOCK * i:BLOCK * (i + 1), 0:1], axis=0, keepdims=True)
                    gsink = gsink + jnp.where(lane_s == _head(j, i, e), gs_h, 0.0)
                t = jnp.dot(ds, ks[j][e], preferred_element_type=F32)
                dq = t if dq is None else dq + t
                half = slice(HEAD_DIM * e, HEAD_DIM * (e + 1))
                a = jnp.dot(q_t[half, :], ds, preferred_element_type=F32)
                b = jnp.dot(do_t[half, :], p, preferred_element_type=F32)
                dk_j = a if dk_j is None else dk_j + a
                dv_j = b if dv_j is None else dv_j + b
            for i in range(PAIRS_PER_KV):
                dpj_ref[:, _pair_cols(j, i, OFF_Q)] = (dq[BLOCK * i:BLOCK * (i + 1), :] * SCALE).astype(BF16)
            dk_t.append(dk_j)
            dv_t.append(dv_j)
        dk = jnp.concatenate(dk_t, axis=0).T
        dv = jnp.concatenate(dv_t, axis=0).T
        dpj_ref[:, OFF_K:OFF_K + D_KV] = (dk[BLOCK:, :] + dkv_scr[:, 0:D_KV]).astype(BF16)
        dpj_ref[:, OFF_V:OFF_V + D_KV] = (dv[BLOCK:, :] + dkv_scr[:, D_KV:2 * D_KV]).astype(BF16)
        dkv_scr[:, 0:D_KV] = dk[:BLOCK, :]
        dkv_scr[:, D_KV:2 * D_KV] = dv[:BLOCK, :]
        gslab_ref[ROW_SINKS:ROW_SINKS + 1, :] += gsink

        @pl.when(step == N_BLOCKS - 1)
        def _():
            for k, slab_row in ((ACC_NORM_CONV, ROW_NORM_CONV), (ACC_NORM_ATTN, ROW_NORM_ATTN), (ACC_CONV0, ROW_CONV0),
                                (ACC_CONV0 + 1, ROW_CONV0 + 1), (ACC_CONV0 + 2, ROW_CONV0 + 2)):
                gslab_ref[slab_row:slab_row + 1, :] = jnp.sum(acc_scr[k], axis=0, keepdims=True)

    per_block = BLOCK // HALO
    last = N_BLOCKS - 1
    return pl.pallas_call(
        body, name="mix_bwd", grid=(N_BLOCKS,),
        in_specs=[
            pl.BlockSpec((BLOCK, D_PROJ), lambda s: (last - s, 0)),
            pl.BlockSpec((BLOCK, 2 * D_KV), lambda s: (jnp.maximum(last - s - 1, 0), OFF_K // (2 * D_KV))),
            pl.BlockSpec((HALO, D_CONV), lambda s: (jnp.maximum((last - s) * per_block - 1, 0), OFF_CC // D_CONV)),
            pl.BlockSpec((HALO, D_CONV), lambda s: (jnp.maximum((last - s) * per_block - 1, 0), OFF_CU // D_CONV)),
            pl.BlockSpec((BLOCK, D_MIX), lambda s: (last - s, 0)),
            pl.BlockSpec((BLOCK, D_ATTN), lambda s: (last - s, 0)),
            pl.BlockSpec((1, 4, STACK, 2 * BLOCK), lambda s: (last - s, 0, 0, 0)),
            pl.BlockSpec((1, 4, STACK, 128), lambda s: (last - s, 0, 0, 0)),
            pl.BlockSpec((8, D_CONV), lambda s: (0, 0)),
            pl.BlockSpec((1, D_CONV), lambda s: (0, 0)),
            pl.BlockSpec((1, D_ATTN), lambda s: (0, 0)),
        ],
        out_specs=(pl.BlockSpec((BLOCK, D_PROJ), lambda s: (last - s, 0)),
                   pl.BlockSpec((8, D_MODEL), lambda s: (0, 0))),
        out_shape=(jax.ShapeDtypeStruct((SEQ, D_PROJ), BF16), jax.ShapeDtypeStruct((8, D_MODEL), F32)),
        scratch_shapes=[pltpu.VMEM((BLOCK, D_ATTN), F32), pltpu.VMEM((CHUNK, D_CONV), F32),
                        pltpu.VMEM((BLOCK, 2 * D_KV), F32), pltpu.VMEM((N_ACC, CHUNK, D_MODEL), F32)],
        compiler_params=_params(dimension_semantics=("arbitrary",)),
    )(proj, proj, proj, proj, dmixed, attn, probs, shares, conv_full, norm_conv, norm_attn)


def _in_bwd_rs(dproj, w_full, x, dx2, norm_in, dw_in_chip, gslab, gnf, loss_part):
    tm = 256
    steps = SEQ // tm
    relay_step = 4

    def body(dp_ref, w_ref, x_ref, dx2_ref, g_ref, dwi_ref, gs_ref, gnf_ref, lp_ref, gx_ref, gwin_ref, gsum_ref,
             gni_scr, own, ici, via, stage, myslab, slabs, send_sems, recv_sems, local_sems):
        i = pl.program_id(0)
        rs_start, rs_relay, rs_finish = _ici_sum(dwi_ref, own, ici, via, stage, send_sems, recv_sems, local_sems)
        slab_start, slab_finish = _slab_sum(myslab, slabs, send_sems, recv_sems, N_ICI_SUM_SEMS)

        @pl.when(i == 0)
        def _():
            gni_scr[...] = jnp.zeros_like(gni_scr)
            rs_start()

        dh = jnp.dot(dp_ref[...], w_ref[...], preferred_element_type=F32)
        xv = x_ref[...]
        r = lax.rsqrt(jnp.mean(xv * xv, axis=-1, keepdims=True) + RMS_EPS)
        xn = xv * r
        u = dh * g_ref[...]
        gx_ref[...] = dx2_ref[...] + r * (u - xn * jnp.mean(u * xn, axis=-1, keepdims=True))
        gni_scr[...] += jnp.sum(dh * xn, axis=0, keepdims=True)

        @pl.when(i == relay_step)
        def _():
            rs_relay()

        @pl.when(i == steps - 1)
        def _():
            row = lax.broadcasted_iota(jnp.int32, (8, D_MODEL), 0)
            lane = lax.broadcasted_iota(jnp.int32, (8, D_MODEL), 1)
            slab = jnp.where(row == ROW_NORM_IN, gni_scr[...], jnp.where(row == ROW_NORM_FINAL, gnf_ref[...], gs_ref[...]))
            myslab[...] = jnp.where((row == ROW_SINKS) & (lane == LOSS_LANE), lp_ref[0:1, 0:1], slab)
            slab_start()
            gwin_ref[...] = rs_finish()
            gsum_ref[...] = slab_finish()

    const = lambda i: (0, 0)
    return pl.pallas_call(
        body, name="in_bwd", grid=(steps,),
        in_specs=[pl.BlockSpec((tm, D_PROJ), lambda i: (i, 0)), pl.BlockSpec(memory_space=pltpu.VMEM),
                  pl.BlockSpec((tm, D_MODEL), lambda i: (i, 0)), pl.BlockSpec((tm, D_MODEL), lambda i: (i, 0)),
                  pl.BlockSpec((1, D_MODEL), const), pl.BlockSpec(memory_space=pl.ANY),
                  pl.BlockSpec((8, D_MODEL), const), pl.BlockSpec((1, D_MODEL), const), pl.BlockSpec((8, 128), const)],
        out_specs=(pl.BlockSpec((tm, D_MODEL), lambda i: (i, 0)), pl.BlockSpec((SHARD_IN, D_MODEL), const),
                   pl.BlockSpec((8, D_MODEL), const)),
        out_shape=(jax.ShapeDtypeStruct((SEQ, D_MODEL), F32), jax.ShapeDtypeStruct((SHARD_IN, D_MODEL), F32),
                   jax.ShapeDtypeStruct((8, D_MODEL), F32)),
        scratch_shapes=[pltpu.VMEM((1, D_MODEL), F32), pltpu.VMEM((SHARD_IN, D_MODEL), BF16),
                        pltpu.VMEM((2, SHARD_IN, D_MODEL), BF16), pltpu.VMEM((2, HALF_IN, D_MODEL), BF16),
                        pltpu.VMEM((2, HALF_IN, D_MODEL), BF16),
                        pltpu.VMEM((8, D_MODEL), F32), pltpu.VMEM((N_DEV, 8, D_MODEL), F32),
                        pltpu.SemaphoreType.DMA((N_ICI_SUM_SEMS + 7,)), pltpu.SemaphoreType.DMA((N_ICI_SUM_SEMS + 7,)),
                        pltpu.SemaphoreType.DMA((3,))],
        compiler_params=_params(dimension_semantics=("arbitrary",)),
    )(dproj, w_full, x, dx2, norm_in, dw_in_chip, gslab, gnf, loss_part)


def _dw_in_rs(dproj, h, dw_out_sh):
    tn = 640
    steps = D_PROJ // tn
    forward_step = 2
    half_step = (D_PROJ // 2) // tn

    def body(a_ref, b_ref, dwo_ref, chip_ref, gwo_ref, dwt, d2d_in, own, d2d, ici, send_sems, recv_sems, local_sems):
        i = pl.program_id(0)
        rs_start, rs_forward, rs_finish = _shard_sum(dwo_ref, own, d2d, ici, send_sems, recv_sems, local_sems)
        pair_send, pair_finish = _chip_sum(dwt, d2d_in, chip_ref, send_sems, recv_sems, local_sems,
                                           N_SHARD_SUM_SEMS, 4)

        @pl.when(i == 0)
        def _():
            rs_start()

        @pl.when(i == half_step)
        def _():
            pair_send(first=True)

        tile = lax.dot_general(a_ref[...], b_ref[...], _TN, preferred_element_type=F32).astype(BF16)
        dwt[pl.ds(pl.multiple_of(i * tn, tn), tn), :] = tile

        @pl.when(i == forward_step)
        def _():
            rs_forward()

        @pl.when(i == steps - 1)
        def _():
            pair_send(first=False)
            gwo_ref[...] = rs_finish()
            pair_finish()

    return pl.pallas_call(
        body, name="dw_in", grid=(steps,),
        in_specs=[pl.BlockSpec((SEQ, tn), lambda i: (0, i)), pl.BlockSpec(memory_space=pltpu.VMEM),
                  pl.BlockSpec(memory_space=pl.ANY)],
        out_specs=(pl.BlockSpec(memory_space=pl.ANY), pl.BlockSpec((SHARD_OUT, D_MODEL), lambda i: (0, 0))),
        out_shape=(jax.ShapeDtypeStruct((4, SHARD_IN, D_MODEL), BF16), jax.ShapeDtypeStruct((SHARD_OUT, D_MODEL), F32)),
        scratch_shapes=[pltpu.VMEM((D_PROJ, D_MODEL), BF16), pltpu.VMEM((4, SHARD_IN, D_MODEL), BF16),
                        *_shard_sum_scratch(SHARD_OUT),
                        pltpu.SemaphoreType.DMA((N_SHARD_SUM_SEMS + 4,)), pltpu.SemaphoreType.DMA((N_SHARD_SUM_SEMS + 4,)),
                        pltpu.SemaphoreType.DMA((8,))],
        compiler_params=_params(dimension_semantics=("arbitrary",)),
    )(dproj, h, dw_out_sh)


def _matmul_tn(a, b, tn, name):
    k, n = a.shape
    _, m = b.shape

    def body(a_ref, b_ref, o_ref):
        o_ref[...] = lax.dot_general(a_ref[...], b_ref[...], _TN, preferred_element_type=F32).astype(BF16)

    return pl.pallas_call(
        body, name=name, grid=(n // tn,),
        in_specs=[pl.BlockSpec((k, tn), lambda i: (0, i)), pl.BlockSpec(memory_space=pltpu.VMEM)],
        out_specs=pl.BlockSpec((tn, m), lambda i: (i, 0)),
        out_shape=jax.ShapeDtypeStruct((n, m), BF16),
        compiler_params=_params(dimension_semantics=("arbitrary",)),
    )(a, b)


def _adam_all(big_in, big_out, gsum, small, grad_x):
    steps = 4
    tr_in, tr_out = SHARD_IN // steps, SHARD_OUT // steps

    def body(*refs):
        ins, outs = refs[:8 + 1 + 18 + 1], refs[8 + 1 + 18 + 1:]
        i = pl.program_id(0)
        outs[33][...] = ins[27][...]
        for b in range(2):
            w_ref, g_ref, m_ref, v_ref = ins[4 * b:4 * b + 4]
            g = g_ref[...]
            delta, mn, vn = _adamw(w_ref[...], g, m_ref[...], v_ref[...])
            for ref, val in zip(outs[4 * b:4 * b + 4], (g, delta, mn, vn)):
                ref[...] = val

        @pl.when(i == 0)
        def _():
            gsum = ins[8][...]
            idx = _slot(lax.axis_index("x"), lax.axis_index("y"), lax.axis_index("c"))
            cg = jnp.zeros((3, SHARD_CONV), F32)
            for d in range(N_DEV):
                cg = jnp.where(idx == d, gsum[ROW_CONV0:ROW_CONV0 + 3, d * SHARD_CONV:(d + 1) * SHARD_CONV], cg)
            grads = (gsum[ROW_NORM_IN:ROW_NORM_IN + 1], gsum[ROW_SINKS:ROW_SINKS + 1, 0:N_Q_HEADS],
                     gsum[ROW_NORM_CONV:ROW_NORM_CONV + 1], gsum[ROW_NORM_ATTN:ROW_NORM_ATTN + 1],
                     gsum[ROW_NORM_FINAL:ROW_NORM_FINAL + 1], cg)
            for s, g in enumerate(grads):
                w_ref, m_ref, v_ref = ins[9 + 3 * s:12 + 3 * s]
                delta, mn, vn = _adamw(w_ref[...], g, m_ref[...], v_ref[...])
                for ref, val in zip(outs[8 + 4 * s:12 + 4 * s], (g, delta, mn, vn)):
                    ref[...] = val
            outs[32][...] = gsum[ROW_SINKS:ROW_SINKS + 1, LOSS_LANE:LOSS_LANE + 1]

    const = lambda i: (0, 0)
    rows = lambda i: (i, 0)
    small_shapes = [a.shape for a in small[::3]]
    in_specs = ([pl.BlockSpec((tr_in, D_MODEL), rows)] * 4 + [pl.BlockSpec((tr_out, D_MODEL), rows)] * 4
                + [pl.BlockSpec((8, D_MODEL), const)] + [pl.BlockSpec(a.shape, const) for a in small]
                + [pl.BlockSpec((SEQ // steps, D_MODEL), rows)])
    out_specs = ([pl.BlockSpec((tr_in, D_MODEL), rows)] * 4 + [pl.BlockSpec((tr_out, D_MODEL), rows)] * 4
                 + [pl.BlockSpec(s, const) for s in small_shapes for _ in range(4)] + [pl.BlockSpec((1, 1), const)]
                 + [pl.BlockSpec((SEQ // steps, D_MODEL), rows)])
    out_shape = ([jax.ShapeDtypeStruct((SHARD_IN, D_MODEL), F32)] * 4 + [jax.ShapeDtypeStruct((SHARD_OUT, D_MODEL), F32)] * 4
                 + [jax.ShapeDtypeStruct(s, F32) for s in small_shapes for _ in range(4)]
                 + [jax.ShapeDtypeStruct((1, 1), F32), jax.ShapeDtypeStruct((SEQ, D_MODEL), F32)])
    outs = pl.pallas_call(
        body, name="adam", grid=(steps,), in_specs=in_specs, out_specs=tuple(out_specs), out_shape=tuple(out_shape),
        compiler_params=_params(dimension_semantics=("arbitrary",)),
    )(*big_in, *big_out, gsum, *small, grad_x)
    return outs[0:4], outs[4:8], [outs[8 + 4 * s:12 + 4 * s] for s in range(6)], outs[32], outs[33]


def _pad_rows(a, rows=8):
    return jnp.pad(a, ((0, rows - a.shape[0]), (0, 0)))


def kernel(x, norm_in, w_in, conv_w, attn_sinks, norm_conv_out, norm_attn_out, w_out, norm_final, loss_target, m_norm_in, m_w_in, m_conv_w, m_attn_sinks, m_norm_conv_out, m_norm_attn_out, m_w_out, m_norm_final, v_norm_in, v_w_in, v_conv_w, v_attn_sinks, v_norm_conv_out, v_norm_attn_out, v_w_out, v_norm_final):
    x2d = x.reshape(SEQ, D_MODEL)
    target = loss_target.reshape(SEQ, D_MODEL)
    nf = norm_final.reshape(1, D_MODEL)

    w_in_t, m_w_in_t, v_w_in_t = w_in[0].T, m_w_in[0].T, v_w_in[0].T
    tiles = jnp.asarray(TILE_ORDER, jnp.int32)[2 * lax.axis_index("x") + lax.axis_index("y")]
    w_in_full, h, proj, g_out, conv_full = _gather_in_proj(x2d, norm_in, w_in_t, w_out[0], _pad_rows(conv_w[0]), tiles)
    sinks = attn_sinks.reshape(N_Q_HEADS)

    mixed, attn, probs, shares = _mix_fwd(proj, conv_full, sinks, norm_conv_out, norm_attn_out)
    dx2, dx2b, dmixed, gnf, loss_part = _out_proj_loss(mixed, x2d, target, g_out.reshape(D_MIX, D_MODEL), nf)
    dproj, gslab = _mix_bwd(proj, dmixed, attn, probs, shares, conv_full, norm_conv_out, norm_attn_out)
    dw_out = _matmul_tn(mixed, dx2b, 512, "dw_out")
    dw_in_chip, g_w_out = _dw_in_rs(dproj, h, dw_out.reshape(N_DEV, SHARD_OUT, D_MODEL))
    grad_x, g_w_in, gsum = _in_bwd_rs(dproj, w_in_full, x2d, dx2, norm_in, dw_in_chip, gslab, gnf, loss_part)

    small = (norm_in, m_norm_in, v_norm_in, attn_sinks, m_attn_sinks, v_attn_sinks,
             norm_conv_out, m_norm_conv_out, v_norm_conv_out, norm_attn_out, m_norm_attn_out, v_norm_attn_out,
             nf, m_norm_final.reshape(1, D_MODEL), v_norm_final.reshape(1, D_MODEL),
             conv_w[0], m_conv_w[0], v_conv_w[0])
    big_in, big_out, (s_ni, s_sk, s_nc, s_na, s_nf, s_cv), loss, grad_x = _adam_all(
        (w_in_t, g_w_in, m_w_in_t, v_w_in_t), (w_out[0], g_w_out, m_w_out[0], v_w_out[0]), gsum, small, grad_x)

    def leaves(k):
        return (s_ni[k], big_in[k].T[None], s_cv[k][None], s_sk[k], s_nc[k], s_na[k], big_out[k][None],
                s_nf[k].reshape(D_MODEL))

    return (loss.reshape(()), grad_x.reshape(1, SEQ, D_MODEL), *leaves(0), *leaves(1), *leaves(2), *leaves(3))
```

```python
import functools
import math

import jax
import jax.numpy as jnp
from jax import lax
from jax.experimental import pallas as pl
from jax.experimental.pallas import tpu as pltpu

F32 = jnp.float32
BF16 = jnp.bfloat16
MESH = pl.DeviceIdType.MESH

N_DEV = 8
SEQ = 2048
D_MODEL = 1024
D_CONV = 1024
D_ATTN = 1024
D_KV = 128
HEAD_DIM = 64
N_Q_HEADS = 16
N_PAIRS = N_Q_HEADS // 2
PAIRS_PER_KV = N_PAIRS // 2
D_MIX = D_CONV + D_ATTN
D_PROJ = 6400
SHARD_IN = D_PROJ // N_DEV
SHARD_OUT = D_MIX // N_DEV
SHARD_CONV = D_CONV // N_DEV
OFF_CB, OFF_CC, OFF_CU, OFF_GC, OFF_Q, OFF_K, OFF_V, OFF_GA = 0, 1024, 2048, 3072, 4096, 5120, 5248, 5376
BLOCK = 128
N_BLOCKS = SEQ // BLOCK
HALO = 8
CHUNK = 16
N_CHUNKS = BLOCK // CHUNK
RMS_EPS = 1e-5
NEG = -1e30
SCALE = HEAD_DIM ** -0.5
SLOPES = tuple(2.0 ** (-8.0 * (h + 1) / N_Q_HEADS) for h in range(N_Q_HEADS))

ADAM_LR = 0.001
ADAM_B1 = 0.9
ADAM_B2 = 0.999
ADAM_EPS = 1e-08
ADAM_WD = 0.01
ADAM_STEP = 10

ROW_NORM_IN, ROW_NORM_CONV, ROW_NORM_ATTN, ROW_NORM_FINAL, ROW_CONV0, ROW_SINKS = 0, 1, 2, 3, 4, 7
LOSS_LANE = N_Q_HEADS
ACC_NORM_CONV, ACC_NORM_ATTN, ACC_CONV0, N_ACC = 0, 1, 2, 5

VMEM_LIMIT = 56 * 1024 * 1024

_NT = (((1,), (1,)), ((), ()))
_TN = (((0,), (0,)), ((), ()))


def _params(**kw):
    return pltpu.CompilerParams(vmem_limit_bytes=VMEM_LIMIT, **kw)


def _adamw(w, g, m, v):
    m = ADAM_B1 * m + (1.0 - ADAM_B1) * g
    v = ADAM_B2 * v + (1.0 - ADAM_B2) * (g * g)
    m_hat = m / (1.0 - ADAM_B1 ** ADAM_STEP)
    v_hat = v / (1.0 - ADAM_B2 ** ADAM_STEP)
    delta = -ADAM_LR * (m_hat / (jnp.sqrt(v_hat) + ADAM_EPS) + ADAM_WD * w)
    return delta, m, v


def _sigmoid(t):
    return 1.0 / (1.0 + jnp.exp(-t))


def _slot(px, py, pc):
    return 4 * px + 2 * py + pc


HALF_IN = SHARD_IN // 2
N_GATHER_KINDS = 13


IN_PROJ_TILE = 640
TILE_ORDER = ((0, 1, 2, 3, 4, 5, 6, 7, 8, 9), (3, 4, 0, 1, 2, 8, 9, 5, 6, 7),
              (5, 6, 0, 1, 7, 8, 9, 2, 3, 4), (8, 9, 3, 4, 5, 6, 7, 0, 1, 2))
TILES_OWN, TILES_NEIGHBOURS = 2, 7


def _gather_in_proj(x, norm_in, w_in_sh, w_out_sh, conv_sh, tiles):
    tn = IN_PROJ_TILE
    steps = D_PROJ // tn
    tm = 256

    def body(tiles_ref, x_ref, g_ref, win_ref, wout_ref, cv_ref, wt_ref, h_ref, proj_ref, gout_ref, conv_ref,
             gin_ref, gcv_ref, wob_ref, send_sems, recv_sems, local_sems):
        p = pl.program_id(0)
        wout_start, wout_forward, wout_finish = _wout_gather(wob_ref, gout_ref, send_sems, recv_sems, local_sems.at[1],
                                                             N_GATHER_KINDS + 7)
        local_sem = local_sems.at[0]
        x, y, c = lax.axis_index("x"), lax.axis_index("y"), lax.axis_index("c")
        me, sibling = (x, y, c), (x, y, 1 - c)
        nx, ny, dg = (1 - x, y, c), (x, 1 - y, c), (1 - x, 1 - y, c)

        def other(dev):
            return (dev[0], dev[1], 1 - dev[2])

        def shard(dev):
            return gin_ref.at[pl.ds(pl.multiple_of(_slot(*dev) * SHARD_IN, 16), SHARD_IN), :]

        def half(dev, h):
            return gin_ref.at[pl.ds(pl.multiple_of(_slot(*dev) * SHARD_IN + h * HALF_IN, 16), HALF_IN), :]

        def rc(ref, k, to):
            return pltpu.make_async_remote_copy(src_ref=ref, dst_ref=ref, send_sem=send_sems.at[k],
                                                recv_sem=recv_sems.at[k], device_id=to, device_id_type=MESH)

        def cv(k, dev, to):
            s = _slot(*dev)
            return pltpu.make_async_remote_copy(src_ref=gcv_ref.at[s], dst_ref=gcv_ref.at[s],
                                                send_sem=send_sems.at[N_GATHER_KINDS + k],
                                                recv_sem=recv_sems.at[N_GATHER_KINDS + k], device_id=to, device_id_type=MESH)

        def own_copies():
            return [rc(shard(me), 0, sibling),
                    rc(half(me, 0), 1, nx), rc(half(me, 1), 2, nx),
                    rc(half(me, 1), 4, ny), rc(half(me, 0), 3, ny),
                    cv(0, me, sibling)] + [cv(1 + j, me, peer) for j, peer in enumerate((nx, ny, dg))]

        def pass_on(dev, h, k_in, k_ici, k_d2d):
            rc(half(dev, h), k_in, me).wait_recv()
            if k_ici is not None:
                rc(half(dev, h), k_ici, ny if dev is nx else nx).start()
            rc(half(dev, h), k_d2d, sibling).start()

        @pl.when(p == 0)
        def _():
            gin_ref[pl.ds(pl.multiple_of(_slot(*me) * SHARD_IN, 16), SHARD_IN), :] = win_ref[...].astype(BF16)
            gcv_ref[_slot(*me)] = jnp.zeros((8, SHARD_CONV), F32)
            gcv_ref[_slot(*me), 0:3, :] = cv_ref[:, 0, :]
            for cp in own_copies():
                cp.start()
            wob_ref[...] = wout_ref[...].astype(BF16)
            for t in range(SEQ // tm):
                xv = x_ref[tm * t:tm * (t + 1), :]
                r = lax.rsqrt(jnp.mean(xv * xv, axis=-1, keepdims=True) + RMS_EPS)
                h_ref[tm * t:tm * (t + 1), :] = (xv * r * g_ref[...]).astype(BF16)
            rc(shard(sibling), 0, me).wait_recv()

        @pl.when(p == TILES_OWN)
        def _():
            for args in ((nx, 0, 1, 5, 7), (ny, 1, 4, 6, 10), (nx, 1, 2, None, 8), (ny, 0, 3, None, 9)):
                pass_on(*args)
            for j, peer in enumerate((nx, ny, dg)):
                cv(1 + j, peer, me).wait_recv()
                cv(4 + j, peer, sibling).start()
            for (dev, h), k in (((nx, 0), 7), ((nx, 1), 8), ((ny, 0), 9), ((ny, 1), 10)):
                rc(half(other(dev), h), k, me).wait_recv()
            wout_start()

        @pl.when(p == TILES_NEIGHBOURS)
        def _():
            pass_on(dg, 0, 5, None, 11)
            pass_on(dg, 1, 6, None, 12)
            for (dev, h), k in (((dg, 0), 11), ((dg, 1), 12)):
                rc(half(other(dev), h), k, me).wait_recv()
            pltpu.make_async_copy(gin_ref, wt_ref, local_sem).start()

        @pl.when(p == steps - 1)
        def _():
            wout_forward()

        w = gin_ref[pl.ds(pl.multiple_of(tiles_ref[p] * tn, tn), tn), :]
        proj_ref[...] = lax.dot_general(h_ref[...], w, _NT, preferred_element_type=F32)

        @pl.when(p == steps - 1)
        def _():
            cv(0, sibling, me).wait_recv()
            for j, peer in enumerate((nx, ny, dg)):
                cv(4 + j, other(peer), me).wait_recv()
            for d in range(N_DEV):
                conv_ref[:, d * SHARD_CONV:(d + 1) * SHARD_CONV] = gcv_ref[d]
            relayed = [rc(half(nx, 0), 5, ny), rc(half(ny, 1), 6, nx)]
            relayed += [rc(half(dev, h), k, sibling) for (dev, h), k in
                        (((nx, 0), 7), ((nx, 1), 8), ((ny, 0), 9), ((ny, 1), 10), ((dg, 0), 11), ((dg, 1), 12))]
            relayed += [cv(4 + j, peer, sibling) for j, peer in enumerate((nx, ny, dg))]
            for cp in own_copies() + relayed:
                cp.wait_send()
            pltpu.make_async_copy(gin_ref, wt_ref, local_sem).wait()
            wout_finish()

    vmem = pl.BlockSpec(memory_space=pltpu.VMEM)
    grid_spec = pltpu.PrefetchScalarGridSpec(
        num_scalar_prefetch=1, grid=(steps,),
        in_specs=[vmem, vmem, vmem, vmem, vmem],
        out_specs=(pl.BlockSpec(memory_space=pl.ANY), vmem,
                   pl.BlockSpec((SEQ, tn), lambda p, tiles_ref: (0, tiles_ref[p])), pl.BlockSpec(memory_space=pl.ANY), vmem),
        scratch_shapes=[pltpu.VMEM((D_PROJ, D_MODEL), BF16), pltpu.VMEM((N_DEV, 8, SHARD_CONV), F32),
                        pltpu.VMEM((SHARD_OUT, D_MODEL), BF16),
                        pltpu.SemaphoreType.DMA((N_GATHER_KINDS + 14,)), pltpu.SemaphoreType.DMA((N_GATHER_KINDS + 14,)),
                        pltpu.SemaphoreType.DMA((2,))])
    return pl.pallas_call(
        body, name="gather_in_proj", grid_spec=grid_spec,
        out_shape=(jax.ShapeDtypeStruct((D_PROJ, D_MODEL), BF16), jax.ShapeDtypeStruct((SEQ, D_MODEL), BF16),
                   jax.ShapeDtypeStruct((SEQ, D_PROJ), F32), jax.ShapeDtypeStruct((N_DEV, SHARD_OUT, D_MODEL), BF16),
                   jax.ShapeDtypeStruct((8, D_CONV), F32)),
        compiler_params=_params(dimension_semantics=("arbitrary",)),
    )(tiles, x, norm_in, w_in_sh, w_out_sh, conv_sh)


def _wout_gather(wo_ref, gout_ref, send_sems, recv_sems, local_sem, base=0):
    x, y, c = lax.axis_index("x"), lax.axis_index("y"), lax.axis_index("c")
    me, sibling = (x, y, c), (x, y, 1 - c)
    chips = [(1 - x, y), (x, 1 - y), (1 - x, 1 - y)]

    def copy(k, block, to, src=None):
        rows = gout_ref.at[_slot(*block)]
        return pltpu.make_async_remote_copy(src_ref=rows if src is None else src, dst_ref=rows,
                                            send_sem=send_sems.at[base + k], recv_sem=recv_sems.at[base + k],
                                            device_id=to, device_id_type=MESH)

    def mine():
        return pltpu.make_async_copy(wo_ref, gout_ref.at[_slot(*me)], local_sem)

    def start():
        mine().start()
        copy(0, me, sibling, src=wo_ref).start()
        for j, chip in enumerate(chips):
            copy(1 + j, me, (*chip, c), src=wo_ref).start()

    def forward():
        for j, chip in enumerate(chips):
            copy(1 + j, (*chip, c), me).wait_recv()
            copy(4 + j, (*chip, c), sibling).start()

    def finish():
        copy(0, sibling, me).wait_recv()
        for j, chip in enumerate(chips):
            copy(4 + j, (*chip, 1 - c), me).wait_recv()
        copy(0, me, sibling, src=wo_ref).wait_send()
        for j, chip in enumerate(chips):
            copy(1 + j, me, (*chip, c), src=wo_ref).wait_send()
            copy(4 + j, (*chip, c), sibling).wait_send()
        mine().wait()

    return start, forward, finish


def _shard_sum(src, own, d2d, ici, send_sems, recv_sems, local_sems, base=0):
    x, y, c = lax.axis_index("x"), lax.axis_index("y"), lax.axis_index("c")
    sibling = (x, y, 1 - c)
    chips = [(x, y), (1 - x, y), (x, 1 - y), (1 - x, 1 - y)]

    def rcopy(s, d, k, to):
        return pltpu.make_async_remote_copy(src_ref=s, dst_ref=d, send_sem=send_sems.at[base + k],
                                            recv_sem=recv_sems.at[base + k], device_id=to, device_id_type=MESH)

    def mine(k):
        return pltpu.make_async_copy(src.at[_slot(*chips[k], c)], own.at[k], local_sems.at[k])

    def to_sibling(k):
        return rcopy(src.at[_slot(*chips[k], 1 - c)], d2d.at[k], k, sibling)

    def to_chip(k):
        return rcopy(own.at[k], ici.at[k - 1], 3 + k, (*chips[k], c))

    def start():
        for k in range(4):
            mine(k).start()
            to_sibling(k).start()

    def forward():
        for k in range(1, 4):
            mine(k).wait()
            to_sibling(k).wait_recv()
            own[k] = (own[k].astype(F32) + d2d[k].astype(F32)).astype(BF16)
            to_chip(k).start()

    def finish():
        mine(0).wait()
        to_sibling(0).wait_recv()
        acc = own[0].astype(F32) + d2d[0].astype(F32)
        for k in range(1, 4):
            to_chip(k).wait_recv()
            acc = acc + ici[k - 1].astype(F32)
        for k in range(4):
            to_sibling(k).wait_send()
        for k in range(1, 4):
            to_chip(k).wait_send()
        return acc

    return start, forward, finish


def _shard_sum_scratch(rows):
    return [pltpu.VMEM((4, rows, D_MODEL), BF16), pltpu.VMEM((4, rows, D_MODEL), BF16),
            pltpu.VMEM((3, rows, D_MODEL), BF16)]


N_SHARD_SUM_SEMS = 7


def _chip_sum(dwt, d2d, out_hbm, send_sems, recv_sems, local_sems, base, local_base):
    x, y, c = lax.axis_index("x"), lax.axis_index("y"), lax.axis_index("c")
    sibling = (x, y, 1 - c)
    chips = [(x, y), (1 - x, y), (x, 1 - y), (1 - x, 1 - y)]

    def shard(s):
        return dwt.at[pl.ds(pl.multiple_of(s * SHARD_IN, 16), SHARD_IN), :]

    def to_sibling(k):
        return pltpu.make_async_remote_copy(src_ref=shard(_slot(*chips[k], 1 - c)), dst_ref=d2d.at[k],
                                            send_sem=send_sems.at[base + k], recv_sem=recv_sems.at[base + k],
                                            device_id=sibling, device_id_type=MESH)

    def save(k):
        return pltpu.make_async_copy(d2d.at[k], out_hbm.at[k], local_sems.at[local_base + k])

    def send(first):
        for k in range(4):
            in_first = _slot(*chips[k], 1 - c) < N_DEV // 2

            @pl.when(in_first if first else jnp.logical_not(in_first))
            def _():
                to_sibling(k).start()

    def finish():
        for k in range(4):
            to_sibling(k).wait_recv()
            d2d[k] = (shard(_slot(*chips[k], c))[...].astype(F32) + d2d[k].astype(F32)).astype(BF16)
            save(k).start()
        for k in range(4):
            save(k).wait()
            to_sibling(k).wait_send()

    return send, finish


N_ICI_SUM_SEMS = 6


def _ici_sum(src, own, ici, via, stage, send_sems, recv_sems, local_sems, base=0):
    x, y, c = lax.axis_index("x"), lax.axis_index("y"), lax.axis_index("c")
    nx, ny = (1 - x, y, c), (x, 1 - y, c)
    OWN, NX, NY, DG = range(4)

    def half(ref, h):
        return ref.at[pl.ds(h * HALF_IN, HALF_IN), :]

    def rc(s, d, k, to):
        return pltpu.make_async_remote_copy(src_ref=s, dst_ref=d, send_sem=send_sems.at[base + k],
                                            recv_sem=recv_sems.at[base + k], device_id=to, device_id_type=MESH)

    for_dg_0 = lambda: rc(half(src.at[DG], 0), via.at[0], 0, nx)
    for_dg_1 = lambda: rc(half(src.at[DG], 1), via.at[1], 1, ny)
    for_nx_0 = lambda: rc(half(src.at[NX], 0), half(ici.at[0], 0), 2, nx)
    for_ny_1 = lambda: rc(half(src.at[NY], 1), half(ici.at[1], 1), 3, ny)
    for_ny_0 = lambda: rc(stage.at[0], half(ici.at[1], 0), 4, ny)
    for_nx_1 = lambda: rc(stage.at[1], half(ici.at[0], 1), 5, nx)
    mine = lambda: pltpu.make_async_copy(src.at[OWN], own, local_sems.at[0])
    stage_0 = lambda: pltpu.make_async_copy(half(src.at[NY], 0), stage.at[0], local_sems.at[1])
    stage_1 = lambda: pltpu.make_async_copy(half(src.at[NX], 1), stage.at[1], local_sems.at[2])

    def start():
        for cp in (for_dg_0, for_dg_1, for_nx_0, for_ny_1, stage_0, stage_1, mine):
            cp().start()

    def relay():
        for h, staged, landed, out in ((0, stage_0, for_dg_0, for_ny_0), (1, stage_1, for_dg_1, for_nx_1)):
            staged().wait()
            landed().wait_recv()
            stage[h] = (stage[h].astype(F32) + via[h].astype(F32)).astype(BF16)
            out().start()

    def finish():
        mine().wait()
        for cp in (for_nx_0, for_nx_1, for_ny_1, for_ny_0):
            cp().wait_recv()
        acc = own[...].astype(F32) + ici[0].astype(F32) + ici[1].astype(F32)
        for cp in (for_dg_0, for_dg_1, for_nx_0, for_ny_1, for_ny_0, for_nx_1):
            cp().wait_send()
        return acc

    return start, relay, finish


def _slab_sum(myslab, slabs, send_sems, recv_sems, base):
    x, y, c = lax.axis_index("x"), lax.axis_index("y"), lax.axis_index("c")
    me = _slot(x, y, c)
    peers = [(x, y, 1 - c), (1 - x, y, c), (x, 1 - y, c), (1 - x, 1 - y, c),
             (1 - x, y, 1 - c), (x, 1 - y, 1 - c), (1 - x, 1 - y, 1 - c)]

    def cp(k):
        return pltpu.make_async_remote_copy(src_ref=myslab, dst_ref=slabs.at[me], send_sem=send_sems.at[base + k],
                                            recv_sem=recv_sems.at[base + k], device_id=peers[k], device_id_type=MESH)

    def start():
        slabs[me] = myslab[...]
        for k in range(7):
            cp(k).start()

    def finish():
        for k in range(7):
            cp(k).wait_recv()
        total = slabs[0]
        for d in range(1, N_DEV):
            total = total + slabs[d]
        for k in range(7):
            cp(k).wait_send()
        return total

    return start, finish


def _chunk_rows(r):
    return pl.ds(pl.multiple_of(r * CHUNK, CHUNK), CHUNK)


def _conv_halo(cch_ref, cuh_ref, n):
    zh = jnp.where(n > 0, cch_ref[...] * cuh_ref[...], 0.0)
    return jnp.concatenate([zh] * (CHUNK // HALO), axis=0)


def _conv_chunk(pj_ref, zhalo, cw, r):
    rows = _chunk_rows(r)
    cc = pj_ref[rows, OFF_CC:OFF_CC + D_CONV]
    cu = pj_ref[rows, OFF_CU:OFF_CU + D_CONV]
    z = cc * cu
    before = _chunk_rows(jnp.maximum(r - 1, 0))
    zprev = jnp.where(r > 0, pj_ref[before, OFF_CC:OFF_CC + D_CONV] * pj_ref[before, OFF_CU:OFF_CU + D_CONV], zhalo)
    row = lax.broadcasted_iota(jnp.int32, (CHUNK, D_CONV), 0)
    z1 = jnp.where(row < 1, pltpu.roll(zprev, 1, 0), pltpu.roll(z, 1, 0))
    z2 = jnp.where(row < 2, pltpu.roll(zprev, 2, 0), pltpu.roll(z, 2, 0))
    co = cw[0] * z2 + cw[1] * z1 + cw[2] * z
    return cc, cu, z, z1, z2, co


def _gated_norm(a, gain, t):
    r = lax.rsqrt(jnp.mean(a * a, axis=-1, keepdims=True) + RMS_EPS)
    return a * r * gain * (t * _sigmoid(t))


def _kv_bands(pj, kvp_ref):
    lane = lax.broadcasted_iota(jnp.int32, (2 * BLOCK, D_KV), 1)
    lo = lane < HEAD_DIM

    def bands(prev, cur):
        b = jnp.concatenate([prev, cur], axis=0)
        br = pltpu.roll(b, HEAD_DIM, 1)
        zero = jnp.zeros_like(b)
        return ((jnp.where(lo, b, zero).astype(BF16), jnp.where(lo, zero, br).astype(BF16)),
                (jnp.where(lo, br, zero).astype(BF16), jnp.where(lo, zero, b).astype(BF16)))

    ks = bands(kvp_ref[:, 0:D_KV], pj[:, OFF_K:OFF_K + D_KV])
    vs = bands(kvp_ref[:, D_KV:2 * D_KV], pj[:, OFF_V:OFF_V + D_KV])
    return ks, vs


STACK = PAIRS_PER_KV * BLOCK


def _head(j, i, e):
    return 2 * (PAIRS_PER_KV * j + i) + e


def _pair_cols(j, i, off):
    p = PAIRS_PER_KV * j + i
    return slice(off + 128 * p, off + 128 * (p + 1))


def _fill_attn_bias(bias_scr, first_block):
    qi = lax.broadcasted_iota(jnp.int32, (BLOCK, 2 * BLOCK), 0)
    kj = lax.broadcasted_iota(jnp.int32, (BLOCK, 2 * BLOCK), 1)
    dist = BLOCK + qi - kj
    valid = (dist >= 0) & (dist < BLOCK)
    if first_block:
        valid = valid & (kj >= BLOCK)
    distf = dist.astype(F32)
    for j in range(2):
        for e in range(2):
            for i in range(PAIRS_PER_KV):
                bias_scr[2 * j + e, BLOCK * i:BLOCK * (i + 1), :] = jnp.where(valid, -SLOPES[_head(j, i, e)] * distf, NEG)


def _q_stack(pj, j):
    return jnp.concatenate([(pj[:, _pair_cols(j, i, OFF_Q)] * SCALE).astype(BF16) for i in range(PAIRS_PER_KV)], axis=0)


def _sink_rows(sink_ref, j, e):
    return jnp.concatenate([jnp.full((BLOCK, 128), sink_ref[_head(j, i, e)], F32) for i in range(PAIRS_PER_KV)], axis=0)


def _attn_probs(q_stack, kband, bias, sink):
    s = lax.dot_general(q_stack, kband, _NT, preferred_element_type=F32) + bias
    m = jnp.broadcast_to(jnp.max(s, axis=-1, keepdims=True), (STACK, 128))
    m = jnp.maximum(m, sink)
    p = [jnp.exp(t - m) for t in (s[:, :128], s[:, 128:])]
    es = jnp.exp(sink - m)
    ones = jnp.ones((128, 128), BF16)
    total = (jnp.dot(p[0].astype(BF16), ones, preferred_element_type=F32)
             + jnp.dot(p[1].astype(BF16), ones, preferred_element_type=F32))
    inv = 1.0 / (total + es)
    return jnp.concatenate([p[0] * inv, p[1] * inv], axis=1), es * inv


def _attn_group(pj, ks, vs, bias_scr, sink_ref, j):
    q_stack = _q_stack(pj, j)
    out, probs, shares = None, [], []
    for e in range(2):
        p, ps = _attn_probs(q_stack, ks[j][e], bias_scr[2 * j + e], _sink_rows(sink_ref, j, e))
        p = p.astype(BF16)
        o = jnp.dot(p, vs[j][e], preferred_element_type=F32)
        out = o if out is None else out + o
        probs.append(p)
        shares.append(ps)
    return out, probs, shares


def _mix_fwd(proj, conv_full, sinks, norm_conv, norm_attn):
    def body(pj_ref, kvp_ref, cch_ref, cuh_ref, cw_ref, sink_ref, gc_ref, ga_ref,
             mixed_ref, attn_scr, p_ref, ps_ref, bias_scr):
        n = pl.program_id(0)
        pj = pj_ref

        @pl.when(n == 0)
        def _():
            _fill_attn_bias(bias_scr, first_block=True)

        @pl.when(n == 1)
        def _():
            _fill_attn_bias(bias_scr, first_block=False)

        zhalo = _conv_halo(cch_ref, cuh_ref, n)
        cw = (cw_ref[0:1, :], cw_ref[1:2, :], cw_ref[2:3, :])
        gain_c = gc_ref[...]

        def conv_chunk(r, carry):
            rows = _chunk_rows(r)
            co = _conv_chunk(pj_ref, zhalo, cw, r)[-1]
            y = _gated_norm(pj_ref[rows, OFF_CB:OFF_CB + D_CONV] * co, gain_c, pj_ref[rows, OFF_GC:OFF_GC + D_CONV])
            mixed_ref[rows, 0:D_CONV] = y.astype(BF16)
            return carry

        lax.fori_loop(0, N_CHUNKS, conv_chunk, 0, unroll=True)

        ks, vs = _kv_bands(pj, kvp_ref)
        for j in range(2):
            out, probs, shares = _attn_group(pj, ks, vs, bias_scr, sink_ref, j)
            for e in range(2):
                p_ref[0, 2 * j + e] = probs[e]
                ps_ref[0, 2 * j + e] = shares[e]
            for i in range(PAIRS_PER_KV):
                attn_scr[:, _pair_cols(j, i, 0)] = out[BLOCK * i:BLOCK * (i + 1), :]
        gain_a = ga_ref[...]

        def norm_chunk(r, carry):
            rows = _chunk_rows(r)
            y = _gated_norm(attn_scr[rows, :], gain_a, pj_ref[rows, OFF_GA:OFF_GA + D_ATTN])
            mixed_ref[rows, D_CONV:D_MIX] = y.astype(BF16)
            return carry

        lax.fori_loop(0, N_CHUNKS, norm_chunk, 0, unroll=True)

    per_block = BLOCK // HALO
    return pl.pallas_call(
        body, name="mix_fwd", grid=(N_BLOCKS,),
        in_specs=[
            pl.BlockSpec((BLOCK, D_PROJ), lambda n: (n, 0)),
            pl.BlockSpec((BLOCK, 2 * D_KV), lambda n: (jnp.maximum(n - 1, 0), OFF_K // (2 * D_KV))),
            pl.BlockSpec((HALO, D_CONV), lambda n: (jnp.maximum(n * per_block - 1, 0), OFF_CC // D_CONV)),
            pl.BlockSpec((HALO, D_CONV), lambda n: (jnp.maximum(n * per_block - 1, 0), OFF_CU // D_CONV)),
            pl.BlockSpec((8, D_CONV), lambda n: (0, 0)),
            pl.BlockSpec(memory_space=pltpu.SMEM),
            pl.BlockSpec((1, D_CONV), lambda n: (0, 0)),
            pl.BlockSpec((1, D_ATTN), lambda n: (0, 0)),
        ],
        out_specs=(pl.BlockSpec((BLOCK, D_MIX), lambda n: (n, 0)), pl.BlockSpec((BLOCK, D_ATTN), lambda n: (n, 0)),
                   pl.BlockSpec((1, 4, STACK, 2 * BLOCK), lambda n: (n, 0, 0, 0)),
                   pl.BlockSpec((1, 4, STACK, 128), lambda n: (n, 0, 0, 0))),
        out_shape=(jax.ShapeDtypeStruct((SEQ, D_MIX), BF16), jax.ShapeDtypeStruct((SEQ, D_ATTN), F32),
                   jax.ShapeDtypeStruct((N_BLOCKS, 4, STACK, 2 * BLOCK), BF16),
                   jax.ShapeDtypeStruct((N_BLOCKS, 4, STACK, 128), F32)),
        scratch_shapes=[pltpu.VMEM((4, STACK, 2 * BLOCK), F32)],
        compiler_params=_params(dimension_semantics=("arbitrary",)),
    )(proj, proj, proj, proj, conv_full, sinks, norm_conv, norm_attn)


def _out_proj_loss(mixed, x, target, w_out_full, norm_final):
    tm = 256

    def body(mx_ref, x_ref, t_ref, w_ref, g_ref, dx2_ref, dx2b_ref, dmix_ref, gnf_ref, loss_ref):
        i = pl.program_id(0)
        w = w_ref[...]
        x2 = x_ref[...] + jnp.dot(mx_ref[...], w, preferred_element_type=F32)
        r = lax.rsqrt(jnp.mean(x2 * x2, axis=-1, keepdims=True) + RMS_EPS)
        xn = x2 * r
        g = g_ref[...]
        err = xn * g - t_ref[...]
        part = 0.5 * jnp.sum(jnp.mean(err * err, axis=-1, keepdims=True), axis=0, keepdims=True)
        dy = err * (1.0 / D_MODEL)
        gnf = jnp.sum(dy * xn, axis=0, keepdims=True)
        u = dy * g
        dx2 = r * (u - xn * jnp.mean(u * xn, axis=-1, keepdims=True))
        dx2_ref[...] = dx2
        dx2b = dx2.astype(BF16)
        dx2b_ref[...] = dx2b
        dmix_ref[...] = lax.dot_general(dx2b, w, _NT, preferred_element_type=F32)

        @pl.when(i == 0)
        def _():
            gnf_ref[...] = jnp.zeros_like(gnf_ref)
            loss_ref[...] = jnp.zeros_like(loss_ref)

        gnf_ref[...] += gnf
        loss_ref[...] += jnp.broadcast_to(part, loss_ref.shape)

    return pl.pallas_call(
        body, name="out_proj_loss", grid=(SEQ // tm,),
        in_specs=[pl.BlockSpec((tm, D_MIX), lambda i: (i, 0)), pl.BlockSpec((tm, D_MODEL), lambda i: (i, 0)),
                  pl.BlockSpec((tm, D_MODEL), lambda i: (i, 0)), pl.BlockSpec(memory_space=pltpu.VMEM),
                  pl.BlockSpec((1, D_MODEL), lambda i: (0, 0))],
        out_specs=(pl.BlockSpec((tm, D_MODEL), lambda i: (i, 0)), pl.BlockSpec((tm, D_MODEL), lambda i: (i, 0)),
                   pl.BlockSpec((tm, D_MIX), lambda i: (i, 0)),
                   pl.BlockSpec((1, D_MODEL), lambda i: (0, 0)), pl.BlockSpec((8, 128), lambda i: (0, 0))),
        out_shape=(jax.ShapeDtypeStruct((SEQ, D_MODEL), F32), jax.ShapeDtypeStruct((SEQ, D_MODEL), BF16),
                   jax.ShapeDtypeStruct((SEQ, D_MIX), F32),
                   jax.ShapeDtypeStruct((1, D_MODEL), F32), jax.ShapeDtypeStruct((8, 128), F32)),
        compiler_params=_params(dimension_semantics=("arbitrary",)),
    )(mixed, x, target, w_out_full, norm_final)


def _gated_norm_bwd(a, gain, t, dy):
    r = lax.rsqrt(jnp.mean(a * a, axis=-1, keepdims=True) + RMS_EPS)
    an = a * r
    sg = _sigmoid(t)
    dn = dy * (t * sg)
    dt = dy * (an * gain) * (sg * (1.0 + t * (1.0 - sg)))
    u = dn * gain
    da = r * (u - an * jnp.mean(u * an, axis=-1, keepdims=True))
    return da, dt, dn * an


def _mix_bwd(proj, dmixed, attn, probs, shares, conv_full, norm_conv, norm_attn):
    def body(pj_ref, kvp_ref, cch_ref, cuh_ref, dmx_ref, attn_ref, p_ref, ps_ref, cw_ref, gc_ref, ga_ref,
             dpj_ref, gslab_ref, dattn_scr, nxt_scr, dkv_scr, acc_scr):
        step = pl.program_id(0)
        n = N_BLOCKS - 1 - step
        pj = pj_ref

        @pl.when(step == 0)
        def _():
            gslab_ref[...] = jnp.zeros_like(gslab_ref)
            nxt_scr[...] = jnp.zeros_like(nxt_scr)
            dkv_scr[...] = jnp.zeros_like(dkv_scr)
            acc_scr[...] = jnp.zeros_like(acc_scr)

        zhalo = _conv_halo(cch_ref, cuh_ref, n)
        cw = (cw_ref[0:1, :], cw_ref[1:2, :], cw_ref[2:3, :])
        gain_c = gc_ref[...]
        row = lax.broadcasted_iota(jnp.int32, (CHUNK, D_CONV), 0)

        def conv_chunk(t, dco_after):
            r = N_CHUNKS - 1 - t
            rows = _chunk_rows(r)
            cc, cu, z, z1, z2, co = _conv_chunk(pj_ref, zhalo, cw, r)
            cb = pj_ref[rows, OFF_CB:OFF_CB + D_CONV]
            da, dgate, gterm = _gated_norm_bwd(cb * co, gain_c, pj_ref[rows, OFF_GC:OFF_GC + D_CONV],
                                               dmx_ref[rows, 0:D_CONV])
            dpj_ref[rows, OFF_GC:OFF_GC + D_CONV] = dgate.astype(BF16)
            dpj_ref[rows, OFF_CB:OFF_CB + D_CONV] = (da * co).astype(BF16)
            dco = da * cb
            dco1 = jnp.where(row >= CHUNK - 1, pltpu.roll(dco_after, CHUNK - 1, 0), pltpu.roll(dco, CHUNK - 1, 0))
            dco2 = jnp.where(row >= CHUNK - 2, pltpu.roll(dco_after, CHUNK - 2, 0), pltpu.roll(dco, CHUNK - 2, 0))
            dz = cw[2] * dco + cw[1] * dco1 + cw[0] * dco2
            dpj_ref[rows, OFF_CC:OFF_CC + D_CONV] = (dz * cu).astype(BF16)
            dpj_ref[rows, OFF_CU:OFF_CU + D_CONV] = (dz * cc).astype(BF16)
            acc_scr[ACC_NORM_CONV] += gterm
            acc_scr[ACC_CONV0] += dco * z2
            acc_scr[ACC_CONV0 + 1] += dco * z1
            acc_scr[ACC_CONV0 + 2] += dco * z
            return dco

        nxt_scr[...] = lax.fori_loop(0, N_CHUNKS, conv_chunk, nxt_scr[...], unroll=True)

        ks, vs = _kv_bands(pj, kvp_ref)
        gain_a = ga_ref[...]

        def norm_chunk(r, carry):
            rows = _chunk_rows(r)
            da, dgate, gterm = _gated_norm_bwd(attn_ref[rows, :], gain_a, pj_ref[rows, OFF_GA:OFF_GA + D_ATTN],
                                               dmx_ref[rows, D_CONV:D_MIX])
            dpj_ref[rows, OFF_GA:OFF_GA + D_ATTN] = dgate.astype(BF16)
            dattn_scr[rows, :] = da
            acc_scr[ACC_NORM_ATTN] += gterm
            return carry

        lax.fori_loop(0, N_CHUNKS, norm_chunk, 0, unroll=True)

        in_lo = lax.broadcasted_iota(jnp.int32, (128, 128), 0) < HEAD_DIM
        half_ones = (jnp.where(in_lo, 1.0, 0.0).astype(BF16), jnp.where(in_lo, 0.0, 1.0).astype(BF16))
        lane_s = lax.broadcasted_iota(jnp.int32, (1, D_MODEL), 1)
        gsink = jnp.zeros((1, D_MODEL), F32)
        dk_t, dv_t = [], []
        for j in range(2):
            q_stack = _q_stack(pj, j)
            do_f = jnp.concatenate([dattn_scr[:, _pair_cols(j, i, 0)] for i in range(PAIRS_PER_KV)], axis=0)
            o_f = jnp.concatenate([attn_ref[:, _pair_cols(j, i, 0)] for i in range(PAIRS_PER_KV)], axis=0)
            prod = (do_f * o_f).astype(BF16)
            deltas = [jnp.dot(prod, half_ones[e], preferred_element_type=F32) for e in range(2)]
            do_b = do_f.astype(BF16)
            q_t, do_t = q_stack.T, do_b.T
            dq, dk_j, dv_j = None, None, None
            for e in range(2):
                p = p_ref[0, 2 * j + e]
                dp = lax.dot_general(do_b, vs[j][e], _NT, preferred_element_type=F32)
                delta = jnp.concatenate([deltas[e], deltas[e]], axis=1)
                ds = (p.astype(F32) * (dp - delta)).astype(BF16)
                gs = ps_ref[0, 2 * j + e] * deltas[e]
                for i in range(PAIRS_PER_KV):
                    gs_h = -jnp.sum(gs[BLOCK * i:BLOCK * (i + 1), 0:1], axis=0, keepdims=True)
                    gsink = gsink + jnp.where(lane_s == _head(j, i, e), gs_h, 0.0)
                t = jnp.dot(ds, ks[j][e], preferred_element_type=F32)
                dq = t if dq is None else dq + t
                half = slice(HEAD_DIM * e, HEAD_DIM * (e + 1))
                a = jnp.dot(q_t[half, :], ds, preferred_element_type=F32)
                b = jnp.dot(do_t[half, :], p, preferred_element_type=F32)
                dk_j = a if dk_j is None else dk_j + a
                dv_j = b if dv_j is None else dv_j + b
            for i in range(PAIRS_PER_KV):
                dpj_ref[:, _pair_cols(j, i, OFF_Q)] = (dq[BLOCK * i:BLOCK * (i + 1), :] * SCALE).astype(BF16)
            dk_t.append(dk_j)
            dv_t.append(dv_j)
        dk = jnp.concatenate(dk_t, axis=0).T
        dv = jnp.concatenate(dv_t, axis=0).T
        dpj_ref[:, OFF_K:OFF_K + D_KV] = (dk[BLOCK:, :] + dkv_scr[:, 0:D_KV]).astype(BF16)
        dpj_ref[:, OFF_V:OFF_V + D_KV] = (dv[BLOCK:, :] + dkv_scr[:, D_KV:2 * D_KV]).astype(BF16)
        dkv_scr[:, 0:D_KV] = dk[:BLOCK, :]
        dkv_scr[:, D_KV:2 * D_KV] = dv[:BLOCK, :]
        gslab_ref[ROW_SINKS:ROW_SINKS + 1, :] += gsink

        @pl.when(step == N_BLOCKS - 1)
        def _():
            for k, slab_row in ((ACC_NORM_CONV, ROW_NORM_CONV), (ACC_NORM_ATTN, ROW_NORM_ATTN), (ACC_CONV0, ROW_CONV0),
                                (ACC_CONV0 + 1, ROW_CONV0 + 1), (ACC_CONV0 + 2, ROW_CONV0 + 2)):
                gslab_ref[slab_row:slab_row + 1, :] = jnp.sum(acc_scr[k], axis=0, keepdims=True)

    per_block = BLOCK // HALO
    last = N_BLOCKS - 1
    return pl.pallas_call(
        body, name="mix_bwd", grid=(N_BLOCKS,),
        in_specs=[
            pl.BlockSpec((BLOCK, D_PROJ), lambda s: (last - s, 0)),
            pl.BlockSpec((BLOCK, 2 * D_KV), lambda s: (jnp.maximum(last - s - 1, 0), OFF_K // (2 * D_KV))),
            pl.BlockSpec((HALO, D_CONV), lambda s: (jnp.maximum((last - s) * per_block - 1, 0), OFF_CC // D_CONV)),
            pl.BlockSpec((HALO, D_CONV), lambda s: (jnp.maximum((last - s) * per_block - 1, 0), OFF_CU // D_CONV)),
            pl.BlockSpec((BLOCK, D_MIX), lambda s: (last - s, 0)),
            pl.BlockSpec((BLOCK, D_ATTN), lambda s: (last - s, 0)),
            pl.BlockSpec((1, 4, STACK, 2 * BLOCK), lambda s: (last - s, 0, 0, 0)),
            pl.BlockSpec((1, 4, STACK, 128), lambda s: (last - s, 0, 0, 0)),
            pl.BlockSpec((8, D_CONV), lambda s: (0, 0)),
            pl.BlockSpec((1, D_CONV), lambda s: (0, 0)),
            pl.BlockSpec((1, D_ATTN), lambda s: (0, 0)),
        ],
        out_specs=(pl.BlockSpec((BLOCK, D_PROJ), lambda s: (last - s, 0)),
                   pl.BlockSpec((8, D_MODEL), lambda s: (0, 0))),
        out_shape=(jax.ShapeDtypeStruct((SEQ, D_PROJ), BF16), jax.ShapeDtypeStruct((8, D_MODEL), F32)),
        scratch_shapes=[pltpu.VMEM((BLOCK, D_ATTN), F32), pltpu.VMEM((CHUNK, D_CONV), F32),
                        pltpu.VMEM((BLOCK, 2 * D_KV), F32), pltpu.VMEM((N_ACC, CHUNK, D_MODEL), F32)],
        compiler_params=_params(dimension_semantics=("arbitrary",)),
    )(proj, proj, proj, proj, dmixed, attn, probs, shares, conv_full, norm_conv, norm_attn)


def _in_bwd_rs(dproj, w_full, x, dx2, norm_in, dw_in_chip, gslab, gnf, loss_part):
    tm = 256
    steps = SEQ // tm
    relay_step = 4

    def body(dp_ref, w_ref, x_ref, dx2_ref, g_ref, dwi_ref, gs_ref, gnf_ref, lp_ref, gx_ref, gwin_ref, gsum_ref,
             gni_scr, own, ici, via, stage, myslab, slabs, send_sems, recv_sems, local_sems):
        i = pl.program_id(0)
        rs_start, rs_relay, rs_finish = _ici_sum(dwi_ref, own, ici, via, stage, send_sems, recv_sems, local_sems)
        slab_start, slab_finish = _slab_sum(myslab, slabs, send_sems, recv_sems, N_ICI_SUM_SEMS)

        @pl.when(i == 0)
        def _():
            gni_scr[...] = jnp.zeros_like(gni_scr)
            rs_start()

        dh = jnp.dot(dp_ref[...], w_ref[...], preferred_element_type=F32)
        xv = x_ref[...]
        r = lax.rsqrt(jnp.mean(xv * xv, axis=-1, keepdims=True) + RMS_EPS)
        xn = xv * r
        u = dh * g_ref[...]
        gx_ref[...] = dx2_ref[...] + r * (u - xn * jnp.mean(u * xn, axis=-1, keepdims=True))
        gni_scr[...] += jnp.sum(dh * xn, axis=0, keepdims=True)

        @pl.when(i == relay_step)
        def _():
            rs_relay()

        @pl.when(i == steps - 1)
        def _():
            row = lax.broadcasted_iota(jnp.int32, (8, D_MODEL), 0)
            lane = lax.broadcasted_iota(jnp.int32, (8, D_MODEL), 1)
            slab = jnp.where(row == ROW_NORM_IN, gni_scr[...], jnp.where(row == ROW_NORM_FINAL, gnf_ref[...], gs_ref[...]))
            myslab[...] = jnp.where((row == ROW_SINKS) & (lane == LOSS_LANE), lp_ref[0:1, 0:1], slab)
            slab_start()
            gwin_ref[...] = rs_finish()
            gsum_ref[...] = slab_finish()

    const = lambda i: (0, 0)
    return pl.pallas_call(
        body, name="in_bwd", grid=(steps,),
        in_specs=[pl.BlockSpec((tm, D_PROJ), lambda i: (i, 0)), pl.BlockSpec(memory_space=pltpu.VMEM),
                  pl.BlockSpec((tm, D_MODEL), lambda i: (i, 0)), pl.BlockSpec((tm, D_MODEL), lambda i: (i, 0)),
                  pl.BlockSpec((1, D_MODEL), const), pl.BlockSpec(memory_space=pl.ANY),
                  pl.BlockSpec((8, D_MODEL), const), pl.BlockSpec((1, D_MODEL), const), pl.BlockSpec((8, 128), const)],
        out_specs=(pl.BlockSpec((tm, D_MODEL), lambda i: (i, 0)), pl.BlockSpec((SHARD_IN, D_MODEL), const),
                   pl.BlockSpec((8, D_MODEL), const)),
        out_shape=(jax.ShapeDtypeStruct((SEQ, D_MODEL), F32), jax.ShapeDtypeStruct((SHARD_IN, D_MODEL), F32),
                   jax.ShapeDtypeStruct((8, D_MODEL), F32)),
        scratch_shapes=[pltpu.VMEM((1, D_MODEL), F32), pltpu.VMEM((SHARD_IN, D_MODEL), BF16),
                        pltpu.VMEM((2, SHARD_IN, D_MODEL), BF16), pltpu.VMEM((2, HALF_IN, D_MODEL), BF16),
                        pltpu.VMEM((2, HALF_IN, D_MODEL), BF16),
                        pltpu.VMEM((8, D_MODEL), F32), pltpu.VMEM((N_DEV, 8, D_MODEL), F32),
                        pltpu.SemaphoreType.DMA((N_ICI_SUM_SEMS + 7,)), pltpu.SemaphoreType.DMA((N_ICI_SUM_SEMS + 7,)),
                        pltpu.SemaphoreType.DMA((3,))],
        compiler_params=_params(dimension_semantics=("arbitrary",)),
    )(dproj, w_full, x, dx2, norm_in, dw_in_chip, gslab, gnf, loss_part)


def _dw_in_rs(dproj, h, dw_out_sh):
    tn = 640
    steps = D_PROJ // tn
    forward_step = 2
    half_step = (D_PROJ // 2) // tn

    def body(a_ref, b_ref, dwo_ref, chip_ref, gwo_ref, dwt, d2d_in, own, d2d, ici, send_sems, recv_sems, local_sems):
        i = pl.program_id(0)
        rs_start, rs_forward, rs_finish = _shard_sum(dwo_ref, own, d2d, ici, send_sems, recv_sems, local_sems)
        pair_send, pair_finish = _chip_sum(dwt, d2d_in, chip_ref, send_sems, recv_sems, local_sems,
                                           N_SHARD_SUM_SEMS, 4)

        @pl.when(i == 0)
        def _():
            rs_start()

        @pl.when(i == half_step)
        def _():
            pair_send(first=True)

        tile = lax.dot_general(a_ref[...], b_ref[...], _TN, preferred_element_type=F32).astype(BF16)
        dwt[pl.ds(pl.multiple_of(i * tn, tn), tn), :] = tile

        @pl.when(i == forward_step)
        def _():
            rs_forward()

        @pl.when(i == steps - 1)
        def _():
            pair_send(first=False)
            gwo_ref[...] = rs_finish()
            pair_finish()

    return pl.pallas_call(
        body, name="dw_in", grid=(steps,),
        in_specs=[pl.BlockSpec((SEQ, tn), lambda i: (0, i)), pl.BlockSpec(memory_space=pltpu.VMEM),
                  pl.BlockSpec(memory_space=pl.ANY)],
        out_specs=(pl.BlockSpec(memory_space=pl.ANY), pl.BlockSpec((SHARD_OUT, D_MODEL), lambda i: (0, 0))),
        out_shape=(jax.ShapeDtypeStruct((4, SHARD_IN, D_MODEL), BF16), jax.ShapeDtypeStruct((SHARD_OUT, D_MODEL), F32)),
        scratch_shapes=[pltpu.VMEM((D_PROJ, D_MODEL), BF16), pltpu.VMEM((4, SHARD_IN, D_MODEL), BF16),
                        *_shard_sum_scratch(SHARD_OUT),
                        pltpu.SemaphoreType.DMA((N_SHARD_SUM_SEMS + 4,)), pltpu.SemaphoreType.DMA((N_SHARD_SUM_SEMS + 4,)),
                        pltpu.SemaphoreType.DMA((8,))],
        compiler_params=_params(dimension_semantics=("arbitrary",)),
    )(dproj, h, dw_out_sh)


def _matmul_tn(a, b, tn, name):
    k, n = a.shape
    _, m = b.shape

    def body(a_ref, b_ref, o_ref):
        o_ref[...] = lax.dot_general(a_ref[...], b_ref[...], _TN, preferred_element_type=F32).astype(BF16)

    return pl.pallas_call(
        body, name=name, grid=(n // tn,),
        in_specs=[pl.BlockSpec((k, tn), lambda i: (0, i)), pl.BlockSpec(memory_space=pltpu.VMEM)],
        out_specs=pl.BlockSpec((tn, m), lambda i: (i, 0)),
        out_shape=jax.ShapeDtypeStruct((n, m), BF16),
        compiler_params=_params(dimension_semantics=("arbitrary",)),
    )(a, b)


def _adam_all(big_in, big_out, gsum, small, grad_x):
    steps = 4
    tr_in, tr_out = SHARD_IN // steps, SHARD_OUT // steps

    def body(*refs):
        ins, outs = refs[:8 + 1 + 18 + 1], refs[8 + 1 + 18 + 1:]
        i = pl.program_id(0)
        outs[33][...] = ins[27][...]
        for b in range(2):
            w_ref, g_ref, m_ref, v_ref = ins[4 * b:4 * b + 4]
            g = g_ref[...]
            delta, mn, vn = _adamw(w_ref[...], g, m_ref[...], v_ref[...])
            for ref, val in zip(outs[4 * b:4 * b + 4], (g, delta, mn, vn)):
                ref[...] = val

        @pl.when(i == 0)
        def _():
            gsum = ins[8][...]
            idx = _slot(lax.axis_index("x"), lax.axis_index("y"), lax.axis_index("c"))
            cg = jnp.zeros((3, SHARD_CONV), F32)
            for d in range(N_DEV):
                cg = jnp.where(idx == d, gsum[ROW_CONV0:ROW_CONV0 + 3, d * SHARD_CONV:(d + 1) * SHARD_CONV], cg)
            grads = (gsum[ROW_NORM_IN:ROW_NORM_IN + 1], gsum[ROW_SINKS:ROW_SINKS + 1, 0:N_Q_HEADS],
                     gsum[ROW_NORM_CONV:ROW_NORM_CONV + 1], gsum[ROW_NORM_ATTN:ROW_NORM_ATTN + 1],
                     gsum[ROW_NORM_FINAL:ROW_NORM_FINAL + 1], cg)
            for s, g in enumerate(grads):
                at = (slice(None), 0, slice(None)) if s == 5 else (slice(None), slice(None))
                w_ref, m_ref, v_ref = ins[9 + 3 * s:12 + 3 * s]
                delta, mn, vn = _adamw(w_ref[at], g, m_ref[at], v_ref[at])
                for ref, val in zip(outs[8 + 4 * s:12 + 4 * s], (g, delta, mn, vn)):
                    ref[at] = val
            outs[32][...] = gsum[ROW_SINKS:ROW_SINKS + 1, LOSS_LANE:LOSS_LANE + 1]

    const = lambda i: (0, 0)
    rows = lambda i: (i, 0)
    whole = lambda shape: pl.BlockSpec(shape, lambda i: (0,) * len(shape))
    small_shapes = [a.shape for a in small[::3]]
    in_specs = ([pl.BlockSpec((tr_in, D_MODEL), rows)] * 4 + [pl.BlockSpec((tr_out, D_MODEL), rows)] * 4
                + [pl.BlockSpec((8, D_MODEL), const)] + [whole(a.shape) for a in small]
                + [pl.BlockSpec((SEQ // steps, D_MODEL), rows)])
    out_specs = ([pl.BlockSpec((tr_in, D_MODEL), rows)] * 4 + [pl.BlockSpec((tr_out, D_MODEL), rows)] * 4
                 + [whole(s) for s in small_shapes for _ in range(4)] + [pl.BlockSpec((1, 1), const)]
                 + [pl.BlockSpec((SEQ // steps, D_MODEL), rows)])
    out_shape = ([jax.ShapeDtypeStruct((SHARD_IN, D_MODEL), F32)] * 4 + [jax.ShapeDtypeStruct((SHARD_OUT, D_MODEL), F32)] * 4
                 + [jax.ShapeDtypeStruct(s, F32) for s in small_shapes for _ in range(4)]
                 + [jax.ShapeDtypeStruct((1, 1), F32), jax.ShapeDtypeStruct((SEQ, D_MODEL), F32)])
    outs = pl.pallas_call(
        body, name="adam", grid=(steps,), in_specs=in_specs, out_specs=tuple(out_specs), out_shape=tuple(out_shape),
        compiler_params=_params(dimension_semantics=("arbitrary",)),
    )(*big_in, *big_out, gsum, *small, grad_x)
    return outs[0:4], outs[4:8], [outs[8 + 4 * s:12 + 4 * s] for s in range(6)], outs[32], outs[33]


def _rows_first(a):
    return jnp.transpose(a, (1, 0, 2))


def kernel(x, norm_in, w_in, conv_w, attn_sinks, norm_conv_out, norm_attn_out, w_out, norm_final, loss_target, m_norm_in, m_w_in, m_conv_w, m_attn_sinks, m_norm_conv_out, m_norm_attn_out, m_w_out, m_norm_final, v_norm_in, v_w_in, v_conv_w, v_attn_sinks, v_norm_conv_out, v_norm_attn_out, v_w_out, v_norm_final):
    x2d = x.reshape(SEQ, D_MODEL)
    target = loss_target.reshape(SEQ, D_MODEL)
    nf = norm_final.reshape(1, D_MODEL)

    w_in_t, m_w_in_t, v_w_in_t = w_in[0].T, m_w_in[0].T, v_w_in[0].T
    tiles = jnp.asarray(TILE_ORDER, jnp.int32)[2 * lax.axis_index("x") + lax.axis_index("y")]
    w_in_full, h, proj, g_out, conv_full = _gather_in_proj(x2d, norm_in, w_in_t, w_out[0], _rows_first(conv_w), tiles)
    sinks = attn_sinks.reshape(N_Q_HEADS)

    mixed, attn, probs, shares = _mix_fwd(proj, conv_full, sinks, norm_conv_out, norm_attn_out)
    dx2, dx2b, dmixed, gnf, loss_part = _out_proj_loss(mixed, x2d, target, g_out.reshape(D_MIX, D_MODEL), nf)
    dproj, gslab = _mix_bwd(proj, dmixed, attn, probs, shares, conv_full, norm_conv_out, norm_attn_out)
    dw_out = _matmul_tn(mixed, dx2b, 512, "dw_out")
    dw_in_chip, g_w_out = _dw_in_rs(dproj, h, dw_out.reshape(N_DEV, SHARD_OUT, D_MODEL))
    grad_x, g_w_in, gsum = _in_bwd_rs(dproj, w_in_full, x2d, dx2, norm_in, dw_in_chip, gslab, gnf, loss_part)

    small = (norm_in, m_norm_in, v_norm_in, attn_sinks, m_attn_sinks, v_attn_sinks,
             norm_conv_out, m_norm_conv_out, v_norm_conv_out, norm_attn_out, m_norm_attn_out, v_norm_attn_out,
             nf, m_norm_final.reshape(1, D_MODEL), v_norm_final.reshape(1, D_MODEL),
             _rows_first(conv_w), _rows_first(m_conv_w), _rows_first(v_conv_w))
    big_in, big_out, (s_ni, s_sk, s_nc, s_na, s_nf, s_cv), loss, grad_x = _adam_all(
        (w_in_t, g_w_in, m_w_in_t, v_w_in_t), (w_out[0], g_w_out, m_w_out[0], v_w_out[0]), gsum, small, grad_x)

    def leaves(k):
        return (s_ni[k], big_in[k].T[None], jnp.transpose(s_cv[k], (1, 0, 2)), s_sk[k], s_nc[k], s_na[k], big_out[k][None],
                s_nf[k].reshape(D_MODEL))

    return (loss.reshape(()), grad_x.reshape(1, SEQ, D_MODEL), *leaves(0), *leaves(1), *leaves(2), *leaves(3))
```

```python
import jax
import jax.numpy as jnp
from jax import lax
from jax.experimental import pallas as pl
from jax.experimental.pallas import tpu as pltpu

F32 = jnp.float32
BF16 = jnp.bfloat16
MESH = pl.DeviceIdType.MESH

N_DEV = 8
SEQ = 2048
D_MODEL = 1024
D_CONV = 1024
D_ATTN = 1024
D_KV = 128
HEAD_DIM = 64
N_Q_HEADS = 16
N_PAIRS = N_Q_HEADS // 2
PAIRS_PER_KV = N_PAIRS // 2
D_MIX = D_CONV + D_ATTN
D_PROJ = 6400
SHARD_IN = D_PROJ // N_DEV
SHARD_OUT = D_MIX // N_DEV
SHARD_CONV = D_CONV // N_DEV
OFF_CB, OFF_CC, OFF_CU, OFF_GC, OFF_Q, OFF_K, OFF_V, OFF_GA = 0, 1024, 2048, 3072, 4096, 5120, 5248, 5376
BLOCK = 128
N_BLOCKS = SEQ // BLOCK
HALO = 8
CHUNK = 16
N_CHUNKS = BLOCK // CHUNK
RMS_EPS = 1e-5
NEG = -1e30
SCALE = HEAD_DIM ** -0.5
SLOPES = tuple(2.0 ** (-8.0 * (h + 1) / N_Q_HEADS) for h in range(N_Q_HEADS))

ADAM_LR = 0.001
ADAM_B1 = 0.9
ADAM_B2 = 0.999
ADAM_EPS = 1e-08
ADAM_WD = 0.01
ADAM_STEP = 10

ROW_NORM_IN, ROW_NORM_CONV, ROW_NORM_ATTN, ROW_NORM_FINAL, ROW_CONV0, ROW_SINKS = 0, 1, 2, 3, 4, 7
LOSS_LANE = N_Q_HEADS
ACC_NORM_CONV, ACC_NORM_ATTN, ACC_CONV0, N_ACC = 0, 1, 2, 5

VMEM_LIMIT = 56 * 1024 * 1024

_NT = (((1,), (1,)), ((), ()))
_TN = (((0,), (0,)), ((), ()))


def _params(**kw):
    return pltpu.CompilerParams(vmem_limit_bytes=VMEM_LIMIT, **kw)


def _adamw(w, g, m, v):
    m = ADAM_B1 * m + (1.0 - ADAM_B1) * g
    v = ADAM_B2 * v + (1.0 - ADAM_B2) * (g * g)
    m_hat = m / (1.0 - ADAM_B1 ** ADAM_STEP)
    v_hat = v / (1.0 - ADAM_B2 ** ADAM_STEP)
    delta = -ADAM_LR * (m_hat / (jnp.sqrt(v_hat) + ADAM_EPS) + ADAM_WD * w)
    return delta, m, v


def _sigmoid(t):
    return 1.0 / (1.0 + jnp.exp(-t))


def _slot(px, py, pc):
    return 4 * px + 2 * py + pc


HALF_IN = SHARD_IN // 2
N_GATHER_KINDS = 13


IN_PROJ_TILE = 640
TILE_ORDER = ((0, 1, 2, 3, 4, 5, 6, 7, 8, 9), (3, 4, 0, 1, 2, 8, 9, 5, 6, 7),
              (5, 6, 0, 1, 7, 8, 9, 2, 3, 4), (8, 9, 3, 4, 5, 6, 7, 0, 1, 2))
TILES_OWN, TILES_NEIGHBOURS = 2, 7


def _gather_in_proj(x, norm_in, w_in_sh, w_out_sh, conv_sh, tiles):
    tn = IN_PROJ_TILE
    steps = D_PROJ // tn
    tm = 256

    def body(tiles_ref, x_ref, g_ref, win_ref, wout_ref, cv_ref, wt_ref, h_ref, proj_ref, gout_ref, conv_ref,
             gin_ref, gcv_ref, wob_ref, send_sems, recv_sems, local_sems):
        p = pl.program_id(0)
        wout_start, wout_forward, wout_finish = _wout_gather(wob_ref, gout_ref, send_sems, recv_sems, local_sems.at[1],
                                                             N_GATHER_KINDS + 7)
        local_sem = local_sems.at[0]
        x, y, c = lax.axis_index("x"), lax.axis_index("y"), lax.axis_index("c")
        me, sibling = (x, y, c), (x, y, 1 - c)
        nx, ny, dg = (1 - x, y, c), (x, 1 - y, c), (1 - x, 1 - y, c)

        def other(dev):
            return (dev[0], dev[1], 1 - dev[2])

        def shard(dev):
            return gin_ref.at[pl.ds(pl.multiple_of(_slot(*dev) * SHARD_IN, 16), SHARD_IN), :]

        def half(dev, h):
            return gin_ref.at[pl.ds(pl.multiple_of(_slot(*dev) * SHARD_IN + h * HALF_IN, 16), HALF_IN), :]

        def rc(ref, k, to):
            return pltpu.make_async_remote_copy(src_ref=ref, dst_ref=ref, send_sem=send_sems.at[k],
                                                recv_sem=recv_sems.at[k], device_id=to, device_id_type=MESH)

        def cv(k, dev, to):
            s = _slot(*dev)
            return pltpu.make_async_remote_copy(src_ref=gcv_ref.at[s], dst_ref=gcv_ref.at[s],
                                                send_sem=send_sems.at[N_GATHER_KINDS + k],
                                                recv_sem=recv_sems.at[N_GATHER_KINDS + k], device_id=to, device_id_type=MESH)

        def own_copies():
            return [rc(shard(me), 0, sibling),
                    rc(half(me, 0), 1, nx), rc(half(me, 1), 2, nx),
                    rc(half(me, 1), 4, ny), rc(half(me, 0), 3, ny),
                    cv(0, me, sibling)] + [cv(1 + j, me, peer) for j, peer in enumerate((nx, ny, dg))]

        def pass_on(dev, h, k_in, k_ici, k_d2d):
            rc(half(dev, h), k_in, me).wait_recv()
            if k_ici is not None:
                rc(half(dev, h), k_ici, ny if dev is nx else nx).start()
            rc(half(dev, h), k_d2d, sibling).start()

        @pl.when(p == 0)
        def _():
            gin_ref[pl.ds(pl.multiple_of(_slot(*me) * SHARD_IN, 16), SHARD_IN), :] = win_ref[...].astype(BF16)
            gcv_ref[_slot(*me)] = jnp.zeros((8, SHARD_CONV), F32)
            gcv_ref[_slot(*me), 0:3, :] = cv_ref[:, 0, :]
            for cp in own_copies():
                cp.start()
            wob_ref[...] = wout_ref[...].astype(BF16)
            for t in range(SEQ // tm):
                xv = x_ref[tm * t:tm * (t + 1), :]
                r = lax.rsqrt(jnp.mean(xv * xv, axis=-1, keepdims=True) + RMS_EPS)
                h_ref[tm * t:tm * (t + 1), :] = (xv * r * g_ref[...]).astype(BF16)
            rc(shard(sibling), 0, me).wait_recv()

        @pl.when(p == TILES_OWN)
        def _():
            for args in ((nx, 0, 1, 5, 7), (ny, 1, 4, 6, 10), (nx, 1, 2, None, 8), (ny, 0, 3, None, 9)):
                pass_on(*args)
            for j, peer in enumerate((nx, ny, dg)):
                cv(1 + j, peer, me).wait_recv()
                cv(4 + j, peer, sibling).start()
            for (dev, h), k in (((nx, 0), 7), ((nx, 1), 8), ((ny, 0), 9), ((ny, 1), 10)):
                rc(half(other(dev), h), k, me).wait_recv()
            wout_start()

        @pl.when(p == TILES_NEIGHBOURS)
        def _():
            pass_on(dg, 0, 5, None, 11)
            pass_on(dg, 1, 6, None, 12)
            for (dev, h), k in (((dg, 0), 11), ((dg, 1), 12)):
                rc(half(other(dev), h), k, me).wait_recv()
            pltpu.make_async_copy(gin_ref, wt_ref, local_sem).start()

        @pl.when(p == steps - 1)
        def _():
            wout_forward()

        w = gin_ref[pl.ds(pl.multiple_of(tiles_ref[p] * tn, tn), tn), :]
        proj_ref[...] = lax.dot_general(h_ref[...], w, _NT, preferred_element_type=F32)

        @pl.when(p == steps - 1)
        def _():
            cv(0, sibling, me).wait_recv()
            for j, peer in enumerate((nx, ny, dg)):
                cv(4 + j, other(peer), me).wait_recv()
            for d in range(N_DEV):
                conv_ref[:, d * SHARD_CONV:(d + 1) * SHARD_CONV] = gcv_ref[d]
            relayed = [rc(half(nx, 0), 5, ny), rc(half(ny, 1), 6, nx)]
            relayed += [rc(half(dev, h), k, sibling) for (dev, h), k in
                        (((nx, 0), 7), ((nx, 1), 8), ((ny, 0), 9), ((ny, 1), 10), ((dg, 0), 11), ((dg, 1), 12))]
            relayed += [cv(4 + j, peer, sibling) for j, peer in enumerate((nx, ny, dg))]
            for cp in own_copies() + relayed:
                cp.wait_send()
            pltpu.make_async_copy(gin_ref, wt_ref, local_sem).wait()
            wout_finish()

    vmem = pl.BlockSpec(memory_space=pltpu.VMEM)
    grid_spec = pltpu.PrefetchScalarGridSpec(
        num_scalar_prefetch=1, grid=(steps,),
        in_specs=[vmem, vmem, vmem, vmem, vmem],
        out_specs=(pl.BlockSpec(memory_space=pl.ANY), vmem,
                   pl.BlockSpec((SEQ, tn), lambda p, tiles_ref: (0, tiles_ref[p])), pl.BlockSpec(memory_space=pl.ANY), vmem),
        scratch_shapes=[pltpu.VMEM((D_PROJ, D_MODEL), BF16), pltpu.VMEM((N_DEV, 8, SHARD_CONV), F32),
                        pltpu.VMEM((SHARD_OUT, D_MODEL), BF16),
                        pltpu.SemaphoreType.DMA((N_GATHER_KINDS + 14,)), pltpu.SemaphoreType.DMA((N_GATHER_KINDS + 14,)),
                        pltpu.SemaphoreType.DMA((2,))])
    return pl.pallas_call(
        body, name="gather_in_proj", grid_spec=grid_spec,
        out_shape=(jax.ShapeDtypeStruct((D_PROJ, D_MODEL), BF16), jax.ShapeDtypeStruct((SEQ, D_MODEL), BF16),
                   jax.ShapeDtypeStruct((SEQ, D_PROJ), F32), jax.ShapeDtypeStruct((N_DEV, SHARD_OUT, D_MODEL), BF16),
                   jax.ShapeDtypeStruct((8, D_CONV), F32)),
        compiler_params=_params(dimension_semantics=("arbitrary",)),
    )(tiles, x, norm_in, w_in_sh, w_out_sh, conv_sh)


def _wout_gather(wo_ref, gout_ref, send_sems, recv_sems, local_sem, base=0):
    x, y, c = lax.axis_index("x"), lax.axis_index("y"), lax.axis_index("c")
    me, sibling = (x, y, c), (x, y, 1 - c)
    chips = [(1 - x, y), (x, 1 - y), (1 - x, 1 - y)]

    def copy(k, block, to, src=None):
        rows = gout_ref.at[_slot(*block)]
        return pltpu.make_async_remote_copy(src_ref=rows if src is None else src, dst_ref=rows,
                                            send_sem=send_sems.at[base + k], recv_sem=recv_sems.at[base + k],
                                            device_id=to, device_id_type=MESH)

    def mine():
        return pltpu.make_async_copy(wo_ref, gout_ref.at[_slot(*me)], local_sem)

    def start():
        mine().start()
        copy(0, me, sibling, src=wo_ref).start()
        for j, chip in enumerate(chips):
            copy(1 + j, me, (*chip, c), src=wo_ref).start()

    def forward():
        for j, chip in enumerate(chips):
            copy(1 + j, (*chip, c), me).wait_recv()
            copy(4 + j, (*chip, c), sibling).start()

    def finish():
        copy(0, sibling, me).wait_recv()
        for j, chip in enumerate(chips):
            copy(4 + j, (*chip, 1 - c), me).wait_recv()
        copy(0, me, sibling, src=wo_ref).wait_send()
        for j, chip in enumerate(chips):
            copy(1 + j, me, (*chip, c), src=wo_ref).wait_send()
            copy(4 + j, (*chip, c), sibling).wait_send()
        mine().wait()

    return start, forward, finish


def _shard_sum(src, own, d2d, ici, send_sems, recv_sems, local_sems, base=0):
    x, y, c = lax.axis_index("x"), lax.axis_index("y"), lax.axis_index("c")
    sibling = (x, y, 1 - c)
    chips = [(x, y), (1 - x, y), (x, 1 - y), (1 - x, 1 - y)]

    def rcopy(s, d, k, to):
        return pltpu.make_async_remote_copy(src_ref=s, dst_ref=d, send_sem=send_sems.at[base + k],
                                            recv_sem=recv_sems.at[base + k], device_id=to, device_id_type=MESH)

    def mine(k):
        return pltpu.make_async_copy(src.at[_slot(*chips[k], c)], own.at[k], local_sems.at[k])

    def to_sibling(k):
        return rcopy(src.at[_slot(*chips[k], 1 - c)], d2d.at[k], k, sibling)

    def to_chip(k):
        return rcopy(own.at[k], ici.at[k - 1], 3 + k, (*chips[k], c))

    def start():
        for k in range(4):
            mine(k).start()
            to_sibling(k).start()

    def forward():
        for k in range(1, 4):
            mine(k).wait()
            to_sibling(k).wait_recv()
            own[k] = (own[k].astype(F32) + d2d[k].astype(F32)).astype(BF16)
            to_chip(k).start()

    def finish():
        mine(0).wait()
        to_sibling(0).wait_recv()
        acc = own[0].astype(F32) + d2d[0].astype(F32)
        for k in range(1, 4):
            to_chip(k).wait_recv()
            acc = acc + ici[k - 1].astype(F32)
        for k in range(4):
            to_sibling(k).wait_send()
        for k in range(1, 4):
            to_chip(k).wait_send()
        return acc

    return start, forward, finish


def _shard_sum_scratch(rows):
    return [pltpu.VMEM((4, rows, D_MODEL), BF16), pltpu.VMEM((4, rows, D_MODEL), BF16),
            pltpu.VMEM((3, rows, D_MODEL), BF16)]


N_SHARD_SUM_SEMS = 7


def _chip_sum(dwt, d2d, out_hbm, send_sems, recv_sems, local_sems, base, local_base):
    x, y, c = lax.axis_index("x"), lax.axis_index("y"), lax.axis_index("c")
    sibling = (x, y, 1 - c)
    chips = [(x, y), (1 - x, y), (x, 1 - y), (1 - x, 1 - y)]

    def shard(s):
        return dwt.at[pl.ds(pl.multiple_of(s * SHARD_IN, 16), SHARD_IN), :]

    def to_sibling(k):
        return pltpu.make_async_remote_copy(src_ref=shard(_slot(*chips[k], 1 - c)), dst_ref=d2d.at[k],
                                            send_sem=send_sems.at[base + k], recv_sem=recv_sems.at[base + k],
                                            device_id=sibling, device_id_type=MESH)

    def save(k):
        return pltpu.make_async_copy(d2d.at[k], out_hbm.at[k], local_sems.at[local_base + k])

    def send(rows_before, rows_done):
        for k in range(4):
            end = (_slot(*chips[k], 1 - c) + 1) * SHARD_IN

            @pl.when((end > rows_before) & (end <= rows_done))
            def _():
                to_sibling(k).start()

    def finish():
        for k in range(4):
            to_sibling(k).wait_recv()
            d2d[k] = (shard(_slot(*chips[k], c))[...].astype(F32) + d2d[k].astype(F32)).astype(BF16)
            save(k).start()
        for k in range(4):
            save(k).wait()
            to_sibling(k).wait_send()

    return send, finish


N_ICI_SUM_SEMS = 6


def _ici_sum(src, own, ici, via, stage, send_sems, recv_sems, local_sems, base=0):
    x, y, c = lax.axis_index("x"), lax.axis_index("y"), lax.axis_index("c")
    nx, ny = (1 - x, y, c), (x, 1 - y, c)
    OWN, NX, NY, DG = range(4)

    def half(ref, h):
        return ref.at[pl.ds(h * HALF_IN, HALF_IN), :]

    def rc(s, d, k, to):
        return pltpu.make_async_remote_copy(src_ref=s, dst_ref=d, send_sem=send_sems.at[base + k],
                                            recv_sem=recv_sems.at[base + k], device_id=to, device_id_type=MESH)

    for_dg_0 = lambda: rc(half(src.at[DG], 0), via.at[0], 0, nx)
    for_dg_1 = lambda: rc(half(src.at[DG], 1), via.at[1], 1, ny)
    for_nx_0 = lambda: rc(half(src.at[NX], 0), half(ici.at[0], 0), 2, nx)
    for_ny_1 = lambda: rc(half(src.at[NY], 1), half(ici.at[1], 1), 3, ny)
    for_ny_0 = lambda: rc(stage.at[0], half(ici.at[1], 0), 4, ny)
    for_nx_1 = lambda: rc(stage.at[1], half(ici.at[0], 1), 5, nx)
    mine = lambda: pltpu.make_async_copy(src.at[OWN], own, local_sems.at[0])
    stage_0 = lambda: pltpu.make_async_copy(half(src.at[NY], 0), stage.at[0], local_sems.at[1])
    stage_1 = lambda: pltpu.make_async_copy(half(src.at[NX], 1), stage.at[1], local_sems.at[2])

    def start():
        for cp in (for_dg_0, for_dg_1, for_nx_0, for_ny_1, stage_0, stage_1, mine):
            cp().start()

    def relay():
        for h, staged, landed, out in ((0, stage_0, for_dg_0, for_ny_0), (1, stage_1, for_dg_1, for_nx_1)):
            staged().wait()
            landed().wait_recv()
            stage[h] = (stage[h].astype(F32) + via[h].astype(F32)).astype(BF16)
            out().start()

    def finish():
        mine().wait()
        for cp in (for_nx_0, for_nx_1, for_ny_1, for_ny_0):
            cp().wait_recv()
        acc = own[...].astype(F32) + ici[0].astype(F32) + ici[1].astype(F32)
        for cp in (for_dg_0, for_dg_1, for_nx_0, for_ny_1, for_ny_0, for_nx_1):
            cp().wait_send()
        return acc

    return start, relay, finish


def _slab_sum(myslab, slabs, send_sems, recv_sems, base):
    x, y, c = lax.axis_index("x"), lax.axis_index("y"), lax.axis_index("c")
    me = _slot(x, y, c)
    peers = [(x, y, 1 - c), (1 - x, y, c), (x, 1 - y, c), (1 - x, 1 - y, c),
             (1 - x, y, 1 - c), (x, 1 - y, 1 - c), (1 - x, 1 - y, 1 - c)]

    def cp(k):
        return pltpu.make_async_remote_copy(src_ref=myslab, dst_ref=slabs.at[me], send_sem=send_sems.at[base + k],
                                            recv_sem=recv_sems.at[base + k], device_id=peers[k], device_id_type=MESH)

    def start():
        slabs[me] = myslab[...]
        for k in range(7):
            cp(k).start()

    def finish():
        for k in range(7):
            cp(k).wait_recv()
        total = slabs[0]
        for d in range(1, N_DEV):
            total = total + slabs[d]
        for k in range(7):
            cp(k).wait_send()
        return total

    return start, finish


def _chunk_rows(r):
    return slice(r * CHUNK, (r + 1) * CHUNK)


def _conv_halo(cch_ref, cuh_ref, n):
    zh = jnp.where(n > 0, cch_ref[...] * cuh_ref[...], 0.0)
    return jnp.concatenate([zh] * (CHUNK // HALO), axis=0)


def _conv_chunk(pj_ref, zhalo, cw, r):
    rows = _chunk_rows(r)
    cc = pj_ref[rows, OFF_CC:OFF_CC + D_CONV]
    cu = pj_ref[rows, OFF_CU:OFF_CU + D_CONV]
    z = cc * cu
    before = _chunk_rows(r - 1)
    zprev = pj_ref[before, OFF_CC:OFF_CC + D_CONV] * pj_ref[before, OFF_CU:OFF_CU + D_CONV] if r > 0 else zhalo
    row = lax.broadcasted_iota(jnp.int32, (CHUNK, D_CONV), 0)
    z1 = jnp.where(row < 1, pltpu.roll(zprev, 1, 0), pltpu.roll(z, 1, 0))
    z2 = jnp.where(row < 2, pltpu.roll(zprev, 2, 0), pltpu.roll(z, 2, 0))
    co = cw[0] * z2 + cw[1] * z1 + cw[2] * z
    return cc, cu, z, z1, z2, co


def _gated_norm(a, gain, t):
    r = lax.rsqrt(jnp.mean(a * a, axis=-1, keepdims=True) + RMS_EPS)
    return a * r * gain * (t * _sigmoid(t))


def _kv_bands(pj, kvp_ref):
    lane = lax.broadcasted_iota(jnp.int32, (2 * BLOCK, D_KV), 1)
    lo = lane < HEAD_DIM

    def bands(prev, cur):
        b = jnp.concatenate([prev, cur], axis=0)
        br = pltpu.roll(b, HEAD_DIM, 1)
        zero = jnp.zeros_like(b)
        return ((jnp.where(lo, b, zero).astype(BF16), jnp.where(lo, zero, br).astype(BF16)),
                (jnp.where(lo, br, zero).astype(BF16), jnp.where(lo, zero, b).astype(BF16)))

    ks = bands(kvp_ref[:, 0:D_KV], pj[:, OFF_K:OFF_K + D_KV])
    vs = bands(kvp_ref[:, D_KV:2 * D_KV], pj[:, OFF_V:OFF_V + D_KV])
    return ks, vs


STACK = PAIRS_PER_KV * BLOCK


def _head(j, i, e):
    return 2 * (PAIRS_PER_KV * j + i) + e


def _pair_cols(j, i, off):
    p = PAIRS_PER_KV * j + i
    return slice(off + 128 * p, off + 128 * (p + 1))


def _fill_attn_bias(bias_scr, first_block):
    qi = lax.broadcasted_iota(jnp.int32, (BLOCK, 2 * BLOCK), 0)
    kj = lax.broadcasted_iota(jnp.int32, (BLOCK, 2 * BLOCK), 1)
    dist = BLOCK + qi - kj
    valid = (dist >= 0) & (dist < BLOCK)
    if first_block:
        valid = valid & (kj >= BLOCK)
    distf = dist.astype(F32)
    for j in range(2):
        for e in range(2):
            for i in range(PAIRS_PER_KV):
                bias_scr[2 * j + e, BLOCK * i:BLOCK * (i + 1), :] = jnp.where(valid, -SLOPES[_head(j, i, e)] * distf, NEG)


def _q_stack(pj, j):
    return jnp.concatenate([(pj[:, _pair_cols(j, i, OFF_Q)] * SCALE).astype(BF16) for i in range(PAIRS_PER_KV)], axis=0)


def _attn_probs(q_stack, kband, bias_ref, sinks):
    s = lax.dot_general(q_stack, kband, _NT, preferred_element_type=F32)
    ones = jnp.ones((128, 128), BF16)
    probs, shares = [], []
    for i, sink in enumerate(sinks):
        rows = slice(BLOCK * i, BLOCK * (i + 1))
        t = s[rows, :] + bias_ref[rows, :]
        m = jnp.broadcast_to(jnp.max(t, axis=-1, keepdims=True), (BLOCK, 128))
        m = jnp.maximum(m, sink)
        p = [jnp.exp(t[:, :128] - m), jnp.exp(t[:, 128:] - m)]
        es = jnp.exp(sink - m)
        total = (jnp.dot(p[0].astype(BF16), ones, preferred_element_type=F32)
                 + jnp.dot(p[1].astype(BF16), ones, preferred_element_type=F32))
        inv = 1.0 / (total + es)
        probs.append(jnp.concatenate([p[0] * inv, p[1] * inv], axis=1))
        shares.append(es * inv)
    return jnp.concatenate(probs, axis=0), jnp.concatenate(shares, axis=0)


def _attn_group(pj, ks, vs, bias_scr, sink_ref, j):
    q_stack = _q_stack(pj, j)
    out, probs, shares = None, [], []
    for e in range(2):
        p, ps = _attn_probs(q_stack, ks[j][e], bias_scr.at[2 * j + e],
                            [sink_ref[_head(j, i, e)] for i in range(PAIRS_PER_KV)])
        p = p.astype(BF16)
        o = jnp.dot(p, vs[j][e], preferred_element_type=F32)
        out = o if out is None else out + o
        probs.append(p)
        shares.append(ps)
    return out, probs, shares


def _mix_fwd(proj, conv_full, sinks, norm_conv, norm_attn):
    def body(pj_ref, kvp_ref, cch_ref, cuh_ref, cw_ref, sink_ref, gc_ref, ga_ref,
             mixed_ref, attn_scr, p_ref, ps_ref, bias_scr):
        n = pl.program_id(0)
        pj = pj_ref

        @pl.when(n == 0)
        def _():
            _fill_attn_bias(bias_scr, first_block=True)

        @pl.when(n == 1)
        def _():
            _fill_attn_bias(bias_scr, first_block=False)

        zhalo = _conv_halo(cch_ref, cuh_ref, n)
        cw = (cw_ref[0:1, :], cw_ref[1:2, :], cw_ref[2:3, :])
        gain_c = gc_ref[...]

        for r in range(N_CHUNKS):
            rows = _chunk_rows(r)
            co = _conv_chunk(pj_ref, zhalo, cw, r)[-1]
            y = _gated_norm(pj_ref[rows, OFF_CB:OFF_CB + D_CONV] * co, gain_c, pj_ref[rows, OFF_GC:OFF_GC + D_CONV])
            mixed_ref[rows, 0:D_CONV] = y.astype(BF16)

        ks, vs = _kv_bands(pj, kvp_ref)
        for j in range(2):
            out, probs, shares = _attn_group(pj, ks, vs, bias_scr, sink_ref, j)
            for e in range(2):
                p_ref[0, 2 * j + e] = probs[e]
                ps_ref[0, 2 * j + e] = shares[e]
            for i in range(PAIRS_PER_KV):
                attn_scr[:, _pair_cols(j, i, 0)] = out[BLOCK * i:BLOCK * (i + 1), :]
        gain_a = ga_ref[...]

        for r in range(N_CHUNKS):
            rows = _chunk_rows(r)
            y = _gated_norm(attn_scr[rows, :], gain_a, pj_ref[rows, OFF_GA:OFF_GA + D_ATTN])
            mixed_ref[rows, D_CONV:D_MIX] = y.astype(BF16)

    per_block = BLOCK // HALO
    return pl.pallas_call(
        body, name="mix_fwd", grid=(N_BLOCKS,),
        in_specs=[
            pl.BlockSpec((BLOCK, D_PROJ), lambda n: (n, 0)),
            pl.BlockSpec((BLOCK, 2 * D_KV), lambda n: (jnp.maximum(n - 1, 0), OFF_K // (2 * D_KV))),
            pl.BlockSpec((HALO, D_CONV), lambda n: (jnp.maximum(n * per_block - 1, 0), OFF_CC // D_CONV)),
            pl.BlockSpec((HALO, D_CONV), lambda n: (jnp.maximum(n * per_block - 1, 0), OFF_CU // D_CONV)),
            pl.BlockSpec((8, D_CONV), lambda n: (0, 0)),
            pl.BlockSpec(memory_space=pltpu.SMEM),
            pl.BlockSpec((1, D_CONV), lambda n: (0, 0)),
            pl.BlockSpec((1, D_ATTN), lambda n: (0, 0)),
        ],
        out_specs=(pl.BlockSpec((BLOCK, D_MIX), lambda n: (n, 0)), pl.BlockSpec((BLOCK, D_ATTN), lambda n: (n, 0)),
                   pl.BlockSpec((1, 4, STACK, 2 * BLOCK), lambda n: (n, 0, 0, 0)),
                   pl.BlockSpec((1, 4, STACK, 128), lambda n: (n, 0, 0, 0))),
        out_shape=(jax.ShapeDtypeStruct((SEQ, D_MIX), BF16), jax.ShapeDtypeStruct((SEQ, D_ATTN), F32),
                   jax.ShapeDtypeStruct((N_BLOCKS, 4, STACK, 2 * BLOCK), BF16),
                   jax.ShapeDtypeStruct((N_BLOCKS, 4, STACK, 128), F32)),
        scratch_shapes=[pltpu.VMEM((4, STACK, 2 * BLOCK), F32)],
        compiler_params=_params(dimension_semantics=("arbitrary",)),
    )(proj, proj, proj, proj, conv_full, sinks, norm_conv, norm_attn)


def _out_proj_loss(mixed, x, target, w_out_full, norm_final):
    tm = 256

    def body(mx_ref, x_ref, t_ref, w_ref, g_ref, dx2_ref, dx2b_ref, dmix_ref, gnf_ref, loss_ref):
        i = pl.program_id(0)
        w = w_ref[...]
        x2 = x_ref[...] + jnp.dot(mx_ref[...], w, preferred_element_type=F32)
        r = lax.rsqrt(jnp.mean(x2 * x2, axis=-1, keepdims=True) + RMS_EPS)
        xn = x2 * r
        g = g_ref[...]
        err = xn * g - t_ref[...]
        part = 0.5 * jnp.sum(jnp.mean(err * err, axis=-1, keepdims=True), axis=0, keepdims=True)
        dy = err * (1.0 / D_MODEL)
        gnf = jnp.sum(dy * xn, axis=0, keepdims=True)
        u = dy * g
        dx2 = r * (u - xn * jnp.mean(u * xn, axis=-1, keepdims=True))
        dx2_ref[...] = dx2
        dx2b = dx2.astype(BF16)
        dx2b_ref[...] = dx2b
        dmix_ref[...] = lax.dot_general(dx2b, w, _NT, preferred_element_type=F32)

        @pl.when(i == 0)
        def _():
            gnf_ref[...] = jnp.zeros_like(gnf_ref)
            loss_ref[...] = jnp.zeros_like(loss_ref)

        gnf_ref[...] += gnf
        loss_ref[...] += jnp.broadcast_to(part, loss_ref.shape)

    return pl.pallas_call(
        body, name="out_proj_loss", grid=(SEQ // tm,),
        in_specs=[pl.BlockSpec((tm, D_MIX), lambda i: (i, 0)), pl.BlockSpec((tm, D_MODEL), lambda i: (i, 0)),
                  pl.BlockSpec((tm, D_MODEL), lambda i: (i, 0)), pl.BlockSpec(memory_space=pltpu.VMEM),
                  pl.BlockSpec((1, D_MODEL), lambda i: (0, 0))],
        out_specs=(pl.BlockSpec((tm, D_MODEL), lambda i: (i, 0)), pl.BlockSpec((tm, D_MODEL), lambda i: (i, 0)),
                   pl.BlockSpec((tm, D_MIX), lambda i: (i, 0)),
                   pl.BlockSpec((1, D_MODEL), lambda i: (0, 0)), pl.BlockSpec((8, 128), lambda i: (0, 0))),
        out_shape=(jax.ShapeDtypeStruct((SEQ, D_MODEL), F32), jax.ShapeDtypeStruct((SEQ, D_MODEL), BF16),
                   jax.ShapeDtypeStruct((SEQ, D_MIX), F32),
                   jax.ShapeDtypeStruct((1, D_MODEL), F32), jax.ShapeDtypeStruct((8, 128), F32)),
        compiler_params=_params(dimension_semantics=("arbitrary",)),
    )(mixed, x, target, w_out_full, norm_final)


def _gated_norm_bwd(a, gain, t, dy):
    r = lax.rsqrt(jnp.mean(a * a, axis=-1, keepdims=True) + RMS_EPS)
    an = a * r
    sg = _sigmoid(t)
    dn = dy * (t * sg)
    dt = dy * (an * gain) * (sg * (1.0 + t * (1.0 - sg)))
    u = dn * gain
    da = r * (u - an * jnp.mean(u * an, axis=-1, keepdims=True))
    return da, dt, dn * an


def _mix_bwd(proj, dmixed, attn, probs, shares, conv_full, norm_conv, norm_attn):
    def body(pj_ref, kvp_ref, cch_ref, cuh_ref, dmx_ref, attn_ref, p_ref, ps_ref, cw_ref, gc_ref, ga_ref,
             dpj_ref, gslab_ref, dattn_scr, nxt_scr, dkv_scr, acc_scr):
        step = pl.program_id(0)
        n = N_BLOCKS - 1 - step
        pj = pj_ref

        @pl.when(step == 0)
        def _():
            gslab_ref[...] = jnp.zeros_like(gslab_ref)
            nxt_scr[...] = jnp.zeros_like(nxt_scr)
            dkv_scr[...] = jnp.zeros_like(dkv_scr)
            acc_scr[...] = jnp.zeros_like(acc_scr)

        zhalo = _conv_halo(cch_ref, cuh_ref, n)
        cw = (cw_ref[0:1, :], cw_ref[1:2, :], cw_ref[2:3, :])
        gain_c = gc_ref[...]
        row = lax.broadcasted_iota(jnp.int32, (CHUNK, D_CONV), 0)

        dco_after = nxt_scr[...]
        for r in reversed(range(N_CHUNKS)):
            rows = _chunk_rows(r)
            cc, cu, z, z1, z2, co = _conv_chunk(pj_ref, zhalo, cw, r)
            cb = pj_ref[rows, OFF_CB:OFF_CB + D_CONV]
            da, dgate, gterm = _gated_norm_bwd(cb * co, gain_c, pj_ref[rows, OFF_GC:OFF_GC + D_CONV],
                                               dmx_ref[rows, 0:D_CONV])
            dpj_ref[rows, OFF_GC:OFF_GC + D_CONV] = dgate.astype(BF16)
            dpj_ref[rows, OFF_CB:OFF_CB + D_CONV] = (da * co).astype(BF16)
            dco = da * cb
            dco1 = jnp.where(row >= CHUNK - 1, pltpu.roll(dco_after, CHUNK - 1, 0), pltpu.roll(dco, CHUNK - 1, 0))
            dco2 = jnp.where(row >= CHUNK - 2, pltpu.roll(dco_after, CHUNK - 2, 0), pltpu.roll(dco, CHUNK - 2, 0))
            dz = cw[2] * dco + cw[1] * dco1 + cw[0] * dco2
            dpj_ref[rows, OFF_CC:OFF_CC + D_CONV] = (dz * cu).astype(BF16)
            dpj_ref[rows, OFF_CU:OFF_CU + D_CONV] = (dz * cc).astype(BF16)
            acc_scr[ACC_NORM_CONV] += gterm
            acc_scr[ACC_CONV0] += dco * z2
            acc_scr[ACC_CONV0 + 1] += dco * z1
            acc_scr[ACC_CONV0 + 2] += dco * z
            dco_after = dco
        nxt_scr[...] = dco_after

        ks, vs = _kv_bands(pj, kvp_ref)
        gain_a = ga_ref[...]

        for r in range(N_CHUNKS):
            rows = _chunk_rows(r)
            da, dgate, gterm = _gated_norm_bwd(attn_ref[rows, :], gain_a, pj_ref[rows, OFF_GA:OFF_GA + D_ATTN],
                                               dmx_ref[rows, D_CONV:D_MIX])
            dpj_ref[rows, OFF_GA:OFF_GA + D_ATTN] = dgate.astype(BF16)
            dattn_scr[rows, :] = da
            acc_scr[ACC_NORM_ATTN] += gterm

        in_lo = lax.broadcasted_iota(jnp.int32, (128, 128), 0) < HEAD_DIM
        half_ones = (jnp.where(in_lo, 1.0, 0.0).astype(BF16), jnp.where(in_lo, 0.0, 1.0).astype(BF16))
        lane_s = lax.broadcasted_iota(jnp.int32, (1, D_MODEL), 1)
        gsink = jnp.zeros((1, D_MODEL), F32)
        dk_t, dv_t = [], []
        for j in range(2):
            q_stack = _q_stack(pj, j)
            do_f = jnp.concatenate([dattn_scr[:, _pair_cols(j, i, 0)] for i in range(PAIRS_PER_KV)], axis=0)
            o_f = jnp.concatenate([attn_ref[:, _pair_cols(j, i, 0)] for i in range(PAIRS_PER_KV)], axis=0)
            prod = (do_f * o_f).astype(BF16)
            deltas = [jnp.dot(prod, half_ones[e], preferred_element_type=F32) for e in range(2)]
            do_b = do_f.astype(BF16)
            q_t, do_t = q_stack.T, do_b.T
            dq, dk_j, dv_j = None, None, None
            for e in range(2):
                p = p_ref[0, 2 * j + e]
                dp = lax.dot_general(do_b, vs[j][e], _NT, preferred_element_type=F32)
                ds = []
                for i in range(PAIRS_PER_KV):
                    rows = slice(BLOCK * i, BLOCK * (i + 1))
                    delta = deltas[e][rows, :]
                    ds.append((p[rows, :].astype(F32) * (dp[rows, :] - jnp.concatenate([delta, delta], axis=1))).astype(BF16))
                    gs_h = -jnp.sum(ps_ref[0, 2 * j + e, rows, 0:1] * delta[:, 0:1], axis=0, keepdims=True)
                    gsink = gsink + jnp.where(lane_s == _head(j, i, e), gs_h, 0.0)
                ds = jnp.concatenate(ds, axis=0)
                t = jnp.dot(ds, ks[j][e], preferred_element_type=F32)
                dq = t if dq is None else dq + t
                half = slice(HEAD_DIM * e, HEAD_DIM * (e + 1))
                a = jnp.dot(q_t[half, :], ds, preferred_element_type=F32)
                b = jnp.dot(do_t[half, :], p, preferred_element_type=F32)
                dk_j = a if dk_j is None else dk_j + a
                dv_j = b if dv_j is None else dv_j + b
            for i in range(PAIRS_PER_KV):
                dpj_ref[:, _pair_cols(j, i, OFF_Q)] = (dq[BLOCK * i:BLOCK * (i + 1), :] * SCALE).astype(BF16)
            dk_t.append(dk_j)
            dv_t.append(dv_j)
        dk = jnp.concatenate(dk_t, axis=0).T
        dv = jnp.concatenate(dv_t, axis=0).T
        dpj_ref[:, OFF_K:OFF_K + D_KV] = (dk[BLOCK:, :] + dkv_scr[:, 0:D_KV]).astype(BF16)
        dpj_ref[:, OFF_V:OFF_V + D_KV] = (dv[BLOCK:, :] + dkv_scr[:, D_KV:2 * D_KV]).astype(BF16)
        dkv_scr[:, 0:D_KV] = dk[:BLOCK, :]
        dkv_scr[:, D_KV:2 * D_KV] = dv[:BLOCK, :]
        gslab_ref[ROW_SINKS:ROW_SINKS + 1, :] += gsink

        @pl.when(step == N_BLOCKS - 1)
        def _():
            for k, slab_row in ((ACC_NORM_CONV, ROW_NORM_CONV), (ACC_NORM_ATTN, ROW_NORM_ATTN), (ACC_CONV0, ROW_CONV0),
                                (ACC_CONV0 + 1, ROW_CONV0 + 1), (ACC_CONV0 + 2, ROW_CONV0 + 2)):
                gslab_ref[slab_row:slab_row + 1, :] = jnp.sum(acc_scr[k], axis=0, keepdims=True)

    per_block = BLOCK // HALO
    last = N_BLOCKS - 1
    return pl.pallas_call(
        body, name="mix_bwd", grid=(N_BLOCKS,),
        in_specs=[
            pl.BlockSpec((BLOCK, D_PROJ), lambda s: (last - s, 0)),
            pl.BlockSpec((BLOCK, 2 * D_KV), lambda s: (jnp.maximum(last - s - 1, 0), OFF_K // (2 * D_KV))),
            pl.BlockSpec((HALO, D_CONV), lambda s: (jnp.maximum((last - s) * per_block - 1, 0), OFF_CC // D_CONV)),
            pl.BlockSpec((HALO, D_CONV), lambda s: (jnp.maximum((last - s) * per_block - 1, 0), OFF_CU // D_CONV)),
            pl.BlockSpec((BLOCK, D_MIX), lambda s: (last - s, 0)),
            pl.BlockSpec((BLOCK, D_ATTN), lambda s: (last - s, 0)),
            pl.BlockSpec((1, 4, STACK, 2 * BLOCK), lambda s: (last - s, 0, 0, 0)),
            pl.BlockSpec((1, 4, STACK, 128), lambda s: (last - s, 0, 0, 0)),
            pl.BlockSpec((8, D_CONV), lambda s: (0, 0)),
            pl.BlockSpec((1, D_CONV), lambda s: (0, 0)),
            pl.BlockSpec((1, D_ATTN), lambda s: (0, 0)),
        ],
        out_specs=(pl.BlockSpec((BLOCK, D_PROJ), lambda s: (last - s, 0)),
                   pl.BlockSpec((8, D_MODEL), lambda s: (0, 0))),
        out_shape=(jax.ShapeDtypeStruct((SEQ, D_PROJ), BF16), jax.ShapeDtypeStruct((8, D_MODEL), F32)),
        scratch_shapes=[pltpu.VMEM((BLOCK, D_ATTN), F32), pltpu.VMEM((CHUNK, D_CONV), F32),
                        pltpu.VMEM((BLOCK, 2 * D_KV), F32), pltpu.VMEM((N_ACC, CHUNK, D_MODEL), F32)],
        compiler_params=_params(dimension_semantics=("arbitrary",)),
    )(proj, proj, proj, proj, dmixed, attn, probs, shares, conv_full, norm_conv, norm_attn)


def _in_bwd_rs(dproj, w_full, x, dx2, norm_in, dw_in_chip, gslab, gnf, loss_part):
    tm = 256
    steps = SEQ // tm
    relay_step = 4

    def body(dp_ref, w_ref, x_ref, dx2_ref, g_ref, dwi_ref, gs_ref, gnf_ref, lp_ref, gx_ref, gwin_ref, gsum_ref,
             gni_scr, own, ici, via, stage, myslab, slabs, send_sems, recv_sems, local_sems):
        i = pl.program_id(0)
        rs_start, rs_relay, rs_finish = _ici_sum(dwi_ref, own, ici, via, stage, send_sems, recv_sems, local_sems)
        slab_start, slab_finish = _slab_sum(myslab, slabs, send_sems, recv_sems, N_ICI_SUM_SEMS)

        @pl.when(i == 0)
        def _():
            gni_scr[...] = jnp.zeros_like(gni_scr)
            rs_start()

        dh = jnp.dot(dp_ref[...], w_ref[...], preferred_element_type=F32)
        xv = x_ref[...]
        r = lax.rsqrt(jnp.mean(xv * xv, axis=-1, keepdims=True) + RMS_EPS)
        xn = xv * r
        u = dh * g_ref[...]
        gx_ref[...] = dx2_ref[...] + r * (u - xn * jnp.mean(u * xn, axis=-1, keepdims=True))
        gni_scr[...] += jnp.sum(dh * xn, axis=0, keepdims=True)

        @pl.when(i == relay_step)
        def _():
            rs_relay()

        @pl.when(i == steps - 1)
        def _():
            row = lax.broadcasted_iota(jnp.int32, (8, D_MODEL), 0)
            lane = lax.broadcasted_iota(jnp.int32, (8, D_MODEL), 1)
            slab = jnp.where(row == ROW_NORM_IN, gni_scr[...], jnp.where(row == ROW_NORM_FINAL, gnf_ref[...], gs_ref[...]))
            myslab[...] = jnp.where((row == ROW_SINKS) & (lane == LOSS_LANE), lp_ref[0:1, 0:1], slab)
            slab_start()
            gwin_ref[...] = rs_finish()
            gsum_ref[...] = slab_finish()

    const = lambda i: (0, 0)
    return pl.pallas_call(
        body, name="in_bwd", grid=(steps,),
        in_specs=[pl.BlockSpec((tm, D_PROJ), lambda i: (i, 0)), pl.BlockSpec(memory_space=pltpu.VMEM),
                  pl.BlockSpec((tm, D_MODEL), lambda i: (i, 0)), pl.BlockSpec((tm, D_MODEL), lambda i: (i, 0)),
                  pl.BlockSpec((1, D_MODEL), const), pl.BlockSpec(memory_space=pl.ANY),
                  pl.BlockSpec((8, D_MODEL), const), pl.BlockSpec((1, D_MODEL), const), pl.BlockSpec((8, 128), const)],
        out_specs=(pl.BlockSpec((tm, D_MODEL), lambda i: (i, 0)), pl.BlockSpec((SHARD_IN, D_MODEL), const),
                   pl.BlockSpec((8, D_MODEL), const)),
        out_shape=(jax.ShapeDtypeStruct((SEQ, D_MODEL), F32), jax.ShapeDtypeStruct((SHARD_IN, D_MODEL), F32),
                   jax.ShapeDtypeStruct((8, D_MODEL), F32)),
        scratch_shapes=[pltpu.VMEM((1, D_MODEL), F32), pltpu.VMEM((SHARD_IN, D_MODEL), BF16),
                        pltpu.VMEM((2, SHARD_IN, D_MODEL), BF16), pltpu.VMEM((2, HALF_IN, D_MODEL), BF16),
                        pltpu.VMEM((2, HALF_IN, D_MODEL), BF16),
                        pltpu.VMEM((8, D_MODEL), F32), pltpu.VMEM((N_DEV, 8, D_MODEL), F32),
                        pltpu.SemaphoreType.DMA((N_ICI_SUM_SEMS + 7,)), pltpu.SemaphoreType.DMA((N_ICI_SUM_SEMS + 7,)),
                        pltpu.SemaphoreType.DMA((3,))],
        compiler_params=_params(dimension_semantics=("arbitrary",)),
    )(dproj, w_full, x, dx2, norm_in, dw_in_chip, gslab, gnf, loss_part)


def _dw_in_rs(dproj, h, dw_out_sh):
    tn = 640
    steps = D_PROJ // tn
    forward_step = 2

    def body(a_ref, b_ref, dwo_ref, chip_ref, gwo_ref, dwt, d2d_in, own, d2d, ici, send_sems, recv_sems, local_sems):
        i = pl.program_id(0)
        rs_start, rs_forward, rs_finish = _shard_sum(dwo_ref, own, d2d, ici, send_sems, recv_sems, local_sems)
        pair_send, pair_finish = _chip_sum(dwt, d2d_in, chip_ref, send_sems, recv_sems, local_sems,
                                           N_SHARD_SUM_SEMS, 4)

        @pl.when(i == 0)
        def _():
            rs_start()

        pair_send((i - 1) * tn, i * tn)

        tile = lax.dot_general(a_ref[...], b_ref[...], _TN, preferred_element_type=F32).astype(BF16)
        dwt[pl.ds(pl.multiple_of(i * tn, tn), tn), :] = tile

        @pl.when(i == forward_step)
        def _():
            rs_forward()

        @pl.when(i == steps - 1)
        def _():
            pair_send((steps - 1) * tn, D_PROJ)
            gwo_ref[...] = rs_finish()
            pair_finish()

    return pl.pallas_call(
        body, name="dw_in", grid=(steps,),
        in_specs=[pl.BlockSpec((SEQ, tn), lambda i: (0, i)), pl.BlockSpec(memory_space=pltpu.VMEM),
                  pl.BlockSpec(memory_space=pl.ANY)],
        out_specs=(pl.BlockSpec(memory_space=pl.ANY), pl.BlockSpec((SHARD_OUT, D_MODEL), lambda i: (0, 0))),
        out_shape=(jax.ShapeDtypeStruct((4, SHARD_IN, D_MODEL), BF16), jax.ShapeDtypeStruct((SHARD_OUT, D_MODEL), F32)),
        scratch_shapes=[pltpu.VMEM((D_PROJ, D_MODEL), BF16), pltpu.VMEM((4, SHARD_IN, D_MODEL), BF16),
                        *_shard_sum_scratch(SHARD_OUT),
                        pltpu.SemaphoreType.DMA((N_SHARD_SUM_SEMS + 4,)), pltpu.SemaphoreType.DMA((N_SHARD_SUM_SEMS + 4,)),
                        pltpu.SemaphoreType.DMA((8,))],
        compiler_params=_params(dimension_semantics=("arbitrary",)),
    )(dproj, h, dw_out_sh)


def _matmul_tn(a, b, tn, name):
    k, n = a.shape
    _, m = b.shape

    def body(a_ref, b_ref, o_ref):
        o_ref[...] = lax.dot_general(a_ref[...], b_ref[...], _TN, preferred_element_type=F32).astype(BF16)

    return pl.pallas_call(
        body, name=name, grid=(n // tn,),
        in_specs=[pl.BlockSpec((k, tn), lambda i: (0, i)), pl.BlockSpec(memory_space=pltpu.VMEM)],
        out_specs=pl.BlockSpec((tn, m), lambda i: (i, 0)),
        out_shape=jax.ShapeDtypeStruct((n, m), BF16),
        compiler_params=_params(dimension_semantics=("arbitrary",)),
    )(a, b)


def _adam_all(big_in, big_out, gsum, small, grad_x):
    steps = 4
    tr_in, tr_out = SHARD_IN // steps, SHARD_OUT // steps

    def body(*refs):
        ins, outs = refs[:8 + 1 + 18 + 1], refs[8 + 1 + 18 + 1:]
        i = pl.program_id(0)
        outs[33][...] = ins[27][...]
        for b in range(2):
            w_ref, g_ref, m_ref, v_ref = ins[4 * b:4 * b + 4]
            g = g_ref[...]
            delta, mn, vn = _adamw(w_ref[...], g, m_ref[...], v_ref[...])
            for ref, val in zip(outs[4 * b:4 * b + 4], (g, delta, mn, vn)):
                ref[...] = val

        @pl.when(i == 0)
        def _():
            gsum = ins[8][...]
            idx = _slot(lax.axis_index("x"), lax.axis_index("y"), lax.axis_index("c"))
            cg = jnp.zeros((3, SHARD_CONV), F32)
            for d in range(N_DEV):
                cg = jnp.where(idx == d, gsum[ROW_CONV0:ROW_CONV0 + 3, d * SHARD_CONV:(d + 1) * SHARD_CONV], cg)
            grads = (gsum[ROW_NORM_IN:ROW_NORM_IN + 1], gsum[ROW_SINKS:ROW_SINKS + 1, 0:N_Q_HEADS],
                     gsum[ROW_NORM_CONV:ROW_NORM_CONV + 1], gsum[ROW_NORM_ATTN:ROW_NORM_ATTN + 1],
                     gsum[ROW_NORM_FINAL:ROW_NORM_FINAL + 1], cg)
            for s, g in enumerate(grads):
                at = (slice(None), 0, slice(None)) if s == 5 else (slice(None), slice(None))
                w_ref, m_ref, v_ref = ins[9 + 3 * s:12 + 3 * s]
                delta, mn, vn = _adamw(w_ref[at], g, m_ref[at], v_ref[at])
                for ref, val in zip(outs[8 + 4 * s:12 + 4 * s], (g, delta, mn, vn)):
                    ref[at] = val
            outs[32][...] = gsum[ROW_SINKS:ROW_SINKS + 1, LOSS_LANE:LOSS_LANE + 1]

    const = lambda i: (0, 0)
    rows = lambda i: (i, 0)
    whole = lambda shape: pl.BlockSpec(shape, lambda i: (0,) * len(shape))
    small_shapes = [a.shape for a in small[::3]]
    in_specs = ([pl.BlockSpec((tr_in, D_MODEL), rows)] * 4 + [pl.BlockSpec((tr_out, D_MODEL), rows)] * 4
                + [pl.BlockSpec((8, D_MODEL), const)] + [whole(a.shape) for a in small]
                + [pl.BlockSpec((SEQ // steps, D_MODEL), rows)])
    out_specs = ([pl.BlockSpec((tr_in, D_MODEL), rows)] * 4 + [pl.BlockSpec((tr_out, D_MODEL), rows)] * 4
                 + [whole(s) for s in small_shapes for _ in range(4)] + [pl.BlockSpec((1, 1), const)]
                 + [pl.BlockSpec((SEQ // steps, D_MODEL), rows)])
    out_shape = ([jax.ShapeDtypeStruct((SHARD_IN, D_MODEL), F32)] * 4 + [jax.ShapeDtypeStruct((SHARD_OUT, D_MODEL), F32)] * 4
                 + [jax.ShapeDtypeStruct(s, F32) for s in small_shapes for _ in range(4)]
                 + [jax.ShapeDtypeStruct((1, 1), F32), jax.ShapeDtypeStruct((SEQ, D_MODEL), F32)])
    outs = pl.pallas_call(
        body, name="adam", grid=(steps,), in_specs=in_specs, out_specs=tuple(out_specs), out_shape=tuple(out_shape),
        compiler_params=_params(dimension_semantics=("arbitrary",)),
    )(*big_in, *big_out, gsum, *small, grad_x)
    return outs[0:4], outs[4:8], [outs[8 + 4 * s:12 + 4 * s] for s in range(6)], outs[32], outs[33]


def _rows_first(a):
    return jnp.transpose(a, (1, 0, 2))


def kernel(x, norm_in, w_in, conv_w, attn_sinks, norm_conv_out, norm_attn_out, w_out, norm_final, loss_target, m_norm_in, m_w_in, m_conv_w, m_attn_sinks, m_norm_conv_out, m_norm_attn_out, m_w_out, m_norm_final, v_norm_in, v_w_in, v_conv_w, v_attn_sinks, v_norm_conv_out, v_norm_attn_out, v_w_out, v_norm_final):
    x2d = x.reshape(SEQ, D_MODEL)
    target = loss_target.reshape(SEQ, D_MODEL)
    nf = norm_final.reshape(1, D_MODEL)

    w_in_t, m_w_in_t, v_w_in_t = w_in[0].T, m_w_in[0].T, v_w_in[0].T
    tiles = jnp.asarray(TILE_ORDER, jnp.int32)[2 * lax.axis_index("x") + lax.axis_index("y")]
    w_in_full, h, proj, g_out, conv_full = _gather_in_proj(x2d, norm_in, w_in_t, w_out[0], _rows_first(conv_w), tiles)
    sinks = attn_sinks.reshape(N_Q_HEADS)

    mixed, attn, probs, shares = _mix_fwd(proj, conv_full, sinks, norm_conv_out, norm_attn_out)
    dx2, dx2b, dmixed, gnf, loss_part = _out_proj_loss(mixed, x2d, target, g_out.reshape(D_MIX, D_MODEL), nf)
    dproj, gslab = _mix_bwd(proj, dmixed, attn, probs, shares, conv_full, norm_conv_out, norm_attn_out)
    dw_out = _matmul_tn(mixed, dx2b, 512, "dw_out")
    dw_in_chip, g_w_out = _dw_in_rs(dproj, h, dw_out.reshape(N_DEV, SHARD_OUT, D_MODEL))
    grad_x, g_w_in, gsum = _in_bwd_rs(dproj, w_in_full, x2d, dx2, norm_in, dw_in_chip, gslab, gnf, loss_part)

    small = (norm_in, m_norm_in, v_norm_in, attn_sinks, m_attn_sinks, v_attn_sinks,
             norm_conv_out, m_norm_conv_out, v_norm_conv_out, norm_attn_out, m_norm_attn_out, v_norm_attn_out,
             nf, m_norm_final.reshape(1, D_MODEL), v_norm_final.reshape(1, D_MODEL),
             _rows_first(conv_w), _rows_first(m_conv_w), _rows_first(v_conv_w))
    big_in, big_out, (s_ni, s_sk, s_nc, s_na, s_nf, s_cv), loss, grad_x = _adam_all(
        (w_in_t, g_w_in, m_w_in_t, v_w_in_t), (w_out[0], g_w_out, m_w_out[0], v_w_out[0]), gsum, small, grad_x)

    def leaves(k):
        return (s_ni[k], big_in[k].T[None], jnp.transpose(s_cv[k], (1, 0, 2)), s_sk[k], s_nc[k], s_na[k], big_out[k][None],
                s_nf[k].reshape(D_MODEL))

    return (loss.reshape(()), grad_x.reshape(1, SEQ, D_MODEL), *leaves(0), *leaves(1), *leaves(2), *leaves(3))
```

```python
import jax
import jax.numpy as jnp
from jax import lax
from jax.experimental import pallas as pl
from jax.experimental.pallas import tpu as pltpu

F32 = jnp.float32
BF16 = jnp.bfloat16
MESH = pl.DeviceIdType.MESH

N_DEV = 8
SEQ = 2048
D_MODEL = 1024
D_CONV = 1024
D_ATTN = 1024
D_KV = 128
HEAD_DIM = 64
N_Q_HEADS = 16
N_PAIRS = N_Q_HEADS // 2
PAIRS_PER_KV = N_PAIRS // 2
D_MIX = D_CONV + D_ATTN
D_PROJ = 6400
SHARD_IN = D_PROJ // N_DEV
SHARD_OUT = D_MIX // N_DEV
SHARD_CONV = D_CONV // N_DEV
OFF_CB, OFF_CC, OFF_CU, OFF_GC, OFF_Q, OFF_K, OFF_V, OFF_GA = 0, 1024, 2048, 3072, 4096, 5120, 5248, 5376
BLOCK = 128
N_BLOCKS = SEQ // BLOCK
HALO = 8
CHUNK = 16
N_CHUNKS = BLOCK // CHUNK
RMS_EPS = 1e-5
NEG = -1e30
SCALE = HEAD_DIM ** -0.5
SLOPES = tuple(2.0 ** (-8.0 * (h + 1) / N_Q_HEADS) for h in range(N_Q_HEADS))

ADAM_LR = 0.001
ADAM_B1 = 0.9
ADAM_B2 = 0.999
ADAM_EPS = 1e-08
ADAM_WD = 0.01
ADAM_STEP = 10

ROW_NORM_IN, ROW_NORM_CONV, ROW_NORM_ATTN, ROW_NORM_FINAL, ROW_CONV0, ROW_SINKS = 0, 1, 2, 3, 4, 7
LOSS_LANE = N_Q_HEADS
ACC_NORM_CONV, ACC_NORM_ATTN, ACC_CONV0, N_ACC = 0, 1, 2, 5

VMEM_LIMIT = 56 * 1024 * 1024

_NT = (((1,), (1,)), ((), ()))
_TN = (((0,), (0,)), ((), ()))


def _params(**kw):
    return pltpu.CompilerParams(vmem_limit_bytes=VMEM_LIMIT, **kw)


def _adamw(w, g, m, v):
    m = ADAM_B1 * m + (1.0 - ADAM_B1) * g
    v = ADAM_B2 * v + (1.0 - ADAM_B2) * (g * g)
    m_hat = m / (1.0 - ADAM_B1 ** ADAM_STEP)
    v_hat = v / (1.0 - ADAM_B2 ** ADAM_STEP)
    delta = -ADAM_LR * (m_hat / (jnp.sqrt(v_hat) + ADAM_EPS) + ADAM_WD * w)
    return delta, m, v


def _sigmoid(t):
    return 1.0 / (1.0 + jnp.exp(-t))


def _slot(px, py, pc):
    return 4 * px + 2 * py + pc


HALF_IN = SHARD_IN // 2
N_GATHER_KINDS = 13
W_OUT_KINDS = N_GATHER_KINDS + 7


IN_PROJ_TILE = 640
TILE_ORDER = ((0, 1, 2, 3, 4, 5, 6, 7, 8, 9), (3, 4, 0, 1, 2, 8, 9, 5, 6, 7),
              (5, 6, 0, 1, 7, 8, 9, 2, 3, 4), (8, 9, 3, 4, 5, 6, 7, 0, 1, 2))
TILES_OWN, TILES_NEIGHBOURS = 2, 7


def _gather_in_proj(x, norm_in, w_in_sh, w_out_sh, conv_sh, tiles):
    tn = IN_PROJ_TILE
    steps = D_PROJ // tn
    tm = 256

    def body(tiles_ref, x_ref, g_ref, win_ref, wout_ref, cv_ref, wt_ref, h_ref, proj_ref, gout_ref, conv_ref,
             gin_ref, gcv_ref, wob_ref, send_sems, recv_sems, local_sems):
        p = pl.program_id(0)
        local_sem = local_sems.at[0]
        x, y, c = lax.axis_index("x"), lax.axis_index("y"), lax.axis_index("c")
        me, sibling = (x, y, c), (x, y, 1 - c)
        nx, ny, dg = (1 - x, y, c), (x, 1 - y, c), (1 - x, 1 - y, c)

        def other(dev):
            return (dev[0], dev[1], 1 - dev[2])

        def shard(dev):
            return gin_ref.at[pl.ds(pl.multiple_of(_slot(*dev) * SHARD_IN, 16), SHARD_IN), :]

        def half(dev, h):
            return gin_ref.at[pl.ds(pl.multiple_of(_slot(*dev) * SHARD_IN + h * HALF_IN, 16), HALF_IN), :]

        def rc(ref, k, to):
            return pltpu.make_async_remote_copy(src_ref=ref, dst_ref=ref, send_sem=send_sems.at[k],
                                                recv_sem=recv_sems.at[k], device_id=to, device_id_type=MESH)

        def cv(k, dev, to):
            s = _slot(*dev)
            return pltpu.make_async_remote_copy(src_ref=gcv_ref.at[s], dst_ref=gcv_ref.at[s],
                                                send_sem=send_sems.at[N_GATHER_KINDS + k],
                                                recv_sem=recv_sems.at[N_GATHER_KINDS + k], device_id=to, device_id_type=MESH)

        def own_copies():
            return [rc(shard(me), 0, sibling),
                    rc(half(me, 0), 1, nx), rc(half(me, 1), 2, nx),
                    rc(half(me, 1), 4, ny), rc(half(me, 0), 3, ny),
                    cv(0, me, sibling)] + [cv(1 + j, me, peer) for j, peer in enumerate((nx, ny, dg))]

        def pass_on(dev, h, k_in, k_ici, k_d2d, half=half, base=0):
            rc(half(dev, h), base + k_in, me).wait_recv()
            if k_ici is not None:
                rc(half(dev, h), base + k_ici, ny if dev is nx else nx).start()
            rc(half(dev, h), base + k_d2d, sibling).start()

        def out_half(dev, h):
            return gout_ref.at[_slot(*dev), pl.ds(h * (SHARD_OUT // 2), SHARD_OUT // 2), :]

        def own_out_copies():
            src = lambda h: wob_ref.at[pl.ds(h * (SHARD_OUT // 2), SHARD_OUT // 2), :]

            def send(ref, dst, k, to):
                return pltpu.make_async_remote_copy(src_ref=ref, dst_ref=dst, send_sem=send_sems.at[W_OUT_KINDS + k],
                                                    recv_sem=recv_sems.at[W_OUT_KINDS + k], device_id=to, device_id_type=MESH)

            return [send(wob_ref, gout_ref.at[_slot(*me)], 0, sibling),
                    send(src(0), out_half(me, 0), 1, nx), send(src(1), out_half(me, 1), 2, nx),
                    send(src(1), out_half(me, 1), 4, ny), send(src(0), out_half(me, 0), 3, ny)]

        def own_out_local():
            return pltpu.make_async_copy(wob_ref, gout_ref.at[_slot(*me)], local_sems.at[1])

        @pl.when(p == 0)
        def _():
            gin_ref[pl.ds(pl.multiple_of(_slot(*me) * SHARD_IN, 16), SHARD_IN), :] = win_ref[...].astype(BF16)
            gcv_ref[_slot(*me)] = jnp.zeros((8, SHARD_CONV), F32)
            gcv_ref[_slot(*me), 0:3, :] = cv_ref[:, 0, :]
            for cp in own_copies():
                cp.start()
            wob_ref[...] = wout_ref[...].astype(BF16)
            for t in range(SEQ // tm):
                xv = x_ref[tm * t:tm * (t + 1), :]
                r = lax.rsqrt(jnp.mean(xv * xv, axis=-1, keepdims=True) + RMS_EPS)
                h_ref[tm * t:tm * (t + 1), :] = (xv * r * g_ref[...]).astype(BF16)
            rc(shard(sibling), 0, me).wait_recv()

        @pl.when(p == TILES_OWN)
        def _():
            for args in ((nx, 0, 1, 5, 7), (ny, 1, 4, 6, 10), (nx, 1, 2, None, 8), (ny, 0, 3, None, 9)):
                pass_on(*args)
            for j, peer in enumerate((nx, ny, dg)):
                cv(1 + j, peer, me).wait_recv()
                cv(4 + j, peer, sibling).start()
            for (dev, h), k in (((nx, 0), 7), ((nx, 1), 8), ((ny, 0), 9), ((ny, 1), 10)):
                rc(half(other(dev), h), k, me).wait_recv()
            own_out_local().start()
            for cp in own_out_copies():
                cp.start()

        @pl.when(p == TILES_NEIGHBOURS)
        def _():
            pass_on(dg, 0, 5, None, 11)
            pass_on(dg, 1, 6, None, 12)
            for (dev, h), k in (((dg, 0), 11), ((dg, 1), 12)):
                rc(half(other(dev), h), k, me).wait_recv()
            pltpu.make_async_copy(gin_ref, wt_ref, local_sem).start()

        @pl.when(p == steps - 2)
        def _():
            for args in ((nx, 0, 1, 5, 7), (ny, 1, 4, 6, 10), (nx, 1, 2, None, 8), (ny, 0, 3, None, 9)):
                pass_on(*args, half=out_half, base=W_OUT_KINDS)

        w = gin_ref[pl.ds(pl.multiple_of(tiles_ref[p] * tn, tn), tn), :]
        proj_ref[...] = lax.dot_general(h_ref[...], w, _NT, preferred_element_type=F32)

        @pl.when(p == steps - 1)
        def _():
            cv(0, sibling, me).wait_recv()
            for j, peer in enumerate((nx, ny, dg)):
                cv(4 + j, other(peer), me).wait_recv()
            for d in range(N_DEV):
                conv_ref[:, d * SHARD_CONV:(d + 1) * SHARD_CONV] = gcv_ref[d]
            relayed = [rc(half(nx, 0), 5, ny), rc(half(ny, 1), 6, nx)]
            relayed += [rc(half(dev, h), k, sibling) for (dev, h), k in
                        (((nx, 0), 7), ((nx, 1), 8), ((ny, 0), 9), ((ny, 1), 10), ((dg, 0), 11), ((dg, 1), 12))]
            relayed += [cv(4 + j, peer, sibling) for j, peer in enumerate((nx, ny, dg))]
            for cp in own_copies() + relayed:
                cp.wait_send()
            pltpu.make_async_copy(gin_ref, wt_ref, local_sem).wait()
            pass_on(dg, 0, 5, None, 11, half=out_half, base=W_OUT_KINDS)
            pass_on(dg, 1, 6, None, 12, half=out_half, base=W_OUT_KINDS)
            rc(gout_ref.at[_slot(*sibling)], W_OUT_KINDS, me).wait_recv()
            out_relayed = [rc(out_half(nx, 0), W_OUT_KINDS + 5, ny), rc(out_half(ny, 1), W_OUT_KINDS + 6, nx)]
            for (dev, h), k in (((nx, 0), 7), ((nx, 1), 8), ((ny, 0), 9), ((ny, 1), 10), ((dg, 0), 11), ((dg, 1), 12)):
                rc(out_half(other(dev), h), W_OUT_KINDS + k, me).wait_recv()
                out_relayed.append(rc(out_half(dev, h), W_OUT_KINDS + k, sibling))
            for cp in own_out_copies() + out_relayed:
                cp.wait_send()
            own_out_local().wait()

    vmem = pl.BlockSpec(memory_space=pltpu.VMEM)
    grid_spec = pltpu.PrefetchScalarGridSpec(
        num_scalar_prefetch=1, grid=(steps,),
        in_specs=[vmem, vmem, vmem, vmem, vmem],
        out_specs=(pl.BlockSpec(memory_space=pl.ANY), vmem,
                   pl.BlockSpec((SEQ, tn), lambda p, tiles_ref: (0, tiles_ref[p])), pl.BlockSpec(memory_space=pl.ANY), vmem),
        scratch_shapes=[pltpu.VMEM((D_PROJ, D_MODEL), BF16), pltpu.VMEM((N_DEV, 8, SHARD_CONV), F32),
                        pltpu.VMEM((SHARD_OUT, D_MODEL), BF16),
                        pltpu.SemaphoreType.DMA((W_OUT_KINDS + N_GATHER_KINDS,)),
                        pltpu.SemaphoreType.DMA((W_OUT_KINDS + N_GATHER_KINDS,)),
                        pltpu.SemaphoreType.DMA((2,))])
    return pl.pallas_call(
        body, name="gather_in_proj", grid_spec=grid_spec,
        out_shape=(jax.ShapeDtypeStruct((D_PROJ, D_MODEL), BF16), jax.ShapeDtypeStruct((SEQ, D_MODEL), BF16),
                   jax.ShapeDtypeStruct((SEQ, D_PROJ), F32), jax.ShapeDtypeStruct((N_DEV, SHARD_OUT, D_MODEL), BF16),
                   jax.ShapeDtypeStruct((8, D_CONV), F32)),
        compiler_params=_params(dimension_semantics=("arbitrary",)),
    )(tiles, x, norm_in, w_in_sh, w_out_sh, conv_sh)


def _shard_sum(src, own, d2d, ici, send_sems, recv_sems, local_sems, base=0):
    x, y, c = lax.axis_index("x"), lax.axis_index("y"), lax.axis_index("c")
    sibling = (x, y, 1 - c)
    chips = [(x, y), (1 - x, y), (x, 1 - y), (1 - x, 1 - y)]

    def rcopy(s, d, k, to):
        return pltpu.make_async_remote_copy(src_ref=s, dst_ref=d, send_sem=send_sems.at[base + k],
                                            recv_sem=recv_sems.at[base + k], device_id=to, device_id_type=MESH)

    def mine(k):
        return pltpu.make_async_copy(src.at[_slot(*chips[k], c)], own.at[k], local_sems.at[k])

    def to_sibling(k):
        return rcopy(src.at[_slot(*chips[k], 1 - c)], d2d.at[k], k, sibling)

    def to_chip(k):
        return rcopy(own.at[k], ici.at[k - 1], 3 + k, (*chips[k], c))

    def start():
        for k in range(4):
            mine(k).start()
            to_sibling(k).start()

    def forward():
        for k in range(1, 4):
            mine(k).wait()
            to_sibling(k).wait_recv()
            own[k] = (own[k].astype(F32) + d2d[k].astype(F32)).astype(BF16)
            to_chip(k).start()

    def finish():
        mine(0).wait()
        to_sibling(0).wait_recv()
        acc = own[0].astype(F32) + d2d[0].astype(F32)
        for k in range(1, 4):
            to_chip(k).wait_recv()
            acc = acc + ici[k - 1].astype(F32)
        for k in range(4):
            to_sibling(k).wait_send()
        for k in range(1, 4):
            to_chip(k).wait_send()
        return acc

    return start, forward, finish


def _shard_sum_scratch(rows):
    return [pltpu.VMEM((4, rows, D_MODEL), BF16), pltpu.VMEM((4, rows, D_MODEL), BF16),
            pltpu.VMEM((3, rows, D_MODEL), BF16)]


N_SHARD_SUM_SEMS = 7


def _chip_sum(dwt, d2d, out_hbm, send_sems, recv_sems, local_sems, base, local_base):
    x, y, c = lax.axis_index("x"), lax.axis_index("y"), lax.axis_index("c")
    sibling = (x, y, 1 - c)
    chips = [(x, y), (1 - x, y), (x, 1 - y), (1 - x, 1 - y)]

    def shard(s):
        return dwt.at[pl.ds(pl.multiple_of(s * SHARD_IN, 16), SHARD_IN), :]

    def to_sibling(k):
        return pltpu.make_async_remote_copy(src_ref=shard(_slot(*chips[k], 1 - c)), dst_ref=d2d.at[k],
                                            send_sem=send_sems.at[base + k], recv_sem=recv_sems.at[base + k],
                                            device_id=sibling, device_id_type=MESH)

    def save(k):
        return pltpu.make_async_copy(d2d.at[k], out_hbm.at[k], local_sems.at[local_base + k])

    def send(rows_before, rows_done):
        for k in range(4):
            end = (_slot(*chips[k], 1 - c) + 1) * SHARD_IN

            @pl.when((end > rows_before) & (end <= rows_done))
            def _():
                to_sibling(k).start()

    def finish():
        for k in range(4):
            to_sibling(k).wait_recv()
            d2d[k] = (shard(_slot(*chips[k], c))[...].astype(F32) + d2d[k].astype(F32)).astype(BF16)
            save(k).start()
        for k in range(4):
            save(k).wait()
            to_sibling(k).wait_send()

    return send, finish


N_ICI_SUM_SEMS = 6


def _ici_sum(src, own, ici, via, stage, send_sems, recv_sems, local_sems, base=0):
    x, y, c = lax.axis_index("x"), lax.axis_index("y"), lax.axis_index("c")
    nx, ny = (1 - x, y, c), (x, 1 - y, c)
    OWN, NX, NY, DG = range(4)

    def half(ref, h):
        return ref.at[pl.ds(h * HALF_IN, HALF_IN), :]

    def rc(s, d, k, to):
        return pltpu.make_async_remote_copy(src_ref=s, dst_ref=d, send_sem=send_sems.at[base + k],
                                            recv_sem=recv_sems.at[base + k], device_id=to, device_id_type=MESH)

    for_dg_0 = lambda: rc(half(src.at[DG], 0), via.at[0], 0, nx)
    for_dg_1 = lambda: rc(half(src.at[DG], 1), via.at[1], 1, ny)
    for_nx_0 = lambda: rc(half(src.at[NX], 0), half(ici.at[0], 0), 2, nx)
    for_ny_1 = lambda: rc(half(src.at[NY], 1), half(ici.at[1], 1), 3, ny)
    for_ny_0 = lambda: rc(stage.at[0], half(ici.at[1], 0), 4, ny)
    for_nx_1 = lambda: rc(stage.at[1], half(ici.at[0], 1), 5, nx)
    mine = lambda: pltpu.make_async_copy(src.at[OWN], own, local_sems.at[0])
    stage_0 = lambda: pltpu.make_async_copy(half(src.at[NY], 0), stage.at[0], local_sems.at[1])
    stage_1 = lambda: pltpu.make_async_copy(half(src.at[NX], 1), stage.at[1], local_sems.at[2])

    def start():
        for cp in (for_dg_0, for_dg_1, for_nx_0, for_ny_1, stage_0, stage_1, mine):
            cp().start()

    def relay():
        for h, staged, landed, out in ((0, stage_0, for_dg_0, for_ny_0), (1, stage_1, for_dg_1, for_nx_1)):
            staged().wait()
            landed().wait_recv()
            stage[h] = (stage[h].astype(F32) + via[h].astype(F32)).astype(BF16)
            out().start()

    def finish():
        mine().wait()
        for cp in (for_nx_0, for_nx_1, for_ny_1, for_ny_0):
            cp().wait_recv()
        acc = own[...].astype(F32) + ici[0].astype(F32) + ici[1].astype(F32)
        for cp in (for_dg_0, for_dg_1, for_nx_0, for_ny_1, for_ny_0, for_nx_1):
            cp().wait_send()
        return acc

    return start, relay, finish


def _slab_sum(myslab, slabs, send_sems, recv_sems, base):
    x, y, c = lax.axis_index("x"), lax.axis_index("y"), lax.axis_index("c")
    me = _slot(x, y, c)
    peers = [(x, y, 1 - c), (1 - x, y, c), (x, 1 - y, c), (1 - x, 1 - y, c),
             (1 - x, y, 1 - c), (x, 1 - y, 1 - c), (1 - x, 1 - y, 1 - c)]

    def cp(k):
        return pltpu.make_async_remote_copy(src_ref=myslab, dst_ref=slabs.at[me], send_sem=send_sems.at[base + k],
                                            recv_sem=recv_sems.at[base + k], device_id=peers[k], device_id_type=MESH)

    def start():
        slabs[me] = myslab[...]
        for k in range(7):
            cp(k).start()

    def finish():
        for k in range(7):
            cp(k).wait_recv()
        total = slabs[0]
        for d in range(1, N_DEV):
            total = total + slabs[d]
        for k in range(7):
            cp(k).wait_send()
        return total

    return start, finish


def _chunk_rows(r):
    return slice(r * CHUNK, (r + 1) * CHUNK)


def _conv_halo(cch_ref, cuh_ref, n):
    zh = jnp.where(n > 0, cch_ref[...] * cuh_ref[...], 0.0)
    return jnp.concatenate([zh] * (CHUNK // HALO), axis=0)


def _conv_chunk(pj_ref, zhalo, cw, r):
    rows = _chunk_rows(r)
    cc = pj_ref[rows, OFF_CC:OFF_CC + D_CONV]
    cu = pj_ref[rows, OFF_CU:OFF_CU + D_CONV]
    z = cc * cu
    before = _chunk_rows(r - 1)
    zprev = pj_ref[before, OFF_CC:OFF_CC + D_CONV] * pj_ref[before, OFF_CU:OFF_CU + D_CONV] if r > 0 else zhalo
    row = lax.broadcasted_iota(jnp.int32, (CHUNK, D_CONV), 0)
    z1 = jnp.where(row < 1, pltpu.roll(zprev, 1, 0), pltpu.roll(z, 1, 0))
    z2 = jnp.where(row < 2, pltpu.roll(zprev, 2, 0), pltpu.roll(z, 2, 0))
    co = cw[0] * z2 + cw[1] * z1 + cw[2] * z
    return cc, cu, z, z1, z2, co


def _gated_norm(a, gain, t):
    r = lax.rsqrt(jnp.mean(a * a, axis=-1, keepdims=True) + RMS_EPS)
    return a * r * gain * (t * _sigmoid(t))


def _kv_bands(pj, kvp_ref):
    lane = lax.broadcasted_iota(jnp.int32, (2 * BLOCK, D_KV), 1)
    lo = lane < HEAD_DIM

    def bands(prev, cur):
        b = jnp.concatenate([prev, cur], axis=0)
        br = pltpu.roll(b, HEAD_DIM, 1)
        zero = jnp.zeros_like(b)
        return ((jnp.where(lo, b, zero).astype(BF16), jnp.where(lo, zero, br).astype(BF16)),
                (jnp.where(lo, br, zero).astype(BF16), jnp.where(lo, zero, b).astype(BF16)))

    ks = bands(kvp_ref[:, 0:D_KV], pj[:, OFF_K:OFF_K + D_KV])
    vs = bands(kvp_ref[:, D_KV:2 * D_KV], pj[:, OFF_V:OFF_V + D_KV])
    return ks, vs


STACK = PAIRS_PER_KV * BLOCK


def _head(j, i, e):
    return 2 * (PAIRS_PER_KV * j + i) + e


def _pair_cols(j, i, off):
    p = PAIRS_PER_KV * j + i
    return slice(off + 128 * p, off + 128 * (p + 1))


def _fill_attn_bias(bias_scr, first_block):
    qi = lax.broadcasted_iota(jnp.int32, (BLOCK, 2 * BLOCK), 0)
    kj = lax.broadcasted_iota(jnp.int32, (BLOCK, 2 * BLOCK), 1)
    dist = BLOCK + qi - kj
    valid = (dist >= 0) & (dist < BLOCK)
    if first_block:
        valid = valid & (kj >= BLOCK)
    distf = dist.astype(F32)
    for j in range(2):
        for e in range(2):
            for i in range(PAIRS_PER_KV):
                bias_scr[2 * j + e, BLOCK * i:BLOCK * (i + 1), :] = jnp.where(valid, -SLOPES[_head(j, i, e)] * distf, NEG)


def _q_stack(pj, j):
    return jnp.concatenate([(pj[:, _pair_cols(j, i, OFF_Q)] * SCALE).astype(BF16) for i in range(PAIRS_PER_KV)], axis=0)


def _attn_probs(q_stack, kband, bias_ref, sinks):
    s = lax.dot_general(q_stack, kband, _NT, preferred_element_type=F32)
    ones = jnp.ones((128, 128), BF16)
    probs, shares = [], []
    for i, sink in enumerate(sinks):
        rows = slice(BLOCK * i, BLOCK * (i + 1))
        t = s[rows, :] + bias_ref[rows, :]
        m = jnp.broadcast_to(jnp.max(t, axis=-1, keepdims=True), (BLOCK, 128))
        m = jnp.maximum(m, sink)
        p = [jnp.exp(t[:, :128] - m), jnp.exp(t[:, 128:] - m)]
        es = jnp.exp(sink - m)
        total = (jnp.dot(p[0].astype(BF16), ones, preferred_element_type=F32)
                 + jnp.dot(p[1].astype(BF16), ones, preferred_element_type=F32))
        inv = 1.0 / (total + es)
        probs.append(jnp.concatenate([p[0] * inv, p[1] * inv], axis=1))
        shares.append(es * inv)
    return jnp.concatenate(probs, axis=0), jnp.concatenate(shares, axis=0)


def _attn_group(pj, ks, vs, bias_scr, sink_ref, j):
    q_stack = _q_stack(pj, j)
    out, probs, shares = None, [], []
    for e in range(2):
        p, ps = _attn_probs(q_stack, ks[j][e], bias_scr.at[2 * j + e],
                            [sink_ref[_head(j, i, e)] for i in range(PAIRS_PER_KV)])
        p = p.astype(BF16)
        o = jnp.dot(p, vs[j][e], preferred_element_type=F32)
        out = o if out is None else out + o
        probs.append(p)
        shares.append(ps)
    return out, probs, shares


def _mix_fwd(proj, conv_full, sinks, norm_conv, norm_attn):
    def body(pj_ref, kvp_ref, cch_ref, cuh_ref, cw_ref, sink_ref, gc_ref, ga_ref,
             mixed_ref, attn_scr, p_ref, ps_ref, bias_scr):
        n = pl.program_id(0)
        pj = pj_ref

        @pl.when(n == 0)
        def _():
            _fill_attn_bias(bias_scr, first_block=True)

        @pl.when(n == 1)
        def _():
            _fill_attn_bias(bias_scr, first_block=False)

        zhalo = _conv_halo(cch_ref, cuh_ref, n)
        cw = (cw_ref[0:1, :], cw_ref[1:2, :], cw_ref[2:3, :])
        gain_c = gc_ref[...]

        for r in range(N_CHUNKS):
            rows = _chunk_rows(r)
            co = _conv_chunk(pj_ref, zhalo, cw, r)[-1]
            y = _gated_norm(pj_ref[rows, OFF_CB:OFF_CB + D_CONV] * co, gain_c, pj_ref[rows, OFF_GC:OFF_GC + D_CONV])
            mixed_ref[rows, 0:D_CONV] = y.astype(BF16)

        ks, vs = _kv_bands(pj, kvp_ref)
        for j in range(2):
            out, probs, shares = _attn_group(pj, ks, vs, bias_scr, sink_ref, j)
            for e in range(2):
                p_ref[0, 2 * j + e] = probs[e]
                ps_ref[0, 2 * j + e] = shares[e]
            for i in range(PAIRS_PER_KV):
                attn_scr[:, _pair_cols(j, i, 0)] = out[BLOCK * i:BLOCK * (i + 1), :]
        gain_a = ga_ref[...]

        for r in range(N_CHUNKS):
            rows = _chunk_rows(r)
            y = _gated_norm(attn_scr[rows, :], gain_a, pj_ref[rows, OFF_GA:OFF_GA + D_ATTN])
            mixed_ref[rows, D_CONV:D_MIX] = y.astype(BF16)

    per_block = BLOCK // HALO
    return pl.pallas_call(
        body, name="mix_fwd", grid=(N_BLOCKS,),
        in_specs=[
            pl.BlockSpec((BLOCK, D_PROJ), lambda n: (n, 0)),
            pl.BlockSpec((BLOCK, 2 * D_KV), lambda n: (jnp.maximum(n - 1, 0), OFF_K // (2 * D_KV))),
            pl.BlockSpec((HALO, D_CONV), lambda n: (jnp.maximum(n * per_block - 1, 0), OFF_CC // D_CONV)),
            pl.BlockSpec((HALO, D_CONV), lambda n: (jnp.maximum(n * per_block - 1, 0), OFF_CU // D_CONV)),
            pl.BlockSpec((8, D_CONV), lambda n: (0, 0)),
            pl.BlockSpec(memory_space=pltpu.SMEM),
            pl.BlockSpec((1, D_CONV), lambda n: (0, 0)),
            pl.BlockSpec((1, D_ATTN), lambda n: (0, 0)),
        ],
        out_specs=(pl.BlockSpec((BLOCK, D_MIX), lambda n: (n, 0)), pl.BlockSpec((BLOCK, D_ATTN), lambda n: (n, 0)),
                   pl.BlockSpec((1, 4, STACK, 2 * BLOCK), lambda n: (n, 0, 0, 0)),
                   pl.BlockSpec((1, 4, STACK, 128), lambda n: (n, 0, 0, 0))),
        out_shape=(jax.ShapeDtypeStruct((SEQ, D_MIX), BF16), jax.ShapeDtypeStruct((SEQ, D_ATTN), F32),
                   jax.ShapeDtypeStruct((N_BLOCKS, 4, STACK, 2 * BLOCK), BF16),
                   jax.ShapeDtypeStruct((N_BLOCKS, 4, STACK, 128), F32)),
        scratch_shapes=[pltpu.VMEM((4, STACK, 2 * BLOCK), F32)],
        compiler_params=_params(dimension_semantics=("arbitrary",)),
    )(proj, proj, proj, proj, conv_full, sinks, norm_conv, norm_attn)


def _out_proj_loss(mixed, x, target, w_out_full, norm_final):
    tm = 256

    def body(mx_ref, x_ref, t_ref, w_ref, g_ref, dx2_ref, dx2b_ref, dmix_ref, gnf_ref, loss_ref):
        i = pl.program_id(0)
        w = w_ref[...]
        x2 = x_ref[...] + jnp.dot(mx_ref[...], w, preferred_element_type=F32)
        r = lax.rsqrt(jnp.mean(x2 * x2, axis=-1, keepdims=True) + RMS_EPS)
        xn = x2 * r
        g = g_ref[...]
        err = xn * g - t_ref[...]
        part = 0.5 * jnp.sum(jnp.mean(err * err, axis=-1, keepdims=True), axis=0, keepdims=True)
        dy = err * (1.0 / D_MODEL)
        gnf = jnp.sum(dy * xn, axis=0, keepdims=True)
        u = dy * g
        dx2 = r * (u - xn * jnp.mean(u * xn, axis=-1, keepdims=True))
        dx2_ref[...] = dx2
        dx2b = dx2.astype(BF16)
        dx2b_ref[...] = dx2b
        dmix_ref[...] = lax.dot_general(dx2b, w, _NT, preferred_element_type=F32)

        @pl.when(i == 0)
        def _():
            gnf_ref[...] = jnp.zeros_like(gnf_ref)
            loss_ref[...] = jnp.zeros_like(loss_ref)

        gnf_ref[...] += gnf
        loss_ref[...] += jnp.broadcast_to(part, loss_ref.shape)

    return pl.pallas_call(
        body, name="out_proj_loss", grid=(SEQ // tm,),
        in_specs=[pl.BlockSpec((tm, D_MIX), lambda i: (i, 0)), pl.BlockSpec((tm, D_MODEL), lambda i: (i, 0)),
                  pl.BlockSpec((tm, D_MODEL), lambda i: (i, 0)), pl.BlockSpec(memory_space=pltpu.VMEM),
                  pl.BlockSpec((1, D_MODEL), lambda i: (0, 0))],
        out_specs=(pl.BlockSpec((tm, D_MODEL), lambda i: (i, 0)), pl.BlockSpec((tm, D_MODEL), lambda i: (i, 0)),
                   pl.BlockSpec((tm, D_MIX), lambda i: (i, 0)),
                   pl.BlockSpec((1, D_MODEL), lambda i: (0, 0)), pl.BlockSpec((8, 128), lambda i: (0, 0))),
        out_shape=(jax.ShapeDtypeStruct((SEQ, D_MODEL), F32), jax.ShapeDtypeStruct((SEQ, D_MODEL), BF16),
                   jax.ShapeDtypeStruct((SEQ, D_MIX), F32),
                   jax.ShapeDtypeStruct((1, D_MODEL), F32), jax.ShapeDtypeStruct((8, 128), F32)),
        compiler_params=_params(dimension_semantics=("arbitrary",)),
    )(mixed, x, target, w_out_full, norm_final)


def _gated_norm_bwd(a, gain, t, dy):
    r = lax.rsqrt(jnp.mean(a * a, axis=-1, keepdims=True) + RMS_EPS)
    an = a * r
    sg = _sigmoid(t)
    dn = dy * (t * sg)
    dt = dy * (an * gain) * (sg * (1.0 + t * (1.0 - sg)))
    u = dn * gain
    da = r * (u - an * jnp.mean(u * an, axis=-1, keepdims=True))
    return da, dt, dn * an


def _mix_bwd(proj, dmixed, attn, probs, shares, conv_full, norm_conv, norm_attn):
    def body(pj_ref, kvp_ref, cch_ref, cuh_ref, dmx_ref, attn_ref, p_ref, ps_ref, cw_ref, gc_ref, ga_ref,
             dpj_ref, gslab_ref, dattn_scr, nxt_scr, dkv_scr, acc_scr):
        step = pl.program_id(0)
        n = N_BLOCKS - 1 - step
        pj = pj_ref

        @pl.when(step == 0)
        def _():
            gslab_ref[...] = jnp.zeros_like(gslab_ref)
            nxt_scr[...] = jnp.zeros_like(nxt_scr)
            dkv_scr[...] = jnp.zeros_like(dkv_scr)
            acc_scr[...] = jnp.zeros_like(acc_scr)

        zhalo = _conv_halo(cch_ref, cuh_ref, n)
        cw = (cw_ref[0:1, :], cw_ref[1:2, :], cw_ref[2:3, :])
        gain_c = gc_ref[...]
        row = lax.broadcasted_iota(jnp.int32, (CHUNK, D_CONV), 0)

        dco_after = nxt_scr[...]
        for r in reversed(range(N_CHUNKS)):
            rows = _chunk_rows(r)
            cc, cu, z, z1, z2, co = _conv_chunk(pj_ref, zhalo, cw, r)
            cb = pj_ref[rows, OFF_CB:OFF_CB + D_CONV]
            da, dgate, gterm = _gated_norm_bwd(cb * co, gain_c, pj_ref[rows, OFF_GC:OFF_GC + D_CONV],
                                               dmx_ref[rows, 0:D_CONV])
            dpj_ref[rows, OFF_GC:OFF_GC + D_CONV] = dgate.astype(BF16)
            dpj_ref[rows, OFF_CB:OFF_CB + D_CONV] = (da * co).astype(BF16)
            dco = da * cb
            dco1 = jnp.where(row >= CHUNK - 1, pltpu.roll(dco_after, CHUNK - 1, 0), pltpu.roll(dco, CHUNK - 1, 0))
            dco2 = jnp.where(row >= CHUNK - 2, pltpu.roll(dco_after, CHUNK - 2, 0), pltpu.roll(dco, CHUNK - 2, 0))
            dz = cw[2] * dco + cw[1] * dco1 + cw[0] * dco2
            dpj_ref[rows, OFF_CC:OFF_CC + D_CONV] = (dz * cu).astype(BF16)
            dpj_ref[rows, OFF_CU:OFF_CU + D_CONV] = (dz * cc).astype(BF16)
            acc_scr[ACC_NORM_CONV] += gterm
            acc_scr[ACC_CONV0] += dco * z2
            acc_scr[ACC_CONV0 + 1] += dco * z1
            acc_scr[ACC_CONV0 + 2] += dco * z
            dco_after = dco
        nxt_scr[...] = dco_after

        ks, vs = _kv_bands(pj, kvp_ref)
        gain_a = ga_ref[...]

        for r in range(N_CHUNKS):
            rows = _chunk_rows(r)
            da, dgate, gterm = _gated_norm_bwd(attn_ref[rows, :], gain_a, pj_ref[rows, OFF_GA:OFF_GA + D_ATTN],
                                               dmx_ref[rows, D_CONV:D_MIX])
            dpj_ref[rows, OFF_GA:OFF_GA + D_ATTN] = dgate.astype(BF16)
            dattn_scr[rows, :] = da
            acc_scr[ACC_NORM_ATTN] += gterm

        in_lo = lax.broadcasted_iota(jnp.int32, (128, 128), 0) < HEAD_DIM
        half_ones = (jnp.where(in_lo, 1.0, 0.0).astype(BF16), jnp.where(in_lo, 0.0, 1.0).astype(BF16))
        lane_s = lax.broadcasted_iota(jnp.int32, (1, D_MODEL), 1)
        gsink = jnp.zeros((1, D_MODEL), F32)
        dk_t, dv_t = [], []
        for j in range(2):
            q_stack = _q_stack(pj, j)
            do_f = jnp.concatenate([dattn_scr[:, _pair_cols(j, i, 0)] for i in range(PAIRS_PER_KV)], axis=0)
            o_f = jnp.concatenate([attn_ref[:, _pair_cols(j, i, 0)] for i in range(PAIRS_PER_KV)], axis=0)
            prod = (do_f * o_f).astype(BF16)
            deltas = [jnp.dot(prod, half_ones[e], preferred_element_type=F32) for e in range(2)]
            do_b = do_f.astype(BF16)
            q_t, do_t = q_stack.T, do_b.T
            dq, dk_j, dv_j = None, None, None
            for e in range(2):
                p = p_ref[0, 2 * j + e]
                dp = lax.dot_general(do_b, vs[j][e], _NT, preferred_element_type=F32)
                ds = []
                for i in range(PAIRS_PER_KV):
                    rows = slice(BLOCK * i, BLOCK * (i + 1))
                    delta = deltas[e][rows, :]
                    ds.append((p[rows, :].astype(F32) * (dp[rows, :] - jnp.concatenate([delta, delta], axis=1))).astype(BF16))
                    gs_h = -jnp.sum(ps_ref[0, 2 * j + e, rows, 0:1] * delta[:, 0:1], axis=0, keepdims=True)
                    gsink = gsink + jnp.where(lane_s == _head(j, i, e), gs_h, 0.0)
                ds = jnp.concatenate(ds, axis=0)
                t = jnp.dot(ds, ks[j][e], preferred_element_type=F32)
                dq = t if dq is None else dq + t
                half = slice(HEAD_DIM * e, HEAD_DIM * (e + 1))
                a = jnp.dot(q_t[half, :], ds, preferred_element_type=F32)
                b = jnp.dot(do_t[half, :], p, preferred_element_type=F32)
                dk_j = a if dk_j is None else dk_j + a
                dv_j = b if dv_j is None else dv_j + b
            for i in range(PAIRS_PER_KV):
                dpj_ref[:, _pair_cols(j, i, OFF_Q)] = (dq[BLOCK * i:BLOCK * (i + 1), :] * SCALE).astype(BF16)
            dk_t.append(dk_j)
            dv_t.append(dv_j)
        dk = jnp.concatenate(dk_t, axis=0).T
        dv = jnp.concatenate(dv_t, axis=0).T
        dpj_ref[:, OFF_K:OFF_K + D_KV] = (dk[BLOCK:, :] + dkv_scr[:, 0:D_KV]).astype(BF16)
        dpj_ref[:, OFF_V:OFF_V + D_KV] = (dv[BLOCK:, :] + dkv_scr[:, D_KV:2 * D_KV]).astype(BF16)
        dkv_scr[:, 0:D_KV] = dk[:BLOCK, :]
        dkv_scr[:, D_KV:2 * D_KV] = dv[:BLOCK, :]
        gslab_ref[ROW_SINKS:ROW_SINKS + 1, :] += gsink

        @pl.when(step == N_BLOCKS - 1)
        def _():
            for k, slab_row in ((ACC_NORM_CONV, ROW_NORM_CONV), (ACC_NORM_ATTN, ROW_NORM_ATTN), (ACC_CONV0, ROW_CONV0),
                                (ACC_CONV0 + 1, ROW_CONV0 + 1), (ACC_CONV0 + 2, ROW_CONV0 + 2)):
                gslab_ref[slab_row:slab_row + 1, :] = jnp.sum(acc_scr[k], axis=0, keepdims=True)

    per_block = BLOCK // HALO
    last = N_BLOCKS - 1
    return pl.pallas_call(
        body, name="mix_bwd", grid=(N_BLOCKS,),
        in_specs=[
            pl.BlockSpec((BLOCK, D_PROJ), lambda s: (last - s, 0)),
            pl.BlockSpec((BLOCK, 2 * D_KV), lambda s: (jnp.maximum(last - s - 1, 0), OFF_K // (2 * D_KV))),
            pl.BlockSpec((HALO, D_CONV), lambda s: (jnp.maximum((last - s) * per_block - 1, 0), OFF_CC // D_CONV)),
            pl.BlockSpec((HALO, D_CONV), lambda s: (jnp.maximum((last - s) * per_block - 1, 0), OFF_CU // D_CONV)),
            pl.BlockSpec((BLOCK, D_MIX), lambda s: (last - s, 0)),
            pl.BlockSpec((BLOCK, D_ATTN), lambda s: (last - s, 0)),
            pl.BlockSpec((1, 4, STACK, 2 * BLOCK), lambda s: (last - s, 0, 0, 0)),
            pl.BlockSpec((1, 4, STACK, 128), lambda s: (last - s, 0, 0, 0)),
            pl.BlockSpec((8, D_CONV), lambda s: (0, 0)),
            pl.BlockSpec((1, D_CONV), lambda s: (0, 0)),
            pl.BlockSpec((1, D_ATTN), lambda s: (0, 0)),
        ],
        out_specs=(pl.BlockSpec((BLOCK, D_PROJ), lambda s: (last - s, 0)),
                   pl.BlockSpec((8, D_MODEL), lambda s: (0, 0))),
        out_shape=(jax.ShapeDtypeStruct((SEQ, D_PROJ), BF16), jax.ShapeDtypeStruct((8, D_MODEL), F32)),
        scratch_shapes=[pltpu.VMEM((BLOCK, D_ATTN), F32), pltpu.VMEM((CHUNK, D_CONV), F32),
                        pltpu.VMEM((BLOCK, 2 * D_KV), F32), pltpu.VMEM((N_ACC, CHUNK, D_MODEL), F32)],
        compiler_params=_params(dimension_semantics=("arbitrary",)),
    )(proj, proj, proj, proj, dmixed, attn, probs, shares, conv_full, norm_conv, norm_attn)


def _in_bwd_rs(dproj, w_full, x, dx2, norm_in, dw_in_chip, gslab, gnf, loss_part):
    tm = 256
    steps = SEQ // tm
    relay_step = 4

    def body(dp_ref, w_ref, x_ref, dx2_ref, g_ref, dwi_ref, gs_ref, gnf_ref, lp_ref, gx_ref, gwin_ref, gsum_ref,
             gni_scr, own, ici, via, stage, myslab, slabs, send_sems, recv_sems, local_sems):
        i = pl.program_id(0)
        rs_start, rs_relay, rs_finish = _ici_sum(dwi_ref, own, ici, via, stage, send_sems, recv_sems, local_sems)
        slab_start, slab_finish = _slab_sum(myslab, slabs, send_sems, recv_sems, N_ICI_SUM_SEMS)

        @pl.when(i == 0)
        def _():
            gni_scr[...] = jnp.zeros_like(gni_scr)
            rs_start()

        dh = jnp.dot(dp_ref[...], w_ref[...], preferred_element_type=F32)
        xv = x_ref[...]
        r = lax.rsqrt(jnp.mean(xv * xv, axis=-1, keepdims=True) + RMS_EPS)
        xn = xv * r
        u = dh * g_ref[...]
        gx_ref[...] = dx2_ref[...] + r * (u - xn * jnp.mean(u * xn, axis=-1, keepdims=True))
        gni_scr[...] += jnp.sum(dh * xn, axis=0, keepdims=True)

        @pl.when(i == relay_step)
        def _():
            rs_relay()

        @pl.when(i == steps - 1)
        def _():
            row = lax.broadcasted_iota(jnp.int32, (8, D_MODEL), 0)
            lane = lax.broadcasted_iota(jnp.int32, (8, D_MODEL), 1)
            slab = jnp.where(row == ROW_NORM_IN, gni_scr[...], jnp.where(row == ROW_NORM_FINAL, gnf_ref[...], gs_ref[...]))
            myslab[...] = jnp.where((row == ROW_SINKS) & (lane == LOSS_LANE), lp_ref[0:1, 0:1], slab)
            slab_start()
            gwin_ref[...] = rs_finish()
            gsum_ref[...] = slab_finish()

    const = lambda i: (0, 0)
    return pl.pallas_call(
        body, name="in_bwd", grid=(steps,),
        in_specs=[pl.BlockSpec((tm, D_PROJ), lambda i: (i, 0)), pl.BlockSpec(memory_space=pltpu.VMEM),
                  pl.BlockSpec((tm, D_MODEL), lambda i: (i, 0)), pl.BlockSpec((tm, D_MODEL), lambda i: (i, 0)),
                  pl.BlockSpec((1, D_MODEL), const), pl.BlockSpec(memory_space=pl.ANY),
                  pl.BlockSpec((8, D_MODEL), const), pl.BlockSpec((1, D_MODEL), const), pl.BlockSpec((8, 128), const)],
        out_specs=(pl.BlockSpec((tm, D_MODEL), lambda i: (i, 0)), pl.BlockSpec((SHARD_IN, D_MODEL), const),
                   pl.BlockSpec((8, D_MODEL), const)),
        out_shape=(jax.ShapeDtypeStruct((SEQ, D_MODEL), F32), jax.ShapeDtypeStruct((SHARD_IN, D_MODEL), F32),
                   jax.ShapeDtypeStruct((8, D_MODEL), F32)),
        scratch_shapes=[pltpu.VMEM((1, D_MODEL), F32), pltpu.VMEM((SHARD_IN, D_MODEL), BF16),
                        pltpu.VMEM((2, SHARD_IN, D_MODEL), BF16), pltpu.VMEM((2, HALF_IN, D_MODEL), BF16),
                        pltpu.VMEM((2, HALF_IN, D_MODEL), BF16),
                        pltpu.VMEM((8, D_MODEL), F32), pltpu.VMEM((N_DEV, 8, D_MODEL), F32),
                        pltpu.SemaphoreType.DMA((N_ICI_SUM_SEMS + 7,)), pltpu.SemaphoreType.DMA((N_ICI_SUM_SEMS + 7,)),
                        pltpu.SemaphoreType.DMA((3,))],
        compiler_params=_params(dimension_semantics=("arbitrary",)),
    )(dproj, w_full, x, dx2, norm_in, dw_in_chip, gslab, gnf, loss_part)


def _dw_in_rs(dproj, h, dw_out_sh):
    tn = 640
    steps = D_PROJ // tn
    forward_step = 2

    def body(a_ref, b_ref, dwo_ref, chip_ref, gwo_ref, dwt, d2d_in, own, d2d, ici, send_sems, recv_sems, local_sems):
        i = pl.program_id(0)
        rs_start, rs_forward, rs_finish = _shard_sum(dwo_ref, own, d2d, ici, send_sems, recv_sems, local_sems)
        pair_send, pair_finish = _chip_sum(dwt, d2d_in, chip_ref, send_sems, recv_sems, local_sems,
                                           N_SHARD_SUM_SEMS, 4)

        @pl.when(i == 0)
        def _():
            rs_start()

        pair_send((i - 1) * tn, i * tn)

        tile = lax.dot_general(a_ref[...], b_ref[...], _TN, preferred_element_type=F32).astype(BF16)
        dwt[pl.ds(pl.multiple_of(i * tn, tn), tn), :] = tile

        @pl.when(i == forward_step)
        def _():
            rs_forward()

        @pl.when(i == steps - 1)
        def _():
            pair_send((steps - 1) * tn, D_PROJ)
            gwo_ref[...] = rs_finish()
            pair_finish()

    return pl.pallas_call(
        body, name="dw_in", grid=(steps,),
        in_specs=[pl.BlockSpec((SEQ, tn), lambda i: (0, i)), pl.BlockSpec(memory_space=pltpu.VMEM),
                  pl.BlockSpec(memory_space=pl.ANY)],
        out_specs=(pl.BlockSpec(memory_space=pl.ANY), pl.BlockSpec((SHARD_OUT, D_MODEL), lambda i: (0, 0))),
        out_shape=(jax.ShapeDtypeStruct((4, SHARD_IN, D_MODEL), BF16), jax.ShapeDtypeStruct((SHARD_OUT, D_MODEL), F32)),
        scratch_shapes=[pltpu.VMEM((D_PROJ, D_MODEL), BF16), pltpu.VMEM((4, SHARD_IN, D_MODEL), BF16),
                        *_shard_sum_scratch(SHARD_OUT),
                        pltpu.SemaphoreType.DMA((N_SHARD_SUM_SEMS + 4,)), pltpu.SemaphoreType.DMA((N_SHARD_SUM_SEMS + 4,)),
                        pltpu.SemaphoreType.DMA((8,))],
        compiler_params=_params(dimension_semantics=("arbitrary",)),
    )(dproj, h, dw_out_sh)


def _matmul_tn(a, b, tn, name):
    k, n = a.shape
    _, m = b.shape

    def body(a_ref, b_ref, o_ref):
        o_ref[...] = lax.dot_general(a_ref[...], b_ref[...], _TN, preferred_element_type=F32).astype(BF16)

    return pl.pallas_call(
        body, name=name, grid=(n // tn,),
        in_specs=[pl.BlockSpec((k, tn), lambda i: (0, i)), pl.BlockSpec(memory_space=pltpu.VMEM)],
        out_specs=pl.BlockSpec((tn, m), lambda i: (i, 0)),
        out_shape=jax.ShapeDtypeStruct((n, m), BF16),
        compiler_params=_params(dimension_semantics=("arbitrary",)),
    )(a, b)


def _adam_all(big_in, big_out, gsum, small, grad_x):
    steps = 4
    tr_in, tr_out = SHARD_IN // steps, SHARD_OUT // steps

    def body(*refs):
        ins, outs = refs[:8 + 1 + 18 + 1], refs[8 + 1 + 18 + 1:]
        i = pl.program_id(0)
        outs[33][...] = ins[27][...]
        for b in range(2):
            w_ref, g_ref, m_ref, v_ref = ins[4 * b:4 * b + 4]
            g = g_ref[...]
            delta, mn, vn = _adamw(w_ref[...], g, m_ref[...], v_ref[...])
            for ref, val in zip(outs[4 * b:4 * b + 4], (g, delta, mn, vn)):
                ref[...] = val

        @pl.when(i == 0)
        def _():
            gsum = ins[8][...]
            idx = _slot(lax.axis_index("x"), lax.axis_index("y"), lax.axis_index("c"))
            cg = jnp.zeros((3, SHARD_CONV), F32)
            for d in range(N_DEV):
                cg = jnp.where(idx == d, gsum[ROW_CONV0:ROW_CONV0 + 3, d * SHARD_CONV:(d + 1) * SHARD_CONV], cg)
            grads = (gsum[ROW_NORM_IN:ROW_NORM_IN + 1], gsum[ROW_SINKS:ROW_SINKS + 1, 0:N_Q_HEADS],
                     gsum[ROW_NORM_CONV:ROW_NORM_CONV + 1], gsum[ROW_NORM_ATTN:ROW_NORM_ATTN + 1],
                     gsum[ROW_NORM_FINAL:ROW_NORM_FINAL + 1], cg)
            for s, g in enumerate(grads):
                at = (slice(None), 0, slice(None)) if s == 5 else (slice(None), slice(None))
                w_ref, m_ref, v_ref = ins[9 + 3 * s:12 + 3 * s]
                delta, mn, vn = _adamw(w_ref[at], g, m_ref[at], v_ref[at])
                for ref, val in zip(outs[8 + 4 * s:12 + 4 * s], (g, delta, mn, vn)):
                    ref[at] = val
            outs[32][...] = gsum[ROW_SINKS:ROW_SINKS + 1, LOSS_LANE:LOSS_LANE + 1]

    const = lambda i: (0, 0)
    rows = lambda i: (i, 0)
    whole = lambda shape: pl.BlockSpec(shape, lambda i: (0,) * len(shape))
    small_shapes = [a.shape for a in small[::3]]
    in_specs = ([pl.BlockSpec((tr_in, D_MODEL), rows)] * 4 + [pl.BlockSpec((tr_out, D_MODEL), rows)] * 4
                + [pl.BlockSpec((8, D_MODEL), const)] + [whole(a.shape) for a in small]
                + [pl.BlockSpec((SEQ // steps, D_MODEL), rows)])
    out_specs = ([pl.BlockSpec((tr_in, D_MODEL), rows)] * 4 + [pl.BlockSpec((tr_out, D_MODEL), rows)] * 4
                 + [whole(s) for s in small_shapes for _ in range(4)] + [pl.BlockSpec((1, 1), const)]
                 + [pl.BlockSpec((SEQ // steps, D_MODEL), rows)])
    out_shape = ([jax.ShapeDtypeStruct((SHARD_IN, D_MODEL), F32)] * 4 + [jax.ShapeDtypeStruct((SHARD_OUT, D_MODEL), F32)] * 4
                 + [jax.ShapeDtypeStruct(s, F32) for s in small_shapes for _ in range(4)]
                 + [jax.ShapeDtypeStruct((1, 1), F32), jax.ShapeDtypeStruct((SEQ, D_MODEL), F32)])
    outs = pl.pallas_call(
        body, name="adam", grid=(steps,), in_specs=in_specs, out_specs=tuple(out_specs), out_shape=tuple(out_shape),
        compiler_params=_params(dimension_semantics=("arbitrary",)),
    )(*big_in, *big_out, gsum, *small, grad_x)
    return outs[0:4], outs[4:8], [outs[8 + 4 * s:12 + 4 * s] for s in range(6)], outs[32], outs[33]


def _rows_first(a):
    return jnp.transpose(a, (1, 0, 2))


def kernel(x, norm_in, w_in, conv_w, attn_sinks, norm_conv_out, norm_attn_out, w_out, norm_final, loss_target, m_norm_in, m_w_in, m_conv_w, m_attn_sinks, m_norm_conv_out, m_norm_attn_out, m_w_out, m_norm_final, v_norm_in, v_w_in, v_conv_w, v_attn_sinks, v_norm_conv_out, v_norm_attn_out, v_w_out, v_norm_final):
    x2d = x.reshape(SEQ, D_MODEL)
    target = loss_target.reshape(SEQ, D_MODEL)
    nf = norm_final.reshape(1, D_MODEL)

    w_in_t, m_w_in_t, v_w_in_t = w_in[0].T, m_w_in[0].T, v_w_in[0].T
    tiles = jnp.asarray(TILE_ORDER, jnp.int32)[2 * lax.axis_index("x") + lax.axis_index("y")]
    w_in_full, h, proj, g_out, conv_full = _gather_in_proj(x2d, norm_in, w_in_t, w_out[0], _rows_first(conv_w), tiles)
    sinks = attn_sinks.reshape(N_Q_HEADS)

    mixed, attn, probs, shares = _mix_fwd(proj, conv_full, sinks, norm_conv_out, norm_attn_out)
    dx2, dx2b, dmixed, gnf, loss_part = _out_proj_loss(mixed, x2d, target, g_out.reshape(D_MIX, D_MODEL), nf)
    dproj, gslab = _mix_bwd(proj, dmixed, attn, probs, shares, conv_full, norm_conv_out, norm_attn_out)
    dw_out = _matmul_tn(mixed, dx2b, 512, "dw_out")
    dw_in_chip, g_w_out = _dw_in_rs(dproj, h, dw_out.reshape(N_DEV, SHARD_OUT, D_MODEL))
    grad_x, g_w_in, gsum = _in_bwd_rs(dproj, w_in_full, x2d, dx2, norm_in, dw_in_chip, gslab, gnf, loss_part)

    small = (norm_in, m_norm_in, v_norm_in, attn_sinks, m_attn_sinks, v_attn_sinks,
             norm_conv_out, m_norm_conv_out, v_norm_conv_out, norm_attn_out, m_norm_attn_out, v_norm_attn_out,
             nf, m_norm_final.reshape(1, D_MODEL), v_norm_final.reshape(1, D_MODEL),
             _rows_first(conv_w), _rows_first(m_conv_w), _rows_first(v_conv_w))
    big_in, big_out, (s_ni, s_sk, s_nc, s_na, s_nf, s_cv), loss, grad_x = _adam_all(
        (w_in_t, g_w_in, m_w_in_t, v_w_in_t), (w_out[0], g_w_out, m_w_out[0], v_w_out[0]), gsum, small, grad_x)

    def leaves(k):
        return (s_ni[k], big_in[k].T[None], jnp.transpose(s_cv[k], (1, 0, 2)), s_sk[k], s_nc[k], s_na[k], big_out[k][None],
                s_nf[k].reshape(D_MODEL))

    return (loss.reshape(()), grad_x.reshape(1, SEQ, D_MODEL), *leaves(0), *leaves(1), *leaves(2), *leaves(3))
```

```python
import jax
import jax.numpy as jnp
from jax import lax
from jax.experimental import pallas as pl
from jax.experimental.pallas import tpu as pltpu

F32 = jnp.float32
BF16 = jnp.bfloat16
MESH = pl.DeviceIdType.MESH

N_DEV = 8
SEQ = 2048
D_MODEL = 1024
D_CONV = 1024
D_ATTN = 1024
D_KV = 128
HEAD_DIM = 64
N_Q_HEADS = 16
N_PAIRS = N_Q_HEADS // 2
PAIRS_PER_KV = N_PAIRS // 2
D_MIX = D_CONV + D_ATTN
D_PROJ = 6400
SHARD_IN = D_PROJ // N_DEV
SHARD_OUT = D_MIX // N_DEV
SHARD_CONV = D_CONV // N_DEV
OFF_CB, OFF_CC, OFF_CU, OFF_GC, OFF_Q, OFF_K, OFF_V, OFF_GA = 0, 1024, 2048, 3072, 4096, 5120, 5248, 5376
BLOCK = 128
N_BLOCKS = SEQ // BLOCK
HALO = 8
CHUNK = 16
N_CHUNKS = BLOCK // CHUNK
RMS_EPS = 1e-5
NEG = -1e30
SCALE = HEAD_DIM ** -0.5
SLOPES = tuple(2.0 ** (-8.0 * (h + 1) / N_Q_HEADS) for h in range(N_Q_HEADS))

ADAM_LR = 0.001
ADAM_B1 = 0.9
ADAM_B2 = 0.999
ADAM_EPS = 1e-08
ADAM_WD = 0.01
ADAM_STEP = 10

ROW_NORM_IN, ROW_NORM_CONV, ROW_NORM_ATTN, ROW_NORM_FINAL, ROW_CONV0, ROW_SINKS = 0, 1, 2, 3, 4, 7
LOSS_LANE = N_Q_HEADS
ACC_NORM_CONV, ACC_NORM_ATTN, ACC_CONV0, N_ACC = 0, 1, 2, 5

VMEM_LIMIT = 56 * 1024 * 1024

_NT = (((1,), (1,)), ((), ()))
_TN = (((0,), (0,)), ((), ()))


def _params(**kw):
    return pltpu.CompilerParams(vmem_limit_bytes=VMEM_LIMIT, **kw)


def _adamw(w, g, m, v):
    m = ADAM_B1 * m + (1.0 - ADAM_B1) * g
    v = ADAM_B2 * v + (1.0 - ADAM_B2) * (g * g)
    m_hat = m / (1.0 - ADAM_B1 ** ADAM_STEP)
    v_hat = v / (1.0 - ADAM_B2 ** ADAM_STEP)
    delta = -ADAM_LR * (m_hat / (jnp.sqrt(v_hat) + ADAM_EPS) + ADAM_WD * w)
    return delta, m, v


def _sigmoid(t):
    return 1.0 / (1.0 + jnp.exp(-t))


def _slot(px, py, pc):
    return 4 * px + 2 * py + pc


HALF_IN = SHARD_IN // 2
N_GATHER_KINDS = 13
W_OUT_KINDS = N_GATHER_KINDS + 7


IN_PROJ_TILE = 640
TILE_ORDER = ((0, 1, 2, 3, 4, 5, 6, 7, 8, 9), (3, 4, 0, 1, 2, 8, 9, 5, 6, 7),
              (5, 6, 0, 1, 7, 8, 9, 2, 3, 4), (8, 9, 3, 4, 5, 6, 7, 0, 1, 2))
TILES_OWN, TILES_NEIGHBOURS = 2, 7


def _gather_in_proj(x, norm_in, w_in_sh, w_out_sh, conv_sh, tiles):
    tn = IN_PROJ_TILE
    steps = D_PROJ // tn
    tm = 256

    def body(tiles_ref, x_hbm, g_ref, win_ref, wout_ref, cv_ref, wt_ref, h_ref, proj_ref, gout_ref, conv_ref,
             gin_ref, gcv_ref, wob_ref, x_ref, send_sems, recv_sems, local_sems):
        p = pl.program_id(0)
        local_sem = local_sems.at[0]
        x, y, c = lax.axis_index("x"), lax.axis_index("y"), lax.axis_index("c")
        me, sibling = (x, y, c), (x, y, 1 - c)
        nx, ny, dg = (1 - x, y, c), (x, 1 - y, c), (1 - x, 1 - y, c)

        def other(dev):
            return (dev[0], dev[1], 1 - dev[2])

        def shard(dev):
            return gin_ref.at[pl.ds(pl.multiple_of(_slot(*dev) * SHARD_IN, 16), SHARD_IN), :]

        def half(dev, h):
            return gin_ref.at[pl.ds(pl.multiple_of(_slot(*dev) * SHARD_IN + h * HALF_IN, 16), HALF_IN), :]

        def rc(ref, k, to):
            return pltpu.make_async_remote_copy(src_ref=ref, dst_ref=ref, send_sem=send_sems.at[k],
                                                recv_sem=recv_sems.at[k], device_id=to, device_id_type=MESH)

        def cv(k, dev, to):
            s = _slot(*dev)
            return pltpu.make_async_remote_copy(src_ref=gcv_ref.at[s], dst_ref=gcv_ref.at[s],
                                                send_sem=send_sems.at[N_GATHER_KINDS + k],
                                                recv_sem=recv_sems.at[N_GATHER_KINDS + k], device_id=to, device_id_type=MESH)

        def own_copies():
            return [rc(shard(me), 0, sibling),
                    rc(half(me, 0), 1, nx), rc(half(me, 1), 2, nx),
                    rc(half(me, 1), 4, ny), rc(half(me, 0), 3, ny),
                    cv(0, me, sibling)] + [cv(1 + j, me, peer) for j, peer in enumerate((nx, ny, dg))]

        def pass_on(dev, h, k_in, k_ici, k_d2d, half=half, base=0):
            rc(half(dev, h), base + k_in, me).wait_recv()
            if k_ici is not None:
                rc(half(dev, h), base + k_ici, ny if dev is nx else nx).start()
            rc(half(dev, h), base + k_d2d, sibling).start()

        def out_half(dev, h):
            return gout_ref.at[_slot(*dev), pl.ds(h * (SHARD_OUT // 2), SHARD_OUT // 2), :]

        def own_out_copies():
            src = lambda h: wob_ref.at[pl.ds(h * (SHARD_OUT // 2), SHARD_OUT // 2), :]

            def send(ref, dst, k, to):
                return pltpu.make_async_remote_copy(src_ref=ref, dst_ref=dst, send_sem=send_sems.at[W_OUT_KINDS + k],
                                                    recv_sem=recv_sems.at[W_OUT_KINDS + k], device_id=to, device_id_type=MESH)

            return [send(wob_ref, gout_ref.at[_slot(*me)], 0, sibling),
                    send(src(0), out_half(me, 0), 1, nx), send(src(1), out_half(me, 1), 2, nx),
                    send(src(1), out_half(me, 1), 4, ny), send(src(0), out_half(me, 0), 3, ny)]

        def own_out_local():
            return pltpu.make_async_copy(wob_ref, gout_ref.at[_slot(*me)], local_sems.at[1])

        @pl.when(p == 0)
        def _():
            gin_ref[pl.ds(pl.multiple_of(_slot(*me) * SHARD_IN, 16), SHARD_IN), :] = win_ref[...].astype(BF16)
            gcv_ref[_slot(*me)] = jnp.zeros((8, SHARD_CONV), F32)
            gcv_ref[_slot(*me), 0:3, :] = cv_ref[:, 0, :]
            for cp in own_copies():
                cp.start()
            wob_ref[...] = wout_ref[...].astype(BF16)
            x_load = pltpu.make_async_copy(x_hbm, x_ref, local_sems.at[2])
            x_load.start()
            x_load.wait()
            for t in range(SEQ // tm):
                xv = x_ref[tm * t:tm * (t + 1), :]
                r = lax.rsqrt(jnp.mean(xv * xv, axis=-1, keepdims=True) + RMS_EPS)
                h_ref[tm * t:tm * (t + 1), :] = (xv * r * g_ref[...]).astype(BF16)
            rc(shard(sibling), 0, me).wait_recv()

        @pl.when(p == TILES_OWN)
        def _():
            for args in ((nx, 0, 1, 5, 7), (ny, 1, 4, 6, 10), (nx, 1, 2, None, 8), (ny, 0, 3, None, 9)):
                pass_on(*args)
            for j, peer in enumerate((nx, ny, dg)):
                cv(1 + j, peer, me).wait_recv()
                cv(4 + j, peer, sibling).start()
            for (dev, h), k in (((nx, 0), 7), ((nx, 1), 8), ((ny, 0), 9), ((ny, 1), 10)):
                rc(half(other(dev), h), k, me).wait_recv()
            own_out_local().start()
            for cp in own_out_copies():
                cp.start()

        @pl.when(p == TILES_NEIGHBOURS)
        def _():
            pass_on(dg, 0, 5, None, 11)
            pass_on(dg, 1, 6, None, 12)
            for (dev, h), k in (((dg, 0), 11), ((dg, 1), 12)):
                rc(half(other(dev), h), k, me).wait_recv()
            pltpu.make_async_copy(gin_ref, wt_ref, local_sem).start()

        @pl.when(p == steps - 2)
        def _():
            for args in ((nx, 0, 1, 5, 7), (ny, 1, 4, 6, 10), (nx, 1, 2, None, 8), (ny, 0, 3, None, 9)):
                pass_on(*args, half=out_half, base=W_OUT_KINDS)

        w = gin_ref[pl.ds(pl.multiple_of(tiles_ref[p] * tn, tn), tn), :]
        proj_ref[...] = lax.dot_general(h_ref[...], w, _NT, preferred_element_type=F32)

        @pl.when(p == steps - 1)
        def _():
            cv(0, sibling, me).wait_recv()
            for j, peer in enumerate((nx, ny, dg)):
                cv(4 + j, other(peer), me).wait_recv()
            for d in range(N_DEV):
                conv_ref[:, d * SHARD_CONV:(d + 1) * SHARD_CONV] = gcv_ref[d]
            relayed = [rc(half(nx, 0), 5, ny), rc(half(ny, 1), 6, nx)]
            relayed += [rc(half(dev, h), k, sibling) for (dev, h), k in
                        (((nx, 0), 7), ((nx, 1), 8), ((ny, 0), 9), ((ny, 1), 10), ((dg, 0), 11), ((dg, 1), 12))]
            relayed += [cv(4 + j, peer, sibling) for j, peer in enumerate((nx, ny, dg))]
            for cp in own_copies() + relayed:
                cp.wait_send()
            pltpu.make_async_copy(gin_ref, wt_ref, local_sem).wait()
            pass_on(dg, 0, 5, None, 11, half=out_half, base=W_OUT_KINDS)
            pass_on(dg, 1, 6, None, 12, half=out_half, base=W_OUT_KINDS)
            rc(gout_ref.at[_slot(*sibling)], W_OUT_KINDS, me).wait_recv()
            out_relayed = [rc(out_half(nx, 0), W_OUT_KINDS + 5, ny), rc(out_half(ny, 1), W_OUT_KINDS + 6, nx)]
            for (dev, h), k in (((nx, 0), 7), ((nx, 1), 8), ((ny, 0), 9), ((ny, 1), 10), ((dg, 0), 11), ((dg, 1), 12)):
                rc(out_half(other(dev), h), W_OUT_KINDS + k, me).wait_recv()
                out_relayed.append(rc(out_half(dev, h), W_OUT_KINDS + k, sibling))
            for cp in own_out_copies() + out_relayed:
                cp.wait_send()
            own_out_local().wait()

    vmem = pl.BlockSpec(memory_space=pltpu.VMEM)
    grid_spec = pltpu.PrefetchScalarGridSpec(
        num_scalar_prefetch=1, grid=(steps,),
        in_specs=[pl.BlockSpec(memory_space=pl.ANY), vmem, vmem, vmem, vmem],
        out_specs=(pl.BlockSpec(memory_space=pl.ANY), vmem,
                   pl.BlockSpec((SEQ, tn), lambda p, tiles_ref: (0, tiles_ref[p])), pl.BlockSpec(memory_space=pl.ANY), vmem),
        scratch_shapes=[pltpu.VMEM((D_PROJ, D_MODEL), BF16), pltpu.VMEM((N_DEV, 8, SHARD_CONV), F32),
                        pltpu.VMEM((SHARD_OUT, D_MODEL), BF16), pltpu.VMEM((SEQ, D_MODEL), F32),
                        pltpu.SemaphoreType.DMA((W_OUT_KINDS + N_GATHER_KINDS,)),
                        pltpu.SemaphoreType.DMA((W_OUT_KINDS + N_GATHER_KINDS,)),
                        pltpu.SemaphoreType.DMA((3,))])
    return pl.pallas_call(
        body, name="gather_in_proj", grid_spec=grid_spec,
        out_shape=(jax.ShapeDtypeStruct((D_PROJ, D_MODEL), BF16), jax.ShapeDtypeStruct((SEQ, D_MODEL), BF16),
                   jax.ShapeDtypeStruct((SEQ, D_PROJ), F32), jax.ShapeDtypeStruct((N_DEV, SHARD_OUT, D_MODEL), BF16),
                   jax.ShapeDtypeStruct((8, D_CONV), F32)),
        compiler_params=_params(dimension_semantics=("arbitrary",)),
    )(tiles, x, norm_in, w_in_sh, w_out_sh, conv_sh)


def _shard_sum(src, own, d2d, ici, send_sems, recv_sems, local_sems, base=0):
    x, y, c = lax.axis_index("x"), lax.axis_index("y"), lax.axis_index("c")
    sibling = (x, y, 1 - c)
    chips = [(x, y), (1 - x, y), (x, 1 - y), (1 - x, 1 - y)]

    def rcopy(s, d, k, to):
        return pltpu.make_async_remote_copy(src_ref=s, dst_ref=d, send_sem=send_sems.at[base + k],
                                            recv_sem=recv_sems.at[base + k], device_id=to, device_id_type=MESH)

    def mine(k):
        return pltpu.make_async_copy(src.at[_slot(*chips[k], c)], own.at[k], local_sems.at[k])

    def to_sibling(k):
        return rcopy(src.at[_slot(*chips[k], 1 - c)], d2d.at[k], k, sibling)

    def to_chip(k):
        return rcopy(own.at[k], ici.at[k - 1], 3 + k, (*chips[k], c))

    def start():
        for k in range(4):
            mine(k).start()
            to_sibling(k).start()

    def forward():
        for k in range(1, 4):
            mine(k).wait()
            to_sibling(k).wait_recv()
            own[k] = (own[k].astype(F32) + d2d[k].astype(F32)).astype(BF16)
            to_chip(k).start()

    def finish():
        mine(0).wait()
        to_sibling(0).wait_recv()
        acc = own[0].astype(F32) + d2d[0].astype(F32)
        for k in range(1, 4):
            to_chip(k).wait_recv()
            acc = acc + ici[k - 1].astype(F32)
        for k in range(4):
            to_sibling(k).wait_send()
        for k in range(1, 4):
            to_chip(k).wait_send()
        return acc

    return start, forward, finish


def _shard_sum_scratch(rows):
    return [pltpu.VMEM((4, rows, D_MODEL), BF16), pltpu.VMEM((4, rows, D_MODEL), BF16),
            pltpu.VMEM((3, rows, D_MODEL), BF16)]


N_SHARD_SUM_SEMS = 7


def _chip_sum(dwt, d2d, out_hbm, send_sems, recv_sems, local_sems, base, local_base):
    x, y, c = lax.axis_index("x"), lax.axis_index("y"), lax.axis_index("c")
    sibling = (x, y, 1 - c)
    chips = [(x, y), (1 - x, y), (x, 1 - y), (1 - x, 1 - y)]

    def shard(s):
        return dwt.at[pl.ds(pl.multiple_of(s * SHARD_IN, 16), SHARD_IN), :]

    def to_sibling(k):
        return pltpu.make_async_remote_copy(src_ref=shard(_slot(*chips[k], 1 - c)), dst_ref=d2d.at[k],
                                            send_sem=send_sems.at[base + k], recv_sem=recv_sems.at[base + k],
                                            device_id=sibling, device_id_type=MESH)

    def save(k):
        return pltpu.make_async_copy(d2d.at[k], out_hbm.at[k], local_sems.at[local_base + k])

    def send(rows_before, rows_done):
        for k in range(4):
            end = (_slot(*chips[k], 1 - c) + 1) * SHARD_IN

            @pl.when((end > rows_before) & (end <= rows_done))
            def _():
                to_sibling(k).start()

    def finish():
        for k in range(4):
            to_sibling(k).wait_recv()
            d2d[k] = (shard(_slot(*chips[k], c))[...].astype(F32) + d2d[k].astype(F32)).astype(BF16)
            save(k).start()
        for k in range(4):
            save(k).wait()
            to_sibling(k).wait_send()

    return send, finish


N_ICI_SUM_SEMS = 6


def _ici_sum(src, own, ici, via, stage, send_sems, recv_sems, local_sems, base=0):
    x, y, c = lax.axis_index("x"), lax.axis_index("y"), lax.axis_index("c")
    nx, ny = (1 - x, y, c), (x, 1 - y, c)
    OWN, NX, NY, DG = range(4)

    def half(ref, h):
        return ref.at[pl.ds(h * HALF_IN, HALF_IN), :]

    def rc(s, d, k, to):
        return pltpu.make_async_remote_copy(src_ref=s, dst_ref=d, send_sem=send_sems.at[base + k],
                                            recv_sem=recv_sems.at[base + k], device_id=to, device_id_type=MESH)

    for_dg_0 = lambda: rc(half(src.at[DG], 0), via.at[0], 0, nx)
    for_dg_1 = lambda: rc(half(src.at[DG], 1), via.at[1], 1, ny)
    for_nx_0 = lambda: rc(half(src.at[NX], 0), half(ici.at[0], 0), 2, nx)
    for_ny_1 = lambda: rc(half(src.at[NY], 1), half(ici.at[1], 1), 3, ny)
    for_ny_0 = lambda: rc(stage.at[0], half(ici.at[1], 0), 4, ny)
    for_nx_1 = lambda: rc(stage.at[1], half(ici.at[0], 1), 5, nx)
    mine = lambda: pltpu.make_async_copy(src.at[OWN], own, local_sems.at[0])
    stage_0 = lambda: pltpu.make_async_copy(half(src.at[NY], 0), stage.at[0], local_sems.at[1])
    stage_1 = lambda: pltpu.make_async_copy(half(src.at[NX], 1), stage.at[1], local_sems.at[2])

    def start():
        for cp in (for_dg_0, for_dg_1, for_nx_0, for_ny_1, stage_0, stage_1, mine):
            cp().start()

    def relay():
        for h, staged, landed, out in ((0, stage_0, for_dg_0, for_ny_0), (1, stage_1, for_dg_1, for_nx_1)):
            staged().wait()
            landed().wait_recv()
            stage[h] = (stage[h].astype(F32) + via[h].astype(F32)).astype(BF16)
            out().start()

    def finish():
        mine().wait()
        for cp in (for_nx_0, for_nx_1, for_ny_1, for_ny_0):
            cp().wait_recv()
        acc = own[...].astype(F32) + ici[0].astype(F32) + ici[1].astype(F32)
        for cp in (for_dg_0, for_dg_1, for_nx_0, for_ny_1, for_ny_0, for_nx_1):
            cp().wait_send()
        return acc

    return start, relay, finish


def _slab_sum(myslab, slabs, send_sems, recv_sems, base):
    x, y, c = lax.axis_index("x"), lax.axis_index("y"), lax.axis_index("c")
    me = _slot(x, y, c)
    peers = [(x, y, 1 - c), (1 - x, y, c), (x, 1 - y, c), (1 - x, 1 - y, c),
             (1 - x, y, 1 - c), (x, 1 - y, 1 - c), (1 - x, 1 - y, 1 - c)]

    def cp(k):
        return pltpu.make_async_remote_copy(src_ref=myslab, dst_ref=slabs.at[me], send_sem=send_sems.at[base + k],
                                            recv_sem=recv_sems.at[base + k], device_id=peers[k], device_id_type=MESH)

    def start():
        slabs[me] = myslab[...]
        for k in range(7):
            cp(k).start()

    def finish():
        for k in range(7):
            cp(k).wait_recv()
        total = slabs[0]
        for d in range(1, N_DEV):
            total = total + slabs[d]
        for k in range(7):
            cp(k).wait_send()
        return total

    return start, finish


def _chunk_rows(r):
    return slice(r * CHUNK, (r + 1) * CHUNK)


def _conv_halo(cch_ref, cuh_ref, n):
    zh = jnp.where(n > 0, cch_ref[...] * cuh_ref[...], 0.0)
    return jnp.concatenate([zh] * (CHUNK // HALO), axis=0)


def _conv_chunk(pj_ref, zhalo, cw, r):
    rows = _chunk_rows(r)
    cc = pj_ref[rows, OFF_CC:OFF_CC + D_CONV]
    cu = pj_ref[rows, OFF_CU:OFF_CU + D_CONV]
    z = cc * cu
    before = _chunk_rows(r - 1)
    zprev = pj_ref[before, OFF_CC:OFF_CC + D_CONV] * pj_ref[before, OFF_CU:OFF_CU + D_CONV] if r > 0 else zhalo
    row = lax.broadcasted_iota(jnp.int32, (CHUNK, D_CONV), 0)
    z1 = jnp.where(row < 1, pltpu.roll(zprev, 1, 0), pltpu.roll(z, 1, 0))
    z2 = jnp.where(row < 2, pltpu.roll(zprev, 2, 0), pltpu.roll(z, 2, 0))
    co = cw[0] * z2 + cw[1] * z1 + cw[2] * z
    return cc, cu, z, z1, z2, co


def _gated_norm(a, gain, t):
    r = lax.rsqrt(jnp.mean(a * a, axis=-1, keepdims=True) + RMS_EPS)
    return a * r * gain * (t * _sigmoid(t))


def _kv_bands(pj, kvp_ref):
    lane = lax.broadcasted_iota(jnp.int32, (2 * BLOCK, D_KV), 1)
    lo = lane < HEAD_DIM

    def bands(prev, cur):
        b = jnp.concatenate([prev, cur], axis=0)
        br = pltpu.roll(b, HEAD_DIM, 1)
        zero = jnp.zeros_like(b)
        return ((jnp.where(lo, b, zero).astype(BF16), jnp.where(lo, zero, br).astype(BF16)),
                (jnp.where(lo, br, zero).astype(BF16), jnp.where(lo, zero, b).astype(BF16)))

    ks = bands(kvp_ref[:, 0:D_KV], pj[:, OFF_K:OFF_K + D_KV])
    vs = bands(kvp_ref[:, D_KV:2 * D_KV], pj[:, OFF_V:OFF_V + D_KV])
    return ks, vs


STACK = PAIRS_PER_KV * BLOCK


def _head(j, i, e):
    return 2 * (PAIRS_PER_KV * j + i) + e


def _pair_cols(j, i, off):
    p = PAIRS_PER_KV * j + i
    return slice(off + 128 * p, off + 128 * (p + 1))


def _fill_attn_bias(bias_scr, first_block):
    qi = lax.broadcasted_iota(jnp.int32, (BLOCK, 2 * BLOCK), 0)
    kj = lax.broadcasted_iota(jnp.int32, (BLOCK, 2 * BLOCK), 1)
    dist = BLOCK + qi - kj
    valid = (dist >= 0) & (dist < BLOCK)
    if first_block:
        valid = valid & (kj >= BLOCK)
    distf = dist.astype(F32)
    for j in range(2):
        for e in range(2):
            for i in range(PAIRS_PER_KV):
                bias_scr[2 * j + e, BLOCK * i:BLOCK * (i + 1), :] = jnp.where(valid, -SLOPES[_head(j, i, e)] * distf, NEG)


def _q_stack(pj, j):
    return jnp.concatenate([(pj[:, _pair_cols(j, i, OFF_Q)] * SCALE).astype(BF16) for i in range(PAIRS_PER_KV)], axis=0)


def _attn_probs(q_stack, kband, bias_ref, sinks):
    s = lax.dot_general(q_stack, kband, _NT, preferred_element_type=F32)
    ones = jnp.ones((128, 128), BF16)
    probs, shares = [], []
    for i, sink in enumerate(sinks):
        rows = slice(BLOCK * i, BLOCK * (i + 1))
        t = s[rows, :] + bias_ref[rows, :]
        m = jnp.broadcast_to(jnp.max(t, axis=-1, keepdims=True), (BLOCK, 128))
        m = jnp.maximum(m, sink)
        p = [jnp.exp(t[:, :128] - m), jnp.exp(t[:, 128:] - m)]
        es = jnp.exp(sink - m)
        total = (jnp.dot(p[0].astype(BF16), ones, preferred_element_type=F32)
                 + jnp.dot(p[1].astype(BF16), ones, preferred_element_type=F32))
        inv = 1.0 / (total + es)
        probs.append(jnp.concatenate([p[0] * inv, p[1] * inv], axis=1))
        shares.append(es * inv)
    return jnp.concatenate(probs, axis=0), jnp.concatenate(shares, axis=0)


def _attn_group(pj, ks, vs, bias_scr, sink_ref, j):
    q_stack = _q_stack(pj, j)
    out, probs, shares = None, [], []
    for e in range(2):
        p, ps = _attn_probs(q_stack, ks[j][e], bias_scr.at[2 * j + e],
                            [sink_ref[_head(j, i, e)] for i in range(PAIRS_PER_KV)])
        p = p.astype(BF16)
        o = jnp.dot(p, vs[j][e], preferred_element_type=F32)
        out = o if out is None else out + o
        probs.append(p)
        shares.append(ps)
    return out, probs, shares


def _mix_fwd(proj, conv_full, sinks, norm_conv, norm_attn):
    def body(pj_ref, kvp_ref, cch_ref, cuh_ref, cw_ref, sink_ref, gc_ref, ga_ref,
             mixed_ref, attn_scr, p_ref, ps_ref, bias_scr):
        n = pl.program_id(0)
        pj = pj_ref

        @pl.when(n == 0)
        def _():
            _fill_attn_bias(bias_scr, first_block=True)

        @pl.when(n == 1)
        def _():
            _fill_attn_bias(bias_scr, first_block=False)

        zhalo = _conv_halo(cch_ref, cuh_ref, n)
        cw = (cw_ref[0:1, :], cw_ref[1:2, :], cw_ref[2:3, :])
        gain_c = gc_ref[...]

        for r in range(N_CHUNKS):
            rows = _chunk_rows(r)
            co = _conv_chunk(pj_ref, zhalo, cw, r)[-1]
            y = _gated_norm(pj_ref[rows, OFF_CB:OFF_CB + D_CONV] * co, gain_c, pj_ref[rows, OFF_GC:OFF_GC + D_CONV])
            mixed_ref[rows, 0:D_CONV] = y.astype(BF16)

        ks, vs = _kv_bands(pj, kvp_ref)
        for j in range(2):
            out, probs, shares = _attn_group(pj, ks, vs, bias_scr, sink_ref, j)
            for e in range(2):
                p_ref[0, 2 * j + e] = probs[e]
                ps_ref[0, 2 * j + e] = shares[e]
            for i in range(PAIRS_PER_KV):
                attn_scr[:, _pair_cols(j, i, 0)] = out[BLOCK * i:BLOCK * (i + 1), :]
        gain_a = ga_ref[...]

        for r in range(N_CHUNKS):
            rows = _chunk_rows(r)
            y = _gated_norm(attn_scr[rows, :], gain_a, pj_ref[rows, OFF_GA:OFF_GA + D_ATTN])
            mixed_ref[rows, D_CONV:D_MIX] = y.astype(BF16)

    per_block = BLOCK // HALO
    return pl.pallas_call(
        body, name="mix_fwd", grid=(N_BLOCKS,),
        in_specs=[
            pl.BlockSpec((BLOCK, D_PROJ), lambda n: (n, 0)),
            pl.BlockSpec((BLOCK, 2 * D_KV), lambda n: (jnp.maximum(n - 1, 0), OFF_K // (2 * D_KV))),
            pl.BlockSpec((HALO, D_CONV), lambda n: (jnp.maximum(n * per_block - 1, 0), OFF_CC // D_CONV)),
            pl.BlockSpec((HALO, D_CONV), lambda n: (jnp.maximum(n * per_block - 1, 0), OFF_CU // D_CONV)),
            pl.BlockSpec((8, D_CONV), lambda n: (0, 0)),
            pl.BlockSpec(memory_space=pltpu.SMEM),
            pl.BlockSpec((1, D_CONV), lambda n: (0, 0)),
            pl.BlockSpec((1, D_ATTN), lambda n: (0, 0)),
        ],
        out_specs=(pl.BlockSpec((BLOCK, D_MIX), lambda n: (n, 0)), pl.BlockSpec((BLOCK, D_ATTN), lambda n: (n, 0)),
                   pl.BlockSpec((1, 4, STACK, 2 * BLOCK), lambda n: (n, 0, 0, 0)),
                   pl.BlockSpec((1, 4, STACK, 128), lambda n: (n, 0, 0, 0))),
        out_shape=(jax.ShapeDtypeStruct((SEQ, D_MIX), BF16), jax.ShapeDtypeStruct((SEQ, D_ATTN), F32),
                   jax.ShapeDtypeStruct((N_BLOCKS, 4, STACK, 2 * BLOCK), BF16),
                   jax.ShapeDtypeStruct((N_BLOCKS, 4, STACK, 128), F32)),
        scratch_shapes=[pltpu.VMEM((4, STACK, 2 * BLOCK), F32)],
        compiler_params=_params(dimension_semantics=("arbitrary",)),
    )(proj, proj, proj, proj, conv_full, sinks, norm_conv, norm_attn)


def _out_proj_loss(mixed, x, target, w_out_full, norm_final):
    tm = 256

    def body(mx_ref, x_ref, t_ref, w_ref, g_ref, dx2_ref, dx2b_ref, dmix_ref, gnf_ref, loss_ref):
        i = pl.program_id(0)
        w = w_ref[...]
        x2 = x_ref[...] + jnp.dot(mx_ref[...], w, preferred_element_type=F32)
        r = lax.rsqrt(jnp.mean(x2 * x2, axis=-1, keepdims=True) + RMS_EPS)
        xn = x2 * r
        g = g_ref[...]
        err = xn * g - t_ref[...]
        part = 0.5 * jnp.sum(jnp.mean(err * err, axis=-1, keepdims=True), axis=0, keepdims=True)
        dy = err * (1.0 / D_MODEL)
        gnf = jnp.sum(dy * xn, axis=0, keepdims=True)
        u = dy * g
        dx2 = r * (u - xn * jnp.mean(u * xn, axis=-1, keepdims=True))
        dx2_ref[...] = dx2
        dx2b = dx2.astype(BF16)
        dx2b_ref[...] = dx2b
        dmix_ref[...] = lax.dot_general(dx2b, w, _NT, preferred_element_type=F32)

        @pl.when(i == 0)
        def _():
            gnf_ref[...] = jnp.zeros_like(gnf_ref)
            loss_ref[...] = jnp.zeros_like(loss_ref)

        gnf_ref[...] += gnf
        loss_ref[...] += jnp.broadcast_to(part, loss_ref.shape)

    return pl.pallas_call(
        body, name="out_proj_loss", grid=(SEQ // tm,),
        in_specs=[pl.BlockSpec((tm, D_MIX), lambda i: (i, 0)), pl.BlockSpec((tm, D_MODEL), lambda i: (i, 0)),
                  pl.BlockSpec((tm, D_MODEL), lambda i: (i, 0)), pl.BlockSpec(memory_space=pltpu.VMEM),
                  pl.BlockSpec((1, D_MODEL), lambda i: (0, 0))],
        out_specs=(pl.BlockSpec((tm, D_MODEL), lambda i: (i, 0)), pl.BlockSpec((tm, D_MODEL), lambda i: (i, 0)),
                   pl.BlockSpec((tm, D_MIX), lambda i: (i, 0)),
                   pl.BlockSpec((1, D_MODEL), lambda i: (0, 0)), pl.BlockSpec((8, 128), lambda i: (0, 0))),
        out_shape=(jax.ShapeDtypeStruct((SEQ, D_MODEL), F32), jax.ShapeDtypeStruct((SEQ, D_MODEL), BF16),
                   jax.ShapeDtypeStruct((SEQ, D_MIX), F32),
                   jax.ShapeDtypeStruct((1, D_MODEL), F32), jax.ShapeDtypeStruct((8, 128), F32)),
        compiler_params=_params(dimension_semantics=("arbitrary",)),
    )(mixed, x, target, w_out_full, norm_final)


def _gated_norm_bwd(a, gain, t, dy):
    r = lax.rsqrt(jnp.mean(a * a, axis=-1, keepdims=True) + RMS_EPS)
    an = a * r
    sg = _sigmoid(t)
    dn = dy * (t * sg)
    dt = dy * (an * gain) * (sg * (1.0 + t * (1.0 - sg)))
    u = dn * gain
    da = r * (u - an * jnp.mean(u * an, axis=-1, keepdims=True))
    return da, dt, dn * an


def _mix_bwd(proj, dmixed, attn, probs, shares, conv_full, norm_conv, norm_attn):
    def body(pj_ref, kvp_ref, cch_ref, cuh_ref, dmx_ref, attn_ref, p_ref, ps_ref, cw_ref, gc_ref, ga_ref,
             dpj_ref, gslab_ref, dattn_scr, nxt_scr, dkv_scr, acc_scr):
        step = pl.program_id(0)
        n = N_BLOCKS - 1 - step
        pj = pj_ref

        @pl.when(step == 0)
        def _():
            gslab_ref[...] = jnp.zeros_like(gslab_ref)
            nxt_scr[...] = jnp.zeros_like(nxt_scr)
            dkv_scr[...] = jnp.zeros_like(dkv_scr)
            acc_scr[...] = jnp.zeros_like(acc_scr)

        zhalo = _conv_halo(cch_ref, cuh_ref, n)
        cw = (cw_ref[0:1, :], cw_ref[1:2, :], cw_ref[2:3, :])
        gain_c = gc_ref[...]
        row = lax.broadcasted_iota(jnp.int32, (CHUNK, D_CONV), 0)

        dco_after = nxt_scr[...]
        for r in reversed(range(N_CHUNKS)):
            rows = _chunk_rows(r)
            cc, cu, z, z1, z2, co = _conv_chunk(pj_ref, zhalo, cw, r)
            cb = pj_ref[rows, OFF_CB:OFF_CB + D_CONV]
            da, dgate, gterm = _gated_norm_bwd(cb * co, gain_c, pj_ref[rows, OFF_GC:OFF_GC + D_CONV],
                                               dmx_ref[rows, 0:D_CONV])
            dpj_ref[rows, OFF_GC:OFF_GC + D_CONV] = dgate.astype(BF16)
            dpj_ref[rows, OFF_CB:OFF_CB + D_CONV] = (da * co).astype(BF16)
            dco = da * cb
            dco1 = jnp.where(row >= CHUNK - 1, pltpu.roll(dco_after, CHUNK - 1, 0), pltpu.roll(dco, CHUNK - 1, 0))
            dco2 = jnp.where(row >= CHUNK - 2, pltpu.roll(dco_after, CHUNK - 2, 0), pltpu.roll(dco, CHUNK - 2, 0))
            dz = cw[2] * dco + cw[1] * dco1 + cw[0] * dco2
            dpj_ref[rows, OFF_CC:OFF_CC + D_CONV] = (dz * cu).astype(BF16)
            dpj_ref[rows, OFF_CU:OFF_CU + D_CONV] = (dz * cc).astype(BF16)
            acc_scr[ACC_NORM_CONV] += gterm
            acc_scr[ACC_CONV0] += dco * z2
            acc_scr[ACC_CONV0 + 1] += dco * z1
            acc_scr[ACC_CONV0 + 2] += dco * z
            dco_after = dco
        nxt_scr[...] = dco_after

        ks, vs = _kv_bands(pj, kvp_ref)
        gain_a = ga_ref[...]

        for r in range(N_CHUNKS):
            rows = _chunk_rows(r)
            da, dgate, gterm = _gated_norm_bwd(attn_ref[rows, :], gain_a, pj_ref[rows, OFF_GA:OFF_GA + D_ATTN],
                                               dmx_ref[rows, D_CONV:D_MIX])
            dpj_ref[rows, OFF_GA:OFF_GA + D_ATTN] = dgate.astype(BF16)
            dattn_scr[rows, :] = da
            acc_scr[ACC_NORM_ATTN] += gterm

        in_lo = lax.broadcasted_iota(jnp.int32, (128, 128), 0) < HEAD_DIM
        half_ones = (jnp.where(in_lo, 1.0, 0.0).astype(BF16), jnp.where(in_lo, 0.0, 1.0).astype(BF16))
        lane_s = lax.broadcasted_iota(jnp.int32, (1, D_MODEL), 1)
        gsink = jnp.zeros((1, D_MODEL), F32)
        dk_t, dv_t = [], []
        for j in range(2):
            q_stack = _q_stack(pj, j)
            do_f = jnp.concatenate([dattn_scr[:, _pair_cols(j, i, 0)] for i in range(PAIRS_PER_KV)], axis=0)
            o_f = jnp.concatenate([attn_ref[:, _pair_cols(j, i, 0)] for i in range(PAIRS_PER_KV)], axis=0)
            prod = (do_f * o_f).astype(BF16)
            deltas = [jnp.dot(prod, half_ones[e], preferred_element_type=F32) for e in range(2)]
            do_b = do_f.astype(BF16)
            q_t, do_t = q_stack.T, do_b.T
            dq, dk_j, dv_j = None, None, None
            for e in range(2):
                p = p_ref[0, 2 * j + e]
                dp = lax.dot_general(do_b, vs[j][e], _NT, preferred_element_type=F32)
                ds = []
                for i in range(PAIRS_PER_KV):
                    rows = slice(BLOCK * i, BLOCK * (i + 1))
                    delta = deltas[e][rows, :]
                    ds.append((p[rows, :].astype(F32) * (dp[rows, :] - jnp.concatenate([delta, delta], axis=1))).astype(BF16))
                    gs_h = -jnp.sum(ps_ref[0, 2 * j + e, rows, 0:1] * delta[:, 0:1], axis=0, keepdims=True)
                    gsink = gsink + jnp.where(lane_s == _head(j, i, e), gs_h, 0.0)
                ds = jnp.concatenate(ds, axis=0)
                t = jnp.dot(ds, ks[j][e], preferred_element_type=F32)
                dq = t if dq is None else dq + t
                half = slice(HEAD_DIM * e, HEAD_DIM * (e + 1))
                a = jnp.dot(q_t[half, :], ds, preferred_element_type=F32)
                b = jnp.dot(do_t[half, :], p, preferred_element_type=F32)
                dk_j = a if dk_j is None else dk_j + a
                dv_j = b if dv_j is None else dv_j + b
            for i in range(PAIRS_PER_KV):
                dpj_ref[:, _pair_cols(j, i, OFF_Q)] = (dq[BLOCK * i:BLOCK * (i + 1), :] * SCALE).astype(BF16)
            dk_t.append(dk_j)
            dv_t.append(dv_j)
        dk = jnp.concatenate(dk_t, axis=0).T
        dv = jnp.concatenate(dv_t, axis=0).T
        dpj_ref[:, OFF_K:OFF_K + D_KV] = (dk[BLOCK:, :] + dkv_scr[:, 0:D_KV]).astype(BF16)
        dpj_ref[:, OFF_V:OFF_V + D_KV] = (dv[BLOCK:, :] + dkv_scr[:, D_KV:2 * D_KV]).astype(BF16)
        dkv_scr[:, 0:D_KV] = dk[:BLOCK, :]
        dkv_scr[:, D_KV:2 * D_KV] = dv[:BLOCK, :]
        gslab_ref[ROW_SINKS:ROW_SINKS + 1, :] += gsink

        @pl.when(step == N_BLOCKS - 1)
        def _():
            for k, slab_row in ((ACC_NORM_CONV, ROW_NORM_CONV), (ACC_NORM_ATTN, ROW_NORM_ATTN), (ACC_CONV0, ROW_CONV0),
                                (ACC_CONV0 + 1, ROW_CONV0 + 1), (ACC_CONV0 + 2, ROW_CONV0 + 2)):
                gslab_ref[slab_row:slab_row + 1, :] = jnp.sum(acc_scr[k], axis=0, keepdims=True)

    per_block = BLOCK // HALO
    last = N_BLOCKS - 1
    return pl.pallas_call(
        body, name="mix_bwd", grid=(N_BLOCKS,),
        in_specs=[
            pl.BlockSpec((BLOCK, D_PROJ), lambda s: (last - s, 0)),
            pl.BlockSpec((BLOCK, 2 * D_KV), lambda s: (jnp.maximum(last - s - 1, 0), OFF_K // (2 * D_KV))),
            pl.BlockSpec((HALO, D_CONV), lambda s: (jnp.maximum((last - s) * per_block - 1, 0), OFF_CC // D_CONV)),
            pl.BlockSpec((HALO, D_CONV), lambda s: (jnp.maximum((last - s) * per_block - 1, 0), OFF_CU // D_CONV)),
            pl.BlockSpec((BLOCK, D_MIX), lambda s: (last - s, 0)),
            pl.BlockSpec((BLOCK, D_ATTN), lambda s: (last - s, 0)),
            pl.BlockSpec((1, 4, STACK, 2 * BLOCK), lambda s: (last - s, 0, 0, 0)),
            pl.BlockSpec((1, 4, STACK, 128), lambda s: (last - s, 0, 0, 0)),
            pl.BlockSpec((8, D_CONV), lambda s: (0, 0)),
            pl.BlockSpec((1, D_CONV), lambda s: (0, 0)),
            pl.BlockSpec((1, D_ATTN), lambda s: (0, 0)),
        ],
        out_specs=(pl.BlockSpec((BLOCK, D_PROJ), lambda s: (last - s, 0)),
                   pl.BlockSpec((8, D_MODEL), lambda s: (0, 0))),
        out_shape=(jax.ShapeDtypeStruct((SEQ, D_PROJ), BF16), jax.ShapeDtypeStruct((8, D_MODEL), F32)),
        scratch_shapes=[pltpu.VMEM((BLOCK, D_ATTN), F32), pltpu.VMEM((CHUNK, D_CONV), F32),
                        pltpu.VMEM((BLOCK, 2 * D_KV), F32), pltpu.VMEM((N_ACC, CHUNK, D_MODEL), F32)],
        compiler_params=_params(dimension_semantics=("arbitrary",)),
    )(proj, proj, proj, proj, dmixed, attn, probs, shares, conv_full, norm_conv, norm_attn)


def _in_bwd_rs(dproj, w_full, x, dx2, norm_in, dw_in_chip, gslab, gnf, loss_part):
    tm = 256
    steps = SEQ // tm
    relay_step = 4

    def body(dp_ref, w_hbm, x_ref, dx2_ref, g_ref, dwi_ref, gs_ref, gnf_ref, lp_ref, gx_ref, gwin_ref, gsum_ref,
             gni_scr, own, ici, via, stage, myslab, slabs, w_ref, send_sems, recv_sems, local_sems):
        i = pl.program_id(0)
        rs_start, rs_relay, rs_finish = _ici_sum(dwi_ref, own, ici, via, stage, send_sems, recv_sems, local_sems)
        slab_start, slab_finish = _slab_sum(myslab, slabs, send_sems, recv_sems, N_ICI_SUM_SEMS)

        @pl.when(i == 0)
        def _():
            gni_scr[...] = jnp.zeros_like(gni_scr)
            rs_start()
            w_load = pltpu.make_async_copy(w_hbm, w_ref, local_sems.at[3])
            w_load.start()
            w_load.wait()

        dh = jnp.dot(dp_ref[...], w_ref[...], preferred_element_type=F32)
        xv = x_ref[...]
        r = lax.rsqrt(jnp.mean(xv * xv, axis=-1, keepdims=True) + RMS_EPS)
        xn = xv * r
        u = dh * g_ref[...]
        gx_ref[...] = dx2_ref[...] + r * (u - xn * jnp.mean(u * xn, axis=-1, keepdims=True))
        gni_scr[...] += jnp.sum(dh * xn, axis=0, keepdims=True)

        @pl.when(i == relay_step)
        def _():
            rs_relay()

        @pl.when(i == steps - 1)
        def _():
            row = lax.broadcasted_iota(jnp.int32, (8, D_MODEL), 0)
            lane = lax.broadcasted_iota(jnp.int32, (8, D_MODEL), 1)
            slab = jnp.where(row == ROW_NORM_IN, gni_scr[...], jnp.where(row == ROW_NORM_FINAL, gnf_ref[...], gs_ref[...]))
            myslab[...] = jnp.where((row == ROW_SINKS) & (lane == LOSS_LANE), lp_ref[0:1, 0:1], slab)
            slab_start()
            gwin_ref[...] = rs_finish()
            gsum_ref[...] = slab_finish()

    const = lambda i: (0, 0)
    return pl.pallas_call(
        body, name="in_bwd", grid=(steps,),
        in_specs=[pl.BlockSpec((tm, D_PROJ), lambda i: (i, 0)), pl.BlockSpec(memory_space=pl.ANY),
                  pl.BlockSpec((tm, D_MODEL), lambda i: (i, 0)), pl.BlockSpec((tm, D_MODEL), lambda i: (i, 0)),
                  pl.BlockSpec((1, D_MODEL), const), pl.BlockSpec(memory_space=pl.ANY),
                  pl.BlockSpec((8, D_MODEL), const), pl.BlockSpec((1, D_MODEL), const), pl.BlockSpec((8, 128), const)],
        out_specs=(pl.BlockSpec((tm, D_MODEL), lambda i: (i, 0)), pl.BlockSpec((SHARD_IN, D_MODEL), const),
                   pl.BlockSpec((8, D_MODEL), const)),
        out_shape=(jax.ShapeDtypeStruct((SEQ, D_MODEL), F32), jax.ShapeDtypeStruct((SHARD_IN, D_MODEL), F32),
                   jax.ShapeDtypeStruct((8, D_MODEL), F32)),
        scratch_shapes=[pltpu.VMEM((1, D_MODEL), F32), pltpu.VMEM((SHARD_IN, D_MODEL), BF16),
                        pltpu.VMEM((2, SHARD_IN, D_MODEL), BF16), pltpu.VMEM((2, HALF_IN, D_MODEL), BF16),
                        pltpu.VMEM((2, HALF_IN, D_MODEL), BF16),
                        pltpu.VMEM((8, D_MODEL), F32), pltpu.VMEM((N_DEV, 8, D_MODEL), F32),
                        pltpu.VMEM((D_PROJ, D_MODEL), BF16),
                        pltpu.SemaphoreType.DMA((N_ICI_SUM_SEMS + 7,)), pltpu.SemaphoreType.DMA((N_ICI_SUM_SEMS + 7,)),
                        pltpu.SemaphoreType.DMA((4,))],
        compiler_params=_params(dimension_semantics=("arbitrary",)),
    )(dproj, w_full, x, dx2, norm_in, dw_in_chip, gslab, gnf, loss_part)


def _dw_in_rs(dproj, h, dw_out_sh):
    tn = 640
    steps = D_PROJ // tn
    forward_step = 2

    def body(a_ref, b_ref, dwo_ref, chip_ref, gwo_ref, dwt, d2d_in, own, d2d, ici, send_sems, recv_sems, local_sems):
        i = pl.program_id(0)
        rs_start, rs_forward, rs_finish = _shard_sum(dwo_ref, own, d2d, ici, send_sems, recv_sems, local_sems)
        pair_send, pair_finish = _chip_sum(dwt, d2d_in, chip_ref, send_sems, recv_sems, local_sems,
                                           N_SHARD_SUM_SEMS, 4)

        @pl.when(i == 0)
        def _():
            rs_start()

        pair_send((i - 1) * tn, i * tn)

        tile = lax.dot_general(a_ref[...], b_ref[...], _TN, preferred_element_type=F32).astype(BF16)
        dwt[pl.ds(pl.multiple_of(i * tn, tn), tn), :] = tile

        @pl.when(i == forward_step)
        def _():
            rs_forward()

        @pl.when(i == steps - 1)
        def _():
            pair_send((steps - 1) * tn, D_PROJ)
            gwo_ref[...] = rs_finish()
            pair_finish()

    return pl.pallas_call(
        body, name="dw_in", grid=(steps,),
        in_specs=[pl.BlockSpec((SEQ, tn), lambda i: (0, i)), pl.BlockSpec(memory_space=pltpu.VMEM),
                  pl.BlockSpec(memory_space=pl.ANY)],
        out_specs=(pl.BlockSpec(memory_space=pl.ANY), pl.BlockSpec((SHARD_OUT, D_MODEL), lambda i: (0, 0))),
        out_shape=(jax.ShapeDtypeStruct((4, SHARD_IN, D_MODEL), BF16), jax.ShapeDtypeStruct((SHARD_OUT, D_MODEL), F32)),
        scratch_shapes=[pltpu.VMEM((D_PROJ, D_MODEL), BF16), pltpu.VMEM((4, SHARD_IN, D_MODEL), BF16),
                        *_shard_sum_scratch(SHARD_OUT),
                        pltpu.SemaphoreType.DMA((N_SHARD_SUM_SEMS + 4,)), pltpu.SemaphoreType.DMA((N_SHARD_SUM_SEMS + 4,)),
                        pltpu.SemaphoreType.DMA((8,))],
        compiler_params=_params(dimension_semantics=("arbitrary",)),
    )(dproj, h, dw_out_sh)


def _matmul_tn(a, b, tn, name):
    k, n = a.shape
    _, m = b.shape

    def body(a_ref, b_ref, o_ref):
        o_ref[...] = lax.dot_general(a_ref[...], b_ref[...], _TN, preferred_element_type=F32).astype(BF16)

    return pl.pallas_call(
        body, name=name, grid=(n // tn,),
        in_specs=[pl.BlockSpec((k, tn), lambda i: (0, i)), pl.BlockSpec(memory_space=pltpu.VMEM)],
        out_specs=pl.BlockSpec((tn, m), lambda i: (i, 0)),
        out_shape=jax.ShapeDtypeStruct((n, m), BF16),
        compiler_params=_params(dimension_semantics=("arbitrary",)),
    )(a, b)


def _adam_all(big_in, big_out, gsum, small, grad_x):
    steps = 4
    tr_in, tr_out = SHARD_IN // steps, SHARD_OUT // steps

    def body(*refs):
        ins, outs = refs[:8 + 1 + 18 + 1], refs[8 + 1 + 18 + 1:]
        i = pl.program_id(0)
        outs[33][...] = ins[27][...]
        for b in range(2):
            w_ref, g_ref, m_ref, v_ref = ins[4 * b:4 * b + 4]
            g = g_ref[...]
            delta, mn, vn = _adamw(w_ref[...], g, m_ref[...], v_ref[...])
            for ref, val in zip(outs[4 * b:4 * b + 4], (g, delta, mn, vn)):
                ref[...] = val

        @pl.when(i == 0)
        def _():
            gsum = ins[8][...]
            idx = _slot(lax.axis_index("x"), lax.axis_index("y"), lax.axis_index("c"))
            cg = jnp.zeros((3, SHARD_CONV), F32)
            for d in range(N_DEV):
                cg = jnp.where(idx == d, gsum[ROW_CONV0:ROW_CONV0 + 3, d * SHARD_CONV:(d + 1) * SHARD_CONV], cg)
            grads = (gsum[ROW_NORM_IN:ROW_NORM_IN + 1], gsum[ROW_SINKS:ROW_SINKS + 1, 0:N_Q_HEADS],
                     gsum[ROW_NORM_CONV:ROW_NORM_CONV + 1], gsum[ROW_NORM_ATTN:ROW_NORM_ATTN + 1],
                     gsum[ROW_NORM_FINAL:ROW_NORM_FINAL + 1], cg)
            for s, g in enumerate(grads):
                at = (slice(None), 0, slice(None)) if s == 5 else (slice(None), slice(None))
                w_ref, m_ref, v_ref = ins[9 + 3 * s:12 + 3 * s]
                delta, mn, vn = _adamw(w_ref[at], g, m_ref[at], v_ref[at])
                for ref, val in zip(outs[8 + 4 * s:12 + 4 * s], (g, delta, mn, vn)):
                    ref[at] = val
            outs[32][...] = gsum[ROW_SINKS:ROW_SINKS + 1, LOSS_LANE:LOSS_LANE + 1]

    const = lambda i: (0, 0)
    rows = lambda i: (i, 0)
    whole = lambda shape: pl.BlockSpec(shape, lambda i: (0,) * len(shape))
    small_shapes = [a.shape for a in small[::3]]
    in_specs = ([pl.BlockSpec((tr_in, D_MODEL), rows)] * 4 + [pl.BlockSpec((tr_out, D_MODEL), rows)] * 4
                + [pl.BlockSpec((8, D_MODEL), const)] + [whole(a.shape) for a in small]
                + [pl.BlockSpec((SEQ // steps, D_MODEL), rows)])
    out_specs = ([pl.BlockSpec((tr_in, D_MODEL), rows)] * 4 + [pl.BlockSpec((tr_out, D_MODEL), rows)] * 4
                 + [whole(s) for s in small_shapes for _ in range(4)] + [pl.BlockSpec((1, 1), const)]
                 + [pl.BlockSpec((SEQ // steps, D_MODEL), rows)])
    out_shape = ([jax.ShapeDtypeStruct((SHARD_IN, D_MODEL), F32)] * 4 + [jax.ShapeDtypeStruct((SHARD_OUT, D_MODEL), F32)] * 4
                 + [jax.ShapeDtypeStruct(s, F32) for s in small_shapes for _ in range(4)]
                 + [jax.ShapeDtypeStruct((1, 1), F32), jax.ShapeDtypeStruct((SEQ, D_MODEL), F32)])
    outs = pl.pallas_call(
        body, name="adam", grid=(steps,), in_specs=in_specs, out_specs=tuple(out_specs), out_shape=tuple(out_shape),
        compiler_params=_params(dimension_semantics=("arbitrary",)),
    )(*big_in, *big_out, gsum, *small, grad_x)
    return outs[0:4], outs[4:8], [outs[8 + 4 * s:12 + 4 * s] for s in range(6)], outs[32], outs[33]


def _rows_first(a):
    return jnp.transpose(a, (1, 0, 2))


def kernel(x, norm_in, w_in, conv_w, attn_sinks, norm_conv_out, norm_attn_out, w_out, norm_final, loss_target, m_norm_in, m_w_in, m_conv_w, m_attn_sinks, m_norm_conv_out, m_norm_attn_out, m_w_out, m_norm_final, v_norm_in, v_w_in, v_conv_w, v_attn_sinks, v_norm_conv_out, v_norm_attn_out, v_w_out, v_norm_final):
    x2d = x.reshape(SEQ, D_MODEL)
    target = loss_target.reshape(SEQ, D_MODEL)
    nf = norm_final.reshape(1, D_MODEL)

    w_in_t, m_w_in_t, v_w_in_t = w_in[0].T, m_w_in[0].T, v_w_in[0].T
    tiles = jnp.asarray(TILE_ORDER, jnp.int32)[2 * lax.axis_index("x") + lax.axis_index("y")]
    w_in_full, h, proj, g_out, conv_full = _gather_in_proj(x2d, norm_in, w_in_t, w_out[0], _rows_first(conv_w), tiles)
    sinks = attn_sinks.reshape(N_Q_HEADS)

    mixed, attn, probs, shares = _mix_fwd(proj, conv_full, sinks, norm_conv_out, norm_attn_out)
    dx2, dx2b, dmixed, gnf, loss_part = _out_proj_loss(mixed, x2d, target, g_out.reshape(D_MIX, D_MODEL), nf)
    dproj, gslab = _mix_bwd(proj, dmixed, attn, probs, shares, conv_full, norm_conv_out, norm_attn_out)
    dw_out = _matmul_tn(mixed, dx2b, 512, "dw_out")
    dw_in_chip, g_w_out = _dw_in_rs(dproj, h, dw_out.reshape(N_DEV, SHARD_OUT, D_MODEL))
    grad_x, g_w_in, gsum = _in_bwd_rs(dproj, w_in_full, x2d, dx2, norm_in, dw_in_chip, gslab, gnf, loss_part)

    small = (norm_in, m_norm_in, v_norm_in, attn_sinks, m_attn_sinks, v_attn_sinks,
             norm_conv_out, m_norm_conv_out, v_norm_conv_out, norm_attn_out, m_norm_attn_out, v_norm_attn_out,
             nf, m_norm_final.reshape(1, D_MODEL), v_norm_final.reshape(1, D_MODEL),
             _rows_first(conv_w), _rows_first(m_conv_w), _rows_first(v_conv_w))
    big_in, big_out, (s_ni, s_sk, s_nc, s_na, s_nf, s_cv), loss, grad_x = _adam_all(
        (w_in_t, g_w_in, m_w_in_t, v_w_in_t), (w_out[0], g_w_out, m_w_out[0], v_w_out[0]), gsum, small, grad_x)

    def leaves(k):
        return (s_ni[k], big_in[k].T[None], jnp.transpose(s_cv[k], (1, 0, 2)), s_sk[k], s_nc[k], s_na[k], big_out[k][None],
                s_nf[k].reshape(D_MODEL))

    return (loss.reshape(()), grad_x.reshape(1, SEQ, D_MODEL), *leaves(0), *leaves(1), *leaves(2), *leaves(3))
```

```python
import jax
import jax.numpy as jnp
from jax import lax
from jax.experimental import pallas as pl
from jax.experimental.pallas import tpu as pltpu

F32 = jnp.float32
BF16 = jnp.bfloat16
MESH = pl.DeviceIdType.MESH

N_DEV = 8
SEQ = 2048
D_MODEL = 1024
D_CONV = 1024
D_ATTN = 1024
D_KV = 128
HEAD_DIM = 64
N_Q_HEADS = 16
N_PAIRS = N_Q_HEADS // 2
PAIRS_PER_KV = N_PAIRS // 2
D_MIX = D_CONV + D_ATTN
D_PROJ = 6400
SHARD_IN = D_PROJ // N_DEV
SHARD_OUT = D_MIX // N_DEV
SHARD_CONV = D_CONV // N_DEV
OFF_CB, OFF_CC, OFF_CU, OFF_GC, OFF_Q, OFF_K, OFF_V, OFF_GA = 0, 1024, 2048, 3072, 4096, 5120, 5248, 5376
BLOCK = 128
N_BLOCKS = SEQ // BLOCK
HALO = 8
CHUNK = 16
N_CHUNKS = BLOCK // CHUNK
RMS_EPS = 1e-5
NEG = -1e30
SCALE = HEAD_DIM ** -0.5
SLOPES = tuple(2.0 ** (-8.0 * (h + 1) / N_Q_HEADS) for h in range(N_Q_HEADS))

ADAM_LR = 0.001
ADAM_B1 = 0.9
ADAM_B2 = 0.999
ADAM_EPS = 1e-08
ADAM_WD = 0.01
ADAM_STEP = 10

ROW_NORM_IN, ROW_NORM_CONV, ROW_NORM_ATTN, ROW_NORM_FINAL, ROW_CONV0, ROW_SINKS = 0, 1, 2, 3, 4, 7
LOSS_LANE = N_Q_HEADS
ACC_NORM_CONV, ACC_NORM_ATTN, ACC_CONV0, N_ACC = 0, 1, 2, 5

VMEM_LIMIT = 56 * 1024 * 1024

_NT = (((1,), (1,)), ((), ()))
_TN = (((0,), (0,)), ((), ()))


def _params(**kw):
    return pltpu.CompilerParams(vmem_limit_bytes=VMEM_LIMIT, **kw)


def _adamw(w, g, m, v):
    m = ADAM_B1 * m + (1.0 - ADAM_B1) * g
    v = ADAM_B2 * v + (1.0 - ADAM_B2) * (g * g)
    m_hat = m / (1.0 - ADAM_B1 ** ADAM_STEP)
    v_hat = v / (1.0 - ADAM_B2 ** ADAM_STEP)
    delta = -ADAM_LR * (m_hat / (jnp.sqrt(v_hat) + ADAM_EPS) + ADAM_WD * w)
    return delta, m, v


def _sigmoid(t):
    return 1.0 / (1.0 + jnp.exp(-t))


def _slot(px, py, pc):
    return 4 * px + 2 * py + pc


HALF_IN = SHARD_IN // 2
N_GATHER_KINDS = 13
W_OUT_KINDS = N_GATHER_KINDS + 7


IN_PROJ_TILE = 640
TILE_ORDER = ((0, 1, 2, 3, 4, 5, 6, 7, 8, 9), (3, 4, 0, 1, 2, 8, 9, 5, 6, 7),
              (5, 6, 0, 1, 7, 8, 9, 2, 3, 4), (8, 9, 3, 4, 5, 6, 7, 0, 1, 2))
TILES_OWN, TILES_NEIGHBOURS = 2, 7


def _tile(table_ref, p):
    chip = 2 * lax.axis_index("x") + lax.axis_index("y")
    return table_ref[chip * len(TILE_ORDER[0]) + p]


def _gather_in_proj(x, norm_in, w_in_sh, w_out_sh, conv_sh, tiles):
    tn = IN_PROJ_TILE
    steps = D_PROJ // tn
    tm = 256

    def body(tiles_ref, x_hbm, g_ref, win_ref, wout_ref, cv_ref, wt_ref, h_ref, proj_ref, gout_ref, conv_ref,
             gin_ref, gcv_ref, wob_ref, x_ref, send_sems, recv_sems, local_sems):
        p = pl.program_id(0)
        local_sem = local_sems.at[0]
        x, y, c = lax.axis_index("x"), lax.axis_index("y"), lax.axis_index("c")
        me, sibling = (x, y, c), (x, y, 1 - c)
        nx, ny, dg = (1 - x, y, c), (x, 1 - y, c), (1 - x, 1 - y, c)

        def other(dev):
            return (dev[0], dev[1], 1 - dev[2])

        def shard(dev):
            return gin_ref.at[pl.ds(pl.multiple_of(_slot(*dev) * SHARD_IN, 16), SHARD_IN), :]

        def half(dev, h):
            return gin_ref.at[pl.ds(pl.multiple_of(_slot(*dev) * SHARD_IN + h * HALF_IN, 16), HALF_IN), :]

        def rc(ref, k, to):
            return pltpu.make_async_remote_copy(src_ref=ref, dst_ref=ref, send_sem=send_sems.at[k],
                                                recv_sem=recv_sems.at[k], device_id=to, device_id_type=MESH)

        def cv(k, dev, to):
            s = _slot(*dev)
            return pltpu.make_async_remote_copy(src_ref=gcv_ref.at[s], dst_ref=gcv_ref.at[s],
                                                send_sem=send_sems.at[N_GATHER_KINDS + k],
                                                recv_sem=recv_sems.at[N_GATHER_KINDS + k], device_id=to, device_id_type=MESH)

        def own_copies():
            return [rc(shard(me), 0, sibling),
                    rc(half(me, 0), 1, nx), rc(half(me, 1), 2, nx),
                    rc(half(me, 1), 4, ny), rc(half(me, 0), 3, ny),
                    cv(0, me, sibling)] + [cv(1 + j, me, peer) for j, peer in enumerate((nx, ny, dg))]

        def pass_on(dev, h, k_in, k_ici, k_d2d, half=half, base=0):
            rc(half(dev, h), base + k_in, me).wait_recv()
            if k_ici is not None:
                rc(half(dev, h), base + k_ici, ny if dev is nx else nx).start()
            rc(half(dev, h), base + k_d2d, sibling).start()

        def out_half(dev, h):
            return gout_ref.at[_slot(*dev), pl.ds(h * (SHARD_OUT // 2), SHARD_OUT // 2), :]

        def own_out_copies():
            src = lambda h: wob_ref.at[pl.ds(h * (SHARD_OUT // 2), SHARD_OUT // 2), :]

            def send(ref, dst, k, to):
                return pltpu.make_async_remote_copy(src_ref=ref, dst_ref=dst, send_sem=send_sems.at[W_OUT_KINDS + k],
                                                    recv_sem=recv_sems.at[W_OUT_KINDS + k], device_id=to, device_id_type=MESH)

            return [send(wob_ref, gout_ref.at[_slot(*me)], 0, sibling),
                    send(src(0), out_half(me, 0), 1, nx), send(src(1), out_half(me, 1), 2, nx),
                    send(src(1), out_half(me, 1), 4, ny), send(src(0), out_half(me, 0), 3, ny)]

        def own_out_local():
            return pltpu.make_async_copy(wob_ref, gout_ref.at[_slot(*me)], local_sems.at[1])

        @pl.when(p == 0)
        def _():
            gin_ref[pl.ds(pl.multiple_of(_slot(*me) * SHARD_IN, 16), SHARD_IN), :] = win_ref[...].astype(BF16)
            gcv_ref[_slot(*me)] = jnp.zeros((8, SHARD_CONV), F32)
            gcv_ref[_slot(*me), 0:3, :] = cv_ref[:, 0, :]
            for cp in own_copies():
                cp.start()
            wob_ref[...] = wout_ref[...].astype(BF16)
            x_load = pltpu.make_async_copy(x_hbm, x_ref, local_sems.at[2])
            x_load.start()
            x_load.wait()
            for t in range(SEQ // tm):
                xv = x_ref[tm * t:tm * (t + 1), :]
                r = lax.rsqrt(jnp.mean(xv * xv, axis=-1, keepdims=True) + RMS_EPS)
                h_ref[tm * t:tm * (t + 1), :] = (xv * r * g_ref[...]).astype(BF16)
            rc(shard(sibling), 0, me).wait_recv()

        @pl.when(p == TILES_OWN)
        def _():
            for args in ((nx, 0, 1, 5, 7), (ny, 1, 4, 6, 10), (nx, 1, 2, None, 8), (ny, 0, 3, None, 9)):
                pass_on(*args)
            for j, peer in enumerate((nx, ny, dg)):
                cv(1 + j, peer, me).wait_recv()
                cv(4 + j, peer, sibling).start()
            for (dev, h), k in (((nx, 0), 7), ((nx, 1), 8), ((ny, 0), 9), ((ny, 1), 10)):
                rc(half(other(dev), h), k, me).wait_recv()
            own_out_local().start()
            for cp in own_out_copies():
                cp.start()

        @pl.when(p == TILES_NEIGHBOURS)
        def _():
            pass_on(dg, 0, 5, None, 11)
            pass_on(dg, 1, 6, None, 12)
            for (dev, h), k in (((dg, 0), 11), ((dg, 1), 12)):
                rc(half(other(dev), h), k, me).wait_recv()
            pltpu.make_async_copy(gin_ref, wt_ref, local_sem).start()

        @pl.when(p == steps - 2)
        def _():
            for args in ((nx, 0, 1, 5, 7), (ny, 1, 4, 6, 10), (nx, 1, 2, None, 8), (ny, 0, 3, None, 9)):
                pass_on(*args, half=out_half, base=W_OUT_KINDS)

        w = gin_ref[pl.ds(pl.multiple_of(_tile(tiles_ref, p) * tn, tn), tn), :]
        proj_ref[...] = lax.dot_general(h_ref[...], w, _NT, preferred_element_type=F32)

        @pl.when(p == steps - 1)
        def _():
            cv(0, sibling, me).wait_recv()
            for j, peer in enumerate((nx, ny, dg)):
                cv(4 + j, other(peer), me).wait_recv()
            for d in range(N_DEV):
                conv_ref[:, d * SHARD_CONV:(d + 1) * SHARD_CONV] = gcv_ref[d]
            relayed = [rc(half(nx, 0), 5, ny), rc(half(ny, 1), 6, nx)]
            relayed += [rc(half(dev, h), k, sibling) for (dev, h), k in
                        (((nx, 0), 7), ((nx, 1), 8), ((ny, 0), 9), ((ny, 1), 10), ((dg, 0), 11), ((dg, 1), 12))]
            relayed += [cv(4 + j, peer, sibling) for j, peer in enumerate((nx, ny, dg))]
            for cp in own_copies() + relayed:
                cp.wait_send()
            pltpu.make_async_copy(gin_ref, wt_ref, local_sem).wait()
            pass_on(dg, 0, 5, None, 11, half=out_half, base=W_OUT_KINDS)
            pass_on(dg, 1, 6, None, 12, half=out_half, base=W_OUT_KINDS)
            rc(gout_ref.at[_slot(*sibling)], W_OUT_KINDS, me).wait_recv()
            out_relayed = [rc(out_half(nx, 0), W_OUT_KINDS + 5, ny), rc(out_half(ny, 1), W_OUT_KINDS + 6, nx)]
            for (dev, h), k in (((nx, 0), 7), ((nx, 1), 8), ((ny, 0), 9), ((ny, 1), 10), ((dg, 0), 11), ((dg, 1), 12)):
                rc(out_half(other(dev), h), W_OUT_KINDS + k, me).wait_recv()
                out_relayed.append(rc(out_half(dev, h), W_OUT_KINDS + k, sibling))
            for cp in own_out_copies() + out_relayed:
                cp.wait_send()
            own_out_local().wait()

    vmem = pl.BlockSpec(memory_space=pltpu.VMEM)
    grid_spec = pltpu.PrefetchScalarGridSpec(
        num_scalar_prefetch=1, grid=(steps,),
        in_specs=[pl.BlockSpec(memory_space=pl.ANY), vmem, vmem, vmem, vmem],
        out_specs=(pl.BlockSpec(memory_space=pl.ANY), vmem,
                   pl.BlockSpec((SEQ, tn), lambda p, tiles_ref: (0, _tile(tiles_ref, p))),
                   pl.BlockSpec(memory_space=pl.ANY), vmem),
        scratch_shapes=[pltpu.VMEM((D_PROJ, D_MODEL), BF16), pltpu.VMEM((N_DEV, 8, SHARD_CONV), F32),
                        pltpu.VMEM((SHARD_OUT, D_MODEL), BF16), pltpu.VMEM((SEQ, D_MODEL), F32),
                        pltpu.SemaphoreType.DMA((W_OUT_KINDS + N_GATHER_KINDS,)),
                        pltpu.SemaphoreType.DMA((W_OUT_KINDS + N_GATHER_KINDS,)),
                        pltpu.SemaphoreType.DMA((3,))])
    return pl.pallas_call(
        body, name="gather_in_proj", grid_spec=grid_spec,
        out_shape=(jax.ShapeDtypeStruct((D_PROJ, D_MODEL), BF16), jax.ShapeDtypeStruct((SEQ, D_MODEL), BF16),
                   jax.ShapeDtypeStruct((SEQ, D_PROJ), F32), jax.ShapeDtypeStruct((N_DEV, SHARD_OUT, D_MODEL), BF16),
                   jax.ShapeDtypeStruct((8, D_CONV), F32)),
        compiler_params=_params(dimension_semantics=("arbitrary",)),
    )(tiles, x, norm_in, w_in_sh, w_out_sh, conv_sh)


def _shard_sum(src, own, d2d, ici, send_sems, recv_sems, local_sems, base=0):
    x, y, c = lax.axis_index("x"), lax.axis_index("y"), lax.axis_index("c")
    sibling = (x, y, 1 - c)
    chips = [(x, y), (1 - x, y), (x, 1 - y), (1 - x, 1 - y)]

    def rcopy(s, d, k, to):
        return pltpu.make_async_remote_copy(src_ref=s, dst_ref=d, send_sem=send_sems.at[base + k],
                                            recv_sem=recv_sems.at[base + k], device_id=to, device_id_type=MESH)

    def mine(k):
        return pltpu.make_async_copy(src.at[_slot(*chips[k], c)], own.at[k], local_sems.at[k])

    def to_sibling(k):
        return rcopy(src.at[_slot(*chips[k], 1 - c)], d2d.at[k], k, sibling)

    def to_chip(k):
        return rcopy(own.at[k], ici.at[k - 1], 3 + k, (*chips[k], c))

    def start():
        for k in range(4):
            mine(k).start()
            to_sibling(k).start()

    def forward():
        for k in range(1, 4):
            mine(k).wait()
            to_sibling(k).wait_recv()
            own[k] = (own[k].astype(F32) + d2d[k].astype(F32)).astype(BF16)
            to_chip(k).start()

    def finish():
        mine(0).wait()
        to_sibling(0).wait_recv()
        acc = own[0].astype(F32) + d2d[0].astype(F32)
        for k in range(1, 4):
            to_chip(k).wait_recv()
            acc = acc + ici[k - 1].astype(F32)
        for k in range(4):
            to_sibling(k).wait_send()
        for k in range(1, 4):
            to_chip(k).wait_send()
        return acc

    return start, forward, finish


def _shard_sum_scratch(rows):
    return [pltpu.VMEM((4, rows, D_MODEL), BF16), pltpu.VMEM((4, rows, D_MODEL), BF16),
            pltpu.VMEM((3, rows, D_MODEL), BF16)]


N_SHARD_SUM_SEMS = 7


def _chip_sum(dwt, d2d, out_hbm, send_sems, recv_sems, local_sems, base, local_base):
    x, y, c = lax.axis_index("x"), lax.axis_index("y"), lax.axis_index("c")
    sibling = (x, y, 1 - c)
    chips = [(x, y), (1 - x, y), (x, 1 - y), (1 - x, 1 - y)]

    def shard(s):
        return dwt.at[pl.ds(pl.multiple_of(s * SHARD_IN, 16), SHARD_IN), :]

    def to_sibling(k):
        return pltpu.make_async_remote_copy(src_ref=shard(_slot(*chips[k], 1 - c)), dst_ref=d2d.at[k],
                                            send_sem=send_sems.at[base + k], recv_sem=recv_sems.at[base + k],
                                            device_id=sibling, device_id_type=MESH)

    def save(k):
        return pltpu.make_async_copy(d2d.at[k], out_hbm.at[k], local_sems.at[local_base + k])

    def send(rows_before, rows_done):
        for k in range(4):
            end = (_slot(*chips[k], 1 - c) + 1) * SHARD_IN

            @pl.when((end > rows_before) & (end <= rows_done))
            def _():
                to_sibling(k).start()

    def finish():
        for k in range(4):
            to_sibling(k).wait_recv()
            d2d[k] = (shard(_slot(*chips[k], c))[...].astype(F32) + d2d[k].astype(F32)).astype(BF16)
            save(k).start()
        for k in range(4):
            save(k).wait()
            to_sibling(k).wait_send()

    return send, finish


N_ICI_SUM_SEMS = 6


def _ici_sum(src, own, ici, via, stage, send_sems, recv_sems, local_sems, base=0):
    x, y, c = lax.axis_index("x"), lax.axis_index("y"), lax.axis_index("c")
    nx, ny = (1 - x, y, c), (x, 1 - y, c)
    OWN, NX, NY, DG = range(4)

    def half(ref, h):
        return ref.at[pl.ds(h * HALF_IN, HALF_IN), :]

    def rc(s, d, k, to):
        return pltpu.make_async_remote_copy(src_ref=s, dst_ref=d, send_sem=send_sems.at[base + k],
                                            recv_sem=recv_sems.at[base + k], device_id=to, device_id_type=MESH)

    for_dg_0 = lambda: rc(half(src.at[DG], 0), via.at[0], 0, nx)
    for_dg_1 = lambda: rc(half(src.at[DG], 1), via.at[1], 1, ny)
    for_nx_0 = lambda: rc(half(src.at[NX], 0), half(ici.at[0], 0), 2, nx)
    for_ny_1 = lambda: rc(half(src.at[NY], 1), half(ici.at[1], 1), 3, ny)
    for_ny_0 = lambda: rc(stage.at[0], half(ici.at[1], 0), 4, ny)
    for_nx_1 = lambda: rc(stage.at[1], half(ici.at[0], 1), 5, nx)
    mine = lambda: pltpu.make_async_copy(src.at[OWN], own, local_sems.at[0])
    stage_0 = lambda: pltpu.make_async_copy(half(src.at[NY], 0), stage.at[0], local_sems.at[1])
    stage_1 = lambda: pltpu.make_async_copy(half(src.at[NX], 1), stage.at[1], local_sems.at[2])

    def start():
        for cp in (for_dg_0, for_dg_1, for_nx_0, for_ny_1, stage_0, stage_1, mine):
            cp().start()

    def relay():
        for h, staged, landed, out in ((0, stage_0, for_dg_0, for_ny_0), (1, stage_1, for_dg_1, for_nx_1)):
            staged().wait()
            landed().wait_recv()
            stage[h] = (stage[h].astype(F32) + via[h].astype(F32)).astype(BF16)
            out().start()

    def finish():
        mine().wait()
        for cp in (for_nx_0, for_nx_1, for_ny_1, for_ny_0):
            cp().wait_recv()
        acc = own[...].astype(F32) + ici[0].astype(F32) + ici[1].astype(F32)
        for cp in (for_dg_0, for_dg_1, for_nx_0, for_ny_1, for_ny_0, for_nx_1):
            cp().wait_send()
        return acc

    return start, relay, finish


def _slab_sum(myslab, slabs, send_sems, recv_sems, base):
    x, y, c = lax.axis_index("x"), lax.axis_index("y"), lax.axis_index("c")
    me = _slot(x, y, c)
    peers = [(x, y, 1 - c), (1 - x, y, c), (x, 1 - y, c), (1 - x, 1 - y, c),
             (1 - x, y, 1 - c), (x, 1 - y, 1 - c), (1 - x, 1 - y, 1 - c)]

    def cp(k):
        return pltpu.make_async_remote_copy(src_ref=myslab, dst_ref=slabs.at[me], send_sem=send_sems.at[base + k],
                                            recv_sem=recv_sems.at[base + k], device_id=peers[k], device_id_type=MESH)

    def start():
        slabs[me] = myslab[...]
        for k in range(7):
            cp(k).start()

    def finish():
        for k in range(7):
            cp(k).wait_recv()
        total = slabs[0]
        for d in range(1, N_DEV):
            total = total + slabs[d]
        for k in range(7):
            cp(k).wait_send()
        return total

    return start, finish


def _chunk_rows(r):
    return slice(r * CHUNK, (r + 1) * CHUNK)


def _conv_halo(cch_ref, cuh_ref, n):
    zh = jnp.where(n > 0, cch_ref[...] * cuh_ref[...], 0.0)
    return jnp.concatenate([zh] * (CHUNK // HALO), axis=0)


def _conv_chunk(pj_ref, zhalo, cw, r):
    rows = _chunk_rows(r)
    cc = pj_ref[rows, OFF_CC:OFF_CC + D_CONV]
    cu = pj_ref[rows, OFF_CU:OFF_CU + D_CONV]
    z = cc * cu
    before = _chunk_rows(r - 1)
    zprev = pj_ref[before, OFF_CC:OFF_CC + D_CONV] * pj_ref[before, OFF_CU:OFF_CU + D_CONV] if r > 0 else zhalo
    row = lax.broadcasted_iota(jnp.int32, (CHUNK, D_CONV), 0)
    z1 = jnp.where(row < 1, pltpu.roll(zprev, 1, 0), pltpu.roll(z, 1, 0))
    z2 = jnp.where(row < 2, pltpu.roll(zprev, 2, 0), pltpu.roll(z, 2, 0))
    co = cw[0] * z2 + cw[1] * z1 + cw[2] * z
    return cc, cu, z, z1, z2, co


def _gated_norm(a, gain, t):
    r = lax.rsqrt(jnp.mean(a * a, axis=-1, keepdims=True) + RMS_EPS)
    return a * r * gain * (t * _sigmoid(t))


def _kv_bands(pj, kvp_ref):
    lane = lax.broadcasted_iota(jnp.int32, (2 * BLOCK, D_KV), 1)
    lo = lane < HEAD_DIM

    def bands(prev, cur):
        b = jnp.concatenate([prev, cur], axis=0)
        br = pltpu.roll(b, HEAD_DIM, 1)
        zero = jnp.zeros_like(b)
        return ((jnp.where(lo, b, zero).astype(BF16), jnp.where(lo, zero, br).astype(BF16)),
                (jnp.where(lo, br, zero).astype(BF16), jnp.where(lo, zero, b).astype(BF16)))

    ks = bands(kvp_ref[:, 0:D_KV], pj[:, OFF_K:OFF_K + D_KV])
    vs = bands(kvp_ref[:, D_KV:2 * D_KV], pj[:, OFF_V:OFF_V + D_KV])
    return ks, vs


STACK = PAIRS_PER_KV * BLOCK


def _head(j, i, e):
    return 2 * (PAIRS_PER_KV * j + i) + e


def _pair_cols(j, i, off):
    p = PAIRS_PER_KV * j + i
    return slice(off + 128 * p, off + 128 * (p + 1))


def _fill_attn_bias(bias_scr, first_block):
    qi = lax.broadcasted_iota(jnp.int32, (BLOCK, 2 * BLOCK), 0)
    kj = lax.broadcasted_iota(jnp.int32, (BLOCK, 2 * BLOCK), 1)
    dist = BLOCK + qi - kj
    valid = (dist >= 0) & (dist < BLOCK)
    if first_block:
        valid = valid & (kj >= BLOCK)
    distf = dist.astype(F32)
    for j in range(2):
        for e in range(2):
            for i in range(PAIRS_PER_KV):
                bias_scr[2 * j + e, BLOCK * i:BLOCK * (i + 1), :] = jnp.where(valid, -SLOPES[_head(j, i, e)] * distf, NEG)


def _q_stack(pj, j):
    return jnp.concatenate([(pj[:, _pair_cols(j, i, OFF_Q)] * SCALE).astype(BF16) for i in range(PAIRS_PER_KV)], axis=0)


def _attn_probs(q_stack, kband, bias_ref, sinks):
    s = lax.dot_general(q_stack, kband, _NT, preferred_element_type=F32)
    ones = jnp.ones((128, 128), BF16)
    probs, shares = [], []
    for i, sink in enumerate(sinks):
        rows = slice(BLOCK * i, BLOCK * (i + 1))
        t = s[rows, :] + bias_ref[rows, :]
        m = jnp.broadcast_to(jnp.max(t, axis=-1, keepdims=True), (BLOCK, 128))
        m = jnp.maximum(m, sink)
        p = [jnp.exp(t[:, :128] - m), jnp.exp(t[:, 128:] - m)]
        es = jnp.exp(sink - m)
        total = (jnp.dot(p[0].astype(BF16), ones, preferred_element_type=F32)
                 + jnp.dot(p[1].astype(BF16), ones, preferred_element_type=F32))
        inv = 1.0 / (total + es)
        probs.append(jnp.concatenate([p[0] * inv, p[1] * inv], axis=1))
        shares.append(es * inv)
    return jnp.concatenate(probs, axis=0), jnp.concatenate(shares, axis=0)


def _attn_group(pj, ks, vs, bias_scr, sink_ref, j):
    q_stack = _q_stack(pj, j)
    out, probs, shares = None, [], []
    for e in range(2):
        p, ps = _attn_probs(q_stack, ks[j][e], bias_scr.at[2 * j + e],
                            [sink_ref[_head(j, i, e)] for i in range(PAIRS_PER_KV)])
        p = p.astype(BF16)
        o = jnp.dot(p, vs[j][e], preferred_element_type=F32)
        out = o if out is None else out + o
        probs.append(p)
        shares.append(ps)
    return out, probs, shares


def _mix_fwd(proj, conv_full, sinks, norm_conv, norm_attn):
    def body(pj_ref, kvp_ref, cch_ref, cuh_ref, cw_ref, sink_ref, gc_ref, ga_ref,
             mixed_ref, attn_scr, p_ref, ps_ref, bias_scr):
        n = pl.program_id(0)
        pj = pj_ref

        @pl.when(n == 0)
        def _():
            _fill_attn_bias(bias_scr, first_block=True)

        @pl.when(n == 1)
        def _():
            _fill_attn_bias(bias_scr, first_block=False)

        zhalo = _conv_halo(cch_ref, cuh_ref, n)
        cw = (cw_ref[0:1, :], cw_ref[1:2, :], cw_ref[2:3, :])
        gain_c = gc_ref[...]

        for r in range(N_CHUNKS):
            rows = _chunk_rows(r)
            co = _conv_chunk(pj_ref, zhalo, cw, r)[-1]
            y = _gated_norm(pj_ref[rows, OFF_CB:OFF_CB + D_CONV] * co, gain_c, pj_ref[rows, OFF_GC:OFF_GC + D_CONV])
            mixed_ref[rows, 0:D_CONV] = y.astype(BF16)

        ks, vs = _kv_bands(pj, kvp_ref)
        for j in range(2):
            out, probs, shares = _attn_group(pj, ks, vs, bias_scr, sink_ref, j)
            for e in range(2):
                p_ref[0, 2 * j + e] = probs[e]
                ps_ref[0, 2 * j + e] = shares[e]
            for i in range(PAIRS_PER_KV):
                attn_scr[:, _pair_cols(j, i, 0)] = out[BLOCK * i:BLOCK * (i + 1), :]
        gain_a = ga_ref[...]

        for r in range(N_CHUNKS):
            rows = _chunk_rows(r)
            y = _gated_norm(attn_scr[rows, :], gain_a, pj_ref[rows, OFF_GA:OFF_GA + D_ATTN])
            mixed_ref[rows, D_CONV:D_MIX] = y.astype(BF16)

    per_block = BLOCK // HALO
    return pl.pallas_call(
        body, name="mix_fwd", grid=(N_BLOCKS,),
        in_specs=[
            pl.BlockSpec((BLOCK, D_PROJ), lambda n: (n, 0)),
            pl.BlockSpec((BLOCK, 2 * D_KV), lambda n: (jnp.maximum(n - 1, 0), OFF_K // (2 * D_KV))),
            pl.BlockSpec((HALO, D_CONV), lambda n: (jnp.maximum(n * per_block - 1, 0), OFF_CC // D_CONV)),
            pl.BlockSpec((HALO, D_CONV), lambda n: (jnp.maximum(n * per_block - 1, 0), OFF_CU // D_CONV)),
            pl.BlockSpec((8, D_CONV), lambda n: (0, 0)),
            pl.BlockSpec(memory_space=pltpu.SMEM),
            pl.BlockSpec((1, D_CONV), lambda n: (0, 0)),
            pl.BlockSpec((1, D_ATTN), lambda n: (0, 0)),
        ],
        out_specs=(pl.BlockSpec((BLOCK, D_MIX), lambda n: (n, 0)), pl.BlockSpec((BLOCK, D_ATTN), lambda n: (n, 0)),
                   pl.BlockSpec((1, 4, STACK, 2 * BLOCK), lambda n: (n, 0, 0, 0)),
                   pl.BlockSpec((1, 4, STACK, 128), lambda n: (n, 0, 0, 0))),
        out_shape=(jax.ShapeDtypeStruct((SEQ, D_MIX), BF16), jax.ShapeDtypeStruct((SEQ, D_ATTN), F32),
                   jax.ShapeDtypeStruct((N_BLOCKS, 4, STACK, 2 * BLOCK), BF16),
                   jax.ShapeDtypeStruct((N_BLOCKS, 4, STACK, 128), F32)),
        scratch_shapes=[pltpu.VMEM((4, STACK, 2 * BLOCK), F32)],
        compiler_params=_params(dimension_semantics=("arbitrary",)),
    )(proj, proj, proj, proj, conv_full, sinks, norm_conv, norm_attn)


def _out_proj_loss(mixed, x, target, w_out_full, norm_final):
    tm = 256

    def body(mx_ref, x_ref, t_ref, w_ref, g_ref, dx2_ref, dx2b_ref, dmix_ref, gnf_ref, loss_ref):
        i = pl.program_id(0)
        w = w_ref[...]
        x2 = x_ref[...] + jnp.dot(mx_ref[...], w, preferred_element_type=F32)
        r = lax.rsqrt(jnp.mean(x2 * x2, axis=-1, keepdims=True) + RMS_EPS)
        xn = x2 * r
        g = g_ref[...]
        err = xn * g - t_ref[...]
        part = 0.5 * jnp.sum(jnp.mean(err * err, axis=-1, keepdims=True), axis=0, keepdims=True)
        dy = err * (1.0 / D_MODEL)
        gnf = jnp.sum(dy * xn, axis=0, keepdims=True)
        u = dy * g
        dx2 = r * (u - xn * jnp.mean(u * xn, axis=-1, keepdims=True))
        dx2_ref[...] = dx2
        dx2b = dx2.astype(BF16)
        dx2b_ref[...] = dx2b
        dmix_ref[...] = lax.dot_general(dx2b, w, _NT, preferred_element_type=F32)

        @pl.when(i == 0)
        def _():
            gnf_ref[...] = jnp.zeros_like(gnf_ref)
            loss_ref[...] = jnp.zeros_like(loss_ref)

        gnf_ref[...] += gnf
        loss_ref[...] += jnp.broadcast_to(part, loss_ref.shape)

    return pl.pallas_call(
        body, name="out_proj_loss", grid=(SEQ // tm,),
        in_specs=[pl.BlockSpec((tm, D_MIX), lambda i: (i, 0)), pl.BlockSpec((tm, D_MODEL), lambda i: (i, 0)),
                  pl.BlockSpec((tm, D_MODEL), lambda i: (i, 0)), pl.BlockSpec(memory_space=pltpu.VMEM),
                  pl.BlockSpec((1, D_MODEL), lambda i: (0, 0))],
        out_specs=(pl.BlockSpec((tm, D_MODEL), lambda i: (i, 0)), pl.BlockSpec((tm, D_MODEL), lambda i: (i, 0)),
                   pl.BlockSpec((tm, D_MIX), lambda i: (i, 0)),
                   pl.BlockSpec((1, D_MODEL), lambda i: (0, 0)), pl.BlockSpec((8, 128), lambda i: (0, 0))),
        out_shape=(jax.ShapeDtypeStruct((SEQ, D_MODEL), F32), jax.ShapeDtypeStruct((SEQ, D_MODEL), BF16),
                   jax.ShapeDtypeStruct((SEQ, D_MIX), F32),
                   jax.ShapeDtypeStruct((1, D_MODEL), F32), jax.ShapeDtypeStruct((8, 128), F32)),
        compiler_params=_params(dimension_semantics=("arbitrary",)),
    )(mixed, x, target, w_out_full, norm_final)


def _gated_norm_bwd(a, gain, t, dy):
    r = lax.rsqrt(jnp.mean(a * a, axis=-1, keepdims=True) + RMS_EPS)
    an = a * r
    sg = _sigmoid(t)
    dn = dy * (t * sg)
    dt = dy * (an * gain) * (sg * (1.0 + t * (1.0 - sg)))
    u = dn * gain
    da = r * (u - an * jnp.mean(u * an, axis=-1, keepdims=True))
    return da, dt, dn * an


def _mix_bwd(proj, dmixed, attn, probs, shares, conv_full, norm_conv, norm_attn):
    def body(pj_ref, kvp_ref, cch_ref, cuh_ref, dmx_ref, attn_ref, p_ref, ps_ref, cw_ref, gc_ref, ga_ref,
             dpj_ref, gslab_ref, dattn_scr, nxt_scr, dkv_scr, acc_scr):
        step = pl.program_id(0)
        n = N_BLOCKS - 1 - step
        pj = pj_ref

        @pl.when(step == 0)
        def _():
            gslab_ref[...] = jnp.zeros_like(gslab_ref)
            nxt_scr[...] = jnp.zeros_like(nxt_scr)
            dkv_scr[...] = jnp.zeros_like(dkv_scr)
            acc_scr[...] = jnp.zeros_like(acc_scr)

        zhalo = _conv_halo(cch_ref, cuh_ref, n)
        cw = (cw_ref[0:1, :], cw_ref[1:2, :], cw_ref[2:3, :])
        gain_c = gc_ref[...]
        row = lax.broadcasted_iota(jnp.int32, (CHUNK, D_CONV), 0)

        dco_after = nxt_scr[...]
        for r in reversed(range(N_CHUNKS)):
            rows = _chunk_rows(r)
            cc, cu, z, z1, z2, co = _conv_chunk(pj_ref, zhalo, cw, r)
            cb = pj_ref[rows, OFF_CB:OFF_CB + D_CONV]
            da, dgate, gterm = _gated_norm_bwd(cb * co, gain_c, pj_ref[rows, OFF_GC:OFF_GC + D_CONV],
                                               dmx_ref[rows, 0:D_CONV])
            dpj_ref[rows, OFF_GC:OFF_GC + D_CONV] = dgate.astype(BF16)
            dpj_ref[rows, OFF_CB:OFF_CB + D_CONV] = (da * co).astype(BF16)
            dco = da * cb
            dco1 = jnp.where(row >= CHUNK - 1, pltpu.roll(dco_after, CHUNK - 1, 0), pltpu.roll(dco, CHUNK - 1, 0))
            dco2 = jnp.where(row >= CHUNK - 2, pltpu.roll(dco_after, CHUNK - 2, 0), pltpu.roll(dco, CHUNK - 2, 0))
            dz = cw[2] * dco + cw[1] * dco1 + cw[0] * dco2
            dpj_ref[rows, OFF_CC:OFF_CC + D_CONV] = (dz * cu).astype(BF16)
            dpj_ref[rows, OFF_CU:OFF_CU + D_CONV] = (dz * cc).astype(BF16)
            acc_scr[ACC_NORM_CONV] += gterm
            acc_scr[ACC_CONV0] += dco * z2
            acc_scr[ACC_CONV0 + 1] += dco * z1
            acc_scr[ACC_CONV0 + 2] += dco * z
            dco_after = dco
        nxt_scr[...] = dco_after

        ks, vs = _kv_bands(pj, kvp_ref)
        gain_a = ga_ref[...]

        for r in range(N_CHUNKS):
            rows = _chunk_rows(r)
            da, dgate, gterm = _gated_norm_bwd(attn_ref[rows, :], gain_a, pj_ref[rows, OFF_GA:OFF_GA + D_ATTN],
                                               dmx_ref[rows, D_CONV:D_MIX])
            dpj_ref[rows, OFF_GA:OFF_GA + D_ATTN] = dgate.astype(BF16)
            dattn_scr[rows, :] = da
            acc_scr[ACC_NORM_ATTN] += gterm

        in_lo = lax.broadcasted_iota(jnp.int32, (128, 128), 0) < HEAD_DIM
        half_ones = (jnp.where(in_lo, 1.0, 0.0).astype(BF16), jnp.where(in_lo, 0.0, 1.0).astype(BF16))
        lane_s = lax.broadcasted_iota(jnp.int32, (1, D_MODEL), 1)
        gsink = jnp.zeros((1, D_MODEL), F32)
        dk_t, dv_t = [], []
        for j in range(2):
            q_stack = _q_stack(pj, j)
            do_f = jnp.concatenate([dattn_scr[:, _pair_cols(j, i, 0)] for i in range(PAIRS_PER_KV)], axis=0)
            o_f = jnp.concatenate([attn_ref[:, _pair_cols(j, i, 0)] for i in range(PAIRS_PER_KV)], axis=0)
            prod = (do_f * o_f).astype(BF16)
            deltas = [jnp.dot(prod, half_ones[e], preferred_element_type=F32) for e in range(2)]
            do_b = do_f.astype(BF16)
            q_t, do_t = q_stack.T, do_b.T
            dq, dk_j, dv_j = None, None, None
            for e in range(2):
                p = p_ref[0, 2 * j + e]
                dp = lax.dot_general(do_b, vs[j][e], _NT, preferred_element_type=F32)
                ds = []
                for i in range(PAIRS_PER_KV):
                    rows = slice(BLOCK * i, BLOCK * (i + 1))
                    delta = deltas[e][rows, :]
                    ds.append((p[rows, :].astype(F32) * (dp[rows, :] - jnp.concatenate([delta, delta], axis=1))).astype(BF16))
                    gs_h = -jnp.sum(ps_ref[0, 2 * j + e, rows, 0:1] * delta[:, 0:1], axis=0, keepdims=True)
                    gsink = gsink + jnp.where(lane_s == _head(j, i, e), gs_h, 0.0)
                ds = jnp.concatenate(ds, axis=0)
                t = jnp.dot(ds, ks[j][e], preferred_element_type=F32)
                dq = t if dq is None else dq + t
                half = slice(HEAD_DIM * e, HEAD_DIM * (e + 1))
                a = jnp.dot(q_t[half, :], ds, preferred_element_type=F32)
                b = jnp.dot(do_t[half, :], p, preferred_element_type=F32)
                dk_j = a if dk_j is None else dk_j + a
                dv_j = b if dv_j is None else dv_j + b
            for i in range(PAIRS_PER_KV):
                dpj_ref[:, _pair_cols(j, i, OFF_Q)] = (dq[BLOCK * i:BLOCK * (i + 1), :] * SCALE).astype(BF16)
            dk_t.append(dk_j)
            dv_t.append(dv_j)
        dk = jnp.concatenate(dk_t, axis=0).T
        dv = jnp.concatenate(dv_t, axis=0).T
        dpj_ref[:, OFF_K:OFF_K + D_KV] = (dk[BLOCK:, :] + dkv_scr[:, 0:D_KV]).astype(BF16)
        dpj_ref[:, OFF_V:OFF_V + D_KV] = (dv[BLOCK:, :] + dkv_scr[:, D_KV:2 * D_KV]).astype(BF16)
        dkv_scr[:, 0:D_KV] = dk[:BLOCK, :]
        dkv_scr[:, D_KV:2 * D_KV] = dv[:BLOCK, :]
        gslab_ref[ROW_SINKS:ROW_SINKS + 1, :] += gsink

        @pl.when(step == N_BLOCKS - 1)
        def _():
            for k, slab_row in ((ACC_NORM_CONV, ROW_NORM_CONV), (ACC_NORM_ATTN, ROW_NORM_ATTN), (ACC_CONV0, ROW_CONV0),
                                (ACC_CONV0 + 1, ROW_CONV0 + 1), (ACC_CONV0 + 2, ROW_CONV0 + 2)):
                gslab_ref[slab_row:slab_row + 1, :] = jnp.sum(acc_scr[k], axis=0, keepdims=True)

    per_block = BLOCK // HALO
    last = N_BLOCKS - 1
    return pl.pallas_call(
        body, name="mix_bwd", grid=(N_BLOCKS,),
        in_specs=[
            pl.BlockSpec((BLOCK, D_PROJ), lambda s: (last - s, 0)),
            pl.BlockSpec((BLOCK, 2 * D_KV), lambda s: (jnp.maximum(last - s - 1, 0), OFF_K // (2 * D_KV))),
            pl.BlockSpec((HALO, D_CONV), lambda s: (jnp.maximum((last - s) * per_block - 1, 0), OFF_CC // D_CONV)),
            pl.BlockSpec((HALO, D_CONV), lambda s: (jnp.maximum((last - s) * per_block - 1, 0), OFF_CU // D_CONV)),
            pl.BlockSpec((BLOCK, D_MIX), lambda s: (last - s, 0)),
            pl.BlockSpec((BLOCK, D_ATTN), lambda s: (last - s, 0)),
            pl.BlockSpec((1, 4, STACK, 2 * BLOCK), lambda s: (last - s, 0, 0, 0)),
            pl.BlockSpec((1, 4, STACK, 128), lambda s: (last - s, 0, 0, 0)),
            pl.BlockSpec((8, D_CONV), lambda s: (0, 0)),
            pl.BlockSpec((1, D_CONV), lambda s: (0, 0)),
            pl.BlockSpec((1, D_ATTN), lambda s: (0, 0)),
        ],
        out_specs=(pl.BlockSpec((BLOCK, D_PROJ), lambda s: (last - s, 0)),
                   pl.BlockSpec((8, D_MODEL), lambda s: (0, 0))),
        out_shape=(jax.ShapeDtypeStruct((SEQ, D_PROJ), BF16), jax.ShapeDtypeStruct((8, D_MODEL), F32)),
        scratch_shapes=[pltpu.VMEM((BLOCK, D_ATTN), F32), pltpu.VMEM((CHUNK, D_CONV), F32),
                        pltpu.VMEM((BLOCK, 2 * D_KV), F32), pltpu.VMEM((N_ACC, CHUNK, D_MODEL), F32)],
        compiler_params=_params(dimension_semantics=("arbitrary",)),
    )(proj, proj, proj, proj, dmixed, attn, probs, shares, conv_full, norm_conv, norm_attn)


def _in_bwd_rs(dproj, w_full, x, dx2, norm_in, dw_in_chip, gslab, gnf, loss_part):
    tm = 256
    steps = SEQ // tm
    relay_step = 4

    def body(dp_ref, w_hbm, x_ref, dx2_ref, g_ref, dwi_ref, gs_ref, gnf_ref, lp_ref, gx_ref, gwin_ref, gsum_ref,
             gni_scr, own, ici, via, stage, myslab, slabs, w_ref, send_sems, recv_sems, local_sems):
        i = pl.program_id(0)
        rs_start, rs_relay, rs_finish = _ici_sum(dwi_ref, own, ici, via, stage, send_sems, recv_sems, local_sems)
        slab_start, slab_finish = _slab_sum(myslab, slabs, send_sems, recv_sems, N_ICI_SUM_SEMS)

        @pl.when(i == 0)
        def _():
            gni_scr[...] = jnp.zeros_like(gni_scr)
            rs_start()
            w_load = pltpu.make_async_copy(w_hbm, w_ref, local_sems.at[3])
            w_load.start()
            w_load.wait()

        dh = jnp.dot(dp_ref[...], w_ref[...], preferred_element_type=F32)
        xv = x_ref[...]
        r = lax.rsqrt(jnp.mean(xv * xv, axis=-1, keepdims=True) + RMS_EPS)
        xn = xv * r
        u = dh * g_ref[...]
        gx_ref[...] = dx2_ref[...] + r * (u - xn * jnp.mean(u * xn, axis=-1, keepdims=True))
        gni_scr[...] += jnp.sum(dh * xn, axis=0, keepdims=True)

        @pl.when(i == relay_step)
        def _():
            rs_relay()

        @pl.when(i == steps - 1)
        def _():
            row = lax.broadcasted_iota(jnp.int32, (8, D_MODEL), 0)
            lane = lax.broadcasted_iota(jnp.int32, (8, D_MODEL), 1)
            slab = jnp.where(row == ROW_NORM_IN, gni_scr[...], jnp.where(row == ROW_NORM_FINAL, gnf_ref[...], gs_ref[...]))
            myslab[...] = jnp.where((row == ROW_SINKS) & (lane == LOSS_LANE), lp_ref[0:1, 0:1], slab)
            slab_start()
            gwin_ref[...] = rs_finish()
            gsum_ref[...] = slab_finish()

    const = lambda i: (0, 0)
    return pl.pallas_call(
        body, name="in_bwd", grid=(steps,),
        in_specs=[pl.BlockSpec((tm, D_PROJ), lambda i: (i, 0)), pl.BlockSpec(memory_space=pl.ANY),
                  pl.BlockSpec((tm, D_MODEL), lambda i: (i, 0)), pl.BlockSpec((tm, D_MODEL), lambda i: (i, 0)),
                  pl.BlockSpec((1, D_MODEL), const), pl.BlockSpec(memory_space=pl.ANY),
                  pl.BlockSpec((8, D_MODEL), const), pl.BlockSpec((1, D_MODEL), const), pl.BlockSpec((8, 128), const)],
        out_specs=(pl.BlockSpec((tm, D_MODEL), lambda i: (i, 0)), pl.BlockSpec((SHARD_IN, D_MODEL), const),
                   pl.BlockSpec((8, D_MODEL), const)),
        out_shape=(jax.ShapeDtypeStruct((SEQ, D_MODEL), F32), jax.ShapeDtypeStruct((SHARD_IN, D_MODEL), F32),
                   jax.ShapeDtypeStruct((8, D_MODEL), F32)),
        scratch_shapes=[pltpu.VMEM((1, D_MODEL), F32), pltpu.VMEM((SHARD_IN, D_MODEL), BF16),
                        pltpu.VMEM((2, SHARD_IN, D_MODEL), BF16), pltpu.VMEM((2, HALF_IN, D_MODEL), BF16),
                        pltpu.VMEM((2, HALF_IN, D_MODEL), BF16),
                        pltpu.VMEM((8, D_MODEL), F32), pltpu.VMEM((N_DEV, 8, D_MODEL), F32),
                        pltpu.VMEM((D_PROJ, D_MODEL), BF16),
                        pltpu.SemaphoreType.DMA((N_ICI_SUM_SEMS + 7,)), pltpu.SemaphoreType.DMA((N_ICI_SUM_SEMS + 7,)),
                        pltpu.SemaphoreType.DMA((4,))],
        compiler_params=_params(dimension_semantics=("arbitrary",)),
    )(dproj, w_full, x, dx2, norm_in, dw_in_chip, gslab, gnf, loss_part)


def _dw_in_rs(dproj, h, dw_out_sh):
    tn = 640
    steps = D_PROJ // tn
    forward_step = 2

    def body(a_ref, b_ref, dwo_ref, chip_ref, gwo_ref, dwt, d2d_in, own, d2d, ici, send_sems, recv_sems, local_sems):
        i = pl.program_id(0)
        rs_start, rs_forward, rs_finish = _shard_sum(dwo_ref, own, d2d, ici, send_sems, recv_sems, local_sems)
        pair_send, pair_finish = _chip_sum(dwt, d2d_in, chip_ref, send_sems, recv_sems, local_sems,
                                           N_SHARD_SUM_SEMS, 4)

        @pl.when(i == 0)
        def _():
            rs_start()

        pair_send((i - 1) * tn, i * tn)

        tile = lax.dot_general(a_ref[...], b_ref[...], _TN, preferred_element_type=F32).astype(BF16)
        dwt[pl.ds(pl.multiple_of(i * tn, tn), tn), :] = tile

        @pl.when(i == forward_step)
        def _():
            rs_forward()

        @pl.when(i == steps - 1)
        def _():
            pair_send((steps - 1) * tn, D_PROJ)
            gwo_ref[...] = rs_finish()
            pair_finish()

    return pl.pallas_call(
        body, name="dw_in", grid=(steps,),
        in_specs=[pl.BlockSpec((SEQ, tn), lambda i: (0, i)), pl.BlockSpec(memory_space=pltpu.VMEM),
                  pl.BlockSpec(memory_space=pl.ANY)],
        out_specs=(pl.BlockSpec(memory_space=pl.ANY), pl.BlockSpec((SHARD_OUT, D_MODEL), lambda i: (0, 0))),
        out_shape=(jax.ShapeDtypeStruct((4, SHARD_IN, D_MODEL), BF16), jax.ShapeDtypeStruct((SHARD_OUT, D_MODEL), F32)),
        scratch_shapes=[pltpu.VMEM((D_PROJ, D_MODEL), BF16), pltpu.VMEM((4, SHARD_IN, D_MODEL), BF16),
                        *_shard_sum_scratch(SHARD_OUT),
                        pltpu.SemaphoreType.DMA((N_SHARD_SUM_SEMS + 4,)), pltpu.SemaphoreType.DMA((N_SHARD_SUM_SEMS + 4,)),
                        pltpu.SemaphoreType.DMA((8,))],
        compiler_params=_params(dimension_semantics=("arbitrary",)),
    )(dproj, h, dw_out_sh)


def _matmul_tn(a, b, tn, name):
    k, n = a.shape
    _, m = b.shape

    def body(a_ref, b_ref, o_ref):
        o_ref[...] = lax.dot_general(a_ref[...], b_ref[...], _TN, preferred_element_type=F32).astype(BF16)

    return pl.pallas_call(
        body, name=name, grid=(n // tn,),
        in_specs=[pl.BlockSpec((k, tn), lambda i: (0, i)), pl.BlockSpec(memory_space=pltpu.VMEM)],
        out_specs=pl.BlockSpec((tn, m), lambda i: (i, 0)),
        out_shape=jax.ShapeDtypeStruct((n, m), BF16),
        compiler_params=_params(dimension_semantics=("arbitrary",)),
    )(a, b)


def _adam_all(big_in, big_out, gsum, small, grad_x):
    steps = 4
    tr_in, tr_out = SHARD_IN // steps, SHARD_OUT // steps

    def body(*refs):
        ins, outs = refs[:8 + 1 + 18 + 1], refs[8 + 1 + 18 + 1:]
        i = pl.program_id(0)
        outs[33][...] = ins[27][...]
        for b in range(2):
            w_ref, g_ref, m_ref, v_ref = ins[4 * b:4 * b + 4]
            g = g_ref[...]
            delta, mn, vn = _adamw(w_ref[...], g, m_ref[...], v_ref[...])
            for ref, val in zip(outs[4 * b:4 * b + 4], (g, delta, mn, vn)):
                ref[...] = val

        @pl.when(i == 0)
        def _():
            gsum = ins[8][...]
            idx = _slot(lax.axis_index("x"), lax.axis_index("y"), lax.axis_index("c"))
            cg = jnp.zeros((3, SHARD_CONV), F32)
            for d in range(N_DEV):
                cg = jnp.where(idx == d, gsum[ROW_CONV0:ROW_CONV0 + 3, d * SHARD_CONV:(d + 1) * SHARD_CONV], cg)
            grads = (gsum[ROW_NORM_IN:ROW_NORM_IN + 1], gsum[ROW_SINKS:ROW_SINKS + 1, 0:N_Q_HEADS],
                     gsum[ROW_NORM_CONV:ROW_NORM_CONV + 1], gsum[ROW_NORM_ATTN:ROW_NORM_ATTN + 1],
                     gsum[ROW_NORM_FINAL:ROW_NORM_FINAL + 1], cg)
            for s, g in enumerate(grads):
                at = (slice(None), 0, slice(None)) if s == 5 else (slice(None), slice(None))
                w_ref, m_ref, v_ref = ins[9 + 3 * s:12 + 3 * s]
                delta, mn, vn = _adamw(w_ref[at], g, m_ref[at], v_ref[at])
                for ref, val in zip(outs[8 + 4 * s:12 + 4 * s], (g, delta, mn, vn)):
                    ref[at] = val
            outs[32][...] = gsum[ROW_SINKS:ROW_SINKS + 1, LOSS_LANE:LOSS_LANE + 1]

    const = lambda i: (0, 0)
    rows = lambda i: (i, 0)
    whole = lambda shape: pl.BlockSpec(shape, lambda i: (0,) * len(shape))
    small_shapes = [a.shape for a in small[::3]]
    in_specs = ([pl.BlockSpec((tr_in, D_MODEL), rows)] * 4 + [pl.BlockSpec((tr_out, D_MODEL), rows)] * 4
                + [pl.BlockSpec((8, D_MODEL), const)] + [whole(a.shape) for a in small]
                + [pl.BlockSpec((SEQ // steps, D_MODEL), rows)])
    out_specs = ([pl.BlockSpec((tr_in, D_MODEL), rows)] * 4 + [pl.BlockSpec((tr_out, D_MODEL), rows)] * 4
                 + [whole(s) for s in small_shapes for _ in range(4)] + [pl.BlockSpec((1, 1), const)]
                 + [pl.BlockSpec((SEQ // steps, D_MODEL), rows)])
    out_shape = ([jax.ShapeDtypeStruct((SHARD_IN, D_MODEL), F32)] * 4 + [jax.ShapeDtypeStruct((SHARD_OUT, D_MODEL), F32)] * 4
                 + [jax.ShapeDtypeStruct(s, F32) for s in small_shapes for _ in range(4)]
                 + [jax.ShapeDtypeStruct((1, 1), F32), jax.ShapeDtypeStruct((SEQ, D_MODEL), F32)])
    outs = pl.pallas_call(
        body, name="adam", grid=(steps,), in_specs=in_specs, out_specs=tuple(out_specs), out_shape=tuple(out_shape),
        compiler_params=_params(dimension_semantics=("arbitrary",)),
    )(*big_in, *big_out, gsum, *small, grad_x)
    return outs[0:4], outs[4:8], [outs[8 + 4 * s:12 + 4 * s] for s in range(6)], outs[32], outs[33]


def _rows_first(a):
    return jnp.transpose(a, (1, 0, 2))


def kernel(x, norm_in, w_in, conv_w, attn_sinks, norm_conv_out, norm_attn_out, w_out, norm_final, loss_target, m_norm_in, m_w_in, m_conv_w, m_attn_sinks, m_norm_conv_out, m_norm_attn_out, m_w_out, m_norm_final, v_norm_in, v_w_in, v_conv_w, v_attn_sinks, v_norm_conv_out, v_norm_attn_out, v_w_out, v_norm_final):
    x2d = x.reshape(SEQ, D_MODEL)
    target = loss_target.reshape(SEQ, D_MODEL)
    nf = norm_final.reshape(1, D_MODEL)

    w_in_t, m_w_in_t, v_w_in_t = w_in[0].T, m_w_in[0].T, v_w_in[0].T
    tiles = jnp.asarray(TILE_ORDER, jnp.int32).reshape(-1)
    w_in_full, h, proj, g_out, conv_full = _gather_in_proj(x2d, norm_in, w_in_t, w_out[0], _rows_first(conv_w), tiles)
    sinks = attn_sinks.reshape(N_Q_HEADS)

    mixed, attn, probs, shares = _mix_fwd(proj, conv_full, sinks, norm_conv_out, norm_attn_out)
    dx2, dx2b, dmixed, gnf, loss_part = _out_proj_loss(mixed, x2d, target, g_out.reshape(D_MIX, D_MODEL), nf)
    dproj, gslab = _mix_bwd(proj, dmixed, attn, probs, shares, conv_full, norm_conv_out, norm_attn_out)
    dw_out = _matmul_tn(mixed, dx2b, 512, "dw_out")
    dw_in_chip, g_w_out = _dw_in_rs(dproj, h, dw_out.reshape(N_DEV, SHARD_OUT, D_MODEL))
    grad_x, g_w_in, gsum = _in_bwd_rs(dproj, w_in_full, x2d, dx2, norm_in, dw_in_chip, gslab, gnf, loss_part)

    small = (norm_in, m_norm_in, v_norm_in, attn_sinks, m_attn_sinks, v_attn_sinks,
             norm_conv_out, m_norm_conv_out, v_norm_conv_out, norm_attn_out, m_norm_attn_out, v_norm_attn_out,
             nf, m_norm_final.reshape(1, D_MODEL), v_norm_final.reshape(1, D_MODEL),
             _rows_first(conv_w), _rows_first(m_conv_w), _rows_first(v_conv_w))
    big_in, big_out, (s_ni, s_sk, s_nc, s_na, s_nf, s_cv), loss, grad_x = _adam_all(
        (w_in_t, g_w_in, m_w_in_t, v_w_in_t), (w_out[0], g_w_out, m_w_out[0], v_w_out[0]), gsum, small, grad_x)

    def leaves(k):
        return (s_ni[k], big_in[k].T[None], jnp.transpose(s_cv[k], (1, 0, 2)), s_sk[k], s_nc[k], s_na[k], big_out[k][None],
                s_nf[k].reshape(D_MODEL))

    return (loss.reshape(()), grad_x.reshape(1, SEQ, D_MODEL), *leaves(0), *leaves(1), *leaves(2), *leaves(3))
```

```python
import jax
import jax.numpy as jnp
from jax import lax
from jax.experimental import pallas as pl
from jax.experimental.pallas import tpu as pltpu

F32 = jnp.float32
BF16 = jnp.bfloat16
MESH = pl.DeviceIdType.MESH

N_DEV = 8
SEQ = 2048
D_MODEL = 1024
D_CONV = 1024
D_ATTN = 1024
D_KV = 128
HEAD_DIM = 64
N_Q_HEADS = 16
N_PAIRS = N_Q_HEADS // 2
PAIRS_PER_KV = N_PAIRS // 2
D_MIX = D_CONV + D_ATTN
D_PROJ = 6400
SHARD_IN = D_PROJ // N_DEV
SHARD_OUT = D_MIX // N_DEV
SHARD_CONV = D_CONV // N_DEV
OFF_CB, OFF_CC, OFF_CU, OFF_GC, OFF_Q, OFF_K, OFF_V, OFF_GA = 0, 1024, 2048, 3072, 4096, 5120, 5248, 5376
BLOCK = 128
N_BLOCKS = SEQ // BLOCK
HALO = 8
CHUNK = 16
N_CHUNKS = BLOCK // CHUNK
RMS_EPS = 1e-5
NEG = -1e30
SCALE = HEAD_DIM ** -0.5
SLOPES = tuple(2.0 ** (-8.0 * (h + 1) / N_Q_HEADS) for h in range(N_Q_HEADS))

ADAM_LR = 0.001
ADAM_B1 = 0.9
ADAM_B2 = 0.999
ADAM_EPS = 1e-08
ADAM_WD = 0.01
ADAM_STEP = 10

ROW_NORM_IN, ROW_NORM_CONV, ROW_NORM_ATTN, ROW_NORM_FINAL, ROW_CONV0, ROW_SINKS = 0, 1, 2, 3, 4, 7
LOSS_LANE = N_Q_HEADS
ACC_NORM_CONV, ACC_NORM_ATTN, ACC_CONV0, N_ACC = 0, 1, 2, 5

VMEM_LIMIT = 56 * 1024 * 1024

_NT = (((1,), (1,)), ((), ()))
_TN = (((0,), (0,)), ((), ()))


def _params(**kw):
    return pltpu.CompilerParams(vmem_limit_bytes=VMEM_LIMIT, **kw)


def _adamw(w, g, m, v):
    m = ADAM_B1 * m + (1.0 - ADAM_B1) * g
    v = ADAM_B2 * v + (1.0 - ADAM_B2) * (g * g)
    m_hat = m / (1.0 - ADAM_B1 ** ADAM_STEP)
    v_hat = v / (1.0 - ADAM_B2 ** ADAM_STEP)
    delta = -ADAM_LR * (m_hat / (jnp.sqrt(v_hat) + ADAM_EPS) + ADAM_WD * w)
    return delta, m, v


def _sigmoid(t):
    return 1.0 / (1.0 + jnp.exp(-t))


def _slot(px, py, pc):
    return 4 * px + 2 * py + pc


HALF_IN = SHARD_IN // 2
N_GATHER_KINDS = 13
W_OUT_KINDS = N_GATHER_KINDS + 7


IN_PROJ_TILE = 640
TILE_ORDER = ((0, 1, 2, 3, 4, 5, 6, 7, 8, 9), (3, 4, 0, 1, 2, 8, 9, 5, 6, 7),
              (5, 6, 0, 1, 7, 8, 9, 2, 3, 4), (8, 9, 3, 4, 5, 6, 7, 0, 1, 2))
TILES_OWN, TILES_NEIGHBOURS = 2, 7


def _tile(table_ref, p):
    chip = 2 * lax.axis_index("x") + lax.axis_index("y")
    return table_ref[chip * len(TILE_ORDER[0]) + p]


def _gather_in_proj(x, norm_in, w_in_sh, w_out_sh, conv_sh, tiles):
    tn = IN_PROJ_TILE
    steps = D_PROJ // tn
    tm = 256

    def body(tiles_ref, x_hbm, g_ref, win_ref, wout_ref, cv_ref, wt_ref, h_ref, proj_ref, gout_ref, conv_ref,
             gin_ref, gcv_ref, wob_ref, x_ref, send_sems, recv_sems, local_sems):
        p = pl.program_id(0)
        local_sem = local_sems.at[0]
        x, y, c = lax.axis_index("x"), lax.axis_index("y"), lax.axis_index("c")
        me, sibling = (x, y, c), (x, y, 1 - c)
        nx, ny, dg = (1 - x, y, c), (x, 1 - y, c), (1 - x, 1 - y, c)

        def other(dev):
            return (dev[0], dev[1], 1 - dev[2])

        def shard(dev):
            return gin_ref.at[pl.ds(pl.multiple_of(_slot(*dev) * SHARD_IN, 16), SHARD_IN), :]

        def half(dev, h):
            return gin_ref.at[pl.ds(pl.multiple_of(_slot(*dev) * SHARD_IN + h * HALF_IN, 16), HALF_IN), :]

        def rc(ref, k, to):
            return pltpu.make_async_remote_copy(src_ref=ref, dst_ref=ref, send_sem=send_sems.at[k],
                                                recv_sem=recv_sems.at[k], device_id=to, device_id_type=MESH)

        def cv(k, dev, to):
            s = _slot(*dev)
            return pltpu.make_async_remote_copy(src_ref=gcv_ref.at[s], dst_ref=gcv_ref.at[s],
                                                send_sem=send_sems.at[N_GATHER_KINDS + k],
                                                recv_sem=recv_sems.at[N_GATHER_KINDS + k], device_id=to, device_id_type=MESH)

        def own_copies():
            return [rc(shard(me), 0, sibling),
                    rc(half(me, 0), 1, nx), rc(half(me, 1), 2, nx),
                    rc(half(me, 1), 4, ny), rc(half(me, 0), 3, ny),
                    cv(0, me, sibling)] + [cv(1 + j, me, peer) for j, peer in enumerate((nx, ny, dg))]

        def pass_on(dev, h, k_in, k_ici, k_d2d, half=half, base=0):
            rc(half(dev, h), base + k_in, me).wait_recv()
            if k_ici is not None:
                rc(half(dev, h), base + k_ici, ny if dev is nx else nx).start()
            rc(half(dev, h), base + k_d2d, sibling).start()

        def out_half(dev, h):
            return gout_ref.at[_slot(*dev), pl.ds(h * (SHARD_OUT // 2), SHARD_OUT // 2), :]

        def own_out_copies():
            src = lambda h: wob_ref.at[pl.ds(h * (SHARD_OUT // 2), SHARD_OUT // 2), :]

            def send(ref, dst, k, to):
                return pltpu.make_async_remote_copy(src_ref=ref, dst_ref=dst, send_sem=send_sems.at[W_OUT_KINDS + k],
                                                    recv_sem=recv_sems.at[W_OUT_KINDS + k], device_id=to, device_id_type=MESH)

            return [send(wob_ref, gout_ref.at[_slot(*me)], 0, sibling),
                    send(src(0), out_half(me, 0), 1, nx), send(src(1), out_half(me, 1), 2, nx),
                    send(src(1), out_half(me, 1), 4, ny), send(src(0), out_half(me, 0), 3, ny)]

        def own_out_local():
            return pltpu.make_async_copy(wob_ref, gout_ref.at[_slot(*me)], local_sems.at[1])

        @pl.when(p == 0)
        def _():
            gin_ref[pl.ds(pl.multiple_of(_slot(*me) * SHARD_IN, 16), SHARD_IN), :] = win_ref[...].astype(BF16)
            gcv_ref[_slot(*me)] = jnp.zeros((8, SHARD_CONV), F32)
            gcv_ref[_slot(*me), 0:3, :] = cv_ref[:, 0, :]
            for cp in own_copies():
                cp.start()
            wob_ref[...] = wout_ref[...].astype(BF16)
            x_load = pltpu.make_async_copy(x_hbm, x_ref, local_sems.at[2])
            x_load.start()
            x_load.wait()
            for t in range(SEQ // tm):
                xv = x_ref[tm * t:tm * (t + 1), :]
                r = lax.rsqrt(jnp.mean(xv * xv, axis=-1, keepdims=True) + RMS_EPS)
                h_ref[tm * t:tm * (t + 1), :] = (xv * r * g_ref[...]).astype(BF16)
            rc(shard(sibling), 0, me).wait_recv()

        @pl.when(p == TILES_OWN)
        def _():
            for args in ((nx, 0, 1, 5, 7), (ny, 1, 4, 6, 10), (nx, 1, 2, None, 8), (ny, 0, 3, None, 9)):
                pass_on(*args)
            for j, peer in enumerate((nx, ny, dg)):
                cv(1 + j, peer, me).wait_recv()
                cv(4 + j, peer, sibling).start()
            for (dev, h), k in (((nx, 0), 7), ((nx, 1), 8), ((ny, 0), 9), ((ny, 1), 10)):
                rc(half(other(dev), h), k, me).wait_recv()
            own_out_local().start()
            for cp in own_out_copies():
                cp.start()

        @pl.when(p == TILES_NEIGHBOURS)
        def _():
            pass_on(dg, 0, 5, None, 11)
            pass_on(dg, 1, 6, None, 12)
            for (dev, h), k in (((dg, 0), 11), ((dg, 1), 12)):
                rc(half(other(dev), h), k, me).wait_recv()
            pltpu.make_async_copy(gin_ref, wt_ref, local_sem).start()

        @pl.when(p == steps - 2)
        def _():
            for args in ((nx, 0, 1, 5, 7), (ny, 1, 4, 6, 10), (nx, 1, 2, None, 8), (ny, 0, 3, None, 9)):
                pass_on(*args, half=out_half, base=W_OUT_KINDS)

        w = gin_ref[pl.ds(pl.multiple_of(_tile(tiles_ref, p) * tn, tn), tn), :]
        proj_ref[...] = lax.dot_general(h_ref[...], w, _NT, preferred_element_type=F32)

        @pl.when(p == steps - 1)
        def _():
            cv(0, sibling, me).wait_recv()
            for j, peer in enumerate((nx, ny, dg)):
                cv(4 + j, other(peer), me).wait_recv()
            for d in range(N_DEV):
                conv_ref[:, d * SHARD_CONV:(d + 1) * SHARD_CONV] = gcv_ref[d]
            relayed = [rc(half(nx, 0), 5, ny), rc(half(ny, 1), 6, nx)]
            relayed += [rc(half(dev, h), k, sibling) for (dev, h), k in
                        (((nx, 0), 7), ((nx, 1), 8), ((ny, 0), 9), ((ny, 1), 10), ((dg, 0), 11), ((dg, 1), 12))]
            relayed += [cv(4 + j, peer, sibling) for j, peer in enumerate((nx, ny, dg))]
            for cp in own_copies() + relayed:
                cp.wait_send()
            pltpu.make_async_copy(gin_ref, wt_ref, local_sem).wait()
            pass_on(dg, 0, 5, None, 11, half=out_half, base=W_OUT_KINDS)
            pass_on(dg, 1, 6, None, 12, half=out_half, base=W_OUT_KINDS)
            rc(gout_ref.at[_slot(*sibling)], W_OUT_KINDS, me).wait_recv()
            out_relayed = [rc(out_half(nx, 0), W_OUT_KINDS + 5, ny), rc(out_half(ny, 1), W_OUT_KINDS + 6, nx)]
            for (dev, h), k in (((nx, 0), 7), ((nx, 1), 8), ((ny, 0), 9), ((ny, 1), 10), ((dg, 0), 11), ((dg, 1), 12)):
                rc(out_half(other(dev), h), W_OUT_KINDS + k, me).wait_recv()
                out_relayed.append(rc(out_half(dev, h), W_OUT_KINDS + k, sibling))
            for cp in own_out_copies() + out_relayed:
                cp.wait_send()
            own_out_local().wait()

    vmem = pl.BlockSpec(memory_space=pltpu.VMEM)
    grid_spec = pltpu.PrefetchScalarGridSpec(
        num_scalar_prefetch=1, grid=(steps,),
        in_specs=[pl.BlockSpec(memory_space=pl.ANY), vmem, vmem, vmem, vmem],
        out_specs=(pl.BlockSpec(memory_space=pl.ANY), vmem,
                   pl.BlockSpec((SEQ, tn), lambda p, tiles_ref: (0, _tile(tiles_ref, p))),
                   pl.BlockSpec(memory_space=pl.ANY), vmem),
        scratch_shapes=[pltpu.VMEM((D_PROJ, D_MODEL), BF16), pltpu.VMEM((N_DEV, 8, SHARD_CONV), F32),
                        pltpu.VMEM((SHARD_OUT, D_MODEL), BF16), pltpu.VMEM((SEQ, D_MODEL), F32),
                        pltpu.SemaphoreType.DMA((W_OUT_KINDS + N_GATHER_KINDS,)),
                        pltpu.SemaphoreType.DMA((W_OUT_KINDS + N_GATHER_KINDS,)),
                        pltpu.SemaphoreType.DMA((3,))])
    return pl.pallas_call(
        body, name="gather_in_proj", grid_spec=grid_spec,
        out_shape=(jax.ShapeDtypeStruct((D_PROJ, D_MODEL), BF16), jax.ShapeDtypeStruct((SEQ, D_MODEL), BF16),
                   jax.ShapeDtypeStruct((SEQ, D_PROJ), F32), jax.ShapeDtypeStruct((N_DEV, SHARD_OUT, D_MODEL), BF16),
                   jax.ShapeDtypeStruct((8, D_CONV), F32)),
        compiler_params=_params(dimension_semantics=("arbitrary",)),
    )(tiles, x, norm_in, w_in_sh, w_out_sh, conv_sh)


def _shard_sum(src, own, d2d, ici, send_sems, recv_sems, local_sems, base=0):
    x, y, c = lax.axis_index("x"), lax.axis_index("y"), lax.axis_index("c")
    sibling = (x, y, 1 - c)
    chips = [(x, y), (1 - x, y), (x, 1 - y), (1 - x, 1 - y)]

    def rcopy(s, d, k, to):
        return pltpu.make_async_remote_copy(src_ref=s, dst_ref=d, send_sem=send_sems.at[base + k],
                                            recv_sem=recv_sems.at[base + k], device_id=to, device_id_type=MESH)

    def mine(k):
        return pltpu.make_async_copy(src.at[_slot(*chips[k], c)], own.at[k], local_sems.at[k])

    def to_sibling(k):
        return rcopy(src.at[_slot(*chips[k], 1 - c)], d2d.at[k], k, sibling)

    def to_chip(k):
        return rcopy(own.at[k], ici.at[k - 1], 3 + k, (*chips[k], c))

    def start():
        for k in range(4):
            mine(k).start()
            to_sibling(k).start()

    def forward():
        for k in range(1, 4):
            mine(k).wait()
            to_sibling(k).wait_recv()
            own[k] = (own[k].astype(F32) + d2d[k].astype(F32)).astype(BF16)
            to_chip(k).start()

    def finish():
        mine(0).wait()
        to_sibling(0).wait_recv()
        acc = own[0].astype(F32) + d2d[0].astype(F32)
        for k in range(1, 4):
            to_chip(k).wait_recv()
            acc = acc + ici[k - 1].astype(F32)
        for k in range(4):
            to_sibling(k).wait_send()
        for k in range(1, 4):
            to_chip(k).wait_send()
        return acc

    return start, forward, finish


def _shard_sum_scratch(rows):
    return [pltpu.VMEM((4, rows, D_MODEL), BF16), pltpu.VMEM((4, rows, D_MODEL), BF16),
            pltpu.VMEM((3, rows, D_MODEL), BF16)]


N_SHARD_SUM_SEMS = 7


def _chip_sum(dwt, d2d, out_hbm, send_sems, recv_sems, local_sems, base, local_base):
    x, y, c = lax.axis_index("x"), lax.axis_index("y"), lax.axis_index("c")
    sibling = (x, y, 1 - c)
    chips = [(x, y), (1 - x, y), (x, 1 - y), (1 - x, 1 - y)]

    def shard(s):
        return dwt.at[pl.ds(pl.multiple_of(s * SHARD_IN, 16), SHARD_IN), :]

    def to_sibling(k):
        return pltpu.make_async_remote_copy(src_ref=shard(_slot(*chips[k], 1 - c)), dst_ref=d2d.at[k],
                                            send_sem=send_sems.at[base + k], recv_sem=recv_sems.at[base + k],
                                            device_id=sibling, device_id_type=MESH)

    def save(k):
        return pltpu.make_async_copy(d2d.at[k], out_hbm.at[k], local_sems.at[local_base + k])

    def send(rows_before, rows_done):
        for k in range(4):
            end = (_slot(*chips[k], 1 - c) + 1) * SHARD_IN

            @pl.when((end > rows_before) & (end <= rows_done))
            def _():
                to_sibling(k).start()

    def finish():
        for k in range(4):
            to_sibling(k).wait_recv()
            d2d[k] = (shard(_slot(*chips[k], c))[...].astype(F32) + d2d[k].astype(F32)).astype(BF16)
            save(k).start()
        for k in range(4):
            save(k).wait()
            to_sibling(k).wait_send()

    return send, finish


N_ICI_SUM_SEMS = 6


def _ici_sum(src, own, ici, via, stage, send_sems, recv_sems, local_sems, base=0):
    x, y, c = lax.axis_index("x"), lax.axis_index("y"), lax.axis_index("c")
    nx, ny = (1 - x, y, c), (x, 1 - y, c)
    OWN, NX, NY, DG = range(4)

    def half(ref, h):
        return ref.at[pl.ds(h * HALF_IN, HALF_IN), :]

    def rc(s, d, k, to):
        return pltpu.make_async_remote_copy(src_ref=s, dst_ref=d, send_sem=send_sems.at[base + k],
                                            recv_sem=recv_sems.at[base + k], device_id=to, device_id_type=MESH)

    for_dg_0 = lambda: rc(half(src.at[DG], 0), via.at[0], 0, nx)
    for_dg_1 = lambda: rc(half(src.at[DG], 1), via.at[1], 1, ny)
    for_nx_0 = lambda: rc(half(src.at[NX], 0), half(ici.at[0], 0), 2, nx)
    for_ny_1 = lambda: rc(half(src.at[NY], 1), half(ici.at[1], 1), 3, ny)
    for_ny_0 = lambda: rc(stage.at[0], half(ici.at[1], 0), 4, ny)
    for_nx_1 = lambda: rc(stage.at[1], half(ici.at[0], 1), 5, nx)
    mine = lambda: pltpu.make_async_copy(src.at[OWN], own, local_sems.at[0])
    stage_0 = lambda: pltpu.make_async_copy(half(src.at[NY], 0), stage.at[0], local_sems.at[1])
    stage_1 = lambda: pltpu.make_async_copy(half(src.at[NX], 1), stage.at[1], local_sems.at[2])

    def start():
        for cp in (for_dg_0, for_dg_1, for_nx_0, for_ny_1, stage_0, stage_1, mine):
            cp().start()

    def relay():
        for h, staged, landed, out in ((0, stage_0, for_dg_0, for_ny_0), (1, stage_1, for_dg_1, for_nx_1)):
            staged().wait()
            landed().wait_recv()
            stage[h] = (stage[h].astype(F32) + via[h].astype(F32)).astype(BF16)
            out().start()

    def finish():
        mine().wait()
        for cp in (for_nx_0, for_nx_1, for_ny_1, for_ny_0):
            cp().wait_recv()
        acc = own[...].astype(F32) + ici[0].astype(F32) + ici[1].astype(F32)
        for cp in (for_dg_0, for_dg_1, for_nx_0, for_ny_1, for_ny_0, for_nx_1):
            cp().wait_send()
        return acc

    return start, relay, finish


def _slab_sum(myslab, slabs, send_sems, recv_sems, base):
    x, y, c = lax.axis_index("x"), lax.axis_index("y"), lax.axis_index("c")
    me = _slot(x, y, c)
    peers = [(x, y, 1 - c), (1 - x, y, c), (x, 1 - y, c), (1 - x, 1 - y, c),
             (1 - x, y, 1 - c), (x, 1 - y, 1 - c), (1 - x, 1 - y, 1 - c)]

    def cp(k):
        return pltpu.make_async_remote_copy(src_ref=myslab, dst_ref=slabs.at[me], send_sem=send_sems.at[base + k],
                                            recv_sem=recv_sems.at[base + k], device_id=peers[k], device_id_type=MESH)

    def start():
        slabs[me] = myslab[...]
        for k in range(7):
            cp(k).start()

    def finish():
        for k in range(7):
            cp(k).wait_recv()
        total = slabs[0]
        for d in range(1, N_DEV):
            total = total + slabs[d]
        for k in range(7):
            cp(k).wait_send()
        return total

    return start, finish


def _chunk_rows(r):
    return slice(r * CHUNK, (r + 1) * CHUNK)


def _conv_halo(cch_ref, cuh_ref, n):
    zh = jnp.where(n > 0, cch_ref[...] * cuh_ref[...], 0.0)
    return jnp.concatenate([zh] * (CHUNK // HALO), axis=0)


def _conv_chunk(pj_ref, zhalo, cw, r):
    rows = _chunk_rows(r)
    cc = pj_ref[rows, OFF_CC:OFF_CC + D_CONV]
    cu = pj_ref[rows, OFF_CU:OFF_CU + D_CONV]
    z = cc * cu
    before = _chunk_rows(r - 1)
    zprev = pj_ref[before, OFF_CC:OFF_CC + D_CONV] * pj_ref[before, OFF_CU:OFF_CU + D_CONV] if r > 0 else zhalo
    row = lax.broadcasted_iota(jnp.int32, (CHUNK, D_CONV), 0)
    z1 = jnp.where(row < 1, pltpu.roll(zprev, 1, 0), pltpu.roll(z, 1, 0))
    z2 = jnp.where(row < 2, pltpu.roll(zprev, 2, 0), pltpu.roll(z, 2, 0))
    co = cw[0] * z2 + cw[1] * z1 + cw[2] * z
    return cc, cu, z, z1, z2, co


def _gated_norm(a, gain, t):
    r = lax.rsqrt(jnp.mean(a * a, axis=-1, keepdims=True) + RMS_EPS)
    return a * r * gain * (t * _sigmoid(t))


def _kv_bands(pj, kvp_ref):
    lane = lax.broadcasted_iota(jnp.int32, (2 * BLOCK, D_KV), 1)
    lo = lane < HEAD_DIM

    def bands(prev, cur):
        b = jnp.concatenate([prev, cur], axis=0)
        br = pltpu.roll(b, HEAD_DIM, 1)
        zero = jnp.zeros_like(b)
        return ((jnp.where(lo, b, zero).astype(BF16), jnp.where(lo, zero, br).astype(BF16)),
                (jnp.where(lo, br, zero).astype(BF16), jnp.where(lo, zero, b).astype(BF16)))

    ks = bands(kvp_ref[:, 0:D_KV], pj[:, OFF_K:OFF_K + D_KV])
    vs = bands(kvp_ref[:, D_KV:2 * D_KV], pj[:, OFF_V:OFF_V + D_KV])
    return ks, vs


STACK = PAIRS_PER_KV * BLOCK


def _head(j, i, e):
    return 2 * (PAIRS_PER_KV * j + i) + e


def _pair_cols(j, i, off):
    p = PAIRS_PER_KV * j + i
    return slice(off + 128 * p, off + 128 * (p + 1))


def _fill_attn_bias(bias_scr, first_block):
    qi = lax.broadcasted_iota(jnp.int32, (BLOCK, 2 * BLOCK), 0)
    kj = lax.broadcasted_iota(jnp.int32, (BLOCK, 2 * BLOCK), 1)
    dist = BLOCK + qi - kj
    valid = (dist >= 0) & (dist < BLOCK)
    if first_block:
        valid = valid & (kj >= BLOCK)
    distf = dist.astype(F32)
    for j in range(2):
        for e in range(2):
            for i in range(PAIRS_PER_KV):
                bias_scr[2 * j + e, BLOCK * i:BLOCK * (i + 1), :] = jnp.where(valid, -SLOPES[_head(j, i, e)] * distf, NEG)


def _q_stack(pj, j):
    return jnp.concatenate([(pj[:, _pair_cols(j, i, OFF_Q)] * SCALE).astype(BF16) for i in range(PAIRS_PER_KV)], axis=0)


def _attn_probs(q_stack, kband, bias_ref, sinks):
    s = lax.dot_general(q_stack, kband, _NT, preferred_element_type=F32)
    ones = jnp.ones((128, 128), BF16)
    probs, shares = [], []
    for i, sink in enumerate(sinks):
        rows = slice(BLOCK * i, BLOCK * (i + 1))
        t = s[rows, :] + bias_ref[rows, :]
        m = jnp.broadcast_to(jnp.max(t, axis=-1, keepdims=True), (BLOCK, 128))
        m = jnp.maximum(m, sink)
        p = [jnp.exp(t[:, :128] - m), jnp.exp(t[:, 128:] - m)]
        es = jnp.exp(sink - m)
        total = (jnp.dot(p[0].astype(BF16), ones, preferred_element_type=F32)
                 + jnp.dot(p[1].astype(BF16), ones, preferred_element_type=F32))
        inv = 1.0 / (total + es)
        probs.append(jnp.concatenate([p[0] * inv, p[1] * inv], axis=1))
        shares.append(es * inv)
    return jnp.concatenate(probs, axis=0), jnp.concatenate(shares, axis=0)


def _attn_group(pj, ks, vs, bias_scr, sink_ref, j):
    q_stack = _q_stack(pj, j)
    out, probs, shares = None, [], []
    for e in range(2):
        p, ps = _attn_probs(q_stack, ks[j][e], bias_scr.at[2 * j + e],
                            [sink_ref[_head(j, i, e)] for i in range(PAIRS_PER_KV)])
        p = p.astype(BF16)
        o = jnp.dot(p, vs[j][e], preferred_element_type=F32)
        out = o if out is None else out + o
        probs.append(p)
        shares.append(ps)
    return out, probs, shares


def _mix_fwd(proj, conv_full, sinks, norm_conv, norm_attn):
    def body(pj_ref, kvp_ref, cch_ref, cuh_ref, cw_ref, sink_ref, gc_ref, ga_ref,
             mixed_ref, attn_scr, p_ref, ps_ref, bias_scr):
        n = pl.program_id(0)
        pj = pj_ref

        @pl.when(n == 0)
        def _():
            _fill_attn_bias(bias_scr, first_block=True)

        @pl.when(n == 1)
        def _():
            _fill_attn_bias(bias_scr, first_block=False)

        zhalo = _conv_halo(cch_ref, cuh_ref, n)
        cw = (cw_ref[0:1, :], cw_ref[1:2, :], cw_ref[2:3, :])
        gain_c = gc_ref[...]

        for r in range(N_CHUNKS):
            rows = _chunk_rows(r)
            co = _conv_chunk(pj_ref, zhalo, cw, r)[-1]
            y = _gated_norm(pj_ref[rows, OFF_CB:OFF_CB + D_CONV] * co, gain_c, pj_ref[rows, OFF_GC:OFF_GC + D_CONV])
            mixed_ref[rows, 0:D_CONV] = y.astype(BF16)

        ks, vs = _kv_bands(pj, kvp_ref)
        for j in range(2):
            out, probs, shares = _attn_group(pj, ks, vs, bias_scr, sink_ref, j)
            for e in range(2):
                p_ref[0, 2 * j + e] = probs[e]
                ps_ref[0, 2 * j + e] = shares[e]
            for i in range(PAIRS_PER_KV):
                attn_scr[:, _pair_cols(j, i, 0)] = out[BLOCK * i:BLOCK * (i + 1), :]
        gain_a = ga_ref[...]

        for r in range(N_CHUNKS):
            rows = _chunk_rows(r)
            y = _gated_norm(attn_scr[rows, :], gain_a, pj_ref[rows, OFF_GA:OFF_GA + D_ATTN])
            mixed_ref[rows, D_CONV:D_MIX] = y.astype(BF16)

    per_block = BLOCK // HALO
    return pl.pallas_call(
        body, name="mix_fwd", grid=(N_BLOCKS,),
        in_specs=[
            pl.BlockSpec((BLOCK, D_PROJ), lambda n: (n, 0)),
            pl.BlockSpec((BLOCK, 2 * D_KV), lambda n: (jnp.maximum(n - 1, 0), OFF_K // (2 * D_KV))),
            pl.BlockSpec((HALO, D_CONV), lambda n: (jnp.maximum(n * per_block - 1, 0), OFF_CC // D_CONV)),
            pl.BlockSpec((HALO, D_CONV), lambda n: (jnp.maximum(n * per_block - 1, 0), OFF_CU // D_CONV)),
            pl.BlockSpec((8, D_CONV), lambda n: (0, 0)),
            pl.BlockSpec(memory_space=pltpu.SMEM),
            pl.BlockSpec((1, D_CONV), lambda n: (0, 0)),
            pl.BlockSpec((1, D_ATTN), lambda n: (0, 0)),
        ],
        out_specs=(pl.BlockSpec((BLOCK, D_MIX), lambda n: (n, 0)), pl.BlockSpec((BLOCK, D_ATTN), lambda n: (n, 0)),
                   pl.BlockSpec((1, 4, STACK, 2 * BLOCK), lambda n: (n, 0, 0, 0)),
                   pl.BlockSpec((1, 4, STACK, 128), lambda n: (n, 0, 0, 0))),
        out_shape=(jax.ShapeDtypeStruct((SEQ, D_MIX), BF16), jax.ShapeDtypeStruct((SEQ, D_ATTN), F32),
                   jax.ShapeDtypeStruct((N_BLOCKS, 4, STACK, 2 * BLOCK), BF16),
                   jax.ShapeDtypeStruct((N_BLOCKS, 4, STACK, 128), F32)),
        scratch_shapes=[pltpu.VMEM((4, STACK, 2 * BLOCK), F32)],
        compiler_params=_params(dimension_semantics=("arbitrary",)),
    )(proj, proj, proj, proj, conv_full, sinks, norm_conv, norm_attn)


def _out_proj_loss(mixed, x, target, w_out_full, norm_final):
    tm = 256

    def body(mx_ref, x_ref, t_ref, w_ref, g_ref, dx2_ref, dx2b_ref, dmix_ref, gnf_ref, loss_ref):
        i = pl.program_id(0)
        w = w_ref[...]
        x2 = x_ref[...] + jnp.dot(mx_ref[...], w, preferred_element_type=F32)
        r = lax.rsqrt(jnp.mean(x2 * x2, axis=-1, keepdims=True) + RMS_EPS)
        xn = x2 * r
        g = g_ref[...]
        err = xn * g - t_ref[...]
        part = 0.5 * jnp.sum(jnp.mean(err * err, axis=-1, keepdims=True), axis=0, keepdims=True)
        dy = err * (1.0 / D_MODEL)
        gnf = jnp.sum(dy * xn, axis=0, keepdims=True)
        u = dy * g
        dx2 = r * (u - xn * jnp.mean(u * xn, axis=-1, keepdims=True))
        dx2_ref[...] = dx2
        dx2b = dx2.astype(BF16)
        dx2b_ref[...] = dx2b
        dmix_ref[...] = lax.dot_general(dx2b, w, _NT, preferred_element_type=F32)

        @pl.when(i == 0)
        def _():
            gnf_ref[...] = jnp.zeros_like(gnf_ref)
            loss_ref[...] = jnp.zeros_like(loss_ref)

        gnf_ref[...] += gnf
        loss_ref[...] += jnp.broadcast_to(part, loss_ref.shape)

    return pl.pallas_call(
        body, name="out_proj_loss", grid=(SEQ // tm,),
        in_specs=[pl.BlockSpec((tm, D_MIX), lambda i: (i, 0)), pl.BlockSpec((tm, D_MODEL), lambda i: (i, 0)),
                  pl.BlockSpec((tm, D_MODEL), lambda i: (i, 0)), pl.BlockSpec(memory_space=pltpu.VMEM),
                  pl.BlockSpec((1, D_MODEL), lambda i: (0, 0))],
        out_specs=(pl.BlockSpec((tm, D_MODEL), lambda i: (i, 0)), pl.BlockSpec((tm, D_MODEL), lambda i: (i, 0)),
                   pl.BlockSpec((tm, D_MIX), lambda i: (i, 0)),
                   pl.BlockSpec((1, D_MODEL), lambda i: (0, 0)), pl.BlockSpec((8, 128), lambda i: (0, 0))),
        out_shape=(jax.ShapeDtypeStruct((SEQ, D_MODEL), F32), jax.ShapeDtypeStruct((SEQ, D_MODEL), BF16),
                   jax.ShapeDtypeStruct((SEQ, D_MIX), F32),
                   jax.ShapeDtypeStruct((1, D_MODEL), F32), jax.ShapeDtypeStruct((8, 128), F32)),
        compiler_params=_params(dimension_semantics=("arbitrary",)),
    )(mixed, x, target, w_out_full, norm_final)


def _gated_norm_bwd(a, gain, t, dy):
    r = lax.rsqrt(jnp.mean(a * a, axis=-1, keepdims=True) + RMS_EPS)
    an = a * r
    sg = _sigmoid(t)
    dn = dy * (t * sg)
    dt = dy * (an * gain) * (sg * (1.0 + t * (1.0 - sg)))
    u = dn * gain
    da = r * (u - an * jnp.mean(u * an, axis=-1, keepdims=True))
    return da, dt, dn * an


def _mix_bwd(proj, dmixed, attn, probs, shares, conv_full, norm_conv, norm_attn):
    def body(pj_ref, kvp_ref, cch_ref, cuh_ref, dmx_ref, attn_ref, p_ref, ps_ref, cw_ref, gc_ref, ga_ref,
             dpj_ref, gslab_ref, dattn_scr, nxt_scr, dkv_scr, acc_scr):
        step = pl.program_id(0)
        n = N_BLOCKS - 1 - step
        pj = pj_ref

        @pl.when(step == 0)
        def _():
            gslab_ref[...] = jnp.zeros_like(gslab_ref)
            nxt_scr[...] = jnp.zeros_like(nxt_scr)
            dkv_scr[...] = jnp.zeros_like(dkv_scr)
            acc_scr[...] = jnp.zeros_like(acc_scr)

        zhalo = _conv_halo(cch_ref, cuh_ref, n)
        cw = (cw_ref[0:1, :], cw_ref[1:2, :], cw_ref[2:3, :])
        gain_c = gc_ref[...]
        row = lax.broadcasted_iota(jnp.int32, (CHUNK, D_CONV), 0)

        dco_after = nxt_scr[...]
        for r in reversed(range(N_CHUNKS)):
            rows = _chunk_rows(r)
            cc, cu, z, z1, z2, co = _conv_chunk(pj_ref, zhalo, cw, r)
            cb = pj_ref[rows, OFF_CB:OFF_CB + D_CONV]
            da, dgate, gterm = _gated_norm_bwd(cb * co, gain_c, pj_ref[rows, OFF_GC:OFF_GC + D_CONV],
                                               dmx_ref[rows, 0:D_CONV])
            dpj_ref[rows, OFF_GC:OFF_GC + D_CONV] = dgate.astype(BF16)
            dpj_ref[rows, OFF_CB:OFF_CB + D_CONV] = (da * co).astype(BF16)
            dco = da * cb
            dco1 = jnp.where(row >= CHUNK - 1, pltpu.roll(dco_after, CHUNK - 1, 0), pltpu.roll(dco, CHUNK - 1, 0))
            dco2 = jnp.where(row >= CHUNK - 2, pltpu.roll(dco_after, CHUNK - 2, 0), pltpu.roll(dco, CHUNK - 2, 0))
            dz = cw[2] * dco + cw[1] * dco1 + cw[0] * dco2
            dpj_ref[rows, OFF_CC:OFF_CC + D_CONV] = (dz * cu).astype(BF16)
            dpj_ref[rows, OFF_CU:OFF_CU + D_CONV] = (dz * cc).astype(BF16)
            acc_scr[ACC_NORM_CONV] += gterm
            acc_scr[ACC_CONV0] += dco * z2
            acc_scr[ACC_CONV0 + 1] += dco * z1
            acc_scr[ACC_CONV0 + 2] += dco * z
            dco_after = dco
        nxt_scr[...] = dco_after

        ks, vs = _kv_bands(pj, kvp_ref)
        gain_a = ga_ref[...]

        for r in range(N_CHUNKS):
            rows = _chunk_rows(r)
            da, dgate, gterm = _gated_norm_bwd(attn_ref[rows, :], gain_a, pj_ref[rows, OFF_GA:OFF_GA + D_ATTN],
                                               dmx_ref[rows, D_CONV:D_MIX])
            dpj_ref[rows, OFF_GA:OFF_GA + D_ATTN] = dgate.astype(BF16)
            dattn_scr[rows, :] = da
            acc_scr[ACC_NORM_ATTN] += gterm

        in_lo = lax.broadcasted_iota(jnp.int32, (128, 128), 0) < HEAD_DIM
        half_ones = (jnp.where(in_lo, 1.0, 0.0).astype(BF16), jnp.where(in_lo, 0.0, 1.0).astype(BF16))
        lane_s = lax.broadcasted_iota(jnp.int32, (1, D_MODEL), 1)
        gsink = jnp.zeros((1, D_MODEL), F32)
        dk_t, dv_t = [], []
        for j in range(2):
            q_stack = _q_stack(pj, j)
            do_f = jnp.concatenate([dattn_scr[:, _pair_cols(j, i, 0)] for i in range(PAIRS_PER_KV)], axis=0)
            o_f = jnp.concatenate([attn_ref[:, _pair_cols(j, i, 0)] for i in range(PAIRS_PER_KV)], axis=0)
            prod = (do_f * o_f).astype(BF16)
            deltas = [jnp.dot(prod, half_ones[e], preferred_element_type=F32) for e in range(2)]
            do_b = do_f.astype(BF16)
            q_t, do_t = q_stack.T, do_b.T
            dq, dk_j, dv_j = None, None, None
            for e in range(2):
                p = p_ref[0, 2 * j + e]
                dp = lax.dot_general(do_b, vs[j][e], _NT, preferred_element_type=F32)
                ds = []
                for i in range(PAIRS_PER_KV):
                    rows = slice(BLOCK * i, BLOCK * (i + 1))
                    delta = deltas[e][rows, :]
                    ds.append((p[rows, :].astype(F32) * (dp[rows, :] - jnp.concatenate([delta, delta], axis=1))).astype(BF16))
                    gs_h = -jnp.sum(ps_ref[0, 2 * j + e, rows, 0:1] * delta[:, 0:1], axis=0, keepdims=True)
                    gsink = gsink + jnp.where(lane_s == _head(j, i, e), gs_h, 0.0)
                ds = jnp.concatenate(ds, axis=0)
                t = jnp.dot(ds, ks[j][e], preferred_element_type=F32)
                dq = t if dq is None else dq + t
                half = slice(HEAD_DIM * e, HEAD_DIM * (e + 1))
                a = jnp.dot(q_t[half, :], ds, preferred_element_type=F32)
                b = jnp.dot(do_t[half, :], p, preferred_element_type=F32)
                dk_j = a if dk_j is None else dk_j + a
                dv_j = b if dv_j is None else dv_j + b
            for i in range(PAIRS_PER_KV):
                dpj_ref[:, _pair_cols(j, i, OFF_Q)] = (dq[BLOCK * i:BLOCK * (i + 1), :] * SCALE).astype(BF16)
            dk_t.append(dk_j)
            dv_t.append(dv_j)
        dk = jnp.concatenate(dk_t, axis=0).T
        dv = jnp.concatenate(dv_t, axis=0).T
        dpj_ref[:, OFF_K:OFF_K + D_KV] = (dk[BLOCK:, :] + dkv_scr[:, 0:D_KV]).astype(BF16)
        dpj_ref[:, OFF_V:OFF_V + D_KV] = (dv[BLOCK:, :] + dkv_scr[:, D_KV:2 * D_KV]).astype(BF16)
        dkv_scr[:, 0:D_KV] = dk[:BLOCK, :]
        dkv_scr[:, D_KV:2 * D_KV] = dv[:BLOCK, :]
        gslab_ref[ROW_SINKS:ROW_SINKS + 1, :] += gsink

        @pl.when(step == N_BLOCKS - 1)
        def _():
            for k, slab_row in ((ACC_NORM_CONV, ROW_NORM_CONV), (ACC_NORM_ATTN, ROW_NORM_ATTN), (ACC_CONV0, ROW_CONV0),
                                (ACC_CONV0 + 1, ROW_CONV0 + 1), (ACC_CONV0 + 2, ROW_CONV0 + 2)):
                gslab_ref[slab_row:slab_row + 1, :] = jnp.sum(acc_scr[k], axis=0, keepdims=True)

    per_block = BLOCK // HALO
    last = N_BLOCKS - 1
    return pl.pallas_call(
        body, name="mix_bwd", grid=(N_BLOCKS,),
        in_specs=[
            pl.BlockSpec((BLOCK, D_PROJ), lambda s: (last - s, 0)),
            pl.BlockSpec((BLOCK, 2 * D_KV), lambda s: (jnp.maximum(last - s - 1, 0), OFF_K // (2 * D_KV))),
            pl.BlockSpec((HALO, D_CONV), lambda s: (jnp.maximum((last - s) * per_block - 1, 0), OFF_CC // D_CONV)),
            pl.BlockSpec((HALO, D_CONV), lambda s: (jnp.maximum((last - s) * per_block - 1, 0), OFF_CU // D_CONV)),
            pl.BlockSpec((BLOCK, D_MIX), lambda s: (last - s, 0)),
            pl.BlockSpec((BLOCK, D_ATTN), lambda s: (last - s, 0)),
            pl.BlockSpec((1, 4, STACK, 2 * BLOCK), lambda s: (last - s, 0, 0, 0)),
            pl.BlockSpec((1, 4, STACK, 128), lambda s: (last - s, 0, 0, 0)),
            pl.BlockSpec((8, D_CONV), lambda s: (0, 0)),
            pl.BlockSpec((1, D_CONV), lambda s: (0, 0)),
            pl.BlockSpec((1, D_ATTN), lambda s: (0, 0)),
        ],
        out_specs=(pl.BlockSpec((BLOCK, D_PROJ), lambda s: (last - s, 0)),
                   pl.BlockSpec((8, D_MODEL), lambda s: (0, 0))),
        out_shape=(jax.ShapeDtypeStruct((SEQ, D_PROJ), BF16), jax.ShapeDtypeStruct((8, D_MODEL), F32)),
        scratch_shapes=[pltpu.VMEM((BLOCK, D_ATTN), F32), pltpu.VMEM((CHUNK, D_CONV), F32),
                        pltpu.VMEM((BLOCK, 2 * D_KV), F32), pltpu.VMEM((N_ACC, CHUNK, D_MODEL), F32)],
        compiler_params=_params(dimension_semantics=("arbitrary",)),
    )(proj, proj, proj, proj, dmixed, attn, probs, shares, conv_full, norm_conv, norm_attn)


def _in_bwd_rs(dproj, w_full, x, dx2, norm_in, dw_in_chip, gslab, gnf, loss_part):
    tm = 256
    steps = SEQ // tm
    relay_step = 4

    def body(dp_ref, w_hbm, x_ref, dx2_ref, g_ref, dwi_ref, gs_ref, gnf_ref, lp_ref, gx_ref, gwin_ref, gsum_ref,
             gni_scr, own, ici, via, stage, myslab, slabs, w_ref, send_sems, recv_sems, local_sems):
        i = pl.program_id(0)
        rs_start, rs_relay, rs_finish = _ici_sum(dwi_ref, own, ici, via, stage, send_sems, recv_sems, local_sems)
        slab_start, slab_finish = _slab_sum(myslab, slabs, send_sems, recv_sems, N_ICI_SUM_SEMS)

        @pl.when(i == 0)
        def _():
            gni_scr[...] = jnp.zeros_like(gni_scr)
            rs_start()
            w_load = pltpu.make_async_copy(w_hbm, w_ref, local_sems.at[3])
            w_load.start()
            w_load.wait()

        dh = jnp.dot(dp_ref[...], w_ref[...], preferred_element_type=F32)
        xv = x_ref[...]
        r = lax.rsqrt(jnp.mean(xv * xv, axis=-1, keepdims=True) + RMS_EPS)
        xn = xv * r
        u = dh * g_ref[...]
        gx_ref[...] = dx2_ref[...] + r * (u - xn * jnp.mean(u * xn, axis=-1, keepdims=True))
        gni_scr[...] += jnp.sum(dh * xn, axis=0, keepdims=True)

        @pl.when(i == relay_step)
        def _():
            rs_relay()

        @pl.when(i == steps - 1)
        def _():
            row = lax.broadcasted_iota(jnp.int32, (8, D_MODEL), 0)
            lane = lax.broadcasted_iota(jnp.int32, (8, D_MODEL), 1)
            slab = jnp.where(row == ROW_NORM_IN, gni_scr[...], jnp.where(row == ROW_NORM_FINAL, gnf_ref[...], gs_ref[...]))
            myslab[...] = jnp.where((row == ROW_SINKS) & (lane == LOSS_LANE), lp_ref[0:1, 0:1], slab)
            slab_start()
            gwin_ref[...] = rs_finish()
            gsum_ref[...] = slab_finish()

    const = lambda i: (0, 0)
    return pl.pallas_call(
        body, name="in_bwd", grid=(steps,),
        in_specs=[pl.BlockSpec((tm, D_PROJ), lambda i: (i, 0)), pl.BlockSpec(memory_space=pl.ANY),
                  pl.BlockSpec((tm, D_MODEL), lambda i: (i, 0)), pl.BlockSpec((tm, D_MODEL), lambda i: (i, 0)),
                  pl.BlockSpec((1, D_MODEL), const), pl.BlockSpec(memory_space=pl.ANY),
                  pl.BlockSpec((8, D_MODEL), const), pl.BlockSpec((1, D_MODEL), const), pl.BlockSpec((8, 128), const)],
        out_specs=(pl.BlockSpec((tm, D_MODEL), lambda i: (i, 0)), pl.BlockSpec((SHARD_IN, D_MODEL), const),
                   pl.BlockSpec((8, D_MODEL), const)),
        out_shape=(jax.ShapeDtypeStruct((SEQ, D_MODEL), F32), jax.ShapeDtypeStruct((SHARD_IN, D_MODEL), F32),
                   jax.ShapeDtypeStruct((8, D_MODEL), F32)),
        scratch_shapes=[pltpu.VMEM((1, D_MODEL), F32), pltpu.VMEM((SHARD_IN, D_MODEL), BF16),
                        pltpu.VMEM((2, SHARD_IN, D_MODEL), BF16), pltpu.VMEM((2, HALF_IN, D_MODEL), BF16),
                        pltpu.VMEM((2, HALF_IN, D_MODEL), BF16),
                        pltpu.VMEM((8, D_MODEL), F32), pltpu.VMEM((N_DEV, 8, D_MODEL), F32),
                        pltpu.VMEM((D_PROJ, D_MODEL), BF16),
                        pltpu.SemaphoreType.DMA((N_ICI_SUM_SEMS + 7,)), pltpu.SemaphoreType.DMA((N_ICI_SUM_SEMS + 7,)),
                        pltpu.SemaphoreType.DMA((4,))],
        compiler_params=_params(dimension_semantics=("arbitrary",)),
    )(dproj, w_full, x, dx2, norm_in, dw_in_chip, gslab, gnf, loss_part)


def _dw_rs(mixed, dx2b, dproj, h):
    tn_out, tn = 512, 640
    out_steps, in_steps = D_MIX // tn_out, D_PROJ // tn
    steps = out_steps + in_steps
    forward_step = out_steps + 2

    def body(mx_ref, dxb_ref, a_ref, b_ref, chip_ref, gwo_ref, dwo, dwt, d2d_in, own, d2d, ici,
             send_sems, recv_sems, local_sems):
        i = pl.program_id(0)
        rs_start, rs_forward, rs_finish = _shard_sum(dwo, own, d2d, ici, send_sems, recv_sems, local_sems)
        pair_send, pair_finish = _chip_sum(dwt, d2d_in, chip_ref, send_sems, recv_sems, local_sems,
                                           N_SHARD_SUM_SEMS, 4)

        @pl.when(i < out_steps)
        def _():
            tile = lax.dot_general(mx_ref[...], dxb_ref[...], _TN, preferred_element_type=F32).astype(BF16)
            for half in range(tn_out // SHARD_OUT):
                dwo[(tn_out // SHARD_OUT) * i + half] = tile[SHARD_OUT * half:SHARD_OUT * (half + 1), :]

        @pl.when(i == out_steps)
        def _():
            rs_start()

        @pl.when(i >= out_steps)
        def _():
            k = i - out_steps
            pair_send((k - 1) * tn, k * tn)
            tile = lax.dot_general(a_ref[...], b_ref[...], _TN, preferred_element_type=F32).astype(BF16)
            dwt[pl.ds(pl.multiple_of(k * tn, tn), tn), :] = tile

        @pl.when(i == forward_step)
        def _():
            rs_forward()

        @pl.when(i == steps - 1)
        def _():
            pair_send((in_steps - 1) * tn, D_PROJ)
            gwo_ref[...] = rs_finish()
            pair_finish()

    vmem = pl.BlockSpec(memory_space=pltpu.VMEM)
    return pl.pallas_call(
        body, name="dw", grid=(steps,),
        in_specs=[pl.BlockSpec((SEQ, tn_out), lambda i: (0, jnp.minimum(i, out_steps - 1))), vmem,
                  pl.BlockSpec((SEQ, tn), lambda i: (0, jnp.maximum(i - out_steps, 0))), vmem],
        out_specs=(pl.BlockSpec(memory_space=pl.ANY), pl.BlockSpec((SHARD_OUT, D_MODEL), lambda i: (0, 0))),
        out_shape=(jax.ShapeDtypeStruct((4, SHARD_IN, D_MODEL), BF16), jax.ShapeDtypeStruct((SHARD_OUT, D_MODEL), F32)),
        scratch_shapes=[pltpu.VMEM((N_DEV, SHARD_OUT, D_MODEL), BF16),
                        pltpu.VMEM((D_PROJ, D_MODEL), BF16), pltpu.VMEM((4, SHARD_IN, D_MODEL), BF16),
                        *_shard_sum_scratch(SHARD_OUT),
                        pltpu.SemaphoreType.DMA((N_SHARD_SUM_SEMS + 4,)), pltpu.SemaphoreType.DMA((N_SHARD_SUM_SEMS + 4,)),
                        pltpu.SemaphoreType.DMA((8,))],
        compiler_params=_params(dimension_semantics=("arbitrary",)),
    )(mixed, dx2b, dproj, h)


def _adam_all(big_in, big_out, gsum, small, grad_x):
    steps = 4
    tr_in, tr_out = SHARD_IN // steps, SHARD_OUT // steps

    def body(*refs):
        ins, outs = refs[:8 + 1 + 18 + 1], refs[8 + 1 + 18 + 1:]
        i = pl.program_id(0)
        outs[33][...] = ins[27][...]
        for b in range(2):
            w_ref, g_ref, m_ref, v_ref = ins[4 * b:4 * b + 4]
            g = g_ref[...]
            delta, mn, vn = _adamw(w_ref[...], g, m_ref[...], v_ref[...])
            for ref, val in zip(outs[4 * b:4 * b + 4], (g, delta, mn, vn)):
                ref[...] = val

        @pl.when(i == 0)
        def _():
            gsum = ins[8][...]
            idx = _slot(lax.axis_index("x"), lax.axis_index("y"), lax.axis_index("c"))
            cg = jnp.zeros((3, SHARD_CONV), F32)
            for d in range(N_DEV):
                cg = jnp.where(idx == d, gsum[ROW_CONV0:ROW_CONV0 + 3, d * SHARD_CONV:(d + 1) * SHARD_CONV], cg)
            grads = (gsum[ROW_NORM_IN:ROW_NORM_IN + 1], gsum[ROW_SINKS:ROW_SINKS + 1, 0:N_Q_HEADS],
                     gsum[ROW_NORM_CONV:ROW_NORM_CONV + 1], gsum[ROW_NORM_ATTN:ROW_NORM_ATTN + 1],
                     gsum[ROW_NORM_FINAL:ROW_NORM_FINAL + 1], cg)
            for s, g in enumerate(grads):
                at = (slice(None), 0, slice(None)) if s == 5 else (slice(None), slice(None))
                w_ref, m_ref, v_ref = ins[9 + 3 * s:12 + 3 * s]
                delta, mn, vn = _adamw(w_ref[at], g, m_ref[at], v_ref[at])
                for ref, val in zip(outs[8 + 4 * s:12 + 4 * s], (g, delta, mn, vn)):
                    ref[at] = val
            outs[32][...] = gsum[ROW_SINKS:ROW_SINKS + 1, LOSS_LANE:LOSS_LANE + 1]

    const = lambda i: (0, 0)
    rows = lambda i: (i, 0)
    whole = lambda shape: pl.BlockSpec(shape, lambda i: (0,) * len(shape))
    small_shapes = [a.shape for a in small[::3]]
    in_specs = ([pl.BlockSpec((tr_in, D_MODEL), rows)] * 4 + [pl.BlockSpec((tr_out, D_MODEL), rows)] * 4
                + [pl.BlockSpec((8, D_MODEL), const)] + [whole(a.shape) for a in small]
                + [pl.BlockSpec((SEQ // steps, D_MODEL), rows)])
    out_specs = ([pl.BlockSpec((tr_in, D_MODEL), rows)] * 4 + [pl.BlockSpec((tr_out, D_MODEL), rows)] * 4
                 + [whole(s) for s in small_shapes for _ in range(4)] + [pl.BlockSpec((1, 1), const)]
                 + [pl.BlockSpec((SEQ // steps, D_MODEL), rows)])
    out_shape = ([jax.ShapeDtypeStruct((SHARD_IN, D_MODEL), F32)] * 4 + [jax.ShapeDtypeStruct((SHARD_OUT, D_MODEL), F32)] * 4
                 + [jax.ShapeDtypeStruct(s, F32) for s in small_shapes for _ in range(4)]
                 + [jax.ShapeDtypeStruct((1, 1), F32), jax.ShapeDtypeStruct((SEQ, D_MODEL), F32)])
    outs = pl.pallas_call(
        body, name="adam", grid=(steps,), in_specs=in_specs, out_specs=tuple(out_specs), out_shape=tuple(out_shape),
        compiler_params=_params(dimension_semantics=("arbitrary",)),
    )(*big_in, *big_out, gsum, *small, grad_x)
    return outs[0:4], outs[4:8], [outs[8 + 4 * s:12 + 4 * s] for s in range(6)], outs[32], outs[33]


def _rows_first(a):
    return jnp.transpose(a, (1, 0, 2))


def kernel(x, norm_in, w_in, conv_w, attn_sinks, norm_conv_out, norm_attn_out, w_out, norm_final, loss_target, m_norm_in, m_w_in, m_conv_w, m_attn_sinks, m_norm_conv_out, m_norm_attn_out, m_w_out, m_norm_final, v_norm_in, v_w_in, v_conv_w, v_attn_sinks, v_norm_conv_out, v_norm_attn_out, v_w_out, v_norm_final):
    x2d = x.reshape(SEQ, D_MODEL)
    target = loss_target.reshape(SEQ, D_MODEL)
    nf = norm_final.reshape(1, D_MODEL)

    w_in_t, m_w_in_t, v_w_in_t = w_in[0].T, m_w_in[0].T, v_w_in[0].T
    tiles = jnp.asarray(TILE_ORDER, jnp.int32).reshape(-1)
    w_in_full, h, proj, g_out, conv_full = _gather_in_proj(x2d, norm_in, w_in_t, w_out[0], _rows_first(conv_w), tiles)
    sinks = attn_sinks.reshape(N_Q_HEADS)

    mixed, attn, probs, shares = _mix_fwd(proj, conv_full, sinks, norm_conv_out, norm_attn_out)
    dx2, dx2b, dmixed, gnf, loss_part = _out_proj_loss(mixed, x2d, target, g_out.reshape(D_MIX, D_MODEL), nf)
    dproj, gslab = _mix_bwd(proj, dmixed, attn, probs, shares, conv_full, norm_conv_out, norm_attn_out)
    dw_in_chip, g_w_out = _dw_rs(mixed, dx2b, dproj, h)
    grad_x, g_w_in, gsum = _in_bwd_rs(dproj, w_in_full, x2d, dx2, norm_in, dw_in_chip, gslab, gnf, loss_part)

    small = (norm_in, m_norm_in, v_norm_in, attn_sinks, m_attn_sinks, v_attn_sinks,
             norm_conv_out, m_norm_conv_out, v_norm_conv_out, norm_attn_out, m_norm_attn_out, v_norm_attn_out,
             nf, m_norm_final.reshape(1, D_MODEL), v_norm_final.reshape(1, D_MODEL),
             _rows_first(conv_w), _rows_first(m_conv_w), _rows_first(v_conv_w))
    big_in, big_out, (s_ni, s_sk, s_nc, s_na, s_nf, s_cv), loss, grad_x = _adam_all(
        (w_in_t, g_w_in, m_w_in_t, v_w_in_t), (w_out[0], g_w_out, m_w_out[0], v_w_out[0]), gsum, small, grad_x)

    def leaves(k):
        return (s_ni[k], big_in[k].T[None], jnp.transpose(s_cv[k], (1, 0, 2)), s_sk[k], s_nc[k], s_na[k], big_out[k][None],
                s_nf[k].reshape(D_MODEL))

    return (loss.reshape(()), grad_x.reshape(1, SEQ, D_MODEL), *leaves(0), *leaves(1), *leaves(2), *leaves(3))
```

```python
import jax
import jax.numpy as jnp
from jax import lax
from jax.experimental import pallas as pl
from jax.experimental.pallas import tpu as pltpu

F32 = jnp.float32
BF16 = jnp.bfloat16
MESH = pl.DeviceIdType.MESH

N_DEV = 8
SEQ = 2048
D_MODEL = 1024
D_CONV = 1024
D_ATTN = 1024
D_KV = 128
HEAD_DIM = 64
N_Q_HEADS = 16
N_PAIRS = N_Q_HEADS // 2
PAIRS_PER_KV = N_PAIRS // 2
D_MIX = D_CONV + D_ATTN
D_PROJ = 6400
SHARD_IN = D_PROJ // N_DEV
SHARD_OUT = D_MIX // N_DEV
SHARD_CONV = D_CONV // N_DEV
OFF_CB, OFF_CC, OFF_CU, OFF_GC, OFF_Q, OFF_K, OFF_V, OFF_GA = 0, 1024, 2048, 3072, 4096, 5120, 5248, 5376
BLOCK = 128
N_BLOCKS = SEQ // BLOCK
HALO = 8
CHUNK = 16
N_CHUNKS = BLOCK // CHUNK
RMS_EPS = 1e-5
NEG = -1e30
SCALE = HEAD_DIM ** -0.5
SLOPES = tuple(2.0 ** (-8.0 * (h + 1) / N_Q_HEADS) for h in range(N_Q_HEADS))

ADAM_LR = 0.001
ADAM_B1 = 0.9
ADAM_B2 = 0.999
ADAM_EPS = 1e-08
ADAM_WD = 0.01
ADAM_STEP = 10

ROW_NORM_IN, ROW_NORM_CONV, ROW_NORM_ATTN, ROW_NORM_FINAL, ROW_CONV0, ROW_SINKS = 0, 1, 2, 3, 4, 7
LOSS_LANE = N_Q_HEADS
ACC_NORM_CONV, ACC_NORM_ATTN, ACC_CONV0, N_ACC = 0, 1, 2, 5

VMEM_LIMIT = 56 * 1024 * 1024

_NT = (((1,), (1,)), ((), ()))
_TN = (((0,), (0,)), ((), ()))


def _params(**kw):
    return pltpu.CompilerParams(vmem_limit_bytes=VMEM_LIMIT, **kw)


def _adamw(w, g, m, v):
    m = ADAM_B1 * m + (1.0 - ADAM_B1) * g
    v = ADAM_B2 * v + (1.0 - ADAM_B2) * (g * g)
    m_hat = m / (1.0 - ADAM_B1 ** ADAM_STEP)
    v_hat = v / (1.0 - ADAM_B2 ** ADAM_STEP)
    delta = -ADAM_LR * (m_hat / (jnp.sqrt(v_hat) + ADAM_EPS) + ADAM_WD * w)
    return delta, m, v


def _sigmoid(t):
    return 1.0 / (1.0 + jnp.exp(-t))


def _slot(px, py, pc):
    return 4 * px + 2 * py + pc


HALF_IN = SHARD_IN // 2
N_GATHER_KINDS = 13
W_OUT_KINDS = N_GATHER_KINDS + 7


IN_PROJ_TILE = 640
TILE_ORDER = ((0, 1, 2, 3, 4, 5, 6, 7, 8, 9), (3, 4, 0, 1, 2, 8, 9, 5, 6, 7),
              (5, 6, 0, 1, 7, 8, 9, 2, 3, 4), (8, 9, 3, 4, 5, 6, 7, 0, 1, 2))
TILES_OWN, TILES_NEIGHBOURS = 2, 7


def _tile(table_ref, p):
    chip = 2 * lax.axis_index("x") + lax.axis_index("y")
    return table_ref[chip * len(TILE_ORDER[0]) + p]


def _gather_in_proj(x, norm_in, w_in_sh, w_out_sh, conv_sh, tiles):
    tn = IN_PROJ_TILE
    steps = D_PROJ // tn
    tm = 256

    def body(tiles_ref, x_hbm, g_ref, win_ref, wout_ref, cv_ref, wt_ref, h_ref, proj_ref, gout_ref, conv_ref,
             gin_ref, gcv_ref, wob_ref, x_ref, send_sems, recv_sems, local_sems):
        p = pl.program_id(0)
        local_sem = local_sems.at[0]
        x, y, c = lax.axis_index("x"), lax.axis_index("y"), lax.axis_index("c")
        me, sibling = (x, y, c), (x, y, 1 - c)
        nx, ny, dg = (1 - x, y, c), (x, 1 - y, c), (1 - x, 1 - y, c)

        def other(dev):
            return (dev[0], dev[1], 1 - dev[2])

        def shard(dev):
            return gin_ref.at[pl.ds(pl.multiple_of(_slot(*dev) * SHARD_IN, 16), SHARD_IN), :]

        def half(dev, h):
            return gin_ref.at[pl.ds(pl.multiple_of(_slot(*dev) * SHARD_IN + h * HALF_IN, 16), HALF_IN), :]

        def rc(ref, k, to):
            return pltpu.make_async_remote_copy(src_ref=ref, dst_ref=ref, send_sem=send_sems.at[k],
                                                recv_sem=recv_sems.at[k], device_id=to, device_id_type=MESH)

        def cv(k, dev, to):
            s = _slot(*dev)
            return pltpu.make_async_remote_copy(src_ref=gcv_ref.at[s], dst_ref=gcv_ref.at[s],
                                                send_sem=send_sems.at[N_GATHER_KINDS + k],
                                                recv_sem=recv_sems.at[N_GATHER_KINDS + k], device_id=to, device_id_type=MESH)

        def own_copies():
            return [rc(shard(me), 0, sibling),
                    rc(half(me, 0), 1, nx), rc(half(me, 1), 2, nx),
                    rc(half(me, 1), 4, ny), rc(half(me, 0), 3, ny),
                    cv(0, me, sibling)] + [cv(1 + j, me, peer) for j, peer in enumerate((nx, ny, dg))]

        def pass_on(dev, h, k_in, k_ici, k_d2d, half=half, base=0):
            rc(half(dev, h), base + k_in, me).wait_recv()
            if k_ici is not None:
                rc(half(dev, h), base + k_ici, ny if dev is nx else nx).start()
            rc(half(dev, h), base + k_d2d, sibling).start()

        def out_half(dev, h):
            return gout_ref.at[_slot(*dev), pl.ds(h * (SHARD_OUT // 2), SHARD_OUT // 2), :]

        def own_out_copies():
            src = lambda h: wob_ref.at[pl.ds(h * (SHARD_OUT // 2), SHARD_OUT // 2), :]

            def send(ref, dst, k, to):
                return pltpu.make_async_remote_copy(src_ref=ref, dst_ref=dst, send_sem=send_sems.at[W_OUT_KINDS + k],
                                                    recv_sem=recv_sems.at[W_OUT_KINDS + k], device_id=to, device_id_type=MESH)

            return [send(wob_ref, gout_ref.at[_slot(*me)], 0, sibling),
                    send(src(0), out_half(me, 0), 1, nx), send(src(1), out_half(me, 1), 2, nx),
                    send(src(1), out_half(me, 1), 4, ny), send(src(0), out_half(me, 0), 3, ny)]

        def own_out_local():
            return pltpu.make_async_copy(wob_ref, gout_ref.at[_slot(*me)], local_sems.at[1])

        @pl.when(p == 0)
        def _():
            gin_ref[pl.ds(pl.multiple_of(_slot(*me) * SHARD_IN, 16), SHARD_IN), :] = win_ref[...].astype(BF16)
            gcv_ref[_slot(*me)] = jnp.zeros((8, SHARD_CONV), F32)
            gcv_ref[_slot(*me), 0:3, :] = cv_ref[:, 0, :]
            for cp in own_copies():
                cp.start()
            wob_ref[...] = wout_ref[...].astype(BF16)
            x_load = pltpu.make_async_copy(x_hbm, x_ref, local_sems.at[2])
            x_load.start()
            x_load.wait()
            for t in range(SEQ // tm):
                xv = x_ref[tm * t:tm * (t + 1), :]
                r = lax.rsqrt(jnp.mean(xv * xv, axis=-1, keepdims=True) + RMS_EPS)
                h_ref[tm * t:tm * (t + 1), :] = (xv * r * g_ref[...]).astype(BF16)
            rc(shard(sibling), 0, me).wait_recv()

        @pl.when(p == TILES_OWN)
        def _():
            for args in ((nx, 0, 1, 5, 7), (ny, 1, 4, 6, 10), (nx, 1, 2, None, 8), (ny, 0, 3, None, 9)):
                pass_on(*args)
            for j, peer in enumerate((nx, ny, dg)):
                cv(1 + j, peer, me).wait_recv()
                cv(4 + j, peer, sibling).start()
            for (dev, h), k in (((nx, 0), 7), ((nx, 1), 8), ((ny, 0), 9), ((ny, 1), 10)):
                rc(half(other(dev), h), k, me).wait_recv()
            own_out_local().start()
            for cp in own_out_copies():
                cp.start()

        @pl.when(p == TILES_NEIGHBOURS - 1)
        def _():
            pass_on(dg, 0, 5, None, 11)
            pass_on(dg, 1, 6, None, 12)

        @pl.when(p == TILES_NEIGHBOURS)
        def _():
            for (dev, h), k in (((dg, 0), 11), ((dg, 1), 12)):
                rc(half(other(dev), h), k, me).wait_recv()
            pltpu.make_async_copy(gin_ref, wt_ref, local_sem).start()

        @pl.when(p == steps - 2)
        def _():
            for args in ((nx, 0, 1, 5, 7), (ny, 1, 4, 6, 10), (nx, 1, 2, None, 8), (ny, 0, 3, None, 9)):
                pass_on(*args, half=out_half, base=W_OUT_KINDS)

        w = gin_ref[pl.ds(pl.multiple_of(_tile(tiles_ref, p) * tn, tn), tn), :]
        proj_ref[...] = lax.dot_general(h_ref[...], w, _NT, preferred_element_type=F32)

        @pl.when(p == steps - 1)
        def _():
            cv(0, sibling, me).wait_recv()
            for j, peer in enumerate((nx, ny, dg)):
                cv(4 + j, other(peer), me).wait_recv()
            for d in range(N_DEV):
                conv_ref[:, d * SHARD_CONV:(d + 1) * SHARD_CONV] = gcv_ref[d]
            relayed = [rc(half(nx, 0), 5, ny), rc(half(ny, 1), 6, nx)]
            relayed += [rc(half(dev, h), k, sibling) for (dev, h), k in
                        (((nx, 0), 7), ((nx, 1), 8), ((ny, 0), 9), ((ny, 1), 10), ((dg, 0), 11), ((dg, 1), 12))]
            relayed += [cv(4 + j, peer, sibling) for j, peer in enumerate((nx, ny, dg))]
            for cp in own_copies() + relayed:
                cp.wait_send()
            pltpu.make_async_copy(gin_ref, wt_ref, local_sem).wait()
            pass_on(dg, 0, 5, None, 11, half=out_half, base=W_OUT_KINDS)
            pass_on(dg, 1, 6, None, 12, half=out_half, base=W_OUT_KINDS)
            rc(gout_ref.at[_slot(*sibling)], W_OUT_KINDS, me).wait_recv()
            out_relayed = [rc(out_half(nx, 0), W_OUT_KINDS + 5, ny), rc(out_half(ny, 1), W_OUT_KINDS + 6, nx)]
            for (dev, h), k in (((nx, 0), 7), ((nx, 1), 8), ((ny, 0), 9), ((ny, 1), 10), ((dg, 0), 11), ((dg, 1), 12)):
                rc(out_half(other(dev), h), W_OUT_KINDS + k, me).wait_recv()
                out_relayed.append(rc(out_half(dev, h), W_OUT_KINDS + k, sibling))
            for cp in own_out_copies() + out_relayed:
                cp.wait_send()
            own_out_local().wait()

    vmem = pl.BlockSpec(memory_space=pltpu.VMEM)
    grid_spec = pltpu.PrefetchScalarGridSpec(
        num_scalar_prefetch=1, grid=(steps,),
        in_specs=[pl.BlockSpec(memory_space=pl.ANY), vmem, vmem, vmem, vmem],
        out_specs=(pl.BlockSpec(memory_space=pl.ANY), vmem,
                   pl.BlockSpec((SEQ, tn), lambda p, tiles_ref: (0, _tile(tiles_ref, p))),
                   pl.BlockSpec(memory_space=pl.ANY), vmem),
        scratch_shapes=[pltpu.VMEM((D_PROJ, D_MODEL), BF16), pltpu.VMEM((N_DEV, 8, SHARD_CONV), F32),
                        pltpu.VMEM((SHARD_OUT, D_MODEL), BF16), pltpu.VMEM((SEQ, D_MODEL), F32),
                        pltpu.SemaphoreType.DMA((W_OUT_KINDS + N_GATHER_KINDS,)),
                        pltpu.SemaphoreType.DMA((W_OUT_KINDS + N_GATHER_KINDS,)),
                        pltpu.SemaphoreType.DMA((3,))])
    return pl.pallas_call(
        body, name="gather_in_proj", grid_spec=grid_spec,
        out_shape=(jax.ShapeDtypeStruct((D_PROJ, D_MODEL), BF16), jax.ShapeDtypeStruct((SEQ, D_MODEL), BF16),
                   jax.ShapeDtypeStruct((SEQ, D_PROJ), F32), jax.ShapeDtypeStruct((N_DEV, SHARD_OUT, D_MODEL), BF16),
                   jax.ShapeDtypeStruct((8, D_CONV), F32)),
        compiler_params=_params(dimension_semantics=("arbitrary",)),
    )(tiles, x, norm_in, w_in_sh, w_out_sh, conv_sh)


def _shard_sum(src, own, d2d, ici, send_sems, recv_sems, local_sems, base=0):
    x, y, c = lax.axis_index("x"), lax.axis_index("y"), lax.axis_index("c")
    sibling = (x, y, 1 - c)
    chips = [(x, y), (1 - x, y), (x, 1 - y), (1 - x, 1 - y)]

    def rcopy(s, d, k, to):
        return pltpu.make_async_remote_copy(src_ref=s, dst_ref=d, send_sem=send_sems.at[base + k],
                                            recv_sem=recv_sems.at[base + k], device_id=to, device_id_type=MESH)

    def mine(k):
        return pltpu.make_async_copy(src.at[_slot(*chips[k], c)], own.at[k], local_sems.at[k])

    def to_sibling(k):
        return rcopy(src.at[_slot(*chips[k], 1 - c)], d2d.at[k], k, sibling)

    def to_chip(k):
        return rcopy(own.at[k], ici.at[k - 1], 3 + k, (*chips[k], c))

    def start():
        for k in range(4):
            mine(k).start()
            to_sibling(k).start()

    def forward():
        for k in range(1, 4):
            mine(k).wait()
            to_sibling(k).wait_recv()
            own[k] = (own[k].astype(F32) + d2d[k].astype(F32)).astype(BF16)
            to_chip(k).start()

    def finish():
        mine(0).wait()
        to_sibling(0).wait_recv()
        acc = own[0].astype(F32) + d2d[0].astype(F32)
        for k in range(1, 4):
            to_chip(k).wait_recv()
            acc = acc + ici[k - 1].astype(F32)
        for k in range(4):
            to_sibling(k).wait_send()
        for k in range(1, 4):
            to_chip(k).wait_send()
        return acc

    return start, forward, finish


def _shard_sum_scratch(rows):
    return [pltpu.VMEM((4, rows, D_MODEL), BF16), pltpu.VMEM((4, rows, D_MODEL), BF16),
            pltpu.VMEM((3, rows, D_MODEL), BF16)]


N_SHARD_SUM_SEMS = 7


def _chip_sum(dwt, d2d, out_hbm, send_sems, recv_sems, local_sems, base, local_base):
    x, y, c = lax.axis_index("x"), lax.axis_index("y"), lax.axis_index("c")
    sibling = (x, y, 1 - c)
    chips = [(x, y), (1 - x, y), (x, 1 - y), (1 - x, 1 - y)]

    def shard(s):
        return dwt.at[pl.ds(pl.multiple_of(s * SHARD_IN, 16), SHARD_IN), :]

    def to_sibling(k):
        return pltpu.make_async_remote_copy(src_ref=shard(_slot(*chips[k], 1 - c)), dst_ref=d2d.at[k],
                                            send_sem=send_sems.at[base + k], recv_sem=recv_sems.at[base + k],
                                            device_id=sibling, device_id_type=MESH)

    def save(k):
        return pltpu.make_async_copy(d2d.at[k], out_hbm.at[k], local_sems.at[local_base + k])

    def send(rows_before, rows_done):
        for k in range(4):
            end = (_slot(*chips[k], 1 - c) + 1) * SHARD_IN

            @pl.when((end > rows_before) & (end <= rows_done))
            def _():
                to_sibling(k).start()

    def finish():
        for k in range(4):
            to_sibling(k).wait_recv()
            d2d[k] = (shard(_slot(*chips[k], c))[...].astype(F32) + d2d[k].astype(F32)).astype(BF16)
            save(k).start()
        for k in range(4):
            save(k).wait()
            to_sibling(k).wait_send()

    return send, finish


N_ICI_SUM_SEMS = 6


def _ici_sum(src, own, ici, via, stage, send_sems, recv_sems, local_sems, base=0):
    x, y, c = lax.axis_index("x"), lax.axis_index("y"), lax.axis_index("c")
    nx, ny = (1 - x, y, c), (x, 1 - y, c)
    OWN, NX, NY, DG = range(4)

    def half(ref, h):
        return ref.at[pl.ds(h * HALF_IN, HALF_IN), :]

    def rc(s, d, k, to):
        return pltpu.make_async_remote_copy(src_ref=s, dst_ref=d, send_sem=send_sems.at[base + k],
                                            recv_sem=recv_sems.at[base + k], device_id=to, device_id_type=MESH)

    for_dg_0 = lambda: rc(half(src.at[DG], 0), via.at[0], 0, nx)
    for_dg_1 = lambda: rc(half(src.at[DG], 1), via.at[1], 1, ny)
    for_nx_0 = lambda: rc(half(src.at[NX], 0), half(ici.at[0], 0), 2, nx)
    for_ny_1 = lambda: rc(half(src.at[NY], 1), half(ici.at[1], 1), 3, ny)
    for_ny_0 = lambda: rc(stage.at[0], half(ici.at[1], 0), 4, ny)
    for_nx_1 = lambda: rc(stage.at[1], half(ici.at[0], 1), 5, nx)
    mine = lambda: pltpu.make_async_copy(src.at[OWN], own, local_sems.at[0])
    stage_0 = lambda: pltpu.make_async_copy(half(src.at[NY], 0), stage.at[0], local_sems.at[1])
    stage_1 = lambda: pltpu.make_async_copy(half(src.at[NX], 1), stage.at[1], local_sems.at[2])

    def start():
        for cp in (for_dg_0, for_dg_1, for_nx_0, for_ny_1, stage_0, stage_1, mine):
            cp().start()

    def relay():
        for h, staged, landed, out in ((0, stage_0, for_dg_0, for_ny_0), (1, stage_1, for_dg_1, for_nx_1)):
            staged().wait()
            landed().wait_recv()
            stage[h] = (stage[h].astype(F32) + via[h].astype(F32)).astype(BF16)
            out().start()

    def finish():
        mine().wait()
        for cp in (for_nx_0, for_nx_1, for_ny_1, for_ny_0):
            cp().wait_recv()
        acc = own[...].astype(F32) + ici[0].astype(F32) + ici[1].astype(F32)
        for cp in (for_dg_0, for_dg_1, for_nx_0, for_ny_1, for_ny_0, for_nx_1):
            cp().wait_send()
        return acc

    return start, relay, finish


def _slab_sum(myslab, slabs, send_sems, recv_sems, base):
    x, y, c = lax.axis_index("x"), lax.axis_index("y"), lax.axis_index("c")
    me = _slot(x, y, c)
    peers = [(x, y, 1 - c), (1 - x, y, c), (x, 1 - y, c), (1 - x, 1 - y, c),
             (1 - x, y, 1 - c), (x, 1 - y, 1 - c), (1 - x, 1 - y, 1 - c)]

    def cp(k):
        return pltpu.make_async_remote_copy(src_ref=myslab, dst_ref=slabs.at[me], send_sem=send_sems.at[base + k],
                                            recv_sem=recv_sems.at[base + k], device_id=peers[k], device_id_type=MESH)

    def start():
        slabs[me] = myslab[...]
        for k in range(7):
            cp(k).start()

    def finish():
        for k in range(7):
            cp(k).wait_recv()
        total = slabs[0]
        for d in range(1, N_DEV):
            total = total + slabs[d]
        for k in range(7):
            cp(k).wait_send()
        return total

    return start, finish


def _chunk_rows(r):
    return slice(r * CHUNK, (r + 1) * CHUNK)


def _conv_halo(cch_ref, cuh_ref, n):
    zh = jnp.where(n > 0, cch_ref[...] * cuh_ref[...], 0.0)
    return jnp.concatenate([zh] * (CHUNK // HALO), axis=0)


def _conv_chunk(pj_ref, zhalo, cw, r):
    rows = _chunk_rows(r)
    cc = pj_ref[rows, OFF_CC:OFF_CC + D_CONV]
    cu = pj_ref[rows, OFF_CU:OFF_CU + D_CONV]
    z = cc * cu
    before = _chunk_rows(r - 1)
    zprev = pj_ref[before, OFF_CC:OFF_CC + D_CONV] * pj_ref[before, OFF_CU:OFF_CU + D_CONV] if r > 0 else zhalo
    row = lax.broadcasted_iota(jnp.int32, (CHUNK, D_CONV), 0)
    z1 = jnp.where(row < 1, pltpu.roll(zprev, 1, 0), pltpu.roll(z, 1, 0))
    z2 = jnp.where(row < 2, pltpu.roll(zprev, 2, 0), pltpu.roll(z, 2, 0))
    co = cw[0] * z2 + cw[1] * z1 + cw[2] * z
    return cc, cu, z, z1, z2, co


def _gated_norm(a, gain, t):
    r = lax.rsqrt(jnp.mean(a * a, axis=-1, keepdims=True) + RMS_EPS)
    return a * r * gain * (t * _sigmoid(t))


def _kv_bands(pj, kvp_ref):
    lane = lax.broadcasted_iota(jnp.int32, (2 * BLOCK, D_KV), 1)
    lo = lane < HEAD_DIM

    def bands(prev, cur):
        b = jnp.concatenate([prev, cur], axis=0)
        br = pltpu.roll(b, HEAD_DIM, 1)
        zero = jnp.zeros_like(b)
        return ((jnp.where(lo, b, zero).astype(BF16), jnp.where(lo, zero, br).astype(BF16)),
                (jnp.where(lo, br, zero).astype(BF16), jnp.where(lo, zero, b).astype(BF16)))

    ks = bands(kvp_ref[:, 0:D_KV], pj[:, OFF_K:OFF_K + D_KV])
    vs = bands(kvp_ref[:, D_KV:2 * D_KV], pj[:, OFF_V:OFF_V + D_KV])
    return ks, vs


STACK = PAIRS_PER_KV * BLOCK


def _head(j, i, e):
    return 2 * (PAIRS_PER_KV * j + i) + e


def _pair_cols(j, i, off):
    p = PAIRS_PER_KV * j + i
    return slice(off + 128 * p, off + 128 * (p + 1))


def _fill_attn_bias(bias_scr, first_block):
    qi = lax.broadcasted_iota(jnp.int32, (BLOCK, 2 * BLOCK), 0)
    kj = lax.broadcasted_iota(jnp.int32, (BLOCK, 2 * BLOCK), 1)
    dist = BLOCK + qi - kj
    valid = (dist >= 0) & (dist < BLOCK)
    if first_block:
        valid = valid & (kj >= BLOCK)
    distf = dist.astype(F32)
    for j in range(2):
        for e in range(2):
            for i in range(PAIRS_PER_KV):
                bias_scr[2 * j + e, BLOCK * i:BLOCK * (i + 1), :] = jnp.where(valid, -SLOPES[_head(j, i, e)] * distf, NEG)


def _q_stack(pj, j):
    return jnp.concatenate([(pj[:, _pair_cols(j, i, OFF_Q)] * SCALE).astype(BF16) for i in range(PAIRS_PER_KV)], axis=0)


def _attn_probs(q_stack, kband, bias_ref, sinks):
    s = lax.dot_general(q_stack, kband, _NT, preferred_element_type=F32)
    ones = jnp.ones((128, 128), BF16)
    probs, shares = [], []
    for i, sink in enumerate(sinks):
        rows = slice(BLOCK * i, BLOCK * (i + 1))
        t = s[rows, :] + bias_ref[rows, :]
        m = jnp.broadcast_to(jnp.max(t, axis=-1, keepdims=True), (BLOCK, 128))
        m = jnp.maximum(m, sink)
        p = [jnp.exp(t[:, :128] - m), jnp.exp(t[:, 128:] - m)]
        es = jnp.exp(sink - m)
        total = (jnp.dot(p[0].astype(BF16), ones, preferred_element_type=F32)
                 + jnp.dot(p[1].astype(BF16), ones, preferred_element_type=F32))
        inv = 1.0 / (total + es)
        probs.append(jnp.concatenate([p[0] * inv, p[1] * inv], axis=1))
        shares.append(es * inv)
    return jnp.concatenate(probs, axis=0), jnp.concatenate(shares, axis=0)


def _attn_group(pj, ks, vs, bias_scr, sink_ref, j):
    q_stack = _q_stack(pj, j)
    out, probs, shares = None, [], []
    for e in range(2):
        p, ps = _attn_probs(q_stack, ks[j][e], bias_scr.at[2 * j + e],
                            [sink_ref[_head(j, i, e)] for i in range(PAIRS_PER_KV)])
        p = p.astype(BF16)
        o = jnp.dot(p, vs[j][e], preferred_element_type=F32)
        out = o if out is None else out + o
        probs.append(p)
        shares.append(ps)
    return out, probs, shares


def _mix_fwd(proj, conv_full, sinks, norm_conv, norm_attn):
    def body(pj_ref, kvp_ref, cch_ref, cuh_ref, cw_ref, sink_ref, gc_ref, ga_ref,
             mixed_ref, attn_scr, p_ref, ps_ref, bias_scr):
        n = pl.program_id(0)
        pj = pj_ref

        @pl.when(n == 0)
        def _():
            _fill_attn_bias(bias_scr, first_block=True)

        @pl.when(n == 1)
        def _():
            _fill_attn_bias(bias_scr, first_block=False)

        zhalo = _conv_halo(cch_ref, cuh_ref, n)
        cw = (cw_ref[0:1, :], cw_ref[1:2, :], cw_ref[2:3, :])
        gain_c = gc_ref[...]

        for r in range(N_CHUNKS):
            rows = _chunk_rows(r)
            co = _conv_chunk(pj_ref, zhalo, cw, r)[-1]
            y = _gated_norm(pj_ref[rows, OFF_CB:OFF_CB + D_CONV] * co, gain_c, pj_ref[rows, OFF_GC:OFF_GC + D_CONV])
            mixed_ref[rows, 0:D_CONV] = y.astype(BF16)

        ks, vs = _kv_bands(pj, kvp_ref)
        for j in range(2):
            out, probs, shares = _attn_group(pj, ks, vs, bias_scr, sink_ref, j)
            for e in range(2):
                p_ref[0, 2 * j + e] = probs[e]
                ps_ref[0, 2 * j + e] = shares[e]
            for i in range(PAIRS_PER_KV):
                attn_scr[:, _pair_cols(j, i, 0)] = out[BLOCK * i:BLOCK * (i + 1), :]
        gain_a = ga_ref[...]

        for r in range(N_CHUNKS):
            rows = _chunk_rows(r)
            y = _gated_norm(attn_scr[rows, :], gain_a, pj_ref[rows, OFF_GA:OFF_GA + D_ATTN])
            mixed_ref[rows, D_CONV:D_MIX] = y.astype(BF16)

    per_block = BLOCK // HALO
    return pl.pallas_call(
        body, name="mix_fwd", grid=(N_BLOCKS,),
        in_specs=[
            pl.BlockSpec((BLOCK, D_PROJ), lambda n: (n, 0)),
            pl.BlockSpec((BLOCK, 2 * D_KV), lambda n: (jnp.maximum(n - 1, 0), OFF_K // (2 * D_KV))),
            pl.BlockSpec((HALO, D_CONV), lambda n: (jnp.maximum(n * per_block - 1, 0), OFF_CC // D_CONV)),
            pl.BlockSpec((HALO, D_CONV), lambda n: (jnp.maximum(n * per_block - 1, 0), OFF_CU // D_CONV)),
            pl.BlockSpec((8, D_CONV), lambda n: (0, 0)),
            pl.BlockSpec(memory_space=pltpu.SMEM),
            pl.BlockSpec((1, D_CONV), lambda n: (0, 0)),
            pl.BlockSpec((1, D_ATTN), lambda n: (0, 0)),
        ],
        out_specs=(pl.BlockSpec((BLOCK, D_MIX), lambda n: (n, 0)), pl.BlockSpec((BLOCK, D_ATTN), lambda n: (n, 0)),
                   pl.BlockSpec((1, 4, STACK, 2 * BLOCK), lambda n: (n, 0, 0, 0)),
                   pl.BlockSpec((1, 4, STACK, 128), lambda n: (n, 0, 0, 0))),
        out_shape=(jax.ShapeDtypeStruct((SEQ, D_MIX), BF16), jax.ShapeDtypeStruct((SEQ, D_ATTN), F32),
                   jax.ShapeDtypeStruct((N_BLOCKS, 4, STACK, 2 * BLOCK), BF16),
                   jax.ShapeDtypeStruct((N_BLOCKS, 4, STACK, 128), F32)),
        scratch_shapes=[pltpu.VMEM((4, STACK, 2 * BLOCK), F32)],
        compiler_params=_params(dimension_semantics=("arbitrary",)),
    )(proj, proj, proj, proj, conv_full, sinks, norm_conv, norm_attn)


def _out_proj_loss(mixed, x, target, w_out_full, norm_final):
    tm = 256

    def body(mx_ref, x_ref, t_ref, w_ref, g_ref, dx2_ref, dx2b_ref, dmix_ref, gnf_ref, loss_ref):
        i = pl.program_id(0)
        w = w_ref[...]
        x2 = x_ref[...] + jnp.dot(mx_ref[...], w, preferred_element_type=F32)
        r = lax.rsqrt(jnp.mean(x2 * x2, axis=-1, keepdims=True) + RMS_EPS)
        xn = x2 * r
        g = g_ref[...]
        err = xn * g - t_ref[...]
        part = 0.5 * jnp.sum(jnp.mean(err * err, axis=-1, keepdims=True), axis=0, keepdims=True)
        dy = err * (1.0 / D_MODEL)
        gnf = jnp.sum(dy * xn, axis=0, keepdims=True)
        u = dy * g
        dx2 = r * (u - xn * jnp.mean(u * xn, axis=-1, keepdims=True))
        dx2_ref[...] = dx2
        dx2b = dx2.astype(BF16)
        dx2b_ref[...] = dx2b
        dmix_ref[...] = lax.dot_general(dx2b, w, _NT, preferred_element_type=F32)

        @pl.when(i == 0)
        def _():
            gnf_ref[...] = jnp.zeros_like(gnf_ref)
            loss_ref[...] = jnp.zeros_like(loss_ref)

        gnf_ref[...] += gnf
        loss_ref[...] += jnp.broadcast_to(part, loss_ref.shape)

    return pl.pallas_call(
        body, name="out_proj_loss", grid=(SEQ // tm,),
        in_specs=[pl.BlockSpec((tm, D_MIX), lambda i: (i, 0)), pl.BlockSpec((tm, D_MODEL), lambda i: (i, 0)),
                  pl.BlockSpec((tm, D_MODEL), lambda i: (i, 0)), pl.BlockSpec(memory_space=pltpu.VMEM),
                  pl.BlockSpec((1, D_MODEL), lambda i: (0, 0))],
        out_specs=(pl.BlockSpec((tm, D_MODEL), lambda i: (i, 0)), pl.BlockSpec((tm, D_MODEL), lambda i: (i, 0)),
                   pl.BlockSpec((tm, D_MIX), lambda i: (i, 0)),
                   pl.BlockSpec((1, D_MODEL), lambda i: (0, 0)), pl.BlockSpec((8, 128), lambda i: (0, 0))),
        out_shape=(jax.ShapeDtypeStruct((SEQ, D_MODEL), F32), jax.ShapeDtypeStruct((SEQ, D_MODEL), BF16),
                   jax.ShapeDtypeStruct((SEQ, D_MIX), F32),
                   jax.ShapeDtypeStruct((1, D_MODEL), F32), jax.ShapeDtypeStruct((8, 128), F32)),
        compiler_params=_params(dimension_semantics=("arbitrary",)),
    )(mixed, x, target, w_out_full, norm_final)


def _gated_norm_bwd(a, gain, t, dy):
    r = lax.rsqrt(jnp.mean(a * a, axis=-1, keepdims=True) + RMS_EPS)
    an = a * r
    sg = _sigmoid(t)
    dn = dy * (t * sg)
    dt = dy * (an * gain) * (sg * (1.0 + t * (1.0 - sg)))
    u = dn * gain
    da = r * (u - an * jnp.mean(u * an, axis=-1, keepdims=True))
    return da, dt, dn * an


def _mix_bwd(proj, dmixed, attn, probs, shares, conv_full, norm_conv, norm_attn):
    def body(pj_ref, kvp_ref, cch_ref, cuh_ref, dmx_ref, attn_ref, p_ref, ps_ref, cw_ref, gc_ref, ga_ref,
             dpj_ref, gslab_ref, dattn_scr, nxt_scr, dkv_scr, acc_scr):
        step = pl.program_id(0)
        n = N_BLOCKS - 1 - step
        pj = pj_ref

        @pl.when(step == 0)
        def _():
            gslab_ref[...] = jnp.zeros_like(gslab_ref)
            nxt_scr[...] = jnp.zeros_like(nxt_scr)
            dkv_scr[...] = jnp.zeros_like(dkv_scr)
            acc_scr[...] = jnp.zeros_like(acc_scr)

        zhalo = _conv_halo(cch_ref, cuh_ref, n)
        cw = (cw_ref[0:1, :], cw_ref[1:2, :], cw_ref[2:3, :])
        gain_c = gc_ref[...]
        row = lax.broadcasted_iota(jnp.int32, (CHUNK, D_CONV), 0)

        dco_after = nxt_scr[...]
        for r in reversed(range(N_CHUNKS)):
            rows = _chunk_rows(r)
            cc, cu, z, z1, z2, co = _conv_chunk(pj_ref, zhalo, cw, r)
            cb = pj_ref[rows, OFF_CB:OFF_CB + D_CONV]
            da, dgate, gterm = _gated_norm_bwd(cb * co, gain_c, pj_ref[rows, OFF_GC:OFF_GC + D_CONV],
                                               dmx_ref[rows, 0:D_CONV])
            dpj_ref[rows, OFF_GC:OFF_GC + D_CONV] = dgate.astype(BF16)
            dpj_ref[rows, OFF_CB:OFF_CB + D_CONV] = (da * co).astype(BF16)
            dco = da * cb
            dco1 = jnp.where(row >= CHUNK - 1, pltpu.roll(dco_after, CHUNK - 1, 0), pltpu.roll(dco, CHUNK - 1, 0))
            dco2 = jnp.where(row >= CHUNK - 2, pltpu.roll(dco_after, CHUNK - 2, 0), pltpu.roll(dco, CHUNK - 2, 0))
            dz = cw[2] * dco + cw[1] * dco1 + cw[0] * dco2
            dpj_ref[rows, OFF_CC:OFF_CC + D_CONV] = (dz * cu).astype(BF16)
            dpj_ref[rows, OFF_CU:OFF_CU + D_CONV] = (dz * cc).astype(BF16)
            acc_scr[ACC_NORM_CONV] += gterm
            acc_scr[ACC_CONV0] += dco * z2
            acc_scr[ACC_CONV0 + 1] += dco * z1
            acc_scr[ACC_CONV0 + 2] += dco * z
            dco_after = dco
        nxt_scr[...] = dco_after

        ks, vs = _kv_bands(pj, kvp_ref)
        gain_a = ga_ref[...]

        for r in range(N_CHUNKS):
            rows = _chunk_rows(r)
            da, dgate, gterm = _gated_norm_bwd(attn_ref[rows, :], gain_a, pj_ref[rows, OFF_GA:OFF_GA + D_ATTN],
                                               dmx_ref[rows, D_CONV:D_MIX])
            dpj_ref[rows, OFF_GA:OFF_GA + D_ATTN] = dgate.astype(BF16)
            dattn_scr[rows, :] = da
            acc_scr[ACC_NORM_ATTN] += gterm

        in_lo = lax.broadcasted_iota(jnp.int32, (128, 128), 0) < HEAD_DIM
        half_ones = (jnp.where(in_lo, 1.0, 0.0).astype(BF16), jnp.where(in_lo, 0.0, 1.0).astype(BF16))
        lane_s = lax.broadcasted_iota(jnp.int32, (1, D_MODEL), 1)
        gsink = jnp.zeros((1, D_MODEL), F32)
        dk_t, dv_t = [], []
        for j in range(2):
            q_stack = _q_stack(pj, j)
            do_f = jnp.concatenate([dattn_scr[:, _pair_cols(j, i, 0)] for i in range(PAIRS_PER_KV)], axis=0)
            o_f = jnp.concatenate([attn_ref[:, _pair_cols(j, i, 0)] for i in range(PAIRS_PER_KV)], axis=0)
            prod = (do_f * o_f).astype(BF16)
            deltas = [jnp.dot(prod, half_ones[e], preferred_element_type=F32) for e in range(2)]
            do_b = do_f.astype(BF16)
            q_t, do_t = q_stack.T, do_b.T
            dq, dk_j, dv_j = None, None, None
            for e in range(2):
                p = p_ref[0, 2 * j + e]
                dp = lax.dot_general(do_b, vs[j][e], _NT, preferred_element_type=F32)
                ds = []
                for i in range(PAIRS_PER_KV):
                    rows = slice(BLOCK * i, BLOCK * (i + 1))
                    delta = deltas[e][rows, :]
                    ds.append((p[rows, :].astype(F32) * (dp[rows, :] - jnp.concatenate([delta, delta], axis=1))).astype(BF16))
                    gs_h = -jnp.sum(ps_ref[0, 2 * j + e, rows, 0:1] * delta[:, 0:1], axis=0, keepdims=True)
                    gsink = gsink + jnp.where(lane_s == _head(j, i, e), gs_h, 0.0)
                ds = jnp.concatenate(ds, axis=0)
                t = jnp.dot(ds, ks[j][e], preferred_element_type=F32)
                dq = t if dq is None else dq + t
                half = slice(HEAD_DIM * e, HEAD_DIM * (e + 1))
                a = jnp.dot(q_t[half, :], ds, preferred_element_type=F32)
                b = jnp.dot(do_t[half, :], p, preferred_element_type=F32)
                dk_j = a if dk_j is None else dk_j + a
                dv_j = b if dv_j is None else dv_j + b
            for i in range(PAIRS_PER_KV):
                dpj_ref[:, _pair_cols(j, i, OFF_Q)] = (dq[BLOCK * i:BLOCK * (i + 1), :] * SCALE).astype(BF16)
            dk_t.append(dk_j)
            dv_t.append(dv_j)
        dk = jnp.concatenate(dk_t, axis=0).T
        dv = jnp.concatenate(dv_t, axis=0).T
        dpj_ref[:, OFF_K:OFF_K + D_KV] = (dk[BLOCK:, :] + dkv_scr[:, 0:D_KV]).astype(BF16)
        dpj_ref[:, OFF_V:OFF_V + D_KV] = (dv[BLOCK:, :] + dkv_scr[:, D_KV:2 * D_KV]).astype(BF16)
        dkv_scr[:, 0:D_KV] = dk[:BLOCK, :]
        dkv_scr[:, D_KV:2 * D_KV] = dv[:BLOCK, :]
        gslab_ref[ROW_SINKS:ROW_SINKS + 1, :] += gsink

        @pl.when(step == N_BLOCKS - 1)
        def _():
            for k, slab_row in ((ACC_NORM_CONV, ROW_NORM_CONV), (ACC_NORM_ATTN, ROW_NORM_ATTN), (ACC_CONV0, ROW_CONV0),
                                (ACC_CONV0 + 1, ROW_CONV0 + 1), (ACC_CONV0 + 2, ROW_CONV0 + 2)):
                gslab_ref[slab_row:slab_row + 1, :] = jnp.sum(acc_scr[k], axis=0, keepdims=True)

    per_block = BLOCK // HALO
    last = N_BLOCKS - 1
    return pl.pallas_call(
        body, name="mix_bwd", grid=(N_BLOCKS,),
        in_specs=[
            pl.BlockSpec((BLOCK, D_PROJ), lambda s: (last - s, 0)),
            pl.BlockSpec((BLOCK, 2 * D_KV), lambda s: (jnp.maximum(last - s - 1, 0), OFF_K // (2 * D_KV))),
            pl.BlockSpec((HALO, D_CONV), lambda s: (jnp.maximum((last - s) * per_block - 1, 0), OFF_CC // D_CONV)),
            pl.BlockSpec((HALO, D_CONV), lambda s: (jnp.maximum((last - s) * per_block - 1, 0), OFF_CU // D_CONV)),
            pl.BlockSpec((BLOCK, D_MIX), lambda s: (last - s, 0)),
            pl.BlockSpec((BLOCK, D_ATTN), lambda s: (last - s, 0)),
            pl.BlockSpec((1, 4, STACK, 2 * BLOCK), lambda s: (last - s, 0, 0, 0)),
            pl.BlockSpec((1, 4, STACK, 128), lambda s: (last - s, 0, 0, 0)),
            pl.BlockSpec((8, D_CONV), lambda s: (0, 0)),
            pl.BlockSpec((1, D_CONV), lambda s: (0, 0)),
            pl.BlockSpec((1, D_ATTN), lambda s: (0, 0)),
        ],
        out_specs=(pl.BlockSpec((BLOCK, D_PROJ), lambda s: (last - s, 0)),
                   pl.BlockSpec((8, D_MODEL), lambda s: (0, 0))),
        out_shape=(jax.ShapeDtypeStruct((SEQ, D_PROJ), BF16), jax.ShapeDtypeStruct((8, D_MODEL), F32)),
        scratch_shapes=[pltpu.VMEM((BLOCK, D_ATTN), F32), pltpu.VMEM((CHUNK, D_CONV), F32),
                        pltpu.VMEM((BLOCK, 2 * D_KV), F32), pltpu.VMEM((N_ACC, CHUNK, D_MODEL), F32)],
        compiler_params=_params(dimension_semantics=("arbitrary",)),
    )(proj, proj, proj, proj, dmixed, attn, probs, shares, conv_full, norm_conv, norm_attn)


def _in_bwd_rs(dproj, w_full, x, dx2, norm_in, dw_in_chip, gslab, gnf, loss_part):
    tm = 256
    steps = SEQ // tm
    relay_step = 4

    def body(dp_ref, w_hbm, x_ref, dx2_ref, g_ref, dwi_ref, gs_ref, gnf_ref, lp_ref, gx_ref, gwin_ref, gsum_ref,
             gni_scr, own, ici, via, stage, myslab, slabs, w_ref, send_sems, recv_sems, local_sems):
        i = pl.program_id(0)
        rs_start, rs_relay, rs_finish = _ici_sum(dwi_ref, own, ici, via, stage, send_sems, recv_sems, local_sems)
        slab_start, slab_finish = _slab_sum(myslab, slabs, send_sems, recv_sems, N_ICI_SUM_SEMS)

        @pl.when(i == 0)
        def _():
            gni_scr[...] = jnp.zeros_like(gni_scr)
            rs_start()
            w_load = pltpu.make_async_copy(w_hbm, w_ref, local_sems.at[3])
            w_load.start()
            w_load.wait()

        dh = jnp.dot(dp_ref[...], w_ref[...], preferred_element_type=F32)
        xv = x_ref[...]
        r = lax.rsqrt(jnp.mean(xv * xv, axis=-1, keepdims=True) + RMS_EPS)
        xn = xv * r
        u = dh * g_ref[...]
        gx_ref[...] = dx2_ref[...] + r * (u - xn * jnp.mean(u * xn, axis=-1, keepdims=True))
        gni_scr[...] += jnp.sum(dh * xn, axis=0, keepdims=True)

        @pl.when(i == relay_step)
        def _():
            rs_relay()

        @pl.when(i == steps - 1)
        def _():
            row = lax.broadcasted_iota(jnp.int32, (8, D_MODEL), 0)
            lane = lax.broadcasted_iota(jnp.int32, (8, D_MODEL), 1)
            slab = jnp.where(row == ROW_NORM_IN, gni_scr[...], jnp.where(row == ROW_NORM_FINAL, gnf_ref[...], gs_ref[...]))
            myslab[...] = jnp.where((row == ROW_SINKS) & (lane == LOSS_LANE), lp_ref[0:1, 0:1], slab)
            slab_start()
            gwin_ref[...] = rs_finish()
            gsum_ref[...] = slab_finish()

    const = lambda i: (0, 0)
    return pl.pallas_call(
        body, name="in_bwd", grid=(steps,),
        in_specs=[pl.BlockSpec((tm, D_PROJ), lambda i: (i, 0)), pl.BlockSpec(memory_space=pl.ANY),
                  pl.BlockSpec((tm, D_MODEL), lambda i: (i, 0)), pl.BlockSpec((tm, D_MODEL), lambda i: (i, 0)),
                  pl.BlockSpec((1, D_MODEL), const), pl.BlockSpec(memory_space=pl.ANY),
                  pl.BlockSpec((8, D_MODEL), const), pl.BlockSpec((1, D_MODEL), const), pl.BlockSpec((8, 128), const)],
        out_specs=(pl.BlockSpec((tm, D_MODEL), lambda i: (i, 0)), pl.BlockSpec((SHARD_IN, D_MODEL), const),
                   pl.BlockSpec((8, D_MODEL), const)),
        out_shape=(jax.ShapeDtypeStruct((SEQ, D_MODEL), F32), jax.ShapeDtypeStruct((SHARD_IN, D_MODEL), F32),
                   jax.ShapeDtypeStruct((8, D_MODEL), F32)),
        scratch_shapes=[pltpu.VMEM((1, D_MODEL), F32), pltpu.VMEM((SHARD_IN, D_MODEL), BF16),
                        pltpu.VMEM((2, SHARD_IN, D_MODEL), BF16), pltpu.VMEM((2, HALF_IN, D_MODEL), BF16),
                        pltpu.VMEM((2, HALF_IN, D_MODEL), BF16),
                        pltpu.VMEM((8, D_MODEL), F32), pltpu.VMEM((N_DEV, 8, D_MODEL), F32),
                        pltpu.VMEM((D_PROJ, D_MODEL), BF16),
                        pltpu.SemaphoreType.DMA((N_ICI_SUM_SEMS + 7,)), pltpu.SemaphoreType.DMA((N_ICI_SUM_SEMS + 7,)),
                        pltpu.SemaphoreType.DMA((4,))],
        compiler_params=_params(dimension_semantics=("arbitrary",)),
    )(dproj, w_full, x, dx2, norm_in, dw_in_chip, gslab, gnf, loss_part)


def _dw_rs(mixed, dx2b, dproj, h):
    tn_out, tn = 512, 640
    out_steps, in_steps = D_MIX // tn_out, D_PROJ // tn
    steps = out_steps + in_steps
    forward_step = out_steps + 2

    def body(mx_ref, dxb_ref, a_ref, b_ref, chip_ref, gwo_ref, dwo, dwt, d2d_in, own, d2d, ici,
             send_sems, recv_sems, local_sems):
        i = pl.program_id(0)
        rs_start, rs_forward, rs_finish = _shard_sum(dwo, own, d2d, ici, send_sems, recv_sems, local_sems)
        pair_send, pair_finish = _chip_sum(dwt, d2d_in, chip_ref, send_sems, recv_sems, local_sems,
                                           N_SHARD_SUM_SEMS, 4)

        @pl.when(i < out_steps)
        def _():
            tile = lax.dot_general(mx_ref[...], dxb_ref[...], _TN, preferred_element_type=F32).astype(BF16)
            for half in range(tn_out // SHARD_OUT):
                dwo[(tn_out // SHARD_OUT) * i + half] = tile[SHARD_OUT * half:SHARD_OUT * (half + 1), :]

        @pl.when(i == out_steps)
        def _():
            rs_start()

        @pl.when(i >= out_steps)
        def _():
            k = i - out_steps
            pair_send((k - 1) * tn, k * tn)
            tile = lax.dot_general(a_ref[...], b_ref[...], _TN, preferred_element_type=F32).astype(BF16)
            dwt[pl.ds(pl.multiple_of(k * tn, tn), tn), :] = tile

        @pl.when(i == forward_step)
        def _():
            rs_forward()

        @pl.when(i == steps - 1)
        def _():
            pair_send((in_steps - 1) * tn, D_PROJ)
            gwo_ref[...] = rs_finish()
            pair_finish()

    vmem = pl.BlockSpec(memory_space=pltpu.VMEM)
    return pl.pallas_call(
        body, name="dw", grid=(steps,),
        in_specs=[pl.BlockSpec((SEQ, tn_out), lambda i: (0, jnp.minimum(i, out_steps - 1))), vmem,
                  pl.BlockSpec((SEQ, tn), lambda i: (0, jnp.maximum(i - out_steps, 0))), vmem],
        out_specs=(pl.BlockSpec(memory_space=pl.ANY), pl.BlockSpec((SHARD_OUT, D_MODEL), lambda i: (0, 0))),
        out_shape=(jax.ShapeDtypeStruct((4, SHARD_IN, D_MODEL), BF16), jax.ShapeDtypeStruct((SHARD_OUT, D_MODEL), F32)),
        scratch_shapes=[pltpu.VMEM((N_DEV, SHARD_OUT, D_MODEL), BF16),
                        pltpu.VMEM((D_PROJ, D_MODEL), BF16), pltpu.VMEM((4, SHARD_IN, D_MODEL), BF16),
                        *_shard_sum_scratch(SHARD_OUT),
                        pltpu.SemaphoreType.DMA((N_SHARD_SUM_SEMS + 4,)), pltpu.SemaphoreType.DMA((N_SHARD_SUM_SEMS + 4,)),
                        pltpu.SemaphoreType.DMA((8,))],
        compiler_params=_params(dimension_semantics=("arbitrary",)),
    )(mixed, dx2b, dproj, h)


def _adam_all(big_in, big_out, gsum, small, grad_x):
    steps = 4
    tr_in, tr_out = SHARD_IN // steps, SHARD_OUT // steps

    def body(*refs):
        ins, outs = refs[:8 + 1 + 18 + 1], refs[8 + 1 + 18 + 1:]
        i = pl.program_id(0)
        outs[33][...] = ins[27][...]
        for b in range(2):
            w_ref, g_ref, m_ref, v_ref = ins[4 * b:4 * b + 4]
            g = g_ref[...]
            delta, mn, vn = _adamw(w_ref[...], g, m_ref[...], v_ref[...])
            for ref, val in zip(outs[4 * b:4 * b + 4], (g, delta, mn, vn)):
                ref[...] = val

        @pl.when(i == 0)
        def _():
            gsum = ins[8][...]
            idx = _slot(lax.axis_index("x"), lax.axis_index("y"), lax.axis_index("c"))
            cg = jnp.zeros((3, SHARD_CONV), F32)
            for d in range(N_DEV):
                cg = jnp.where(idx == d, gsum[ROW_CONV0:ROW_CONV0 + 3, d * SHARD_CONV:(d + 1) * SHARD_CONV], cg)
            grads = (gsum[ROW_NORM_IN:ROW_NORM_IN + 1], gsum[ROW_SINKS:ROW_SINKS + 1, 0:N_Q_HEADS],
                     gsum[ROW_NORM_CONV:ROW_NORM_CONV + 1], gsum[ROW_NORM_ATTN:ROW_NORM_ATTN + 1],
                     gsum[ROW_NORM_FINAL:ROW_NORM_FINAL + 1], cg)
            for s, g in enumerate(grads):
                at = (slice(None), 0, slice(None)) if s == 5 else (slice(None), slice(None))
                w_ref, m_ref, v_ref = ins[9 + 3 * s:12 + 3 * s]
                delta, mn, vn = _adamw(w_ref[at], g, m_ref[at], v_ref[at])
                for ref, val in zip(outs[8 + 4 * s:12 + 4 * s], (g, delta, mn, vn)):
                    ref[at] = val
            outs[32][...] = gsum[ROW_SINKS:ROW_SINKS + 1, LOSS_LANE:LOSS_LANE + 1]

    const = lambda i: (0, 0)
    rows = lambda i: (i, 0)
    whole = lambda shape: pl.BlockSpec(shape, lambda i: (0,) * len(shape))
    small_shapes = [a.shape for a in small[::3]]
    in_specs = ([pl.BlockSpec((tr_in, D_MODEL), rows)] * 4 + [pl.BlockSpec((tr_out, D_MODEL), rows)] * 4
                + [pl.BlockSpec((8, D_MODEL), const)] + [whole(a.shape) for a in small]
                + [pl.BlockSpec((SEQ // steps, D_MODEL), rows)])
    out_specs = ([pl.BlockSpec((tr_in, D_MODEL), rows)] * 4 + [pl.BlockSpec((tr_out, D_MODEL), rows)] * 4
                 + [whole(s) for s in small_shapes for _ in range(4)] + [pl.BlockSpec((1, 1), const)]
                 + [pl.BlockSpec((SEQ // steps, D_MODEL), rows)])
    out_shape = ([jax.ShapeDtypeStruct((SHARD_IN, D_MODEL), F32)] * 4 + [jax.ShapeDtypeStruct((SHARD_OUT, D_MODEL), F32)] * 4
                 + [jax.ShapeDtypeStruct(s, F32) for s in small_shapes for _ in range(4)]
                 + [jax.ShapeDtypeStruct((1, 1), F32), jax.ShapeDtypeStruct((SEQ, D_MODEL), F32)])
    outs = pl.pallas_call(
        body, name="adam", grid=(steps,), in_specs=in_specs, out_specs=tuple(out_specs), out_shape=tuple(out_shape),
        compiler_params=_params(dimension_semantics=("arbitrary",)),
    )(*big_in, *big_out, gsum, *small, grad_x)
    return outs[0:4], outs[4:8], [outs[8 + 4 * s:12 + 4 * s] for s in range(6)], outs[32], outs[33]


def _rows_first(a):
    return jnp.transpose(a, (1, 0, 2))


def kernel(x, norm_in, w_in, conv_w, attn_sinks, norm_conv_out, norm_attn_out, w_out, norm_final, loss_target, m_norm_in, m_w_in, m_conv_w, m_attn_sinks, m_norm_conv_out, m_norm_attn_out, m_w_out, m_norm_final, v_norm_in, v_w_in, v_conv_w, v_attn_sinks, v_norm_conv_out, v_norm_attn_out, v_w_out, v_norm_final):
    x2d = x.reshape(SEQ, D_MODEL)
    target = loss_target.reshape(SEQ, D_MODEL)
    nf = norm_final.reshape(1, D_MODEL)

    w_in_t, m_w_in_t, v_w_in_t = w_in[0].T, m_w_in[0].T, v_w_in[0].T
    tiles = jnp.asarray(TILE_ORDER, jnp.int32).reshape(-1)
    w_in_full, h, proj, g_out, conv_full = _gather_in_proj(x2d, norm_in, w_in_t, w_out[0], _rows_first(conv_w), tiles)
    sinks = attn_sinks.reshape(N_Q_HEADS)

    mixed, attn, probs, shares = _mix_fwd(proj, conv_full, sinks, norm_conv_out, norm_attn_out)
    dx2, dx2b, dmixed, gnf, loss_part = _out_proj_loss(mixed, x2d, target, g_out.reshape(D_MIX, D_MODEL), nf)
    dproj, gslab = _mix_bwd(proj, dmixed, attn, probs, shares, conv_full, norm_conv_out, norm_attn_out)
    dw_in_chip, g_w_out = _dw_rs(mixed, dx2b, dproj, h)
    grad_x, g_w_in, gsum = _in_bwd_rs(dproj, w_in_full, x2d, dx2, norm_in, dw_in_chip, gslab, gnf, loss_part)

    small = (norm_in, m_norm_in, v_norm_in, attn_sinks, m_attn_sinks, v_attn_sinks,
             norm_conv_out, m_norm_conv_out, v_norm_conv_out, norm_attn_out, m_norm_attn_out, v_norm_attn_out,
             nf, m_norm_final.reshape(1, D_MODEL), v_norm_final.reshape(1, D_MODEL),
             _rows_first(conv_w), _rows_first(m_conv_w), _rows_first(v_conv_w))
    big_in, big_out, (s_ni, s_sk, s_nc, s_na, s_nf, s_cv), loss, grad_x = _adam_all(
        (w_in_t, g_w_in, m_w_in_t, v_w_in_t), (w_out[0], g_w_out, m_w_out[0], v_w_out[0]), gsum, small, grad_x)

    def leaves(k):
        return (s_ni[k], big_in[k].T[None], jnp.transpose(s_cv[k], (1, 0, 2)), s_sk[k], s_nc[k], s_na[k], big_out[k][None],
                s_nf[k].reshape(D_MODEL))

    return (loss.reshape(()), grad_x.reshape(1, SEQ, D_MODEL), *leaves(0), *leaves(1), *leaves(2), *leaves(3))
```

```python
import jax
import jax.numpy as jnp
from jax import lax
from jax.experimental import pallas as pl
from jax.experimental.pallas import tpu as pltpu

F32 = jnp.float32
BF16 = jnp.bfloat16
MESH = pl.DeviceIdType.MESH

N_DEV = 8
SEQ = 2048
D_MODEL = 1024
D_CONV = 1024
D_ATTN = 1024
D_KV = 128
HEAD_DIM = 64
N_Q_HEADS = 16
N_PAIRS = N_Q_HEADS // 2
PAIRS_PER_KV = N_PAIRS // 2
D_MIX = D_CONV + D_ATTN
D_PROJ = 6400
SHARD_IN = D_PROJ // N_DEV
SHARD_OUT = D_MIX // N_DEV
SHARD_CONV = D_CONV // N_DEV
OFF_CB, OFF_CC, OFF_CU, OFF_GC, OFF_Q, OFF_K, OFF_V, OFF_GA = 0, 1024, 2048, 3072, 4096, 5120, 5248, 5376
BLOCK = 128
N_BLOCKS = SEQ // BLOCK
HALO = 8
CHUNK = 16
N_CHUNKS = BLOCK // CHUNK
RMS_EPS = 1e-5
NEG = -1e30
SCALE = HEAD_DIM ** -0.5
SLOPES = tuple(2.0 ** (-8.0 * (h + 1) / N_Q_HEADS) for h in range(N_Q_HEADS))

ADAM_LR = 0.001
ADAM_B1 = 0.9
ADAM_B2 = 0.999
ADAM_EPS = 1e-08
ADAM_WD = 0.01
ADAM_STEP = 10

ROW_NORM_IN, ROW_NORM_CONV, ROW_NORM_ATTN, ROW_NORM_FINAL, ROW_CONV0, ROW_SINKS = 0, 1, 2, 3, 4, 7
LOSS_LANE = N_Q_HEADS
ACC_NORM_CONV, ACC_NORM_ATTN, ACC_CONV0, N_ACC = 0, 1, 2, 5

VMEM_LIMIT = 56 * 1024 * 1024

_NT = (((1,), (1,)), ((), ()))
_TN = (((0,), (0,)), ((), ()))


def _params(**kw):
    return pltpu.CompilerParams(vmem_limit_bytes=VMEM_LIMIT, **kw)


def _adamw(w, g, m, v):
    m = ADAM_B1 * m + (1.0 - ADAM_B1) * g
    v = ADAM_B2 * v + (1.0 - ADAM_B2) * (g * g)
    m_hat = m / (1.0 - ADAM_B1 ** ADAM_STEP)
    v_hat = v / (1.0 - ADAM_B2 ** ADAM_STEP)
    delta = -ADAM_LR * (m_hat / (jnp.sqrt(v_hat) + ADAM_EPS) + ADAM_WD * w)
    return delta, m, v


def _sigmoid(t):
    return 1.0 / (1.0 + jnp.exp(-t))


def _slot(px, py, pc):
    return 4 * px + 2 * py + pc


HALF_IN = SHARD_IN // 2
N_GATHER_KINDS = 13
W_OUT_KINDS = N_GATHER_KINDS + 7


IN_PROJ_TILE = 640
TILE_ORDER = ((0, 1, 2, 3, 4, 5, 6, 7, 8, 9), (3, 4, 0, 1, 2, 8, 9, 5, 6, 7),
              (5, 6, 0, 1, 7, 8, 9, 2, 3, 4), (8, 9, 3, 4, 5, 6, 7, 0, 1, 2))
TILES_OWN, TILES_NEIGHBOURS = 2, 7


def _tile(table_ref, p):
    chip = 2 * lax.axis_index("x") + lax.axis_index("y")
    return table_ref[chip * len(TILE_ORDER[0]) + p]


def _gather_in_proj(x, norm_in, w_in_sh, w_out_sh, conv_sh, tiles):
    tn = IN_PROJ_TILE
    steps = D_PROJ // tn
    tm = 256

    def body(tiles_ref, x_hbm, g_ref, win_ref, wout_ref, cv_ref, wt_ref, h_ref, proj_ref, gout_ref, conv_ref,
             gin_ref, gcv_ref, wob_ref, x_ref, send_sems, recv_sems, local_sems):
        p = pl.program_id(0)
        local_sem = local_sems.at[0]
        x, y, c = lax.axis_index("x"), lax.axis_index("y"), lax.axis_index("c")
        me, sibling = (x, y, c), (x, y, 1 - c)
        nx, ny, dg = (1 - x, y, c), (x, 1 - y, c), (1 - x, 1 - y, c)

        def other(dev):
            return (dev[0], dev[1], 1 - dev[2])

        def shard(dev):
            return gin_ref.at[pl.ds(pl.multiple_of(_slot(*dev) * SHARD_IN, 16), SHARD_IN), :]

        def half(dev, h):
            return gin_ref.at[pl.ds(pl.multiple_of(_slot(*dev) * SHARD_IN + h * HALF_IN, 16), HALF_IN), :]

        def rc(ref, k, to):
            return pltpu.make_async_remote_copy(src_ref=ref, dst_ref=ref, send_sem=send_sems.at[k],
                                                recv_sem=recv_sems.at[k], device_id=to, device_id_type=MESH)

        def cv(k, dev, to):
            s = _slot(*dev)
            return pltpu.make_async_remote_copy(src_ref=gcv_ref.at[s], dst_ref=gcv_ref.at[s],
                                                send_sem=send_sems.at[N_GATHER_KINDS + k],
                                                recv_sem=recv_sems.at[N_GATHER_KINDS + k], device_id=to, device_id_type=MESH)

        def own_copies():
            return [rc(shard(me), 0, sibling),
                    rc(half(me, 0), 1, nx), rc(half(me, 1), 2, nx),
                    rc(half(me, 1), 4, ny), rc(half(me, 0), 3, ny),
                    cv(0, me, sibling)] + [cv(1 + j, me, peer) for j, peer in enumerate((nx, ny, dg))]

        def pass_on(dev, h, k_in, k_ici, k_d2d, half=half, base=0):
            rc(half(dev, h), base + k_in, me).wait_recv()
            if k_ici is not None:
                rc(half(dev, h), base + k_ici, ny if dev is nx else nx).start()
            rc(half(dev, h), base + k_d2d, sibling).start()

        def out_half(dev, h):
            return gout_ref.at[_slot(*dev), pl.ds(h * (SHARD_OUT // 2), SHARD_OUT // 2), :]

        def own_out_copies():
            src = lambda h: wob_ref.at[pl.ds(h * (SHARD_OUT // 2), SHARD_OUT // 2), :]

            def send(ref, dst, k, to):
                return pltpu.make_async_remote_copy(src_ref=ref, dst_ref=dst, send_sem=send_sems.at[W_OUT_KINDS + k],
                                                    recv_sem=recv_sems.at[W_OUT_KINDS + k], device_id=to, device_id_type=MESH)

            return [send(wob_ref, gout_ref.at[_slot(*me)], 0, sibling),
                    send(src(0), out_half(me, 0), 1, nx), send(src(1), out_half(me, 1), 2, nx),
                    send(src(1), out_half(me, 1), 4, ny), send(src(0), out_half(me, 0), 3, ny)]

        def own_out_local():
            return pltpu.make_async_copy(wob_ref, gout_ref.at[_slot(*me)], local_sems.at[1])

        @pl.when(p == 0)
        def _():
            gin_ref[pl.ds(pl.multiple_of(_slot(*me) * SHARD_IN, 16), SHARD_IN), :] = win_ref[...].astype(BF16)
            gcv_ref[_slot(*me)] = jnp.zeros((8, SHARD_CONV), F32)
            gcv_ref[_slot(*me), 0:3, :] = cv_ref[:, 0, :]
            for cp in own_copies():
                cp.start()
            wob_ref[...] = wout_ref[...].astype(BF16)
            x_load = pltpu.make_async_copy(x_hbm, x_ref, local_sems.at[2])
            x_load.start()
            x_load.wait()
            for t in range(SEQ // tm):
                xv = x_ref[tm * t:tm * (t + 1), :]
                r = lax.rsqrt(jnp.mean(xv * xv, axis=-1, keepdims=True) + RMS_EPS)
                h_ref[tm * t:tm * (t + 1), :] = (xv * r * g_ref[...]).astype(BF16)
            rc(shard(sibling), 0, me).wait_recv()

        @pl.when(p == TILES_OWN)
        def _():
            for args in ((nx, 0, 1, 5, 7), (ny, 1, 4, 6, 10), (nx, 1, 2, None, 8), (ny, 0, 3, None, 9)):
                pass_on(*args)
            for j, peer in enumerate((nx, ny, dg)):
                cv(1 + j, peer, me).wait_recv()
                cv(4 + j, peer, sibling).start()
            for (dev, h), k in (((nx, 0), 7), ((nx, 1), 8), ((ny, 0), 9), ((ny, 1), 10)):
                rc(half(other(dev), h), k, me).wait_recv()
            own_out_local().start()
            for cp in own_out_copies():
                cp.start()

        @pl.when(p == TILES_NEIGHBOURS - 1)
        def _():
            pass_on(dg, 0, 5, None, 11)
            pass_on(dg, 1, 6, None, 12)

        @pl.when(p == TILES_NEIGHBOURS)
        def _():
            for (dev, h), k in (((dg, 0), 11), ((dg, 1), 12)):
                rc(half(other(dev), h), k, me).wait_recv()
            pltpu.make_async_copy(gin_ref, wt_ref, local_sem).start()

        @pl.when(p == steps - 2)
        def _():
            for args in ((nx, 0, 1, 5, 7), (ny, 1, 4, 6, 10), (nx, 1, 2, None, 8), (ny, 0, 3, None, 9)):
                pass_on(*args, half=out_half, base=W_OUT_KINDS)

        w = gin_ref[pl.ds(pl.multiple_of(_tile(tiles_ref, p) * tn, tn), tn), :]
        proj_ref[...] = lax.dot_general(h_ref[...], w, _NT, preferred_element_type=F32)

        @pl.when(p == steps - 1)
        def _():
            cv(0, sibling, me).wait_recv()
            for j, peer in enumerate((nx, ny, dg)):
                cv(4 + j, other(peer), me).wait_recv()
            for d in range(N_DEV):
                conv_ref[:, d * SHARD_CONV:(d + 1) * SHARD_CONV] = gcv_ref[d]
            relayed = [rc(half(nx, 0), 5, ny), rc(half(ny, 1), 6, nx)]
            relayed += [rc(half(dev, h), k, sibling) for (dev, h), k in
                        (((nx, 0), 7), ((nx, 1), 8), ((ny, 0), 9), ((ny, 1), 10), ((dg, 0), 11), ((dg, 1), 12))]
            relayed += [cv(4 + j, peer, sibling) for j, peer in enumerate((nx, ny, dg))]
            for cp in own_copies() + relayed:
                cp.wait_send()
            pltpu.make_async_copy(gin_ref, wt_ref, local_sem).wait()
            pass_on(dg, 0, 5, None, 11, half=out_half, base=W_OUT_KINDS)
            pass_on(dg, 1, 6, None, 12, half=out_half, base=W_OUT_KINDS)
            rc(gout_ref.at[_slot(*sibling)], W_OUT_KINDS, me).wait_recv()
            out_relayed = [rc(out_half(nx, 0), W_OUT_KINDS + 5, ny), rc(out_half(ny, 1), W_OUT_KINDS + 6, nx)]
            for (dev, h), k in (((nx, 0), 7), ((nx, 1), 8), ((ny, 0), 9), ((ny, 1), 10), ((dg, 0), 11), ((dg, 1), 12)):
                rc(out_half(other(dev), h), W_OUT_KINDS + k, me).wait_recv()
                out_relayed.append(rc(out_half(dev, h), W_OUT_KINDS + k, sibling))
            for cp in own_out_copies() + out_relayed:
                cp.wait_send()
            own_out_local().wait()

    vmem = pl.BlockSpec(memory_space=pltpu.VMEM)
    grid_spec = pltpu.PrefetchScalarGridSpec(
        num_scalar_prefetch=1, grid=(steps,),
        in_specs=[pl.BlockSpec(memory_space=pl.ANY), vmem, vmem, vmem, vmem],
        out_specs=(pl.BlockSpec(memory_space=pl.ANY), vmem,
                   pl.BlockSpec((SEQ, tn), lambda p, tiles_ref: (0, _tile(tiles_ref, p))),
                   pl.BlockSpec(memory_space=pl.ANY), vmem),
        scratch_shapes=[pltpu.VMEM((D_PROJ, D_MODEL), BF16), pltpu.VMEM((N_DEV, 8, SHARD_CONV), F32),
                        pltpu.VMEM((SHARD_OUT, D_MODEL), BF16), pltpu.VMEM((SEQ, D_MODEL), F32),
                        pltpu.SemaphoreType.DMA((W_OUT_KINDS + N_GATHER_KINDS,)),
                        pltpu.SemaphoreType.DMA((W_OUT_KINDS + N_GATHER_KINDS,)),
                        pltpu.SemaphoreType.DMA((3,))])
    return pl.pallas_call(
        body, name="gather_in_proj", grid_spec=grid_spec,
        out_shape=(jax.ShapeDtypeStruct((D_PROJ, D_MODEL), BF16), jax.ShapeDtypeStruct((SEQ, D_MODEL), BF16),
                   jax.ShapeDtypeStruct((SEQ, D_PROJ), F32), jax.ShapeDtypeStruct((N_DEV, SHARD_OUT, D_MODEL), BF16),
                   jax.ShapeDtypeStruct((8, D_CONV), F32)),
        compiler_params=_params(dimension_semantics=("arbitrary",)),
    )(tiles, x, norm_in, w_in_sh, w_out_sh, conv_sh)


def _shard_sum(src, own, d2d, ici, send_sems, recv_sems, local_sems, base=0):
    x, y, c = lax.axis_index("x"), lax.axis_index("y"), lax.axis_index("c")
    sibling = (x, y, 1 - c)
    chips = [(x, y), (1 - x, y), (x, 1 - y), (1 - x, 1 - y)]

    def rcopy(s, d, k, to):
        return pltpu.make_async_remote_copy(src_ref=s, dst_ref=d, send_sem=send_sems.at[base + k],
                                            recv_sem=recv_sems.at[base + k], device_id=to, device_id_type=MESH)

    def mine(k):
        return pltpu.make_async_copy(src.at[_slot(*chips[k], c)], own.at[k], local_sems.at[k])

    def to_sibling(k):
        return rcopy(src.at[_slot(*chips[k], 1 - c)], d2d.at[k], k, sibling)

    def to_chip(k):
        return rcopy(own.at[k], ici.at[k - 1], 3 + k, (*chips[k], c))

    def start():
        for k in range(4):
            mine(k).start()
            to_sibling(k).start()

    def forward():
        for k in range(1, 4):
            mine(k).wait()
            to_sibling(k).wait_recv()
            own[k] = (own[k].astype(F32) + d2d[k].astype(F32)).astype(BF16)
            to_chip(k).start()

    def finish():
        mine(0).wait()
        to_sibling(0).wait_recv()
        acc = own[0].astype(F32) + d2d[0].astype(F32)
        for k in range(1, 4):
            to_chip(k).wait_recv()
            acc = acc + ici[k - 1].astype(F32)
        for k in range(4):
            to_sibling(k).wait_send()
        for k in range(1, 4):
            to_chip(k).wait_send()
        return acc

    return start, forward, finish


def _shard_sum_scratch(rows):
    return [pltpu.VMEM((4, rows, D_MODEL), BF16), pltpu.VMEM((4, rows, D_MODEL), BF16),
            pltpu.VMEM((3, rows, D_MODEL), BF16)]


N_SHARD_SUM_SEMS = 7


def _chip_sum(dwt, d2d, out_hbm, send_sems, recv_sems, local_sems, base, local_base):
    x, y, c = lax.axis_index("x"), lax.axis_index("y"), lax.axis_index("c")
    sibling = (x, y, 1 - c)
    chips = [(x, y), (1 - x, y), (x, 1 - y), (1 - x, 1 - y)]

    def shard(s):
        return dwt.at[pl.ds(pl.multiple_of(s * SHARD_IN, 16), SHARD_IN), :]

    def to_sibling(k):
        return pltpu.make_async_remote_copy(src_ref=shard(_slot(*chips[k], 1 - c)), dst_ref=d2d.at[k],
                                            send_sem=send_sems.at[base + k], recv_sem=recv_sems.at[base + k],
                                            device_id=sibling, device_id_type=MESH)

    def save(k):
        return pltpu.make_async_copy(d2d.at[k], out_hbm.at[k], local_sems.at[local_base + k])

    def send(rows_before, rows_done):
        for k in range(4):
            end = (_slot(*chips[k], 1 - c) + 1) * SHARD_IN

            @pl.when((end > rows_before) & (end <= rows_done))
            def _():
                to_sibling(k).start()

    def add(rows_before, rows_done):
        for k in range(4):
            end = (_slot(*chips[k], c) + 1) * SHARD_IN

            @pl.when((end > rows_before) & (end <= rows_done))
            def _():
                to_sibling(k).wait_recv()
                d2d[k] = (shard(_slot(*chips[k], c))[...].astype(F32) + d2d[k].astype(F32)).astype(BF16)
                save(k).start()

    def finish():
        for k in range(4):
            save(k).wait()
            to_sibling(k).wait_send()

    return send, add, finish


N_ICI_SUM_SEMS = 6


def _ici_sum(src, own, ici, via, stage, send_sems, recv_sems, local_sems, base=0):
    x, y, c = lax.axis_index("x"), lax.axis_index("y"), lax.axis_index("c")
    nx, ny = (1 - x, y, c), (x, 1 - y, c)
    OWN, NX, NY, DG = range(4)

    def half(ref, h):
        return ref.at[pl.ds(h * HALF_IN, HALF_IN), :]

    def rc(s, d, k, to):
        return pltpu.make_async_remote_copy(src_ref=s, dst_ref=d, send_sem=send_sems.at[base + k],
                                            recv_sem=recv_sems.at[base + k], device_id=to, device_id_type=MESH)

    for_dg_0 = lambda: rc(half(src.at[DG], 0), via.at[0], 0, nx)
    for_dg_1 = lambda: rc(half(src.at[DG], 1), via.at[1], 1, ny)
    for_nx_0 = lambda: rc(half(src.at[NX], 0), half(ici.at[0], 0), 2, nx)
    for_ny_1 = lambda: rc(half(src.at[NY], 1), half(ici.at[1], 1), 3, ny)
    for_ny_0 = lambda: rc(stage.at[0], half(ici.at[1], 0), 4, ny)
    for_nx_1 = lambda: rc(stage.at[1], half(ici.at[0], 1), 5, nx)
    mine = lambda: pltpu.make_async_copy(src.at[OWN], own, local_sems.at[0])
    stage_0 = lambda: pltpu.make_async_copy(half(src.at[NY], 0), stage.at[0], local_sems.at[1])
    stage_1 = lambda: pltpu.make_async_copy(half(src.at[NX], 1), stage.at[1], local_sems.at[2])

    def start():
        for cp in (for_dg_0, for_dg_1, for_nx_0, for_ny_1, stage_0, stage_1, mine):
            cp().start()

    def relay():
        for h, staged, landed, out in ((0, stage_0, for_dg_0, for_ny_0), (1, stage_1, for_dg_1, for_nx_1)):
            staged().wait()
            landed().wait_recv()
            stage[h] = (stage[h].astype(F32) + via[h].astype(F32)).astype(BF16)
            out().start()

    def finish():
        mine().wait()
        for cp in (for_nx_0, for_nx_1, for_ny_1, for_ny_0):
            cp().wait_recv()
        acc = own[...].astype(F32) + ici[0].astype(F32) + ici[1].astype(F32)
        for cp in (for_dg_0, for_dg_1, for_nx_0, for_ny_1, for_ny_0, for_nx_1):
            cp().wait_send()
        return acc

    return start, relay, finish


def _slab_sum(myslab, slabs, send_sems, recv_sems, base):
    x, y, c = lax.axis_index("x"), lax.axis_index("y"), lax.axis_index("c")
    me = _slot(x, y, c)
    peers = [(x, y, 1 - c), (1 - x, y, c), (x, 1 - y, c), (1 - x, 1 - y, c),
             (1 - x, y, 1 - c), (x, 1 - y, 1 - c), (1 - x, 1 - y, 1 - c)]

    def cp(k):
        return pltpu.make_async_remote_copy(src_ref=myslab, dst_ref=slabs.at[me], send_sem=send_sems.at[base + k],
                                            recv_sem=recv_sems.at[base + k], device_id=peers[k], device_id_type=MESH)

    def start():
        slabs[me] = myslab[...]
        for k in range(7):
            cp(k).start()

    def finish():
        for k in range(7):
            cp(k).wait_recv()
        total = slabs[0]
        for d in range(1, N_DEV):
            total = total + slabs[d]
        for k in range(7):
            cp(k).wait_send()
        return total

    return start, finish


def _chunk_rows(r):
    return slice(r * CHUNK, (r + 1) * CHUNK)


def _conv_halo(cch_ref, cuh_ref, n):
    zh = jnp.where(n > 0, cch_ref[...] * cuh_ref[...], 0.0)
    return jnp.concatenate([zh] * (CHUNK // HALO), axis=0)


def _conv_chunk(pj_ref, zhalo, cw, r):
    rows = _chunk_rows(r)
    cc = pj_ref[rows, OFF_CC:OFF_CC + D_CONV]
    cu = pj_ref[rows, OFF_CU:OFF_CU + D_CONV]
    z = cc * cu
    before = _chunk_rows(r - 1)
    zprev = pj_ref[before, OFF_CC:OFF_CC + D_CONV] * pj_ref[before, OFF_CU:OFF_CU + D_CONV] if r > 0 else zhalo
    row = lax.broadcasted_iota(jnp.int32, (CHUNK, D_CONV), 0)
    z1 = jnp.where(row < 1, pltpu.roll(zprev, 1, 0), pltpu.roll(z, 1, 0))
    z2 = jnp.where(row < 2, pltpu.roll(zprev, 2, 0), pltpu.roll(z, 2, 0))
    co = cw[0] * z2 + cw[1] * z1 + cw[2] * z
    return cc, cu, z, z1, z2, co


def _gated_norm(a, gain, t):
    r = lax.rsqrt(jnp.mean(a * a, axis=-1, keepdims=True) + RMS_EPS)
    return a * r * gain * (t * _sigmoid(t))


def _kv_bands(pj, kvp_ref):
    lane = lax.broadcasted_iota(jnp.int32, (2 * BLOCK, D_KV), 1)
    lo = lane < HEAD_DIM

    def bands(prev, cur):
        b = jnp.concatenate([prev, cur], axis=0)
        br = pltpu.roll(b, HEAD_DIM, 1)
        zero = jnp.zeros_like(b)
        return ((jnp.where(lo, b, zero).astype(BF16), jnp.where(lo, zero, br).astype(BF16)),
                (jnp.where(lo, br, zero).astype(BF16), jnp.where(lo, zero, b).astype(BF16)))

    ks = bands(kvp_ref[:, 0:D_KV], pj[:, OFF_K:OFF_K + D_KV])
    vs = bands(kvp_ref[:, D_KV:2 * D_KV], pj[:, OFF_V:OFF_V + D_KV])
    return ks, vs


STACK = PAIRS_PER_KV * BLOCK


def _head(j, i, e):
    return 2 * (PAIRS_PER_KV * j + i) + e


def _pair_cols(j, i, off):
    p = PAIRS_PER_KV * j + i
    return slice(off + 128 * p, off + 128 * (p + 1))


def _fill_attn_bias(bias_scr, first_block):
    qi = lax.broadcasted_iota(jnp.int32, (BLOCK, 2 * BLOCK), 0)
    kj = lax.broadcasted_iota(jnp.int32, (BLOCK, 2 * BLOCK), 1)
    dist = BLOCK + qi - kj
    valid = (dist >= 0) & (dist < BLOCK)
    if first_block:
        valid = valid & (kj >= BLOCK)
    distf = dist.astype(F32)
    for j in range(2):
        for e in range(2):
            for i in range(PAIRS_PER_KV):
                bias_scr[2 * j + e, BLOCK * i:BLOCK * (i + 1), :] = jnp.where(valid, -SLOPES[_head(j, i, e)] * distf, NEG)


def _q_stack(pj, j):
    return jnp.concatenate([(pj[:, _pair_cols(j, i, OFF_Q)] * SCALE).astype(BF16) for i in range(PAIRS_PER_KV)], axis=0)


def _attn_probs(q_stack, kband, bias_ref, sinks):
    s = lax.dot_general(q_stack, kband, _NT, preferred_element_type=F32)
    ones = jnp.ones((128, 128), BF16)
    probs, shares = [], []
    for i, sink in enumerate(sinks):
        rows = slice(BLOCK * i, BLOCK * (i + 1))
        t = s[rows, :] + bias_ref[rows, :]
        m = jnp.broadcast_to(jnp.max(t, axis=-1, keepdims=True), (BLOCK, 128))
        m = jnp.maximum(m, sink)
        p = [jnp.exp(t[:, :128] - m), jnp.exp(t[:, 128:] - m)]
        es = jnp.exp(sink - m)
        total = (jnp.dot(p[0].astype(BF16), ones, preferred_element_type=F32)
                 + jnp.dot(p[1].astype(BF16), ones, preferred_element_type=F32))
        inv = 1.0 / (total + es)
        probs.append(jnp.concatenate([p[0] * inv, p[1] * inv], axis=1))
        shares.append(es * inv)
    return jnp.concatenate(probs, axis=0), jnp.concatenate(shares, axis=0)


def _attn_group(pj, ks, vs, bias_scr, sink_ref, j):
    q_stack = _q_stack(pj, j)
    out, probs, shares = None, [], []
    for e in range(2):
        p, ps = _attn_probs(q_stack, ks[j][e], bias_scr.at[2 * j + e],
                            [sink_ref[_head(j, i, e)] for i in range(PAIRS_PER_KV)])
        p = p.astype(BF16)
        o = jnp.dot(p, vs[j][e], preferred_element_type=F32)
        out = o if out is None else out + o
        probs.append(p)
        shares.append(ps)
    return out, probs, shares


def _mix_fwd(proj, conv_full, sinks, norm_conv, norm_attn):
    def body(pj_ref, kvp_ref, cch_ref, cuh_ref, cw_ref, sink_ref, gc_ref, ga_ref,
             mixed_ref, attn_scr, p_ref, ps_ref, bias_scr):
        n = pl.program_id(0)
        pj = pj_ref

        @pl.when(n == 0)
        def _():
            _fill_attn_bias(bias_scr, first_block=True)

        @pl.when(n == 1)
        def _():
            _fill_attn_bias(bias_scr, first_block=False)

        zhalo = _conv_halo(cch_ref, cuh_ref, n)
        cw = (cw_ref[0:1, :], cw_ref[1:2, :], cw_ref[2:3, :])
        gain_c = gc_ref[...]

        for r in range(N_CHUNKS):
            rows = _chunk_rows(r)
            co = _conv_chunk(pj_ref, zhalo, cw, r)[-1]
            y = _gated_norm(pj_ref[rows, OFF_CB:OFF_CB + D_CONV] * co, gain_c, pj_ref[rows, OFF_GC:OFF_GC + D_CONV])
            mixed_ref[rows, 0:D_CONV] = y.astype(BF16)

        ks, vs = _kv_bands(pj, kvp_ref)
        for j in range(2):
            out, probs, shares = _attn_group(pj, ks, vs, bias_scr, sink_ref, j)
            for e in range(2):
                p_ref[0, 2 * j + e] = probs[e]
                ps_ref[0, 2 * j + e] = shares[e]
            for i in range(PAIRS_PER_KV):
                attn_scr[:, _pair_cols(j, i, 0)] = out[BLOCK * i:BLOCK * (i + 1), :]
        gain_a = ga_ref[...]

        for r in range(N_CHUNKS):
            rows = _chunk_rows(r)
            y = _gated_norm(attn_scr[rows, :], gain_a, pj_ref[rows, OFF_GA:OFF_GA + D_ATTN])
            mixed_ref[rows, D_CONV:D_MIX] = y.astype(BF16)

    per_block = BLOCK // HALO
    return pl.pallas_call(
        body, name="mix_fwd", grid=(N_BLOCKS,),
        in_specs=[
            pl.BlockSpec((BLOCK, D_PROJ), lambda n: (n, 0)),
            pl.BlockSpec((BLOCK, 2 * D_KV), lambda n: (jnp.maximum(n - 1, 0), OFF_K // (2 * D_KV))),
            pl.BlockSpec((HALO, D_CONV), lambda n: (jnp.maximum(n * per_block - 1, 0), OFF_CC // D_CONV)),
            pl.BlockSpec((HALO, D_CONV), lambda n: (jnp.maximum(n * per_block - 1, 0), OFF_CU // D_CONV)),
            pl.BlockSpec((8, D_CONV), lambda n: (0, 0)),
            pl.BlockSpec(memory_space=pltpu.SMEM),
            pl.BlockSpec((1, D_CONV), lambda n: (0, 0)),
            pl.BlockSpec((1, D_ATTN), lambda n: (0, 0)),
        ],
        out_specs=(pl.BlockSpec((BLOCK, D_MIX), lambda n: (n, 0)), pl.BlockSpec((BLOCK, D_ATTN), lambda n: (n, 0)),
                   pl.BlockSpec((1, 4, STACK, 2 * BLOCK), lambda n: (n, 0, 0, 0)),
                   pl.BlockSpec((1, 4, STACK, 128), lambda n: (n, 0, 0, 0))),
        out_shape=(jax.ShapeDtypeStruct((SEQ, D_MIX), BF16), jax.ShapeDtypeStruct((SEQ, D_ATTN), F32),
                   jax.ShapeDtypeStruct((N_BLOCKS, 4, STACK, 2 * BLOCK), BF16),
                   jax.ShapeDtypeStruct((N_BLOCKS, 4, STACK, 128), F32)),
        scratch_shapes=[pltpu.VMEM((4, STACK, 2 * BLOCK), F32)],
        compiler_params=_params(dimension_semantics=("arbitrary",)),
    )(proj, proj, proj, proj, conv_full, sinks, norm_conv, norm_attn)


def _out_proj_loss(mixed, x, target, w_out_full, norm_final):
    tm = 256

    def body(mx_ref, x_ref, t_ref, w_ref, g_ref, dx2_ref, dx2b_ref, dmix_ref, gnf_ref, loss_ref):
        i = pl.program_id(0)
        w = w_ref[...]
        x2 = x_ref[...] + jnp.dot(mx_ref[...], w, preferred_element_type=F32)
        r = lax.rsqrt(jnp.mean(x2 * x2, axis=-1, keepdims=True) + RMS_EPS)
        xn = x2 * r
        g = g_ref[...]
        err = xn * g - t_ref[...]
        part = 0.5 * jnp.sum(jnp.mean(err * err, axis=-1, keepdims=True), axis=0, keepdims=True)
        dy = err * (1.0 / D_MODEL)
        gnf = jnp.sum(dy * xn, axis=0, keepdims=True)
        u = dy * g
        dx2 = r * (u - xn * jnp.mean(u * xn, axis=-1, keepdims=True))
        dx2_ref[...] = dx2
        dx2b = dx2.astype(BF16)
        dx2b_ref[...] = dx2b
        dmix_ref[...] = lax.dot_general(dx2b, w, _NT, preferred_element_type=F32)

        @pl.when(i == 0)
        def _():
            gnf_ref[...] = jnp.zeros_like(gnf_ref)
            loss_ref[...] = jnp.zeros_like(loss_ref)

        gnf_ref[...] += gnf
        loss_ref[...] += jnp.broadcast_to(part, loss_ref.shape)

    return pl.pallas_call(
        body, name="out_proj_loss", grid=(SEQ // tm,),
        in_specs=[pl.BlockSpec((tm, D_MIX), lambda i: (i, 0)), pl.BlockSpec((tm, D_MODEL), lambda i: (i, 0)),
                  pl.BlockSpec((tm, D_MODEL), lambda i: (i, 0)), pl.BlockSpec(memory_space=pltpu.VMEM),
                  pl.BlockSpec((1, D_MODEL), lambda i: (0, 0))],
        out_specs=(pl.BlockSpec((tm, D_MODEL), lambda i: (i, 0)), pl.BlockSpec((tm, D_MODEL), lambda i: (i, 0)),
                   pl.BlockSpec((tm, D_MIX), lambda i: (i, 0)),
                   pl.BlockSpec((1, D_MODEL), lambda i: (0, 0)), pl.BlockSpec((8, 128), lambda i: (0, 0))),
        out_shape=(jax.ShapeDtypeStruct((SEQ, D_MODEL), F32), jax.ShapeDtypeStruct((SEQ, D_MODEL), BF16),
                   jax.ShapeDtypeStruct((SEQ, D_MIX), F32),
                   jax.ShapeDtypeStruct((1, D_MODEL), F32), jax.ShapeDtypeStruct((8, 128), F32)),
        compiler_params=_params(dimension_semantics=("arbitrary",)),
    )(mixed, x, target, w_out_full, norm_final)


def _gated_norm_bwd(a, gain, t, dy):
    r = lax.rsqrt(jnp.mean(a * a, axis=-1, keepdims=True) + RMS_EPS)
    an = a * r
    sg = _sigmoid(t)
    dn = dy * (t * sg)
    dt = dy * (an * gain) * (sg * (1.0 + t * (1.0 - sg)))
    u = dn * gain
    da = r * (u - an * jnp.mean(u * an, axis=-1, keepdims=True))
    return da, dt, dn * an


def _mix_bwd(proj, dmixed, attn, probs, shares, conv_full, norm_conv, norm_attn):
    def body(pj_ref, kvp_ref, cch_ref, cuh_ref, dmx_ref, attn_ref, p_ref, ps_ref, cw_ref, gc_ref, ga_ref,
             dpj_ref, gslab_ref, dattn_scr, nxt_scr, dkv_scr, acc_scr):
        step = pl.program_id(0)
        n = N_BLOCKS - 1 - step
        pj = pj_ref

        @pl.when(step == 0)
        def _():
            gslab_ref[...] = jnp.zeros_like(gslab_ref)
            nxt_scr[...] = jnp.zeros_like(nxt_scr)
            dkv_scr[...] = jnp.zeros_like(dkv_scr)
            acc_scr[...] = jnp.zeros_like(acc_scr)

        zhalo = _conv_halo(cch_ref, cuh_ref, n)
        cw = (cw_ref[0:1, :], cw_ref[1:2, :], cw_ref[2:3, :])
        gain_c = gc_ref[...]
        row = lax.broadcasted_iota(jnp.int32, (CHUNK, D_CONV), 0)

        dco_after = nxt_scr[...]
        for r in reversed(range(N_CHUNKS)):
            rows = _chunk_rows(r)
            cc, cu, z, z1, z2, co = _conv_chunk(pj_ref, zhalo, cw, r)
            cb = pj_ref[rows, OFF_CB:OFF_CB + D_CONV]
            da, dgate, gterm = _gated_norm_bwd(cb * co, gain_c, pj_ref[rows, OFF_GC:OFF_GC + D_CONV],
                                               dmx_ref[rows, 0:D_CONV])
            dpj_ref[rows, OFF_GC:OFF_GC + D_CONV] = dgate.astype(BF16)
            dpj_ref[rows, OFF_CB:OFF_CB + D_CONV] = (da * co).astype(BF16)
            dco = da * cb
            dco1 = jnp.where(row >= CHUNK - 1, pltpu.roll(dco_after, CHUNK - 1, 0), pltpu.roll(dco, CHUNK - 1, 0))
            dco2 = jnp.where(row >= CHUNK - 2, pltpu.roll(dco_after, CHUNK - 2, 0), pltpu.roll(dco, CHUNK - 2, 0))
            dz = cw[2] * dco + cw[1] * dco1 + cw[0] * dco2
            dpj_ref[rows, OFF_CC:OFF_CC + D_CONV] = (dz * cu).astype(BF16)
            dpj_ref[rows, OFF_CU:OFF_CU + D_CONV] = (dz * cc).astype(BF16)
            acc_scr[ACC_NORM_CONV] += gterm
            acc_scr[ACC_CONV0] += dco * z2
            acc_scr[ACC_CONV0 + 1] += dco * z1
            acc_scr[ACC_CONV0 + 2] += dco * z
            dco_after = dco
        nxt_scr[...] = dco_after

        ks, vs = _kv_bands(pj, kvp_ref)
        gain_a = ga_ref[...]

        for r in range(N_CHUNKS):
            rows = _chunk_rows(r)
            da, dgate, gterm = _gated_norm_bwd(attn_ref[rows, :], gain_a, pj_ref[rows, OFF_GA:OFF_GA + D_ATTN],
                                               dmx_ref[rows, D_CONV:D_MIX])
            dpj_ref[rows, OFF_GA:OFF_GA + D_ATTN] = dgate.astype(BF16)
            dattn_scr[rows, :] = da
            acc_scr[ACC_NORM_ATTN] += gterm

        in_lo = lax.broadcasted_iota(jnp.int32, (128, 128), 0) < HEAD_DIM
        half_ones = (jnp.where(in_lo, 1.0, 0.0).astype(BF16), jnp.where(in_lo, 0.0, 1.0).astype(BF16))
        lane_s = lax.broadcasted_iota(jnp.int32, (1, D_MODEL), 1)
        gsink = jnp.zeros((1, D_MODEL), F32)
        dk_t, dv_t = [], []
        for j in range(2):
            q_stack = _q_stack(pj, j)
            do_f = jnp.concatenate([dattn_scr[:, _pair_cols(j, i, 0)] for i in range(PAIRS_PER_KV)], axis=0)
            o_f = jnp.concatenate([attn_ref[:, _pair_cols(j, i, 0)] for i in range(PAIRS_PER_KV)], axis=0)
            prod = (do_f * o_f).astype(BF16)
            deltas = [jnp.dot(prod, half_ones[e], preferred_element_type=F32) for e in range(2)]
            do_b = do_f.astype(BF16)
            q_t, do_t = q_stack.T, do_b.T
            dq, dk_j, dv_j = None, None, None
            for e in range(2):
                p = p_ref[0, 2 * j + e]
                dp = lax.dot_general(do_b, vs[j][e], _NT, preferred_element_type=F32)
                ds = []
                for i in range(PAIRS_PER_KV):
                    rows = slice(BLOCK * i, BLOCK * (i + 1))
                    delta = deltas[e][rows, :]
                    ds.append((p[rows, :].astype(F32) * (dp[rows, :] - jnp.concatenate([delta, delta], axis=1))).astype(BF16))
                    gs_h = -jnp.sum(ps_ref[0, 2 * j + e, rows, 0:1] * delta[:, 0:1], axis=0, keepdims=True)
                    gsink = gsink + jnp.where(lane_s == _head(j, i, e), gs_h, 0.0)
                ds = jnp.concatenate(ds, axis=0)
                t = jnp.dot(ds, ks[j][e], preferred_element_type=F32)
                dq = t if dq is None else dq + t
                half = slice(HEAD_DIM * e, HEAD_DIM * (e + 1))
                a = jnp.dot(q_t[half, :], ds, preferred_element_type=F32)
                b = jnp.dot(do_t[half, :], p, preferred_element_type=F32)
                dk_j = a if dk_j is None else dk_j + a
                dv_j = b if dv_j is None else dv_j + b
            for i in range(PAIRS_PER_KV):
                dpj_ref[:, _pair_cols(j, i, OFF_Q)] = (dq[BLOCK * i:BLOCK * (i + 1), :] * SCALE).astype(BF16)
            dk_t.append(dk_j)
            dv_t.append(dv_j)
        dk = jnp.concatenate(dk_t, axis=0).T
        dv = jnp.concatenate(dv_t, axis=0).T
        dpj_ref[:, OFF_K:OFF_K + D_KV] = (dk[BLOCK:, :] + dkv_scr[:, 0:D_KV]).astype(BF16)
        dpj_ref[:, OFF_V:OFF_V + D_KV] = (dv[BLOCK:, :] + dkv_scr[:, D_KV:2 * D_KV]).astype(BF16)
        dkv_scr[:, 0:D_KV] = dk[:BLOCK, :]
        dkv_scr[:, D_KV:2 * D_KV] = dv[:BLOCK, :]
        gslab_ref[ROW_SINKS:ROW_SINKS + 1, :] += gsink

        @pl.when(step == N_BLOCKS - 1)
        def _():
            for k, slab_row in ((ACC_NORM_CONV, ROW_NORM_CONV), (ACC_NORM_ATTN, ROW_NORM_ATTN), (ACC_CONV0, ROW_CONV0),
                                (ACC_CONV0 + 1, ROW_CONV0 + 1), (ACC_CONV0 + 2, ROW_CONV0 + 2)):
                gslab_ref[slab_row:slab_row + 1, :] = jnp.sum(acc_scr[k], axis=0, keepdims=True)

    per_block = BLOCK // HALO
    last = N_BLOCKS - 1
    return pl.pallas_call(
        body, name="mix_bwd", grid=(N_BLOCKS,),
        in_specs=[
            pl.BlockSpec((BLOCK, D_PROJ), lambda s: (last - s, 0)),
            pl.BlockSpec((BLOCK, 2 * D_KV), lambda s: (jnp.maximum(last - s - 1, 0), OFF_K // (2 * D_KV))),
            pl.BlockSpec((HALO, D_CONV), lambda s: (jnp.maximum((last - s) * per_block - 1, 0), OFF_CC // D_CONV)),
            pl.BlockSpec((HALO, D_CONV), lambda s: (jnp.maximum((last - s) * per_block - 1, 0), OFF_CU // D_CONV)),
            pl.BlockSpec((BLOCK, D_MIX), lambda s: (last - s, 0)),
            pl.BlockSpec((BLOCK, D_ATTN), lambda s: (last - s, 0)),
            pl.BlockSpec((1, 4, STACK, 2 * BLOCK), lambda s: (last - s, 0, 0, 0)),
            pl.BlockSpec((1, 4, STACK, 128), lambda s: (last - s, 0, 0, 0)),
            pl.BlockSpec((8, D_CONV), lambda s: (0, 0)),
            pl.BlockSpec((1, D_CONV), lambda s: (0, 0)),
            pl.BlockSpec((1, D_ATTN), lambda s: (0, 0)),
        ],
        out_specs=(pl.BlockSpec((BLOCK, D_PROJ), lambda s: (last - s, 0)),
                   pl.BlockSpec((8, D_MODEL), lambda s: (0, 0))),
        out_shape=(jax.ShapeDtypeStruct((SEQ, D_PROJ), BF16), jax.ShapeDtypeStruct((8, D_MODEL), F32)),
        scratch_shapes=[pltpu.VMEM((BLOCK, D_ATTN), F32), pltpu.VMEM((CHUNK, D_CONV), F32),
                        pltpu.VMEM((BLOCK, 2 * D_KV), F32), pltpu.VMEM((N_ACC, CHUNK, D_MODEL), F32)],
        compiler_params=_params(dimension_semantics=("arbitrary",)),
    )(proj, proj, proj, proj, dmixed, attn, probs, shares, conv_full, norm_conv, norm_attn)


def _in_bwd_rs(dproj, w_full, x, dx2, norm_in, dw_in_chip, gslab, gnf, loss_part):
    tm = 256
    steps = SEQ // tm
    relay_step = 4

    def body(dp_ref, w_hbm, x_ref, dx2_ref, g_ref, dwi_ref, gs_ref, gnf_ref, lp_ref, gx_ref, gwin_ref, gsum_ref,
             gni_scr, own, ici, via, stage, myslab, slabs, w_ref, send_sems, recv_sems, local_sems):
        i = pl.program_id(0)
        rs_start, rs_relay, rs_finish = _ici_sum(dwi_ref, own, ici, via, stage, send_sems, recv_sems, local_sems)
        slab_start, slab_finish = _slab_sum(myslab, slabs, send_sems, recv_sems, N_ICI_SUM_SEMS)

        @pl.when(i == 0)
        def _():
            gni_scr[...] = jnp.zeros_like(gni_scr)
            rs_start()
            w_load = pltpu.make_async_copy(w_hbm, w_ref, local_sems.at[3])
            w_load.start()
            w_load.wait()

        dh = jnp.dot(dp_ref[...], w_ref[...], preferred_element_type=F32)
        xv = x_ref[...]
        r = lax.rsqrt(jnp.mean(xv * xv, axis=-1, keepdims=True) + RMS_EPS)
        xn = xv * r
        u = dh * g_ref[...]
        gx_ref[...] = dx2_ref[...] + r * (u - xn * jnp.mean(u * xn, axis=-1, keepdims=True))
        gni_scr[...] += jnp.sum(dh * xn, axis=0, keepdims=True)

        @pl.when(i == relay_step)
        def _():
            rs_relay()

        @pl.when(i == steps - 1)
        def _():
            row = lax.broadcasted_iota(jnp.int32, (8, D_MODEL), 0)
            lane = lax.broadcasted_iota(jnp.int32, (8, D_MODEL), 1)
            slab = jnp.where(row == ROW_NORM_IN, gni_scr[...], jnp.where(row == ROW_NORM_FINAL, gnf_ref[...], gs_ref[...]))
            myslab[...] = jnp.where((row == ROW_SINKS) & (lane == LOSS_LANE), lp_ref[0:1, 0:1], slab)
            slab_start()
            gwin_ref[...] = rs_finish()
            gsum_ref[...] = slab_finish()

    const = lambda i: (0, 0)
    return pl.pallas_call(
        body, name="in_bwd", grid=(steps,),
        in_specs=[pl.BlockSpec((tm, D_PROJ), lambda i: (i, 0)), pl.BlockSpec(memory_space=pl.ANY),
                  pl.BlockSpec((tm, D_MODEL), lambda i: (i, 0)), pl.BlockSpec((tm, D_MODEL), lambda i: (i, 0)),
                  pl.BlockSpec((1, D_MODEL), const), pl.BlockSpec(memory_space=pl.ANY),
                  pl.BlockSpec((8, D_MODEL), const), pl.BlockSpec((1, D_MODEL), const), pl.BlockSpec((8, 128), const)],
        out_specs=(pl.BlockSpec((tm, D_MODEL), lambda i: (i, 0)), pl.BlockSpec((SHARD_IN, D_MODEL), const),
                   pl.BlockSpec((8, D_MODEL), const)),
        out_shape=(jax.ShapeDtypeStruct((SEQ, D_MODEL), F32), jax.ShapeDtypeStruct((SHARD_IN, D_MODEL), F32),
                   jax.ShapeDtypeStruct((8, D_MODEL), F32)),
        scratch_shapes=[pltpu.VMEM((1, D_MODEL), F32), pltpu.VMEM((SHARD_IN, D_MODEL), BF16),
                        pltpu.VMEM((2, SHARD_IN, D_MODEL), BF16), pltpu.VMEM((2, HALF_IN, D_MODEL), BF16),
                        pltpu.VMEM((2, HALF_IN, D_MODEL), BF16),
                        pltpu.VMEM((8, D_MODEL), F32), pltpu.VMEM((N_DEV, 8, D_MODEL), F32),
                        pltpu.VMEM((D_PROJ, D_MODEL), BF16),
                        pltpu.SemaphoreType.DMA((N_ICI_SUM_SEMS + 7,)), pltpu.SemaphoreType.DMA((N_ICI_SUM_SEMS + 7,)),
                        pltpu.SemaphoreType.DMA((4,))],
        compiler_params=_params(dimension_semantics=("arbitrary",)),
    )(dproj, w_full, x, dx2, norm_in, dw_in_chip, gslab, gnf, loss_part)


def _dw_rs(mixed, dx2b, dproj, h):
    tn_out, tn = 512, 640
    out_steps, in_steps = D_MIX // tn_out, D_PROJ // tn
    steps = out_steps + in_steps
    forward_step = out_steps + 2
    early_add_step = out_steps + 7

    def body(mx_ref, dxb_ref, a_ref, b_ref, chip_ref, gwo_ref, dwo, dwt, d2d_in, own, d2d, ici,
             send_sems, recv_sems, local_sems):
        i = pl.program_id(0)
        rs_start, rs_forward, rs_finish = _shard_sum(dwo, own, d2d, ici, send_sems, recv_sems, local_sems)
        pair_send, pair_add, pair_finish = _chip_sum(dwt, d2d_in, chip_ref, send_sems, recv_sems, local_sems,
                                                     N_SHARD_SUM_SEMS, 4)

        @pl.when(i < out_steps)
        def _():
            tile = lax.dot_general(mx_ref[...], dxb_ref[...], _TN, preferred_element_type=F32).astype(BF16)
            for half in range(tn_out // SHARD_OUT):
                dwo[(tn_out // SHARD_OUT) * i + half] = tile[SHARD_OUT * half:SHARD_OUT * (half + 1), :]

        @pl.when(i == out_steps)
        def _():
            rs_start()

        @pl.when(i >= out_steps)
        def _():
            k = i - out_steps
            pair_send((k - 1) * tn, k * tn)
            tile = lax.dot_general(a_ref[...], b_ref[...], _TN, preferred_element_type=F32).astype(BF16)
            dwt[pl.ds(pl.multiple_of(k * tn, tn), tn), :] = tile

        @pl.when(i == forward_step)
        def _():
            rs_forward()

        @pl.when(i == early_add_step)
        def _():
            pair_add(0, D_PROJ // 2)

        @pl.when(i == steps - 1)
        def _():
            pair_send((in_steps - 1) * tn, D_PROJ)
            gwo_ref[...] = rs_finish()
            pair_add(D_PROJ // 2, D_PROJ)
            pair_finish()

    vmem = pl.BlockSpec(memory_space=pltpu.VMEM)
    return pl.pallas_call(
        body, name="dw", grid=(steps,),
        in_specs=[pl.BlockSpec((SEQ, tn_out), lambda i: (0, jnp.minimum(i, out_steps - 1))), vmem,
                  pl.BlockSpec((SEQ, tn), lambda i: (0, jnp.maximum(i - out_steps, 0))), vmem],
        out_specs=(pl.BlockSpec(memory_space=pl.ANY), pl.BlockSpec((SHARD_OUT, D_MODEL), lambda i: (0, 0))),
        out_shape=(jax.ShapeDtypeStruct((4, SHARD_IN, D_MODEL), BF16), jax.ShapeDtypeStruct((SHARD_OUT, D_MODEL), F32)),
        scratch_shapes=[pltpu.VMEM((N_DEV, SHARD_OUT, D_MODEL), BF16),
                        pltpu.VMEM((D_PROJ, D_MODEL), BF16), pltpu.VMEM((4, SHARD_IN, D_MODEL), BF16),
                        *_shard_sum_scratch(SHARD_OUT),
                        pltpu.SemaphoreType.DMA((N_SHARD_SUM_SEMS + 4,)), pltpu.SemaphoreType.DMA((N_SHARD_SUM_SEMS + 4,)),
                        pltpu.SemaphoreType.DMA((8,))],
        compiler_params=_params(dimension_semantics=("arbitrary",)),
    )(mixed, dx2b, dproj, h)


def _adam_all(big_in, big_out, gsum, small, grad_x):
    steps = 4
    tr_in, tr_out = SHARD_IN // steps, SHARD_OUT // steps

    def body(*refs):
        ins, outs = refs[:8 + 1 + 18 + 1], refs[8 + 1 + 18 + 1:]
        i = pl.program_id(0)
        outs[33][...] = ins[27][...]
        for b in range(2):
            w_ref, g_ref, m_ref, v_ref = ins[4 * b:4 * b + 4]
            g = g_ref[...]
            delta, mn, vn = _adamw(w_ref[...], g, m_ref[...], v_ref[...])
            for ref, val in zip(outs[4 * b:4 * b + 4], (g, delta, mn, vn)):
                ref[...] = val

        @pl.when(i == 0)
        def _():
            gsum = ins[8][...]
            idx = _slot(lax.axis_index("x"), lax.axis_index("y"), lax.axis_index("c"))
            cg = jnp.zeros((3, SHARD_CONV), F32)
            for d in range(N_DEV):
                cg = jnp.where(idx == d, gsum[ROW_CONV0:ROW_CONV0 + 3, d * SHARD_CONV:(d + 1) * SHARD_CONV], cg)
            grads = (gsum[ROW_NORM_IN:ROW_NORM_IN + 1], gsum[ROW_SINKS:ROW_SINKS + 1, 0:N_Q_HEADS],
                     gsum[ROW_NORM_CONV:ROW_NORM_CONV + 1], gsum[ROW_NORM_ATTN:ROW_NORM_ATTN + 1],
                     gsum[ROW_NORM_FINAL:ROW_NORM_FINAL + 1], cg)
            for s, g in enumerate(grads):
                at = (slice(None), 0, slice(None)) if s == 5 else (slice(None), slice(None))
                w_ref, m_ref, v_ref = ins[9 + 3 * s:12 + 3 * s]
                delta, mn, vn = _adamw(w_ref[at], g, m_ref[at], v_ref[at])
                for ref, val in zip(outs[8 + 4 * s:12 + 4 * s], (g, delta, mn, vn)):
                    ref[at] = val
            outs[32][...] = gsum[ROW_SINKS:ROW_SINKS + 1, LOSS_LANE:LOSS_LANE + 1]

    const = lambda i: (0, 0)
    rows = lambda i: (i, 0)
    whole = lambda shape: pl.BlockSpec(shape, lambda i: (0,) * len(shape))
    small_shapes = [a.shape for a in small[::3]]
    in_specs = ([pl.BlockSpec((tr_in, D_MODEL), rows)] * 4 + [pl.BlockSpec((tr_out, D_MODEL), rows)] * 4
                + [pl.BlockSpec((8, D_MODEL), const)] + [whole(a.shape) for a in small]
                + [pl.BlockSpec((SEQ // steps, D_MODEL), rows)])
    out_specs = ([pl.BlockSpec((tr_in, D_MODEL), rows)] * 4 + [pl.BlockSpec((tr_out, D_MODEL), rows)] * 4
                 + [whole(s) for s in small_shapes for _ in range(4)] + [pl.BlockSpec((1, 1), const)]
                 + [pl.BlockSpec((SEQ // steps, D_MODEL), rows)])
    out_shape = ([jax.ShapeDtypeStruct((SHARD_IN, D_MODEL), F32)] * 4 + [jax.ShapeDtypeStruct((SHARD_OUT, D_MODEL), F32)] * 4
                 + [jax.ShapeDtypeStruct(s, F32) for s in small_shapes for _ in range(4)]
                 + [jax.ShapeDtypeStruct((1, 1), F32), jax.ShapeDtypeStruct((SEQ, D_MODEL), F32)])
    outs = pl.pallas_call(
        body, name="adam", grid=(steps,), in_specs=in_specs, out_specs=tuple(out_specs), out_shape=tuple(out_shape),
        compiler_params=_params(dimension_semantics=("arbitrary",)),
    )(*big_in, *big_out, gsum, *small, grad_x)
    return outs[0:4], outs[4:8], [outs[8 + 4 * s:12 + 4 * s] for s in range(6)], outs[32], outs[33]


def _rows_first(a):
    return jnp.transpose(a, (1, 0, 2))


def kernel(x, norm_in, w_in, conv_w, attn_sinks, norm_conv_out, norm_attn_out, w_out, norm_final, loss_target, m_norm_in, m_w_in, m_conv_w, m_attn_sinks, m_norm_conv_out, m_norm_attn_out, m_w_out, m_norm_final, v_norm_in, v_w_in, v_conv_w, v_attn_sinks, v_norm_conv_out, v_norm_attn_out, v_w_out, v_norm_final):
    x2d = x.reshape(SEQ, D_MODEL)
    target = loss_target.reshape(SEQ, D_MODEL)
    nf = norm_final.reshape(1, D_MODEL)

    w_in_t, m_w_in_t, v_w_in_t = w_in[0].T, m_w_in[0].T, v_w_in[0].T
    tiles = jnp.asarray(TILE_ORDER, jnp.int32).reshape(-1)
    w_in_full, h, proj, g_out, conv_full = _gather_in_proj(x2d, norm_in, w_in_t, w_out[0], _rows_first(conv_w), tiles)
    sinks = attn_sinks.reshape(N_Q_HEADS)

    mixed, attn, probs, shares = _mix_fwd(proj, conv_full, sinks, norm_conv_out, norm_attn_out)
    dx2, dx2b, dmixed, gnf, loss_part = _out_proj_loss(mixed, x2d, target, g_out.reshape(D_MIX, D_MODEL), nf)
    dproj, gslab = _mix_bwd(proj, dmixed, attn, probs, shares, conv_full, norm_conv_out, norm_attn_out)
    dw_in_chip, g_w_out = _dw_rs(mixed, dx2b, dproj, h)
    grad_x, g_w_in, gsum = _in_bwd_rs(dproj, w_in_full, x2d, dx2, norm_in, dw_in_chip, gslab, gnf, loss_part)

    small = (norm_in, m_norm_in, v_norm_in, attn_sinks, m_attn_sinks, v_attn_sinks,
             norm_conv_out, m_norm_conv_out, v_norm_conv_out, norm_attn_out, m_norm_attn_out, v_norm_attn_out,
             nf, m_norm_final.reshape(1, D_MODEL), v_norm_final.reshape(1, D_MODEL),
             _rows_first(conv_w), _rows_first(m_conv_w), _rows_first(v_conv_w))
    big_in, big_out, (s_ni, s_sk, s_nc, s_na, s_nf, s_cv), loss, grad_x = _adam_all(
        (w_in_t, g_w_in, m_w_in_t, v_w_in_t), (w_out[0], g_w_out, m_w_out[0], v_w_out[0]), gsum, small, grad_x)

    def leaves(k):
        return (s_ni[k], big_in[k].T[None], jnp.transpose(s_cv[k], (1, 0, 2)), s_sk[k], s_nc[k], s_na[k], big_out[k][None],
                s_nf[k].reshape(D_MODEL))

    return (loss.reshape(()), grad_x.reshape(1, SEQ, D_MODEL), *leaves(0), *leaves(1), *leaves(2), *leaves(3))
```

```python
import jax
import jax.numpy as jnp
from jax import lax
from jax.experimental import pallas as pl
from jax.experimental.pallas import tpu as pltpu

F32 = jnp.float32
BF16 = jnp.bfloat16
MESH = pl.DeviceIdType.MESH

N_DEV = 8
SEQ = 2048
D_MODEL = 1024
D_CONV = 1024
D_ATTN = 1024
D_KV = 128
HEAD_DIM = 64
N_Q_HEADS = 16
N_PAIRS = N_Q_HEADS // 2
PAIRS_PER_KV = N_PAIRS // 2
D_MIX = D_CONV + D_ATTN
D_PROJ = 6400
SHARD_IN = D_PROJ // N_DEV
SHARD_OUT = D_MIX // N_DEV
SHARD_CONV = D_CONV // N_DEV
OFF_CB, OFF_CC, OFF_CU, OFF_GC, OFF_Q, OFF_K, OFF_V, OFF_GA = 0, 1024, 2048, 3072, 4096, 5120, 5248, 5376
BLOCK = 128
N_BLOCKS = SEQ // BLOCK
HALO = 8
CHUNK = 16
N_CHUNKS = BLOCK // CHUNK
RMS_EPS = 1e-5
NEG = -1e30
SCALE = HEAD_DIM ** -0.5
SLOPES = tuple(2.0 ** (-8.0 * (h + 1) / N_Q_HEADS) for h in range(N_Q_HEADS))

ADAM_LR = 0.001
ADAM_B1 = 0.9
ADAM_B2 = 0.999
ADAM_EPS = 1e-08
ADAM_WD = 0.01
ADAM_STEP = 10

ROW_NORM_IN, ROW_NORM_CONV, ROW_NORM_ATTN, ROW_NORM_FINAL, ROW_CONV0, ROW_SINKS = 0, 1, 2, 3, 4, 7
LOSS_LANE = N_Q_HEADS
ACC_NORM_CONV, ACC_NORM_ATTN, ACC_CONV0, N_ACC = 0, 1, 2, 5

VMEM_LIMIT = 56 * 1024 * 1024

_NT = (((1,), (1,)), ((), ()))
_TN = (((0,), (0,)), ((), ()))


def _params(**kw):
    return pltpu.CompilerParams(vmem_limit_bytes=VMEM_LIMIT, **kw)


def _adamw(w, g, m, v):
    m = ADAM_B1 * m + (1.0 - ADAM_B1) * g
    v = ADAM_B2 * v + (1.0 - ADAM_B2) * (g * g)
    m_hat = m / (1.0 - ADAM_B1 ** ADAM_STEP)
    v_hat = v / (1.0 - ADAM_B2 ** ADAM_STEP)
    delta = -ADAM_LR * (m_hat / (jnp.sqrt(v_hat) + ADAM_EPS) + ADAM_WD * w)
    return delta, m, v


def _sigmoid(t):
    return 1.0 / (1.0 + jnp.exp(-t))


def _slot(px, py, pc):
    return 4 * px + 2 * py + pc


HALF_IN = SHARD_IN // 2
N_GATHER_KINDS = 13
W_OUT_KINDS = N_GATHER_KINDS + 7


IN_PROJ_TILE = 640
TILE_ORDER = ((0, 1, 2, 3, 4, 5, 6, 7, 8, 9), (3, 4, 0, 1, 2, 8, 9, 5, 6, 7),
              (5, 6, 0, 1, 7, 8, 9, 2, 3, 4), (8, 9, 3, 4, 5, 6, 7, 0, 1, 2))
TILES_OWN, TILES_NEIGHBOURS = 2, 7


def _tile(table_ref, p):
    chip = 2 * lax.axis_index("x") + lax.axis_index("y")
    return table_ref[chip * len(TILE_ORDER[0]) + p]


DW_TILE_ORDER = tuple(tuple(reversed(row)) for row in TILE_ORDER)


def _tiles_until_complete(chip, owner):
    lo, hi = owner * 2 * SHARD_IN, (owner + 1) * 2 * SHARD_IN
    touching = [t for t in range(len(TILE_ORDER[0])) if t * IN_PROJ_TILE < hi and (t + 1) * IN_PROJ_TILE > lo]
    return 1 + max(DW_TILE_ORDER[chip].index(t) for t in touching)


DW_TABLE = tuple(DW_TILE_ORDER[chip] + tuple(_tiles_until_complete(chip, chip ^ flip) for flip in (0, 2, 1, 3))
                 for chip in range(4))


def _dw_entry(table_ref, p):
    chip = 2 * lax.axis_index("x") + lax.axis_index("y")
    return table_ref[chip * len(DW_TABLE[0]) + p]


def _gather_in_proj(x, norm_in, w_in_sh, w_out_sh, conv_sh, tiles):
    tn = IN_PROJ_TILE
    steps = D_PROJ // tn
    tm = 256

    def body(tiles_ref, x_hbm, g_ref, win_ref, wout_ref, cv_ref, wt_ref, h_ref, proj_ref, gout_ref, conv_ref,
             gin_ref, gcv_ref, wob_ref, x_ref, send_sems, recv_sems, local_sems):
        p = pl.program_id(0)
        local_sem = local_sems.at[0]
        x, y, c = lax.axis_index("x"), lax.axis_index("y"), lax.axis_index("c")
        me, sibling = (x, y, c), (x, y, 1 - c)
        nx, ny, dg = (1 - x, y, c), (x, 1 - y, c), (1 - x, 1 - y, c)

        def other(dev):
            return (dev[0], dev[1], 1 - dev[2])

        def shard(dev):
            return gin_ref.at[pl.ds(pl.multiple_of(_slot(*dev) * SHARD_IN, 16), SHARD_IN), :]

        def half(dev, h):
            return gin_ref.at[pl.ds(pl.multiple_of(_slot(*dev) * SHARD_IN + h * HALF_IN, 16), HALF_IN), :]

        def rc(ref, k, to):
            return pltpu.make_async_remote_copy(src_ref=ref, dst_ref=ref, send_sem=send_sems.at[k],
                                                recv_sem=recv_sems.at[k], device_id=to, device_id_type=MESH)

        def cv(k, dev, to):
            s = _slot(*dev)
            return pltpu.make_async_remote_copy(src_ref=gcv_ref.at[s], dst_ref=gcv_ref.at[s],
                                                send_sem=send_sems.at[N_GATHER_KINDS + k],
                                                recv_sem=recv_sems.at[N_GATHER_KINDS + k], device_id=to, device_id_type=MESH)

        def own_copies():
            return [rc(shard(me), 0, sibling),
                    rc(half(me, 0), 1, nx), rc(half(me, 1), 2, nx),
                    rc(half(me, 1), 4, ny), rc(half(me, 0), 3, ny),
                    cv(0, me, sibling)] + [cv(1 + j, me, peer) for j, peer in enumerate((nx, ny, dg))]

        def pass_on(dev, h, k_in, k_ici, k_d2d, half=half, base=0):
            rc(half(dev, h), base + k_in, me).wait_recv()
            if k_ici is not None:
                rc(half(dev, h), base + k_ici, ny if dev is nx else nx).start()
            rc(half(dev, h), base + k_d2d, sibling).start()

        def out_half(dev, h):
            return gout_ref.at[_slot(*dev), pl.ds(h * (SHARD_OUT // 2), SHARD_OUT // 2), :]

        def own_out_copies():
            src = lambda h: wob_ref.at[pl.ds(h * (SHARD_OUT // 2), SHARD_OUT // 2), :]

            def send(ref, dst, k, to):
                return pltpu.make_async_remote_copy(src_ref=ref, dst_ref=dst, send_sem=send_sems.at[W_OUT_KINDS + k],
                                                    recv_sem=recv_sems.at[W_OUT_KINDS + k], device_id=to, device_id_type=MESH)

            return [send(wob_ref, gout_ref.at[_slot(*me)], 0, sibling),
                    send(src(0), out_half(me, 0), 1, nx), send(src(1), out_half(me, 1), 2, nx),
                    send(src(1), out_half(me, 1), 4, ny), send(src(0), out_half(me, 0), 3, ny)]

        def own_out_local():
            return pltpu.make_async_copy(wob_ref, gout_ref.at[_slot(*me)], local_sems.at[1])

        @pl.when(p == 0)
        def _():
            gin_ref[pl.ds(pl.multiple_of(_slot(*me) * SHARD_IN, 16), SHARD_IN), :] = win_ref[...].astype(BF16)
            gcv_ref[_slot(*me)] = jnp.zeros((8, SHARD_CONV), F32)
            gcv_ref[_slot(*me), 0:3, :] = cv_ref[:, 0, :]
            for cp in own_copies():
                cp.start()
            wob_ref[...] = wout_ref[...].astype(BF16)
            x_load = pltpu.make_async_copy(x_hbm, x_ref, local_sems.at[2])
            x_load.start()
            x_load.wait()
            for t in range(SEQ // tm):
                xv = x_ref[tm * t:tm * (t + 1), :]
                r = lax.rsqrt(jnp.mean(xv * xv, axis=-1, keepdims=True) + RMS_EPS)
                h_ref[tm * t:tm * (t + 1), :] = (xv * r * g_ref[...]).astype(BF16)
            rc(shard(sibling), 0, me).wait_recv()

        @pl.when(p == TILES_OWN)
        def _():
            for args in ((nx, 0, 1, 5, 7), (ny, 1, 4, 6, 10), (nx, 1, 2, None, 8), (ny, 0, 3, None, 9)):
                pass_on(*args)
            for j, peer in enumerate((nx, ny, dg)):
                cv(1 + j, peer, me).wait_recv()
                cv(4 + j, peer, sibling).start()
            for (dev, h), k in (((nx, 0), 7), ((nx, 1), 8), ((ny, 0), 9), ((ny, 1), 10)):
                rc(half(other(dev), h), k, me).wait_recv()
            own_out_local().start()
            for cp in own_out_copies():
                cp.start()

        @pl.when(p == TILES_NEIGHBOURS - 1)
        def _():
            pass_on(dg, 0, 5, None, 11)
            pass_on(dg, 1, 6, None, 12)

        @pl.when(p == TILES_NEIGHBOURS)
        def _():
            for (dev, h), k in (((dg, 0), 11), ((dg, 1), 12)):
                rc(half(other(dev), h), k, me).wait_recv()
            pltpu.make_async_copy(gin_ref, wt_ref, local_sem).start()

        @pl.when(p == steps - 2)
        def _():
            for args in ((nx, 0, 1, 5, 7), (ny, 1, 4, 6, 10), (nx, 1, 2, None, 8), (ny, 0, 3, None, 9)):
                pass_on(*args, half=out_half, base=W_OUT_KINDS)

        w = gin_ref[pl.ds(pl.multiple_of(_tile(tiles_ref, p) * tn, tn), tn), :]
        proj_ref[...] = lax.dot_general(h_ref[...], w, _NT, preferred_element_type=F32)

        @pl.when(p == steps - 1)
        def _():
            cv(0, sibling, me).wait_recv()
            for j, peer in enumerate((nx, ny, dg)):
                cv(4 + j, other(peer), me).wait_recv()
            for d in range(N_DEV):
                conv_ref[:, d * SHARD_CONV:(d + 1) * SHARD_CONV] = gcv_ref[d]
            relayed = [rc(half(nx, 0), 5, ny), rc(half(ny, 1), 6, nx)]
            relayed += [rc(half(dev, h), k, sibling) for (dev, h), k in
                        (((nx, 0), 7), ((nx, 1), 8), ((ny, 0), 9), ((ny, 1), 10), ((dg, 0), 11), ((dg, 1), 12))]
            relayed += [cv(4 + j, peer, sibling) for j, peer in enumerate((nx, ny, dg))]
            for cp in own_copies() + relayed:
                cp.wait_send()
            pltpu.make_async_copy(gin_ref, wt_ref, local_sem).wait()
            pass_on(dg, 0, 5, None, 11, half=out_half, base=W_OUT_KINDS)
            pass_on(dg, 1, 6, None, 12, half=out_half, base=W_OUT_KINDS)
            rc(gout_ref.at[_slot(*sibling)], W_OUT_KINDS, me).wait_recv()
            out_relayed = [rc(out_half(nx, 0), W_OUT_KINDS + 5, ny), rc(out_half(ny, 1), W_OUT_KINDS + 6, nx)]
            for (dev, h), k in (((nx, 0), 7), ((nx, 1), 8), ((ny, 0), 9), ((ny, 1), 10), ((dg, 0), 11), ((dg, 1), 12)):
                rc(out_half(other(dev), h), W_OUT_KINDS + k, me).wait_recv()
                out_relayed.append(rc(out_half(dev, h), W_OUT_KINDS + k, sibling))
            for cp in own_out_copies() + out_relayed:
                cp.wait_send()
            own_out_local().wait()

    vmem = pl.BlockSpec(memory_space=pltpu.VMEM)
    grid_spec = pltpu.PrefetchScalarGridSpec(
        num_scalar_prefetch=1, grid=(steps,),
        in_specs=[pl.BlockSpec(memory_space=pl.ANY), vmem, vmem, vmem, vmem],
        out_specs=(pl.BlockSpec(memory_space=pl.ANY), vmem,
                   pl.BlockSpec((SEQ, tn), lambda p, tiles_ref: (0, _tile(tiles_ref, p))),
                   pl.BlockSpec(memory_space=pl.ANY), vmem),
        scratch_shapes=[pltpu.VMEM((D_PROJ, D_MODEL), BF16), pltpu.VMEM((N_DEV, 8, SHARD_CONV), F32),
                        pltpu.VMEM((SHARD_OUT, D_MODEL), BF16), pltpu.VMEM((SEQ, D_MODEL), F32),
                        pltpu.SemaphoreType.DMA((W_OUT_KINDS + N_GATHER_KINDS,)),
                        pltpu.SemaphoreType.DMA((W_OUT_KINDS + N_GATHER_KINDS,)),
                        pltpu.SemaphoreType.DMA((3,))])
    return pl.pallas_call(
        body, name="gather_in_proj", grid_spec=grid_spec,
        out_shape=(jax.ShapeDtypeStruct((D_PROJ, D_MODEL), BF16), jax.ShapeDtypeStruct((SEQ, D_MODEL), BF16),
                   jax.ShapeDtypeStruct((SEQ, D_PROJ), F32), jax.ShapeDtypeStruct((N_DEV, SHARD_OUT, D_MODEL), BF16),
                   jax.ShapeDtypeStruct((8, D_CONV), F32)),
        compiler_params=_params(dimension_semantics=("arbitrary",)),
    )(tiles, x, norm_in, w_in_sh, w_out_sh, conv_sh)


def _shard_sum(src, own, d2d, ici, send_sems, recv_sems, local_sems, base=0):
    x, y, c = lax.axis_index("x"), lax.axis_index("y"), lax.axis_index("c")
    sibling = (x, y, 1 - c)
    chips = [(x, y), (1 - x, y), (x, 1 - y), (1 - x, 1 - y)]

    def rcopy(s, d, k, to):
        return pltpu.make_async_remote_copy(src_ref=s, dst_ref=d, send_sem=send_sems.at[base + k],
                                            recv_sem=recv_sems.at[base + k], device_id=to, device_id_type=MESH)

    def mine(k):
        return pltpu.make_async_copy(src.at[_slot(*chips[k], c)], own.at[k], local_sems.at[k])

    def to_sibling(k):
        return rcopy(src.at[_slot(*chips[k], 1 - c)], d2d.at[k], k, sibling)

    def to_chip(k):
        return rcopy(own.at[k], ici.at[k - 1], 3 + k, (*chips[k], c))

    def start():
        for k in range(4):
            mine(k).start()
            to_sibling(k).start()

    def forward():
        for k in range(1, 4):
            mine(k).wait()
            to_sibling(k).wait_recv()
            own[k] = (own[k].astype(F32) + d2d[k].astype(F32)).astype(BF16)
            to_chip(k).start()

    def finish():
        mine(0).wait()
        to_sibling(0).wait_recv()
        acc = own[0].astype(F32) + d2d[0].astype(F32)
        for k in range(1, 4):
            to_chip(k).wait_recv()
            acc = acc + ici[k - 1].astype(F32)
        for k in range(4):
            to_sibling(k).wait_send()
        for k in range(1, 4):
            to_chip(k).wait_send()
        return acc

    return start, forward, finish


def _shard_sum_scratch(rows):
    return [pltpu.VMEM((4, rows, D_MODEL), BF16), pltpu.VMEM((4, rows, D_MODEL), BF16),
            pltpu.VMEM((3, rows, D_MODEL), BF16)]


N_SHARD_SUM_SEMS = 7


OWN, NX, NY, DG = range(4)
N_CHIP_SUM_SEMS = 6


def _chip_sum(dwt, d2d, via, out_hbm, tiles_until, send_sems, recv_sems, local_sems, base, local_base):
    x, y, c = lax.axis_index("x"), lax.axis_index("y"), lax.axis_index("c")
    sibling, nx, ny = (x, y, 1 - c), (1 - x, y, c), (x, 1 - y, c)
    chips = [(x, y), (1 - x, y), (x, 1 - y), (1 - x, 1 - y)]

    def shard(s):
        return dwt.at[pl.ds(pl.multiple_of(s * SHARD_IN, 16), SHARD_IN), :]

    def half(ref, h):
        return ref.at[pl.ds(h * HALF_IN, HALF_IN), :]

    def rc(s, d, k, to):
        return pltpu.make_async_remote_copy(src_ref=s, dst_ref=d, send_sem=send_sems.at[base + k],
                                            recv_sem=recv_sems.at[base + k], device_id=to, device_id_type=MESH)

    def to_sibling(k):
        return rc(shard(_slot(*chips[k], 1 - c)), d2d.at[k], k, sibling)

    for_dg = (lambda: rc(half(d2d.at[DG], 0), via.at[0], 4, nx), lambda: rc(half(d2d.at[DG], 1), via.at[1], 5, ny))

    def save(k):
        return pltpu.make_async_copy(d2d.at[k], out_hbm.at[k], local_sems.at[local_base + k])

    def chip_sum(k):
        to_sibling(k).wait_recv()
        d2d[k] = (shard(_slot(*chips[k], c))[...].astype(F32) + d2d[k].astype(F32)).astype(BF16)

    def before_tile(n):
        for k in (NX, NY, DG):
            @pl.when(tiles_until(k) == n)
            def _():
                to_sibling(k).start()

            @pl.when(tiles_until(k) + 1 == n)
            def _():
                chip_sum(k)
                if k == DG:
                    for cp in for_dg:
                        cp().start()

    def after_tiles():
        to_sibling(OWN).start()

    def finish():
        for k, h in ((NY, 0), (NX, 1)):
            for_dg[h]().wait_recv()
            rows = pl.ds(h * HALF_IN, HALF_IN)
            d2d[k, rows, :] = (d2d[k, rows, :].astype(F32) + via[h].astype(F32)).astype(BF16)
            save(k).start()
        chip_sum(OWN)
        save(OWN).start()
        for k in (OWN, NX, NY):
            save(k).wait()
        for k in range(4):
            to_sibling(k).wait_send()
        for cp in for_dg:
            cp().wait_send()

    return before_tile, after_tiles, finish


N_ICI_SUM_SEMS = 2


def _ici_sum(src, own, ici, send_sems, recv_sems, local_sems, base=0):
    x, y, c = lax.axis_index("x"), lax.axis_index("y"), lax.axis_index("c")

    def to_chip(k, to):
        return pltpu.make_async_remote_copy(src_ref=src.at[k], dst_ref=ici.at[k - 1], send_sem=send_sems.at[base + k - 1],
                                            recv_sem=recv_sems.at[base + k - 1], device_id=to, device_id_type=MESH)

    copies = (lambda: to_chip(NX, (1 - x, y, c)), lambda: to_chip(NY, (x, 1 - y, c)))
    mine = lambda: pltpu.make_async_copy(src.at[OWN], own, local_sems.at[0])

    def start():
        for cp in copies + (mine,):
            cp().start()

    def finish():
        mine().wait()
        for cp in copies:
            cp().wait_recv()
        acc = own[...].astype(F32) + ici[0].astype(F32) + ici[1].astype(F32)
        for cp in copies:
            cp().wait_send()
        return acc

    return start, finish


def _slab_sum(myslab, slabs, send_sems, recv_sems, base):
    x, y, c = lax.axis_index("x"), lax.axis_index("y"), lax.axis_index("c")
    me = _slot(x, y, c)
    peers = [(x, y, 1 - c), (1 - x, y, c), (x, 1 - y, c), (1 - x, 1 - y, c),
             (1 - x, y, 1 - c), (x, 1 - y, 1 - c), (1 - x, 1 - y, 1 - c)]

    def cp(k):
        return pltpu.make_async_remote_copy(src_ref=myslab, dst_ref=slabs.at[me], send_sem=send_sems.at[base + k],
                                            recv_sem=recv_sems.at[base + k], device_id=peers[k], device_id_type=MESH)

    def start():
        slabs[me] = myslab[...]
        for k in range(7):
            cp(k).start()

    def finish():
        for k in range(7):
            cp(k).wait_recv()
        total = slabs[0]
        for d in range(1, N_DEV):
            total = total + slabs[d]
        for k in range(7):
            cp(k).wait_send()
        return total

    return start, finish


def _chunk_rows(r):
    return slice(r * CHUNK, (r + 1) * CHUNK)


def _conv_halo(cch_ref, cuh_ref, n):
    zh = jnp.where(n > 0, cch_ref[...] * cuh_ref[...], 0.0)
    return jnp.concatenate([zh] * (CHUNK // HALO), axis=0)


def _conv_chunk(pj_ref, zhalo, cw, r):
    rows = _chunk_rows(r)
    cc = pj_ref[rows, OFF_CC:OFF_CC + D_CONV]
    cu = pj_ref[rows, OFF_CU:OFF_CU + D_CONV]
    z = cc * cu
    before = _chunk_rows(r - 1)
    zprev = pj_ref[before, OFF_CC:OFF_CC + D_CONV] * pj_ref[before, OFF_CU:OFF_CU + D_CONV] if r > 0 else zhalo
    row = lax.broadcasted_iota(jnp.int32, (CHUNK, D_CONV), 0)
    z1 = jnp.where(row < 1, pltpu.roll(zprev, 1, 0), pltpu.roll(z, 1, 0))
    z2 = jnp.where(row < 2, pltpu.roll(zprev, 2, 0), pltpu.roll(z, 2, 0))
    co = cw[0] * z2 + cw[1] * z1 + cw[2] * z
    return cc, cu, z, z1, z2, co


def _gated_norm(a, gain, t):
    r = lax.rsqrt(jnp.mean(a * a, axis=-1, keepdims=True) + RMS_EPS)
    return a * r * gain * (t * _sigmoid(t))


def _kv_bands(pj, kvp_ref):
    lane = lax.broadcasted_iota(jnp.int32, (2 * BLOCK, D_KV), 1)
    lo = lane < HEAD_DIM

    def bands(prev, cur):
        b = jnp.concatenate([prev, cur], axis=0)
        br = pltpu.roll(b, HEAD_DIM, 1)
        zero = jnp.zeros_like(b)
        return ((jnp.where(lo, b, zero).astype(BF16), jnp.where(lo, zero, br).astype(BF16)),
                (jnp.where(lo, br, zero).astype(BF16), jnp.where(lo, zero, b).astype(BF16)))

    ks = bands(kvp_ref[:, 0:D_KV], pj[:, OFF_K:OFF_K + D_KV])
    vs = bands(kvp_ref[:, D_KV:2 * D_KV], pj[:, OFF_V:OFF_V + D_KV])
    return ks, vs


STACK = PAIRS_PER_KV * BLOCK


def _head(j, i, e):
    return 2 * (PAIRS_PER_KV * j + i) + e


def _pair_cols(j, i, off):
    p = PAIRS_PER_KV * j + i
    return slice(off + 128 * p, off + 128 * (p + 1))


def _fill_attn_bias(bias_scr, first_block):
    qi = lax.broadcasted_iota(jnp.int32, (BLOCK, 2 * BLOCK), 0)
    kj = lax.broadcasted_iota(jnp.int32, (BLOCK, 2 * BLOCK), 1)
    dist = BLOCK + qi - kj
    valid = (dist >= 0) & (dist < BLOCK)
    if first_block:
        valid = valid & (kj >= BLOCK)
    distf = dist.astype(F32)
    for j in range(2):
        for e in range(2):
            for i in range(PAIRS_PER_KV):
                bias_scr[2 * j + e, BLOCK * i:BLOCK * (i + 1), :] = jnp.where(valid, -SLOPES[_head(j, i, e)] * distf, NEG)


def _q_stack(pj, j):
    return jnp.concatenate([(pj[:, _pair_cols(j, i, OFF_Q)] * SCALE).astype(BF16) for i in range(PAIRS_PER_KV)], axis=0)


def _attn_probs(q_stack, kband, bias_ref, sinks):
    s = lax.dot_general(q_stack, kband, _NT, preferred_element_type=F32)
    ones = jnp.ones((128, 128), BF16)
    probs, shares = [], []
    for i, sink in enumerate(sinks):
        rows = slice(BLOCK * i, BLOCK * (i + 1))
        t = s[rows, :] + bias_ref[rows, :]
        m = jnp.broadcast_to(jnp.max(t, axis=-1, keepdims=True), (BLOCK, 128))
        m = jnp.maximum(m, sink)
        p = [jnp.exp(t[:, :128] - m), jnp.exp(t[:, 128:] - m)]
        es = jnp.exp(sink - m)
        total = (jnp.dot(p[0].astype(BF16), ones, preferred_element_type=F32)
                 + jnp.dot(p[1].astype(BF16), ones, preferred_element_type=F32))
        inv = 1.0 / (total + es)
        probs.append(jnp.concatenate([p[0] * inv, p[1] * inv], axis=1))
        shares.append(es * inv)
    return jnp.concatenate(probs, axis=0), jnp.concatenate(shares, axis=0)


def _attn_group(pj, ks, vs, bias_scr, sink_ref, j):
    q_stack = _q_stack(pj, j)
    out, probs, shares = None, [], []
    for e in range(2):
        p, ps = _attn_probs(q_stack, ks[j][e], bias_scr.at[2 * j + e],
                            [sink_ref[_head(j, i, e)] for i in range(PAIRS_PER_KV)])
        p = p.astype(BF16)
        o = jnp.dot(p, vs[j][e], preferred_element_type=F32)
        out = o if out is None else out + o
        probs.append(p)
        shares.append(ps)
    return out, probs, shares


def _mix_fwd(proj, conv_full, sinks, norm_conv, norm_attn):
    def body(pj_ref, kvp_ref, cch_ref, cuh_ref, cw_ref, sink_ref, gc_ref, ga_ref,
             mixed_ref, attn_scr, p_ref, ps_ref, bias_scr):
        n = pl.program_id(0)
        pj = pj_ref

        @pl.when(n == 0)
        def _():
            _fill_attn_bias(bias_scr, first_block=True)

        @pl.when(n == 1)
        def _():
            _fill_attn_bias(bias_scr, first_block=False)

        zhalo = _conv_halo(cch_ref, cuh_ref, n)
        cw = (cw_ref[0:1, :], cw_ref[1:2, :], cw_ref[2:3, :])
        gain_c = gc_ref[...]

        for r in range(N_CHUNKS):
            rows = _chunk_rows(r)
            co = _conv_chunk(pj_ref, zhalo, cw, r)[-1]
            y = _gated_norm(pj_ref[rows, OFF_CB:OFF_CB + D_CONV] * co, gain_c, pj_ref[rows, OFF_GC:OFF_GC + D_CONV])
            mixed_ref[rows, 0:D_CONV] = y.astype(BF16)

        ks, vs = _kv_bands(pj, kvp_ref)
        for j in range(2):
            out, probs, shares = _attn_group(pj, ks, vs, bias_scr, sink_ref, j)
            for e in range(2):
                p_ref[0, 2 * j + e] = probs[e]
                ps_ref[0, 2 * j + e] = shares[e]
            for i in range(PAIRS_PER_KV):
                attn_scr[:, _pair_cols(j, i, 0)] = out[BLOCK * i:BLOCK * (i + 1), :]
        gain_a = ga_ref[...]

        for r in range(N_CHUNKS):
            rows = _chunk_rows(r)
            y = _gated_norm(attn_scr[rows, :], gain_a, pj_ref[rows, OFF_GA:OFF_GA + D_ATTN])
            mixed_ref[rows, D_CONV:D_MIX] = y.astype(BF16)

    per_block = BLOCK // HALO
    return pl.pallas_call(
        body, name="mix_fwd", grid=(N_BLOCKS,),
        in_specs=[
            pl.BlockSpec((BLOCK, D_PROJ), lambda n: (n, 0)),
            pl.BlockSpec((BLOCK, 2 * D_KV), lambda n: (jnp.maximum(n - 1, 0), OFF_K // (2 * D_KV))),
            pl.BlockSpec((HALO, D_CONV), lambda n: (jnp.maximum(n * per_block - 1, 0), OFF_CC // D_CONV)),
            pl.BlockSpec((HALO, D_CONV), lambda n: (jnp.maximum(n * per_block - 1, 0), OFF_CU // D_CONV)),
            pl.BlockSpec((8, D_CONV), lambda n: (0, 0)),
            pl.BlockSpec(memory_space=pltpu.SMEM),
            pl.BlockSpec((1, D_CONV), lambda n: (0, 0)),
            pl.BlockSpec((1, D_ATTN), lambda n: (0, 0)),
        ],
        out_specs=(pl.BlockSpec((BLOCK, D_MIX), lambda n: (n, 0)), pl.BlockSpec((BLOCK, D_ATTN), lambda n: (n, 0)),
                   pl.BlockSpec((1, 4, STACK, 2 * BLOCK), lambda n: (n, 0, 0, 0)),
                   pl.BlockSpec((1, 4, STACK, 128), lambda n: (n, 0, 0, 0))),
        out_shape=(jax.ShapeDtypeStruct((SEQ, D_MIX), BF16), jax.ShapeDtypeStruct((SEQ, D_ATTN), F32),
                   jax.ShapeDtypeStruct((N_BLOCKS, 4, STACK, 2 * BLOCK), BF16),
                   jax.ShapeDtypeStruct((N_BLOCKS, 4, STACK, 128), F32)),
        scratch_shapes=[pltpu.VMEM((4, STACK, 2 * BLOCK), F32)],
        compiler_params=_params(dimension_semantics=("arbitrary",)),
    )(proj, proj, proj, proj, conv_full, sinks, norm_conv, norm_attn)


def _out_proj_loss(mixed, x, target, w_out_full, norm_final):
    tm = 256

    def body(mx_ref, x_ref, t_ref, w_ref, g_ref, dx2_ref, dx2b_ref, dmix_ref, gnf_ref, loss_ref):
        i = pl.program_id(0)
        w = w_ref[...]
        x2 = x_ref[...] + jnp.dot(mx_ref[...], w, preferred_element_type=F32)
        r = lax.rsqrt(jnp.mean(x2 * x2, axis=-1, keepdims=True) + RMS_EPS)
        xn = x2 * r
        g = g_ref[...]
        err = xn * g - t_ref[...]
        part = 0.5 * jnp.sum(jnp.mean(err * err, axis=-1, keepdims=True), axis=0, keepdims=True)
        dy = err * (1.0 / D_MODEL)
        gnf = jnp.sum(dy * xn, axis=0, keepdims=True)
        u = dy * g
        dx2 = r * (u - xn * jnp.mean(u * xn, axis=-1, keepdims=True))
        dx2_ref[...] = dx2
        dx2b = dx2.astype(BF16)
        dx2b_ref[...] = dx2b
        dmix_ref[...] = lax.dot_general(dx2b, w, _NT, preferred_element_type=F32)

        @pl.when(i == 0)
        def _():
            gnf_ref[...] = jnp.zeros_like(gnf_ref)
            loss_ref[...] = jnp.zeros_like(loss_ref)

        gnf_ref[...] += gnf
        loss_ref[...] += jnp.broadcast_to(part, loss_ref.shape)

    return pl.pallas_call(
        body, name="out_proj_loss", grid=(SEQ // tm,),
        in_specs=[pl.BlockSpec((tm, D_MIX), lambda i: (i, 0)), pl.BlockSpec((tm, D_MODEL), lambda i: (i, 0)),
                  pl.BlockSpec((tm, D_MODEL), lambda i: (i, 0)), pl.BlockSpec(memory_space=pltpu.VMEM),
                  pl.BlockSpec((1, D_MODEL), lambda i: (0, 0))],
        out_specs=(pl.BlockSpec((tm, D_MODEL), lambda i: (i, 0)), pl.BlockSpec((tm, D_MODEL), lambda i: (i, 0)),
                   pl.BlockSpec((tm, D_MIX), lambda i: (i, 0)),
                   pl.BlockSpec((1, D_MODEL), lambda i: (0, 0)), pl.BlockSpec((8, 128), lambda i: (0, 0))),
        out_shape=(jax.ShapeDtypeStruct((SEQ, D_MODEL), F32), jax.ShapeDtypeStruct((SEQ, D_MODEL), BF16),
                   jax.ShapeDtypeStruct((SEQ, D_MIX), F32),
                   jax.ShapeDtypeStruct((1, D_MODEL), F32), jax.ShapeDtypeStruct((8, 128), F32)),
        compiler_params=_params(dimension_semantics=("arbitrary",)),
    )(mixed, x, target, w_out_full, norm_final)


def _gated_norm_bwd(a, gain, t, dy):
    r = lax.rsqrt(jnp.mean(a * a, axis=-1, keepdims=True) + RMS_EPS)
    an = a * r
    sg = _sigmoid(t)
    dn = dy * (t * sg)
    dt = dy * (an * gain) * (sg * (1.0 + t * (1.0 - sg)))
    u = dn * gain
    da = r * (u - an * jnp.mean(u * an, axis=-1, keepdims=True))
    return da, dt, dn * an


def _mix_bwd(proj, dmixed, attn, probs, shares, conv_full, norm_conv, norm_attn):
    def body(pj_ref, kvp_ref, cch_ref, cuh_ref, dmx_ref, attn_ref, p_ref, ps_ref, cw_ref, gc_ref, ga_ref,
             dpj_ref, gslab_ref, dattn_scr, nxt_scr, dkv_scr, acc_scr):
        step = pl.program_id(0)
        n = N_BLOCKS - 1 - step
        pj = pj_ref

        @pl.when(step == 0)
        def _():
            gslab_ref[...] = jnp.zeros_like(gslab_ref)
            nxt_scr[...] = jnp.zeros_like(nxt_scr)
            dkv_scr[...] = jnp.zeros_like(dkv_scr)
            acc_scr[...] = jnp.zeros_like(acc_scr)

        zhalo = _conv_halo(cch_ref, cuh_ref, n)
        cw = (cw_ref[0:1, :], cw_ref[1:2, :], cw_ref[2:3, :])
        gain_c = gc_ref[...]
        row = lax.broadcasted_iota(jnp.int32, (CHUNK, D_CONV), 0)

        dco_after = nxt_scr[...]
        for r in reversed(range(N_CHUNKS)):
            rows = _chunk_rows(r)
            cc, cu, z, z1, z2, co = _conv_chunk(pj_ref, zhalo, cw, r)
            cb = pj_ref[rows, OFF_CB:OFF_CB + D_CONV]
            da, dgate, gterm = _gated_norm_bwd(cb * co, gain_c, pj_ref[rows, OFF_GC:OFF_GC + D_CONV],
                                               dmx_ref[rows, 0:D_CONV])
            dpj_ref[rows, OFF_GC:OFF_GC + D_CONV] = dgate.astype(BF16)
            dpj_ref[rows, OFF_CB:OFF_CB + D_CONV] = (da * co).astype(BF16)
            dco = da * cb
            dco1 = jnp.where(row >= CHUNK - 1, pltpu.roll(dco_after, CHUNK - 1, 0), pltpu.roll(dco, CHUNK - 1, 0))
            dco2 = jnp.where(row >= CHUNK - 2, pltpu.roll(dco_after, CHUNK - 2, 0), pltpu.roll(dco, CHUNK - 2, 0))
            dz = cw[2] * dco + cw[1] * dco1 + cw[0] * dco2
            dpj_ref[rows, OFF_CC:OFF_CC + D_CONV] = (dz * cu).astype(BF16)
            dpj_ref[rows, OFF_CU:OFF_CU + D_CONV] = (dz * cc).astype(BF16)
            acc_scr[ACC_NORM_CONV] += gterm
            acc_scr[ACC_CONV0] += dco * z2
            acc_scr[ACC_CONV0 + 1] += dco * z1
            acc_scr[ACC_CONV0 + 2] += dco * z
            dco_after = dco
        nxt_scr[...] = dco_after

        ks, vs = _kv_bands(pj, kvp_ref)
        gain_a = ga_ref[...]

        for r in range(N_CHUNKS):
            rows = _chunk_rows(r)
            da, dgate, gterm = _gated_norm_bwd(attn_ref[rows, :], gain_a, pj_ref[rows, OFF_GA:OFF_GA + D_ATTN],
                                               dmx_ref[rows, D_CONV:D_MIX])
            dpj_ref[rows, OFF_GA:OFF_GA + D_ATTN] = dgate.astype(BF16)
            dattn_scr[rows, :] = da
            acc_scr[ACC_NORM_ATTN] += gterm

        in_lo = lax.broadcasted_iota(jnp.int32, (128, 128), 0) < HEAD_DIM
        half_ones = (jnp.where(in_lo, 1.0, 0.0).astype(BF16), jnp.where(in_lo, 0.0, 1.0).astype(BF16))
        lane_s = lax.broadcasted_iota(jnp.int32, (1, D_MODEL), 1)
        gsink = jnp.zeros((1, D_MODEL), F32)
        dk_t, dv_t = [], []
        for j in range(2):
            q_stack = _q_stack(pj, j)
            do_f = jnp.concatenate([dattn_scr[:, _pair_cols(j, i, 0)] for i in range(PAIRS_PER_KV)], axis=0)
            o_f = jnp.concatenate([attn_ref[:, _pair_cols(j, i, 0)] for i in range(PAIRS_PER_KV)], axis=0)
            prod = (do_f * o_f).astype(BF16)
            deltas = [jnp.dot(prod, half_ones[e], preferred_element_type=F32) for e in range(2)]
            do_b = do_f.astype(BF16)
            q_t, do_t = q_stack.T, do_b.T
            dq, dk_j, dv_j = None, None, None
            for e in range(2):
                p = p_ref[0, 2 * j + e]
                dp = lax.dot_general(do_b, vs[j][e], _NT, preferred_element_type=F32)
                ds = []
                for i in range(PAIRS_PER_KV):
                    rows = slice(BLOCK * i, BLOCK * (i + 1))
                    delta = deltas[e][rows, :]
                    ds.append((p[rows, :].astype(F32) * (dp[rows, :] - jnp.concatenate([delta, delta], axis=1))).astype(BF16))
                    gs_h = -jnp.sum(ps_ref[0, 2 * j + e, rows, 0:1] * delta[:, 0:1], axis=0, keepdims=True)
                    gsink = gsink + jnp.where(lane_s == _head(j, i, e), gs_h, 0.0)
                ds = jnp.concatenate(ds, axis=0)
                t = jnp.dot(ds, ks[j][e], preferred_element_type=F32)
                dq = t if dq is None else dq + t
                half = slice(HEAD_DIM * e, HEAD_DIM * (e + 1))
                a = jnp.dot(q_t[half, :], ds, preferred_element_type=F32)
                b = jnp.dot(do_t[half, :], p, preferred_element_type=F32)
                dk_j = a if dk_j is None else dk_j + a
                dv_j = b if dv_j is None else dv_j + b
            for i in range(PAIRS_PER_KV):
                dpj_ref[:, _pair_cols(j, i, OFF_Q)] = (dq[BLOCK * i:BLOCK * (i + 1), :] * SCALE).astype(BF16)
            dk_t.append(dk_j)
            dv_t.append(dv_j)
        dk = jnp.concatenate(dk_t, axis=0).T
        dv = jnp.concatenate(dv_t, axis=0).T
        dpj_ref[:, OFF_K:OFF_K + D_KV] = (dk[BLOCK:, :] + dkv_scr[:, 0:D_KV]).astype(BF16)
        dpj_ref[:, OFF_V:OFF_V + D_KV] = (dv[BLOCK:, :] + dkv_scr[:, D_KV:2 * D_KV]).astype(BF16)
        dkv_scr[:, 0:D_KV] = dk[:BLOCK, :]
        dkv_scr[:, D_KV:2 * D_KV] = dv[:BLOCK, :]
        gslab_ref[ROW_SINKS:ROW_SINKS + 1, :] += gsink

        @pl.when(step == N_BLOCKS - 1)
        def _():
            for k, slab_row in ((ACC_NORM_CONV, ROW_NORM_CONV), (ACC_NORM_ATTN, ROW_NORM_ATTN), (ACC_CONV0, ROW_CONV0),
                                (ACC_CONV0 + 1, ROW_CONV0 + 1), (ACC_CONV0 + 2, ROW_CONV0 + 2)):
                gslab_ref[slab_row:slab_row + 1, :] = jnp.sum(acc_scr[k], axis=0, keepdims=True)

    per_block = BLOCK // HALO
    last = N_BLOCKS - 1
    return pl.pallas_call(
        body, name="mix_bwd", grid=(N_BLOCKS,),
        in_specs=[
            pl.BlockSpec((BLOCK, D_PROJ), lambda s: (last - s, 0)),
            pl.BlockSpec((BLOCK, 2 * D_KV), lambda s: (jnp.maximum(last - s - 1, 0), OFF_K // (2 * D_KV))),
            pl.BlockSpec((HALO, D_CONV), lambda s: (jnp.maximum((last - s) * per_block - 1, 0), OFF_CC // D_CONV)),
            pl.BlockSpec((HALO, D_CONV), lambda s: (jnp.maximum((last - s) * per_block - 1, 0), OFF_CU // D_CONV)),
            pl.BlockSpec((BLOCK, D_MIX), lambda s: (last - s, 0)),
            pl.BlockSpec((BLOCK, D_ATTN), lambda s: (last - s, 0)),
            pl.BlockSpec((1, 4, STACK, 2 * BLOCK), lambda s: (last - s, 0, 0, 0)),
            pl.BlockSpec((1, 4, STACK, 128), lambda s: (last - s, 0, 0, 0)),
            pl.BlockSpec((8, D_CONV), lambda s: (0, 0)),
            pl.BlockSpec((1, D_CONV), lambda s: (0, 0)),
            pl.BlockSpec((1, D_ATTN), lambda s: (0, 0)),
        ],
        out_specs=(pl.BlockSpec((BLOCK, D_PROJ), lambda s: (last - s, 0)),
                   pl.BlockSpec((8, D_MODEL), lambda s: (0, 0))),
        out_shape=(jax.ShapeDtypeStruct((SEQ, D_PROJ), BF16), jax.ShapeDtypeStruct((8, D_MODEL), F32)),
        scratch_shapes=[pltpu.VMEM((BLOCK, D_ATTN), F32), pltpu.VMEM((CHUNK, D_CONV), F32),
                        pltpu.VMEM((BLOCK, 2 * D_KV), F32), pltpu.VMEM((N_ACC, CHUNK, D_MODEL), F32)],
        compiler_params=_params(dimension_semantics=("arbitrary",)),
    )(proj, proj, proj, proj, dmixed, attn, probs, shares, conv_full, norm_conv, norm_attn)


def _in_bwd_rs(dproj, w_full, x, dx2, norm_in, dw_in_chip, gslab, gnf, loss_part):
    tm = 256
    steps = SEQ // tm

    def body(dp_ref, w_hbm, x_ref, dx2_ref, g_ref, dwi_ref, gs_ref, gnf_ref, lp_ref, gx_ref, gwin_ref, gsum_ref,
             gni_scr, own, ici, myslab, slabs, w_ref, send_sems, recv_sems, local_sems):
        i = pl.program_id(0)
        rs_start, rs_finish = _ici_sum(dwi_ref, own, ici, send_sems, recv_sems, local_sems)
        slab_start, slab_finish = _slab_sum(myslab, slabs, send_sems, recv_sems, N_ICI_SUM_SEMS)

        @pl.when(i == 0)
        def _():
            gni_scr[...] = jnp.zeros_like(gni_scr)
            rs_start()
            w_load = pltpu.make_async_copy(w_hbm, w_ref, local_sems.at[1])
            w_load.start()
            w_load.wait()

        dh = jnp.dot(dp_ref[...], w_ref[...], preferred_element_type=F32)
        xv = x_ref[...]
        r = lax.rsqrt(jnp.mean(xv * xv, axis=-1, keepdims=True) + RMS_EPS)
        xn = xv * r
        u = dh * g_ref[...]
        gx_ref[...] = dx2_ref[...] + r * (u - xn * jnp.mean(u * xn, axis=-1, keepdims=True))
        gni_scr[...] += jnp.sum(dh * xn, axis=0, keepdims=True)

        @pl.when(i == steps - 1)
        def _():
            row = lax.broadcasted_iota(jnp.int32, (8, D_MODEL), 0)
            lane = lax.broadcasted_iota(jnp.int32, (8, D_MODEL), 1)
            slab = jnp.where(row == ROW_NORM_IN, gni_scr[...], jnp.where(row == ROW_NORM_FINAL, gnf_ref[...], gs_ref[...]))
            myslab[...] = jnp.where((row == ROW_SINKS) & (lane == LOSS_LANE), lp_ref[0:1, 0:1], slab)
            slab_start()
            gwin_ref[...] = rs_finish()
            gsum_ref[...] = slab_finish()

    const = lambda i: (0, 0)
    return pl.pallas_call(
        body, name="in_bwd", grid=(steps,),
        in_specs=[pl.BlockSpec((tm, D_PROJ), lambda i: (i, 0)), pl.BlockSpec(memory_space=pl.ANY),
                  pl.BlockSpec((tm, D_MODEL), lambda i: (i, 0)), pl.BlockSpec((tm, D_MODEL), lambda i: (i, 0)),
                  pl.BlockSpec((1, D_MODEL), const), pl.BlockSpec(memory_space=pl.ANY),
                  pl.BlockSpec((8, D_MODEL), const), pl.BlockSpec((1, D_MODEL), const), pl.BlockSpec((8, 128), const)],
        out_specs=(pl.BlockSpec((tm, D_MODEL), lambda i: (i, 0)), pl.BlockSpec((SHARD_IN, D_MODEL), const),
                   pl.BlockSpec((8, D_MODEL), const)),
        out_shape=(jax.ShapeDtypeStruct((SEQ, D_MODEL), F32), jax.ShapeDtypeStruct((SHARD_IN, D_MODEL), F32),
                   jax.ShapeDtypeStruct((8, D_MODEL), F32)),
        scratch_shapes=[pltpu.VMEM((1, D_MODEL), F32), pltpu.VMEM((SHARD_IN, D_MODEL), BF16),
                        pltpu.VMEM((2, SHARD_IN, D_MODEL), BF16),
                        pltpu.VMEM((8, D_MODEL), F32), pltpu.VMEM((N_DEV, 8, D_MODEL), F32),
                        pltpu.VMEM((D_PROJ, D_MODEL), BF16),
                        pltpu.SemaphoreType.DMA((N_ICI_SUM_SEMS + 7,)), pltpu.SemaphoreType.DMA((N_ICI_SUM_SEMS + 7,)),
                        pltpu.SemaphoreType.DMA((2,))],
        compiler_params=_params(dimension_semantics=("arbitrary",)),
    )(dproj, w_full, x, dx2, norm_in, dw_in_chip, gslab, gnf, loss_part)


def _dw_rs(mixed, dx2b, dproj, h, table):
    tn_out, tn = 512, IN_PROJ_TILE
    out_steps, in_steps = D_MIX // tn_out, D_PROJ // tn
    steps = out_steps + in_steps
    forward_step = out_steps + 1

    def in_tile(table_ref, i):
        return _dw_entry(table_ref, jnp.maximum(i - out_steps, 0))

    def body(table_ref, mx_ref, dxb_ref, a_ref, b_ref, chip_ref, gwo_ref, dwo, dwt, d2d_in, via, own, d2d, ici,
             send_sems, recv_sems, local_sems):
        i = pl.program_id(0)
        rs_start, rs_forward, rs_finish = _shard_sum(dwo, own, d2d, ici, send_sems, recv_sems, local_sems)
        before_tile, after_tiles, chip_finish = _chip_sum(
            dwt, d2d_in, via, chip_ref, lambda k: _dw_entry(table_ref, in_steps + k), send_sems, recv_sems, local_sems,
            N_SHARD_SUM_SEMS, 4)

        @pl.when(i < out_steps)
        def _():
            tile = lax.dot_general(mx_ref[...], dxb_ref[...], _TN, preferred_element_type=F32).astype(BF16)
            for half in range(tn_out // SHARD_OUT):
                dwo[(tn_out // SHARD_OUT) * i + half] = tile[SHARD_OUT * half:SHARD_OUT * (half + 1), :]

        @pl.when(i == out_steps)
        def _():
            rs_start()

        @pl.when(i >= out_steps)
        def _():
            before_tile(i - out_steps)
            tile = lax.dot_general(a_ref[...], b_ref[...], _TN, preferred_element_type=F32).astype(BF16)
            dwt[pl.ds(pl.multiple_of(in_tile(table_ref, i) * tn, tn), tn), :] = tile

        @pl.when(i == forward_step)
        def _():
            rs_forward()

        @pl.when(i == steps - 1)
        def _():
            after_tiles()
            gwo_ref[...] = rs_finish()
            chip_finish()

    vmem = pl.BlockSpec(memory_space=pltpu.VMEM)
    grid_spec = pltpu.PrefetchScalarGridSpec(
        num_scalar_prefetch=1, grid=(steps,),
        in_specs=[pl.BlockSpec((SEQ, tn_out), lambda i, table_ref: (0, jnp.minimum(i, out_steps - 1))), vmem,
                  pl.BlockSpec((SEQ, tn), lambda i, table_ref: (0, in_tile(table_ref, i))), vmem],
        out_specs=(pl.BlockSpec(memory_space=pl.ANY), pl.BlockSpec((SHARD_OUT, D_MODEL), lambda i, table_ref: (0, 0))),
        scratch_shapes=[pltpu.VMEM((N_DEV, SHARD_OUT, D_MODEL), BF16),
                        pltpu.VMEM((D_PROJ, D_MODEL), BF16), pltpu.VMEM((4, SHARD_IN, D_MODEL), BF16),
                        pltpu.VMEM((2, HALF_IN, D_MODEL), BF16),
                        *_shard_sum_scratch(SHARD_OUT),
                        pltpu.SemaphoreType.DMA((N_SHARD_SUM_SEMS + N_CHIP_SUM_SEMS,)),
                        pltpu.SemaphoreType.DMA((N_SHARD_SUM_SEMS + N_CHIP_SUM_SEMS,)),
                        pltpu.SemaphoreType.DMA((7,))])
    return pl.pallas_call(
        body, name="dw", grid_spec=grid_spec,
        out_shape=(jax.ShapeDtypeStruct((3, SHARD_IN, D_MODEL), BF16), jax.ShapeDtypeStruct((SHARD_OUT, D_MODEL), F32)),
        compiler_params=_params(dimension_semantics=("arbitrary",)),
    )(table, mixed, dx2b, dproj, h)


def _adam_all(big_in, big_out, gsum, small, grad_x):
    steps = 4
    tr_in, tr_out = SHARD_IN // steps, SHARD_OUT // steps

    def body(*refs):
        ins, outs = refs[:8 + 1 + 18 + 1], refs[8 + 1 + 18 + 1:]
        i = pl.program_id(0)
        outs[33][...] = ins[27][...]
        for b in range(2):
            w_ref, g_ref, m_ref, v_ref = ins[4 * b:4 * b + 4]
            g = g_ref[...]
            delta, mn, vn = _adamw(w_ref[...], g, m_ref[...], v_ref[...])
            for ref, val in zip(outs[4 * b:4 * b + 4], (g, delta, mn, vn)):
                ref[...] = val

        @pl.when(i == 0)
        def _():
            gsum = ins[8][...]
            idx = _slot(lax.axis_index("x"), lax.axis_index("y"), lax.axis_index("c"))
            cg = jnp.zeros((3, SHARD_CONV), F32)
            for d in range(N_DEV):
                cg = jnp.where(idx == d, gsum[ROW_CONV0:ROW_CONV0 + 3, d * SHARD_CONV:(d + 1) * SHARD_CONV], cg)
            grads = (gsum[ROW_NORM_IN:ROW_NORM_IN + 1], gsum[ROW_SINKS:ROW_SINKS + 1, 0:N_Q_HEADS],
                     gsum[ROW_NORM_CONV:ROW_NORM_CONV + 1], gsum[ROW_NORM_ATTN:ROW_NORM_ATTN + 1],
                     gsum[ROW_NORM_FINAL:ROW_NORM_FINAL + 1], cg)
            for s, g in enumerate(grads):
                at = (slice(None), 0, slice(None)) if s == 5 else (slice(None), slice(None))
                w_ref, m_ref, v_ref = ins[9 + 3 * s:12 + 3 * s]
                delta, mn, vn = _adamw(w_ref[at], g, m_ref[at], v_ref[at])
                for ref, val in zip(outs[8 + 4 * s:12 + 4 * s], (g, delta, mn, vn)):
                    ref[at] = val
            outs[32][...] = gsum[ROW_SINKS:ROW_SINKS + 1, LOSS_LANE:LOSS_LANE + 1]

    const = lambda i: (0, 0)
    rows = lambda i: (i, 0)
    whole = lambda shape: pl.BlockSpec(shape, lambda i: (0,) * len(shape))
    small_shapes = [a.shape for a in small[::3]]
    in_specs = ([pl.BlockSpec((tr_in, D_MODEL), rows)] * 4 + [pl.BlockSpec((tr_out, D_MODEL), rows)] * 4
                + [pl.BlockSpec((8, D_MODEL), const)] + [whole(a.shape) for a in small]
                + [pl.BlockSpec((SEQ // steps, D_MODEL), rows)])
    out_specs = ([pl.BlockSpec((tr_in, D_MODEL), rows)] * 4 + [pl.BlockSpec((tr_out, D_MODEL), rows)] * 4
                 + [whole(s) for s in small_shapes for _ in range(4)] + [pl.BlockSpec((1, 1), const)]
                 + [pl.BlockSpec((SEQ // steps, D_MODEL), rows)])
    out_shape = ([jax.ShapeDtypeStruct((SHARD_IN, D_MODEL), F32)] * 4 + [jax.ShapeDtypeStruct((SHARD_OUT, D_MODEL), F32)] * 4
                 + [jax.ShapeDtypeStruct(s, F32) for s in small_shapes for _ in range(4)]
                 + [jax.ShapeDtypeStruct((1, 1), F32), jax.ShapeDtypeStruct((SEQ, D_MODEL), F32)])
    outs = pl.pallas_call(
        body, name="adam", grid=(steps,), in_specs=in_specs, out_specs=tuple(out_specs), out_shape=tuple(out_shape),
        compiler_params=_params(dimension_semantics=("arbitrary",)),
    )(*big_in, *big_out, gsum, *small, grad_x)
    return outs[0:4], outs[4:8], [outs[8 + 4 * s:12 + 4 * s] for s in range(6)], outs[32], outs[33]


def _rows_first(a):
    return jnp.transpose(a, (1, 0, 2))


def kernel(x, norm_in, w_in, conv_w, attn_sinks, norm_conv_out, norm_attn_out, w_out, norm_final, loss_target, m_norm_in, m_w_in, m_conv_w, m_attn_sinks, m_norm_conv_out, m_norm_attn_out, m_w_out, m_norm_final, v_norm_in, v_w_in, v_conv_w, v_attn_sinks, v_norm_conv_out, v_norm_attn_out, v_w_out, v_norm_final):
    x2d = x.reshape(SEQ, D_MODEL)
    target = loss_target.reshape(SEQ, D_MODEL)
    nf = norm_final.reshape(1, D_MODEL)

    w_in_t, m_w_in_t, v_w_in_t = w_in[0].T, m_w_in[0].T, v_w_in[0].T
    tiles = jnp.asarray(TILE_ORDER, jnp.int32).reshape(-1)
    w_in_full, h, proj, g_out, conv_full = _gather_in_proj(x2d, norm_in, w_in_t, w_out[0], _rows_first(conv_w), tiles)
    sinks = attn_sinks.reshape(N_Q_HEADS)

    mixed, attn, probs, shares = _mix_fwd(proj, conv_full, sinks, norm_conv_out, norm_attn_out)
    dx2, dx2b, dmixed, gnf, loss_part = _out_proj_loss(mixed, x2d, target, g_out.reshape(D_MIX, D_MODEL), nf)
    dproj, gslab = _mix_bwd(proj, dmixed, attn, probs, shares, conv_full, norm_conv_out, norm_attn_out)
    dw_in_chip, g_w_out = _dw_rs(mixed, dx2b, dproj, h, jnp.asarray(DW_TABLE, jnp.int32).reshape(-1))
    grad_x, g_w_in, gsum = _in_bwd_rs(dproj, w_in_full, x2d, dx2, norm_in, dw_in_chip, gslab, gnf, loss_part)

    small = (norm_in, m_norm_in, v_norm_in, attn_sinks, m_attn_sinks, v_attn_sinks,
             norm_conv_out, m_norm_conv_out, v_norm_conv_out, norm_attn_out, m_norm_attn_out, v_norm_attn_out,
             nf, m_norm_final.reshape(1, D_MODEL), v_norm_final.reshape(1, D_MODEL),
             _rows_first(conv_w), _rows_first(m_conv_w), _rows_first(v_conv_w))
    big_in, big_out, (s_ni, s_sk, s_nc, s_na, s_nf, s_cv), loss, grad_x = _adam_all(
        (w_in_t, g_w_in, m_w_in_t, v_w_in_t), (w_out[0], g_w_out, m_w_out[0], v_w_out[0]), gsum, small, grad_x)

    def leaves(k):
        return (s_ni[k], big_in[k].T[None], jnp.transpose(s_cv[k], (1, 0, 2)), s_sk[k], s_nc[k], s_na[k], big_out[k][None],
                s_nf[k].reshape(D_MODEL))

    return (loss.reshape(()), grad_x.reshape(1, SEQ, D_MODEL), *leaves(0), *leaves(1), *leaves(2), *leaves(3))
```

```python
import jax
import jax.numpy as jnp
from jax import lax
from jax.experimental import pallas as pl
from jax.experimental.pallas import tpu as pltpu

F32 = jnp.float32
BF16 = jnp.bfloat16
MESH = pl.DeviceIdType.MESH

N_DEV = 8
SEQ = 2048
D_MODEL = 1024
D_CONV = 1024
D_ATTN = 1024
D_KV = 128
HEAD_DIM = 64
N_Q_HEADS = 16
N_PAIRS = N_Q_HEADS // 2
PAIRS_PER_KV = N_PAIRS // 2
D_MIX = D_CONV + D_ATTN
D_PROJ = 6400
SHARD_IN = D_PROJ // N_DEV
SHARD_OUT = D_MIX // N_DEV
SHARD_CONV = D_CONV // N_DEV
OFF_CB, OFF_CC, OFF_CU, OFF_GC, OFF_Q, OFF_K, OFF_V, OFF_GA = 0, 1024, 2048, 3072, 4096, 5120, 5248, 5376
BLOCK = 128
N_BLOCKS = SEQ // BLOCK
HALO = 8
CHUNK = 16
N_CHUNKS = BLOCK // CHUNK
RMS_EPS = 1e-5
NEG = -1e30
SCALE = HEAD_DIM ** -0.5
SLOPES = tuple(2.0 ** (-8.0 * (h + 1) / N_Q_HEADS) for h in range(N_Q_HEADS))

ADAM_LR = 0.001
ADAM_B1 = 0.9
ADAM_B2 = 0.999
ADAM_EPS = 1e-08
ADAM_WD = 0.01
ADAM_STEP = 10

ROW_NORM_IN, ROW_NORM_CONV, ROW_NORM_ATTN, ROW_NORM_FINAL, ROW_CONV0, ROW_SINKS = 0, 1, 2, 3, 4, 7
LOSS_LANE = N_Q_HEADS
ACC_NORM_CONV, ACC_NORM_ATTN, ACC_CONV0, N_ACC = 0, 1, 2, 5

VMEM_LIMIT = 56 * 1024 * 1024

_NT = (((1,), (1,)), ((), ()))
_TN = (((0,), (0,)), ((), ()))


def _params(**kw):
    return pltpu.CompilerParams(vmem_limit_bytes=VMEM_LIMIT, **kw)


def _adamw(w, g, m, v):
    m = ADAM_B1 * m + (1.0 - ADAM_B1) * g
    v = ADAM_B2 * v + (1.0 - ADAM_B2) * (g * g)
    m_hat = m / (1.0 - ADAM_B1 ** ADAM_STEP)
    v_hat = v / (1.0 - ADAM_B2 ** ADAM_STEP)
    delta = -ADAM_LR * (m_hat / (jnp.sqrt(v_hat) + ADAM_EPS) + ADAM_WD * w)
    return delta, m, v


def _sigmoid(t):
    return 1.0 / (1.0 + jnp.exp(-t))


def _slot(px, py, pc):
    return 4 * px + 2 * py + pc


OWN, NX, NY, DG = range(4)
HALF_IN = SHARD_IN // 2
N_GATHER_KINDS = 13
W_OUT_KINDS = N_GATHER_KINDS + 7


IN_PROJ_TILE = 640
TILE_ORDER = ((0, 1, 2, 3, 4, 5, 6, 7, 8, 9), (3, 4, 0, 1, 2, 8, 9, 5, 6, 7),
              (5, 6, 0, 1, 7, 8, 9, 2, 3, 4), (8, 9, 3, 4, 5, 6, 7, 0, 1, 2))
TILES_OWN, TILES_NEIGHBOURS = 2, 7


def _tile(table_ref, p):
    chip = 2 * lax.axis_index("x") + lax.axis_index("y")
    return table_ref[chip * len(TILE_ORDER[0]) + p]


DW_TILE_ORDER = tuple(tuple(reversed(row)) for row in TILE_ORDER)


def _tiles_until_complete(chip, owner):
    lo, hi = owner * 2 * SHARD_IN, (owner + 1) * 2 * SHARD_IN
    touching = [t for t in range(len(TILE_ORDER[0])) if t * IN_PROJ_TILE < hi and (t + 1) * IN_PROJ_TILE > lo]
    return 1 + max(DW_TILE_ORDER[chip].index(t) for t in touching)


DW_TABLE = tuple(DW_TILE_ORDER[chip] + tuple(_tiles_until_complete(chip, chip ^ flip) for flip in (0, 2, 1, 3))
                 for chip in range(4))


def _dw_entry(table_ref, p):
    chip = 2 * lax.axis_index("x") + lax.axis_index("y")
    return table_ref[chip * len(DW_TABLE[0]) + p]


def _gather_in_proj(x, norm_in, w_in_sh, w_out_sh, conv_sh, tiles):
    tn = IN_PROJ_TILE
    steps = D_PROJ // tn
    tm = 256

    def body(tiles_ref, x_hbm, g_ref, win_ref, wout_ref, cv_ref, wt_ref, h_ref, proj_ref, gout_ref, conv_ref,
             gin_ref, gcv_ref, wob_ref, x_ref, send_sems, recv_sems, local_sems):
        p = pl.program_id(0)
        local_sem = local_sems.at[0]
        x, y, c = lax.axis_index("x"), lax.axis_index("y"), lax.axis_index("c")
        me, sibling = (x, y, c), (x, y, 1 - c)
        nx, ny, dg = (1 - x, y, c), (x, 1 - y, c), (1 - x, 1 - y, c)

        def other(dev):
            return (dev[0], dev[1], 1 - dev[2])

        def shard(dev):
            return gin_ref.at[pl.ds(pl.multiple_of(_slot(*dev) * SHARD_IN, 16), SHARD_IN), :]

        def half(dev, h):
            return gin_ref.at[pl.ds(pl.multiple_of(_slot(*dev) * SHARD_IN + h * HALF_IN, 16), HALF_IN), :]

        def rc(ref, k, to):
            return pltpu.make_async_remote_copy(src_ref=ref, dst_ref=ref, send_sem=send_sems.at[k],
                                                recv_sem=recv_sems.at[k], device_id=to, device_id_type=MESH)

        def cv(k, dev, to):
            s = _slot(*dev)
            return pltpu.make_async_remote_copy(src_ref=gcv_ref.at[s], dst_ref=gcv_ref.at[s],
                                                send_sem=send_sems.at[N_GATHER_KINDS + k],
                                                recv_sem=recv_sems.at[N_GATHER_KINDS + k], device_id=to, device_id_type=MESH)

        def own_copies():
            return [rc(shard(me), 0, sibling),
                    rc(half(me, 0), 1, nx), rc(half(me, 1), 2, nx),
                    rc(half(me, 1), 4, ny), rc(half(me, 0), 3, ny),
                    cv(0, me, sibling)] + [cv(1 + j, me, peer) for j, peer in enumerate((nx, ny, dg))]

        def pass_on(dev, h, k_in, k_ici, k_d2d, half=half, base=0):
            rc(half(dev, h), base + k_in, me).wait_recv()
            if k_ici is not None:
                rc(half(dev, h), base + k_ici, ny if dev is nx else nx).start()
            rc(half(dev, h), base + k_d2d, sibling).start()

        def out_half(dev, h):
            return gout_ref.at[_slot(*dev), pl.ds(h * (SHARD_OUT // 2), SHARD_OUT // 2), :]

        def own_out_copies():
            src = lambda h: wob_ref.at[pl.ds(h * (SHARD_OUT // 2), SHARD_OUT // 2), :]

            def send(ref, dst, k, to):
                return pltpu.make_async_remote_copy(src_ref=ref, dst_ref=dst, send_sem=send_sems.at[W_OUT_KINDS + k],
                                                    recv_sem=recv_sems.at[W_OUT_KINDS + k], device_id=to, device_id_type=MESH)

            return [send(wob_ref, gout_ref.at[_slot(*me)], 0, sibling),
                    send(src(0), out_half(me, 0), 1, nx), send(src(1), out_half(me, 1), 2, nx),
                    send(src(1), out_half(me, 1), 4, ny), send(src(0), out_half(me, 0), 3, ny)]

        def own_out_local():
            return pltpu.make_async_copy(wob_ref, gout_ref.at[_slot(*me)], local_sems.at[1])

        @pl.when(p == 0)
        def _():
            gin_ref[pl.ds(pl.multiple_of(_slot(*me) * SHARD_IN, 16), SHARD_IN), :] = win_ref[...].astype(BF16)
            gcv_ref[_slot(*me)] = jnp.zeros((8, SHARD_CONV), F32)
            gcv_ref[_slot(*me), 0:3, :] = cv_ref[:, 0, :]
            for cp in own_copies():
                cp.start()
            wob_ref[...] = wout_ref[...].astype(BF16)
            x_load = pltpu.make_async_copy(x_hbm, x_ref, local_sems.at[2])
            x_load.start()
            x_load.wait()
            for t in range(SEQ // tm):
                xv = x_ref[tm * t:tm * (t + 1), :]
                r = lax.rsqrt(jnp.mean(xv * xv, axis=-1, keepdims=True) + RMS_EPS)
                h_ref[tm * t:tm * (t + 1), :] = (xv * r * g_ref[...]).astype(BF16)
            rc(shard(sibling), 0, me).wait_recv()

        @pl.when(p == TILES_OWN)
        def _():
            for args in ((nx, 0, 1, 5, 7), (ny, 1, 4, 6, 10), (nx, 1, 2, None, 8), (ny, 0, 3, None, 9)):
                pass_on(*args)
            for j, peer in enumerate((nx, ny, dg)):
                cv(1 + j, peer, me).wait_recv()
                cv(4 + j, peer, sibling).start()
            for (dev, h), k in (((nx, 0), 7), ((nx, 1), 8), ((ny, 0), 9), ((ny, 1), 10)):
                rc(half(other(dev), h), k, me).wait_recv()
            own_out_local().start()
            for cp in own_out_copies():
                cp.start()

        @pl.when(p == TILES_NEIGHBOURS - 1)
        def _():
            pass_on(dg, 0, 5, None, 11)
            pass_on(dg, 1, 6, None, 12)

        @pl.when(p == TILES_NEIGHBOURS)
        def _():
            for (dev, h), k in (((dg, 0), 11), ((dg, 1), 12)):
                rc(half(other(dev), h), k, me).wait_recv()
            pltpu.make_async_copy(gin_ref, wt_ref, local_sem).start()

        @pl.when(p == steps - 2)
        def _():
            for args in ((nx, 0, 1, 5, 7), (ny, 1, 4, 6, 10), (nx, 1, 2, None, 8), (ny, 0, 3, None, 9)):
                pass_on(*args, half=out_half, base=W_OUT_KINDS)

        w = gin_ref[pl.ds(pl.multiple_of(_tile(tiles_ref, p) * tn, tn), tn), :]
        proj_ref[...] = lax.dot_general(h_ref[...], w, _NT, preferred_element_type=F32)

        @pl.when(p == steps - 1)
        def _():
            cv(0, sibling, me).wait_recv()
            for j, peer in enumerate((nx, ny, dg)):
                cv(4 + j, other(peer), me).wait_recv()
            for d in range(N_DEV):
                conv_ref[:, d * SHARD_CONV:(d + 1) * SHARD_CONV] = gcv_ref[d]
            relayed = [rc(half(nx, 0), 5, ny), rc(half(ny, 1), 6, nx)]
            relayed += [rc(half(dev, h), k, sibling) for (dev, h), k in
                        (((nx, 0), 7), ((nx, 1), 8), ((ny, 0), 9), ((ny, 1), 10), ((dg, 0), 11), ((dg, 1), 12))]
            relayed += [cv(4 + j, peer, sibling) for j, peer in enumerate((nx, ny, dg))]
            for cp in own_copies() + relayed:
                cp.wait_send()
            pltpu.make_async_copy(gin_ref, wt_ref, local_sem).wait()
            pass_on(dg, 0, 5, None, 11, half=out_half, base=W_OUT_KINDS)
            pass_on(dg, 1, 6, None, 12, half=out_half, base=W_OUT_KINDS)
            rc(gout_ref.at[_slot(*sibling)], W_OUT_KINDS, me).wait_recv()
            out_relayed = [rc(out_half(nx, 0), W_OUT_KINDS + 5, ny), rc(out_half(ny, 1), W_OUT_KINDS + 6, nx)]
            for (dev, h), k in (((nx, 0), 7), ((nx, 1), 8), ((ny, 0), 9), ((ny, 1), 10), ((dg, 0), 11), ((dg, 1), 12)):
                rc(out_half(other(dev), h), W_OUT_KINDS + k, me).wait_recv()
                out_relayed.append(rc(out_half(dev, h), W_OUT_KINDS + k, sibling))
            for cp in own_out_copies() + out_relayed:
                cp.wait_send()
            own_out_local().wait()

    vmem = pl.BlockSpec(memory_space=pltpu.VMEM)
    grid_spec = pltpu.PrefetchScalarGridSpec(
        num_scalar_prefetch=1, grid=(steps,),
        in_specs=[pl.BlockSpec(memory_space=pl.ANY), vmem, vmem, vmem, vmem],
        out_specs=(pl.BlockSpec(memory_space=pl.ANY), vmem,
                   pl.BlockSpec((SEQ, tn), lambda p, tiles_ref: (0, _tile(tiles_ref, p))),
                   pl.BlockSpec(memory_space=pl.ANY), vmem),
        scratch_shapes=[pltpu.VMEM((D_PROJ, D_MODEL), BF16), pltpu.VMEM((N_DEV, 8, SHARD_CONV), F32),
                        pltpu.VMEM((SHARD_OUT, D_MODEL), BF16), pltpu.VMEM((SEQ, D_MODEL), F32),
                        pltpu.SemaphoreType.DMA((W_OUT_KINDS + N_GATHER_KINDS,)),
                        pltpu.SemaphoreType.DMA((W_OUT_KINDS + N_GATHER_KINDS,)),
                        pltpu.SemaphoreType.DMA((3,))])
    return pl.pallas_call(
        body, name="gather_in_proj", grid_spec=grid_spec,
        out_shape=(jax.ShapeDtypeStruct((D_PROJ, D_MODEL), BF16), jax.ShapeDtypeStruct((SEQ, D_MODEL), BF16),
                   jax.ShapeDtypeStruct((SEQ, D_PROJ), F32), jax.ShapeDtypeStruct((N_DEV, SHARD_OUT, D_MODEL), BF16),
                   jax.ShapeDtypeStruct((8, D_CONV), F32)),
        compiler_params=_params(dimension_semantics=("arbitrary",)),
    )(tiles, x, norm_in, w_in_sh, w_out_sh, conv_sh)


def _shard_sum(src, own, d2d, ici, send_sems, recv_sems, local_sems, base=0):
    x, y, c = lax.axis_index("x"), lax.axis_index("y"), lax.axis_index("c")
    sibling = (x, y, 1 - c)
    chips = [(x, y), (1 - x, y), (x, 1 - y), (1 - x, 1 - y)]

    def rcopy(s, d, k, to):
        return pltpu.make_async_remote_copy(src_ref=s, dst_ref=d, send_sem=send_sems.at[base + k],
                                            recv_sem=recv_sems.at[base + k], device_id=to, device_id_type=MESH)

    def mine(k):
        return pltpu.make_async_copy(src.at[_slot(*chips[k], c)], own.at[k], local_sems.at[k])

    def to_sibling(k):
        return rcopy(src.at[_slot(*chips[k], 1 - c)], d2d.at[k], k, sibling)

    def to_chip(k):
        return rcopy(own.at[k], ici.at[k - 1], 3 + k, (*chips[k], c))

    def start(k):
        mine(k).start()
        to_sibling(k).start()

    def forward(k):
        mine(k).wait()
        to_sibling(k).wait_recv()
        own[k] = (own[k].astype(F32) + d2d[k].astype(F32)).astype(BF16)
        to_chip(k).start()

    def finish():
        mine(0).wait()
        to_sibling(0).wait_recv()
        acc = own[0].astype(F32) + d2d[0].astype(F32)
        for k in range(1, 4):
            to_chip(k).wait_recv()
            acc = acc + ici[k - 1].astype(F32)
        for k in range(4):
            to_sibling(k).wait_send()
        for k in range(1, 4):
            to_chip(k).wait_send()
        return acc

    return start, forward, finish


def _shard_sum_scratch(rows):
    return [pltpu.VMEM((4, rows, D_MODEL), BF16), pltpu.VMEM((4, rows, D_MODEL), BF16),
            pltpu.VMEM((3, rows, D_MODEL), BF16)]


N_SHARD_SUM_SEMS = 7


N_CHIP_SUM_SEMS = 6


def _chip_sum(dwt, d2d, via, out_hbm, tiles_until, send_sems, recv_sems, local_sems, base, local_base):
    x, y, c = lax.axis_index("x"), lax.axis_index("y"), lax.axis_index("c")
    sibling, nx, ny = (x, y, 1 - c), (1 - x, y, c), (x, 1 - y, c)
    chips = [(x, y), (1 - x, y), (x, 1 - y), (1 - x, 1 - y)]

    def shard(s):
        return dwt.at[pl.ds(pl.multiple_of(s * SHARD_IN, 16), SHARD_IN), :]

    def half(ref, h):
        return ref.at[pl.ds(h * HALF_IN, HALF_IN), :]

    def rc(s, d, k, to):
        return pltpu.make_async_remote_copy(src_ref=s, dst_ref=d, send_sem=send_sems.at[base + k],
                                            recv_sem=recv_sems.at[base + k], device_id=to, device_id_type=MESH)

    def to_sibling(k):
        return rc(shard(_slot(*chips[k], 1 - c)), d2d.at[k], k, sibling)

    for_dg = (lambda: rc(half(d2d.at[DG], 0), via.at[0], 4, nx), lambda: rc(half(d2d.at[DG], 1), via.at[1], 5, ny))

    def save(k):
        return pltpu.make_async_copy(d2d.at[k], out_hbm.at[k], local_sems.at[local_base + k])

    def chip_sum(k):
        to_sibling(k).wait_recv()
        d2d[k] = (shard(_slot(*chips[k], c))[...].astype(F32) + d2d[k].astype(F32)).astype(BF16)

    def before_tile(n):
        for k in (NX, NY, DG):
            @pl.when(tiles_until(k) == n)
            def _():
                to_sibling(k).start()

            @pl.when(tiles_until(k) + 1 == n)
            def _():
                chip_sum(k)
                if k == DG:
                    for cp in for_dg:
                        cp().start()

    def after_tiles():
        to_sibling(OWN).start()

    def finish():
        for k, h in ((NY, 0), (NX, 1)):
            for_dg[h]().wait_recv()
            rows = pl.ds(h * HALF_IN, HALF_IN)
            d2d[k, rows, :] = (d2d[k, rows, :].astype(F32) + via[h].astype(F32)).astype(BF16)
            save(k).start()
        chip_sum(OWN)
        save(OWN).start()
        for k in (OWN, NX, NY):
            save(k).wait()
        for k in range(4):
            to_sibling(k).wait_send()
        for cp in for_dg:
            cp().wait_send()

    return before_tile, after_tiles, finish


N_ICI_SUM_SEMS = 2


def _ici_sum(src, own, ici, send_sems, recv_sems, local_sems, base=0):
    x, y, c = lax.axis_index("x"), lax.axis_index("y"), lax.axis_index("c")

    def to_chip(k, to):
        return pltpu.make_async_remote_copy(src_ref=src.at[k], dst_ref=ici.at[k - 1], send_sem=send_sems.at[base + k - 1],
                                            recv_sem=recv_sems.at[base + k - 1], device_id=to, device_id_type=MESH)

    copies = (lambda: to_chip(NX, (1 - x, y, c)), lambda: to_chip(NY, (x, 1 - y, c)))
    mine = lambda: pltpu.make_async_copy(src.at[OWN], own, local_sems.at[0])

    def start():
        for cp in copies + (mine,):
            cp().start()

    def finish():
        mine().wait()
        for cp in copies:
            cp().wait_recv()
        acc = own[...].astype(F32) + ici[0].astype(F32) + ici[1].astype(F32)
        for cp in copies:
            cp().wait_send()
        return acc

    return start, finish


def _slab_sum(myslab, slabs, send_sems, recv_sems, base):
    x, y, c = lax.axis_index("x"), lax.axis_index("y"), lax.axis_index("c")
    me = _slot(x, y, c)
    peers = [(x, y, 1 - c), (1 - x, y, c), (x, 1 - y, c), (1 - x, 1 - y, c),
             (1 - x, y, 1 - c), (x, 1 - y, 1 - c), (1 - x, 1 - y, 1 - c)]

    def cp(k):
        return pltpu.make_async_remote_copy(src_ref=myslab, dst_ref=slabs.at[me], send_sem=send_sems.at[base + k],
                                            recv_sem=recv_sems.at[base + k], device_id=peers[k], device_id_type=MESH)

    def start():
        slabs[me] = myslab[...]
        for k in range(7):
            cp(k).start()

    def finish():
        for k in range(7):
            cp(k).wait_recv()
        total = slabs[0]
        for d in range(1, N_DEV):
            total = total + slabs[d]
        for k in range(7):
            cp(k).wait_send()
        return total

    return start, finish


def _chunk_rows(r):
    return slice(r * CHUNK, (r + 1) * CHUNK)


def _conv_halo(cch_ref, cuh_ref, n):
    zh = jnp.where(n > 0, cch_ref[...] * cuh_ref[...], 0.0)
    return jnp.concatenate([zh] * (CHUNK // HALO), axis=0)


def _conv_chunk(pj_ref, zhalo, cw, r):
    rows = _chunk_rows(r)
    cc = pj_ref[rows, OFF_CC:OFF_CC + D_CONV]
    cu = pj_ref[rows, OFF_CU:OFF_CU + D_CONV]
    z = cc * cu
    before = _chunk_rows(r - 1)
    zprev = pj_ref[before, OFF_CC:OFF_CC + D_CONV] * pj_ref[before, OFF_CU:OFF_CU + D_CONV] if r > 0 else zhalo
    row = lax.broadcasted_iota(jnp.int32, (CHUNK, D_CONV), 0)
    z1 = jnp.where(row < 1, pltpu.roll(zprev, 1, 0), pltpu.roll(z, 1, 0))
    z2 = jnp.where(row < 2, pltpu.roll(zprev, 2, 0), pltpu.roll(z, 2, 0))
    co = cw[0] * z2 + cw[1] * z1 + cw[2] * z
    return cc, cu, z, z1, z2, co


def _gated_norm(a, gain, t):
    r = lax.rsqrt(jnp.mean(a * a, axis=-1, keepdims=True) + RMS_EPS)
    return a * r * gain * (t * _sigmoid(t))


def _kv_bands(pj, kvp_ref):
    lane = lax.broadcasted_iota(jnp.int32, (2 * BLOCK, D_KV), 1)
    lo = lane < HEAD_DIM

    def bands(prev, cur):
        b = jnp.concatenate([prev, cur], axis=0)
        br = pltpu.roll(b, HEAD_DIM, 1)
        zero = jnp.zeros_like(b)
        return ((jnp.where(lo, b, zero).astype(BF16), jnp.where(lo, zero, br).astype(BF16)),
                (jnp.where(lo, br, zero).astype(BF16), jnp.where(lo, zero, b).astype(BF16)))

    ks = bands(kvp_ref[:, 0:D_KV], pj[:, OFF_K:OFF_K + D_KV])
    vs = bands(kvp_ref[:, D_KV:2 * D_KV], pj[:, OFF_V:OFF_V + D_KV])
    return ks, vs


STACK = PAIRS_PER_KV * BLOCK


def _head(j, i, e):
    return 2 * (PAIRS_PER_KV * j + i) + e


def _pair_cols(j, i, off):
    p = PAIRS_PER_KV * j + i
    return slice(off + 128 * p, off + 128 * (p + 1))


def _fill_attn_bias(bias_scr, first_block):
    qi = lax.broadcasted_iota(jnp.int32, (BLOCK, 2 * BLOCK), 0)
    kj = lax.broadcasted_iota(jnp.int32, (BLOCK, 2 * BLOCK), 1)
    dist = BLOCK + qi - kj
    valid = (dist >= 0) & (dist < BLOCK)
    if first_block:
        valid = valid & (kj >= BLOCK)
    distf = dist.astype(F32)
    for j in range(2):
        for e in range(2):
            for i in range(PAIRS_PER_KV):
                bias_scr[2 * j + e, BLOCK * i:BLOCK * (i + 1), :] = jnp.where(valid, -SLOPES[_head(j, i, e)] * distf, NEG)


def _q_stack(pj, j):
    return jnp.concatenate([(pj[:, _pair_cols(j, i, OFF_Q)] * SCALE).astype(BF16) for i in range(PAIRS_PER_KV)], axis=0)


def _attn_probs(q_stack, kband, bias_ref, sinks):
    s = lax.dot_general(q_stack, kband, _NT, preferred_element_type=F32)
    ones = jnp.ones((128, 128), BF16)
    probs, shares = [], []
    for i, sink in enumerate(sinks):
        rows = slice(BLOCK * i, BLOCK * (i + 1))
        t = s[rows, :] + bias_ref[rows, :]
        m = jnp.broadcast_to(jnp.max(t, axis=-1, keepdims=True), (BLOCK, 128))
        m = jnp.maximum(m, sink)
        p = [jnp.exp(t[:, :128] - m), jnp.exp(t[:, 128:] - m)]
        es = jnp.exp(sink - m)
        total = (jnp.dot(p[0].astype(BF16), ones, preferred_element_type=F32)
                 + jnp.dot(p[1].astype(BF16), ones, preferred_element_type=F32))
        inv = 1.0 / (total + es)
        probs.append(jnp.concatenate([p[0] * inv, p[1] * inv], axis=1))
        shares.append(es * inv)
    return jnp.concatenate(probs, axis=0), jnp.concatenate(shares, axis=0)


def _attn_group(pj, ks, vs, bias_scr, sink_ref, j):
    q_stack = _q_stack(pj, j)
    out, probs, shares = None, [], []
    for e in range(2):
        p, ps = _attn_probs(q_stack, ks[j][e], bias_scr.at[2 * j + e],
                            [sink_ref[_head(j, i, e)] for i in range(PAIRS_PER_KV)])
        p = p.astype(BF16)
        o = jnp.dot(p, vs[j][e], preferred_element_type=F32)
        out = o if out is None else out + o
        probs.append(p)
        shares.append(ps)
    return out, probs, shares


def _mix_fwd(proj, conv_full, sinks, norm_conv, norm_attn):
    def body(pj_ref, kvp_ref, cch_ref, cuh_ref, cw_ref, sink_ref, gc_ref, ga_ref,
             mixed_ref, attn_scr, p_ref, ps_ref, bias_scr):
        n = pl.program_id(0)
        pj = pj_ref

        @pl.when(n == 0)
        def _():
            _fill_attn_bias(bias_scr, first_block=True)

        @pl.when(n == 1)
        def _():
            _fill_attn_bias(bias_scr, first_block=False)

        zhalo = _conv_halo(cch_ref, cuh_ref, n)
        cw = (cw_ref[0:1, :], cw_ref[1:2, :], cw_ref[2:3, :])
        gain_c = gc_ref[...]

        for r in range(N_CHUNKS):
            rows = _chunk_rows(r)
            co = _conv_chunk(pj_ref, zhalo, cw, r)[-1]
            y = _gated_norm(pj_ref[rows, OFF_CB:OFF_CB + D_CONV] * co, gain_c, pj_ref[rows, OFF_GC:OFF_GC + D_CONV])
            mixed_ref[rows, 0:D_CONV] = y.astype(BF16)

        ks, vs = _kv_bands(pj, kvp_ref)
        for j in range(2):
            out, probs, shares = _attn_group(pj, ks, vs, bias_scr, sink_ref, j)
            for e in range(2):
                p_ref[0, 2 * j + e] = probs[e]
                ps_ref[0, 2 * j + e] = shares[e]
            for i in range(PAIRS_PER_KV):
                attn_scr[:, _pair_cols(j, i, 0)] = out[BLOCK * i:BLOCK * (i + 1), :]
        gain_a = ga_ref[...]

        for r in range(N_CHUNKS):
            rows = _chunk_rows(r)
            y = _gated_norm(attn_scr[rows, :], gain_a, pj_ref[rows, OFF_GA:OFF_GA + D_ATTN])
            mixed_ref[rows, D_CONV:D_MIX] = y.astype(BF16)

    per_block = BLOCK // HALO
    return pl.pallas_call(
        body, name="mix_fwd", grid=(N_BLOCKS,),
        in_specs=[
            pl.BlockSpec((BLOCK, D_PROJ), lambda n: (n, 0)),
            pl.BlockSpec((BLOCK, 2 * D_KV), lambda n: (jnp.maximum(n - 1, 0), OFF_K // (2 * D_KV))),
            pl.BlockSpec((HALO, D_CONV), lambda n: (jnp.maximum(n * per_block - 1, 0), OFF_CC // D_CONV)),
            pl.BlockSpec((HALO, D_CONV), lambda n: (jnp.maximum(n * per_block - 1, 0), OFF_CU // D_CONV)),
            pl.BlockSpec((8, D_CONV), lambda n: (0, 0)),
            pl.BlockSpec(memory_space=pltpu.SMEM),
            pl.BlockSpec((1, D_CONV), lambda n: (0, 0)),
            pl.BlockSpec((1, D_ATTN), lambda n: (0, 0)),
        ],
        out_specs=(pl.BlockSpec((BLOCK, D_MIX), lambda n: (n, 0)), pl.BlockSpec((BLOCK, D_ATTN), lambda n: (n, 0)),
                   pl.BlockSpec((1, 4, STACK, 2 * BLOCK), lambda n: (n, 0, 0, 0)),
                   pl.BlockSpec((1, 4, STACK, 128), lambda n: (n, 0, 0, 0))),
        out_shape=(jax.ShapeDtypeStruct((SEQ, D_MIX), BF16), jax.ShapeDtypeStruct((SEQ, D_ATTN), F32),
                   jax.ShapeDtypeStruct((N_BLOCKS, 4, STACK, 2 * BLOCK), BF16),
                   jax.ShapeDtypeStruct((N_BLOCKS, 4, STACK, 128), F32)),
        scratch_shapes=[pltpu.VMEM((4, STACK, 2 * BLOCK), F32)],
        compiler_params=_params(dimension_semantics=("arbitrary",)),
    )(proj, proj, proj, proj, conv_full, sinks, norm_conv, norm_attn)


def _out_proj_loss(mixed, x, target, w_out_full, norm_final):
    tm = 256

    def body(mx_ref, x_ref, t_ref, w_ref, g_ref, dx2_ref, dx2b_ref, dmix_ref, gnf_ref, loss_ref):
        i = pl.program_id(0)
        w = w_ref[...]
        x2 = x_ref[...] + jnp.dot(mx_ref[...], w, preferred_element_type=F32)
        r = lax.rsqrt(jnp.mean(x2 * x2, axis=-1, keepdims=True) + RMS_EPS)
        xn = x2 * r
        g = g_ref[...]
        err = xn * g - t_ref[...]
        part = 0.5 * jnp.sum(jnp.mean(err * err, axis=-1, keepdims=True), axis=0, keepdims=True)
        dy = err * (1.0 / D_MODEL)
        gnf = jnp.sum(dy * xn, axis=0, keepdims=True)
        u = dy * g
        dx2 = r * (u - xn * jnp.mean(u * xn, axis=-1, keepdims=True))
        dx2_ref[...] = dx2
        dx2b = dx2.astype(BF16)
        dx2b_ref[...] = dx2b
        dmix_ref[...] = lax.dot_general(dx2b, w, _NT, preferred_element_type=F32)

        @pl.when(i == 0)
        def _():
            gnf_ref[...] = jnp.zeros_like(gnf_ref)
            loss_ref[...] = jnp.zeros_like(loss_ref)

        gnf_ref[...] += gnf
        loss_ref[...] += jnp.broadcast_to(part, loss_ref.shape)

    return pl.pallas_call(
        body, name="out_proj_loss", grid=(SEQ // tm,),
        in_specs=[pl.BlockSpec((tm, D_MIX), lambda i: (i, 0)), pl.BlockSpec((tm, D_MODEL), lambda i: (i, 0)),
                  pl.BlockSpec((tm, D_MODEL), lambda i: (i, 0)), pl.BlockSpec(memory_space=pltpu.VMEM),
                  pl.BlockSpec((1, D_MODEL), lambda i: (0, 0))],
        out_specs=(pl.BlockSpec((tm, D_MODEL), lambda i: (i, 0)), pl.BlockSpec((tm, D_MODEL), lambda i: (i, 0)),
                   pl.BlockSpec((tm, D_MIX), lambda i: (i, 0)),
                   pl.BlockSpec((1, D_MODEL), lambda i: (0, 0)), pl.BlockSpec((8, 128), lambda i: (0, 0))),
        out_shape=(jax.ShapeDtypeStruct((SEQ, D_MODEL), F32), jax.ShapeDtypeStruct((SEQ, D_MODEL), BF16),
                   jax.ShapeDtypeStruct((SEQ, D_MIX), F32),
                   jax.ShapeDtypeStruct((1, D_MODEL), F32), jax.ShapeDtypeStruct((8, 128), F32)),
        compiler_params=_params(dimension_semantics=("arbitrary",)),
    )(mixed, x, target, w_out_full, norm_final)


def _gated_norm_bwd(a, gain, t, dy):
    r = lax.rsqrt(jnp.mean(a * a, axis=-1, keepdims=True) + RMS_EPS)
    an = a * r
    sg = _sigmoid(t)
    dn = dy * (t * sg)
    dt = dy * (an * gain) * (sg * (1.0 + t * (1.0 - sg)))
    u = dn * gain
    da = r * (u - an * jnp.mean(u * an, axis=-1, keepdims=True))
    return da, dt, dn * an


def _mix_bwd(proj, dmixed, attn, probs, shares, conv_full, norm_conv, norm_attn):
    def body(pj_ref, kvp_ref, cch_ref, cuh_ref, dmx_ref, attn_ref, p_ref, ps_ref, cw_ref, gc_ref, ga_ref,
             dpj_ref, gslab_ref, dattn_scr, nxt_scr, dkv_scr, acc_scr):
        step = pl.program_id(0)
        n = N_BLOCKS - 1 - step
        pj = pj_ref

        @pl.when(step == 0)
        def _():
            gslab_ref[...] = jnp.zeros_like(gslab_ref)
            nxt_scr[...] = jnp.zeros_like(nxt_scr)
            dkv_scr[...] = jnp.zeros_like(dkv_scr)
            acc_scr[...] = jnp.zeros_like(acc_scr)

        zhalo = _conv_halo(cch_ref, cuh_ref, n)
        cw = (cw_ref[0:1, :], cw_ref[1:2, :], cw_ref[2:3, :])
        gain_c = gc_ref[...]
        row = lax.broadcasted_iota(jnp.int32, (CHUNK, D_CONV), 0)

        dco_after = nxt_scr[...]
        for r in reversed(range(N_CHUNKS)):
            rows = _chunk_rows(r)
            cc, cu, z, z1, z2, co = _conv_chunk(pj_ref, zhalo, cw, r)
            cb = pj_ref[rows, OFF_CB:OFF_CB + D_CONV]
            da, dgate, gterm = _gated_norm_bwd(cb * co, gain_c, pj_ref[rows, OFF_GC:OFF_GC + D_CONV],
                                               dmx_ref[rows, 0:D_CONV])
            dpj_ref[rows, OFF_GC:OFF_GC + D_CONV] = dgate.astype(BF16)
            dpj_ref[rows, OFF_CB:OFF_CB + D_CONV] = (da * co).astype(BF16)
            dco = da * cb
            dco1 = jnp.where(row >= CHUNK - 1, pltpu.roll(dco_after, CHUNK - 1, 0), pltpu.roll(dco, CHUNK - 1, 0))
            dco2 = jnp.where(row >= CHUNK - 2, pltpu.roll(dco_after, CHUNK - 2, 0), pltpu.roll(dco, CHUNK - 2, 0))
            dz = cw[2] * dco + cw[1] * dco1 + cw[0] * dco2
            dpj_ref[rows, OFF_CC:OFF_CC + D_CONV] = (dz * cu).astype(BF16)
            dpj_ref[rows, OFF_CU:OFF_CU + D_CONV] = (dz * cc).astype(BF16)
            acc_scr[ACC_NORM_CONV] += gterm
            acc_scr[ACC_CONV0] += dco * z2
            acc_scr[ACC_CONV0 + 1] += dco * z1
            acc_scr[ACC_CONV0 + 2] += dco * z
            dco_after = dco
        nxt_scr[...] = dco_after

        ks, vs = _kv_bands(pj, kvp_ref)
        gain_a = ga_ref[...]

        for r in range(N_CHUNKS):
            rows = _chunk_rows(r)
            da, dgate, gterm = _gated_norm_bwd(attn_ref[rows, :], gain_a, pj_ref[rows, OFF_GA:OFF_GA + D_ATTN],
                                               dmx_ref[rows, D_CONV:D_MIX])
            dpj_ref[rows, OFF_GA:OFF_GA + D_ATTN] = dgate.astype(BF16)
            dattn_scr[rows, :] = da
            acc_scr[ACC_NORM_ATTN] += gterm

        in_lo = lax.broadcasted_iota(jnp.int32, (128, 128), 0) < HEAD_DIM
        half_ones = (jnp.where(in_lo, 1.0, 0.0).astype(BF16), jnp.where(in_lo, 0.0, 1.0).astype(BF16))
        lane_s = lax.broadcasted_iota(jnp.int32, (1, D_MODEL), 1)
        gsink = jnp.zeros((1, D_MODEL), F32)
        dk_t, dv_t = [], []
        for j in range(2):
            q_stack = _q_stack(pj, j)
            do_f = jnp.concatenate([dattn_scr[:, _pair_cols(j, i, 0)] for i in range(PAIRS_PER_KV)], axis=0)
            o_f = jnp.concatenate([attn_ref[:, _pair_cols(j, i, 0)] for i in range(PAIRS_PER_KV)], axis=0)
            prod = (do_f * o_f).astype(BF16)
            deltas = [jnp.dot(prod, half_ones[e], preferred_element_type=F32) for e in range(2)]
            do_b = do_f.astype(BF16)
            q_t, do_t = q_stack.T, do_b.T
            dq, dk_j, dv_j = None, None, None
            for e in range(2):
                p = p_ref[0, 2 * j + e]
                dp = lax.dot_general(do_b, vs[j][e], _NT, preferred_element_type=F32)
                ds = []
                for i in range(PAIRS_PER_KV):
                    rows = slice(BLOCK * i, BLOCK * (i + 1))
                    delta = deltas[e][rows, :]
                    ds.append((p[rows, :].astype(F32) * (dp[rows, :] - jnp.concatenate([delta, delta], axis=1))).astype(BF16))
                    gs_h = -jnp.sum(ps_ref[0, 2 * j + e, rows, 0:1] * delta[:, 0:1], axis=0, keepdims=True)
                    gsink = gsink + jnp.where(lane_s == _head(j, i, e), gs_h, 0.0)
                ds = jnp.concatenate(ds, axis=0)
                t = jnp.dot(ds, ks[j][e], preferred_element_type=F32)
                dq = t if dq is None else dq + t
                half = slice(HEAD_DIM * e, HEAD_DIM * (e + 1))
                a = jnp.dot(q_t[half, :], ds, preferred_element_type=F32)
                b = jnp.dot(do_t[half, :], p, preferred_element_type=F32)
                dk_j = a if dk_j is None else dk_j + a
                dv_j = b if dv_j is None else dv_j + b
            for i in range(PAIRS_PER_KV):
                dpj_ref[:, _pair_cols(j, i, OFF_Q)] = (dq[BLOCK * i:BLOCK * (i + 1), :] * SCALE).astype(BF16)
            dk_t.append(dk_j)
            dv_t.append(dv_j)
        dk = jnp.concatenate(dk_t, axis=0).T
        dv = jnp.concatenate(dv_t, axis=0).T
        dpj_ref[:, OFF_K:OFF_K + D_KV] = (dk[BLOCK:, :] + dkv_scr[:, 0:D_KV]).astype(BF16)
        dpj_ref[:, OFF_V:OFF_V + D_KV] = (dv[BLOCK:, :] + dkv_scr[:, D_KV:2 * D_KV]).astype(BF16)
        dkv_scr[:, 0:D_KV] = dk[:BLOCK, :]
        dkv_scr[:, D_KV:2 * D_KV] = dv[:BLOCK, :]
        gslab_ref[ROW_SINKS:ROW_SINKS + 1, :] += gsink

        @pl.when(step == N_BLOCKS - 1)
        def _():
            for k, slab_row in ((ACC_NORM_CONV, ROW_NORM_CONV), (ACC_NORM_ATTN, ROW_NORM_ATTN), (ACC_CONV0, ROW_CONV0),
                                (ACC_CONV0 + 1, ROW_CONV0 + 1), (ACC_CONV0 + 2, ROW_CONV0 + 2)):
                gslab_ref[slab_row:slab_row + 1, :] = jnp.sum(acc_scr[k], axis=0, keepdims=True)

    per_block = BLOCK // HALO
    last = N_BLOCKS - 1
    return pl.pallas_call(
        body, name="mix_bwd", grid=(N_BLOCKS,),
        in_specs=[
            pl.BlockSpec((BLOCK, D_PROJ), lambda s: (last - s, 0)),
            pl.BlockSpec((BLOCK, 2 * D_KV), lambda s: (jnp.maximum(last - s - 1, 0), OFF_K // (2 * D_KV))),
            pl.BlockSpec((HALO, D_CONV), lambda s: (jnp.maximum((last - s) * per_block - 1, 0), OFF_CC // D_CONV)),
            pl.BlockSpec((HALO, D_CONV), lambda s: (jnp.maximum((last - s) * per_block - 1, 0), OFF_CU // D_CONV)),
            pl.BlockSpec((BLOCK, D_MIX), lambda s: (last - s, 0)),
            pl.BlockSpec((BLOCK, D_ATTN), lambda s: (last - s, 0)),
            pl.BlockSpec((1, 4, STACK, 2 * BLOCK), lambda s: (last - s, 0, 0, 0)),
            pl.BlockSpec((1, 4, STACK, 128), lambda s: (last - s, 0, 0, 0)),
            pl.BlockSpec((8, D_CONV), lambda s: (0, 0)),
            pl.BlockSpec((1, D_CONV), lambda s: (0, 0)),
            pl.BlockSpec((1, D_ATTN), lambda s: (0, 0)),
        ],
        out_specs=(pl.BlockSpec((BLOCK, D_PROJ), lambda s: (last - s, 0)),
                   pl.BlockSpec((8, D_MODEL), lambda s: (0, 0))),
        out_shape=(jax.ShapeDtypeStruct((SEQ, D_PROJ), BF16), jax.ShapeDtypeStruct((8, D_MODEL), F32)),
        scratch_shapes=[pltpu.VMEM((BLOCK, D_ATTN), F32), pltpu.VMEM((CHUNK, D_CONV), F32),
                        pltpu.VMEM((BLOCK, 2 * D_KV), F32), pltpu.VMEM((N_ACC, CHUNK, D_MODEL), F32)],
        compiler_params=_params(dimension_semantics=("arbitrary",)),
    )(proj, proj, proj, proj, dmixed, attn, probs, shares, conv_full, norm_conv, norm_attn)


def _in_bwd_rs(dproj, w_full, x, dx2, norm_in, dw_in_chip, gslab, gnf, loss_part):
    tm = 256
    steps = SEQ // tm

    def body(dp_ref, w_hbm, x_ref, dx2_ref, g_ref, dwi_ref, gs_ref, gnf_ref, lp_ref, gx_ref, gwin_ref, gsum_ref,
             gni_scr, own, ici, myslab, slabs, w_ref, send_sems, recv_sems, local_sems):
        i = pl.program_id(0)
        rs_start, rs_finish = _ici_sum(dwi_ref, own, ici, send_sems, recv_sems, local_sems)
        slab_start, slab_finish = _slab_sum(myslab, slabs, send_sems, recv_sems, N_ICI_SUM_SEMS)

        @pl.when(i == 0)
        def _():
            gni_scr[...] = jnp.zeros_like(gni_scr)
            rs_start()
            w_load = pltpu.make_async_copy(w_hbm, w_ref, local_sems.at[1])
            w_load.start()
            w_load.wait()

        dh = jnp.dot(dp_ref[...], w_ref[...], preferred_element_type=F32)
        xv = x_ref[...]
        r = lax.rsqrt(jnp.mean(xv * xv, axis=-1, keepdims=True) + RMS_EPS)
        xn = xv * r
        u = dh * g_ref[...]
        gx_ref[...] = dx2_ref[...] + r * (u - xn * jnp.mean(u * xn, axis=-1, keepdims=True))
        gni_scr[...] += jnp.sum(dh * xn, axis=0, keepdims=True)

        @pl.when(i == steps - 1)
        def _():
            row = lax.broadcasted_iota(jnp.int32, (8, D_MODEL), 0)
            lane = lax.broadcasted_iota(jnp.int32, (8, D_MODEL), 1)
            slab = jnp.where(row == ROW_NORM_IN, gni_scr[...], jnp.where(row == ROW_NORM_FINAL, gnf_ref[...], gs_ref[...]))
            myslab[...] = jnp.where((row == ROW_SINKS) & (lane == LOSS_LANE), lp_ref[0:1, 0:1], slab)
            slab_start()
            gwin_ref[...] = rs_finish()
            gsum_ref[...] = slab_finish()

    const = lambda i: (0, 0)
    return pl.pallas_call(
        body, name="in_bwd", grid=(steps,),
        in_specs=[pl.BlockSpec((tm, D_PROJ), lambda i: (i, 0)), pl.BlockSpec(memory_space=pl.ANY),
                  pl.BlockSpec((tm, D_MODEL), lambda i: (i, 0)), pl.BlockSpec((tm, D_MODEL), lambda i: (i, 0)),
                  pl.BlockSpec((1, D_MODEL), const), pl.BlockSpec(memory_space=pl.ANY),
                  pl.BlockSpec((8, D_MODEL), const), pl.BlockSpec((1, D_MODEL), const), pl.BlockSpec((8, 128), const)],
        out_specs=(pl.BlockSpec((tm, D_MODEL), lambda i: (i, 0)), pl.BlockSpec((SHARD_IN, D_MODEL), const),
                   pl.BlockSpec((8, D_MODEL), const)),
        out_shape=(jax.ShapeDtypeStruct((SEQ, D_MODEL), F32), jax.ShapeDtypeStruct((SHARD_IN, D_MODEL), F32),
                   jax.ShapeDtypeStruct((8, D_MODEL), F32)),
        scratch_shapes=[pltpu.VMEM((1, D_MODEL), F32), pltpu.VMEM((SHARD_IN, D_MODEL), BF16),
                        pltpu.VMEM((2, SHARD_IN, D_MODEL), BF16),
                        pltpu.VMEM((8, D_MODEL), F32), pltpu.VMEM((N_DEV, 8, D_MODEL), F32),
                        pltpu.VMEM((D_PROJ, D_MODEL), BF16),
                        pltpu.SemaphoreType.DMA((N_ICI_SUM_SEMS + 7,)), pltpu.SemaphoreType.DMA((N_ICI_SUM_SEMS + 7,)),
                        pltpu.SemaphoreType.DMA((2,))],
        compiler_params=_params(dimension_semantics=("arbitrary",)),
    )(dproj, w_full, x, dx2, norm_in, dw_in_chip, gslab, gnf, loss_part)


def _dw_rs(mixed, dx2b, dproj, h, table):
    tn_out, tn = 2 * SHARD_OUT, IN_PROJ_TILE
    out_steps, in_steps = D_MIX // tn_out, D_PROJ // tn
    steps = out_steps + in_steps
    out_order = (DG, NX, NY, OWN)

    def out_tile(i):
        chip = 2 * lax.axis_index("x") + lax.axis_index("y")
        return jnp.bitwise_xor(chip, (out_steps - 1) - jnp.minimum(i, out_steps - 1))

    def in_tile(table_ref, i):
        return _dw_entry(table_ref, jnp.maximum(i - out_steps, 0))

    def body(table_ref, mx_ref, dxb_ref, a_ref, b_ref, chip_ref, gwo_ref, dwo, dwt, d2d_in, via, own, d2d, ici,
             send_sems, recv_sems, local_sems):
        i = pl.program_id(0)
        rs_start, rs_forward, rs_finish = _shard_sum(dwo, own, d2d, ici, send_sems, recv_sems, local_sems)
        before_tile, after_tiles, chip_finish = _chip_sum(
            dwt, d2d_in, via, chip_ref, lambda k: _dw_entry(table_ref, in_steps + k), send_sems, recv_sems, local_sems,
            N_SHARD_SUM_SEMS, 4)

        for j, k in enumerate(out_order):
            @pl.when(i == j + 1)
            def _():
                rs_start(k)

            if k != OWN:
                @pl.when(i == j + 2)
                def _():
                    rs_forward(k)

        @pl.when(i < out_steps)
        def _():
            tile = lax.dot_general(mx_ref[...], dxb_ref[...], _TN, preferred_element_type=F32).astype(BF16)
            for core in range(2):
                dwo[2 * out_tile(i) + core] = tile[SHARD_OUT * core:SHARD_OUT * (core + 1), :]

        @pl.when(i >= out_steps)
        def _():
            before_tile(i - out_steps)
            tile = lax.dot_general(a_ref[...], b_ref[...], _TN, preferred_element_type=F32).astype(BF16)
            dwt[pl.ds(pl.multiple_of(in_tile(table_ref, i) * tn, tn), tn), :] = tile

        @pl.when(i == steps - 1)
        def _():
            after_tiles()
            gwo_ref[...] = rs_finish()
            chip_finish()

    vmem = pl.BlockSpec(memory_space=pltpu.VMEM)
    grid_spec = pltpu.PrefetchScalarGridSpec(
        num_scalar_prefetch=1, grid=(steps,),
        in_specs=[pl.BlockSpec((SEQ, tn_out), lambda i, table_ref: (0, out_tile(i))), vmem,
                  pl.BlockSpec((SEQ, tn), lambda i, table_ref: (0, in_tile(table_ref, i))), vmem],
        out_specs=(pl.BlockSpec(memory_space=pl.ANY), pl.BlockSpec((SHARD_OUT, D_MODEL), lambda i, table_ref: (0, 0))),
        scratch_shapes=[pltpu.VMEM((N_DEV, SHARD_OUT, D_MODEL), BF16),
                        pltpu.VMEM((D_PROJ, D_MODEL), BF16), pltpu.VMEM((4, SHARD_IN, D_MODEL), BF16),
                        pltpu.VMEM((2, HALF_IN, D_MODEL), BF16),
                        *_shard_sum_scratch(SHARD_OUT),
                        pltpu.SemaphoreType.DMA((N_SHARD_SUM_SEMS + N_CHIP_SUM_SEMS,)),
                        pltpu.SemaphoreType.DMA((N_SHARD_SUM_SEMS + N_CHIP_SUM_SEMS,)),
                        pltpu.SemaphoreType.DMA((7,))])
    return pl.pallas_call(
        body, name="dw", grid_spec=grid_spec,
        out_shape=(jax.ShapeDtypeStruct((3, SHARD_IN, D_MODEL), BF16), jax.ShapeDtypeStruct((SHARD_OUT, D_MODEL), F32)),
        compiler_params=_params(dimension_semantics=("arbitrary",)),
    )(table, mixed, dx2b, dproj, h)


def _adam_all(big_in, big_out, gsum, small, grad_x):
    steps = 4
    tr_in, tr_out = SHARD_IN // steps, SHARD_OUT // steps

    def body(*refs):
        ins, outs = refs[:8 + 1 + 18 + 1], refs[8 + 1 + 18 + 1:]
        i = pl.program_id(0)
        outs[33][...] = ins[27][...]
        for b in range(2):
            w_ref, g_ref, m_ref, v_ref = ins[4 * b:4 * b + 4]
            g = g_ref[...]
            delta, mn, vn = _adamw(w_ref[...], g, m_ref[...], v_ref[...])
            for ref, val in zip(outs[4 * b:4 * b + 4], (g, delta, mn, vn)):
                ref[...] = val

        @pl.when(i == 0)
        def _():
            gsum = ins[8][...]
            idx = _slot(lax.axis_index("x"), lax.axis_index("y"), lax.axis_index("c"))
            cg = jnp.zeros((3, SHARD_CONV), F32)
            for d in range(N_DEV):
                cg = jnp.where(idx == d, gsum[ROW_CONV0:ROW_CONV0 + 3, d * SHARD_CONV:(d + 1) * SHARD_CONV], cg)
            grads = (gsum[ROW_NORM_IN:ROW_NORM_IN + 1], gsum[ROW_SINKS:ROW_SINKS + 1, 0:N_Q_HEADS],
                     gsum[ROW_NORM_CONV:ROW_NORM_CONV + 1], gsum[ROW_NORM_ATTN:ROW_NORM_ATTN + 1],
                     gsum[ROW_NORM_FINAL:ROW_NORM_FINAL + 1], cg)
            for s, g in enumerate(grads):
                at = (slice(None), 0, slice(None)) if s == 5 else (slice(None), slice(None))
                w_ref, m_ref, v_ref = ins[9 + 3 * s:12 + 3 * s]
                delta, mn, vn = _adamw(w_ref[at], g, m_ref[at], v_ref[at])
                for ref, val in zip(outs[8 + 4 * s:12 + 4 * s], (g, delta, mn, vn)):
                    ref[at] = val
            outs[32][...] = gsum[ROW_SINKS:ROW_SINKS + 1, LOSS_LANE:LOSS_LANE + 1]

    const = lambda i: (0, 0)
    rows = lambda i: (i, 0)
    whole = lambda shape: pl.BlockSpec(shape, lambda i: (0,) * len(shape))
    small_shapes = [a.shape for a in small[::3]]
    in_specs = ([pl.BlockSpec((tr_in, D_MODEL), rows)] * 4 + [pl.BlockSpec((tr_out, D_MODEL), rows)] * 4
                + [pl.BlockSpec((8, D_MODEL), const)] + [whole(a.shape) for a in small]
                + [pl.BlockSpec((SEQ // steps, D_MODEL), rows)])
    out_specs = ([pl.BlockSpec((tr_in, D_MODEL), rows)] * 4 + [pl.BlockSpec((tr_out, D_MODEL), rows)] * 4
                 + [whole(s) for s in small_shapes for _ in range(4)] + [pl.BlockSpec((1, 1), const)]
                 + [pl.BlockSpec((SEQ // steps, D_MODEL), rows)])
    out_shape = ([jax.ShapeDtypeStruct((SHARD_IN, D_MODEL), F32)] * 4 + [jax.ShapeDtypeStruct((SHARD_OUT, D_MODEL), F32)] * 4
                 + [jax.ShapeDtypeStruct(s, F32) for s in small_shapes for _ in range(4)]
                 + [jax.ShapeDtypeStruct((1, 1), F32), jax.ShapeDtypeStruct((SEQ, D_MODEL), F32)])
    outs = pl.pallas_call(
        body, name="adam", grid=(steps,), in_specs=in_specs, out_specs=tuple(out_specs), out_shape=tuple(out_shape),
        compiler_params=_params(dimension_semantics=("arbitrary",)),
    )(*big_in, *big_out, gsum, *small, grad_x)
    return outs[0:4], outs[4:8], [outs[8 + 4 * s:12 + 4 * s] for s in range(6)], outs[32], outs[33]


def _rows_first(a):
    return jnp.transpose(a, (1, 0, 2))


def kernel(x, norm_in, w_in, conv_w, attn_sinks, norm_conv_out, norm_attn_out, w_out, norm_final, loss_target, m_norm_in, m_w_in, m_conv_w, m_attn_sinks, m_norm_conv_out, m_norm_attn_out, m_w_out, m_norm_final, v_norm_in, v_w_in, v_conv_w, v_attn_sinks, v_norm_conv_out, v_norm_attn_out, v_w_out, v_norm_final):
    x2d = x.reshape(SEQ, D_MODEL)
    target = loss_target.reshape(SEQ, D_MODEL)
    nf = norm_final.reshape(1, D_MODEL)

    w_in_t, m_w_in_t, v_w_in_t = w_in[0].T, m_w_in[0].T, v_w_in[0].T
    tiles = jnp.asarray(TILE_ORDER, jnp.int32).reshape(-1)
    w_in_full, h, proj, g_out, conv_full = _gather_in_proj(x2d, norm_in, w_in_t, w_out[0], _rows_first(conv_w), tiles)
    sinks = attn_sinks.reshape(N_Q_HEADS)

    mixed, attn, probs, shares = _mix_fwd(proj, conv_full, sinks, norm_conv_out, norm_attn_out)
    dx2, dx2b, dmixed, gnf, loss_part = _out_proj_loss(mixed, x2d, target, g_out.reshape(D_MIX, D_MODEL), nf)
    dproj, gslab = _mix_bwd(proj, dmixed, attn, probs, shares, conv_full, norm_conv_out, norm_attn_out)
    dw_in_chip, g_w_out = _dw_rs(mixed, dx2b, dproj, h, jnp.asarray(DW_TABLE, jnp.int32).reshape(-1))
    grad_x, g_w_in, gsum = _in_bwd_rs(dproj, w_in_full, x2d, dx2, norm_in, dw_in_chip, gslab, gnf, loss_part)

    small = (norm_in, m_norm_in, v_norm_in, attn_sinks, m_attn_sinks, v_attn_sinks,
             norm_conv_out, m_norm_conv_out, v_norm_conv_out, norm_attn_out, m_norm_attn_out, v_norm_attn_out,
             nf, m_norm_final.reshape(1, D_MODEL), v_norm_final.reshape(1, D_MODEL),
             _rows_first(conv_w), _rows_first(m_conv_w), _rows_first(v_conv_w))
    big_in, big_out, (s_ni, s_sk, s_nc, s_na, s_nf, s_cv), loss, grad_x = _adam_all(
        (w_in_t, g_w_in, m_w_in_t, v_w_in_t), (w_out[0], g_w_out, m_w_out[0], v_w_out[0]), gsum, small, grad_x)

    def leaves(k):
        return (s_ni[k], big_in[k].T[None], jnp.transpose(s_cv[k], (1, 0, 2)), s_sk[k], s_nc[k], s_na[k], big_out[k][None],
                s_nf[k].reshape(D_MODEL))

    return (loss.reshape(()), grad_x.reshape(1, SEQ, D_MODEL), *leaves(0), *leaves(1), *leaves(2), *leaves(3))
```

```python
import jax
import jax.numpy as jnp
from jax import lax
from jax.experimental import pallas as pl
from jax.experimental.pallas import tpu as pltpu

F32 = jnp.float32
BF16 = jnp.bfloat16
MESH = pl.DeviceIdType.MESH

N_DEV = 8
SEQ = 2048
D_MODEL = 1024
D_CONV = 1024
D_ATTN = 1024
D_KV = 128
HEAD_DIM = 64
N_Q_HEADS = 16
N_PAIRS = N_Q_HEADS // 2
PAIRS_PER_KV = N_PAIRS // 2
D_MIX = D_CONV + D_ATTN
D_PROJ = 6400
SHARD_IN = D_PROJ // N_DEV
SHARD_OUT = D_MIX // N_DEV
SHARD_CONV = D_CONV // N_DEV
OFF_CB, OFF_CC, OFF_CU, OFF_GC, OFF_Q, OFF_K, OFF_V, OFF_GA = 0, 1024, 2048, 3072, 4096, 5120, 5248, 5376
BLOCK = 128
N_BLOCKS = SEQ // BLOCK
HALO = 8
CHUNK = 16
N_CHUNKS = BLOCK // CHUNK
RMS_EPS = 1e-5
NEG = -1e30
SCALE = HEAD_DIM ** -0.5
SLOPES = tuple(2.0 ** (-8.0 * (h + 1) / N_Q_HEADS) for h in range(N_Q_HEADS))

ADAM_LR = 0.001
ADAM_B1 = 0.9
ADAM_B2 = 0.999
ADAM_EPS = 1e-08
ADAM_WD = 0.01
ADAM_STEP = 10

ROW_NORM_IN, ROW_NORM_CONV, ROW_NORM_ATTN, ROW_NORM_FINAL, ROW_CONV0, ROW_SINKS = 0, 1, 2, 3, 4, 7
LOSS_LANE = N_Q_HEADS
ACC_NORM_CONV, ACC_NORM_ATTN, ACC_CONV0, N_ACC = 0, 1, 2, 5

VMEM_LIMIT = 56 * 1024 * 1024

_NT = (((1,), (1,)), ((), ()))
_TN = (((0,), (0,)), ((), ()))


def _params(**kw):
    return pltpu.CompilerParams(vmem_limit_bytes=VMEM_LIMIT, **kw)


def _adamw(w, g, m, v):
    m = ADAM_B1 * m + (1.0 - ADAM_B1) * g
    v = ADAM_B2 * v + (1.0 - ADAM_B2) * (g * g)
    m_hat = m / (1.0 - ADAM_B1 ** ADAM_STEP)
    v_hat = v / (1.0 - ADAM_B2 ** ADAM_STEP)
    delta = -ADAM_LR * (m_hat / (jnp.sqrt(v_hat) + ADAM_EPS) + ADAM_WD * w)
    return delta, m, v


def _sigmoid(t):
    return 1.0 / (1.0 + jnp.exp(-t))


def _slot(px, py, pc):
    return 4 * px + 2 * py + pc


OWN, NX, NY, DG = range(4)
HALF_IN = SHARD_IN // 2
N_GATHER_KINDS = 13
W_OUT_KINDS = N_GATHER_KINDS + 7


IN_PROJ_TILE = 640
TILE_ORDER = ((0, 1, 2, 3, 4, 5, 6, 7, 8, 9), (3, 4, 0, 1, 2, 8, 9, 5, 6, 7),
              (5, 6, 0, 1, 7, 8, 9, 2, 3, 4), (8, 9, 3, 4, 5, 6, 7, 0, 1, 2))
TILES_OWN, TILES_NEIGHBOURS = 2, 7


def _tile(table_ref, p):
    chip = 2 * lax.axis_index("x") + lax.axis_index("y")
    return table_ref[chip * len(TILE_ORDER[0]) + p]


DW_TILE_ORDER = tuple(tuple(reversed(row)) for row in TILE_ORDER)


def _tiles_until_complete(chip, owner):
    lo, hi = owner * 2 * SHARD_IN, (owner + 1) * 2 * SHARD_IN
    touching = [t for t in range(len(TILE_ORDER[0])) if t * IN_PROJ_TILE < hi and (t + 1) * IN_PROJ_TILE > lo]
    return 1 + max(DW_TILE_ORDER[chip].index(t) for t in touching)


DW_TABLE = tuple(DW_TILE_ORDER[chip] + tuple(_tiles_until_complete(chip, chip ^ flip) for flip in (0, 2, 1, 3))
                 for chip in range(4))


def _dw_entry(table_ref, p):
    chip = 2 * lax.axis_index("x") + lax.axis_index("y")
    return table_ref[chip * len(DW_TABLE[0]) + p]


def _gather_in_proj(x, norm_in, w_in_sh, w_out_sh, conv_sh, tiles):
    tn = IN_PROJ_TILE
    steps = D_PROJ // tn
    tm = 256

    def body(tiles_ref, x_hbm, g_ref, win_ref, wout_ref, cv_ref, wt_ref, h_ref, proj_ref, gout_ref, conv_ref,
             gin_ref, gcv_ref, wob_ref, x_ref, send_sems, recv_sems, local_sems):
        p = pl.program_id(0)
        local_sem = local_sems.at[0]
        x, y, c = lax.axis_index("x"), lax.axis_index("y"), lax.axis_index("c")
        me, sibling = (x, y, c), (x, y, 1 - c)
        nx, ny, dg = (1 - x, y, c), (x, 1 - y, c), (1 - x, 1 - y, c)

        def other(dev):
            return (dev[0], dev[1], 1 - dev[2])

        def shard(dev):
            return gin_ref.at[pl.ds(pl.multiple_of(_slot(*dev) * SHARD_IN, 16), SHARD_IN), :]

        def half(dev, h):
            return gin_ref.at[pl.ds(pl.multiple_of(_slot(*dev) * SHARD_IN + h * HALF_IN, 16), HALF_IN), :]

        def rc(ref, k, to):
            return pltpu.make_async_remote_copy(src_ref=ref, dst_ref=ref, send_sem=send_sems.at[k],
                                                recv_sem=recv_sems.at[k], device_id=to, device_id_type=MESH)

        def cv(k, dev, to):
            s = _slot(*dev)
            return pltpu.make_async_remote_copy(src_ref=gcv_ref.at[s], dst_ref=gcv_ref.at[s],
                                                send_sem=send_sems.at[N_GATHER_KINDS + k],
                                                recv_sem=recv_sems.at[N_GATHER_KINDS + k], device_id=to, device_id_type=MESH)

        def own_copies():
            return [rc(shard(me), 0, sibling),
                    rc(half(me, 0), 1, nx), rc(half(me, 1), 2, nx),
                    rc(half(me, 1), 4, ny), rc(half(me, 0), 3, ny),
                    cv(0, me, sibling)] + [cv(1 + j, me, peer) for j, peer in enumerate((nx, ny, dg))]

        def pass_on(dev, h, k_in, k_ici, k_d2d, half=half, base=0):
            rc(half(dev, h), base + k_in, me).wait_recv()
            if k_ici is not None:
                rc(half(dev, h), base + k_ici, ny if dev is nx else nx).start()
            rc(half(dev, h), base + k_d2d, sibling).start()

        def out_half(dev, h):
            return gout_ref.at[_slot(*dev), pl.ds(h * (SHARD_OUT // 2), SHARD_OUT // 2), :]

        def own_out_copies():
            src = lambda h: wob_ref.at[pl.ds(h * (SHARD_OUT // 2), SHARD_OUT // 2), :]

            def send(ref, dst, k, to):
                return pltpu.make_async_remote_copy(src_ref=ref, dst_ref=dst, send_sem=send_sems.at[W_OUT_KINDS + k],
                                                    recv_sem=recv_sems.at[W_OUT_KINDS + k], device_id=to, device_id_type=MESH)

            return [send(wob_ref, gout_ref.at[_slot(*me)], 0, sibling),
                    send(src(0), out_half(me, 0), 1, nx), send(src(1), out_half(me, 1), 2, nx),
                    send(src(1), out_half(me, 1), 4, ny), send(src(0), out_half(me, 0), 3, ny)]

        def own_out_local():
            return pltpu.make_async_copy(wob_ref, gout_ref.at[_slot(*me)], local_sems.at[1])

        @pl.when(p == 0)
        def _():
            gin_ref[pl.ds(pl.multiple_of(_slot(*me) * SHARD_IN, 16), SHARD_IN), :] = win_ref[...].astype(BF16)
            gcv_ref[_slot(*me)] = jnp.zeros((8, SHARD_CONV), F32)
            gcv_ref[_slot(*me), 0:3, :] = cv_ref[:, 0, :]
            for cp in own_copies():
                cp.start()
            wob_ref[...] = wout_ref[...].astype(BF16)
            x_load = pltpu.make_async_copy(x_hbm, x_ref, local_sems.at[2])
            x_load.start()
            x_load.wait()
            for t in range(SEQ // tm):
                xv = x_ref[tm * t:tm * (t + 1), :]
                r = lax.rsqrt(jnp.mean(xv * xv, axis=-1, keepdims=True) + RMS_EPS)
                h_ref[tm * t:tm * (t + 1), :] = (xv * r * g_ref[...]).astype(BF16)
            rc(shard(sibling), 0, me).wait_recv()

        @pl.when(p == TILES_OWN)
        def _():
            for args in ((nx, 0, 1, 5, 7), (ny, 1, 4, 6, 10), (nx, 1, 2, None, 8), (ny, 0, 3, None, 9)):
                pass_on(*args)
            for j, peer in enumerate((nx, ny, dg)):
                cv(1 + j, peer, me).wait_recv()
                cv(4 + j, peer, sibling).start()
            for (dev, h), k in (((nx, 0), 7), ((nx, 1), 8), ((ny, 0), 9), ((ny, 1), 10)):
                rc(half(other(dev), h), k, me).wait_recv()
            own_out_local().start()
            for cp in own_out_copies():
                cp.start()

        @pl.when(p == TILES_NEIGHBOURS - 1)
        def _():
            pass_on(dg, 0, 5, None, 11)
            pass_on(dg, 1, 6, None, 12)

        @pl.when(p == TILES_NEIGHBOURS)
        def _():
            for (dev, h), k in (((dg, 0), 11), ((dg, 1), 12)):
                rc(half(other(dev), h), k, me).wait_recv()
            pltpu.make_async_copy(gin_ref, wt_ref, local_sem).start()

        @pl.when(p == steps - 2)
        def _():
            for args in ((nx, 0, 1, 5, 7), (ny, 1, 4, 6, 10), (nx, 1, 2, None, 8), (ny, 0, 3, None, 9)):
                pass_on(*args, half=out_half, base=W_OUT_KINDS)

        w = gin_ref[pl.ds(pl.multiple_of(_tile(tiles_ref, p) * tn, tn), tn), :]
        proj_ref[...] = lax.dot_general(h_ref[...], w, _NT, preferred_element_type=F32)

        @pl.when(p == steps - 1)
        def _():
            cv(0, sibling, me).wait_recv()
            for j, peer in enumerate((nx, ny, dg)):
                cv(4 + j, other(peer), me).wait_recv()
            for d in range(N_DEV):
                conv_ref[:, d * SHARD_CONV:(d + 1) * SHARD_CONV] = gcv_ref[d]
            relayed = [rc(half(nx, 0), 5, ny), rc(half(ny, 1), 6, nx)]
            relayed += [rc(half(dev, h), k, sibling) for (dev, h), k in
                        (((nx, 0), 7), ((nx, 1), 8), ((ny, 0), 9), ((ny, 1), 10), ((dg, 0), 11), ((dg, 1), 12))]
            relayed += [cv(4 + j, peer, sibling) for j, peer in enumerate((nx, ny, dg))]
            for cp in own_copies() + relayed:
                cp.wait_send()
            pltpu.make_async_copy(gin_ref, wt_ref, local_sem).wait()
            pass_on(dg, 0, 5, None, 11, half=out_half, base=W_OUT_KINDS)
            pass_on(dg, 1, 6, None, 12, half=out_half, base=W_OUT_KINDS)
            rc(gout_ref.at[_slot(*sibling)], W_OUT_KINDS, me).wait_recv()
            out_relayed = [rc(out_half(nx, 0), W_OUT_KINDS + 5, ny), rc(out_half(ny, 1), W_OUT_KINDS + 6, nx)]
            for (dev, h), k in (((nx, 0), 7), ((nx, 1), 8), ((ny, 0), 9), ((ny, 1), 10), ((dg, 0), 11), ((dg, 1), 12)):
                rc(out_half(other(dev), h), W_OUT_KINDS + k, me).wait_recv()
                out_relayed.append(rc(out_half(dev, h), W_OUT_KINDS + k, sibling))
            for cp in own_out_copies() + out_relayed:
                cp.wait_send()
            own_out_local().wait()

    vmem = pl.BlockSpec(memory_space=pltpu.VMEM)
    grid_spec = pltpu.PrefetchScalarGridSpec(
        num_scalar_prefetch=1, grid=(steps,),
        in_specs=[pl.BlockSpec(memory_space=pl.ANY), vmem, vmem, vmem, vmem],
        out_specs=(pl.BlockSpec(memory_space=pl.ANY), vmem,
                   pl.BlockSpec((SEQ, tn), lambda p, tiles_ref: (0, _tile(tiles_ref, p))),
                   pl.BlockSpec(memory_space=pl.ANY), vmem),
        scratch_shapes=[pltpu.VMEM((D_PROJ, D_MODEL), BF16), pltpu.VMEM((N_DEV, 8, SHARD_CONV), F32),
                        pltpu.VMEM((SHARD_OUT, D_MODEL), BF16), pltpu.VMEM((SEQ, D_MODEL), F32),
                        pltpu.SemaphoreType.DMA((W_OUT_KINDS + N_GATHER_KINDS,)),
                        pltpu.SemaphoreType.DMA((W_OUT_KINDS + N_GATHER_KINDS,)),
                        pltpu.SemaphoreType.DMA((3,))])
    return pl.pallas_call(
        body, name="gather_in_proj", grid_spec=grid_spec,
        out_shape=(jax.ShapeDtypeStruct((D_PROJ, D_MODEL), BF16), jax.ShapeDtypeStruct((SEQ, D_MODEL), BF16),
                   jax.ShapeDtypeStruct((SEQ, D_PROJ), F32), jax.ShapeDtypeStruct((N_DEV, SHARD_OUT, D_MODEL), BF16),
                   jax.ShapeDtypeStruct((8, D_CONV), F32)),
        compiler_params=_params(dimension_semantics=("arbitrary",)),
    )(tiles, x, norm_in, w_in_sh, w_out_sh, conv_sh)


def _shard_sum(src, own, d2d, ici, send_sems, recv_sems, local_sems, base=0):
    x, y, c = lax.axis_index("x"), lax.axis_index("y"), lax.axis_index("c")
    sibling = (x, y, 1 - c)
    chips = [(x, y), (1 - x, y), (x, 1 - y), (1 - x, 1 - y)]

    def rcopy(s, d, k, to):
        return pltpu.make_async_remote_copy(src_ref=s, dst_ref=d, send_sem=send_sems.at[base + k],
                                            recv_sem=recv_sems.at[base + k], device_id=to, device_id_type=MESH)

    def mine(k):
        return pltpu.make_async_copy(src.at[_slot(*chips[k], c)], own.at[k], local_sems.at[k])

    def to_sibling(k):
        return rcopy(src.at[_slot(*chips[k], 1 - c)], d2d.at[k], k, sibling)

    def to_chip(k):
        return rcopy(own.at[k], ici.at[k - 1], 3 + k, (*chips[k], c))

    def start(k):
        mine(k).start()
        to_sibling(k).start()

    def forward(k):
        mine(k).wait()
        to_sibling(k).wait_recv()
        own[k] = (own[k].astype(F32) + d2d[k].astype(F32)).astype(BF16)
        to_chip(k).start()

    def finish():
        mine(0).wait()
        to_sibling(0).wait_recv()
        acc = own[0].astype(F32) + d2d[0].astype(F32)
        for k in range(1, 4):
            to_chip(k).wait_recv()
            acc = acc + ici[k - 1].astype(F32)
        for k in range(4):
            to_sibling(k).wait_send()
        for k in range(1, 4):
            to_chip(k).wait_send()
        return acc

    return start, forward, finish


def _shard_sum_scratch(rows):
    return [pltpu.VMEM((4, rows, D_MODEL), BF16), pltpu.VMEM((4, rows, D_MODEL), BF16),
            pltpu.VMEM((3, rows, D_MODEL), BF16)]


N_SHARD_SUM_SEMS = 7


N_CHIP_SUM_SEMS = 5


def _chip_sum(dwt, d2d, via, out_hbm, tiles_until, send_sems, recv_sems, local_sems, base, local_base):
    x, y, c = lax.axis_index("x"), lax.axis_index("y"), lax.axis_index("c")
    sibling, nx, ny = (x, y, 1 - c), (1 - x, y, c), (x, 1 - y, c)
    chips = [(x, y), (1 - x, y), (x, 1 - y), (1 - x, 1 - y)]

    def shard(s):
        return dwt.at[pl.ds(pl.multiple_of(s * SHARD_IN, 16), SHARD_IN), :]

    def half(ref, h):
        return ref.at[pl.ds(h * HALF_IN, HALF_IN), :]

    def rc(s, d, k, to):
        return pltpu.make_async_remote_copy(src_ref=s, dst_ref=d, send_sem=send_sems.at[base + k],
                                            recv_sem=recv_sems.at[base + k], device_id=to, device_id_type=MESH)

    def to_sibling(k):
        return rc(shard(_slot(*chips[k], 1 - c)), d2d.at[k - 1], k - 1, sibling)

    for_dg = (lambda: rc(half(d2d.at[DG - 1], 0), via.at[0], 3, nx), lambda: rc(half(d2d.at[DG - 1], 1), via.at[1], 4, ny))

    def save(k):
        return pltpu.make_async_copy(d2d.at[k - 1], out_hbm.at[k], local_sems.at[local_base + k])

    own_saves = (lambda: pltpu.make_async_copy(shard(_slot(x, y, c)), out_hbm.at[OWN], local_sems.at[local_base]),
                 lambda: pltpu.make_async_copy(shard(_slot(x, y, 1 - c)), out_hbm.at[3], local_sems.at[local_base + 3]))

    def before_tile(n):
        for k in (NX, NY, DG):
            @pl.when(tiles_until(k) == n)
            def _():
                to_sibling(k).start()

            @pl.when(tiles_until(k) + 1 == n)
            def _():
                to_sibling(k).wait_recv()
                d2d[k - 1] = (shard(_slot(*chips[k], c))[...].astype(F32) + d2d[k - 1].astype(F32)).astype(BF16)
                if k == DG:
                    for cp in for_dg:
                        cp().start()

    def after_tiles():
        for cp in own_saves:
            cp().start()

    def finish():
        for k, h in ((NY, 0), (NX, 1)):
            for_dg[h]().wait_recv()
            rows = pl.ds(h * HALF_IN, HALF_IN)
            d2d[k - 1, rows, :] = (d2d[k - 1, rows, :].astype(F32) + via[h].astype(F32)).astype(BF16)
            save(k).start()
        for cp in own_saves + (lambda: save(NX), lambda: save(NY)):
            cp().wait()
        for cp in (lambda: to_sibling(NX), lambda: to_sibling(NY), lambda: to_sibling(DG)) + for_dg:
            cp().wait_send()

    return before_tile, after_tiles, finish


N_ICI_SUM_SEMS = 3


def _ici_sum(src, own, d2d, ici, send_sems, recv_sems, local_sems, base=0):
    x, y, c = lax.axis_index("x"), lax.axis_index("y"), lax.axis_index("c")

    def rc(s, d, k, to):
        return pltpu.make_async_remote_copy(src_ref=s, dst_ref=d, send_sem=send_sems.at[base + k],
                                            recv_sem=recv_sems.at[base + k], device_id=to, device_id_type=MESH)

    copies = (lambda: rc(src.at[NX], ici.at[0], 0, (1 - x, y, c)), lambda: rc(src.at[NY], ici.at[1], 1, (x, 1 - y, c)),
              lambda: rc(src.at[3], d2d, 2, (x, y, 1 - c)))
    mine = lambda: pltpu.make_async_copy(src.at[OWN], own, local_sems.at[0])

    def start():
        for cp in copies + (mine,):
            cp().start()

    def finish():
        mine().wait()
        for cp in copies:
            cp().wait_recv()
        acc = own[...].astype(F32) + d2d[...].astype(F32) + ici[0].astype(F32) + ici[1].astype(F32)
        for cp in copies:
            cp().wait_send()
        return acc

    return start, finish


def _slab_sum(myslab, slabs, send_sems, recv_sems, base):
    x, y, c = lax.axis_index("x"), lax.axis_index("y"), lax.axis_index("c")
    me = _slot(x, y, c)
    peers = [(x, y, 1 - c), (1 - x, y, c), (x, 1 - y, c), (1 - x, 1 - y, c),
             (1 - x, y, 1 - c), (x, 1 - y, 1 - c), (1 - x, 1 - y, 1 - c)]

    def cp(k):
        return pltpu.make_async_remote_copy(src_ref=myslab, dst_ref=slabs.at[me], send_sem=send_sems.at[base + k],
                                            recv_sem=recv_sems.at[base + k], device_id=peers[k], device_id_type=MESH)

    def start():
        slabs[me] = myslab[...]
        for k in range(7):
            cp(k).start()

    def finish():
        for k in range(7):
            cp(k).wait_recv()
        total = slabs[0]
        for d in range(1, N_DEV):
            total = total + slabs[d]
        for k in range(7):
            cp(k).wait_send()
        return total

    return start, finish


def _chunk_rows(r):
    return slice(r * CHUNK, (r + 1) * CHUNK)


def _conv_halo(cch_ref, cuh_ref, n):
    zh = jnp.where(n > 0, cch_ref[...] * cuh_ref[...], 0.0)
    return jnp.concatenate([zh] * (CHUNK // HALO), axis=0)


def _conv_chunk(pj_ref, zhalo, cw, r):
    rows = _chunk_rows(r)
    cc = pj_ref[rows, OFF_CC:OFF_CC + D_CONV]
    cu = pj_ref[rows, OFF_CU:OFF_CU + D_CONV]
    z = cc * cu
    before = _chunk_rows(r - 1)
    zprev = pj_ref[before, OFF_CC:OFF_CC + D_CONV] * pj_ref[before, OFF_CU:OFF_CU + D_CONV] if r > 0 else zhalo
    row = lax.broadcasted_iota(jnp.int32, (CHUNK, D_CONV), 0)
    z1 = jnp.where(row < 1, pltpu.roll(zprev, 1, 0), pltpu.roll(z, 1, 0))
    z2 = jnp.where(row < 2, pltpu.roll(zprev, 2, 0), pltpu.roll(z, 2, 0))
    co = cw[0] * z2 + cw[1] * z1 + cw[2] * z
    return cc, cu, z, z1, z2, co


def _gated_norm(a, gain, t):
    r = lax.rsqrt(jnp.mean(a * a, axis=-1, keepdims=True) + RMS_EPS)
    return a * r * gain * (t * _sigmoid(t))


def _kv_bands(pj, kvp_ref):
    lane = lax.broadcasted_iota(jnp.int32, (2 * BLOCK, D_KV), 1)
    lo = lane < HEAD_DIM

    def bands(prev, cur):
        b = jnp.concatenate([prev, cur], axis=0)
        br = pltpu.roll(b, HEAD_DIM, 1)
        zero = jnp.zeros_like(b)
        return ((jnp.where(lo, b, zero).astype(BF16), jnp.where(lo, zero, br).astype(BF16)),
                (jnp.where(lo, br, zero).astype(BF16), jnp.where(lo, zero, b).astype(BF16)))

    ks = bands(kvp_ref[:, 0:D_KV], pj[:, OFF_K:OFF_K + D_KV])
    vs = bands(kvp_ref[:, D_KV:2 * D_KV], pj[:, OFF_V:OFF_V + D_KV])
    return ks, vs


STACK = PAIRS_PER_KV * BLOCK


def _head(j, i, e):
    return 2 * (PAIRS_PER_KV * j + i) + e


def _pair_cols(j, i, off):
    p = PAIRS_PER_KV * j + i
    return slice(off + 128 * p, off + 128 * (p + 1))


def _fill_attn_bias(bias_scr, first_block):
    qi = lax.broadcasted_iota(jnp.int32, (BLOCK, 2 * BLOCK), 0)
    kj = lax.broadcasted_iota(jnp.int32, (BLOCK, 2 * BLOCK), 1)
    dist = BLOCK + qi - kj
    valid = (dist >= 0) & (dist < BLOCK)
    if first_block:
        valid = valid & (kj >= BLOCK)
    distf = dist.astype(F32)
    for j in range(2):
        for e in range(2):
            for i in range(PAIRS_PER_KV):
                bias_scr[2 * j + e, BLOCK * i:BLOCK * (i + 1), :] = jnp.where(valid, -SLOPES[_head(j, i, e)] * distf, NEG)


def _q_stack(pj, j):
    return jnp.concatenate([(pj[:, _pair_cols(j, i, OFF_Q)] * SCALE).astype(BF16) for i in range(PAIRS_PER_KV)], axis=0)


def _attn_probs(q_stack, kband, bias_ref, sinks):
    s = lax.dot_general(q_stack, kband, _NT, preferred_element_type=F32)
    ones = jnp.ones((128, 128), BF16)
    probs, shares = [], []
    for i, sink in enumerate(sinks):
        rows = slice(BLOCK * i, BLOCK * (i + 1))
        t = s[rows, :] + bias_ref[rows, :]
        m = jnp.broadcast_to(jnp.max(t, axis=-1, keepdims=True), (BLOCK, 128))
        m = jnp.maximum(m, sink)
        p = [jnp.exp(t[:, :128] - m), jnp.exp(t[:, 128:] - m)]
        es = jnp.exp(sink - m)
        total = (jnp.dot(p[0].astype(BF16), ones, preferred_element_type=F32)
                 + jnp.dot(p[1].astype(BF16), ones, preferred_element_type=F32))
        inv = 1.0 / (total + es)
        probs.append(jnp.concatenate([p[0] * inv, p[1] * inv], axis=1))
        shares.append(es * inv)
    return jnp.concatenate(probs, axis=0), jnp.concatenate(shares, axis=0)


def _attn_group(pj, ks, vs, bias_scr, sink_ref, j):
    q_stack = _q_stack(pj, j)
    out, probs, shares = None, [], []
    for e in range(2):
        p, ps = _attn_probs(q_stack, ks[j][e], bias_scr.at[2 * j + e],
                            [sink_ref[_head(j, i, e)] for i in range(PAIRS_PER_KV)])
        p = p.astype(BF16)
        o = jnp.dot(p, vs[j][e], preferred_element_type=F32)
        out = o if out is None else out + o
        probs.append(p)
        shares.append(ps)
    return out, probs, shares


def _mix_fwd(proj, conv_full, sinks, norm_conv, norm_attn):
    def body(pj_ref, kvp_ref, cch_ref, cuh_ref, cw_ref, sink_ref, gc_ref, ga_ref,
             mixed_ref, attn_scr, p_ref, ps_ref, bias_scr):
        n = pl.program_id(0)
        pj = pj_ref

        @pl.when(n == 0)
        def _():
            _fill_attn_bias(bias_scr, first_block=True)

        @pl.when(n == 1)
        def _():
            _fill_attn_bias(bias_scr, first_block=False)

        zhalo = _conv_halo(cch_ref, cuh_ref, n)
        cw = (cw_ref[0:1, :], cw_ref[1:2, :], cw_ref[2:3, :])
        gain_c = gc_ref[...]

        for r in range(N_CHUNKS):
            rows = _chunk_rows(r)
            co = _conv_chunk(pj_ref, zhalo, cw, r)[-1]
            y = _gated_norm(pj_ref[rows, OFF_CB:OFF_CB + D_CONV] * co, gain_c, pj_ref[rows, OFF_GC:OFF_GC + D_CONV])
            mixed_ref[rows, 0:D_CONV] = y.astype(BF16)

        ks, vs = _kv_bands(pj, kvp_ref)
        for j in range(2):
            out, probs, shares = _attn_group(pj, ks, vs, bias_scr, sink_ref, j)
            for e in range(2):
                p_ref[0, 2 * j + e] = probs[e]
                ps_ref[0, 2 * j + e] = shares[e]
            for i in range(PAIRS_PER_KV):
                attn_scr[:, _pair_cols(j, i, 0)] = out[BLOCK * i:BLOCK * (i + 1), :]
        gain_a = ga_ref[...]

        for r in range(N_CHUNKS):
            rows = _chunk_rows(r)
            y = _gated_norm(attn_scr[rows, :], gain_a, pj_ref[rows, OFF_GA:OFF_GA + D_ATTN])
            mixed_ref[rows, D_CONV:D_MIX] = y.astype(BF16)

    per_block = BLOCK // HALO
    return pl.pallas_call(
        body, name="mix_fwd", grid=(N_BLOCKS,),
        in_specs=[
            pl.BlockSpec((BLOCK, D_PROJ), lambda n: (n, 0)),
            pl.BlockSpec((BLOCK, 2 * D_KV), lambda n: (jnp.maximum(n - 1, 0), OFF_K // (2 * D_KV))),
            pl.BlockSpec((HALO, D_CONV), lambda n: (jnp.maximum(n * per_block - 1, 0), OFF_CC // D_CONV)),
            pl.BlockSpec((HALO, D_CONV), lambda n: (jnp.maximum(n * per_block - 1, 0), OFF_CU // D_CONV)),
            pl.BlockSpec((8, D_CONV), lambda n: (0, 0)),
            pl.BlockSpec(memory_space=pltpu.SMEM),
            pl.BlockSpec((1, D_CONV), lambda n: (0, 0)),
            pl.BlockSpec((1, D_ATTN), lambda n: (0, 0)),
        ],
        out_specs=(pl.BlockSpec((BLOCK, D_MIX), lambda n: (n, 0)), pl.BlockSpec((BLOCK, D_ATTN), lambda n: (n, 0)),
                   pl.BlockSpec((1, 4, STACK, 2 * BLOCK), lambda n: (n, 0, 0, 0)),
                   pl.BlockSpec((1, 4, STACK, 128), lambda n: (n, 0, 0, 0))),
        out_shape=(jax.ShapeDtypeStruct((SEQ, D_MIX), BF16), jax.ShapeDtypeStruct((SEQ, D_ATTN), F32),
                   jax.ShapeDtypeStruct((N_BLOCKS, 4, STACK, 2 * BLOCK), BF16),
                   jax.ShapeDtypeStruct((N_BLOCKS, 4, STACK, 128), F32)),
        scratch_shapes=[pltpu.VMEM((4, STACK, 2 * BLOCK), F32)],
        compiler_params=_params(dimension_semantics=("arbitrary",)),
    )(proj, proj, proj, proj, conv_full, sinks, norm_conv, norm_attn)


def _out_proj_loss(mixed, x, target, w_out_full, norm_final):
    tm = 256

    def body(mx_ref, x_ref, t_ref, w_ref, g_ref, dx2_ref, dx2b_ref, dmix_ref, gnf_ref, loss_ref):
        i = pl.program_id(0)
        w = w_ref[...]
        x2 = x_ref[...] + jnp.dot(mx_ref[...], w, preferred_element_type=F32)
        r = lax.rsqrt(jnp.mean(x2 * x2, axis=-1, keepdims=True) + RMS_EPS)
        xn = x2 * r
        g = g_ref[...]
        err = xn * g - t_ref[...]
        part = 0.5 * jnp.sum(jnp.mean(err * err, axis=-1, keepdims=True), axis=0, keepdims=True)
        dy = err * (1.0 / D_MODEL)
        gnf = jnp.sum(dy * xn, axis=0, keepdims=True)
        u = dy * g
        dx2 = r * (u - xn * jnp.mean(u * xn, axis=-1, keepdims=True))
        dx2_ref[...] = dx2
        dx2b = dx2.astype(BF16)
        dx2b_ref[...] = dx2b
        dmix_ref[...] = lax.dot_general(dx2b, w, _NT, preferred_element_type=F32)

        @pl.when(i == 0)
        def _():
            gnf_ref[...] = jnp.zeros_like(gnf_ref)
            loss_ref[...] = jnp.zeros_like(loss_ref)

        gnf_ref[...] += gnf
        loss_ref[...] += jnp.broadcast_to(part, loss_ref.shape)

    return pl.pallas_call(
        body, name="out_proj_loss", grid=(SEQ // tm,),
        in_specs=[pl.BlockSpec((tm, D_MIX), lambda i: (i, 0)), pl.BlockSpec((tm, D_MODEL), lambda i: (i, 0)),
                  pl.BlockSpec((tm, D_MODEL), lambda i: (i, 0)), pl.BlockSpec(memory_space=pltpu.VMEM),
                  pl.BlockSpec((1, D_MODEL), lambda i: (0, 0))],
        out_specs=(pl.BlockSpec((tm, D_MODEL), lambda i: (i, 0)), pl.BlockSpec((tm, D_MODEL), lambda i: (i, 0)),
                   pl.BlockSpec((tm, D_MIX), lambda i: (i, 0)),
                   pl.BlockSpec((1, D_MODEL), lambda i: (0, 0)), pl.BlockSpec((8, 128), lambda i: (0, 0))),
        out_shape=(jax.ShapeDtypeStruct((SEQ, D_MODEL), F32), jax.ShapeDtypeStruct((SEQ, D_MODEL), BF16),
                   jax.ShapeDtypeStruct((SEQ, D_MIX), F32),
                   jax.ShapeDtypeStruct((1, D_MODEL), F32), jax.ShapeDtypeStruct((8, 128), F32)),
        compiler_params=_params(dimension_semantics=("arbitrary",)),
    )(mixed, x, target, w_out_full, norm_final)


def _gated_norm_bwd(a, gain, t, dy):
    r = lax.rsqrt(jnp.mean(a * a, axis=-1, keepdims=True) + RMS_EPS)
    an = a * r
    sg = _sigmoid(t)
    dn = dy * (t * sg)
    dt = dy * (an * gain) * (sg * (1.0 + t * (1.0 - sg)))
    u = dn * gain
    da = r * (u - an * jnp.mean(u * an, axis=-1, keepdims=True))
    return da, dt, dn * an


def _mix_bwd(proj, dmixed, attn, probs, shares, conv_full, norm_conv, norm_attn):
    def body(pj_ref, kvp_ref, cch_ref, cuh_ref, dmx_ref, attn_ref, p_ref, ps_ref, cw_ref, gc_ref, ga_ref,
             dpj_ref, gslab_ref, dattn_scr, nxt_scr, dkv_scr, acc_scr):
        step = pl.program_id(0)
        n = N_BLOCKS - 1 - step
        pj = pj_ref

        @pl.when(step == 0)
        def _():
            gslab_ref[...] = jnp.zeros_like(gslab_ref)
            nxt_scr[...] = jnp.zeros_like(nxt_scr)
            dkv_scr[...] = jnp.zeros_like(dkv_scr)
            acc_scr[...] = jnp.zeros_like(acc_scr)

        zhalo = _conv_halo(cch_ref, cuh_ref, n)
        cw = (cw_ref[0:1, :], cw_ref[1:2, :], cw_ref[2:3, :])
        gain_c = gc_ref[...]
        row = lax.broadcasted_iota(jnp.int32, (CHUNK, D_CONV), 0)

        dco_after = nxt_scr[...]
        for r in reversed(range(N_CHUNKS)):
            rows = _chunk_rows(r)
            cc, cu, z, z1, z2, co = _conv_chunk(pj_ref, zhalo, cw, r)
            cb = pj_ref[rows, OFF_CB:OFF_CB + D_CONV]
            da, dgate, gterm = _gated_norm_bwd(cb * co, gain_c, pj_ref[rows, OFF_GC:OFF_GC + D_CONV],
                                               dmx_ref[rows, 0:D_CONV])
            dpj_ref[rows, OFF_GC:OFF_GC + D_CONV] = dgate.astype(BF16)
            dpj_ref[rows, OFF_CB:OFF_CB + D_CONV] = (da * co).astype(BF16)
            dco = da * cb
            dco1 = jnp.where(row >= CHUNK - 1, pltpu.roll(dco_after, CHUNK - 1, 0), pltpu.roll(dco, CHUNK - 1, 0))
            dco2 = jnp.where(row >= CHUNK - 2, pltpu.roll(dco_after, CHUNK - 2, 0), pltpu.roll(dco, CHUNK - 2, 0))
            dz = cw[2] * dco + cw[1] * dco1 + cw[0] * dco2
            dpj_ref[rows, OFF_CC:OFF_CC + D_CONV] = (dz * cu).astype(BF16)
            dpj_ref[rows, OFF_CU:OFF_CU + D_CONV] = (dz * cc).astype(BF16)
            acc_scr[ACC_NORM_CONV] += gterm
            acc_scr[ACC_CONV0] += dco * z2
            acc_scr[ACC_CONV0 + 1] += dco * z1
            acc_scr[ACC_CONV0 + 2] += dco * z
            dco_after = dco
        nxt_scr[...] = dco_after

        ks, vs = _kv_bands(pj, kvp_ref)
        gain_a = ga_ref[...]

        for r in range(N_CHUNKS):
            rows = _chunk_rows(r)
            da, dgate, gterm = _gated_norm_bwd(attn_ref[rows, :], gain_a, pj_ref[rows, OFF_GA:OFF_GA + D_ATTN],
                                               dmx_ref[rows, D_CONV:D_MIX])
            dpj_ref[rows, OFF_GA:OFF_GA + D_ATTN] = dgate.astype(BF16)
            dattn_scr[rows, :] = da
            acc_scr[ACC_NORM_ATTN] += gterm

        in_lo = lax.broadcasted_iota(jnp.int32, (128, 128), 0) < HEAD_DIM
        half_ones = (jnp.where(in_lo, 1.0, 0.0).astype(BF16), jnp.where(in_lo, 0.0, 1.0).astype(BF16))
        lane_s = lax.broadcasted_iota(jnp.int32, (1, D_MODEL), 1)
        gsink = jnp.zeros((1, D_MODEL), F32)
        dk_t, dv_t = [], []
        for j in range(2):
            q_stack = _q_stack(pj, j)
            do_f = jnp.concatenate([dattn_scr[:, _pair_cols(j, i, 0)] for i in range(PAIRS_PER_KV)], axis=0)
            o_f = jnp.concatenate([attn_ref[:, _pair_cols(j, i, 0)] for i in range(PAIRS_PER_KV)], axis=0)
            prod = (do_f * o_f).astype(BF16)
            deltas = [jnp.dot(prod, half_ones[e], preferred_element_type=F32) for e in range(2)]
            do_b = do_f.astype(BF16)
            q_t, do_t = q_stack.T, do_b.T
            dq, dk_j, dv_j = None, None, None
            for e in range(2):
                p = p_ref[0, 2 * j + e]
                dp = lax.dot_general(do_b, vs[j][e], _NT, preferred_element_type=F32)
                ds = []
                for i in range(PAIRS_PER_KV):
                    rows = slice(BLOCK * i, BLOCK * (i + 1))
                    delta = deltas[e][rows, :]
                    ds.append((p[rows, :].astype(F32) * (dp[rows, :] - jnp.concatenate([delta, delta], axis=1))).astype(BF16))
                    gs_h = -jnp.sum(ps_ref[0, 2 * j + e, rows, 0:1] * delta[:, 0:1], axis=0, keepdims=True)
                    gsink = gsink + jnp.where(lane_s == _head(j, i, e), gs_h, 0.0)
                ds = jnp.concatenate(ds, axis=0)
                t = jnp.dot(ds, ks[j][e], preferred_element_type=F32)
                dq = t if dq is None else dq + t
                half = slice(HEAD_DIM * e, HEAD_DIM * (e + 1))
                a = jnp.dot(q_t[half, :], ds, preferred_element_type=F32)
                b = jnp.dot(do_t[half, :], p, preferred_element_type=F32)
                dk_j = a if dk_j is None else dk_j + a
                dv_j = b if dv_j is None else dv_j + b
            for i in range(PAIRS_PER_KV):
                dpj_ref[:, _pair_cols(j, i, OFF_Q)] = (dq[BLOCK * i:BLOCK * (i + 1), :] * SCALE).astype(BF16)
            dk_t.append(dk_j)
            dv_t.append(dv_j)
        dk = jnp.concatenate(dk_t, axis=0).T
        dv = jnp.concatenate(dv_t, axis=0).T
        dpj_ref[:, OFF_K:OFF_K + D_KV] = (dk[BLOCK:, :] + dkv_scr[:, 0:D_KV]).astype(BF16)
        dpj_ref[:, OFF_V:OFF_V + D_KV] = (dv[BLOCK:, :] + dkv_scr[:, D_KV:2 * D_KV]).astype(BF16)
        dkv_scr[:, 0:D_KV] = dk[:BLOCK, :]
        dkv_scr[:, D_KV:2 * D_KV] = dv[:BLOCK, :]
        gslab_ref[ROW_SINKS:ROW_SINKS + 1, :] += gsink

        @pl.when(step == N_BLOCKS - 1)
        def _():
            for k, slab_row in ((ACC_NORM_CONV, ROW_NORM_CONV), (ACC_NORM_ATTN, ROW_NORM_ATTN), (ACC_CONV0, ROW_CONV0),
                                (ACC_CONV0 + 1, ROW_CONV0 + 1), (ACC_CONV0 + 2, ROW_CONV0 + 2)):
                gslab_ref[slab_row:slab_row + 1, :] = jnp.sum(acc_scr[k], axis=0, keepdims=True)

    per_block = BLOCK // HALO
    last = N_BLOCKS - 1
    return pl.pallas_call(
        body, name="mix_bwd", grid=(N_BLOCKS,),
        in_specs=[
            pl.BlockSpec((BLOCK, D_PROJ), lambda s: (last - s, 0)),
            pl.BlockSpec((BLOCK, 2 * D_KV), lambda s: (jnp.maximum(last - s - 1, 0), OFF_K // (2 * D_KV))),
            pl.BlockSpec((HALO, D_CONV), lambda s: (jnp.maximum((last - s) * per_block - 1, 0), OFF_CC // D_CONV)),
            pl.BlockSpec((HALO, D_CONV), lambda s: (jnp.maximum((last - s) * per_block - 1, 0), OFF_CU // D_CONV)),
            pl.BlockSpec((BLOCK, D_MIX), lambda s: (last - s, 0)),
            pl.BlockSpec((BLOCK, D_ATTN), lambda s: (last - s, 0)),
            pl.BlockSpec((1, 4, STACK, 2 * BLOCK), lambda s: (last - s, 0, 0, 0)),
            pl.BlockSpec((1, 4, STACK, 128), lambda s: (last - s, 0, 0, 0)),
            pl.BlockSpec((8, D_CONV), lambda s: (0, 0)),
            pl.BlockSpec((1, D_CONV), lambda s: (0, 0)),
            pl.BlockSpec((1, D_ATTN), lambda s: (0, 0)),
        ],
        out_specs=(pl.BlockSpec((BLOCK, D_PROJ), lambda s: (last - s, 0)),
                   pl.BlockSpec((8, D_MODEL), lambda s: (0, 0))),
        out_shape=(jax.ShapeDtypeStruct((SEQ, D_PROJ), BF16), jax.ShapeDtypeStruct((8, D_MODEL), F32)),
        scratch_shapes=[pltpu.VMEM((BLOCK, D_ATTN), F32), pltpu.VMEM((CHUNK, D_CONV), F32),
                        pltpu.VMEM((BLOCK, 2 * D_KV), F32), pltpu.VMEM((N_ACC, CHUNK, D_MODEL), F32)],
        compiler_params=_params(dimension_semantics=("arbitrary",)),
    )(proj, proj, proj, proj, dmixed, attn, probs, shares, conv_full, norm_conv, norm_attn)


def _in_bwd_rs(dproj, w_full, x, dx2, norm_in, dw_in_chip, gslab, gnf, loss_part):
    tm = 256
    steps = SEQ // tm

    def body(dp_ref, w_hbm, x_ref, dx2_ref, g_ref, dwi_ref, gs_ref, gnf_ref, lp_ref, gx_ref, gwin_ref, gsum_ref,
             gni_scr, own, d2d, ici, myslab, slabs, w_ref, send_sems, recv_sems, local_sems):
        i = pl.program_id(0)
        rs_start, rs_finish = _ici_sum(dwi_ref, own, d2d, ici, send_sems, recv_sems, local_sems)
        slab_start, slab_finish = _slab_sum(myslab, slabs, send_sems, recv_sems, N_ICI_SUM_SEMS)

        @pl.when(i == 0)
        def _():
            gni_scr[...] = jnp.zeros_like(gni_scr)
            rs_start()
            w_load = pltpu.make_async_copy(w_hbm, w_ref, local_sems.at[1])
            w_load.start()
            w_load.wait()

        dh = jnp.dot(dp_ref[...], w_ref[...], preferred_element_type=F32)
        xv = x_ref[...]
        r = lax.rsqrt(jnp.mean(xv * xv, axis=-1, keepdims=True) + RMS_EPS)
        xn = xv * r
        u = dh * g_ref[...]
        gx_ref[...] = dx2_ref[...] + r * (u - xn * jnp.mean(u * xn, axis=-1, keepdims=True))
        gni_scr[...] += jnp.sum(dh * xn, axis=0, keepdims=True)

        @pl.when(i == steps - 1)
        def _():
            row = lax.broadcasted_iota(jnp.int32, (8, D_MODEL), 0)
            lane = lax.broadcasted_iota(jnp.int32, (8, D_MODEL), 1)
            slab = jnp.where(row == ROW_NORM_IN, gni_scr[...], jnp.where(row == ROW_NORM_FINAL, gnf_ref[...], gs_ref[...]))
            myslab[...] = jnp.where((row == ROW_SINKS) & (lane == LOSS_LANE), lp_ref[0:1, 0:1], slab)
            slab_start()
            gwin_ref[...] = rs_finish()
            gsum_ref[...] = slab_finish()

    const = lambda i: (0, 0)
    return pl.pallas_call(
        body, name="in_bwd", grid=(steps,),
        in_specs=[pl.BlockSpec((tm, D_PROJ), lambda i: (i, 0)), pl.BlockSpec(memory_space=pl.ANY),
                  pl.BlockSpec((tm, D_MODEL), lambda i: (i, 0)), pl.BlockSpec((tm, D_MODEL), lambda i: (i, 0)),
                  pl.BlockSpec((1, D_MODEL), const), pl.BlockSpec(memory_space=pl.ANY),
                  pl.BlockSpec((8, D_MODEL), const), pl.BlockSpec((1, D_MODEL), const), pl.BlockSpec((8, 128), const)],
        out_specs=(pl.BlockSpec((tm, D_MODEL), lambda i: (i, 0)), pl.BlockSpec((SHARD_IN, D_MODEL), const),
                   pl.BlockSpec((8, D_MODEL), const)),
        out_shape=(jax.ShapeDtypeStruct((SEQ, D_MODEL), F32), jax.ShapeDtypeStruct((SHARD_IN, D_MODEL), F32),
                   jax.ShapeDtypeStruct((8, D_MODEL), F32)),
        scratch_shapes=[pltpu.VMEM((1, D_MODEL), F32), pltpu.VMEM((SHARD_IN, D_MODEL), BF16),
                        pltpu.VMEM((SHARD_IN, D_MODEL), BF16), pltpu.VMEM((2, SHARD_IN, D_MODEL), BF16),
                        pltpu.VMEM((8, D_MODEL), F32), pltpu.VMEM((N_DEV, 8, D_MODEL), F32),
                        pltpu.VMEM((D_PROJ, D_MODEL), BF16),
                        pltpu.SemaphoreType.DMA((N_ICI_SUM_SEMS + 7,)), pltpu.SemaphoreType.DMA((N_ICI_SUM_SEMS + 7,)),
                        pltpu.SemaphoreType.DMA((2,))],
        compiler_params=_params(dimension_semantics=("arbitrary",)),
    )(dproj, w_full, x, dx2, norm_in, dw_in_chip, gslab, gnf, loss_part)


def _dw_rs(mixed, dx2b, dproj, h, table):
    tn_out, tn = 2 * SHARD_OUT, IN_PROJ_TILE
    out_steps, in_steps = D_MIX // tn_out, D_PROJ // tn
    steps = out_steps + in_steps
    out_order = (DG, NX, NY, OWN)

    def out_tile(i):
        chip = 2 * lax.axis_index("x") + lax.axis_index("y")
        return jnp.bitwise_xor(chip, (out_steps - 1) - jnp.minimum(i, out_steps - 1))

    def in_tile(table_ref, i):
        return _dw_entry(table_ref, jnp.maximum(i - out_steps, 0))

    def body(table_ref, mx_ref, dxb_ref, a_ref, b_ref, chip_ref, gwo_ref, dwo, dwt, d2d_in, via, own, d2d, ici,
             send_sems, recv_sems, local_sems):
        i = pl.program_id(0)
        rs_start, rs_forward, rs_finish = _shard_sum(dwo, own, d2d, ici, send_sems, recv_sems, local_sems)
        before_tile, after_tiles, chip_finish = _chip_sum(
            dwt, d2d_in, via, chip_ref, lambda k: _dw_entry(table_ref, in_steps + k), send_sems, recv_sems, local_sems,
            N_SHARD_SUM_SEMS, 4)

        for j, k in enumerate(out_order):
            @pl.when(i == j + 1)
            def _():
                rs_start(k)

            if k != OWN:
                @pl.when(i == j + 2)
                def _():
                    rs_forward(k)

        @pl.when(i < out_steps)
        def _():
            tile = lax.dot_general(mx_ref[...], dxb_ref[...], _TN, preferred_element_type=F32).astype(BF16)
            for core in range(2):
                dwo[2 * out_tile(i) + core] = tile[SHARD_OUT * core:SHARD_OUT * (core + 1), :]

        @pl.when(i >= out_steps)
        def _():
            before_tile(i - out_steps)
            tile = lax.dot_general(a_ref[...], b_ref[...], _TN, preferred_element_type=F32).astype(BF16)
            dwt[pl.ds(pl.multiple_of(in_tile(table_ref, i) * tn, tn), tn), :] = tile

        @pl.when(i == steps - 1)
        def _():
            after_tiles()
            gwo_ref[...] = rs_finish()
            chip_finish()

    vmem = pl.BlockSpec(memory_space=pltpu.VMEM)
    grid_spec = pltpu.PrefetchScalarGridSpec(
        num_scalar_prefetch=1, grid=(steps,),
        in_specs=[pl.BlockSpec((SEQ, tn_out), lambda i, table_ref: (0, out_tile(i))), vmem,
                  pl.BlockSpec((SEQ, tn), lambda i, table_ref: (0, in_tile(table_ref, i))), vmem],
        out_specs=(pl.BlockSpec(memory_space=pl.ANY), pl.BlockSpec((SHARD_OUT, D_MODEL), lambda i, table_ref: (0, 0))),
        scratch_shapes=[pltpu.VMEM((N_DEV, SHARD_OUT, D_MODEL), BF16),
                        pltpu.VMEM((D_PROJ, D_MODEL), BF16), pltpu.VMEM((3, SHARD_IN, D_MODEL), BF16),
                        pltpu.VMEM((2, HALF_IN, D_MODEL), BF16),
                        *_shard_sum_scratch(SHARD_OUT),
                        pltpu.SemaphoreType.DMA((N_SHARD_SUM_SEMS + N_CHIP_SUM_SEMS,)),
                        pltpu.SemaphoreType.DMA((N_SHARD_SUM_SEMS + N_CHIP_SUM_SEMS,)),
                        pltpu.SemaphoreType.DMA((8,))])
    return pl.pallas_call(
        body, name="dw", grid_spec=grid_spec,
        out_shape=(jax.ShapeDtypeStruct((4, SHARD_IN, D_MODEL), BF16), jax.ShapeDtypeStruct((SHARD_OUT, D_MODEL), F32)),
        compiler_params=_params(dimension_semantics=("arbitrary",)),
    )(table, mixed, dx2b, dproj, h)


def _adam_all(big_in, big_out, gsum, small, grad_x):
    steps = 4
    tr_in, tr_out = SHARD_IN // steps, SHARD_OUT // steps

    def body(*refs):
        ins, outs = refs[:8 + 1 + 18 + 1], refs[8 + 1 + 18 + 1:]
        i = pl.program_id(0)
        outs[33][...] = ins[27][...]
        for b in range(2):
            w_ref, g_ref, m_ref, v_ref = ins[4 * b:4 * b + 4]
            g = g_ref[...]
            delta, mn, vn = _adamw(w_ref[...], g, m_ref[...], v_ref[...])
            for ref, val in zip(outs[4 * b:4 * b + 4], (g, delta, mn, vn)):
                ref[...] = val

        @pl.when(i == 0)
        def _():
            gsum = ins[8][...]
            idx = _slot(lax.axis_index("x"), lax.axis_index("y"), lax.axis_index("c"))
            cg = jnp.zeros((3, SHARD_CONV), F32)
            for d in range(N_DEV):
                cg = jnp.where(idx == d, gsum[ROW_CONV0:ROW_CONV0 + 3, d * SHARD_CONV:(d + 1) * SHARD_CONV], cg)
            grads = (gsum[ROW_NORM_IN:ROW_NORM_IN + 1], gsum[ROW_SINKS:ROW_SINKS + 1, 0:N_Q_HEADS],
                     gsum[ROW_NORM_CONV:ROW_NORM_CONV + 1], gsum[ROW_NORM_ATTN:ROW_NORM_ATTN + 1],
                     gsum[ROW_NORM_FINAL:ROW_NORM_FINAL + 1], cg)
            for s, g in enumerate(grads):
                at = (slice(None), 0, slice(None)) if s == 5 else (slice(None), slice(None))
                w_ref, m_ref, v_ref = ins[9 + 3 * s:12 + 3 * s]
                delta, mn, vn = _adamw(w_ref[at], g, m_ref[at], v_ref[at])
                for ref, val in zip(outs[8 + 4 * s:12 + 4 * s], (g, delta, mn, vn)):
                    ref[at] = val
            outs[32][...] = gsum[ROW_SINKS:ROW_SINKS + 1, LOSS_LANE:LOSS_LANE + 1]

    const = lambda i: (0, 0)
    rows = lambda i: (i, 0)
    whole = lambda shape: pl.BlockSpec(shape, lambda i: (0,) * len(shape))
    small_shapes = [a.shape for a in small[::3]]
    in_specs = ([pl.BlockSpec((tr_in, D_MODEL), rows)] * 4 + [pl.BlockSpec((tr_out, D_MODEL), rows)] * 4
                + [pl.BlockSpec((8, D_MODEL), const)] + [whole(a.shape) for a in small]
                + [pl.BlockSpec((SEQ // steps, D_MODEL), rows)])
    out_specs = ([pl.BlockSpec((tr_in, D_MODEL), rows)] * 4 + [pl.BlockSpec((tr_out, D_MODEL), rows)] * 4
                 + [whole(s) for s in small_shapes for _ in range(4)] + [pl.BlockSpec((1, 1), const)]
                 + [pl.BlockSpec((SEQ // steps, D_MODEL), rows)])
    out_shape = ([jax.ShapeDtypeStruct((SHARD_IN, D_MODEL), F32)] * 4 + [jax.ShapeDtypeStruct((SHARD_OUT, D_MODEL), F32)] * 4
                 + [jax.ShapeDtypeStruct(s, F32) for s in small_shapes for _ in range(4)]
                 + [jax.ShapeDtypeStruct((1, 1), F32), jax.ShapeDtypeStruct((SEQ, D_MODEL), F32)])
    outs = pl.pallas_call(
        body, name="adam", grid=(steps,), in_specs=in_specs, out_specs=tuple(out_specs), out_shape=tuple(out_shape),
        compiler_params=_params(dimension_semantics=("arbitrary",)),
    )(*big_in, *big_out, gsum, *small, grad_x)
    return outs[0:4], outs[4:8], [outs[8 + 4 * s:12 + 4 * s] for s in range(6)], outs[32], outs[33]


def _rows_first(a):
    return jnp.transpose(a, (1, 0, 2))


def kernel(x, norm_in, w_in, conv_w, attn_sinks, norm_conv_out, norm_attn_out, w_out, norm_final, loss_target, m_norm_in, m_w_in, m_conv_w, m_attn_sinks, m_norm_conv_out, m_norm_attn_out, m_w_out, m_norm_final, v_norm_in, v_w_in, v_conv_w, v_attn_sinks, v_norm_conv_out, v_norm_attn_out, v_w_out, v_norm_final):
    x2d = x.reshape(SEQ, D_MODEL)
    target = loss_target.reshape(SEQ, D_MODEL)
    nf = norm_final.reshape(1, D_MODEL)

    w_in_t, m_w_in_t, v_w_in_t = w_in[0].T, m_w_in[0].T, v_w_in[0].T
    tiles = jnp.asarray(TILE_ORDER, jnp.int32).reshape(-1)
    w_in_full, h, proj, g_out, conv_full = _gather_in_proj(x2d, norm_in, w_in_t, w_out[0], _rows_first(conv_w), tiles)
    sinks = attn_sinks.reshape(N_Q_HEADS)

    mixed, attn, probs, shares = _mix_fwd(proj, conv_full, sinks, norm_conv_out, norm_attn_out)
    dx2, dx2b, dmixed, gnf, loss_part = _out_proj_loss(mixed, x2d, target, g_out.reshape(D_MIX, D_MODEL), nf)
    dproj, gslab = _mix_bwd(proj, dmixed, attn, probs, shares, conv_full, norm_conv_out, norm_attn_out)
    dw_in_chip, g_w_out = _dw_rs(mixed, dx2b, dproj, h, jnp.asarray(DW_TABLE, jnp.int32).reshape(-1))
    grad_x, g_w_in, gsum = _in_bwd_rs(dproj, w_in_full, x2d, dx2, norm_in, dw_in_chip, gslab, gnf, loss_part)

    small = (norm_in, m_norm_in, v_norm_in, attn_sinks, m_attn_sinks, v_attn_sinks,
             norm_conv_out, m_norm_conv_out, v_norm_conv_out, norm_attn_out, m_norm_attn_out, v_norm_attn_out,
             nf, m_norm_final.reshape(1, D_MODEL), v_norm_final.reshape(1, D_MODEL),
             _rows_first(conv_w), _rows_first(m_conv_w), _rows_first(v_conv_w))
    big_in, big_out, (s_ni, s_sk, s_nc, s_na, s_nf, s_cv), loss, grad_x = _adam_all(
        (w_in_t, g_w_in, m_w_in_t, v_w_in_t), (w_out[0], g_w_out, m_w_out[0], v_w_out[0]), gsum, small, grad_x)

    def leaves(k):
        return (s_ni[k], big_in[k].T[None], jnp.transpose(s_cv[k], (1, 0, 2)), s_sk[k], s_nc[k], s_na[k], big_out[k][None],
                s_nf[k].reshape(D_MODEL))

    return (loss.reshape(()), grad_x.reshape(1, SEQ, D_MODEL), *leaves(0), *leaves(1), *leaves(2), *leaves(3))
```

```python
import jax
import jax.numpy as jnp
from jax import lax
from jax.experimental import pallas as pl
from jax.experimental.pallas import tpu as pltpu

F32 = jnp.float32
BF16 = jnp.bfloat16
MESH = pl.DeviceIdType.MESH

N_DEV = 8
SEQ = 2048
D_MODEL = 1024
D_CONV = 1024
D_ATTN = 1024
D_KV = 128
HEAD_DIM = 64
N_Q_HEADS = 16
N_PAIRS = N_Q_HEADS // 2
PAIRS_PER_KV = N_PAIRS // 2
D_MIX = D_CONV + D_ATTN
D_PROJ = 6400
SHARD_IN = D_PROJ // N_DEV
SHARD_OUT = D_MIX // N_DEV
SHARD_CONV = D_CONV // N_DEV
OFF_CB, OFF_CC, OFF_CU, OFF_GC, OFF_Q, OFF_K, OFF_V, OFF_GA = 0, 1024, 2048, 3072, 4096, 5120, 5248, 5376
BLOCK = 128
N_BLOCKS = SEQ // BLOCK
HALO = 8
CHUNK = 16
N_CHUNKS = BLOCK // CHUNK
RMS_EPS = 1e-5
NEG = -1e30
SCALE = HEAD_DIM ** -0.5
SLOPES = tuple(2.0 ** (-8.0 * (h + 1) / N_Q_HEADS) for h in range(N_Q_HEADS))

ADAM_LR = 0.001
ADAM_B1 = 0.9
ADAM_B2 = 0.999
ADAM_EPS = 1e-08
ADAM_WD = 0.01
ADAM_STEP = 10

ROW_NORM_IN, ROW_NORM_CONV, ROW_NORM_ATTN, ROW_NORM_FINAL, ROW_CONV0, ROW_SINKS = 0, 1, 2, 3, 4, 7
LOSS_LANE = N_Q_HEADS
ACC_NORM_CONV, ACC_NORM_ATTN, ACC_CONV0, N_ACC = 0, 1, 2, 5

VMEM_LIMIT = 56 * 1024 * 1024

_NT = (((1,), (1,)), ((), ()))
_TN = (((0,), (0,)), ((), ()))


def _params(**kw):
    return pltpu.CompilerParams(vmem_limit_bytes=VMEM_LIMIT, **kw)


def _adamw(w, g, m, v):
    m = ADAM_B1 * m + (1.0 - ADAM_B1) * g
    v = ADAM_B2 * v + (1.0 - ADAM_B2) * (g * g)
    m_hat = m / (1.0 - ADAM_B1 ** ADAM_STEP)
    v_hat = v / (1.0 - ADAM_B2 ** ADAM_STEP)
    delta = -ADAM_LR * (m_hat / (jnp.sqrt(v_hat) + ADAM_EPS) + ADAM_WD * w)
    return delta, m, v


def _sigmoid(t):
    return 1.0 / (1.0 + jnp.exp(-t))


def _slot(px, py, pc):
    return 4 * px + 2 * py + pc


OWN, NX, NY, DG = range(4)
HALF_IN = SHARD_IN // 2
N_GATHER_KINDS = 13
W_OUT_KINDS = N_GATHER_KINDS + 7


IN_PROJ_TILE = 640
TILE_ORDER = ((0, 1, 2, 3, 4, 5, 6, 7, 8, 9), (3, 4, 0, 1, 2, 8, 9, 5, 6, 7),
              (5, 6, 0, 1, 7, 8, 9, 2, 3, 4), (8, 9, 3, 4, 5, 6, 7, 0, 1, 2))
TILES_OWN, TILES_NEIGHBOURS = 2, 7


def _tile(table_ref, p):
    chip = 2 * lax.axis_index("x") + lax.axis_index("y")
    return table_ref[chip * len(TILE_ORDER[0]) + p]


DW_TILE_ORDER = tuple(tuple(reversed(row)) for row in TILE_ORDER)


def _tiles_until_complete(chip, owner):
    lo, hi = owner * 2 * SHARD_IN, (owner + 1) * 2 * SHARD_IN
    touching = [t for t in range(len(TILE_ORDER[0])) if t * IN_PROJ_TILE < hi and (t + 1) * IN_PROJ_TILE > lo]
    return 1 + max(DW_TILE_ORDER[chip].index(t) for t in touching)


DW_TABLE = tuple(DW_TILE_ORDER[chip] + tuple(_tiles_until_complete(chip, chip ^ flip) for flip in (0, 2, 1, 3))
                 for chip in range(4))


def _dw_entry(table_ref, p):
    chip = 2 * lax.axis_index("x") + lax.axis_index("y")
    return table_ref[chip * len(DW_TABLE[0]) + p]


def _gather_in_proj(x, norm_in, w_in_sh, w_out_sh, conv_sh, tiles):
    tn = IN_PROJ_TILE
    steps = D_PROJ // tn
    tm = 256

    def body(tiles_ref, x_hbm, g_ref, win_ref, wout_ref, cv_ref, wt_ref, h_ref, proj_ref, gout_ref, conv_ref,
             gin_ref, gcv_ref, wob_ref, x_ref, send_sems, recv_sems, local_sems):
        p = pl.program_id(0)
        local_sem = local_sems.at[0]
        x, y, c = lax.axis_index("x"), lax.axis_index("y"), lax.axis_index("c")
        me, sibling = (x, y, c), (x, y, 1 - c)
        nx, ny, dg = (1 - x, y, c), (x, 1 - y, c), (1 - x, 1 - y, c)

        def other(dev):
            return (dev[0], dev[1], 1 - dev[2])

        def shard(dev):
            return gin_ref.at[pl.ds(pl.multiple_of(_slot(*dev) * SHARD_IN, 16), SHARD_IN), :]

        def half(dev, h):
            return gin_ref.at[pl.ds(pl.multiple_of(_slot(*dev) * SHARD_IN + h * HALF_IN, 16), HALF_IN), :]

        def rc(ref, k, to):
            return pltpu.make_async_remote_copy(src_ref=ref, dst_ref=ref, send_sem=send_sems.at[k],
                                                recv_sem=recv_sems.at[k], device_id=to, device_id_type=MESH)

        def cv(k, dev, to):
            s = _slot(*dev)
            return pltpu.make_async_remote_copy(src_ref=gcv_ref.at[s], dst_ref=gcv_ref.at[s],
                                                send_sem=send_sems.at[N_GATHER_KINDS + k],
                                                recv_sem=recv_sems.at[N_GATHER_KINDS + k], device_id=to, device_id_type=MESH)

        def own_copies():
            return [rc(shard(me), 0, sibling),
                    rc(half(me, 0), 1, nx), rc(half(me, 1), 2, nx),
                    rc(half(me, 1), 4, ny), rc(half(me, 0), 3, ny),
                    cv(0, me, sibling)] + [cv(1 + j, me, peer) for j, peer in enumerate((nx, ny, dg))]

        def pass_on(dev, h, k_in, k_ici, k_d2d, half=half, base=0):
            rc(half(dev, h), base + k_in, me).wait_recv()
            if k_ici is not None:
                rc(half(dev, h), base + k_ici, ny if dev is nx else nx).start()
            rc(half(dev, h), base + k_d2d, sibling).start()

        def out_half(dev, h):
            return gout_ref.at[_slot(*dev), pl.ds(h * (SHARD_OUT // 2), SHARD_OUT // 2), :]

        def own_out_copies():
            src = lambda h: wob_ref.at[pl.ds(h * (SHARD_OUT // 2), SHARD_OUT // 2), :]

            def send(ref, dst, k, to):
                return pltpu.make_async_remote_copy(src_ref=ref, dst_ref=dst, send_sem=send_sems.at[W_OUT_KINDS + k],
                                                    recv_sem=recv_sems.at[W_OUT_KINDS + k], device_id=to, device_id_type=MESH)

            return [send(wob_ref, gout_ref.at[_slot(*me)], 0, sibling),
                    send(src(0), out_half(me, 0), 1, nx), send(src(1), out_half(me, 1), 2, nx),
                    send(src(1), out_half(me, 1), 4, ny), send(src(0), out_half(me, 0), 3, ny)]

        def own_out_local():
            return pltpu.make_async_copy(wob_ref, gout_ref.at[_slot(*me)], local_sems.at[1])

        @pl.when(p == 0)
        def _():
            gin_ref[pl.ds(pl.multiple_of(_slot(*me) * SHARD_IN, 16), SHARD_IN), :] = win_ref[...].astype(BF16)
            gcv_ref[_slot(*me)] = jnp.zeros((8, SHARD_CONV), F32)
            gcv_ref[_slot(*me), 0:3, :] = cv_ref[:, 0, :]
            for cp in own_copies():
                cp.start()
            wob_ref[...] = wout_ref[...].astype(BF16)
            x_load = pltpu.make_async_copy(x_hbm, x_ref, local_sems.at[2])
            x_load.start()
            x_load.wait()
            for t in range(SEQ // tm):
                xv = x_ref[tm * t:tm * (t + 1), :]
                r = lax.rsqrt(jnp.mean(xv * xv, axis=-1, keepdims=True) + RMS_EPS)
                h_ref[tm * t:tm * (t + 1), :] = (xv * r * g_ref[...]).astype(BF16)
            rc(shard(sibling), 0, me).wait_recv()

        @pl.when(p == TILES_OWN)
        def _():
            for args in ((nx, 0, 1, 5, 7), (ny, 1, 4, 6, 10), (nx, 1, 2, None, 8), (ny, 0, 3, None, 9)):
                pass_on(*args)
            for j, peer in enumerate((nx, ny, dg)):
                cv(1 + j, peer, me).wait_recv()
                cv(4 + j, peer, sibling).start()
            for (dev, h), k in (((nx, 0), 7), ((nx, 1), 8), ((ny, 0), 9), ((ny, 1), 10)):
                rc(half(other(dev), h), k, me).wait_recv()
            own_out_local().start()
            for cp in own_out_copies():
                cp.start()

        @pl.when(p == TILES_NEIGHBOURS - 1)
        def _():
            pass_on(dg, 0, 5, None, 11)
            pass_on(dg, 1, 6, None, 12)

        @pl.when(p == TILES_NEIGHBOURS)
        def _():
            for (dev, h), k in (((dg, 0), 11), ((dg, 1), 12)):
                rc(half(other(dev), h), k, me).wait_recv()
            pltpu.make_async_copy(gin_ref, wt_ref, local_sem).start()

        @pl.when(p == steps - 2)
        def _():
            for args in ((nx, 0, 1, 5, 7), (ny, 1, 4, 6, 10), (nx, 1, 2, None, 8), (ny, 0, 3, None, 9)):
                pass_on(*args, half=out_half, base=W_OUT_KINDS)

        w = gin_ref[pl.ds(pl.multiple_of(_tile(tiles_ref, p) * tn, tn), tn), :]
        proj_ref[...] = lax.dot_general(h_ref[...], w, _NT, preferred_element_type=F32)

        @pl.when(p == steps - 1)
        def _():
            cv(0, sibling, me).wait_recv()
            for j, peer in enumerate((nx, ny, dg)):
                cv(4 + j, other(peer), me).wait_recv()
            for d in range(N_DEV):
                conv_ref[:, d * SHARD_CONV:(d + 1) * SHARD_CONV] = gcv_ref[d]
            relayed = [rc(half(nx, 0), 5, ny), rc(half(ny, 1), 6, nx)]
            relayed += [rc(half(dev, h), k, sibling) for (dev, h), k in
                        (((nx, 0), 7), ((nx, 1), 8), ((ny, 0), 9), ((ny, 1), 10), ((dg, 0), 11), ((dg, 1), 12))]
            relayed += [cv(4 + j, peer, sibling) for j, peer in enumerate((nx, ny, dg))]
            for cp in own_copies() + relayed:
                cp.wait_send()
            pltpu.make_async_copy(gin_ref, wt_ref, local_sem).wait()
            pass_on(dg, 0, 5, None, 11, half=out_half, base=W_OUT_KINDS)
            pass_on(dg, 1, 6, None, 12, half=out_half, base=W_OUT_KINDS)
            rc(gout_ref.at[_slot(*sibling)], W_OUT_KINDS, me).wait_recv()
            out_relayed = [rc(out_half(nx, 0), W_OUT_KINDS + 5, ny), rc(out_half(ny, 1), W_OUT_KINDS + 6, nx)]
            for (dev, h), k in (((nx, 0), 7), ((nx, 1), 8), ((ny, 0), 9), ((ny, 1), 10), ((dg, 0), 11), ((dg, 1), 12)):
                rc(out_half(other(dev), h), W_OUT_KINDS + k, me).wait_recv()
                out_relayed.append(rc(out_half(dev, h), W_OUT_KINDS + k, sibling))
            for cp in own_out_copies() + out_relayed:
                cp.wait_send()
            own_out_local().wait()

    vmem = pl.BlockSpec(memory_space=pltpu.VMEM)
    grid_spec = pltpu.PrefetchScalarGridSpec(
        num_scalar_prefetch=1, grid=(steps,),
        in_specs=[pl.BlockSpec(memory_space=pl.ANY), vmem, vmem, vmem, vmem],
        out_specs=(pl.BlockSpec(memory_space=pl.ANY), vmem,
                   pl.BlockSpec((SEQ, tn), lambda p, tiles_ref: (0, _tile(tiles_ref, p))),
                   pl.BlockSpec(memory_space=pl.ANY), vmem),
        scratch_shapes=[pltpu.VMEM((D_PROJ, D_MODEL), BF16), pltpu.VMEM((N_DEV, 8, SHARD_CONV), F32),
                        pltpu.VMEM((SHARD_OUT, D_MODEL), BF16), pltpu.VMEM((SEQ, D_MODEL), F32),
                        pltpu.SemaphoreType.DMA((W_OUT_KINDS + N_GATHER_KINDS,)),
                        pltpu.SemaphoreType.DMA((W_OUT_KINDS + N_GATHER_KINDS,)),
                        pltpu.SemaphoreType.DMA((3,))])
    return pl.pallas_call(
        body, name="gather_in_proj", grid_spec=grid_spec,
        out_shape=(jax.ShapeDtypeStruct((D_PROJ, D_MODEL), BF16), jax.ShapeDtypeStruct((SEQ, D_MODEL), BF16),
                   jax.ShapeDtypeStruct((SEQ, D_PROJ), F32), jax.ShapeDtypeStruct((N_DEV, SHARD_OUT, D_MODEL), BF16),
                   jax.ShapeDtypeStruct((8, D_CONV), F32)),
        compiler_params=_params(dimension_semantics=("arbitrary",)),
    )(tiles, x, norm_in, w_in_sh, w_out_sh, conv_sh)


def _shard_sum(src, own, d2d, ici, send_sems, recv_sems, local_sems, base=0):
    x, y, c = lax.axis_index("x"), lax.axis_index("y"), lax.axis_index("c")
    sibling = (x, y, 1 - c)
    chips = [(x, y), (1 - x, y), (x, 1 - y), (1 - x, 1 - y)]

    def rcopy(s, d, k, to):
        return pltpu.make_async_remote_copy(src_ref=s, dst_ref=d, send_sem=send_sems.at[base + k],
                                            recv_sem=recv_sems.at[base + k], device_id=to, device_id_type=MESH)

    def mine(k):
        return pltpu.make_async_copy(src.at[_slot(*chips[k], c)], own.at[k], local_sems.at[k])

    def to_sibling(k):
        return rcopy(src.at[_slot(*chips[k], 1 - c)], d2d.at[k], k, sibling)

    def to_chip(k):
        return rcopy(own.at[k], ici.at[k - 1], 3 + k, (*chips[k], c))

    def start(k):
        mine(k).start()
        to_sibling(k).start()

    def forward(k):
        mine(k).wait()
        to_sibling(k).wait_recv()
        own[k] = (own[k].astype(F32) + d2d[k].astype(F32)).astype(BF16)
        to_chip(k).start()

    def finish():
        mine(0).wait()
        to_sibling(0).wait_recv()
        acc = own[0].astype(F32) + d2d[0].astype(F32)
        for k in range(1, 4):
            to_chip(k).wait_recv()
            acc = acc + ici[k - 1].astype(F32)
        for k in range(4):
            to_sibling(k).wait_send()
        for k in range(1, 4):
            to_chip(k).wait_send()
        return acc

    return start, forward, finish


def _shard_sum_scratch(rows):
    return [pltpu.VMEM((4, rows, D_MODEL), BF16), pltpu.VMEM((4, rows, D_MODEL), BF16),
            pltpu.VMEM((3, rows, D_MODEL), BF16)]


N_SHARD_SUM_SEMS = 7


N_CHIP_SUM_SEMS = 5


def _chip_sum(dwt, d2d, via, out_hbm, tiles_until, send_sems, recv_sems, local_sems, base, local_base):
    x, y, c = lax.axis_index("x"), lax.axis_index("y"), lax.axis_index("c")
    sibling, nx, ny = (x, y, 1 - c), (1 - x, y, c), (x, 1 - y, c)
    chips = [(x, y), (1 - x, y), (x, 1 - y), (1 - x, 1 - y)]

    def shard(s):
        return dwt.at[pl.ds(pl.multiple_of(s * SHARD_IN, 16), SHARD_IN), :]

    def half(ref, h):
        return ref.at[pl.ds(h * HALF_IN, HALF_IN), :]

    def rc(s, d, k, to):
        return pltpu.make_async_remote_copy(src_ref=s, dst_ref=d, send_sem=send_sems.at[base + k],
                                            recv_sem=recv_sems.at[base + k], device_id=to, device_id_type=MESH)

    def to_sibling(k):
        return rc(shard(_slot(*chips[k], 1 - c)), d2d.at[k - 1], k - 1, sibling)

    for_dg = (lambda: rc(half(d2d.at[DG - 1], 0), via.at[0], 3, nx), lambda: rc(half(d2d.at[DG - 1], 1), via.at[1], 4, ny))

    def save(k):
        return pltpu.make_async_copy(d2d.at[k - 1], out_hbm.at[k], local_sems.at[local_base + k])

    own_saves = (lambda: pltpu.make_async_copy(shard(_slot(x, y, c)), out_hbm.at[OWN], local_sems.at[local_base]),
                 lambda: pltpu.make_async_copy(shard(_slot(x, y, 1 - c)), out_hbm.at[3], local_sems.at[local_base + 3]))

    def before_tile(n):
        for k in (NX, NY, DG):
            @pl.when(tiles_until(k) == n)
            def _():
                to_sibling(k).start()

            @pl.when(tiles_until(k) + 1 == n)
            def _():
                to_sibling(k).wait_recv()
                d2d[k - 1] = (shard(_slot(*chips[k], c))[...].astype(F32) + d2d[k - 1].astype(F32)).astype(BF16)
                if k == DG:
                    for cp in for_dg:
                        cp().start()

    def after_tiles():
        for cp in own_saves:
            cp().start()

    def finish():
        for k, h in ((NY, 0), (NX, 1)):
            for_dg[h]().wait_recv()
            rows = pl.ds(h * HALF_IN, HALF_IN)
            d2d[k - 1, rows, :] = (d2d[k - 1, rows, :].astype(F32) + via[h].astype(F32)).astype(BF16)
            save(k).start()
        for cp in own_saves + (lambda: save(NX), lambda: save(NY)):
            cp().wait()
        for cp in (lambda: to_sibling(NX), lambda: to_sibling(NY), lambda: to_sibling(DG)) + for_dg:
            cp().wait_send()

    return before_tile, after_tiles, finish


N_ICI_SUM_SEMS = 3


def _ici_sum(src, own, d2d, ici, send_sems, recv_sems, local_sems, base=0):
    x, y, c = lax.axis_index("x"), lax.axis_index("y"), lax.axis_index("c")

    def rc(s, d, k, to):
        return pltpu.make_async_remote_copy(src_ref=s, dst_ref=d, send_sem=send_sems.at[base + k],
                                            recv_sem=recv_sems.at[base + k], device_id=to, device_id_type=MESH)

    copies = (lambda: rc(src.at[NX], ici.at[0], 0, (1 - x, y, c)), lambda: rc(src.at[NY], ici.at[1], 1, (x, 1 - y, c)),
              lambda: rc(src.at[3], d2d, 2, (x, y, 1 - c)))
    mine = lambda: pltpu.make_async_copy(src.at[OWN], own, local_sems.at[0])

    def start():
        for cp in copies + (mine,):
            cp().start()

    def finish():
        mine().wait()
        for cp in copies:
            cp().wait_recv()
        acc = own[...].astype(F32) + d2d[...].astype(F32) + ici[0].astype(F32) + ici[1].astype(F32)
        for cp in copies:
            cp().wait_send()
        return acc

    return start, finish


def _slab_sum(myslab, slabs, send_sems, recv_sems, base):
    x, y, c = lax.axis_index("x"), lax.axis_index("y"), lax.axis_index("c")
    me = _slot(x, y, c)
    peers = [(x, y, 1 - c), (1 - x, y, c), (x, 1 - y, c), (1 - x, 1 - y, c),
             (1 - x, y, 1 - c), (x, 1 - y, 1 - c), (1 - x, 1 - y, 1 - c)]

    def cp(k):
        return pltpu.make_async_remote_copy(src_ref=myslab, dst_ref=slabs.at[me], send_sem=send_sems.at[base + k],
                                            recv_sem=recv_sems.at[base + k], device_id=peers[k], device_id_type=MESH)

    def start():
        slabs[me] = myslab[...]
        for k in range(7):
            cp(k).start()

    def finish():
        for k in range(7):
            cp(k).wait_recv()
        total = slabs[0]
        for d in range(1, N_DEV):
            total = total + slabs[d]
        for k in range(7):
            cp(k).wait_send()
        return total

    return start, finish


def _chunk_rows(r):
    return slice(r * CHUNK, (r + 1) * CHUNK)


def _conv_halo(cch_ref, cuh_ref, n):
    zh = jnp.where(n > 0, cch_ref[...] * cuh_ref[...], 0.0)
    return jnp.concatenate([zh] * (CHUNK // HALO), axis=0)


def _conv_chunk(pj_ref, zhalo, cw, r):
    rows = _chunk_rows(r)
    cc = pj_ref[rows, OFF_CC:OFF_CC + D_CONV]
    cu = pj_ref[rows, OFF_CU:OFF_CU + D_CONV]
    z = cc * cu
    before = _chunk_rows(r - 1)
    zprev = pj_ref[before, OFF_CC:OFF_CC + D_CONV] * pj_ref[before, OFF_CU:OFF_CU + D_CONV] if r > 0 else zhalo
    row = lax.broadcasted_iota(jnp.int32, (CHUNK, D_CONV), 0)
    z1 = jnp.where(row < 1, pltpu.roll(zprev, 1, 0), pltpu.roll(z, 1, 0))
    z2 = jnp.where(row < 2, pltpu.roll(zprev, 2, 0), pltpu.roll(z, 2, 0))
    co = cw[0] * z2 + cw[1] * z1 + cw[2] * z
    return cc, cu, z, z1, z2, co


def _gated_norm(a, gain, t):
    r = lax.rsqrt(jnp.mean(a * a, axis=-1, keepdims=True) + RMS_EPS)
    return a * r * gain * (t * _sigmoid(t))


def _kv_bands(pj, kvp_ref):
    lane = lax.broadcasted_iota(jnp.int32, (2 * BLOCK, D_KV), 1)
    lo = lane < HEAD_DIM

    def bands(prev, cur):
        b = jnp.concatenate([prev, cur], axis=0)
        br = pltpu.roll(b, HEAD_DIM, 1)
        zero = jnp.zeros_like(b)
        return ((jnp.where(lo, b, zero).astype(BF16), jnp.where(lo, zero, br).astype(BF16)),
                (jnp.where(lo, br, zero).astype(BF16), jnp.where(lo, zero, b).astype(BF16)))

    ks = bands(kvp_ref[:, 0:D_KV], pj[:, OFF_K:OFF_K + D_KV])
    vs = bands(kvp_ref[:, D_KV:2 * D_KV], pj[:, OFF_V:OFF_V + D_KV])
    return ks, vs


STACK = PAIRS_PER_KV * BLOCK


def _head(j, i, e):
    return 2 * (PAIRS_PER_KV * j + i) + e


def _pair_cols(j, i, off):
    p = PAIRS_PER_KV * j + i
    return slice(off + 128 * p, off + 128 * (p + 1))


def _fill_attn_bias(bias_scr, first_block):
    qi = lax.broadcasted_iota(jnp.int32, (BLOCK, 2 * BLOCK), 0)
    kj = lax.broadcasted_iota(jnp.int32, (BLOCK, 2 * BLOCK), 1)
    dist = BLOCK + qi - kj
    valid = (dist >= 0) & (dist < BLOCK)
    if first_block:
        valid = valid & (kj >= BLOCK)
    distf = dist.astype(F32)
    for j in range(2):
        for e in range(2):
            for i in range(PAIRS_PER_KV):
                bias_scr[2 * j + e, BLOCK * i:BLOCK * (i + 1), :] = jnp.where(valid, -SLOPES[_head(j, i, e)] * distf, NEG)


def _q_stack(pj, j):
    return jnp.concatenate([(pj[:, _pair_cols(j, i, OFF_Q)] * SCALE).astype(BF16) for i in range(PAIRS_PER_KV)], axis=0)


def _attn_probs(q_stack, kband, bias_ref, sinks):
    s = lax.dot_general(q_stack, kband, _NT, preferred_element_type=F32)
    ones = jnp.ones((128, 128), BF16)
    probs, shares = [], []
    for i, sink in enumerate(sinks):
        rows = slice(BLOCK * i, BLOCK * (i + 1))
        t = s[rows, :] + bias_ref[rows, :]
        m = jnp.broadcast_to(jnp.max(t, axis=-1, keepdims=True), (BLOCK, 128))
        m = jnp.maximum(m, sink)
        p = [jnp.exp(t[:, :128] - m), jnp.exp(t[:, 128:] - m)]
        es = jnp.exp(sink - m)
        total = (jnp.dot(p[0].astype(BF16), ones, preferred_element_type=F32)
                 + jnp.dot(p[1].astype(BF16), ones, preferred_element_type=F32))
        inv = 1.0 / (total + es)
        probs.append(jnp.concatenate([p[0] * inv, p[1] * inv], axis=1))
        shares.append(es * inv)
    return jnp.concatenate(probs, axis=0), jnp.concatenate(shares, axis=0)


def _attn_group(pj, ks, vs, bias_scr, sink_ref, j):
    q_stack = _q_stack(pj, j)
    out, probs, shares = None, [], []
    for e in range(2):
        p, ps = _attn_probs(q_stack, ks[j][e], bias_scr.at[2 * j + e],
                            [sink_ref[_head(j, i, e)] for i in range(PAIRS_PER_KV)])
        p = p.astype(BF16)
        o = jnp.dot(p, vs[j][e], preferred_element_type=F32)
        out = o if out is None else out + o
        probs.append(p)
        shares.append(ps)
    return out, probs, shares


def _mix_fwd(proj, conv_full, sinks, norm_conv, norm_attn):
    def body(pj_ref, kvp_ref, cch_ref, cuh_ref, cw_ref, sink_ref, gc_ref, ga_ref,
             mixed_ref, attn_scr, p_ref, ps_ref, bias_scr):
        n = pl.program_id(0)
        pj = pj_ref

        @pl.when(n == 0)
        def _():
            _fill_attn_bias(bias_scr, first_block=True)

        @pl.when(n == 1)
        def _():
            _fill_attn_bias(bias_scr, first_block=False)

        zhalo = _conv_halo(cch_ref, cuh_ref, n)
        cw = (cw_ref[0:1, :], cw_ref[1:2, :], cw_ref[2:3, :])
        gain_c = gc_ref[...]

        for r in range(N_CHUNKS):
            rows = _chunk_rows(r)
            co = _conv_chunk(pj_ref, zhalo, cw, r)[-1]
            y = _gated_norm(pj_ref[rows, OFF_CB:OFF_CB + D_CONV] * co, gain_c, pj_ref[rows, OFF_GC:OFF_GC + D_CONV])
            mixed_ref[rows, 0:D_CONV] = y.astype(BF16)

        ks, vs = _kv_bands(pj, kvp_ref)
        for j in range(2):
            out, probs, shares = _attn_group(pj, ks, vs, bias_scr, sink_ref, j)
            for e in range(2):
                p_ref[0, 2 * j + e] = probs[e]
                ps_ref[0, 2 * j + e] = shares[e]
            for i in range(PAIRS_PER_KV):
                attn_scr[:, _pair_cols(j, i, 0)] = out[BLOCK * i:BLOCK * (i + 1), :]
        gain_a = ga_ref[...]

        for r in range(N_CHUNKS):
            rows = _chunk_rows(r)
            y = _gated_norm(attn_scr[rows, :], gain_a, pj_ref[rows, OFF_GA:OFF_GA + D_ATTN])
            mixed_ref[rows, D_CONV:D_MIX] = y.astype(BF16)

    per_block = BLOCK // HALO
    return pl.pallas_call(
        body, name="mix_fwd", grid=(N_BLOCKS,),
        in_specs=[
            pl.BlockSpec((BLOCK, D_PROJ), lambda n: (n, 0)),
            pl.BlockSpec((BLOCK, 2 * D_KV), lambda n: (jnp.maximum(n - 1, 0), OFF_K // (2 * D_KV))),
            pl.BlockSpec((HALO, D_CONV), lambda n: (jnp.maximum(n * per_block - 1, 0), OFF_CC // D_CONV)),
            pl.BlockSpec((HALO, D_CONV), lambda n: (jnp.maximum(n * per_block - 1, 0), OFF_CU // D_CONV)),
            pl.BlockSpec((8, D_CONV), lambda n: (0, 0)),
            pl.BlockSpec(memory_space=pltpu.SMEM),
            pl.BlockSpec((1, D_CONV), lambda n: (0, 0)),
            pl.BlockSpec((1, D_ATTN), lambda n: (0, 0)),
        ],
        out_specs=(pl.BlockSpec((BLOCK, D_MIX), lambda n: (n, 0)), pl.BlockSpec((BLOCK, D_ATTN), lambda n: (n, 0)),
                   pl.BlockSpec((1, 4, STACK, 2 * BLOCK), lambda n: (n, 0, 0, 0)),
                   pl.BlockSpec((1, 4, STACK, 128), lambda n: (n, 0, 0, 0))),
        out_shape=(jax.ShapeDtypeStruct((SEQ, D_MIX), BF16), jax.ShapeDtypeStruct((SEQ, D_ATTN), F32),
                   jax.ShapeDtypeStruct((N_BLOCKS, 4, STACK, 2 * BLOCK), BF16),
                   jax.ShapeDtypeStruct((N_BLOCKS, 4, STACK, 128), F32)),
        scratch_shapes=[pltpu.VMEM((4, STACK, 2 * BLOCK), F32)],
        compiler_params=_params(dimension_semantics=("arbitrary",)),
    )(proj, proj, proj, proj, conv_full, sinks, norm_conv, norm_attn)


def _out_proj_loss(mixed, x, target, w_out_full, norm_final):
    tm = 256

    def body(mx_ref, x_ref, t_ref, w_ref, g_ref, dx2_ref, dx2b_ref, dmix_ref, gnf_ref, loss_ref):
        i = pl.program_id(0)
        w = w_ref[...]
        x2 = x_ref[...] + jnp.dot(mx_ref[...], w, preferred_element_type=F32)
        r = lax.rsqrt(jnp.mean(x2 * x2, axis=-1, keepdims=True) + RMS_EPS)
        xn = x2 * r
        g = g_ref[...]
        err = xn * g - t_ref[...]
        part = 0.5 * jnp.sum(jnp.mean(err * err, axis=-1, keepdims=True), axis=0, keepdims=True)
        dy = err * (1.0 / D_MODEL)
        gnf = jnp.sum(dy * xn, axis=0, keepdims=True)
        u = dy * g
        dx2 = r * (u - xn * jnp.mean(u * xn, axis=-1, keepdims=True))
        dx2_ref[...] = dx2
        dx2b = dx2.astype(BF16)
        dx2b_ref[...] = dx2b
        dmix_ref[...] = lax.dot_general(dx2b, w, _NT, preferred_element_type=F32)

        @pl.when(i == 0)
        def _():
            gnf_ref[...] = jnp.zeros_like(gnf_ref)
            loss_ref[...] = jnp.zeros_like(loss_ref)

        gnf_ref[...] += gnf
        loss_ref[...] += jnp.broadcast_to(part, loss_ref.shape)

    return pl.pallas_call(
        body, name="out_proj_loss", grid=(SEQ // tm,),
        in_specs=[pl.BlockSpec((tm, D_MIX), lambda i: (i, 0)), pl.BlockSpec((tm, D_MODEL), lambda i: (i, 0)),
                  pl.BlockSpec((tm, D_MODEL), lambda i: (i, 0)), pl.BlockSpec(memory_space=pltpu.VMEM),
                  pl.BlockSpec((1, D_MODEL), lambda i: (0, 0))],
        out_specs=(pl.BlockSpec((tm, D_MODEL), lambda i: (i, 0)), pl.BlockSpec((tm, D_MODEL), lambda i: (i, 0)),
                   pl.BlockSpec((tm, D_MIX), lambda i: (i, 0)),
                   pl.BlockSpec((1, D_MODEL), lambda i: (0, 0)), pl.BlockSpec((8, 128), lambda i: (0, 0))),
        out_shape=(jax.ShapeDtypeStruct((SEQ, D_MODEL), F32), jax.ShapeDtypeStruct((SEQ, D_MODEL), BF16),
                   jax.ShapeDtypeStruct((SEQ, D_MIX), F32),
                   jax.ShapeDtypeStruct((1, D_MODEL), F32), jax.ShapeDtypeStruct((8, 128), F32)),
        compiler_params=_params(dimension_semantics=("arbitrary",)),
    )(mixed, x, target, w_out_full, norm_final)


def _gated_norm_bwd(a, gain, t, dy):
    r = lax.rsqrt(jnp.mean(a * a, axis=-1, keepdims=True) + RMS_EPS)
    an = a * r
    sg = _sigmoid(t)
    dn = dy * (t * sg)
    dt = dy * (an * gain) * (sg * (1.0 + t * (1.0 - sg)))
    u = dn * gain
    da = r * (u - an * jnp.mean(u * an, axis=-1, keepdims=True))
    return da, dt, dn * an


def _mix_bwd(proj, dmixed, attn, probs, shares, conv_full, norm_conv, norm_attn):
    def body(pj_ref, kvp_ref, cch_ref, cuh_ref, dmx_ref, attn_ref, p_ref, ps_ref, cw_ref, gc_ref, ga_ref,
             dpj_ref, gslab_ref, dattn_scr, nxt_scr, dkv_scr, acc_scr):
        step = pl.program_id(0)
        n = N_BLOCKS - 1 - step
        pj = pj_ref

        @pl.when(step == 0)
        def _():
            gslab_ref[...] = jnp.zeros_like(gslab_ref)
            nxt_scr[...] = jnp.zeros_like(nxt_scr)
            dkv_scr[...] = jnp.zeros_like(dkv_scr)
            acc_scr[...] = jnp.zeros_like(acc_scr)

        zhalo = _conv_halo(cch_ref, cuh_ref, n)
        cw = (cw_ref[0:1, :], cw_ref[1:2, :], cw_ref[2:3, :])
        gain_c = gc_ref[...]
        row = lax.broadcasted_iota(jnp.int32, (CHUNK, D_CONV), 0)

        dco_after = nxt_scr[...]
        for r in reversed(range(N_CHUNKS)):
            rows = _chunk_rows(r)
            cc, cu, z, z1, z2, co = _conv_chunk(pj_ref, zhalo, cw, r)
            cb = pj_ref[rows, OFF_CB:OFF_CB + D_CONV]
            da, dgate, gterm = _gated_norm_bwd(cb * co, gain_c, pj_ref[rows, OFF_GC:OFF_GC + D_CONV],
                                               dmx_ref[rows, 0:D_CONV])
            dpj_ref[rows, OFF_GC:OFF_GC + D_CONV] = dgate.astype(BF16)
            dpj_ref[rows, OFF_CB:OFF_CB + D_CONV] = (da * co).astype(BF16)
            dco = da * cb
            dco1 = jnp.where(row >= CHUNK - 1, pltpu.roll(dco_after, CHUNK - 1, 0), pltpu.roll(dco, CHUNK - 1, 0))
            dco2 = jnp.where(row >= CHUNK - 2, pltpu.roll(dco_after, CHUNK - 2, 0), pltpu.roll(dco, CHUNK - 2, 0))
            dz = cw[2] * dco + cw[1] * dco1 + cw[0] * dco2
            dpj_ref[rows, OFF_CC:OFF_CC + D_CONV] = (dz * cu).astype(BF16)
            dpj_ref[rows, OFF_CU:OFF_CU + D_CONV] = (dz * cc).astype(BF16)
            acc_scr[ACC_NORM_CONV] += gterm
            acc_scr[ACC_CONV0] += dco * z2
            acc_scr[ACC_CONV0 + 1] += dco * z1
            acc_scr[ACC_CONV0 + 2] += dco * z
            dco_after = dco
        nxt_scr[...] = dco_after

        ks, vs = _kv_bands(pj, kvp_ref)
        gain_a = ga_ref[...]

        for r in range(N_CHUNKS):
            rows = _chunk_rows(r)
            da, dgate, gterm = _gated_norm_bwd(attn_ref[rows, :], gain_a, pj_ref[rows, OFF_GA:OFF_GA + D_ATTN],
                                               dmx_ref[rows, D_CONV:D_MIX])
            dpj_ref[rows, OFF_GA:OFF_GA + D_ATTN] = dgate.astype(BF16)
            dattn_scr[rows, :] = da
            acc_scr[ACC_NORM_ATTN] += gterm

        in_lo = lax.broadcasted_iota(jnp.int32, (128, 128), 0) < HEAD_DIM
        half_ones = (jnp.where(in_lo, 1.0, 0.0).astype(BF16), jnp.where(in_lo, 0.0, 1.0).astype(BF16))
        lane_s = lax.broadcasted_iota(jnp.int32, (1, D_MODEL), 1)
        gsink = jnp.zeros((1, D_MODEL), F32)
        dk_t, dv_t = [], []
        for j in range(2):
            q_stack = _q_stack(pj, j)
            do_f = jnp.concatenate([dattn_scr[:, _pair_cols(j, i, 0)] for i in range(PAIRS_PER_KV)], axis=0)
            o_f = jnp.concatenate([attn_ref[:, _pair_cols(j, i, 0)] for i in range(PAIRS_PER_KV)], axis=0)
            prod = (do_f * o_f).astype(BF16)
            deltas = [jnp.dot(prod, half_ones[e], preferred_element_type=F32) for e in range(2)]
            do_b = do_f.astype(BF16)
            q_t, do_t = q_stack.T, do_b.T
            dq, dk_j, dv_j = None, None, None
            for e in range(2):
                p = p_ref[0, 2 * j + e]
                dp = lax.dot_general(do_b, vs[j][e], _NT, preferred_element_type=F32)
                ds = []
                for i in range(PAIRS_PER_KV):
                    rows = slice(BLOCK * i, BLOCK * (i + 1))
                    delta = deltas[e][rows, :]
                    ds.append((p[rows, :].astype(F32) * (dp[rows, :] - jnp.concatenate([delta, delta], axis=1))).astype(BF16))
                    gs_h = -jnp.sum(ps_ref[0, 2 * j + e, rows, 0:1] * delta[:, 0:1], axis=0, keepdims=True)
                    gsink = gsink + jnp.where(lane_s == _head(j, i, e), gs_h, 0.0)
                ds = jnp.concatenate(ds, axis=0)
                t = jnp.dot(ds, ks[j][e], preferred_element_type=F32)
                dq = t if dq is None else dq + t
                half = slice(HEAD_DIM * e, HEAD_DIM * (e + 1))
                a = jnp.dot(q_t[half, :], ds, preferred_element_type=F32)
                b = jnp.dot(do_t[half, :], p, preferred_element_type=F32)
                dk_j = a if dk_j is None else dk_j + a
                dv_j = b if dv_j is None else dv_j + b
            for i in range(PAIRS_PER_KV):
                dpj_ref[:, _pair_cols(j, i, OFF_Q)] = (dq[BLOCK * i:BLOCK * (i + 1), :] * SCALE).astype(BF16)
            dk_t.append(dk_j)
            dv_t.append(dv_j)
        dk = jnp.concatenate(dk_t, axis=0).T
        dv = jnp.concatenate(dv_t, axis=0).T
        dpj_ref[:, OFF_K:OFF_K + D_KV] = (dk[BLOCK:, :] + dkv_scr[:, 0:D_KV]).astype(BF16)
        dpj_ref[:, OFF_V:OFF_V + D_KV] = (dv[BLOCK:, :] + dkv_scr[:, D_KV:2 * D_KV]).astype(BF16)
        dkv_scr[:, 0:D_KV] = dk[:BLOCK, :]
        dkv_scr[:, D_KV:2 * D_KV] = dv[:BLOCK, :]
        gslab_ref[ROW_SINKS:ROW_SINKS + 1, :] += gsink

        @pl.when(step == N_BLOCKS - 1)
        def _():
            for k, slab_row in ((ACC_NORM_CONV, ROW_NORM_CONV), (ACC_NORM_ATTN, ROW_NORM_ATTN), (ACC_CONV0, ROW_CONV0),
                                (ACC_CONV0 + 1, ROW_CONV0 + 1), (ACC_CONV0 + 2, ROW_CONV0 + 2)):
                gslab_ref[slab_row:slab_row + 1, :] = jnp.sum(acc_scr[k], axis=0, keepdims=True)

    per_block = BLOCK // HALO
    last = N_BLOCKS - 1
    return pl.pallas_call(
        body, name="mix_bwd", grid=(N_BLOCKS,),
        in_specs=[
            pl.BlockSpec((BLOCK, D_PROJ), lambda s: (last - s, 0)),
            pl.BlockSpec((BLOCK, 2 * D_KV), lambda s: (jnp.maximum(last - s - 1, 0), OFF_K // (2 * D_KV))),
            pl.BlockSpec((HALO, D_CONV), lambda s: (jnp.maximum((last - s) * per_block - 1, 0), OFF_CC // D_CONV)),
            pl.BlockSpec((HALO, D_CONV), lambda s: (jnp.maximum((last - s) * per_block - 1, 0), OFF_CU // D_CONV)),
            pl.BlockSpec((BLOCK, D_MIX), lambda s: (last - s, 0)),
            pl.BlockSpec((BLOCK, D_ATTN), lambda s: (last - s, 0)),
            pl.BlockSpec((1, 4, STACK, 2 * BLOCK), lambda s: (last - s, 0, 0, 0)),
            pl.BlockSpec((1, 4, STACK, 128), lambda s: (last - s, 0, 0, 0)),
            pl.BlockSpec((8, D_CONV), lambda s: (0, 0)),
            pl.BlockSpec((1, D_CONV), lambda s: (0, 0)),
            pl.BlockSpec((1, D_ATTN), lambda s: (0, 0)),
        ],
        out_specs=(pl.BlockSpec((BLOCK, D_PROJ), lambda s: (last - s, 0)),
                   pl.BlockSpec((8, D_MODEL), lambda s: (0, 0))),
        out_shape=(jax.ShapeDtypeStruct((SEQ, D_PROJ), BF16), jax.ShapeDtypeStruct((8, D_MODEL), F32)),
        scratch_shapes=[pltpu.VMEM((BLOCK, D_ATTN), F32), pltpu.VMEM((CHUNK, D_CONV), F32),
                        pltpu.VMEM((BLOCK, 2 * D_KV), F32), pltpu.VMEM((N_ACC, CHUNK, D_MODEL), F32)],
        compiler_params=_params(dimension_semantics=("arbitrary",)),
    )(proj, proj, proj, proj, dmixed, attn, probs, shares, conv_full, norm_conv, norm_attn)


def _in_bwd_rs(dproj, w_full, x, dx2, norm_in, dw_in_chip, gslab, gnf, loss_part):
    tm = 256
    steps = SEQ // tm
    w_chunk = D_PROJ // 5

    def body(dp_ref, w_hbm, x_ref, dx2_ref, g_ref, dwi_ref, gs_ref, gnf_ref, lp_ref, gx_ref, gwin_ref, gsum_ref,
             gni_scr, dh_scr, own, d2d, ici, myslab, slabs, w_ref, send_sems, recv_sems, local_sems):
        i = pl.program_id(0)
        rs_start, rs_finish = _ici_sum(dwi_ref, own, d2d, ici, send_sems, recv_sems, local_sems)
        slab_start, slab_finish = _slab_sum(myslab, slabs, send_sems, recv_sems, N_ICI_SUM_SEMS)

        @pl.when(i == 0)
        def _():
            gni_scr[...] = jnp.zeros_like(gni_scr)
            rs_start()
            chunks = [pl.ds(j * w_chunk, w_chunk) for j in range(D_PROJ // w_chunk)]
            loads = [pltpu.make_async_copy(w_hbm.at[rows, :], w_ref.at[rows, :], local_sems.at[1 + j])
                     for j, rows in enumerate(chunks)]
            for cp in loads:
                cp.start()
            acc = jnp.zeros((tm, D_MODEL), F32)
            for cp, rows in zip(loads, chunks):
                cp.wait()
                acc = acc + jnp.dot(dp_ref[:, rows], w_ref[rows, :], preferred_element_type=F32)
            dh_scr[...] = acc

        @pl.when(i > 0)
        def _():
            dh_scr[...] = jnp.dot(dp_ref[...], w_ref[...], preferred_element_type=F32)

        dh = dh_scr[...]
        xv = x_ref[...]
        r = lax.rsqrt(jnp.mean(xv * xv, axis=-1, keepdims=True) + RMS_EPS)
        xn = xv * r
        u = dh * g_ref[...]
        gx_ref[...] = dx2_ref[...] + r * (u - xn * jnp.mean(u * xn, axis=-1, keepdims=True))
        gni_scr[...] += jnp.sum(dh * xn, axis=0, keepdims=True)

        @pl.when(i == steps - 1)
        def _():
            row = lax.broadcasted_iota(jnp.int32, (8, D_MODEL), 0)
            lane = lax.broadcasted_iota(jnp.int32, (8, D_MODEL), 1)
            slab = jnp.where(row == ROW_NORM_IN, gni_scr[...], jnp.where(row == ROW_NORM_FINAL, gnf_ref[...], gs_ref[...]))
            myslab[...] = jnp.where((row == ROW_SINKS) & (lane == LOSS_LANE), lp_ref[0:1, 0:1], slab)
            slab_start()
            gwin_ref[...] = rs_finish()
            gsum_ref[...] = slab_finish()

    const = lambda i: (0, 0)
    return pl.pallas_call(
        body, name="in_bwd", grid=(steps,),
        in_specs=[pl.BlockSpec((tm, D_PROJ), lambda i: (i, 0)), pl.BlockSpec(memory_space=pl.ANY),
                  pl.BlockSpec((tm, D_MODEL), lambda i: (i, 0)), pl.BlockSpec((tm, D_MODEL), lambda i: (i, 0)),
                  pl.BlockSpec((1, D_MODEL), const), pl.BlockSpec(memory_space=pl.ANY),
                  pl.BlockSpec((8, D_MODEL), const), pl.BlockSpec((1, D_MODEL), const), pl.BlockSpec((8, 128), const)],
        out_specs=(pl.BlockSpec((tm, D_MODEL), lambda i: (i, 0)), pl.BlockSpec((SHARD_IN, D_MODEL), const),
                   pl.BlockSpec((8, D_MODEL), const)),
        out_shape=(jax.ShapeDtypeStruct((SEQ, D_MODEL), F32), jax.ShapeDtypeStruct((SHARD_IN, D_MODEL), F32),
                   jax.ShapeDtypeStruct((8, D_MODEL), F32)),
        scratch_shapes=[pltpu.VMEM((1, D_MODEL), F32), pltpu.VMEM((tm, D_MODEL), F32),
                        pltpu.VMEM((SHARD_IN, D_MODEL), BF16),
                        pltpu.VMEM((SHARD_IN, D_MODEL), BF16), pltpu.VMEM((2, SHARD_IN, D_MODEL), BF16),
                        pltpu.VMEM((8, D_MODEL), F32), pltpu.VMEM((N_DEV, 8, D_MODEL), F32),
                        pltpu.VMEM((D_PROJ, D_MODEL), BF16),
                        pltpu.SemaphoreType.DMA((N_ICI_SUM_SEMS + 7,)), pltpu.SemaphoreType.DMA((N_ICI_SUM_SEMS + 7,)),
                        pltpu.SemaphoreType.DMA((1 + D_PROJ // w_chunk,))],
        compiler_params=_params(dimension_semantics=("arbitrary",)),
    )(dproj, w_full, x, dx2, norm_in, dw_in_chip, gslab, gnf, loss_part)


def _dw_rs(mixed, dx2b, dproj, h, table):
    tn_out, tn = 2 * SHARD_OUT, IN_PROJ_TILE
    out_steps, in_steps = D_MIX // tn_out, D_PROJ // tn
    steps = out_steps + in_steps
    out_order = (DG, NX, NY, OWN)

    def out_tile(i):
        chip = 2 * lax.axis_index("x") + lax.axis_index("y")
        return jnp.bitwise_xor(chip, (out_steps - 1) - jnp.minimum(i, out_steps - 1))

    def in_tile(table_ref, i):
        return _dw_entry(table_ref, jnp.maximum(i - out_steps, 0))

    def body(table_ref, mx_ref, dxb_ref, a_ref, b_ref, chip_ref, gwo_ref, dwo, dwt, d2d_in, via, own, d2d, ici,
             send_sems, recv_sems, local_sems):
        i = pl.program_id(0)
        rs_start, rs_forward, rs_finish = _shard_sum(dwo, own, d2d, ici, send_sems, recv_sems, local_sems)
        before_tile, after_tiles, chip_finish = _chip_sum(
            dwt, d2d_in, via, chip_ref, lambda k: _dw_entry(table_ref, in_steps + k), send_sems, recv_sems, local_sems,
            N_SHARD_SUM_SEMS, 4)

        for j, k in enumerate(out_order):
            @pl.when(i == j + 1)
            def _():
                rs_start(k)

            if k != OWN:
                @pl.when(i == j + 2)
                def _():
                    rs_forward(k)

        @pl.when(i < out_steps)
        def _():
            tile = lax.dot_general(mx_ref[...], dxb_ref[...], _TN, preferred_element_type=F32).astype(BF16)
            for core in range(2):
                dwo[2 * out_tile(i) + core] = tile[SHARD_OUT * core:SHARD_OUT * (core + 1), :]

        @pl.when(i >= out_steps)
        def _():
            before_tile(i - out_steps)
            tile = lax.dot_general(a_ref[...], b_ref[...], _TN, preferred_element_type=F32).astype(BF16)
            dwt[pl.ds(pl.multiple_of(in_tile(table_ref, i) * tn, tn), tn), :] = tile

        @pl.when(i == steps - 1)
        def _():
            after_tiles()
            gwo_ref[...] = rs_finish()
            chip_finish()

    vmem = pl.BlockSpec(memory_space=pltpu.VMEM)
    grid_spec = pltpu.PrefetchScalarGridSpec(
        num_scalar_prefetch=1, grid=(steps,),
        in_specs=[pl.BlockSpec((SEQ, tn_out), lambda i, table_ref: (0, out_tile(i))), vmem,
                  pl.BlockSpec((SEQ, tn), lambda i, table_ref: (0, in_tile(table_ref, i))), vmem],
        out_specs=(pl.BlockSpec(memory_space=pl.ANY), pl.BlockSpec((SHARD_OUT, D_MODEL), lambda i, table_ref: (0, 0))),
        scratch_shapes=[pltpu.VMEM((N_DEV, SHARD_OUT, D_MODEL), BF16),
                        pltpu.VMEM((D_PROJ, D_MODEL), BF16), pltpu.VMEM((3, SHARD_IN, D_MODEL), BF16),
                        pltpu.VMEM((2, HALF_IN, D_MODEL), BF16),
                        *_shard_sum_scratch(SHARD_OUT),
                        pltpu.SemaphoreType.DMA((N_SHARD_SUM_SEMS + N_CHIP_SUM_SEMS,)),
                        pltpu.SemaphoreType.DMA((N_SHARD_SUM_SEMS + N_CHIP_SUM_SEMS,)),
                        pltpu.SemaphoreType.DMA((8,))])
    return pl.pallas_call(
        body, name="dw", grid_spec=grid_spec,
        out_shape=(jax.ShapeDtypeStruct((4, SHARD_IN, D_MODEL), BF16), jax.ShapeDtypeStruct((SHARD_OUT, D_MODEL), F32)),
        compiler_params=_params(dimension_semantics=("arbitrary",)),
    )(table, mixed, dx2b, dproj, h)


def _adam_all(big_in, big_out, gsum, small, grad_x):
    steps = 4
    tr_in, tr_out = SHARD_IN // steps, SHARD_OUT // steps

    def body(*refs):
        ins, outs = refs[:8 + 1 + 18 + 1], refs[8 + 1 + 18 + 1:]
        i = pl.program_id(0)
        outs[33][...] = ins[27][...]
        for b in range(2):
            w_ref, g_ref, m_ref, v_ref = ins[4 * b:4 * b + 4]
            g = g_ref[...]
            delta, mn, vn = _adamw(w_ref[...], g, m_ref[...], v_ref[...])
            for ref, val in zip(outs[4 * b:4 * b + 4], (g, delta, mn, vn)):
                ref[...] = val

        @pl.when(i == 0)
        def _():
            gsum = ins[8][...]
            idx = _slot(lax.axis_index("x"), lax.axis_index("y"), lax.axis_index("c"))
            cg = jnp.zeros((3, SHARD_CONV), F32)
            for d in range(N_DEV):
                cg = jnp.where(idx == d, gsum[ROW_CONV0:ROW_CONV0 + 3, d * SHARD_CONV:(d + 1) * SHARD_CONV], cg)
            grads = (gsum[ROW_NORM_IN:ROW_NORM_IN + 1], gsum[ROW_SINKS:ROW_SINKS + 1, 0:N_Q_HEADS],
                     gsum[ROW_NORM_CONV:ROW_NORM_CONV + 1], gsum[ROW_NORM_ATTN:ROW_NORM_ATTN + 1],
                     gsum[ROW_NORM_FINAL:ROW_NORM_FINAL + 1], cg)
            for s, g in enumerate(grads):
                at = (slice(None), 0, slice(None)) if s == 5 else (slice(None), slice(None))
                w_ref, m_ref, v_ref = ins[9 + 3 * s:12 + 3 * s]
                delta, mn, vn = _adamw(w_ref[at], g, m_ref[at], v_ref[at])
                for ref, val in zip(outs[8 + 4 * s:12 + 4 * s], (g, delta, mn, vn)):
                    ref[at] = val
            outs[32][...] = gsum[ROW_SINKS:ROW_SINKS + 1, LOSS_LANE:LOSS_LANE + 1]

    const = lambda i: (0, 0)
    rows = lambda i: (i, 0)
    whole = lambda shape: pl.BlockSpec(shape, lambda i: (0,) * len(shape))
    small_shapes = [a.shape for a in small[::3]]
    in_specs = ([pl.BlockSpec((tr_in, D_MODEL), rows)] * 4 + [pl.BlockSpec((tr_out, D_MODEL), rows)] * 4
                + [pl.BlockSpec((8, D_MODEL), const)] + [whole(a.shape) for a in small]
                + [pl.BlockSpec((SEQ // steps, D_MODEL), rows)])
    out_specs = ([pl.BlockSpec((tr_in, D_MODEL), rows)] * 4 + [pl.BlockSpec((tr_out, D_MODEL), rows)] * 4
                 + [whole(s) for s in small_shapes for _ in range(4)] + [pl.BlockSpec((1, 1), const)]
                 + [pl.BlockSpec((SEQ // steps, D_MODEL), rows)])
    out_shape = ([jax.ShapeDtypeStruct((SHARD_IN, D_MODEL), F32)] * 4 + [jax.ShapeDtypeStruct((SHARD_OUT, D_MODEL), F32)] * 4
                 + [jax.ShapeDtypeStruct(s, F32) for s in small_shapes for _ in range(4)]
                 + [jax.ShapeDtypeStruct((1, 1), F32), jax.ShapeDtypeStruct((SEQ, D_MODEL), F32)])
    outs = pl.pallas_call(
        body, name="adam", grid=(steps,), in_specs=in_specs, out_specs=tuple(out_specs), out_shape=tuple(out_shape),
        compiler_params=_params(dimension_semantics=("arbitrary",)),
    )(*big_in, *big_out, gsum, *small, grad_x)
    return outs[0:4], outs[4:8], [outs[8 + 4 * s:12 + 4 * s] for s in range(6)], outs[32], outs[33]


def _rows_first(a):
    return jnp.transpose(a, (1, 0, 2))


def kernel(x, norm_in, w_in, conv_w, attn_sinks, norm_conv_out, norm_attn_out, w_out, norm_final, loss_target, m_norm_in, m_w_in, m_conv_w, m_attn_sinks, m_norm_conv_out, m_norm_attn_out, m_w_out, m_norm_final, v_norm_in, v_w_in, v_conv_w, v_attn_sinks, v_norm_conv_out, v_norm_attn_out, v_w_out, v_norm_final):
    x2d = x.reshape(SEQ, D_MODEL)
    target = loss_target.reshape(SEQ, D_MODEL)
    nf = norm_final.reshape(1, D_MODEL)

    w_in_t, m_w_in_t, v_w_in_t = w_in[0].T, m_w_in[0].T, v_w_in[0].T
    tiles = jnp.asarray(TILE_ORDER, jnp.int32).reshape(-1)
    w_in_full, h, proj, g_out, conv_full = _gather_in_proj(x2d, norm_in, w_in_t, w_out[0], _rows_first(conv_w), tiles)
    sinks = attn_sinks.reshape(N_Q_HEADS)

    mixed, attn, probs, shares = _mix_fwd(proj, conv_full, sinks, norm_conv_out, norm_attn_out)
    dx2, dx2b, dmixed, gnf, loss_part = _out_proj_loss(mixed, x2d, target, g_out.reshape(D_MIX, D_MODEL), nf)
    dproj, gslab = _mix_bwd(proj, dmixed, attn, probs, shares, conv_full, norm_conv_out, norm_attn_out)
    dw_in_chip, g_w_out = _dw_rs(mixed, dx2b, dproj, h, jnp.asarray(DW_TABLE, jnp.int32).reshape(-1))
    grad_x, g_w_in, gsum = _in_bwd_rs(dproj, w_in_full, x2d, dx2, norm_in, dw_in_chip, gslab, gnf, loss_part)

    small = (norm_in, m_norm_in, v_norm_in, attn_sinks, m_attn_sinks, v_attn_sinks,
             norm_conv_out, m_norm_conv_out, v_norm_conv_out, norm_attn_out, m_norm_attn_out, v_norm_attn_out,
             nf, m_norm_final.reshape(1, D_MODEL), v_norm_final.reshape(1, D_MODEL),
             _rows_first(conv_w), _rows_first(m_conv_w), _rows_first(v_conv_w))
    big_in, big_out, (s_ni, s_sk, s_nc, s_na, s_nf, s_cv), loss, grad_x = _adam_all(
        (w_in_t, g_w_in, m_w_in_t, v_w_in_t), (w_out[0], g_w_out, m_w_out[0], v_w_out[0]), gsum, small, grad_x)

    def leaves(k):
        return (s_ni[k], big_in[k].T[None], jnp.transpose(s_cv[k], (1, 0, 2)), s_sk[k], s_nc[k], s_na[k], big_out[k][None],
                s_nf[k].reshape(D_MODEL))

    return (loss.reshape(()), grad_x.reshape(1, SEQ, D_MODEL), *leaves(0), *leaves(1), *leaves(2), *leaves(3))
```

```python
import jax
import jax.numpy as jnp
from jax import lax
from jax.experimental import pallas as pl
from jax.experimental.pallas import tpu as pltpu

F32 = jnp.float32
BF16 = jnp.bfloat16
MESH = pl.DeviceIdType.MESH

N_DEV = 8
SEQ = 2048
D_MODEL = 1024
D_CONV = 1024
D_ATTN = 1024
D_KV = 128
HEAD_DIM = 64
N_Q_HEADS = 16
N_PAIRS = N_Q_HEADS // 2
PAIRS_PER_KV = N_PAIRS // 2
D_MIX = D_CONV + D_ATTN
D_PROJ = 6400
SHARD_IN = D_PROJ // N_DEV
SHARD_OUT = D_MIX // N_DEV
SHARD_CONV = D_CONV // N_DEV
OFF_CB, OFF_CC, OFF_CU, OFF_GC, OFF_Q, OFF_K, OFF_V, OFF_GA = 0, 1024, 2048, 3072, 4096, 5120, 5248, 5376
BLOCK = 128
N_BLOCKS = SEQ // BLOCK
HALO = 8
CHUNK = 16
N_CHUNKS = BLOCK // CHUNK
RMS_EPS = 1e-5
NEG = -1e30
SCALE = HEAD_DIM ** -0.5
SLOPES = tuple(2.0 ** (-8.0 * (h + 1) / N_Q_HEADS) for h in range(N_Q_HEADS))

ADAM_LR = 0.001
ADAM_B1 = 0.9
ADAM_B2 = 0.999
ADAM_EPS = 1e-08
ADAM_WD = 0.01
ADAM_STEP = 10

ROW_NORM_IN, ROW_NORM_CONV, ROW_NORM_ATTN, ROW_NORM_FINAL, ROW_CONV0, ROW_SINKS = 0, 1, 2, 3, 4, 7
LOSS_LANE = N_Q_HEADS
ACC_NORM_CONV, ACC_NORM_ATTN, ACC_CONV0, N_ACC = 0, 1, 2, 5

VMEM_LIMIT = 56 * 1024 * 1024

_NT = (((1,), (1,)), ((), ()))
_TN = (((0,), (0,)), ((), ()))


def _params(**kw):
    return pltpu.CompilerParams(vmem_limit_bytes=VMEM_LIMIT, **kw)


def _adamw(w, g, m, v):
    m = ADAM_B1 * m + (1.0 - ADAM_B1) * g
    v = ADAM_B2 * v + (1.0 - ADAM_B2) * (g * g)
    m_hat = m / (1.0 - ADAM_B1 ** ADAM_STEP)
    v_hat = v / (1.0 - ADAM_B2 ** ADAM_STEP)
    delta = -ADAM_LR * (m_hat / (jnp.sqrt(v_hat) + ADAM_EPS) + ADAM_WD * w)
    return delta, m, v


def _sigmoid(t):
    return 1.0 / (1.0 + jnp.exp(-t))


def _slot(px, py, pc):
    return 4 * px + 2 * py + pc


OWN, NX, NY, DG = range(4)
HALF_IN = SHARD_IN // 2
N_GATHER_KINDS = 13


IN_PROJ_TILE = 640
TILE_ORDER = ((0, 1, 2, 3, 4, 5, 6, 7, 8, 9), (3, 4, 0, 1, 2, 8, 9, 5, 6, 7),
              (5, 6, 0, 1, 7, 8, 9, 2, 3, 4), (8, 9, 3, 4, 5, 6, 7, 0, 1, 2))
TILES_OWN, TILES_NEIGHBOURS = 2, 7


def _tile(table_ref, p):
    chip = 2 * lax.axis_index("x") + lax.axis_index("y")
    return table_ref[chip * len(TILE_ORDER[0]) + p]


DW_TILE_ORDER = tuple(tuple(reversed(row)) for row in TILE_ORDER)


def _tiles_until_complete(chip, owner):
    lo, hi = owner * 2 * SHARD_IN, (owner + 1) * 2 * SHARD_IN
    touching = [t for t in range(len(TILE_ORDER[0])) if t * IN_PROJ_TILE < hi and (t + 1) * IN_PROJ_TILE > lo]
    return 1 + max(DW_TILE_ORDER[chip].index(t) for t in touching)


DW_TABLE = tuple(DW_TILE_ORDER[chip] + tuple(_tiles_until_complete(chip, chip ^ flip) for flip in (0, 2, 1, 3))
                 for chip in range(4))


def _dw_entry(table_ref, p):
    chip = 2 * lax.axis_index("x") + lax.axis_index("y")
    return table_ref[chip * len(DW_TABLE[0]) + p]


def _gather_in_proj(x, norm_in, w_in_sh, w_out_sh, conv_sh, tiles):
    tn = IN_PROJ_TILE
    steps = D_PROJ // tn
    tm = 256

    def body(tiles_ref, x_hbm, g_ref, win_ref, wout_ref, cv_ref, wt_ref, h_ref, proj_ref, wob_ref, conv_ref,
             gin_ref, gcv_ref, x_ref, send_sems, recv_sems, local_sems):
        p = pl.program_id(0)
        local_sem = local_sems.at[0]
        x, y, c = lax.axis_index("x"), lax.axis_index("y"), lax.axis_index("c")
        me, sibling = (x, y, c), (x, y, 1 - c)
        nx, ny, dg = (1 - x, y, c), (x, 1 - y, c), (1 - x, 1 - y, c)

        def other(dev):
            return (dev[0], dev[1], 1 - dev[2])

        def shard(dev):
            return gin_ref.at[pl.ds(pl.multiple_of(_slot(*dev) * SHARD_IN, 16), SHARD_IN), :]

        def half(dev, h):
            return gin_ref.at[pl.ds(pl.multiple_of(_slot(*dev) * SHARD_IN + h * HALF_IN, 16), HALF_IN), :]

        def rc(ref, k, to):
            return pltpu.make_async_remote_copy(src_ref=ref, dst_ref=ref, send_sem=send_sems.at[k],
                                                recv_sem=recv_sems.at[k], device_id=to, device_id_type=MESH)

        def cv(k, dev, to):
            s = _slot(*dev)
            return pltpu.make_async_remote_copy(src_ref=gcv_ref.at[s], dst_ref=gcv_ref.at[s],
                                                send_sem=send_sems.at[N_GATHER_KINDS + k],
                                                recv_sem=recv_sems.at[N_GATHER_KINDS + k], device_id=to, device_id_type=MESH)

        def own_copies():
            return [rc(shard(me), 0, sibling),
                    rc(half(me, 0), 1, nx), rc(half(me, 1), 2, nx),
                    rc(half(me, 1), 4, ny), rc(half(me, 0), 3, ny),
                    cv(0, me, sibling)] + [cv(1 + j, me, peer) for j, peer in enumerate((nx, ny, dg))]

        def pass_on(dev, h, k_in, k_ici, k_d2d):
            rc(half(dev, h), k_in, me).wait_recv()
            if k_ici is not None:
                rc(half(dev, h), k_ici, ny if dev is nx else nx).start()
            rc(half(dev, h), k_d2d, sibling).start()

        @pl.when(p == 0)
        def _():
            gin_ref[pl.ds(pl.multiple_of(_slot(*me) * SHARD_IN, 16), SHARD_IN), :] = win_ref[...].astype(BF16)
            gcv_ref[_slot(*me)] = jnp.zeros((8, SHARD_CONV), F32)
            gcv_ref[_slot(*me), 0:3, :] = cv_ref[:, 0, :]
            for cp in own_copies():
                cp.start()
            wob_ref[...] = wout_ref[...].astype(BF16)
            x_load = pltpu.make_async_copy(x_hbm, x_ref, local_sems.at[1])
            x_load.start()
            x_load.wait()
            for t in range(SEQ // tm):
                xv = x_ref[tm * t:tm * (t + 1), :]
                r = lax.rsqrt(jnp.mean(xv * xv, axis=-1, keepdims=True) + RMS_EPS)
                h_ref[tm * t:tm * (t + 1), :] = (xv * r * g_ref[...]).astype(BF16)
            rc(shard(sibling), 0, me).wait_recv()

        @pl.when(p == TILES_OWN)
        def _():
            for args in ((nx, 0, 1, 5, 7), (ny, 1, 4, 6, 10), (nx, 1, 2, None, 8), (ny, 0, 3, None, 9)):
                pass_on(*args)
            for j, peer in enumerate((nx, ny, dg)):
                cv(1 + j, peer, me).wait_recv()
                cv(4 + j, peer, sibling).start()
            for (dev, h), k in (((nx, 0), 7), ((nx, 1), 8), ((ny, 0), 9), ((ny, 1), 10)):
                rc(half(other(dev), h), k, me).wait_recv()

        @pl.when(p == TILES_NEIGHBOURS - 1)
        def _():
            pass_on(dg, 0, 5, None, 11)
            pass_on(dg, 1, 6, None, 12)

        @pl.when(p == TILES_NEIGHBOURS)
        def _():
            for (dev, h), k in (((dg, 0), 11), ((dg, 1), 12)):
                rc(half(other(dev), h), k, me).wait_recv()
            pltpu.make_async_copy(gin_ref, wt_ref, local_sem).start()

        w = gin_ref[pl.ds(pl.multiple_of(_tile(tiles_ref, p) * tn, tn), tn), :]
        proj_ref[...] = lax.dot_general(h_ref[...], w, _NT, preferred_element_type=F32)

        @pl.when(p == steps - 1)
        def _():
            cv(0, sibling, me).wait_recv()
            for j, peer in enumerate((nx, ny, dg)):
                cv(4 + j, other(peer), me).wait_recv()
            for d in range(N_DEV):
                conv_ref[:, d * SHARD_CONV:(d + 1) * SHARD_CONV] = gcv_ref[d]
            relayed = [rc(half(nx, 0), 5, ny), rc(half(ny, 1), 6, nx)]
            relayed += [rc(half(dev, h), k, sibling) for (dev, h), k in
                        (((nx, 0), 7), ((nx, 1), 8), ((ny, 0), 9), ((ny, 1), 10), ((dg, 0), 11), ((dg, 1), 12))]
            relayed += [cv(4 + j, peer, sibling) for j, peer in enumerate((nx, ny, dg))]
            for cp in own_copies() + relayed:
                cp.wait_send()
            pltpu.make_async_copy(gin_ref, wt_ref, local_sem).wait()

    vmem = pl.BlockSpec(memory_space=pltpu.VMEM)
    grid_spec = pltpu.PrefetchScalarGridSpec(
        num_scalar_prefetch=1, grid=(steps,),
        in_specs=[pl.BlockSpec(memory_space=pl.ANY), vmem, vmem, vmem, vmem],
        out_specs=(pl.BlockSpec(memory_space=pl.ANY), vmem,
                   pl.BlockSpec((SEQ, tn), lambda p, tiles_ref: (0, _tile(tiles_ref, p))), vmem, vmem),
        scratch_shapes=[pltpu.VMEM((D_PROJ, D_MODEL), BF16), pltpu.VMEM((N_DEV, 8, SHARD_CONV), F32),
                        pltpu.VMEM((SEQ, D_MODEL), F32),
                        pltpu.SemaphoreType.DMA((N_GATHER_KINDS + 7,)), pltpu.SemaphoreType.DMA((N_GATHER_KINDS + 7,)),
                        pltpu.SemaphoreType.DMA((2,))])
    return pl.pallas_call(
        body, name="gather_in_proj", grid_spec=grid_spec,
        out_shape=(jax.ShapeDtypeStruct((D_PROJ, D_MODEL), BF16), jax.ShapeDtypeStruct((SEQ, D_MODEL), BF16),
                   jax.ShapeDtypeStruct((SEQ, D_PROJ), F32), jax.ShapeDtypeStruct((SHARD_OUT, D_MODEL), BF16),
                   jax.ShapeDtypeStruct((8, D_CONV), F32)),
        compiler_params=_params(dimension_semantics=("arbitrary",)),
    )(tiles, x, norm_in, w_in_sh, w_out_sh, conv_sh)


def _wout_gather(wo_ref, gout_ref, send_sems, recv_sems, local_sem):
    x, y, c = lax.axis_index("x"), lax.axis_index("y"), lax.axis_index("c")
    me, sibling = (x, y, c), (x, y, 1 - c)
    chips = [(1 - x, y), (x, 1 - y), (1 - x, 1 - y)]

    def copy(k, block, to, src=None):
        rows = gout_ref.at[_slot(*block)]
        return pltpu.make_async_remote_copy(src_ref=rows if src is None else src, dst_ref=rows,
                                            send_sem=send_sems.at[k], recv_sem=recv_sems.at[k],
                                            device_id=to, device_id_type=MESH)

    def mine():
        return pltpu.make_async_copy(wo_ref, gout_ref.at[_slot(*me)], local_sem)

    def start():
        mine().start()
        copy(0, me, sibling, src=wo_ref).start()
        for j, chip in enumerate(chips):
            copy(1 + j, me, (*chip, c), src=wo_ref).start()

    def forward():
        for j, chip in enumerate(chips):
            copy(1 + j, (*chip, c), me).wait_recv()
            copy(4 + j, (*chip, c), sibling).start()

    def finish():
        copy(0, sibling, me).wait_recv()
        for j, chip in enumerate(chips):
            copy(4 + j, (*chip, 1 - c), me).wait_recv()
        copy(0, me, sibling, src=wo_ref).wait_send()
        for j, chip in enumerate(chips):
            copy(1 + j, me, (*chip, c), src=wo_ref).wait_send()
            copy(4 + j, (*chip, c), sibling).wait_send()
        mine().wait()

    return start, forward, finish


def _shard_sum(src, own, d2d, ici, send_sems, recv_sems, local_sems, base=0):
    x, y, c = lax.axis_index("x"), lax.axis_index("y"), lax.axis_index("c")
    sibling = (x, y, 1 - c)
    chips = [(x, y), (1 - x, y), (x, 1 - y), (1 - x, 1 - y)]

    def rcopy(s, d, k, to):
        return pltpu.make_async_remote_copy(src_ref=s, dst_ref=d, send_sem=send_sems.at[base + k],
                                            recv_sem=recv_sems.at[base + k], device_id=to, device_id_type=MESH)

    def mine(k):
        return pltpu.make_async_copy(src.at[_slot(*chips[k], c)], own.at[k], local_sems.at[k])

    def to_sibling(k):
        return rcopy(src.at[_slot(*chips[k], 1 - c)], d2d.at[k], k, sibling)

    def to_chip(k):
        return rcopy(own.at[k], ici.at[k - 1], 3 + k, (*chips[k], c))

    def start(k):
        mine(k).start()
        to_sibling(k).start()

    def forward(k):
        mine(k).wait()
        to_sibling(k).wait_recv()
        own[k] = (own[k].astype(F32) + d2d[k].astype(F32)).astype(BF16)
        to_chip(k).start()

    def finish():
        mine(0).wait()
        to_sibling(0).wait_recv()
        acc = own[0].astype(F32) + d2d[0].astype(F32)
        for k in range(1, 4):
            to_chip(k).wait_recv()
            acc = acc + ici[k - 1].astype(F32)
        for k in range(4):
            to_sibling(k).wait_send()
        for k in range(1, 4):
            to_chip(k).wait_send()
        return acc

    return start, forward, finish


def _shard_sum_scratch(rows):
    return [pltpu.VMEM((4, rows, D_MODEL), BF16), pltpu.VMEM((4, rows, D_MODEL), BF16),
            pltpu.VMEM((3, rows, D_MODEL), BF16)]


N_SHARD_SUM_SEMS = 7


N_CHIP_SUM_SEMS = 5


def _chip_sum(dwt, d2d, via, out_hbm, tiles_until, send_sems, recv_sems, local_sems, base, local_base):
    x, y, c = lax.axis_index("x"), lax.axis_index("y"), lax.axis_index("c")
    sibling, nx, ny = (x, y, 1 - c), (1 - x, y, c), (x, 1 - y, c)
    chips = [(x, y), (1 - x, y), (x, 1 - y), (1 - x, 1 - y)]

    def shard(s):
        return dwt.at[pl.ds(pl.multiple_of(s * SHARD_IN, 16), SHARD_IN), :]

    def half(ref, h):
        return ref.at[pl.ds(h * HALF_IN, HALF_IN), :]

    def rc(s, d, k, to):
        return pltpu.make_async_remote_copy(src_ref=s, dst_ref=d, send_sem=send_sems.at[base + k],
                                            recv_sem=recv_sems.at[base + k], device_id=to, device_id_type=MESH)

    def to_sibling(k):
        return rc(shard(_slot(*chips[k], 1 - c)), d2d.at[k - 1], k - 1, sibling)

    for_dg = (lambda: rc(half(d2d.at[DG - 1], 0), via.at[0], 3, nx), lambda: rc(half(d2d.at[DG - 1], 1), via.at[1], 4, ny))

    def save(k):
        return pltpu.make_async_copy(d2d.at[k - 1], out_hbm.at[k], local_sems.at[local_base + k])

    own_saves = (lambda: pltpu.make_async_copy(shard(_slot(x, y, c)), out_hbm.at[OWN], local_sems.at[local_base]),
                 lambda: pltpu.make_async_copy(shard(_slot(x, y, 1 - c)), out_hbm.at[3], local_sems.at[local_base + 3]))

    def before_tile(n):
        for k in (NX, NY, DG):
            @pl.when(tiles_until(k) == n)
            def _():
                to_sibling(k).start()

            @pl.when(tiles_until(k) + 1 == n)
            def _():
                to_sibling(k).wait_recv()
                d2d[k - 1] = (shard(_slot(*chips[k], c))[...].astype(F32) + d2d[k - 1].astype(F32)).astype(BF16)
                if k == DG:
                    for cp in for_dg:
                        cp().start()

    def after_tiles():
        for cp in own_saves:
            cp().start()

    def finish():
        for k, h in ((NY, 0), (NX, 1)):
            for_dg[h]().wait_recv()
            rows = pl.ds(h * HALF_IN, HALF_IN)
            d2d[k - 1, rows, :] = (d2d[k - 1, rows, :].astype(F32) + via[h].astype(F32)).astype(BF16)
            save(k).start()
        for cp in own_saves + (lambda: save(NX), lambda: save(NY)):
            cp().wait()
        for cp in (lambda: to_sibling(NX), lambda: to_sibling(NY), lambda: to_sibling(DG)) + for_dg:
            cp().wait_send()

    return before_tile, after_tiles, finish


N_ICI_SUM_SEMS = 3


def _ici_sum(src, own, d2d, ici, send_sems, recv_sems, local_sems, base=0):
    x, y, c = lax.axis_index("x"), lax.axis_index("y"), lax.axis_index("c")

    def rc(s, d, k, to):
        return pltpu.make_async_remote_copy(src_ref=s, dst_ref=d, send_sem=send_sems.at[base + k],
                                            recv_sem=recv_sems.at[base + k], device_id=to, device_id_type=MESH)

    copies = (lambda: rc(src.at[NX], ici.at[0], 0, (1 - x, y, c)), lambda: rc(src.at[NY], ici.at[1], 1, (x, 1 - y, c)),
              lambda: rc(src.at[3], d2d, 2, (x, y, 1 - c)))
    mine = lambda: pltpu.make_async_copy(src.at[OWN], own, local_sems.at[0])

    def start():
        for cp in copies + (mine,):
            cp().start()

    def finish():
        mine().wait()
        for cp in copies:
            cp().wait_recv()
        acc = own[...].astype(F32) + d2d[...].astype(F32) + ici[0].astype(F32) + ici[1].astype(F32)
        for cp in copies:
            cp().wait_send()
        return acc

    return start, finish


def _slab_sum(myslab, slabs, send_sems, recv_sems, base):
    x, y, c = lax.axis_index("x"), lax.axis_index("y"), lax.axis_index("c")
    me = _slot(x, y, c)
    peers = [(x, y, 1 - c), (1 - x, y, c), (x, 1 - y, c), (1 - x, 1 - y, c),
             (1 - x, y, 1 - c), (x, 1 - y, 1 - c), (1 - x, 1 - y, 1 - c)]

    def cp(k):
        return pltpu.make_async_remote_copy(src_ref=myslab, dst_ref=slabs.at[me], send_sem=send_sems.at[base + k],
                                            recv_sem=recv_sems.at[base + k], device_id=peers[k], device_id_type=MESH)

    def start():
        slabs[me] = myslab[...]
        for k in range(7):
            cp(k).start()

    def finish():
        for k in range(7):
            cp(k).wait_recv()
        total = slabs[0]
        for d in range(1, N_DEV):
            total = total + slabs[d]
        for k in range(7):
            cp(k).wait_send()
        return total

    return start, finish


def _chunk_rows(r):
    return slice(r * CHUNK, (r + 1) * CHUNK)


def _conv_halo(cch_ref, cuh_ref, n):
    zh = jnp.where(n > 0, cch_ref[...] * cuh_ref[...], 0.0)
    return jnp.concatenate([zh] * (CHUNK // HALO), axis=0)


def _conv_chunk(pj_ref, zhalo, cw, r):
    rows = _chunk_rows(r)
    cc = pj_ref[rows, OFF_CC:OFF_CC + D_CONV]
    cu = pj_ref[rows, OFF_CU:OFF_CU + D_CONV]
    z = cc * cu
    before = _chunk_rows(r - 1)
    zprev = pj_ref[before, OFF_CC:OFF_CC + D_CONV] * pj_ref[before, OFF_CU:OFF_CU + D_CONV] if r > 0 else zhalo
    row = lax.broadcasted_iota(jnp.int32, (CHUNK, D_CONV), 0)
    z1 = jnp.where(row < 1, pltpu.roll(zprev, 1, 0), pltpu.roll(z, 1, 0))
    z2 = jnp.where(row < 2, pltpu.roll(zprev, 2, 0), pltpu.roll(z, 2, 0))
    co = cw[0] * z2 + cw[1] * z1 + cw[2] * z
    return cc, cu, z, z1, z2, co


def _gated_norm(a, gain, t):
    r = lax.rsqrt(jnp.mean(a * a, axis=-1, keepdims=True) + RMS_EPS)
    return a * r * gain * (t * _sigmoid(t))


def _kv_bands(pj, kvp_ref):
    lane = lax.broadcasted_iota(jnp.int32, (2 * BLOCK, D_KV), 1)
    lo = lane < HEAD_DIM

    def bands(prev, cur):
        b = jnp.concatenate([prev, cur], axis=0)
        br = pltpu.roll(b, HEAD_DIM, 1)
        zero = jnp.zeros_like(b)
        return ((jnp.where(lo, b, zero).astype(BF16), jnp.where(lo, zero, br).astype(BF16)),
                (jnp.where(lo, br, zero).astype(BF16), jnp.where(lo, zero, b).astype(BF16)))

    ks = bands(kvp_ref[:, 0:D_KV], pj[:, OFF_K:OFF_K + D_KV])
    vs = bands(kvp_ref[:, D_KV:2 * D_KV], pj[:, OFF_V:OFF_V + D_KV])
    return ks, vs


STACK = PAIRS_PER_KV * BLOCK


def _head(j, i, e):
    return 2 * (PAIRS_PER_KV * j + i) + e


def _pair_cols(j, i, off):
    p = PAIRS_PER_KV * j + i
    return slice(off + 128 * p, off + 128 * (p + 1))


def _fill_attn_bias(bias_scr, first_block):
    qi = lax.broadcasted_iota(jnp.int32, (BLOCK, 2 * BLOCK), 0)
    kj = lax.broadcasted_iota(jnp.int32, (BLOCK, 2 * BLOCK), 1)
    dist = BLOCK + qi - kj
    valid = (dist >= 0) & (dist < BLOCK)
    if first_block:
        valid = valid & (kj >= BLOCK)
    distf = dist.astype(F32)
    for j in range(2):
        for e in range(2):
            for i in range(PAIRS_PER_KV):
                bias_scr[2 * j + e, BLOCK * i:BLOCK * (i + 1), :] = jnp.where(valid, -SLOPES[_head(j, i, e)] * distf, NEG)


def _q_stack(pj, j):
    return jnp.concatenate([(pj[:, _pair_cols(j, i, OFF_Q)] * SCALE).astype(BF16) for i in range(PAIRS_PER_KV)], axis=0)


def _attn_probs(q_stack, kband, bias_ref, sinks):
    s = lax.dot_general(q_stack, kband, _NT, preferred_element_type=F32)
    ones = jnp.ones((128, 128), BF16)
    probs, shares = [], []
    for i, sink in enumerate(sinks):
        rows = slice(BLOCK * i, BLOCK * (i + 1))
        t = s[rows, :] + bias_ref[rows, :]
        m = jnp.broadcast_to(jnp.max(t, axis=-1, keepdims=True), (BLOCK, 128))
        m = jnp.maximum(m, sink)
        p = [jnp.exp(t[:, :128] - m), jnp.exp(t[:, 128:] - m)]
        es = jnp.exp(sink - m)
        total = (jnp.dot(p[0].astype(BF16), ones, preferred_element_type=F32)
                 + jnp.dot(p[1].astype(BF16), ones, preferred_element_type=F32))
        inv = 1.0 / (total + es)
        probs.append(jnp.concatenate([p[0] * inv, p[1] * inv], axis=1))
        shares.append(es * inv)
    return jnp.concatenate(probs, axis=0), jnp.concatenate(shares, axis=0)


def _attn_group(pj, ks, vs, bias_scr, sink_ref, j):
    q_stack = _q_stack(pj, j)
    out, probs, shares = None, [], []
    for e in range(2):
        p, ps = _attn_probs(q_stack, ks[j][e], bias_scr.at[2 * j + e],
                            [sink_ref[_head(j, i, e)] for i in range(PAIRS_PER_KV)])
        p = p.astype(BF16)
        o = jnp.dot(p, vs[j][e], preferred_element_type=F32)
        out = o if out is None else out + o
        probs.append(p)
        shares.append(ps)
    return out, probs, shares


def _mix_fwd(proj, conv_full, sinks, norm_conv, norm_attn, w_out_b):
    forward_step = N_BLOCKS - 3

    def body(pj_ref, kvp_ref, cch_ref, cuh_ref, cw_ref, sink_ref, gc_ref, ga_ref, wo_ref,
             mixed_ref, attn_scr, p_ref, ps_ref, gout_ref, bias_scr, send_sems, recv_sems, local_sem):
        n = pl.program_id(0)
        pj = pj_ref
        wout_start, wout_forward, wout_finish = _wout_gather(wo_ref, gout_ref, send_sems, recv_sems, local_sem)

        @pl.when(n == 0)
        def _():
            wout_start()
            _fill_attn_bias(bias_scr, first_block=True)

        @pl.when(n == 1)
        def _():
            _fill_attn_bias(bias_scr, first_block=False)

        zhalo = _conv_halo(cch_ref, cuh_ref, n)
        cw = (cw_ref[0:1, :], cw_ref[1:2, :], cw_ref[2:3, :])
        gain_c = gc_ref[...]

        for r in range(N_CHUNKS):
            rows = _chunk_rows(r)
            co = _conv_chunk(pj_ref, zhalo, cw, r)[-1]
            y = _gated_norm(pj_ref[rows, OFF_CB:OFF_CB + D_CONV] * co, gain_c, pj_ref[rows, OFF_GC:OFF_GC + D_CONV])
            mixed_ref[rows, 0:D_CONV] = y.astype(BF16)

        ks, vs = _kv_bands(pj, kvp_ref)
        for j in range(2):
            out, probs, shares = _attn_group(pj, ks, vs, bias_scr, sink_ref, j)
            for e in range(2):
                p_ref[0, 2 * j + e] = probs[e]
                ps_ref[0, 2 * j + e] = shares[e]
            for i in range(PAIRS_PER_KV):
                attn_scr[:, _pair_cols(j, i, 0)] = out[BLOCK * i:BLOCK * (i + 1), :]
        gain_a = ga_ref[...]

        for r in range(N_CHUNKS):
            rows = _chunk_rows(r)
            y = _gated_norm(attn_scr[rows, :], gain_a, pj_ref[rows, OFF_GA:OFF_GA + D_ATTN])
            mixed_ref[rows, D_CONV:D_MIX] = y.astype(BF16)

        @pl.when(n == forward_step)
        def _():
            wout_forward()

        @pl.when(n == N_BLOCKS - 1)
        def _():
            wout_finish()

    per_block = BLOCK // HALO
    return pl.pallas_call(
        body, name="mix_fwd", grid=(N_BLOCKS,),
        in_specs=[
            pl.BlockSpec((BLOCK, D_PROJ), lambda n: (n, 0)),
            pl.BlockSpec((BLOCK, 2 * D_KV), lambda n: (jnp.maximum(n - 1, 0), OFF_K // (2 * D_KV))),
            pl.BlockSpec((HALO, D_CONV), lambda n: (jnp.maximum(n * per_block - 1, 0), OFF_CC // D_CONV)),
            pl.BlockSpec((HALO, D_CONV), lambda n: (jnp.maximum(n * per_block - 1, 0), OFF_CU // D_CONV)),
            pl.BlockSpec((8, D_CONV), lambda n: (0, 0)),
            pl.BlockSpec(memory_space=pltpu.SMEM),
            pl.BlockSpec((1, D_CONV), lambda n: (0, 0)),
            pl.BlockSpec((1, D_ATTN), lambda n: (0, 0)),
            pl.BlockSpec(memory_space=pl.ANY),
        ],
        out_specs=(pl.BlockSpec((BLOCK, D_MIX), lambda n: (n, 0)), pl.BlockSpec((BLOCK, D_ATTN), lambda n: (n, 0)),
                   pl.BlockSpec((1, 4, STACK, 2 * BLOCK), lambda n: (n, 0, 0, 0)),
                   pl.BlockSpec((1, 4, STACK, 128), lambda n: (n, 0, 0, 0)),
                   pl.BlockSpec(memory_space=pl.ANY)),
        out_shape=(jax.ShapeDtypeStruct((SEQ, D_MIX), BF16), jax.ShapeDtypeStruct((SEQ, D_ATTN), F32),
                   jax.ShapeDtypeStruct((N_BLOCKS, 4, STACK, 2 * BLOCK), BF16),
                   jax.ShapeDtypeStruct((N_BLOCKS, 4, STACK, 128), F32),
                   jax.ShapeDtypeStruct((N_DEV, SHARD_OUT, D_MODEL), BF16)),
        scratch_shapes=[pltpu.VMEM((4, STACK, 2 * BLOCK), F32),
                        pltpu.SemaphoreType.DMA((7,)), pltpu.SemaphoreType.DMA((7,)), pltpu.SemaphoreType.DMA],
        compiler_params=_params(dimension_semantics=("arbitrary",)),
    )(proj, proj, proj, proj, conv_full, sinks, norm_conv, norm_attn, w_out_b)


def _out_proj_loss(mixed, x, target, w_out_full, norm_final):
    tm = 256

    def body(mx_ref, x_ref, t_ref, w_ref, g_ref, dx2_ref, dx2b_ref, dmix_ref, gnf_ref, loss_ref):
        i = pl.program_id(0)
        w = w_ref[...]
        x2 = x_ref[...] + jnp.dot(mx_ref[...], w, preferred_element_type=F32)
        r = lax.rsqrt(jnp.mean(x2 * x2, axis=-1, keepdims=True) + RMS_EPS)
        xn = x2 * r
        g = g_ref[...]
        err = xn * g - t_ref[...]
        part = 0.5 * jnp.sum(jnp.mean(err * err, axis=-1, keepdims=True), axis=0, keepdims=True)
        dy = err * (1.0 / D_MODEL)
        gnf = jnp.sum(dy * xn, axis=0, keepdims=True)
        u = dy * g
        dx2 = r * (u - xn * jnp.mean(u * xn, axis=-1, keepdims=True))
        dx2_ref[...] = dx2
        dx2b = dx2.astype(BF16)
        dx2b_ref[...] = dx2b
        dmix_ref[...] = lax.dot_general(dx2b, w, _NT, preferred_element_type=F32)

        @pl.when(i == 0)
        def _():
            gnf_ref[...] = jnp.zeros_like(gnf_ref)
            loss_ref[...] = jnp.zeros_like(loss_ref)

        gnf_ref[...] += gnf
        loss_ref[...] += jnp.broadcast_to(part, loss_ref.shape)

    return pl.pallas_call(
        body, name="out_proj_loss", grid=(SEQ // tm,),
        in_specs=[pl.BlockSpec((tm, D_MIX), lambda i: (i, 0)), pl.BlockSpec((tm, D_MODEL), lambda i: (i, 0)),
                  pl.BlockSpec((tm, D_MODEL), lambda i: (i, 0)), pl.BlockSpec(memory_space=pltpu.VMEM),
                  pl.BlockSpec((1, D_MODEL), lambda i: (0, 0))],
        out_specs=(pl.BlockSpec((tm, D_MODEL), lambda i: (i, 0)), pl.BlockSpec((tm, D_MODEL), lambda i: (i, 0)),
                   pl.BlockSpec((tm, D_MIX), lambda i: (i, 0)),
                   pl.BlockSpec((1, D_MODEL), lambda i: (0, 0)), pl.BlockSpec((8, 128), lambda i: (0, 0))),
        out_shape=(jax.ShapeDtypeStruct((SEQ, D_MODEL), F32), jax.ShapeDtypeStruct((SEQ, D_MODEL), BF16),
                   jax.ShapeDtypeStruct((SEQ, D_MIX), F32),
                   jax.ShapeDtypeStruct((1, D_MODEL), F32), jax.ShapeDtypeStruct((8, 128), F32)),
        compiler_params=_params(dimension_semantics=("arbitrary",)),
    )(mixed, x, target, w_out_full, norm_final)


def _gated_norm_bwd(a, gain, t, dy):
    r = lax.rsqrt(jnp.mean(a * a, axis=-1, keepdims=True) + RMS_EPS)
    an = a * r
    sg = _sigmoid(t)
    dn = dy * (t * sg)
    dt = dy * (an * gain) * (sg * (1.0 + t * (1.0 - sg)))
    u = dn * gain
    da = r * (u - an * jnp.mean(u * an, axis=-1, keepdims=True))
    return da, dt, dn * an


def _mix_bwd(proj, dmixed, attn, probs, shares, conv_full, norm_conv, norm_attn):
    def body(pj_ref, kvp_ref, cch_ref, cuh_ref, dmx_ref, attn_ref, p_ref, ps_ref, cw_ref, gc_ref, ga_ref,
             dpj_ref, gslab_ref, dattn_scr, nxt_scr, dkv_scr, acc_scr):
        step = pl.program_id(0)
        n = N_BLOCKS - 1 - step
        pj = pj_ref

        @pl.when(step == 0)
        def _():
            gslab_ref[...] = jnp.zeros_like(gslab_ref)
            nxt_scr[...] = jnp.zeros_like(nxt_scr)
            dkv_scr[...] = jnp.zeros_like(dkv_scr)
            acc_scr[...] = jnp.zeros_like(acc_scr)

        zhalo = _conv_halo(cch_ref, cuh_ref, n)
        cw = (cw_ref[0:1, :], cw_ref[1:2, :], cw_ref[2:3, :])
        gain_c = gc_ref[...]
        row = lax.broadcasted_iota(jnp.int32, (CHUNK, D_CONV), 0)

        dco_after = nxt_scr[...]
        for r in reversed(range(N_CHUNKS)):
            rows = _chunk_rows(r)
            cc, cu, z, z1, z2, co = _conv_chunk(pj_ref, zhalo, cw, r)
            cb = pj_ref[rows, OFF_CB:OFF_CB + D_CONV]
            da, dgate, gterm = _gated_norm_bwd(cb * co, gain_c, pj_ref[rows, OFF_GC:OFF_GC + D_CONV],
                                               dmx_ref[rows, 0:D_CONV])
            dpj_ref[rows, OFF_GC:OFF_GC + D_CONV] = dgate.astype(BF16)
            dpj_ref[rows, OFF_CB:OFF_CB + D_CONV] = (da * co).astype(BF16)
            dco = da * cb
            dco1 = jnp.where(row >= CHUNK - 1, pltpu.roll(dco_after, CHUNK - 1, 0), pltpu.roll(dco, CHUNK - 1, 0))
            dco2 = jnp.where(row >= CHUNK - 2, pltpu.roll(dco_after, CHUNK - 2, 0), pltpu.roll(dco, CHUNK - 2, 0))
            dz = cw[2] * dco + cw[1] * dco1 + cw[0] * dco2
            dpj_ref[rows, OFF_CC:OFF_CC + D_CONV] = (dz * cu).astype(BF16)
            dpj_ref[rows, OFF_CU:OFF_CU + D_CONV] = (dz * cc).astype(BF16)
            acc_scr[ACC_NORM_CONV] += gterm
            acc_scr[ACC_CONV0] += dco * z2
            acc_scr[ACC_CONV0 + 1] += dco * z1
            acc_scr[ACC_CONV0 + 2] += dco * z
            dco_after = dco
        nxt_scr[...] = dco_after

        ks, vs = _kv_bands(pj, kvp_ref)
        gain_a = ga_ref[...]

        for r in range(N_CHUNKS):
            rows = _chunk_rows(r)
            da, dgate, gterm = _gated_norm_bwd(attn_ref[rows, :], gain_a, pj_ref[rows, OFF_GA:OFF_GA + D_ATTN],
                                               dmx_ref[rows, D_CONV:D_MIX])
            dpj_ref[rows, OFF_GA:OFF_GA + D_ATTN] = dgate.astype(BF16)
            dattn_scr[rows, :] = da
            acc_scr[ACC_NORM_ATTN] += gterm

        in_lo = lax.broadcasted_iota(jnp.int32, (128, 128), 0) < HEAD_DIM
        half_ones = (jnp.where(in_lo, 1.0, 0.0).astype(BF16), jnp.where(in_lo, 0.0, 1.0).astype(BF16))
        lane_s = lax.broadcasted_iota(jnp.int32, (1, D_MODEL), 1)
        gsink = jnp.zeros((1, D_MODEL), F32)
        dk_t, dv_t = [], []
        for j in range(2):
            q_stack = _q_stack(pj, j)
            do_f = jnp.concatenate([dattn_scr[:, _pair_cols(j, i, 0)] for i in range(PAIRS_PER_KV)], axis=0)
            o_f = jnp.concatenate([attn_ref[:, _pair_cols(j, i, 0)] for i in range(PAIRS_PER_KV)], axis=0)
            prod = (do_f * o_f).astype(BF16)
            deltas = [jnp.dot(prod, half_ones[e], preferred_element_type=F32) for e in range(2)]
            do_b = do_f.astype(BF16)
            q_t, do_t = q_stack.T, do_b.T
            dq, dk_j, dv_j = None, None, None
            for e in range(2):
                p = p_ref[0, 2 * j + e]
                dp = lax.dot_general(do_b, vs[j][e], _NT, preferred_element_type=F32)
                ds = []
                for i in range(PAIRS_PER_KV):
                    rows = slice(BLOCK * i, BLOCK * (i + 1))
                    delta = deltas[e][rows, :]
                    ds.append((p[rows, :].astype(F32) * (dp[rows, :] - jnp.concatenate([delta, delta], axis=1))).astype(BF16))
                    gs_h = -jnp.sum(ps_ref[0, 2 * j + e, rows, 0:1] * delta[:, 0:1], axis=0, keepdims=True)
                    gsink = gsink + jnp.where(lane_s == _head(j, i, e), gs_h, 0.0)
                ds = jnp.concatenate(ds, axis=0)
                t = jnp.dot(ds, ks[j][e], preferred_element_type=F32)
                dq = t if dq is None else dq + t
                half = slice(HEAD_DIM * e, HEAD_DIM * (e + 1))
                a = jnp.dot(q_t[half, :], ds, preferred_element_type=F32)
                b = jnp.dot(do_t[half, :], p, preferred_element_type=F32)
                dk_j = a if dk_j is None else dk_j + a
                dv_j = b if dv_j is None else dv_j + b
            for i in range(PAIRS_PER_KV):
                dpj_ref[:, _pair_cols(j, i, OFF_Q)] = (dq[BLOCK * i:BLOCK * (i + 1), :] * SCALE).astype(BF16)
            dk_t.append(dk_j)
            dv_t.append(dv_j)
        dk = jnp.concatenate(dk_t, axis=0).T
        dv = jnp.concatenate(dv_t, axis=0).T
        dpj_ref[:, OFF_K:OFF_K + D_KV] = (dk[BLOCK:, :] + dkv_scr[:, 0:D_KV]).astype(BF16)
        dpj_ref[:, OFF_V:OFF_V + D_KV] = (dv[BLOCK:, :] + dkv_scr[:, D_KV:2 * D_KV]).astype(BF16)
        dkv_scr[:, 0:D_KV] = dk[:BLOCK, :]
        dkv_scr[:, D_KV:2 * D_KV] = dv[:BLOCK, :]
        gslab_ref[ROW_SINKS:ROW_SINKS + 1, :] += gsink

        @pl.when(step == N_BLOCKS - 1)
        def _():
            for k, slab_row in ((ACC_NORM_CONV, ROW_NORM_CONV), (ACC_NORM_ATTN, ROW_NORM_ATTN), (ACC_CONV0, ROW_CONV0),
                                (ACC_CONV0 + 1, ROW_CONV0 + 1), (ACC_CONV0 + 2, ROW_CONV0 + 2)):
                gslab_ref[slab_row:slab_row + 1, :] = jnp.sum(acc_scr[k], axis=0, keepdims=True)

    per_block = BLOCK // HALO
    last = N_BLOCKS - 1
    return pl.pallas_call(
        body, name="mix_bwd", grid=(N_BLOCKS,),
        in_specs=[
            pl.BlockSpec((BLOCK, D_PROJ), lambda s: (last - s, 0)),
            pl.BlockSpec((BLOCK, 2 * D_KV), lambda s: (jnp.maximum(last - s - 1, 0), OFF_K // (2 * D_KV))),
            pl.BlockSpec((HALO, D_CONV), lambda s: (jnp.maximum((last - s) * per_block - 1, 0), OFF_CC // D_CONV)),
            pl.BlockSpec((HALO, D_CONV), lambda s: (jnp.maximum((last - s) * per_block - 1, 0), OFF_CU // D_CONV)),
            pl.BlockSpec((BLOCK, D_MIX), lambda s: (last - s, 0)),
            pl.BlockSpec((BLOCK, D_ATTN), lambda s: (last - s, 0)),
            pl.BlockSpec((1, 4, STACK, 2 * BLOCK), lambda s: (last - s, 0, 0, 0)),
            pl.BlockSpec((1, 4, STACK, 128), lambda s: (last - s, 0, 0, 0)),
            pl.BlockSpec((8, D_CONV), lambda s: (0, 0)),
            pl.BlockSpec((1, D_CONV), lambda s: (0, 0)),
            pl.BlockSpec((1, D_ATTN), lambda s: (0, 0)),
        ],
        out_specs=(pl.BlockSpec((BLOCK, D_PROJ), lambda s: (last - s, 0)),
                   pl.BlockSpec((8, D_MODEL), lambda s: (0, 0))),
        out_shape=(jax.ShapeDtypeStruct((SEQ, D_PROJ), BF16), jax.ShapeDtypeStruct((8, D_MODEL), F32)),
        scratch_shapes=[pltpu.VMEM((BLOCK, D_ATTN), F32), pltpu.VMEM((CHUNK, D_CONV), F32),
                        pltpu.VMEM((BLOCK, 2 * D_KV), F32), pltpu.VMEM((N_ACC, CHUNK, D_MODEL), F32)],
        compiler_params=_params(dimension_semantics=("arbitrary",)),
    )(proj, proj, proj, proj, dmixed, attn, probs, shares, conv_full, norm_conv, norm_attn)


def _in_bwd_rs(dproj, w_full, x, dx2, norm_in, dw_in_chip, gslab, gnf, loss_part):
    tm = 256
    steps = SEQ // tm

    def body(dp_ref, w_hbm, x_ref, dx2_ref, g_ref, dwi_ref, gs_ref, gnf_ref, lp_ref, gx_ref, gwin_ref, gsum_ref,
             gni_scr, own, d2d, ici, myslab, slabs, w_ref, send_sems, recv_sems, local_sems):
        i = pl.program_id(0)
        rs_start, rs_finish = _ici_sum(dwi_ref, own, d2d, ici, send_sems, recv_sems, local_sems)
        slab_start, slab_finish = _slab_sum(myslab, slabs, send_sems, recv_sems, N_ICI_SUM_SEMS)

        @pl.when(i == 0)
        def _():
            gni_scr[...] = jnp.zeros_like(gni_scr)
            rs_start()
            w_load = pltpu.make_async_copy(w_hbm, w_ref, local_sems.at[1])
            w_load.start()
            w_load.wait()

        dh = jnp.dot(dp_ref[...], w_ref[...], preferred_element_type=F32)
        xv = x_ref[...]
        r = lax.rsqrt(jnp.mean(xv * xv, axis=-1, keepdims=True) + RMS_EPS)
        xn = xv * r
        u = dh * g_ref[...]
        gx_ref[...] = dx2_ref[...] + r * (u - xn * jnp.mean(u * xn, axis=-1, keepdims=True))
        gni_scr[...] += jnp.sum(dh * xn, axis=0, keepdims=True)

        @pl.when(i == steps - 1)
        def _():
            row = lax.broadcasted_iota(jnp.int32, (8, D_MODEL), 0)
            lane = lax.broadcasted_iota(jnp.int32, (8, D_MODEL), 1)
            slab = jnp.where(row == ROW_NORM_IN, gni_scr[...], jnp.where(row == ROW_NORM_FINAL, gnf_ref[...], gs_ref[...]))
            myslab[...] = jnp.where((row == ROW_SINKS) & (lane == LOSS_LANE), lp_ref[0:1, 0:1], slab)
            slab_start()
            gwin_ref[...] = rs_finish()
            gsum_ref[...] = slab_finish()

    const = lambda i: (0, 0)
    return pl.pallas_call(
        body, name="in_bwd", grid=(steps,),
        in_specs=[pl.BlockSpec((tm, D_PROJ), lambda i: (i, 0)), pl.BlockSpec(memory_space=pl.ANY),
                  pl.BlockSpec((tm, D_MODEL), lambda i: (i, 0)), pl.BlockSpec((tm, D_MODEL), lambda i: (i, 0)),
                  pl.BlockSpec((1, D_MODEL), const), pl.BlockSpec(memory_space=pl.ANY),
                  pl.BlockSpec((8, D_MODEL), const), pl.BlockSpec((1, D_MODEL), const), pl.BlockSpec((8, 128), const)],
        out_specs=(pl.BlockSpec((tm, D_MODEL), lambda i: (i, 0)), pl.BlockSpec((SHARD_IN, D_MODEL), const),
                   pl.BlockSpec((8, D_MODEL), const)),
        out_shape=(jax.ShapeDtypeStruct((SEQ, D_MODEL), F32), jax.ShapeDtypeStruct((SHARD_IN, D_MODEL), F32),
                   jax.ShapeDtypeStruct((8, D_MODEL), F32)),
        scratch_shapes=[pltpu.VMEM((1, D_MODEL), F32), pltpu.VMEM((SHARD_IN, D_MODEL), BF16),
                        pltpu.VMEM((SHARD_IN, D_MODEL), BF16), pltpu.VMEM((2, SHARD_IN, D_MODEL), BF16),
                        pltpu.VMEM((8, D_MODEL), F32), pltpu.VMEM((N_DEV, 8, D_MODEL), F32),
                        pltpu.VMEM((D_PROJ, D_MODEL), BF16),
                        pltpu.SemaphoreType.DMA((N_ICI_SUM_SEMS + 7,)), pltpu.SemaphoreType.DMA((N_ICI_SUM_SEMS + 7,)),
                        pltpu.SemaphoreType.DMA((2,))],
        compiler_params=_params(dimension_semantics=("arbitrary",)),
    )(dproj, w_full, x, dx2, norm_in, dw_in_chip, gslab, gnf, loss_part)


def _dw_rs(mixed, dx2b, dproj, h, table):
    tn_out, tn = 2 * SHARD_OUT, IN_PROJ_TILE
    out_steps, in_steps = D_MIX // tn_out, D_PROJ // tn
    steps = out_steps + in_steps
    out_order = (DG, NX, NY, OWN)

    def out_tile(i):
        chip = 2 * lax.axis_index("x") + lax.axis_index("y")
        return jnp.bitwise_xor(chip, (out_steps - 1) - jnp.minimum(i, out_steps - 1))

    def in_tile(table_ref, i):
        return _dw_entry(table_ref, jnp.maximum(i - out_steps, 0))

    def body(table_ref, mx_ref, dxb_ref, a_ref, b_ref, chip_ref, gwo_ref, dwo, dwt, d2d_in, via, own, d2d, ici,
             send_sems, recv_sems, local_sems):
        i = pl.program_id(0)
        rs_start, rs_forward, rs_finish = _shard_sum(dwo, own, d2d, ici, send_sems, recv_sems, local_sems)
        before_tile, after_tiles, chip_finish = _chip_sum(
            dwt, d2d_in, via, chip_ref, lambda k: _dw_entry(table_ref, in_steps + k), send_sems, recv_sems, local_sems,
            N_SHARD_SUM_SEMS, 4)

        for j, k in enumerate(out_order):
            @pl.when(i == j + 1)
            def _():
                rs_start(k)

            if k != OWN:
                @pl.when(i == j + 2)
                def _():
                    rs_forward(k)

        @pl.when(i < out_steps)
        def _():
            tile = lax.dot_general(mx_ref[...], dxb_ref[...], _TN, preferred_element_type=F32).astype(BF16)
            for core in range(2):
                dwo[2 * out_tile(i) + core] = tile[SHARD_OUT * core:SHARD_OUT * (core + 1), :]

        @pl.when(i >= out_steps)
        def _():
            before_tile(i - out_steps)
            tile = lax.dot_general(a_ref[...], b_ref[...], _TN, preferred_element_type=F32).astype(BF16)
            dwt[pl.ds(pl.multiple_of(in_tile(table_ref, i) * tn, tn), tn), :] = tile

        @pl.when(i == steps - 1)
        def _():
            after_tiles()
            gwo_ref[...] = rs_finish()
            chip_finish()

    vmem = pl.BlockSpec(memory_space=pltpu.VMEM)
    grid_spec = pltpu.PrefetchScalarGridSpec(
        num_scalar_prefetch=1, grid=(steps,),
        in_specs=[pl.BlockSpec((SEQ, tn_out), lambda i, table_ref: (0, out_tile(i))), vmem,
                  pl.BlockSpec((SEQ, tn), lambda i, table_ref: (0, in_tile(table_ref, i))), vmem],
        out_specs=(pl.BlockSpec(memory_space=pl.ANY), pl.BlockSpec((SHARD_OUT, D_MODEL), lambda i, table_ref: (0, 0))),
        scratch_shapes=[pltpu.VMEM((N_DEV, SHARD_OUT, D_MODEL), BF16),
                        pltpu.VMEM((D_PROJ, D_MODEL), BF16), pltpu.VMEM((3, SHARD_IN, D_MODEL), BF16),
                        pltpu.VMEM((2, HALF_IN, D_MODEL), BF16),
                        *_shard_sum_scratch(SHARD_OUT),
                        pltpu.SemaphoreType.DMA((N_SHARD_SUM_SEMS + N_CHIP_SUM_SEMS,)),
                        pltpu.SemaphoreType.DMA((N_SHARD_SUM_SEMS + N_CHIP_SUM_SEMS,)),
                        pltpu.SemaphoreType.DMA((8,))])
    return pl.pallas_call(
        body, name="dw", grid_spec=grid_spec,
        out_shape=(jax.ShapeDtypeStruct((4, SHARD_IN, D_MODEL), BF16), jax.ShapeDtypeStruct((SHARD_OUT, D_MODEL), F32)),
        compiler_params=_params(dimension_semantics=("arbitrary",)),
    )(table, mixed, dx2b, dproj, h)


def _adam_all(big_in, big_out, gsum, small, grad_x):
    steps = 4
    tr_in, tr_out = SHARD_IN // steps, SHARD_OUT // steps

    def body(*refs):
        ins, outs = refs[:8 + 1 + 18 + 1], refs[8 + 1 + 18 + 1:]
        i = pl.program_id(0)
        outs[33][...] = ins[27][...]
        for b in range(2):
            w_ref, g_ref, m_ref, v_ref = ins[4 * b:4 * b + 4]
            g = g_ref[...]
            delta, mn, vn = _adamw(w_ref[...], g, m_ref[...], v_ref[...])
            for ref, val in zip(outs[4 * b:4 * b + 4], (g, delta, mn, vn)):
                ref[...] = val

        @pl.when(i == 0)
        def _():
            gsum = ins[8][...]
            idx = _slot(lax.axis_index("x"), lax.axis_index("y"), lax.axis_index("c"))
            cg = jnp.zeros((3, SHARD_CONV), F32)
            for d in range(N_DEV):
                cg = jnp.where(idx == d, gsum[ROW_CONV0:ROW_CONV0 + 3, d * SHARD_CONV:(d + 1) * SHARD_CONV], cg)
            grads = (gsum[ROW_NORM_IN:ROW_NORM_IN + 1], gsum[ROW_SINKS:ROW_SINKS + 1, 0:N_Q_HEADS],
                     gsum[ROW_NORM_CONV:ROW_NORM_CONV + 1], gsum[ROW_NORM_ATTN:ROW_NORM_ATTN + 1],
                     gsum[ROW_NORM_FINAL:ROW_NORM_FINAL + 1], cg)
            for s, g in enumerate(grads):
                at = (slice(None), 0, slice(None)) if s == 5 else (slice(None), slice(None))
                w_ref, m_ref, v_ref = ins[9 + 3 * s:12 + 3 * s]
                delta, mn, vn = _adamw(w_ref[at], g, m_ref[at], v_ref[at])
                for ref, val in zip(outs[8 + 4 * s:12 + 4 * s], (g, delta, mn, vn)):
                    ref[at] = val
            outs[32][...] = gsum[ROW_SINKS:ROW_SINKS + 1, LOSS_LANE:LOSS_LANE + 1]

    const = lambda i: (0, 0)
    rows = lambda i: (i, 0)
    whole = lambda shape: pl.BlockSpec(shape, lambda i: (0,) * len(shape))
    small_shapes = [a.shape for a in small[::3]]
    in_specs = ([pl.BlockSpec((tr_in, D_MODEL), rows)] * 4 + [pl.BlockSpec((tr_out, D_MODEL), rows)] * 4
                + [pl.BlockSpec((8, D_MODEL), const)] + [whole(a.shape) for a in small]
                + [pl.BlockSpec((SEQ // steps, D_MODEL), rows)])
    out_specs = ([pl.BlockSpec((tr_in, D_MODEL), rows)] * 4 + [pl.BlockSpec((tr_out, D_MODEL), rows)] * 4
                 + [whole(s) for s in small_shapes for _ in range(4)] + [pl.BlockSpec((1, 1), const)]
                 + [pl.BlockSpec((SEQ // steps, D_MODEL), rows)])
    out_shape = ([jax.ShapeDtypeStruct((SHARD_IN, D_MODEL), F32)] * 4 + [jax.ShapeDtypeStruct((SHARD_OUT, D_MODEL), F32)] * 4
                 + [jax.ShapeDtypeStruct(s, F32) for s in small_shapes for _ in range(4)]
                 + [jax.ShapeDtypeStruct((1, 1), F32), jax.ShapeDtypeStruct((SEQ, D_MODEL), F32)])
    outs = pl.pallas_call(
        body, name="adam", grid=(steps,), in_specs=in_specs, out_specs=tuple(out_specs), out_shape=tuple(out_shape),
        compiler_params=_params(dimension_semantics=("arbitrary",)),
    )(*big_in, *big_out, gsum, *small, grad_x)
    return outs[0:4], outs[4:8], [outs[8 + 4 * s:12 + 4 * s] for s in range(6)], outs[32], outs[33]


def _rows_first(a):
    return jnp.transpose(a, (1, 0, 2))


def kernel(x, norm_in, w_in, conv_w, attn_sinks, norm_conv_out, norm_attn_out, w_out, norm_final, loss_target, m_norm_in, m_w_in, m_conv_w, m_attn_sinks, m_norm_conv_out, m_norm_attn_out, m_w_out, m_norm_final, v_norm_in, v_w_in, v_conv_w, v_attn_sinks, v_norm_conv_out, v_norm_attn_out, v_w_out, v_norm_final):
    x2d = x.reshape(SEQ, D_MODEL)
    target = loss_target.reshape(SEQ, D_MODEL)
    nf = norm_final.reshape(1, D_MODEL)

    w_in_t, m_w_in_t, v_w_in_t = w_in[0].T, m_w_in[0].T, v_w_in[0].T
    tiles = jnp.asarray(TILE_ORDER, jnp.int32).reshape(-1)
    w_in_full, h, proj, w_out_b, conv_full = _gather_in_proj(x2d, norm_in, w_in_t, w_out[0], _rows_first(conv_w), tiles)
    sinks = attn_sinks.reshape(N_Q_HEADS)

    mixed, attn, probs, shares, g_out = _mix_fwd(proj, conv_full, sinks, norm_conv_out, norm_attn_out, w_out_b)
    dx2, dx2b, dmixed, gnf, loss_part = _out_proj_loss(mixed, x2d, target, g_out.reshape(D_MIX, D_MODEL), nf)
    dproj, gslab = _mix_bwd(proj, dmixed, attn, probs, shares, conv_full, norm_conv_out, norm_attn_out)
    dw_in_chip, g_w_out = _dw_rs(mixed, dx2b, dproj, h, jnp.asarray(DW_TABLE, jnp.int32).reshape(-1))
    grad_x, g_w_in, gsum = _in_bwd_rs(dproj, w_in_full, x2d, dx2, norm_in, dw_in_chip, gslab, gnf, loss_part)

    small = (norm_in, m_norm_in, v_norm_in, attn_sinks, m_attn_sinks, v_attn_sinks,
             norm_conv_out, m_norm_conv_out, v_norm_conv_out, norm_attn_out, m_norm_attn_out, v_norm_attn_out,
             nf, m_norm_final.reshape(1, D_MODEL), v_norm_final.reshape(1, D_MODEL),
             _rows_first(conv_w), _rows_first(m_conv_w), _rows_first(v_conv_w))
    big_in, big_out, (s_ni, s_sk, s_nc, s_na, s_nf, s_cv), loss, grad_x = _adam_all(
        (w_in_t, g_w_in, m_w_in_t, v_w_in_t), (w_out[0], g_w_out, m_w_out[0], v_w_out[0]), gsum, small, grad_x)

    def leaves(k):
        return (s_ni[k], big_in[k].T[None], jnp.transpose(s_cv[k], (1, 0, 2)), s_sk[k], s_nc[k], s_na[k], big_out[k][None],
                s_nf[k].reshape(D_MODEL))

    return (loss.reshape(()), grad_x.reshape(1, SEQ, D_MODEL), *leaves(0), *leaves(1), *leaves(2), *leaves(3))
```

```python
import jax
import jax.numpy as jnp
from jax import lax
from jax.experimental import pallas as pl
from jax.experimental.pallas import tpu as pltpu

F32 = jnp.float32
BF16 = jnp.bfloat16
MESH = pl.DeviceIdType.MESH

N_DEV = 8
SEQ = 2048
D_MODEL = 1024
D_CONV = 1024
D_ATTN = 1024
D_KV = 128
HEAD_DIM = 64
N_Q_HEADS = 16
N_PAIRS = N_Q_HEADS // 2
PAIRS_PER_KV = N_PAIRS // 2
D_MIX = D_CONV + D_ATTN
D_PROJ = 6400
SHARD_IN = D_PROJ // N_DEV
SHARD_OUT = D_MIX // N_DEV
SHARD_CONV = D_CONV // N_DEV
OFF_CB, OFF_CC, OFF_CU, OFF_GC, OFF_Q, OFF_K, OFF_V, OFF_GA = 0, 1024, 2048, 3072, 4096, 5120, 5248, 5376
BLOCK = 128
N_BLOCKS = SEQ // BLOCK
HALO = 8
CHUNK = 16
N_CHUNKS = BLOCK // CHUNK
RMS_EPS = 1e-5
NEG = -1e30
SCALE = HEAD_DIM ** -0.5
SLOPES = tuple(2.0 ** (-8.0 * (h + 1) / N_Q_HEADS) for h in range(N_Q_HEADS))

ADAM_LR = 0.001
ADAM_B1 = 0.9
ADAM_B2 = 0.999
ADAM_EPS = 1e-08
ADAM_WD = 0.01
ADAM_STEP = 10

ROW_NORM_IN, ROW_NORM_CONV, ROW_NORM_ATTN, ROW_NORM_FINAL, ROW_CONV0, ROW_SINKS = 0, 1, 2, 3, 4, 7
LOSS_LANE = N_Q_HEADS
ACC_NORM_CONV, ACC_NORM_ATTN, ACC_CONV0, N_ACC = 0, 1, 2, 5

VMEM_LIMIT = 56 * 1024 * 1024

_NT = (((1,), (1,)), ((), ()))
_TN = (((0,), (0,)), ((), ()))


def _params(**kw):
    return pltpu.CompilerParams(vmem_limit_bytes=VMEM_LIMIT, **kw)


def _adamw(w, g, m, v):
    m = ADAM_B1 * m + (1.0 - ADAM_B1) * g
    v = ADAM_B2 * v + (1.0 - ADAM_B2) * (g * g)
    m_hat = m / (1.0 - ADAM_B1 ** ADAM_STEP)
    v_hat = v / (1.0 - ADAM_B2 ** ADAM_STEP)
    delta = -ADAM_LR * (m_hat / (jnp.sqrt(v_hat) + ADAM_EPS) + ADAM_WD * w)
    return delta, m, v


def _sigmoid(t):
    return 1.0 / (1.0 + jnp.exp(-t))


def _slot(px, py, pc):
    return 4 * px + 2 * py + pc


OWN, NX, NY, DG = range(4)
HALF_IN = SHARD_IN // 2
N_GATHER_KINDS = 13
W_OUT_KINDS = N_GATHER_KINDS + 7


IN_PROJ_TILE = 640
TILE_ORDER = ((0, 1, 2, 3, 4, 5, 6, 7, 8, 9), (3, 4, 0, 1, 2, 8, 9, 5, 6, 7),
              (5, 6, 0, 1, 7, 8, 9, 2, 3, 4), (8, 9, 3, 4, 5, 6, 7, 0, 1, 2))
TILES_OWN, TILES_NEIGHBOURS = 2, 7


def _tile(table_ref, p):
    chip = 2 * lax.axis_index("x") + lax.axis_index("y")
    return table_ref[chip * len(TILE_ORDER[0]) + p]


DW_TILE_ORDER = tuple(tuple(reversed(row)) for row in TILE_ORDER)


def _tiles_until_complete(chip, owner):
    lo, hi = owner * 2 * SHARD_IN, (owner + 1) * 2 * SHARD_IN
    touching = [t for t in range(len(TILE_ORDER[0])) if t * IN_PROJ_TILE < hi and (t + 1) * IN_PROJ_TILE > lo]
    return 1 + max(DW_TILE_ORDER[chip].index(t) for t in touching)


DW_TABLE = tuple(DW_TILE_ORDER[chip] + tuple(_tiles_until_complete(chip, chip ^ flip) for flip in (0, 2, 1, 3))
                 for chip in range(4))


def _dw_entry(table_ref, p):
    chip = 2 * lax.axis_index("x") + lax.axis_index("y")
    return table_ref[chip * len(DW_TABLE[0]) + p]


def _gather_in_proj(x, norm_in, w_in_sh, w_out_sh, conv_sh, tiles):
    tn = IN_PROJ_TILE
    steps = D_PROJ // tn
    tm = 256

    def body(tiles_ref, x_hbm, g_ref, win_ref, wout_ref, cv_ref, wt_ref, h_ref, proj_ref, gout_ref, conv_ref,
             gin_ref, gcv_ref, wob_ref, x_ref, send_sems, recv_sems, local_sems):
        p = pl.program_id(0)
        local_sem = local_sems.at[0]
        x, y, c = lax.axis_index("x"), lax.axis_index("y"), lax.axis_index("c")
        me, sibling = (x, y, c), (x, y, 1 - c)
        nx, ny, dg = (1 - x, y, c), (x, 1 - y, c), (1 - x, 1 - y, c)

        def other(dev):
            return (dev[0], dev[1], 1 - dev[2])

        def shard(dev):
            return gin_ref.at[pl.ds(pl.multiple_of(_slot(*dev) * SHARD_IN, 16), SHARD_IN), :]

        def half(dev, h):
            return gin_ref.at[pl.ds(pl.multiple_of(_slot(*dev) * SHARD_IN + h * HALF_IN, 16), HALF_IN), :]

        def rc(ref, k, to):
            return pltpu.make_async_remote_copy(src_ref=ref, dst_ref=ref, send_sem=send_sems.at[k],
                                                recv_sem=recv_sems.at[k], device_id=to, device_id_type=MESH)

        def cv(k, dev, to):
            s = _slot(*dev)
            return pltpu.make_async_remote_copy(src_ref=gcv_ref.at[s], dst_ref=gcv_ref.at[s],
                                                send_sem=send_sems.at[N_GATHER_KINDS + k],
                                                recv_sem=recv_sems.at[N_GATHER_KINDS + k], device_id=to, device_id_type=MESH)

        def own_copies():
            return [rc(shard(me), 0, sibling),
                    rc(half(me, 0), 1, nx), rc(half(me, 1), 2, nx),
                    rc(half(me, 1), 4, ny), rc(half(me, 0), 3, ny),
                    cv(0, me, sibling)] + [cv(1 + j, me, peer) for j, peer in enumerate((nx, ny, dg))]

        def pass_on(dev, h, k_in, k_ici, k_d2d, half=half, base=0):
            rc(half(dev, h), base + k_in, me).wait_recv()
            if k_ici is not None:
                rc(half(dev, h), base + k_ici, ny if dev is nx else nx).start()
            rc(half(dev, h), base + k_d2d, sibling).start()

        def out_half(dev, h):
            return gout_ref.at[_slot(*dev), pl.ds(h * (SHARD_OUT // 2), SHARD_OUT // 2), :]

        def own_out_copies():
            src = lambda h: wob_ref.at[pl.ds(h * (SHARD_OUT // 2), SHARD_OUT // 2), :]

            def send(ref, dst, k, to):
                return pltpu.make_async_remote_copy(src_ref=ref, dst_ref=dst, send_sem=send_sems.at[W_OUT_KINDS + k],
                                                    recv_sem=recv_sems.at[W_OUT_KINDS + k], device_id=to, device_id_type=MESH)

            return [send(wob_ref, gout_ref.at[_slot(*me)], 0, sibling),
                    send(src(0), out_half(me, 0), 1, nx), send(src(1), out_half(me, 1), 2, nx),
                    send(src(1), out_half(me, 1), 4, ny), send(src(0), out_half(me, 0), 3, ny)]

        def own_out_local():
            return pltpu.make_async_copy(wob_ref, gout_ref.at[_slot(*me)], local_sems.at[1])

        @pl.when(p == 0)
        def _():
            gin_ref[pl.ds(pl.multiple_of(_slot(*me) * SHARD_IN, 16), SHARD_IN), :] = win_ref[...].astype(BF16)
            gcv_ref[_slot(*me)] = jnp.zeros((8, SHARD_CONV), F32)
            gcv_ref[_slot(*me), 0:3, :] = cv_ref[:, 0, :]
            for cp in own_copies():
                cp.start()
            wob_ref[...] = wout_ref[...].astype(BF16)
            x_load = pltpu.make_async_copy(x_hbm, x_ref, local_sems.at[2])
            x_load.start()
            x_load.wait()
            for t in range(SEQ // tm):
                xv = x_ref[tm * t:tm * (t + 1), :]
                r = lax.rsqrt(jnp.mean(xv * xv, axis=-1, keepdims=True) + RMS_EPS)
                h_ref[tm * t:tm * (t + 1), :] = (xv * r * g_ref[...]).astype(BF16)
            rc(shard(sibling), 0, me).wait_recv()

        @pl.when(p == TILES_OWN)
        def _():
            for args in ((nx, 0, 1, 5, 7), (ny, 1, 4, 6, 10), (nx, 1, 2, None, 8), (ny, 0, 3, None, 9)):
                pass_on(*args)
            for j, peer in enumerate((nx, ny, dg)):
                cv(1 + j, peer, me).wait_recv()
                cv(4 + j, peer, sibling).start()
            for (dev, h), k in (((nx, 0), 7), ((nx, 1), 8), ((ny, 0), 9), ((ny, 1), 10)):
                rc(half(other(dev), h), k, me).wait_recv()
            own_out_local().start()
            for cp in own_out_copies():
                cp.start()

        @pl.when(p == TILES_NEIGHBOURS - 1)
        def _():
            pass_on(dg, 0, 5, None, 11)
            pass_on(dg, 1, 6, None, 12)

        @pl.when(p == TILES_NEIGHBOURS)
        def _():
            for (dev, h), k in (((dg, 0), 11), ((dg, 1), 12)):
                rc(half(other(dev), h), k, me).wait_recv()
            pltpu.make_async_copy(gin_ref, wt_ref, local_sem).start()

        @pl.when(p == steps - 2)
        def _():
            for args in ((nx, 0, 1, 5, 7), (ny, 1, 4, 6, 10), (nx, 1, 2, None, 8), (ny, 0, 3, None, 9)):
                pass_on(*args, half=out_half, base=W_OUT_KINDS)

        w = gin_ref[pl.ds(pl.multiple_of(_tile(tiles_ref, p) * tn, tn), tn), :]
        proj_ref[...] = lax.dot_general(h_ref[...], w, _NT, preferred_element_type=F32)

        @pl.when(p == steps - 1)
        def _():
            cv(0, sibling, me).wait_recv()
            for j, peer in enumerate((nx, ny, dg)):
                cv(4 + j, other(peer), me).wait_recv()
            for d in range(N_DEV):
                conv_ref[:, d * SHARD_CONV:(d + 1) * SHARD_CONV] = gcv_ref[d]
            relayed = [rc(half(nx, 0), 5, ny), rc(half(ny, 1), 6, nx)]
            relayed += [rc(half(dev, h), k, sibling) for (dev, h), k in
                        (((nx, 0), 7), ((nx, 1), 8), ((ny, 0), 9), ((ny, 1), 10), ((dg, 0), 11), ((dg, 1), 12))]
            relayed += [cv(4 + j, peer, sibling) for j, peer in enumerate((nx, ny, dg))]
            for cp in own_copies() + relayed:
                cp.wait_send()
            pltpu.make_async_copy(gin_ref, wt_ref, local_sem).wait()
            pass_on(dg, 0, 5, None, 11, half=out_half, base=W_OUT_KINDS)
            pass_on(dg, 1, 6, None, 12, half=out_half, base=W_OUT_KINDS)
            rc(gout_ref.at[_slot(*sibling)], W_OUT_KINDS, me).wait_recv()
            out_relayed = [rc(out_half(nx, 0), W_OUT_KINDS + 5, ny), rc(out_half(ny, 1), W_OUT_KINDS + 6, nx)]
            for (dev, h), k in (((nx, 0), 7), ((nx, 1), 8), ((ny, 0), 9), ((ny, 1), 10), ((dg, 0), 11), ((dg, 1), 12)):
                rc(out_half(other(dev), h), W_OUT_KINDS + k, me).wait_recv()
                out_relayed.append(rc(out_half(dev, h), W_OUT_KINDS + k, sibling))
            for cp in own_out_copies() + out_relayed:
                cp.wait_send()
            own_out_local().wait()

    vmem = pl.BlockSpec(memory_space=pltpu.VMEM)
    grid_spec = pltpu.PrefetchScalarGridSpec(
        num_scalar_prefetch=1, grid=(steps,),
        in_specs=[pl.BlockSpec(memory_space=pl.ANY), vmem, vmem, vmem, vmem],
        out_specs=(pl.BlockSpec(memory_space=pl.ANY), vmem,
                   pl.BlockSpec((SEQ, tn), lambda p, tiles_ref: (0, _tile(tiles_ref, p))),
                   pl.BlockSpec(memory_space=pl.ANY), vmem),
        scratch_shapes=[pltpu.VMEM((D_PROJ, D_MODEL), BF16), pltpu.VMEM((N_DEV, 8, SHARD_CONV), F32),
                        pltpu.VMEM((SHARD_OUT, D_MODEL), BF16), pltpu.VMEM((SEQ, D_MODEL), F32),
                        pltpu.SemaphoreType.DMA((W_OUT_KINDS + N_GATHER_KINDS,)),
                        pltpu.SemaphoreType.DMA((W_OUT_KINDS + N_GATHER_KINDS,)),
                        pltpu.SemaphoreType.DMA((3,))])
    return pl.pallas_call(
        body, name="gather_in_proj", grid_spec=grid_spec,
        out_shape=(jax.ShapeDtypeStruct((D_PROJ, D_MODEL), BF16), jax.ShapeDtypeStruct((SEQ, D_MODEL), BF16),
                   jax.ShapeDtypeStruct((SEQ, D_PROJ), F32), jax.ShapeDtypeStruct((N_DEV, SHARD_OUT, D_MODEL), BF16),
                   jax.ShapeDtypeStruct((8, D_CONV), F32)),
        compiler_params=_params(dimension_semantics=("arbitrary",)),
    )(tiles, x, norm_in, w_in_sh, w_out_sh, conv_sh)


def _shard_sum(src, own, d2d, ici, send_sems, recv_sems, local_sems, base=0):
    x, y, c = lax.axis_index("x"), lax.axis_index("y"), lax.axis_index("c")
    sibling = (x, y, 1 - c)
    chips = [(x, y), (1 - x, y), (x, 1 - y), (1 - x, 1 - y)]

    def rcopy(s, d, k, to):
        return pltpu.make_async_remote_copy(src_ref=s, dst_ref=d, send_sem=send_sems.at[base + k],
                                            recv_sem=recv_sems.at[base + k], device_id=to, device_id_type=MESH)

    def mine(k):
        return pltpu.make_async_copy(src.at[_slot(*chips[k], c)], own.at[k], local_sems.at[k])

    def to_sibling(k):
        return rcopy(src.at[_slot(*chips[k], 1 - c)], d2d.at[k], k, sibling)

    def to_chip(k):
        return rcopy(own.at[k], ici.at[k - 1], 3 + k, (*chips[k], c))

    def start(k):
        mine(k).start()
        to_sibling(k).start()

    def forward(k):
        mine(k).wait()
        to_sibling(k).wait_recv()
        own[k] = (own[k].astype(F32) + d2d[k].astype(F32)).astype(BF16)
        to_chip(k).start()

    def finish():
        mine(0).wait()
        to_sibling(0).wait_recv()
        acc = own[0].astype(F32) + d2d[0].astype(F32)
        for k in range(1, 4):
            to_chip(k).wait_recv()
            acc = acc + ici[k - 1].astype(F32)
        for k in range(4):
            to_sibling(k).wait_send()
        for k in range(1, 4):
            to_chip(k).wait_send()
        return acc

    return start, forward, finish


def _shard_sum_scratch(rows):
    return [pltpu.VMEM((4, rows, D_MODEL), BF16), pltpu.VMEM((4, rows, D_MODEL), BF16),
            pltpu.VMEM((3, rows, D_MODEL), BF16)]


N_SHARD_SUM_SEMS = 7


N_CHIP_SUM_SEMS = 5


def _chip_sum(dwt, d2d, via, out_hbm, tiles_until, send_sems, recv_sems, local_sems, base, local_base):
    x, y, c = lax.axis_index("x"), lax.axis_index("y"), lax.axis_index("c")
    sibling, nx, ny = (x, y, 1 - c), (1 - x, y, c), (x, 1 - y, c)
    chips = [(x, y), (1 - x, y), (x, 1 - y), (1 - x, 1 - y)]

    def shard(s):
        return dwt.at[pl.ds(pl.multiple_of(s * SHARD_IN, 16), SHARD_IN), :]

    def half(ref, h):
        return ref.at[pl.ds(h * HALF_IN, HALF_IN), :]

    def rc(s, d, k, to):
        return pltpu.make_async_remote_copy(src_ref=s, dst_ref=d, send_sem=send_sems.at[base + k],
                                            recv_sem=recv_sems.at[base + k], device_id=to, device_id_type=MESH)

    def to_sibling(k):
        return rc(shard(_slot(*chips[k], 1 - c)), d2d.at[k - 1], k - 1, sibling)

    for_dg = (lambda: rc(half(d2d.at[DG - 1], 0), via.at[0], 3, nx), lambda: rc(half(d2d.at[DG - 1], 1), via.at[1], 4, ny))

    def save(k):
        return pltpu.make_async_copy(d2d.at[k - 1], out_hbm.at[k], local_sems.at[local_base + k])

    own_saves = (lambda: pltpu.make_async_copy(shard(_slot(x, y, c)), out_hbm.at[OWN], local_sems.at[local_base]),
                 lambda: pltpu.make_async_copy(shard(_slot(x, y, 1 - c)), out_hbm.at[3], local_sems.at[local_base + 3]))

    def before_tile(n):
        for k in (NX, NY, DG):
            @pl.when(tiles_until(k) == n)
            def _():
                to_sibling(k).start()

            @pl.when(tiles_until(k) + 1 == n)
            def _():
                to_sibling(k).wait_recv()
                d2d[k - 1] = (shard(_slot(*chips[k], c))[...].astype(F32) + d2d[k - 1].astype(F32)).astype(BF16)
                if k == DG:
                    for cp in for_dg:
                        cp().start()

    def after_tiles():
        for cp in own_saves:
            cp().start()

    def finish():
        for k, h in ((NY, 0), (NX, 1)):
            for_dg[h]().wait_recv()
            rows = pl.ds(h * HALF_IN, HALF_IN)
            d2d[k - 1, rows, :] = (d2d[k - 1, rows, :].astype(F32) + via[h].astype(F32)).astype(BF16)
            save(k).start()
        for cp in own_saves + (lambda: save(NX), lambda: save(NY)):
            cp().wait()
        for cp in (lambda: to_sibling(NX), lambda: to_sibling(NY), lambda: to_sibling(DG)) + for_dg:
            cp().wait_send()

    return before_tile, after_tiles, finish


N_ICI_SUM_SEMS = 11


def _ici_sum(src, own, d2d, ici, send_sems, recv_sems, local_sems, base=0):
    x, y, c = lax.axis_index("x"), lax.axis_index("y"), lax.axis_index("c")

    def rc(s, d, k, to):
        return pltpu.make_async_remote_copy(src_ref=s, dst_ref=d, send_sem=send_sems.at[base + k],
                                            recv_sem=recv_sems.at[base + k], device_id=to, device_id_type=MESH)

    def part(k, q, to):
        rows = pl.ds(q * (SHARD_IN // 5), SHARD_IN // 5)
        return lambda: rc(src.at[k, rows, :], ici.at[k - 1, rows, :], 1 + 5 * (k - 1) + q, to)

    copies = (tuple(part(NX, q, (1 - x, y, c)) for q in range(5)) + tuple(part(NY, q, (x, 1 - y, c)) for q in range(5))
              + (lambda: rc(src.at[3], d2d, 0, (x, y, 1 - c)),))
    mine = lambda: pltpu.make_async_copy(src.at[OWN], own, local_sems.at[0])

    def start():
        for cp in copies + (mine,):
            cp().start()

    def finish():
        mine().wait()
        for cp in copies:
            cp().wait_recv()
        acc = own[...].astype(F32) + d2d[...].astype(F32) + ici[0].astype(F32) + ici[1].astype(F32)
        for cp in copies:
            cp().wait_send()
        return acc

    return start, finish


def _slab_sum(myslab, slabs, send_sems, recv_sems, base):
    x, y, c = lax.axis_index("x"), lax.axis_index("y"), lax.axis_index("c")
    me = _slot(x, y, c)
    peers = [(x, y, 1 - c), (1 - x, y, c), (x, 1 - y, c), (1 - x, 1 - y, c),
             (1 - x, y, 1 - c), (x, 1 - y, 1 - c), (1 - x, 1 - y, 1 - c)]

    def cp(k):
        return pltpu.make_async_remote_copy(src_ref=myslab, dst_ref=slabs.at[me], send_sem=send_sems.at[base + k],
                                            recv_sem=recv_sems.at[base + k], device_id=peers[k], device_id_type=MESH)

    def start():
        slabs[me] = myslab[...]
        for k in range(7):
            cp(k).start()

    def finish():
        for k in range(7):
            cp(k).wait_recv()
        total = slabs[0]
        for d in range(1, N_DEV):
            total = total + slabs[d]
        for k in range(7):
            cp(k).wait_send()
        return total

    return start, finish


def _chunk_rows(r):
    return slice(r * CHUNK, (r + 1) * CHUNK)


def _conv_halo(cch_ref, cuh_ref, n):
    zh = jnp.where(n > 0, cch_ref[...] * cuh_ref[...], 0.0)
    return jnp.concatenate([zh] * (CHUNK // HALO), axis=0)


def _conv_chunk(pj_ref, zhalo, cw, r):
    rows = _chunk_rows(r)
    cc = pj_ref[rows, OFF_CC:OFF_CC + D_CONV]
    cu = pj_ref[rows, OFF_CU:OFF_CU + D_CONV]
    z = cc * cu
    before = _chunk_rows(r - 1)
    zprev = pj_ref[before, OFF_CC:OFF_CC + D_CONV] * pj_ref[before, OFF_CU:OFF_CU + D_CONV] if r > 0 else zhalo
    row = lax.broadcasted_iota(jnp.int32, (CHUNK, D_CONV), 0)
    z1 = jnp.where(row < 1, pltpu.roll(zprev, 1, 0), pltpu.roll(z, 1, 0))
    z2 = jnp.where(row < 2, pltpu.roll(zprev, 2, 0), pltpu.roll(z, 2, 0))
    co = cw[0] * z2 + cw[1] * z1 + cw[2] * z
    return cc, cu, z, z1, z2, co


def _gated_norm(a, gain, t):
    r = lax.rsqrt(jnp.mean(a * a, axis=-1, keepdims=True) + RMS_EPS)
    return a * r * gain * (t * _sigmoid(t))


def _kv_bands(pj, kvp_ref):
    lane = lax.broadcasted_iota(jnp.int32, (2 * BLOCK, D_KV), 1)
    lo = lane < HEAD_DIM

    def bands(prev, cur):
        b = jnp.concatenate([prev, cur], axis=0)
        br = pltpu.roll(b, HEAD_DIM, 1)
        zero = jnp.zeros_like(b)
        return ((jnp.where(lo, b, zero).astype(BF16), jnp.where(lo, zero, br).astype(BF16)),
                (jnp.where(lo, br, zero).astype(BF16), jnp.where(lo, zero, b).astype(BF16)))

    ks = bands(kvp_ref[:, 0:D_KV], pj[:, OFF_K:OFF_K + D_KV])
    vs = bands(kvp_ref[:, D_KV:2 * D_KV], pj[:, OFF_V:OFF_V + D_KV])
    return ks, vs


STACK = PAIRS_PER_KV * BLOCK


def _head(j, i, e):
    return 2 * (PAIRS_PER_KV * j + i) + e


def _pair_cols(j, i, off):
    p = PAIRS_PER_KV * j + i
    return slice(off + 128 * p, off + 128 * (p + 1))


def _fill_attn_bias(bias_scr, first_block):
    qi = lax.broadcasted_iota(jnp.int32, (BLOCK, 2 * BLOCK), 0)
    kj = lax.broadcasted_iota(jnp.int32, (BLOCK, 2 * BLOCK), 1)
    dist = BLOCK + qi - kj
    valid = (dist >= 0) & (dist < BLOCK)
    if first_block:
        valid = valid & (kj >= BLOCK)
    distf = dist.astype(F32)
    for j in range(2):
        for e in range(2):
            for i in range(PAIRS_PER_KV):
                bias_scr[2 * j + e, BLOCK * i:BLOCK * (i + 1), :] = jnp.where(valid, -SLOPES[_head(j, i, e)] * distf, NEG)


def _q_stack(pj, j):
    return jnp.concatenate([(pj[:, _pair_cols(j, i, OFF_Q)] * SCALE).astype(BF16) for i in range(PAIRS_PER_KV)], axis=0)


def _attn_probs(q_stack, kband, bias_ref, sinks):
    s = lax.dot_general(q_stack, kband, _NT, preferred_element_type=F32)
    ones = jnp.ones((128, 128), BF16)
    probs, shares = [], []
    for i, sink in enumerate(sinks):
        rows = slice(BLOCK * i, BLOCK * (i + 1))
        t = s[rows, :] + bias_ref[rows, :]
        m = jnp.broadcast_to(jnp.max(t, axis=-1, keepdims=True), (BLOCK, 128))
        m = jnp.maximum(m, sink)
        p = [jnp.exp(t[:, :128] - m), jnp.exp(t[:, 128:] - m)]
        es = jnp.exp(sink - m)
        total = (jnp.dot(p[0].astype(BF16), ones, preferred_element_type=F32)
                 + jnp.dot(p[1].astype(BF16), ones, preferred_element_type=F32))
        inv = 1.0 / (total + es)
        probs.append(jnp.concatenate([p[0] * inv, p[1] * inv], axis=1))
        shares.append(es * inv)
    return jnp.concatenate(probs, axis=0), jnp.concatenate(shares, axis=0)


def _attn_group(pj, ks, vs, bias_scr, sink_ref, j):
    q_stack = _q_stack(pj, j)
    out, probs, shares = None, [], []
    for e in range(2):
        p, ps = _attn_probs(q_stack, ks[j][e], bias_scr.at[2 * j + e],
                            [sink_ref[_head(j, i, e)] for i in range(PAIRS_PER_KV)])
        p = p.astype(BF16)
        o = jnp.dot(p, vs[j][e], preferred_element_type=F32)
        out = o if out is None else out + o
        probs.append(p)
        shares.append(ps)
    return out, probs, shares


def _mix_fwd(proj, conv_full, sinks, norm_conv, norm_attn):
    def body(pj_ref, kvp_ref, cch_ref, cuh_ref, cw_ref, sink_ref, gc_ref, ga_ref,
             mixed_ref, attn_scr, p_ref, ps_ref, bias_scr):
        n = pl.program_id(0)
        pj = pj_ref

        @pl.when(n == 0)
        def _():
            _fill_attn_bias(bias_scr, first_block=True)

        @pl.when(n == 1)
        def _():
            _fill_attn_bias(bias_scr, first_block=False)

        zhalo = _conv_halo(cch_ref, cuh_ref, n)
        cw = (cw_ref[0:1, :], cw_ref[1:2, :], cw_ref[2:3, :])
        gain_c = gc_ref[...]

        for r in range(N_CHUNKS):
            rows = _chunk_rows(r)
            co = _conv_chunk(pj_ref, zhalo, cw, r)[-1]
            y = _gated_norm(pj_ref[rows, OFF_CB:OFF_CB + D_CONV] * co, gain_c, pj_ref[rows, OFF_GC:OFF_GC + D_CONV])
            mixed_ref[rows, 0:D_CONV] = y.astype(BF16)

        ks, vs = _kv_bands(pj, kvp_ref)
        for j in range(2):
            out, probs, shares = _attn_group(pj, ks, vs, bias_scr, sink_ref, j)
            for e in range(2):
                p_ref[0, 2 * j + e] = probs[e]
                ps_ref[0, 2 * j + e] = shares[e]
            for i in range(PAIRS_PER_KV):
                attn_scr[:, _pair_cols(j, i, 0)] = out[BLOCK * i:BLOCK * (i + 1), :]
        gain_a = ga_ref[...]

        for r in range(N_CHUNKS):
            rows = _chunk_rows(r)
            y = _gated_norm(attn_scr[rows, :], gain_a, pj_ref[rows, OFF_GA:OFF_GA + D_ATTN])
            mixed_ref[rows, D_CONV:D_MIX] = y.astype(BF16)

    per_block = BLOCK // HALO
    return pl.pallas_call(
        body, name="mix_fwd", grid=(N_BLOCKS,),
        in_specs=[
            pl.BlockSpec((BLOCK, D_PROJ), lambda n: (n, 0)),
            pl.BlockSpec((BLOCK, 2 * D_KV), lambda n: (jnp.maximum(n - 1, 0), OFF_K // (2 * D_KV))),
            pl.BlockSpec((HALO, D_CONV), lambda n: (jnp.maximum(n * per_block - 1, 0), OFF_CC // D_CONV)),
            pl.BlockSpec((HALO, D_CONV), lambda n: (jnp.maximum(n * per_block - 1, 0), OFF_CU // D_CONV)),
            pl.BlockSpec((8, D_CONV), lambda n: (0, 0)),
            pl.BlockSpec(memory_space=pltpu.SMEM),
            pl.BlockSpec((1, D_CONV), lambda n: (0, 0)),
            pl.BlockSpec((1, D_ATTN), lambda n: (0, 0)),
        ],
        out_specs=(pl.BlockSpec((BLOCK, D_MIX), lambda n: (n, 0)), pl.BlockSpec((BLOCK, D_ATTN), lambda n: (n, 0)),
                   pl.BlockSpec((1, 4, STACK, 2 * BLOCK), lambda n: (n, 0, 0, 0)),
                   pl.BlockSpec((1, 4, STACK, 128), lambda n: (n, 0, 0, 0))),
        out_shape=(jax.ShapeDtypeStruct((SEQ, D_MIX), BF16), jax.ShapeDtypeStruct((SEQ, D_ATTN), F32),
                   jax.ShapeDtypeStruct((N_BLOCKS, 4, STACK, 2 * BLOCK), BF16),
                   jax.ShapeDtypeStruct((N_BLOCKS, 4, STACK, 128), F32)),
        scratch_shapes=[pltpu.VMEM((4, STACK, 2 * BLOCK), F32)],
        compiler_params=_params(dimension_semantics=("arbitrary",)),
    )(proj, proj, proj, proj, conv_full, sinks, norm_conv, norm_attn)


def _out_proj_loss(mixed, x, target, w_out_full, norm_final):
    tm = 256

    def body(mx_ref, x_ref, t_ref, w_ref, g_ref, dx2_ref, dx2b_ref, dmix_ref, gnf_ref, loss_ref):
        i = pl.program_id(0)
        w = w_ref[...]
        x2 = x_ref[...] + jnp.dot(mx_ref[...], w, preferred_element_type=F32)
        r = lax.rsqrt(jnp.mean(x2 * x2, axis=-1, keepdims=True) + RMS_EPS)
        xn = x2 * r
        g = g_ref[...]
        err = xn * g - t_ref[...]
        part = 0.5 * jnp.sum(jnp.mean(err * err, axis=-1, keepdims=True), axis=0, keepdims=True)
        dy = err * (1.0 / D_MODEL)
        gnf = jnp.sum(dy * xn, axis=0, keepdims=True)
        u = dy * g
        dx2 = r * (u - xn * jnp.mean(u * xn, axis=-1, keepdims=True))
        dx2_ref[...] = dx2
        dx2b = dx2.astype(BF16)
        dx2b_ref[...] = dx2b
        dmix_ref[...] = lax.dot_general(dx2b, w, _NT, preferred_element_type=F32)

        @pl.when(i == 0)
        def _():
            gnf_ref[...] = jnp.zeros_like(gnf_ref)
            loss_ref[...] = jnp.zeros_like(loss_ref)

        gnf_ref[...] += gnf
        loss_ref[...] += jnp.broadcast_to(part, loss_ref.shape)

    return pl.pallas_call(
        body, name="out_proj_loss", grid=(SEQ // tm,),
        in_specs=[pl.BlockSpec((tm, D_MIX), lambda i: (i, 0)), pl.BlockSpec((tm, D_MODEL), lambda i: (i, 0)),
                  pl.BlockSpec((tm, D_MODEL), lambda i: (i, 0)), pl.BlockSpec(memory_space=pltpu.VMEM),
                  pl.BlockSpec((1, D_MODEL), lambda i: (0, 0))],
        out_specs=(pl.BlockSpec((tm, D_MODEL), lambda i: (i, 0)), pl.BlockSpec((tm, D_MODEL), lambda i: (i, 0)),
                   pl.BlockSpec((tm, D_MIX), lambda i: (i, 0)),
                   pl.BlockSpec((1, D_MODEL), lambda i: (0, 0)), pl.BlockSpec((8, 128), lambda i: (0, 0))),
        out_shape=(jax.ShapeDtypeStruct((SEQ, D_MODEL), F32), jax.ShapeDtypeStruct((SEQ, D_MODEL), BF16),
                   jax.ShapeDtypeStruct((SEQ, D_MIX), F32),
                   jax.ShapeDtypeStruct((1, D_MODEL), F32), jax.ShapeDtypeStruct((8, 128), F32)),
        compiler_params=_params(dimension_semantics=("arbitrary",)),
    )(mixed, x, target, w_out_full, norm_final)


def _gated_norm_bwd(a, gain, t, dy):
    r = lax.rsqrt(jnp.mean(a * a, axis=-1, keepdims=True) + RMS_EPS)
    an = a * r
    sg = _sigmoid(t)
    dn = dy * (t * sg)
    dt = dy * (an * gain) * (sg * (1.0 + t * (1.0 - sg)))
    u = dn * gain
    da = r * (u - an * jnp.mean(u * an, axis=-1, keepdims=True))
    return da, dt, dn * an


def _mix_bwd(proj, dmixed, attn, probs, shares, conv_full, norm_conv, norm_attn):
    def body(pj_ref, kvp_ref, cch_ref, cuh_ref, dmx_ref, attn_ref, p_ref, ps_ref, cw_ref, gc_ref, ga_ref,
             dpj_ref, gslab_ref, dattn_scr, nxt_scr, dkv_scr, acc_scr):
        step = pl.program_id(0)
        n = N_BLOCKS - 1 - step
        pj = pj_ref

        @pl.when(step == 0)
        def _():
            gslab_ref[...] = jnp.zeros_like(gslab_ref)
            nxt_scr[...] = jnp.zeros_like(nxt_scr)
            dkv_scr[...] = jnp.zeros_like(dkv_scr)
            acc_scr[...] = jnp.zeros_like(acc_scr)

        zhalo = _conv_halo(cch_ref, cuh_ref, n)
        cw = (cw_ref[0:1, :], cw_ref[1:2, :], cw_ref[2:3, :])
        gain_c = gc_ref[...]
        row = lax.broadcasted_iota(jnp.int32, (CHUNK, D_CONV), 0)

        dco_after = nxt_scr[...]
        for r in reversed(range(N_CHUNKS)):
            rows = _chunk_rows(r)
            cc, cu, z, z1, z2, co = _conv_chunk(pj_ref, zhalo, cw, r)
            cb = pj_ref[rows, OFF_CB:OFF_CB + D_CONV]
            da, dgate, gterm = _gated_norm_bwd(cb * co, gain_c, pj_ref[rows, OFF_GC:OFF_GC + D_CONV],
                                               dmx_ref[rows, 0:D_CONV])
            dpj_ref[rows, OFF_GC:OFF_GC + D_CONV] = dgate.astype(BF16)
            dpj_ref[rows, OFF_CB:OFF_CB + D_CONV] = (da * co).astype(BF16)
            dco = da * cb
            dco1 = jnp.where(row >= CHUNK - 1, pltpu.roll(dco_after, CHUNK - 1, 0), pltpu.roll(dco, CHUNK - 1, 0))
            dco2 = jnp.where(row >= CHUNK - 2, pltpu.roll(dco_after, CHUNK - 2, 0), pltpu.roll(dco, CHUNK - 2, 0))
            dz = cw[2] * dco + cw[1] * dco1 + cw[0] * dco2
            dpj_ref[rows, OFF_CC:OFF_CC + D_CONV] = (dz * cu).astype(BF16)
            dpj_ref[rows, OFF_CU:OFF_CU + D_CONV] = (dz * cc).astype(BF16)
            acc_scr[ACC_NORM_CONV] += gterm
            acc_scr[ACC_CONV0] += dco * z2
            acc_scr[ACC_CONV0 + 1] += dco * z1
            acc_scr[ACC_CONV0 + 2] += dco * z
            dco_after = dco
        nxt_scr[...] = dco_after

        ks, vs = _kv_bands(pj, kvp_ref)
        gain_a = ga_ref[...]

        for r in range(N_CHUNKS):
            rows = _chunk_rows(r)
            da, dgate, gterm = _gated_norm_bwd(attn_ref[rows, :], gain_a, pj_ref[rows, OFF_GA:OFF_GA + D_ATTN],
                                               dmx_ref[rows, D_CONV:D_MIX])
            dpj_ref[rows, OFF_GA:OFF_GA + D_ATTN] = dgate.astype(BF16)
            dattn_scr[rows, :] = da
            acc_scr[ACC_NORM_ATTN] += gterm

        in_lo = lax.broadcasted_iota(jnp.int32, (128, 128), 0) < HEAD_DIM
        half_ones = (jnp.where(in_lo, 1.0, 0.0).astype(BF16), jnp.where(in_lo, 0.0, 1.0).astype(BF16))
        lane_s = lax.broadcasted_iota(jnp.int32, (1, D_MODEL), 1)
        gsink = jnp.zeros((1, D_MODEL), F32)
        dk_t, dv_t = [], []
        for j in range(2):
            q_stack = _q_stack(pj, j)
            do_f = jnp.concatenate([dattn_scr[:, _pair_cols(j, i, 0)] for i in range(PAIRS_PER_KV)], axis=0)
            o_f = jnp.concatenate([attn_ref[:, _pair_cols(j, i, 0)] for i in range(PAIRS_PER_KV)], axis=0)
            prod = (do_f * o_f).astype(BF16)
            deltas = [jnp.dot(prod, half_ones[e], preferred_element_type=F32) for e in range(2)]
            do_b = do_f.astype(BF16)
            q_t, do_t = q_stack.T, do_b.T
            dq, dk_j, dv_j = None, None, None
            for e in range(2):
                p = p_ref[0, 2 * j + e]
                dp = lax.dot_general(do_b, vs[j][e], _NT, preferred_element_type=F32)
                ds = []
                for i in range(PAIRS_PER_KV):
                    rows = slice(BLOCK * i, BLOCK * (i + 1))
                    delta = deltas[e][rows, :]
                    ds.append((p[rows, :].astype(F32) * (dp[rows, :] - jnp.concatenate([delta, delta], axis=1))).astype(BF16))
                    gs_h = -jnp.sum(ps_ref[0, 2 * j + e, rows, 0:1] * delta[:, 0:1], axis=0, keepdims=True)
                    gsink = gsink + jnp.where(lane_s == _head(j, i, e), gs_h, 0.0)
                ds = jnp.concatenate(ds, axis=0)
                t = jnp.dot(ds, ks[j][e], preferred_element_type=F32)
                dq = t if dq is None else dq + t
                half = slice(HEAD_DIM * e, HEAD_DIM * (e + 1))
                a = jnp.dot(q_t[half, :], ds, preferred_element_type=F32)
                b = jnp.dot(do_t[half, :], p, preferred_element_type=F32)
                dk_j = a if dk_j is None else dk_j + a
                dv_j = b if dv_j is None else dv_j + b
            for i in range(PAIRS_PER_KV):
                dpj_ref[:, _pair_cols(j, i, OFF_Q)] = (dq[BLOCK * i:BLOCK * (i + 1), :] * SCALE).astype(BF16)
            dk_t.append(dk_j)
            dv_t.append(dv_j)
        dk = jnp.concatenate(dk_t, axis=0).T
        dv = jnp.concatenate(dv_t, axis=0).T
        dpj_ref[:, OFF_K:OFF_K + D_KV] = (dk[BLOCK:, :] + dkv_scr[:, 0:D_KV]).astype(BF16)
        dpj_ref[:, OFF_V:OFF_V + D_KV] = (dv[BLOCK:, :] + dkv_scr[:, D_KV:2 * D_KV]).astype(BF16)
        dkv_scr[:, 0:D_KV] = dk[:BLOCK, :]
        dkv_scr[:, D_KV:2 * D_KV] = dv[:BLOCK, :]
        gslab_ref[ROW_SINKS:ROW_SINKS + 1, :] += gsink

        @pl.when(step == N_BLOCKS - 1)
        def _():
            for k, slab_row in ((ACC_NORM_CONV, ROW_NORM_CONV), (ACC_NORM_ATTN, ROW_NORM_ATTN), (ACC_CONV0, ROW_CONV0),
                                (ACC_CONV0 + 1, ROW_CONV0 + 1), (ACC_CONV0 + 2, ROW_CONV0 + 2)):
                gslab_ref[slab_row:slab_row + 1, :] = jnp.sum(acc_scr[k], axis=0, keepdims=True)

    per_block = BLOCK // HALO
    last = N_BLOCKS - 1
    return pl.pallas_call(
        body, name="mix_bwd", grid=(N_BLOCKS,),
        in_specs=[
            pl.BlockSpec((BLOCK, D_PROJ), lambda s: (last - s, 0)),
            pl.BlockSpec((BLOCK, 2 * D_KV), lambda s: (jnp.maximum(last - s - 1, 0), OFF_K // (2 * D_KV))),
            pl.BlockSpec((HALO, D_CONV), lambda s: (jnp.maximum((last - s) * per_block - 1, 0), OFF_CC // D_CONV)),
            pl.BlockSpec((HALO, D_CONV), lambda s: (jnp.maximum((last - s) * per_block - 1, 0), OFF_CU // D_CONV)),
            pl.BlockSpec((BLOCK, D_MIX), lambda s: (last - s, 0)),
            pl.BlockSpec((BLOCK, D_ATTN), lambda s: (last - s, 0)),
            pl.BlockSpec((1, 4, STACK, 2 * BLOCK), lambda s: (last - s, 0, 0, 0)),
            pl.BlockSpec((1, 4, STACK, 128), lambda s: (last - s, 0, 0, 0)),
            pl.BlockSpec((8, D_CONV), lambda s: (0, 0)),
            pl.BlockSpec((1, D_CONV), lambda s: (0, 0)),
            pl.BlockSpec((1, D_ATTN), lambda s: (0, 0)),
        ],
        out_specs=(pl.BlockSpec((BLOCK, D_PROJ), lambda s: (last - s, 0)),
                   pl.BlockSpec((8, D_MODEL), lambda s: (0, 0))),
        out_shape=(jax.ShapeDtypeStruct((SEQ, D_PROJ), BF16), jax.ShapeDtypeStruct((8, D_MODEL), F32)),
        scratch_shapes=[pltpu.VMEM((BLOCK, D_ATTN), F32), pltpu.VMEM((CHUNK, D_CONV), F32),
                        pltpu.VMEM((BLOCK, 2 * D_KV), F32), pltpu.VMEM((N_ACC, CHUNK, D_MODEL), F32)],
        compiler_params=_params(dimension_semantics=("arbitrary",)),
    )(proj, proj, proj, proj, dmixed, attn, probs, shares, conv_full, norm_conv, norm_attn)


def _in_bwd_rs(dproj, w_full, x, dx2, norm_in, dw_in_chip, gslab, gnf, loss_part):
    tm = 256
    steps = SEQ // tm

    def body(dp_ref, w_hbm, x_ref, dx2_ref, g_ref, dwi_ref, gs_ref, gnf_ref, lp_ref, gx_ref, gwin_ref, gsum_ref,
             gni_scr, own, d2d, ici, myslab, slabs, w_ref, send_sems, recv_sems, local_sems):
        i = pl.program_id(0)
        rs_start, rs_finish = _ici_sum(dwi_ref, own, d2d, ici, send_sems, recv_sems, local_sems)
        slab_start, slab_finish = _slab_sum(myslab, slabs, send_sems, recv_sems, N_ICI_SUM_SEMS)

        @pl.when(i == 0)
        def _():
            gni_scr[...] = jnp.zeros_like(gni_scr)
            rs_start()
            w_load = pltpu.make_async_copy(w_hbm, w_ref, local_sems.at[1])
            w_load.start()
            w_load.wait()

        dh = jnp.dot(dp_ref[...], w_ref[...], preferred_element_type=F32)
        xv = x_ref[...]
        r = lax.rsqrt(jnp.mean(xv * xv, axis=-1, keepdims=True) + RMS_EPS)
        xn = xv * r
        u = dh * g_ref[...]
        gx_ref[...] = dx2_ref[...] + r * (u - xn * jnp.mean(u * xn, axis=-1, keepdims=True))
        gni_scr[...] += jnp.sum(dh * xn, axis=0, keepdims=True)

        @pl.when(i == steps - 1)
        def _():
            row = lax.broadcasted_iota(jnp.int32, (8, D_MODEL), 0)
            lane = lax.broadcasted_iota(jnp.int32, (8, D_MODEL), 1)
            slab = jnp.where(row == ROW_NORM_IN, gni_scr[...], jnp.where(row == ROW_NORM_FINAL, gnf_ref[...], gs_ref[...]))
            myslab[...] = jnp.where((row == ROW_SINKS) & (lane == LOSS_LANE), lp_ref[0:1, 0:1], slab)
            slab_start()
            gwin_ref[...] = rs_finish()
            gsum_ref[...] = slab_finish()

    const = lambda i: (0, 0)
    return pl.pallas_call(
        body, name="in_bwd", grid=(steps,),
        in_specs=[pl.BlockSpec((tm, D_PROJ), lambda i: (i, 0)), pl.BlockSpec(memory_space=pl.ANY),
                  pl.BlockSpec((tm, D_MODEL), lambda i: (i, 0)), pl.BlockSpec((tm, D_MODEL), lambda i: (i, 0)),
                  pl.BlockSpec((1, D_MODEL), const), pl.BlockSpec(memory_space=pl.ANY),
                  pl.BlockSpec((8, D_MODEL), const), pl.BlockSpec((1, D_MODEL), const), pl.BlockSpec((8, 128), const)],
        out_specs=(pl.BlockSpec((tm, D_MODEL), lambda i: (i, 0)), pl.BlockSpec((SHARD_IN, D_MODEL), const),
                   pl.BlockSpec((8, D_MODEL), const)),
        out_shape=(jax.ShapeDtypeStruct((SEQ, D_MODEL), F32), jax.ShapeDtypeStruct((SHARD_IN, D_MODEL), F32),
                   jax.ShapeDtypeStruct((8, D_MODEL), F32)),
        scratch_shapes=[pltpu.VMEM((1, D_MODEL), F32), pltpu.VMEM((SHARD_IN, D_MODEL), BF16),
                        pltpu.VMEM((SHARD_IN, D_MODEL), BF16), pltpu.VMEM((2, SHARD_IN, D_MODEL), BF16),
                        pltpu.VMEM((8, D_MODEL), F32), pltpu.VMEM((N_DEV, 8, D_MODEL), F32),
                        pltpu.VMEM((D_PROJ, D_MODEL), BF16),
                        pltpu.SemaphoreType.DMA((N_ICI_SUM_SEMS + 7,)), pltpu.SemaphoreType.DMA((N_ICI_SUM_SEMS + 7,)),
                        pltpu.SemaphoreType.DMA((2,))],
        compiler_params=_params(dimension_semantics=("arbitrary",)),
    )(dproj, w_full, x, dx2, norm_in, dw_in_chip, gslab, gnf, loss_part)


def _dw_rs(mixed, dx2b, dproj, h, table):
    tn_out, tn = 2 * SHARD_OUT, IN_PROJ_TILE
    out_steps, in_steps = D_MIX // tn_out, D_PROJ // tn
    steps = out_steps + in_steps
    out_order = (DG, NX, NY, OWN)

    def out_tile(i):
        chip = 2 * lax.axis_index("x") + lax.axis_index("y")
        return jnp.bitwise_xor(chip, (out_steps - 1) - jnp.minimum(i, out_steps - 1))

    def in_tile(table_ref, i):
        return _dw_entry(table_ref, jnp.maximum(i - out_steps, 0))

    def body(table_ref, mx_ref, dxb_ref, a_ref, b_ref, chip_ref, gwo_ref, dwo, dwt, d2d_in, via, own, d2d, ici,
             send_sems, recv_sems, local_sems):
        i = pl.program_id(0)
        rs_start, rs_forward, rs_finish = _shard_sum(dwo, own, d2d, ici, send_sems, recv_sems, local_sems)
        before_tile, after_tiles, chip_finish = _chip_sum(
            dwt, d2d_in, via, chip_ref, lambda k: _dw_entry(table_ref, in_steps + k), send_sems, recv_sems, local_sems,
            N_SHARD_SUM_SEMS, 4)

        for j, k in enumerate(out_order):
            @pl.when(i == j + 1)
            def _():
                rs_start(k)

            if k != OWN:
                @pl.when(i == j + 2)
                def _():
                    rs_forward(k)

        @pl.when(i < out_steps)
        def _():
            tile = lax.dot_general(mx_ref[...], dxb_ref[...], _TN, preferred_element_type=F32).astype(BF16)
            for core in range(2):
                dwo[2 * out_tile(i) + core] = tile[SHARD_OUT * core:SHARD_OUT * (core + 1), :]

        @pl.when(i >= out_steps)
        def _():
            before_tile(i - out_steps)
            tile = lax.dot_general(a_ref[...], b_ref[...], _TN, preferred_element_type=F32).astype(BF16)
            dwt[pl.ds(pl.multiple_of(in_tile(table_ref, i) * tn, tn), tn), :] = tile

        @pl.when(i == steps - 1)
        def _():
            after_tiles()
            gwo_ref[...] = rs_finish()
            chip_finish()

    vmem = pl.BlockSpec(memory_space=pltpu.VMEM)
    grid_spec = pltpu.PrefetchScalarGridSpec(
        num_scalar_prefetch=1, grid=(steps,),
        in_specs=[pl.BlockSpec((SEQ, tn_out), lambda i, table_ref: (0, out_tile(i))), vmem,
                  pl.BlockSpec((SEQ, tn), lambda i, table_ref: (0, in_tile(table_ref, i))), vmem],
        out_specs=(pl.BlockSpec(memory_space=pl.ANY), pl.BlockSpec((SHARD_OUT, D_MODEL), lambda i, table_ref: (0, 0))),
        scratch_shapes=[pltpu.VMEM((N_DEV, SHARD_OUT, D_MODEL), BF16),
                        pltpu.VMEM((D_PROJ, D_MODEL), BF16), pltpu.VMEM((3, SHARD_IN, D_MODEL), BF16),
                        pltpu.VMEM((2, HALF_IN, D_MODEL), BF16),
                        *_shard_sum_scratch(SHARD_OUT),
                        pltpu.SemaphoreType.DMA((N_SHARD_SUM_SEMS + N_CHIP_SUM_SEMS,)),
                        pltpu.SemaphoreType.DMA((N_SHARD_SUM_SEMS + N_CHIP_SUM_SEMS,)),
                        pltpu.SemaphoreType.DMA((8,))])
    return pl.pallas_call(
        body, name="dw", grid_spec=grid_spec,
        out_shape=(jax.ShapeDtypeStruct((4, SHARD_IN, D_MODEL), BF16), jax.ShapeDtypeStruct((SHARD_OUT, D_MODEL), F32)),
        compiler_params=_params(dimension_semantics=("arbitrary",)),
    )(table, mixed, dx2b, dproj, h)


def _adam_all(big_in, big_out, gsum, small, grad_x):
    steps = 4
    tr_in, tr_out = SHARD_IN // steps, SHARD_OUT // steps

    def body(*refs):
        ins, outs = refs[:8 + 1 + 18 + 1], refs[8 + 1 + 18 + 1:]
        i = pl.program_id(0)
        outs[33][...] = ins[27][...]
        for b in range(2):
            w_ref, g_ref, m_ref, v_ref = ins[4 * b:4 * b + 4]
            g = g_ref[...]
            delta, mn, vn = _adamw(w_ref[...], g, m_ref[...], v_ref[...])
            for ref, val in zip(outs[4 * b:4 * b + 4], (g, delta, mn, vn)):
                ref[...] = val

        @pl.when(i == 0)
        def _():
            gsum = ins[8][...]
            idx = _slot(lax.axis_index("x"), lax.axis_index("y"), lax.axis_index("c"))
            cg = jnp.zeros((3, SHARD_CONV), F32)
            for d in range(N_DEV):
                cg = jnp.where(idx == d, gsum[ROW_CONV0:ROW_CONV0 + 3, d * SHARD_CONV:(d + 1) * SHARD_CONV], cg)
            grads = (gsum[ROW_NORM_IN:ROW_NORM_IN + 1], gsum[ROW_SINKS:ROW_SINKS + 1, 0:N_Q_HEADS],
                     gsum[ROW_NORM_CONV:ROW_NORM_CONV + 1], gsum[ROW_NORM_ATTN:ROW_NORM_ATTN + 1],
                     gsum[ROW_NORM_FINAL:ROW_NORM_FINAL + 1], cg)
            for s, g in enumerate(grads):
                at = (slice(None), 0, slice(None)) if s == 5 else (slice(None), slice(None))
                w_ref, m_ref, v_ref = ins[9 + 3 * s:12 + 3 * s]
                delta, mn, vn = _adamw(w_ref[at], g, m_ref[at], v_ref[at])
                for ref, val in zip(outs[8 + 4 * s:12 + 4 * s], (g, delta, mn, vn)):
                    ref[at] = val
            outs[32][...] = gsum[ROW_SINKS:ROW_SINKS + 1, LOSS_LANE:LOSS_LANE + 1]

    const = lambda i: (0, 0)
    rows = lambda i: (i, 0)
    whole = lambda shape: pl.BlockSpec(shape, lambda i: (0,) * len(shape))
    small_shapes = [a.shape for a in small[::3]]
    in_specs = ([pl.BlockSpec((tr_in, D_MODEL), rows)] * 4 + [pl.BlockSpec((tr_out, D_MODEL), rows)] * 4
                + [pl.BlockSpec((8, D_MODEL), const)] + [whole(a.shape) for a in small]
                + [pl.BlockSpec((SEQ // steps, D_MODEL), rows)])
    out_specs = ([pl.BlockSpec((tr_in, D_MODEL), rows)] * 4 + [pl.BlockSpec((tr_out, D_MODEL), rows)] * 4
                 + [whole(s) for s in small_shapes for _ in range(4)] + [pl.BlockSpec((1, 1), const)]
                 + [pl.BlockSpec((SEQ // steps, D_MODEL), rows)])
    out_shape = ([jax.ShapeDtypeStruct((SHARD_IN, D_MODEL), F32)] * 4 + [jax.ShapeDtypeStruct((SHARD_OUT, D_MODEL), F32)] * 4
                 + [jax.ShapeDtypeStruct(s, F32) for s in small_shapes for _ in range(4)]
                 + [jax.ShapeDtypeStruct((1, 1), F32), jax.ShapeDtypeStruct((SEQ, D_MODEL), F32)])
    outs = pl.pallas_call(
        body, name="adam", grid=(steps,), in_specs=in_specs, out_specs=tuple(out_specs), out_shape=tuple(out_shape),
        compiler_params=_params(dimension_semantics=("arbitrary",)),
    )(*big_in, *big_out, gsum, *small, grad_x)
    return outs[0:4], outs[4:8], [outs[8 + 4 * s:12 + 4 * s] for s in range(6)], outs[32], outs[33]


def _rows_first(a):
    return jnp.transpose(a, (1, 0, 2))


def kernel(x, norm_in, w_in, conv_w, attn_sinks, norm_conv_out, norm_attn_out, w_out, norm_final, loss_target, m_norm_in, m_w_in, m_conv_w, m_attn_sinks, m_norm_conv_out, m_norm_attn_out, m_w_out, m_norm_final, v_norm_in, v_w_in, v_conv_w, v_attn_sinks, v_norm_conv_out, v_norm_attn_out, v_w_out, v_norm_final):
    x2d = x.reshape(SEQ, D_MODEL)
    target = loss_target.reshape(SEQ, D_MODEL)
    nf = norm_final.reshape(1, D_MODEL)

    w_in_t, m_w_in_t, v_w_in_t = w_in[0].T, m_w_in[0].T, v_w_in[0].T
    tiles = jnp.asarray(TILE_ORDER, jnp.int32).reshape(-1)
    w_in_full, h, proj, g_out, conv_full = _gather_in_proj(x2d, norm_in, w_in_t, w_out[0], _rows_first(conv_w), tiles)
    sinks = attn_sinks.reshape(N_Q_HEADS)

    mixed, attn, probs, shares = _mix_fwd(proj, conv_full, sinks, norm_conv_out, norm_attn_out)
    dx2, dx2b, dmixed, gnf, loss_part = _out_proj_loss(mixed, x2d, target, g_out.reshape(D_MIX, D_MODEL), nf)
    dproj, gslab = _mix_bwd(proj, dmixed, attn, probs, shares, conv_full, norm_conv_out, norm_attn_out)
    dw_in_chip, g_w_out = _dw_rs(mixed, dx2b, dproj, h, jnp.asarray(DW_TABLE, jnp.int32).reshape(-1))
    grad_x, g_w_in, gsum = _in_bwd_rs(dproj, w_in_full, x2d, dx2, norm_in, dw_in_chip, gslab, gnf, loss_part)

    small = (norm_in, m_norm_in, v_norm_in, attn_sinks, m_attn_sinks, v_attn_sinks,
             norm_conv_out, m_norm_conv_out, v_norm_conv_out, norm_attn_out, m_norm_attn_out, v_norm_attn_out,
             nf, m_norm_final.reshape(1, D_MODEL), v_norm_final.reshape(1, D_MODEL),
             _rows_first(conv_w), _rows_first(m_conv_w), _rows_first(v_conv_w))
    big_in, big_out, (s_ni, s_sk, s_nc, s_na, s_nf, s_cv), loss, grad_x = _adam_all(
        (w_in_t, g_w_in, m_w_in_t, v_w_in_t), (w_out[0], g_w_out, m_w_out[0], v_w_out[0]), gsum, small, grad_x)

    def leaves(k):
        return (s_ni[k], big_in[k].T[None], jnp.transpose(s_cv[k], (1, 0, 2)), s_sk[k], s_nc[k], s_na[k], big_out[k][None],
                s_nf[k].reshape(D_MODEL))

    return (loss.reshape(()), grad_x.reshape(1, SEQ, D_MODEL), *leaves(0), *leaves(1), *leaves(2), *leaves(3))
```

```python
import jax
import jax.numpy as jnp
from jax import lax
from jax.experimental import pallas as pl
from jax.experimental.pallas import tpu as pltpu

F32 = jnp.float32
BF16 = jnp.bfloat16
MESH = pl.DeviceIdType.MESH

N_DEV = 8
SEQ = 2048
D_MODEL = 1024
D_CONV = 1024
D_ATTN = 1024
D_KV = 128
HEAD_DIM = 64
N_Q_HEADS = 16
N_PAIRS = N_Q_HEADS // 2
PAIRS_PER_KV = N_PAIRS // 2
D_MIX = D_CONV + D_ATTN
D_PROJ = 6400
SHARD_IN = D_PROJ // N_DEV
SHARD_OUT = D_MIX // N_DEV
SHARD_CONV = D_CONV // N_DEV
OFF_CB, OFF_CC, OFF_CU, OFF_GC, OFF_Q, OFF_K, OFF_V, OFF_GA = 0, 1024, 2048, 3072, 4096, 5120, 5248, 5376
BLOCK = 128
N_BLOCKS = SEQ // BLOCK
HALO = 8
CHUNK = 16
N_CHUNKS = BLOCK // CHUNK
RMS_EPS = 1e-5
NEG = -1e30
SCALE = HEAD_DIM ** -0.5
SLOPES = tuple(2.0 ** (-8.0 * (h + 1) / N_Q_HEADS) for h in range(N_Q_HEADS))

ADAM_LR = 0.001
ADAM_B1 = 0.9
ADAM_B2 = 0.999
ADAM_EPS = 1e-08
ADAM_WD = 0.01
ADAM_STEP = 10

ROW_NORM_IN, ROW_NORM_CONV, ROW_NORM_ATTN, ROW_NORM_FINAL, ROW_CONV0, ROW_SINKS = 0, 1, 2, 3, 4, 7
LOSS_LANE = N_Q_HEADS
ACC_NORM_CONV, ACC_NORM_ATTN, ACC_CONV0, N_ACC = 0, 1, 2, 5

VMEM_LIMIT = 56 * 1024 * 1024

_NT = (((1,), (1,)), ((), ()))
_TN = (((0,), (0,)), ((), ()))


def _params(**kw):
    return pltpu.CompilerParams(vmem_limit_bytes=VMEM_LIMIT, **kw)


def _adamw(w, g, m, v):
    m = ADAM_B1 * m + (1.0 - ADAM_B1) * g
    v = ADAM_B2 * v + (1.0 - ADAM_B2) * (g * g)
    m_hat = m / (1.0 - ADAM_B1 ** ADAM_STEP)
    v_hat = v / (1.0 - ADAM_B2 ** ADAM_STEP)
    delta = -ADAM_LR * (m_hat / (jnp.sqrt(v_hat) + ADAM_EPS) + ADAM_WD * w)
    return delta, m, v


def _sigmoid(t):
    return 1.0 / (1.0 + jnp.exp(-t))


def _slot(px, py, pc):
    return 4 * px + 2 * py + pc


OWN, NX, NY, DG = range(4)
HALF_IN = SHARD_IN // 2
N_GATHER_KINDS = 13
W_OUT_KINDS = N_GATHER_KINDS + 7


IN_PROJ_TILE = 640
TILE_ORDER = ((0, 1, 2, 3, 4, 5, 6, 7, 8, 9), (3, 4, 0, 1, 2, 8, 9, 5, 6, 7),
              (5, 6, 0, 1, 7, 8, 9, 2, 3, 4), (8, 9, 3, 4, 5, 6, 7, 0, 1, 2))
TILES_OWN, TILES_NEIGHBOURS = 2, 7


def _tile(table_ref, p):
    chip = 2 * lax.axis_index("x") + lax.axis_index("y")
    return table_ref[chip * len(TILE_ORDER[0]) + p]


DW_TILE_ORDER = tuple(tuple(reversed(row)) for row in TILE_ORDER)


def _tiles_until_complete(chip, owner):
    lo, hi = owner * 2 * SHARD_IN, (owner + 1) * 2 * SHARD_IN
    touching = [t for t in range(len(TILE_ORDER[0])) if t * IN_PROJ_TILE < hi and (t + 1) * IN_PROJ_TILE > lo]
    return 1 + max(DW_TILE_ORDER[chip].index(t) for t in touching)


DW_TABLE = tuple(DW_TILE_ORDER[chip] + tuple(_tiles_until_complete(chip, chip ^ flip) for flip in (0, 2, 1, 3))
                 for chip in range(4))


def _dw_entry(table_ref, p):
    chip = 2 * lax.axis_index("x") + lax.axis_index("y")
    return table_ref[chip * len(DW_TABLE[0]) + p]


def _gather_in_proj(x, norm_in, w_in_sh, w_out_sh, conv_sh, tiles):
    tn = IN_PROJ_TILE
    steps = D_PROJ // tn
    tm = 256

    def body(tiles_ref, x_hbm, g_ref, win_ref, wout_ref, cv_ref, wt_ref, h_ref, proj_ref, gout_ref, conv_ref,
             gin_ref, gcv_ref, wob_ref, x_ref, send_sems, recv_sems, local_sems):
        p = pl.program_id(0)
        local_sem = local_sems.at[0]
        x, y, c = lax.axis_index("x"), lax.axis_index("y"), lax.axis_index("c")
        me, sibling = (x, y, c), (x, y, 1 - c)
        nx, ny, dg = (1 - x, y, c), (x, 1 - y, c), (1 - x, 1 - y, c)

        def other(dev):
            return (dev[0], dev[1], 1 - dev[2])

        def shard(dev):
            return gin_ref.at[pl.ds(pl.multiple_of(_slot(*dev) * SHARD_IN, 16), SHARD_IN), :]

        def half(dev, h):
            return gin_ref.at[pl.ds(pl.multiple_of(_slot(*dev) * SHARD_IN + h * HALF_IN, 16), HALF_IN), :]

        def rc(ref, k, to):
            return pltpu.make_async_remote_copy(src_ref=ref, dst_ref=ref, send_sem=send_sems.at[k],
                                                recv_sem=recv_sems.at[k], device_id=to, device_id_type=MESH)

        def cv(k, dev, to):
            s = _slot(*dev)
            return pltpu.make_async_remote_copy(src_ref=gcv_ref.at[s], dst_ref=gcv_ref.at[s],
                                                send_sem=send_sems.at[N_GATHER_KINDS + k],
                                                recv_sem=recv_sems.at[N_GATHER_KINDS + k], device_id=to, device_id_type=MESH)

        def own_copies():
            return [rc(shard(me), 0, sibling),
                    rc(half(me, 0), 1, nx), rc(half(me, 1), 2, nx),
                    rc(half(me, 1), 4, ny), rc(half(me, 0), 3, ny),
                    cv(0, me, sibling)] + [cv(1 + j, me, peer) for j, peer in enumerate((nx, ny, dg))]

        def pass_on(dev, h, k_in, k_ici, k_d2d, half=half, base=0):
            rc(half(dev, h), base + k_in, me).wait_recv()
            if k_ici is not None:
                rc(half(dev, h), base + k_ici, ny if dev is nx else nx).start()
            rc(half(dev, h), base + k_d2d, sibling).start()

        def out_half(dev, h):
            return gout_ref.at[_slot(*dev), pl.ds(h * (SHARD_OUT // 2), SHARD_OUT // 2), :]

        def own_out_copies():
            src = lambda h: wob_ref.at[pl.ds(h * (SHARD_OUT // 2), SHARD_OUT // 2), :]

            def send(ref, dst, k, to):
                return pltpu.make_async_remote_copy(src_ref=ref, dst_ref=dst, send_sem=send_sems.at[W_OUT_KINDS + k],
                                                    recv_sem=recv_sems.at[W_OUT_KINDS + k], device_id=to, device_id_type=MESH)

            return [send(wob_ref, gout_ref.at[_slot(*me)], 0, sibling),
                    send(src(0), out_half(me, 0), 1, nx), send(src(1), out_half(me, 1), 2, nx),
                    send(src(1), out_half(me, 1), 4, ny), send(src(0), out_half(me, 0), 3, ny)]

        def own_out_local():
            return pltpu.make_async_copy(wob_ref, gout_ref.at[_slot(*me)], local_sems.at[1])

        @pl.when(p == 0)
        def _():
            gin_ref[pl.ds(pl.multiple_of(_slot(*me) * SHARD_IN, 16), SHARD_IN), :] = win_ref[...].astype(BF16)
            gcv_ref[_slot(*me)] = jnp.zeros((8, SHARD_CONV), F32)
            gcv_ref[_slot(*me), 0:3, :] = cv_ref[:, 0, :]
            for cp in own_copies():
                cp.start()
            wob_ref[...] = wout_ref[...].astype(BF16)
            x_load = pltpu.make_async_copy(x_hbm, x_ref, local_sems.at[2])
            x_load.start()
            x_load.wait()
            for t in range(SEQ // tm):
                xv = x_ref[tm * t:tm * (t + 1), :]
                r = lax.rsqrt(jnp.mean(xv * xv, axis=-1, keepdims=True) + RMS_EPS)
                h_ref[tm * t:tm * (t + 1), :] = (xv * r * g_ref[...]).astype(BF16)
            rc(shard(sibling), 0, me).wait_recv()

        @pl.when(p == TILES_OWN)
        def _():
            for args in ((nx, 0, 1, 5, 7), (ny, 1, 4, 6, 10), (nx, 1, 2, None, 8), (ny, 0, 3, None, 9)):
                pass_on(*args)
            for j, peer in enumerate((nx, ny, dg)):
                cv(1 + j, peer, me).wait_recv()
                cv(4 + j, peer, sibling).start()
            for (dev, h), k in (((nx, 0), 7), ((nx, 1), 8), ((ny, 0), 9), ((ny, 1), 10)):
                rc(half(other(dev), h), k, me).wait_recv()
            own_out_local().start()
            for cp in own_out_copies():
                cp.start()

        @pl.when(p == TILES_NEIGHBOURS - 1)
        def _():
            pass_on(dg, 0, 5, None, 11)
            pass_on(dg, 1, 6, None, 12)

        @pl.when(p == TILES_NEIGHBOURS)
        def _():
            for (dev, h), k in (((dg, 0), 11), ((dg, 1), 12)):
                rc(half(other(dev), h), k, me).wait_recv()
            pltpu.make_async_copy(gin_ref, wt_ref, local_sem).start()

        @pl.when(p == steps - 2)
        def _():
            for args in ((nx, 0, 1, 5, 7), (ny, 1, 4, 6, 10), (nx, 1, 2, None, 8), (ny, 0, 3, None, 9)):
                pass_on(*args, half=out_half, base=W_OUT_KINDS)

        w = gin_ref[pl.ds(pl.multiple_of(_tile(tiles_ref, p) * tn, tn), tn), :]
        proj_ref[...] = lax.dot_general(h_ref[...], w, _NT, preferred_element_type=F32)

        @pl.when(p == steps - 1)
        def _():
            cv(0, sibling, me).wait_recv()
            for j, peer in enumerate((nx, ny, dg)):
                cv(4 + j, other(peer), me).wait_recv()
            for d in range(N_DEV):
                conv_ref[:, d * SHARD_CONV:(d + 1) * SHARD_CONV] = gcv_ref[d]
            relayed = [rc(half(nx, 0), 5, ny), rc(half(ny, 1), 6, nx)]
            relayed += [rc(half(dev, h), k, sibling) for (dev, h), k in
                        (((nx, 0), 7), ((nx, 1), 8), ((ny, 0), 9), ((ny, 1), 10), ((dg, 0), 11), ((dg, 1), 12))]
            relayed += [cv(4 + j, peer, sibling) for j, peer in enumerate((nx, ny, dg))]
            for cp in own_copies() + relayed:
                cp.wait_send()
            pltpu.make_async_copy(gin_ref, wt_ref, local_sem).wait()
            pass_on(dg, 0, 5, None, 11, half=out_half, base=W_OUT_KINDS)
            pass_on(dg, 1, 6, None, 12, half=out_half, base=W_OUT_KINDS)
            rc(gout_ref.at[_slot(*sibling)], W_OUT_KINDS, me).wait_recv()
            out_relayed = [rc(out_half(nx, 0), W_OUT_KINDS + 5, ny), rc(out_half(ny, 1), W_OUT_KINDS + 6, nx)]
            for (dev, h), k in (((nx, 0), 7), ((nx, 1), 8), ((ny, 0), 9), ((ny, 1), 10), ((dg, 0), 11), ((dg, 1), 12)):
                rc(out_half(other(dev), h), W_OUT_KINDS + k, me).wait_recv()
                out_relayed.append(rc(out_half(dev, h), W_OUT_KINDS + k, sibling))
            for cp in own_out_copies() + out_relayed:
                cp.wait_send()
            own_out_local().wait()

    vmem = pl.BlockSpec(memory_space=pltpu.VMEM)
    grid_spec = pltpu.PrefetchScalarGridSpec(
        num_scalar_prefetch=1, grid=(steps,),
        in_specs=[pl.BlockSpec(memory_space=pl.ANY), vmem, vmem, vmem, vmem],
        out_specs=(pl.BlockSpec(memory_space=pl.ANY), vmem,
                   pl.BlockSpec((SEQ, tn), lambda p, tiles_ref: (0, _tile(tiles_ref, p))),
                   pl.BlockSpec(memory_space=pl.ANY), vmem),
        scratch_shapes=[pltpu.VMEM((D_PROJ, D_MODEL), BF16), pltpu.VMEM((N_DEV, 8, SHARD_CONV), F32),
                        pltpu.VMEM((SHARD_OUT, D_MODEL), BF16), pltpu.VMEM((SEQ, D_MODEL), F32),
                        pltpu.SemaphoreType.DMA((W_OUT_KINDS + N_GATHER_KINDS,)),
                        pltpu.SemaphoreType.DMA((W_OUT_KINDS + N_GATHER_KINDS,)),
                        pltpu.SemaphoreType.DMA((3,))])
    return pl.pallas_call(
        body, name="gather_in_proj", grid_spec=grid_spec,
        out_shape=(jax.ShapeDtypeStruct((D_PROJ, D_MODEL), BF16), jax.ShapeDtypeStruct((SEQ, D_MODEL), BF16),
                   jax.ShapeDtypeStruct((SEQ, D_PROJ), F32), jax.ShapeDtypeStruct((N_DEV, SHARD_OUT, D_MODEL), BF16),
                   jax.ShapeDtypeStruct((8, D_CONV), F32)),
        compiler_params=_params(dimension_semantics=("arbitrary",)),
    )(tiles, x, norm_in, w_in_sh, w_out_sh, conv_sh)


def _shard_sum(src, own, d2d, ici, send_sems, recv_sems, local_sems, base=0):
    x, y, c = lax.axis_index("x"), lax.axis_index("y"), lax.axis_index("c")
    sibling = (x, y, 1 - c)
    chips = [(x, y), (1 - x, y), (x, 1 - y), (1 - x, 1 - y)]

    def rcopy(s, d, k, to):
        return pltpu.make_async_remote_copy(src_ref=s, dst_ref=d, send_sem=send_sems.at[base + k],
                                            recv_sem=recv_sems.at[base + k], device_id=to, device_id_type=MESH)

    def mine(k):
        return pltpu.make_async_copy(src.at[_slot(*chips[k], c)], own.at[k], local_sems.at[k])

    def to_sibling(k):
        return rcopy(src.at[_slot(*chips[k], 1 - c)], d2d.at[k], k, sibling)

    def to_chip(k):
        return rcopy(own.at[k], ici.at[k - 1], 3 + k, (*chips[k], c))

    def start(k):
        mine(k).start()
        to_sibling(k).start()

    def forward(k):
        mine(k).wait()
        to_sibling(k).wait_recv()
        own[k] = (own[k].astype(F32) + d2d[k].astype(F32)).astype(BF16)
        to_chip(k).start()

    def finish():
        mine(0).wait()
        to_sibling(0).wait_recv()
        acc = own[0].astype(F32) + d2d[0].astype(F32)
        for k in range(1, 4):
            to_chip(k).wait_recv()
            acc = acc + ici[k - 1].astype(F32)
        for k in range(4):
            to_sibling(k).wait_send()
        for k in range(1, 4):
            to_chip(k).wait_send()
        return acc

    return start, forward, finish


def _shard_sum_scratch(rows):
    return [pltpu.VMEM((4, rows, D_MODEL), BF16), pltpu.VMEM((4, rows, D_MODEL), BF16),
            pltpu.VMEM((3, rows, D_MODEL), BF16)]


N_SHARD_SUM_SEMS = 7


N_CHIP_SUM_SEMS = 5


def _chip_sum(dwt, d2d, via, out_hbm, tiles_until, send_sems, recv_sems, local_sems, base, local_base):
    x, y, c = lax.axis_index("x"), lax.axis_index("y"), lax.axis_index("c")
    sibling, nx, ny = (x, y, 1 - c), (1 - x, y, c), (x, 1 - y, c)
    chips = [(x, y), (1 - x, y), (x, 1 - y), (1 - x, 1 - y)]

    def shard(s):
        return dwt.at[pl.ds(pl.multiple_of(s * SHARD_IN, 16), SHARD_IN), :]

    def half(ref, h):
        return ref.at[pl.ds(h * HALF_IN, HALF_IN), :]

    def rc(s, d, k, to):
        return pltpu.make_async_remote_copy(src_ref=s, dst_ref=d, send_sem=send_sems.at[base + k],
                                            recv_sem=recv_sems.at[base + k], device_id=to, device_id_type=MESH)

    def to_sibling(k):
        return rc(shard(_slot(*chips[k], 1 - c)), d2d.at[k - 1], k - 1, sibling)

    for_dg = (lambda: rc(half(d2d.at[DG - 1], 0), via.at[0], 3, nx), lambda: rc(half(d2d.at[DG - 1], 1), via.at[1], 4, ny))

    def save(k):
        return pltpu.make_async_copy(d2d.at[k - 1], out_hbm.at[k], local_sems.at[local_base + k])

    own_saves = (lambda: pltpu.make_async_copy(shard(_slot(x, y, c)), out_hbm.at[OWN], local_sems.at[local_base]),
                 lambda: pltpu.make_async_copy(shard(_slot(x, y, 1 - c)), out_hbm.at[3], local_sems.at[local_base + 3]))

    def before_tile(n):
        for k in (NX, NY, DG):
            @pl.when(tiles_until(k) == n)
            def _():
                to_sibling(k).start()

            @pl.when(tiles_until(k) + 1 == n)
            def _():
                to_sibling(k).wait_recv()
                d2d[k - 1] = (shard(_slot(*chips[k], c))[...].astype(F32) + d2d[k - 1].astype(F32)).astype(BF16)
                if k == DG:
                    for cp in for_dg:
                        cp().start()

    def after_tiles():
        for cp in own_saves:
            cp().start()

    def finish():
        for k, h in ((NY, 0), (NX, 1)):
            for_dg[h]().wait_recv()
            rows = pl.ds(h * HALF_IN, HALF_IN)
            d2d[k - 1, rows, :] = (d2d[k - 1, rows, :].astype(F32) + via[h].astype(F32)).astype(BF16)
            save(k).start()
        for cp in own_saves + (lambda: save(NX), lambda: save(NY)):
            cp().wait()
        for cp in (lambda: to_sibling(NX), lambda: to_sibling(NY), lambda: to_sibling(DG)) + for_dg:
            cp().wait_send()

    return before_tile, after_tiles, finish


N_ICI_SUM_SEMS = 3


def _ici_sum(src, own, d2d, ici, send_sems, recv_sems, local_sems, base=0):
    x, y, c = lax.axis_index("x"), lax.axis_index("y"), lax.axis_index("c")

    def rc(s, d, k, to):
        return pltpu.make_async_remote_copy(src_ref=s, dst_ref=d, send_sem=send_sems.at[base + k],
                                            recv_sem=recv_sems.at[base + k], device_id=to, device_id_type=MESH)

    copies = (lambda: rc(src.at[NX], ici.at[0], 0, (1 - x, y, c)), lambda: rc(src.at[NY], ici.at[1], 1, (x, 1 - y, c)),
              lambda: rc(src.at[3], d2d, 2, (x, y, 1 - c)))
    mine = lambda: pltpu.make_async_copy(src.at[OWN], own, local_sems.at[0])

    def start():
        for cp in copies + (mine,):
            cp().start()

    def finish():
        mine().wait()
        for cp in copies:
            cp().wait_recv()
        acc = own[...].astype(F32) + d2d[...].astype(F32) + ici[0].astype(F32) + ici[1].astype(F32)
        for cp in copies:
            cp().wait_send()
        return acc

    return start, finish


def _slab_sum(myslab, slabs, send_sems, recv_sems, base):
    x, y, c = lax.axis_index("x"), lax.axis_index("y"), lax.axis_index("c")
    me = _slot(x, y, c)
    peers = [(x, y, 1 - c), (1 - x, y, c), (x, 1 - y, c), (1 - x, 1 - y, c),
             (1 - x, y, 1 - c), (x, 1 - y, 1 - c), (1 - x, 1 - y, 1 - c)]

    def cp(k):
        return pltpu.make_async_remote_copy(src_ref=myslab, dst_ref=slabs.at[me], send_sem=send_sems.at[base + k],
                                            recv_sem=recv_sems.at[base + k], device_id=peers[k], device_id_type=MESH)

    def start():
        slabs[me] = myslab[...]
        for k in range(7):
            cp(k).start()

    def finish():
        for k in range(7):
            cp(k).wait_recv()
        total = slabs[0]
        for d in range(1, N_DEV):
            total = total + slabs[d]
        for k in range(7):
            cp(k).wait_send()
        return total

    return start, finish


def _chunk_rows(r):
    return slice(r * CHUNK, (r + 1) * CHUNK)


def _conv_halo(cch_ref, cuh_ref, n):
    zh = jnp.where(n > 0, cch_ref[...] * cuh_ref[...], 0.0)
    return jnp.concatenate([zh] * (CHUNK // HALO), axis=0)


def _conv_chunk(pj_ref, zhalo, cw, r):
    rows = _chunk_rows(r)
    cc = pj_ref[rows, OFF_CC:OFF_CC + D_CONV]
    cu = pj_ref[rows, OFF_CU:OFF_CU + D_CONV]
    z = cc * cu
    before = _chunk_rows(r - 1)
    zprev = pj_ref[before, OFF_CC:OFF_CC + D_CONV] * pj_ref[before, OFF_CU:OFF_CU + D_CONV] if r > 0 else zhalo
    row = lax.broadcasted_iota(jnp.int32, (CHUNK, D_CONV), 0)
    z1 = jnp.where(row < 1, pltpu.roll(zprev, 1, 0), pltpu.roll(z, 1, 0))
    z2 = jnp.where(row < 2, pltpu.roll(zprev, 2, 0), pltpu.roll(z, 2, 0))
    co = cw[0] * z2 + cw[1] * z1 + cw[2] * z
    return cc, cu, z, z1, z2, co


def _gated_norm(a, gain, t):
    r = lax.rsqrt(jnp.mean(a * a, axis=-1, keepdims=True) + RMS_EPS)
    return a * r * gain * (t * _sigmoid(t))


def _kv_bands(pj, kvp_ref):
    lane = lax.broadcasted_iota(jnp.int32, (2 * BLOCK, D_KV), 1)
    lo = lane < HEAD_DIM

    def bands(prev, cur):
        b = jnp.concatenate([prev, cur], axis=0)
        br = pltpu.roll(b, HEAD_DIM, 1)
        zero = jnp.zeros_like(b)
        return ((jnp.where(lo, b, zero).astype(BF16), jnp.where(lo, zero, br).astype(BF16)),
                (jnp.where(lo, br, zero).astype(BF16), jnp.where(lo, zero, b).astype(BF16)))

    ks = bands(kvp_ref[:, 0:D_KV], pj[:, OFF_K:OFF_K + D_KV])
    vs = bands(kvp_ref[:, D_KV:2 * D_KV], pj[:, OFF_V:OFF_V + D_KV])
    return ks, vs


STACK = PAIRS_PER_KV * BLOCK


def _head(j, i, e):
    return 2 * (PAIRS_PER_KV * j + i) + e


def _pair_cols(j, i, off):
    p = PAIRS_PER_KV * j + i
    return slice(off + 128 * p, off + 128 * (p + 1))


def _fill_attn_bias(bias_scr, first_block):
    qi = lax.broadcasted_iota(jnp.int32, (BLOCK, 2 * BLOCK), 0)
    kj = lax.broadcasted_iota(jnp.int32, (BLOCK, 2 * BLOCK), 1)
    dist = BLOCK + qi - kj
    valid = (dist >= 0) & (dist < BLOCK)
    if first_block:
        valid = valid & (kj >= BLOCK)
    distf = dist.astype(F32)
    for j in range(2):
        for e in range(2):
            for i in range(PAIRS_PER_KV):
                bias_scr[2 * j + e, BLOCK * i:BLOCK * (i + 1), :] = jnp.where(valid, -SLOPES[_head(j, i, e)] * distf, NEG)


def _q_stack(pj, j):
    return jnp.concatenate([(pj[:, _pair_cols(j, i, OFF_Q)] * SCALE).astype(BF16) for i in range(PAIRS_PER_KV)], axis=0)


def _attn_probs(q_stack, kband, bias_ref, sinks):
    s = lax.dot_general(q_stack, kband, _NT, preferred_element_type=F32)
    ones = jnp.ones((128, 128), BF16)
    probs, shares = [], []
    for i, sink in enumerate(sinks):
        rows = slice(BLOCK * i, BLOCK * (i + 1))
        t = s[rows, :] + bias_ref[rows, :]
        m = jnp.broadcast_to(jnp.max(t, axis=-1, keepdims=True), (BLOCK, 128))
        m = jnp.maximum(m, sink)
        p = [jnp.exp(t[:, :128] - m), jnp.exp(t[:, 128:] - m)]
        es = jnp.exp(sink - m)
        total = (jnp.dot(p[0].astype(BF16), ones, preferred_element_type=F32)
                 + jnp.dot(p[1].astype(BF16), ones, preferred_element_type=F32))
        inv = 1.0 / (total + es)
        probs.append(jnp.concatenate([p[0] * inv, p[1] * inv], axis=1))
        shares.append(es * inv)
    return jnp.concatenate(probs, axis=0), jnp.concatenate(shares, axis=0)


def _attn_group(pj, ks, vs, bias_scr, sink_ref, j):
    q_stack = _q_stack(pj, j)
    out, probs, shares = None, [], []
    for e in range(2):
        p, ps = _attn_probs(q_stack, ks[j][e], bias_scr.at[2 * j + e],
                            [sink_ref[_head(j, i, e)] for i in range(PAIRS_PER_KV)])
        p = p.astype(BF16)
        o = jnp.dot(p, vs[j][e], preferred_element_type=F32)
        out = o if out is None else out + o
        probs.append(p)
        shares.append(ps)
    return out, probs, shares


def _mix_fwd(proj, conv_full, sinks, norm_conv, norm_attn):
    def body(pj_ref, kvp_ref, cch_ref, cuh_ref, cw_ref, sink_ref, gc_ref, ga_ref,
             mixed_ref, attn_scr, p_ref, ps_ref, bias_scr):
        n = pl.program_id(0)
        pj = pj_ref

        @pl.when(n == 0)
        def _():
            _fill_attn_bias(bias_scr, first_block=True)

        @pl.when(n == 1)
        def _():
            _fill_attn_bias(bias_scr, first_block=False)

        zhalo = _conv_halo(cch_ref, cuh_ref, n)
        cw = (cw_ref[0:1, :], cw_ref[1:2, :], cw_ref[2:3, :])
        gain_c = gc_ref[...]

        for r in range(N_CHUNKS):
            rows = _chunk_rows(r)
            co = _conv_chunk(pj_ref, zhalo, cw, r)[-1]
            y = _gated_norm(pj_ref[rows, OFF_CB:OFF_CB + D_CONV] * co, gain_c, pj_ref[rows, OFF_GC:OFF_GC + D_CONV])
            mixed_ref[rows, 0:D_CONV] = y.astype(BF16)

        ks, vs = _kv_bands(pj, kvp_ref)
        for j in range(2):
            out, probs, shares = _attn_group(pj, ks, vs, bias_scr, sink_ref, j)
            for e in range(2):
                p_ref[0, 2 * j + e] = probs[e]
                ps_ref[0, 2 * j + e] = shares[e]
            for i in range(PAIRS_PER_KV):
                attn_scr[:, _pair_cols(j, i, 0)] = out[BLOCK * i:BLOCK * (i + 1), :]
        gain_a = ga_ref[...]

        for r in range(N_CHUNKS):
            rows = _chunk_rows(r)
            y = _gated_norm(attn_scr[rows, :], gain_a, pj_ref[rows, OFF_GA:OFF_GA + D_ATTN])
            mixed_ref[rows, D_CONV:D_MIX] = y.astype(BF16)

    per_block = BLOCK // HALO
    return pl.pallas_call(
        body, name="mix_fwd", grid=(N_BLOCKS,),
        in_specs=[
            pl.BlockSpec((BLOCK, D_PROJ), lambda n: (n, 0)),
            pl.BlockSpec((BLOCK, 2 * D_KV), lambda n: (jnp.maximum(n - 1, 0), OFF_K // (2 * D_KV))),
            pl.BlockSpec((HALO, D_CONV), lambda n: (jnp.maximum(n * per_block - 1, 0), OFF_CC // D_CONV)),
            pl.BlockSpec((HALO, D_CONV), lambda n: (jnp.maximum(n * per_block - 1, 0), OFF_CU // D_CONV)),
            pl.BlockSpec((8, D_CONV), lambda n: (0, 0)),
            pl.BlockSpec(memory_space=pltpu.SMEM),
            pl.BlockSpec((1, D_CONV), lambda n: (0, 0)),
            pl.BlockSpec((1, D_ATTN), lambda n: (0, 0)),
        ],
        out_specs=(pl.BlockSpec((BLOCK, D_MIX), lambda n: (n, 0)), pl.BlockSpec((BLOCK, D_ATTN), lambda n: (n, 0)),
                   pl.BlockSpec((1, 4, STACK, 2 * BLOCK), lambda n: (n, 0, 0, 0)),
                   pl.BlockSpec((1, 4, STACK, 128), lambda n: (n, 0, 0, 0))),
        out_shape=(jax.ShapeDtypeStruct((SEQ, D_MIX), BF16), jax.ShapeDtypeStruct((SEQ, D_ATTN), F32),
                   jax.ShapeDtypeStruct((N_BLOCKS, 4, STACK, 2 * BLOCK), BF16),
                   jax.ShapeDtypeStruct((N_BLOCKS, 4, STACK, 128), F32)),
        scratch_shapes=[pltpu.VMEM((4, STACK, 2 * BLOCK), F32)],
        compiler_params=_params(dimension_semantics=("arbitrary",)),
    )(proj, proj, proj, proj, conv_full, sinks, norm_conv, norm_attn)


def _out_proj_loss(mixed, x, target, w_out_full, norm_final):
    tm = 256

    def body(mx_ref, x_ref, t_ref, w_ref, g_ref, dx2_ref, dx2b_ref, dmix_ref, gnf_ref, loss_ref):
        i = pl.program_id(0)
        w = w_ref[...]
        x2 = x_ref[...] + jnp.dot(mx_ref[...], w, preferred_element_type=F32)
        r = lax.rsqrt(jnp.mean(x2 * x2, axis=-1, keepdims=True) + RMS_EPS)
        xn = x2 * r
        g = g_ref[...]
        err = xn * g - t_ref[...]
        part = 0.5 * jnp.sum(jnp.mean(err * err, axis=-1, keepdims=True), axis=0, keepdims=True)
        dy = err * (1.0 / D_MODEL)
        gnf = jnp.sum(dy * xn, axis=0, keepdims=True)
        u = dy * g
        dx2 = r * (u - xn * jnp.mean(u * xn, axis=-1, keepdims=True))
        dx2_ref[...] = dx2
        dx2b = dx2.astype(BF16)
        dx2b_ref[...] = dx2b
        dmix_ref[...] = lax.dot_general(dx2b, w, _NT, preferred_element_type=F32)

        @pl.when(i == 0)
        def _():
            gnf_ref[...] = jnp.zeros_like(gnf_ref)
            loss_ref[...] = jnp.zeros_like(loss_ref)

        gnf_ref[...] += gnf
        loss_ref[...] += jnp.broadcast_to(part, loss_ref.shape)

    return pl.pallas_call(
        body, name="out_proj_loss", grid=(SEQ // tm,),
        in_specs=[pl.BlockSpec((tm, D_MIX), lambda i: (i, 0)), pl.BlockSpec((tm, D_MODEL), lambda i: (i, 0)),
                  pl.BlockSpec((tm, D_MODEL), lambda i: (i, 0)), pl.BlockSpec(memory_space=pltpu.VMEM),
                  pl.BlockSpec((1, D_MODEL), lambda i: (0, 0))],
        out_specs=(pl.BlockSpec((tm, D_MODEL), lambda i: (i, 0)), pl.BlockSpec((tm, D_MODEL), lambda i: (i, 0)),
                   pl.BlockSpec((tm, D_MIX), lambda i: (i, 0)),
                   pl.BlockSpec((1, D_MODEL), lambda i: (0, 0)), pl.BlockSpec((8, 128), lambda i: (0, 0))),
        out_shape=(jax.ShapeDtypeStruct((SEQ, D_MODEL), F32), jax.ShapeDtypeStruct((SEQ, D_MODEL), BF16),
                   jax.ShapeDtypeStruct((SEQ, D_MIX), F32),
                   jax.ShapeDtypeStruct((1, D_MODEL), F32), jax.ShapeDtypeStruct((8, 128), F32)),
        compiler_params=_params(dimension_semantics=("arbitrary",)),
    )(mixed, x, target, w_out_full, norm_final)


def _gated_norm_bwd(a, gain, t, dy):
    r = lax.rsqrt(jnp.mean(a * a, axis=-1, keepdims=True) + RMS_EPS)
    an = a * r
    sg = _sigmoid(t)
    dn = dy * (t * sg)
    dt = dy * (an * gain) * (sg * (1.0 + t * (1.0 - sg)))
    u = dn * gain
    da = r * (u - an * jnp.mean(u * an, axis=-1, keepdims=True))
    return da, dt, dn * an


def _mix_bwd(proj, dmixed, attn, probs, shares, conv_full, norm_conv, norm_attn):
    def body(pj_ref, kvp_ref, cch_ref, cuh_ref, dmx_ref, attn_ref, p_ref, ps_ref, cw_ref, gc_ref, ga_ref,
             dpj_ref, gslab_ref, dattn_scr, nxt_scr, dkv_scr, acc_scr):
        step = pl.program_id(0)
        n = N_BLOCKS - 1 - step
        pj = pj_ref

        @pl.when(step == 0)
        def _():
            gslab_ref[...] = jnp.zeros_like(gslab_ref)
            nxt_scr[...] = jnp.zeros_like(nxt_scr)
            dkv_scr[...] = jnp.zeros_like(dkv_scr)
            acc_scr[...] = jnp.zeros_like(acc_scr)

        zhalo = _conv_halo(cch_ref, cuh_ref, n)
        cw = (cw_ref[0:1, :], cw_ref[1:2, :], cw_ref[2:3, :])
        gain_c = gc_ref[...]
        row = lax.broadcasted_iota(jnp.int32, (CHUNK, D_CONV), 0)

        dco_after = nxt_scr[...]
        for r in reversed(range(N_CHUNKS)):
            rows = _chunk_rows(r)
            cc, cu, z, z1, z2, co = _conv_chunk(pj_ref, zhalo, cw, r)
            cb = pj_ref[rows, OFF_CB:OFF_CB + D_CONV]
            da, dgate, gterm = _gated_norm_bwd(cb * co, gain_c, pj_ref[rows, OFF_GC:OFF_GC + D_CONV],
                                               dmx_ref[rows, 0:D_CONV])
            dpj_ref[rows, OFF_GC:OFF_GC + D_CONV] = dgate.astype(BF16)
            dpj_ref[rows, OFF_CB:OFF_CB + D_CONV] = (da * co).astype(BF16)
            dco = da * cb
            dco1 = jnp.where(row >= CHUNK - 1, pltpu.roll(dco_after, CHUNK - 1, 0), pltpu.roll(dco, CHUNK - 1, 0))
            dco2 = jnp.where(row >= CHUNK - 2, pltpu.roll(dco_after, CHUNK - 2, 0), pltpu.roll(dco, CHUNK - 2, 0))
            dz = cw[2] * dco + cw[1] * dco1 + cw[0] * dco2
            dpj_ref[rows, OFF_CC:OFF_CC + D_CONV] = (dz * cu).astype(BF16)
            dpj_ref[rows, OFF_CU:OFF_CU + D_CONV] = (dz * cc).astype(BF16)
            acc_scr[ACC_NORM_CONV] += gterm
            acc_scr[ACC_CONV0] += dco * z2
            acc_scr[ACC_CONV0 + 1] += dco * z1
            acc_scr[ACC_CONV0 + 2] += dco * z
            dco_after = dco
        nxt_scr[...] = dco_after

        ks, vs = _kv_bands(pj, kvp_ref)
        gain_a = ga_ref[...]

        for r in range(N_CHUNKS):
            rows = _chunk_rows(r)
            da, dgate, gterm = _gated_norm_bwd(attn_ref[rows, :], gain_a, pj_ref[rows, OFF_GA:OFF_GA + D_ATTN],
                                               dmx_ref[rows, D_CONV:D_MIX])
            dpj_ref[rows, OFF_GA:OFF_GA + D_ATTN] = dgate.astype(BF16)
            dattn_scr[rows, :] = da
            acc_scr[ACC_NORM_ATTN] += gterm

        in_lo = lax.broadcasted_iota(jnp.int32, (128, 128), 0) < HEAD_DIM
        half_ones = (jnp.where(in_lo, 1.0, 0.0).astype(BF16), jnp.where(in_lo, 0.0, 1.0).astype(BF16))
        lane_s = lax.broadcasted_iota(jnp.int32, (1, D_MODEL), 1)
        gsink = jnp.zeros((1, D_MODEL), F32)
        dk_t, dv_t = [], []
        for j in range(2):
            q_stack = _q_stack(pj, j)
            do_f = jnp.concatenate([dattn_scr[:, _pair_cols(j, i, 0)] for i in range(PAIRS_PER_KV)], axis=0)
            o_f = jnp.concatenate([attn_ref[:, _pair_cols(j, i, 0)] for i in range(PAIRS_PER_KV)], axis=0)
            prod = (do_f * o_f).astype(BF16)
            deltas = [jnp.dot(prod, half_ones[e], preferred_element_type=F32) for e in range(2)]
            do_b = do_f.astype(BF16)
            q_t, do_t = q_stack.T, do_b.T
            dq, dk_j, dv_j = None, None, None
            for e in range(2):
                p = p_ref[0, 2 * j + e]
                dp = lax.dot_general(do_b, vs[j][e], _NT, preferred_element_type=F32)
                ds = []
                for i in range(PAIRS_PER_KV):
                    rows = slice(BLOCK * i, BLOCK * (i + 1))
                    delta = deltas[e][rows, :]
                    ds.append((p[rows, :].astype(F32) * (dp[rows, :] - jnp.concatenate([delta, delta], axis=1))).astype(BF16))
                    gs_h = -jnp.sum(ps_ref[0, 2 * j + e, rows, 0:1] * delta[:, 0:1], axis=0, keepdims=True)
                    gsink = gsink + jnp.where(lane_s == _head(j, i, e), gs_h, 0.0)
                ds = jnp.concatenate(ds, axis=0)
                t = jnp.dot(ds, ks[j][e], preferred_element_type=F32)
                dq = t if dq is None else dq + t
                half = slice(HEAD_DIM * e, HEAD_DIM * (e + 1))
                a = jnp.dot(q_t[half, :], ds, preferred_element_type=F32)
                b = jnp.dot(do_t[half, :], p, preferred_element_type=F32)
                dk_j = a if dk_j is None else dk_j + a
                dv_j = b if dv_j is None else dv_j + b
            for i in range(PAIRS_PER_KV):
                dpj_ref[:, _pair_cols(j, i, OFF_Q)] = (dq[BLOCK * i:BLOCK * (i + 1), :] * SCALE).astype(BF16)
            dk_t.append(dk_j)
            dv_t.append(dv_j)
        dk = jnp.concatenate(dk_t, axis=0).T
        dv = jnp.concatenate(dv_t, axis=0).T
        dpj_ref[:, OFF_K:OFF_K + D_KV] = (dk[BLOCK:, :] + dkv_scr[:, 0:D_KV]).astype(BF16)
        dpj_ref[:, OFF_V:OFF_V + D_KV] = (dv[BLOCK:, :] + dkv_scr[:, D_KV:2 * D_KV]).astype(BF16)
        dkv_scr[:, 0:D_KV] = dk[:BLOCK, :]
        dkv_scr[:, D_KV:2 * D_KV] = dv[:BLOCK, :]
        gslab_ref[ROW_SINKS:ROW_SINKS + 1, :] += gsink

        @pl.when(step == N_BLOCKS - 1)
        def _():
            for k, slab_row in ((ACC_NORM_CONV, ROW_NORM_CONV), (ACC_NORM_ATTN, ROW_NORM_ATTN), (ACC_CONV0, ROW_CONV0),
                                (ACC_CONV0 + 1, ROW_CONV0 + 1), (ACC_CONV0 + 2, ROW_CONV0 + 2)):
                gslab_ref[slab_row:slab_row + 1, :] = jnp.sum(acc_scr[k], axis=0, keepdims=True)

    per_block = BLOCK // HALO
    last = N_BLOCKS - 1
    return pl.pallas_call(
        body, name="mix_bwd", grid=(N_BLOCKS,),
        in_specs=[
            pl.BlockSpec((BLOCK, D_PROJ), lambda s: (last - s, 0)),
            pl.BlockSpec((BLOCK, 2 * D_KV), lambda s: (jnp.maximum(last - s - 1, 0), OFF_K // (2 * D_KV))),
            pl.BlockSpec((HALO, D_CONV), lambda s: (jnp.maximum((last - s) * per_block - 1, 0), OFF_CC // D_CONV)),
            pl.BlockSpec((HALO, D_CONV), lambda s: (jnp.maximum((last - s) * per_block - 1, 0), OFF_CU // D_CONV)),
            pl.BlockSpec((BLOCK, D_MIX), lambda s: (last - s, 0)),
            pl.BlockSpec((BLOCK, D_ATTN), lambda s: (last - s, 0)),
            pl.BlockSpec((1, 4, STACK, 2 * BLOCK), lambda s: (last - s, 0, 0, 0)),
            pl.BlockSpec((1, 4, STACK, 128), lambda s: (last - s, 0, 0, 0)),
            pl.BlockSpec((8, D_CONV), lambda s: (0, 0)),
            pl.BlockSpec((1, D_CONV), lambda s: (0, 0)),
            pl.BlockSpec((1, D_ATTN), lambda s: (0, 0)),
        ],
        out_specs=(pl.BlockSpec((BLOCK, D_PROJ), lambda s: (last - s, 0)),
                   pl.BlockSpec((8, D_MODEL), lambda s: (0, 0))),
        out_shape=(jax.ShapeDtypeStruct((SEQ, D_PROJ), BF16), jax.ShapeDtypeStruct((8, D_MODEL), F32)),
        scratch_shapes=[pltpu.VMEM((BLOCK, D_ATTN), F32), pltpu.VMEM((CHUNK, D_CONV), F32),
                        pltpu.VMEM((BLOCK, 2 * D_KV), F32), pltpu.VMEM((N_ACC, CHUNK, D_MODEL), F32)],
        compiler_params=_params(dimension_semantics=("arbitrary",)),
    )(proj, proj, proj, proj, dmixed, attn, probs, shares, conv_full, norm_conv, norm_attn)


def _in_bwd_rs(dproj, w_full, x, dx2, norm_in, dw_in_chip, gslab, gnf, loss_part):
    tm = 256
    steps = SEQ // tm

    def body(dp_ref, w_hbm, x_ref, dx2_ref, g_ref, dwi_ref, gs_ref, gnf_ref, lp_ref, gx_ref, gwin_ref, gsum_ref,
             gni_scr, own, d2d, ici, myslab, slabs, w_ref, send_sems, recv_sems, local_sems):
        i = pl.program_id(0)
        rs_start, rs_finish = _ici_sum(dwi_ref, own, d2d, ici, send_sems, recv_sems, local_sems)
        slab_start, slab_finish = _slab_sum(myslab, slabs, send_sems, recv_sems, N_ICI_SUM_SEMS)

        @pl.when(i == 0)
        def _():
            gni_scr[...] = jnp.zeros_like(gni_scr)
            rs_start()
            w_load = pltpu.make_async_copy(w_hbm, w_ref, local_sems.at[1])
            w_load.start()
            w_load.wait()

        dh = jnp.dot(dp_ref[...], w_ref[...], preferred_element_type=F32)
        xv = x_ref[...]
        r = lax.rsqrt(jnp.mean(xv * xv, axis=-1, keepdims=True) + RMS_EPS)
        xn = xv * r
        u = dh * g_ref[...]
        gx_ref[...] = dx2_ref[...] + r * (u - xn * jnp.mean(u * xn, axis=-1, keepdims=True))
        gni_scr[...] += jnp.sum(dh * xn, axis=0, keepdims=True)

        @pl.when(i == steps - 1)
        def _():
            row = lax.broadcasted_iota(jnp.int32, (8, D_MODEL), 0)
            lane = lax.broadcasted_iota(jnp.int32, (8, D_MODEL), 1)
            slab = jnp.where(row == ROW_NORM_IN, gni_scr[...], jnp.where(row == ROW_NORM_FINAL, gnf_ref[...], gs_ref[...]))
            myslab[...] = jnp.where((row == ROW_SINKS) & (lane == LOSS_LANE), lp_ref[0:1, 0:1], slab)
            slab_start()
            gwin_ref[...] = rs_finish()
            gsum_ref[...] = slab_finish()

    const = lambda i: (0, 0)
    return pl.pallas_call(
        body, name="in_bwd", grid=(steps,),
        in_specs=[pl.BlockSpec((tm, D_PROJ), lambda i: (i, 0)), pl.BlockSpec(memory_space=pl.ANY),
                  pl.BlockSpec((tm, D_MODEL), lambda i: (i, 0)), pl.BlockSpec((tm, D_MODEL), lambda i: (i, 0)),
                  pl.BlockSpec((1, D_MODEL), const), pl.BlockSpec(memory_space=pl.ANY),
                  pl.BlockSpec((8, D_MODEL), const), pl.BlockSpec((1, D_MODEL), const), pl.BlockSpec((8, 128), const)],
        out_specs=(pl.BlockSpec((tm, D_MODEL), lambda i: (i, 0)), pl.BlockSpec((SHARD_IN, D_MODEL), const),
                   pl.BlockSpec((8, D_MODEL), const)),
        out_shape=(jax.ShapeDtypeStruct((SEQ, D_MODEL), F32), jax.ShapeDtypeStruct((SHARD_IN, D_MODEL), F32),
                   jax.ShapeDtypeStruct((8, D_MODEL), F32)),
        scratch_shapes=[pltpu.VMEM((1, D_MODEL), F32), pltpu.VMEM((SHARD_IN, D_MODEL), BF16),
                        pltpu.VMEM((SHARD_IN, D_MODEL), BF16), pltpu.VMEM((2, SHARD_IN, D_MODEL), BF16),
                        pltpu.VMEM((8, D_MODEL), F32), pltpu.VMEM((N_DEV, 8, D_MODEL), F32),
                        pltpu.VMEM((D_PROJ, D_MODEL), BF16),
                        pltpu.SemaphoreType.DMA((N_ICI_SUM_SEMS + 7,)), pltpu.SemaphoreType.DMA((N_ICI_SUM_SEMS + 7,)),
                        pltpu.SemaphoreType.DMA((2,))],
        compiler_params=_params(dimension_semantics=("arbitrary",)),
    )(dproj, w_full, x, dx2, norm_in, dw_in_chip, gslab, gnf, loss_part)


def _dw_rs(mixed, dx2b, dproj, h, table):
    tn_out, tn = 2 * SHARD_OUT, IN_PROJ_TILE
    out_steps, in_steps = D_MIX // tn_out, D_PROJ // tn
    steps = out_steps + in_steps
    out_order = (DG, NX, NY, OWN)

    def out_tile(i):
        chip = 2 * lax.axis_index("x") + lax.axis_index("y")
        return jnp.bitwise_xor(chip, (out_steps - 1) - jnp.minimum(i, out_steps - 1))

    def in_tile(table_ref, i):
        return _dw_entry(table_ref, jnp.maximum(i - out_steps, 0))

    def body(table_ref, mx_ref, dxb_ref, a_ref, b_ref, chip_ref, gwo_ref, dwo, dwt, d2d_in, via, own, d2d, ici,
             send_sems, recv_sems, local_sems):
        i = pl.program_id(0)
        rs_start, rs_forward, rs_finish = _shard_sum(dwo, own, d2d, ici, send_sems, recv_sems, local_sems)
        before_tile, after_tiles, chip_finish = _chip_sum(
            dwt, d2d_in, via, chip_ref, lambda k: _dw_entry(table_ref, in_steps + k), send_sems, recv_sems, local_sems,
            N_SHARD_SUM_SEMS, 4)

        for j, k in enumerate(out_order):
            @pl.when(i == j + 1)
            def _():
                rs_start(k)

            if k != OWN:
                @pl.when(i == j + 2)
                def _():
                    rs_forward(k)

        @pl.when(i < out_steps)
        def _():
            tile = lax.dot_general(mx_ref[...], dxb_ref[...], _TN, preferred_element_type=F32).astype(BF16)
            for core in range(2):
                dwo[2 * out_tile(i) + core] = tile[SHARD_OUT * core:SHARD_OUT * (core + 1), :]

        @pl.when(i >= out_steps)
        def _():
            before_tile(i - out_steps)
            tile = lax.dot_general(a_ref[...], b_ref[...], _TN, preferred_element_type=F32).astype(BF16)
            dwt[pl.ds(pl.multiple_of(in_tile(table_ref, i) * tn, tn), tn), :] = tile

        @pl.when(i == steps - 1)
        def _():
            after_tiles()
            gwo_ref[...] = rs_finish()
            chip_finish()

    vmem = pl.BlockSpec(memory_space=pltpu.VMEM)
    grid_spec = pltpu.PrefetchScalarGridSpec(
        num_scalar_prefetch=1, grid=(steps,),
        in_specs=[pl.BlockSpec((SEQ, tn_out), lambda i, table_ref: (0, out_tile(i))), vmem,
                  pl.BlockSpec((SEQ, tn), lambda i, table_ref: (0, in_tile(table_ref, i))), vmem],
        out_specs=(pl.BlockSpec(memory_space=pl.ANY), pl.BlockSpec((SHARD_OUT, D_MODEL), lambda i, table_ref: (0, 0))),
        scratch_shapes=[pltpu.VMEM((N_DEV, SHARD_OUT, D_MODEL), BF16),
                        pltpu.VMEM((D_PROJ, D_MODEL), BF16), pltpu.VMEM((3, SHARD_IN, D_MODEL), BF16),
                        pltpu.VMEM((2, HALF_IN, D_MODEL), BF16),
                        *_shard_sum_scratch(SHARD_OUT),
                        pltpu.SemaphoreType.DMA((N_SHARD_SUM_SEMS + N_CHIP_SUM_SEMS,)),
                        pltpu.SemaphoreType.DMA((N_SHARD_SUM_SEMS + N_CHIP_SUM_SEMS,)),
                        pltpu.SemaphoreType.DMA((8,))])
    return pl.pallas_call(
        body, name="dw", grid_spec=grid_spec,
        out_shape=(jax.ShapeDtypeStruct((4, SHARD_IN, D_MODEL), BF16), jax.ShapeDtypeStruct((SHARD_OUT, D_MODEL), F32)),
        compiler_params=_params(dimension_semantics=("arbitrary",)),
    )(table, mixed, dx2b, dproj, h)


def _adam_all(big_in, big_out, gsum, small, grad_x):
    n_chunks = 4
    n_big = 8

    def body(*refs):
        ins, outs = refs[:n_big + 1 + 18 + 1], refs[n_big + 1 + 18 + 1:n_big + 1 + 18 + 1 + 34]
        in_bufs, out_bufs = refs[-n_big - 6 - 3:-6 - 3], refs[-6 - 3:-3]
        in_sems, out_sems, gx_sem = refs[-3:]
        gx_copy = pltpu.make_async_copy(ins[27], outs[33], gx_sem)
        gx_copy.start()

        def rows(a, j):
            tr = ins[a].shape[0] // n_chunks
            return pl.ds(j * tr, tr)

        def load(a, j):
            return pltpu.make_async_copy(ins[a].at[rows(a, j), :], in_bufs[a].at[rows(a, j), :], in_sems.at[a * n_chunks + j])

        def store(a, j):
            b, kind = divmod(a, 4)
            src = in_bufs[4 * b + 1] if kind == 0 else out_bufs[3 * b + kind - 1]
            return pltpu.make_async_copy(src.at[rows(a, j), :], outs[a].at[rows(a, j), :], out_sems.at[a * n_chunks + j])

        for j in range(n_chunks):
            for a in range(n_big):
                load(a, j).start()

        def small_weights():
            gsum = ins[8][...]
            idx = _slot(lax.axis_index("x"), lax.axis_index("y"), lax.axis_index("c"))
            cg = jnp.zeros((3, SHARD_CONV), F32)
            for d in range(N_DEV):
                cg = jnp.where(idx == d, gsum[ROW_CONV0:ROW_CONV0 + 3, d * SHARD_CONV:(d + 1) * SHARD_CONV], cg)
            grads = (gsum[ROW_NORM_IN:ROW_NORM_IN + 1], gsum[ROW_SINKS:ROW_SINKS + 1, 0:N_Q_HEADS],
                     gsum[ROW_NORM_CONV:ROW_NORM_CONV + 1], gsum[ROW_NORM_ATTN:ROW_NORM_ATTN + 1],
                     gsum[ROW_NORM_FINAL:ROW_NORM_FINAL + 1], cg)
            for s, g in enumerate(grads):
                at = (slice(None), 0, slice(None)) if s == 5 else (slice(None), slice(None))
                w_ref, m_ref, v_ref = ins[9 + 3 * s:12 + 3 * s]
                delta, mn, vn = _adamw(w_ref[at], g, m_ref[at], v_ref[at])
                for ref, val in zip(outs[8 + 4 * s:12 + 4 * s], (g, delta, mn, vn)):
                    ref[at] = val
            outs[32][...] = gsum[ROW_SINKS:ROW_SINKS + 1, LOSS_LANE:LOSS_LANE + 1]

        small_weights()
        for j in range(n_chunks):
            for b in range(2):
                for a in range(4 * b, 4 * b + 4):
                    load(a, j).wait()
                w_buf, g_buf, m_buf, v_buf = in_bufs[4 * b:4 * b + 4]
                r = rows(4 * b, j)
                results = _adamw(w_buf[r, :], g_buf[r, :], m_buf[r, :], v_buf[r, :])
                for buf, val in zip(out_bufs[3 * b:3 * b + 3], results):
                    buf[r, :] = val
                for a in range(4 * b, 4 * b + 4):
                    store(a, j).start()
        for j in range(n_chunks):
            for a in range(n_big):
                store(a, j).wait()
        gx_copy.wait()

    vmem, hbm = pl.BlockSpec(memory_space=pltpu.VMEM), pl.BlockSpec(memory_space=pl.ANY)
    small_shapes = [a.shape for a in small[::3]]
    big_shapes = [(SHARD_IN, D_MODEL)] * 4 + [(SHARD_OUT, D_MODEL)] * 4
    out_shape = ([jax.ShapeDtypeStruct(s, F32) for s in big_shapes]
                 + [jax.ShapeDtypeStruct(s, F32) for s in small_shapes for _ in range(4)]
                 + [jax.ShapeDtypeStruct((1, 1), F32), jax.ShapeDtypeStruct((SEQ, D_MODEL), F32)])
    outs = pl.pallas_call(
        body, name="adam", in_specs=[hbm] * n_big + [vmem] * (1 + len(small)) + [hbm],
        out_specs=tuple([hbm] * n_big + [vmem] * (4 * len(small_shapes) + 1) + [hbm]), out_shape=tuple(out_shape),
        scratch_shapes=[pltpu.VMEM(s, F32) for s in big_shapes]
                       + [pltpu.VMEM(s, F32) for s in [(SHARD_IN, D_MODEL)] * 3 + [(SHARD_OUT, D_MODEL)] * 3]
                       + [pltpu.SemaphoreType.DMA((n_big * n_chunks,)), pltpu.SemaphoreType.DMA((n_big * n_chunks,)),
                          pltpu.SemaphoreType.DMA],
        compiler_params=_params(),
    )(*big_in, *big_out, gsum, *small, grad_x)
    return outs[0:4], outs[4:8], [outs[8 + 4 * s:12 + 4 * s] for s in range(6)], outs[32], outs[33]


def _rows_first(a):
    return jnp.transpose(a, (1, 0, 2))


def kernel(x, norm_in, w_in, conv_w, attn_sinks, norm_conv_out, norm_attn_out, w_out, norm_final, loss_target, m_norm_in, m_w_in, m_conv_w, m_attn_sinks, m_norm_conv_out, m_norm_attn_out, m_w_out, m_norm_final, v_norm_in, v_w_in, v_conv_w, v_attn_sinks, v_norm_conv_out, v_norm_attn_out, v_w_out, v_norm_final):
    x2d = x.reshape(SEQ, D_MODEL)
    target = loss_target.reshape(SEQ, D_MODEL)
    nf = norm_final.reshape(1, D_MODEL)

    w_in_t, m_w_in_t, v_w_in_t = w_in[0].T, m_w_in[0].T, v_w_in[0].T
    tiles = jnp.asarray(TILE_ORDER, jnp.int32).reshape(-1)
    w_in_full, h, proj, g_out, conv_full = _gather_in_proj(x2d, norm_in, w_in_t, w_out[0], _rows_first(conv_w), tiles)
    sinks = attn_sinks.reshape(N_Q_HEADS)

    mixed, attn, probs, shares = _mix_fwd(proj, conv_full, sinks, norm_conv_out, norm_attn_out)
    dx2, dx2b, dmixed, gnf, loss_part = _out_proj_loss(mixed, x2d, target, g_out.reshape(D_MIX, D_MODEL), nf)
    dproj, gslab = _mix_bwd(proj, dmixed, attn, probs, shares, conv_full, norm_conv_out, norm_attn_out)
    dw_in_chip, g_w_out = _dw_rs(mixed, dx2b, dproj, h, jnp.asarray(DW_TABLE, jnp.int32).reshape(-1))
    grad_x, g_w_in, gsum = _in_bwd_rs(dproj, w_in_full, x2d, dx2, norm_in, dw_in_chip, gslab, gnf, loss_part)

    small = (norm_in, m_norm_in, v_norm_in, attn_sinks, m_attn_sinks, v_attn_sinks,
             norm_conv_out, m_norm_conv_out, v_norm_conv_out, norm_attn_out, m_norm_attn_out, v_norm_attn_out,
             nf, m_norm_final.reshape(1, D_MODEL), v_norm_final.reshape(1, D_MODEL),
             _rows_first(conv_w), _rows_first(m_conv_w), _rows_first(v_conv_w))
    big_in, big_out, (s_ni, s_sk, s_nc, s_na, s_nf, s_cv), loss, grad_x = _adam_all(
        (w_in_t, g_w_in, m_w_in_t, v_w_in_t), (w_out[0], g_w_out, m_w_out[0], v_w_out[0]), gsum, small, grad_x)

    def leaves(k):
        return (s_ni[k], big_in[k].T[None], jnp.transpose(s_cv[k], (1, 0, 2)), s_sk[k], s_nc[k], s_na[k], big_out[k][None],
                s_nf[k].reshape(D_MODEL))

    return (loss.reshape(()), grad_x.reshape(1, SEQ, D_MODEL), *leaves(0), *leaves(1), *leaves(2), *leaves(3))
```

```python
import jax
import jax.numpy as jnp
from jax import lax
from jax.experimental import pallas as pl
from jax.experimental.pallas import tpu as pltpu

F32 = jnp.float32
BF16 = jnp.bfloat16
MESH = pl.DeviceIdType.MESH

N_DEV = 8
SEQ = 2048
D_MODEL = 1024
D_CONV = 1024
D_ATTN = 1024
D_KV = 128
HEAD_DIM = 64
N_Q_HEADS = 16
N_PAIRS = N_Q_HEADS // 2
PAIRS_PER_KV = N_PAIRS // 2
D_MIX = D_CONV + D_ATTN
D_PROJ = 6400
SHARD_IN = D_PROJ // N_DEV
SHARD_OUT = D_MIX // N_DEV
SHARD_CONV = D_CONV // N_DEV
OFF_CB, OFF_CC, OFF_CU, OFF_GC, OFF_Q, OFF_K, OFF_V, OFF_GA = 0, 1024, 2048, 3072, 4096, 5120, 5248, 5376
BLOCK = 128
N_BLOCKS = SEQ // BLOCK
HALO = 8
CHUNK = 16
N_CHUNKS = BLOCK // CHUNK
RMS_EPS = 1e-5
NEG = -1e30
SCALE = HEAD_DIM ** -0.5
SLOPES = tuple(2.0 ** (-8.0 * (h + 1) / N_Q_HEADS) for h in range(N_Q_HEADS))

ADAM_LR = 0.001
ADAM_B1 = 0.9
ADAM_B2 = 0.999
ADAM_EPS = 1e-08
ADAM_WD = 0.01
ADAM_STEP = 10

ROW_NORM_IN, ROW_NORM_CONV, ROW_NORM_ATTN, ROW_NORM_FINAL, ROW_CONV0, ROW_SINKS = 0, 1, 2, 3, 4, 7
LOSS_LANE = N_Q_HEADS
ACC_NORM_CONV, ACC_NORM_ATTN, ACC_CONV0, N_ACC = 0, 1, 2, 5

VMEM_LIMIT = 56 * 1024 * 1024

_NT = (((1,), (1,)), ((), ()))
_TN = (((0,), (0,)), ((), ()))


def _params(**kw):
    return pltpu.CompilerParams(vmem_limit_bytes=VMEM_LIMIT, **kw)


def _adamw(w, g, m, v):
    m = ADAM_B1 * m + (1.0 - ADAM_B1) * g
    v = ADAM_B2 * v + (1.0 - ADAM_B2) * (g * g)
    m_hat = m / (1.0 - ADAM_B1 ** ADAM_STEP)
    v_hat = v / (1.0 - ADAM_B2 ** ADAM_STEP)
    delta = -ADAM_LR * (m_hat / (jnp.sqrt(v_hat) + ADAM_EPS) + ADAM_WD * w)
    return delta, m, v


def _sigmoid(t):
    return 1.0 / (1.0 + jnp.exp(-t))


def _slot(px, py, pc):
    return 4 * px + 2 * py + pc


OWN, NX, NY, DG = range(4)
HALF_IN = SHARD_IN // 2
N_GATHER_KINDS = 13
W_OUT_KINDS = N_GATHER_KINDS + 7


IN_PROJ_TILE = 640
TILE_ORDER = ((0, 1, 2, 3, 4, 5, 6, 7, 8, 9), (3, 4, 0, 1, 2, 8, 9, 5, 6, 7),
              (5, 6, 0, 1, 7, 8, 9, 2, 3, 4), (8, 9, 3, 4, 5, 6, 7, 0, 1, 2))
TILES_OWN, TILES_NEIGHBOURS = 2, 7


def _tile(table_ref, p):
    chip = 2 * lax.axis_index("x") + lax.axis_index("y")
    return table_ref[chip * len(TILE_ORDER[0]) + p]


DW_TILE_ORDER = tuple(tuple(reversed(row)) for row in TILE_ORDER)


def _tiles_until_complete(chip, owner):
    lo, hi = owner * 2 * SHARD_IN, (owner + 1) * 2 * SHARD_IN
    touching = [t for t in range(len(TILE_ORDER[0])) if t * IN_PROJ_TILE < hi and (t + 1) * IN_PROJ_TILE > lo]
    return 1 + max(DW_TILE_ORDER[chip].index(t) for t in touching)


DW_TABLE = tuple(DW_TILE_ORDER[chip] + tuple(_tiles_until_complete(chip, chip ^ flip) for flip in (0, 2, 1, 3))
                 for chip in range(4))


def _dw_entry(table_ref, p):
    chip = 2 * lax.axis_index("x") + lax.axis_index("y")
    return table_ref[chip * len(DW_TABLE[0]) + p]


def _gather_in_proj(x, norm_in, w_in_sh, w_out_sh, conv_sh, tiles):
    tn = IN_PROJ_TILE
    steps = D_PROJ // tn
    tm = 256

    def body(tiles_ref, x_hbm, g_ref, win_ref, wout_ref, cv_ref, wt_ref, h_ref, proj_ref, gout_ref, conv_ref,
             gin_ref, gcv_ref, wob_ref, x_ref, send_sems, recv_sems, local_sems):
        p = pl.program_id(0)
        local_sem = local_sems.at[0]
        x, y, c = lax.axis_index("x"), lax.axis_index("y"), lax.axis_index("c")
        me, sibling = (x, y, c), (x, y, 1 - c)
        nx, ny, dg = (1 - x, y, c), (x, 1 - y, c), (1 - x, 1 - y, c)

        def other(dev):
            return (dev[0], dev[1], 1 - dev[2])

        def shard(dev):
            return gin_ref.at[pl.ds(pl.multiple_of(_slot(*dev) * SHARD_IN, 16), SHARD_IN), :]

        def half(dev, h):
            return gin_ref.at[pl.ds(pl.multiple_of(_slot(*dev) * SHARD_IN + h * HALF_IN, 16), HALF_IN), :]

        def rc(ref, k, to):
            return pltpu.make_async_remote_copy(src_ref=ref, dst_ref=ref, send_sem=send_sems.at[k],
                                                recv_sem=recv_sems.at[k], device_id=to, device_id_type=MESH)

        def cv(k, dev, to):
            s = _slot(*dev)
            return pltpu.make_async_remote_copy(src_ref=gcv_ref.at[s], dst_ref=gcv_ref.at[s],
                                                send_sem=send_sems.at[N_GATHER_KINDS + k],
                                                recv_sem=recv_sems.at[N_GATHER_KINDS + k], device_id=to, device_id_type=MESH)

        def own_copies():
            return [rc(shard(me), 0, sibling),
                    rc(half(me, 0), 1, nx), rc(half(me, 1), 2, nx),
                    rc(half(me, 1), 4, ny), rc(half(me, 0), 3, ny),
                    cv(0, me, sibling)] + [cv(1 + j, me, peer) for j, peer in enumerate((nx, ny, dg))]

        def pass_on(dev, h, k_in, k_ici, k_d2d, half=half, base=0):
            rc(half(dev, h), base + k_in, me).wait_recv()
            if k_ici is not None:
                rc(half(dev, h), base + k_ici, ny if dev is nx else nx).start()
            rc(half(dev, h), base + k_d2d, sibling).start()

        def out_half(dev, h):
            return gout_ref.at[_slot(*dev), pl.ds(h * (SHARD_OUT // 2), SHARD_OUT // 2), :]

        def own_out_copies():
            src = lambda h: wob_ref.at[pl.ds(h * (SHARD_OUT // 2), SHARD_OUT // 2), :]

            def send(ref, dst, k, to):
                return pltpu.make_async_remote_copy(src_ref=ref, dst_ref=dst, send_sem=send_sems.at[W_OUT_KINDS + k],
                                                    recv_sem=recv_sems.at[W_OUT_KINDS + k], device_id=to, device_id_type=MESH)

            return [send(wob_ref, gout_ref.at[_slot(*me)], 0, sibling),
                    send(src(0), out_half(me, 0), 1, nx), send(src(1), out_half(me, 1), 2, nx),
                    send(src(1), out_half(me, 1), 4, ny), send(src(0), out_half(me, 0), 3, ny)]

        def own_out_local():
            return pltpu.make_async_copy(wob_ref, gout_ref.at[_slot(*me)], local_sems.at[1])

        @pl.when(p == 0)
        def _():
            gin_ref[pl.ds(pl.multiple_of(_slot(*me) * SHARD_IN, 16), SHARD_IN), :] = win_ref[...].astype(BF16)
            gcv_ref[_slot(*me)] = jnp.zeros((8, SHARD_CONV), F32)
            gcv_ref[_slot(*me), 0:3, :] = cv_ref[:, 0, :]
            for cp in own_copies():
                cp.start()
            wob_ref[...] = wout_ref[...].astype(BF16)
            x_load = pltpu.make_async_copy(x_hbm, x_ref, local_sems.at[2])
            x_load.start()
            x_load.wait()
            for t in range(SEQ // tm):
                xv = x_ref[tm * t:tm * (t + 1), :]
                r = lax.rsqrt(jnp.mean(xv * xv, axis=-1, keepdims=True) + RMS_EPS)
                h_ref[tm * t:tm * (t + 1), :] = (xv * r * g_ref[...]).astype(BF16)
            rc(shard(sibling), 0, me).wait_recv()

        @pl.when(p == TILES_OWN)
        def _():
            for args in ((nx, 0, 1, 5, 7), (ny, 1, 4, 6, 10), (nx, 1, 2, None, 8), (ny, 0, 3, None, 9)):
                pass_on(*args)
            for j, peer in enumerate((nx, ny, dg)):
                cv(1 + j, peer, me).wait_recv()
                cv(4 + j, peer, sibling).start()
            for (dev, h), k in (((nx, 0), 7), ((nx, 1), 8), ((ny, 0), 9), ((ny, 1), 10)):
                rc(half(other(dev), h), k, me).wait_recv()
            own_out_local().start()
            for cp in own_out_copies():
                cp.start()

        @pl.when(p == TILES_NEIGHBOURS - 1)
        def _():
            pass_on(dg, 0, 5, None, 11)
            pass_on(dg, 1, 6, None, 12)

        @pl.when(p == TILES_NEIGHBOURS)
        def _():
            for (dev, h), k in (((dg, 0), 11), ((dg, 1), 12)):
                rc(half(other(dev), h), k, me).wait_recv()
            pltpu.make_async_copy(gin_ref, wt_ref, local_sem).start()

        @pl.when(p == steps - 2)
        def _():
            for args in ((nx, 0, 1, 5, 7), (ny, 1, 4, 6, 10), (nx, 1, 2, None, 8), (ny, 0, 3, None, 9)):
                pass_on(*args, half=out_half, base=W_OUT_KINDS)

        w = gin_ref[pl.ds(pl.multiple_of(_tile(tiles_ref, p) * tn, tn), tn), :]
        proj_ref[...] = lax.dot_general(h_ref[...], w, _NT, preferred_element_type=F32)

        @pl.when(p == steps - 1)
        def _():
            cv(0, sibling, me).wait_recv()
            for j, peer in enumerate((nx, ny, dg)):
                cv(4 + j, other(peer), me).wait_recv()
            for d in range(N_DEV):
                conv_ref[:, d * SHARD_CONV:(d + 1) * SHARD_CONV] = gcv_ref[d]
            relayed = [rc(half(nx, 0), 5, ny), rc(half(ny, 1), 6, nx)]
            relayed += [rc(half(dev, h), k, sibling) for (dev, h), k in
                        (((nx, 0), 7), ((nx, 1), 8), ((ny, 0), 9), ((ny, 1), 10), ((dg, 0), 11), ((dg, 1), 12))]
            relayed += [cv(4 + j, peer, sibling) for j, peer in enumerate((nx, ny, dg))]
            for cp in own_copies() + relayed:
                cp.wait_send()
            pltpu.make_async_copy(gin_ref, wt_ref, local_sem).wait()
            pass_on(dg, 0, 5, None, 11, half=out_half, base=W_OUT_KINDS)
            pass_on(dg, 1, 6, None, 12, half=out_half, base=W_OUT_KINDS)
            rc(gout_ref.at[_slot(*sibling)], W_OUT_KINDS, me).wait_recv()
            out_relayed = [rc(out_half(nx, 0), W_OUT_KINDS + 5, ny), rc(out_half(ny, 1), W_OUT_KINDS + 6, nx)]
            for (dev, h), k in (((nx, 0), 7), ((nx, 1), 8), ((ny, 0), 9), ((ny, 1), 10), ((dg, 0), 11), ((dg, 1), 12)):
                rc(out_half(other(dev), h), W_OUT_KINDS + k, me).wait_recv()
                out_relayed.append(rc(out_half(dev, h), W_OUT_KINDS + k, sibling))
            for cp in own_out_copies() + out_relayed:
                cp.wait_send()
            own_out_local().wait()

    vmem = pl.BlockSpec(memory_space=pltpu.VMEM)
    grid_spec = pltpu.PrefetchScalarGridSpec(
        num_scalar_prefetch=1, grid=(steps,),
        in_specs=[pl.BlockSpec(memory_space=pl.ANY), vmem, vmem, vmem, vmem],
        out_specs=(pl.BlockSpec(memory_space=pl.ANY), vmem,
                   pl.BlockSpec((SEQ, tn), lambda p, tiles_ref: (0, _tile(tiles_ref, p))),
                   pl.BlockSpec(memory_space=pl.ANY), vmem),
        scratch_shapes=[pltpu.VMEM((D_PROJ, D_MODEL), BF16), pltpu.VMEM((N_DEV, 8, SHARD_CONV), F32),
                        pltpu.VMEM((SHARD_OUT, D_MODEL), BF16), pltpu.VMEM((SEQ, D_MODEL), F32),
                        pltpu.SemaphoreType.DMA((W_OUT_KINDS + N_GATHER_KINDS,)),
                        pltpu.SemaphoreType.DMA((W_OUT_KINDS + N_GATHER_KINDS,)),
                        pltpu.SemaphoreType.DMA((3,))])
    return pl.pallas_call(
        body, name="gather_in_proj", grid_spec=grid_spec,
        out_shape=(jax.ShapeDtypeStruct((D_PROJ, D_MODEL), BF16), jax.ShapeDtypeStruct((SEQ, D_MODEL), BF16),
                   jax.ShapeDtypeStruct((SEQ, D_PROJ), F32), jax.ShapeDtypeStruct((N_DEV, SHARD_OUT, D_MODEL), BF16),
                   jax.ShapeDtypeStruct((8, D_CONV), F32)),
        compiler_params=_params(dimension_semantics=("arbitrary",)),
    )(tiles, x, norm_in, w_in_sh, w_out_sh, conv_sh)


def _shard_sum(src, own, d2d, ici, send_sems, recv_sems, local_sems, base=0):
    x, y, c = lax.axis_index("x"), lax.axis_index("y"), lax.axis_index("c")
    sibling = (x, y, 1 - c)
    chips = [(x, y), (1 - x, y), (x, 1 - y), (1 - x, 1 - y)]

    def rcopy(s, d, k, to):
        return pltpu.make_async_remote_copy(src_ref=s, dst_ref=d, send_sem=send_sems.at[base + k],
                                            recv_sem=recv_sems.at[base + k], device_id=to, device_id_type=MESH)

    def mine(k):
        return pltpu.make_async_copy(src.at[_slot(*chips[k], c)], own.at[k], local_sems.at[k])

    def to_sibling(k):
        return rcopy(src.at[_slot(*chips[k], 1 - c)], d2d.at[k], k, sibling)

    def to_chip(k):
        return rcopy(own.at[k], ici.at[k - 1], 3 + k, (*chips[k], c))

    def start(k):
        mine(k).start()
        to_sibling(k).start()

    def forward(k):
        mine(k).wait()
        to_sibling(k).wait_recv()
        own[k] = (own[k].astype(F32) + d2d[k].astype(F32)).astype(BF16)
        to_chip(k).start()

    def finish():
        mine(0).wait()
        to_sibling(0).wait_recv()
        acc = own[0].astype(F32) + d2d[0].astype(F32)
        for k in range(1, 4):
            to_chip(k).wait_recv()
            acc = acc + ici[k - 1].astype(F32)
        for k in range(4):
            to_sibling(k).wait_send()
        for k in range(1, 4):
            to_chip(k).wait_send()
        return acc

    return start, forward, finish


def _shard_sum_scratch(rows):
    return [pltpu.VMEM((4, rows, D_MODEL), BF16), pltpu.VMEM((4, rows, D_MODEL), BF16),
            pltpu.VMEM((3, rows, D_MODEL), BF16)]


N_SHARD_SUM_SEMS = 7


N_CHIP_SUM_SEMS = 5


def _chip_sum(dwt, d2d, via, out_hbm, tiles_until, send_sems, recv_sems, local_sems, base, local_base):
    x, y, c = lax.axis_index("x"), lax.axis_index("y"), lax.axis_index("c")
    sibling, nx, ny = (x, y, 1 - c), (1 - x, y, c), (x, 1 - y, c)
    chips = [(x, y), (1 - x, y), (x, 1 - y), (1 - x, 1 - y)]

    def shard(s):
        return dwt.at[pl.ds(pl.multiple_of(s * SHARD_IN, 16), SHARD_IN), :]

    def half(ref, h):
        return ref.at[pl.ds(h * HALF_IN, HALF_IN), :]

    def rc(s, d, k, to):
        return pltpu.make_async_remote_copy(src_ref=s, dst_ref=d, send_sem=send_sems.at[base + k],
                                            recv_sem=recv_sems.at[base + k], device_id=to, device_id_type=MESH)

    def to_sibling(k):
        return rc(shard(_slot(*chips[k], 1 - c)), d2d.at[k - 1], k - 1, sibling)

    for_dg = (lambda: rc(half(d2d.at[DG - 1], 0), via.at[0], 3, nx), lambda: rc(half(d2d.at[DG - 1], 1), via.at[1], 4, ny))

    def save(k):
        return pltpu.make_async_copy(d2d.at[k - 1], out_hbm.at[k], local_sems.at[local_base + k])

    own_saves = (lambda: pltpu.make_async_copy(shard(_slot(x, y, c)), out_hbm.at[OWN], local_sems.at[local_base]),
                 lambda: pltpu.make_async_copy(shard(_slot(x, y, 1 - c)), out_hbm.at[3], local_sems.at[local_base + 3]))

    def before_tile(n):
        for k in (NX, NY, DG):
            @pl.when(tiles_until(k) == n)
            def _():
                to_sibling(k).start()

            @pl.when(tiles_until(k) + 1 == n)
            def _():
                to_sibling(k).wait_recv()
                d2d[k - 1] = (shard(_slot(*chips[k], c))[...].astype(F32) + d2d[k - 1].astype(F32)).astype(BF16)
                if k == DG:
                    for cp in for_dg:
                        cp().start()

    def after_tiles():
        for cp in own_saves:
            cp().start()

    def finish():
        for k, h in ((NY, 0), (NX, 1)):
            for_dg[h]().wait_recv()
            rows = pl.ds(h * HALF_IN, HALF_IN)
            d2d[k - 1, rows, :] = (d2d[k - 1, rows, :].astype(F32) + via[h].astype(F32)).astype(BF16)
            save(k).start()
        for cp in own_saves + (lambda: save(NX), lambda: save(NY)):
            cp().wait()
        for cp in (lambda: to_sibling(NX), lambda: to_sibling(NY), lambda: to_sibling(DG)) + for_dg:
            cp().wait_send()

    return before_tile, after_tiles, finish


N_ICI_SUM_SEMS = 3


def _ici_sum(src, own, d2d, ici, send_sems, recv_sems, local_sems, base=0):
    x, y, c = lax.axis_index("x"), lax.axis_index("y"), lax.axis_index("c")

    def rc(s, d, k, to):
        return pltpu.make_async_remote_copy(src_ref=s, dst_ref=d, send_sem=send_sems.at[base + k],
                                            recv_sem=recv_sems.at[base + k], device_id=to, device_id_type=MESH)

    copies = (lambda: rc(src.at[NX], ici.at[0], 0, (1 - x, y, c)), lambda: rc(src.at[NY], ici.at[1], 1, (x, 1 - y, c)),
              lambda: rc(src.at[3], d2d, 2, (x, y, 1 - c)))
    mine = lambda: pltpu.make_async_copy(src.at[OWN], own, local_sems.at[0])

    def start():
        for cp in copies + (mine,):
            cp().start()

    def finish():
        mine().wait()
        for cp in copies:
            cp().wait_recv()
        acc = own[...].astype(F32) + d2d[...].astype(F32) + ici[0].astype(F32) + ici[1].astype(F32)
        for cp in copies:
            cp().wait_send()
        return acc

    return start, finish


def _slab_sum(myslab, slabs, send_sems, recv_sems, base):
    x, y, c = lax.axis_index("x"), lax.axis_index("y"), lax.axis_index("c")
    me = _slot(x, y, c)
    peers = [(x, y, 1 - c), (1 - x, y, c), (x, 1 - y, c), (1 - x, 1 - y, c),
             (1 - x, y, 1 - c), (x, 1 - y, 1 - c), (1 - x, 1 - y, 1 - c)]

    def cp(k):
        return pltpu.make_async_remote_copy(src_ref=myslab, dst_ref=slabs.at[me], send_sem=send_sems.at[base + k],
                                            recv_sem=recv_sems.at[base + k], device_id=peers[k], device_id_type=MESH)

    def start():
        slabs[me] = myslab[...]
        for k in range(7):
            cp(k).start()

    def finish():
        for k in range(7):
            cp(k).wait_recv()
        total = slabs[0]
        for d in range(1, N_DEV):
            total = total + slabs[d]
        for k in range(7):
            cp(k).wait_send()
        return total

    return start, finish


def _chunk_rows(r):
    return slice(r * CHUNK, (r + 1) * CHUNK)


def _conv_halo(cch_ref, cuh_ref, n):
    zh = jnp.where(n > 0, cch_ref[...] * cuh_ref[...], 0.0)
    return jnp.concatenate([zh] * (CHUNK // HALO), axis=0)


def _conv_chunk(pj_ref, zhalo, cw, r):
    rows = _chunk_rows(r)
    cc = pj_ref[rows, OFF_CC:OFF_CC + D_CONV]
    cu = pj_ref[rows, OFF_CU:OFF_CU + D_CONV]
    z = cc * cu
    before = _chunk_rows(r - 1)
    zprev = pj_ref[before, OFF_CC:OFF_CC + D_CONV] * pj_ref[before, OFF_CU:OFF_CU + D_CONV] if r > 0 else zhalo
    row = lax.broadcasted_iota(jnp.int32, (CHUNK, D_CONV), 0)
    z1 = jnp.where(row < 1, pltpu.roll(zprev, 1, 0), pltpu.roll(z, 1, 0))
    z2 = jnp.where(row < 2, pltpu.roll(zprev, 2, 0), pltpu.roll(z, 2, 0))
    co = cw[0] * z2 + cw[1] * z1 + cw[2] * z
    return cc, cu, z, z1, z2, co


def _gated_norm(a, gain, t):
    r = lax.rsqrt(jnp.mean(a * a, axis=-1, keepdims=True) + RMS_EPS)
    return a * r * gain * (t * _sigmoid(t))


def _kv_bands(pj, kvp_ref):
    lane = lax.broadcasted_iota(jnp.int32, (2 * BLOCK, D_KV), 1)
    lo = lane < HEAD_DIM

    def bands(prev, cur):
        b = jnp.concatenate([prev, cur], axis=0)
        br = pltpu.roll(b, HEAD_DIM, 1)
        zero = jnp.zeros_like(b)
        return ((jnp.where(lo, b, zero).astype(BF16), jnp.where(lo, zero, br).astype(BF16)),
                (jnp.where(lo, br, zero).astype(BF16), jnp.where(lo, zero, b).astype(BF16)))

    ks = bands(kvp_ref[:, 0:D_KV], pj[:, OFF_K:OFF_K + D_KV])
    vs = bands(kvp_ref[:, D_KV:2 * D_KV], pj[:, OFF_V:OFF_V + D_KV])
    return ks, vs


STACK = PAIRS_PER_KV * BLOCK


def _head(j, i, e):
    return 2 * (PAIRS_PER_KV * j + i) + e


def _pair_cols(j, i, off):
    p = PAIRS_PER_KV * j + i
    return slice(off + 128 * p, off + 128 * (p + 1))


def _fill_attn_bias(bias_scr, first_block):
    qi = lax.broadcasted_iota(jnp.int32, (BLOCK, 2 * BLOCK), 0)
    kj = lax.broadcasted_iota(jnp.int32, (BLOCK, 2 * BLOCK), 1)
    dist = BLOCK + qi - kj
    valid = (dist >= 0) & (dist < BLOCK)
    if first_block:
        valid = valid & (kj >= BLOCK)
    distf = dist.astype(F32)
    for j in range(2):
        for e in range(2):
            for i in range(PAIRS_PER_KV):
                bias_scr[2 * j + e, BLOCK * i:BLOCK * (i + 1), :] = jnp.where(valid, -SLOPES[_head(j, i, e)] * distf, NEG)


def _q_stack(pj, j):
    return jnp.concatenate([(pj[:, _pair_cols(j, i, OFF_Q)] * SCALE).astype(BF16) for i in range(PAIRS_PER_KV)], axis=0)


def _attn_probs(q_stack, kband, bias_ref, sinks):
    s = lax.dot_general(q_stack, kband, _NT, preferred_element_type=F32)
    ones = jnp.ones((128, 128), BF16)
    probs, shares = [], []
    for i, sink in enumerate(sinks):
        rows = slice(BLOCK * i, BLOCK * (i + 1))
        t = s[rows, :] + bias_ref[rows, :]
        m = jnp.broadcast_to(jnp.max(t, axis=-1, keepdims=True), (BLOCK, 128))
        m = jnp.maximum(m, sink)
        p = [jnp.exp(t[:, :128] - m), jnp.exp(t[:, 128:] - m)]
        es = jnp.exp(sink - m)
        total = (jnp.dot(p[0].astype(BF16), ones, preferred_element_type=F32)
                 + jnp.dot(p[1].astype(BF16), ones, preferred_element_type=F32))
        inv = 1.0 / (total + es)
        probs.append(jnp.concatenate([p[0] * inv, p[1] * inv], axis=1))
        shares.append(es * inv)
    return jnp.concatenate(probs, axis=0), jnp.concatenate(shares, axis=0)


def _attn_group(pj, ks, vs, bias_scr, sink_ref, j):
    q_stack = _q_stack(pj, j)
    out, probs, shares = None, [], []
    for e in range(2):
        p, ps = _attn_probs(q_stack, ks[j][e], bias_scr.at[2 * j + e],
                            [sink_ref[_head(j, i, e)] for i in range(PAIRS_PER_KV)])
        p = p.astype(BF16)
        o = jnp.dot(p, vs[j][e], preferred_element_type=F32)
        out = o if out is None else out + o
        probs.append(p)
        shares.append(ps)
    return out, probs, shares


def _mix_fwd(proj, conv_full, sinks, norm_conv, norm_attn):
    def body(pj_ref, kvp_ref, cch_ref, cuh_ref, cw_ref, sink_ref, gc_ref, ga_ref,
             mixed_ref, attn_scr, p_ref, ps_ref, bias_scr):
        n = pl.program_id(0)
        pj = pj_ref

        @pl.when(n == 0)
        def _():
            _fill_attn_bias(bias_scr, first_block=True)

        @pl.when(n == 1)
        def _():
            _fill_attn_bias(bias_scr, first_block=False)

        zhalo = _conv_halo(cch_ref, cuh_ref, n)
        cw = (cw_ref[0:1, :], cw_ref[1:2, :], cw_ref[2:3, :])
        gain_c = gc_ref[...]

        for r in range(N_CHUNKS):
            rows = _chunk_rows(r)
            co = _conv_chunk(pj_ref, zhalo, cw, r)[-1]
            y = _gated_norm(pj_ref[rows, OFF_CB:OFF_CB + D_CONV] * co, gain_c, pj_ref[rows, OFF_GC:OFF_GC + D_CONV])
            mixed_ref[rows, 0:D_CONV] = y.astype(BF16)

        ks, vs = _kv_bands(pj, kvp_ref)
        for j in range(2):
            out, probs, shares = _attn_group(pj, ks, vs, bias_scr, sink_ref, j)
            for e in range(2):
                p_ref[0, 2 * j + e] = probs[e]
                ps_ref[0, 2 * j + e] = shares[e]
            for i in range(PAIRS_PER_KV):
                attn_scr[:, _pair_cols(j, i, 0)] = out[BLOCK * i:BLOCK * (i + 1), :]
        gain_a = ga_ref[...]

        for r in range(N_CHUNKS):
            rows = _chunk_rows(r)
            y = _gated_norm(attn_scr[rows, :], gain_a, pj_ref[rows, OFF_GA:OFF_GA + D_ATTN])
            mixed_ref[rows, D_CONV:D_MIX] = y.astype(BF16)

    per_block = BLOCK // HALO
    return pl.pallas_call(
        body, name="mix_fwd", grid=(N_BLOCKS,),
        in_specs=[
            pl.BlockSpec((BLOCK, D_PROJ), lambda n: (n, 0)),
            pl.BlockSpec((BLOCK, 2 * D_KV), lambda n: (jnp.maximum(n - 1, 0), OFF_K // (2 * D_KV))),
            pl.BlockSpec((HALO, D_CONV), lambda n: (jnp.maximum(n * per_block - 1, 0), OFF_CC // D_CONV)),
            pl.BlockSpec((HALO, D_CONV), lambda n: (jnp.maximum(n * per_block - 1, 0), OFF_CU // D_CONV)),
            pl.BlockSpec((8, D_CONV), lambda n: (0, 0)),
            pl.BlockSpec(memory_space=pltpu.SMEM),
            pl.BlockSpec((1, D_CONV), lambda n: (0, 0)),
            pl.BlockSpec((1, D_ATTN), lambda n: (0, 0)),
        ],
        out_specs=(pl.BlockSpec((BLOCK, D_MIX), lambda n: (n, 0)), pl.BlockSpec((BLOCK, D_ATTN), lambda n: (n, 0)),
                   pl.BlockSpec((1, 4, STACK, 2 * BLOCK), lambda n: (n, 0, 0, 0)),
                   pl.BlockSpec((1, 4, STACK, 128), lambda n: (n, 0, 0, 0))),
        out_shape=(jax.ShapeDtypeStruct((SEQ, D_MIX), BF16), jax.ShapeDtypeStruct((SEQ, D_ATTN), F32),
                   jax.ShapeDtypeStruct((N_BLOCKS, 4, STACK, 2 * BLOCK), BF16),
                   jax.ShapeDtypeStruct((N_BLOCKS, 4, STACK, 128), F32)),
        scratch_shapes=[pltpu.VMEM((4, STACK, 2 * BLOCK), F32)],
        compiler_params=_params(dimension_semantics=("arbitrary",)),
    )(proj, proj, proj, proj, conv_full, sinks, norm_conv, norm_attn)


def _out_proj_loss(mixed, x, target, w_out_full, norm_final):
    tm = 256

    def body(mx_ref, x_ref, t_ref, w_ref, g_ref, dx2_ref, dx2b_ref, dmix_ref, gnf_ref, loss_ref):
        i = pl.program_id(0)
        w = w_ref[...]
        x2 = x_ref[...] + jnp.dot(mx_ref[...], w, preferred_element_type=F32)
        r = lax.rsqrt(jnp.mean(x2 * x2, axis=-1, keepdims=True) + RMS_EPS)
        xn = x2 * r
        g = g_ref[...]
        err = xn * g - t_ref[...]
        part = 0.5 * jnp.sum(jnp.mean(err * err, axis=-1, keepdims=True), axis=0, keepdims=True)
        dy = err * (1.0 / D_MODEL)
        gnf = jnp.sum(dy * xn, axis=0, keepdims=True)
        u = dy * g
        dx2 = r * (u - xn * jnp.mean(u * xn, axis=-1, keepdims=True))
        dx2_ref[...] = dx2
        dx2b = dx2.astype(BF16)
        dx2b_ref[...] = dx2b
        dmix_ref[...] = lax.dot_general(dx2b, w, _NT, preferred_element_type=F32)

        @pl.when(i == 0)
        def _():
            gnf_ref[...] = jnp.zeros_like(gnf_ref)
            loss_ref[...] = jnp.zeros_like(loss_ref)

        gnf_ref[...] += gnf
        loss_ref[...] += jnp.broadcast_to(part, loss_ref.shape)

    return pl.pallas_call(
        body, name="out_proj_loss", grid=(SEQ // tm,),
        in_specs=[pl.BlockSpec((tm, D_MIX), lambda i: (i, 0)), pl.BlockSpec((tm, D_MODEL), lambda i: (i, 0)),
                  pl.BlockSpec((tm, D_MODEL), lambda i: (i, 0)), pl.BlockSpec(memory_space=pltpu.VMEM),
                  pl.BlockSpec((1, D_MODEL), lambda i: (0, 0))],
        out_specs=(pl.BlockSpec((tm, D_MODEL), lambda i: (i, 0)), pl.BlockSpec((tm, D_MODEL), lambda i: (i, 0)),
                   pl.BlockSpec((tm, D_MIX), lambda i: (i, 0)),
                   pl.BlockSpec((1, D_MODEL), lambda i: (0, 0)), pl.BlockSpec((8, 128), lambda i: (0, 0))),
        out_shape=(jax.ShapeDtypeStruct((SEQ, D_MODEL), F32), jax.ShapeDtypeStruct((SEQ, D_MODEL), BF16),
                   jax.ShapeDtypeStruct((SEQ, D_MIX), F32),
                   jax.ShapeDtypeStruct((1, D_MODEL), F32), jax.ShapeDtypeStruct((8, 128), F32)),
        compiler_params=_params(dimension_semantics=("arbitrary",)),
    )(mixed, x, target, w_out_full, norm_final)


def _gated_norm_bwd(a, gain, t, dy):
    r = lax.rsqrt(jnp.mean(a * a, axis=-1, keepdims=True) + RMS_EPS)
    an = a * r
    sg = _sigmoid(t)
    dn = dy * (t * sg)
    dt = dy * (an * gain) * (sg * (1.0 + t * (1.0 - sg)))
    u = dn * gain
    da = r * (u - an * jnp.mean(u * an, axis=-1, keepdims=True))
    return da, dt, dn * an


def _mix_bwd(proj, dmixed, attn, probs, shares, conv_full, norm_conv, norm_attn):
    def body(pj_ref, kvp_ref, cch_ref, cuh_ref, dmx_ref, attn_ref, p_ref, ps_ref, cw_ref, gc_ref, ga_ref,
             dpj_ref, gslab_ref, dattn_scr, nxt_scr, dkv_scr, acc_scr):
        step = pl.program_id(0)
        n = N_BLOCKS - 1 - step
        pj = pj_ref

        @pl.when(step == 0)
        def _():
            gslab_ref[...] = jnp.zeros_like(gslab_ref)
            nxt_scr[...] = jnp.zeros_like(nxt_scr)
            dkv_scr[...] = jnp.zeros_like(dkv_scr)
            acc_scr[...] = jnp.zeros_like(acc_scr)

        zhalo = _conv_halo(cch_ref, cuh_ref, n)
        cw = (cw_ref[0:1, :], cw_ref[1:2, :], cw_ref[2:3, :])
        gain_c = gc_ref[...]
        row = lax.broadcasted_iota(jnp.int32, (CHUNK, D_CONV), 0)

        dco_after = nxt_scr[...]
        for r in reversed(range(N_CHUNKS)):
            rows = _chunk_rows(r)
            cc, cu, z, z1, z2, co = _conv_chunk(pj_ref, zhalo, cw, r)
            cb = pj_ref[rows, OFF_CB:OFF_CB + D_CONV]
            da, dgate, gterm = _gated_norm_bwd(cb * co, gain_c, pj_ref[rows, OFF_GC:OFF_GC + D_CONV],
                                               dmx_ref[rows, 0:D_CONV])
            dpj_ref[rows, OFF_GC:OFF_GC + D_CONV] = dgate.astype(BF16)
            dpj_ref[rows, OFF_CB:OFF_CB + D_CONV] = (da * co).astype(BF16)
            dco = da * cb
            dco1 = jnp.where(row >= CHUNK - 1, pltpu.roll(dco_after, CHUNK - 1, 0), pltpu.roll(dco, CHUNK - 1, 0))
            dco2 = jnp.where(row >= CHUNK - 2, pltpu.roll(dco_after, CHUNK - 2, 0), pltpu.roll(dco, CHUNK - 2, 0))
            dz = cw[2] * dco + cw[1] * dco1 + cw[0] * dco2
            dpj_ref[rows, OFF_CC:OFF_CC + D_CONV] = (dz * cu).astype(BF16)
            dpj_ref[rows, OFF_CU:OFF_CU + D_CONV] = (dz * cc).astype(BF16)
            acc_scr[ACC_NORM_CONV] += gterm
            acc_scr[ACC_CONV0] += dco * z2
            acc_scr[ACC_CONV0 + 1] += dco * z1
            acc_scr[ACC_CONV0 + 2] += dco * z
            dco_after = dco
        nxt_scr[...] = dco_after

        ks, vs = _kv_bands(pj, kvp_ref)
        gain_a = ga_ref[...]

        for r in range(N_CHUNKS):
            rows = _chunk_rows(r)
            da, dgate, gterm = _gated_norm_bwd(attn_ref[rows, :], gain_a, pj_ref[rows, OFF_GA:OFF_GA + D_ATTN],
                                               dmx_ref[rows, D_CONV:D_MIX])
            dpj_ref[rows, OFF_GA:OFF_GA + D_ATTN] = dgate.astype(BF16)
            dattn_scr[rows, :] = da
            acc_scr[ACC_NORM_ATTN] += gterm

        in_lo = lax.broadcasted_iota(jnp.int32, (128, 128), 0) < HEAD_DIM
        half_ones = (jnp.where(in_lo, 1.0, 0.0).astype(BF16), jnp.where(in_lo, 0.0, 1.0).astype(BF16))
        lane_s = lax.broadcasted_iota(jnp.int32, (1, D_MODEL), 1)
        gsink = jnp.zeros((1, D_MODEL), F32)
        dk_t, dv_t = [], []
        for j in range(2):
            q_stack = _q_stack(pj, j)
            do_f = jnp.concatenate([dattn_scr[:, _pair_cols(j, i, 0)] for i in range(PAIRS_PER_KV)], axis=0)
            o_f = jnp.concatenate([attn_ref[:, _pair_cols(j, i, 0)] for i in range(PAIRS_PER_KV)], axis=0)
            prod = (do_f * o_f).astype(BF16)
            deltas = [jnp.dot(prod, half_ones[e], preferred_element_type=F32) for e in range(2)]
            do_b = do_f.astype(BF16)
            q_t, do_t = q_stack.T, do_b.T
            dq, dk_j, dv_j = None, None, None
            for e in range(2):
                p = p_ref[0, 2 * j + e]
                dp = lax.dot_general(do_b, vs[j][e], _NT, preferred_element_type=F32)
                ds = []
                for i in range(PAIRS_PER_KV):
                    rows = slice(BLOCK * i, BLOCK * (i + 1))
                    delta = deltas[e][rows, :]
                    ds.append((p[rows, :].astype(F32) * (dp[rows, :] - jnp.concatenate([delta, delta], axis=1))).astype(BF16))
                    gs_h = -jnp.sum(ps_ref[0, 2 * j + e, rows, 0:1] * delta[:, 0:1], axis=0, keepdims=True)
                    gsink = gsink + jnp.where(lane_s == _head(j, i, e), gs_h, 0.0)
                ds = jnp.concatenate(ds, axis=0)
                t = jnp.dot(ds, ks[j][e], preferred_element_type=F32)
                dq = t if dq is None else dq + t
                half = slice(HEAD_DIM * e, HEAD_DIM * (e + 1))
                a = jnp.dot(q_t[half, :], ds, preferred_element_type=F32)
                b = jnp.dot(do_t[half, :], p, preferred_element_type=F32)
                dk_j = a if dk_j is None else dk_j + a
                dv_j = b if dv_j is None else dv_j + b
            for i in range(PAIRS_PER_KV):
                dpj_ref[:, _pair_cols(j, i, OFF_Q)] = (dq[BLOCK * i:BLOCK * (i + 1), :] * SCALE).astype(BF16)
            dk_t.append(dk_j)
            dv_t.append(dv_j)
        dk = jnp.concatenate(dk_t, axis=0).T
        dv = jnp.concatenate(dv_t, axis=0).T
        dpj_ref[:, OFF_K:OFF_K + D_KV] = (dk[BLOCK:, :] + dkv_scr[:, 0:D_KV]).astype(BF16)
        dpj_ref[:, OFF_V:OFF_V + D_KV] = (dv[BLOCK:, :] + dkv_scr[:, D_KV:2 * D_KV]).astype(BF16)
        dkv_scr[:, 0:D_KV] = dk[:BLOCK, :]
        dkv_scr[:, D_KV:2 * D_KV] = dv[:BLOCK, :]
        gslab_ref[ROW_SINKS:ROW_SINKS + 1, :] += gsink

        @pl.when(step == N_BLOCKS - 1)
        def _():
            for k, slab_row in ((ACC_NORM_CONV, ROW_NORM_CONV), (ACC_NORM_ATTN, ROW_NORM_ATTN), (ACC_CONV0, ROW_CONV0),
                                (ACC_CONV0 + 1, ROW_CONV0 + 1), (ACC_CONV0 + 2, ROW_CONV0 + 2)):
                gslab_ref[slab_row:slab_row + 1, :] = jnp.sum(acc_scr[k], axis=0, keepdims=True)

    per_block = BLOCK // HALO
    last = N_BLOCKS - 1
    return pl.pallas_call(
        body, name="mix_bwd", grid=(N_BLOCKS,),
        in_specs=[
            pl.BlockSpec((BLOCK, D_PROJ), lambda s: (last - s, 0)),
            pl.BlockSpec((BLOCK, 2 * D_KV), lambda s: (jnp.maximum(last - s - 1, 0), OFF_K // (2 * D_KV))),
            pl.BlockSpec((HALO, D_CONV), lambda s: (jnp.maximum((last - s) * per_block - 1, 0), OFF_CC // D_CONV)),
            pl.BlockSpec((HALO, D_CONV), lambda s: (jnp.maximum((last - s) * per_block - 1, 0), OFF_CU // D_CONV)),
            pl.BlockSpec((BLOCK, D_MIX), lambda s: (last - s, 0)),
            pl.BlockSpec((BLOCK, D_ATTN), lambda s: (last - s, 0)),
            pl.BlockSpec((1, 4, STACK, 2 * BLOCK), lambda s: (last - s, 0, 0, 0)),
            pl.BlockSpec((1, 4, STACK, 128), lambda s: (last - s, 0, 0, 0)),
            pl.BlockSpec((8, D_CONV), lambda s: (0, 0)),
            pl.BlockSpec((1, D_CONV), lambda s: (0, 0)),
            pl.BlockSpec((1, D_ATTN), lambda s: (0, 0)),
        ],
        out_specs=(pl.BlockSpec((BLOCK, D_PROJ), lambda s: (last - s, 0)),
                   pl.BlockSpec((8, D_MODEL), lambda s: (0, 0))),
        out_shape=(jax.ShapeDtypeStruct((SEQ, D_PROJ), BF16), jax.ShapeDtypeStruct((8, D_MODEL), F32)),
        scratch_shapes=[pltpu.VMEM((BLOCK, D_ATTN), F32), pltpu.VMEM((CHUNK, D_CONV), F32),
                        pltpu.VMEM((BLOCK, 2 * D_KV), F32), pltpu.VMEM((N_ACC, CHUNK, D_MODEL), F32)],
        compiler_params=_params(dimension_semantics=("arbitrary",)),
    )(proj, proj, proj, proj, dmixed, attn, probs, shares, conv_full, norm_conv, norm_attn)


def _in_bwd_rs(dproj, w_full, x, dx2, norm_in, dw_in_chip, gslab, gnf, loss_part):
    tm = 256
    steps = SEQ // tm

    def body(dp_ref, w_hbm, x_ref, dx2_ref, g_ref, dwi_ref, gs_ref, gnf_ref, lp_ref, gx_ref, gwin_ref, gsum_ref,
             gni_scr, own, d2d, ici, myslab, slabs, w_ref, send_sems, recv_sems, local_sems):
        i = pl.program_id(0)
        rs_start, rs_finish = _ici_sum(dwi_ref, own, d2d, ici, send_sems, recv_sems, local_sems)
        slab_start, slab_finish = _slab_sum(myslab, slabs, send_sems, recv_sems, N_ICI_SUM_SEMS)

        @pl.when(i == 0)
        def _():
            gni_scr[...] = jnp.zeros_like(gni_scr)
            rs_start()
            w_load = pltpu.make_async_copy(w_hbm, w_ref, local_sems.at[1])
            w_load.start()
            w_load.wait()

        dh = jnp.dot(dp_ref[...], w_ref[...], preferred_element_type=F32)
        xv = x_ref[...]
        r = lax.rsqrt(jnp.mean(xv * xv, axis=-1, keepdims=True) + RMS_EPS)
        xn = xv * r
        u = dh * g_ref[...]
        gx_ref[...] = dx2_ref[...] + r * (u - xn * jnp.mean(u * xn, axis=-1, keepdims=True))
        gni_scr[...] += jnp.sum(dh * xn, axis=0, keepdims=True)

        @pl.when(i == steps - 1)
        def _():
            row = lax.broadcasted_iota(jnp.int32, (8, D_MODEL), 0)
            lane = lax.broadcasted_iota(jnp.int32, (8, D_MODEL), 1)
            slab = jnp.where(row == ROW_NORM_IN, gni_scr[...], jnp.where(row == ROW_NORM_FINAL, gnf_ref[...], gs_ref[...]))
            myslab[...] = jnp.where((row == ROW_SINKS) & (lane == LOSS_LANE), lp_ref[0:1, 0:1], slab)
            slab_start()
            gwin_ref[...] = rs_finish()
            gsum_ref[...] = slab_finish()

    const = lambda i: (0, 0)
    return pl.pallas_call(
        body, name="in_bwd", grid=(steps,),
        in_specs=[pl.BlockSpec((tm, D_PROJ), lambda i: (i, 0)), pl.BlockSpec(memory_space=pl.ANY),
                  pl.BlockSpec((tm, D_MODEL), lambda i: (i, 0)), pl.BlockSpec((tm, D_MODEL), lambda i: (i, 0)),
                  pl.BlockSpec((1, D_MODEL), const), pl.BlockSpec(memory_space=pl.ANY),
                  pl.BlockSpec((8, D_MODEL), const), pl.BlockSpec((1, D_MODEL), const), pl.BlockSpec((8, 128), const)],
        out_specs=(pl.BlockSpec((tm, D_MODEL), lambda i: (i, 0)), pl.BlockSpec((SHARD_IN, D_MODEL), const),
                   pl.BlockSpec((8, D_MODEL), const)),
        out_shape=(jax.ShapeDtypeStruct((SEQ, D_MODEL), F32), jax.ShapeDtypeStruct((SHARD_IN, D_MODEL), F32),
                   jax.ShapeDtypeStruct((8, D_MODEL), F32)),
        scratch_shapes=[pltpu.VMEM((1, D_MODEL), F32), pltpu.VMEM((SHARD_IN, D_MODEL), BF16),
                        pltpu.VMEM((SHARD_IN, D_MODEL), BF16), pltpu.VMEM((2, SHARD_IN, D_MODEL), BF16),
                        pltpu.VMEM((8, D_MODEL), F32), pltpu.VMEM((N_DEV, 8, D_MODEL), F32),
                        pltpu.VMEM((D_PROJ, D_MODEL), BF16),
                        pltpu.SemaphoreType.DMA((N_ICI_SUM_SEMS + 7,)), pltpu.SemaphoreType.DMA((N_ICI_SUM_SEMS + 7,)),
                        pltpu.SemaphoreType.DMA((2,))],
        compiler_params=_params(dimension_semantics=("arbitrary",)),
    )(dproj, w_full, x, dx2, norm_in, dw_in_chip, gslab, gnf, loss_part)


def _dw_rs(mixed, dx2b, dproj, h, table):
    tn_out, tn = 2 * SHARD_OUT, IN_PROJ_TILE
    out_steps, in_steps = D_MIX // tn_out, D_PROJ // tn
    steps = out_steps + in_steps
    out_order = (DG, NX, NY, OWN)

    def out_tile(i):
        chip = 2 * lax.axis_index("x") + lax.axis_index("y")
        return jnp.bitwise_xor(chip, (out_steps - 1) - jnp.minimum(i, out_steps - 1))

    def in_tile(table_ref, i):
        return _dw_entry(table_ref, jnp.maximum(i - out_steps, 0))

    def body(table_ref, mx_ref, dxb_ref, a_ref, b_ref, chip_ref, gwo_ref, dwo, dwt, d2d_in, via, own, d2d, ici,
             send_sems, recv_sems, local_sems):
        i = pl.program_id(0)
        rs_start, rs_forward, rs_finish = _shard_sum(dwo, own, d2d, ici, send_sems, recv_sems, local_sems)
        before_tile, after_tiles, chip_finish = _chip_sum(
            dwt, d2d_in, via, chip_ref, lambda k: _dw_entry(table_ref, in_steps + k), send_sems, recv_sems, local_sems,
            N_SHARD_SUM_SEMS, 4)

        for j, k in enumerate(out_order):
            @pl.when(i == j + 1)
            def _():
                rs_start(k)

            if k != OWN:
                @pl.when(i == j + 2)
                def _():
                    rs_forward(k)

        @pl.when(i < out_steps)
        def _():
            tile = lax.dot_general(mx_ref[...], dxb_ref[...], _TN, preferred_element_type=F32).astype(BF16)
            for core in range(2):
                dwo[2 * out_tile(i) + core] = tile[SHARD_OUT * core:SHARD_OUT * (core + 1), :]

        @pl.when(i >= out_steps)
        def _():
            before_tile(i - out_steps)
            tile = lax.dot_general(a_ref[...], b_ref[...], _TN, preferred_element_type=F32).astype(BF16)
            dwt[pl.ds(pl.multiple_of(in_tile(table_ref, i) * tn, tn), tn), :] = tile

        @pl.when(i == steps - 1)
        def _():
            after_tiles()
            gwo_ref[...] = rs_finish()
            chip_finish()

    vmem = pl.BlockSpec(memory_space=pltpu.VMEM)
    grid_spec = pltpu.PrefetchScalarGridSpec(
        num_scalar_prefetch=1, grid=(steps,),
        in_specs=[pl.BlockSpec((SEQ, tn_out), lambda i, table_ref: (0, out_tile(i))), vmem,
                  pl.BlockSpec((SEQ, tn), lambda i, table_ref: (0, in_tile(table_ref, i))), vmem],
        out_specs=(pl.BlockSpec(memory_space=pl.ANY), pl.BlockSpec((SHARD_OUT, D_MODEL), lambda i, table_ref: (0, 0))),
        scratch_shapes=[pltpu.VMEM((N_DEV, SHARD_OUT, D_MODEL), BF16),
                        pltpu.VMEM((D_PROJ, D_MODEL), BF16), pltpu.VMEM((3, SHARD_IN, D_MODEL), BF16),
                        pltpu.VMEM((2, HALF_IN, D_MODEL), BF16),
                        *_shard_sum_scratch(SHARD_OUT),
                        pltpu.SemaphoreType.DMA((N_SHARD_SUM_SEMS + N_CHIP_SUM_SEMS,)),
                        pltpu.SemaphoreType.DMA((N_SHARD_SUM_SEMS + N_CHIP_SUM_SEMS,)),
                        pltpu.SemaphoreType.DMA((8,))])
    return pl.pallas_call(
        body, name="dw", grid_spec=grid_spec,
        out_shape=(jax.ShapeDtypeStruct((4, SHARD_IN, D_MODEL), BF16), jax.ShapeDtypeStruct((SHARD_OUT, D_MODEL), F32)),
        compiler_params=_params(dimension_semantics=("arbitrary",)),
    )(table, mixed, dx2b, dproj, h)


def _adam_all(big_in, big_out, gsum, small, grad_x):
    n_chunks = 4
    n_big = 8

    def body(*refs):
        ins, outs = refs[:n_big + 1 + 18 + 1], refs[n_big + 1 + 18 + 1:n_big + 1 + 18 + 1 + 34]
        in_bufs, out_bufs, gx_buf = refs[-n_big - 6 - 4:-6 - 4], refs[-6 - 4:-4], refs[-4]
        in_sems, out_sems, gx_sems = refs[-3:]

        def gx_rows(j):
            return pl.ds(j * (SEQ // n_chunks), SEQ // n_chunks)

        def gx_load(j):
            return pltpu.make_async_copy(ins[27].at[gx_rows(j), :], gx_buf.at[gx_rows(j), :], gx_sems.at[j])

        def gx_store(j):
            return pltpu.make_async_copy(gx_buf.at[gx_rows(j), :], outs[33].at[gx_rows(j), :], gx_sems.at[n_chunks + j])

        def rows(a, j):
            tr = ins[a].shape[0] // n_chunks
            return pl.ds(j * tr, tr)

        def load(a, j):
            return pltpu.make_async_copy(ins[a].at[rows(a, j), :], in_bufs[a].at[rows(a, j), :], in_sems.at[a * n_chunks + j])

        def store(a, j):
            b, kind = divmod(a, 4)
            src = in_bufs[4 * b + 1] if kind == 0 else out_bufs[3 * b + kind - 1]
            return pltpu.make_async_copy(src.at[rows(a, j), :], outs[a].at[rows(a, j), :], out_sems.at[a * n_chunks + j])

        for j in range(n_chunks):
            for a in range(n_big):
                load(a, j).start()
            gx_load(j).start()

        def small_weights():
            gsum = ins[8][...]
            idx = _slot(lax.axis_index("x"), lax.axis_index("y"), lax.axis_index("c"))
            cg = jnp.zeros((3, SHARD_CONV), F32)
            for d in range(N_DEV):
                cg = jnp.where(idx == d, gsum[ROW_CONV0:ROW_CONV0 + 3, d * SHARD_CONV:(d + 1) * SHARD_CONV], cg)
            grads = (gsum[ROW_NORM_IN:ROW_NORM_IN + 1], gsum[ROW_SINKS:ROW_SINKS + 1, 0:N_Q_HEADS],
                     gsum[ROW_NORM_CONV:ROW_NORM_CONV + 1], gsum[ROW_NORM_ATTN:ROW_NORM_ATTN + 1],
                     gsum[ROW_NORM_FINAL:ROW_NORM_FINAL + 1], cg)
            for s, g in enumerate(grads):
                at = (slice(None), 0, slice(None)) if s == 5 else (slice(None), slice(None))
                w_ref, m_ref, v_ref = ins[9 + 3 * s:12 + 3 * s]
                delta, mn, vn = _adamw(w_ref[at], g, m_ref[at], v_ref[at])
                for ref, val in zip(outs[8 + 4 * s:12 + 4 * s], (g, delta, mn, vn)):
                    ref[at] = val
            outs[32][...] = gsum[ROW_SINKS:ROW_SINKS + 1, LOSS_LANE:LOSS_LANE + 1]

        small_weights()
        for j in range(n_chunks):
            for b in range(2):
                for a in range(4 * b, 4 * b + 4):
                    load(a, j).wait()
                w_buf, g_buf, m_buf, v_buf = in_bufs[4 * b:4 * b + 4]
                r = rows(4 * b, j)
                results = _adamw(w_buf[r, :], g_buf[r, :], m_buf[r, :], v_buf[r, :])
                for buf, val in zip(out_bufs[3 * b:3 * b + 3], results):
                    buf[r, :] = val
                for a in range(4 * b, 4 * b + 4):
                    store(a, j).start()
            gx_load(j).wait()
            gx_store(j).start()
        for j in range(n_chunks):
            for a in range(n_big):
                store(a, j).wait()
            gx_store(j).wait()

    vmem, hbm = pl.BlockSpec(memory_space=pltpu.VMEM), pl.BlockSpec(memory_space=pl.ANY)
    small_shapes = [a.shape for a in small[::3]]
    big_shapes = [(SHARD_IN, D_MODEL)] * 4 + [(SHARD_OUT, D_MODEL)] * 4
    out_shape = ([jax.ShapeDtypeStruct(s, F32) for s in big_shapes]
                 + [jax.ShapeDtypeStruct(s, F32) for s in small_shapes for _ in range(4)]
                 + [jax.ShapeDtypeStruct((1, 1), F32), jax.ShapeDtypeStruct((SEQ, D_MODEL), F32)])
    outs = pl.pallas_call(
        body, name="adam", in_specs=[hbm] * n_big + [vmem] * (1 + len(small)) + [hbm],
        out_specs=tuple([hbm] * n_big + [vmem] * (4 * len(small_shapes) + 1) + [hbm]), out_shape=tuple(out_shape),
        scratch_shapes=[pltpu.VMEM(s, F32) for s in big_shapes]
                       + [pltpu.VMEM(s, F32) for s in [(SHARD_IN, D_MODEL)] * 3 + [(SHARD_OUT, D_MODEL)] * 3]
                       + [pltpu.VMEM((SEQ, D_MODEL), F32),
                          pltpu.SemaphoreType.DMA((n_big * n_chunks,)), pltpu.SemaphoreType.DMA((n_big * n_chunks,)),
                          pltpu.SemaphoreType.DMA((2 * n_chunks,))],
        compiler_params=_params(),
    )(*big_in, *big_out, gsum, *small, grad_x)
    return outs[0:4], outs[4:8], [outs[8 + 4 * s:12 + 4 * s] for s in range(6)], outs[32], outs[33]


def _rows_first(a):
    return jnp.transpose(a, (1, 0, 2))


def kernel(x, norm_in, w_in, conv_w, attn_sinks, norm_conv_out, norm_attn_out, w_out, norm_final, loss_target, m_norm_in, m_w_in, m_conv_w, m_attn_sinks, m_norm_conv_out, m_norm_attn_out, m_w_out, m_norm_final, v_norm_in, v_w_in, v_conv_w, v_attn_sinks, v_norm_conv_out, v_norm_attn_out, v_w_out, v_norm_final):
    x2d = x.reshape(SEQ, D_MODEL)
    target = loss_target.reshape(SEQ, D_MODEL)
    nf = norm_final.reshape(1, D_MODEL)

    w_in_t, m_w_in_t, v_w_in_t = w_in[0].T, m_w_in[0].T, v_w_in[0].T
    tiles = jnp.asarray(TILE_ORDER, jnp.int32).reshape(-1)
    w_in_full, h, proj, g_out, conv_full = _gather_in_proj(x2d, norm_in, w_in_t, w_out[0], _rows_first(conv_w), tiles)
    sinks = attn_sinks.reshape(N_Q_HEADS)

    mixed, attn, probs, shares = _mix_fwd(proj, conv_full, sinks, norm_conv_out, norm_attn_out)
    dx2, dx2b, dmixed, gnf, loss_part = _out_proj_loss(mixed, x2d, target, g_out.reshape(D_MIX, D_MODEL), nf)
    dproj, gslab = _mix_bwd(proj, dmixed, attn, probs, shares, conv_full, norm_conv_out, norm_attn_out)
    dw_in_chip, g_w_out = _dw_rs(mixed, dx2b, dproj, h, jnp.asarray(DW_TABLE, jnp.int32).reshape(-1))
    grad_x, g_w_in, gsum = _in_bwd_rs(dproj, w_in_full, x2d, dx2, norm_in, dw_in_chip, gslab, gnf, loss_part)

    small = (norm_in, m_norm_in, v_norm_in, attn_sinks, m_attn_sinks, v_attn_sinks,
             norm_conv_out, m_norm_conv_out, v_norm_conv_out, norm_attn_out, m_norm_attn_out, v_norm_attn_out,
             nf, m_norm_final.reshape(1, D_MODEL), v_norm_final.reshape(1, D_MODEL),
             _rows_first(conv_w), _rows_first(m_conv_w), _rows_first(v_conv_w))
    big_in, big_out, (s_ni, s_sk, s_nc, s_na, s_nf, s_cv), loss, grad_x = _adam_all(
        (w_in_t, g_w_in, m_w_in_t, v_w_in_t), (w_out[0], g_w_out, m_w_out[0], v_w_out[0]), gsum, small, grad_x)

    def leaves(k):
        return (s_ni[k], big_in[k].T[None], jnp.transpose(s_cv[k], (1, 0, 2)), s_sk[k], s_nc[k], s_na[k], big_out[k][None],
                s_nf[k].reshape(D_MODEL))

    return (loss.reshape(()), grad_x.reshape(1, SEQ, D_MODEL), *leaves(0), *leaves(1), *leaves(2), *leaves(3))
```

```python
import jax
import jax.numpy as jnp
from jax import lax
from jax.experimental import pallas as pl
from jax.experimental.pallas import tpu as pltpu

F32 = jnp.float32
BF16 = jnp.bfloat16
MESH = pl.DeviceIdType.MESH

N_DEV = 8
SEQ = 2048
D_MODEL = 1024
D_CONV = 1024
D_ATTN = 1024
D_KV = 128
HEAD_DIM = 64
N_Q_HEADS = 16
N_PAIRS = N_Q_HEADS // 2
PAIRS_PER_KV = N_PAIRS // 2
D_MIX = D_CONV + D_ATTN
D_PROJ = 6400
SHARD_IN = D_PROJ // N_DEV
SHARD_OUT = D_MIX // N_DEV
SHARD_CONV = D_CONV // N_DEV
OFF_CB, OFF_CC, OFF_CU, OFF_GC, OFF_Q, OFF_K, OFF_V, OFF_GA = 0, 1024, 2048, 3072, 4096, 5120, 5248, 5376
BLOCK = 128
N_BLOCKS = SEQ // BLOCK
HALO = 8
CHUNK = 16
N_CHUNKS = BLOCK // CHUNK
RMS_EPS = 1e-5
NEG = -1e30
SCALE = HEAD_DIM ** -0.5
SLOPES = tuple(2.0 ** (-8.0 * (h + 1) / N_Q_HEADS) for h in range(N_Q_HEADS))

ADAM_LR = 0.001
ADAM_B1 = 0.9
ADAM_B2 = 0.999
ADAM_EPS = 1e-08
ADAM_WD = 0.01
ADAM_STEP = 10

ROW_NORM_IN, ROW_NORM_CONV, ROW_NORM_ATTN, ROW_NORM_FINAL, ROW_CONV0, ROW_SINKS = 0, 1, 2, 3, 4, 7
LOSS_LANE = N_Q_HEADS
ACC_NORM_CONV, ACC_NORM_ATTN, ACC_CONV0, N_ACC = 0, 1, 2, 5

VMEM_LIMIT = 56 * 1024 * 1024

_NT = (((1,), (1,)), ((), ()))
_TN = (((0,), (0,)), ((), ()))


def _params(**kw):
    return pltpu.CompilerParams(vmem_limit_bytes=VMEM_LIMIT, **kw)


def _adamw(w, g, m, v):
    m = ADAM_B1 * m + (1.0 - ADAM_B1) * g
    v = ADAM_B2 * v + (1.0 - ADAM_B2) * (g * g)
    m_hat = m / (1.0 - ADAM_B1 ** ADAM_STEP)
    v_hat = v / (1.0 - ADAM_B2 ** ADAM_STEP)
    delta = -ADAM_LR * (m_hat / (jnp.sqrt(v_hat) + ADAM_EPS) + ADAM_WD * w)
    return delta, m, v


def _sigmoid(t):
    return 1.0 / (1.0 + jnp.exp(-t))


def _slot(px, py, pc):
    return 4 * px + 2 * py + pc


OWN, NX, NY, DG = range(4)
HALF_IN = SHARD_IN // 2
N_GATHER_KINDS = 13
W_OUT_KINDS = N_GATHER_KINDS + 7


IN_PROJ_TILE = 640
TILE_ORDER = ((0, 1, 2, 3, 4, 5, 6, 7, 8, 9), (3, 4, 0, 1, 2, 8, 9, 5, 6, 7),
              (5, 6, 0, 1, 7, 8, 9, 2, 3, 4), (8, 9, 3, 4, 5, 6, 7, 0, 1, 2))
TILES_OWN, TILES_NEIGHBOURS = 2, 7


def _tile(table_ref, p):
    chip = 2 * lax.axis_index("x") + lax.axis_index("y")
    return table_ref[chip * len(TILE_ORDER[0]) + p]


DW_TILE_ORDER = tuple(tuple(reversed(row)) for row in TILE_ORDER)


def _tiles_until_complete(chip, owner):
    lo, hi = owner * 2 * SHARD_IN, (owner + 1) * 2 * SHARD_IN
    touching = [t for t in range(len(TILE_ORDER[0])) if t * IN_PROJ_TILE < hi and (t + 1) * IN_PROJ_TILE > lo]
    return 1 + max(DW_TILE_ORDER[chip].index(t) for t in touching)


DW_TABLE = tuple(DW_TILE_ORDER[chip] + tuple(_tiles_until_complete(chip, chip ^ flip) for flip in (0, 2, 1, 3))
                 for chip in range(4))


def _dw_entry(table_ref, p):
    chip = 2 * lax.axis_index("x") + lax.axis_index("y")
    return table_ref[chip * len(DW_TABLE[0]) + p]


def _gather_in_proj(x, norm_in, w_in_sh, w_out_sh, conv_sh, tiles):
    tn = IN_PROJ_TILE
    steps = D_PROJ // tn
    tm = 256

    def body(tiles_ref, x_hbm, g_ref, win_ref, wout_ref, cv_ref, wt_ref, h_ref, proj_ref, gout_ref, conv_ref,
             gin_ref, gcv_ref, wob_ref, x_ref, send_sems, recv_sems, local_sems):
        p = pl.program_id(0)
        local_sem = local_sems.at[0]
        x, y, c = lax.axis_index("x"), lax.axis_index("y"), lax.axis_index("c")
        me, sibling = (x, y, c), (x, y, 1 - c)
        nx, ny, dg = (1 - x, y, c), (x, 1 - y, c), (1 - x, 1 - y, c)

        def other(dev):
            return (dev[0], dev[1], 1 - dev[2])

        def shard(dev):
            return gin_ref.at[pl.ds(pl.multiple_of(_slot(*dev) * SHARD_IN, 16), SHARD_IN), :]

        def half(dev, h):
            return gin_ref.at[pl.ds(pl.multiple_of(_slot(*dev) * SHARD_IN + h * HALF_IN, 16), HALF_IN), :]

        def rc(ref, k, to):
            return pltpu.make_async_remote_copy(src_ref=ref, dst_ref=ref, send_sem=send_sems.at[k],
                                                recv_sem=recv_sems.at[k], device_id=to, device_id_type=MESH)

        def cv(k, dev, to):
            s = _slot(*dev)
            return pltpu.make_async_remote_copy(src_ref=gcv_ref.at[s], dst_ref=gcv_ref.at[s],
                                                send_sem=send_sems.at[N_GATHER_KINDS + k],
                                                recv_sem=recv_sems.at[N_GATHER_KINDS + k], device_id=to, device_id_type=MESH)

        def own_copies():
            return [rc(shard(me), 0, sibling),
                    rc(half(me, 0), 1, nx), rc(half(me, 1), 2, nx),
                    rc(half(me, 1), 4, ny), rc(half(me, 0), 3, ny),
                    cv(0, me, sibling)] + [cv(1 + j, me, peer) for j, peer in enumerate((nx, ny, dg))]

        def pass_on(dev, h, k_in, k_ici, k_d2d, half=half, base=0):
            rc(half(dev, h), base + k_in, me).wait_recv()
            if k_ici is not None:
                rc(half(dev, h), base + k_ici, ny if dev is nx else nx).start()
            rc(half(dev, h), base + k_d2d, sibling).start()

        def out_half(dev, h):
            return wob_ref.at[_slot(*dev), pl.ds(h * (SHARD_OUT // 2), SHARD_OUT // 2), :]

        def own_out_copies():
            return [rc(wob_ref.at[_slot(*me)], W_OUT_KINDS, sibling),
                    rc(out_half(me, 0), W_OUT_KINDS + 1, nx), rc(out_half(me, 1), W_OUT_KINDS + 2, nx),
                    rc(out_half(me, 1), W_OUT_KINDS + 4, ny), rc(out_half(me, 0), W_OUT_KINDS + 3, ny)]

        def out_save():
            return pltpu.make_async_copy(wob_ref, gout_ref, local_sems.at[1])

        @pl.when(p == 0)
        def _():
            gin_ref[pl.ds(pl.multiple_of(_slot(*me) * SHARD_IN, 16), SHARD_IN), :] = win_ref[...].astype(BF16)
            gcv_ref[_slot(*me)] = jnp.zeros((8, SHARD_CONV), F32)
            gcv_ref[_slot(*me), 0:3, :] = cv_ref[:, 0, :]
            for cp in own_copies():
                cp.start()
            wob_ref[_slot(*me)] = wout_ref[...].astype(BF16)
            x_load = pltpu.make_async_copy(x_hbm, x_ref, local_sems.at[2])
            x_load.start()
            x_load.wait()
            for t in range(SEQ // tm):
                xv = x_ref[tm * t:tm * (t + 1), :]
                r = lax.rsqrt(jnp.mean(xv * xv, axis=-1, keepdims=True) + RMS_EPS)
                h_ref[tm * t:tm * (t + 1), :] = (xv * r * g_ref[...]).astype(BF16)
            rc(shard(sibling), 0, me).wait_recv()

        @pl.when(p == TILES_OWN)
        def _():
            for args in ((nx, 0, 1, 5, 7), (ny, 1, 4, 6, 10), (nx, 1, 2, None, 8), (ny, 0, 3, None, 9)):
                pass_on(*args)
            for j, peer in enumerate((nx, ny, dg)):
                cv(1 + j, peer, me).wait_recv()
                cv(4 + j, peer, sibling).start()
            for (dev, h), k in (((nx, 0), 7), ((nx, 1), 8), ((ny, 0), 9), ((ny, 1), 10)):
                rc(half(other(dev), h), k, me).wait_recv()
            for cp in own_out_copies():
                cp.start()

        @pl.when(p == TILES_NEIGHBOURS - 1)
        def _():
            pass_on(dg, 0, 5, None, 11)
            pass_on(dg, 1, 6, None, 12)

        @pl.when(p == TILES_NEIGHBOURS)
        def _():
            for (dev, h), k in (((dg, 0), 11), ((dg, 1), 12)):
                rc(half(other(dev), h), k, me).wait_recv()
            pltpu.make_async_copy(gin_ref, wt_ref, local_sem).start()

        @pl.when(p == steps - 2)
        def _():
            for args in ((nx, 0, 1, 5, 7), (ny, 1, 4, 6, 10), (nx, 1, 2, None, 8), (ny, 0, 3, None, 9)):
                pass_on(*args, half=out_half, base=W_OUT_KINDS)

        w = gin_ref[pl.ds(pl.multiple_of(_tile(tiles_ref, p) * tn, tn), tn), :]
        proj_ref[...] = lax.dot_general(h_ref[...], w, _NT, preferred_element_type=F32)

        @pl.when(p == steps - 1)
        def _():
            cv(0, sibling, me).wait_recv()
            for j, peer in enumerate((nx, ny, dg)):
                cv(4 + j, other(peer), me).wait_recv()
            for d in range(N_DEV):
                conv_ref[:, d * SHARD_CONV:(d + 1) * SHARD_CONV] = gcv_ref[d]
            relayed = [rc(half(nx, 0), 5, ny), rc(half(ny, 1), 6, nx)]
            relayed += [rc(half(dev, h), k, sibling) for (dev, h), k in
                        (((nx, 0), 7), ((nx, 1), 8), ((ny, 0), 9), ((ny, 1), 10), ((dg, 0), 11), ((dg, 1), 12))]
            relayed += [cv(4 + j, peer, sibling) for j, peer in enumerate((nx, ny, dg))]
            for cp in own_copies() + relayed:
                cp.wait_send()
            pltpu.make_async_copy(gin_ref, wt_ref, local_sem).wait()
            pass_on(dg, 0, 5, None, 11, half=out_half, base=W_OUT_KINDS)
            pass_on(dg, 1, 6, None, 12, half=out_half, base=W_OUT_KINDS)
            rc(wob_ref.at[_slot(*sibling)], W_OUT_KINDS, me).wait_recv()
            out_relayed = [rc(out_half(nx, 0), W_OUT_KINDS + 5, ny), rc(out_half(ny, 1), W_OUT_KINDS + 6, nx)]
            for (dev, h), k in (((nx, 0), 7), ((nx, 1), 8), ((ny, 0), 9), ((ny, 1), 10), ((dg, 0), 11), ((dg, 1), 12)):
                rc(out_half(other(dev), h), W_OUT_KINDS + k, me).wait_recv()
                out_relayed.append(rc(out_half(dev, h), W_OUT_KINDS + k, sibling))
            out_save().start()
            for cp in own_out_copies() + out_relayed:
                cp.wait_send()
            out_save().wait()

    vmem = pl.BlockSpec(memory_space=pltpu.VMEM)
    grid_spec = pltpu.PrefetchScalarGridSpec(
        num_scalar_prefetch=1, grid=(steps,),
        in_specs=[pl.BlockSpec(memory_space=pl.ANY), vmem, vmem, vmem, vmem],
        out_specs=(pl.BlockSpec(memory_space=pl.ANY), vmem,
                   pl.BlockSpec((SEQ, tn), lambda p, tiles_ref: (0, _tile(tiles_ref, p))),
                   pl.BlockSpec(memory_space=pl.ANY), vmem),
        scratch_shapes=[pltpu.VMEM((D_PROJ, D_MODEL), BF16), pltpu.VMEM((N_DEV, 8, SHARD_CONV), F32),
                        pltpu.VMEM((N_DEV, SHARD_OUT, D_MODEL), BF16), pltpu.VMEM((SEQ, D_MODEL), F32),
                        pltpu.SemaphoreType.DMA((W_OUT_KINDS + N_GATHER_KINDS,)),
                        pltpu.SemaphoreType.DMA((W_OUT_KINDS + N_GATHER_KINDS,)),
                        pltpu.SemaphoreType.DMA((3,))])
    return pl.pallas_call(
        body, name="gather_in_proj", grid_spec=grid_spec,
        out_shape=(jax.ShapeDtypeStruct((D_PROJ, D_MODEL), BF16), jax.ShapeDtypeStruct((SEQ, D_MODEL), BF16),
                   jax.ShapeDtypeStruct((SEQ, D_PROJ), F32), jax.ShapeDtypeStruct((N_DEV, SHARD_OUT, D_MODEL), BF16),
                   jax.ShapeDtypeStruct((8, D_CONV), F32)),
        compiler_params=_params(dimension_semantics=("arbitrary",)),
    )(tiles, x, norm_in, w_in_sh, w_out_sh, conv_sh)


def _shard_sum(src, own, d2d, ici, send_sems, recv_sems, local_sems, base=0):
    x, y, c = lax.axis_index("x"), lax.axis_index("y"), lax.axis_index("c")
    sibling = (x, y, 1 - c)
    chips = [(x, y), (1 - x, y), (x, 1 - y), (1 - x, 1 - y)]

    def rcopy(s, d, k, to):
        return pltpu.make_async_remote_copy(src_ref=s, dst_ref=d, send_sem=send_sems.at[base + k],
                                            recv_sem=recv_sems.at[base + k], device_id=to, device_id_type=MESH)

    def mine(k):
        return pltpu.make_async_copy(src.at[_slot(*chips[k], c)], own.at[k], local_sems.at[k])

    def to_sibling(k):
        return rcopy(src.at[_slot(*chips[k], 1 - c)], d2d.at[k], k, sibling)

    def to_chip(k):
        return rcopy(own.at[k], ici.at[k - 1], 3 + k, (*chips[k], c))

    def start(k):
        mine(k).start()
        to_sibling(k).start()

    def forward(k):
        mine(k).wait()
        to_sibling(k).wait_recv()
        own[k] = (own[k].astype(F32) + d2d[k].astype(F32)).astype(BF16)
        to_chip(k).start()

    def finish():
        mine(0).wait()
        to_sibling(0).wait_recv()
        acc = own[0].astype(F32) + d2d[0].astype(F32)
        for k in range(1, 4):
            to_chip(k).wait_recv()
            acc = acc + ici[k - 1].astype(F32)
        for k in range(4):
            to_sibling(k).wait_send()
        for k in range(1, 4):
            to_chip(k).wait_send()
        return acc

    return start, forward, finish


def _shard_sum_scratch(rows):
    return [pltpu.VMEM((4, rows, D_MODEL), BF16), pltpu.VMEM((4, rows, D_MODEL), BF16),
            pltpu.VMEM((3, rows, D_MODEL), BF16)]


N_SHARD_SUM_SEMS = 7


N_CHIP_SUM_SEMS = 5


def _chip_sum(dwt, d2d, via, out_hbm, tiles_until, send_sems, recv_sems, local_sems, base, local_base):
    x, y, c = lax.axis_index("x"), lax.axis_index("y"), lax.axis_index("c")
    sibling, nx, ny = (x, y, 1 - c), (1 - x, y, c), (x, 1 - y, c)
    chips = [(x, y), (1 - x, y), (x, 1 - y), (1 - x, 1 - y)]

    def shard(s):
        return dwt.at[pl.ds(pl.multiple_of(s * SHARD_IN, 16), SHARD_IN), :]

    def half(ref, h):
        return ref.at[pl.ds(h * HALF_IN, HALF_IN), :]

    def rc(s, d, k, to):
        return pltpu.make_async_remote_copy(src_ref=s, dst_ref=d, send_sem=send_sems.at[base + k],
                                            recv_sem=recv_sems.at[base + k], device_id=to, device_id_type=MESH)

    def to_sibling(k):
        return rc(shard(_slot(*chips[k], 1 - c)), d2d.at[k - 1], k - 1, sibling)

    for_dg = (lambda: rc(half(d2d.at[DG - 1], 0), via.at[0], 3, nx), lambda: rc(half(d2d.at[DG - 1], 1), via.at[1], 4, ny))

    def save(k):
        return pltpu.make_async_copy(d2d.at[k - 1], out_hbm.at[k], local_sems.at[local_base + k])

    own_saves = (lambda: pltpu.make_async_copy(shard(_slot(x, y, c)), out_hbm.at[OWN], local_sems.at[local_base]),
                 lambda: pltpu.make_async_copy(shard(_slot(x, y, 1 - c)), out_hbm.at[3], local_sems.at[local_base + 3]))

    def before_tile(n):
        for k in (NX, NY, DG):
            @pl.when(tiles_until(k) == n)
            def _():
                to_sibling(k).start()

            @pl.when(tiles_until(k) + 1 == n)
            def _():
                to_sibling(k).wait_recv()
                d2d[k - 1] = (shard(_slot(*chips[k], c))[...].astype(F32) + d2d[k - 1].astype(F32)).astype(BF16)
                if k == DG:
                    for cp in for_dg:
                        cp().start()

    def after_tiles():
        for cp in own_saves:
            cp().start()

    def finish():
        for k, h in ((NY, 0), (NX, 1)):
            for_dg[h]().wait_recv()
            rows = pl.ds(h * HALF_IN, HALF_IN)
            d2d[k - 1, rows, :] = (d2d[k - 1, rows, :].astype(F32) + via[h].astype(F32)).astype(BF16)
            save(k).start()
        for cp in own_saves + (lambda: save(NX), lambda: save(NY)):
            cp().wait()
        for cp in (lambda: to_sibling(NX), lambda: to_sibling(NY), lambda: to_sibling(DG)) + for_dg:
            cp().wait_send()

    return before_tile, after_tiles, finish


N_ICI_SUM_SEMS = 3


def _ici_sum(src, own, d2d, ici, send_sems, recv_sems, local_sems, base=0):
    x, y, c = lax.axis_index("x"), lax.axis_index("y"), lax.axis_index("c")

    def rc(s, d, k, to):
        return pltpu.make_async_remote_copy(src_ref=s, dst_ref=d, send_sem=send_sems.at[base + k],
                                            recv_sem=recv_sems.at[base + k], device_id=to, device_id_type=MESH)

    copies = (lambda: rc(src.at[NX], ici.at[0], 0, (1 - x, y, c)), lambda: rc(src.at[NY], ici.at[1], 1, (x, 1 - y, c)),
              lambda: rc(src.at[3], d2d, 2, (x, y, 1 - c)))
    mine = lambda: pltpu.make_async_copy(src.at[OWN], own, local_sems.at[0])

    def start():
        for cp in copies + (mine,):
            cp().start()

    def finish():
        mine().wait()
        for cp in copies:
            cp().wait_recv()
        acc = own[...].astype(F32) + d2d[...].astype(F32) + ici[0].astype(F32) + ici[1].astype(F32)
        for cp in copies:
            cp().wait_send()
        return acc

    return start, finish


def _slab_sum(myslab, slabs, send_sems, recv_sems, base):
    x, y, c = lax.axis_index("x"), lax.axis_index("y"), lax.axis_index("c")
    me = _slot(x, y, c)
    peers = [(x, y, 1 - c), (1 - x, y, c), (x, 1 - y, c), (1 - x, 1 - y, c),
             (1 - x, y, 1 - c), (x, 1 - y, 1 - c), (1 - x, 1 - y, 1 - c)]

    def cp(k):
        return pltpu.make_async_remote_copy(src_ref=myslab, dst_ref=slabs.at[me], send_sem=send_sems.at[base + k],
                                            recv_sem=recv_sems.at[base + k], device_id=peers[k], device_id_type=MESH)

    def start():
        slabs[me] = myslab[...]
        for k in range(7):
            cp(k).start()

    def finish():
        for k in range(7):
            cp(k).wait_recv()
        total = slabs[0]
        for d in range(1, N_DEV):
            total = total + slabs[d]
        for k in range(7):
            cp(k).wait_send()
        return total

    return start, finish


def _chunk_rows(r):
    return slice(r * CHUNK, (r + 1) * CHUNK)


def _conv_halo(cch_ref, cuh_ref, n):
    zh = jnp.where(n > 0, cch_ref[...] * cuh_ref[...], 0.0)
    return jnp.concatenate([zh] * (CHUNK // HALO), axis=0)


def _conv_chunk(pj_ref, zhalo, cw, r):
    rows = _chunk_rows(r)
    cc = pj_ref[rows, OFF_CC:OFF_CC + D_CONV]
    cu = pj_ref[rows, OFF_CU:OFF_CU + D_CONV]
    z = cc * cu
    before = _chunk_rows(r - 1)
    zprev = pj_ref[before, OFF_CC:OFF_CC + D_CONV] * pj_ref[before, OFF_CU:OFF_CU + D_CONV] if r > 0 else zhalo
    row = lax.broadcasted_iota(jnp.int32, (CHUNK, D_CONV), 0)
    z1 = jnp.where(row < 1, pltpu.roll(zprev, 1, 0), pltpu.roll(z, 1, 0))
    z2 = jnp.where(row < 2, pltpu.roll(zprev, 2, 0), pltpu.roll(z, 2, 0))
    co = cw[0] * z2 + cw[1] * z1 + cw[2] * z
    return cc, cu, z, z1, z2, co


def _gated_norm(a, gain, t):
    r = lax.rsqrt(jnp.mean(a * a, axis=-1, keepdims=True) + RMS_EPS)
    return a * r * gain * (t * _sigmoid(t))


def _kv_bands(pj, kvp_ref):
    lane = lax.broadcasted_iota(jnp.int32, (2 * BLOCK, D_KV), 1)
    lo = lane < HEAD_DIM

    def bands(prev, cur):
        b = jnp.concatenate([prev, cur], axis=0)
        br = pltpu.roll(b, HEAD_DIM, 1)
        zero = jnp.zeros_like(b)
        return ((jnp.where(lo, b, zero).astype(BF16), jnp.where(lo, zero, br).astype(BF16)),
                (jnp.where(lo, br, zero).astype(BF16), jnp.where(lo, zero, b).astype(BF16)))

    ks = bands(kvp_ref[:, 0:D_KV], pj[:, OFF_K:OFF_K + D_KV])
    vs = bands(kvp_ref[:, D_KV:2 * D_KV], pj[:, OFF_V:OFF_V + D_KV])
    return ks, vs


STACK = PAIRS_PER_KV * BLOCK


def _head(j, i, e):
    return 2 * (PAIRS_PER_KV * j + i) + e


def _pair_cols(j, i, off):
    p = PAIRS_PER_KV * j + i
    return slice(off + 128 * p, off + 128 * (p + 1))


def _fill_attn_bias(bias_scr, first_block):
    qi = lax.broadcasted_iota(jnp.int32, (BLOCK, 2 * BLOCK), 0)
    kj = lax.broadcasted_iota(jnp.int32, (BLOCK, 2 * BLOCK), 1)
    dist = BLOCK + qi - kj
    valid = (dist >= 0) & (dist < BLOCK)
    if first_block:
        valid = valid & (kj >= BLOCK)
    distf = dist.astype(F32)
    for j in range(2):
        for e in range(2):
            for i in range(PAIRS_PER_KV):
                bias_scr[2 * j + e, BLOCK * i:BLOCK * (i + 1), :] = jnp.where(valid, -SLOPES[_head(j, i, e)] * distf, NEG)


def _q_stack(pj, j):
    return jnp.concatenate([(pj[:, _pair_cols(j, i, OFF_Q)] * SCALE).astype(BF16) for i in range(PAIRS_PER_KV)], axis=0)


def _attn_probs(q_stack, kband, bias_ref, sinks):
    s = lax.dot_general(q_stack, kband, _NT, preferred_element_type=F32)
    ones = jnp.ones((128, 128), BF16)
    probs, shares = [], []
    for i, sink in enumerate(sinks):
        rows = slice(BLOCK * i, BLOCK * (i + 1))
        t = s[rows, :] + bias_ref[rows, :]
        m = jnp.broadcast_to(jnp.max(t, axis=-1, keepdims=True), (BLOCK, 128))
        m = jnp.maximum(m, sink)
        p = [jnp.exp(t[:, :128] - m), jnp.exp(t[:, 128:] - m)]
        es = jnp.exp(sink - m)
        total = (jnp.dot(p[0].astype(BF16), ones, preferred_element_type=F32)
                 + jnp.dot(p[1].astype(BF16), ones, preferred_element_type=F32))
        inv = 1.0 / (total + es)
        probs.append(jnp.concatenate([p[0] * inv, p[1] * inv], axis=1))
        shares.append(es * inv)
    return jnp.concatenate(probs, axis=0), jnp.concatenate(shares, axis=0)


def _attn_group(pj, ks, vs, bias_scr, sink_ref, j):
    q_stack = _q_stack(pj, j)
    out, probs, shares = None, [], []
    for e in range(2):
        p, ps = _attn_probs(q_stack, ks[j][e], bias_scr.at[2 * j + e],
                            [sink_ref[_head(j, i, e)] for i in range(PAIRS_PER_KV)])
        p = p.astype(BF16)
        o = jnp.dot(p, vs[j][e], preferred_element_type=F32)
        out = o if out is None else out + o
        probs.append(p)
        shares.append(ps)
    return out, probs, shares


def _mix_fwd(proj, conv_full, sinks, norm_conv, norm_attn):
    def body(pj_ref, kvp_ref, cch_ref, cuh_ref, cw_ref, sink_ref, gc_ref, ga_ref,
             mixed_ref, attn_scr, p_ref, ps_ref, bias_scr):
        n = pl.program_id(0)
        pj = pj_ref

        @pl.when(n == 0)
        def _():
            _fill_attn_bias(bias_scr, first_block=True)

        @pl.when(n == 1)
        def _():
            _fill_attn_bias(bias_scr, first_block=False)

        zhalo = _conv_halo(cch_ref, cuh_ref, n)
        cw = (cw_ref[0:1, :], cw_ref[1:2, :], cw_ref[2:3, :])
        gain_c = gc_ref[...]

        for r in range(N_CHUNKS):
            rows = _chunk_rows(r)
            co = _conv_chunk(pj_ref, zhalo, cw, r)[-1]
            y = _gated_norm(pj_ref[rows, OFF_CB:OFF_CB + D_CONV] * co, gain_c, pj_ref[rows, OFF_GC:OFF_GC + D_CONV])
            mixed_ref[rows, 0:D_CONV] = y.astype(BF16)

        ks, vs = _kv_bands(pj, kvp_ref)
        for j in range(2):
            out, probs, shares = _attn_group(pj, ks, vs, bias_scr, sink_ref, j)
            for e in range(2):
                p_ref[0, 2 * j + e] = probs[e]
                ps_ref[0, 2 * j + e] = shares[e]
            for i in range(PAIRS_PER_KV):
                attn_scr[:, _pair_cols(j, i, 0)] = out[BLOCK * i:BLOCK * (i + 1), :]
        gain_a = ga_ref[...]

        for r in range(N_CHUNKS):
            rows = _chunk_rows(r)
            y = _gated_norm(attn_scr[rows, :], gain_a, pj_ref[rows, OFF_GA:OFF_GA + D_ATTN])
            mixed_ref[rows, D_CONV:D_MIX] = y.astype(BF16)

    per_block = BLOCK // HALO
    return pl.pallas_call(
        body, name="mix_fwd", grid=(N_BLOCKS,),
        in_specs=[
            pl.BlockSpec((BLOCK, D_PROJ), lambda n: (n, 0)),
            pl.BlockSpec((BLOCK, 2 * D_KV), lambda n: (jnp.maximum(n - 1, 0), OFF_K // (2 * D_KV))),
            pl.BlockSpec((HALO, D_CONV), lambda n: (jnp.maximum(n * per_block - 1, 0), OFF_CC // D_CONV)),
            pl.BlockSpec((HALO, D_CONV), lambda n: (jnp.maximum(n * per_block - 1, 0), OFF_CU // D_CONV)),
            pl.BlockSpec((8, D_CONV), lambda n: (0, 0)),
            pl.BlockSpec(memory_space=pltpu.SMEM),
            pl.BlockSpec((1, D_CONV), lambda n: (0, 0)),
            pl.BlockSpec((1, D_ATTN), lambda n: (0, 0)),
        ],
        out_specs=(pl.BlockSpec((BLOCK, D_MIX), lambda n: (n, 0)), pl.BlockSpec((BLOCK, D_ATTN), lambda n: (n, 0)),
                   pl.BlockSpec((1, 4, STACK, 2 * BLOCK), lambda n: (n, 0, 0, 0)),
                   pl.BlockSpec((1, 4, STACK, 128), lambda n: (n, 0, 0, 0))),
        out_shape=(jax.ShapeDtypeStruct((SEQ, D_MIX), BF16), jax.ShapeDtypeStruct((SEQ, D_ATTN), F32),
                   jax.ShapeDtypeStruct((N_BLOCKS, 4, STACK, 2 * BLOCK), BF16),
                   jax.ShapeDtypeStruct((N_BLOCKS, 4, STACK, 128), F32)),
        scratch_shapes=[pltpu.VMEM((4, STACK, 2 * BLOCK), F32)],
        compiler_params=_params(dimension_semantics=("arbitrary",)),
    )(proj, proj, proj, proj, conv_full, sinks, norm_conv, norm_attn)


def _out_proj_loss(mixed, x, target, w_out_full, norm_final):
    tm = 256

    def body(mx_ref, x_ref, t_ref, w_ref, g_ref, dx2_ref, dx2b_ref, dmix_ref, gnf_ref, loss_ref):
        i = pl.program_id(0)
        w = w_ref[...]
        x2 = x_ref[...] + jnp.dot(mx_ref[...], w, preferred_element_type=F32)
        r = lax.rsqrt(jnp.mean(x2 * x2, axis=-1, keepdims=True) + RMS_EPS)
        xn = x2 * r
        g = g_ref[...]
        err = xn * g - t_ref[...]
        part = 0.5 * jnp.sum(jnp.mean(err * err, axis=-1, keepdims=True), axis=0, keepdims=True)
        dy = err * (1.0 / D_MODEL)
        gnf = jnp.sum(dy * xn, axis=0, keepdims=True)
        u = dy * g
        dx2 = r * (u - xn * jnp.mean(u * xn, axis=-1, keepdims=True))
        dx2_ref[...] = dx2
        dx2b = dx2.astype(BF16)
        dx2b_ref[...] = dx2b
        dmix_ref[...] = lax.dot_general(dx2b, w, _NT, preferred_element_type=F32)

        @pl.when(i == 0)
        def _():
            gnf_ref[...] = jnp.zeros_like(gnf_ref)
            loss_ref[...] = jnp.zeros_like(loss_ref)

        gnf_ref[...] += gnf
        loss_ref[...] += jnp.broadcast_to(part, loss_ref.shape)

    return pl.pallas_call(
        body, name="out_proj_loss", grid=(SEQ // tm,),
        in_specs=[pl.BlockSpec((tm, D_MIX), lambda i: (i, 0)), pl.BlockSpec((tm, D_MODEL), lambda i: (i, 0)),
                  pl.BlockSpec((tm, D_MODEL), lambda i: (i, 0)), pl.BlockSpec(memory_space=pltpu.VMEM),
                  pl.BlockSpec((1, D_MODEL), lambda i: (0, 0))],
        out_specs=(pl.BlockSpec((tm, D_MODEL), lambda i: (i, 0)), pl.BlockSpec((tm, D_MODEL), lambda i: (i, 0)),
                   pl.BlockSpec((tm, D_MIX), lambda i: (i, 0)),
                   pl.BlockSpec((1, D_MODEL), lambda i: (0, 0)), pl.BlockSpec((8, 128), lambda i: (0, 0))),
        out_shape=(jax.ShapeDtypeStruct((SEQ, D_MODEL), F32), jax.ShapeDtypeStruct((SEQ, D_MODEL), BF16),
                   jax.ShapeDtypeStruct((SEQ, D_MIX), F32),
                   jax.ShapeDtypeStruct((1, D_MODEL), F32), jax.ShapeDtypeStruct((8, 128), F32)),
        compiler_params=_params(dimension_semantics=("arbitrary",)),
    )(mixed, x, target, w_out_full, norm_final)


def _gated_norm_bwd(a, gain, t, dy):
    r = lax.rsqrt(jnp.mean(a * a, axis=-1, keepdims=True) + RMS_EPS)
    an = a * r
    sg = _sigmoid(t)
    dn = dy * (t * sg)
    dt = dy * (an * gain) * (sg * (1.0 + t * (1.0 - sg)))
    u = dn * gain
    da = r * (u - an * jnp.mean(u * an, axis=-1, keepdims=True))
    return da, dt, dn * an


def _mix_bwd(proj, dmixed, attn, probs, shares, conv_full, norm_conv, norm_attn):
    def body(pj_ref, kvp_ref, cch_ref, cuh_ref, dmx_ref, attn_ref, p_ref, ps_ref, cw_ref, gc_ref, ga_ref,
             dpj_ref, gslab_ref, dattn_scr, nxt_scr, dkv_scr, acc_scr):
        step = pl.program_id(0)
        n = N_BLOCKS - 1 - step
        pj = pj_ref

        @pl.when(step == 0)
        def _():
            gslab_ref[...] = jnp.zeros_like(gslab_ref)
            nxt_scr[...] = jnp.zeros_like(nxt_scr)
            dkv_scr[...] = jnp.zeros_like(dkv_scr)
            acc_scr[...] = jnp.zeros_like(acc_scr)

        zhalo = _conv_halo(cch_ref, cuh_ref, n)
        cw = (cw_ref[0:1, :], cw_ref[1:2, :], cw_ref[2:3, :])
        gain_c = gc_ref[...]
        row = lax.broadcasted_iota(jnp.int32, (CHUNK, D_CONV), 0)

        dco_after = nxt_scr[...]
        for r in reversed(range(N_CHUNKS)):
            rows = _chunk_rows(r)
            cc, cu, z, z1, z2, co = _conv_chunk(pj_ref, zhalo, cw, r)
            cb = pj_ref[rows, OFF_CB:OFF_CB + D_CONV]
            da, dgate, gterm = _gated_norm_bwd(cb * co, gain_c, pj_ref[rows, OFF_GC:OFF_GC + D_CONV],
                                               dmx_ref[rows, 0:D_CONV])
            dpj_ref[rows, OFF_GC:OFF_GC + D_CONV] = dgate.astype(BF16)
            dpj_ref[rows, OFF_CB:OFF_CB + D_CONV] = (da * co).astype(BF16)
            dco = da * cb
            dco1 = jnp.where(row >= CHUNK - 1, pltpu.roll(dco_after, CHUNK - 1, 0), pltpu.roll(dco, CHUNK - 1, 0))
            dco2 = jnp.where(row >= CHUNK - 2, pltpu.roll(dco_after, CHUNK - 2, 0), pltpu.roll(dco, CHUNK - 2, 0))
            dz = cw[2] * dco + cw[1] * dco1 + cw[0] * dco2
            dpj_ref[rows, OFF_CC:OFF_CC + D_CONV] = (dz * cu).astype(BF16)
            dpj_ref[rows, OFF_CU:OFF_CU + D_CONV] = (dz * cc).astype(BF16)
            acc_scr[ACC_NORM_CONV] += gterm
            acc_scr[ACC_CONV0] += dco * z2
            acc_scr[ACC_CONV0 + 1] += dco * z1
            acc_scr[ACC_CONV0 + 2] += dco * z
            dco_after = dco
        nxt_scr[...] = dco_after

        ks, vs = _kv_bands(pj, kvp_ref)
        gain_a = ga_ref[...]

        for r in range(N_CHUNKS):
            rows = _chunk_rows(r)
            da, dgate, gterm = _gated_norm_bwd(attn_ref[rows, :], gain_a, pj_ref[rows, OFF_GA:OFF_GA + D_ATTN],
                                               dmx_ref[rows, D_CONV:D_MIX])
            dpj_ref[rows, OFF_GA:OFF_GA + D_ATTN] = dgate.astype(BF16)
            dattn_scr[rows, :] = da
            acc_scr[ACC_NORM_ATTN] += gterm

        in_lo = lax.broadcasted_iota(jnp.int32, (128, 128), 0) < HEAD_DIM
        half_ones = (jnp.where(in_lo, 1.0, 0.0).astype(BF16), jnp.where(in_lo, 0.0, 1.0).astype(BF16))
        lane_s = lax.broadcasted_iota(jnp.int32, (1, D_MODEL), 1)
        gsink = jnp.zeros((1, D_MODEL), F32)
        dk_t, dv_t = [], []
        for j in range(2):
            q_stack = _q_stack(pj, j)
            do_f = jnp.concatenate([dattn_scr[:, _pair_cols(j, i, 0)] for i in range(PAIRS_PER_KV)], axis=0)
            o_f = jnp.concatenate([attn_ref[:, _pair_cols(j, i, 0)] for i in range(PAIRS_PER_KV)], axis=0)
            prod = (do_f * o_f).astype(BF16)
            deltas = [jnp.dot(prod, half_ones[e], preferred_element_type=F32) for e in range(2)]
            do_b = do_f.astype(BF16)
            q_t, do_t = q_stack.T, do_b.T
            dq, dk_j, dv_j = None, None, None
            for e in range(2):
                p = p_ref[0, 2 * j + e]
                dp = lax.dot_general(do_b, vs[j][e], _NT, preferred_element_type=F32)
                ds = []
                for i in range(PAIRS_PER_KV):
                    rows = slice(BLOCK * i, BLOCK * (i + 1))
                    delta = deltas[e][rows, :]
                    ds.append((p[rows, :].astype(F32) * (dp[rows, :] - jnp.concatenate([delta, delta], axis=1))).astype(BF16))
                    gs_h = -jnp.sum(ps_ref[0, 2 * j + e, rows, 0:1] * delta[:, 0:1], axis=0, keepdims=True)
                    gsink = gsink + jnp.where(lane_s == _head(j, i, e), gs_h, 0.0)
                ds = jnp.concatenate(ds, axis=0)
                t = jnp.dot(ds, ks[j][e], preferred_element_type=F32)
                dq = t if dq is None else dq + t
                half = slice(HEAD_DIM * e, HEAD_DIM * (e + 1))
                a = jnp.dot(q_t[half, :], ds, preferred_element_type=F32)
                b = jnp.dot(do_t[half, :], p, preferred_element_type=F32)
                dk_j = a if dk_j is None else dk_j + a
                dv_j = b if dv_j is None else dv_j + b
            for i in range(PAIRS_PER_KV):
                dpj_ref[:, _pair_cols(j, i, OFF_Q)] = (dq[BLOCK * i:BLOCK * (i + 1), :] * SCALE).astype(BF16)
            dk_t.append(dk_j)
            dv_t.append(dv_j)
        dk = jnp.concatenate(dk_t, axis=0).T
        dv = jnp.concatenate(dv_t, axis=0).T
        dpj_ref[:, OFF_K:OFF_K + D_KV] = (dk[BLOCK:, :] + dkv_scr[:, 0:D_KV]).astype(BF16)
        dpj_ref[:, OFF_V:OFF_V + D_KV] = (dv[BLOCK:, :] + dkv_scr[:, D_KV:2 * D_KV]).astype(BF16)
        dkv_scr[:, 0:D_KV] = dk[:BLOCK, :]
        dkv_scr[:, D_KV:2 * D_KV] = dv[:BLOCK, :]
        gslab_ref[ROW_SINKS:ROW_SINKS + 1, :] += gsink

        @pl.when(step == N_BLOCKS - 1)
        def _():
            for k, slab_row in ((ACC_NORM_CONV, ROW_NORM_CONV), (ACC_NORM_ATTN, ROW_NORM_ATTN), (ACC_CONV0, ROW_CONV0),
                                (ACC_CONV0 + 1, ROW_CONV0 + 1), (ACC_CONV0 + 2, ROW_CONV0 + 2)):
                gslab_ref[slab_row:slab_row + 1, :] = jnp.sum(acc_scr[k], axis=0, keepdims=True)

    per_block = BLOCK // HALO
    last = N_BLOCKS - 1
    return pl.pallas_call(
        body, name="mix_bwd", grid=(N_BLOCKS,),
        in_specs=[
            pl.BlockSpec((BLOCK, D_PROJ), lambda s: (last - s, 0)),
            pl.BlockSpec((BLOCK, 2 * D_KV), lambda s: (jnp.maximum(last - s - 1, 0), OFF_K // (2 * D_KV))),
            pl.BlockSpec((HALO, D_CONV), lambda s: (jnp.maximum((last - s) * per_block - 1, 0), OFF_CC // D_CONV)),
            pl.BlockSpec((HALO, D_CONV), lambda s: (jnp.maximum((last - s) * per_block - 1, 0), OFF_CU // D_CONV)),
            pl.BlockSpec((BLOCK, D_MIX), lambda s: (last - s, 0)),
            pl.BlockSpec((BLOCK, D_ATTN), lambda s: (last - s, 0)),
            pl.BlockSpec((1, 4, STACK, 2 * BLOCK), lambda s: (last - s, 0, 0, 0)),
            pl.BlockSpec((1, 4, STACK, 128), lambda s: (last - s, 0, 0, 0)),
            pl.BlockSpec((8, D_CONV), lambda s: (0, 0)),
            pl.BlockSpec((1, D_CONV), lambda s: (0, 0)),
            pl.BlockSpec((1, D_ATTN), lambda s: (0, 0)),
        ],
        out_specs=(pl.BlockSpec((BLOCK, D_PROJ), lambda s: (last - s, 0)),
                   pl.BlockSpec((8, D_MODEL), lambda s: (0, 0))),
        out_shape=(jax.ShapeDtypeStruct((SEQ, D_PROJ), BF16), jax.ShapeDtypeStruct((8, D_MODEL), F32)),
        scratch_shapes=[pltpu.VMEM((BLOCK, D_ATTN), F32), pltpu.VMEM((CHUNK, D_CONV), F32),
                        pltpu.VMEM((BLOCK, 2 * D_KV), F32), pltpu.VMEM((N_ACC, CHUNK, D_MODEL), F32)],
        compiler_params=_params(dimension_semantics=("arbitrary",)),
    )(proj, proj, proj, proj, dmixed, attn, probs, shares, conv_full, norm_conv, norm_attn)


def _in_bwd_rs(dproj, w_full, x, dx2, norm_in, dw_in_chip, gslab, gnf, loss_part):
    tm = 256
    steps = SEQ // tm

    def body(dp_ref, w_hbm, x_ref, dx2_ref, g_ref, dwi_ref, gs_ref, gnf_ref, lp_ref, gx_ref, gwin_ref, gsum_ref,
             gni_scr, own, d2d, ici, myslab, slabs, w_ref, send_sems, recv_sems, local_sems):
        i = pl.program_id(0)
        rs_start, rs_finish = _ici_sum(dwi_ref, own, d2d, ici, send_sems, recv_sems, local_sems)
        slab_start, slab_finish = _slab_sum(myslab, slabs, send_sems, recv_sems, N_ICI_SUM_SEMS)

        @pl.when(i == 0)
        def _():
            gni_scr[...] = jnp.zeros_like(gni_scr)
            rs_start()
            w_load = pltpu.make_async_copy(w_hbm, w_ref, local_sems.at[1])
            w_load.start()
            w_load.wait()

        dh = jnp.dot(dp_ref[...], w_ref[...], preferred_element_type=F32)
        xv = x_ref[...]
        r = lax.rsqrt(jnp.mean(xv * xv, axis=-1, keepdims=True) + RMS_EPS)
        xn = xv * r
        u = dh * g_ref[...]
        gx_ref[...] = dx2_ref[...] + r * (u - xn * jnp.mean(u * xn, axis=-1, keepdims=True))
        gni_scr[...] += jnp.sum(dh * xn, axis=0, keepdims=True)

        @pl.when(i == steps - 1)
        def _():
            row = lax.broadcasted_iota(jnp.int32, (8, D_MODEL), 0)
            lane = lax.broadcasted_iota(jnp.int32, (8, D_MODEL), 1)
            slab = jnp.where(row == ROW_NORM_IN, gni_scr[...], jnp.where(row == ROW_NORM_FINAL, gnf_ref[...], gs_ref[...]))
            myslab[...] = jnp.where((row == ROW_SINKS) & (lane == LOSS_LANE), lp_ref[0:1, 0:1], slab)
            slab_start()
            gwin_ref[...] = rs_finish()
            gsum_ref[...] = slab_finish()

    const = lambda i: (0, 0)
    return pl.pallas_call(
        body, name="in_bwd", grid=(steps,),
        in_specs=[pl.BlockSpec((tm, D_PROJ), lambda i: (i, 0)), pl.BlockSpec(memory_space=pl.ANY),
                  pl.BlockSpec((tm, D_MODEL), lambda i: (i, 0)), pl.BlockSpec((tm, D_MODEL), lambda i: (i, 0)),
                  pl.BlockSpec((1, D_MODEL), const), pl.BlockSpec(memory_space=pl.ANY),
                  pl.BlockSpec((8, D_MODEL), const), pl.BlockSpec((1, D_MODEL), const), pl.BlockSpec((8, 128), const)],
        out_specs=(pl.BlockSpec((tm, D_MODEL), lambda i: (i, 0)), pl.BlockSpec((SHARD_IN, D_MODEL), const),
                   pl.BlockSpec((8, D_MODEL), const)),
        out_shape=(jax.ShapeDtypeStruct((SEQ, D_MODEL), F32), jax.ShapeDtypeStruct((SHARD_IN, D_MODEL), F32),
                   jax.ShapeDtypeStruct((8, D_MODEL), F32)),
        scratch_shapes=[pltpu.VMEM((1, D_MODEL), F32), pltpu.VMEM((SHARD_IN, D_MODEL), BF16),
                        pltpu.VMEM((SHARD_IN, D_MODEL), BF16), pltpu.VMEM((2, SHARD_IN, D_MODEL), BF16),
                        pltpu.VMEM((8, D_MODEL), F32), pltpu.VMEM((N_DEV, 8, D_MODEL), F32),
                        pltpu.VMEM((D_PROJ, D_MODEL), BF16),
                        pltpu.SemaphoreType.DMA((N_ICI_SUM_SEMS + 7,)), pltpu.SemaphoreType.DMA((N_ICI_SUM_SEMS + 7,)),
                        pltpu.SemaphoreType.DMA((2,))],
        compiler_params=_params(dimension_semantics=("arbitrary",)),
    )(dproj, w_full, x, dx2, norm_in, dw_in_chip, gslab, gnf, loss_part)


def _dw_rs(mixed, dx2b, dproj, h, table):
    tn_out, tn = 2 * SHARD_OUT, IN_PROJ_TILE
    out_steps, in_steps = D_MIX // tn_out, D_PROJ // tn
    steps = out_steps + in_steps
    out_order = (DG, NX, NY, OWN)

    def out_tile(i):
        chip = 2 * lax.axis_index("x") + lax.axis_index("y")
        return jnp.bitwise_xor(chip, (out_steps - 1) - jnp.minimum(i, out_steps - 1))

    def in_tile(table_ref, i):
        return _dw_entry(table_ref, jnp.maximum(i - out_steps, 0))

    def body(table_ref, mx_ref, dxb_ref, a_ref, b_ref, chip_ref, gwo_ref, dwo, dwt, d2d_in, via, own, d2d, ici,
             send_sems, recv_sems, local_sems):
        i = pl.program_id(0)
        rs_start, rs_forward, rs_finish = _shard_sum(dwo, own, d2d, ici, send_sems, recv_sems, local_sems)
        before_tile, after_tiles, chip_finish = _chip_sum(
            dwt, d2d_in, via, chip_ref, lambda k: _dw_entry(table_ref, in_steps + k), send_sems, recv_sems, local_sems,
            N_SHARD_SUM_SEMS, 4)

        for j, k in enumerate(out_order):
            @pl.when(i == j + 1)
            def _():
                rs_start(k)

            if k != OWN:
                @pl.when(i == j + 2)
                def _():
                    rs_forward(k)

        @pl.when(i < out_steps)
        def _():
            tile = lax.dot_general(mx_ref[...], dxb_ref[...], _TN, preferred_element_type=F32).astype(BF16)
            for core in range(2):
                dwo[2 * out_tile(i) + core] = tile[SHARD_OUT * core:SHARD_OUT * (core + 1), :]

        @pl.when(i >= out_steps)
        def _():
            before_tile(i - out_steps)
            tile = lax.dot_general(a_ref[...], b_ref[...], _TN, preferred_element_type=F32).astype(BF16)
            dwt[pl.ds(pl.multiple_of(in_tile(table_ref, i) * tn, tn), tn), :] = tile

        @pl.when(i == steps - 1)
        def _():
            after_tiles()
            gwo_ref[...] = rs_finish()
            chip_finish()

    vmem = pl.BlockSpec(memory_space=pltpu.VMEM)
    grid_spec = pltpu.PrefetchScalarGridSpec(
        num_scalar_prefetch=1, grid=(steps,),
        in_specs=[pl.BlockSpec((SEQ, tn_out), lambda i, table_ref: (0, out_tile(i))), vmem,
                  pl.BlockSpec((SEQ, tn), lambda i, table_ref: (0, in_tile(table_ref, i))), vmem],
        out_specs=(pl.BlockSpec(memory_space=pl.ANY), pl.BlockSpec((SHARD_OUT, D_MODEL), lambda i, table_ref: (0, 0))),
        scratch_shapes=[pltpu.VMEM((N_DEV, SHARD_OUT, D_MODEL), BF16),
                        pltpu.VMEM((D_PROJ, D_MODEL), BF16), pltpu.VMEM((3, SHARD_IN, D_MODEL), BF16),
                        pltpu.VMEM((2, HALF_IN, D_MODEL), BF16),
                        *_shard_sum_scratch(SHARD_OUT),
                        pltpu.SemaphoreType.DMA((N_SHARD_SUM_SEMS + N_CHIP_SUM_SEMS,)),
                        pltpu.SemaphoreType.DMA((N_SHARD_SUM_SEMS + N_CHIP_SUM_SEMS,)),
                        pltpu.SemaphoreType.DMA((8,))])
    return pl.pallas_call(
        body, name="dw", grid_spec=grid_spec,
        out_shape=(jax.ShapeDtypeStruct((4, SHARD_IN, D_MODEL), BF16), jax.ShapeDtypeStruct((SHARD_OUT, D_MODEL), F32)),
        compiler_params=_params(dimension_semantics=("arbitrary",)),
    )(table, mixed, dx2b, dproj, h)


def _adam_all(big_in, big_out, gsum, small, grad_x):
    n_chunks = 4
    n_big = 8

    def body(*refs):
        ins, outs = refs[:n_big + 1 + 18 + 1], refs[n_big + 1 + 18 + 1:n_big + 1 + 18 + 1 + 34]
        in_bufs, out_bufs, gx_buf = refs[-n_big - 6 - 4:-6 - 4], refs[-6 - 4:-4], refs[-4]
        in_sems, out_sems, gx_sems = refs[-3:]

        def gx_rows(j):
            return pl.ds(j * (SEQ // n_chunks), SEQ // n_chunks)

        def gx_load(j):
            return pltpu.make_async_copy(ins[27].at[gx_rows(j), :], gx_buf.at[gx_rows(j), :], gx_sems.at[j])

        def gx_store(j):
            return pltpu.make_async_copy(gx_buf.at[gx_rows(j), :], outs[33].at[gx_rows(j), :], gx_sems.at[n_chunks + j])

        def rows(a, j):
            tr = ins[a].shape[0] // n_chunks
            return pl.ds(j * tr, tr)

        def load(a, j):
            return pltpu.make_async_copy(ins[a].at[rows(a, j), :], in_bufs[a].at[rows(a, j), :], in_sems.at[a * n_chunks + j])

        def store(a, j):
            b, kind = divmod(a, 4)
            src = in_bufs[4 * b + 1] if kind == 0 else out_bufs[3 * b + kind - 1]
            return pltpu.make_async_copy(src.at[rows(a, j), :], outs[a].at[rows(a, j), :], out_sems.at[a * n_chunks + j])

        for j in range(n_chunks):
            for a in range(n_big):
                load(a, j).start()
            gx_load(j).start()

        def small_weights():
            gsum = ins[8][...]
            idx = _slot(lax.axis_index("x"), lax.axis_index("y"), lax.axis_index("c"))
            cg = jnp.zeros((3, SHARD_CONV), F32)
            for d in range(N_DEV):
                cg = jnp.where(idx == d, gsum[ROW_CONV0:ROW_CONV0 + 3, d * SHARD_CONV:(d + 1) * SHARD_CONV], cg)
            grads = (gsum[ROW_NORM_IN:ROW_NORM_IN + 1], gsum[ROW_SINKS:ROW_SINKS + 1, 0:N_Q_HEADS],
                     gsum[ROW_NORM_CONV:ROW_NORM_CONV + 1], gsum[ROW_NORM_ATTN:ROW_NORM_ATTN + 1],
                     gsum[ROW_NORM_FINAL:ROW_NORM_FINAL + 1], cg)
            for s, g in enumerate(grads):
                at = (slice(None), 0, slice(None)) if s == 5 else (slice(None), slice(None))
                w_ref, m_ref, v_ref = ins[9 + 3 * s:12 + 3 * s]
                delta, mn, vn = _adamw(w_ref[at], g, m_ref[at], v_ref[at])
                for ref, val in zip(outs[8 + 4 * s:12 + 4 * s], (g, delta, mn, vn)):
                    ref[at] = val
            outs[32][...] = gsum[ROW_SINKS:ROW_SINKS + 1, LOSS_LANE:LOSS_LANE + 1]

        small_weights()
        for j in range(n_chunks):
            for b in range(2):
                for a in range(4 * b, 4 * b + 4):
                    load(a, j).wait()
                w_buf, g_buf, m_buf, v_buf = in_bufs[4 * b:4 * b + 4]
                r = rows(4 * b, j)
                results = _adamw(w_buf[r, :], g_buf[r, :], m_buf[r, :], v_buf[r, :])
                for buf, val in zip(out_bufs[3 * b:3 * b + 3], results):
                    buf[r, :] = val
                for a in range(4 * b, 4 * b + 4):
                    store(a, j).start()
            gx_load(j).wait()
            gx_store(j).start()
        for j in range(n_chunks):
            for a in range(n_big):
                store(a, j).wait()
            gx_store(j).wait()

    vmem, hbm = pl.BlockSpec(memory_space=pltpu.VMEM), pl.BlockSpec(memory_space=pl.ANY)
    small_shapes = [a.shape for a in small[::3]]
    big_shapes = [(SHARD_IN, D_MODEL)] * 4 + [(SHARD_OUT, D_MODEL)] * 4
    out_shape = ([jax.ShapeDtypeStruct(s, F32) for s in big_shapes]
                 + [jax.ShapeDtypeStruct(s, F32) for s in small_shapes for _ in range(4)]
                 + [jax.ShapeDtypeStruct((1, 1), F32), jax.ShapeDtypeStruct((SEQ, D_MODEL), F32)])
    outs = pl.pallas_call(
        body, name="adam", in_specs=[hbm] * n_big + [vmem] * (1 + len(small)) + [hbm],
        out_specs=tuple([hbm] * n_big + [vmem] * (4 * len(small_shapes) + 1) + [hbm]), out_shape=tuple(out_shape),
        scratch_shapes=[pltpu.VMEM(s, F32) for s in big_shapes]
                       + [pltpu.VMEM(s, F32) for s in [(SHARD_IN, D_MODEL)] * 3 + [(SHARD_OUT, D_MODEL)] * 3]
                       + [pltpu.VMEM((SEQ, D_MODEL), F32),
                          pltpu.SemaphoreType.DMA((n_big * n_chunks,)), pltpu.SemaphoreType.DMA((n_big * n_chunks,)),
                          pltpu.SemaphoreType.DMA((2 * n_chunks,))],
        compiler_params=_params(),
    )(*big_in, *big_out, gsum, *small, grad_x)
    return outs[0:4], outs[4:8], [outs[8 + 4 * s:12 + 4 * s] for s in range(6)], outs[32], outs[33]


def _rows_first(a):
    return jnp.transpose(a, (1, 0, 2))


def kernel(x, norm_in, w_in, conv_w, attn_sinks, norm_conv_out, norm_attn_out, w_out, norm_final, loss_target, m_norm_in, m_w_in, m_conv_w, m_attn_sinks, m_norm_conv_out, m_norm_attn_out, m_w_out, m_norm_final, v_norm_in, v_w_in, v_conv_w, v_attn_sinks, v_norm_conv_out, v_norm_attn_out, v_w_out, v_norm_final):
    x2d = x.reshape(SEQ, D_MODEL)
    target = loss_target.reshape(SEQ, D_MODEL)
    nf = norm_final.reshape(1, D_MODEL)

    w_in_t, m_w_in_t, v_w_in_t = w_in[0].T, m_w_in[0].T, v_w_in[0].T
    tiles = jnp.asarray(TILE_ORDER, jnp.int32).reshape(-1)
    w_in_full, h, proj, g_out, conv_full = _gather_in_proj(x2d, norm_in, w_in_t, w_out[0], _rows_first(conv_w), tiles)
    sinks = attn_sinks.reshape(N_Q_HEADS)

    mixed, attn, probs, shares = _mix_fwd(proj, conv_full, sinks, norm_conv_out, norm_attn_out)
    dx2, dx2b, dmixed, gnf, loss_part = _out_proj_loss(mixed, x2d, target, g_out.reshape(D_MIX, D_MODEL), nf)
    dproj, gslab = _mix_bwd(proj, dmixed, attn, probs, shares, conv_full, norm_conv_out, norm_attn_out)
    dw_in_chip, g_w_out = _dw_rs(mixed, dx2b, dproj, h, jnp.asarray(DW_TABLE, jnp.int32).reshape(-1))
    grad_x, g_w_in, gsum = _in_bwd_rs(dproj, w_in_full, x2d, dx2, norm_in, dw_in_chip, gslab, gnf, loss_part)

    small = (norm_in, m_norm_in, v_norm_in, attn_sinks, m_attn_sinks, v_attn_sinks,
             norm_conv_out, m_norm_conv_out, v_norm_conv_out, norm_attn_out, m_norm_attn_out, v_norm_attn_out,
             nf, m_norm_final.reshape(1, D_MODEL), v_norm_final.reshape(1, D_MODEL),
             _rows_first(conv_w), _rows_first(m_conv_w), _rows_first(v_conv_w))
    big_in, big_out, (s_ni, s_sk, s_nc, s_na, s_nf, s_cv), loss, grad_x = _adam_all(
        (w_in_t, g_w_in, m_w_in_t, v_w_in_t), (w_out[0], g_w_out, m_w_out[0], v_w_out[0]), gsum, small, grad_x)

    def leaves(k):
        return (s_ni[k], big_in[k].T[None], jnp.transpose(s_cv[k], (1, 0, 2)), s_sk[k], s_nc[k], s_na[k], big_out[k][None],
                s_nf[k].reshape(D_MODEL))

    return (loss.reshape(()), grad_x.reshape(1, SEQ, D_MODEL), *leaves(0), *leaves(1), *leaves(2), *leaves(3))
```

```python
import jax
import jax.numpy as jnp
from jax import lax
from jax.experimental import pallas as pl
from jax.experimental.pallas import tpu as pltpu

F32 = jnp.float32
BF16 = jnp.bfloat16
MESH = pl.DeviceIdType.MESH

N_DEV = 8
SEQ = 2048
D_MODEL = 1024
D_CONV = 1024
D_ATTN = 1024
D_KV = 128
HEAD_DIM = 64
N_Q_HEADS = 16
N_PAIRS = N_Q_HEADS // 2
PAIRS_PER_KV = N_PAIRS // 2
D_MIX = D_CONV + D_ATTN
D_PROJ = 6400
SHARD_IN = D_PROJ // N_DEV
SHARD_OUT = D_MIX // N_DEV
SHARD_CONV = D_CONV // N_DEV
OFF_CB, OFF_CC, OFF_CU, OFF_GC, OFF_Q, OFF_K, OFF_V, OFF_GA = 0, 1024, 2048, 3072, 4096, 5120, 5248, 5376
BLOCK = 128
N_BLOCKS = SEQ // BLOCK
HALO = 8
CHUNK = 16
N_CHUNKS = BLOCK // CHUNK
RMS_EPS = 1e-5
NEG = -1e30
SCALE = HEAD_DIM ** -0.5
SLOPES = tuple(2.0 ** (-8.0 * (h + 1) / N_Q_HEADS) for h in range(N_Q_HEADS))

ADAM_LR = 0.001
ADAM_B1 = 0.9
ADAM_B2 = 0.999
ADAM_EPS = 1e-08
ADAM_WD = 0.01
ADAM_STEP = 10

ROW_NORM_IN, ROW_NORM_CONV, ROW_NORM_ATTN, ROW_NORM_FINAL, ROW_CONV0, ROW_SINKS = 0, 1, 2, 3, 4, 7
LOSS_LANE = N_Q_HEADS
ACC_NORM_CONV, ACC_NORM_ATTN, ACC_CONV0, N_ACC = 0, 1, 2, 5

VMEM_LIMIT = 56 * 1024 * 1024

_NT = (((1,), (1,)), ((), ()))
_TN = (((0,), (0,)), ((), ()))


def _params(**kw):
    return pltpu.CompilerParams(vmem_limit_bytes=VMEM_LIMIT, **kw)


def _adamw(w, g, m, v):
    m = ADAM_B1 * m + (1.0 - ADAM_B1) * g
    v = ADAM_B2 * v + (1.0 - ADAM_B2) * (g * g)
    m_hat = m / (1.0 - ADAM_B1 ** ADAM_STEP)
    v_hat = v / (1.0 - ADAM_B2 ** ADAM_STEP)
    delta = -ADAM_LR * (m_hat / (jnp.sqrt(v_hat) + ADAM_EPS) + ADAM_WD * w)
    return delta, m, v


def _sigmoid(t):
    return 1.0 / (1.0 + jnp.exp(-t))


def _slot(px, py, pc):
    return 4 * px + 2 * py + pc


OWN, NX, NY, DG = range(4)
HALF_IN = SHARD_IN // 2
N_GATHER_KINDS = 13
W_OUT_KINDS = N_GATHER_KINDS + 7


IN_PROJ_TILE = 640
TILE_ORDER = ((0, 1, 2, 3, 4, 5, 6, 7, 8, 9), (3, 4, 0, 1, 2, 8, 9, 5, 6, 7),
              (5, 6, 0, 1, 7, 8, 9, 2, 3, 4), (8, 9, 3, 4, 5, 6, 7, 0, 1, 2))
TILES_OWN, TILES_NEIGHBOURS = 2, 7


def _tile(table_ref, p):
    chip = 2 * lax.axis_index("x") + lax.axis_index("y")
    return table_ref[chip * len(TILE_ORDER[0]) + p]


DW_TILE_ORDER = tuple(tuple(reversed(row)) for row in TILE_ORDER)


def _tiles_until_complete(chip, owner):
    lo, hi = owner * 2 * SHARD_IN, (owner + 1) * 2 * SHARD_IN
    touching = [t for t in range(len(TILE_ORDER[0])) if t * IN_PROJ_TILE < hi and (t + 1) * IN_PROJ_TILE > lo]
    return 1 + max(DW_TILE_ORDER[chip].index(t) for t in touching)


DW_TABLE = tuple(DW_TILE_ORDER[chip] + tuple(_tiles_until_complete(chip, chip ^ flip) for flip in (0, 2, 1, 3))
                 for chip in range(4))


def _dw_entry(table_ref, p):
    chip = 2 * lax.axis_index("x") + lax.axis_index("y")
    return table_ref[chip * len(DW_TABLE[0]) + p]


def _gather_in_proj(x, norm_in, w_in_sh, w_out_sh, conv_sh, tiles):
    tn = IN_PROJ_TILE
    steps = D_PROJ // tn
    tm = 256

    def body(tiles_ref, x_hbm, g_ref, win_ref, wout_ref, cv_ref, wt_ref, h_ref, proj_ref, gout_ref, conv_ref,
             gin_ref, gcv_ref, wob_ref, x_ref, send_sems, recv_sems, local_sems):
        p = pl.program_id(0)
        local_sem = local_sems.at[0]
        x, y, c = lax.axis_index("x"), lax.axis_index("y"), lax.axis_index("c")
        me, sibling = (x, y, c), (x, y, 1 - c)
        nx, ny, dg = (1 - x, y, c), (x, 1 - y, c), (1 - x, 1 - y, c)

        def other(dev):
            return (dev[0], dev[1], 1 - dev[2])

        def shard(dev):
            return gin_ref.at[pl.ds(pl.multiple_of(_slot(*dev) * SHARD_IN, 16), SHARD_IN), :]

        def half(dev, h):
            return gin_ref.at[pl.ds(pl.multiple_of(_slot(*dev) * SHARD_IN + h * HALF_IN, 16), HALF_IN), :]

        def rc(ref, k, to):
            return pltpu.make_async_remote_copy(src_ref=ref, dst_ref=ref, send_sem=send_sems.at[k],
                                                recv_sem=recv_sems.at[k], device_id=to, device_id_type=MESH)

        def cv(k, dev, to):
            s = _slot(*dev)
            return pltpu.make_async_remote_copy(src_ref=gcv_ref.at[s], dst_ref=gcv_ref.at[s],
                                                send_sem=send_sems.at[N_GATHER_KINDS + k],
                                                recv_sem=recv_sems.at[N_GATHER_KINDS + k], device_id=to, device_id_type=MESH)

        def own_copies():
            return [rc(shard(me), 0, sibling),
                    rc(half(me, 0), 1, nx), rc(half(me, 1), 2, nx),
                    rc(half(me, 1), 4, ny), rc(half(me, 0), 3, ny),
                    cv(0, me, sibling)] + [cv(1 + j, me, peer) for j, peer in enumerate((nx, ny, dg))]

        def pass_on(dev, h, k_in, k_ici, k_d2d, half=half, base=0):
            rc(half(dev, h), base + k_in, me).wait_recv()
            if k_ici is not None:
                rc(half(dev, h), base + k_ici, ny if dev is nx else nx).start()
            rc(half(dev, h), base + k_d2d, sibling).start()

        def out_half(dev, h):
            return gout_ref.at[_slot(*dev), pl.ds(h * (SHARD_OUT // 2), SHARD_OUT // 2), :]

        def own_out_copies():
            src = lambda h: wob_ref.at[pl.ds(h * (SHARD_OUT // 2), SHARD_OUT // 2), :]

            def send(ref, dst, k, to):
                return pltpu.make_async_remote_copy(src_ref=ref, dst_ref=dst, send_sem=send_sems.at[W_OUT_KINDS + k],
                                                    recv_sem=recv_sems.at[W_OUT_KINDS + k], device_id=to, device_id_type=MESH)

            return [send(wob_ref, gout_ref.at[_slot(*me)], 0, sibling),
                    send(src(0), out_half(me, 0), 1, nx), send(src(1), out_half(me, 1), 2, nx),
                    send(src(1), out_half(me, 1), 4, ny), send(src(0), out_half(me, 0), 3, ny)]

        def own_out_local():
            return pltpu.make_async_copy(wob_ref, gout_ref.at[_slot(*me)], local_sems.at[1])

        @pl.when(p == 0)
        def _():
            gin_ref[pl.ds(pl.multiple_of(_slot(*me) * SHARD_IN, 16), SHARD_IN), :] = win_ref[...].astype(BF16)
            gcv_ref[_slot(*me)] = jnp.zeros((8, SHARD_CONV), F32)
            gcv_ref[_slot(*me), 0:3, :] = cv_ref[:, 0, :]
            for cp in own_copies():
                cp.start()
            wob_ref[...] = wout_ref[...].astype(BF16)
            x_load = pltpu.make_async_copy(x_hbm, x_ref, local_sems.at[2])
            x_load.start()
            x_load.wait()
            for t in range(SEQ // tm):
                xv = x_ref[tm * t:tm * (t + 1), :]
                r = lax.rsqrt(jnp.mean(xv * xv, axis=-1, keepdims=True) + RMS_EPS)
                h_ref[tm * t:tm * (t + 1), :] = (xv * r * g_ref[...]).astype(BF16)
            rc(shard(sibling), 0, me).wait_recv()

        @pl.when(p == TILES_OWN)
        def _():
            for args in ((nx, 0, 1, 5, 7), (ny, 1, 4, 6, 10), (nx, 1, 2, None, 8), (ny, 0, 3, None, 9)):
                pass_on(*args)
            for j, peer in enumerate((nx, ny, dg)):
                cv(1 + j, peer, me).wait_recv()
                cv(4 + j, peer, sibling).start()
            for (dev, h), k in (((nx, 0), 7), ((nx, 1), 8), ((ny, 0), 9), ((ny, 1), 10)):
                rc(half(other(dev), h), k, me).wait_recv()
            own_out_local().start()
            for cp in own_out_copies():
                cp.start()

        @pl.when(p == TILES_NEIGHBOURS - 1)
        def _():
            pass_on(dg, 0, 5, None, 11)
            pass_on(dg, 1, 6, None, 12)

        @pl.when(p == TILES_NEIGHBOURS)
        def _():
            for (dev, h), k in (((dg, 0), 11), ((dg, 1), 12)):
                rc(half(other(dev), h), k, me).wait_recv()
            pltpu.make_async_copy(gin_ref, wt_ref, local_sem).start()

        @pl.when(p == steps - 2)
        def _():
            for args in ((nx, 0, 1, 5, 7), (ny, 1, 4, 6, 10), (nx, 1, 2, None, 8), (ny, 0, 3, None, 9)):
                pass_on(*args, half=out_half, base=W_OUT_KINDS)

        w = gin_ref[pl.ds(pl.multiple_of(_tile(tiles_ref, p) * tn, tn), tn), :]
        proj_ref[...] = lax.dot_general(h_ref[...], w, _NT, preferred_element_type=F32)

        @pl.when(p == steps - 1)
        def _():
            cv(0, sibling, me).wait_recv()
            for j, peer in enumerate((nx, ny, dg)):
                cv(4 + j, other(peer), me).wait_recv()
            for d in range(N_DEV):
                conv_ref[:, d * SHARD_CONV:(d + 1) * SHARD_CONV] = gcv_ref[d]
            relayed = [rc(half(nx, 0), 5, ny), rc(half(ny, 1), 6, nx)]
            relayed += [rc(half(dev, h), k, sibling) for (dev, h), k in
                        (((nx, 0), 7), ((nx, 1), 8), ((ny, 0), 9), ((ny, 1), 10), ((dg, 0), 11), ((dg, 1), 12))]
            relayed += [cv(4 + j, peer, sibling) for j, peer in enumerate((nx, ny, dg))]
            for cp in own_copies() + relayed:
                cp.wait_send()
            pltpu.make_async_copy(gin_ref, wt_ref, local_sem).wait()
            pass_on(dg, 0, 5, None, 11, half=out_half, base=W_OUT_KINDS)
            pass_on(dg, 1, 6, None, 12, half=out_half, base=W_OUT_KINDS)
            rc(gout_ref.at[_slot(*sibling)], W_OUT_KINDS, me).wait_recv()
            out_relayed = [rc(out_half(nx, 0), W_OUT_KINDS + 5, ny), rc(out_half(ny, 1), W_OUT_KINDS + 6, nx)]
            for (dev, h), k in (((nx, 0), 7), ((nx, 1), 8), ((ny, 0), 9), ((ny, 1), 10), ((dg, 0), 11), ((dg, 1), 12)):
                rc(out_half(other(dev), h), W_OUT_KINDS + k, me).wait_recv()
                out_relayed.append(rc(out_half(dev, h), W_OUT_KINDS + k, sibling))
            for cp in own_out_copies() + out_relayed:
                cp.wait_send()
            own_out_local().wait()

    vmem = pl.BlockSpec(memory_space=pltpu.VMEM)
    grid_spec = pltpu.PrefetchScalarGridSpec(
        num_scalar_prefetch=1, grid=(steps,),
        in_specs=[pl.BlockSpec(memory_space=pl.ANY), vmem, vmem, vmem, vmem],
        out_specs=(pl.BlockSpec(memory_space=pl.ANY), vmem,
                   pl.BlockSpec((SEQ, tn), lambda p, tiles_ref: (0, _tile(tiles_ref, p))),
                   pl.BlockSpec(memory_space=pl.ANY), vmem),
        scratch_shapes=[pltpu.VMEM((D_PROJ, D_MODEL), BF16), pltpu.VMEM((N_DEV, 8, SHARD_CONV), F32),
                        pltpu.VMEM((SHARD_OUT, D_MODEL), BF16), pltpu.VMEM((SEQ, D_MODEL), F32),
                        pltpu.SemaphoreType.DMA((W_OUT_KINDS + N_GATHER_KINDS,)),
                        pltpu.SemaphoreType.DMA((W_OUT_KINDS + N_GATHER_KINDS,)),
                        pltpu.SemaphoreType.DMA((3,))])
    return pl.pallas_call(
        body, name="gather_in_proj", grid_spec=grid_spec,
        out_shape=(jax.ShapeDtypeStruct((D_PROJ, D_MODEL), BF16), jax.ShapeDtypeStruct((SEQ, D_MODEL), BF16),
                   jax.ShapeDtypeStruct((SEQ, D_PROJ), F32), jax.ShapeDtypeStruct((N_DEV, SHARD_OUT, D_MODEL), BF16),
                   jax.ShapeDtypeStruct((8, D_CONV), F32)),
        compiler_params=_params(dimension_semantics=("arbitrary",)),
    )(tiles, x, norm_in, w_in_sh, w_out_sh, conv_sh)


def _shard_sum(src, own, d2d, ici, send_sems, recv_sems, local_sems, base=0):
    x, y, c = lax.axis_index("x"), lax.axis_index("y"), lax.axis_index("c")
    sibling = (x, y, 1 - c)
    chips = [(x, y), (1 - x, y), (x, 1 - y), (1 - x, 1 - y)]

    def rcopy(s, d, k, to):
        return pltpu.make_async_remote_copy(src_ref=s, dst_ref=d, send_sem=send_sems.at[base + k],
                                            recv_sem=recv_sems.at[base + k], device_id=to, device_id_type=MESH)

    def mine(k):
        return pltpu.make_async_copy(src.at[_slot(*chips[k], c)], own.at[k], local_sems.at[k])

    def to_sibling(k):
        return rcopy(src.at[_slot(*chips[k], 1 - c)], d2d.at[k], k, sibling)

    def to_chip(k):
        return rcopy(own.at[k], ici.at[k - 1], 3 + k, (*chips[k], c))

    def start(k):
        mine(k).start()
        to_sibling(k).start()

    def forward(k):
        mine(k).wait()
        to_sibling(k).wait_recv()
        own[k] = (own[k].astype(F32) + d2d[k].astype(F32)).astype(BF16)
        to_chip(k).start()

    def finish():
        mine(0).wait()
        to_sibling(0).wait_recv()
        acc = own[0].astype(F32) + d2d[0].astype(F32)
        for k in range(1, 4):
            to_chip(k).wait_recv()
            acc = acc + ici[k - 1].astype(F32)
        for k in range(4):
            to_sibling(k).wait_send()
        for k in range(1, 4):
            to_chip(k).wait_send()
        return acc

    return start, forward, finish


def _shard_sum_scratch(rows):
    return [pltpu.VMEM((4, rows, D_MODEL), BF16), pltpu.VMEM((4, rows, D_MODEL), BF16),
            pltpu.VMEM((3, rows, D_MODEL), BF16)]


N_SHARD_SUM_SEMS = 7


N_CHIP_SUM_SEMS = 5


def _chip_sum(dwt, d2d, via, out_hbm, tiles_until, send_sems, recv_sems, local_sems, base, local_base):
    x, y, c = lax.axis_index("x"), lax.axis_index("y"), lax.axis_index("c")
    sibling, nx, ny = (x, y, 1 - c), (1 - x, y, c), (x, 1 - y, c)
    chips = [(x, y), (1 - x, y), (x, 1 - y), (1 - x, 1 - y)]

    def shard(s):
        return dwt.at[pl.ds(pl.multiple_of(s * SHARD_IN, 16), SHARD_IN), :]

    def half(ref, h):
        return ref.at[pl.ds(h * HALF_IN, HALF_IN), :]

    def rc(s, d, k, to):
        return pltpu.make_async_remote_copy(src_ref=s, dst_ref=d, send_sem=send_sems.at[base + k],
                                            recv_sem=recv_sems.at[base + k], device_id=to, device_id_type=MESH)

    def to_sibling(k):
        return rc(shard(_slot(*chips[k], 1 - c)), d2d.at[k - 1], k - 1, sibling)

    for_dg = (lambda: rc(half(d2d.at[DG - 1], 0), via.at[0], 3, nx), lambda: rc(half(d2d.at[DG - 1], 1), via.at[1], 4, ny))

    def save(k):
        return pltpu.make_async_copy(d2d.at[k - 1], out_hbm.at[k], local_sems.at[local_base + k])

    own_saves = (lambda: pltpu.make_async_copy(shard(_slot(x, y, c)), out_hbm.at[OWN], local_sems.at[local_base]),
                 lambda: pltpu.make_async_copy(shard(_slot(x, y, 1 - c)), out_hbm.at[3], local_sems.at[local_base + 3]))

    def before_tile(n):
        for k in (NX, NY, DG):
            @pl.when(tiles_until(k) == n)
            def _():
                to_sibling(k).start()

            @pl.when(tiles_until(k) + 1 == n)
            def _():
                to_sibling(k).wait_recv()
                d2d[k - 1] = (shard(_slot(*chips[k], c))[...].astype(F32) + d2d[k - 1].astype(F32)).astype(BF16)
                if k == DG:
                    for cp in for_dg:
                        cp().start()

    def after_tiles():
        for cp in own_saves:
            cp().start()

    def finish():
        for k, h in ((NY, 0), (NX, 1)):
            for_dg[h]().wait_recv()
            rows = pl.ds(h * HALF_IN, HALF_IN)
            d2d[k - 1, rows, :] = (d2d[k - 1, rows, :].astype(F32) + via[h].astype(F32)).astype(BF16)
            save(k).start()
        for cp in own_saves + (lambda: save(NX), lambda: save(NY)):
            cp().wait()
        for cp in (lambda: to_sibling(NX), lambda: to_sibling(NY), lambda: to_sibling(DG)) + for_dg:
            cp().wait_send()

    return before_tile, after_tiles, finish


N_ICI_SUM_SEMS = 3


def _ici_sum(src, own, d2d, ici, send_sems, recv_sems, local_sems, base=0):
    x, y, c = lax.axis_index("x"), lax.axis_index("y"), lax.axis_index("c")

    def rc(s, d, k, to):
        return pltpu.make_async_remote_copy(src_ref=s, dst_ref=d, send_sem=send_sems.at[base + k],
                                            recv_sem=recv_sems.at[base + k], device_id=to, device_id_type=MESH)

    copies = (lambda: rc(src.at[NX], ici.at[0], 0, (1 - x, y, c)), lambda: rc(src.at[NY], ici.at[1], 1, (x, 1 - y, c)),
              lambda: rc(src.at[3], d2d, 2, (x, y, 1 - c)))
    mine = lambda: pltpu.make_async_copy(src.at[OWN], own, local_sems.at[0])

    def start():
        for cp in copies + (mine,):
            cp().start()

    def finish():
        mine().wait()
        for cp in copies:
            cp().wait_recv()
        acc = own[...].astype(F32) + d2d[...].astype(F32) + ici[0].astype(F32) + ici[1].astype(F32)
        for cp in copies:
            cp().wait_send()
        return acc

    return start, finish


def _slab_sum(myslab, slabs, send_sems, recv_sems, base):
    x, y, c = lax.axis_index("x"), lax.axis_index("y"), lax.axis_index("c")
    me = _slot(x, y, c)
    peers = [(x, y, 1 - c), (1 - x, y, c), (x, 1 - y, c), (1 - x, 1 - y, c),
             (1 - x, y, 1 - c), (x, 1 - y, 1 - c), (1 - x, 1 - y, 1 - c)]

    def cp(k):
        return pltpu.make_async_remote_copy(src_ref=myslab, dst_ref=slabs.at[me], send_sem=send_sems.at[base + k],
                                            recv_sem=recv_sems.at[base + k], device_id=peers[k], device_id_type=MESH)

    def start():
        slabs[me] = myslab[...]
        for k in range(7):
            cp(k).start()

    def finish():
        for k in range(7):
            cp(k).wait_recv()
        total = slabs[0]
        for d in range(1, N_DEV):
            total = total + slabs[d]
        for k in range(7):
            cp(k).wait_send()
        return total

    return start, finish


def _chunk_rows(r):
    return slice(r * CHUNK, (r + 1) * CHUNK)


def _conv_halo(cch_ref, cuh_ref, n):
    zh = jnp.where(n > 0, cch_ref[...] * cuh_ref[...], 0.0)
    return jnp.concatenate([zh] * (CHUNK // HALO), axis=0)


def _conv_chunk(pj_ref, zhalo, cw, r):
    rows = _chunk_rows(r)
    cc = pj_ref[rows, OFF_CC:OFF_CC + D_CONV]
    cu = pj_ref[rows, OFF_CU:OFF_CU + D_CONV]
    z = cc * cu
    before = _chunk_rows(r - 1)
    zprev = pj_ref[before, OFF_CC:OFF_CC + D_CONV] * pj_ref[before, OFF_CU:OFF_CU + D_CONV] if r > 0 else zhalo
    row = lax.broadcasted_iota(jnp.int32, (CHUNK, D_CONV), 0)
    z1 = jnp.where(row < 1, pltpu.roll(zprev, 1, 0), pltpu.roll(z, 1, 0))
    z2 = jnp.where(row < 2, pltpu.roll(zprev, 2, 0), pltpu.roll(z, 2, 0))
    co = cw[0] * z2 + cw[1] * z1 + cw[2] * z
    return cc, cu, z, z1, z2, co


def _gated_norm(a, gain, t):
    r = lax.rsqrt(jnp.mean(a * a, axis=-1, keepdims=True) + RMS_EPS)
    return a * r * gain * (t * _sigmoid(t))


def _kv_bands(pj, kvp_ref):
    lane = lax.broadcasted_iota(jnp.int32, (2 * BLOCK, D_KV), 1)
    lo = lane < HEAD_DIM

    def bands(prev, cur):
        b = jnp.concatenate([prev, cur], axis=0)
        br = pltpu.roll(b, HEAD_DIM, 1)
        zero = jnp.zeros_like(b)
        return ((jnp.where(lo, b, zero).astype(BF16), jnp.where(lo, zero, br).astype(BF16)),
                (jnp.where(lo, br, zero).astype(BF16), jnp.where(lo, zero, b).astype(BF16)))

    ks = bands(kvp_ref[:, 0:D_KV], pj[:, OFF_K:OFF_K + D_KV])
    vs = bands(kvp_ref[:, D_KV:2 * D_KV], pj[:, OFF_V:OFF_V + D_KV])
    return ks, vs


STACK = PAIRS_PER_KV * BLOCK


def _head(j, i, e):
    return 2 * (PAIRS_PER_KV * j + i) + e


def _pair_cols(j, i, off):
    p = PAIRS_PER_KV * j + i
    return slice(off + 128 * p, off + 128 * (p + 1))


def _fill_attn_bias(bias_scr, first_block):
    qi = lax.broadcasted_iota(jnp.int32, (BLOCK, 2 * BLOCK), 0)
    kj = lax.broadcasted_iota(jnp.int32, (BLOCK, 2 * BLOCK), 1)
    dist = BLOCK + qi - kj
    valid = (dist >= 0) & (dist < BLOCK)
    if first_block:
        valid = valid & (kj >= BLOCK)
    distf = dist.astype(F32)
    for j in range(2):
        for e in range(2):
            for i in range(PAIRS_PER_KV):
                bias_scr[2 * j + e, BLOCK * i:BLOCK * (i + 1), :] = jnp.where(valid, -SLOPES[_head(j, i, e)] * distf, NEG)


def _q_stack(pj, j):
    return jnp.concatenate([(pj[:, _pair_cols(j, i, OFF_Q)] * SCALE).astype(BF16) for i in range(PAIRS_PER_KV)], axis=0)


def _attn_probs(q_stack, kband, bias_ref, sinks):
    s = lax.dot_general(q_stack, kband, _NT, preferred_element_type=F32)
    ones = jnp.ones((128, 128), BF16)
    probs, shares = [], []
    for i, sink in enumerate(sinks):
        rows = slice(BLOCK * i, BLOCK * (i + 1))
        t = s[rows, :] + bias_ref[rows, :]
        m = jnp.broadcast_to(jnp.max(t, axis=-1, keepdims=True), (BLOCK, 128))
        m = jnp.maximum(m, sink)
        p = [jnp.exp(t[:, :128] - m), jnp.exp(t[:, 128:] - m)]
        es = jnp.exp(sink - m)
        total = (jnp.dot(p[0].astype(BF16), ones, preferred_element_type=F32)
                 + jnp.dot(p[1].astype(BF16), ones, preferred_element_type=F32))
        inv = 1.0 / (total + es)
        probs.append(jnp.concatenate([p[0] * inv, p[1] * inv], axis=1))
        shares.append(es * inv)
    return jnp.concatenate(probs, axis=0), jnp.concatenate(shares, axis=0)


def _attn_group(pj, ks, vs, bias_scr, sink_ref, j):
    q_stack = _q_stack(pj, j)
    out, probs, shares = None, [], []
    for e in range(2):
        p, ps = _attn_probs(q_stack, ks[j][e], bias_scr.at[2 * j + e],
                            [sink_ref[_head(j, i, e)] for i in range(PAIRS_PER_KV)])
        p = p.astype(BF16)
        o = jnp.dot(p, vs[j][e], preferred_element_type=F32)
        out = o if out is None else out + o
        probs.append(p)
        shares.append(ps)
    return out, probs, shares


def _mix_fwd(proj, conv_full, sinks, norm_conv, norm_attn):
    def body(pj_ref, kvp_ref, cch_ref, cuh_ref, cw_ref, sink_ref, gc_ref, ga_ref,
             mixed_ref, attn_scr, p_ref, ps_ref, bias_scr):
        n = pl.program_id(0)
        pj = pj_ref

        @pl.when(n == 0)
        def _():
            _fill_attn_bias(bias_scr, first_block=True)

        @pl.when(n == 1)
        def _():
            _fill_attn_bias(bias_scr, first_block=False)

        zhalo = _conv_halo(cch_ref, cuh_ref, n)
        cw = (cw_ref[0:1, :], cw_ref[1:2, :], cw_ref[2:3, :])
        gain_c = gc_ref[...]

        for r in range(N_CHUNKS):
            rows = _chunk_rows(r)
            co = _conv_chunk(pj_ref, zhalo, cw, r)[-1]
            y = _gated_norm(pj_ref[rows, OFF_CB:OFF_CB + D_CONV] * co, gain_c, pj_ref[rows, OFF_GC:OFF_GC + D_CONV])
            mixed_ref[rows, 0:D_CONV] = y.astype(BF16)

        ks, vs = _kv_bands(pj, kvp_ref)
        for j in range(2):
            out, probs, shares = _attn_group(pj, ks, vs, bias_scr, sink_ref, j)
            for e in range(2):
                p_ref[0, 2 * j + e] = probs[e]
                ps_ref[0, 2 * j + e] = shares[e]
            for i in range(PAIRS_PER_KV):
                attn_scr[:, _pair_cols(j, i, 0)] = out[BLOCK * i:BLOCK * (i + 1), :]
        gain_a = ga_ref[...]

        for r in range(N_CHUNKS):
            rows = _chunk_rows(r)
            y = _gated_norm(attn_scr[rows, :], gain_a, pj_ref[rows, OFF_GA:OFF_GA + D_ATTN])
            mixed_ref[rows, D_CONV:D_MIX] = y.astype(BF16)

    per_block = BLOCK // HALO
    return pl.pallas_call(
        body, name="mix_fwd", grid=(N_BLOCKS,),
        in_specs=[
            pl.BlockSpec((BLOCK, D_PROJ), lambda n: (n, 0)),
            pl.BlockSpec((BLOCK, 2 * D_KV), lambda n: (jnp.maximum(n - 1, 0), OFF_K // (2 * D_KV))),
            pl.BlockSpec((HALO, D_CONV), lambda n: (jnp.maximum(n * per_block - 1, 0), OFF_CC // D_CONV)),
            pl.BlockSpec((HALO, D_CONV), lambda n: (jnp.maximum(n * per_block - 1, 0), OFF_CU // D_CONV)),
            pl.BlockSpec((8, D_CONV), lambda n: (0, 0)),
            pl.BlockSpec(memory_space=pltpu.SMEM),
            pl.BlockSpec((1, D_CONV), lambda n: (0, 0)),
            pl.BlockSpec((1, D_ATTN), lambda n: (0, 0)),
        ],
        out_specs=(pl.BlockSpec((BLOCK, D_MIX), lambda n: (n, 0)), pl.BlockSpec((BLOCK, D_ATTN), lambda n: (n, 0)),
                   pl.BlockSpec((1, 4, STACK, 2 * BLOCK), lambda n: (n, 0, 0, 0)),
                   pl.BlockSpec((1, 4, STACK, 128), lambda n: (n, 0, 0, 0))),
        out_shape=(jax.ShapeDtypeStruct((SEQ, D_MIX), BF16), jax.ShapeDtypeStruct((SEQ, D_ATTN), F32),
                   jax.ShapeDtypeStruct((N_BLOCKS, 4, STACK, 2 * BLOCK), BF16),
                   jax.ShapeDtypeStruct((N_BLOCKS, 4, STACK, 128), F32)),
        scratch_shapes=[pltpu.VMEM((4, STACK, 2 * BLOCK), F32)],
        compiler_params=_params(dimension_semantics=("arbitrary",)),
    )(proj, proj, proj, proj, conv_full, sinks, norm_conv, norm_attn)


def _out_proj_loss(mixed, x, target, w_out_full, norm_final):
    tm = 256

    def body(mx_ref, x_ref, t_ref, w_ref, g_ref, dx2_ref, dx2b_ref, dmix_ref, gnf_ref, loss_ref):
        i = pl.program_id(0)
        w = w_ref[...]
        x2 = x_ref[...] + jnp.dot(mx_ref[...], w, preferred_element_type=F32)
        r = lax.rsqrt(jnp.mean(x2 * x2, axis=-1, keepdims=True) + RMS_EPS)
        xn = x2 * r
        g = g_ref[...]
        err = xn * g - t_ref[...]
        part = 0.5 * jnp.sum(jnp.mean(err * err, axis=-1, keepdims=True), axis=0, keepdims=True)
        dy = err * (1.0 / D_MODEL)
        gnf = jnp.sum(dy * xn, axis=0, keepdims=True)
        u = dy * g
        dx2 = r * (u - xn * jnp.mean(u * xn, axis=-1, keepdims=True))
        dx2_ref[...] = dx2
        dx2b = dx2.astype(BF16)
        dx2b_ref[...] = dx2b
        dmix_ref[...] = lax.dot_general(dx2b, w, _NT, preferred_element_type=F32)

        @pl.when(i == 0)
        def _():
            gnf_ref[...] = jnp.zeros_like(gnf_ref)
            loss_ref[...] = jnp.zeros_like(loss_ref)

        gnf_ref[...] += gnf
        loss_ref[...] += jnp.broadcast_to(part, loss_ref.shape)

    return pl.pallas_call(
        body, name="out_proj_loss", grid=(SEQ // tm,),
        in_specs=[pl.BlockSpec((tm, D_MIX), lambda i: (i, 0)), pl.BlockSpec((tm, D_MODEL), lambda i: (i, 0)),
                  pl.BlockSpec((tm, D_MODEL), lambda i: (i, 0)), pl.BlockSpec(memory_space=pltpu.VMEM),
                  pl.BlockSpec((1, D_MODEL), lambda i: (0, 0))],
        out_specs=(pl.BlockSpec((tm, D_MODEL), lambda i: (i, 0)), pl.BlockSpec((tm, D_MODEL), lambda i: (i, 0)),
                   pl.BlockSpec((tm, D_MIX), lambda i: (i, 0)),
                   pl.BlockSpec((1, D_MODEL), lambda i: (0, 0)), pl.BlockSpec((8, 128), lambda i: (0, 0))),
        out_shape=(jax.ShapeDtypeStruct((SEQ, D_MODEL), F32), jax.ShapeDtypeStruct((SEQ, D_MODEL), BF16),
                   jax.ShapeDtypeStruct((SEQ, D_MIX), F32),
                   jax.ShapeDtypeStruct((1, D_MODEL), F32), jax.ShapeDtypeStruct((8, 128), F32)),
        compiler_params=_params(dimension_semantics=("arbitrary",)),
    )(mixed, x, target, w_out_full, norm_final)


def _gated_norm_bwd(a, gain, t, dy):
    r = lax.rsqrt(jnp.mean(a * a, axis=-1, keepdims=True) + RMS_EPS)
    an = a * r
    sg = _sigmoid(t)
    dn = dy * (t * sg)
    dt = dy * (an * gain) * (sg * (1.0 + t * (1.0 - sg)))
    u = dn * gain
    da = r * (u - an * jnp.mean(u * an, axis=-1, keepdims=True))
    return da, dt, dn * an


def _mix_bwd(proj, dmixed, attn, probs, shares, conv_full, norm_conv, norm_attn):
    def body(pj_ref, kvp_ref, cch_ref, cuh_ref, dmx_ref, attn_ref, p_ref, ps_ref, cw_ref, gc_ref, ga_ref,
             dpj_ref, gslab_ref, dattn_scr, nxt_scr, dkv_scr, acc_scr):
        step = pl.program_id(0)
        n = N_BLOCKS - 1 - step
        pj = pj_ref

        @pl.when(step == 0)
        def _():
            gslab_ref[...] = jnp.zeros_like(gslab_ref)
            nxt_scr[...] = jnp.zeros_like(nxt_scr)
            dkv_scr[...] = jnp.zeros_like(dkv_scr)
            acc_scr[...] = jnp.zeros_like(acc_scr)

        zhalo = _conv_halo(cch_ref, cuh_ref, n)
        cw = (cw_ref[0:1, :], cw_ref[1:2, :], cw_ref[2:3, :])
        gain_c = gc_ref[...]
        row = lax.broadcasted_iota(jnp.int32, (CHUNK, D_CONV), 0)

        dco_after = nxt_scr[...]
        for r in reversed(range(N_CHUNKS)):
            rows = _chunk_rows(r)
            cc, cu, z, z1, z2, co = _conv_chunk(pj_ref, zhalo, cw, r)
            cb = pj_ref[rows, OFF_CB:OFF_CB + D_CONV]
            da, dgate, gterm = _gated_norm_bwd(cb * co, gain_c, pj_ref[rows, OFF_GC:OFF_GC + D_CONV],
                                               dmx_ref[rows, 0:D_CONV])
            dpj_ref[rows, OFF_GC:OFF_GC + D_CONV] = dgate.astype(BF16)
            dpj_ref[rows, OFF_CB:OFF_CB + D_CONV] = (da * co).astype(BF16)
            dco = da * cb
            dco1 = jnp.where(row >= CHUNK - 1, pltpu.roll(dco_after, CHUNK - 1, 0), pltpu.roll(dco, CHUNK - 1, 0))
            dco2 = jnp.where(row >= CHUNK - 2, pltpu.roll(dco_after, CHUNK - 2, 0), pltpu.roll(dco, CHUNK - 2, 0))
            dz = cw[2] * dco + cw[1] * dco1 + cw[0] * dco2
            dpj_ref[rows, OFF_CC:OFF_CC + D_CONV] = (dz * cu).astype(BF16)
            dpj_ref[rows, OFF_CU:OFF_CU + D_CONV] = (dz * cc).astype(BF16)
            acc_scr[ACC_NORM_CONV] += gterm
            acc_scr[ACC_CONV0] += dco * z2
            acc_scr[ACC_CONV0 + 1] += dco * z1
            acc_scr[ACC_CONV0 + 2] += dco * z
            dco_after = dco
        nxt_scr[...] = dco_after

        ks, vs = _kv_bands(pj, kvp_ref)
        gain_a = ga_ref[...]

        for r in range(N_CHUNKS):
            rows = _chunk_rows(r)
            da, dgate, gterm = _gated_norm_bwd(attn_ref[rows, :], gain_a, pj_ref[rows, OFF_GA:OFF_GA + D_ATTN],
                                               dmx_ref[rows, D_CONV:D_MIX])
            dpj_ref[rows, OFF_GA:OFF_GA + D_ATTN] = dgate.astype(BF16)
            dattn_scr[rows, :] = da
            acc_scr[ACC_NORM_ATTN] += gterm

        in_lo = lax.broadcasted_iota(jnp.int32, (128, 128), 0) < HEAD_DIM
        half_ones = (jnp.where(in_lo, 1.0, 0.0).astype(BF16), jnp.where(in_lo, 0.0, 1.0).astype(BF16))
        lane_s = lax.broadcasted_iota(jnp.int32, (1, D_MODEL), 1)
        gsink = jnp.zeros((1, D_MODEL), F32)
        dk_t, dv_t = [], []
        for j in range(2):
            q_stack = _q_stack(pj, j)
            do_f = jnp.concatenate([dattn_scr[:, _pair_cols(j, i, 0)] for i in range(PAIRS_PER_KV)], axis=0)
            o_f = jnp.concatenate([attn_ref[:, _pair_cols(j, i, 0)] for i in range(PAIRS_PER_KV)], axis=0)
            prod = (do_f * o_f).astype(BF16)
            deltas = [jnp.dot(prod, half_ones[e], preferred_element_type=F32) for e in range(2)]
            do_b = do_f.astype(BF16)
            q_t, do_t = q_stack.T, do_b.T
            dq, dk_j, dv_j = None, None, None
            for e in range(2):
                p = p_ref[0, 2 * j + e]
                dp = lax.dot_general(do_b, vs[j][e], _NT, preferred_element_type=F32)
                ds = []
                for i in range(PAIRS_PER_KV):
                    rows = slice(BLOCK * i, BLOCK * (i + 1))
                    delta = deltas[e][rows, :]
                    ds.append((p[rows, :].astype(F32) * (dp[rows, :] - jnp.concatenate([delta, delta], axis=1))).astype(BF16))
                    gs_h = -jnp.sum(ps_ref[0, 2 * j + e, rows, 0:1] * delta[:, 0:1], axis=0, keepdims=True)
                    gsink = gsink + jnp.where(lane_s == _head(j, i, e), gs_h, 0.0)
                ds = jnp.concatenate(ds, axis=0)
                t = jnp.dot(ds, ks[j][e], preferred_element_type=F32)
                dq = t if dq is None else dq + t
                half = slice(HEAD_DIM * e, HEAD_DIM * (e + 1))
                a = jnp.dot(q_t[half, :], ds, preferred_element_type=F32)
                b = jnp.dot(do_t[half, :], p, preferred_element_type=F32)
                dk_j = a if dk_j is None else dk_j + a
                dv_j = b if dv_j is None else dv_j + b
            for i in range(PAIRS_PER_KV):
                dpj_ref[:, _pair_cols(j, i, OFF_Q)] = (dq[BLOCK * i:BLOCK * (i + 1), :] * SCALE).astype(BF16)
            dk_t.append(dk_j)
            dv_t.append(dv_j)
        dk = jnp.concatenate(dk_t, axis=0).T
        dv = jnp.concatenate(dv_t, axis=0).T
        dpj_ref[:, OFF_K:OFF_K + D_KV] = (dk[BLOCK:, :] + dkv_scr[:, 0:D_KV]).astype(BF16)
        dpj_ref[:, OFF_V:OFF_V + D_KV] = (dv[BLOCK:, :] + dkv_scr[:, D_KV:2 * D_KV]).astype(BF16)
        dkv_scr[:, 0:D_KV] = dk[:BLOCK, :]
        dkv_scr[:, D_KV:2 * D_KV] = dv[:BLOCK, :]
        gslab_ref[ROW_SINKS:ROW_SINKS + 1, :] += gsink

        @pl.when(step == N_BLOCKS - 1)
        def _():
            for k, slab_row in ((ACC_NORM_CONV, ROW_NORM_CONV), (ACC_NORM_ATTN, ROW_NORM_ATTN), (ACC_CONV0, ROW_CONV0),
                                (ACC_CONV0 + 1, ROW_CONV0 + 1), (ACC_CONV0 + 2, ROW_CONV0 + 2)):
                gslab_ref[slab_row:slab_row + 1, :] = jnp.sum(acc_scr[k], axis=0, keepdims=True)

    per_block = BLOCK // HALO
    last = N_BLOCKS - 1
    return pl.pallas_call(
        body, name="mix_bwd", grid=(N_BLOCKS,),
        in_specs=[
            pl.BlockSpec((BLOCK, D_PROJ), lambda s: (last - s, 0)),
            pl.BlockSpec((BLOCK, 2 * D_KV), lambda s: (jnp.maximum(last - s - 1, 0), OFF_K // (2 * D_KV))),
            pl.BlockSpec((HALO, D_CONV), lambda s: (jnp.maximum((last - s) * per_block - 1, 0), OFF_CC // D_CONV)),
            pl.BlockSpec((HALO, D_CONV), lambda s: (jnp.maximum((last - s) * per_block - 1, 0), OFF_CU // D_CONV)),
            pl.BlockSpec((BLOCK, D_MIX), lambda s: (last - s, 0)),
            pl.BlockSpec((BLOCK, D_ATTN), lambda s: (last - s, 0)),
            pl.BlockSpec((1, 4, STACK, 2 * BLOCK), lambda s: (last - s, 0, 0, 0)),
            pl.BlockSpec((1, 4, STACK, 128), lambda s: (last - s, 0, 0, 0)),
            pl.BlockSpec((8, D_CONV), lambda s: (0, 0)),
            pl.BlockSpec((1, D_CONV), lambda s: (0, 0)),
            pl.BlockSpec((1, D_ATTN), lambda s: (0, 0)),
        ],
        out_specs=(pl.BlockSpec((BLOCK, D_PROJ), lambda s: (last - s, 0)),
                   pl.BlockSpec((8, D_MODEL), lambda s: (0, 0))),
        out_shape=(jax.ShapeDtypeStruct((SEQ, D_PROJ), BF16), jax.ShapeDtypeStruct((8, D_MODEL), F32)),
        scratch_shapes=[pltpu.VMEM((BLOCK, D_ATTN), F32), pltpu.VMEM((CHUNK, D_CONV), F32),
                        pltpu.VMEM((BLOCK, 2 * D_KV), F32), pltpu.VMEM((N_ACC, CHUNK, D_MODEL), F32)],
        compiler_params=_params(dimension_semantics=("arbitrary",)),
    )(proj, proj, proj, proj, dmixed, attn, probs, shares, conv_full, norm_conv, norm_attn)


def _in_bwd_rs(dproj, w_full, x, dx2, norm_in, dw_in_chip, gslab, gnf, loss_part):
    tm = 256
    steps = SEQ // tm

    def body(dp_ref, w_hbm, x_ref, dx2_ref, g_ref, dwi_ref, gs_ref, gnf_ref, lp_ref, gx_ref, gwin_ref, gsum_ref,
             gni_scr, own, d2d, ici, myslab, slabs, w_ref, send_sems, recv_sems, local_sems):
        i = pl.program_id(0)
        rs_start, rs_finish = _ici_sum(dwi_ref, own, d2d, ici, send_sems, recv_sems, local_sems)
        slab_start, slab_finish = _slab_sum(myslab, slabs, send_sems, recv_sems, N_ICI_SUM_SEMS)

        @pl.when(i == 0)
        def _():
            gni_scr[...] = jnp.zeros_like(gni_scr)
            rs_start()
            w_load = pltpu.make_async_copy(w_hbm, w_ref, local_sems.at[1])
            w_load.start()
            w_load.wait()

        dh = jnp.dot(dp_ref[...], w_ref[...], preferred_element_type=F32)
        xv = x_ref[...]
        r = lax.rsqrt(jnp.mean(xv * xv, axis=-1, keepdims=True) + RMS_EPS)
        xn = xv * r
        u = dh * g_ref[...]
        gx_ref[...] = dx2_ref[...] + r * (u - xn * jnp.mean(u * xn, axis=-1, keepdims=True))
        gni_scr[...] += jnp.sum(dh * xn, axis=0, keepdims=True)

        @pl.when(i == steps - 1)
        def _():
            row = lax.broadcasted_iota(jnp.int32, (8, D_MODEL), 0)
            lane = lax.broadcasted_iota(jnp.int32, (8, D_MODEL), 1)
            slab = jnp.where(row == ROW_NORM_IN, gni_scr[...], jnp.where(row == ROW_NORM_FINAL, gnf_ref[...], gs_ref[...]))
            myslab[...] = jnp.where((row == ROW_SINKS) & (lane == LOSS_LANE), lp_ref[0:1, 0:1], slab)
            slab_start()
            gwin_ref[...] = rs_finish()
            gsum_ref[...] = slab_finish()

    const = lambda i: (0, 0)
    return pl.pallas_call(
        body, name="in_bwd", grid=(steps,),
        in_specs=[pl.BlockSpec((tm, D_PROJ), lambda i: (i, 0)), pl.BlockSpec(memory_space=pl.ANY),
                  pl.BlockSpec((tm, D_MODEL), lambda i: (i, 0)), pl.BlockSpec((tm, D_MODEL), lambda i: (i, 0)),
                  pl.BlockSpec((1, D_MODEL), const), pl.BlockSpec(memory_space=pl.ANY),
                  pl.BlockSpec((8, D_MODEL), const), pl.BlockSpec((1, D_MODEL), const), pl.BlockSpec((8, 128), const)],
        out_specs=(pl.BlockSpec((tm, D_MODEL), lambda i: (i, 0)), pl.BlockSpec((SHARD_IN, D_MODEL), const),
                   pl.BlockSpec((8, D_MODEL), const)),
        out_shape=(jax.ShapeDtypeStruct((SEQ, D_MODEL), F32), jax.ShapeDtypeStruct((SHARD_IN, D_MODEL), F32),
                   jax.ShapeDtypeStruct((8, D_MODEL), F32)),
        scratch_shapes=[pltpu.VMEM((1, D_MODEL), F32), pltpu.VMEM((SHARD_IN, D_MODEL), BF16),
                        pltpu.VMEM((SHARD_IN, D_MODEL), BF16), pltpu.VMEM((2, SHARD_IN, D_MODEL), BF16),
                        pltpu.VMEM((8, D_MODEL), F32), pltpu.VMEM((N_DEV, 8, D_MODEL), F32),
                        pltpu.VMEM((D_PROJ, D_MODEL), BF16),
                        pltpu.SemaphoreType.DMA((N_ICI_SUM_SEMS + 7,)), pltpu.SemaphoreType.DMA((N_ICI_SUM_SEMS + 7,)),
                        pltpu.SemaphoreType.DMA((2,))],
        compiler_params=_params(dimension_semantics=("arbitrary",)),
    )(dproj, w_full, x, dx2, norm_in, dw_in_chip, gslab, gnf, loss_part)


def _dw_rs(mixed, dx2b, dproj, h, table):
    tn_out, tn = 2 * SHARD_OUT, IN_PROJ_TILE
    out_steps, in_steps = D_MIX // tn_out, D_PROJ // tn
    steps = out_steps + in_steps
    out_order = (DG, NX, NY, OWN)

    def out_tile(i):
        chip = 2 * lax.axis_index("x") + lax.axis_index("y")
        return jnp.bitwise_xor(chip, (out_steps - 1) - jnp.minimum(i, out_steps - 1))

    def in_tile(table_ref, i):
        return _dw_entry(table_ref, jnp.maximum(i - out_steps, 0))

    def body(table_ref, mx_ref, dxb_ref, a_ref, h_hbm, chip_ref, gwo_ref, dwo, dwt, d2d_in, via, own, d2d, ici, b_ref,
             send_sems, recv_sems, local_sems):
        i = pl.program_id(0)
        h_load = pltpu.make_async_copy(h_hbm, b_ref, local_sems.at[8])

        @pl.when(i == 0)
        def _():
            h_load.start()

        @pl.when(i == out_steps)
        def _():
            h_load.wait()

        rs_start, rs_forward, rs_finish = _shard_sum(dwo, own, d2d, ici, send_sems, recv_sems, local_sems)
        before_tile, after_tiles, chip_finish = _chip_sum(
            dwt, d2d_in, via, chip_ref, lambda k: _dw_entry(table_ref, in_steps + k), send_sems, recv_sems, local_sems,
            N_SHARD_SUM_SEMS, 4)

        for j, k in enumerate(out_order):
            @pl.when(i == j + 1)
            def _():
                rs_start(k)

            if k != OWN:
                @pl.when(i == j + 2)
                def _():
                    rs_forward(k)

        @pl.when(i < out_steps)
        def _():
            tile = lax.dot_general(mx_ref[...], dxb_ref[...], _TN, preferred_element_type=F32).astype(BF16)
            for core in range(2):
                dwo[2 * out_tile(i) + core] = tile[SHARD_OUT * core:SHARD_OUT * (core + 1), :]

        @pl.when(i >= out_steps)
        def _():
            before_tile(i - out_steps)
            tile = lax.dot_general(a_ref[...], b_ref[...], _TN, preferred_element_type=F32).astype(BF16)
            dwt[pl.ds(pl.multiple_of(in_tile(table_ref, i) * tn, tn), tn), :] = tile

        @pl.when(i == steps - 1)
        def _():
            after_tiles()
            gwo_ref[...] = rs_finish()
            chip_finish()

    vmem = pl.BlockSpec(memory_space=pltpu.VMEM)
    grid_spec = pltpu.PrefetchScalarGridSpec(
        num_scalar_prefetch=1, grid=(steps,),
        in_specs=[pl.BlockSpec((SEQ, tn_out), lambda i, table_ref: (0, out_tile(i))), vmem,
                  pl.BlockSpec((SEQ, tn), lambda i, table_ref: (0, in_tile(table_ref, i))),
                  pl.BlockSpec(memory_space=pl.ANY)],
        out_specs=(pl.BlockSpec(memory_space=pl.ANY), pl.BlockSpec((SHARD_OUT, D_MODEL), lambda i, table_ref: (0, 0))),
        scratch_shapes=[pltpu.VMEM((N_DEV, SHARD_OUT, D_MODEL), BF16),
                        pltpu.VMEM((D_PROJ, D_MODEL), BF16), pltpu.VMEM((3, SHARD_IN, D_MODEL), BF16),
                        pltpu.VMEM((2, HALF_IN, D_MODEL), BF16),
                        *_shard_sum_scratch(SHARD_OUT), pltpu.VMEM((SEQ, D_MODEL), BF16),
                        pltpu.SemaphoreType.DMA((N_SHARD_SUM_SEMS + N_CHIP_SUM_SEMS,)),
                        pltpu.SemaphoreType.DMA((N_SHARD_SUM_SEMS + N_CHIP_SUM_SEMS,)),
                        pltpu.SemaphoreType.DMA((9,))])
    return pl.pallas_call(
        body, name="dw", grid_spec=grid_spec,
        out_shape=(jax.ShapeDtypeStruct((4, SHARD_IN, D_MODEL), BF16), jax.ShapeDtypeStruct((SHARD_OUT, D_MODEL), F32)),
        compiler_params=_params(dimension_semantics=("arbitrary",)),
    )(table, mixed, dx2b, dproj, h)


def _adam_all(big_in, big_out, gsum, small, grad_x):
    n_chunks = 4
    n_big = 8

    def body(*refs):
        ins, outs = refs[:n_big + 1 + 18 + 1], refs[n_big + 1 + 18 + 1:n_big + 1 + 18 + 1 + 34]
        in_bufs, out_bufs, gx_buf = refs[-n_big - 6 - 4:-6 - 4], refs[-6 - 4:-4], refs[-4]
        in_sems, out_sems, gx_sems = refs[-3:]

        def gx_rows(j):
            return pl.ds(j * (SEQ // n_chunks), SEQ // n_chunks)

        def gx_load(j):
            return pltpu.make_async_copy(ins[27].at[gx_rows(j), :], gx_buf.at[gx_rows(j), :], gx_sems.at[j])

        def gx_store(j):
            return pltpu.make_async_copy(gx_buf.at[gx_rows(j), :], outs[33].at[gx_rows(j), :], gx_sems.at[n_chunks + j])

        def rows(a, j):
            tr = ins[a].shape[0] // n_chunks
            return pl.ds(j * tr, tr)

        def load(a, j):
            return pltpu.make_async_copy(ins[a].at[rows(a, j), :], in_bufs[a].at[rows(a, j), :], in_sems.at[a * n_chunks + j])

        def store(a, j):
            b, kind = divmod(a, 4)
            src = in_bufs[4 * b + 1] if kind == 0 else out_bufs[3 * b + kind - 1]
            return pltpu.make_async_copy(src.at[rows(a, j), :], outs[a].at[rows(a, j), :], out_sems.at[a * n_chunks + j])

        for j in range(n_chunks):
            for a in range(n_big):
                load(a, j).start()
            gx_load(j).start()

        def small_weights():
            gsum = ins[8][...]
            idx = _slot(lax.axis_index("x"), lax.axis_index("y"), lax.axis_index("c"))
            cg = jnp.zeros((3, SHARD_CONV), F32)
            for d in range(N_DEV):
                cg = jnp.where(idx == d, gsum[ROW_CONV0:ROW_CONV0 + 3, d * SHARD_CONV:(d + 1) * SHARD_CONV], cg)
            grads = (gsum[ROW_NORM_IN:ROW_NORM_IN + 1], gsum[ROW_SINKS:ROW_SINKS + 1, 0:N_Q_HEADS],
                     gsum[ROW_NORM_CONV:ROW_NORM_CONV + 1], gsum[ROW_NORM_ATTN:ROW_NORM_ATTN + 1],
                     gsum[ROW_NORM_FINAL:ROW_NORM_FINAL + 1], cg)
            for s, g in enumerate(grads):
                at = (slice(None), 0, slice(None)) if s == 5 else (slice(None), slice(None))
                w_ref, m_ref, v_ref = ins[9 + 3 * s:12 + 3 * s]
                delta, mn, vn = _adamw(w_ref[at], g, m_ref[at], v_ref[at])
                for ref, val in zip(outs[8 + 4 * s:12 + 4 * s], (g, delta, mn, vn)):
                    ref[at] = val
            outs[32][...] = gsum[ROW_SINKS:ROW_SINKS + 1, LOSS_LANE:LOSS_LANE + 1]

        small_weights()
        for j in range(n_chunks):
            for b in range(2):
                for a in range(4 * b, 4 * b + 4):
                    load(a, j).wait()
                w_buf, g_buf, m_buf, v_buf = in_bufs[4 * b:4 * b + 4]
                r = rows(4 * b, j)
                results = _adamw(w_buf[r, :], g_buf[r, :], m_buf[r, :], v_buf[r, :])
                for buf, val in zip(out_bufs[3 * b:3 * b + 3], results):
                    buf[r, :] = val
                for a in range(4 * b, 4 * b + 4):
                    store(a, j).start()
            gx_load(j).wait()
            gx_store(j).start()
        for j in range(n_chunks):
            for a in range(n_big):
                store(a, j).wait()
            gx_store(j).wait()

    vmem, hbm = pl.BlockSpec(memory_space=pltpu.VMEM), pl.BlockSpec(memory_space=pl.ANY)
    small_shapes = [a.shape for a in small[::3]]
    big_shapes = [(SHARD_IN, D_MODEL)] * 4 + [(SHARD_OUT, D_MODEL)] * 4
    out_shape = ([jax.ShapeDtypeStruct(s, F32) for s in big_shapes]
                 + [jax.ShapeDtypeStruct(s, F32) for s in small_shapes for _ in range(4)]
                 + [jax.ShapeDtypeStruct((1, 1), F32), jax.ShapeDtypeStruct((SEQ, D_MODEL), F32)])
    outs = pl.pallas_call(
        body, name="adam", in_specs=[hbm] * n_big + [vmem] * (1 + len(small)) + [hbm],
        out_specs=tuple([hbm] * n_big + [vmem] * (4 * len(small_shapes) + 1) + [hbm]), out_shape=tuple(out_shape),
        scratch_shapes=[pltpu.VMEM(s, F32) for s in big_shapes]
                       + [pltpu.VMEM(s, F32) for s in [(SHARD_IN, D_MODEL)] * 3 + [(SHARD_OUT, D_MODEL)] * 3]
                       + [pltpu.VMEM((SEQ, D_MODEL), F32),
                          pltpu.SemaphoreType.DMA((n_big * n_chunks,)), pltpu.SemaphoreType.DMA((n_big * n_chunks,)),
                          pltpu.SemaphoreType.DMA((2 * n_chunks,))],
        compiler_params=_params(),
    )(*big_in, *big_out, gsum, *small, grad_x)
    return outs[0:4], outs[4:8], [outs[8 + 4 * s:12 + 4 * s] for s in range(6)], outs[32], outs[33]


def _rows_first(a):
    return jnp.transpose(a, (1, 0, 2))


def kernel(x, norm_in, w_in, conv_w, attn_sinks, norm_conv_out, norm_attn_out, w_out, norm_final, loss_target, m_norm_in, m_w_in, m_conv_w, m_attn_sinks, m_norm_conv_out, m_norm_attn_out, m_w_out, m_norm_final, v_norm_in, v_w_in, v_conv_w, v_attn_sinks, v_norm_conv_out, v_norm_attn_out, v_w_out, v_norm_final):
    x2d = x.reshape(SEQ, D_MODEL)
    target = loss_target.reshape(SEQ, D_MODEL)
    nf = norm_final.reshape(1, D_MODEL)

    w_in_t, m_w_in_t, v_w_in_t = w_in[0].T, m_w_in[0].T, v_w_in[0].T
    tiles = jnp.asarray(TILE_ORDER, jnp.int32).reshape(-1)
    w_in_full, h, proj, g_out, conv_full = _gather_in_proj(x2d, norm_in, w_in_t, w_out[0], _rows_first(conv_w), tiles)
    sinks = attn_sinks.reshape(N_Q_HEADS)

    mixed, attn, probs, shares = _mix_fwd(proj, conv_full, sinks, norm_conv_out, norm_attn_out)
    dx2, dx2b, dmixed, gnf, loss_part = _out_proj_loss(mixed, x2d, target, g_out.reshape(D_MIX, D_MODEL), nf)
    dproj, gslab = _mix_bwd(proj, dmixed, attn, probs, shares, conv_full, norm_conv_out, norm_attn_out)
    dw_in_chip, g_w_out = _dw_rs(mixed, dx2b, dproj, h, jnp.asarray(DW_TABLE, jnp.int32).reshape(-1))
    grad_x, g_w_in, gsum = _in_bwd_rs(dproj, w_in_full, x2d, dx2, norm_in, dw_in_chip, gslab, gnf, loss_part)

    small = (norm_in, m_norm_in, v_norm_in, attn_sinks, m_attn_sinks, v_attn_sinks,
             norm_conv_out, m_norm_conv_out, v_norm_conv_out, norm_attn_out, m_norm_attn_out, v_norm_attn_out,
             nf, m_norm_final.reshape(1, D_MODEL), v_norm_final.reshape(1, D_MODEL),
             _rows_first(conv_w), _rows_first(m_conv_w), _rows_first(v_conv_w))
    big_in, big_out, (s_ni, s_sk, s_nc, s_na, s_nf, s_cv), loss, grad_x = _adam_all(
        (w_in_t, g_w_in, m_w_in_t, v_w_in_t), (w_out[0], g_w_out, m_w_out[0], v_w_out[0]), gsum, small, grad_x)

    def leaves(k):
        return (s_ni[k], big_in[k].T[None], jnp.transpose(s_cv[k], (1, 0, 2)), s_sk[k], s_nc[k], s_na[k], big_out[k][None],
                s_nf[k].reshape(D_MODEL))

    return (loss.reshape(()), grad_x.reshape(1, SEQ, D_MODEL), *leaves(0), *leaves(1), *leaves(2), *leaves(3))
```

```python
import jax
import jax.numpy as jnp
from jax import lax
from jax.experimental import pallas as pl
from jax.experimental.pallas import tpu as pltpu

F32 = jnp.float32
BF16 = jnp.bfloat16
MESH = pl.DeviceIdType.MESH

N_DEV = 8
SEQ = 2048
D_MODEL = 1024
D_CONV = 1024
D_ATTN = 1024
D_KV = 128
HEAD_DIM = 64
N_Q_HEADS = 16
N_PAIRS = N_Q_HEADS // 2
PAIRS_PER_KV = N_PAIRS // 2
D_MIX = D_CONV + D_ATTN
D_PROJ = 6400
SHARD_IN = D_PROJ // N_DEV
SHARD_OUT = D_MIX // N_DEV
SHARD_CONV = D_CONV // N_DEV
OFF_CB, OFF_CC, OFF_CU, OFF_GC, OFF_Q, OFF_K, OFF_V, OFF_GA = 0, 1024, 2048, 3072, 4096, 5120, 5248, 5376
BLOCK = 128
N_BLOCKS = SEQ // BLOCK
HALO = 8
CHUNK = 16
N_CHUNKS = BLOCK // CHUNK
RMS_EPS = 1e-5
NEG = -1e30
SCALE = HEAD_DIM ** -0.5
SLOPES = tuple(2.0 ** (-8.0 * (h + 1) / N_Q_HEADS) for h in range(N_Q_HEADS))

ADAM_LR = 0.001
ADAM_B1 = 0.9
ADAM_B2 = 0.999
ADAM_EPS = 1e-08
ADAM_WD = 0.01
ADAM_STEP = 10

ROW_NORM_IN, ROW_NORM_CONV, ROW_NORM_ATTN, ROW_NORM_FINAL, ROW_CONV0, ROW_SINKS = 0, 1, 2, 3, 4, 7
LOSS_LANE = N_Q_HEADS
ACC_NORM_CONV, ACC_NORM_ATTN, ACC_CONV0, N_ACC = 0, 1, 2, 5

VMEM_LIMIT = 56 * 1024 * 1024

_NT = (((1,), (1,)), ((), ()))
_TN = (((0,), (0,)), ((), ()))


def _params(**kw):
    return pltpu.CompilerParams(vmem_limit_bytes=VMEM_LIMIT, **kw)


def _adamw(w, g, m, v):
    m = ADAM_B1 * m + (1.0 - ADAM_B1) * g
    v = ADAM_B2 * v + (1.0 - ADAM_B2) * (g * g)
    m_hat = m / (1.0 - ADAM_B1 ** ADAM_STEP)
    v_hat = v / (1.0 - ADAM_B2 ** ADAM_STEP)
    delta = -ADAM_LR * (m_hat / (jnp.sqrt(v_hat) + ADAM_EPS) + ADAM_WD * w)
    return delta, m, v


def _sigmoid(t):
    return 1.0 / (1.0 + jnp.exp(-t))


def _slot(px, py, pc):
    return 4 * px + 2 * py + pc


OWN, NX, NY, DG = range(4)
HALF_IN = SHARD_IN // 2
N_GATHER_KINDS = 13
W_OUT_KINDS = N_GATHER_KINDS + 7


IN_PROJ_TILE = 640
TILE_ORDER = ((0, 1, 2, 3, 4, 5, 6, 7, 8, 9), (3, 4, 0, 1, 2, 8, 9, 5, 6, 7),
              (5, 6, 0, 1, 7, 8, 9, 2, 3, 4), (8, 9, 3, 4, 5, 6, 7, 0, 1, 2))
TILES_OWN, TILES_NEIGHBOURS = 2, 7


def _tile(table_ref, p):
    chip = 2 * lax.axis_index("x") + lax.axis_index("y")
    return table_ref[chip * len(TILE_ORDER[0]) + p]


DW_TILE_ORDER = tuple(tuple(reversed(row)) for row in TILE_ORDER)


def _tiles_until_complete(chip, owner):
    lo, hi = owner * 2 * SHARD_IN, (owner + 1) * 2 * SHARD_IN
    touching = [t for t in range(len(TILE_ORDER[0])) if t * IN_PROJ_TILE < hi and (t + 1) * IN_PROJ_TILE > lo]
    return 1 + max(DW_TILE_ORDER[chip].index(t) for t in touching)


DW_TABLE = tuple(DW_TILE_ORDER[chip] + tuple(_tiles_until_complete(chip, chip ^ flip) for flip in (0, 2, 1, 3))
                 for chip in range(4))


def _dw_entry(table_ref, p):
    chip = 2 * lax.axis_index("x") + lax.axis_index("y")
    return table_ref[chip * len(DW_TABLE[0]) + p]


def _gather_in_proj(x, norm_in, w_in_sh, w_out_sh, conv_sh, tiles):
    tn = IN_PROJ_TILE
    steps = D_PROJ // tn
    tm = 256

    def body(tiles_ref, x_hbm, g_ref, win_hbm, wout_hbm, cv_ref, wt_ref, h_ref, proj_ref, gout_ref, conv_ref,
             gin_ref, gcv_ref, wob_ref, x_ref, win_ref, wout_ref, send_sems, recv_sems, local_sems):
        p = pl.program_id(0)
        local_sem = local_sems.at[0]
        x, y, c = lax.axis_index("x"), lax.axis_index("y"), lax.axis_index("c")
        me, sibling = (x, y, c), (x, y, 1 - c)
        nx, ny, dg = (1 - x, y, c), (x, 1 - y, c), (1 - x, 1 - y, c)

        def other(dev):
            return (dev[0], dev[1], 1 - dev[2])

        def shard(dev):
            return gin_ref.at[pl.ds(pl.multiple_of(_slot(*dev) * SHARD_IN, 16), SHARD_IN), :]

        def half(dev, h):
            return gin_ref.at[pl.ds(pl.multiple_of(_slot(*dev) * SHARD_IN + h * HALF_IN, 16), HALF_IN), :]

        def rc(ref, k, to):
            return pltpu.make_async_remote_copy(src_ref=ref, dst_ref=ref, send_sem=send_sems.at[k],
                                                recv_sem=recv_sems.at[k], device_id=to, device_id_type=MESH)

        def cv(k, dev, to):
            s = _slot(*dev)
            return pltpu.make_async_remote_copy(src_ref=gcv_ref.at[s], dst_ref=gcv_ref.at[s],
                                                send_sem=send_sems.at[N_GATHER_KINDS + k],
                                                recv_sem=recv_sems.at[N_GATHER_KINDS + k], device_id=to, device_id_type=MESH)

        def own_copies():
            return [rc(shard(me), 0, sibling),
                    rc(half(me, 0), 1, nx), rc(half(me, 1), 2, nx),
                    rc(half(me, 1), 4, ny), rc(half(me, 0), 3, ny),
                    cv(0, me, sibling)] + [cv(1 + j, me, peer) for j, peer in enumerate((nx, ny, dg))]

        def pass_on(dev, h, k_in, k_ici, k_d2d, half=half, base=0):
            rc(half(dev, h), base + k_in, me).wait_recv()
            if k_ici is not None:
                rc(half(dev, h), base + k_ici, ny if dev is nx else nx).start()
            rc(half(dev, h), base + k_d2d, sibling).start()

        def out_half(dev, h):
            return gout_ref.at[_slot(*dev), pl.ds(h * (SHARD_OUT // 2), SHARD_OUT // 2), :]

        def own_out_copies():
            src = lambda h: wob_ref.at[pl.ds(h * (SHARD_OUT // 2), SHARD_OUT // 2), :]

            def send(ref, dst, k, to):
                return pltpu.make_async_remote_copy(src_ref=ref, dst_ref=dst, send_sem=send_sems.at[W_OUT_KINDS + k],
                                                    recv_sem=recv_sems.at[W_OUT_KINDS + k], device_id=to, device_id_type=MESH)

            return [send(wob_ref, gout_ref.at[_slot(*me)], 0, sibling),
                    send(src(0), out_half(me, 0), 1, nx), send(src(1), out_half(me, 1), 2, nx),
                    send(src(1), out_half(me, 1), 4, ny), send(src(0), out_half(me, 0), 3, ny)]

        def own_out_local():
            return pltpu.make_async_copy(wob_ref, gout_ref.at[_slot(*me)], local_sems.at[1])

        @pl.when(p == 0)
        def _():
            halves = [pl.ds(h * HALF_IN, HALF_IN) for h in range(2)]
            win_loads = [pltpu.make_async_copy(win_hbm.at[rows, :], win_ref.at[rows, :], local_sems.at[3 + h])
                         for h, rows in enumerate(halves)]
            wout_load = pltpu.make_async_copy(wout_hbm, wout_ref, local_sems.at[5])
            for cp in win_loads + [wout_load]:
                cp.start()
            gcv_ref[_slot(*me)] = jnp.zeros((8, SHARD_CONV), F32)
            gcv_ref[_slot(*me), 0:3, :] = cv_ref[:, 0, :]
            sends = own_copies()
            for h, first in ((0, [sends[1]]), (1, [sends[3], sends[2], sends[4], sends[0]] + sends[5:])):
                win_loads[h].wait()
                gin_ref[pl.ds(pl.multiple_of(_slot(*me) * SHARD_IN + h * HALF_IN, 16), HALF_IN), :] = (
                    win_ref[halves[h], :].astype(BF16))
                for cp in first:
                    cp.start()
            wout_load.wait()
            wob_ref[...] = wout_ref[...].astype(BF16)
            x_load = pltpu.make_async_copy(x_hbm, x_ref, local_sems.at[2])
            x_load.start()
            x_load.wait()
            for t in range(SEQ // tm):
                xv = x_ref[tm * t:tm * (t + 1), :]
                r = lax.rsqrt(jnp.mean(xv * xv, axis=-1, keepdims=True) + RMS_EPS)
                h_ref[tm * t:tm * (t + 1), :] = (xv * r * g_ref[...]).astype(BF16)
            rc(shard(sibling), 0, me).wait_recv()

        @pl.when(p == TILES_OWN)
        def _():
            for args in ((nx, 0, 1, 5, 7), (ny, 1, 4, 6, 10), (nx, 1, 2, None, 8), (ny, 0, 3, None, 9)):
                pass_on(*args)
            for j, peer in enumerate((nx, ny, dg)):
                cv(1 + j, peer, me).wait_recv()
                cv(4 + j, peer, sibling).start()
            for (dev, h), k in (((nx, 0), 7), ((nx, 1), 8), ((ny, 0), 9), ((ny, 1), 10)):
                rc(half(other(dev), h), k, me).wait_recv()
            own_out_local().start()
            for cp in own_out_copies():
                cp.start()

        @pl.when(p == TILES_NEIGHBOURS - 1)
        def _():
            pass_on(dg, 0, 5, None, 11)
            pass_on(dg, 1, 6, None, 12)

        @pl.when(p == TILES_NEIGHBOURS)
        def _():
            for (dev, h), k in (((dg, 0), 11), ((dg, 1), 12)):
                rc(half(other(dev), h), k, me).wait_recv()
            pltpu.make_async_copy(gin_ref, wt_ref, local_sem).start()

        @pl.when(p == steps - 2)
        def _():
            for args in ((nx, 0, 1, 5, 7), (ny, 1, 4, 6, 10), (nx, 1, 2, None, 8), (ny, 0, 3, None, 9)):
                pass_on(*args, half=out_half, base=W_OUT_KINDS)

        w = gin_ref[pl.ds(pl.multiple_of(_tile(tiles_ref, p) * tn, tn), tn), :]
        proj_ref[...] = lax.dot_general(h_ref[...], w, _NT, preferred_element_type=F32)

        @pl.when(p == steps - 1)
        def _():
            cv(0, sibling, me).wait_recv()
            for j, peer in enumerate((nx, ny, dg)):
                cv(4 + j, other(peer), me).wait_recv()
            for d in range(N_DEV):
                conv_ref[:, d * SHARD_CONV:(d + 1) * SHARD_CONV] = gcv_ref[d]
            relayed = [rc(half(nx, 0), 5, ny), rc(half(ny, 1), 6, nx)]
            relayed += [rc(half(dev, h), k, sibling) for (dev, h), k in
                        (((nx, 0), 7), ((nx, 1), 8), ((ny, 0), 9), ((ny, 1), 10), ((dg, 0), 11), ((dg, 1), 12))]
            relayed += [cv(4 + j, peer, sibling) for j, peer in enumerate((nx, ny, dg))]
            for cp in own_copies() + relayed:
                cp.wait_send()
            pltpu.make_async_copy(gin_ref, wt_ref, local_sem).wait()
            pass_on(dg, 0, 5, None, 11, half=out_half, base=W_OUT_KINDS)
            pass_on(dg, 1, 6, None, 12, half=out_half, base=W_OUT_KINDS)
            rc(gout_ref.at[_slot(*sibling)], W_OUT_KINDS, me).wait_recv()
            out_relayed = [rc(out_half(nx, 0), W_OUT_KINDS + 5, ny), rc(out_half(ny, 1), W_OUT_KINDS + 6, nx)]
            for (dev, h), k in (((nx, 0), 7), ((nx, 1), 8), ((ny, 0), 9), ((ny, 1), 10), ((dg, 0), 11), ((dg, 1), 12)):
                rc(out_half(other(dev), h), W_OUT_KINDS + k, me).wait_recv()
                out_relayed.append(rc(out_half(dev, h), W_OUT_KINDS + k, sibling))
            for cp in own_out_copies() + out_relayed:
                cp.wait_send()
            own_out_local().wait()

    vmem = pl.BlockSpec(memory_space=pltpu.VMEM)
    grid_spec = pltpu.PrefetchScalarGridSpec(
        num_scalar_prefetch=1, grid=(steps,),
        in_specs=[pl.BlockSpec(memory_space=pl.ANY), vmem, pl.BlockSpec(memory_space=pl.ANY),
                  pl.BlockSpec(memory_space=pl.ANY), vmem],
        out_specs=(pl.BlockSpec(memory_space=pl.ANY), vmem,
                   pl.BlockSpec((SEQ, tn), lambda p, tiles_ref: (0, _tile(tiles_ref, p))),
                   pl.BlockSpec(memory_space=pl.ANY), vmem),
        scratch_shapes=[pltpu.VMEM((D_PROJ, D_MODEL), BF16), pltpu.VMEM((N_DEV, 8, SHARD_CONV), F32),
                        pltpu.VMEM((SHARD_OUT, D_MODEL), BF16), pltpu.VMEM((SEQ, D_MODEL), F32),
                        pltpu.VMEM((SHARD_IN, D_MODEL), F32), pltpu.VMEM((SHARD_OUT, D_MODEL), F32),
                        pltpu.SemaphoreType.DMA((W_OUT_KINDS + N_GATHER_KINDS,)),
                        pltpu.SemaphoreType.DMA((W_OUT_KINDS + N_GATHER_KINDS,)),
                        pltpu.SemaphoreType.DMA((6,))])
    return pl.pallas_call(
        body, name="gather_in_proj", grid_spec=grid_spec,
        out_shape=(jax.ShapeDtypeStruct((D_PROJ, D_MODEL), BF16), jax.ShapeDtypeStruct((SEQ, D_MODEL), BF16),
                   jax.ShapeDtypeStruct((SEQ, D_PROJ), F32), jax.ShapeDtypeStruct((N_DEV, SHARD_OUT, D_MODEL), BF16),
                   jax.ShapeDtypeStruct((8, D_CONV), F32)),
        compiler_params=_params(dimension_semantics=("arbitrary",)),
    )(tiles, x, norm_in, w_in_sh, w_out_sh, conv_sh)


def _shard_sum(src, own, d2d, ici, send_sems, recv_sems, local_sems, base=0):
    x, y, c = lax.axis_index("x"), lax.axis_index("y"), lax.axis_index("c")
    sibling = (x, y, 1 - c)
    chips = [(x, y), (1 - x, y), (x, 1 - y), (1 - x, 1 - y)]

    def rcopy(s, d, k, to):
        return pltpu.make_async_remote_copy(src_ref=s, dst_ref=d, send_sem=send_sems.at[base + k],
                                            recv_sem=recv_sems.at[base + k], device_id=to, device_id_type=MESH)

    def mine(k):
        return pltpu.make_async_copy(src.at[_slot(*chips[k], c)], own.at[k], local_sems.at[k])

    def to_sibling(k):
        return rcopy(src.at[_slot(*chips[k], 1 - c)], d2d.at[k], k, sibling)

    def to_chip(k):
        return rcopy(own.at[k], ici.at[k - 1], 3 + k, (*chips[k], c))

    def start(k):
        mine(k).start()
        to_sibling(k).start()

    def forward(k):
        mine(k).wait()
        to_sibling(k).wait_recv()
        own[k] = (own[k].astype(F32) + d2d[k].astype(F32)).astype(BF16)
        to_chip(k).start()

    def finish():
        mine(0).wait()
        to_sibling(0).wait_recv()
        acc = own[0].astype(F32) + d2d[0].astype(F32)
        for k in range(1, 4):
            to_chip(k).wait_recv()
            acc = acc + ici[k - 1].astype(F32)
        for k in range(4):
            to_sibling(k).wait_send()
        for k in range(1, 4):
            to_chip(k).wait_send()
        return acc

    return start, forward, finish


def _shard_sum_scratch(rows):
    return [pltpu.VMEM((4, rows, D_MODEL), BF16), pltpu.VMEM((4, rows, D_MODEL), BF16),
            pltpu.VMEM((3, rows, D_MODEL), BF16)]


N_SHARD_SUM_SEMS = 7


N_CHIP_SUM_SEMS = 5


def _chip_sum(dwt, d2d, via, out_hbm, tiles_until, send_sems, recv_sems, local_sems, base, local_base):
    x, y, c = lax.axis_index("x"), lax.axis_index("y"), lax.axis_index("c")
    sibling, nx, ny = (x, y, 1 - c), (1 - x, y, c), (x, 1 - y, c)
    chips = [(x, y), (1 - x, y), (x, 1 - y), (1 - x, 1 - y)]

    def shard(s):
        return dwt.at[pl.ds(pl.multiple_of(s * SHARD_IN, 16), SHARD_IN), :]

    def half(ref, h):
        return ref.at[pl.ds(h * HALF_IN, HALF_IN), :]

    def rc(s, d, k, to):
        return pltpu.make_async_remote_copy(src_ref=s, dst_ref=d, send_sem=send_sems.at[base + k],
                                            recv_sem=recv_sems.at[base + k], device_id=to, device_id_type=MESH)

    def to_sibling(k):
        return rc(shard(_slot(*chips[k], 1 - c)), d2d.at[k - 1], k - 1, sibling)

    for_dg = (lambda: rc(half(d2d.at[DG - 1], 0), via.at[0], 3, nx), lambda: rc(half(d2d.at[DG - 1], 1), via.at[1], 4, ny))

    def save(k):
        return pltpu.make_async_copy(d2d.at[k - 1], out_hbm.at[k], local_sems.at[local_base + k])

    own_saves = (lambda: pltpu.make_async_copy(shard(_slot(x, y, c)), out_hbm.at[OWN], local_sems.at[local_base]),
                 lambda: pltpu.make_async_copy(shard(_slot(x, y, 1 - c)), out_hbm.at[3], local_sems.at[local_base + 3]))

    def before_tile(n):
        for k in (NX, NY, DG):
            @pl.when(tiles_until(k) == n)
            def _():
                to_sibling(k).start()

            @pl.when(tiles_until(k) + 1 == n)
            def _():
                to_sibling(k).wait_recv()
                d2d[k - 1] = (shard(_slot(*chips[k], c))[...].astype(F32) + d2d[k - 1].astype(F32)).astype(BF16)
                if k == DG:
                    for cp in for_dg:
                        cp().start()

    def after_tiles():
        for cp in own_saves:
            cp().start()

    def finish():
        for k, h in ((NY, 0), (NX, 1)):
            for_dg[h]().wait_recv()
            rows = pl.ds(h * HALF_IN, HALF_IN)
            d2d[k - 1, rows, :] = (d2d[k - 1, rows, :].astype(F32) + via[h].astype(F32)).astype(BF16)
            save(k).start()
        for cp in own_saves + (lambda: save(NX), lambda: save(NY)):
            cp().wait()
        for cp in (lambda: to_sibling(NX), lambda: to_sibling(NY), lambda: to_sibling(DG)) + for_dg:
            cp().wait_send()

    return before_tile, after_tiles, finish


N_ICI_SUM_SEMS = 3


def _ici_sum(src, own, d2d, ici, send_sems, recv_sems, local_sems, base=0):
    x, y, c = lax.axis_index("x"), lax.axis_index("y"), lax.axis_index("c")

    def rc(s, d, k, to):
        return pltpu.make_async_remote_copy(src_ref=s, dst_ref=d, send_sem=send_sems.at[base + k],
                                            recv_sem=recv_sems.at[base + k], device_id=to, device_id_type=MESH)

    copies = (lambda: rc(src.at[NX], ici.at[0], 0, (1 - x, y, c)), lambda: rc(src.at[NY], ici.at[1], 1, (x, 1 - y, c)),
              lambda: rc(src.at[3], d2d, 2, (x, y, 1 - c)))
    mine = lambda: pltpu.make_async_copy(src.at[OWN], own, local_sems.at[0])

    def start():
        for cp in copies + (mine,):
            cp().start()

    def finish():
        mine().wait()
        for cp in copies:
            cp().wait_recv()
        acc = own[...].astype(F32) + d2d[...].astype(F32) + ici[0].astype(F32) + ici[1].astype(F32)
        for cp in copies:
            cp().wait_send()
        return acc

    return start, finish


def _slab_sum(myslab, slabs, send_sems, recv_sems, base):
    x, y, c = lax.axis_index("x"), lax.axis_index("y"), lax.axis_index("c")
    me = _slot(x, y, c)
    peers = [(x, y, 1 - c), (1 - x, y, c), (x, 1 - y, c), (1 - x, 1 - y, c),
             (1 - x, y, 1 - c), (x, 1 - y, 1 - c), (1 - x, 1 - y, 1 - c)]

    def cp(k):
        return pltpu.make_async_remote_copy(src_ref=myslab, dst_ref=slabs.at[me], send_sem=send_sems.at[base + k],
                                            recv_sem=recv_sems.at[base + k], device_id=peers[k], device_id_type=MESH)

    def start():
        slabs[me] = myslab[...]
        for k in range(7):
            cp(k).start()

    def finish():
        for k in range(7):
            cp(k).wait_recv()
        total = slabs[0]
        for d in range(1, N_DEV):
            total = total + slabs[d]
        for k in range(7):
            cp(k).wait_send()
        return total

    return start, finish


def _chunk_rows(r):
    return slice(r * CHUNK, (r + 1) * CHUNK)


def _conv_halo(cch_ref, cuh_ref, n):
    zh = jnp.where(n > 0, cch_ref[...] * cuh_ref[...], 0.0)
    return jnp.concatenate([zh] * (CHUNK // HALO), axis=0)


def _conv_chunk(pj_ref, zhalo, cw, r):
    rows = _chunk_rows(r)
    cc = pj_ref[rows, OFF_CC:OFF_CC + D_CONV]
    cu = pj_ref[rows, OFF_CU:OFF_CU + D_CONV]
    z = cc * cu
    before = _chunk_rows(r - 1)
    zprev = pj_ref[before, OFF_CC:OFF_CC + D_CONV] * pj_ref[before, OFF_CU:OFF_CU + D_CONV] if r > 0 else zhalo
    row = lax.broadcasted_iota(jnp.int32, (CHUNK, D_CONV), 0)
    z1 = jnp.where(row < 1, pltpu.roll(zprev, 1, 0), pltpu.roll(z, 1, 0))
    z2 = jnp.where(row < 2, pltpu.roll(zprev, 2, 0), pltpu.roll(z, 2, 0))
    co = cw[0] * z2 + cw[1] * z1 + cw[2] * z
    return cc, cu, z, z1, z2, co


def _gated_norm(a, gain, t):
    r = lax.rsqrt(jnp.mean(a * a, axis=-1, keepdims=True) + RMS_EPS)
    return a * r * gain * (t * _sigmoid(t))


def _kv_bands(pj, kvp_ref):
    lane = lax.broadcasted_iota(jnp.int32, (2 * BLOCK, D_KV), 1)
    lo = lane < HEAD_DIM

    def bands(prev, cur):
        b = jnp.concatenate([prev, cur], axis=0)
        br = pltpu.roll(b, HEAD_DIM, 1)
        zero = jnp.zeros_like(b)
        return ((jnp.where(lo, b, zero).astype(BF16), jnp.where(lo, zero, br).astype(BF16)),
                (jnp.where(lo, br, zero).astype(BF16), jnp.where(lo, zero, b).astype(BF16)))

    ks = bands(kvp_ref[:, 0:D_KV], pj[:, OFF_K:OFF_K + D_KV])
    vs = bands(kvp_ref[:, D_KV:2 * D_KV], pj[:, OFF_V:OFF_V + D_KV])
    return ks, vs


STACK = PAIRS_PER_KV * BLOCK


def _head(j, i, e):
    return 2 * (PAIRS_PER_KV * j + i) + e


def _pair_cols(j, i, off):
    p = PAIRS_PER_KV * j + i
    return slice(off + 128 * p, off + 128 * (p + 1))


def _fill_attn_bias(bias_scr, first_block):
    qi = lax.broadcasted_iota(jnp.int32, (BLOCK, 2 * BLOCK), 0)
    kj = lax.broadcasted_iota(jnp.int32, (BLOCK, 2 * BLOCK), 1)
    dist = BLOCK + qi - kj
    valid = (dist >= 0) & (dist < BLOCK)
    if first_block:
        valid = valid & (kj >= BLOCK)
    distf = dist.astype(F32)
    for j in range(2):
        for e in range(2):
            for i in range(PAIRS_PER_KV):
                bias_scr[2 * j + e, BLOCK * i:BLOCK * (i + 1), :] = jnp.where(valid, -SLOPES[_head(j, i, e)] * distf, NEG)


def _q_stack(pj, j):
    return jnp.concatenate([(pj[:, _pair_cols(j, i, OFF_Q)] * SCALE).astype(BF16) for i in range(PAIRS_PER_KV)], axis=0)


def _attn_probs(q_stack, kband, bias_ref, sinks):
    s = lax.dot_general(q_stack, kband, _NT, preferred_element_type=F32)
    ones = jnp.ones((128, 128), BF16)
    probs, shares = [], []
    for i, sink in enumerate(sinks):
        rows = slice(BLOCK * i, BLOCK * (i + 1))
        t = s[rows, :] + bias_ref[rows, :]
        m = jnp.broadcast_to(jnp.max(t, axis=-1, keepdims=True), (BLOCK, 128))
        m = jnp.maximum(m, sink)
        p = [jnp.exp(t[:, :128] - m), jnp.exp(t[:, 128:] - m)]
        es = jnp.exp(sink - m)
        total = (jnp.dot(p[0].astype(BF16), ones, preferred_element_type=F32)
                 + jnp.dot(p[1].astype(BF16), ones, preferred_element_type=F32))
        inv = 1.0 / (total + es)
        probs.append(jnp.concatenate([p[0] * inv, p[1] * inv], axis=1))
        shares.append(es * inv)
    return jnp.concatenate(probs, axis=0), jnp.concatenate(shares, axis=0)


def _attn_group(pj, ks, vs, bias_scr, sink_ref, j):
    q_stack = _q_stack(pj, j)
    out, probs, shares = None, [], []
    for e in range(2):
        p, ps = _attn_probs(q_stack, ks[j][e], bias_scr.at[2 * j + e],
                            [sink_ref[_head(j, i, e)] for i in range(PAIRS_PER_KV)])
        p = p.astype(BF16)
        o = jnp.dot(p, vs[j][e], preferred_element_type=F32)
        out = o if out is None else out + o
        probs.append(p)
        shares.append(ps)
    return out, probs, shares


def _mix_fwd(proj, conv_full, sinks, norm_conv, norm_attn):
    def body(pj_ref, kvp_ref, cch_ref, cuh_ref, cw_ref, sink_ref, gc_ref, ga_ref,
             mixed_ref, attn_scr, p_ref, ps_ref, bias_scr):
        n = pl.program_id(0)
        pj = pj_ref

        @pl.when(n == 0)
        def _():
            _fill_attn_bias(bias_scr, first_block=True)

        @pl.when(n == 1)
        def _():
            _fill_attn_bias(bias_scr, first_block=False)

        zhalo = _conv_halo(cch_ref, cuh_ref, n)
        cw = (cw_ref[0:1, :], cw_ref[1:2, :], cw_ref[2:3, :])
        gain_c = gc_ref[...]

        for r in range(N_CHUNKS):
            rows = _chunk_rows(r)
            co = _conv_chunk(pj_ref, zhalo, cw, r)[-1]
            y = _gated_norm(pj_ref[rows, OFF_CB:OFF_CB + D_CONV] * co, gain_c, pj_ref[rows, OFF_GC:OFF_GC + D_CONV])
            mixed_ref[rows, 0:D_CONV] = y.astype(BF16)

        ks, vs = _kv_bands(pj, kvp_ref)
        for j in range(2):
            out, probs, shares = _attn_group(pj, ks, vs, bias_scr, sink_ref, j)
            for e in range(2):
                p_ref[0, 2 * j + e] = probs[e]
                ps_ref[0, 2 * j + e] = shares[e]
            for i in range(PAIRS_PER_KV):
                attn_scr[:, _pair_cols(j, i, 0)] = out[BLOCK * i:BLOCK * (i + 1), :]
        gain_a = ga_ref[...]

        for r in range(N_CHUNKS):
            rows = _chunk_rows(r)
            y = _gated_norm(attn_scr[rows, :], gain_a, pj_ref[rows, OFF_GA:OFF_GA + D_ATTN])
            mixed_ref[rows, D_CONV:D_MIX] = y.astype(BF16)

    per_block = BLOCK // HALO
    return pl.pallas_call(
        body, name="mix_fwd", grid=(N_BLOCKS,),
        in_specs=[
            pl.BlockSpec((BLOCK, D_PROJ), lambda n: (n, 0)),
            pl.BlockSpec((BLOCK, 2 * D_KV), lambda n: (jnp.maximum(n - 1, 0), OFF_K // (2 * D_KV))),
            pl.BlockSpec((HALO, D_CONV), lambda n: (jnp.maximum(n * per_block - 1, 0), OFF_CC // D_CONV)),
            pl.BlockSpec((HALO, D_CONV), lambda n: (jnp.maximum(n * per_block - 1, 0), OFF_CU // D_CONV)),
            pl.BlockSpec((8, D_CONV), lambda n: (0, 0)),
            pl.BlockSpec(memory_space=pltpu.SMEM),
            pl.BlockSpec((1, D_CONV), lambda n: (0, 0)),
            pl.BlockSpec((1, D_ATTN), lambda n: (0, 0)),
        ],
        out_specs=(pl.BlockSpec((BLOCK, D_MIX), lambda n: (n, 0)), pl.BlockSpec((BLOCK, D_ATTN), lambda n: (n, 0)),
                   pl.BlockSpec((1, 4, STACK, 2 * BLOCK), lambda n: (n, 0, 0, 0)),
                   pl.BlockSpec((1, 4, STACK, 128), lambda n: (n, 0, 0, 0))),
        out_shape=(jax.ShapeDtypeStruct((SEQ, D_MIX), BF16), jax.ShapeDtypeStruct((SEQ, D_ATTN), F32),
                   jax.ShapeDtypeStruct((N_BLOCKS, 4, STACK, 2 * BLOCK), BF16),
                   jax.ShapeDtypeStruct((N_BLOCKS, 4, STACK, 128), F32)),
        scratch_shapes=[pltpu.VMEM((4, STACK, 2 * BLOCK), F32)],
        compiler_params=_params(dimension_semantics=("arbitrary",)),
    )(proj, proj, proj, proj, conv_full, sinks, norm_conv, norm_attn)


def _out_proj_loss(mixed, x, target, w_out_full, norm_final):
    tm = 256

    def body(mx_ref, x_ref, t_ref, w_ref, g_ref, dx2_ref, dx2b_ref, dmix_ref, gnf_ref, loss_ref):
        i = pl.program_id(0)
        w = w_ref[...]
        x2 = x_ref[...] + jnp.dot(mx_ref[...], w, preferred_element_type=F32)
        r = lax.rsqrt(jnp.mean(x2 * x2, axis=-1, keepdims=True) + RMS_EPS)
        xn = x2 * r
        g = g_ref[...]
        err = xn * g - t_ref[...]
        part = 0.5 * jnp.sum(jnp.mean(err * err, axis=-1, keepdims=True), axis=0, keepdims=True)
        dy = err * (1.0 / D_MODEL)
        gnf = jnp.sum(dy * xn, axis=0, keepdims=True)
        u = dy * g
        dx2 = r * (u - xn * jnp.mean(u * xn, axis=-1, keepdims=True))
        dx2_ref[...] = dx2
        dx2b = dx2.astype(BF16)
        dx2b_ref[...] = dx2b
        dmix_ref[...] = lax.dot_general(dx2b, w, _NT, preferred_element_type=F32)

        @pl.when(i == 0)
        def _():
            gnf_ref[...] = jnp.zeros_like(gnf_ref)
            loss_ref[...] = jnp.zeros_like(loss_ref)

        gnf_ref[...] += gnf
        loss_ref[...] += jnp.broadcast_to(part, loss_ref.shape)

    return pl.pallas_call(
        body, name="out_proj_loss", grid=(SEQ // tm,),
        in_specs=[pl.BlockSpec((tm, D_MIX), lambda i: (i, 0)), pl.BlockSpec((tm, D_MODEL), lambda i: (i, 0)),
                  pl.BlockSpec((tm, D_MODEL), lambda i: (i, 0)), pl.BlockSpec(memory_space=pltpu.VMEM),
                  pl.BlockSpec((1, D_MODEL), lambda i: (0, 0))],
        out_specs=(pl.BlockSpec((tm, D_MODEL), lambda i: (i, 0)), pl.BlockSpec((tm, D_MODEL), lambda i: (i, 0)),
                   pl.BlockSpec((tm, D_MIX), lambda i: (i, 0)),
                   pl.BlockSpec((1, D_MODEL), lambda i: (0, 0)), pl.BlockSpec((8, 128), lambda i: (0, 0))),
        out_shape=(jax.ShapeDtypeStruct((SEQ, D_MODEL), F32), jax.ShapeDtypeStruct((SEQ, D_MODEL), BF16),
                   jax.ShapeDtypeStruct((SEQ, D_MIX), F32),
                   jax.ShapeDtypeStruct((1, D_MODEL), F32), jax.ShapeDtypeStruct((8, 128), F32)),
        compiler_params=_params(dimension_semantics=("arbitrary",)),
    )(mixed, x, target, w_out_full, norm_final)


def _gated_norm_bwd(a, gain, t, dy):
    r = lax.rsqrt(jnp.mean(a * a, axis=-1, keepdims=True) + RMS_EPS)
    an = a * r
    sg = _sigmoid(t)
    dn = dy * (t * sg)
    dt = dy * (an * gain) * (sg * (1.0 + t * (1.0 - sg)))
    u = dn * gain
    da = r * (u - an * jnp.mean(u * an, axis=-1, keepdims=True))
    return da, dt, dn * an


def _mix_bwd(proj, dmixed, attn, probs, shares, conv_full, norm_conv, norm_attn):
    def body(pj_ref, kvp_ref, cch_ref, cuh_ref, dmx_ref, attn_ref, p_ref, ps_ref, cw_ref, gc_ref, ga_ref,
             dpj_ref, gslab_ref, dattn_scr, nxt_scr, dkv_scr, acc_scr):
        step = pl.program_id(0)
        n = N_BLOCKS - 1 - step
        pj = pj_ref

        @pl.when(step == 0)
        def _():
            gslab_ref[...] = jnp.zeros_like(gslab_ref)
            nxt_scr[...] = jnp.zeros_like(nxt_scr)
            dkv_scr[...] = jnp.zeros_like(dkv_scr)
            acc_scr[...] = jnp.zeros_like(acc_scr)

        zhalo = _conv_halo(cch_ref, cuh_ref, n)
        cw = (cw_ref[0:1, :], cw_ref[1:2, :], cw_ref[2:3, :])
        gain_c = gc_ref[...]
        row = lax.broadcasted_iota(jnp.int32, (CHUNK, D_CONV), 0)

        dco_after = nxt_scr[...]
        for r in reversed(range(N_CHUNKS)):
            rows = _chunk_rows(r)
            cc, cu, z, z1, z2, co = _conv_chunk(pj_ref, zhalo, cw, r)
            cb = pj_ref[rows, OFF_CB:OFF_CB + D_CONV]
            da, dgate, gterm = _gated_norm_bwd(cb * co, gain_c, pj_ref[rows, OFF_GC:OFF_GC + D_CONV],
                                               dmx_ref[rows, 0:D_CONV])
            dpj_ref[rows, OFF_GC:OFF_GC + D_CONV] = dgate.astype(BF16)
            dpj_ref[rows, OFF_CB:OFF_CB + D_CONV] = (da * co).astype(BF16)
            dco = da * cb
            dco1 = jnp.where(row >= CHUNK - 1, pltpu.roll(dco_after, CHUNK - 1, 0), pltpu.roll(dco, CHUNK - 1, 0))
            dco2 = jnp.where(row >= CHUNK - 2, pltpu.roll(dco_after, CHUNK - 2, 0), pltpu.roll(dco, CHUNK - 2, 0))
            dz = cw[2] * dco + cw[1] * dco1 + cw[0] * dco2
            dpj_ref[rows, OFF_CC:OFF_CC + D_CONV] = (dz * cu).astype(BF16)
            dpj_ref[rows, OFF_CU:OFF_CU + D_CONV] = (dz * cc).astype(BF16)
            acc_scr[ACC_NORM_CONV] += gterm
            acc_scr[ACC_CONV0] += dco * z2
            acc_scr[ACC_CONV0 + 1] += dco * z1
            acc_scr[ACC_CONV0 + 2] += dco * z
            dco_after = dco
        nxt_scr[...] = dco_after

        ks, vs = _kv_bands(pj, kvp_ref)
        gain_a = ga_ref[...]

        for r in range(N_CHUNKS):
            rows = _chunk_rows(r)
            da, dgate, gterm = _gated_norm_bwd(attn_ref[rows, :], gain_a, pj_ref[rows, OFF_GA:OFF_GA + D_ATTN],
                                               dmx_ref[rows, D_CONV:D_MIX])
            dpj_ref[rows, OFF_GA:OFF_GA + D_ATTN] = dgate.astype(BF16)
            dattn_scr[rows, :] = da
            acc_scr[ACC_NORM_ATTN] += gterm

        in_lo = lax.broadcasted_iota(jnp.int32, (128, 128), 0) < HEAD_DIM
        half_ones = (jnp.where(in_lo, 1.0, 0.0).astype(BF16), jnp.where(in_lo, 0.0, 1.0).astype(BF16))
        lane_s = lax.broadcasted_iota(jnp.int32, (1, D_MODEL), 1)
        gsink = jnp.zeros((1, D_MODEL), F32)
        dk_t, dv_t = [], []
        for j in range(2):
            q_stack = _q_stack(pj, j)
            do_f = jnp.concatenate([dattn_scr[:, _pair_cols(j, i, 0)] for i in range(PAIRS_PER_KV)], axis=0)
            o_f = jnp.concatenate([attn_ref[:, _pair_cols(j, i, 0)] for i in range(PAIRS_PER_KV)], axis=0)
            prod = (do_f * o_f).astype(BF16)
            deltas = [jnp.dot(prod, half_ones[e], preferred_element_type=F32) for e in range(2)]
            do_b = do_f.astype(BF16)
            q_t, do_t = q_stack.T, do_b.T
            dq, dk_j, dv_j = None, None, None
            for e in range(2):
                p = p_ref[0, 2 * j + e]
                dp = lax.dot_general(do_b, vs[j][e], _NT, preferred_element_type=F32)
                ds = []
                for i in range(PAIRS_PER_KV):
                    rows = slice(BLOCK * i, BLOCK * (i + 1))
                    delta = deltas[e][rows, :]
                    ds.append((p[rows, :].astype(F32) * (dp[rows, :] - jnp.concatenate([delta, delta], axis=1))).astype(BF16))
                    gs_h = -jnp.sum(ps_ref[0, 2 * j + e, rows, 0:1] * delta[:, 0:1], axis=0, keepdims=True)
                    gsink = gsink + jnp.where(lane_s == _head(j, i, e), gs_h, 0.0)
                ds = jnp.concatenate(ds, axis=0)
                t = jnp.dot(ds, ks[j][e], preferred_element_type=F32)
                dq = t if dq is None else dq + t
                half = slice(HEAD_DIM * e, HEAD_DIM * (e + 1))
                a = jnp.dot(q_t[half, :], ds, preferred_element_type=F32)
                b = jnp.dot(do_t[half, :], p, preferred_element_type=F32)
                dk_j = a if dk_j is None else dk_j + a
                dv_j = b if dv_j is None else dv_j + b
            for i in range(PAIRS_PER_KV):
                dpj_ref[:, _pair_cols(j, i, OFF_Q)] = (dq[BLOCK * i:BLOCK * (i + 1), :] * SCALE).astype(BF16)
            dk_t.append(dk_j)
            dv_t.append(dv_j)
        dk = jnp.concatenate(dk_t, axis=0).T
        dv = jnp.concatenate(dv_t, axis=0).T
        dpj_ref[:, OFF_K:OFF_K + D_KV] = (dk[BLOCK:, :] + dkv_scr[:, 0:D_KV]).astype(BF16)
        dpj_ref[:, OFF_V:OFF_V + D_KV] = (dv[BLOCK:, :] + dkv_scr[:, D_KV:2 * D_KV]).astype(BF16)
        dkv_scr[:, 0:D_KV] = dk[:BLOCK, :]
        dkv_scr[:, D_KV:2 * D_KV] = dv[:BLOCK, :]
        gslab_ref[ROW_SINKS:ROW_SINKS + 1, :] += gsink

        @pl.when(step == N_BLOCKS - 1)
        def _():
            for k, slab_row in ((ACC_NORM_CONV, ROW_NORM_CONV), (ACC_NORM_ATTN, ROW_NORM_ATTN), (ACC_CONV0, ROW_CONV0),
                                (ACC_CONV0 + 1, ROW_CONV0 + 1), (ACC_CONV0 + 2, ROW_CONV0 + 2)):
                gslab_ref[slab_row:slab_row + 1, :] = jnp.sum(acc_scr[k], axis=0, keepdims=True)

    per_block = BLOCK // HALO
    last = N_BLOCKS - 1
    return pl.pallas_call(
        body, name="mix_bwd", grid=(N_BLOCKS,),
        in_specs=[
            pl.BlockSpec((BLOCK, D_PROJ), lambda s: (last - s, 0)),
            pl.BlockSpec((BLOCK, 2 * D_KV), lambda s: (jnp.maximum(last - s - 1, 0), OFF_K // (2 * D_KV))),
            pl.BlockSpec((HALO, D_CONV), lambda s: (jnp.maximum((last - s) * per_block - 1, 0), OFF_CC // D_CONV)),
            pl.BlockSpec((HALO, D_CONV), lambda s: (jnp.maximum((last - s) * per_block - 1, 0), OFF_CU // D_CONV)),
            pl.BlockSpec((BLOCK, D_MIX), lambda s: (last - s, 0)),
            pl.BlockSpec((BLOCK, D_ATTN), lambda s: (last - s, 0)),
            pl.BlockSpec((1, 4, STACK, 2 * BLOCK), lambda s: (last - s, 0, 0, 0)),
            pl.BlockSpec((1, 4, STACK, 128), lambda s: (last - s, 0, 0, 0)),
            pl.BlockSpec((8, D_CONV), lambda s: (0, 0)),
            pl.BlockSpec((1, D_CONV), lambda s: (0, 0)),
            pl.BlockSpec((1, D_ATTN), lambda s: (0, 0)),
        ],
        out_specs=(pl.BlockSpec((BLOCK, D_PROJ), lambda s: (last - s, 0)),
                   pl.BlockSpec((8, D_MODEL), lambda s: (0, 0))),
        out_shape=(jax.ShapeDtypeStruct((SEQ, D_PROJ), BF16), jax.ShapeDtypeStruct((8, D_MODEL), F32)),
        scratch_shapes=[pltpu.VMEM((BLOCK, D_ATTN), F32), pltpu.VMEM((CHUNK, D_CONV), F32),
                        pltpu.VMEM((BLOCK, 2 * D_KV), F32), pltpu.VMEM((N_ACC, CHUNK, D_MODEL), F32)],
        compiler_params=_params(dimension_semantics=("arbitrary",)),
    )(proj, proj, proj, proj, dmixed, attn, probs, shares, conv_full, norm_conv, norm_attn)


def _in_bwd_rs(dproj, w_full, x, dx2, norm_in, dw_in_chip, gslab, gnf, loss_part):
    tm = 256
    steps = SEQ // tm

    def body(dp_ref, w_hbm, x_ref, dx2_ref, g_ref, dwi_ref, gs_ref, gnf_ref, lp_ref, gx_ref, gwin_ref, gsum_ref,
             gni_scr, own, d2d, ici, myslab, slabs, w_ref, send_sems, recv_sems, local_sems):
        i = pl.program_id(0)
        rs_start, rs_finish = _ici_sum(dwi_ref, own, d2d, ici, send_sems, recv_sems, local_sems)
        slab_start, slab_finish = _slab_sum(myslab, slabs, send_sems, recv_sems, N_ICI_SUM_SEMS)

        @pl.when(i == 0)
        def _():
            gni_scr[...] = jnp.zeros_like(gni_scr)
            rs_start()
            w_load = pltpu.make_async_copy(w_hbm, w_ref, local_sems.at[1])
            w_load.start()
            w_load.wait()

        dh = jnp.dot(dp_ref[...], w_ref[...], preferred_element_type=F32)
        xv = x_ref[...]
        r = lax.rsqrt(jnp.mean(xv * xv, axis=-1, keepdims=True) + RMS_EPS)
        xn = xv * r
        u = dh * g_ref[...]
        gx_ref[...] = dx2_ref[...] + r * (u - xn * jnp.mean(u * xn, axis=-1, keepdims=True))
        gni_scr[...] += jnp.sum(dh * xn, axis=0, keepdims=True)

        @pl.when(i == steps - 1)
        def _():
            row = lax.broadcasted_iota(jnp.int32, (8, D_MODEL), 0)
            lane = lax.broadcasted_iota(jnp.int32, (8, D_MODEL), 1)
            slab = jnp.where(row == ROW_NORM_IN, gni_scr[...], jnp.where(row == ROW_NORM_FINAL, gnf_ref[...], gs_ref[...]))
            myslab[...] = jnp.where((row == ROW_SINKS) & (lane == LOSS_LANE), lp_ref[0:1, 0:1], slab)
            slab_start()
            gwin_ref[...] = rs_finish()
            gsum_ref[...] = slab_finish()

    const = lambda i: (0, 0)
    return pl.pallas_call(
        body, name="in_bwd", grid=(steps,),
        in_specs=[pl.BlockSpec((tm, D_PROJ), lambda i: (i, 0)), pl.BlockSpec(memory_space=pl.ANY),
                  pl.BlockSpec((tm, D_MODEL), lambda i: (i, 0)), pl.BlockSpec((tm, D_MODEL), lambda i: (i, 0)),
                  pl.BlockSpec((1, D_MODEL), const), pl.BlockSpec(memory_space=pl.ANY),
                  pl.BlockSpec((8, D_MODEL), const), pl.BlockSpec((1, D_MODEL), const), pl.BlockSpec((8, 128), const)],
        out_specs=(pl.BlockSpec((tm, D_MODEL), lambda i: (i, 0)), pl.BlockSpec((SHARD_IN, D_MODEL), const),
                   pl.BlockSpec((8, D_MODEL), const)),
        out_shape=(jax.ShapeDtypeStruct((SEQ, D_MODEL), F32), jax.ShapeDtypeStruct((SHARD_IN, D_MODEL), F32),
                   jax.ShapeDtypeStruct((8, D_MODEL), F32)),
        scratch_shapes=[pltpu.VMEM((1, D_MODEL), F32), pltpu.VMEM((SHARD_IN, D_MODEL), BF16),
                        pltpu.VMEM((SHARD_IN, D_MODEL), BF16), pltpu.VMEM((2, SHARD_IN, D_MODEL), BF16),
                        pltpu.VMEM((8, D_MODEL), F32), pltpu.VMEM((N_DEV, 8, D_MODEL), F32),
                        pltpu.VMEM((D_PROJ, D_MODEL), BF16),
                        pltpu.SemaphoreType.DMA((N_ICI_SUM_SEMS + 7,)), pltpu.SemaphoreType.DMA((N_ICI_SUM_SEMS + 7,)),
                        pltpu.SemaphoreType.DMA((2,))],
        compiler_params=_params(dimension_semantics=("arbitrary",)),
    )(dproj, w_full, x, dx2, norm_in, dw_in_chip, gslab, gnf, loss_part)


def _dw_rs(mixed, dx2b, dproj, h, table):
    tn_out, tn = 2 * SHARD_OUT, IN_PROJ_TILE
    out_steps, in_steps = D_MIX // tn_out, D_PROJ // tn
    steps = out_steps + in_steps
    out_order = (DG, NX, NY, OWN)

    def out_tile(i):
        chip = 2 * lax.axis_index("x") + lax.axis_index("y")
        return jnp.bitwise_xor(chip, (out_steps - 1) - jnp.minimum(i, out_steps - 1))

    def in_tile(table_ref, i):
        return _dw_entry(table_ref, jnp.maximum(i - out_steps, 0))

    def body(table_ref, mx_ref, dxb_ref, a_ref, h_hbm, chip_ref, gwo_ref, dwo, dwt, d2d_in, via, own, d2d, ici, b_ref,
             send_sems, recv_sems, local_sems):
        i = pl.program_id(0)
        h_load = pltpu.make_async_copy(h_hbm, b_ref, local_sems.at[8])

        @pl.when(i == 0)
        def _():
            h_load.start()

        @pl.when(i == out_steps)
        def _():
            h_load.wait()

        rs_start, rs_forward, rs_finish = _shard_sum(dwo, own, d2d, ici, send_sems, recv_sems, local_sems)
        before_tile, after_tiles, chip_finish = _chip_sum(
            dwt, d2d_in, via, chip_ref, lambda k: _dw_entry(table_ref, in_steps + k), send_sems, recv_sems, local_sems,
            N_SHARD_SUM_SEMS, 4)

        for j, k in enumerate(out_order):
            @pl.when(i == j + 1)
            def _():
                rs_start(k)

            if k != OWN:
                @pl.when(i == j + 2)
                def _():
                    rs_forward(k)

        @pl.when(i < out_steps)
        def _():
            tile = lax.dot_general(mx_ref[...], dxb_ref[...], _TN, preferred_element_type=F32).astype(BF16)
            for core in range(2):
                dwo[2 * out_tile(i) + core] = tile[SHARD_OUT * core:SHARD_OUT * (core + 1), :]

        @pl.when(i >= out_steps)
        def _():
            before_tile(i - out_steps)
            tile = lax.dot_general(a_ref[...], b_ref[...], _TN, preferred_element_type=F32).astype(BF16)
            dwt[pl.ds(pl.multiple_of(in_tile(table_ref, i) * tn, tn), tn), :] = tile

        @pl.when(i == steps - 1)
        def _():
            after_tiles()
            gwo_ref[...] = rs_finish()
            chip_finish()

    vmem = pl.BlockSpec(memory_space=pltpu.VMEM)
    grid_spec = pltpu.PrefetchScalarGridSpec(
        num_scalar_prefetch=1, grid=(steps,),
        in_specs=[pl.BlockSpec((SEQ, tn_out), lambda i, table_ref: (0, out_tile(i))), vmem,
                  pl.BlockSpec((SEQ, tn), lambda i, table_ref: (0, in_tile(table_ref, i))),
                  pl.BlockSpec(memory_space=pl.ANY)],
        out_specs=(pl.BlockSpec(memory_space=pl.ANY), pl.BlockSpec((SHARD_OUT, D_MODEL), lambda i, table_ref: (0, 0))),
        scratch_shapes=[pltpu.VMEM((N_DEV, SHARD_OUT, D_MODEL), BF16),
                        pltpu.VMEM((D_PROJ, D_MODEL), BF16), pltpu.VMEM((3, SHARD_IN, D_MODEL), BF16),
                        pltpu.VMEM((2, HALF_IN, D_MODEL), BF16),
                        *_shard_sum_scratch(SHARD_OUT), pltpu.VMEM((SEQ, D_MODEL), BF16),
                        pltpu.SemaphoreType.DMA((N_SHARD_SUM_SEMS + N_CHIP_SUM_SEMS,)),
                        pltpu.SemaphoreType.DMA((N_SHARD_SUM_SEMS + N_CHIP_SUM_SEMS,)),
                        pltpu.SemaphoreType.DMA((9,))])
    return pl.pallas_call(
        body, name="dw", grid_spec=grid_spec,
        out_shape=(jax.ShapeDtypeStruct((4, SHARD_IN, D_MODEL), BF16), jax.ShapeDtypeStruct((SHARD_OUT, D_MODEL), F32)),
        compiler_params=_params(dimension_semantics=("arbitrary",)),
    )(table, mixed, dx2b, dproj, h)


def _adam_all(big_in, big_out, gsum, small, grad_x):
    n_chunks = 4
    n_big = 8

    def body(*refs):
        ins, outs = refs[:n_big + 1 + 18 + 1], refs[n_big + 1 + 18 + 1:n_big + 1 + 18 + 1 + 34]
        in_bufs, out_bufs, gx_buf = refs[-n_big - 6 - 4:-6 - 4], refs[-6 - 4:-4], refs[-4]
        in_sems, out_sems, gx_sems = refs[-3:]

        def gx_rows(j):
            return pl.ds(j * (SEQ // n_chunks), SEQ // n_chunks)

        def gx_load(j):
            return pltpu.make_async_copy(ins[27].at[gx_rows(j), :], gx_buf.at[gx_rows(j), :], gx_sems.at[j])

        def gx_store(j):
            return pltpu.make_async_copy(gx_buf.at[gx_rows(j), :], outs[33].at[gx_rows(j), :], gx_sems.at[n_chunks + j])

        def rows(a, j):
            tr = ins[a].shape[0] // n_chunks
            return pl.ds(j * tr, tr)

        def load(a, j):
            return pltpu.make_async_copy(ins[a].at[rows(a, j), :], in_bufs[a].at[rows(a, j), :], in_sems.at[a * n_chunks + j])

        def store(a, j):
            b, kind = divmod(a, 4)
            src = in_bufs[4 * b + 1] if kind == 0 else out_bufs[3 * b + kind - 1]
            return pltpu.make_async_copy(src.at[rows(a, j), :], outs[a].at[rows(a, j), :], out_sems.at[a * n_chunks + j])

        for j in range(n_chunks):
            for a in range(n_big):
                load(a, j).start()
            gx_load(j).start()

        def small_weights():
            gsum = ins[8][...]
            idx = _slot(lax.axis_index("x"), lax.axis_index("y"), lax.axis_index("c"))
            cg = jnp.zeros((3, SHARD_CONV), F32)
            for d in range(N_DEV):
                cg = jnp.where(idx == d, gsum[ROW_CONV0:ROW_CONV0 + 3, d * SHARD_CONV:(d + 1) * SHARD_CONV], cg)
            grads = (gsum[ROW_NORM_IN:ROW_NORM_IN + 1], gsum[ROW_SINKS:ROW_SINKS + 1, 0:N_Q_HEADS],
                     gsum[ROW_NORM_CONV:ROW_NORM_CONV + 1], gsum[ROW_NORM_ATTN:ROW_NORM_ATTN + 1],
                     gsum[ROW_NORM_FINAL:ROW_NORM_FINAL + 1], cg)
            for s, g in enumerate(grads):
                at = (slice(None), 0, slice(None)) if s == 5 else (slice(None), slice(None))
                w_ref, m_ref, v_ref = ins[9 + 3 * s:12 + 3 * s]
                delta, mn, vn = _adamw(w_ref[at], g, m_ref[at], v_ref[at])
                for ref, val in zip(outs[8 + 4 * s:12 + 4 * s], (g, delta, mn, vn)):
                    ref[at] = val
            outs[32][...] = gsum[ROW_SINKS:ROW_SINKS + 1, LOSS_LANE:LOSS_LANE + 1]

        small_weights()
        for j in range(n_chunks):
            for b in range(2):
                for a in range(4 * b, 4 * b + 4):
                    load(a, j).wait()
                w_buf, g_buf, m_buf, v_buf = in_bufs[4 * b:4 * b + 4]
                r = rows(4 * b, j)
                results = _adamw(w_buf[r, :], g_buf[r, :], m_buf[r, :], v_buf[r, :])
                for buf, val in zip(out_bufs[3 * b:3 * b + 3], results):
                    buf[r, :] = val
                for a in range(4 * b, 4 * b + 4):
                    store(a, j).start()
            gx_load(j).wait()
            gx_store(j).start()
        for j in range(n_chunks):
            for a in range(n_big):
                store(a, j).wait()
            gx_store(j).wait()

    vmem, hbm = pl.BlockSpec(memory_space=pltpu.VMEM), pl.BlockSpec(memory_space=pl.ANY)
    small_shapes = [a.shape for a in small[::3]]
    big_shapes = [(SHARD_IN, D_MODEL)] * 4 + [(SHARD_OUT, D_MODEL)] * 4
    out_shape = ([jax.ShapeDtypeStruct(s, F32) for s in big_shapes]
                 + [jax.ShapeDtypeStruct(s, F32) for s in small_shapes for _ in range(4)]
                 + [jax.ShapeDtypeStruct((1, 1), F32), jax.ShapeDtypeStruct((SEQ, D_MODEL), F32)])
    outs = pl.pallas_call(
        body, name="adam", in_specs=[hbm] * n_big + [vmem] * (1 + len(small)) + [hbm],
        out_specs=tuple([hbm] * n_big + [vmem] * (4 * len(small_shapes) + 1) + [hbm]), out_shape=tuple(out_shape),
        scratch_shapes=[pltpu.VMEM(s, F32) for s in big_shapes]
                       + [pltpu.VMEM(s, F32) for s in [(SHARD_IN, D_MODEL)] * 3 + [(SHARD_OUT, D_MODEL)] * 3]
                       + [pltpu.VMEM((SEQ, D_MODEL), F32),
                          pltpu.SemaphoreType.DMA((n_big * n_chunks,)), pltpu.SemaphoreType.DMA((n_big * n_chunks,)),
                          pltpu.SemaphoreType.DMA((2 * n_chunks,))],
        compiler_params=_params(),
    )(*big_in, *big_out, gsum, *small, grad_x)
    return outs[0:4], outs[4:8], [outs[8 + 4 * s:12 + 4 * s] for s in range(6)], outs[32], outs[33]


def _rows_first(a):
    return jnp.transpose(a, (1, 0, 2))


def kernel(x, norm_in, w_in, conv_w, attn_sinks, norm_conv_out, norm_attn_out, w_out, norm_final, loss_target, m_norm_in, m_w_in, m_conv_w, m_attn_sinks, m_norm_conv_out, m_norm_attn_out, m_w_out, m_norm_final, v_norm_in, v_w_in, v_conv_w, v_attn_sinks, v_norm_conv_out, v_norm_attn_out, v_w_out, v_norm_final):
    x2d = x.reshape(SEQ, D_MODEL)
    target = loss_target.reshape(SEQ, D_MODEL)
    nf = norm_final.reshape(1, D_MODEL)

    w_in_t, m_w_in_t, v_w_in_t = w_in[0].T, m_w_in[0].T, v_w_in[0].T
    tiles = jnp.asarray(TILE_ORDER, jnp.int32).reshape(-1)
    w_in_full, h, proj, g_out, conv_full = _gather_in_proj(x2d, norm_in, w_in_t, w_out[0], _rows_first(conv_w), tiles)
    sinks = attn_sinks.reshape(N_Q_HEADS)

    mixed, attn, probs, shares = _mix_fwd(proj, conv_full, sinks, norm_conv_out, norm_attn_out)
    dx2, dx2b, dmixed, gnf, loss_part = _out_proj_loss(mixed, x2d, target, g_out.reshape(D_MIX, D_MODEL), nf)
    dproj, gslab = _mix_bwd(proj, dmixed, attn, probs, shares, conv_full, norm_conv_out, norm_attn_out)
    dw_in_chip, g_w_out = _dw_rs(mixed, dx2b, dproj, h, jnp.asarray(DW_TABLE, jnp.int32).reshape(-1))
    grad_x, g_w_in, gsum = _in_bwd_rs(dproj, w_in_full, x2d, dx2, norm_in, dw_in_chip, gslab, gnf, loss_part)

    small = (norm_in, m_norm_in, v_norm_in, attn_sinks, m_attn_sinks, v_attn_sinks,
             norm_conv_out, m_norm_conv_out, v_norm_conv_out, norm_attn_out, m_norm_attn_out, v_norm_attn_out,
             nf, m_norm_final.reshape(1, D_MODEL), v_norm_final.reshape(1, D_MODEL),
             _rows_first(conv_w), _rows_first(m_conv_w), _rows_first(v_conv_w))
    big_in, big_out, (s_ni, s_sk, s_nc, s_na, s_nf, s_cv), loss, grad_x = _adam_all(
        (w_in_t, g_w_in, m_w_in_t, v_w_in_t), (w_out[0], g_w_out, m_w_out[0], v_w_out[0]), gsum, small, grad_x)

    def leaves(k):
        return (s_ni[k], big_in[k].T[None], jnp.transpose(s_cv[k], (1, 0, 2)), s_sk[k], s_nc[k], s_na[k], big_out[k][None],
                s_nf[k].reshape(D_MODEL))

    return (loss.reshape(()), grad_x.reshape(1, SEQ, D_MODEL), *leaves(0), *leaves(1), *leaves(2), *leaves(3))
```

```python
import jax
import jax.numpy as jnp
from jax import lax
from jax.experimental import pallas as pl
from jax.experimental.pallas import tpu as pltpu

F32 = jnp.float32
BF16 = jnp.bfloat16
MESH = pl.DeviceIdType.MESH

N_DEV = 8
SEQ = 2048
D_MODEL = 1024
D_CONV = 1024
D_ATTN = 1024
D_KV = 128
HEAD_DIM = 64
N_Q_HEADS = 16
N_PAIRS = N_Q_HEADS // 2
PAIRS_PER_KV = N_PAIRS // 2
D_MIX = D_CONV + D_ATTN
D_PROJ = 6400
SHARD_IN = D_PROJ // N_DEV
SHARD_OUT = D_MIX // N_DEV
SHARD_CONV = D_CONV // N_DEV
OFF_CB, OFF_CC, OFF_CU, OFF_GC, OFF_Q, OFF_K, OFF_V, OFF_GA = 0, 1024, 2048, 3072, 4096, 5120, 5248, 5376
BLOCK = 128
N_BLOCKS = SEQ // BLOCK
HALO = 8
CHUNK = 16
N_CHUNKS = BLOCK // CHUNK
RMS_EPS = 1e-5
NEG = -1e30
SCALE = HEAD_DIM ** -0.5
SLOPES = tuple(2.0 ** (-8.0 * (h + 1) / N_Q_HEADS) for h in range(N_Q_HEADS))

ADAM_LR = 0.001
ADAM_B1 = 0.9
ADAM_B2 = 0.999
ADAM_EPS = 1e-08
ADAM_WD = 0.01
ADAM_STEP = 10

ROW_NORM_IN, ROW_NORM_CONV, ROW_NORM_ATTN, ROW_NORM_FINAL, ROW_CONV0, ROW_SINKS = 0, 1, 2, 3, 4, 7
LOSS_LANE = N_Q_HEADS
ACC_NORM_CONV, ACC_NORM_ATTN, ACC_CONV0, N_ACC = 0, 1, 2, 5

VMEM_LIMIT = 56 * 1024 * 1024

_NT = (((1,), (1,)), ((), ()))
_TN = (((0,), (0,)), ((), ()))


def _params(**kw):
    return pltpu.CompilerParams(vmem_limit_bytes=VMEM_LIMIT, **kw)


def _adamw(w, g, m, v):
    m = ADAM_B1 * m + (1.0 - ADAM_B1) * g
    v = ADAM_B2 * v + (1.0 - ADAM_B2) * (g * g)
    m_hat = m / (1.0 - ADAM_B1 ** ADAM_STEP)
    v_hat = v / (1.0 - ADAM_B2 ** ADAM_STEP)
    delta = -ADAM_LR * (m_hat / (jnp.sqrt(v_hat) + ADAM_EPS) + ADAM_WD * w)
    return delta, m, v


def _sigmoid(t):
    return 1.0 / (1.0 + jnp.exp(-t))


def _slot(px, py, pc):
    return 4 * px + 2 * py + pc


OWN, NX, NY, DG = range(4)
HALF_IN = SHARD_IN // 2
N_GATHER_KINDS = 13
W_OUT_KINDS = N_GATHER_KINDS + 7


IN_PROJ_TILE = 640
TILE_ORDER = ((0, 1, 2, 3, 4, 5, 6, 7, 8, 9), (3, 4, 0, 1, 2, 8, 9, 5, 6, 7),
              (5, 6, 0, 1, 7, 8, 9, 2, 3, 4), (8, 9, 3, 4, 5, 6, 7, 0, 1, 2))
TILES_OWN, TILES_NEIGHBOURS = 2, 7


def _tile(table_ref, p):
    chip = 2 * lax.axis_index("x") + lax.axis_index("y")
    return table_ref[chip * len(TILE_ORDER[0]) + p]


DW_TILE_ORDER = tuple(tuple(reversed(row)) for row in TILE_ORDER)


def _tiles_until_complete(chip, owner):
    lo, hi = owner * 2 * SHARD_IN, (owner + 1) * 2 * SHARD_IN
    touching = [t for t in range(len(TILE_ORDER[0])) if t * IN_PROJ_TILE < hi and (t + 1) * IN_PROJ_TILE > lo]
    return 1 + max(DW_TILE_ORDER[chip].index(t) for t in touching)


DW_TABLE = tuple(DW_TILE_ORDER[chip] + tuple(_tiles_until_complete(chip, chip ^ flip) for flip in (0, 2, 1, 3))
                 for chip in range(4))


def _dw_entry(table_ref, p):
    chip = 2 * lax.axis_index("x") + lax.axis_index("y")
    return table_ref[chip * len(DW_TABLE[0]) + p]


def _gather_in_proj(x, norm_in, w_in_sh, w_out_sh, conv_sh, tiles):
    tn = IN_PROJ_TILE
    steps = D_PROJ // tn
    tm = 256

    def body(tiles_ref, x_hbm, g_ref, win_ref, wout_ref, cv_ref, wt_ref, h_ref, proj_ref, gout_ref, conv_ref,
             gin_ref, gcv_ref, wob_ref, x_ref, send_sems, recv_sems, local_sems):
        p = pl.program_id(0)
        local_sem = local_sems.at[0]
        x, y, c = lax.axis_index("x"), lax.axis_index("y"), lax.axis_index("c")
        me, sibling = (x, y, c), (x, y, 1 - c)
        nx, ny, dg = (1 - x, y, c), (x, 1 - y, c), (1 - x, 1 - y, c)

        def other(dev):
            return (dev[0], dev[1], 1 - dev[2])

        def shard(dev):
            return gin_ref.at[pl.ds(pl.multiple_of(_slot(*dev) * SHARD_IN, 16), SHARD_IN), :]

        def half(dev, h):
            return gin_ref.at[pl.ds(pl.multiple_of(_slot(*dev) * SHARD_IN + h * HALF_IN, 16), HALF_IN), :]

        def rc(ref, k, to):
            return pltpu.make_async_remote_copy(src_ref=ref, dst_ref=ref, send_sem=send_sems.at[k],
                                                recv_sem=recv_sems.at[k], device_id=to, device_id_type=MESH)

        def cv(k, dev, to):
            s = _slot(*dev)
            return pltpu.make_async_remote_copy(src_ref=gcv_ref.at[s], dst_ref=gcv_ref.at[s],
                                                send_sem=send_sems.at[N_GATHER_KINDS + k],
                                                recv_sem=recv_sems.at[N_GATHER_KINDS + k], device_id=to, device_id_type=MESH)

        def own_copies():
            return [rc(shard(me), 0, sibling),
                    rc(half(me, 0), 1, nx), rc(half(me, 1), 2, nx),
                    rc(half(me, 1), 4, ny), rc(half(me, 0), 3, ny),
                    cv(0, me, sibling)] + [cv(1 + j, me, peer) for j, peer in enumerate((nx, ny, dg))]

        def pass_on(dev, h, k_in, k_ici, k_d2d, half=half, base=0):
            rc(half(dev, h), base + k_in, me).wait_recv()
            if k_ici is not None:
                rc(half(dev, h), base + k_ici, ny if dev is nx else nx).start()
            rc(half(dev, h), base + k_d2d, sibling).start()

        def out_half(dev, h):
            return gout_ref.at[_slot(*dev), pl.ds(h * (SHARD_OUT // 2), SHARD_OUT // 2), :]

        def own_out_copies():
            src = lambda h: wob_ref.at[pl.ds(h * (SHARD_OUT // 2), SHARD_OUT // 2), :]

            def send(ref, dst, k, to):
                return pltpu.make_async_remote_copy(src_ref=ref, dst_ref=dst, send_sem=send_sems.at[W_OUT_KINDS + k],
                                                    recv_sem=recv_sems.at[W_OUT_KINDS + k], device_id=to, device_id_type=MESH)

            return [send(wob_ref, gout_ref.at[_slot(*me)], 0, sibling),
                    send(src(0), out_half(me, 0), 1, nx), send(src(1), out_half(me, 1), 2, nx),
                    send(src(1), out_half(me, 1), 4, ny), send(src(0), out_half(me, 0), 3, ny)]

        def own_out_local():
            return pltpu.make_async_copy(wob_ref, gout_ref.at[_slot(*me)], local_sems.at[1])

        @pl.when(p == 0)
        def _():
            gin_ref[pl.ds(pl.multiple_of(_slot(*me) * SHARD_IN, 16), SHARD_IN), :] = win_ref[...].astype(BF16)
            gcv_ref[_slot(*me)] = jnp.zeros((8, SHARD_CONV), F32)
            gcv_ref[_slot(*me), 0:3, :] = cv_ref[:, 0, :]
            for cp in own_copies():
                cp.start()
            wob_ref[...] = wout_ref[...].astype(BF16)
            x_load = pltpu.make_async_copy(x_hbm, x_ref, local_sems.at[2])
            x_load.start()
            x_load.wait()
            for t in range(SEQ // tm):
                xv = x_ref[tm * t:tm * (t + 1), :]
                r = lax.rsqrt(jnp.mean(xv * xv, axis=-1, keepdims=True) + RMS_EPS)
                h_ref[tm * t:tm * (t + 1), :] = (xv * r * g_ref[...]).astype(BF16)
            rc(shard(sibling), 0, me).wait_recv()

        @pl.when(p == TILES_OWN)
        def _():
            for args in ((nx, 0, 1, 5, 7), (ny, 1, 4, 6, 10), (nx, 1, 2, None, 8), (ny, 0, 3, None, 9)):
                pass_on(*args)
            for j, peer in enumerate((nx, ny, dg)):
                cv(1 + j, peer, me).wait_recv()
                cv(4 + j, peer, sibling).start()
            for (dev, h), k in (((nx, 0), 7), ((nx, 1), 8), ((ny, 0), 9), ((ny, 1), 10)):
                rc(half(other(dev), h), k, me).wait_recv()
            own_out_local().start()
            for cp in own_out_copies():
                cp.start()

        @pl.when(p == TILES_NEIGHBOURS - 1)
        def _():
            pass_on(dg, 0, 5, None, 11)
            pass_on(dg, 1, 6, None, 12)

        @pl.when(p == TILES_NEIGHBOURS)
        def _():
            for (dev, h), k in (((dg, 0), 11), ((dg, 1), 12)):
                rc(half(other(dev), h), k, me).wait_recv()
            pltpu.make_async_copy(gin_ref, wt_ref, local_sem).start()

        @pl.when(p == steps - 2)
        def _():
            for args in ((nx, 0, 1, 5, 7), (ny, 1, 4, 6, 10), (nx, 1, 2, None, 8), (ny, 0, 3, None, 9)):
                pass_on(*args, half=out_half, base=W_OUT_KINDS)

        w = gin_ref[pl.ds(pl.multiple_of(_tile(tiles_ref, p) * tn, tn), tn), :]
        proj_ref[...] = lax.dot_general(h_ref[...], w, _NT, preferred_element_type=F32)

        @pl.when(p == steps - 1)
        def _():
            cv(0, sibling, me).wait_recv()
            for j, peer in enumerate((nx, ny, dg)):
                cv(4 + j, other(peer), me).wait_recv()
            for d in range(N_DEV):
                conv_ref[:, d * SHARD_CONV:(d + 1) * SHARD_CONV] = gcv_ref[d]
            relayed = [rc(half(nx, 0), 5, ny), rc(half(ny, 1), 6, nx)]
            relayed += [rc(half(dev, h), k, sibling) for (dev, h), k in
                        (((nx, 0), 7), ((nx, 1), 8), ((ny, 0), 9), ((ny, 1), 10), ((dg, 0), 11), ((dg, 1), 12))]
            relayed += [cv(4 + j, peer, sibling) for j, peer in enumerate((nx, ny, dg))]
            for cp in own_copies() + relayed:
                cp.wait_send()
            pltpu.make_async_copy(gin_ref, wt_ref, local_sem).wait()
            pass_on(dg, 0, 5, None, 11, half=out_half, base=W_OUT_KINDS)
            pass_on(dg, 1, 6, None, 12, half=out_half, base=W_OUT_KINDS)
            rc(gout_ref.at[_slot(*sibling)], W_OUT_KINDS, me).wait_recv()
            out_relayed = [rc(out_half(nx, 0), W_OUT_KINDS + 5, ny), rc(out_half(ny, 1), W_OUT_KINDS + 6, nx)]
            for (dev, h), k in (((nx, 0), 7), ((nx, 1), 8), ((ny, 0), 9), ((ny, 1), 10), ((dg, 0), 11), ((dg, 1), 12)):
                rc(out_half(other(dev), h), W_OUT_KINDS + k, me).wait_recv()
                out_relayed.append(rc(out_half(dev, h), W_OUT_KINDS + k, sibling))
            for cp in own_out_copies() + out_relayed:
                cp.wait_send()
            own_out_local().wait()

    vmem = pl.BlockSpec(memory_space=pltpu.VMEM)
    grid_spec = pltpu.PrefetchScalarGridSpec(
        num_scalar_prefetch=1, grid=(steps,),
        in_specs=[pl.BlockSpec(memory_space=pl.ANY), vmem, vmem, vmem, vmem],
        out_specs=(pl.BlockSpec(memory_space=pl.ANY), vmem,
                   pl.BlockSpec((SEQ, tn), lambda p, tiles_ref: (0, _tile(tiles_ref, p))),
                   pl.BlockSpec(memory_space=pl.ANY), vmem),
        scratch_shapes=[pltpu.VMEM((D_PROJ, D_MODEL), BF16), pltpu.VMEM((N_DEV, 8, SHARD_CONV), F32),
                        pltpu.VMEM((SHARD_OUT, D_MODEL), BF16), pltpu.VMEM((SEQ, D_MODEL), F32),
                        pltpu.SemaphoreType.DMA((W_OUT_KINDS + N_GATHER_KINDS,)),
                        pltpu.SemaphoreType.DMA((W_OUT_KINDS + N_GATHER_KINDS,)),
                        pltpu.SemaphoreType.DMA((3,))])
    return pl.pallas_call(
        body, name="gather_in_proj", grid_spec=grid_spec,
        out_shape=(jax.ShapeDtypeStruct((D_PROJ, D_MODEL), BF16), jax.ShapeDtypeStruct((SEQ, D_MODEL), BF16),
                   jax.ShapeDtypeStruct((SEQ, D_PROJ), F32), jax.ShapeDtypeStruct((N_DEV, SHARD_OUT, D_MODEL), BF16),
                   jax.ShapeDtypeStruct((8, D_CONV), F32)),
        compiler_params=_params(dimension_semantics=("arbitrary",)),
    )(tiles, x, norm_in, w_in_sh, w_out_sh, conv_sh)


def _shard_sum(src, own, d2d, ici, send_sems, recv_sems, local_sems, base=0):
    x, y, c = lax.axis_index("x"), lax.axis_index("y"), lax.axis_index("c")
    sibling = (x, y, 1 - c)
    chips = [(x, y), (1 - x, y), (x, 1 - y), (1 - x, 1 - y)]

    def rcopy(s, d, k, to):
        return pltpu.make_async_remote_copy(src_ref=s, dst_ref=d, send_sem=send_sems.at[base + k],
                                            recv_sem=recv_sems.at[base + k], device_id=to, device_id_type=MESH)

    def mine(k):
        return pltpu.make_async_copy(src.at[_slot(*chips[k], c)], own.at[k], local_sems.at[k])

    def to_sibling(k):
        return rcopy(src.at[_slot(*chips[k], 1 - c)], d2d.at[k], k, sibling)

    def to_chip(k):
        return rcopy(own.at[k], ici.at[k - 1], 3 + k, (*chips[k], c))

    def start(k):
        mine(k).start()
        to_sibling(k).start()

    def forward(k):
        mine(k).wait()
        to_sibling(k).wait_recv()
        own[k] = (own[k].astype(F32) + d2d[k].astype(F32)).astype(BF16)
        to_chip(k).start()

    def finish():
        mine(0).wait()
        to_sibling(0).wait_recv()
        acc = own[0].astype(F32) + d2d[0].astype(F32)
        for k in range(1, 4):
            to_chip(k).wait_recv()
            acc = acc + ici[k - 1].astype(F32)
        for k in range(4):
            to_sibling(k).wait_send()
        for k in range(1, 4):
            to_chip(k).wait_send()
        return acc

    return start, forward, finish


def _shard_sum_scratch(rows):
    return [pltpu.VMEM((4, rows, D_MODEL), BF16), pltpu.VMEM((4, rows, D_MODEL), BF16),
            pltpu.VMEM((3, rows, D_MODEL), BF16)]


N_SHARD_SUM_SEMS = 7


N_CHIP_SUM_SEMS = 5


def _chip_sum(dwt, d2d, via, out_hbm, tiles_until, send_sems, recv_sems, local_sems, base, local_base):
    x, y, c = lax.axis_index("x"), lax.axis_index("y"), lax.axis_index("c")
    sibling, nx, ny = (x, y, 1 - c), (1 - x, y, c), (x, 1 - y, c)
    chips = [(x, y), (1 - x, y), (x, 1 - y), (1 - x, 1 - y)]

    def shard(s):
        return dwt.at[pl.ds(pl.multiple_of(s * SHARD_IN, 16), SHARD_IN), :]

    def half(ref, h):
        return ref.at[pl.ds(h * HALF_IN, HALF_IN), :]

    def rc(s, d, k, to):
        return pltpu.make_async_remote_copy(src_ref=s, dst_ref=d, send_sem=send_sems.at[base + k],
                                            recv_sem=recv_sems.at[base + k], device_id=to, device_id_type=MESH)

    def to_sibling(k):
        return rc(shard(_slot(*chips[k], 1 - c)), d2d.at[k - 1], k - 1, sibling)

    for_dg = (lambda: rc(half(d2d.at[DG - 1], 0), via.at[0], 3, nx), lambda: rc(half(d2d.at[DG - 1], 1), via.at[1], 4, ny))

    def save(k):
        return pltpu.make_async_copy(d2d.at[k - 1], out_hbm.at[k], local_sems.at[local_base + k])

    own_saves = (lambda: pltpu.make_async_copy(shard(_slot(x, y, c)), out_hbm.at[OWN], local_sems.at[local_base]),
                 lambda: pltpu.make_async_copy(shard(_slot(x, y, 1 - c)), out_hbm.at[3], local_sems.at[local_base + 3]))

    def before_tile(n):
        for k in (NX, NY, DG):
            @pl.when(tiles_until(k) == n)
            def _():
                to_sibling(k).start()

            @pl.when(tiles_until(k) + 1 == n)
            def _():
                to_sibling(k).wait_recv()
                d2d[k - 1] = (shard(_slot(*chips[k], c))[...].astype(F32) + d2d[k - 1].astype(F32)).astype(BF16)
                if k == DG:
                    for cp in for_dg:
                        cp().start()

    def after_tiles():
        for cp in own_saves:
            cp().start()

    def finish():
        for k, h in ((NY, 0), (NX, 1)):
            for_dg[h]().wait_recv()
            rows = pl.ds(h * HALF_IN, HALF_IN)
            d2d[k - 1, rows, :] = (d2d[k - 1, rows, :].astype(F32) + via[h].astype(F32)).astype(BF16)
            save(k).start()
        for cp in own_saves + (lambda: save(NX), lambda: save(NY)):
            cp().wait()
        for cp in (lambda: to_sibling(NX), lambda: to_sibling(NY), lambda: to_sibling(DG)) + for_dg:
            cp().wait_send()

    return before_tile, after_tiles, finish


N_ICI_SUM_SEMS = 3


def _ici_sum(src, own, d2d, ici, send_sems, recv_sems, local_sems, base=0):
    x, y, c = lax.axis_index("x"), lax.axis_index("y"), lax.axis_index("c")

    def rc(s, d, k, to):
        return pltpu.make_async_remote_copy(src_ref=s, dst_ref=d, send_sem=send_sems.at[base + k],
                                            recv_sem=recv_sems.at[base + k], device_id=to, device_id_type=MESH)

    copies = (lambda: rc(src.at[NX], ici.at[0], 0, (1 - x, y, c)), lambda: rc(src.at[NY], ici.at[1], 1, (x, 1 - y, c)),
              lambda: rc(src.at[3], d2d, 2, (x, y, 1 - c)))
    mine = lambda: pltpu.make_async_copy(src.at[OWN], own, local_sems.at[0])

    def start():
        for cp in copies + (mine,):
            cp().start()

    def finish():
        mine().wait()
        for cp in copies:
            cp().wait_recv()
        acc = own[...].astype(F32) + d2d[...].astype(F32) + ici[0].astype(F32) + ici[1].astype(F32)
        for cp in copies:
            cp().wait_send()
        return acc

    return start, finish


def _slab_sum(myslab, slabs, send_sems, recv_sems, base):
    x, y, c = lax.axis_index("x"), lax.axis_index("y"), lax.axis_index("c")
    me = _slot(x, y, c)
    peers = [(x, y, 1 - c), (1 - x, y, c), (x, 1 - y, c), (1 - x, 1 - y, c),
             (1 - x, y, 1 - c), (x, 1 - y, 1 - c), (1 - x, 1 - y, 1 - c)]

    def cp(k):
        return pltpu.make_async_remote_copy(src_ref=myslab, dst_ref=slabs.at[me], send_sem=send_sems.at[base + k],
                                            recv_sem=recv_sems.at[base + k], device_id=peers[k], device_id_type=MESH)

    def start():
        slabs[me] = myslab[...]
        for k in range(7):
            cp(k).start()

    def finish():
        for k in range(7):
            cp(k).wait_recv()
        total = slabs[0]
        for d in range(1, N_DEV):
            total = total + slabs[d]
        for k in range(7):
            cp(k).wait_send()
        return total

    return start, finish


def _chunk_rows(r):
    return slice(r * CHUNK, (r + 1) * CHUNK)


def _conv_halo(cch_ref, cuh_ref, n):
    zh = jnp.where(n > 0, cch_ref[...] * cuh_ref[...], 0.0)
    return jnp.concatenate([zh] * (CHUNK // HALO), axis=0)


def _conv_chunk(pj_ref, zhalo, cw, r):
    rows = _chunk_rows(r)
    cc = pj_ref[rows, OFF_CC:OFF_CC + D_CONV]
    cu = pj_ref[rows, OFF_CU:OFF_CU + D_CONV]
    z = cc * cu
    before = _chunk_rows(r - 1)
    zprev = pj_ref[before, OFF_CC:OFF_CC + D_CONV] * pj_ref[before, OFF_CU:OFF_CU + D_CONV] if r > 0 else zhalo
    row = lax.broadcasted_iota(jnp.int32, (CHUNK, D_CONV), 0)
    z1 = jnp.where(row < 1, pltpu.roll(zprev, 1, 0), pltpu.roll(z, 1, 0))
    z2 = jnp.where(row < 2, pltpu.roll(zprev, 2, 0), pltpu.roll(z, 2, 0))
    co = cw[0] * z2 + cw[1] * z1 + cw[2] * z
    return cc, cu, z, z1, z2, co


def _gated_norm(a, gain, t):
    r = lax.rsqrt(jnp.mean(a * a, axis=-1, keepdims=True) + RMS_EPS)
    return a * r * gain * (t * _sigmoid(t))


def _kv_bands(pj, kvp_ref):
    lane = lax.broadcasted_iota(jnp.int32, (2 * BLOCK, D_KV), 1)
    lo = lane < HEAD_DIM

    def bands(prev, cur):
        b = jnp.concatenate([prev, cur], axis=0)
        br = pltpu.roll(b, HEAD_DIM, 1)
        zero = jnp.zeros_like(b)
        return ((jnp.where(lo, b, zero).astype(BF16), jnp.where(lo, zero, br).astype(BF16)),
                (jnp.where(lo, br, zero).astype(BF16), jnp.where(lo, zero, b).astype(BF16)))

    ks = bands(kvp_ref[:, 0:D_KV], pj[:, OFF_K:OFF_K + D_KV])
    vs = bands(kvp_ref[:, D_KV:2 * D_KV], pj[:, OFF_V:OFF_V + D_KV])
    return ks, vs


STACK = PAIRS_PER_KV * BLOCK


def _head(j, i, e):
    return 2 * (PAIRS_PER_KV * j + i) + e


def _pair_cols(j, i, off):
    p = PAIRS_PER_KV * j + i
    return slice(off + 128 * p, off + 128 * (p + 1))


def _fill_attn_bias(bias_scr, first_block):
    qi = lax.broadcasted_iota(jnp.int32, (BLOCK, 2 * BLOCK), 0)
    kj = lax.broadcasted_iota(jnp.int32, (BLOCK, 2 * BLOCK), 1)
    dist = BLOCK + qi - kj
    valid = (dist >= 0) & (dist < BLOCK)
    if first_block:
        valid = valid & (kj >= BLOCK)
    distf = dist.astype(F32)
    for j in range(2):
        for e in range(2):
            for i in range(PAIRS_PER_KV):
                bias_scr[2 * j + e, BLOCK * i:BLOCK * (i + 1), :] = jnp.where(valid, -SLOPES[_head(j, i, e)] * distf, NEG)


def _q_stack(pj, j):
    return jnp.concatenate([(pj[:, _pair_cols(j, i, OFF_Q)] * SCALE).astype(BF16) for i in range(PAIRS_PER_KV)], axis=0)


def _attn_probs(q_stack, kband, bias_ref, sinks):
    s = lax.dot_general(q_stack, kband, _NT, preferred_element_type=F32)
    ones = jnp.ones((128, 128), BF16)
    probs, shares = [], []
    for i, sink in enumerate(sinks):
        rows = slice(BLOCK * i, BLOCK * (i + 1))
        t = s[rows, :] + bias_ref[rows, :]
        m = jnp.broadcast_to(jnp.max(t, axis=-1, keepdims=True), (BLOCK, 128))
        m = jnp.maximum(m, sink)
        p = [jnp.exp(t[:, :128] - m), jnp.exp(t[:, 128:] - m)]
        es = jnp.exp(sink - m)
        total = (jnp.dot(p[0].astype(BF16), ones, preferred_element_type=F32)
                 + jnp.dot(p[1].astype(BF16), ones, preferred_element_type=F32))
        inv = 1.0 / (total + es)
        probs.append(jnp.concatenate([p[0] * inv, p[1] * inv], axis=1))
        shares.append(es * inv)
    return jnp.concatenate(probs, axis=0), jnp.concatenate(shares, axis=0)


def _attn_group(pj, ks, vs, bias_scr, sink_ref, j):
    q_stack = _q_stack(pj, j)
    out, probs, shares = None, [], []
    for e in range(2):
        p, ps = _attn_probs(q_stack, ks[j][e], bias_scr.at[2 * j + e],
                            [sink_ref[_head(j, i, e)] for i in range(PAIRS_PER_KV)])
        p = p.astype(BF16)
        o = jnp.dot(p, vs[j][e], preferred_element_type=F32)
        out = o if out is None else out + o
        probs.append(p)
        shares.append(ps)
    return out, probs, shares


def _mix_fwd(proj, conv_full, sinks, norm_conv, norm_attn):
    def body(pj_ref, kvp_ref, cch_ref, cuh_ref, cw_ref, sink_ref, gc_ref, ga_ref,
             mixed_ref, attn_scr, p_ref, ps_ref, bias_scr):
        n = pl.program_id(0)
        pj = pj_ref

        @pl.when(n == 0)
        def _():
            _fill_attn_bias(bias_scr, first_block=True)

        @pl.when(n == 1)
        def _():
            _fill_attn_bias(bias_scr, first_block=False)

        zhalo = _conv_halo(cch_ref, cuh_ref, n)
        cw = (cw_ref[0:1, :], cw_ref[1:2, :], cw_ref[2:3, :])
        gain_c = gc_ref[...]

        for r in range(N_CHUNKS):
            rows = _chunk_rows(r)
            co = _conv_chunk(pj_ref, zhalo, cw, r)[-1]
            y = _gated_norm(pj_ref[rows, OFF_CB:OFF_CB + D_CONV] * co, gain_c, pj_ref[rows, OFF_GC:OFF_GC + D_CONV])
            mixed_ref[rows, 0:D_CONV] = y.astype(BF16)

        ks, vs = _kv_bands(pj, kvp_ref)
        for j in range(2):
            out, probs, shares = _attn_group(pj, ks, vs, bias_scr, sink_ref, j)
            for e in range(2):
                p_ref[0, 2 * j + e] = probs[e]
                ps_ref[0, 2 * j + e] = shares[e]
            for i in range(PAIRS_PER_KV):
                attn_scr[:, _pair_cols(j, i, 0)] = out[BLOCK * i:BLOCK * (i + 1), :]
        gain_a = ga_ref[...]

        for r in range(N_CHUNKS):
            rows = _chunk_rows(r)
            y = _gated_norm(attn_scr[rows, :], gain_a, pj_ref[rows, OFF_GA:OFF_GA + D_ATTN])
            mixed_ref[rows, D_CONV:D_MIX] = y.astype(BF16)

    per_block = BLOCK // HALO
    return pl.pallas_call(
        body, name="mix_fwd", grid=(N_BLOCKS,),
        in_specs=[
            pl.BlockSpec((BLOCK, D_PROJ), lambda n: (n, 0)),
            pl.BlockSpec((BLOCK, 2 * D_KV), lambda n: (jnp.maximum(n - 1, 0), OFF_K // (2 * D_KV))),
            pl.BlockSpec((HALO, D_CONV), lambda n: (jnp.maximum(n * per_block - 1, 0), OFF_CC // D_CONV)),
            pl.BlockSpec((HALO, D_CONV), lambda n: (jnp.maximum(n * per_block - 1, 0), OFF_CU // D_CONV)),
            pl.BlockSpec((8, D_CONV), lambda n: (0, 0)),
            pl.BlockSpec(memory_space=pltpu.SMEM),
            pl.BlockSpec((1, D_CONV), lambda n: (0, 0)),
            pl.BlockSpec((1, D_ATTN), lambda n: (0, 0)),
        ],
        out_specs=(pl.BlockSpec((BLOCK, D_MIX), lambda n: (n, 0)), pl.BlockSpec((BLOCK, D_ATTN), lambda n: (n, 0)),
                   pl.BlockSpec((1, 4, STACK, 2 * BLOCK), lambda n: (n, 0, 0, 0)),
                   pl.BlockSpec((1, 4, STACK, 128), lambda n: (n, 0, 0, 0))),
        out_shape=(jax.ShapeDtypeStruct((SEQ, D_MIX), BF16), jax.ShapeDtypeStruct((SEQ, D_ATTN), F32),
                   jax.ShapeDtypeStruct((N_BLOCKS, 4, STACK, 2 * BLOCK), BF16),
                   jax.ShapeDtypeStruct((N_BLOCKS, 4, STACK, 128), F32)),
        scratch_shapes=[pltpu.VMEM((4, STACK, 2 * BLOCK), F32)],
        compiler_params=_params(dimension_semantics=("arbitrary",)),
    )(proj, proj, proj, proj, conv_full, sinks, norm_conv, norm_attn)


def _out_proj_loss(mixed, x, target, w_out_full, norm_final):
    tm = 256

    w_chunk = D_MIX // 4

    def body(mx_ref, x_ref, t_ref, w_hbm, g_ref, dx2_ref, dx2b_ref, dmix_ref, gnf_ref, loss_ref, w_ref, x2_scr, w_sems):
        i = pl.program_id(0)

        @pl.when(i == 0)
        def _():
            chunks = [pl.ds(j * w_chunk, w_chunk) for j in range(D_MIX // w_chunk)]
            loads = [pltpu.make_async_copy(w_hbm.at[rows, :], w_ref.at[rows, :], w_sems.at[j]) for j, rows in enumerate(chunks)]
            for cp in loads:
                cp.start()
            acc = x_ref[...]
            for cp, rows in zip(loads, chunks):
                cp.wait()
                acc = acc + jnp.dot(mx_ref[:, rows], w_ref[rows, :], preferred_element_type=F32)
            x2_scr[...] = acc

        @pl.when(i > 0)
        def _():
            x2_scr[...] = x_ref[...] + jnp.dot(mx_ref[...], w_ref[...], preferred_element_type=F32)

        w = w_ref[...]
        x2 = x2_scr[...]
        r = lax.rsqrt(jnp.mean(x2 * x2, axis=-1, keepdims=True) + RMS_EPS)
        xn = x2 * r
        g = g_ref[...]
        err = xn * g - t_ref[...]
        part = 0.5 * jnp.sum(jnp.mean(err * err, axis=-1, keepdims=True), axis=0, keepdims=True)
        dy = err * (1.0 / D_MODEL)
        gnf = jnp.sum(dy * xn, axis=0, keepdims=True)
        u = dy * g
        dx2 = r * (u - xn * jnp.mean(u * xn, axis=-1, keepdims=True))
        dx2_ref[...] = dx2
        dx2b = dx2.astype(BF16)
        dx2b_ref[...] = dx2b
        dmix_ref[...] = lax.dot_general(dx2b, w, _NT, preferred_element_type=F32)

        @pl.when(i == 0)
        def _():
            gnf_ref[...] = jnp.zeros_like(gnf_ref)
            loss_ref[...] = jnp.zeros_like(loss_ref)

        gnf_ref[...] += gnf
        loss_ref[...] += jnp.broadcast_to(part, loss_ref.shape)

    return pl.pallas_call(
        body, name="out_proj_loss", grid=(SEQ // tm,),
        in_specs=[pl.BlockSpec((tm, D_MIX), lambda i: (i, 0)), pl.BlockSpec((tm, D_MODEL), lambda i: (i, 0)),
                  pl.BlockSpec((tm, D_MODEL), lambda i: (i, 0)), pl.BlockSpec(memory_space=pl.ANY),
                  pl.BlockSpec((1, D_MODEL), lambda i: (0, 0))],
        out_specs=(pl.BlockSpec((tm, D_MODEL), lambda i: (i, 0)), pl.BlockSpec((tm, D_MODEL), lambda i: (i, 0)),
                   pl.BlockSpec((tm, D_MIX), lambda i: (i, 0)),
                   pl.BlockSpec((1, D_MODEL), lambda i: (0, 0)), pl.BlockSpec((8, 128), lambda i: (0, 0))),
        out_shape=(jax.ShapeDtypeStruct((SEQ, D_MODEL), F32), jax.ShapeDtypeStruct((SEQ, D_MODEL), BF16),
                   jax.ShapeDtypeStruct((SEQ, D_MIX), F32),
                   jax.ShapeDtypeStruct((1, D_MODEL), F32), jax.ShapeDtypeStruct((8, 128), F32)),
        scratch_shapes=[pltpu.VMEM((D_MIX, D_MODEL), BF16), pltpu.VMEM((tm, D_MODEL), F32),
                        pltpu.SemaphoreType.DMA((D_MIX // w_chunk,))],
        compiler_params=_params(dimension_semantics=("arbitrary",)),
    )(mixed, x, target, w_out_full, norm_final)


def _gated_norm_bwd(a, gain, t, dy):
    r = lax.rsqrt(jnp.mean(a * a, axis=-1, keepdims=True) + RMS_EPS)
    an = a * r
    sg = _sigmoid(t)
    dn = dy * (t * sg)
    dt = dy * (an * gain) * (sg * (1.0 + t * (1.0 - sg)))
    u = dn * gain
    da = r * (u - an * jnp.mean(u * an, axis=-1, keepdims=True))
    return da, dt, dn * an


def _mix_bwd(proj, dmixed, attn, probs, shares, conv_full, norm_conv, norm_attn):
    def body(pj_ref, kvp_ref, cch_ref, cuh_ref, dmx_ref, attn_ref, p_ref, ps_ref, cw_ref, gc_ref, ga_ref,
             dpj_ref, gslab_ref, dattn_scr, nxt_scr, dkv_scr, acc_scr):
        step = pl.program_id(0)
        n = N_BLOCKS - 1 - step
        pj = pj_ref

        @pl.when(step == 0)
        def _():
            gslab_ref[...] = jnp.zeros_like(gslab_ref)
            nxt_scr[...] = jnp.zeros_like(nxt_scr)
            dkv_scr[...] = jnp.zeros_like(dkv_scr)
            acc_scr[...] = jnp.zeros_like(acc_scr)

        zhalo = _conv_halo(cch_ref, cuh_ref, n)
        cw = (cw_ref[0:1, :], cw_ref[1:2, :], cw_ref[2:3, :])
        gain_c = gc_ref[...]
        row = lax.broadcasted_iota(jnp.int32, (CHUNK, D_CONV), 0)

        dco_after = nxt_scr[...]
        for r in reversed(range(N_CHUNKS)):
            rows = _chunk_rows(r)
            cc, cu, z, z1, z2, co = _conv_chunk(pj_ref, zhalo, cw, r)
            cb = pj_ref[rows, OFF_CB:OFF_CB + D_CONV]
            da, dgate, gterm = _gated_norm_bwd(cb * co, gain_c, pj_ref[rows, OFF_GC:OFF_GC + D_CONV],
                                               dmx_ref[rows, 0:D_CONV])
            dpj_ref[rows, OFF_GC:OFF_GC + D_CONV] = dgate.astype(BF16)
            dpj_ref[rows, OFF_CB:OFF_CB + D_CONV] = (da * co).astype(BF16)
            dco = da * cb
            dco1 = jnp.where(row >= CHUNK - 1, pltpu.roll(dco_after, CHUNK - 1, 0), pltpu.roll(dco, CHUNK - 1, 0))
            dco2 = jnp.where(row >= CHUNK - 2, pltpu.roll(dco_after, CHUNK - 2, 0), pltpu.roll(dco, CHUNK - 2, 0))
            dz = cw[2] * dco + cw[1] * dco1 + cw[0] * dco2
            dpj_ref[rows, OFF_CC:OFF_CC + D_CONV] = (dz * cu).astype(BF16)
            dpj_ref[rows, OFF_CU:OFF_CU + D_CONV] = (dz * cc).astype(BF16)
            acc_scr[ACC_NORM_CONV] += gterm
            acc_scr[ACC_CONV0] += dco * z2
            acc_scr[ACC_CONV0 + 1] += dco * z1
            acc_scr[ACC_CONV0 + 2] += dco * z
            dco_after = dco
        nxt_scr[...] = dco_after

        ks, vs = _kv_bands(pj, kvp_ref)
        gain_a = ga_ref[...]

        for r in range(N_CHUNKS):
            rows = _chunk_rows(r)
            da, dgate, gterm = _gated_norm_bwd(attn_ref[rows, :], gain_a, pj_ref[rows, OFF_GA:OFF_GA + D_ATTN],
                                               dmx_ref[rows, D_CONV:D_MIX])
            dpj_ref[rows, OFF_GA:OFF_GA + D_ATTN] = dgate.astype(BF16)
            dattn_scr[rows, :] = da
            acc_scr[ACC_NORM_ATTN] += gterm

        in_lo = lax.broadcasted_iota(jnp.int32, (128, 128), 0) < HEAD_DIM
        half_ones = (jnp.where(in_lo, 1.0, 0.0).astype(BF16), jnp.where(in_lo, 0.0, 1.0).astype(BF16))
        lane_s = lax.broadcasted_iota(jnp.int32, (1, D_MODEL), 1)
        gsink = jnp.zeros((1, D_MODEL), F32)
        dk_t, dv_t = [], []
        for j in range(2):
            q_stack = _q_stack(pj, j)
            do_f = jnp.concatenate([dattn_scr[:, _pair_cols(j, i, 0)] for i in range(PAIRS_PER_KV)], axis=0)
            o_f = jnp.concatenate([attn_ref[:, _pair_cols(j, i, 0)] for i in range(PAIRS_PER_KV)], axis=0)
            prod = (do_f * o_f).astype(BF16)
            deltas = [jnp.dot(prod, half_ones[e], preferred_element_type=F32) for e in range(2)]
            do_b = do_f.astype(BF16)
            q_t, do_t = q_stack.T, do_b.T
            dq, dk_j, dv_j = None, None, None
            for e in range(2):
                p = p_ref[0, 2 * j + e]
                dp = lax.dot_general(do_b, vs[j][e], _NT, preferred_element_type=F32)
                ds = []
                for i in range(PAIRS_PER_KV):
                    rows = slice(BLOCK * i, BLOCK * (i + 1))
                    delta = deltas[e][rows, :]
                    ds.append((p[rows, :].astype(F32) * (dp[rows, :] - jnp.concatenate([delta, delta], axis=1))).astype(BF16))
                    gs_h = -jnp.sum(ps_ref[0, 2 * j + e, rows, 0:1] * delta[:, 0:1], axis=0, keepdims=True)
                    gsink = gsink + jnp.where(lane_s == _head(j, i, e), gs_h, 0.0)
                ds = jnp.concatenate(ds, axis=0)
                t = jnp.dot(ds, ks[j][e], preferred_element_type=F32)
                dq = t if dq is None else dq + t
                half = slice(HEAD_DIM * e, HEAD_DIM * (e + 1))
                a = jnp.dot(q_t[half, :], ds, preferred_element_type=F32)
                b = jnp.dot(do_t[half, :], p, preferred_element_type=F32)
                dk_j = a if dk_j is None else dk_j + a
                dv_j = b if dv_j is None else dv_j + b
            for i in range(PAIRS_PER_KV):
                dpj_ref[:, _pair_cols(j, i, OFF_Q)] = (dq[BLOCK * i:BLOCK * (i + 1), :] * SCALE).astype(BF16)
            dk_t.append(dk_j)
            dv_t.append(dv_j)
        dk = jnp.concatenate(dk_t, axis=0).T
        dv = jnp.concatenate(dv_t, axis=0).T
        dpj_ref[:, OFF_K:OFF_K + D_KV] = (dk[BLOCK:, :] + dkv_scr[:, 0:D_KV]).astype(BF16)
        dpj_ref[:, OFF_V:OFF_V + D_KV] = (dv[BLOCK:, :] + dkv_scr[:, D_KV:2 * D_KV]).astype(BF16)
        dkv_scr[:, 0:D_KV] = dk[:BLOCK, :]
        dkv_scr[:, D_KV:2 * D_KV] = dv[:BLOCK, :]
        gslab_ref[ROW_SINKS:ROW_SINKS + 1, :] += gsink

        @pl.when(step == N_BLOCKS - 1)
        def _():
            for k, slab_row in ((ACC_NORM_CONV, ROW_NORM_CONV), (ACC_NORM_ATTN, ROW_NORM_ATTN), (ACC_CONV0, ROW_CONV0),
                                (ACC_CONV0 + 1, ROW_CONV0 + 1), (ACC_CONV0 + 2, ROW_CONV0 + 2)):
                gslab_ref[slab_row:slab_row + 1, :] = jnp.sum(acc_scr[k], axis=0, keepdims=True)

    per_block = BLOCK // HALO
    last = N_BLOCKS - 1
    return pl.pallas_call(
        body, name="mix_bwd", grid=(N_BLOCKS,),
        in_specs=[
            pl.BlockSpec((BLOCK, D_PROJ), lambda s: (last - s, 0)),
            pl.BlockSpec((BLOCK, 2 * D_KV), lambda s: (jnp.maximum(last - s - 1, 0), OFF_K // (2 * D_KV))),
            pl.BlockSpec((HALO, D_CONV), lambda s: (jnp.maximum((last - s) * per_block - 1, 0), OFF_CC // D_CONV)),
            pl.BlockSpec((HALO, D_CONV), lambda s: (jnp.maximum((last - s) * per_block - 1, 0), OFF_CU // D_CONV)),
            pl.BlockSpec((BLOCK, D_MIX), lambda s: (last - s, 0)),
            pl.BlockSpec((BLOCK, D_ATTN), lambda s: (last - s, 0)),
            pl.BlockSpec((1, 4, STACK, 2 * BLOCK), lambda s: (last - s, 0, 0, 0)),
            pl.BlockSpec((1, 4, STACK, 128), lambda s: (last - s, 0, 0, 0)),
            pl.BlockSpec((8, D_CONV), lambda s: (0, 0)),
            pl.BlockSpec((1, D_CONV), lambda s: (0, 0)),
            pl.BlockSpec((1, D_ATTN), lambda s: (0, 0)),
        ],
        out_specs=(pl.BlockSpec((BLOCK, D_PROJ), lambda s: (last - s, 0)),
                   pl.BlockSpec((8, D_MODEL), lambda s: (0, 0))),
        out_shape=(jax.ShapeDtypeStruct((SEQ, D_PROJ), BF16), jax.ShapeDtypeStruct((8, D_MODEL), F32)),
        scratch_shapes=[pltpu.VMEM((BLOCK, D_ATTN), F32), pltpu.VMEM((CHUNK, D_CONV), F32),
                        pltpu.VMEM((BLOCK, 2 * D_KV), F32), pltpu.VMEM((N_ACC, CHUNK, D_MODEL), F32)],
        compiler_params=_params(dimension_semantics=("arbitrary",)),
    )(proj, proj, proj, proj, dmixed, attn, probs, shares, conv_full, norm_conv, norm_attn)


def _in_bwd_rs(dproj, w_full, x, dx2, norm_in, dw_in_chip, gslab, gnf, loss_part):
    tm = 256
    steps = SEQ // tm

    def body(dp_ref, w_hbm, x_ref, dx2_ref, g_ref, dwi_ref, gs_ref, gnf_ref, lp_ref, gx_ref, gwin_ref, gsum_ref,
             gni_scr, own, d2d, ici, myslab, slabs, w_ref, send_sems, recv_sems, local_sems):
        i = pl.program_id(0)
        rs_start, rs_finish = _ici_sum(dwi_ref, own, d2d, ici, send_sems, recv_sems, local_sems)
        slab_start, slab_finish = _slab_sum(myslab, slabs, send_sems, recv_sems, N_ICI_SUM_SEMS)

        @pl.when(i == 0)
        def _():
            gni_scr[...] = jnp.zeros_like(gni_scr)
            rs_start()
            w_load = pltpu.make_async_copy(w_hbm, w_ref, local_sems.at[1])
            w_load.start()
            w_load.wait()

        dh = jnp.dot(dp_ref[...], w_ref[...], preferred_element_type=F32)
        xv = x_ref[...]
        r = lax.rsqrt(jnp.mean(xv * xv, axis=-1, keepdims=True) + RMS_EPS)
        xn = xv * r
        u = dh * g_ref[...]
        gx_ref[...] = dx2_ref[...] + r * (u - xn * jnp.mean(u * xn, axis=-1, keepdims=True))
        gni_scr[...] += jnp.sum(dh * xn, axis=0, keepdims=True)

        @pl.when(i == steps - 1)
        def _():
            row = lax.broadcasted_iota(jnp.int32, (8, D_MODEL), 0)
            lane = lax.broadcasted_iota(jnp.int32, (8, D_MODEL), 1)
            slab = jnp.where(row == ROW_NORM_IN, gni_scr[...], jnp.where(row == ROW_NORM_FINAL, gnf_ref[...], gs_ref[...]))
            myslab[...] = jnp.where((row == ROW_SINKS) & (lane == LOSS_LANE), lp_ref[0:1, 0:1], slab)
            slab_start()
            gwin_ref[...] = rs_finish()
            gsum_ref[...] = slab_finish()

    const = lambda i: (0, 0)
    return pl.pallas_call(
        body, name="in_bwd", grid=(steps,),
        in_specs=[pl.BlockSpec((tm, D_PROJ), lambda i: (i, 0)), pl.BlockSpec(memory_space=pl.ANY),
                  pl.BlockSpec((tm, D_MODEL), lambda i: (i, 0)), pl.BlockSpec((tm, D_MODEL), lambda i: (i, 0)),
                  pl.BlockSpec((1, D_MODEL), const), pl.BlockSpec(memory_space=pl.ANY),
                  pl.BlockSpec((8, D_MODEL), const), pl.BlockSpec((1, D_MODEL), const), pl.BlockSpec((8, 128), const)],
        out_specs=(pl.BlockSpec((tm, D_MODEL), lambda i: (i, 0)), pl.BlockSpec((SHARD_IN, D_MODEL), const),
                   pl.BlockSpec((8, D_MODEL), const)),
        out_shape=(jax.ShapeDtypeStruct((SEQ, D_MODEL), F32), jax.ShapeDtypeStruct((SHARD_IN, D_MODEL), F32),
                   jax.ShapeDtypeStruct((8, D_MODEL), F32)),
        scratch_shapes=[pltpu.VMEM((1, D_MODEL), F32), pltpu.VMEM((SHARD_IN, D_MODEL), BF16),
                        pltpu.VMEM((SHARD_IN, D_MODEL), BF16), pltpu.VMEM((2, SHARD_IN, D_MODEL), BF16),
                        pltpu.VMEM((8, D_MODEL), F32), pltpu.VMEM((N_DEV, 8, D_MODEL), F32),
                        pltpu.VMEM((D_PROJ, D_MODEL), BF16),
                        pltpu.SemaphoreType.DMA((N_ICI_SUM_SEMS + 7,)), pltpu.SemaphoreType.DMA((N_ICI_SUM_SEMS + 7,)),
                        pltpu.SemaphoreType.DMA((2,))],
        compiler_params=_params(dimension_semantics=("arbitrary",)),
    )(dproj, w_full, x, dx2, norm_in, dw_in_chip, gslab, gnf, loss_part)


def _dw_rs(mixed, dx2b, dproj, h, table):
    tn_out, tn = 2 * SHARD_OUT, IN_PROJ_TILE
    out_steps, in_steps = D_MIX // tn_out, D_PROJ // tn
    steps = out_steps + in_steps
    out_order = (DG, NX, NY, OWN)

    def out_tile(i):
        chip = 2 * lax.axis_index("x") + lax.axis_index("y")
        return jnp.bitwise_xor(chip, (out_steps - 1) - jnp.minimum(i, out_steps - 1))

    def in_tile(table_ref, i):
        return _dw_entry(table_ref, jnp.maximum(i - out_steps, 0))

    def body(table_ref, mx_ref, dxb_ref, a_ref, h_hbm, chip_ref, gwo_ref, dwo, dwt, d2d_in, via, own, d2d, ici, b_ref,
             send_sems, recv_sems, local_sems):
        i = pl.program_id(0)
        h_load = pltpu.make_async_copy(h_hbm, b_ref, local_sems.at[8])

        @pl.when(i == 0)
        def _():
            h_load.start()

        @pl.when(i == out_steps)
        def _():
            h_load.wait()

        rs_start, rs_forward, rs_finish = _shard_sum(dwo, own, d2d, ici, send_sems, recv_sems, local_sems)
        before_tile, after_tiles, chip_finish = _chip_sum(
            dwt, d2d_in, via, chip_ref, lambda k: _dw_entry(table_ref, in_steps + k), send_sems, recv_sems, local_sems,
            N_SHARD_SUM_SEMS, 4)

        for j, k in enumerate(out_order):
            @pl.when(i == j + 1)
            def _():
                rs_start(k)

            if k != OWN:
                @pl.when(i == j + 2)
                def _():
                    rs_forward(k)

        @pl.when(i < out_steps)
        def _():
            tile = lax.dot_general(mx_ref[...], dxb_ref[...], _TN, preferred_element_type=F32).astype(BF16)
            for core in range(2):
                dwo[2 * out_tile(i) + core] = tile[SHARD_OUT * core:SHARD_OUT * (core + 1), :]

        @pl.when(i >= out_steps)
        def _():
            before_tile(i - out_steps)
            tile = lax.dot_general(a_ref[...], b_ref[...], _TN, preferred_element_type=F32).astype(BF16)
            dwt[pl.ds(pl.multiple_of(in_tile(table_ref, i) * tn, tn), tn), :] = tile

        @pl.when(i == steps - 1)
        def _():
            after_tiles()
            gwo_ref[...] = rs_finish()
            chip_finish()

    vmem = pl.BlockSpec(memory_space=pltpu.VMEM)
    grid_spec = pltpu.PrefetchScalarGridSpec(
        num_scalar_prefetch=1, grid=(steps,),
        in_specs=[pl.BlockSpec((SEQ, tn_out), lambda i, table_ref: (0, out_tile(i))), vmem,
                  pl.BlockSpec((SEQ, tn), lambda i, table_ref: (0, in_tile(table_ref, i))),
                  pl.BlockSpec(memory_space=pl.ANY)],
        out_specs=(pl.BlockSpec(memory_space=pl.ANY), pl.BlockSpec((SHARD_OUT, D_MODEL), lambda i, table_ref: (0, 0))),
        scratch_shapes=[pltpu.VMEM((N_DEV, SHARD_OUT, D_MODEL), BF16),
                        pltpu.VMEM((D_PROJ, D_MODEL), BF16), pltpu.VMEM((3, SHARD_IN, D_MODEL), BF16),
                        pltpu.VMEM((2, HALF_IN, D_MODEL), BF16),
                        *_shard_sum_scratch(SHARD_OUT), pltpu.VMEM((SEQ, D_MODEL), BF16),
                        pltpu.SemaphoreType.DMA((N_SHARD_SUM_SEMS + N_CHIP_SUM_SEMS,)),
                        pltpu.SemaphoreType.DMA((N_SHARD_SUM_SEMS + N_CHIP_SUM_SEMS,)),
                        pltpu.SemaphoreType.DMA((9,))])
    return pl.pallas_call(
        body, name="dw", grid_spec=grid_spec,
        out_shape=(jax.ShapeDtypeStruct((4, SHARD_IN, D_MODEL), BF16), jax.ShapeDtypeStruct((SHARD_OUT, D_MODEL), F32)),
        compiler_params=_params(dimension_semantics=("arbitrary",)),
    )(table, mixed, dx2b, dproj, h)


def _adam_all(big_in, big_out, gsum, small, grad_x):
    n_chunks = 4
    n_big = 8

    def body(*refs):
        ins, outs = refs[:n_big + 1 + 18 + 1], refs[n_big + 1 + 18 + 1:n_big + 1 + 18 + 1 + 34]
        in_bufs, out_bufs, gx_buf = refs[-n_big - 6 - 4:-6 - 4], refs[-6 - 4:-4], refs[-4]
        in_sems, out_sems, gx_sems = refs[-3:]

        def gx_rows(j):
            return pl.ds(j * (SEQ // n_chunks), SEQ // n_chunks)

        def gx_load(j):
            return pltpu.make_async_copy(ins[27].at[gx_rows(j), :], gx_buf.at[gx_rows(j), :], gx_sems.at[j])

        def gx_store(j):
            return pltpu.make_async_copy(gx_buf.at[gx_rows(j), :], outs[33].at[gx_rows(j), :], gx_sems.at[n_chunks + j])

        def rows(a, j):
            tr = ins[a].shape[0] // n_chunks
            return pl.ds(j * tr, tr)

        def load(a, j):
            return pltpu.make_async_copy(ins[a].at[rows(a, j), :], in_bufs[a].at[rows(a, j), :], in_sems.at[a * n_chunks + j])

        def store(a, j):
            b, kind = divmod(a, 4)
            src = in_bufs[4 * b + 1] if kind == 0 else out_bufs[3 * b + kind - 1]
            return pltpu.make_async_copy(src.at[rows(a, j), :], outs[a].at[rows(a, j), :], out_sems.at[a * n_chunks + j])

        for j in range(n_chunks):
            for a in range(n_big):
                load(a, j).start()
            gx_load(j).start()

        def small_weights():
            gsum = ins[8][...]
            idx = _slot(lax.axis_index("x"), lax.axis_index("y"), lax.axis_index("c"))
            cg = jnp.zeros((3, SHARD_CONV), F32)
            for d in range(N_DEV):
                cg = jnp.where(idx == d, gsum[ROW_CONV0:ROW_CONV0 + 3, d * SHARD_CONV:(d + 1) * SHARD_CONV], cg)
            grads = (gsum[ROW_NORM_IN:ROW_NORM_IN + 1], gsum[ROW_SINKS:ROW_SINKS + 1, 0:N_Q_HEADS],
                     gsum[ROW_NORM_CONV:ROW_NORM_CONV + 1], gsum[ROW_NORM_ATTN:ROW_NORM_ATTN + 1],
                     gsum[ROW_NORM_FINAL:ROW_NORM_FINAL + 1], cg)
            for s, g in enumerate(grads):
                at = (slice(None), 0, slice(None)) if s == 5 else (slice(None), slice(None))
                w_ref, m_ref, v_ref = ins[9 + 3 * s:12 + 3 * s]
                delta, mn, vn = _adamw(w_ref[at], g, m_ref[at], v_ref[at])
                for ref, val in zip(outs[8 + 4 * s:12 + 4 * s], (g, delta, mn, vn)):
                    ref[at] = val
            outs[32][...] = gsum[ROW_SINKS:ROW_SINKS + 1, LOSS_LANE:LOSS_LANE + 1]

        small_weights()
        for j in range(n_chunks):
            for b in range(2):
                for a in range(4 * b, 4 * b + 4):
                    load(a, j).wait()
                w_buf, g_buf, m_buf, v_buf = in_bufs[4 * b:4 * b + 4]
                r = rows(4 * b, j)
                results = _adamw(w_buf[r, :], g_buf[r, :], m_buf[r, :], v_buf[r, :])
                for buf, val in zip(out_bufs[3 * b:3 * b + 3], results):
                    buf[r, :] = val
                for a in range(4 * b, 4 * b + 4):
                    store(a, j).start()
            gx_load(j).wait()
            gx_store(j).start()
        for j in range(n_chunks):
            for a in range(n_big):
                store(a, j).wait()
            gx_store(j).wait()

    vmem, hbm = pl.BlockSpec(memory_space=pltpu.VMEM), pl.BlockSpec(memory_space=pl.ANY)
    small_shapes = [a.shape for a in small[::3]]
    big_shapes = [(SHARD_IN, D_MODEL)] * 4 + [(SHARD_OUT, D_MODEL)] * 4
    out_shape = ([jax.ShapeDtypeStruct(s, F32) for s in big_shapes]
                 + [jax.ShapeDtypeStruct(s, F32) for s in small_shapes for _ in range(4)]
                 + [jax.ShapeDtypeStruct((1, 1), F32), jax.ShapeDtypeStruct((SEQ, D_MODEL), F32)])
    outs = pl.pallas_call(
        body, name="adam", in_specs=[hbm] * n_big + [vmem] * (1 + len(small)) + [hbm],
        out_specs=tuple([hbm] * n_big + [vmem] * (4 * len(small_shapes) + 1) + [hbm]), out_shape=tuple(out_shape),
        scratch_shapes=[pltpu.VMEM(s, F32) for s in big_shapes]
                       + [pltpu.VMEM(s, F32) for s in [(SHARD_IN, D_MODEL)] * 3 + [(SHARD_OUT, D_MODEL)] * 3]
                       + [pltpu.VMEM((SEQ, D_MODEL), F32),
                          pltpu.SemaphoreType.DMA((n_big * n_chunks,)), pltpu.SemaphoreType.DMA((n_big * n_chunks,)),
                          pltpu.SemaphoreType.DMA((2 * n_chunks,))],
        compiler_params=_params(),
    )(*big_in, *big_out, gsum, *small, grad_x)
    return outs[0:4], outs[4:8], [outs[8 + 4 * s:12 + 4 * s] for s in range(6)], outs[32], outs[33]


def _rows_first(a):
    return jnp.transpose(a, (1, 0, 2))


def kernel(x, norm_in, w_in, conv_w, attn_sinks, norm_conv_out, norm_attn_out, w_out, norm_final, loss_target, m_norm_in, m_w_in, m_conv_w, m_attn_sinks, m_norm_conv_out, m_norm_attn_out, m_w_out, m_norm_final, v_norm_in, v_w_in, v_conv_w, v_attn_sinks, v_norm_conv_out, v_norm_attn_out, v_w_out, v_norm_final):
    x2d = x.reshape(SEQ, D_MODEL)
    target = loss_target.reshape(SEQ, D_MODEL)
    nf = norm_final.reshape(1, D_MODEL)

    w_in_t, m_w_in_t, v_w_in_t = w_in[0].T, m_w_in[0].T, v_w_in[0].T
    tiles = jnp.asarray(TILE_ORDER, jnp.int32).reshape(-1)
    w_in_full, h, proj, g_out, conv_full = _gather_in_proj(x2d, norm_in, w_in_t, w_out[0], _rows_first(conv_w), tiles)
    sinks = attn_sinks.reshape(N_Q_HEADS)

    mixed, attn, probs, shares = _mix_fwd(proj, conv_full, sinks, norm_conv_out, norm_attn_out)
    dx2, dx2b, dmixed, gnf, loss_part = _out_proj_loss(mixed, x2d, target, g_out.reshape(D_MIX, D_MODEL), nf)
    dproj, gslab = _mix_bwd(proj, dmixed, attn, probs, shares, conv_full, norm_conv_out, norm_attn_out)
    dw_in_chip, g_w_out = _dw_rs(mixed, dx2b, dproj, h, jnp.asarray(DW_TABLE, jnp.int32).reshape(-1))
    grad_x, g_w_in, gsum = _in_bwd_rs(dproj, w_in_full, x2d, dx2, norm_in, dw_in_chip, gslab, gnf, loss_part)

    small = (norm_in, m_norm_in, v_norm_in, attn_sinks, m_attn_sinks, v_attn_sinks,
             norm_conv_out, m_norm_conv_out, v_norm_conv_out, norm_attn_out, m_norm_attn_out, v_norm_attn_out,
             nf, m_norm_final.reshape(1, D_MODEL), v_norm_final.reshape(1, D_MODEL),
             _rows_first(conv_w), _rows_first(m_conv_w), _rows_first(v_conv_w))
    big_in, big_out, (s_ni, s_sk, s_nc, s_na, s_nf, s_cv), loss, grad_x = _adam_all(
        (w_in_t, g_w_in, m_w_in_t, v_w_in_t), (w_out[0], g_w_out, m_w_out[0], v_w_out[0]), gsum, small, grad_x)

    def leaves(k):
        return (s_ni[k], big_in[k].T[None], jnp.transpose(s_cv[k], (1, 0, 2)), s_sk[k], s_nc[k], s_na[k], big_out[k][None],
                s_nf[k].reshape(D_MODEL))

    return (loss.reshape(()), grad_x.reshape(1, SEQ, D_MODEL), *leaves(0), *leaves(1), *leaves(2), *leaves(3))
```

```python
import jax
import jax.numpy as jnp
from jax import lax
from jax.experimental import pallas as pl
from jax.experimental.pallas import tpu as pltpu

F32 = jnp.float32
BF16 = jnp.bfloat16
MESH = pl.DeviceIdType.MESH

N_DEV = 8
SEQ = 2048
D_MODEL = 1024
D_CONV = 1024
D_ATTN = 1024
D_KV = 128
HEAD_DIM = 64
N_Q_HEADS = 16
N_PAIRS = N_Q_HEADS // 2
PAIRS_PER_KV = N_PAIRS // 2
D_MIX = D_CONV + D_ATTN
D_PROJ = 6400
SHARD_IN = D_PROJ // N_DEV
SHARD_OUT = D_MIX // N_DEV
SHARD_CONV = D_CONV // N_DEV
OFF_CB, OFF_CC, OFF_CU, OFF_GC, OFF_Q, OFF_K, OFF_V, OFF_GA = 0, 1024, 2048, 3072, 4096, 5120, 5248, 5376
BLOCK = 128
N_BLOCKS = SEQ // BLOCK
HALO = 8
CHUNK = 16
N_CHUNKS = BLOCK // CHUNK
RMS_EPS = 1e-5
NEG = -1e30
SCALE = HEAD_DIM ** -0.5
SLOPES = tuple(2.0 ** (-8.0 * (h + 1) / N_Q_HEADS) for h in range(N_Q_HEADS))

ADAM_LR = 0.001
ADAM_B1 = 0.9
ADAM_B2 = 0.999
ADAM_EPS = 1e-08
ADAM_WD = 0.01
ADAM_STEP = 10

ROW_NORM_IN, ROW_NORM_CONV, ROW_NORM_ATTN, ROW_NORM_FINAL, ROW_CONV0, ROW_SINKS = 0, 1, 2, 3, 4, 7
LOSS_LANE = N_Q_HEADS
ACC_NORM_CONV, ACC_NORM_ATTN, ACC_CONV0, N_ACC = 0, 1, 2, 5

VMEM_LIMIT = 56 * 1024 * 1024

_NT = (((1,), (1,)), ((), ()))
_TN = (((0,), (0,)), ((), ()))


def _params(**kw):
    return pltpu.CompilerParams(vmem_limit_bytes=VMEM_LIMIT, **kw)


def _adamw(w, g, m, v):
    m = ADAM_B1 * m + (1.0 - ADAM_B1) * g
    v = ADAM_B2 * v + (1.0 - ADAM_B2) * (g * g)
    m_hat = m / (1.0 - ADAM_B1 ** ADAM_STEP)
    v_hat = v / (1.0 - ADAM_B2 ** ADAM_STEP)
    delta = -ADAM_LR * (m_hat / (jnp.sqrt(v_hat) + ADAM_EPS) + ADAM_WD * w)
    return delta, m, v


def _sigmoid(t):
    return 1.0 / (1.0 + jnp.exp(-t))


def _slot(px, py, pc):
    return 4 * px + 2 * py + pc


OWN, NX, NY, DG = range(4)
HALF_IN = SHARD_IN // 2
N_GATHER_KINDS = 13
W_OUT_KINDS = N_GATHER_KINDS + 7


IN_PROJ_TILE = 640
TILE_ORDER = ((0, 1, 2, 3, 4, 5, 6, 7, 8, 9), (3, 4, 0, 1, 2, 8, 9, 5, 6, 7),
              (5, 6, 0, 1, 7, 8, 9, 2, 3, 4), (8, 9, 3, 4, 5, 6, 7, 0, 1, 2))
TILES_OWN, TILES_NEIGHBOURS = 2, 7


def _tile(table_ref, p):
    chip = 2 * lax.axis_index("x") + lax.axis_index("y")
    return table_ref[chip * len(TILE_ORDER[0]) + p]


DW_TILE_ORDER = tuple(tuple(reversed(row)) for row in TILE_ORDER)


def _tiles_until_complete(chip, owner):
    lo, hi = owner * 2 * SHARD_IN, (owner + 1) * 2 * SHARD_IN
    touching = [t for t in range(len(TILE_ORDER[0])) if t * IN_PROJ_TILE < hi and (t + 1) * IN_PROJ_TILE > lo]
    return 1 + max(DW_TILE_ORDER[chip].index(t) for t in touching)


DW_TABLE = tuple(DW_TILE_ORDER[chip] + tuple(_tiles_until_complete(chip, chip ^ flip) for flip in (0, 2, 1, 3))
                 for chip in range(4))


def _dw_entry(table_ref, p):
    chip = 2 * lax.axis_index("x") + lax.axis_index("y")
    return table_ref[chip * len(DW_TABLE[0]) + p]


def _gather_in_proj(x, norm_in, w_in_sh, w_out_sh, conv_sh, tiles):
    tn = IN_PROJ_TILE
    steps = D_PROJ // tn
    tm = 256

    def body(tiles_ref, x_hbm, g_ref, win_ref, wout_ref, cv_ref, wt_ref, h_ref, proj_ref, gout_ref, conv_ref,
             gin_ref, gcv_ref, wob_ref, x_ref, send_sems, recv_sems, local_sems):
        p = pl.program_id(0)
        local_sem = local_sems.at[0]
        x, y, c = lax.axis_index("x"), lax.axis_index("y"), lax.axis_index("c")
        me, sibling = (x, y, c), (x, y, 1 - c)
        nx, ny, dg = (1 - x, y, c), (x, 1 - y, c), (1 - x, 1 - y, c)

        def other(dev):
            return (dev[0], dev[1], 1 - dev[2])

        def shard(dev):
            return gin_ref.at[pl.ds(pl.multiple_of(_slot(*dev) * SHARD_IN, 16), SHARD_IN), :]

        def half(dev, h):
            return gin_ref.at[pl.ds(pl.multiple_of(_slot(*dev) * SHARD_IN + h * HALF_IN, 16), HALF_IN), :]

        def rc(ref, k, to):
            return pltpu.make_async_remote_copy(src_ref=ref, dst_ref=ref, send_sem=send_sems.at[k],
                                                recv_sem=recv_sems.at[k], device_id=to, device_id_type=MESH)

        def cv(k, dev, to):
            s = _slot(*dev)
            return pltpu.make_async_remote_copy(src_ref=gcv_ref.at[s], dst_ref=gcv_ref.at[s],
                                                send_sem=send_sems.at[N_GATHER_KINDS + k],
                                                recv_sem=recv_sems.at[N_GATHER_KINDS + k], device_id=to, device_id_type=MESH)

        def own_copies():
            return [rc(shard(me), 0, sibling),
                    rc(half(me, 0), 1, nx), rc(half(me, 1), 2, nx),
                    rc(half(me, 1), 4, ny), rc(half(me, 0), 3, ny),
                    cv(0, me, sibling)] + [cv(1 + j, me, peer) for j, peer in enumerate((nx, ny, dg))]

        def pass_on(dev, h, k_in, k_ici, k_d2d, half=half, base=0):
            rc(half(dev, h), base + k_in, me).wait_recv()
            if k_ici is not None:
                rc(half(dev, h), base + k_ici, ny if dev is nx else nx).start()
            rc(half(dev, h), base + k_d2d, sibling).start()

        def out_half(dev, h):
            return gout_ref.at[_slot(*dev), pl.ds(h * (SHARD_OUT // 2), SHARD_OUT // 2), :]

        def own_out_copies():
            src = lambda h: wob_ref.at[pl.ds(h * (SHARD_OUT // 2), SHARD_OUT // 2), :]

            def send(ref, dst, k, to):
                return pltpu.make_async_remote_copy(src_ref=ref, dst_ref=dst, send_sem=send_sems.at[W_OUT_KINDS + k],
                                                    recv_sem=recv_sems.at[W_OUT_KINDS + k], device_id=to, device_id_type=MESH)

            return [send(wob_ref, gout_ref.at[_slot(*me)], 0, sibling),
                    send(src(0), out_half(me, 0), 1, nx), send(src(1), out_half(me, 1), 2, nx),
                    send(src(1), out_half(me, 1), 4, ny), send(src(0), out_half(me, 0), 3, ny)]

        def own_out_local():
            return pltpu.make_async_copy(wob_ref, gout_ref.at[_slot(*me)], local_sems.at[1])

        @pl.when(p == 0)
        def _():
            gin_ref[pl.ds(pl.multiple_of(_slot(*me) * SHARD_IN, 16), SHARD_IN), :] = win_ref[...].astype(BF16)
            gcv_ref[_slot(*me)] = jnp.zeros((8, SHARD_CONV), F32)
            gcv_ref[_slot(*me), 0:3, :] = cv_ref[:, 0, :]
            for cp in own_copies():
                cp.start()
            wob_ref[...] = wout_ref[...].astype(BF16)
            x_load = pltpu.make_async_copy(x_hbm, x_ref, local_sems.at[2])
            x_load.start()
            x_load.wait()
            for t in range(SEQ // tm):
                xv = x_ref[tm * t:tm * (t + 1), :]
                r = lax.rsqrt(jnp.mean(xv * xv, axis=-1, keepdims=True) + RMS_EPS)
                h_ref[tm * t:tm * (t + 1), :] = (xv * r * g_ref[...]).astype(BF16)
            rc(shard(sibling), 0, me).wait_recv()

        @pl.when(p == TILES_OWN)
        def _():
            for args in ((nx, 0, 1, 5, 7), (ny, 1, 4, 6, 10), (nx, 1, 2, None, 8), (ny, 0, 3, None, 9)):
                pass_on(*args)
            for j, peer in enumerate((nx, ny, dg)):
                cv(1 + j, peer, me).wait_recv()
                cv(4 + j, peer, sibling).start()
            for (dev, h), k in (((nx, 0), 7), ((nx, 1), 8), ((ny, 0), 9), ((ny, 1), 10)):
                rc(half(other(dev), h), k, me).wait_recv()
            own_out_local().start()
            for cp in own_out_copies():
                cp.start()

        @pl.when(p == TILES_NEIGHBOURS - 1)
        def _():
            pass_on(dg, 0, 5, None, 11)
            pass_on(dg, 1, 6, None, 12)

        @pl.when(p == TILES_NEIGHBOURS)
        def _():
            for (dev, h), k in (((dg, 0), 11), ((dg, 1), 12)):
                rc(half(other(dev), h), k, me).wait_recv()
            pltpu.make_async_copy(gin_ref, wt_ref, local_sem).start()

        @pl.when(p == steps - 2)
        def _():
            for args in ((nx, 0, 1, 5, 7), (ny, 1, 4, 6, 10), (nx, 1, 2, None, 8), (ny, 0, 3, None, 9)):
                pass_on(*args, half=out_half, base=W_OUT_KINDS)

        w = gin_ref[pl.ds(pl.multiple_of(_tile(tiles_ref, p) * tn, tn), tn), :]
        proj_ref[...] = lax.dot_general(h_ref[...], w, _NT, preferred_element_type=F32)

        @pl.when(p == steps - 1)
        def _():
            cv(0, sibling, me).wait_recv()
            for j, peer in enumerate((nx, ny, dg)):
                cv(4 + j, other(peer), me).wait_recv()
            for d in range(N_DEV):
                conv_ref[:, d * SHARD_CONV:(d + 1) * SHARD_CONV] = gcv_ref[d]
            relayed = [rc(half(nx, 0), 5, ny), rc(half(ny, 1), 6, nx)]
            relayed += [rc(half(dev, h), k, sibling) for (dev, h), k in
                        (((nx, 0), 7), ((nx, 1), 8), ((ny, 0), 9), ((ny, 1), 10), ((dg, 0), 11), ((dg, 1), 12))]
            relayed += [cv(4 + j, peer, sibling) for j, peer in enumerate((nx, ny, dg))]
            for cp in own_copies() + relayed:
                cp.wait_send()
            pltpu.make_async_copy(gin_ref, wt_ref, local_sem).wait()
            pass_on(dg, 0, 5, None, 11, half=out_half, base=W_OUT_KINDS)
            pass_on(dg, 1, 6, None, 12, half=out_half, base=W_OUT_KINDS)
            rc(gout_ref.at[_slot(*sibling)], W_OUT_KINDS, me).wait_recv()
            out_relayed = [rc(out_half(nx, 0), W_OUT_KINDS + 5, ny), rc(out_half(ny, 1), W_OUT_KINDS + 6, nx)]
            for (dev, h), k in (((nx, 0), 7), ((nx, 1), 8), ((ny, 0), 9), ((ny, 1), 10), ((dg, 0), 11), ((dg, 1), 12)):
                rc(out_half(other(dev), h), W_OUT_KINDS + k, me).wait_recv()
                out_relayed.append(rc(out_half(dev, h), W_OUT_KINDS + k, sibling))
            for cp in own_out_copies() + out_relayed:
                cp.wait_send()
            own_out_local().wait()

    vmem = pl.BlockSpec(memory_space=pltpu.VMEM)
    grid_spec = pltpu.PrefetchScalarGridSpec(
        num_scalar_prefetch=1, grid=(steps,),
        in_specs=[pl.BlockSpec(memory_space=pl.ANY), vmem, vmem, vmem, vmem],
        out_specs=(pl.BlockSpec(memory_space=pl.ANY), vmem,
                   pl.BlockSpec((SEQ, tn), lambda p, tiles_ref: (0, _tile(tiles_ref, p))),
                   pl.BlockSpec(memory_space=pl.ANY), vmem),
        scratch_shapes=[pltpu.VMEM((D_PROJ, D_MODEL), BF16), pltpu.VMEM((N_DEV, 8, SHARD_CONV), F32),
                        pltpu.VMEM((SHARD_OUT, D_MODEL), BF16), pltpu.VMEM((SEQ, D_MODEL), F32),
                        pltpu.SemaphoreType.DMA((W_OUT_KINDS + N_GATHER_KINDS,)),
                        pltpu.SemaphoreType.DMA((W_OUT_KINDS + N_GATHER_KINDS,)),
                        pltpu.SemaphoreType.DMA((3,))])
    return pl.pallas_call(
        body, name="gather_in_proj", grid_spec=grid_spec,
        out_shape=(jax.ShapeDtypeStruct((D_PROJ, D_MODEL), BF16), jax.ShapeDtypeStruct((SEQ, D_MODEL), BF16),
                   jax.ShapeDtypeStruct((SEQ, D_PROJ), F32), jax.ShapeDtypeStruct((N_DEV, SHARD_OUT, D_MODEL), BF16),
                   jax.ShapeDtypeStruct((8, D_CONV), F32)),
        compiler_params=_params(dimension_semantics=("arbitrary",)),
    )(tiles, x, norm_in, w_in_sh, w_out_sh, conv_sh)


def _shard_sum(src, own, d2d, ici, send_sems, recv_sems, local_sems, base=0):
    x, y, c = lax.axis_index("x"), lax.axis_index("y"), lax.axis_index("c")
    sibling = (x, y, 1 - c)
    chips = [(x, y), (1 - x, y), (x, 1 - y), (1 - x, 1 - y)]

    def rcopy(s, d, k, to):
        return pltpu.make_async_remote_copy(src_ref=s, dst_ref=d, send_sem=send_sems.at[base + k],
                                            recv_sem=recv_sems.at[base + k], device_id=to, device_id_type=MESH)

    def mine(k):
        return pltpu.make_async_copy(src.at[_slot(*chips[k], c)], own.at[k], local_sems.at[k])

    def to_sibling(k):
        return rcopy(src.at[_slot(*chips[k], 1 - c)], d2d.at[k], k, sibling)

    def to_chip(k):
        return rcopy(own.at[k], ici.at[k - 1], 3 + k, (*chips[k], c))

    def start(k):
        mine(k).start()
        to_sibling(k).start()

    def forward(k):
        mine(k).wait()
        to_sibling(k).wait_recv()
        own[k] = (own[k].astype(F32) + d2d[k].astype(F32)).astype(BF16)
        to_chip(k).start()

    def finish():
        mine(0).wait()
        to_sibling(0).wait_recv()
        acc = own[0].astype(F32) + d2d[0].astype(F32)
        for k in range(1, 4):
            to_chip(k).wait_recv()
            acc = acc + ici[k - 1].astype(F32)
        for k in range(4):
            to_sibling(k).wait_send()
        for k in range(1, 4):
            to_chip(k).wait_send()
        return acc

    return start, forward, finish


def _shard_sum_scratch(rows):
    return [pltpu.VMEM((4, rows, D_MODEL), BF16), pltpu.VMEM((4, rows, D_MODEL), BF16),
            pltpu.VMEM((3, rows, D_MODEL), BF16)]


N_SHARD_SUM_SEMS = 7


N_CHIP_SUM_SEMS = 5


def _chip_sum(dwt, d2d, via, out_hbm, tiles_until, send_sems, recv_sems, local_sems, base, local_base):
    x, y, c = lax.axis_index("x"), lax.axis_index("y"), lax.axis_index("c")
    sibling, nx, ny = (x, y, 1 - c), (1 - x, y, c), (x, 1 - y, c)
    chips = [(x, y), (1 - x, y), (x, 1 - y), (1 - x, 1 - y)]

    def shard(s):
        return dwt.at[pl.ds(pl.multiple_of(s * SHARD_IN, 16), SHARD_IN), :]

    def half(ref, h):
        return ref.at[pl.ds(h * HALF_IN, HALF_IN), :]

    def rc(s, d, k, to):
        return pltpu.make_async_remote_copy(src_ref=s, dst_ref=d, send_sem=send_sems.at[base + k],
                                            recv_sem=recv_sems.at[base + k], device_id=to, device_id_type=MESH)

    def to_sibling(k):
        return rc(shard(_slot(*chips[k], 1 - c)), d2d.at[k - 1], k - 1, sibling)

    for_dg = (lambda: rc(half(d2d.at[DG - 1], 0), via.at[0], 3, nx), lambda: rc(half(d2d.at[DG - 1], 1), via.at[1], 4, ny))

    def save(k):
        return pltpu.make_async_copy(d2d.at[k - 1], out_hbm.at[k], local_sems.at[local_base + k])

    own_saves = (lambda: pltpu.make_async_copy(shard(_slot(x, y, c)), out_hbm.at[OWN], local_sems.at[local_base]),
                 lambda: pltpu.make_async_copy(shard(_slot(x, y, 1 - c)), out_hbm.at[3], local_sems.at[local_base + 3]))

    def before_tile(n):
        for k in (NX, NY, DG):
            @pl.when(tiles_until(k) == n)
            def _():
                to_sibling(k).start()

            @pl.when(tiles_until(k) + 1 == n)
            def _():
                to_sibling(k).wait_recv()
                d2d[k - 1] = (shard(_slot(*chips[k], c))[...].astype(F32) + d2d[k - 1].astype(F32)).astype(BF16)
                if k == DG:
                    for cp in for_dg:
                        cp().start()

    def after_tiles():
        for cp in own_saves:
            cp().start()

    def finish():
        for k, h in ((NY, 0), (NX, 1)):
            for_dg[h]().wait_recv()
            rows = pl.ds(h * HALF_IN, HALF_IN)
            d2d[k - 1, rows, :] = (d2d[k - 1, rows, :].astype(F32) + via[h].astype(F32)).astype(BF16)
            save(k).start()
        for cp in own_saves + (lambda: save(NX), lambda: save(NY)):
            cp().wait()
        for cp in (lambda: to_sibling(NX), lambda: to_sibling(NY), lambda: to_sibling(DG)) + for_dg:
            cp().wait_send()

    return before_tile, after_tiles, finish


N_ICI_SUM_SEMS = 3


def _ici_sum(src, own, d2d, ici, send_sems, recv_sems, local_sems, base=0):
    x, y, c = lax.axis_index("x"), lax.axis_index("y"), lax.axis_index("c")

    def rc(s, d, k, to):
        return pltpu.make_async_remote_copy(src_ref=s, dst_ref=d, send_sem=send_sems.at[base + k],
                                            recv_sem=recv_sems.at[base + k], device_id=to, device_id_type=MESH)

    copies = (lambda: rc(src.at[NX], ici.at[0], 0, (1 - x, y, c)), lambda: rc(src.at[NY], ici.at[1], 1, (x, 1 - y, c)),
              lambda: rc(src.at[3], d2d, 2, (x, y, 1 - c)))
    mine = lambda: pltpu.make_async_copy(src.at[OWN], own, local_sems.at[0])

    def start():
        for cp in copies + (mine,):
            cp().start()

    def finish():
        mine().wait()
        for cp in copies:
            cp().wait_recv()
        acc = own[...].astype(F32) + d2d[...].astype(F32) + ici[0].astype(F32) + ici[1].astype(F32)
        for cp in copies:
            cp().wait_send()
        return acc

    return start, finish


def _slab_sum(myslab, slabs, send_sems, recv_sems, base):
    x, y, c = lax.axis_index("x"), lax.axis_index("y"), lax.axis_index("c")
    me = _slot(x, y, c)
    peers = [(x, y, 1 - c), (1 - x, y, c), (x, 1 - y, c), (1 - x, 1 - y, c),
             (1 - x, y, 1 - c), (x, 1 - y, 1 - c), (1 - x, 1 - y, 1 - c)]

    def cp(k):
        return pltpu.make_async_remote_copy(src_ref=myslab, dst_ref=slabs.at[me], send_sem=send_sems.at[base + k],
                                            recv_sem=recv_sems.at[base + k], device_id=peers[k], device_id_type=MESH)

    def start():
        slabs[me] = myslab[...]
        for k in range(7):
            cp(k).start()

    def finish():
        for k in range(7):
            cp(k).wait_recv()
        total = slabs[0]
        for d in range(1, N_DEV):
            total = total + slabs[d]
        for k in range(7):
            cp(k).wait_send()
        return total

    return start, finish


def _chunk_rows(r):
    return slice(r * CHUNK, (r + 1) * CHUNK)


def _conv_halo(cch_ref, cuh_ref, n):
    zh = jnp.where(n > 0, cch_ref[...] * cuh_ref[...], 0.0)
    return jnp.concatenate([zh] * (CHUNK // HALO), axis=0)


def _conv_chunk(pj_ref, zhalo, cw, r):
    rows = _chunk_rows(r)
    cc = pj_ref[rows, OFF_CC:OFF_CC + D_CONV]
    cu = pj_ref[rows, OFF_CU:OFF_CU + D_CONV]
    z = cc * cu
    before = _chunk_rows(r - 1)
    zprev = pj_ref[before, OFF_CC:OFF_CC + D_CONV] * pj_ref[before, OFF_CU:OFF_CU + D_CONV] if r > 0 else zhalo
    row = lax.broadcasted_iota(jnp.int32, (CHUNK, D_CONV), 0)
    z1 = jnp.where(row < 1, pltpu.roll(zprev, 1, 0), pltpu.roll(z, 1, 0))
    z2 = jnp.where(row < 2, pltpu.roll(zprev, 2, 0), pltpu.roll(z, 2, 0))
    co = cw[0] * z2 + cw[1] * z1 + cw[2] * z
    return cc, cu, z, z1, z2, co


def _gated_norm(a, gain, t):
    r = lax.rsqrt(jnp.mean(a * a, axis=-1, keepdims=True) + RMS_EPS)
    return a * r * gain * (t * _sigmoid(t))


def _kv_bands(pj, kvp_ref):
    lane = lax.broadcasted_iota(jnp.int32, (2 * BLOCK, D_KV), 1)
    lo = lane < HEAD_DIM

    def bands(prev, cur):
        b = jnp.concatenate([prev, cur], axis=0)
        br = pltpu.roll(b, HEAD_DIM, 1)
        zero = jnp.zeros_like(b)
        return ((jnp.where(lo, b, zero).astype(BF16), jnp.where(lo, zero, br).astype(BF16)),
                (jnp.where(lo, br, zero).astype(BF16), jnp.where(lo, zero, b).astype(BF16)))

    ks = bands(kvp_ref[:, 0:D_KV], pj[:, OFF_K:OFF_K + D_KV])
    vs = bands(kvp_ref[:, D_KV:2 * D_KV], pj[:, OFF_V:OFF_V + D_KV])
    return ks, vs


STACK = PAIRS_PER_KV * BLOCK


def _head(j, i, e):
    return 2 * (PAIRS_PER_KV * j + i) + e


def _pair_cols(j, i, off):
    p = PAIRS_PER_KV * j + i
    return slice(off + 128 * p, off + 128 * (p + 1))


def _fill_attn_bias(bias_scr, first_block):
    qi = lax.broadcasted_iota(jnp.int32, (BLOCK, 2 * BLOCK), 0)
    kj = lax.broadcasted_iota(jnp.int32, (BLOCK, 2 * BLOCK), 1)
    dist = BLOCK + qi - kj
    valid = (dist >= 0) & (dist < BLOCK)
    if first_block:
        valid = valid & (kj >= BLOCK)
    distf = dist.astype(F32)
    for j in range(2):
        for e in range(2):
            for i in range(PAIRS_PER_KV):
                bias_scr[2 * j + e, BLOCK * i:BLOCK * (i + 1), :] = jnp.where(valid, -SLOPES[_head(j, i, e)] * distf, NEG)


def _q_stack(pj, j):
    return jnp.concatenate([(pj[:, _pair_cols(j, i, OFF_Q)] * SCALE).astype(BF16) for i in range(PAIRS_PER_KV)], axis=0)


def _attn_probs(q_stack, kband, bias_ref, sinks):
    s = lax.dot_general(q_stack, kband, _NT, preferred_element_type=F32)
    ones = jnp.ones((128, 128), BF16)
    probs, shares = [], []
    for i, sink in enumerate(sinks):
        rows = slice(BLOCK * i, BLOCK * (i + 1))
        t = s[rows, :] + bias_ref[rows, :]
        m = jnp.broadcast_to(jnp.max(t, axis=-1, keepdims=True), (BLOCK, 128))
        m = jnp.maximum(m, sink)
        p = [jnp.exp(t[:, :128] - m), jnp.exp(t[:, 128:] - m)]
        es = jnp.exp(sink - m)
        total = (jnp.dot(p[0].astype(BF16), ones, preferred_element_type=F32)
                 + jnp.dot(p[1].astype(BF16), ones, preferred_element_type=F32))
        inv = 1.0 / (total + es)
        probs.append(jnp.concatenate([p[0] * inv, p[1] * inv], axis=1))
        shares.append(es * inv)
    return jnp.concatenate(probs, axis=0), jnp.concatenate(shares, axis=0)


def _attn_group(pj, ks, vs, bias_scr, sink_ref, j):
    q_stack = _q_stack(pj, j)
    out, probs, shares = None, [], []
    for e in range(2):
        p, ps = _attn_probs(q_stack, ks[j][e], bias_scr.at[2 * j + e],
                            [sink_ref[_head(j, i, e)] for i in range(PAIRS_PER_KV)])
        p = p.astype(BF16)
        o = jnp.dot(p, vs[j][e], preferred_element_type=F32)
        out = o if out is None else out + o
        probs.append(p)
        shares.append(ps)
    return out, probs, shares


def _mix_fwd(proj, conv_full, sinks, norm_conv, norm_attn):
    def body(pj_ref, kvp_ref, cch_ref, cuh_ref, cw_ref, sink_ref, gc_ref, ga_ref,
             mixed_ref, attn_scr, p_ref, ps_ref, bias_scr):
        n = pl.program_id(0)
        pj = pj_ref

        @pl.when(n == 0)
        def _():
            _fill_attn_bias(bias_scr, first_block=True)

        @pl.when(n == 1)
        def _():
            _fill_attn_bias(bias_scr, first_block=False)

        zhalo = _conv_halo(cch_ref, cuh_ref, n)
        cw = (cw_ref[0:1, :], cw_ref[1:2, :], cw_ref[2:3, :])
        gain_c = gc_ref[...]

        for r in range(N_CHUNKS):
            rows = _chunk_rows(r)
            co = _conv_chunk(pj_ref, zhalo, cw, r)[-1]
            y = _gated_norm(pj_ref[rows, OFF_CB:OFF_CB + D_CONV] * co, gain_c, pj_ref[rows, OFF_GC:OFF_GC + D_CONV])
            mixed_ref[rows, 0:D_CONV] = y.astype(BF16)

        ks, vs = _kv_bands(pj, kvp_ref)
        for j in range(2):
            out, probs, shares = _attn_group(pj, ks, vs, bias_scr, sink_ref, j)
            for e in range(2):
                p_ref[0, 2 * j + e] = probs[e]
                ps_ref[0, 2 * j + e] = shares[e]
            for i in range(PAIRS_PER_KV):
                attn_scr[:, _pair_cols(j, i, 0)] = out[BLOCK * i:BLOCK * (i + 1), :]
        gain_a = ga_ref[...]

        for r in range(N_CHUNKS):
            rows = _chunk_rows(r)
            y = _gated_norm(attn_scr[rows, :], gain_a, pj_ref[rows, OFF_GA:OFF_GA + D_ATTN])
            mixed_ref[rows, D_CONV:D_MIX] = y.astype(BF16)

    per_block = BLOCK // HALO
    return pl.pallas_call(
        body, name="mix_fwd", grid=(N_BLOCKS,),
        in_specs=[
            pl.BlockSpec((BLOCK, D_PROJ), lambda n: (n, 0)),
            pl.BlockSpec((BLOCK, 2 * D_KV), lambda n: (jnp.maximum(n - 1, 0), OFF_K // (2 * D_KV))),
            pl.BlockSpec((HALO, D_CONV), lambda n: (jnp.maximum(n * per_block - 1, 0), OFF_CC // D_CONV)),
            pl.BlockSpec((HALO, D_CONV), lambda n: (jnp.maximum(n * per_block - 1, 0), OFF_CU // D_CONV)),
            pl.BlockSpec((8, D_CONV), lambda n: (0, 0)),
            pl.BlockSpec(memory_space=pltpu.SMEM),
            pl.BlockSpec((1, D_CONV), lambda n: (0, 0)),
            pl.BlockSpec((1, D_ATTN), lambda n: (0, 0)),
        ],
        out_specs=(pl.BlockSpec((BLOCK, D_MIX), lambda n: (n, 0)), pl.BlockSpec((BLOCK, D_ATTN), lambda n: (n, 0)),
                   pl.BlockSpec((1, 4, STACK, 2 * BLOCK), lambda n: (n, 0, 0, 0)),
                   pl.BlockSpec((1, 4, STACK, 128), lambda n: (n, 0, 0, 0))),
        out_shape=(jax.ShapeDtypeStruct((SEQ, D_MIX), BF16), jax.ShapeDtypeStruct((SEQ, D_ATTN), F32),
                   jax.ShapeDtypeStruct((N_BLOCKS, 4, STACK, 2 * BLOCK), BF16),
                   jax.ShapeDtypeStruct((N_BLOCKS, 4, STACK, 128), F32)),
        scratch_shapes=[pltpu.VMEM((4, STACK, 2 * BLOCK), F32)],
        compiler_params=_params(dimension_semantics=("arbitrary",)),
    )(proj, proj, proj, proj, conv_full, sinks, norm_conv, norm_attn)


def _out_proj_loss(mixed, x, target, w_out_full, norm_final):
    tm = 256

    def body(mx_ref, x_ref, t_ref, w_ref, g_ref, dx2_ref, dx2b_ref, dmix_ref, gnf_ref, loss_ref):
        i = pl.program_id(0)
        w = w_ref[...]
        x2 = x_ref[...] + jnp.dot(mx_ref[...], w, preferred_element_type=F32)
        r = lax.rsqrt(jnp.mean(x2 * x2, axis=-1, keepdims=True) + RMS_EPS)
        xn = x2 * r
        g = g_ref[...]
        err = xn * g - t_ref[...]
        part = 0.5 * jnp.sum(jnp.mean(err * err, axis=-1, keepdims=True), axis=0, keepdims=True)
        dy = err * (1.0 / D_MODEL)
        gnf = jnp.sum(dy * xn, axis=0, keepdims=True)
        u = dy * g
        dx2 = r * (u - xn * jnp.mean(u * xn, axis=-1, keepdims=True))
        dx2_ref[...] = dx2
        dx2b = dx2.astype(BF16)
        dx2b_ref[...] = dx2b
        dmix_ref[...] = lax.dot_general(dx2b, w, _NT, preferred_element_type=F32)

        @pl.when(i == 0)
        def _():
            gnf_ref[...] = jnp.zeros_like(gnf_ref)
            loss_ref[...] = jnp.zeros_like(loss_ref)

        gnf_ref[...] += gnf
        loss_ref[...] += jnp.broadcast_to(part, loss_ref.shape)

    return pl.pallas_call(
        body, name="out_proj_loss", grid=(SEQ // tm,),
        in_specs=[pl.BlockSpec((tm, D_MIX), lambda i: (i, 0)), pl.BlockSpec((tm, D_MODEL), lambda i: (i, 0)),
                  pl.BlockSpec((tm, D_MODEL), lambda i: (i, 0)), pl.BlockSpec(memory_space=pltpu.VMEM),
                  pl.BlockSpec((1, D_MODEL), lambda i: (0, 0))],
        out_specs=(pl.BlockSpec((tm, D_MODEL), lambda i: (i, 0)), pl.BlockSpec((tm, D_MODEL), lambda i: (i, 0)),
                   pl.BlockSpec((tm, D_MIX), lambda i: (i, 0)),
                   pl.BlockSpec((1, D_MODEL), lambda i: (0, 0)), pl.BlockSpec((8, 128), lambda i: (0, 0))),
        out_shape=(jax.ShapeDtypeStruct((SEQ, D_MODEL), F32), jax.ShapeDtypeStruct((SEQ, D_MODEL), BF16),
                   jax.ShapeDtypeStruct((SEQ, D_MIX), F32),
                   jax.ShapeDtypeStruct((1, D_MODEL), F32), jax.ShapeDtypeStruct((8, 128), F32)),
        compiler_params=_params(dimension_semantics=("arbitrary",)),
    )(mixed, x, target, w_out_full, norm_final)


def _gated_norm_bwd(a, gain, t, dy):
    r = lax.rsqrt(jnp.mean(a * a, axis=-1, keepdims=True) + RMS_EPS)
    an = a * r
    sg = _sigmoid(t)
    dn = dy * (t * sg)
    dt = dy * (an * gain) * (sg * (1.0 + t * (1.0 - sg)))
    u = dn * gain
    da = r * (u - an * jnp.mean(u * an, axis=-1, keepdims=True))
    return da, dt, dn * an


def _mix_bwd(proj, dmixed, attn, probs, shares, conv_full, norm_conv, norm_attn):
    def body(pj_ref, kvp_ref, cch_ref, cuh_ref, dmx_ref, attn_ref, p_ref, ps_ref, cw_ref, gc_ref, ga_ref,
             dpj_ref, gslab_ref, dattn_scr, nxt_scr, dkv_scr, acc_scr):
        step = pl.program_id(0)
        n = N_BLOCKS - 1 - step
        pj = pj_ref

        @pl.when(step == 0)
        def _():
            gslab_ref[...] = jnp.zeros_like(gslab_ref)
            nxt_scr[...] = jnp.zeros_like(nxt_scr)
            dkv_scr[...] = jnp.zeros_like(dkv_scr)
            acc_scr[...] = jnp.zeros_like(acc_scr)

        zhalo = _conv_halo(cch_ref, cuh_ref, n)
        cw = (cw_ref[0:1, :], cw_ref[1:2, :], cw_ref[2:3, :])
        gain_c = gc_ref[...]
        row = lax.broadcasted_iota(jnp.int32, (CHUNK, D_CONV), 0)

        dco_after = nxt_scr[...]
        for r in reversed(range(N_CHUNKS)):
            rows = _chunk_rows(r)
            cc, cu, z, z1, z2, co = _conv_chunk(pj_ref, zhalo, cw, r)
            cb = pj_ref[rows, OFF_CB:OFF_CB + D_CONV]
            da, dgate, gterm = _gated_norm_bwd(cb * co, gain_c, pj_ref[rows, OFF_GC:OFF_GC + D_CONV],
                                               dmx_ref[rows, 0:D_CONV])
            dpj_ref[rows, OFF_GC:OFF_GC + D_CONV] = dgate.astype(BF16)
            dpj_ref[rows, OFF_CB:OFF_CB + D_CONV] = (da * co).astype(BF16)
            dco = da * cb
            dco1 = jnp.where(row >= CHUNK - 1, pltpu.roll(dco_after, CHUNK - 1, 0), pltpu.roll(dco, CHUNK - 1, 0))
            dco2 = jnp.where(row >= CHUNK - 2, pltpu.roll(dco_after, CHUNK - 2, 0), pltpu.roll(dco, CHUNK - 2, 0))
            dz = cw[2] * dco + cw[1] * dco1 + cw[0] * dco2
            dpj_ref[rows, OFF_CC:OFF_CC + D_CONV] = (dz * cu).astype(BF16)
            dpj_ref[rows, OFF_CU:OFF_CU + D_CONV] = (dz * cc).astype(BF16)
            acc_scr[ACC_NORM_CONV] += gterm
            acc_scr[ACC_CONV0] += dco * z2
            acc_scr[ACC_CONV0 + 1] += dco * z1
            acc_scr[ACC_CONV0 + 2] += dco * z
            dco_after = dco
        nxt_scr[...] = dco_after

        ks, vs = _kv_bands(pj, kvp_ref)
        gain_a = ga_ref[...]

        for r in range(N_CHUNKS):
            rows = _chunk_rows(r)
            da, dgate, gterm = _gated_norm_bwd(attn_ref[rows, :], gain_a, pj_ref[rows, OFF_GA:OFF_GA + D_ATTN],
                                               dmx_ref[rows, D_CONV:D_MIX])
            dpj_ref[rows, OFF_GA:OFF_GA + D_ATTN] = dgate.astype(BF16)
            dattn_scr[rows, :] = da
            acc_scr[ACC_NORM_ATTN] += gterm

        in_lo = lax.broadcasted_iota(jnp.int32, (128, 128), 0) < HEAD_DIM
        half_ones = (jnp.where(in_lo, 1.0, 0.0).astype(BF16), jnp.where(in_lo, 0.0, 1.0).astype(BF16))
        lane_s = lax.broadcasted_iota(jnp.int32, (1, D_MODEL), 1)
        gsink = jnp.zeros((1, D_MODEL), F32)
        dk_t, dv_t = [], []
        for j in range(2):
            q_stack = _q_stack(pj, j)
            do_f = jnp.concatenate([dattn_scr[:, _pair_cols(j, i, 0)] for i in range(PAIRS_PER_KV)], axis=0)
            o_f = jnp.concatenate([attn_ref[:, _pair_cols(j, i, 0)] for i in range(PAIRS_PER_KV)], axis=0)
            prod = (do_f * o_f).astype(BF16)
            deltas = [jnp.dot(prod, half_ones[e], preferred_element_type=F32) for e in range(2)]
            do_b = do_f.astype(BF16)
            q_t, do_t = q_stack.T, do_b.T
            dq, dk_j, dv_j = None, None, None
            for e in range(2):
                p = p_ref[0, 2 * j + e]
                dp = lax.dot_general(do_b, vs[j][e], _NT, preferred_element_type=F32)
                ds = []
                for i in range(PAIRS_PER_KV):
                    rows = slice(BLOCK * i, BLOCK * (i + 1))
                    delta = deltas[e][rows, :]
                    ds.append((p[rows, :].astype(F32) * (dp[rows, :] - jnp.concatenate([delta, delta], axis=1))).astype(BF16))
                    gs_h = -jnp.sum(ps_ref[0, 2 * j + e, rows, 0:1] * delta[:, 0:1], axis=0, keepdims=True)
                    gsink = gsink + jnp.where(lane_s == _head(j, i, e), gs_h, 0.0)
                ds = jnp.concatenate(ds, axis=0)
                t = jnp.dot(ds, ks[j][e], preferred_element_type=F32)
                dq = t if dq is None else dq + t
                half = slice(HEAD_DIM * e, HEAD_DIM * (e + 1))
                a = jnp.dot(q_t[half, :], ds, preferred_element_type=F32)
                b = jnp.dot(do_t[half, :], p, preferred_element_type=F32)
                dk_j = a if dk_j is None else dk_j + a
                dv_j = b if dv_j is None else dv_j + b
            for i in range(PAIRS_PER_KV):
                dpj_ref[:, _pair_cols(j, i, OFF_Q)] = (dq[BLOCK * i:BLOCK * (i + 1), :] * SCALE).astype(BF16)
            dk_t.append(dk_j)
            dv_t.append(dv_j)
        dk = jnp.concatenate(dk_t, axis=0).T
        dv = jnp.concatenate(dv_t, axis=0).T
        dpj_ref[:, OFF_K:OFF_K + D_KV] = (dk[BLOCK:, :] + dkv_scr[:, 0:D_KV]).astype(BF16)
        dpj_ref[:, OFF_V:OFF_V + D_KV] = (dv[BLOCK:, :] + dkv_scr[:, D_KV:2 * D_KV]).astype(BF16)
        dkv_scr[:, 0:D_KV] = dk[:BLOCK, :]
        dkv_scr[:, D_KV:2 * D_KV] = dv[:BLOCK, :]
        gslab_ref[ROW_SINKS:ROW_SINKS + 1, :] += gsink

        @pl.when(step == N_BLOCKS - 1)
        def _():
            for k, slab_row in ((ACC_NORM_CONV, ROW_NORM_CONV), (ACC_NORM_ATTN, ROW_NORM_ATTN), (ACC_CONV0, ROW_CONV0),
                                (ACC_CONV0 + 1, ROW_CONV0 + 1), (ACC_CONV0 + 2, ROW_CONV0 + 2)):
                gslab_ref[slab_row:slab_row + 1, :] = jnp.sum(acc_scr[k], axis=0, keepdims=True)

    per_block = BLOCK // HALO
    last = N_BLOCKS - 1
    return pl.pallas_call(
        body, name="mix_bwd", grid=(N_BLOCKS,),
        in_specs=[
            pl.BlockSpec((BLOCK, D_PROJ), lambda s: (last - s, 0)),
            pl.BlockSpec((BLOCK, 2 * D_KV), lambda s: (jnp.maximum(last - s - 1, 0), OFF_K // (2 * D_KV))),
            pl.BlockSpec((HALO, D_CONV), lambda s: (jnp.maximum((last - s) * per_block - 1, 0), OFF_CC // D_CONV)),
            pl.BlockSpec((HALO, D_CONV), lambda s: (jnp.maximum((last - s) * per_block - 1, 0), OFF_CU // D_CONV)),
            pl.BlockSpec((BLOCK, D_MIX), lambda s: (last - s, 0)),
            pl.BlockSpec((BLOCK, D_ATTN), lambda s: (last - s, 0)),
            pl.BlockSpec((1, 4, STACK, 2 * BLOCK), lambda s: (last - s, 0, 0, 0)),
            pl.BlockSpec((1, 4, STACK, 128), lambda s: (last - s, 0, 0, 0)),
            pl.BlockSpec((8, D_CONV), lambda s: (0, 0)),
            pl.BlockSpec((1, D_CONV), lambda s: (0, 0)),
            pl.BlockSpec((1, D_ATTN), lambda s: (0, 0)),
        ],
        out_specs=(pl.BlockSpec((BLOCK, D_PROJ), lambda s: (last - s, 0)),
                   pl.BlockSpec((8, D_MODEL), lambda s: (0, 0))),
        out_shape=(jax.ShapeDtypeStruct((SEQ, D_PROJ), BF16), jax.ShapeDtypeStruct((8, D_MODEL), F32)),
        scratch_shapes=[pltpu.VMEM((BLOCK, D_ATTN), F32), pltpu.VMEM((CHUNK, D_CONV), F32),
                        pltpu.VMEM((BLOCK, 2 * D_KV), F32), pltpu.VMEM((N_ACC, CHUNK, D_MODEL), F32)],
        compiler_params=_params(dimension_semantics=("arbitrary",)),
    )(proj, proj, proj, proj, dmixed, attn, probs, shares, conv_full, norm_conv, norm_attn)


def _in_bwd_rs(dproj, w_full, x, dx2, norm_in, dw_in_chip, gslab, gnf, loss_part):
    tm = 256
    steps = SEQ // tm

    def body(dp_ref, w_hbm, x_ref, dx2_ref, g_ref, dwi_ref, gs_ref, gnf_ref, lp_ref, gx_ref, gwin_ref, gsum_ref,
             gni_scr, own, d2d, ici, myslab, slabs, w_ref, send_sems, recv_sems, local_sems):
        i = pl.program_id(0)
        rs_start, rs_finish = _ici_sum(dwi_ref, own, d2d, ici, send_sems, recv_sems, local_sems)
        slab_start, slab_finish = _slab_sum(myslab, slabs, send_sems, recv_sems, N_ICI_SUM_SEMS)

        @pl.when(i == 0)
        def _():
            gni_scr[...] = jnp.zeros_like(gni_scr)
            rs_start()
            w_load = pltpu.make_async_copy(w_hbm, w_ref, local_sems.at[1])
            w_load.start()
            w_load.wait()

        dh = jnp.dot(dp_ref[...], w_ref[...], preferred_element_type=F32)
        xv = x_ref[...]
        r = lax.rsqrt(jnp.mean(xv * xv, axis=-1, keepdims=True) + RMS_EPS)
        xn = xv * r
        u = dh * g_ref[...]
        gx_ref[...] = dx2_ref[...] + r * (u - xn * jnp.mean(u * xn, axis=-1, keepdims=True))
        gni_scr[...] += jnp.sum(dh * xn, axis=0, keepdims=True)

        @pl.when(i == steps - 1)
        def _():
            row = lax.broadcasted_iota(jnp.int32, (8, D_MODEL), 0)
            lane = lax.broadcasted_iota(jnp.int32, (8, D_MODEL), 1)
            slab = jnp.where(row == ROW_NORM_IN, gni_scr[...], jnp.where(row == ROW_NORM_FINAL, gnf_ref[...], gs_ref[...]))
            myslab[...] = jnp.where((row == ROW_SINKS) & (lane == LOSS_LANE), lp_ref[0:1, 0:1], slab)
            slab_start()
            gwin_ref[...] = rs_finish()
            gsum_ref[...] = slab_finish()

    const = lambda i: (0, 0)
    return pl.pallas_call(
        body, name="in_bwd", grid=(steps,),
        in_specs=[pl.BlockSpec((tm, D_PROJ), lambda i: (i, 0)), pl.BlockSpec(memory_space=pl.ANY),
                  pl.BlockSpec((tm, D_MODEL), lambda i: (i, 0)), pl.BlockSpec((tm, D_MODEL), lambda i: (i, 0)),
                  pl.BlockSpec((1, D_MODEL), const), pl.BlockSpec(memory_space=pl.ANY),
                  pl.BlockSpec((8, D_MODEL), const), pl.BlockSpec((1, D_MODEL), const), pl.BlockSpec((8, 128), const)],
        out_specs=(pl.BlockSpec((tm, D_MODEL), lambda i: (i, 0)), pl.BlockSpec((SHARD_IN, D_MODEL), const),
                   pl.BlockSpec((8, D_MODEL), const)),
        out_shape=(jax.ShapeDtypeStruct((SEQ, D_MODEL), F32), jax.ShapeDtypeStruct((SHARD_IN, D_MODEL), F32),
                   jax.ShapeDtypeStruct((8, D_MODEL), F32)),
        scratch_shapes=[pltpu.VMEM((1, D_MODEL), F32), pltpu.VMEM((SHARD_IN, D_MODEL), BF16),
                        pltpu.VMEM((SHARD_IN, D_MODEL), BF16), pltpu.VMEM((2, SHARD_IN, D_MODEL), BF16),
                        pltpu.VMEM((8, D_MODEL), F32), pltpu.VMEM((N_DEV, 8, D_MODEL), F32),
                        pltpu.VMEM((D_PROJ, D_MODEL), BF16),
                        pltpu.SemaphoreType.DMA((N_ICI_SUM_SEMS + 7,)), pltpu.SemaphoreType.DMA((N_ICI_SUM_SEMS + 7,)),
                        pltpu.SemaphoreType.DMA((2,))],
        compiler_params=_params(dimension_semantics=("arbitrary",)),
    )(dproj, w_full, x, dx2, norm_in, dw_in_chip, gslab, gnf, loss_part)


def _dw_rs(mixed, dx2b, dproj, h, table):
    tn_out, tn = 2 * SHARD_OUT, IN_PROJ_TILE
    out_steps, in_steps = D_MIX // tn_out, D_PROJ // tn
    steps = out_steps + in_steps
    out_order = (DG, NX, NY, OWN)

    def out_tile(i):
        chip = 2 * lax.axis_index("x") + lax.axis_index("y")
        return jnp.bitwise_xor(chip, (out_steps - 1) - jnp.minimum(i, out_steps - 1))

    def in_tile(table_ref, i):
        return _dw_entry(table_ref, jnp.maximum(i - out_steps, 0))

    def body(table_ref, mx_ref, dxb_ref, a_ref, h_hbm, chip_ref, gwo_ref, dwo, dwt, d2d_in, via, own, d2d, ici, b_ref,
             send_sems, recv_sems, local_sems):
        i = pl.program_id(0)
        h_load = pltpu.make_async_copy(h_hbm, b_ref, local_sems.at[8])

        @pl.when(i == 0)
        def _():
            h_load.start()

        @pl.when(i == out_steps)
        def _():
            h_load.wait()

        rs_start, rs_forward, rs_finish = _shard_sum(dwo, own, d2d, ici, send_sems, recv_sems, local_sems)
        before_tile, after_tiles, chip_finish = _chip_sum(
            dwt, d2d_in, via, chip_ref, lambda k: _dw_entry(table_ref, in_steps + k), send_sems, recv_sems, local_sems,
            N_SHARD_SUM_SEMS, 4)

        for j, k in enumerate(out_order):
            @pl.when(i == j + 1)
            def _():
                rs_start(k)

            if k != OWN:
                @pl.when(i == j + 2)
                def _():
                    rs_forward(k)

        @pl.when(i < out_steps)
        def _():
            tile = lax.dot_general(mx_ref[...], dxb_ref[...], _TN, preferred_element_type=F32).astype(BF16)
            for core in range(2):
                dwo[2 * out_tile(i) + core] = tile[SHARD_OUT * core:SHARD_OUT * (core + 1), :]

        @pl.when(i >= out_steps)
        def _():
            before_tile(i - out_steps)
            tile = lax.dot_general(a_ref[...], b_ref[...], _TN, preferred_element_type=F32).astype(BF16)
            dwt[pl.ds(pl.multiple_of(in_tile(table_ref, i) * tn, tn), tn), :] = tile

        @pl.when(i == steps - 2)
        def _():
            gwo_ref[...] = rs_finish()

        @pl.when(i == steps - 1)
        def _():
            after_tiles()
            chip_finish()

    vmem = pl.BlockSpec(memory_space=pltpu.VMEM)
    grid_spec = pltpu.PrefetchScalarGridSpec(
        num_scalar_prefetch=1, grid=(steps,),
        in_specs=[pl.BlockSpec((SEQ, tn_out), lambda i, table_ref: (0, out_tile(i))), vmem,
                  pl.BlockSpec((SEQ, tn), lambda i, table_ref: (0, in_tile(table_ref, i))),
                  pl.BlockSpec(memory_space=pl.ANY)],
        out_specs=(pl.BlockSpec(memory_space=pl.ANY), pl.BlockSpec((SHARD_OUT, D_MODEL), lambda i, table_ref: (0, 0))),
        scratch_shapes=[pltpu.VMEM((N_DEV, SHARD_OUT, D_MODEL), BF16),
                        pltpu.VMEM((D_PROJ, D_MODEL), BF16), pltpu.VMEM((3, SHARD_IN, D_MODEL), BF16),
                        pltpu.VMEM((2, HALF_IN, D_MODEL), BF16),
                        *_shard_sum_scratch(SHARD_OUT), pltpu.VMEM((SEQ, D_MODEL), BF16),
                        pltpu.SemaphoreType.DMA((N_SHARD_SUM_SEMS + N_CHIP_SUM_SEMS,)),
                        pltpu.SemaphoreType.DMA((N_SHARD_SUM_SEMS + N_CHIP_SUM_SEMS,)),
                        pltpu.SemaphoreType.DMA((9,))])
    return pl.pallas_call(
        body, name="dw", grid_spec=grid_spec,
        out_shape=(jax.ShapeDtypeStruct((4, SHARD_IN, D_MODEL), BF16), jax.ShapeDtypeStruct((SHARD_OUT, D_MODEL), F32)),
        compiler_params=_params(dimension_semantics=("arbitrary",)),
    )(table, mixed, dx2b, dproj, h)


def _adam_all(big_in, big_out, gsum, small, grad_x):
    n_chunks = 4
    n_big = 8

    def body(*refs):
        ins, outs = refs[:n_big + 1 + 18 + 1], refs[n_big + 1 + 18 + 1:n_big + 1 + 18 + 1 + 34]
        in_bufs, out_bufs, gx_buf = refs[-n_big - 6 - 4:-6 - 4], refs[-6 - 4:-4], refs[-4]
        in_sems, out_sems, gx_sems = refs[-3:]

        def gx_rows(j):
            return pl.ds(j * (SEQ // n_chunks), SEQ // n_chunks)

        def gx_load(j):
            return pltpu.make_async_copy(ins[27].at[gx_rows(j), :], gx_buf.at[gx_rows(j), :], gx_sems.at[j])

        def gx_store(j):
            return pltpu.make_async_copy(gx_buf.at[gx_rows(j), :], outs[33].at[gx_rows(j), :], gx_sems.at[n_chunks + j])

        def rows(a, j):
            tr = ins[a].shape[0] // n_chunks
            return pl.ds(j * tr, tr)

        def load(a, j):
            return pltpu.make_async_copy(ins[a].at[rows(a, j), :], in_bufs[a].at[rows(a, j), :], in_sems.at[a * n_chunks + j])

        def store(a, j):
            b, kind = divmod(a, 4)
            src = in_bufs[4 * b + 1] if kind == 0 else out_bufs[3 * b + kind - 1]
            return pltpu.make_async_copy(src.at[rows(a, j), :], outs[a].at[rows(a, j), :], out_sems.at[a * n_chunks + j])

        for j in range(n_chunks):
            for a in range(n_big):
                load(a, j).start()
            gx_load(j).start()

        def small_weights():
            gsum = ins[8][...]
            idx = _slot(lax.axis_index("x"), lax.axis_index("y"), lax.axis_index("c"))
            cg = jnp.zeros((3, SHARD_CONV), F32)
            for d in range(N_DEV):
                cg = jnp.where(idx == d, gsum[ROW_CONV0:ROW_CONV0 + 3, d * SHARD_CONV:(d + 1) * SHARD_CONV], cg)
            grads = (gsum[ROW_NORM_IN:ROW_NORM_IN + 1], gsum[ROW_SINKS:ROW_SINKS + 1, 0:N_Q_HEADS],
                     gsum[ROW_NORM_CONV:ROW_NORM_CONV + 1], gsum[ROW_NORM_ATTN:ROW_NORM_ATTN + 1],
                     gsum[ROW_NORM_FINAL:ROW_NORM_FINAL + 1], cg)
            for s, g in enumerate(grads):
                at = (slice(None), 0, slice(None)) if s == 5 else (slice(None), slice(None))
                w_ref, m_ref, v_ref = ins[9 + 3 * s:12 + 3 * s]
                delta, mn, vn = _adamw(w_ref[at], g, m_ref[at], v_ref[at])
                for ref, val in zip(outs[8 + 4 * s:12 + 4 * s], (g, delta, mn, vn)):
                    ref[at] = val
            outs[32][...] = gsum[ROW_SINKS:ROW_SINKS + 1, LOSS_LANE:LOSS_LANE + 1]

        small_weights()
        for j in range(n_chunks):
            for b in range(2):
                for a in range(4 * b, 4 * b + 4):
                    load(a, j).wait()
                w_buf, g_buf, m_buf, v_buf = in_bufs[4 * b:4 * b + 4]
                r = rows(4 * b, j)
                results = _adamw(w_buf[r, :], g_buf[r, :], m_buf[r, :], v_buf[r, :])
                for buf, val in zip(out_bufs[3 * b:3 * b + 3], results):
                    buf[r, :] = val
                for a in range(4 * b, 4 * b + 4):
                    store(a, j).start()
            gx_load(j).wait()
            gx_store(j).start()
        for j in range(n_chunks):
            for a in range(n_big):
                store(a, j).wait()
            gx_store(j).wait()

    vmem, hbm = pl.BlockSpec(memory_space=pltpu.VMEM), pl.BlockSpec(memory_space=pl.ANY)
    small_shapes = [a.shape for a in small[::3]]
    big_shapes = [(SHARD_IN, D_MODEL)] * 4 + [(SHARD_OUT, D_MODEL)] * 4
    out_shape = ([jax.ShapeDtypeStruct(s, F32) for s in big_shapes]
                 + [jax.ShapeDtypeStruct(s, F32) for s in small_shapes for _ in range(4)]
                 + [jax.ShapeDtypeStruct((1, 1), F32), jax.ShapeDtypeStruct((SEQ, D_MODEL), F32)])
    outs = pl.pallas_call(
        body, name="adam", in_specs=[hbm] * n_big + [vmem] * (1 + len(small)) + [hbm],
        out_specs=tuple([hbm] * n_big + [vmem] * (4 * len(small_shapes) + 1) + [hbm]), out_shape=tuple(out_shape),
        scratch_shapes=[pltpu.VMEM(s, F32) for s in big_shapes]
                       + [pltpu.VMEM(s, F32) for s in [(SHARD_IN, D_MODEL)] * 3 + [(SHARD_OUT, D_MODEL)] * 3]
                       + [pltpu.VMEM((SEQ, D_MODEL), F32),
                          pltpu.SemaphoreType.DMA((n_big * n_chunks,)), pltpu.SemaphoreType.DMA((n_big * n_chunks,)),
                          pltpu.SemaphoreType.DMA((2 * n_chunks,))],
        compiler_params=_params(),
    )(*big_in, *big_out, gsum, *small, grad_x)
    return outs[0:4], outs[4:8], [outs[8 + 4 * s:12 + 4 * s] for s in range(6)], outs[32], outs[33]


def _rows_first(a):
    return jnp.transpose(a, (1, 0, 2))


def kernel(x, norm_in, w_in, conv_w, attn_sinks, norm_conv_out, norm_attn_out, w_out, norm_final, loss_target, m_norm_in, m_w_in, m_conv_w, m_attn_sinks, m_norm_conv_out, m_norm_attn_out, m_w_out, m_norm_final, v_norm_in, v_w_in, v_conv_w, v_attn_sinks, v_norm_conv_out, v_norm_attn_out, v_w_out, v_norm_final):
    x2d = x.reshape(SEQ, D_MODEL)
    target = loss_target.reshape(SEQ, D_MODEL)
    nf = norm_final.reshape(1, D_MODEL)

    w_in_t, m_w_in_t, v_w_in_t = w_in[0].T, m_w_in[0].T, v_w_in[0].T
    tiles = jnp.asarray(TILE_ORDER, jnp.int32).reshape(-1)
    w_in_full, h, proj, g_out, conv_full = _gather_in_proj(x2d, norm_in, w_in_t, w_out[0], _rows_first(conv_w), tiles)
    sinks = attn_sinks.reshape(N_Q_HEADS)

    mixed, attn, probs, shares = _mix_fwd(proj, conv_full, sinks, norm_conv_out, norm_attn_out)
    dx2, dx2b, dmixed, gnf, loss_part = _out_proj_loss(mixed, x2d, target, g_out.reshape(D_MIX, D_MODEL), nf)
    dproj, gslab = _mix_bwd(proj, dmixed, attn, probs, shares, conv_full, norm_conv_out, norm_attn_out)
    dw_in_chip, g_w_out = _dw_rs(mixed, dx2b, dproj, h, jnp.asarray(DW_TABLE, jnp.int32).reshape(-1))
    grad_x, g_w_in, gsum = _in_bwd_rs(dproj, w_in_full, x2d, dx2, norm_in, dw_in_chip, gslab, gnf, loss_part)

    small = (norm_in, m_norm_in, v_norm_in, attn_sinks, m_attn_sinks, v_attn_sinks,
             norm_conv_out, m_norm_conv_out, v_norm_conv_out, norm_attn_out, m_norm_attn_out, v_norm_attn_out,
             nf, m_norm_final.reshape(1, D_MODEL), v_norm_final.reshape(1, D_MODEL),
             _rows_first(conv_w), _rows_first(m_conv_w), _rows_first(v_conv_w))
    big_in, big_out, (s_ni, s_sk, s_nc, s_na, s_nf, s_cv), loss, grad_x = _adam_all(
        (w_in_t, g_w_in, m_w_in_t, v_w_in_t), (w_out[0], g_w_out, m_w_out[0], v_w_out[0]), gsum, small, grad_x)

    def leaves(k):
        return (s_ni[k], big_in[k].T[None], jnp.transpose(s_cv[k], (1, 0, 2)), s_sk[k], s_nc[k], s_na[k], big_out[k][None],
                s_nf[k].reshape(D_MODEL))

    return (loss.reshape(()), grad_x.reshape(1, SEQ, D_MODEL), *leaves(0), *leaves(1), *leaves(2), *leaves(3))
```

```python
import jax
import jax.numpy as jnp
from jax import lax
from jax.experimental import pallas as pl
from jax.experimental.pallas import tpu as pltpu

F32 = jnp.float32
BF16 = jnp.bfloat16
MESH = pl.DeviceIdType.MESH

N_DEV = 8
SEQ = 2048
D_MODEL = 1024
D_CONV = 1024
D_ATTN = 1024
D_KV = 128
HEAD_DIM = 64
N_Q_HEADS = 16
N_PAIRS = N_Q_HEADS // 2
PAIRS_PER_KV = N_PAIRS // 2
D_MIX = D_CONV + D_ATTN
D_PROJ = 6400
SHARD_IN = D_PROJ // N_DEV
SHARD_OUT = D_MIX // N_DEV
SHARD_CONV = D_CONV // N_DEV
OFF_CB, OFF_CC, OFF_CU, OFF_GC, OFF_Q, OFF_K, OFF_V, OFF_GA = 0, 1024, 2048, 3072, 4096, 5120, 5248, 5376
BLOCK = 128
N_BLOCKS = SEQ // BLOCK
HALO = 8
CHUNK = 16
N_CHUNKS = BLOCK // CHUNK
RMS_EPS = 1e-5
NEG = -1e30
SCALE = HEAD_DIM ** -0.5
SLOPES = tuple(2.0 ** (-8.0 * (h + 1) / N_Q_HEADS) for h in range(N_Q_HEADS))

ADAM_LR = 0.001
ADAM_B1 = 0.9
ADAM_B2 = 0.999
ADAM_EPS = 1e-08
ADAM_WD = 0.01
ADAM_STEP = 10

ROW_NORM_IN, ROW_NORM_CONV, ROW_NORM_ATTN, ROW_NORM_FINAL, ROW_CONV0, ROW_SINKS = 0, 1, 2, 3, 4, 7
LOSS_LANE = N_Q_HEADS
ACC_NORM_CONV, ACC_NORM_ATTN, ACC_CONV0, N_ACC = 0, 1, 2, 5

VMEM_LIMIT = 56 * 1024 * 1024

_NT = (((1,), (1,)), ((), ()))
_TN = (((0,), (0,)), ((), ()))


def _params(**kw):
    return pltpu.CompilerParams(vmem_limit_bytes=VMEM_LIMIT, **kw)


def _adamw(w, g, m, v):
    m = ADAM_B1 * m + (1.0 - ADAM_B1) * g
    v = ADAM_B2 * v + (1.0 - ADAM_B2) * (g * g)
    m_hat = m / (1.0 - ADAM_B1 ** ADAM_STEP)
    v_hat = v / (1.0 - ADAM_B2 ** ADAM_STEP)
    delta = -ADAM_LR * (m_hat / (jnp.sqrt(v_hat) + ADAM_EPS) + ADAM_WD * w)
    return delta, m, v


def _sigmoid(t):
    return 1.0 / (1.0 + jnp.exp(-t))


def _slot(px, py, pc):
    return 4 * px + 2 * py + pc


OWN, NX, NY, DG = range(4)
HALF_IN = SHARD_IN // 2
N_GATHER_KINDS = 13
W_OUT_KINDS = N_GATHER_KINDS + 7


IN_PROJ_TILE = 640
TILE_ORDER = ((0, 1, 2, 3, 4, 5, 6, 7, 8, 9), (3, 4, 0, 1, 2, 8, 9, 5, 6, 7),
              (5, 6, 0, 1, 7, 8, 9, 2, 3, 4), (8, 9, 3, 4, 5, 6, 7, 0, 1, 2))
TILES_OWN, TILES_NEIGHBOURS = 2, 7


def _tile(table_ref, p):
    chip = 2 * lax.axis_index("x") + lax.axis_index("y")
    return table_ref[chip * len(TILE_ORDER[0]) + p]


DW_TILE_ORDER = tuple(tuple(reversed(row)) for row in TILE_ORDER)


def _tiles_until_complete(chip, owner):
    lo, hi = owner * 2 * SHARD_IN, (owner + 1) * 2 * SHARD_IN
    touching = [t for t in range(len(TILE_ORDER[0])) if t * IN_PROJ_TILE < hi and (t + 1) * IN_PROJ_TILE > lo]
    return 1 + max(DW_TILE_ORDER[chip].index(t) for t in touching)


DW_TABLE = tuple(DW_TILE_ORDER[chip] + tuple(_tiles_until_complete(chip, chip ^ flip) for flip in (0, 2, 1, 3))
                 for chip in range(4))


def _dw_entry(table_ref, p):
    chip = 2 * lax.axis_index("x") + lax.axis_index("y")
    return table_ref[chip * len(DW_TABLE[0]) + p]


def _gather_in_proj(x, norm_in, w_in_sh, w_out_sh, conv_sh, tiles):
    tn = IN_PROJ_TILE
    steps = D_PROJ // tn
    tm = 256

    def body(tiles_ref, x_hbm, g_ref, win_ref, wout_ref, cv_ref, wt_ref, h_ref, proj_ref, gout_ref, conv_ref,
             gin_ref, gcv_ref, wob_ref, x_ref, send_sems, recv_sems, local_sems):
        p = pl.program_id(0)
        local_sem = local_sems.at[0]
        x, y, c = lax.axis_index("x"), lax.axis_index("y"), lax.axis_index("c")
        me, sibling = (x, y, c), (x, y, 1 - c)
        nx, ny, dg = (1 - x, y, c), (x, 1 - y, c), (1 - x, 1 - y, c)

        def other(dev):
            return (dev[0], dev[1], 1 - dev[2])

        def shard(dev):
            return gin_ref.at[pl.ds(pl.multiple_of(_slot(*dev) * SHARD_IN, 16), SHARD_IN), :]

        def half(dev, h):
            return gin_ref.at[pl.ds(pl.multiple_of(_slot(*dev) * SHARD_IN + h * HALF_IN, 16), HALF_IN), :]

        def rc(ref, k, to):
            return pltpu.make_async_remote_copy(src_ref=ref, dst_ref=ref, send_sem=send_sems.at[k],
                                                recv_sem=recv_sems.at[k], device_id=to, device_id_type=MESH)

        def cv(k, dev, to):
            s = _slot(*dev)
            return pltpu.make_async_remote_copy(src_ref=gcv_ref.at[s], dst_ref=gcv_ref.at[s],
                                                send_sem=send_sems.at[N_GATHER_KINDS + k],
                                                recv_sem=recv_sems.at[N_GATHER_KINDS + k], device_id=to, device_id_type=MESH)

        def own_copies():
            return [rc(shard(me), 0, sibling),
                    rc(half(me, 0), 1, nx), rc(half(me, 1), 2, nx),
                    rc(half(me, 1), 4, ny), rc(half(me, 0), 3, ny),
                    cv(0, me, sibling)] + [cv(1 + j, me, peer) for j, peer in enumerate((nx, ny, dg))]

        def pass_on(dev, h, k_in, k_ici, k_d2d, half=half, base=0):
            rc(half(dev, h), base + k_in, me).wait_recv()
            if k_ici is not None:
                rc(half(dev, h), base + k_ici, ny if dev is nx else nx).start()
            rc(half(dev, h), base + k_d2d, sibling).start()

        def out_half(dev, h):
            return gout_ref.at[_slot(*dev), pl.ds(h * (SHARD_OUT // 2), SHARD_OUT // 2), :]

        def own_out_copies():
            src = lambda h: wob_ref.at[pl.ds(h * (SHARD_OUT // 2), SHARD_OUT // 2), :]

            def send(ref, dst, k, to):
                return pltpu.make_async_remote_copy(src_ref=ref, dst_ref=dst, send_sem=send_sems.at[W_OUT_KINDS + k],
                                                    recv_sem=recv_sems.at[W_OUT_KINDS + k], device_id=to, device_id_type=MESH)

            return [send(wob_ref, gout_ref.at[_slot(*me)], 0, sibling),
                    send(src(0), out_half(me, 0), 1, nx), send(src(1), out_half(me, 1), 2, nx),
                    send(src(1), out_half(me, 1), 4, ny), send(src(0), out_half(me, 0), 3, ny)]

        def own_out_local():
            return pltpu.make_async_copy(wob_ref, gout_ref.at[_slot(*me)], local_sems.at[1])

        @pl.when(p == 0)
        def _():
            gin_ref[pl.ds(pl.multiple_of(_slot(*me) * SHARD_IN, 16), SHARD_IN), :] = win_ref[...].astype(BF16)
            gcv_ref[_slot(*me)] = jnp.zeros((8, SHARD_CONV), F32)
            gcv_ref[_slot(*me), 0:3, :] = cv_ref[:, 0, :]
            for cp in own_copies():
                cp.start()
            wob_ref[...] = wout_ref[...].astype(BF16)
            x_load = pltpu.make_async_copy(x_hbm, x_ref, local_sems.at[2])
            x_load.start()
            x_load.wait()
            for t in range(SEQ // tm):
                xv = x_ref[tm * t:tm * (t + 1), :]
                r = lax.rsqrt(jnp.mean(xv * xv, axis=-1, keepdims=True) + RMS_EPS)
                h_ref[tm * t:tm * (t + 1), :] = (xv * r * g_ref[...]).astype(BF16)
            rc(shard(sibling), 0, me).wait_recv()

        @pl.when(p == TILES_OWN)
        def _():
            for args in ((nx, 0, 1, 5, 7), (ny, 1, 4, 6, 10), (nx, 1, 2, None, 8), (ny, 0, 3, None, 9)):
                pass_on(*args)
            for j, peer in enumerate((nx, ny, dg)):
                cv(1 + j, peer, me).wait_recv()
                cv(4 + j, peer, sibling).start()
            for (dev, h), k in (((nx, 0), 7), ((nx, 1), 8), ((ny, 0), 9), ((ny, 1), 10)):
                rc(half(other(dev), h), k, me).wait_recv()
            own_out_local().start()
            for cp in own_out_copies():
                cp.start()

        @pl.when(p == TILES_NEIGHBOURS - 1)
        def _():
            pass_on(dg, 0, 5, None, 11)
            pass_on(dg, 1, 6, None, 12)

        @pl.when(p == TILES_NEIGHBOURS)
        def _():
            for (dev, h), k in (((dg, 0), 11), ((dg, 1), 12)):
                rc(half(other(dev), h), k, me).wait_recv()
            pltpu.make_async_copy(gin_ref, wt_ref, local_sem).start()

        @pl.when(p == steps - 2)
        def _():
            for args in ((nx, 0, 1, 5, 7), (ny, 1, 4, 6, 10), (nx, 1, 2, None, 8), (ny, 0, 3, None, 9)):
                pass_on(*args, half=out_half, base=W_OUT_KINDS)

        w = gin_ref[pl.ds(pl.multiple_of(_tile(tiles_ref, p) * tn, tn), tn), :]
        proj_ref[...] = lax.dot_general(h_ref[...], w, _NT, preferred_element_type=F32)

        @pl.when(p == steps - 1)
        def _():
            cv(0, sibling, me).wait_recv()
            for j, peer in enumerate((nx, ny, dg)):
                cv(4 + j, other(peer), me).wait_recv()
            for d in range(N_DEV):
                conv_ref[:, d * SHARD_CONV:(d + 1) * SHARD_CONV] = gcv_ref[d]
            relayed = [rc(half(nx, 0), 5, ny), rc(half(ny, 1), 6, nx)]
            relayed += [rc(half(dev, h), k, sibling) for (dev, h), k in
                        (((nx, 0), 7), ((nx, 1), 8), ((ny, 0), 9), ((ny, 1), 10), ((dg, 0), 11), ((dg, 1), 12))]
            relayed += [cv(4 + j, peer, sibling) for j, peer in enumerate((nx, ny, dg))]
            for cp in own_copies() + relayed:
                cp.wait_send()
            pltpu.make_async_copy(gin_ref, wt_ref, local_sem).wait()
            pass_on(dg, 0, 5, None, 11, half=out_half, base=W_OUT_KINDS)
            pass_on(dg, 1, 6, None, 12, half=out_half, base=W_OUT_KINDS)
            rc(gout_ref.at[_slot(*sibling)], W_OUT_KINDS, me).wait_recv()
            out_relayed = [rc(out_half(nx, 0), W_OUT_KINDS + 5, ny), rc(out_half(ny, 1), W_OUT_KINDS + 6, nx)]
            for (dev, h), k in (((nx, 0), 7), ((nx, 1), 8), ((ny, 0), 9), ((ny, 1), 10), ((dg, 0), 11), ((dg, 1), 12)):
                rc(out_half(other(dev), h), W_OUT_KINDS + k, me).wait_recv()
                out_relayed.append(rc(out_half(dev, h), W_OUT_KINDS + k, sibling))
            for cp in own_out_copies() + out_relayed:
                cp.wait_send()
            own_out_local().wait()

    vmem = pl.BlockSpec(memory_space=pltpu.VMEM)
    grid_spec = pltpu.PrefetchScalarGridSpec(
        num_scalar_prefetch=1, grid=(steps,),
        in_specs=[pl.BlockSpec(memory_space=pl.ANY), vmem, vmem, vmem, vmem],
        out_specs=(pl.BlockSpec(memory_space=pl.ANY), vmem,
                   pl.BlockSpec((SEQ, tn), lambda p, tiles_ref: (0, _tile(tiles_ref, p))),
                   pl.BlockSpec(memory_space=pl.ANY), vmem),
        scratch_shapes=[pltpu.VMEM((D_PROJ, D_MODEL), BF16), pltpu.VMEM((N_DEV, 8, SHARD_CONV), F32),
                        pltpu.VMEM((SHARD_OUT, D_MODEL), BF16), pltpu.VMEM((SEQ, D_MODEL), F32),
                        pltpu.SemaphoreType.DMA((W_OUT_KINDS + N_GATHER_KINDS,)),
                        pltpu.SemaphoreType.DMA((W_OUT_KINDS + N_GATHER_KINDS,)),
                        pltpu.SemaphoreType.DMA((3,))])
    return pl.pallas_call(
        body, name="gather_in_proj", grid_spec=grid_spec,
        out_shape=(jax.ShapeDtypeStruct((D_PROJ, D_MODEL), BF16), jax.ShapeDtypeStruct((SEQ, D_MODEL), BF16),
                   jax.ShapeDtypeStruct((SEQ, D_PROJ), F32), jax.ShapeDtypeStruct((N_DEV, SHARD_OUT, D_MODEL), BF16),
                   jax.ShapeDtypeStruct((8, D_CONV), F32)),
        compiler_params=_params(dimension_semantics=("arbitrary",)),
    )(tiles, x, norm_in, w_in_sh, w_out_sh, conv_sh)


def _shard_sum(src, own, d2d, ici, send_sems, recv_sems, local_sems, base=0):
    x, y, c = lax.axis_index("x"), lax.axis_index("y"), lax.axis_index("c")
    sibling = (x, y, 1 - c)
    chips = [(x, y), (1 - x, y), (x, 1 - y), (1 - x, 1 - y)]

    def rcopy(s, d, k, to):
        return pltpu.make_async_remote_copy(src_ref=s, dst_ref=d, send_sem=send_sems.at[base + k],
                                            recv_sem=recv_sems.at[base + k], device_id=to, device_id_type=MESH)

    def mine(k):
        return pltpu.make_async_copy(src.at[_slot(*chips[k], c)], own.at[k], local_sems.at[k])

    def to_sibling(k):
        return rcopy(src.at[_slot(*chips[k], 1 - c)], d2d.at[k], k, sibling)

    def to_chip(k):
        return rcopy(own.at[k], ici.at[k - 1], 3 + k, (*chips[k], c))

    def start(k):
        mine(k).start()
        to_sibling(k).start()

    def forward(k):
        mine(k).wait()
        to_sibling(k).wait_recv()
        own[k] = (own[k].astype(F32) + d2d[k].astype(F32)).astype(BF16)
        to_chip(k).start()

    def finish():
        mine(0).wait()
        to_sibling(0).wait_recv()
        acc = own[0].astype(F32) + d2d[0].astype(F32)
        for k in range(1, 4):
            to_chip(k).wait_recv()
            acc = acc + ici[k - 1].astype(F32)
        for k in range(4):
            to_sibling(k).wait_send()
        for k in range(1, 4):
            to_chip(k).wait_send()
        return acc

    return start, forward, finish


def _shard_sum_scratch(rows):
    return [pltpu.VMEM((4, rows, D_MODEL), BF16), pltpu.VMEM((4, rows, D_MODEL), BF16),
            pltpu.VMEM((3, rows, D_MODEL), BF16)]


N_SHARD_SUM_SEMS = 7


N_CHIP_SUM_SEMS = 5


def _chip_sum(dwt, d2d, via, out_hbm, tiles_until, send_sems, recv_sems, local_sems, base, local_base):
    x, y, c = lax.axis_index("x"), lax.axis_index("y"), lax.axis_index("c")
    sibling, nx, ny = (x, y, 1 - c), (1 - x, y, c), (x, 1 - y, c)
    chips = [(x, y), (1 - x, y), (x, 1 - y), (1 - x, 1 - y)]

    def shard(s):
        return dwt.at[pl.ds(pl.multiple_of(s * SHARD_IN, 16), SHARD_IN), :]

    def half(ref, h):
        return ref.at[pl.ds(h * HALF_IN, HALF_IN), :]

    def rc(s, d, k, to):
        return pltpu.make_async_remote_copy(src_ref=s, dst_ref=d, send_sem=send_sems.at[base + k],
                                            recv_sem=recv_sems.at[base + k], device_id=to, device_id_type=MESH)

    def to_sibling(k):
        return rc(shard(_slot(*chips[k], 1 - c)), d2d.at[k - 1], k - 1, sibling)

    for_dg = (lambda: rc(half(d2d.at[DG - 1], 0), via.at[0], 3, nx), lambda: rc(half(d2d.at[DG - 1], 1), via.at[1], 4, ny))

    def save(k):
        return pltpu.make_async_copy(d2d.at[k - 1], out_hbm.at[k], local_sems.at[local_base + k])

    own_saves = (lambda: pltpu.make_async_copy(shard(_slot(x, y, c)), out_hbm.at[OWN], local_sems.at[local_base]),
                 lambda: pltpu.make_async_copy(shard(_slot(x, y, 1 - c)), out_hbm.at[3], local_sems.at[local_base + 3]))

    def before_tile(n):
        for k in (NX, NY, DG):
            @pl.when(tiles_until(k) == n)
            def _():
                to_sibling(k).start()

            @pl.when(tiles_until(k) + 1 == n)
            def _():
                to_sibling(k).wait_recv()
                d2d[k - 1] = (shard(_slot(*chips[k], c))[...].astype(F32) + d2d[k - 1].astype(F32)).astype(BF16)
                if k == DG:
                    for cp in for_dg:
                        cp().start()

    def after_tiles():
        for cp in own_saves:
            cp().start()

    def finish():
        for k, h in ((NY, 0), (NX, 1)):
            for_dg[h]().wait_recv()
            rows = pl.ds(h * HALF_IN, HALF_IN)
            d2d[k - 1, rows, :] = (d2d[k - 1, rows, :].astype(F32) + via[h].astype(F32)).astype(BF16)
            save(k).start()
        for cp in own_saves + (lambda: save(NX), lambda: save(NY)):
            cp().wait()
        for cp in (lambda: to_sibling(NX), lambda: to_sibling(NY), lambda: to_sibling(DG)) + for_dg:
            cp().wait_send()

    return before_tile, after_tiles, finish


N_ICI_SUM_SEMS = 3


def _ici_sum(src, own, d2d, ici, send_sems, recv_sems, local_sems, base=0):
    x, y, c = lax.axis_index("x"), lax.axis_index("y"), lax.axis_index("c")

    def rc(s, d, k, to):
        return pltpu.make_async_remote_copy(src_ref=s, dst_ref=d, send_sem=send_sems.at[base + k],
                                            recv_sem=recv_sems.at[base + k], device_id=to, device_id_type=MESH)

    copies = (lambda: rc(src.at[NX], ici.at[0], 0, (1 - x, y, c)), lambda: rc(src.at[NY], ici.at[1], 1, (x, 1 - y, c)),
              lambda: rc(src.at[3], d2d, 2, (x, y, 1 - c)))
    mine = lambda: pltpu.make_async_copy(src.at[OWN], own, local_sems.at[0])

    def start():
        for cp in copies + (mine,):
            cp().start()

    def finish():
        mine().wait()
        for cp in copies:
            cp().wait_recv()
        acc = own[...].astype(F32) + d2d[...].astype(F32) + ici[0].astype(F32) + ici[1].astype(F32)
        for cp in copies:
            cp().wait_send()
        return acc

    return start, finish


def _slab_sum(myslab, slabs, send_sems, recv_sems, base):
    x, y, c = lax.axis_index("x"), lax.axis_index("y"), lax.axis_index("c")
    me = _slot(x, y, c)
    peers = [(x, y, 1 - c), (1 - x, y, c), (x, 1 - y, c), (1 - x, 1 - y, c),
             (1 - x, y, 1 - c), (x, 1 - y, 1 - c), (1 - x, 1 - y, 1 - c)]

    def cp(k):
        return pltpu.make_async_remote_copy(src_ref=myslab, dst_ref=slabs.at[me], send_sem=send_sems.at[base + k],
                                            recv_sem=recv_sems.at[base + k], device_id=peers[k], device_id_type=MESH)

    def start():
        slabs[me] = myslab[...]
        for k in range(7):
            cp(k).start()

    def finish():
        for k in range(7):
            cp(k).wait_recv()
        total = slabs[0]
        for d in range(1, N_DEV):
            total = total + slabs[d]
        for k in range(7):
            cp(k).wait_send()
        return total

    return start, finish


def _chunk_rows(r):
    return slice(r * CHUNK, (r + 1) * CHUNK)


def _conv_halo(cch_ref, cuh_ref, n):
    zh = jnp.where(n > 0, cch_ref[...] * cuh_ref[...], 0.0)
    return jnp.concatenate([zh] * (CHUNK // HALO), axis=0)


def _conv_chunk(pj_ref, zhalo, cw, r):
    rows = _chunk_rows(r)
    cc = pj_ref[rows, OFF_CC:OFF_CC + D_CONV]
    cu = pj_ref[rows, OFF_CU:OFF_CU + D_CONV]
    z = cc * cu
    before = _chunk_rows(r - 1)
    zprev = pj_ref[before, OFF_CC:OFF_CC + D_CONV] * pj_ref[before, OFF_CU:OFF_CU + D_CONV] if r > 0 else zhalo
    row = lax.broadcasted_iota(jnp.int32, (CHUNK, D_CONV), 0)
    z1 = jnp.where(row < 1, pltpu.roll(zprev, 1, 0), pltpu.roll(z, 1, 0))
    z2 = jnp.where(row < 2, pltpu.roll(zprev, 2, 0), pltpu.roll(z, 2, 0))
    co = cw[0] * z2 + cw[1] * z1 + cw[2] * z
    return cc, cu, z, z1, z2, co


def _gated_norm(a, gain, t):
    r = lax.rsqrt(jnp.mean(a * a, axis=-1, keepdims=True) + RMS_EPS)
    return a * r * gain * (t * _sigmoid(t))


def _kv_bands(pj, kvp_ref):
    lane = lax.broadcasted_iota(jnp.int32, (2 * BLOCK, D_KV), 1)
    lo = lane < HEAD_DIM

    def bands(prev, cur):
        b = jnp.concatenate([prev, cur], axis=0)
        br = pltpu.roll(b, HEAD_DIM, 1)
        zero = jnp.zeros_like(b)
        return ((jnp.where(lo, b, zero).astype(BF16), jnp.where(lo, zero, br).astype(BF16)),
                (jnp.where(lo, br, zero).astype(BF16), jnp.where(lo, zero, b).astype(BF16)))

    ks = bands(kvp_ref[:, 0:D_KV], pj[:, OFF_K:OFF_K + D_KV])
    vs = bands(kvp_ref[:, D_KV:2 * D_KV], pj[:, OFF_V:OFF_V + D_KV])
    return ks, vs


STACK = PAIRS_PER_KV * BLOCK


def _head(j, i, e):
    return 2 * (PAIRS_PER_KV * j + i) + e


def _pair_cols(j, i, off):
    p = PAIRS_PER_KV * j + i
    return slice(off + 128 * p, off + 128 * (p + 1))


def _fill_attn_bias(bias_scr, first_block):
    qi = lax.broadcasted_iota(jnp.int32, (BLOCK, 2 * BLOCK), 0)
    kj = lax.broadcasted_iota(jnp.int32, (BLOCK, 2 * BLOCK), 1)
    dist = BLOCK + qi - kj
    valid = (dist >= 0) & (dist < BLOCK)
    if first_block:
        valid = valid & (kj >= BLOCK)
    distf = dist.astype(F32)
    for j in range(2):
        for e in range(2):
            for i in range(PAIRS_PER_KV):
                bias_scr[2 * j + e, BLOCK * i:BLOCK * (i + 1), :] = jnp.where(valid, -SLOPES[_head(j, i, e)] * distf, NEG)


def _q_stack(pj, j):
    return jnp.concatenate([(pj[:, _pair_cols(j, i, OFF_Q)] * SCALE).astype(BF16) for i in range(PAIRS_PER_KV)], axis=0)


def _attn_probs(q_stack, kband, bias_ref, sinks):
    s = lax.dot_general(q_stack, kband, _NT, preferred_element_type=F32)
    ones = jnp.ones((128, 128), BF16)
    probs, shares = [], []
    for i, sink in enumerate(sinks):
        rows = slice(BLOCK * i, BLOCK * (i + 1))
        t = s[rows, :] + bias_ref[rows, :]
        m = jnp.broadcast_to(jnp.max(t, axis=-1, keepdims=True), (BLOCK, 128))
        m = jnp.maximum(m, sink)
        p = [jnp.exp(t[:, :128] - m), jnp.exp(t[:, 128:] - m)]
        es = jnp.exp(sink - m)
        total = (jnp.dot(p[0].astype(BF16), ones, preferred_element_type=F32)
                 + jnp.dot(p[1].astype(BF16), ones, preferred_element_type=F32))
        inv = 1.0 / (total + es)
        probs.append(jnp.concatenate([p[0] * inv, p[1] * inv], axis=1))
        shares.append(es * inv)
    return jnp.concatenate(probs, axis=0), jnp.concatenate(shares, axis=0)


def _attn_group(pj, ks, vs, bias_scr, sink_ref, j):
    q_stack = _q_stack(pj, j)
    out, probs, shares = None, [], []
    for e in range(2):
        p, ps = _attn_probs(q_stack, ks[j][e], bias_scr.at[2 * j + e],
                            [sink_ref[_head(j, i, e)] for i in range(PAIRS_PER_KV)])
        p = p.astype(BF16)
        o = jnp.dot(p, vs[j][e], preferred_element_type=F32)
        out = o if out is None else out + o
        probs.append(p)
        shares.append(ps)
    return out, probs, shares


def _mix_fwd(proj, conv_full, sinks, norm_conv, norm_attn):
    def body(pj_ref, kvp_ref, cch_ref, cuh_ref, cw_ref, sink_ref, gc_ref, ga_ref,
             mixed_ref, attn_scr, p_ref, ps_ref, bias_scr):
        n = pl.program_id(0)
        pj = pj_ref

        @pl.when(n == 0)
        def _():
            _fill_attn_bias(bias_scr, first_block=True)

        @pl.when(n == 1)
        def _():
            _fill_attn_bias(bias_scr, first_block=False)

        zhalo = _conv_halo(cch_ref, cuh_ref, n)
        cw = (cw_ref[0:1, :], cw_ref[1:2, :], cw_ref[2:3, :])
        gain_c = gc_ref[...]

        for r in range(N_CHUNKS):
            rows = _chunk_rows(r)
            co = _conv_chunk(pj_ref, zhalo, cw, r)[-1]
            y = _gated_norm(pj_ref[rows, OFF_CB:OFF_CB + D_CONV] * co, gain_c, pj_ref[rows, OFF_GC:OFF_GC + D_CONV])
            mixed_ref[rows, 0:D_CONV] = y.astype(BF16)

        ks, vs = _kv_bands(pj, kvp_ref)
        for j in range(2):
            out, probs, shares = _attn_group(pj, ks, vs, bias_scr, sink_ref, j)
            for e in range(2):
                p_ref[0, 2 * j + e] = probs[e]
                ps_ref[0, 2 * j + e] = shares[e]
            for i in range(PAIRS_PER_KV):
                attn_scr[:, _pair_cols(j, i, 0)] = out[BLOCK * i:BLOCK * (i + 1), :]
        gain_a = ga_ref[...]

        for r in range(N_CHUNKS):
            rows = _chunk_rows(r)
            y = _gated_norm(attn_scr[rows, :], gain_a, pj_ref[rows, OFF_GA:OFF_GA + D_ATTN])
            mixed_ref[rows, D_CONV:D_MIX] = y.astype(BF16)

    per_block = BLOCK // HALO
    return pl.pallas_call(
        body, name="mix_fwd", grid=(N_BLOCKS,),
        in_specs=[
            pl.BlockSpec((BLOCK, D_PROJ), lambda n: (n, 0)),
            pl.BlockSpec((BLOCK, 2 * D_KV), lambda n: (jnp.maximum(n - 1, 0), OFF_K // (2 * D_KV))),
            pl.BlockSpec((HALO, D_CONV), lambda n: (jnp.maximum(n * per_block - 1, 0), OFF_CC // D_CONV)),
            pl.BlockSpec((HALO, D_CONV), lambda n: (jnp.maximum(n * per_block - 1, 0), OFF_CU // D_CONV)),
            pl.BlockSpec((8, D_CONV), lambda n: (0, 0)),
            pl.BlockSpec(memory_space=pltpu.SMEM),
            pl.BlockSpec((1, D_CONV), lambda n: (0, 0)),
            pl.BlockSpec((1, D_ATTN), lambda n: (0, 0)),
        ],
        out_specs=(pl.BlockSpec((BLOCK, D_MIX), lambda n: (n, 0)), pl.BlockSpec((BLOCK, D_ATTN), lambda n: (n, 0)),
                   pl.BlockSpec((1, 4, STACK, 2 * BLOCK), lambda n: (n, 0, 0, 0)),
                   pl.BlockSpec((1, 4, STACK, 128), lambda n: (n, 0, 0, 0))),
        out_shape=(jax.ShapeDtypeStruct((SEQ, D_MIX), BF16), jax.ShapeDtypeStruct((SEQ, D_ATTN), F32),
                   jax.ShapeDtypeStruct((N_BLOCKS, 4, STACK, 2 * BLOCK), BF16),
                   jax.ShapeDtypeStruct((N_BLOCKS, 4, STACK, 128), F32)),
        scratch_shapes=[pltpu.VMEM((4, STACK, 2 * BLOCK), F32)],
        compiler_params=_params(dimension_semantics=("arbitrary",)),
    )(proj, proj, proj, proj, conv_full, sinks, norm_conv, norm_attn)


def _out_proj_loss(mixed, x, target, w_out_full, norm_final):
    tm = 256

    def body(mx_ref, x_ref, t_ref, w_ref, g_ref, dx2_ref, dx2b_ref, dmix_ref, gnf_ref, loss_ref):
        i = pl.program_id(0)
        w = w_ref[...]
        x2 = x_ref[...] + jnp.dot(mx_ref[...], w, preferred_element_type=F32)
        r = lax.rsqrt(jnp.mean(x2 * x2, axis=-1, keepdims=True) + RMS_EPS)
        xn = x2 * r
        g = g_ref[...]
        err = xn * g - t_ref[...]
        part = 0.5 * jnp.sum(jnp.mean(err * err, axis=-1, keepdims=True), axis=0, keepdims=True)
        dy = err * (1.0 / D_MODEL)
        gnf = jnp.sum(dy * xn, axis=0, keepdims=True)
        u = dy * g
        dx2 = r * (u - xn * jnp.mean(u * xn, axis=-1, keepdims=True))
        dx2_ref[...] = dx2
        dx2b = dx2.astype(BF16)
        dx2b_ref[...] = dx2b
        dmix_ref[...] = lax.dot_general(dx2b, w, _NT, preferred_element_type=F32)

        @pl.when(i == 0)
        def _():
            gnf_ref[...] = jnp.zeros_like(gnf_ref)
            loss_ref[...] = jnp.zeros_like(loss_ref)

        gnf_ref[...] += gnf
        loss_ref[...] += jnp.broadcast_to(part, loss_ref.shape)

    return pl.pallas_call(
        body, name="out_proj_loss", grid=(SEQ // tm,),
        in_specs=[pl.BlockSpec((tm, D_MIX), lambda i: (i, 0)), pl.BlockSpec((tm, D_MODEL), lambda i: (i, 0)),
                  pl.BlockSpec((tm, D_MODEL), lambda i: (i, 0)), pl.BlockSpec(memory_space=pltpu.VMEM),
                  pl.BlockSpec((1, D_MODEL), lambda i: (0, 0))],
        out_specs=(pl.BlockSpec((tm, D_MODEL), lambda i: (i, 0)), pl.BlockSpec((tm, D_MODEL), lambda i: (i, 0)),
                   pl.BlockSpec((tm, D_MIX), lambda i: (i, 0)),
                   pl.BlockSpec((1, D_MODEL), lambda i: (0, 0)), pl.BlockSpec((8, 128), lambda i: (0, 0))),
        out_shape=(jax.ShapeDtypeStruct((SEQ, D_MODEL), F32), jax.ShapeDtypeStruct((SEQ, D_MODEL), BF16),
                   jax.ShapeDtypeStruct((SEQ, D_MIX), F32),
                   jax.ShapeDtypeStruct((1, D_MODEL), F32), jax.ShapeDtypeStruct((8, 128), F32)),
        compiler_params=_params(dimension_semantics=("arbitrary",)),
    )(mixed, x, target, w_out_full, norm_final)


def _gated_norm_bwd(a, gain, t, dy):
    r = lax.rsqrt(jnp.mean(a * a, axis=-1, keepdims=True) + RMS_EPS)
    an = a * r
    sg = _sigmoid(t)
    dn = dy * (t * sg)
    dt = dy * (an * gain) * (sg * (1.0 + t * (1.0 - sg)))
    u = dn * gain
    da = r * (u - an * jnp.mean(u * an, axis=-1, keepdims=True))
    return da, dt, dn * an


def _mix_bwd(proj, dmixed, attn, probs, shares, conv_full, norm_conv, norm_attn):
    def body(pj_ref, kvp_ref, cch_ref, cuh_ref, dmx_ref, attn_ref, p_ref, ps_ref, cw_ref, gc_ref, ga_ref,
             dpj_ref, gslab_ref, dattn_scr, nxt_scr, dkv_scr, acc_scr):
        step = pl.program_id(0)
        n = N_BLOCKS - 1 - step
        pj = pj_ref

        @pl.when(step == 0)
        def _():
            gslab_ref[...] = jnp.zeros_like(gslab_ref)
            nxt_scr[...] = jnp.zeros_like(nxt_scr)
            dkv_scr[...] = jnp.zeros_like(dkv_scr)
            acc_scr[...] = jnp.zeros_like(acc_scr)

        zhalo = _conv_halo(cch_ref, cuh_ref, n)
        cw = (cw_ref[0:1, :], cw_ref[1:2, :], cw_ref[2:3, :])
        gain_c = gc_ref[...]
        row = lax.broadcasted_iota(jnp.int32, (CHUNK, D_CONV), 0)

        dco_after = nxt_scr[...]
        for r in reversed(range(N_CHUNKS)):
            rows = _chunk_rows(r)
            cc, cu, z, z1, z2, co = _conv_chunk(pj_ref, zhalo, cw, r)
            cb = pj_ref[rows, OFF_CB:OFF_CB + D_CONV]
            da, dgate, gterm = _gated_norm_bwd(cb * co, gain_c, pj_ref[rows, OFF_GC:OFF_GC + D_CONV],
                                               dmx_ref[rows, 0:D_CONV])
            dpj_ref[rows, OFF_GC:OFF_GC + D_CONV] = dgate.astype(BF16)
            dpj_ref[rows, OFF_CB:OFF_CB + D_CONV] = (da * co).astype(BF16)
            dco = da * cb
            dco1 = jnp.where(row >= CHUNK - 1, pltpu.roll(dco_after, CHUNK - 1, 0), pltpu.roll(dco, CHUNK - 1, 0))
            dco2 = jnp.where(row >= CHUNK - 2, pltpu.roll(dco_after, CHUNK - 2, 0), pltpu.roll(dco, CHUNK - 2, 0))
            dz = cw[2] * dco + cw[1] * dco1 + cw[0] * dco2
            dpj_ref[rows, OFF_CC:OFF_CC + D_CONV] = (dz * cu).astype(BF16)
            dpj_ref[rows, OFF_CU:OFF_CU + D_CONV] = (dz * cc).astype(BF16)
            acc_scr[ACC_NORM_CONV] += gterm
            acc_scr[ACC_CONV0] += dco * z2
            acc_scr[ACC_CONV0 + 1] += dco * z1
            acc_scr[ACC_CONV0 + 2] += dco * z
            dco_after = dco
        nxt_scr[...] = dco_after

        ks, vs = _kv_bands(pj, kvp_ref)
        gain_a = ga_ref[...]

        for r in range(N_CHUNKS):
            rows = _chunk_rows(r)
            da, dgate, gterm = _gated_norm_bwd(attn_ref[rows, :], gain_a, pj_ref[rows, OFF_GA:OFF_GA + D_ATTN],
                                               dmx_ref[rows, D_CONV:D_MIX])
            dpj_ref[rows, OFF_GA:OFF_GA + D_ATTN] = dgate.astype(BF16)
            dattn_scr[rows, :] = da
            acc_scr[ACC_NORM_ATTN] += gterm

        in_lo = lax.broadcasted_iota(jnp.int32, (128, 128), 0) < HEAD_DIM
        half_ones = (jnp.where(in_lo, 1.0, 0.0).astype(BF16), jnp.where(in_lo, 0.0, 1.0).astype(BF16))
        lane_s = lax.broadcasted_iota(jnp.int32, (1, D_MODEL), 1)
        gsink = jnp.zeros((1, D_MODEL), F32)
        dk_t, dv_t = [], []
        for j in range(2):
            q_stack = _q_stack(pj, j)
            do_f = jnp.concatenate([dattn_scr[:, _pair_cols(j, i, 0)] for i in range(PAIRS_PER_KV)], axis=0)
            o_f = jnp.concatenate([attn_ref[:, _pair_cols(j, i, 0)] for i in range(PAIRS_PER_KV)], axis=0)
            prod = (do_f * o_f).astype(BF16)
            deltas = [jnp.dot(prod, half_ones[e], preferred_element_type=F32) for e in range(2)]
            do_b = do_f.astype(BF16)
            q_t, do_t = q_stack.T, do_b.T
            dq, dk_j, dv_j = None, None, None
            for e in range(2):
                p = p_ref[0, 2 * j + e]
                dp = lax.dot_general(do_b, vs[j][e], _NT, preferred_element_type=F32)
                ds = []
                for i in range(PAIRS_PER_KV):
                    rows = slice(BLOCK * i, BLOCK * (i + 1))
                    delta = deltas[e][rows, :]
                    ds.append((p[rows, :].astype(F32) * (dp[rows, :] - jnp.concatenate([delta, delta], axis=1))).astype(BF16))
                    gs_h = -jnp.sum(ps_ref[0, 2 * j + e, rows, 0:1] * delta[:, 0:1], axis=0, keepdims=True)
                    gsink = gsink + jnp.where(lane_s == _head(j, i, e), gs_h, 0.0)
                ds = jnp.concatenate(ds, axis=0)
                t = jnp.dot(ds, ks[j][e], preferred_element_type=F32)
                dq = t if dq is None else dq + t
                half = slice(HEAD_DIM * e, HEAD_DIM * (e + 1))
                a = jnp.dot(q_t[half, :], ds, preferred_element_type=F32)
                b = jnp.dot(do_t[half, :], p, preferred_element_type=F32)
                dk_j = a if dk_j is None else dk_j + a
                dv_j = b if dv_j is None else dv_j + b
            for i in range(PAIRS_PER_KV):
                dpj_ref[:, _pair_cols(j, i, OFF_Q)] = (dq[BLOCK * i:BLOCK * (i + 1), :] * SCALE).astype(BF16)
            dk_t.append(dk_j)
            dv_t.append(dv_j)
        dk = jnp.concatenate(dk_t, axis=0).T
        dv = jnp.concatenate(dv_t, axis=0).T
        dpj_ref[:, OFF_K:OFF_K + D_KV] = (dk[BLOCK:, :] + dkv_scr[:, 0:D_KV]).astype(BF16)
        dpj_ref[:, OFF_V:OFF_V + D_KV] = (dv[BLOCK:, :] + dkv_scr[:, D_KV:2 * D_KV]).astype(BF16)
        dkv_scr[:, 0:D_KV] = dk[:BLOCK, :]
        dkv_scr[:, D_KV:2 * D_KV] = dv[:BLOCK, :]
        gslab_ref[ROW_SINKS:ROW_SINKS + 1, :] += gsink

        @pl.when(step == N_BLOCKS - 1)
        def _():
            for k, slab_row in ((ACC_NORM_CONV, ROW_NORM_CONV), (ACC_NORM_ATTN, ROW_NORM_ATTN), (ACC_CONV0, ROW_CONV0),
                                (ACC_CONV0 + 1, ROW_CONV0 + 1), (ACC_CONV0 + 2, ROW_CONV0 + 2)):
                gslab_ref[slab_row:slab_row + 1, :] = jnp.sum(acc_scr[k], axis=0, keepdims=True)

    per_block = BLOCK // HALO
    last = N_BLOCKS - 1
    return pl.pallas_call(
        body, name="mix_bwd", grid=(N_BLOCKS,),
        in_specs=[
            pl.BlockSpec((BLOCK, D_PROJ), lambda s: (last - s, 0)),
            pl.BlockSpec((BLOCK, 2 * D_KV), lambda s: (jnp.maximum(last - s - 1, 0), OFF_K // (2 * D_KV))),
            pl.BlockSpec((HALO, D_CONV), lambda s: (jnp.maximum((last - s) * per_block - 1, 0), OFF_CC // D_CONV)),
            pl.BlockSpec((HALO, D_CONV), lambda s: (jnp.maximum((last - s) * per_block - 1, 0), OFF_CU // D_CONV)),
            pl.BlockSpec((BLOCK, D_MIX), lambda s: (last - s, 0)),
            pl.BlockSpec((BLOCK, D_ATTN), lambda s: (last - s, 0)),
            pl.BlockSpec((1, 4, STACK, 2 * BLOCK), lambda s: (last - s, 0, 0, 0)),
            pl.BlockSpec((1, 4, STACK, 128), lambda s: (last - s, 0, 0, 0)),
            pl.BlockSpec((8, D_CONV), lambda s: (0, 0)),
            pl.BlockSpec((1, D_CONV), lambda s: (0, 0)),
            pl.BlockSpec((1, D_ATTN), lambda s: (0, 0)),
        ],
        out_specs=(pl.BlockSpec((BLOCK, D_PROJ), lambda s: (last - s, 0)),
                   pl.BlockSpec((8, D_MODEL), lambda s: (0, 0))),
        out_shape=(jax.ShapeDtypeStruct((SEQ, D_PROJ), BF16), jax.ShapeDtypeStruct((8, D_MODEL), F32)),
        scratch_shapes=[pltpu.VMEM((BLOCK, D_ATTN), F32), pltpu.VMEM((CHUNK, D_CONV), F32),
                        pltpu.VMEM((BLOCK, 2 * D_KV), F32), pltpu.VMEM((N_ACC, CHUNK, D_MODEL), F32)],
        compiler_params=_params(dimension_semantics=("arbitrary",)),
    )(proj, proj, proj, proj, dmixed, attn, probs, shares, conv_full, norm_conv, norm_attn)


def _in_bwd_rs(dproj, w_full, x, dx2, norm_in, dw_in_chip, gslab, gnf, loss_part):
    tm = 256
    steps = SEQ // tm

    def body(dp_ref, w_hbm, x_ref, dx2_ref, g_ref, dwi_ref, gs_ref, gnf_ref, lp_ref, gx_ref, gwin_ref, gsum_ref,
             gni_scr, own, d2d, ici, myslab, slabs, w_ref, send_sems, recv_sems, local_sems):
        i = pl.program_id(0)
        rs_start, rs_finish = _ici_sum(dwi_ref, own, d2d, ici, send_sems, recv_sems, local_sems)
        slab_start, slab_finish = _slab_sum(myslab, slabs, send_sems, recv_sems, N_ICI_SUM_SEMS)

        @pl.when(i == 0)
        def _():
            gni_scr[...] = jnp.zeros_like(gni_scr)
            rs_start()
            w_load = pltpu.make_async_copy(w_hbm, w_ref, local_sems.at[1])
            w_load.start()
            w_load.wait()

        dh = jnp.dot(dp_ref[...], w_ref[...], preferred_element_type=F32)
        xv = x_ref[...]
        r = lax.rsqrt(jnp.mean(xv * xv, axis=-1, keepdims=True) + RMS_EPS)
        xn = xv * r
        u = dh * g_ref[...]
        gx_ref[...] = dx2_ref[...] + r * (u - xn * jnp.mean(u * xn, axis=-1, keepdims=True))
        gni_scr[...] += jnp.sum(dh * xn, axis=0, keepdims=True)

        @pl.when(i == steps - 1)
        def _():
            row = lax.broadcasted_iota(jnp.int32, (8, D_MODEL), 0)
            lane = lax.broadcasted_iota(jnp.int32, (8, D_MODEL), 1)
            slab = jnp.where(row == ROW_NORM_IN, gni_scr[...], jnp.where(row == ROW_NORM_FINAL, gnf_ref[...], gs_ref[...]))
            myslab[...] = jnp.where((row == ROW_SINKS) & (lane == LOSS_LANE), lp_ref[0:1, 0:1], slab)
            slab_start()
            gwin_ref[...] = rs_finish()
            gsum_ref[...] = slab_finish()

    const = lambda i: (0, 0)
    return pl.pallas_call(
        body, name="in_bwd", grid=(steps,),
        in_specs=[pl.BlockSpec((tm, D_PROJ), lambda i: (i, 0)), pl.BlockSpec(memory_space=pl.ANY),
                  pl.BlockSpec((tm, D_MODEL), lambda i: (i, 0)), pl.BlockSpec((tm, D_MODEL), lambda i: (i, 0)),
                  pl.BlockSpec((1, D_MODEL), const), pl.BlockSpec(memory_space=pl.ANY),
                  pl.BlockSpec((8, D_MODEL), const), pl.BlockSpec((1, D_MODEL), const), pl.BlockSpec((8, 128), const)],
        out_specs=(pl.BlockSpec((tm, D_MODEL), lambda i: (i, 0)), pl.BlockSpec((SHARD_IN, D_MODEL), const),
                   pl.BlockSpec((8, D_MODEL), const)),
        out_shape=(jax.ShapeDtypeStruct((SEQ, D_MODEL), F32), jax.ShapeDtypeStruct((SHARD_IN, D_MODEL), F32),
                   jax.ShapeDtypeStruct((8, D_MODEL), F32)),
        scratch_shapes=[pltpu.VMEM((1, D_MODEL), F32), pltpu.VMEM((SHARD_IN, D_MODEL), BF16),
                        pltpu.VMEM((SHARD_IN, D_MODEL), BF16), pltpu.VMEM((2, SHARD_IN, D_MODEL), BF16),
                        pltpu.VMEM((8, D_MODEL), F32), pltpu.VMEM((N_DEV, 8, D_MODEL), F32),
                        pltpu.VMEM((D_PROJ, D_MODEL), BF16),
                        pltpu.SemaphoreType.DMA((N_ICI_SUM_SEMS + 7,)), pltpu.SemaphoreType.DMA((N_ICI_SUM_SEMS + 7,)),
                        pltpu.SemaphoreType.DMA((2,))],
        compiler_params=_params(dimension_semantics=("arbitrary",)),
    )(dproj, w_full, x, dx2, norm_in, dw_in_chip, gslab, gnf, loss_part)


def _dw_rs(mixed, dx2b, dproj, h, table):
    tn_out, tn = 2 * SHARD_OUT, IN_PROJ_TILE
    out_steps, in_steps = D_MIX // tn_out, D_PROJ // tn
    steps = out_steps + in_steps
    out_order = (DG, NX, NY, OWN)

    def out_tile(i):
        chip = 2 * lax.axis_index("x") + lax.axis_index("y")
        return jnp.bitwise_xor(chip, (out_steps - 1) - jnp.minimum(i, out_steps - 1))

    def in_tile(table_ref, i):
        return _dw_entry(table_ref, jnp.maximum(i - out_steps, 0))

    def body(table_ref, mx_ref, dxb_ref, a_ref, h_hbm, chip_ref, gwo_ref, dwo, dwt, d2d_in, via, own, d2d, ici, b_ref,
             send_sems, recv_sems, local_sems):
        i = pl.program_id(0)
        h_load = pltpu.make_async_copy(h_hbm, b_ref, local_sems.at[8])

        @pl.when(i == 0)
        def _():
            h_load.start()

        @pl.when(i == out_steps)
        def _():
            h_load.wait()

        rs_start, rs_forward, rs_finish = _shard_sum(dwo, own, d2d, ici, send_sems, recv_sems, local_sems)
        before_tile, after_tiles, chip_finish = _chip_sum(
            dwt, d2d_in, via, chip_ref, lambda k: _dw_entry(table_ref, in_steps + k), send_sems, recv_sems, local_sems,
            N_SHARD_SUM_SEMS, 4)

        for j, k in enumerate(out_order):
            @pl.when(i == j + 1)
            def _():
                rs_start(k)

            if k != OWN:
                @pl.when(i == j + 2)
                def _():
                    rs_forward(k)

        @pl.when(i < out_steps)
        def _():
            tile = lax.dot_general(mx_ref[...], dxb_ref[...], _TN, preferred_element_type=F32).astype(BF16)
            for core in range(2):
                dwo[2 * out_tile(i) + core] = tile[SHARD_OUT * core:SHARD_OUT * (core + 1), :]

        @pl.when(i >= out_steps)
        def _():
            before_tile(i - out_steps)
            tile = lax.dot_general(a_ref[...], b_ref[...], _TN, preferred_element_type=F32).astype(BF16)
            dwt[pl.ds(pl.multiple_of(in_tile(table_ref, i) * tn, tn), tn), :] = tile

        @pl.when(i == steps - 2)
        def _():
            gwo_ref[...] = rs_finish()

        @pl.when(i == steps - 1)
        def _():
            after_tiles()
            chip_finish()

    vmem = pl.BlockSpec(memory_space=pltpu.VMEM)
    grid_spec = pltpu.PrefetchScalarGridSpec(
        num_scalar_prefetch=1, grid=(steps,),
        in_specs=[pl.BlockSpec((SEQ, tn_out), lambda i, table_ref: (0, out_tile(i))), vmem,
                  pl.BlockSpec((SEQ, tn), lambda i, table_ref: (0, in_tile(table_ref, i))),
                  pl.BlockSpec(memory_space=pl.ANY)],
        out_specs=(pl.BlockSpec(memory_space=pl.ANY), pl.BlockSpec((SHARD_OUT, D_MODEL), lambda i, table_ref: (0, 0))),
        scratch_shapes=[pltpu.VMEM((N_DEV, SHARD_OUT, D_MODEL), BF16),
                        pltpu.VMEM((D_PROJ, D_MODEL), BF16), pltpu.VMEM((3, SHARD_IN, D_MODEL), BF16),
                        pltpu.VMEM((2, HALF_IN, D_MODEL), BF16),
                        *_shard_sum_scratch(SHARD_OUT), pltpu.VMEM((SEQ, D_MODEL), BF16),
                        pltpu.SemaphoreType.DMA((N_SHARD_SUM_SEMS + N_CHIP_SUM_SEMS,)),
                        pltpu.SemaphoreType.DMA((N_SHARD_SUM_SEMS + N_CHIP_SUM_SEMS,)),
                        pltpu.SemaphoreType.DMA((9,))])
    return pl.pallas_call(
        body, name="dw", grid_spec=grid_spec,
        out_shape=(jax.ShapeDtypeStruct((4, SHARD_IN, D_MODEL), BF16), jax.ShapeDtypeStruct((SHARD_OUT, D_MODEL), F32)),
        compiler_params=_params(dimension_semantics=("arbitrary",)),
    )(table, mixed, dx2b, dproj, h)


def _adam_all(big_in, big_out, gsum, small, grad_x):
    n_chunks = 4
    n_big = 8

    def body(*refs):
        ins, outs = refs[:n_big + 1 + 18 + 1], refs[n_big + 1 + 18 + 1:n_big + 1 + 18 + 1 + 34]
        in_bufs, out_bufs, gx_buf = refs[-n_big - 6 - 4:-6 - 4], refs[-6 - 4:-4], refs[-4]
        in_sems, out_sems, gx_sems = refs[-3:]

        def gx_rows(j):
            return pl.ds(j * (SEQ // n_chunks), SEQ // n_chunks)

        def gx_load(j):
            return pltpu.make_async_copy(ins[27].at[gx_rows(j), :], gx_buf.at[gx_rows(j), :], gx_sems.at[j])

        def gx_store(j):
            return pltpu.make_async_copy(gx_buf.at[gx_rows(j), :], outs[33].at[gx_rows(j), :], gx_sems.at[n_chunks + j])

        def rows(a, j):
            tr = ins[a].shape[0] // n_chunks
            return pl.ds(j * tr, tr)

        def load(a, j):
            return pltpu.make_async_copy(ins[a].at[rows(a, j), :], in_bufs[a].at[rows(a, j), :], in_sems.at[a * n_chunks + j])

        def store(a, j):
            b, kind = divmod(a, 4)
            src = in_bufs[4 * b + 1] if kind == 0 else out_bufs[3 * b + kind - 1]
            return pltpu.make_async_copy(src.at[rows(a, j), :], outs[a].at[rows(a, j), :], out_sems.at[a * n_chunks + j])

        def request(j):
            for a in range(n_big):
                load(a, j).start()
            gx_load(j).start()

        ahead = 2
        for j in range(ahead):
            request(j)

        def small_weights():
            gsum = ins[8][...]
            idx = _slot(lax.axis_index("x"), lax.axis_index("y"), lax.axis_index("c"))
            cg = jnp.zeros((3, SHARD_CONV), F32)
            for d in range(N_DEV):
                cg = jnp.where(idx == d, gsum[ROW_CONV0:ROW_CONV0 + 3, d * SHARD_CONV:(d + 1) * SHARD_CONV], cg)
            grads = (gsum[ROW_NORM_IN:ROW_NORM_IN + 1], gsum[ROW_SINKS:ROW_SINKS + 1, 0:N_Q_HEADS],
                     gsum[ROW_NORM_CONV:ROW_NORM_CONV + 1], gsum[ROW_NORM_ATTN:ROW_NORM_ATTN + 1],
                     gsum[ROW_NORM_FINAL:ROW_NORM_FINAL + 1], cg)
            for s, g in enumerate(grads):
                at = (slice(None), 0, slice(None)) if s == 5 else (slice(None), slice(None))
                w_ref, m_ref, v_ref = ins[9 + 3 * s:12 + 3 * s]
                delta, mn, vn = _adamw(w_ref[at], g, m_ref[at], v_ref[at])
                for ref, val in zip(outs[8 + 4 * s:12 + 4 * s], (g, delta, mn, vn)):
                    ref[at] = val
            outs[32][...] = gsum[ROW_SINKS:ROW_SINKS + 1, LOSS_LANE:LOSS_LANE + 1]

        small_weights()
        for j in range(n_chunks):
            for b in range(2):
                for a in range(4 * b, 4 * b + 4):
                    load(a, j).wait()
                if b == 0 and j + ahead < n_chunks:
                    request(j + ahead)
                w_buf, g_buf, m_buf, v_buf = in_bufs[4 * b:4 * b + 4]
                r = rows(4 * b, j)
                results = _adamw(w_buf[r, :], g_buf[r, :], m_buf[r, :], v_buf[r, :])
                for buf, val in zip(out_bufs[3 * b:3 * b + 3], results):
                    buf[r, :] = val
                for a in range(4 * b, 4 * b + 4):
                    store(a, j).start()
            gx_load(j).wait()
            gx_store(j).start()
        for j in range(n_chunks):
            for a in range(n_big):
                store(a, j).wait()
            gx_store(j).wait()

    vmem, hbm = pl.BlockSpec(memory_space=pltpu.VMEM), pl.BlockSpec(memory_space=pl.ANY)
    small_shapes = [a.shape for a in small[::3]]
    big_shapes = [(SHARD_IN, D_MODEL)] * 4 + [(SHARD_OUT, D_MODEL)] * 4
    out_shape = ([jax.ShapeDtypeStruct(s, F32) for s in big_shapes]
                 + [jax.ShapeDtypeStruct(s, F32) for s in small_shapes for _ in range(4)]
                 + [jax.ShapeDtypeStruct((1, 1), F32), jax.ShapeDtypeStruct((SEQ, D_MODEL), F32)])
    outs = pl.pallas_call(
        body, name="adam", in_specs=[hbm] * n_big + [vmem] * (1 + len(small)) + [hbm],
        out_specs=tuple([hbm] * n_big + [vmem] * (4 * len(small_shapes) + 1) + [hbm]), out_shape=tuple(out_shape),
        scratch_shapes=[pltpu.VMEM(s, F32) for s in big_shapes]
                       + [pltpu.VMEM(s, F32) for s in [(SHARD_IN, D_MODEL)] * 3 + [(SHARD_OUT, D_MODEL)] * 3]
                       + [pltpu.VMEM((SEQ, D_MODEL), F32),
                          pltpu.SemaphoreType.DMA((n_big * n_chunks,)), pltpu.SemaphoreType.DMA((n_big * n_chunks,)),
                          pltpu.SemaphoreType.DMA((2 * n_chunks,))],
        compiler_params=_params(),
    )(*big_in, *big_out, gsum, *small, grad_x)
    return outs[0:4], outs[4:8], [outs[8 + 4 * s:12 + 4 * s] for s in range(6)], outs[32], outs[33]


def _rows_first(a):
    return jnp.transpose(a, (1, 0, 2))


def kernel(x, norm_in, w_in, conv_w, attn_sinks, norm_conv_out, norm_attn_out, w_out, norm_final, loss_target, m_norm_in, m_w_in, m_conv_w, m_attn_sinks, m_norm_conv_out, m_norm_attn_out, m_w_out, m_norm_final, v_norm_in, v_w_in, v_conv_w, v_attn_sinks, v_norm_conv_out, v_norm_attn_out, v_w_out, v_norm_final):
    x2d = x.reshape(SEQ, D_MODEL)
    target = loss_target.reshape(SEQ, D_MODEL)
    nf = norm_final.reshape(1, D_MODEL)

    w_in_t, m_w_in_t, v_w_in_t = w_in[0].T, m_w_in[0].T, v_w_in[0].T
    tiles = jnp.asarray(TILE_ORDER, jnp.int32).reshape(-1)
    w_in_full, h, proj, g_out, conv_full = _gather_in_proj(x2d, norm_in, w_in_t, w_out[0], _rows_first(conv_w), tiles)
    sinks = attn_sinks.reshape(N_Q_HEADS)

    mixed, attn, probs, shares = _mix_fwd(proj, conv_full, sinks, norm_conv_out, norm_attn_out)
    dx2, dx2b, dmixed, gnf, loss_part = _out_proj_loss(mixed, x2d, target, g_out.reshape(D_MIX, D_MODEL), nf)
    dproj, gslab = _mix_bwd(proj, dmixed, attn, probs, shares, conv_full, norm_conv_out, norm_attn_out)
    dw_in_chip, g_w_out = _dw_rs(mixed, dx2b, dproj, h, jnp.asarray(DW_TABLE, jnp.int32).reshape(-1))
    grad_x, g_w_in, gsum = _in_bwd_rs(dproj, w_in_full, x2d, dx2, norm_in, dw_in_chip, gslab, gnf, loss_part)

    small = (norm_in, m_norm_in, v_norm_in, attn_sinks, m_attn_sinks, v_attn_sinks,
             norm_conv_out, m_norm_conv_out, v_norm_conv_out, norm_attn_out, m_norm_attn_out, v_norm_attn_out,
             nf, m_norm_final.reshape(1, D_MODEL), v_norm_final.reshape(1, D_MODEL),
             _rows_first(conv_w), _rows_first(m_conv_w), _rows_first(v_conv_w))
    big_in, big_out, (s_ni, s_sk, s_nc, s_na, s_nf, s_cv), loss, grad_x = _adam_all(
        (w_in_t, g_w_in, m_w_in_t, v_w_in_t), (w_out[0], g_w_out, m_w_out[0], v_w_out[0]), gsum, small, grad_x)

    def leaves(k):
        return (s_ni[k], big_in[k].T[None], jnp.transpose(s_cv[k], (1, 0, 2)), s_sk[k], s_nc[k], s_na[k], big_out[k][None],
                s_nf[k].reshape(D_MODEL))

    return (loss.reshape(()), grad_x.reshape(1, SEQ, D_MODEL), *leaves(0), *leaves(1), *leaves(2), *leaves(3))
```

```python
import jax
import jax.numpy as jnp
from jax import lax
from jax.experimental import pallas as pl
from jax.experimental.pallas import tpu as pltpu

F32 = jnp.float32
BF16 = jnp.bfloat16
MESH = pl.DeviceIdType.MESH

N_DEV = 8
SEQ = 2048
D_MODEL = 1024
D_CONV = 1024
D_ATTN = 1024
D_KV = 128
HEAD_DIM = 64
N_Q_HEADS = 16
N_PAIRS = N_Q_HEADS // 2
PAIRS_PER_KV = N_PAIRS // 2
D_MIX = D_CONV + D_ATTN
D_PROJ = 6400
SHARD_IN = D_PROJ // N_DEV
SHARD_OUT = D_MIX // N_DEV
SHARD_CONV = D_CONV // N_DEV
OFF_CB, OFF_CC, OFF_CU, OFF_GC, OFF_Q, OFF_K, OFF_V, OFF_GA = 0, 1024, 2048, 3072, 4096, 5120, 5248, 5376
BLOCK = 128
N_BLOCKS = SEQ // BLOCK
HALO = 8
CHUNK = 16
N_CHUNKS = BLOCK // CHUNK
RMS_EPS = 1e-5
NEG = -1e30
SCALE = HEAD_DIM ** -0.5
SLOPES = tuple(2.0 ** (-8.0 * (h + 1) / N_Q_HEADS) for h in range(N_Q_HEADS))

ADAM_LR = 0.001
ADAM_B1 = 0.9
ADAM_B2 = 0.999
ADAM_EPS = 1e-08
ADAM_WD = 0.01
ADAM_STEP = 10

ROW_NORM_IN, ROW_NORM_CONV, ROW_NORM_ATTN, ROW_NORM_FINAL, ROW_CONV0, ROW_SINKS = 0, 1, 2, 3, 4, 7
LOSS_LANE = N_Q_HEADS
ACC_NORM_CONV, ACC_NORM_ATTN, ACC_CONV0, N_ACC = 0, 1, 2, 5

VMEM_LIMIT = 56 * 1024 * 1024

_NT = (((1,), (1,)), ((), ()))
_TN = (((0,), (0,)), ((), ()))


def _params(**kw):
    return pltpu.CompilerParams(vmem_limit_bytes=VMEM_LIMIT, **kw)


def _adamw(w, g, m, v):
    m = ADAM_B1 * m + (1.0 - ADAM_B1) * g
    v = ADAM_B2 * v + (1.0 - ADAM_B2) * (g * g)
    m_hat = m / (1.0 - ADAM_B1 ** ADAM_STEP)
    v_hat = v / (1.0 - ADAM_B2 ** ADAM_STEP)
    delta = -ADAM_LR * (m_hat / (jnp.sqrt(v_hat) + ADAM_EPS) + ADAM_WD * w)
    return delta, m, v


def _sigmoid(t):
    return 1.0 / (1.0 + jnp.exp(-t))


def _slot(px, py, pc):
    return 4 * px + 2 * py + pc


OWN, NX, NY, DG = range(4)
HALF_IN = SHARD_IN // 2
N_GATHER_KINDS = 13
W_OUT_KINDS = N_GATHER_KINDS + 7


IN_PROJ_TILE = 640
TILE_ORDER = ((0, 1, 2, 3, 4, 5, 6, 7, 8, 9), (3, 4, 0, 1, 2, 8, 9, 5, 6, 7),
              (5, 6, 0, 1, 7, 8, 9, 2, 3, 4), (8, 9, 3, 4, 5, 6, 7, 0, 1, 2))
TILES_OWN, TILES_NEIGHBOURS = 2, 7


def _tile(table_ref, p):
    chip = 2 * lax.axis_index("x") + lax.axis_index("y")
    return table_ref[chip * len(TILE_ORDER[0]) + p]


DW_TILE_ORDER = tuple(tuple(reversed(row)) for row in TILE_ORDER)


def _tiles_until_complete(chip, owner):
    lo, hi = owner * 2 * SHARD_IN, (owner + 1) * 2 * SHARD_IN
    touching = [t for t in range(len(TILE_ORDER[0])) if t * IN_PROJ_TILE < hi and (t + 1) * IN_PROJ_TILE > lo]
    return 1 + max(DW_TILE_ORDER[chip].index(t) for t in touching)


DW_TABLE = tuple(DW_TILE_ORDER[chip] + tuple(_tiles_until_complete(chip, chip ^ flip) for flip in (0, 2, 1, 3))
                 for chip in range(4))


def _dw_entry(table_ref, p):
    chip = 2 * lax.axis_index("x") + lax.axis_index("y")
    return table_ref[chip * len(DW_TABLE[0]) + p]


def _gather_in_proj(x, norm_in, w_in_sh, w_out_sh, conv_sh, tiles):
    tn = IN_PROJ_TILE
    steps = D_PROJ // tn
    tm = 256

    def body(tiles_ref, x_hbm, g_ref, win_ref, wout_ref, cv_ref, wt_ref, h_ref, proj_ref, gout_ref, conv_ref,
             gin_ref, gcv_ref, wob_ref, x_ref, send_sems, recv_sems, local_sems):
        p = pl.program_id(0)
        local_sem = local_sems.at[0]
        x, y, c = lax.axis_index("x"), lax.axis_index("y"), lax.axis_index("c")
        me, sibling = (x, y, c), (x, y, 1 - c)
        nx, ny, dg = (1 - x, y, c), (x, 1 - y, c), (1 - x, 1 - y, c)

        def other(dev):
            return (dev[0], dev[1], 1 - dev[2])

        def shard(dev):
            return gin_ref.at[pl.ds(pl.multiple_of(_slot(*dev) * SHARD_IN, 16), SHARD_IN), :]

        def half(dev, h):
            return gin_ref.at[pl.ds(pl.multiple_of(_slot(*dev) * SHARD_IN + h * HALF_IN, 16), HALF_IN), :]

        def rc(ref, k, to):
            return pltpu.make_async_remote_copy(src_ref=ref, dst_ref=ref, send_sem=send_sems.at[k],
                                                recv_sem=recv_sems.at[k], device_id=to, device_id_type=MESH)

        def cv(k, dev, to):
            s = _slot(*dev)
            return pltpu.make_async_remote_copy(src_ref=gcv_ref.at[s], dst_ref=gcv_ref.at[s],
                                                send_sem=send_sems.at[N_GATHER_KINDS + k],
                                                recv_sem=recv_sems.at[N_GATHER_KINDS + k], device_id=to, device_id_type=MESH)

        def own_copies():
            return [rc(shard(me), 0, sibling),
                    rc(half(me, 0), 1, nx), rc(half(me, 1), 2, nx),
                    rc(half(me, 1), 4, ny), rc(half(me, 0), 3, ny),
                    cv(0, me, sibling)] + [cv(1 + j, me, peer) for j, peer in enumerate((nx, ny, dg))]

        def pass_on(dev, h, k_in, k_ici, k_d2d, half=half, base=0):
            rc(half(dev, h), base + k_in, me).wait_recv()
            if k_ici is not None:
                rc(half(dev, h), base + k_ici, ny if dev is nx else nx).start()
            rc(half(dev, h), base + k_d2d, sibling).start()

        def out_half(dev, h):
            return gout_ref.at[_slot(*dev), pl.ds(h * (SHARD_OUT // 2), SHARD_OUT // 2), :]

        def own_out_copies():
            src = lambda h: wob_ref.at[pl.ds(h * (SHARD_OUT // 2), SHARD_OUT // 2), :]

            def send(ref, dst, k, to):
                return pltpu.make_async_remote_copy(src_ref=ref, dst_ref=dst, send_sem=send_sems.at[W_OUT_KINDS + k],
                                                    recv_sem=recv_sems.at[W_OUT_KINDS + k], device_id=to, device_id_type=MESH)

            return [send(wob_ref, gout_ref.at[_slot(*me)], 0, sibling),
                    send(src(0), out_half(me, 0), 1, nx), send(src(1), out_half(me, 1), 2, nx),
                    send(src(1), out_half(me, 1), 4, ny), send(src(0), out_half(me, 0), 3, ny)]

        def own_out_local():
            return pltpu.make_async_copy(wob_ref, gout_ref.at[_slot(*me)], local_sems.at[1])

        @pl.when(p == 0)
        def _():
            gin_ref[pl.ds(pl.multiple_of(_slot(*me) * SHARD_IN, 16), SHARD_IN), :] = win_ref[...].astype(BF16)
            gcv_ref[_slot(*me)] = jnp.zeros((8, SHARD_CONV), F32)
            gcv_ref[_slot(*me), 0:3, :] = cv_ref[:, 0, :]
            for cp in own_copies():
                cp.start()
            wob_ref[...] = wout_ref[...].astype(BF16)
            x_load = pltpu.make_async_copy(x_hbm, x_ref, local_sems.at[2])
            x_load.start()
            x_load.wait()
            for t in range(SEQ // tm):
                xv = x_ref[tm * t:tm * (t + 1), :]
                r = lax.rsqrt(jnp.mean(xv * xv, axis=-1, keepdims=True) + RMS_EPS)
                h_ref[tm * t:tm * (t + 1), :] = (xv * r * g_ref[...]).astype(BF16)
            rc(shard(sibling), 0, me).wait_recv()

        @pl.when(p == TILES_OWN)
        def _():
            for args in ((nx, 0, 1, 5, 7), (ny, 1, 4, 6, 10), (nx, 1, 2, None, 8), (ny, 0, 3, None, 9)):
                pass_on(*args)
            for j, peer in enumerate((nx, ny, dg)):
                cv(1 + j, peer, me).wait_recv()
                cv(4 + j, peer, sibling).start()
            for (dev, h), k in (((nx, 0), 7), ((nx, 1), 8), ((ny, 0), 9), ((ny, 1), 10)):
                rc(half(other(dev), h), k, me).wait_recv()
            own_out_local().start()
            for cp in own_out_copies():
                cp.start()

        @pl.when(p == TILES_NEIGHBOURS - 1)
        def _():
            pass_on(dg, 0, 5, None, 11)
            pass_on(dg, 1, 6, None, 12)

        @pl.when(p == TILES_NEIGHBOURS)
        def _():
            for (dev, h), k in (((dg, 0), 11), ((dg, 1), 12)):
                rc(half(other(dev), h), k, me).wait_recv()
            pltpu.make_async_copy(gin_ref, wt_ref, local_sem).start()

        @pl.when(p == steps - 2)
        def _():
            for args in ((nx, 0, 1, 5, 7), (ny, 1, 4, 6, 10), (nx, 1, 2, None, 8), (ny, 0, 3, None, 9)):
                pass_on(*args, half=out_half, base=W_OUT_KINDS)

        w = gin_ref[pl.ds(pl.multiple_of(_tile(tiles_ref, p) * tn, tn), tn), :]
        proj_ref[...] = lax.dot_general(h_ref[...], w, _NT, preferred_element_type=F32)

        @pl.when(p == steps - 1)
        def _():
            cv(0, sibling, me).wait_recv()
            for j, peer in enumerate((nx, ny, dg)):
                cv(4 + j, other(peer), me).wait_recv()
            for d in range(N_DEV):
                conv_ref[:, d * SHARD_CONV:(d + 1) * SHARD_CONV] = gcv_ref[d]
            relayed = [rc(half(nx, 0), 5, ny), rc(half(ny, 1), 6, nx)]
            relayed += [rc(half(dev, h), k, sibling) for (dev, h), k in
                        (((nx, 0), 7), ((nx, 1), 8), ((ny, 0), 9), ((ny, 1), 10), ((dg, 0), 11), ((dg, 1), 12))]
            relayed += [cv(4 + j, peer, sibling) for j, peer in enumerate((nx, ny, dg))]
            for cp in own_copies() + relayed:
                cp.wait_send()
            pltpu.make_async_copy(gin_ref, wt_ref, local_sem).wait()
            pass_on(dg, 0, 5, None, 11, half=out_half, base=W_OUT_KINDS)
            pass_on(dg, 1, 6, None, 12, half=out_half, base=W_OUT_KINDS)
            rc(gout_ref.at[_slot(*sibling)], W_OUT_KINDS, me).wait_recv()
            out_relayed = [rc(out_half(nx, 0), W_OUT_KINDS + 5, ny), rc(out_half(ny, 1), W_OUT_KINDS + 6, nx)]
            for (dev, h), k in (((nx, 0), 7), ((nx, 1), 8), ((ny, 0), 9), ((ny, 1), 10), ((dg, 0), 11), ((dg, 1), 12)):
                rc(out_half(other(dev), h), W_OUT_KINDS + k, me).wait_recv()
                out_relayed.append(rc(out_half(dev, h), W_OUT_KINDS + k, sibling))
            for cp in own_out_copies() + out_relayed:
                cp.wait_send()
            own_out_local().wait()

    vmem = pl.BlockSpec(memory_space=pltpu.VMEM)
    grid_spec = pltpu.PrefetchScalarGridSpec(
        num_scalar_prefetch=1, grid=(steps,),
        in_specs=[pl.BlockSpec(memory_space=pl.ANY), vmem, vmem, vmem, vmem],
        out_specs=(pl.BlockSpec(memory_space=pl.ANY), vmem,
                   pl.BlockSpec((SEQ, tn), lambda p, tiles_ref: (0, _tile(tiles_ref, p))),
                   pl.BlockSpec(memory_space=pl.ANY), vmem),
        scratch_shapes=[pltpu.VMEM((D_PROJ, D_MODEL), BF16), pltpu.VMEM((N_DEV, 8, SHARD_CONV), F32),
                        pltpu.VMEM((SHARD_OUT, D_MODEL), BF16), pltpu.VMEM((SEQ, D_MODEL), F32),
                        pltpu.SemaphoreType.DMA((W_OUT_KINDS + N_GATHER_KINDS,)),
                        pltpu.SemaphoreType.DMA((W_OUT_KINDS + N_GATHER_KINDS,)),
                        pltpu.SemaphoreType.DMA((3,))])
    return pl.pallas_call(
        body, name="gather_in_proj", grid_spec=grid_spec,
        out_shape=(jax.ShapeDtypeStruct((D_PROJ, D_MODEL), BF16), jax.ShapeDtypeStruct((SEQ, D_MODEL), BF16),
                   jax.ShapeDtypeStruct((SEQ, D_PROJ), F32), jax.ShapeDtypeStruct((N_DEV, SHARD_OUT, D_MODEL), BF16),
                   jax.ShapeDtypeStruct((8, D_CONV), F32)),
        compiler_params=_params(dimension_semantics=("arbitrary",)),
    )(tiles, x, norm_in, w_in_sh, w_out_sh, conv_sh)


def _shard_sum(src, own, d2d, ici, send_sems, recv_sems, local_sems, base=0):
    x, y, c = lax.axis_index("x"), lax.axis_index("y"), lax.axis_index("c")
    sibling = (x, y, 1 - c)
    chips = [(x, y), (1 - x, y), (x, 1 - y), (1 - x, 1 - y)]

    def rcopy(s, d, k, to):
        return pltpu.make_async_remote_copy(src_ref=s, dst_ref=d, send_sem=send_sems.at[base + k],
                                            recv_sem=recv_sems.at[base + k], device_id=to, device_id_type=MESH)

    def mine(k):
        return pltpu.make_async_copy(src.at[_slot(*chips[k], c)], own.at[k], local_sems.at[k])

    def to_sibling(k):
        return rcopy(src.at[_slot(*chips[k], 1 - c)], d2d.at[k], k, sibling)

    def to_chip(k):
        return rcopy(own.at[k], ici.at[k - 1], 3 + k, (*chips[k], c))

    def start(k):
        mine(k).start()
        to_sibling(k).start()

    def forward(k):
        mine(k).wait()
        to_sibling(k).wait_recv()
        own[k] = (own[k].astype(F32) + d2d[k].astype(F32)).astype(BF16)
        to_chip(k).start()

    def finish():
        mine(0).wait()
        to_sibling(0).wait_recv()
        acc = own[0].astype(F32) + d2d[0].astype(F32)
        for k in range(1, 4):
            to_chip(k).wait_recv()
            acc = acc + ici[k - 1].astype(F32)
        for k in range(4):
            to_sibling(k).wait_send()
        for k in range(1, 4):
            to_chip(k).wait_send()
        return acc

    return start, forward, finish


def _shard_sum_scratch(rows):
    return [pltpu.VMEM((4, rows, D_MODEL), BF16), pltpu.VMEM((4, rows, D_MODEL), BF16),
            pltpu.VMEM((3, rows, D_MODEL), BF16)]


N_SHARD_SUM_SEMS = 7


N_CHIP_SUM_SEMS = 5


def _chip_sum(dwt, d2d, via, out_hbm, tiles_until, send_sems, recv_sems, local_sems, base, local_base):
    x, y, c = lax.axis_index("x"), lax.axis_index("y"), lax.axis_index("c")
    sibling, nx, ny = (x, y, 1 - c), (1 - x, y, c), (x, 1 - y, c)
    chips = [(x, y), (1 - x, y), (x, 1 - y), (1 - x, 1 - y)]

    def shard(s):
        return dwt.at[pl.ds(pl.multiple_of(s * SHARD_IN, 16), SHARD_IN), :]

    def half(ref, h):
        return ref.at[pl.ds(h * HALF_IN, HALF_IN), :]

    def rc(s, d, k, to):
        return pltpu.make_async_remote_copy(src_ref=s, dst_ref=d, send_sem=send_sems.at[base + k],
                                            recv_sem=recv_sems.at[base + k], device_id=to, device_id_type=MESH)

    def to_sibling(k):
        return rc(shard(_slot(*chips[k], 1 - c)), d2d.at[k - 1], k - 1, sibling)

    for_dg = (lambda: rc(half(d2d.at[DG - 1], 0), via.at[0], 3, nx), lambda: rc(half(d2d.at[DG - 1], 1), via.at[1], 4, ny))

    def save(k):
        return pltpu.make_async_copy(d2d.at[k - 1], out_hbm.at[k], local_sems.at[local_base + k])

    own_saves = (lambda: pltpu.make_async_copy(shard(_slot(x, y, c)), out_hbm.at[OWN], local_sems.at[local_base]),
                 lambda: pltpu.make_async_copy(shard(_slot(x, y, 1 - c)), out_hbm.at[3], local_sems.at[local_base + 3]))

    def before_tile(n):
        for k in (NX, NY, DG):
            @pl.when(tiles_until(k) == n)
            def _():
                to_sibling(k).start()

            @pl.when(tiles_until(k) + 1 == n)
            def _():
                to_sibling(k).wait_recv()
                d2d[k - 1] = shard(_slot(*chips[k], c))[...] + d2d[k - 1]
                if k == DG:
                    for cp in for_dg:
                        cp().start()

    def after_tiles():
        for cp in own_saves:
            cp().start()

    def finish():
        for k, h in ((NY, 0), (NX, 1)):
            for_dg[h]().wait_recv()
            rows = pl.ds(h * HALF_IN, HALF_IN)
            d2d[k - 1, rows, :] = (d2d[k - 1, rows, :].astype(F32) + via[h].astype(F32)).astype(BF16)
            save(k).start()
        for cp in own_saves + (lambda: save(NX), lambda: save(NY)):
            cp().wait()
        for cp in (lambda: to_sibling(NX), lambda: to_sibling(NY), lambda: to_sibling(DG)) + for_dg:
            cp().wait_send()

    return before_tile, after_tiles, finish


N_ICI_SUM_SEMS = 3


def _ici_sum(src, own, d2d, ici, send_sems, recv_sems, local_sems, base=0):
    x, y, c = lax.axis_index("x"), lax.axis_index("y"), lax.axis_index("c")

    def rc(s, d, k, to):
        return pltpu.make_async_remote_copy(src_ref=s, dst_ref=d, send_sem=send_sems.at[base + k],
                                            recv_sem=recv_sems.at[base + k], device_id=to, device_id_type=MESH)

    copies = (lambda: rc(src.at[NX], ici.at[0], 0, (1 - x, y, c)), lambda: rc(src.at[NY], ici.at[1], 1, (x, 1 - y, c)),
              lambda: rc(src.at[3], d2d, 2, (x, y, 1 - c)))
    mine = lambda: pltpu.make_async_copy(src.at[OWN], own, local_sems.at[0])

    def start():
        for cp in copies + (mine,):
            cp().start()

    def finish():
        mine().wait()
        for cp in copies:
            cp().wait_recv()
        acc = own[...].astype(F32) + d2d[...].astype(F32) + ici[0].astype(F32) + ici[1].astype(F32)
        for cp in copies:
            cp().wait_send()
        return acc

    return start, finish


def _slab_sum(myslab, slabs, send_sems, recv_sems, base):
    x, y, c = lax.axis_index("x"), lax.axis_index("y"), lax.axis_index("c")
    me = _slot(x, y, c)
    peers = [(x, y, 1 - c), (1 - x, y, c), (x, 1 - y, c), (1 - x, 1 - y, c),
             (1 - x, y, 1 - c), (x, 1 - y, 1 - c), (1 - x, 1 - y, 1 - c)]

    def cp(k):
        return pltpu.make_async_remote_copy(src_ref=myslab, dst_ref=slabs.at[me], send_sem=send_sems.at[base + k],
                                            recv_sem=recv_sems.at[base + k], device_id=peers[k], device_id_type=MESH)

    def start():
        slabs[me] = myslab[...]
        for k in range(7):
            cp(k).start()

    def finish():
        for k in range(7):
            cp(k).wait_recv()
        total = slabs[0]
        for d in range(1, N_DEV):
            total = total + slabs[d]
        for k in range(7):
            cp(k).wait_send()
        return total

    return start, finish


def _chunk_rows(r):
    return slice(r * CHUNK, (r + 1) * CHUNK)


def _conv_halo(cch_ref, cuh_ref, n):
    zh = jnp.where(n > 0, cch_ref[...] * cuh_ref[...], 0.0)
    return jnp.concatenate([zh] * (CHUNK // HALO), axis=0)


def _conv_chunk(pj_ref, zhalo, cw, r):
    rows = _chunk_rows(r)
    cc = pj_ref[rows, OFF_CC:OFF_CC + D_CONV]
    cu = pj_ref[rows, OFF_CU:OFF_CU + D_CONV]
    z = cc * cu
    before = _chunk_rows(r - 1)
    zprev = pj_ref[before, OFF_CC:OFF_CC + D_CONV] * pj_ref[before, OFF_CU:OFF_CU + D_CONV] if r > 0 else zhalo
    row = lax.broadcasted_iota(jnp.int32, (CHUNK, D_CONV), 0)
    z1 = jnp.where(row < 1, pltpu.roll(zprev, 1, 0), pltpu.roll(z, 1, 0))
    z2 = jnp.where(row < 2, pltpu.roll(zprev, 2, 0), pltpu.roll(z, 2, 0))
    co = cw[0] * z2 + cw[1] * z1 + cw[2] * z
    return cc, cu, z, z1, z2, co


def _gated_norm(a, gain, t):
    r = lax.rsqrt(jnp.mean(a * a, axis=-1, keepdims=True) + RMS_EPS)
    return a * r * gain * (t * _sigmoid(t))


def _kv_bands(pj, kvp_ref):
    lane = lax.broadcasted_iota(jnp.int32, (2 * BLOCK, D_KV), 1)
    lo = lane < HEAD_DIM

    def bands(prev, cur):
        b = jnp.concatenate([prev, cur], axis=0)
        br = pltpu.roll(b, HEAD_DIM, 1)
        zero = jnp.zeros_like(b)
        return ((jnp.where(lo, b, zero).astype(BF16), jnp.where(lo, zero, br).astype(BF16)),
                (jnp.where(lo, br, zero).astype(BF16), jnp.where(lo, zero, b).astype(BF16)))

    ks = bands(kvp_ref[:, 0:D_KV], pj[:, OFF_K:OFF_K + D_KV])
    vs = bands(kvp_ref[:, D_KV:2 * D_KV], pj[:, OFF_V:OFF_V + D_KV])
    return ks, vs


STACK = PAIRS_PER_KV * BLOCK


def _head(j, i, e):
    return 2 * (PAIRS_PER_KV * j + i) + e


def _pair_cols(j, i, off):
    p = PAIRS_PER_KV * j + i
    return slice(off + 128 * p, off + 128 * (p + 1))


def _fill_attn_bias(bias_scr, first_block):
    qi = lax.broadcasted_iota(jnp.int32, (BLOCK, 2 * BLOCK), 0)
    kj = lax.broadcasted_iota(jnp.int32, (BLOCK, 2 * BLOCK), 1)
    dist = BLOCK + qi - kj
    valid = (dist >= 0) & (dist < BLOCK)
    if first_block:
        valid = valid & (kj >= BLOCK)
    distf = dist.astype(F32)
    for j in range(2):
        for e in range(2):
            for i in range(PAIRS_PER_KV):
                bias_scr[2 * j + e, BLOCK * i:BLOCK * (i + 1), :] = jnp.where(valid, -SLOPES[_head(j, i, e)] * distf, NEG)


def _q_stack(pj, j):
    return jnp.concatenate([(pj[:, _pair_cols(j, i, OFF_Q)] * SCALE).astype(BF16) for i in range(PAIRS_PER_KV)], axis=0)


def _attn_probs(q_stack, kband, bias_ref, sinks):
    s = lax.dot_general(q_stack, kband, _NT, preferred_element_type=F32)
    ones = jnp.ones((128, 128), BF16)
    probs, shares = [], []
    for i, sink in enumerate(sinks):
        rows = slice(BLOCK * i, BLOCK * (i + 1))
        t = s[rows, :] + bias_ref[rows, :]
        m = jnp.broadcast_to(jnp.max(t, axis=-1, keepdims=True), (BLOCK, 128))
        m = jnp.maximum(m, sink)
        p = [jnp.exp(t[:, :128] - m), jnp.exp(t[:, 128:] - m)]
        es = jnp.exp(sink - m)
        total = (jnp.dot(p[0].astype(BF16), ones, preferred_element_type=F32)
                 + jnp.dot(p[1].astype(BF16), ones, preferred_element_type=F32))
        inv = 1.0 / (total + es)
        probs.append(jnp.concatenate([p[0] * inv, p[1] * inv], axis=1))
        shares.append(es * inv)
    return jnp.concatenate(probs, axis=0), jnp.concatenate(shares, axis=0)


def _attn_group(pj, ks, vs, bias_scr, sink_ref, j):
    q_stack = _q_stack(pj, j)
    out, probs, shares = None, [], []
    for e in range(2):
        p, ps = _attn_probs(q_stack, ks[j][e], bias_scr.at[2 * j + e],
                            [sink_ref[_head(j, i, e)] for i in range(PAIRS_PER_KV)])
        p = p.astype(BF16)
        o = jnp.dot(p, vs[j][e], preferred_element_type=F32)
        out = o if out is None else out + o
        probs.append(p)
        shares.append(ps)
    return out, probs, shares


def _mix_fwd(proj, conv_full, sinks, norm_conv, norm_attn):
    def body(pj_ref, kvp_ref, cch_ref, cuh_ref, cw_ref, sink_ref, gc_ref, ga_ref,
             mixed_ref, attn_scr, p_ref, ps_ref, bias_scr):
        n = pl.program_id(0)
        pj = pj_ref

        @pl.when(n == 0)
        def _():
            _fill_attn_bias(bias_scr, first_block=True)

        @pl.when(n == 1)
        def _():
            _fill_attn_bias(bias_scr, first_block=False)

        zhalo = _conv_halo(cch_ref, cuh_ref, n)
        cw = (cw_ref[0:1, :], cw_ref[1:2, :], cw_ref[2:3, :])
        gain_c = gc_ref[...]

        for r in range(N_CHUNKS):
            rows = _chunk_rows(r)
            co = _conv_chunk(pj_ref, zhalo, cw, r)[-1]
            y = _gated_norm(pj_ref[rows, OFF_CB:OFF_CB + D_CONV] * co, gain_c, pj_ref[rows, OFF_GC:OFF_GC + D_CONV])
            mixed_ref[rows, 0:D_CONV] = y.astype(BF16)

        ks, vs = _kv_bands(pj, kvp_ref)
        for j in range(2):
            out, probs, shares = _attn_group(pj, ks, vs, bias_scr, sink_ref, j)
            for e in range(2):
                p_ref[0, 2 * j + e] = probs[e]
                ps_ref[0, 2 * j + e] = shares[e]
            for i in range(PAIRS_PER_KV):
                attn_scr[:, _pair_cols(j, i, 0)] = out[BLOCK * i:BLOCK * (i + 1), :]
        gain_a = ga_ref[...]

        for r in range(N_CHUNKS):
            rows = _chunk_rows(r)
            y = _gated_norm(attn_scr[rows, :], gain_a, pj_ref[rows, OFF_GA:OFF_GA + D_ATTN])
            mixed_ref[rows, D_CONV:D_MIX] = y.astype(BF16)

    per_block = BLOCK // HALO
    return pl.pallas_call(
        body, name="mix_fwd", grid=(N_BLOCKS,),
        in_specs=[
            pl.BlockSpec((BLOCK, D_PROJ), lambda n: (n, 0)),
            pl.BlockSpec((BLOCK, 2 * D_KV), lambda n: (jnp.maximum(n - 1, 0), OFF_K // (2 * D_KV))),
            pl.BlockSpec((HALO, D_CONV), lambda n: (jnp.maximum(n * per_block - 1, 0), OFF_CC // D_CONV)),
            pl.BlockSpec((HALO, D_CONV), lambda n: (jnp.maximum(n * per_block - 1, 0), OFF_CU // D_CONV)),
            pl.BlockSpec((8, D_CONV), lambda n: (0, 0)),
            pl.BlockSpec(memory_space=pltpu.SMEM),
            pl.BlockSpec((1, D_CONV), lambda n: (0, 0)),
            pl.BlockSpec((1, D_ATTN), lambda n: (0, 0)),
        ],
        out_specs=(pl.BlockSpec((BLOCK, D_MIX), lambda n: (n, 0)), pl.BlockSpec((BLOCK, D_ATTN), lambda n: (n, 0)),
                   pl.BlockSpec((1, 4, STACK, 2 * BLOCK), lambda n: (n, 0, 0, 0)),
                   pl.BlockSpec((1, 4, STACK, 128), lambda n: (n, 0, 0, 0))),
        out_shape=(jax.ShapeDtypeStruct((SEQ, D_MIX), BF16), jax.ShapeDtypeStruct((SEQ, D_ATTN), F32),
                   jax.ShapeDtypeStruct((N_BLOCKS, 4, STACK, 2 * BLOCK), BF16),
                   jax.ShapeDtypeStruct((N_BLOCKS, 4, STACK, 128), F32)),
        scratch_shapes=[pltpu.VMEM((4, STACK, 2 * BLOCK), F32)],
        compiler_params=_params(dimension_semantics=("arbitrary",)),
    )(proj, proj, proj, proj, conv_full, sinks, norm_conv, norm_attn)


def _out_proj_loss(mixed, x, target, w_out_full, norm_final):
    tm = 256

    def body(mx_ref, x_ref, t_ref, w_ref, g_ref, dx2_ref, dx2b_ref, dmix_ref, gnf_ref, loss_ref):
        i = pl.program_id(0)
        w = w_ref[...]
        x2 = x_ref[...] + jnp.dot(mx_ref[...], w, preferred_element_type=F32)
        r = lax.rsqrt(jnp.mean(x2 * x2, axis=-1, keepdims=True) + RMS_EPS)
        xn = x2 * r
        g = g_ref[...]
        err = xn * g - t_ref[...]
        part = 0.5 * jnp.sum(jnp.mean(err * err, axis=-1, keepdims=True), axis=0, keepdims=True)
        dy = err * (1.0 / D_MODEL)
        gnf = jnp.sum(dy * xn, axis=0, keepdims=True)
        u = dy * g
        dx2 = r * (u - xn * jnp.mean(u * xn, axis=-1, keepdims=True))
        dx2_ref[...] = dx2
        dx2b = dx2.astype(BF16)
        dx2b_ref[...] = dx2b
        dmix_ref[...] = lax.dot_general(dx2b, w, _NT, preferred_element_type=F32)

        @pl.when(i == 0)
        def _():
            gnf_ref[...] = jnp.zeros_like(gnf_ref)
            loss_ref[...] = jnp.zeros_like(loss_ref)

        gnf_ref[...] += gnf
        loss_ref[...] += jnp.broadcast_to(part, loss_ref.shape)

    return pl.pallas_call(
        body, name="out_proj_loss", grid=(SEQ // tm,),
        in_specs=[pl.BlockSpec((tm, D_MIX), lambda i: (i, 0)), pl.BlockSpec((tm, D_MODEL), lambda i: (i, 0)),
                  pl.BlockSpec((tm, D_MODEL), lambda i: (i, 0)), pl.BlockSpec(memory_space=pltpu.VMEM),
                  pl.BlockSpec((1, D_MODEL), lambda i: (0, 0))],
        out_specs=(pl.BlockSpec((tm, D_MODEL), lambda i: (i, 0)), pl.BlockSpec((tm, D_MODEL), lambda i: (i, 0)),
                   pl.BlockSpec((tm, D_MIX), lambda i: (i, 0)),
                   pl.BlockSpec((1, D_MODEL), lambda i: (0, 0)), pl.BlockSpec((8, 128), lambda i: (0, 0))),
        out_shape=(jax.ShapeDtypeStruct((SEQ, D_MODEL), F32), jax.ShapeDtypeStruct((SEQ, D_MODEL), BF16),
                   jax.ShapeDtypeStruct((SEQ, D_MIX), F32),
                   jax.ShapeDtypeStruct((1, D_MODEL), F32), jax.ShapeDtypeStruct((8, 128), F32)),
        compiler_params=_params(dimension_semantics=("arbitrary",)),
    )(mixed, x, target, w_out_full, norm_final)


def _gated_norm_bwd(a, gain, t, dy):
    r = lax.rsqrt(jnp.mean(a * a, axis=-1, keepdims=True) + RMS_EPS)
    an = a * r
    sg = _sigmoid(t)
    dn = dy * (t * sg)
    dt = dy * (an * gain) * (sg * (1.0 + t * (1.0 - sg)))
    u = dn * gain
    da = r * (u - an * jnp.mean(u * an, axis=-1, keepdims=True))
    return da, dt, dn * an


def _mix_bwd(proj, dmixed, attn, probs, shares, conv_full, norm_conv, norm_attn):
    def body(pj_ref, kvp_ref, cch_ref, cuh_ref, dmx_ref, attn_ref, p_ref, ps_ref, cw_ref, gc_ref, ga_ref,
             dpj_ref, gslab_ref, dattn_scr, nxt_scr, dkv_scr, acc_scr):
        step = pl.program_id(0)
        n = N_BLOCKS - 1 - step
        pj = pj_ref

        @pl.when(step == 0)
        def _():
            gslab_ref[...] = jnp.zeros_like(gslab_ref)
            nxt_scr[...] = jnp.zeros_like(nxt_scr)
            dkv_scr[...] = jnp.zeros_like(dkv_scr)
            acc_scr[...] = jnp.zeros_like(acc_scr)

        zhalo = _conv_halo(cch_ref, cuh_ref, n)
        cw = (cw_ref[0:1, :], cw_ref[1:2, :], cw_ref[2:3, :])
        gain_c = gc_ref[...]
        row = lax.broadcasted_iota(jnp.int32, (CHUNK, D_CONV), 0)

        dco_after = nxt_scr[...]
        for r in reversed(range(N_CHUNKS)):
            rows = _chunk_rows(r)
            cc, cu, z, z1, z2, co = _conv_chunk(pj_ref, zhalo, cw, r)
            cb = pj_ref[rows, OFF_CB:OFF_CB + D_CONV]
            da, dgate, gterm = _gated_norm_bwd(cb * co, gain_c, pj_ref[rows, OFF_GC:OFF_GC + D_CONV],
                                               dmx_ref[rows, 0:D_CONV])
            dpj_ref[rows, OFF_GC:OFF_GC + D_CONV] = dgate.astype(BF16)
            dpj_ref[rows, OFF_CB:OFF_CB + D_CONV] = (da * co).astype(BF16)
            dco = da * cb
            dco1 = jnp.where(row >= CHUNK - 1, pltpu.roll(dco_after, CHUNK - 1, 0), pltpu.roll(dco, CHUNK - 1, 0))
            dco2 = jnp.where(row >= CHUNK - 2, pltpu.roll(dco_after, CHUNK - 2, 0), pltpu.roll(dco, CHUNK - 2, 0))
            dz = cw[2] * dco + cw[1] * dco1 + cw[0] * dco2
            dpj_ref[rows, OFF_CC:OFF_CC + D_CONV] = (dz * cu).astype(BF16)
            dpj_ref[rows, OFF_CU:OFF_CU + D_CONV] = (dz * cc).astype(BF16)
            acc_scr[ACC_NORM_CONV] += gterm
            acc_scr[ACC_CONV0] += dco * z2
            acc_scr[ACC_CONV0 + 1] += dco * z1
            acc_scr[ACC_CONV0 + 2] += dco * z
            dco_after = dco
        nxt_scr[...] = dco_after

        ks, vs = _kv_bands(pj, kvp_ref)
        gain_a = ga_ref[...]

        for r in range(N_CHUNKS):
            rows = _chunk_rows(r)
            da, dgate, gterm = _gated_norm_bwd(attn_ref[rows, :], gain_a, pj_ref[rows, OFF_GA:OFF_GA + D_ATTN],
                                               dmx_ref[rows, D_CONV:D_MIX])
            dpj_ref[rows, OFF_GA:OFF_GA + D_ATTN] = dgate.astype(BF16)
            dattn_scr[rows, :] = da
            acc_scr[ACC_NORM_ATTN] += gterm

        in_lo = lax.broadcasted_iota(jnp.int32, (128, 128), 0) < HEAD_DIM
        half_ones = (jnp.where(in_lo, 1.0, 0.0).astype(BF16), jnp.where(in_lo, 0.0, 1.0).astype(BF16))
        lane_s = lax.broadcasted_iota(jnp.int32, (1, D_MODEL), 1)
        gsink = jnp.zeros((1, D_MODEL), F32)
        dk_t, dv_t = [], []
        for j in range(2):
            q_stack = _q_stack(pj, j)
            do_f = jnp.concatenate([dattn_scr[:, _pair_cols(j, i, 0)] for i in range(PAIRS_PER_KV)], axis=0)
            o_f = jnp.concatenate([attn_ref[:, _pair_cols(j, i, 0)] for i in range(PAIRS_PER_KV)], axis=0)
            prod = (do_f * o_f).astype(BF16)
            deltas = [jnp.dot(prod, half_ones[e], preferred_element_type=F32) for e in range(2)]
            do_b = do_f.astype(BF16)
            q_t, do_t = q_stack.T, do_b.T
            dq, dk_j, dv_j = None, None, None
            for e in range(2):
                p = p_ref[0, 2 * j + e]
                dp = lax.dot_general(do_b, vs[j][e], _NT, preferred_element_type=F32)
                ds = []
                for i in range(PAIRS_PER_KV):
                    rows = slice(BLOCK * i, BLOCK * (i + 1))
                    delta = deltas[e][rows, :]
                    ds.append((p[rows, :].astype(F32) * (dp[rows, :] - jnp.concatenate([delta, delta], axis=1))).astype(BF16))
                    gs_h = -jnp.sum(ps_ref[0, 2 * j + e, rows, 0:1] * delta[:, 0:1], axis=0, keepdims=True)
                    gsink = gsink + jnp.where(lane_s == _head(j, i, e), gs_h, 0.0)
                ds = jnp.concatenate(ds, axis=0)
                t = jnp.dot(ds, ks[j][e], preferred_element_type=F32)
                dq = t if dq is None else dq + t
                half = slice(HEAD_DIM * e, HEAD_DIM * (e + 1))
                a = jnp.dot(q_t[half, :], ds, preferred_element_type=F32)
                b = jnp.dot(do_t[half, :], p, preferred_element_type=F32)
                dk_j = a if dk_j is None else dk_j + a
                dv_j = b if dv_j is None else dv_j + b
            for i in range(PAIRS_PER_KV):
                dpj_ref[:, _pair_cols(j, i, OFF_Q)] = (dq[BLOCK * i:BLOCK * (i + 1), :] * SCALE).astype(BF16)
            dk_t.append(dk_j)
            dv_t.append(dv_j)
        dk = jnp.concatenate(dk_t, axis=0).T
        dv = jnp.concatenate(dv_t, axis=0).T
        dpj_ref[:, OFF_K:OFF_K + D_KV] = (dk[BLOCK:, :] + dkv_scr[:, 0:D_KV]).astype(BF16)
        dpj_ref[:, OFF_V:OFF_V + D_KV] = (dv[BLOCK:, :] + dkv_scr[:, D_KV:2 * D_KV]).astype(BF16)
        dkv_scr[:, 0:D_KV] = dk[:BLOCK, :]
        dkv_scr[:, D_KV:2 * D_KV] = dv[:BLOCK, :]
        gslab_ref[ROW_SINKS:ROW_SINKS + 1, :] += gsink

        @pl.when(step == N_BLOCKS - 1)
        def _():
            for k, slab_row in ((ACC_NORM_CONV, ROW_NORM_CONV), (ACC_NORM_ATTN, ROW_NORM_ATTN), (ACC_CONV0, ROW_CONV0),
                                (ACC_CONV0 + 1, ROW_CONV0 + 1), (ACC_CONV0 + 2, ROW_CONV0 + 2)):
                gslab_ref[slab_row:slab_row + 1, :] = jnp.sum(acc_scr[k], axis=0, keepdims=True)

    per_block = BLOCK // HALO
    last = N_BLOCKS - 1
    return pl.pallas_call(
        body, name="mix_bwd", grid=(N_BLOCKS,),
        in_specs=[
            pl.BlockSpec((BLOCK, D_PROJ), lambda s: (last - s, 0)),
            pl.BlockSpec((BLOCK, 2 * D_KV), lambda s: (jnp.maximum(last - s - 1, 0), OFF_K // (2 * D_KV))),
            pl.BlockSpec((HALO, D_CONV), lambda s: (jnp.maximum((last - s) * per_block - 1, 0), OFF_CC // D_CONV)),
            pl.BlockSpec((HALO, D_CONV), lambda s: (jnp.maximum((last - s) * per_block - 1, 0), OFF_CU // D_CONV)),
            pl.BlockSpec((BLOCK, D_MIX), lambda s: (last - s, 0)),
            pl.BlockSpec((BLOCK, D_ATTN), lambda s: (last - s, 0)),
            pl.BlockSpec((1, 4, STACK, 2 * BLOCK), lambda s: (last - s, 0, 0, 0)),
            pl.BlockSpec((1, 4, STACK, 128), lambda s: (last - s, 0, 0, 0)),
            pl.BlockSpec((8, D_CONV), lambda s: (0, 0)),
            pl.BlockSpec((1, D_CONV), lambda s: (0, 0)),
            pl.BlockSpec((1, D_ATTN), lambda s: (0, 0)),
        ],
        out_specs=(pl.BlockSpec((BLOCK, D_PROJ), lambda s: (last - s, 0)),
                   pl.BlockSpec((8, D_MODEL), lambda s: (0, 0))),
        out_shape=(jax.ShapeDtypeStruct((SEQ, D_PROJ), BF16), jax.ShapeDtypeStruct((8, D_MODEL), F32)),
        scratch_shapes=[pltpu.VMEM((BLOCK, D_ATTN), F32), pltpu.VMEM((CHUNK, D_CONV), F32),
                        pltpu.VMEM((BLOCK, 2 * D_KV), F32), pltpu.VMEM((N_ACC, CHUNK, D_MODEL), F32)],
        compiler_params=_params(dimension_semantics=("arbitrary",)),
    )(proj, proj, proj, proj, dmixed, attn, probs, shares, conv_full, norm_conv, norm_attn)


def _in_bwd_rs(dproj, w_full, x, dx2, norm_in, dw_in_chip, gslab, gnf, loss_part):
    tm = 256
    steps = SEQ // tm

    def body(dp_ref, w_hbm, x_ref, dx2_ref, g_ref, dwi_ref, gs_ref, gnf_ref, lp_ref, gx_ref, gwin_ref, gsum_ref,
             gni_scr, own, d2d, ici, myslab, slabs, w_ref, send_sems, recv_sems, local_sems):
        i = pl.program_id(0)
        rs_start, rs_finish = _ici_sum(dwi_ref, own, d2d, ici, send_sems, recv_sems, local_sems)
        slab_start, slab_finish = _slab_sum(myslab, slabs, send_sems, recv_sems, N_ICI_SUM_SEMS)

        @pl.when(i == 0)
        def _():
            gni_scr[...] = jnp.zeros_like(gni_scr)
            rs_start()
            w_load = pltpu.make_async_copy(w_hbm, w_ref, local_sems.at[1])
            w_load.start()
            w_load.wait()

        dh = jnp.dot(dp_ref[...], w_ref[...], preferred_element_type=F32)
        xv = x_ref[...]
        r = lax.rsqrt(jnp.mean(xv * xv, axis=-1, keepdims=True) + RMS_EPS)
        xn = xv * r
        u = dh * g_ref[...]
        gx_ref[...] = dx2_ref[...] + r * (u - xn * jnp.mean(u * xn, axis=-1, keepdims=True))
        gni_scr[...] += jnp.sum(dh * xn, axis=0, keepdims=True)

        @pl.when(i == steps - 1)
        def _():
            row = lax.broadcasted_iota(jnp.int32, (8, D_MODEL), 0)
            lane = lax.broadcasted_iota(jnp.int32, (8, D_MODEL), 1)
            slab = jnp.where(row == ROW_NORM_IN, gni_scr[...], jnp.where(row == ROW_NORM_FINAL, gnf_ref[...], gs_ref[...]))
            myslab[...] = jnp.where((row == ROW_SINKS) & (lane == LOSS_LANE), lp_ref[0:1, 0:1], slab)
            slab_start()
            gwin_ref[...] = rs_finish()
            gsum_ref[...] = slab_finish()

    const = lambda i: (0, 0)
    return pl.pallas_call(
        body, name="in_bwd", grid=(steps,),
        in_specs=[pl.BlockSpec((tm, D_PROJ), lambda i: (i, 0)), pl.BlockSpec(memory_space=pl.ANY),
                  pl.BlockSpec((tm, D_MODEL), lambda i: (i, 0)), pl.BlockSpec((tm, D_MODEL), lambda i: (i, 0)),
                  pl.BlockSpec((1, D_MODEL), const), pl.BlockSpec(memory_space=pl.ANY),
                  pl.BlockSpec((8, D_MODEL), const), pl.BlockSpec((1, D_MODEL), const), pl.BlockSpec((8, 128), const)],
        out_specs=(pl.BlockSpec((tm, D_MODEL), lambda i: (i, 0)), pl.BlockSpec((SHARD_IN, D_MODEL), const),
                   pl.BlockSpec((8, D_MODEL), const)),
        out_shape=(jax.ShapeDtypeStruct((SEQ, D_MODEL), F32), jax.ShapeDtypeStruct((SHARD_IN, D_MODEL), F32),
                   jax.ShapeDtypeStruct((8, D_MODEL), F32)),
        scratch_shapes=[pltpu.VMEM((1, D_MODEL), F32), pltpu.VMEM((SHARD_IN, D_MODEL), BF16),
                        pltpu.VMEM((SHARD_IN, D_MODEL), BF16), pltpu.VMEM((2, SHARD_IN, D_MODEL), BF16),
                        pltpu.VMEM((8, D_MODEL), F32), pltpu.VMEM((N_DEV, 8, D_MODEL), F32),
                        pltpu.VMEM((D_PROJ, D_MODEL), BF16),
                        pltpu.SemaphoreType.DMA((N_ICI_SUM_SEMS + 7,)), pltpu.SemaphoreType.DMA((N_ICI_SUM_SEMS + 7,)),
                        pltpu.SemaphoreType.DMA((2,))],
        compiler_params=_params(dimension_semantics=("arbitrary",)),
    )(dproj, w_full, x, dx2, norm_in, dw_in_chip, gslab, gnf, loss_part)


def _dw_rs(mixed, dx2b, dproj, h, table):
    tn_out, tn = 2 * SHARD_OUT, IN_PROJ_TILE
    out_steps, in_steps = D_MIX // tn_out, D_PROJ // tn
    steps = out_steps + in_steps
    out_order = (DG, NX, NY, OWN)

    def out_tile(i):
        chip = 2 * lax.axis_index("x") + lax.axis_index("y")
        return jnp.bitwise_xor(chip, (out_steps - 1) - jnp.minimum(i, out_steps - 1))

    def in_tile(table_ref, i):
        return _dw_entry(table_ref, jnp.maximum(i - out_steps, 0))

    def body(table_ref, mx_ref, dxb_ref, a_ref, h_hbm, chip_ref, gwo_ref, dwo, dwt, d2d_in, via, own, d2d, ici, b_ref,
             send_sems, recv_sems, local_sems):
        i = pl.program_id(0)
        h_load = pltpu.make_async_copy(h_hbm, b_ref, local_sems.at[8])

        @pl.when(i == 0)
        def _():
            h_load.start()

        @pl.when(i == out_steps)
        def _():
            h_load.wait()

        rs_start, rs_forward, rs_finish = _shard_sum(dwo, own, d2d, ici, send_sems, recv_sems, local_sems)
        before_tile, after_tiles, chip_finish = _chip_sum(
            dwt, d2d_in, via, chip_ref, lambda k: _dw_entry(table_ref, in_steps + k), send_sems, recv_sems, local_sems,
            N_SHARD_SUM_SEMS, 4)

        for j, k in enumerate(out_order):
            @pl.when(i == j + 1)
            def _():
                rs_start(k)

            if k != OWN:
                @pl.when(i == j + 2)
                def _():
                    rs_forward(k)

        @pl.when(i < out_steps)
        def _():
            tile = lax.dot_general(mx_ref[...], dxb_ref[...], _TN, preferred_element_type=F32).astype(BF16)
            for core in range(2):
                dwo[2 * out_tile(i) + core] = tile[SHARD_OUT * core:SHARD_OUT * (core + 1), :]

        @pl.when(i >= out_steps)
        def _():
            before_tile(i - out_steps)
            tile = lax.dot_general(a_ref[...], b_ref[...], _TN, preferred_element_type=F32).astype(BF16)
            dwt[pl.ds(pl.multiple_of(in_tile(table_ref, i) * tn, tn), tn), :] = tile

        @pl.when(i == steps - 2)
        def _():
            gwo_ref[...] = rs_finish()

        @pl.when(i == steps - 1)
        def _():
            after_tiles()
            chip_finish()

    vmem = pl.BlockSpec(memory_space=pltpu.VMEM)
    grid_spec = pltpu.PrefetchScalarGridSpec(
        num_scalar_prefetch=1, grid=(steps,),
        in_specs=[pl.BlockSpec((SEQ, tn_out), lambda i, table_ref: (0, out_tile(i))), vmem,
                  pl.BlockSpec((SEQ, tn), lambda i, table_ref: (0, in_tile(table_ref, i))),
                  pl.BlockSpec(memory_space=pl.ANY)],
        out_specs=(pl.BlockSpec(memory_space=pl.ANY), pl.BlockSpec((SHARD_OUT, D_MODEL), lambda i, table_ref: (0, 0))),
        scratch_shapes=[pltpu.VMEM((N_DEV, SHARD_OUT, D_MODEL), BF16),
                        pltpu.VMEM((D_PROJ, D_MODEL), BF16), pltpu.VMEM((3, SHARD_IN, D_MODEL), BF16),
                        pltpu.VMEM((2, HALF_IN, D_MODEL), BF16),
                        *_shard_sum_scratch(SHARD_OUT), pltpu.VMEM((SEQ, D_MODEL), BF16),
                        pltpu.SemaphoreType.DMA((N_SHARD_SUM_SEMS + N_CHIP_SUM_SEMS,)),
                        pltpu.SemaphoreType.DMA((N_SHARD_SUM_SEMS + N_CHIP_SUM_SEMS,)),
                        pltpu.SemaphoreType.DMA((9,))])
    return pl.pallas_call(
        body, name="dw", grid_spec=grid_spec,
        out_shape=(jax.ShapeDtypeStruct((4, SHARD_IN, D_MODEL), BF16), jax.ShapeDtypeStruct((SHARD_OUT, D_MODEL), F32)),
        compiler_params=_params(dimension_semantics=("arbitrary",)),
    )(table, mixed, dx2b, dproj, h)


def _adam_all(big_in, big_out, gsum, small, grad_x):
    n_chunks = 4
    n_big = 8

    def body(*refs):
        ins, outs = refs[:n_big + 1 + 18 + 1], refs[n_big + 1 + 18 + 1:n_big + 1 + 18 + 1 + 34]
        in_bufs, out_bufs, gx_buf = refs[-n_big - 6 - 4:-6 - 4], refs[-6 - 4:-4], refs[-4]
        in_sems, out_sems, gx_sems = refs[-3:]

        def gx_rows(j):
            return pl.ds(j * (SEQ // n_chunks), SEQ // n_chunks)

        def gx_load(j):
            return pltpu.make_async_copy(ins[27].at[gx_rows(j), :], gx_buf.at[gx_rows(j), :], gx_sems.at[j])

        def gx_store(j):
            return pltpu.make_async_copy(gx_buf.at[gx_rows(j), :], outs[33].at[gx_rows(j), :], gx_sems.at[n_chunks + j])

        def rows(a, j):
            tr = ins[a].shape[0] // n_chunks
            return pl.ds(j * tr, tr)

        def load(a, j):
            return pltpu.make_async_copy(ins[a].at[rows(a, j), :], in_bufs[a].at[rows(a, j), :], in_sems.at[a * n_chunks + j])

        def store(a, j):
            b, kind = divmod(a, 4)
            src = in_bufs[4 * b + 1] if kind == 0 else out_bufs[3 * b + kind - 1]
            return pltpu.make_async_copy(src.at[rows(a, j), :], outs[a].at[rows(a, j), :], out_sems.at[a * n_chunks + j])

        for j in range(n_chunks):
            for a in range(n_big):
                load(a, j).start()
            gx_load(j).start()

        def small_weights():
            gsum = ins[8][...]
            idx = _slot(lax.axis_index("x"), lax.axis_index("y"), lax.axis_index("c"))
            cg = jnp.zeros((3, SHARD_CONV), F32)
            for d in range(N_DEV):
                cg = jnp.where(idx == d, gsum[ROW_CONV0:ROW_CONV0 + 3, d * SHARD_CONV:(d + 1) * SHARD_CONV], cg)
            grads = (gsum[ROW_NORM_IN:ROW_NORM_IN + 1], gsum[ROW_SINKS:ROW_SINKS + 1, 0:N_Q_HEADS],
                     gsum[ROW_NORM_CONV:ROW_NORM_CONV + 1], gsum[ROW_NORM_ATTN:ROW_NORM_ATTN + 1],
                     gsum[ROW_NORM_FINAL:ROW_NORM_FINAL + 1], cg)
            for s, g in enumerate(grads):
                at = (slice(None), 0, slice(None)) if s == 5 else (slice(None), slice(None))
                w_ref, m_ref, v_ref = ins[9 + 3 * s:12 + 3 * s]
                delta, mn, vn = _adamw(w_ref[at], g, m_ref[at], v_ref[at])
                for ref, val in zip(outs[8 + 4 * s:12 + 4 * s], (g, delta, mn, vn)):
                    ref[at] = val
            outs[32][...] = gsum[ROW_SINKS:ROW_SINKS + 1, LOSS_LANE:LOSS_LANE + 1]

        small_weights()
        for j in range(n_chunks):
            for b in range(2):
                for a in range(4 * b, 4 * b + 4):
                    load(a, j).wait()
                w_buf, g_buf, m_buf, v_buf = in_bufs[4 * b:4 * b + 4]
                r = rows(4 * b, j)
                results = _adamw(w_buf[r, :], g_buf[r, :], m_buf[r, :], v_buf[r, :])
                for buf, val in zip(out_bufs[3 * b:3 * b + 3], results):
                    buf[r, :] = val
                for a in range(4 * b, 4 * b + 4):
                    store(a, j).start()
            gx_load(j).wait()
            gx_store(j).start()
        for j in range(n_chunks):
            for a in range(n_big):
                store(a, j).wait()
            gx_store(j).wait()

    vmem, hbm = pl.BlockSpec(memory_space=pltpu.VMEM), pl.BlockSpec(memory_space=pl.ANY)
    small_shapes = [a.shape for a in small[::3]]
    big_shapes = [(SHARD_IN, D_MODEL)] * 4 + [(SHARD_OUT, D_MODEL)] * 4
    out_shape = ([jax.ShapeDtypeStruct(s, F32) for s in big_shapes]
                 + [jax.ShapeDtypeStruct(s, F32) for s in small_shapes for _ in range(4)]
                 + [jax.ShapeDtypeStruct((1, 1), F32), jax.ShapeDtypeStruct((SEQ, D_MODEL), F32)])
    outs = pl.pallas_call(
        body, name="adam", in_specs=[hbm] * n_big + [vmem] * (1 + len(small)) + [hbm],
        out_specs=tuple([hbm] * n_big + [vmem] * (4 * len(small_shapes) + 1) + [hbm]), out_shape=tuple(out_shape),
        scratch_shapes=[pltpu.VMEM(s, F32) for s in big_shapes]
                       + [pltpu.VMEM(s, F32) for s in [(SHARD_IN, D_MODEL)] * 3 + [(SHARD_OUT, D_MODEL)] * 3]
                       + [pltpu.VMEM((SEQ, D_MODEL), F32),
                          pltpu.SemaphoreType.DMA((n_big * n_chunks,)), pltpu.SemaphoreType.DMA((n_big * n_chunks,)),
                          pltpu.SemaphoreType.DMA((2 * n_chunks,))],
        compiler_params=_params(),
    )(*big_in, *big_out, gsum, *small, grad_x)
    return outs[0:4], outs[4:8], [outs[8 + 4 * s:12 + 4 * s] for s in range(6)], outs[32], outs[33]


def _rows_first(a):
    return jnp.transpose(a, (1, 0, 2))


def kernel(x, norm_in, w_in, conv_w, attn_sinks, norm_conv_out, norm_attn_out, w_out, norm_final, loss_target, m_norm_in, m_w_in, m_conv_w, m_attn_sinks, m_norm_conv_out, m_norm_attn_out, m_w_out, m_norm_final, v_norm_in, v_w_in, v_conv_w, v_attn_sinks, v_norm_conv_out, v_norm_attn_out, v_w_out, v_norm_final):
    x2d = x.reshape(SEQ, D_MODEL)
    target = loss_target.reshape(SEQ, D_MODEL)
    nf = norm_final.reshape(1, D_MODEL)

    w_in_t, m_w_in_t, v_w_in_t = w_in[0].T, m_w_in[0].T, v_w_in[0].T
    tiles = jnp.asarray(TILE_ORDER, jnp.int32).reshape(-1)
    w_in_full, h, proj, g_out, conv_full = _gather_in_proj(x2d, norm_in, w_in_t, w_out[0], _rows_first(conv_w), tiles)
    sinks = attn_sinks.reshape(N_Q_HEADS)

    mixed, attn, probs, shares = _mix_fwd(proj, conv_full, sinks, norm_conv_out, norm_attn_out)
    dx2, dx2b, dmixed, gnf, loss_part = _out_proj_loss(mixed, x2d, target, g_out.reshape(D_MIX, D_MODEL), nf)
    dproj, gslab = _mix_bwd(proj, dmixed, attn, probs, shares, conv_full, norm_conv_out, norm_attn_out)
    dw_in_chip, g_w_out = _dw_rs(mixed, dx2b, dproj, h, jnp.asarray(DW_TABLE, jnp.int32).reshape(-1))
    grad_x, g_w_in, gsum = _in_bwd_rs(dproj, w_in_full, x2d, dx2, norm_in, dw_in_chip, gslab, gnf, loss_part)

    small = (norm_in, m_norm_in, v_norm_in, attn_sinks, m_attn_sinks, v_attn_sinks,
             norm_conv_out, m_norm_conv_out, v_norm_conv_out, norm_attn_out, m_norm_attn_out, v_norm_attn_out,
             nf, m_norm_final.reshape(1, D_MODEL), v_norm_final.reshape(1, D_MODEL),
             _rows_first(conv_w), _rows_first(m_conv_w), _rows_first(v_conv_w))
    big_in, big_out, (s_ni, s_sk, s_nc, s_na, s_nf, s_cv), loss, grad_x = _adam_all(
        (w_in_t, g_w_in, m_w_in_t, v_w_in_t), (w_out[0], g_w_out, m_w_out[0], v_w_out[0]), gsum, small, grad_x)

    def leaves(k):
        return (s_ni[k], big_in[k].T[None], jnp.transpose(s_cv[k], (1, 0, 2)), s_sk[k], s_nc[k], s_na[k], big_out[k][None],
                s_nf[k].reshape(D_MODEL))

    return (loss.reshape(()), grad_x.reshape(1, SEQ, D_MODEL), *leaves(0), *leaves(1), *leaves(2), *leaves(3))
```

```python
import jax
import jax.numpy as jnp
from jax import lax
from jax.experimental import pallas as pl
from jax.experimental.pallas import tpu as pltpu

F32 = jnp.float32
BF16 = jnp.bfloat16
MESH = pl.DeviceIdType.MESH

N_DEV = 8
SEQ = 2048
D_MODEL = 1024
D_CONV = 1024
D_ATTN = 1024
D_KV = 128
HEAD_DIM = 64
N_Q_HEADS = 16
N_PAIRS = N_Q_HEADS // 2
PAIRS_PER_KV = N_PAIRS // 2
D_MIX = D_CONV + D_ATTN
D_PROJ = 6400
SHARD_IN = D_PROJ // N_DEV
SHARD_OUT = D_MIX // N_DEV
SHARD_CONV = D_CONV // N_DEV
OFF_CB, OFF_CC, OFF_CU, OFF_GC, OFF_Q, OFF_K, OFF_V, OFF_GA = 0, 1024, 2048, 3072, 4096, 5120, 5248, 5376
BLOCK = 128
N_BLOCKS = SEQ // BLOCK
HALO = 8
CHUNK = 16
N_CHUNKS = BLOCK // CHUNK
RMS_EPS = 1e-5
NEG = -1e30
SCALE = HEAD_DIM ** -0.5
SLOPES = tuple(2.0 ** (-8.0 * (h + 1) / N_Q_HEADS) for h in range(N_Q_HEADS))

ADAM_LR = 0.001
ADAM_B1 = 0.9
ADAM_B2 = 0.999
ADAM_EPS = 1e-08
ADAM_WD = 0.01
ADAM_STEP = 10

ROW_NORM_IN, ROW_NORM_CONV, ROW_NORM_ATTN, ROW_NORM_FINAL, ROW_CONV0, ROW_SINKS = 0, 1, 2, 3, 4, 7
LOSS_LANE = N_Q_HEADS
ACC_NORM_CONV, ACC_NORM_ATTN, ACC_CONV0, N_ACC = 0, 1, 2, 5

VMEM_LIMIT = 56 * 1024 * 1024

_NT = (((1,), (1,)), ((), ()))
_TN = (((0,), (0,)), ((), ()))


def _params(**kw):
    return pltpu.CompilerParams(vmem_limit_bytes=VMEM_LIMIT, **kw)


def _adamw(w, g, m, v):
    m = ADAM_B1 * m + (1.0 - ADAM_B1) * g
    v = ADAM_B2 * v + (1.0 - ADAM_B2) * (g * g)
    m_hat = m / (1.0 - ADAM_B1 ** ADAM_STEP)
    v_hat = v / (1.0 - ADAM_B2 ** ADAM_STEP)
    delta = -ADAM_LR * (m_hat / (jnp.sqrt(v_hat) + ADAM_EPS) + ADAM_WD * w)
    return delta, m, v


def _sigmoid(t):
    return 1.0 / (1.0 + jnp.exp(-t))


def _slot(px, py, pc):
    return 4 * px + 2 * py + pc


OWN, NX, NY, DG = range(4)
HALF_IN = SHARD_IN // 2
N_GATHER_KINDS = 13
W_OUT_KINDS = N_GATHER_KINDS + 7


IN_PROJ_TILE = 640
TILE_ORDER = ((0, 1, 2, 3, 4, 5, 6, 7, 8, 9), (3, 4, 0, 1, 2, 8, 9, 5, 6, 7),
              (5, 6, 0, 1, 7, 8, 9, 2, 3, 4), (8, 9, 3, 4, 5, 6, 7, 0, 1, 2))
TILES_OWN, TILES_NEIGHBOURS = 2, 7


def _tile(table_ref, p):
    chip = 2 * lax.axis_index("x") + lax.axis_index("y")
    return table_ref[chip * len(TILE_ORDER[0]) + p]


DW_TILE_ORDER = tuple(tuple(reversed(row)) for row in TILE_ORDER)


def _tiles_until_complete(chip, owner):
    lo, hi = owner * 2 * SHARD_IN, (owner + 1) * 2 * SHARD_IN
    touching = [t for t in range(len(TILE_ORDER[0])) if t * IN_PROJ_TILE < hi and (t + 1) * IN_PROJ_TILE > lo]
    return 1 + max(DW_TILE_ORDER[chip].index(t) for t in touching)


DW_TABLE = tuple(DW_TILE_ORDER[chip] + tuple(_tiles_until_complete(chip, chip ^ flip) for flip in (0, 2, 1, 3))
                 for chip in range(4))


def _dw_entry(table_ref, p):
    chip = 2 * lax.axis_index("x") + lax.axis_index("y")
    return table_ref[chip * len(DW_TABLE[0]) + p]


def _gather_in_proj(x, norm_in, w_in_sh, w_out_sh, conv_sh, tiles):
    tn = IN_PROJ_TILE
    steps = D_PROJ // tn
    tm = 256

    def body(tiles_ref, x_hbm, g_ref, win_ref, wout_ref, cv_ref, wt_ref, h_ref, proj_ref, gout_ref, conv_ref,
             gin_ref, gcv_ref, wob_ref, x_ref, send_sems, recv_sems, local_sems):
        p = pl.program_id(0)
        local_sem = local_sems.at[0]
        x, y, c = lax.axis_index("x"), lax.axis_index("y"), lax.axis_index("c")
        me, sibling = (x, y, c), (x, y, 1 - c)
        nx, ny, dg = (1 - x, y, c), (x, 1 - y, c), (1 - x, 1 - y, c)

        def other(dev):
            return (dev[0], dev[1], 1 - dev[2])

        def shard(dev):
            return gin_ref.at[pl.ds(pl.multiple_of(_slot(*dev) * SHARD_IN, 16), SHARD_IN), :]

        def half(dev, h):
            return gin_ref.at[pl.ds(pl.multiple_of(_slot(*dev) * SHARD_IN + h * HALF_IN, 16), HALF_IN), :]

        def rc(ref, k, to):
            return pltpu.make_async_remote_copy(src_ref=ref, dst_ref=ref, send_sem=send_sems.at[k],
                                                recv_sem=recv_sems.at[k], device_id=to, device_id_type=MESH)

        def cv(k, dev, to):
            s = _slot(*dev)
            return pltpu.make_async_remote_copy(src_ref=gcv_ref.at[s], dst_ref=gcv_ref.at[s],
                                                send_sem=send_sems.at[N_GATHER_KINDS + k],
                                                recv_sem=recv_sems.at[N_GATHER_KINDS + k], device_id=to, device_id_type=MESH)

        def own_copies():
            return [rc(shard(me), 0, sibling),
                    rc(half(me, 0), 1, nx), rc(half(me, 1), 2, nx),
                    rc(half(me, 1), 4, ny), rc(half(me, 0), 3, ny),
                    cv(0, me, sibling)] + [cv(1 + j, me, peer) for j, peer in enumerate((nx, ny, dg))]

        def pass_on(dev, h, k_in, k_ici, k_d2d, half=half, base=0):
            rc(half(dev, h), base + k_in, me).wait_recv()
            if k_ici is not None:
                rc(half(dev, h), base + k_ici, ny if dev is nx else nx).start()
            rc(half(dev, h), base + k_d2d, sibling).start()

        def out_half(dev, h):
            return gout_ref.at[_slot(*dev), pl.ds(h * (SHARD_OUT // 2), SHARD_OUT // 2), :]

        def own_out_copies():
            src = lambda h: wob_ref.at[pl.ds(h * (SHARD_OUT // 2), SHARD_OUT // 2), :]

            def send(ref, dst, k, to):
                return pltpu.make_async_remote_copy(src_ref=ref, dst_ref=dst, send_sem=send_sems.at[W_OUT_KINDS + k],
                                                    recv_sem=recv_sems.at[W_OUT_KINDS + k], device_id=to, device_id_type=MESH)

            return [send(wob_ref, gout_ref.at[_slot(*me)], 0, sibling),
                    send(src(0), out_half(me, 0), 1, nx), send(src(1), out_half(me, 1), 2, nx),
                    send(src(1), out_half(me, 1), 4, ny), send(src(0), out_half(me, 0), 3, ny)]

        def own_out_local():
            return pltpu.make_async_copy(wob_ref, gout_ref.at[_slot(*me)], local_sems.at[1])

        @pl.when(p == 0)
        def _():
            gin_ref[pl.ds(pl.multiple_of(_slot(*me) * SHARD_IN, 16), SHARD_IN), :] = win_ref[...].astype(BF16)
            gcv_ref[_slot(*me)] = jnp.zeros((8, SHARD_CONV), F32)
            gcv_ref[_slot(*me), 0:3, :] = cv_ref[:, 0, :]
            for cp in own_copies():
                cp.start()
            wob_ref[...] = wout_ref[...].astype(BF16)
            x_load = pltpu.make_async_copy(x_hbm, x_ref, local_sems.at[2])
            x_load.start()
            x_load.wait()
            for t in range(SEQ // tm):
                xv = x_ref[tm * t:tm * (t + 1), :]
                r = lax.rsqrt(jnp.mean(xv * xv, axis=-1, keepdims=True) + RMS_EPS)
                h_ref[tm * t:tm * (t + 1), :] = (xv * r * g_ref[...]).astype(BF16)
            rc(shard(sibling), 0, me).wait_recv()

        @pl.when(p == TILES_OWN)
        def _():
            for args in ((nx, 0, 1, 5, 7), (ny, 1, 4, 6, 10), (nx, 1, 2, None, 8), (ny, 0, 3, None, 9)):
                pass_on(*args)
            for j, peer in enumerate((nx, ny, dg)):
                cv(1 + j, peer, me).wait_recv()
                cv(4 + j, peer, sibling).start()
            for (dev, h), k in (((nx, 0), 7), ((nx, 1), 8), ((ny, 0), 9), ((ny, 1), 10)):
                rc(half(other(dev), h), k, me).wait_recv()
            own_out_local().start()
            for cp in own_out_copies():
                cp.start()

        @pl.when(p == TILES_NEIGHBOURS - 1)
        def _():
            pass_on(dg, 0, 5, None, 11)
            pass_on(dg, 1, 6, None, 12)

        @pl.when(p == TILES_NEIGHBOURS)
        def _():
            for (dev, h), k in (((dg, 0), 11), ((dg, 1), 12)):
                rc(half(other(dev), h), k, me).wait_recv()
            pltpu.make_async_copy(gin_ref, wt_ref, local_sem).start()

        @pl.when(p == steps - 2)
        def _():
            for args in ((nx, 0, 1, 5, 7), (ny, 1, 4, 6, 10), (nx, 1, 2, None, 8), (ny, 0, 3, None, 9)):
                pass_on(*args, half=out_half, base=W_OUT_KINDS)

        w = gin_ref[pl.ds(pl.multiple_of(_tile(tiles_ref, p) * tn, tn), tn), :]
        proj_ref[...] = lax.dot_general(h_ref[...], w, _NT, preferred_element_type=F32)

        @pl.when(p == steps - 1)
        def _():
            cv(0, sibling, me).wait_recv()
            for j, peer in enumerate((nx, ny, dg)):
                cv(4 + j, other(peer), me).wait_recv()
            for d in range(N_DEV):
                conv_ref[:, d * SHARD_CONV:(d + 1) * SHARD_CONV] = gcv_ref[d]
            relayed = [rc(half(nx, 0), 5, ny), rc(half(ny, 1), 6, nx)]
            relayed += [rc(half(dev, h), k, sibling) for (dev, h), k in
                        (((nx, 0), 7), ((nx, 1), 8), ((ny, 0), 9), ((ny, 1), 10), ((dg, 0), 11), ((dg, 1), 12))]
            relayed += [cv(4 + j, peer, sibling) for j, peer in enumerate((nx, ny, dg))]
            for cp in own_copies() + relayed:
                cp.wait_send()
            pltpu.make_async_copy(gin_ref, wt_ref, local_sem).wait()
            pass_on(dg, 0, 5, None, 11, half=out_half, base=W_OUT_KINDS)
            pass_on(dg, 1, 6, None, 12, half=out_half, base=W_OUT_KINDS)
            rc(gout_ref.at[_slot(*sibling)], W_OUT_KINDS, me).wait_recv()
            out_relayed = [rc(out_half(nx, 0), W_OUT_KINDS + 5, ny), rc(out_half(ny, 1), W_OUT_KINDS + 6, nx)]
            for (dev, h), k in (((nx, 0), 7), ((nx, 1), 8), ((ny, 0), 9), ((ny, 1), 10), ((dg, 0), 11), ((dg, 1), 12)):
                rc(out_half(other(dev), h), W_OUT_KINDS + k, me).wait_recv()
                out_relayed.append(rc(out_half(dev, h), W_OUT_KINDS + k, sibling))
            for cp in own_out_copies() + out_relayed:
                cp.wait_send()
            own_out_local().wait()

    vmem = pl.BlockSpec(memory_space=pltpu.VMEM)
    grid_spec = pltpu.PrefetchScalarGridSpec(
        num_scalar_prefetch=1, grid=(steps,),
        in_specs=[pl.BlockSpec(memory_space=pl.ANY), vmem, vmem, vmem, vmem],
        out_specs=(pl.BlockSpec(memory_space=pl.ANY), vmem,
                   pl.BlockSpec((SEQ, tn), lambda p, tiles_ref: (0, _tile(tiles_ref, p))),
                   pl.BlockSpec(memory_space=pl.ANY), vmem),
        scratch_shapes=[pltpu.VMEM((D_PROJ, D_MODEL), BF16), pltpu.VMEM((N_DEV, 8, SHARD_CONV), F32),
                        pltpu.VMEM((SHARD_OUT, D_MODEL), BF16), pltpu.VMEM((SEQ, D_MODEL), F32),
                        pltpu.SemaphoreType.DMA((W_OUT_KINDS + N_GATHER_KINDS,)),
                        pltpu.SemaphoreType.DMA((W_OUT_KINDS + N_GATHER_KINDS,)),
                        pltpu.SemaphoreType.DMA((3,))])
    return pl.pallas_call(
        body, name="gather_in_proj", grid_spec=grid_spec,
        out_shape=(jax.ShapeDtypeStruct((D_PROJ, D_MODEL), BF16), jax.ShapeDtypeStruct((SEQ, D_MODEL), BF16),
                   jax.ShapeDtypeStruct((SEQ, D_PROJ), F32), jax.ShapeDtypeStruct((N_DEV, SHARD_OUT, D_MODEL), BF16),
                   jax.ShapeDtypeStruct((8, D_CONV), F32)),
        compiler_params=_params(dimension_semantics=("arbitrary",)),
    )(tiles, x, norm_in, w_in_sh, w_out_sh, conv_sh)


def _shard_sum(src, own, d2d, ici, via, send_sems, recv_sems, local_sems, base=0):
    x, y, c = lax.axis_index("x"), lax.axis_index("y"), lax.axis_index("c")
    sibling = (x, y, 1 - c)
    chips = [(x, y), (1 - x, y), (x, 1 - y), (1 - x, 1 - y)]
    half_rows = src.shape[1] // 2

    def rcopy(s, d, k, to):
        return pltpu.make_async_remote_copy(src_ref=s, dst_ref=d, send_sem=send_sems.at[base + k],
                                            recv_sem=recv_sems.at[base + k], device_id=to, device_id_type=MESH)

    def mine(k):
        return pltpu.make_async_copy(src.at[_slot(*chips[k], c)], own.at[k], local_sems.at[k])

    def to_sibling(k):
        return rcopy(src.at[_slot(*chips[k], 1 - c)], d2d.at[k], k, sibling)

    def to_chip(k):
        return rcopy(own.at[k], ici.at[k - 1], 3 + k, (*chips[k], c))

    def for_dg(h):
        return rcopy(own.at[DG, pl.ds(h * half_rows, half_rows), :], via.at[h], 6 + h, (*chips[NX + h], c))

    def start(k):
        mine(k).start()
        to_sibling(k).start()

    def forward(k):
        mine(k).wait()
        to_sibling(k).wait_recv()
        own[k] = (own[k].astype(F32) + d2d[k].astype(F32)).astype(BF16)
        if k == DG:
            for h in range(2):
                for_dg(h).start()
        else:
            h = 0 if k == NY else 1
            for_dg(h).wait_recv()
            rows = pl.ds(h * half_rows, half_rows)
            own[k, rows, :] = (own[k, rows, :].astype(F32) + via[h].astype(F32)).astype(BF16)
            to_chip(k).start()

    def finish():
        mine(0).wait()
        to_sibling(0).wait_recv()
        acc = own[0].astype(F32) + d2d[0].astype(F32)
        for k in (NX, NY):
            to_chip(k).wait_recv()
            acc = acc + ici[k - 1].astype(F32)
        for k in range(4):
            to_sibling(k).wait_send()
        for k in (NX, NY):
            to_chip(k).wait_send()
        for h in range(2):
            for_dg(h).wait_send()
        return acc

    return start, forward, finish


def _shard_sum_scratch(rows):
    return [pltpu.VMEM((4, rows, D_MODEL), BF16), pltpu.VMEM((4, rows, D_MODEL), BF16),
            pltpu.VMEM((3, rows, D_MODEL), BF16), pltpu.VMEM((2, rows // 2, D_MODEL), BF16)]


N_SHARD_SUM_SEMS = 8


N_CHIP_SUM_SEMS = 5


def _chip_sum(dwt, d2d, via, out_hbm, tiles_until, send_sems, recv_sems, local_sems, base, local_base):
    x, y, c = lax.axis_index("x"), lax.axis_index("y"), lax.axis_index("c")
    sibling, nx, ny = (x, y, 1 - c), (1 - x, y, c), (x, 1 - y, c)
    chips = [(x, y), (1 - x, y), (x, 1 - y), (1 - x, 1 - y)]

    def shard(s):
        return dwt.at[pl.ds(pl.multiple_of(s * SHARD_IN, 16), SHARD_IN), :]

    def half(ref, h):
        return ref.at[pl.ds(h * HALF_IN, HALF_IN), :]

    def rc(s, d, k, to):
        return pltpu.make_async_remote_copy(src_ref=s, dst_ref=d, send_sem=send_sems.at[base + k],
                                            recv_sem=recv_sems.at[base + k], device_id=to, device_id_type=MESH)

    def to_sibling(k):
        return rc(shard(_slot(*chips[k], 1 - c)), d2d.at[k - 1], k - 1, sibling)

    for_dg = (lambda: rc(half(d2d.at[DG - 1], 0), via.at[0], 3, nx), lambda: rc(half(d2d.at[DG - 1], 1), via.at[1], 4, ny))

    def save(k):
        return pltpu.make_async_copy(d2d.at[k - 1], out_hbm.at[k], local_sems.at[local_base + k])

    own_saves = (lambda: pltpu.make_async_copy(shard(_slot(x, y, c)), out_hbm.at[OWN], local_sems.at[local_base]),
                 lambda: pltpu.make_async_copy(shard(_slot(x, y, 1 - c)), out_hbm.at[3], local_sems.at[local_base + 3]))

    def before_tile(n):
        for k in (NX, NY, DG):
            @pl.when(tiles_until(k) == n)
            def _():
                to_sibling(k).start()

            @pl.when(tiles_until(k) + 1 == n)
            def _():
                to_sibling(k).wait_recv()
                d2d[k - 1] = (shard(_slot(*chips[k], c))[...].astype(F32) + d2d[k - 1].astype(F32)).astype(BF16)
                if k == DG:
                    for cp in for_dg:
                        cp().start()

    def after_tiles():
        for cp in own_saves:
            cp().start()

    def finish():
        for k, h in ((NY, 0), (NX, 1)):
            for_dg[h]().wait_recv()
            rows = pl.ds(h * HALF_IN, HALF_IN)
            d2d[k - 1, rows, :] = (d2d[k - 1, rows, :].astype(F32) + via[h].astype(F32)).astype(BF16)
            save(k).start()
        for cp in own_saves + (lambda: save(NX), lambda: save(NY)):
            cp().wait()
        for cp in (lambda: to_sibling(NX), lambda: to_sibling(NY), lambda: to_sibling(DG)) + for_dg:
            cp().wait_send()

    return before_tile, after_tiles, finish


N_ICI_SUM_SEMS = 3


def _ici_sum(src, own, d2d, ici, send_sems, recv_sems, local_sems, base=0):
    x, y, c = lax.axis_index("x"), lax.axis_index("y"), lax.axis_index("c")

    def rc(s, d, k, to):
        return pltpu.make_async_remote_copy(src_ref=s, dst_ref=d, send_sem=send_sems.at[base + k],
                                            recv_sem=recv_sems.at[base + k], device_id=to, device_id_type=MESH)

    copies = (lambda: rc(src.at[NX], ici.at[0], 0, (1 - x, y, c)), lambda: rc(src.at[NY], ici.at[1], 1, (x, 1 - y, c)),
              lambda: rc(src.at[3], d2d, 2, (x, y, 1 - c)))
    mine = lambda: pltpu.make_async_copy(src.at[OWN], own, local_sems.at[0])

    def start():
        for cp in copies + (mine,):
            cp().start()

    def finish():
        mine().wait()
        for cp in copies:
            cp().wait_recv()
        acc = own[...].astype(F32) + d2d[...].astype(F32) + ici[0].astype(F32) + ici[1].astype(F32)
        for cp in copies:
            cp().wait_send()
        return acc

    return start, finish


def _slab_sum(myslab, slabs, send_sems, recv_sems, base):
    x, y, c = lax.axis_index("x"), lax.axis_index("y"), lax.axis_index("c")
    me = _slot(x, y, c)
    peers = [(x, y, 1 - c), (1 - x, y, c), (x, 1 - y, c), (1 - x, 1 - y, c),
             (1 - x, y, 1 - c), (x, 1 - y, 1 - c), (1 - x, 1 - y, 1 - c)]

    def cp(k):
        return pltpu.make_async_remote_copy(src_ref=myslab, dst_ref=slabs.at[me], send_sem=send_sems.at[base + k],
                                            recv_sem=recv_sems.at[base + k], device_id=peers[k], device_id_type=MESH)

    def start():
        slabs[me] = myslab[...]
        for k in range(7):
            cp(k).start()

    def finish():
        for k in range(7):
            cp(k).wait_recv()
        total = slabs[0]
        for d in range(1, N_DEV):
            total = total + slabs[d]
        for k in range(7):
            cp(k).wait_send()
        return total

    return start, finish


def _chunk_rows(r):
    return slice(r * CHUNK, (r + 1) * CHUNK)


def _conv_halo(cch_ref, cuh_ref, n):
    zh = jnp.where(n > 0, cch_ref[...] * cuh_ref[...], 0.0)
    return jnp.concatenate([zh] * (CHUNK // HALO), axis=0)


def _conv_chunk(pj_ref, zhalo, cw, r):
    rows = _chunk_rows(r)
    cc = pj_ref[rows, OFF_CC:OFF_CC + D_CONV]
    cu = pj_ref[rows, OFF_CU:OFF_CU + D_CONV]
    z = cc * cu
    before = _chunk_rows(r - 1)
    zprev = pj_ref[before, OFF_CC:OFF_CC + D_CONV] * pj_ref[before, OFF_CU:OFF_CU + D_CONV] if r > 0 else zhalo
    row = lax.broadcasted_iota(jnp.int32, (CHUNK, D_CONV), 0)
    z1 = jnp.where(row < 1, pltpu.roll(zprev, 1, 0), pltpu.roll(z, 1, 0))
    z2 = jnp.where(row < 2, pltpu.roll(zprev, 2, 0), pltpu.roll(z, 2, 0))
    co = cw[0] * z2 + cw[1] * z1 + cw[2] * z
    return cc, cu, z, z1, z2, co


def _gated_norm(a, gain, t):
    r = lax.rsqrt(jnp.mean(a * a, axis=-1, keepdims=True) + RMS_EPS)
    return a * r * gain * (t * _sigmoid(t))


def _kv_bands(pj, kvp_ref):
    lane = lax.broadcasted_iota(jnp.int32, (2 * BLOCK, D_KV), 1)
    lo = lane < HEAD_DIM

    def bands(prev, cur):
        b = jnp.concatenate([prev, cur], axis=0)
        br = pltpu.roll(b, HEAD_DIM, 1)
        zero = jnp.zeros_like(b)
        return ((jnp.where(lo, b, zero).astype(BF16), jnp.where(lo, zero, br).astype(BF16)),
                (jnp.where(lo, br, zero).astype(BF16), jnp.where(lo, zero, b).astype(BF16)))

    ks = bands(kvp_ref[:, 0:D_KV], pj[:, OFF_K:OFF_K + D_KV])
    vs = bands(kvp_ref[:, D_KV:2 * D_KV], pj[:, OFF_V:OFF_V + D_KV])
    return ks, vs


STACK = PAIRS_PER_KV * BLOCK


def _head(j, i, e):
    return 2 * (PAIRS_PER_KV * j + i) + e


def _pair_cols(j, i, off):
    p = PAIRS_PER_KV * j + i
    return slice(off + 128 * p, off + 128 * (p + 1))


def _fill_attn_bias(bias_scr, first_block):
    qi = lax.broadcasted_iota(jnp.int32, (BLOCK, 2 * BLOCK), 0)
    kj = lax.broadcasted_iota(jnp.int32, (BLOCK, 2 * BLOCK), 1)
    dist = BLOCK + qi - kj
    valid = (dist >= 0) & (dist < BLOCK)
    if first_block:
        valid = valid & (kj >= BLOCK)
    distf = dist.astype(F32)
    for j in range(2):
        for e in range(2):
            for i in range(PAIRS_PER_KV):
                bias_scr[2 * j + e, BLOCK * i:BLOCK * (i + 1), :] = jnp.where(valid, -SLOPES[_head(j, i, e)] * distf, NEG)


def _q_stack(pj, j):
    return jnp.concatenate([(pj[:, _pair_cols(j, i, OFF_Q)] * SCALE).astype(BF16) for i in range(PAIRS_PER_KV)], axis=0)


def _attn_probs(q_stack, kband, bias_ref, sinks):
    s = lax.dot_general(q_stack, kband, _NT, preferred_element_type=F32)
    ones = jnp.ones((128, 128), BF16)
    probs, shares = [], []
    for i, sink in enumerate(sinks):
        rows = slice(BLOCK * i, BLOCK * (i + 1))
        t = s[rows, :] + bias_ref[rows, :]
        m = jnp.broadcast_to(jnp.max(t, axis=-1, keepdims=True), (BLOCK, 128))
        m = jnp.maximum(m, sink)
        p = [jnp.exp(t[:, :128] - m), jnp.exp(t[:, 128:] - m)]
        es = jnp.exp(sink - m)
        total = (jnp.dot(p[0].astype(BF16), ones, preferred_element_type=F32)
                 + jnp.dot(p[1].astype(BF16), ones, preferred_element_type=F32))
        inv = 1.0 / (total + es)
        probs.append(jnp.concatenate([p[0] * inv, p[1] * inv], axis=1))
        shares.append(es * inv)
    return jnp.concatenate(probs, axis=0), jnp.concatenate(shares, axis=0)


def _attn_group(pj, ks, vs, bias_scr, sink_ref, j):
    q_stack = _q_stack(pj, j)
    out, probs, shares = None, [], []
    for e in range(2):
        p, ps = _attn_probs(q_stack, ks[j][e], bias_scr.at[2 * j + e],
                            [sink_ref[_head(j, i, e)] for i in range(PAIRS_PER_KV)])
        p = p.astype(BF16)
        o = jnp.dot(p, vs[j][e], preferred_element_type=F32)
        out = o if out is None else out + o
        probs.append(p)
        shares.append(ps)
    return out, probs, shares


def _mix_fwd(proj, conv_full, sinks, norm_conv, norm_attn):
    def body(pj_ref, kvp_ref, cch_ref, cuh_ref, cw_ref, sink_ref, gc_ref, ga_ref,
             mixed_ref, attn_scr, p_ref, ps_ref, bias_scr):
        n = pl.program_id(0)
        pj = pj_ref

        @pl.when(n == 0)
        def _():
            _fill_attn_bias(bias_scr, first_block=True)

        @pl.when(n == 1)
        def _():
            _fill_attn_bias(bias_scr, first_block=False)

        zhalo = _conv_halo(cch_ref, cuh_ref, n)
        cw = (cw_ref[0:1, :], cw_ref[1:2, :], cw_ref[2:3, :])
        gain_c = gc_ref[...]

        for r in range(N_CHUNKS):
            rows = _chunk_rows(r)
            co = _conv_chunk(pj_ref, zhalo, cw, r)[-1]
            y = _gated_norm(pj_ref[rows, OFF_CB:OFF_CB + D_CONV] * co, gain_c, pj_ref[rows, OFF_GC:OFF_GC + D_CONV])
            mixed_ref[rows, 0:D_CONV] = y.astype(BF16)

        ks, vs = _kv_bands(pj, kvp_ref)
        for j in range(2):
            out, probs, shares = _attn_group(pj, ks, vs, bias_scr, sink_ref, j)
            for e in range(2):
                p_ref[0, 2 * j + e] = probs[e]
                ps_ref[0, 2 * j + e] = shares[e]
            for i in range(PAIRS_PER_KV):
                attn_scr[:, _pair_cols(j, i, 0)] = out[BLOCK * i:BLOCK * (i + 1), :]
        gain_a = ga_ref[...]

        for r in range(N_CHUNKS):
            rows = _chunk_rows(r)
            y = _gated_norm(attn_scr[rows, :], gain_a, pj_ref[rows, OFF_GA:OFF_GA + D_ATTN])
            mixed_ref[rows, D_CONV:D_MIX] = y.astype(BF16)

    per_block = BLOCK // HALO
    return pl.pallas_call(
        body, name="mix_fwd", grid=(N_BLOCKS,),
        in_specs=[
            pl.BlockSpec((BLOCK, D_PROJ), lambda n: (n, 0)),
            pl.BlockSpec((BLOCK, 2 * D_KV), lambda n: (jnp.maximum(n - 1, 0), OFF_K // (2 * D_KV))),
            pl.BlockSpec((HALO, D_CONV), lambda n: (jnp.maximum(n * per_block - 1, 0), OFF_CC // D_CONV)),
            pl.BlockSpec((HALO, D_CONV), lambda n: (jnp.maximum(n * per_block - 1, 0), OFF_CU // D_CONV)),
            pl.BlockSpec((8, D_CONV), lambda n: (0, 0)),
            pl.BlockSpec(memory_space=pltpu.SMEM),
            pl.BlockSpec((1, D_CONV), lambda n: (0, 0)),
            pl.BlockSpec((1, D_ATTN), lambda n: (0, 0)),
        ],
        out_specs=(pl.BlockSpec((BLOCK, D_MIX), lambda n: (n, 0)), pl.BlockSpec((BLOCK, D_ATTN), lambda n: (n, 0)),
                   pl.BlockSpec((1, 4, STACK, 2 * BLOCK), lambda n: (n, 0, 0, 0)),
                   pl.BlockSpec((1, 4, STACK, 128), lambda n: (n, 0, 0, 0))),
        out_shape=(jax.ShapeDtypeStruct((SEQ, D_MIX), BF16), jax.ShapeDtypeStruct((SEQ, D_ATTN), F32),
                   jax.ShapeDtypeStruct((N_BLOCKS, 4, STACK, 2 * BLOCK), BF16),
                   jax.ShapeDtypeStruct((N_BLOCKS, 4, STACK, 128), F32)),
        scratch_shapes=[pltpu.VMEM((4, STACK, 2 * BLOCK), F32)],
        compiler_params=_params(dimension_semantics=("arbitrary",)),
    )(proj, proj, proj, proj, conv_full, sinks, norm_conv, norm_attn)


def _out_proj_loss(mixed, x, target, w_out_full, norm_final):
    tm = 256

    def body(mx_ref, x_ref, t_ref, w_ref, g_ref, dx2_ref, dx2b_ref, dmix_ref, gnf_ref, loss_ref):
        i = pl.program_id(0)
        w = w_ref[...]
        x2 = x_ref[...] + jnp.dot(mx_ref[...], w, preferred_element_type=F32)
        r = lax.rsqrt(jnp.mean(x2 * x2, axis=-1, keepdims=True) + RMS_EPS)
        xn = x2 * r
        g = g_ref[...]
        err = xn * g - t_ref[...]
        part = 0.5 * jnp.sum(jnp.mean(err * err, axis=-1, keepdims=True), axis=0, keepdims=True)
        dy = err * (1.0 / D_MODEL)
        gnf = jnp.sum(dy * xn, axis=0, keepdims=True)
        u = dy * g
        dx2 = r * (u - xn * jnp.mean(u * xn, axis=-1, keepdims=True))
        dx2_ref[...] = dx2
        dx2b = dx2.astype(BF16)
        dx2b_ref[...] = dx2b
        dmix_ref[...] = lax.dot_general(dx2b, w, _NT, preferred_element_type=F32)

        @pl.when(i == 0)
        def _():
            gnf_ref[...] = jnp.zeros_like(gnf_ref)
            loss_ref[...] = jnp.zeros_like(loss_ref)

        gnf_ref[...] += gnf
        loss_ref[...] += jnp.broadcast_to(part, loss_ref.shape)

    return pl.pallas_call(
        body, name="out_proj_loss", grid=(SEQ // tm,),
        in_specs=[pl.BlockSpec((tm, D_MIX), lambda i: (i, 0)), pl.BlockSpec((tm, D_MODEL), lambda i: (i, 0)),
                  pl.BlockSpec((tm, D_MODEL), lambda i: (i, 0)), pl.BlockSpec(memory_space=pltpu.VMEM),
                  pl.BlockSpec((1, D_MODEL), lambda i: (0, 0))],
        out_specs=(pl.BlockSpec((tm, D_MODEL), lambda i: (i, 0)), pl.BlockSpec((tm, D_MODEL), lambda i: (i, 0)),
                   pl.BlockSpec((tm, D_MIX), lambda i: (i, 0)),
                   pl.BlockSpec((1, D_MODEL), lambda i: (0, 0)), pl.BlockSpec((8, 128), lambda i: (0, 0))),
        out_shape=(jax.ShapeDtypeStruct((SEQ, D_MODEL), F32), jax.ShapeDtypeStruct((SEQ, D_MODEL), BF16),
                   jax.ShapeDtypeStruct((SEQ, D_MIX), F32),
                   jax.ShapeDtypeStruct((1, D_MODEL), F32), jax.ShapeDtypeStruct((8, 128), F32)),
        compiler_params=_params(dimension_semantics=("arbitrary",)),
    )(mixed, x, target, w_out_full, norm_final)


def _gated_norm_bwd(a, gain, t, dy):
    r = lax.rsqrt(jnp.mean(a * a, axis=-1, keepdims=True) + RMS_EPS)
    an = a * r
    sg = _sigmoid(t)
    dn = dy * (t * sg)
    dt = dy * (an * gain) * (sg * (1.0 + t * (1.0 - sg)))
    u = dn * gain
    da = r * (u - an * jnp.mean(u * an, axis=-1, keepdims=True))
    return da, dt, dn * an


def _mix_bwd(proj, dmixed, attn, probs, shares, conv_full, norm_conv, norm_attn):
    def body(pj_ref, kvp_ref, cch_ref, cuh_ref, dmx_ref, attn_ref, p_ref, ps_ref, cw_ref, gc_ref, ga_ref,
             dpj_ref, gslab_ref, dattn_scr, nxt_scr, dkv_scr, acc_scr):
        step = pl.program_id(0)
        n = N_BLOCKS - 1 - step
        pj = pj_ref

        @pl.when(step == 0)
        def _():
            gslab_ref[...] = jnp.zeros_like(gslab_ref)
            nxt_scr[...] = jnp.zeros_like(nxt_scr)
            dkv_scr[...] = jnp.zeros_like(dkv_scr)
            acc_scr[...] = jnp.zeros_like(acc_scr)

        zhalo = _conv_halo(cch_ref, cuh_ref, n)
        cw = (cw_ref[0:1, :], cw_ref[1:2, :], cw_ref[2:3, :])
        gain_c = gc_ref[...]
        row = lax.broadcasted_iota(jnp.int32, (CHUNK, D_CONV), 0)

        dco_after = nxt_scr[...]
        for r in reversed(range(N_CHUNKS)):
            rows = _chunk_rows(r)
            cc, cu, z, z1, z2, co = _conv_chunk(pj_ref, zhalo, cw, r)
            cb = pj_ref[rows, OFF_CB:OFF_CB + D_CONV]
            da, dgate, gterm = _gated_norm_bwd(cb * co, gain_c, pj_ref[rows, OFF_GC:OFF_GC + D_CONV],
                                               dmx_ref[rows, 0:D_CONV])
            dpj_ref[rows, OFF_GC:OFF_GC + D_CONV] = dgate.astype(BF16)
            dpj_ref[rows, OFF_CB:OFF_CB + D_CONV] = (da * co).astype(BF16)
            dco = da * cb
            dco1 = jnp.where(row >= CHUNK - 1, pltpu.roll(dco_after, CHUNK - 1, 0), pltpu.roll(dco, CHUNK - 1, 0))
            dco2 = jnp.where(row >= CHUNK - 2, pltpu.roll(dco_after, CHUNK - 2, 0), pltpu.roll(dco, CHUNK - 2, 0))
            dz = cw[2] * dco + cw[1] * dco1 + cw[0] * dco2
            dpj_ref[rows, OFF_CC:OFF_CC + D_CONV] = (dz * cu).astype(BF16)
            dpj_ref[rows, OFF_CU:OFF_CU + D_CONV] = (dz * cc).astype(BF16)
            acc_scr[ACC_NORM_CONV] += gterm
            acc_scr[ACC_CONV0] += dco * z2
            acc_scr[ACC_CONV0 + 1] += dco * z1
            acc_scr[ACC_CONV0 + 2] += dco * z
            dco_after = dco
        nxt_scr[...] = dco_after

        ks, vs = _kv_bands(pj, kvp_ref)
        gain_a = ga_ref[...]

        for r in range(N_CHUNKS):
            rows = _chunk_rows(r)
            da, dgate, gterm = _gated_norm_bwd(attn_ref[rows, :], gain_a, pj_ref[rows, OFF_GA:OFF_GA + D_ATTN],
                                               dmx_ref[rows, D_CONV:D_MIX])
            dpj_ref[rows, OFF_GA:OFF_GA + D_ATTN] = dgate.astype(BF16)
            dattn_scr[rows, :] = da
            acc_scr[ACC_NORM_ATTN] += gterm

        in_lo = lax.broadcasted_iota(jnp.int32, (128, 128), 0) < HEAD_DIM
        half_ones = (jnp.where(in_lo, 1.0, 0.0).astype(BF16), jnp.where(in_lo, 0.0, 1.0).astype(BF16))
        lane_s = lax.broadcasted_iota(jnp.int32, (1, D_MODEL), 1)
        gsink = jnp.zeros((1, D_MODEL), F32)
        dk_t, dv_t = [], []
        for j in range(2):
            q_stack = _q_stack(pj, j)
            do_f = jnp.concatenate([dattn_scr[:, _pair_cols(j, i, 0)] for i in range(PAIRS_PER_KV)], axis=0)
            o_f = jnp.concatenate([attn_ref[:, _pair_cols(j, i, 0)] for i in range(PAIRS_PER_KV)], axis=0)
            prod = (do_f * o_f).astype(BF16)
            deltas = [jnp.dot(prod, half_ones[e], preferred_element_type=F32) for e in range(2)]
            do_b = do_f.astype(BF16)
            q_t, do_t = q_stack.T, do_b.T
            dq, dk_j, dv_j = None, None, None
            for e in range(2):
                p = p_ref[0, 2 * j + e]
                dp = lax.dot_general(do_b, vs[j][e], _NT, preferred_element_type=F32)
                ds = []
                for i in range(PAIRS_PER_KV):
                    rows = slice(BLOCK * i, BLOCK * (i + 1))
                    delta = deltas[e][rows, :]
                    ds.append((p[rows, :].astype(F32) * (dp[rows, :] - jnp.concatenate([delta, delta], axis=1))).astype(BF16))
                    gs_h = -jnp.sum(ps_ref[0, 2 * j + e, rows, 0:1] * delta[:, 0:1], axis=0, keepdims=True)
                    gsink = gsink + jnp.where(lane_s == _head(j, i, e), gs_h, 0.0)
                ds = jnp.concatenate(ds, axis=0)
                t = jnp.dot(ds, ks[j][e], preferred_element_type=F32)
                dq = t if dq is None else dq + t
                half = slice(HEAD_DIM * e, HEAD_DIM * (e + 1))
                a = jnp.dot(q_t[half, :], ds, preferred_element_type=F32)
                b = jnp.dot(do_t[half, :], p, preferred_element_type=F32)
                dk_j = a if dk_j is None else dk_j + a
                dv_j = b if dv_j is None else dv_j + b
            for i in range(PAIRS_PER_KV):
                dpj_ref[:, _pair_cols(j, i, OFF_Q)] = (dq[BLOCK * i:BLOCK * (i + 1), :] * SCALE).astype(BF16)
            dk_t.append(dk_j)
            dv_t.append(dv_j)
        dk = jnp.concatenate(dk_t, axis=0).T
        dv = jnp.concatenate(dv_t, axis=0).T
        dpj_ref[:, OFF_K:OFF_K + D_KV] = (dk[BLOCK:, :] + dkv_scr[:, 0:D_KV]).astype(BF16)
        dpj_ref[:, OFF_V:OFF_V + D_KV] = (dv[BLOCK:, :] + dkv_scr[:, D_KV:2 * D_KV]).astype(BF16)
        dkv_scr[:, 0:D_KV] = dk[:BLOCK, :]
        dkv_scr[:, D_KV:2 * D_KV] = dv[:BLOCK, :]
        gslab_ref[ROW_SINKS:ROW_SINKS + 1, :] += gsink

        @pl.when(step == N_BLOCKS - 1)
        def _():
            for k, slab_row in ((ACC_NORM_CONV, ROW_NORM_CONV), (ACC_NORM_ATTN, ROW_NORM_ATTN), (ACC_CONV0, ROW_CONV0),
                                (ACC_CONV0 + 1, ROW_CONV0 + 1), (ACC_CONV0 + 2, ROW_CONV0 + 2)):
                gslab_ref[slab_row:slab_row + 1, :] = jnp.sum(acc_scr[k], axis=0, keepdims=True)

    per_block = BLOCK // HALO
    last = N_BLOCKS - 1
    return pl.pallas_call(
        body, name="mix_bwd", grid=(N_BLOCKS,),
        in_specs=[
            pl.BlockSpec((BLOCK, D_PROJ), lambda s: (last - s, 0)),
            pl.BlockSpec((BLOCK, 2 * D_KV), lambda s: (jnp.maximum(last - s - 1, 0), OFF_K // (2 * D_KV))),
            pl.BlockSpec((HALO, D_CONV), lambda s: (jnp.maximum((last - s) * per_block - 1, 0), OFF_CC // D_CONV)),
            pl.BlockSpec((HALO, D_CONV), lambda s: (jnp.maximum((last - s) * per_block - 1, 0), OFF_CU // D_CONV)),
            pl.BlockSpec((BLOCK, D_MIX), lambda s: (last - s, 0)),
            pl.BlockSpec((BLOCK, D_ATTN), lambda s: (last - s, 0)),
            pl.BlockSpec((1, 4, STACK, 2 * BLOCK), lambda s: (last - s, 0, 0, 0)),
            pl.BlockSpec((1, 4, STACK, 128), lambda s: (last - s, 0, 0, 0)),
            pl.BlockSpec((8, D_CONV), lambda s: (0, 0)),
            pl.BlockSpec((1, D_CONV), lambda s: (0, 0)),
            pl.BlockSpec((1, D_ATTN), lambda s: (0, 0)),
        ],
        out_specs=(pl.BlockSpec((BLOCK, D_PROJ), lambda s: (last - s, 0)),
                   pl.BlockSpec((8, D_MODEL), lambda s: (0, 0))),
        out_shape=(jax.ShapeDtypeStruct((SEQ, D_PROJ), BF16), jax.ShapeDtypeStruct((8, D_MODEL), F32)),
        scratch_shapes=[pltpu.VMEM((BLOCK, D_ATTN), F32), pltpu.VMEM((CHUNK, D_CONV), F32),
                        pltpu.VMEM((BLOCK, 2 * D_KV), F32), pltpu.VMEM((N_ACC, CHUNK, D_MODEL), F32)],
        compiler_params=_params(dimension_semantics=("arbitrary",)),
    )(proj, proj, proj, proj, dmixed, attn, probs, shares, conv_full, norm_conv, norm_attn)


def _in_bwd_rs(dproj, w_full, x, dx2, norm_in, dw_in_chip, gslab, gnf, loss_part):
    tm = 256
    steps = SEQ // tm

    def body(dp_ref, w_hbm, x_ref, dx2_ref, g_ref, dwi_ref, gs_ref, gnf_ref, lp_ref, gx_ref, gwin_ref, gsum_ref,
             gni_scr, own, d2d, ici, myslab, slabs, w_ref, send_sems, recv_sems, local_sems):
        i = pl.program_id(0)
        rs_start, rs_finish = _ici_sum(dwi_ref, own, d2d, ici, send_sems, recv_sems, local_sems)
        slab_start, slab_finish = _slab_sum(myslab, slabs, send_sems, recv_sems, N_ICI_SUM_SEMS)

        @pl.when(i == 0)
        def _():
            gni_scr[...] = jnp.zeros_like(gni_scr)
            rs_start()
            w_load = pltpu.make_async_copy(w_hbm, w_ref, local_sems.at[1])
            w_load.start()
            w_load.wait()

        dh = jnp.dot(dp_ref[...], w_ref[...], preferred_element_type=F32)
        xv = x_ref[...]
        r = lax.rsqrt(jnp.mean(xv * xv, axis=-1, keepdims=True) + RMS_EPS)
        xn = xv * r
        u = dh * g_ref[...]
        gx_ref[...] = dx2_ref[...] + r * (u - xn * jnp.mean(u * xn, axis=-1, keepdims=True))
        gni_scr[...] += jnp.sum(dh * xn, axis=0, keepdims=True)

        @pl.when(i == steps - 1)
        def _():
            row = lax.broadcasted_iota(jnp.int32, (8, D_MODEL), 0)
            lane = lax.broadcasted_iota(jnp.int32, (8, D_MODEL), 1)
            slab = jnp.where(row == ROW_NORM_IN, gni_scr[...], jnp.where(row == ROW_NORM_FINAL, gnf_ref[...], gs_ref[...]))
            myslab[...] = jnp.where((row == ROW_SINKS) & (lane == LOSS_LANE), lp_ref[0:1, 0:1], slab)
            slab_start()
            gwin_ref[...] = rs_finish()
            gsum_ref[...] = slab_finish()

    const = lambda i: (0, 0)
    return pl.pallas_call(
        body, name="in_bwd", grid=(steps,),
        in_specs=[pl.BlockSpec((tm, D_PROJ), lambda i: (i, 0)), pl.BlockSpec(memory_space=pl.ANY),
                  pl.BlockSpec((tm, D_MODEL), lambda i: (i, 0)), pl.BlockSpec((tm, D_MODEL), lambda i: (i, 0)),
                  pl.BlockSpec((1, D_MODEL), const), pl.BlockSpec(memory_space=pl.ANY),
                  pl.BlockSpec((8, D_MODEL), const), pl.BlockSpec((1, D_MODEL), const), pl.BlockSpec((8, 128), const)],
        out_specs=(pl.BlockSpec((tm, D_MODEL), lambda i: (i, 0)), pl.BlockSpec((SHARD_IN, D_MODEL), const),
                   pl.BlockSpec((8, D_MODEL), const)),
        out_shape=(jax.ShapeDtypeStruct((SEQ, D_MODEL), F32), jax.ShapeDtypeStruct((SHARD_IN, D_MODEL), F32),
                   jax.ShapeDtypeStruct((8, D_MODEL), F32)),
        scratch_shapes=[pltpu.VMEM((1, D_MODEL), F32), pltpu.VMEM((SHARD_IN, D_MODEL), BF16),
                        pltpu.VMEM((SHARD_IN, D_MODEL), BF16), pltpu.VMEM((2, SHARD_IN, D_MODEL), BF16),
                        pltpu.VMEM((8, D_MODEL), F32), pltpu.VMEM((N_DEV, 8, D_MODEL), F32),
                        pltpu.VMEM((D_PROJ, D_MODEL), BF16),
                        pltpu.SemaphoreType.DMA((N_ICI_SUM_SEMS + 7,)), pltpu.SemaphoreType.DMA((N_ICI_SUM_SEMS + 7,)),
                        pltpu.SemaphoreType.DMA((2,))],
        compiler_params=_params(dimension_semantics=("arbitrary",)),
    )(dproj, w_full, x, dx2, norm_in, dw_in_chip, gslab, gnf, loss_part)


def _dw_rs(mixed, dx2b, dproj, h, table):
    tn_out, tn = 2 * SHARD_OUT, IN_PROJ_TILE
    out_steps, in_steps = D_MIX // tn_out, D_PROJ // tn
    steps = out_steps + in_steps
    out_order = (DG, NX, NY, OWN)

    def out_tile(i):
        chip = 2 * lax.axis_index("x") + lax.axis_index("y")
        return jnp.bitwise_xor(chip, (out_steps - 1) - jnp.minimum(i, out_steps - 1))

    def in_tile(table_ref, i):
        return _dw_entry(table_ref, jnp.maximum(i - out_steps, 0))

    def body(table_ref, mx_ref, dxb_ref, a_ref, h_hbm, chip_ref, gwo_ref, dwo, dwt, d2d_in, via, own, d2d, ici, via_out,
             b_ref, send_sems, recv_sems, local_sems):
        i = pl.program_id(0)
        h_load = pltpu.make_async_copy(h_hbm, b_ref, local_sems.at[8])

        @pl.when(i == 0)
        def _():
            h_load.start()

        @pl.when(i == out_steps)
        def _():
            h_load.wait()

        rs_start, rs_forward, rs_finish = _shard_sum(dwo, own, d2d, ici, via_out, send_sems, recv_sems, local_sems)
        before_tile, after_tiles, chip_finish = _chip_sum(
            dwt, d2d_in, via, chip_ref, lambda k: _dw_entry(table_ref, in_steps + k), send_sems, recv_sems, local_sems,
            N_SHARD_SUM_SEMS, 4)

        for j, k in enumerate(out_order):
            @pl.when(i == j + 1)
            def _():
                rs_start(k)

            if k != OWN:
                @pl.when(i == j + (2 if k == DG else 3))
                def _():
                    rs_forward(k)

        @pl.when(i < out_steps)
        def _():
            tile = lax.dot_general(mx_ref[...], dxb_ref[...], _TN, preferred_element_type=F32).astype(BF16)
            for core in range(2):
                dwo[2 * out_tile(i) + core] = tile[SHARD_OUT * core:SHARD_OUT * (core + 1), :]

        @pl.when(i >= out_steps)
        def _():
            before_tile(i - out_steps)
            tile = lax.dot_general(a_ref[...], b_ref[...], _TN, preferred_element_type=F32).astype(BF16)
            dwt[pl.ds(pl.multiple_of(in_tile(table_ref, i) * tn, tn), tn), :] = tile

        @pl.when(i == steps - 2)
        def _():
            gwo_ref[...] = rs_finish()

        @pl.when(i == steps - 1)
        def _():
            after_tiles()
            chip_finish()

    vmem = pl.BlockSpec(memory_space=pltpu.VMEM)
    grid_spec = pltpu.PrefetchScalarGridSpec(
        num_scalar_prefetch=1, grid=(steps,),
        in_specs=[pl.BlockSpec((SEQ, tn_out), lambda i, table_ref: (0, out_tile(i))), vmem,
                  pl.BlockSpec((SEQ, tn), lambda i, table_ref: (0, in_tile(table_ref, i))),
                  pl.BlockSpec(memory_space=pl.ANY)],
        out_specs=(pl.BlockSpec(memory_space=pl.ANY), pl.BlockSpec((SHARD_OUT, D_MODEL), lambda i, table_ref: (0, 0))),
        scratch_shapes=[pltpu.VMEM((N_DEV, SHARD_OUT, D_MODEL), BF16),
                        pltpu.VMEM((D_PROJ, D_MODEL), BF16), pltpu.VMEM((3, SHARD_IN, D_MODEL), BF16),
                        pltpu.VMEM((2, HALF_IN, D_MODEL), BF16),
                        *_shard_sum_scratch(SHARD_OUT), pltpu.VMEM((SEQ, D_MODEL), BF16),
                        pltpu.SemaphoreType.DMA((N_SHARD_SUM_SEMS + N_CHIP_SUM_SEMS,)),
                        pltpu.SemaphoreType.DMA((N_SHARD_SUM_SEMS + N_CHIP_SUM_SEMS,)),
                        pltpu.SemaphoreType.DMA((9,))])
    return pl.pallas_call(
        body, name="dw", grid_spec=grid_spec,
        out_shape=(jax.ShapeDtypeStruct((4, SHARD_IN, D_MODEL), BF16), jax.ShapeDtypeStruct((SHARD_OUT, D_MODEL), F32)),
        compiler_params=_params(dimension_semantics=("arbitrary",)),
    )(table, mixed, dx2b, dproj, h)


def _adam_all(big_in, big_out, gsum, small, grad_x):
    n_chunks = 4
    n_big = 8

    def body(*refs):
        ins, outs = refs[:n_big + 1 + 18 + 1], refs[n_big + 1 + 18 + 1:n_big + 1 + 18 + 1 + 34]
        in_bufs, out_bufs, gx_buf = refs[-n_big - 6 - 4:-6 - 4], refs[-6 - 4:-4], refs[-4]
        in_sems, out_sems, gx_sems = refs[-3:]

        def gx_rows(j):
            return pl.ds(j * (SEQ // n_chunks), SEQ // n_chunks)

        def gx_load(j):
            return pltpu.make_async_copy(ins[27].at[gx_rows(j), :], gx_buf.at[gx_rows(j), :], gx_sems.at[j])

        def gx_store(j):
            return pltpu.make_async_copy(gx_buf.at[gx_rows(j), :], outs[33].at[gx_rows(j), :], gx_sems.at[n_chunks + j])

        def rows(a, j):
            tr = ins[a].shape[0] // n_chunks
            return pl.ds(j * tr, tr)

        def load(a, j):
            return pltpu.make_async_copy(ins[a].at[rows(a, j), :], in_bufs[a].at[rows(a, j), :], in_sems.at[a * n_chunks + j])

        def store(a, j):
            b, kind = divmod(a, 4)
            src = in_bufs[4 * b + 1] if kind == 0 else out_bufs[3 * b + kind - 1]
            return pltpu.make_async_copy(src.at[rows(a, j), :], outs[a].at[rows(a, j), :], out_sems.at[a * n_chunks + j])

        for j in range(n_chunks):
            for a in range(n_big):
                load(a, j).start()
            gx_load(j).start()

        def small_weights():
            gsum = ins[8][...]
            idx = _slot(lax.axis_index("x"), lax.axis_index("y"), lax.axis_index("c"))
            cg = jnp.zeros((3, SHARD_CONV), F32)
            for d in range(N_DEV):
                cg = jnp.where(idx == d, gsum[ROW_CONV0:ROW_CONV0 + 3, d * SHARD_CONV:(d + 1) * SHARD_CONV], cg)
            grads = (gsum[ROW_NORM_IN:ROW_NORM_IN + 1], gsum[ROW_SINKS:ROW_SINKS + 1, 0:N_Q_HEADS],
                     gsum[ROW_NORM_CONV:ROW_NORM_CONV + 1], gsum[ROW_NORM_ATTN:ROW_NORM_ATTN + 1],
                     gsum[ROW_NORM_FINAL:ROW_NORM_FINAL + 1], cg)
            for s, g in enumerate(grads):
                at = (slice(None), 0, slice(None)) if s == 5 else (slice(None), slice(None))
                w_ref, m_ref, v_ref = ins[9 + 3 * s:12 + 3 * s]
                delta, mn, vn = _adamw(w_ref[at], g, m_ref[at], v_ref[at])
                for ref, val in zip(outs[8 + 4 * s:12 + 4 * s], (g, delta, mn, vn)):
                    ref[at] = val
            outs[32][...] = gsum[ROW_SINKS:ROW_SINKS + 1, LOSS_LANE:LOSS_LANE + 1]

        small_weights()
        for j in range(n_chunks):
            for b in range(2):
                for a in range(4 * b, 4 * b + 4):
                    load(a, j).wait()
                w_buf, g_buf, m_buf, v_buf = in_bufs[4 * b:4 * b + 4]
                r = rows(4 * b, j)
                results = _adamw(w_buf[r, :], g_buf[r, :], m_buf[r, :], v_buf[r, :])
                for buf, val in zip(out_bufs[3 * b:3 * b + 3], results):
                    buf[r, :] = val
                for a in range(4 * b, 4 * b + 4):
                    store(a, j).start()
            gx_load(j).wait()
            gx_store(j).start()
        for j in range(n_chunks):
            for a in range(n_big):
                store(a, j).wait()
            gx_store(j).wait()

    vmem, hbm = pl.BlockSpec(memory_space=pltpu.VMEM), pl.BlockSpec(memory_space=pl.ANY)
    small_shapes = [a.shape for a in small[::3]]
    big_shapes = [(SHARD_IN, D_MODEL)] * 4 + [(SHARD_OUT, D_MODEL)] * 4
    out_shape = ([jax.ShapeDtypeStruct(s, F32) for s in big_shapes]
                 + [jax.ShapeDtypeStruct(s, F32) for s in small_shapes for _ in range(4)]
                 + [jax.ShapeDtypeStruct((1, 1), F32), jax.ShapeDtypeStruct((SEQ, D_MODEL), F32)])
    outs = pl.pallas_call(
        body, name="adam", in_specs=[hbm] * n_big + [vmem] * (1 + len(small)) + [hbm],
        out_specs=tuple([hbm] * n_big + [vmem] * (4 * len(small_shapes) + 1) + [hbm]), out_shape=tuple(out_shape),
        scratch_shapes=[pltpu.VMEM(s, F32) for s in big_shapes]
                       + [pltpu.VMEM(s, F32) for s in [(SHARD_IN, D_MODEL)] * 3 + [(SHARD_OUT, D_MODEL)] * 3]
                       + [pltpu.VMEM((SEQ, D_MODEL), F32),
                          pltpu.SemaphoreType.DMA((n_big * n_chunks,)), pltpu.SemaphoreType.DMA((n_big * n_chunks,)),
                          pltpu.SemaphoreType.DMA((2 * n_chunks,))],
        compiler_params=_params(),
    )(*big_in, *big_out, gsum, *small, grad_x)
    return outs[0:4], outs[4:8], [outs[8 + 4 * s:12 + 4 * s] for s in range(6)], outs[32], outs[33]


def _rows_first(a):
    return jnp.transpose(a, (1, 0, 2))


def kernel(x, norm_in, w_in, conv_w, attn_sinks, norm_conv_out, norm_attn_out, w_out, norm_final, loss_target, m_norm_in, m_w_in, m_conv_w, m_attn_sinks, m_norm_conv_out, m_norm_attn_out, m_w_out, m_norm_final, v_norm_in, v_w_in, v_conv_w, v_attn_sinks, v_norm_conv_out, v_norm_attn_out, v_w_out, v_norm_final):
    x2d = x.reshape(SEQ, D_MODEL)
    target = loss_target.reshape(SEQ, D_MODEL)
    nf = norm_final.reshape(1, D_MODEL)

    w_in_t, m_w_in_t, v_w_in_t = w_in[0].T, m_w_in[0].T, v_w_in[0].T
    tiles = jnp.asarray(TILE_ORDER, jnp.int32).reshape(-1)
    w_in_full, h, proj, g_out, conv_full = _gather_in_proj(x2d, norm_in, w_in_t, w_out[0], _rows_first(conv_w), tiles)
    sinks = attn_sinks.reshape(N_Q_HEADS)

    mixed, attn, probs, shares = _mix_fwd(proj, conv_full, sinks, norm_conv_out, norm_attn_out)
    dx2, dx2b, dmixed, gnf, loss_part = _out_proj_loss(mixed, x2d, target, g_out.reshape(D_MIX, D_MODEL), nf)
    dproj, gslab = _mix_bwd(proj, dmixed, attn, probs, shares, conv_full, norm_conv_out, norm_attn_out)
    dw_in_chip, g_w_out = _dw_rs(mixed, dx2b, dproj, h, jnp.asarray(DW_TABLE, jnp.int32).reshape(-1))
    grad_x, g_w_in, gsum = _in_bwd_rs(dproj, w_in_full, x2d, dx2, norm_in, dw_in_chip, gslab, gnf, loss_part)

    small = (norm_in, m_norm_in, v_norm_in, attn_sinks, m_attn_sinks, v_attn_sinks,
             norm_conv_out, m_norm_conv_out, v_norm_conv_out, norm_attn_out, m_norm_attn_out, v_norm_attn_out,
             nf, m_norm_final.reshape(1, D_MODEL), v_norm_final.reshape(1, D_MODEL),
             _rows_first(conv_w), _rows_first(m_conv_w), _rows_first(v_conv_w))
    big_in, big_out, (s_ni, s_sk, s_nc, s_na, s_nf, s_cv), loss, grad_x = _adam_all(
        (w_in_t, g_w_in, m_w_in_t, v_w_in_t), (w_out[0], g_w_out, m_w_out[0], v_w_out[0]), gsum, small, grad_x)

    def leaves(k):
        return (s_ni[k], big_in[k].T[None], jnp.transpose(s_cv[k], (1, 0, 2)), s_sk[k], s_nc[k], s_na[k], big_out[k][None],
                s_nf[k].reshape(D_MODEL))

    return (loss.reshape(()), grad_x.reshape(1, SEQ, D_MODEL), *leaves(0), *leaves(1), *leaves(2), *leaves(3))
```

```python
import jax
import jax.numpy as jnp
from jax import lax
from jax.experimental import pallas as pl
from jax.experimental.pallas import tpu as pltpu

F32 = jnp.float32
BF16 = jnp.bfloat16
MESH = pl.DeviceIdType.MESH

N_DEV = 8
SEQ = 2048
D_MODEL = 1024
D_CONV = 1024
D_ATTN = 1024
D_KV = 128
HEAD_DIM = 64
N_Q_HEADS = 16
N_PAIRS = N_Q_HEADS // 2
PAIRS_PER_KV = N_PAIRS // 2
D_MIX = D_CONV + D_ATTN
D_PROJ = 6400
SHARD_IN = D_PROJ // N_DEV
SHARD_OUT = D_MIX // N_DEV
SHARD_CONV = D_CONV // N_DEV
OFF_CB, OFF_CC, OFF_CU, OFF_GC, OFF_Q, OFF_K, OFF_V, OFF_GA = 0, 1024, 2048, 3072, 4096, 5120, 5248, 5376
BLOCK = 128
N_BLOCKS = SEQ // BLOCK
HALO = 8
CHUNK = 16
N_CHUNKS = BLOCK // CHUNK
RMS_EPS = 1e-5
NEG = -1e30
SCALE = HEAD_DIM ** -0.5
SLOPES = tuple(2.0 ** (-8.0 * (h + 1) / N_Q_HEADS) for h in range(N_Q_HEADS))

ADAM_LR = 0.001
ADAM_B1 = 0.9
ADAM_B2 = 0.999
ADAM_EPS = 1e-08
ADAM_WD = 0.01
ADAM_STEP = 10

ROW_NORM_IN, ROW_NORM_CONV, ROW_NORM_ATTN, ROW_NORM_FINAL, ROW_CONV0, ROW_SINKS = 0, 1, 2, 3, 4, 7
LOSS_LANE = N_Q_HEADS
ACC_NORM_CONV, ACC_NORM_ATTN, ACC_CONV0, N_ACC = 0, 1, 2, 5

VMEM_LIMIT = 56 * 1024 * 1024

_NT = (((1,), (1,)), ((), ()))
_TN = (((0,), (0,)), ((), ()))


def _params(**kw):
    return pltpu.CompilerParams(vmem_limit_bytes=VMEM_LIMIT, **kw)


def _adamw(w, g, m, v):
    m = ADAM_B1 * m + (1.0 - ADAM_B1) * g
    v = ADAM_B2 * v + (1.0 - ADAM_B2) * (g * g)
    m_hat = m / (1.0 - ADAM_B1 ** ADAM_STEP)
    v_hat = v / (1.0 - ADAM_B2 ** ADAM_STEP)
    delta = -ADAM_LR * (m_hat / (jnp.sqrt(v_hat) + ADAM_EPS) + ADAM_WD * w)
    return delta, m, v


def _sigmoid(t):
    return 1.0 / (1.0 + jnp.exp(-t))


def _slot(px, py, pc):
    return 4 * px + 2 * py + pc


OWN, NX, NY, DG = range(4)
HALF_IN = SHARD_IN // 2
N_GATHER_KINDS = 13
W_OUT_KINDS = N_GATHER_KINDS + 7


IN_PROJ_TILE = 640
TILE_ORDER = ((0, 1, 2, 3, 4, 5, 6, 7, 8, 9), (3, 4, 0, 1, 2, 8, 9, 5, 6, 7),
              (5, 6, 0, 1, 7, 8, 9, 2, 3, 4), (8, 9, 3, 4, 5, 6, 7, 0, 1, 2))
TILES_OWN, TILES_NEIGHBOURS = 2, 7


def _tile(table_ref, p):
    chip = 2 * lax.axis_index("x") + lax.axis_index("y")
    return table_ref[chip * len(TILE_ORDER[0]) + p]


DW_TILE_ORDER = tuple(tuple(reversed(row)) for row in TILE_ORDER)


def _tiles_until_complete(chip, owner):
    lo, hi = owner * 2 * SHARD_IN, (owner + 1) * 2 * SHARD_IN
    touching = [t for t in range(len(TILE_ORDER[0])) if t * IN_PROJ_TILE < hi and (t + 1) * IN_PROJ_TILE > lo]
    return 1 + max(DW_TILE_ORDER[chip].index(t) for t in touching)


DW_TABLE = tuple(DW_TILE_ORDER[chip] + tuple(_tiles_until_complete(chip, chip ^ flip) for flip in (0, 2, 1, 3))
                 for chip in range(4))


def _dw_entry(table_ref, p):
    chip = 2 * lax.axis_index("x") + lax.axis_index("y")
    return table_ref[chip * len(DW_TABLE[0]) + p]


def _gather_in_proj(x, norm_in, w_in_sh, w_out_sh, conv_sh, tiles):
    tn = IN_PROJ_TILE
    steps = D_PROJ // tn
    tm = 256

    def body(tiles_ref, x_hbm, g_ref, win_ref, wout_ref, cv_ref, wt_ref, h_ref, proj_ref, gout_ref, conv_ref,
             gin_ref, gcv_ref, wob_ref, x_ref, send_sems, recv_sems, local_sems):
        p = pl.program_id(0)
        local_sem = local_sems.at[0]
        x, y, c = lax.axis_index("x"), lax.axis_index("y"), lax.axis_index("c")
        me, sibling = (x, y, c), (x, y, 1 - c)
        nx, ny, dg = (1 - x, y, c), (x, 1 - y, c), (1 - x, 1 - y, c)

        def other(dev):
            return (dev[0], dev[1], 1 - dev[2])

        def shard(dev):
            return gin_ref.at[pl.ds(pl.multiple_of(_slot(*dev) * SHARD_IN, 16), SHARD_IN), :]

        def half(dev, h):
            return gin_ref.at[pl.ds(pl.multiple_of(_slot(*dev) * SHARD_IN + h * HALF_IN, 16), HALF_IN), :]

        def rc(ref, k, to):
            return pltpu.make_async_remote_copy(src_ref=ref, dst_ref=ref, send_sem=send_sems.at[k],
                                                recv_sem=recv_sems.at[k], device_id=to, device_id_type=MESH)

        def cv(k, dev, to):
            s = _slot(*dev)
            return pltpu.make_async_remote_copy(src_ref=gcv_ref.at[s], dst_ref=gcv_ref.at[s],
                                                send_sem=send_sems.at[N_GATHER_KINDS + k],
                                                recv_sem=recv_sems.at[N_GATHER_KINDS + k], device_id=to, device_id_type=MESH)

        def own_copies():
            return [rc(shard(me), 0, sibling),
                    rc(half(me, 0), 1, nx), rc(half(me, 1), 2, nx),
                    rc(half(me, 1), 4, ny), rc(half(me, 0), 3, ny),
                    cv(0, me, sibling)] + [cv(1 + j, me, peer) for j, peer in enumerate((nx, ny, dg))]

        def pass_on(dev, h, k_in, k_ici, k_d2d, half=half, base=0):
            rc(half(dev, h), base + k_in, me).wait_recv()
            if k_ici is not None:
                rc(half(dev, h), base + k_ici, ny if dev is nx else nx).start()
            rc(half(dev, h), base + k_d2d, sibling).start()

        def out_half(dev, h):
            return gout_ref.at[_slot(*dev), pl.ds(h * (SHARD_OUT // 2), SHARD_OUT // 2), :]

        def own_out_copies():
            src = lambda h: wob_ref.at[pl.ds(h * (SHARD_OUT // 2), SHARD_OUT // 2), :]

            def send(ref, dst, k, to):
                return pltpu.make_async_remote_copy(src_ref=ref, dst_ref=dst, send_sem=send_sems.at[W_OUT_KINDS + k],
                                                    recv_sem=recv_sems.at[W_OUT_KINDS + k], device_id=to, device_id_type=MESH)

            return [send(wob_ref, gout_ref.at[_slot(*me)], 0, sibling),
                    send(src(0), out_half(me, 0), 1, nx), send(src(1), out_half(me, 1), 2, nx),
                    send(src(1), out_half(me, 1), 4, ny), send(src(0), out_half(me, 0), 3, ny)]

        def own_out_local():
            return pltpu.make_async_copy(wob_ref, gout_ref.at[_slot(*me)], local_sems.at[1])

        @pl.when(p == 0)
        def _():
            gin_ref[pl.ds(pl.multiple_of(_slot(*me) * SHARD_IN, 16), SHARD_IN), :] = win_ref[...].astype(BF16)
            gcv_ref[_slot(*me)] = jnp.zeros((8, SHARD_CONV), F32)
            gcv_ref[_slot(*me), 0:3, :] = cv_ref[:, 0, :]
            for cp in own_copies():
                cp.start()
            wob_ref[...] = wout_ref[...].astype(BF16)
            x_load = pltpu.make_async_copy(x_hbm, x_ref, local_sems.at[2])
            x_load.start()
            x_load.wait()
            for t in range(SEQ // tm):
                xv = x_ref[tm * t:tm * (t + 1), :]
                r = lax.rsqrt(jnp.mean(xv * xv, axis=-1, keepdims=True) + RMS_EPS)
                h_ref[tm * t:tm * (t + 1), :] = (xv * r * g_ref[...]).astype(BF16)
            rc(shard(sibling), 0, me).wait_recv()

        @pl.when(p == TILES_OWN)
        def _():
            for args in ((nx, 0, 1, 5, 7), (ny, 1, 4, 6, 10), (nx, 1, 2, None, 8), (ny, 0, 3, None, 9)):
                pass_on(*args)
            for j, peer in enumerate((nx, ny, dg)):
                cv(1 + j, peer, me).wait_recv()
                cv(4 + j, peer, sibling).start()
            for (dev, h), k in (((nx, 0), 7), ((nx, 1), 8), ((ny, 0), 9), ((ny, 1), 10)):
                rc(half(other(dev), h), k, me).wait_recv()
            own_out_local().start()
            for cp in own_out_copies():
                cp.start()

        @pl.when(p == TILES_NEIGHBOURS - 1)
        def _():
            pass_on(dg, 0, 5, None, 11)
            pass_on(dg, 1, 6, None, 12)

        @pl.when(p == TILES_NEIGHBOURS)
        def _():
            for (dev, h), k in (((dg, 0), 11), ((dg, 1), 12)):
                rc(half(other(dev), h), k, me).wait_recv()
            pltpu.make_async_copy(gin_ref, wt_ref, local_sem).start()

        @pl.when(p == steps - 2)
        def _():
            for args in ((nx, 0, 1, 5, 7), (ny, 1, 4, 6, 10), (nx, 1, 2, None, 8), (ny, 0, 3, None, 9)):
                pass_on(*args, half=out_half, base=W_OUT_KINDS)

        w = gin_ref[pl.ds(pl.multiple_of(_tile(tiles_ref, p) * tn, tn), tn), :]
        proj_ref[...] = lax.dot_general(h_ref[...], w, _NT, preferred_element_type=F32)

        @pl.when(p == steps - 1)
        def _():
            cv(0, sibling, me).wait_recv()
            for j, peer in enumerate((nx, ny, dg)):
                cv(4 + j, other(peer), me).wait_recv()
            for d in range(N_DEV):
                conv_ref[:, d * SHARD_CONV:(d + 1) * SHARD_CONV] = gcv_ref[d]
            relayed = [rc(half(nx, 0), 5, ny), rc(half(ny, 1), 6, nx)]
            relayed += [rc(half(dev, h), k, sibling) for (dev, h), k in
                        (((nx, 0), 7), ((nx, 1), 8), ((ny, 0), 9), ((ny, 1), 10), ((dg, 0), 11), ((dg, 1), 12))]
            relayed += [cv(4 + j, peer, sibling) for j, peer in enumerate((nx, ny, dg))]
            for cp in own_copies() + relayed:
                cp.wait_send()
            pltpu.make_async_copy(gin_ref, wt_ref, local_sem).wait()
            pass_on(dg, 0, 5, None, 11, half=out_half, base=W_OUT_KINDS)
            pass_on(dg, 1, 6, None, 12, half=out_half, base=W_OUT_KINDS)
            rc(gout_ref.at[_slot(*sibling)], W_OUT_KINDS, me).wait_recv()
            out_relayed = [rc(out_half(nx, 0), W_OUT_KINDS + 5, ny), rc(out_half(ny, 1), W_OUT_KINDS + 6, nx)]
            for (dev, h), k in (((nx, 0), 7), ((nx, 1), 8), ((ny, 0), 9), ((ny, 1), 10), ((dg, 0), 11), ((dg, 1), 12)):
                rc(out_half(other(dev), h), W_OUT_KINDS + k, me).wait_recv()
                out_relayed.append(rc(out_half(dev, h), W_OUT_KINDS + k, sibling))
            for cp in own_out_copies() + out_relayed:
                cp.wait_send()
            own_out_local().wait()

    vmem = pl.BlockSpec(memory_space=pltpu.VMEM)
    grid_spec = pltpu.PrefetchScalarGridSpec(
        num_scalar_prefetch=1, grid=(steps,),
        in_specs=[pl.BlockSpec(memory_space=pl.ANY), vmem, vmem, vmem, vmem],
        out_specs=(pl.BlockSpec(memory_space=pl.ANY), vmem,
                   pl.BlockSpec((SEQ, tn), lambda p, tiles_ref: (0, _tile(tiles_ref, p))),
                   pl.BlockSpec(memory_space=pl.ANY), vmem),
        scratch_shapes=[pltpu.VMEM((D_PROJ, D_MODEL), BF16), pltpu.VMEM((N_DEV, 8, SHARD_CONV), F32),
                        pltpu.VMEM((SHARD_OUT, D_MODEL), BF16), pltpu.VMEM((SEQ, D_MODEL), F32),
                        pltpu.SemaphoreType.DMA((W_OUT_KINDS + N_GATHER_KINDS,)),
                        pltpu.SemaphoreType.DMA((W_OUT_KINDS + N_GATHER_KINDS,)),
                        pltpu.SemaphoreType.DMA((3,))])
    return pl.pallas_call(
        body, name="gather_in_proj", grid_spec=grid_spec,
        out_shape=(jax.ShapeDtypeStruct((D_PROJ, D_MODEL), BF16), jax.ShapeDtypeStruct((SEQ, D_MODEL), BF16),
                   jax.ShapeDtypeStruct((SEQ, D_PROJ), F32), jax.ShapeDtypeStruct((N_DEV, SHARD_OUT, D_MODEL), BF16),
                   jax.ShapeDtypeStruct((8, D_CONV), F32)),
        compiler_params=_params(dimension_semantics=("arbitrary",)),
    )(tiles, x, norm_in, w_in_sh, w_out_sh, conv_sh)


def _shard_sum(src, own, d2d, ici, send_sems, recv_sems, local_sems, base=0):
    x, y, c = lax.axis_index("x"), lax.axis_index("y"), lax.axis_index("c")
    sibling = (x, y, 1 - c)
    chips = [(x, y), (1 - x, y), (x, 1 - y), (1 - x, 1 - y)]

    def rcopy(s, d, k, to):
        return pltpu.make_async_remote_copy(src_ref=s, dst_ref=d, send_sem=send_sems.at[base + k],
                                            recv_sem=recv_sems.at[base + k], device_id=to, device_id_type=MESH)

    def mine(k):
        return pltpu.make_async_copy(src.at[_slot(*chips[k], c)], own.at[k], local_sems.at[k])

    def to_sibling(k):
        return rcopy(src.at[_slot(*chips[k], 1 - c)], d2d.at[k], k, sibling)

    def to_chip(k):
        return rcopy(own.at[k], ici.at[k - 1], 3 + k, (*chips[k], c))

    def start(k):
        mine(k).start()
        to_sibling(k).start()

    def forward(k):
        mine(k).wait()
        to_sibling(k).wait_recv()
        own[k] = (own[k].astype(F32) + d2d[k].astype(F32)).astype(BF16)
        to_chip(k).start()

    def finish():
        mine(0).wait()
        to_sibling(0).wait_recv()
        acc = own[0].astype(F32) + d2d[0].astype(F32)
        for k in range(1, 4):
            to_chip(k).wait_recv()
            acc = acc + ici[k - 1].astype(F32)
        for k in range(4):
            to_sibling(k).wait_send()
        for k in range(1, 4):
            to_chip(k).wait_send()
        return acc

    return start, forward, finish


def _shard_sum_scratch(rows):
    return [pltpu.VMEM((4, rows, D_MODEL), BF16), pltpu.VMEM((4, rows, D_MODEL), BF16),
            pltpu.VMEM((3, rows, D_MODEL), BF16)]


N_SHARD_SUM_SEMS = 7


N_CHIP_SUM_SEMS = 5


def _chip_sum(dwt, d2d, via, out_hbm, tiles_until, send_sems, recv_sems, local_sems, base, local_base):
    x, y, c = lax.axis_index("x"), lax.axis_index("y"), lax.axis_index("c")
    sibling, nx, ny = (x, y, 1 - c), (1 - x, y, c), (x, 1 - y, c)
    chips = [(x, y), (1 - x, y), (x, 1 - y), (1 - x, 1 - y)]

    def shard(s):
        return dwt.at[pl.ds(pl.multiple_of(s * SHARD_IN, 16), SHARD_IN), :]

    def half(ref, h):
        return ref.at[pl.ds(h * HALF_IN, HALF_IN), :]

    def rc(s, d, k, to):
        return pltpu.make_async_remote_copy(src_ref=s, dst_ref=d, send_sem=send_sems.at[base + k],
                                            recv_sem=recv_sems.at[base + k], device_id=to, device_id_type=MESH)

    def to_sibling(k):
        return rc(shard(_slot(*chips[k], 1 - c)), d2d.at[k - 1], k - 1, sibling)

    for_dg = (lambda: rc(half(d2d.at[DG - 1], 0), via.at[0], 3, nx), lambda: rc(half(d2d.at[DG - 1], 1), via.at[1], 4, ny))

    def save(k):
        return pltpu.make_async_copy(d2d.at[k - 1], out_hbm.at[k], local_sems.at[local_base + k])

    own_saves = (lambda: pltpu.make_async_copy(shard(_slot(x, y, c)), out_hbm.at[OWN], local_sems.at[local_base]),
                 lambda: pltpu.make_async_copy(shard(_slot(x, y, 1 - c)), out_hbm.at[3], local_sems.at[local_base + 3]))

    def before_tile(n):
        for k in (NX, NY, DG):
            @pl.when(tiles_until(k) == n)
            def _():
                to_sibling(k).start()

            @pl.when(tiles_until(k) + 1 == n)
            def _():
                to_sibling(k).wait_recv()
                d2d[k - 1] = (shard(_slot(*chips[k], c))[...].astype(F32) + d2d[k - 1].astype(F32)).astype(BF16)
                if k == DG:
                    for cp in for_dg:
                        cp().start()

    def after_tiles():
        for cp in own_saves:
            cp().start()

    def finish():
        for k, h in ((NY, 0), (NX, 1)):
            for_dg[h]().wait_recv()
            rows = pl.ds(h * HALF_IN, HALF_IN)
            d2d[k - 1, rows, :] = (d2d[k - 1, rows, :].astype(F32) + via[h].astype(F32)).astype(BF16)
            save(k).start()
        for cp in own_saves + (lambda: save(NX), lambda: save(NY)):
            cp().wait()
        for cp in (lambda: to_sibling(NX), lambda: to_sibling(NY), lambda: to_sibling(DG)) + for_dg:
            cp().wait_send()

    return before_tile, after_tiles, finish


N_ICI_SUM_SEMS = 3


def _ici_sum(src, own, d2d, ici, send_sems, recv_sems, local_sems, base=0):
    x, y, c = lax.axis_index("x"), lax.axis_index("y"), lax.axis_index("c")

    def rc(s, d, k, to):
        return pltpu.make_async_remote_copy(src_ref=s, dst_ref=d, send_sem=send_sems.at[base + k],
                                            recv_sem=recv_sems.at[base + k], device_id=to, device_id_type=MESH)

    copies = (lambda: rc(src.at[NX], ici.at[0], 0, (1 - x, y, c)), lambda: rc(src.at[NY], ici.at[1], 1, (x, 1 - y, c)),
              lambda: rc(src.at[3], d2d, 2, (x, y, 1 - c)))
    mine = lambda: pltpu.make_async_copy(src.at[OWN], own, local_sems.at[0])

    def start():
        for cp in copies + (mine,):
            cp().start()

    def finish():
        mine().wait()
        for cp in copies:
            cp().wait_recv()
        acc = own[...].astype(F32) + d2d[...].astype(F32) + ici[0].astype(F32) + ici[1].astype(F32)
        for cp in copies:
            cp().wait_send()
        return acc

    return start, finish


def _slab_sum(myslab, slabs, send_sems, recv_sems, base):
    x, y, c = lax.axis_index("x"), lax.axis_index("y"), lax.axis_index("c")
    me = _slot(x, y, c)
    peers = [(x, y, 1 - c), (1 - x, y, c), (x, 1 - y, c), (1 - x, 1 - y, c),
             (1 - x, y, 1 - c), (x, 1 - y, 1 - c), (1 - x, 1 - y, 1 - c)]

    def cp(k):
        return pltpu.make_async_remote_copy(src_ref=myslab, dst_ref=slabs.at[me], send_sem=send_sems.at[base + k],
                                            recv_sem=recv_sems.at[base + k], device_id=peers[k], device_id_type=MESH)

    def start():
        slabs[me] = myslab[...]
        for k in range(7):
            cp(k).start()

    def finish():
        for k in range(7):
            cp(k).wait_recv()
        total = slabs[0]
        for d in range(1, N_DEV):
            total = total + slabs[d]
        for k in range(7):
            cp(k).wait_send()
        return total

    return start, finish


def _chunk_rows(r):
    return slice(r * CHUNK, (r + 1) * CHUNK)


def _conv_halo(cch_ref, cuh_ref, n):
    zh = jnp.where(n > 0, cch_ref[...] * cuh_ref[...], 0.0)
    return jnp.concatenate([zh] * (CHUNK // HALO), axis=0)


def _conv_chunk(pj_ref, zhalo, cw, r):
    rows = _chunk_rows(r)
    cc = pj_ref[rows, OFF_CC:OFF_CC + D_CONV]
    cu = pj_ref[rows, OFF_CU:OFF_CU + D_CONV]
    z = cc * cu
    before = _chunk_rows(r - 1)
    zprev = pj_ref[before, OFF_CC:OFF_CC + D_CONV] * pj_ref[before, OFF_CU:OFF_CU + D_CONV] if r > 0 else zhalo
    row = lax.broadcasted_iota(jnp.int32, (CHUNK, D_CONV), 0)
    z1 = jnp.where(row < 1, pltpu.roll(zprev, 1, 0), pltpu.roll(z, 1, 0))
    z2 = jnp.where(row < 2, pltpu.roll(zprev, 2, 0), pltpu.roll(z, 2, 0))
    co = cw[0] * z2 + cw[1] * z1 + cw[2] * z
    return cc, cu, z, z1, z2, co


def _gated_norm(a, gain, t):
    r = lax.rsqrt(jnp.mean(a * a, axis=-1, keepdims=True) + RMS_EPS)
    return a * r * gain * (t * _sigmoid(t))


def _kv_bands(pj, kvp_ref):
    lane = lax.broadcasted_iota(jnp.int32, (2 * BLOCK, D_KV), 1)
    lo = lane < HEAD_DIM

    def bands(prev, cur):
        b = jnp.concatenate([prev, cur], axis=0)
        br = pltpu.roll(b, HEAD_DIM, 1)
        zero = jnp.zeros_like(b)
        return ((jnp.where(lo, b, zero).astype(BF16), jnp.where(lo, zero, br).astype(BF16)),
                (jnp.where(lo, br, zero).astype(BF16), jnp.where(lo, zero, b).astype(BF16)))

    ks = bands(kvp_ref[:, 0:D_KV], pj[:, OFF_K:OFF_K + D_KV])
    vs = bands(kvp_ref[:, D_KV:2 * D_KV], pj[:, OFF_V:OFF_V + D_KV])
    return ks, vs


STACK = PAIRS_PER_KV * BLOCK


def _head(j, i, e):
    return 2 * (PAIRS_PER_KV * j + i) + e


def _pair_cols(j, i, off):
    p = PAIRS_PER_KV * j + i
    return slice(off + 128 * p, off + 128 * (p + 1))


def _fill_attn_bias(bias_scr, first_block):
    qi = lax.broadcasted_iota(jnp.int32, (BLOCK, 2 * BLOCK), 0)
    kj = lax.broadcasted_iota(jnp.int32, (BLOCK, 2 * BLOCK), 1)
    dist = BLOCK + qi - kj
    valid = (dist >= 0) & (dist < BLOCK)
    if first_block:
        valid = valid & (kj >= BLOCK)
    distf = dist.astype(F32)
    for j in range(2):
        for e in range(2):
            for i in range(PAIRS_PER_KV):
                bias_scr[2 * j + e, BLOCK * i:BLOCK * (i + 1), :] = jnp.where(valid, -SLOPES[_head(j, i, e)] * distf, NEG)


def _q_stack(pj, j):
    return jnp.concatenate([(pj[:, _pair_cols(j, i, OFF_Q)] * SCALE).astype(BF16) for i in range(PAIRS_PER_KV)], axis=0)


def _attn_probs(q_stack, kband, bias_ref, sinks):
    s = lax.dot_general(q_stack, kband, _NT, preferred_element_type=F32)
    ones = jnp.ones((128, 128), BF16)
    probs, shares = [], []
    for i, sink in enumerate(sinks):
        rows = slice(BLOCK * i, BLOCK * (i + 1))
        t = s[rows, :] + bias_ref[rows, :]
        m = jnp.broadcast_to(jnp.max(t, axis=-1, keepdims=True), (BLOCK, 128))
        m = jnp.maximum(m, sink)
        p = [jnp.exp(t[:, :128] - m), jnp.exp(t[:, 128:] - m)]
        es = jnp.exp(sink - m)
        total = (jnp.dot(p[0].astype(BF16), ones, preferred_element_type=F32)
                 + jnp.dot(p[1].astype(BF16), ones, preferred_element_type=F32))
        inv = 1.0 / (total + es)
        probs.append(jnp.concatenate([p[0] * inv, p[1] * inv], axis=1))
        shares.append(es * inv)
    return jnp.concatenate(probs, axis=0), jnp.concatenate(shares, axis=0)


def _attn_group(pj, ks, vs, bias_scr, sink_ref, j):
    q_stack = _q_stack(pj, j)
    out, probs, shares = None, [], []
    for e in range(2):
        p, ps = _attn_probs(q_stack, ks[j][e], bias_scr.at[2 * j + e],
                            [sink_ref[_head(j, i, e)] for i in range(PAIRS_PER_KV)])
        p = p.astype(BF16)
        o = jnp.dot(p, vs[j][e], preferred_element_type=F32)
        out = o if out is None else out + o
        probs.append(p)
        shares.append(ps)
    return out, probs, shares


def _mix_fwd(proj, conv_full, sinks, norm_conv, norm_attn):
    ring = 3

    def body(pj_hbm, kvp_ref, cch_ref, cuh_ref, cw_ref, sink_ref, gc_ref, ga_ref,
             mixed_ref, attn_scr, p_ref, ps_ref, bias_scr, pj_ring, pj_sems):
        n = pl.program_id(0)

        def fetch(b):
            slot = lax.rem(b, ring)
            return pltpu.make_async_copy(pj_hbm.at[pl.ds(pl.multiple_of(b * BLOCK, BLOCK), BLOCK), :], pj_ring.at[slot],
                                         pj_sems.at[slot])

        @pl.when(n == 0)
        def _():
            for b in range(ring - 1):
                fetch(b).start()

        @pl.when(n + ring - 1 < N_BLOCKS)
        def _():
            fetch(n + ring - 1).start()

        fetch(n).wait()
        pj_ref = pj_ring.at[lax.rem(n, ring)]
        pj = pj_ref

        @pl.when(n == 0)
        def _():
            _fill_attn_bias(bias_scr, first_block=True)

        @pl.when(n == 1)
        def _():
            _fill_attn_bias(bias_scr, first_block=False)

        zhalo = _conv_halo(cch_ref, cuh_ref, n)
        cw = (cw_ref[0:1, :], cw_ref[1:2, :], cw_ref[2:3, :])
        gain_c = gc_ref[...]

        for r in range(N_CHUNKS):
            rows = _chunk_rows(r)
            co = _conv_chunk(pj_ref, zhalo, cw, r)[-1]
            y = _gated_norm(pj_ref[rows, OFF_CB:OFF_CB + D_CONV] * co, gain_c, pj_ref[rows, OFF_GC:OFF_GC + D_CONV])
            mixed_ref[rows, 0:D_CONV] = y.astype(BF16)

        ks, vs = _kv_bands(pj, kvp_ref)
        for j in range(2):
            out, probs, shares = _attn_group(pj, ks, vs, bias_scr, sink_ref, j)
            for e in range(2):
                p_ref[0, 2 * j + e] = probs[e]
                ps_ref[0, 2 * j + e] = shares[e]
            for i in range(PAIRS_PER_KV):
                attn_scr[:, _pair_cols(j, i, 0)] = out[BLOCK * i:BLOCK * (i + 1), :]
        gain_a = ga_ref[...]

        for r in range(N_CHUNKS):
            rows = _chunk_rows(r)
            y = _gated_norm(attn_scr[rows, :], gain_a, pj_ref[rows, OFF_GA:OFF_GA + D_ATTN])
            mixed_ref[rows, D_CONV:D_MIX] = y.astype(BF16)

    per_block = BLOCK // HALO
    return pl.pallas_call(
        body, name="mix_fwd", grid=(N_BLOCKS,),
        in_specs=[
            pl.BlockSpec(memory_space=pl.ANY),
            pl.BlockSpec((BLOCK, 2 * D_KV), lambda n: (jnp.maximum(n - 1, 0), OFF_K // (2 * D_KV))),
            pl.BlockSpec((HALO, D_CONV), lambda n: (jnp.maximum(n * per_block - 1, 0), OFF_CC // D_CONV)),
            pl.BlockSpec((HALO, D_CONV), lambda n: (jnp.maximum(n * per_block - 1, 0), OFF_CU // D_CONV)),
            pl.BlockSpec((8, D_CONV), lambda n: (0, 0)),
            pl.BlockSpec(memory_space=pltpu.SMEM),
            pl.BlockSpec((1, D_CONV), lambda n: (0, 0)),
            pl.BlockSpec((1, D_ATTN), lambda n: (0, 0)),
        ],
        out_specs=(pl.BlockSpec((BLOCK, D_MIX), lambda n: (n, 0)), pl.BlockSpec((BLOCK, D_ATTN), lambda n: (n, 0)),
                   pl.BlockSpec((1, 4, STACK, 2 * BLOCK), lambda n: (n, 0, 0, 0)),
                   pl.BlockSpec((1, 4, STACK, 128), lambda n: (n, 0, 0, 0))),
        out_shape=(jax.ShapeDtypeStruct((SEQ, D_MIX), BF16), jax.ShapeDtypeStruct((SEQ, D_ATTN), F32),
                   jax.ShapeDtypeStruct((N_BLOCKS, 4, STACK, 2 * BLOCK), BF16),
                   jax.ShapeDtypeStruct((N_BLOCKS, 4, STACK, 128), F32)),
        scratch_shapes=[pltpu.VMEM((4, STACK, 2 * BLOCK), F32), pltpu.VMEM((ring, BLOCK, D_PROJ), F32),
                        pltpu.SemaphoreType.DMA((ring,))],
        compiler_params=_params(dimension_semantics=("arbitrary",)),
    )(proj, proj, proj, proj, conv_full, sinks, norm_conv, norm_attn)


def _out_proj_loss(mixed, x, target, w_out_full, norm_final):
    tm = 256

    def body(mx_ref, x_ref, t_ref, w_ref, g_ref, dx2_ref, dx2b_ref, dmix_ref, gnf_ref, loss_ref):
        i = pl.program_id(0)
        w = w_ref[...]
        x2 = x_ref[...] + jnp.dot(mx_ref[...], w, preferred_element_type=F32)
        r = lax.rsqrt(jnp.mean(x2 * x2, axis=-1, keepdims=True) + RMS_EPS)
        xn = x2 * r
        g = g_ref[...]
        err = xn * g - t_ref[...]
        part = 0.5 * jnp.sum(jnp.mean(err * err, axis=-1, keepdims=True), axis=0, keepdims=True)
        dy = err * (1.0 / D_MODEL)
        gnf = jnp.sum(dy * xn, axis=0, keepdims=True)
        u = dy * g
        dx2 = r * (u - xn * jnp.mean(u * xn, axis=-1, keepdims=True))
        dx2_ref[...] = dx2
        dx2b = dx2.astype(BF16)
        dx2b_ref[...] = dx2b
        dmix_ref[...] = lax.dot_general(dx2b, w, _NT, preferred_element_type=F32)

        @pl.when(i == 0)
        def _():
            gnf_ref[...] = jnp.zeros_like(gnf_ref)
            loss_ref[...] = jnp.zeros_like(loss_ref)

        gnf_ref[...] += gnf
        loss_ref[...] += jnp.broadcast_to(part, loss_ref.shape)

    return pl.pallas_call(
        body, name="out_proj_loss", grid=(SEQ // tm,),
        in_specs=[pl.BlockSpec((tm, D_MIX), lambda i: (i, 0)), pl.BlockSpec((tm, D_MODEL), lambda i: (i, 0)),
                  pl.BlockSpec((tm, D_MODEL), lambda i: (i, 0)), pl.BlockSpec(memory_space=pltpu.VMEM),
                  pl.BlockSpec((1, D_MODEL), lambda i: (0, 0))],
        out_specs=(pl.BlockSpec((tm, D_MODEL), lambda i: (i, 0)), pl.BlockSpec((tm, D_MODEL), lambda i: (i, 0)),
                   pl.BlockSpec((tm, D_MIX), lambda i: (i, 0)),
                   pl.BlockSpec((1, D_MODEL), lambda i: (0, 0)), pl.BlockSpec((8, 128), lambda i: (0, 0))),
        out_shape=(jax.ShapeDtypeStruct((SEQ, D_MODEL), F32), jax.ShapeDtypeStruct((SEQ, D_MODEL), BF16),
                   jax.ShapeDtypeStruct((SEQ, D_MIX), F32),
                   jax.ShapeDtypeStruct((1, D_MODEL), F32), jax.ShapeDtypeStruct((8, 128), F32)),
        compiler_params=_params(dimension_semantics=("arbitrary",)),
    )(mixed, x, target, w_out_full, norm_final)


def _gated_norm_bwd(a, gain, t, dy):
    r = lax.rsqrt(jnp.mean(a * a, axis=-1, keepdims=True) + RMS_EPS)
    an = a * r
    sg = _sigmoid(t)
    dn = dy * (t * sg)
    dt = dy * (an * gain) * (sg * (1.0 + t * (1.0 - sg)))
    u = dn * gain
    da = r * (u - an * jnp.mean(u * an, axis=-1, keepdims=True))
    return da, dt, dn * an


def _mix_bwd(proj, dmixed, attn, probs, shares, conv_full, norm_conv, norm_attn):
    def body(pj_ref, kvp_ref, cch_ref, cuh_ref, dmx_ref, attn_ref, p_ref, ps_ref, cw_ref, gc_ref, ga_ref,
             dpj_ref, gslab_ref, dattn_scr, nxt_scr, dkv_scr, acc_scr):
        step = pl.program_id(0)
        n = N_BLOCKS - 1 - step
        pj = pj_ref

        @pl.when(step == 0)
        def _():
            gslab_ref[...] = jnp.zeros_like(gslab_ref)
            nxt_scr[...] = jnp.zeros_like(nxt_scr)
            dkv_scr[...] = jnp.zeros_like(dkv_scr)
            acc_scr[...] = jnp.zeros_like(acc_scr)

        zhalo = _conv_halo(cch_ref, cuh_ref, n)
        cw = (cw_ref[0:1, :], cw_ref[1:2, :], cw_ref[2:3, :])
        gain_c = gc_ref[...]
        row = lax.broadcasted_iota(jnp.int32, (CHUNK, D_CONV), 0)

        dco_after = nxt_scr[...]
        for r in reversed(range(N_CHUNKS)):
            rows = _chunk_rows(r)
            cc, cu, z, z1, z2, co = _conv_chunk(pj_ref, zhalo, cw, r)
            cb = pj_ref[rows, OFF_CB:OFF_CB + D_CONV]
            da, dgate, gterm = _gated_norm_bwd(cb * co, gain_c, pj_ref[rows, OFF_GC:OFF_GC + D_CONV],
                                               dmx_ref[rows, 0:D_CONV])
            dpj_ref[rows, OFF_GC:OFF_GC + D_CONV] = dgate.astype(BF16)
            dpj_ref[rows, OFF_CB:OFF_CB + D_CONV] = (da * co).astype(BF16)
            dco = da * cb
            dco1 = jnp.where(row >= CHUNK - 1, pltpu.roll(dco_after, CHUNK - 1, 0), pltpu.roll(dco, CHUNK - 1, 0))
            dco2 = jnp.where(row >= CHUNK - 2, pltpu.roll(dco_after, CHUNK - 2, 0), pltpu.roll(dco, CHUNK - 2, 0))
            dz = cw[2] * dco + cw[1] * dco1 + cw[0] * dco2
            dpj_ref[rows, OFF_CC:OFF_CC + D_CONV] = (dz * cu).astype(BF16)
            dpj_ref[rows, OFF_CU:OFF_CU + D_CONV] = (dz * cc).astype(BF16)
            acc_scr[ACC_NORM_CONV] += gterm
            acc_scr[ACC_CONV0] += dco * z2
            acc_scr[ACC_CONV0 + 1] += dco * z1
            acc_scr[ACC_CONV0 + 2] += dco * z
            dco_after = dco
        nxt_scr[...] = dco_after

        ks, vs = _kv_bands(pj, kvp_ref)
        gain_a = ga_ref[...]

        for r in range(N_CHUNKS):
            rows = _chunk_rows(r)
            da, dgate, gterm = _gated_norm_bwd(attn_ref[rows, :], gain_a, pj_ref[rows, OFF_GA:OFF_GA + D_ATTN],
                                               dmx_ref[rows, D_CONV:D_MIX])
            dpj_ref[rows, OFF_GA:OFF_GA + D_ATTN] = dgate.astype(BF16)
            dattn_scr[rows, :] = da
            acc_scr[ACC_NORM_ATTN] += gterm

        in_lo = lax.broadcasted_iota(jnp.int32, (128, 128), 0) < HEAD_DIM
        half_ones = (jnp.where(in_lo, 1.0, 0.0).astype(BF16), jnp.where(in_lo, 0.0, 1.0).astype(BF16))
        lane_s = lax.broadcasted_iota(jnp.int32, (1, D_MODEL), 1)
        gsink = jnp.zeros((1, D_MODEL), F32)
        dk_t, dv_t = [], []
        for j in range(2):
            q_stack = _q_stack(pj, j)
            do_f = jnp.concatenate([dattn_scr[:, _pair_cols(j, i, 0)] for i in range(PAIRS_PER_KV)], axis=0)
            o_f = jnp.concatenate([attn_ref[:, _pair_cols(j, i, 0)] for i in range(PAIRS_PER_KV)], axis=0)
            prod = (do_f * o_f).astype(BF16)
            deltas = [jnp.dot(prod, half_ones[e], preferred_element_type=F32) for e in range(2)]
            do_b = do_f.astype(BF16)
            q_t, do_t = q_stack.T, do_b.T
            dq, dk_j, dv_j = None, None, None
            for e in range(2):
                p = p_ref[0, 2 * j + e]
                dp = lax.dot_general(do_b, vs[j][e], _NT, preferred_element_type=F32)
                ds = []
                for i in range(PAIRS_PER_KV):
                    rows = slice(BLOCK * i, BLOCK * (i + 1))
                    delta = deltas[e][rows, :]
                    ds.append((p[rows, :].astype(F32) * (dp[rows, :] - jnp.concatenate([delta, delta], axis=1))).astype(BF16))
                    gs_h = -jnp.sum(ps_ref[0, 2 * j + e, rows, 0:1] * delta[:, 0:1], axis=0, keepdims=True)
                    gsink = gsink + jnp.where(lane_s == _head(j, i, e), gs_h, 0.0)
                ds = jnp.concatenate(ds, axis=0)
                t = jnp.dot(ds, ks[j][e], preferred_element_type=F32)
                dq = t if dq is None else dq + t
                half = slice(HEAD_DIM * e, HEAD_DIM * (e + 1))
                a = jnp.dot(q_t[half, :], ds, preferred_element_type=F32)
                b = jnp.dot(do_t[half, :], p, preferred_element_type=F32)
                dk_j = a if dk_j is None else dk_j + a
                dv_j = b if dv_j is None else dv_j + b
            for i in range(PAIRS_PER_KV):
                dpj_ref[:, _pair_cols(j, i, OFF_Q)] = (dq[BLOCK * i:BLOCK * (i + 1), :] * SCALE).astype(BF16)
            dk_t.append(dk_j)
            dv_t.append(dv_j)
        dk = jnp.concatenate(dk_t, axis=0).T
        dv = jnp.concatenate(dv_t, axis=0).T
        dpj_ref[:, OFF_K:OFF_K + D_KV] = (dk[BLOCK:, :] + dkv_scr[:, 0:D_KV]).astype(BF16)
        dpj_ref[:, OFF_V:OFF_V + D_KV] = (dv[BLOCK:, :] + dkv_scr[:, D_KV:2 * D_KV]).astype(BF16)
        dkv_scr[:, 0:D_KV] = dk[:BLOCK, :]
        dkv_scr[:, D_KV:2 * D_KV] = dv[:BLOCK, :]
        gslab_ref[ROW_SINKS:ROW_SINKS + 1, :] += gsink

        @pl.when(step == N_BLOCKS - 1)
        def _():
            for k, slab_row in ((ACC_NORM_CONV, ROW_NORM_CONV), (ACC_NORM_ATTN, ROW_NORM_ATTN), (ACC_CONV0, ROW_CONV0),
                                (ACC_CONV0 + 1, ROW_CONV0 + 1), (ACC_CONV0 + 2, ROW_CONV0 + 2)):
                gslab_ref[slab_row:slab_row + 1, :] = jnp.sum(acc_scr[k], axis=0, keepdims=True)

    per_block = BLOCK // HALO
    last = N_BLOCKS - 1
    return pl.pallas_call(
        body, name="mix_bwd", grid=(N_BLOCKS,),
        in_specs=[
            pl.BlockSpec((BLOCK, D_PROJ), lambda s: (last - s, 0)),
            pl.BlockSpec((BLOCK, 2 * D_KV), lambda s: (jnp.maximum(last - s - 1, 0), OFF_K // (2 * D_KV))),
            pl.BlockSpec((HALO, D_CONV), lambda s: (jnp.maximum((last - s) * per_block - 1, 0), OFF_CC // D_CONV)),
            pl.BlockSpec((HALO, D_CONV), lambda s: (jnp.maximum((last - s) * per_block - 1, 0), OFF_CU // D_CONV)),
            pl.BlockSpec((BLOCK, D_MIX), lambda s: (last - s, 0)),
            pl.BlockSpec((BLOCK, D_ATTN), lambda s: (last - s, 0)),
            pl.BlockSpec((1, 4, STACK, 2 * BLOCK), lambda s: (last - s, 0, 0, 0)),
            pl.BlockSpec((1, 4, STACK, 128), lambda s: (last - s, 0, 0, 0)),
            pl.BlockSpec((8, D_CONV), lambda s: (0, 0)),
            pl.BlockSpec((1, D_CONV), lambda s: (0, 0)),
            pl.BlockSpec((1, D_ATTN), lambda s: (0, 0)),
        ],
        out_specs=(pl.BlockSpec((BLOCK, D_PROJ), lambda s: (last - s, 0)),
                   pl.BlockSpec((8, D_MODEL), lambda s: (0, 0))),
        out_shape=(jax.ShapeDtypeStruct((SEQ, D_PROJ), BF16), jax.ShapeDtypeStruct((8, D_MODEL), F32)),
        scratch_shapes=[pltpu.VMEM((BLOCK, D_ATTN), F32), pltpu.VMEM((CHUNK, D_CONV), F32),
                        pltpu.VMEM((BLOCK, 2 * D_KV), F32), pltpu.VMEM((N_ACC, CHUNK, D_MODEL), F32)],
        compiler_params=_params(dimension_semantics=("arbitrary",)),
    )(proj, proj, proj, proj, dmixed, attn, probs, shares, conv_full, norm_conv, norm_attn)


def _in_bwd_rs(dproj, w_full, x, dx2, norm_in, dw_in_chip, gslab, gnf, loss_part):
    tm = 256
    steps = SEQ // tm

    def body(dp_ref, w_hbm, x_ref, dx2_ref, g_ref, dwi_ref, gs_ref, gnf_ref, lp_ref, gx_ref, gwin_ref, gsum_ref,
             gni_scr, own, d2d, ici, myslab, slabs, w_ref, send_sems, recv_sems, local_sems):
        i = pl.program_id(0)
        rs_start, rs_finish = _ici_sum(dwi_ref, own, d2d, ici, send_sems, recv_sems, local_sems)
        slab_start, slab_finish = _slab_sum(myslab, slabs, send_sems, recv_sems, N_ICI_SUM_SEMS)

        @pl.when(i == 0)
        def _():
            gni_scr[...] = jnp.zeros_like(gni_scr)
            rs_start()
            w_load = pltpu.make_async_copy(w_hbm, w_ref, local_sems.at[1])
            w_load.start()
            w_load.wait()

        dh = jnp.dot(dp_ref[...], w_ref[...], preferred_element_type=F32)
        xv = x_ref[...]
        r = lax.rsqrt(jnp.mean(xv * xv, axis=-1, keepdims=True) + RMS_EPS)
        xn = xv * r
        u = dh * g_ref[...]
        gx_ref[...] = dx2_ref[...] + r * (u - xn * jnp.mean(u * xn, axis=-1, keepdims=True))
        gni_scr[...] += jnp.sum(dh * xn, axis=0, keepdims=True)

        @pl.when(i == steps - 1)
        def _():
            row = lax.broadcasted_iota(jnp.int32, (8, D_MODEL), 0)
            lane = lax.broadcasted_iota(jnp.int32, (8, D_MODEL), 1)
            slab = jnp.where(row == ROW_NORM_IN, gni_scr[...], jnp.where(row == ROW_NORM_FINAL, gnf_ref[...], gs_ref[...]))
            myslab[...] = jnp.where((row == ROW_SINKS) & (lane == LOSS_LANE), lp_ref[0:1, 0:1], slab)
            slab_start()
            gwin_ref[...] = rs_finish()
            gsum_ref[...] = slab_finish()

    const = lambda i: (0, 0)
    return pl.pallas_call(
        body, name="in_bwd", grid=(steps,),
        in_specs=[pl.BlockSpec((tm, D_PROJ), lambda i: (i, 0)), pl.BlockSpec(memory_space=pl.ANY),
                  pl.BlockSpec((tm, D_MODEL), lambda i: (i, 0)), pl.BlockSpec((tm, D_MODEL), lambda i: (i, 0)),
                  pl.BlockSpec((1, D_MODEL), const), pl.BlockSpec(memory_space=pl.ANY),
                  pl.BlockSpec((8, D_MODEL), const), pl.BlockSpec((1, D_MODEL), const), pl.BlockSpec((8, 128), const)],
        out_specs=(pl.BlockSpec((tm, D_MODEL), lambda i: (i, 0)), pl.BlockSpec((SHARD_IN, D_MODEL), const),
                   pl.BlockSpec((8, D_MODEL), const)),
        out_shape=(jax.ShapeDtypeStruct((SEQ, D_MODEL), F32), jax.ShapeDtypeStruct((SHARD_IN, D_MODEL), F32),
                   jax.ShapeDtypeStruct((8, D_MODEL), F32)),
        scratch_shapes=[pltpu.VMEM((1, D_MODEL), F32), pltpu.VMEM((SHARD_IN, D_MODEL), BF16),
                        pltpu.VMEM((SHARD_IN, D_MODEL), BF16), pltpu.VMEM((2, SHARD_IN, D_MODEL), BF16),
                        pltpu.VMEM((8, D_MODEL), F32), pltpu.VMEM((N_DEV, 8, D_MODEL), F32),
                        pltpu.VMEM((D_PROJ, D_MODEL), BF16),
                        pltpu.SemaphoreType.DMA((N_ICI_SUM_SEMS + 7,)), pltpu.SemaphoreType.DMA((N_ICI_SUM_SEMS + 7,)),
                        pltpu.SemaphoreType.DMA((2,))],
        compiler_params=_params(dimension_semantics=("arbitrary",)),
    )(dproj, w_full, x, dx2, norm_in, dw_in_chip, gslab, gnf, loss_part)


def _dw_rs(mixed, dx2b, dproj, h, table):
    tn_out, tn = 2 * SHARD_OUT, IN_PROJ_TILE
    out_steps, in_steps = D_MIX // tn_out, D_PROJ // tn
    steps = out_steps + in_steps
    out_order = (DG, NX, NY, OWN)

    def out_tile(i):
        chip = 2 * lax.axis_index("x") + lax.axis_index("y")
        return jnp.bitwise_xor(chip, (out_steps - 1) - jnp.minimum(i, out_steps - 1))

    def in_tile(table_ref, i):
        return _dw_entry(table_ref, jnp.maximum(i - out_steps, 0))

    def body(table_ref, mx_ref, dxb_ref, a_ref, h_hbm, chip_ref, gwo_ref, dwo, dwt, d2d_in, via, own, d2d, ici, b_ref,
             send_sems, recv_sems, local_sems):
        i = pl.program_id(0)
        h_load = pltpu.make_async_copy(h_hbm, b_ref, local_sems.at[8])

        @pl.when(i == 0)
        def _():
            h_load.start()

        @pl.when(i == out_steps)
        def _():
            h_load.wait()

        rs_start, rs_forward, rs_finish = _shard_sum(dwo, own, d2d, ici, send_sems, recv_sems, local_sems)
        before_tile, after_tiles, chip_finish = _chip_sum(
            dwt, d2d_in, via, chip_ref, lambda k: _dw_entry(table_ref, in_steps + k), send_sems, recv_sems, local_sems,
            N_SHARD_SUM_SEMS, 4)

        for j, k in enumerate(out_order):
            @pl.when(i == j + 1)
            def _():
                rs_start(k)

            if k != OWN:
                @pl.when(i == j + 2)
                def _():
                    rs_forward(k)

        @pl.when(i < out_steps)
        def _():
            tile = lax.dot_general(mx_ref[...], dxb_ref[...], _TN, preferred_element_type=F32).astype(BF16)
            for core in range(2):
                dwo[2 * out_tile(i) + core] = tile[SHARD_OUT * core:SHARD_OUT * (core + 1), :]

        @pl.when(i >= out_steps)
        def _():
            before_tile(i - out_steps)
            tile = lax.dot_general(a_ref[...], b_ref[...], _TN, preferred_element_type=F32).astype(BF16)
            dwt[pl.ds(pl.multiple_of(in_tile(table_ref, i) * tn, tn), tn), :] = tile

        @pl.when(i == steps - 2)
        def _():
            gwo_ref[...] = rs_finish()

        @pl.when(i == steps - 1)
        def _():
            after_tiles()
            chip_finish()

    vmem = pl.BlockSpec(memory_space=pltpu.VMEM)
    grid_spec = pltpu.PrefetchScalarGridSpec(
        num_scalar_prefetch=1, grid=(steps,),
        in_specs=[pl.BlockSpec((SEQ, tn_out), lambda i, table_ref: (0, out_tile(i))), vmem,
                  pl.BlockSpec((SEQ, tn), lambda i, table_ref: (0, in_tile(table_ref, i))),
                  pl.BlockSpec(memory_space=pl.ANY)],
        out_specs=(pl.BlockSpec(memory_space=pl.ANY), pl.BlockSpec((SHARD_OUT, D_MODEL), lambda i, table_ref: (0, 0))),
        scratch_shapes=[pltpu.VMEM((N_DEV, SHARD_OUT, D_MODEL), BF16),
                        pltpu.VMEM((D_PROJ, D_MODEL), BF16), pltpu.VMEM((3, SHARD_IN, D_MODEL), BF16),
                        pltpu.VMEM((2, HALF_IN, D_MODEL), BF16),
                        *_shard_sum_scratch(SHARD_OUT), pltpu.VMEM((SEQ, D_MODEL), BF16),
                        pltpu.SemaphoreType.DMA((N_SHARD_SUM_SEMS + N_CHIP_SUM_SEMS,)),
                        pltpu.SemaphoreType.DMA((N_SHARD_SUM_SEMS + N_CHIP_SUM_SEMS,)),
                        pltpu.SemaphoreType.DMA((9,))])
    return pl.pallas_call(
        body, name="dw", grid_spec=grid_spec,
        out_shape=(jax.ShapeDtypeStruct((4, SHARD_IN, D_MODEL), BF16), jax.ShapeDtypeStruct((SHARD_OUT, D_MODEL), F32)),
        compiler_params=_params(dimension_semantics=("arbitrary",)),
    )(table, mixed, dx2b, dproj, h)


def _adam_all(big_in, big_out, gsum, small, grad_x):
    n_chunks = 4
    n_big = 8

    def body(*refs):
        ins, outs = refs[:n_big + 1 + 18 + 1], refs[n_big + 1 + 18 + 1:n_big + 1 + 18 + 1 + 34]
        in_bufs, out_bufs, gx_buf = refs[-n_big - 6 - 4:-6 - 4], refs[-6 - 4:-4], refs[-4]
        in_sems, out_sems, gx_sems = refs[-3:]

        def gx_rows(j):
            return pl.ds(j * (SEQ // n_chunks), SEQ // n_chunks)

        def gx_load(j):
            return pltpu.make_async_copy(ins[27].at[gx_rows(j), :], gx_buf.at[gx_rows(j), :], gx_sems.at[j])

        def gx_store(j):
            return pltpu.make_async_copy(gx_buf.at[gx_rows(j), :], outs[33].at[gx_rows(j), :], gx_sems.at[n_chunks + j])

        def rows(a, j):
            tr = ins[a].shape[0] // n_chunks
            return pl.ds(j * tr, tr)

        def load(a, j):
            return pltpu.make_async_copy(ins[a].at[rows(a, j), :], in_bufs[a].at[rows(a, j), :], in_sems.at[a * n_chunks + j])

        def store(a, j):
            b, kind = divmod(a, 4)
            src = in_bufs[4 * b + 1] if kind == 0 else out_bufs[3 * b + kind - 1]
            return pltpu.make_async_copy(src.at[rows(a, j), :], outs[a].at[rows(a, j), :], out_sems.at[a * n_chunks + j])

        for j in range(n_chunks):
            for a in range(n_big):
                load(a, j).start()
            gx_load(j).start()

        def small_weights():
            gsum = ins[8][...]
            idx = _slot(lax.axis_index("x"), lax.axis_index("y"), lax.axis_index("c"))
            cg = jnp.zeros((3, SHARD_CONV), F32)
            for d in range(N_DEV):
                cg = jnp.where(idx == d, gsum[ROW_CONV0:ROW_CONV0 + 3, d * SHARD_CONV:(d + 1) * SHARD_CONV], cg)
            grads = (gsum[ROW_NORM_IN:ROW_NORM_IN + 1], gsum[ROW_SINKS:ROW_SINKS + 1, 0:N_Q_HEADS],
                     gsum[ROW_NORM_CONV:ROW_NORM_CONV + 1], gsum[ROW_NORM_ATTN:ROW_NORM_ATTN + 1],
                     gsum[ROW_NORM_FINAL:ROW_NORM_FINAL + 1], cg)
            for s, g in enumerate(grads):
                at = (slice(None), 0, slice(None)) if s == 5 else (slice(None), slice(None))
                w_ref, m_ref, v_ref = ins[9 + 3 * s:12 + 3 * s]
                delta, mn, vn = _adamw(w_ref[at], g, m_ref[at], v_ref[at])
                for ref, val in zip(outs[8 + 4 * s:12 + 4 * s], (g, delta, mn, vn)):
                    ref[at] = val
            outs[32][...] = gsum[ROW_SINKS:ROW_SINKS + 1, LOSS_LANE:LOSS_LANE + 1]

        small_weights()
        for j in range(n_chunks):
            for b in range(2):
                for a in range(4 * b, 4 * b + 4):
                    load(a, j).wait()
                w_buf, g_buf, m_buf, v_buf = in_bufs[4 * b:4 * b + 4]
                r = rows(4 * b, j)
                results = _adamw(w_buf[r, :], g_buf[r, :], m_buf[r, :], v_buf[r, :])
                for buf, val in zip(out_bufs[3 * b:3 * b + 3], results):
                    buf[r, :] = val
                for a in range(4 * b, 4 * b + 4):
                    store(a, j).start()
            gx_load(j).wait()
            gx_store(j).start()
        for j in range(n_chunks):
            for a in range(n_big):
                store(a, j).wait()
            gx_store(j).wait()

    vmem, hbm = pl.BlockSpec(memory_space=pltpu.VMEM), pl.BlockSpec(memory_space=pl.ANY)
    small_shapes = [a.shape for a in small[::3]]
    big_shapes = [(SHARD_IN, D_MODEL)] * 4 + [(SHARD_OUT, D_MODEL)] * 4
    out_shape = ([jax.ShapeDtypeStruct(s, F32) for s in big_shapes]
                 + [jax.ShapeDtypeStruct(s, F32) for s in small_shapes for _ in range(4)]
                 + [jax.ShapeDtypeStruct((1, 1), F32), jax.ShapeDtypeStruct((SEQ, D_MODEL), F32)])
    outs = pl.pallas_call(
        body, name="adam", in_specs=[hbm] * n_big + [vmem] * (1 + len(small)) + [hbm],
        out_specs=tuple([hbm] * n_big + [vmem] * (4 * len(small_shapes) + 1) + [hbm]), out_shape=tuple(out_shape),
        scratch_shapes=[pltpu.VMEM(s, F32) for s in big_shapes]
                       + [pltpu.VMEM(s, F32) for s in [(SHARD_IN, D_MODEL)] * 3 + [(SHARD_OUT, D_MODEL)] * 3]
                       + [pltpu.VMEM((SEQ, D_MODEL), F32),
                          pltpu.SemaphoreType.DMA((n_big * n_chunks,)), pltpu.SemaphoreType.DMA((n_big * n_chunks,)),
                          pltpu.SemaphoreType.DMA((2 * n_chunks,))],
        compiler_params=_params(),
    )(*big_in, *big_out, gsum, *small, grad_x)
    return outs[0:4], outs[4:8], [outs[8 + 4 * s:12 + 4 * s] for s in range(6)], outs[32], outs[33]


def _rows_first(a):
    return jnp.transpose(a, (1, 0, 2))


def kernel(x, norm_in, w_in, conv_w, attn_sinks, norm_conv_out, norm_attn_out, w_out, norm_final, loss_target, m_norm_in, m_w_in, m_conv_w, m_attn_sinks, m_norm_conv_out, m_norm_attn_out, m_w_out, m_norm_final, v_norm_in, v_w_in, v_conv_w, v_attn_sinks, v_norm_conv_out, v_norm_attn_out, v_w_out, v_norm_final):
    x2d = x.reshape(SEQ, D_MODEL)
    target = loss_target.reshape(SEQ, D_MODEL)
    nf = norm_final.reshape(1, D_MODEL)

    w_in_t, m_w_in_t, v_w_in_t = w_in[0].T, m_w_in[0].T, v_w_in[0].T
    tiles = jnp.asarray(TILE_ORDER, jnp.int32).reshape(-1)
    w_in_full, h, proj, g_out, conv_full = _gather_in_proj(x2d, norm_in, w_in_t, w_out[0], _rows_first(conv_w), tiles)
    sinks = attn_sinks.reshape(N_Q_HEADS)

    mixed, attn, probs, shares = _mix_fwd(proj, conv_full, sinks, norm_conv_out, norm_attn_out)
    dx2, dx2b, dmixed, gnf, loss_part = _out_proj_loss(mixed, x2d, target, g_out.reshape(D_MIX, D_MODEL), nf)
    dproj, gslab = _mix_bwd(proj, dmixed, attn, probs, shares, conv_full, norm_conv_out, norm_attn_out)
    dw_in_chip, g_w_out = _dw_rs(mixed, dx2b, dproj, h, jnp.asarray(DW_TABLE, jnp.int32).reshape(-1))
    grad_x, g_w_in, gsum = _in_bwd_rs(dproj, w_in_full, x2d, dx2, norm_in, dw_in_chip, gslab, gnf, loss_part)

    small = (norm_in, m_norm_in, v_norm_in, attn_sinks, m_attn_sinks, v_attn_sinks,
             norm_conv_out, m_norm_conv_out, v_norm_conv_out, norm_attn_out, m_norm_attn_out, v_norm_attn_out,
             nf, m_norm_final.reshape(1, D_MODEL), v_norm_final.reshape(1, D_MODEL),
             _rows_first(conv_w), _rows_first(m_conv_w), _rows_first(v_conv_w))
    big_in, big_out, (s_ni, s_sk, s_nc, s_na, s_nf, s_cv), loss, grad_x = _adam_all(
        (w_in_t, g_w_in, m_w_in_t, v_w_in_t), (w_out[0], g_w_out, m_w_out[0], v_w_out[0]), gsum, small, grad_x)

    def leaves(k):
        return (s_ni[k], big_in[k].T[None], jnp.transpose(s_cv[k], (1, 0, 2)), s_sk[k], s_nc[k], s_na[k], big_out[k][None],
                s_nf[k].reshape(D_MODEL))

    return (loss.reshape(()), grad_x.reshape(1, SEQ, D_MODEL), *leaves(0), *leaves(1), *leaves(2), *leaves(3))
```

```python
import jax
import jax.numpy as jnp
from jax import lax
from jax.experimental import pallas as pl
from jax.experimental.pallas import tpu as pltpu

F32 = jnp.float32
BF16 = jnp.bfloat16
MESH = pl.DeviceIdType.MESH

N_DEV = 8
SEQ = 2048
D_MODEL = 1024
D_CONV = 1024
D_ATTN = 1024
D_KV = 128
HEAD_DIM = 64
N_Q_HEADS = 16
N_PAIRS = N_Q_HEADS // 2
PAIRS_PER_KV = N_PAIRS // 2
D_MIX = D_CONV + D_ATTN
D_PROJ = 6400
SHARD_IN = D_PROJ // N_DEV
SHARD_OUT = D_MIX // N_DEV
SHARD_CONV = D_CONV // N_DEV
OFF_CB, OFF_CC, OFF_CU, OFF_GC, OFF_Q, OFF_K, OFF_V, OFF_GA = 0, 1024, 2048, 3072, 4096, 5120, 5248, 5376
BLOCK = 128
N_BLOCKS = SEQ // BLOCK
HALO = 8
CHUNK = 16
N_CHUNKS = BLOCK // CHUNK
RMS_EPS = 1e-5
NEG = -1e30
SCALE = HEAD_DIM ** -0.5
SLOPES = tuple(2.0 ** (-8.0 * (h + 1) / N_Q_HEADS) for h in range(N_Q_HEADS))

ADAM_LR = 0.001
ADAM_B1 = 0.9
ADAM_B2 = 0.999
ADAM_EPS = 1e-08
ADAM_WD = 0.01
ADAM_STEP = 10

ROW_NORM_IN, ROW_NORM_CONV, ROW_NORM_ATTN, ROW_NORM_FINAL, ROW_CONV0, ROW_SINKS = 0, 1, 2, 3, 4, 7
LOSS_LANE = N_Q_HEADS
ACC_NORM_CONV, ACC_NORM_ATTN, ACC_CONV0, N_ACC = 0, 1, 2, 5

VMEM_LIMIT = 56 * 1024 * 1024

_NT = (((1,), (1,)), ((), ()))
_TN = (((0,), (0,)), ((), ()))


def _params(**kw):
    return pltpu.CompilerParams(vmem_limit_bytes=VMEM_LIMIT, **kw)


def _adamw(w, g, m, v):
    m = ADAM_B1 * m + (1.0 - ADAM_B1) * g
    v = ADAM_B2 * v + (1.0 - ADAM_B2) * (g * g)
    m_hat = m / (1.0 - ADAM_B1 ** ADAM_STEP)
    v_hat = v / (1.0 - ADAM_B2 ** ADAM_STEP)
    delta = -ADAM_LR * (m_hat / (jnp.sqrt(v_hat) + ADAM_EPS) + ADAM_WD * w)
    return delta, m, v


def _sigmoid(t):
    return 1.0 / (1.0 + jnp.exp(-t))


def _slot(px, py, pc):
    return 4 * px + 2 * py + pc


OWN, NX, NY, DG = range(4)
HALF_IN = SHARD_IN // 2
N_GATHER_KINDS = 13
W_OUT_KINDS = N_GATHER_KINDS + 7


IN_PROJ_TILE = 640
TILE_ORDER = ((0, 1, 2, 3, 4, 5, 6, 7, 8, 9), (3, 4, 0, 1, 2, 8, 9, 5, 6, 7),
              (5, 6, 0, 1, 7, 8, 9, 2, 3, 4), (8, 9, 3, 4, 5, 6, 7, 0, 1, 2))
TILES_OWN, TILES_NEIGHBOURS = 2, 7


def _tile(table_ref, p):
    chip = 2 * lax.axis_index("x") + lax.axis_index("y")
    return table_ref[chip * len(TILE_ORDER[0]) + p]


DW_TILE_ORDER = tuple(tuple(reversed(row)) for row in TILE_ORDER)


def _tiles_until_complete(chip, owner):
    lo, hi = owner * 2 * SHARD_IN, (owner + 1) * 2 * SHARD_IN
    touching = [t for t in range(len(TILE_ORDER[0])) if t * IN_PROJ_TILE < hi and (t + 1) * IN_PROJ_TILE > lo]
    return 1 + max(DW_TILE_ORDER[chip].index(t) for t in touching)


DW_TABLE = tuple(DW_TILE_ORDER[chip] + tuple(_tiles_until_complete(chip, chip ^ flip) for flip in (0, 2, 1, 3))
                 for chip in range(4))


def _dw_entry(table_ref, p):
    chip = 2 * lax.axis_index("x") + lax.axis_index("y")
    return table_ref[chip * len(DW_TABLE[0]) + p]


def _gather_in_proj(x, norm_in, w_in_sh, w_out_sh, conv_sh, tiles):
    tn = IN_PROJ_TILE
    steps = D_PROJ // tn
    tm = 256

    def body(tiles_ref, x_hbm, g_ref, win_ref, wout_ref, cv_ref, wt_ref, h_ref, proj_ref, gout_ref, conv_ref,
             gin_ref, gcv_ref, wob_ref, x_ref, send_sems, recv_sems, local_sems):
        p = pl.program_id(0)
        local_sem = local_sems.at[0]
        x, y, c = lax.axis_index("x"), lax.axis_index("y"), lax.axis_index("c")
        me, sibling = (x, y, c), (x, y, 1 - c)
        nx, ny, dg = (1 - x, y, c), (x, 1 - y, c), (1 - x, 1 - y, c)

        def other(dev):
            return (dev[0], dev[1], 1 - dev[2])

        def shard(dev):
            return gin_ref.at[pl.ds(pl.multiple_of(_slot(*dev) * SHARD_IN, 16), SHARD_IN), :]

        def half(dev, h):
            return gin_ref.at[pl.ds(pl.multiple_of(_slot(*dev) * SHARD_IN + h * HALF_IN, 16), HALF_IN), :]

        def rc(ref, k, to):
            return pltpu.make_async_remote_copy(src_ref=ref, dst_ref=ref, send_sem=send_sems.at[k],
                                                recv_sem=recv_sems.at[k], device_id=to, device_id_type=MESH)

        def cv(k, dev, to):
            s = _slot(*dev)
            return pltpu.make_async_remote_copy(src_ref=gcv_ref.at[s], dst_ref=gcv_ref.at[s],
                                                send_sem=send_sems.at[N_GATHER_KINDS + k],
                                                recv_sem=recv_sems.at[N_GATHER_KINDS + k], device_id=to, device_id_type=MESH)

        def own_copies():
            return [rc(shard(me), 0, sibling),
                    rc(half(me, 0), 1, nx), rc(half(me, 1), 2, nx),
                    rc(half(me, 1), 4, ny), rc(half(me, 0), 3, ny),
                    cv(0, me, sibling)] + [cv(1 + j, me, peer) for j, peer in enumerate((nx, ny, dg))]

        def pass_on(dev, h, k_in, k_ici, k_d2d, half=half, base=0):
            rc(half(dev, h), base + k_in, me).wait_recv()
            if k_ici is not None:
                rc(half(dev, h), base + k_ici, ny if dev is nx else nx).start()
            rc(half(dev, h), base + k_d2d, sibling).start()

        def out_half(dev, h):
            return gout_ref.at[_slot(*dev), pl.ds(h * (SHARD_OUT // 2), SHARD_OUT // 2), :]

        def own_out_copies():
            src = lambda h: wob_ref.at[pl.ds(h * (SHARD_OUT // 2), SHARD_OUT // 2), :]

            def send(ref, dst, k, to):
                return pltpu.make_async_remote_copy(src_ref=ref, dst_ref=dst, send_sem=send_sems.at[W_OUT_KINDS + k],
                                                    recv_sem=recv_sems.at[W_OUT_KINDS + k], device_id=to, device_id_type=MESH)

            return [send(wob_ref, gout_ref.at[_slot(*me)], 0, sibling),
                    send(src(0), out_half(me, 0), 1, nx), send(src(1), out_half(me, 1), 2, nx),
                    send(src(1), out_half(me, 1), 4, ny), send(src(0), out_half(me, 0), 3, ny)]

        def own_out_local():
            return pltpu.make_async_copy(wob_ref, gout_ref.at[_slot(*me)], local_sems.at[1])

        @pl.when(p == 0)
        def _():
            gin_ref[pl.ds(pl.multiple_of(_slot(*me) * SHARD_IN, 16), SHARD_IN), :] = win_ref[...].astype(BF16)
            gcv_ref[_slot(*me)] = jnp.zeros((8, SHARD_CONV), F32)
            gcv_ref[_slot(*me), 0:3, :] = cv_ref[:, 0, :]
            for cp in own_copies():
                cp.start()
            wob_ref[...] = wout_ref[...].astype(BF16)
            x_load = pltpu.make_async_copy(x_hbm, x_ref, local_sems.at[2])
            x_load.start()
            x_load.wait()
            for t in range(SEQ // tm):
                xv = x_ref[tm * t:tm * (t + 1), :]
                r = lax.rsqrt(jnp.mean(xv * xv, axis=-1, keepdims=True) + RMS_EPS)
                h_ref[tm * t:tm * (t + 1), :] = (xv * r * g_ref[...]).astype(BF16)
            rc(shard(sibling), 0, me).wait_recv()

        @pl.when(p == TILES_OWN)
        def _():
            for args in ((nx, 0, 1, 5, 7), (ny, 1, 4, 6, 10), (nx, 1, 2, None, 8), (ny, 0, 3, None, 9)):
                pass_on(*args)
            for j, peer in enumerate((nx, ny, dg)):
                cv(1 + j, peer, me).wait_recv()
                cv(4 + j, peer, sibling).start()
            for (dev, h), k in (((nx, 0), 7), ((nx, 1), 8), ((ny, 0), 9), ((ny, 1), 10)):
                rc(half(other(dev), h), k, me).wait_recv()
            own_out_local().start()
            for cp in own_out_copies():
                cp.start()

        @pl.when(p == TILES_NEIGHBOURS - 1)
        def _():
            pass_on(dg, 0, 5, None, 11)
            pass_on(dg, 1, 6, None, 12)

        @pl.when(p == TILES_NEIGHBOURS)
        def _():
            for (dev, h), k in (((dg, 0), 11), ((dg, 1), 12)):
                rc(half(other(dev), h), k, me).wait_recv()
            pltpu.make_async_copy(gin_ref, wt_ref, local_sem).start()

        @pl.when(p == steps - 2)
        def _():
            for args in ((nx, 0, 1, 5, 7), (ny, 1, 4, 6, 10), (nx, 1, 2, None, 8), (ny, 0, 3, None, 9)):
                pass_on(*args, half=out_half, base=W_OUT_KINDS)

        w = gin_ref[pl.ds(pl.multiple_of(_tile(tiles_ref, p) * tn, tn), tn), :]
        proj_ref[...] = lax.dot_general(h_ref[...], w, _NT, preferred_element_type=F32)

        @pl.when(p == steps - 1)
        def _():
            cv(0, sibling, me).wait_recv()
            for j, peer in enumerate((nx, ny, dg)):
                cv(4 + j, other(peer), me).wait_recv()
            for d in range(N_DEV):
                conv_ref[:, d * SHARD_CONV:(d + 1) * SHARD_CONV] = gcv_ref[d]
            relayed = [rc(half(nx, 0), 5, ny), rc(half(ny, 1), 6, nx)]
            relayed += [rc(half(dev, h), k, sibling) for (dev, h), k in
                        (((nx, 0), 7), ((nx, 1), 8), ((ny, 0), 9), ((ny, 1), 10), ((dg, 0), 11), ((dg, 1), 12))]
            relayed += [cv(4 + j, peer, sibling) for j, peer in enumerate((nx, ny, dg))]
            for cp in own_copies() + relayed:
                cp.wait_send()
            pltpu.make_async_copy(gin_ref, wt_ref, local_sem).wait()
            pass_on(dg, 0, 5, None, 11, half=out_half, base=W_OUT_KINDS)
            pass_on(dg, 1, 6, None, 12, half=out_half, base=W_OUT_KINDS)
            rc(gout_ref.at[_slot(*sibling)], W_OUT_KINDS, me).wait_recv()
            out_relayed = [rc(out_half(nx, 0), W_OUT_KINDS + 5, ny), rc(out_half(ny, 1), W_OUT_KINDS + 6, nx)]
            for (dev, h), k in (((nx, 0), 7), ((nx, 1), 8), ((ny, 0), 9), ((ny, 1), 10), ((dg, 0), 11), ((dg, 1), 12)):
                rc(out_half(other(dev), h), W_OUT_KINDS + k, me).wait_recv()
                out_relayed.append(rc(out_half(dev, h), W_OUT_KINDS + k, sibling))
            for cp in own_out_copies() + out_relayed:
                cp.wait_send()
            own_out_local().wait()

    vmem = pl.BlockSpec(memory_space=pltpu.VMEM)
    grid_spec = pltpu.PrefetchScalarGridSpec(
        num_scalar_prefetch=1, grid=(steps,),
        in_specs=[pl.BlockSpec(memory_space=pl.ANY), vmem, vmem, vmem, vmem],
        out_specs=(pl.BlockSpec(memory_space=pl.ANY), vmem,
                   pl.BlockSpec((SEQ, tn), lambda p, tiles_ref: (0, _tile(tiles_ref, p))),
                   pl.BlockSpec(memory_space=pl.ANY), vmem),
        scratch_shapes=[pltpu.VMEM((D_PROJ, D_MODEL), BF16), pltpu.VMEM((N_DEV, 8, SHARD_CONV), F32),
                        pltpu.VMEM((SHARD_OUT, D_MODEL), BF16), pltpu.VMEM((SEQ, D_MODEL), F32),
                        pltpu.SemaphoreType.DMA((W_OUT_KINDS + N_GATHER_KINDS,)),
                        pltpu.SemaphoreType.DMA((W_OUT_KINDS + N_GATHER_KINDS,)),
                        pltpu.SemaphoreType.DMA((3,))])
    return pl.pallas_call(
        body, name="gather_in_proj", grid_spec=grid_spec,
        out_shape=(jax.ShapeDtypeStruct((D_PROJ, D_MODEL), BF16), jax.ShapeDtypeStruct((SEQ, D_MODEL), BF16),
                   jax.ShapeDtypeStruct((SEQ, D_PROJ), F32), jax.ShapeDtypeStruct((N_DEV, SHARD_OUT, D_MODEL), BF16),
                   jax.ShapeDtypeStruct((8, D_CONV), F32)),
        compiler_params=_params(dimension_semantics=("arbitrary",)),
    )(tiles, x, norm_in, w_in_sh, w_out_sh, conv_sh)


def _shard_sum(src, own, d2d, ici, send_sems, recv_sems, local_sems, base=0):
    x, y, c = lax.axis_index("x"), lax.axis_index("y"), lax.axis_index("c")
    sibling = (x, y, 1 - c)
    chips = [(x, y), (1 - x, y), (x, 1 - y), (1 - x, 1 - y)]

    def rcopy(s, d, k, to):
        return pltpu.make_async_remote_copy(src_ref=s, dst_ref=d, send_sem=send_sems.at[base + k],
                                            recv_sem=recv_sems.at[base + k], device_id=to, device_id_type=MESH)

    def mine(k):
        return pltpu.make_async_copy(src.at[_slot(*chips[k], c)], own.at[k], local_sems.at[k])

    def to_sibling(k):
        return rcopy(src.at[_slot(*chips[k], 1 - c)], d2d.at[k], k, sibling)

    def to_chip(k):
        return rcopy(own.at[k], ici.at[k - 1], 3 + k, (*chips[k], c))

    def start(k):
        mine(k).start()
        to_sibling(k).start()

    def forward(k):
        mine(k).wait()
        to_sibling(k).wait_recv()
        own[k] = (own[k].astype(F32) + d2d[k].astype(F32)).astype(BF16)
        to_chip(k).start()

    def finish():
        mine(0).wait()
        to_sibling(0).wait_recv()
        acc = own[0].astype(F32) + d2d[0].astype(F32)
        for k in range(1, 4):
            to_chip(k).wait_recv()
            acc = acc + ici[k - 1].astype(F32)
        for k in range(4):
            to_sibling(k).wait_send()
        for k in range(1, 4):
            to_chip(k).wait_send()
        return acc

    return start, forward, finish


def _shard_sum_scratch(rows):
    return [pltpu.VMEM((4, rows, D_MODEL), BF16), pltpu.VMEM((4, rows, D_MODEL), BF16),
            pltpu.VMEM((3, rows, D_MODEL), BF16)]


N_SHARD_SUM_SEMS = 7


N_CHIP_SUM_SEMS = 5


def _chip_sum(dwt, d2d, via, out_hbm, tiles_until, send_sems, recv_sems, local_sems, base, local_base):
    x, y, c = lax.axis_index("x"), lax.axis_index("y"), lax.axis_index("c")
    sibling, nx, ny = (x, y, 1 - c), (1 - x, y, c), (x, 1 - y, c)
    chips = [(x, y), (1 - x, y), (x, 1 - y), (1 - x, 1 - y)]

    def shard(s):
        return dwt.at[pl.ds(pl.multiple_of(s * SHARD_IN, 16), SHARD_IN), :]

    def half(ref, h):
        return ref.at[pl.ds(h * HALF_IN, HALF_IN), :]

    def rc(s, d, k, to):
        return pltpu.make_async_remote_copy(src_ref=s, dst_ref=d, send_sem=send_sems.at[base + k],
                                            recv_sem=recv_sems.at[base + k], device_id=to, device_id_type=MESH)

    def to_sibling(k):
        return rc(shard(_slot(*chips[k], 1 - c)), d2d.at[k - 1], k - 1, sibling)

    for_dg = (lambda: rc(half(d2d.at[DG - 1], 0), via.at[0], 3, nx), lambda: rc(half(d2d.at[DG - 1], 1), via.at[1], 4, ny))

    def save(k):
        return pltpu.make_async_copy(d2d.at[k - 1], out_hbm.at[k], local_sems.at[local_base + k])

    own_saves = (lambda: pltpu.make_async_copy(shard(_slot(x, y, c)), out_hbm.at[OWN], local_sems.at[local_base]),
                 lambda: pltpu.make_async_copy(shard(_slot(x, y, 1 - c)), out_hbm.at[3], local_sems.at[local_base + 3]))

    def before_tile(n):
        for k in (NX, NY, DG):
            @pl.when(tiles_until(k) == n)
            def _():
                to_sibling(k).start()

            @pl.when(tiles_until(k) + 1 == n)
            def _():
                to_sibling(k).wait_recv()
                d2d[k - 1] = (shard(_slot(*chips[k], c))[...].astype(F32) + d2d[k - 1].astype(F32)).astype(BF16)
                if k == DG:
                    for cp in for_dg:
                        cp().start()

    def after_tiles():
        for cp in own_saves:
            cp().start()

    def finish():
        for k, h in ((NY, 0), (NX, 1)):
            for_dg[h]().wait_recv()
            rows = pl.ds(h * HALF_IN, HALF_IN)
            d2d[k - 1, rows, :] = (d2d[k - 1, rows, :].astype(F32) + via[h].astype(F32)).astype(BF16)
            save(k).start()
        for cp in own_saves + (lambda: save(NX), lambda: save(NY)):
            cp().wait()
        for cp in (lambda: to_sibling(NX), lambda: to_sibling(NY), lambda: to_sibling(DG)) + for_dg:
            cp().wait_send()

    return before_tile, after_tiles, finish


N_ICI_SUM_SEMS = 3


def _ici_sum(src, own, d2d, ici, send_sems, recv_sems, local_sems, base=0):
    x, y, c = lax.axis_index("x"), lax.axis_index("y"), lax.axis_index("c")

    def rc(s, d, k, to):
        return pltpu.make_async_remote_copy(src_ref=s, dst_ref=d, send_sem=send_sems.at[base + k],
                                            recv_sem=recv_sems.at[base + k], device_id=to, device_id_type=MESH)

    copies = (lambda: rc(src.at[NX], ici.at[0], 0, (1 - x, y, c)), lambda: rc(src.at[NY], ici.at[1], 1, (x, 1 - y, c)),
              lambda: rc(src.at[3], d2d, 2, (x, y, 1 - c)))
    mine = lambda: pltpu.make_async_copy(src.at[OWN], own, local_sems.at[0])

    def start():
        for cp in copies + (mine,):
            cp().start()

    def finish():
        mine().wait()
        for cp in copies:
            cp().wait_recv()
        acc = own[...].astype(F32) + d2d[...].astype(F32) + ici[0].astype(F32) + ici[1].astype(F32)
        for cp in copies:
            cp().wait_send()
        return acc

    return start, finish


def _slab_sum(myslab, slabs, send_sems, recv_sems, base):
    x, y, c = lax.axis_index("x"), lax.axis_index("y"), lax.axis_index("c")
    me = _slot(x, y, c)
    peers = [(x, y, 1 - c), (1 - x, y, c), (x, 1 - y, c), (1 - x, 1 - y, c),
             (1 - x, y, 1 - c), (x, 1 - y, 1 - c), (1 - x, 1 - y, 1 - c)]

    def cp(k):
        return pltpu.make_async_remote_copy(src_ref=myslab, dst_ref=slabs.at[me], send_sem=send_sems.at[base + k],
                                            recv_sem=recv_sems.at[base + k], device_id=peers[k], device_id_type=MESH)

    def start():
        slabs[me] = myslab[...]
        for k in range(7):
            cp(k).start()

    def finish():
        for k in range(7):
            cp(k).wait_recv()
        total = slabs[0]
        for d in range(1, N_DEV):
            total = total + slabs[d]
        for k in range(7):
            cp(k).wait_send()
        return total

    return start, finish


def _chunk_rows(r):
    return slice(r * CHUNK, (r + 1) * CHUNK)


def _conv_halo(cch_ref, cuh_ref, n):
    zh = jnp.where(n > 0, cch_ref[...] * cuh_ref[...], 0.0)
    return jnp.concatenate([zh] * (CHUNK // HALO), axis=0)


def _conv_chunk(pj_ref, zhalo, cw, r):
    rows = _chunk_rows(r)
    cc = pj_ref[rows, OFF_CC:OFF_CC + D_CONV]
    cu = pj_ref[rows, OFF_CU:OFF_CU + D_CONV]
    z = cc * cu
    before = _chunk_rows(r - 1)
    zprev = pj_ref[before, OFF_CC:OFF_CC + D_CONV] * pj_ref[before, OFF_CU:OFF_CU + D_CONV] if r > 0 else zhalo
    row = lax.broadcasted_iota(jnp.int32, (CHUNK, D_CONV), 0)
    z1 = jnp.where(row < 1, pltpu.roll(zprev, 1, 0), pltpu.roll(z, 1, 0))
    z2 = jnp.where(row < 2, pltpu.roll(zprev, 2, 0), pltpu.roll(z, 2, 0))
    co = cw[0] * z2 + cw[1] * z1 + cw[2] * z
    return cc, cu, z, z1, z2, co


def _gated_norm(a, gain, t):
    r = lax.rsqrt(jnp.mean(a * a, axis=-1, keepdims=True) + RMS_EPS)
    return a * r * gain * (t * _sigmoid(t))


def _kv_bands(pj, kvp_ref):
    lane = lax.broadcasted_iota(jnp.int32, (2 * BLOCK, D_KV), 1)
    lo = lane < HEAD_DIM

    def bands(prev, cur):
        b = jnp.concatenate([prev, cur], axis=0)
        br = pltpu.roll(b, HEAD_DIM, 1)
        zero = jnp.zeros_like(b)
        return ((jnp.where(lo, b, zero).astype(BF16), jnp.where(lo, zero, br).astype(BF16)),
                (jnp.where(lo, br, zero).astype(BF16), jnp.where(lo, zero, b).astype(BF16)))

    ks = bands(kvp_ref[:, 0:D_KV], pj[:, OFF_K:OFF_K + D_KV])
    vs = bands(kvp_ref[:, D_KV:2 * D_KV], pj[:, OFF_V:OFF_V + D_KV])
    return ks, vs


STACK = PAIRS_PER_KV * BLOCK


def _head(j, i, e):
    return 2 * (PAIRS_PER_KV * j + i) + e


def _pair_cols(j, i, off):
    p = PAIRS_PER_KV * j + i
    return slice(off + 128 * p, off + 128 * (p + 1))


def _fill_attn_bias(bias_scr, first_block):
    qi = lax.broadcasted_iota(jnp.int32, (BLOCK, 2 * BLOCK), 0)
    kj = lax.broadcasted_iota(jnp.int32, (BLOCK, 2 * BLOCK), 1)
    dist = BLOCK + qi - kj
    valid = (dist >= 0) & (dist < BLOCK)
    if first_block:
        valid = valid & (kj >= BLOCK)
    distf = dist.astype(F32)
    for j in range(2):
        for e in range(2):
            for i in range(PAIRS_PER_KV):
                bias_scr[2 * j + e, BLOCK * i:BLOCK * (i + 1), :] = jnp.where(valid, -SLOPES[_head(j, i, e)] * distf, NEG)


def _q_stack(pj, j):
    return jnp.concatenate([(pj[:, _pair_cols(j, i, OFF_Q)] * SCALE).astype(BF16) for i in range(PAIRS_PER_KV)], axis=0)


def _attn_probs(q_stack, kband, bias_ref, sinks):
    s = lax.dot_general(q_stack, kband, _NT, preferred_element_type=F32)
    ones = jnp.ones((128, 128), BF16)
    probs, shares = [], []
    for i, sink in enumerate(sinks):
        rows = slice(BLOCK * i, BLOCK * (i + 1))
        t = s[rows, :] + bias_ref[rows, :]
        m = jnp.broadcast_to(jnp.max(t, axis=-1, keepdims=True), (BLOCK, 128))
        m = jnp.maximum(m, sink)
        p = [jnp.exp(t[:, :128] - m), jnp.exp(t[:, 128:] - m)]
        es = jnp.exp(sink - m)
        total = (jnp.dot(p[0].astype(BF16), ones, preferred_element_type=F32)
                 + jnp.dot(p[1].astype(BF16), ones, preferred_element_type=F32))
        inv = 1.0 / (total + es)
        probs.append(jnp.concatenate([p[0] * inv, p[1] * inv], axis=1))
        shares.append(es * inv)
    return jnp.concatenate(probs, axis=0), jnp.concatenate(shares, axis=0)


def _attn_group(pj, ks, vs, bias_scr, sink_ref, j):
    q_stack = _q_stack(pj, j)
    out, probs, shares = None, [], []
    for e in range(2):
        p, ps = _attn_probs(q_stack, ks[j][e], bias_scr.at[2 * j + e],
                            [sink_ref[_head(j, i, e)] for i in range(PAIRS_PER_KV)])
        p = p.astype(BF16)
        o = jnp.dot(p, vs[j][e], preferred_element_type=F32)
        out = o if out is None else out + o
        probs.append(p)
        shares.append(ps)
    return out, probs, shares


def _mix_fwd(proj, conv_full, sinks, norm_conv, norm_attn):
    ring = 3

    def body(pj_hbm, kvp_ref, cch_ref, cuh_ref, cw_ref, sink_ref, gc_ref, ga_ref,
             mixed_ref, attn_scr, p_ref, ps_ref, bias_scr, pj_ring, pj_sems):
        n = pl.program_id(0)

        def fetch(b):
            slot = lax.rem(b, ring)
            return pltpu.make_async_copy(pj_hbm.at[pl.ds(pl.multiple_of(b * BLOCK, BLOCK), BLOCK), :], pj_ring.at[slot],
                                         pj_sems.at[slot])

        @pl.when(n == 0)
        def _():
            for b in range(ring - 1):
                fetch(b).start()

        @pl.when(n + ring - 1 < N_BLOCKS)
        def _():
            fetch(n + ring - 1).start()

        fetch(n).wait()
        pj_ref = pj_ring.at[lax.rem(n, ring)]
        pj = pj_ref

        @pl.when(n == 0)
        def _():
            _fill_attn_bias(bias_scr, first_block=True)

        @pl.when(n == 1)
        def _():
            _fill_attn_bias(bias_scr, first_block=False)

        zhalo = _conv_halo(cch_ref, cuh_ref, n)
        cw = (cw_ref[0:1, :], cw_ref[1:2, :], cw_ref[2:3, :])
        gain_c = gc_ref[...]

        for r in range(N_CHUNKS):
            rows = _chunk_rows(r)
            co = _conv_chunk(pj_ref, zhalo, cw, r)[-1]
            y = _gated_norm(pj_ref[rows, OFF_CB:OFF_CB + D_CONV] * co, gain_c, pj_ref[rows, OFF_GC:OFF_GC + D_CONV])
            mixed_ref[rows, 0:D_CONV] = y.astype(BF16)

        ks, vs = _kv_bands(pj, kvp_ref)
        for j in range(2):
            out, probs, shares = _attn_group(pj, ks, vs, bias_scr, sink_ref, j)
            for e in range(2):
                p_ref[0, 2 * j + e] = probs[e]
                ps_ref[0, 2 * j + e] = shares[e]
            for i in range(PAIRS_PER_KV):
                attn_scr[:, _pair_cols(j, i, 0)] = out[BLOCK * i:BLOCK * (i + 1), :]
        gain_a = ga_ref[...]

        for r in range(N_CHUNKS):
            rows = _chunk_rows(r)
            y = _gated_norm(attn_scr[rows, :], gain_a, pj_ref[rows, OFF_GA:OFF_GA + D_ATTN])
            mixed_ref[rows, D_CONV:D_MIX] = y.astype(BF16)

    per_block = BLOCK // HALO
    return pl.pallas_call(
        body, name="mix_fwd", grid=(N_BLOCKS,),
        in_specs=[
            pl.BlockSpec(memory_space=pl.ANY),
            pl.BlockSpec((BLOCK, 2 * D_KV), lambda n: (jnp.maximum(n - 1, 0), OFF_K // (2 * D_KV))),
            pl.BlockSpec((HALO, D_CONV), lambda n: (jnp.maximum(n * per_block - 1, 0), OFF_CC // D_CONV)),
            pl.BlockSpec((HALO, D_CONV), lambda n: (jnp.maximum(n * per_block - 1, 0), OFF_CU // D_CONV)),
            pl.BlockSpec((8, D_CONV), lambda n: (0, 0)),
            pl.BlockSpec(memory_space=pltpu.SMEM),
            pl.BlockSpec((1, D_CONV), lambda n: (0, 0)),
            pl.BlockSpec((1, D_ATTN), lambda n: (0, 0)),
        ],
        out_specs=(pl.BlockSpec((BLOCK, D_MIX), lambda n: (n, 0)), pl.BlockSpec((BLOCK, D_ATTN), lambda n: (n, 0)),
                   pl.BlockSpec((1, 4, STACK, 2 * BLOCK), lambda n: (n, 0, 0, 0)),
                   pl.BlockSpec((1, 4, STACK, 128), lambda n: (n, 0, 0, 0))),
        out_shape=(jax.ShapeDtypeStruct((SEQ, D_MIX), BF16), jax.ShapeDtypeStruct((SEQ, D_ATTN), F32),
                   jax.ShapeDtypeStruct((N_BLOCKS, 4, STACK, 2 * BLOCK), BF16),
                   jax.ShapeDtypeStruct((N_BLOCKS, 4, STACK, 128), F32)),
        scratch_shapes=[pltpu.VMEM((4, STACK, 2 * BLOCK), F32), pltpu.VMEM((ring, BLOCK, D_PROJ), F32),
                        pltpu.SemaphoreType.DMA((ring,))],
        compiler_params=_params(dimension_semantics=("arbitrary",)),
    )(proj, proj, proj, proj, conv_full, sinks, norm_conv, norm_attn)


def _out_proj_loss(mixed, x, target, w_out_full, norm_final):
    tm = 256

    def body(mx_ref, x_ref, t_ref, w_ref, g_ref, dx2_ref, dx2b_ref, dmix_ref, gnf_ref, loss_ref):
        i = pl.program_id(0)
        w = w_ref[...]
        x2 = x_ref[...] + jnp.dot(mx_ref[...], w, preferred_element_type=F32)
        r = lax.rsqrt(jnp.mean(x2 * x2, axis=-1, keepdims=True) + RMS_EPS)
        xn = x2 * r
        g = g_ref[...]
        err = xn * g - t_ref[...]
        part = 0.5 * jnp.sum(jnp.mean(err * err, axis=-1, keepdims=True), axis=0, keepdims=True)
        dy = err * (1.0 / D_MODEL)
        gnf = jnp.sum(dy * xn, axis=0, keepdims=True)
        u = dy * g
        dx2 = r * (u - xn * jnp.mean(u * xn, axis=-1, keepdims=True))
        dx2_ref[...] = dx2
        dx2b = dx2.astype(BF16)
        dx2b_ref[...] = dx2b
        dmix_ref[...] = lax.dot_general(dx2b, w, _NT, preferred_element_type=F32)

        @pl.when(i == 0)
        def _():
            gnf_ref[...] = jnp.zeros_like(gnf_ref)
            loss_ref[...] = jnp.zeros_like(loss_ref)

        gnf_ref[...] += gnf
        loss_ref[...] += jnp.broadcast_to(part, loss_ref.shape)

    return pl.pallas_call(
        body, name="out_proj_loss", grid=(SEQ // tm,),
        in_specs=[pl.BlockSpec((tm, D_MIX), lambda i: (i, 0)), pl.BlockSpec((tm, D_MODEL), lambda i: (i, 0)),
                  pl.BlockSpec((tm, D_MODEL), lambda i: (i, 0)), pl.BlockSpec(memory_space=pltpu.VMEM),
                  pl.BlockSpec((1, D_MODEL), lambda i: (0, 0))],
        out_specs=(pl.BlockSpec((tm, D_MODEL), lambda i: (i, 0)), pl.BlockSpec((tm, D_MODEL), lambda i: (i, 0)),
                   pl.BlockSpec((tm, D_MIX), lambda i: (i, 0)),
                   pl.BlockSpec((1, D_MODEL), lambda i: (0, 0)), pl.BlockSpec((8, 128), lambda i: (0, 0))),
        out_shape=(jax.ShapeDtypeStruct((SEQ, D_MODEL), F32), jax.ShapeDtypeStruct((SEQ, D_MODEL), BF16),
                   jax.ShapeDtypeStruct((SEQ, D_MIX), F32),
                   jax.ShapeDtypeStruct((1, D_MODEL), F32), jax.ShapeDtypeStruct((8, 128), F32)),
        compiler_params=_params(dimension_semantics=("arbitrary",)),
    )(mixed, x, target, w_out_full, norm_final)


def _gated_norm_bwd(a, gain, t, dy):
    r = lax.rsqrt(jnp.mean(a * a, axis=-1, keepdims=True) + RMS_EPS)
    an = a * r
    sg = _sigmoid(t)
    dn = dy * (t * sg)
    dt = dy * (an * gain) * (sg * (1.0 + t * (1.0 - sg)))
    u = dn * gain
    da = r * (u - an * jnp.mean(u * an, axis=-1, keepdims=True))
    return da, dt, dn * an


def _mix_bwd(proj, dmixed, attn, probs, shares, conv_full, norm_conv, norm_attn):
    ring = 3

    def body(pj_hbm, kvp_ref, cch_ref, cuh_ref, dmx_ref, attn_ref, p_ref, ps_ref, cw_ref, gc_ref, ga_ref,
             dpj_ref, gslab_ref, dattn_scr, nxt_scr, dkv_scr, acc_scr, pj_ring, pj_sems):
        step = pl.program_id(0)
        n = N_BLOCKS - 1 - step

        def fetch(s):
            slot = lax.rem(s, ring)
            rows = pl.ds(pl.multiple_of((N_BLOCKS - 1 - s) * BLOCK, BLOCK), BLOCK)
            return pltpu.make_async_copy(pj_hbm.at[rows, :], pj_ring.at[slot], pj_sems.at[slot])

        @pl.when(step == 0)
        def _():
            for s in range(ring - 1):
                fetch(s).start()

        @pl.when(step + ring - 1 < N_BLOCKS)
        def _():
            fetch(step + ring - 1).start()

        fetch(step).wait()
        pj_ref = pj_ring.at[lax.rem(step, ring)]
        pj = pj_ref

        @pl.when(step == 0)
        def _():
            gslab_ref[...] = jnp.zeros_like(gslab_ref)
            nxt_scr[...] = jnp.zeros_like(nxt_scr)
            dkv_scr[...] = jnp.zeros_like(dkv_scr)
            acc_scr[...] = jnp.zeros_like(acc_scr)

        zhalo = _conv_halo(cch_ref, cuh_ref, n)
        cw = (cw_ref[0:1, :], cw_ref[1:2, :], cw_ref[2:3, :])
        gain_c = gc_ref[...]
        row = lax.broadcasted_iota(jnp.int32, (CHUNK, D_CONV), 0)

        dco_after = nxt_scr[...]
        for r in reversed(range(N_CHUNKS)):
            rows = _chunk_rows(r)
            cc, cu, z, z1, z2, co = _conv_chunk(pj_ref, zhalo, cw, r)
            cb = pj_ref[rows, OFF_CB:OFF_CB + D_CONV]
            da, dgate, gterm = _gated_norm_bwd(cb * co, gain_c, pj_ref[rows, OFF_GC:OFF_GC + D_CONV],
                                               dmx_ref[rows, 0:D_CONV])
            dpj_ref[rows, OFF_GC:OFF_GC + D_CONV] = dgate.astype(BF16)
            dpj_ref[rows, OFF_CB:OFF_CB + D_CONV] = (da * co).astype(BF16)
            dco = da * cb
            dco1 = jnp.where(row >= CHUNK - 1, pltpu.roll(dco_after, CHUNK - 1, 0), pltpu.roll(dco, CHUNK - 1, 0))
            dco2 = jnp.where(row >= CHUNK - 2, pltpu.roll(dco_after, CHUNK - 2, 0), pltpu.roll(dco, CHUNK - 2, 0))
            dz = cw[2] * dco + cw[1] * dco1 + cw[0] * dco2
            dpj_ref[rows, OFF_CC:OFF_CC + D_CONV] = (dz * cu).astype(BF16)
            dpj_ref[rows, OFF_CU:OFF_CU + D_CONV] = (dz * cc).astype(BF16)
            acc_scr[ACC_NORM_CONV] += gterm
            acc_scr[ACC_CONV0] += dco * z2
            acc_scr[ACC_CONV0 + 1] += dco * z1
            acc_scr[ACC_CONV0 + 2] += dco * z
            dco_after = dco
        nxt_scr[...] = dco_after

        ks, vs = _kv_bands(pj, kvp_ref)
        gain_a = ga_ref[...]

        for r in range(N_CHUNKS):
            rows = _chunk_rows(r)
            da, dgate, gterm = _gated_norm_bwd(attn_ref[rows, :], gain_a, pj_ref[rows, OFF_GA:OFF_GA + D_ATTN],
                                               dmx_ref[rows, D_CONV:D_MIX])
            dpj_ref[rows, OFF_GA:OFF_GA + D_ATTN] = dgate.astype(BF16)
            dattn_scr[rows, :] = da
            acc_scr[ACC_NORM_ATTN] += gterm

        in_lo = lax.broadcasted_iota(jnp.int32, (128, 128), 0) < HEAD_DIM
        half_ones = (jnp.where(in_lo, 1.0, 0.0).astype(BF16), jnp.where(in_lo, 0.0, 1.0).astype(BF16))
        lane_s = lax.broadcasted_iota(jnp.int32, (1, D_MODEL), 1)
        gsink = jnp.zeros((1, D_MODEL), F32)
        dk_t, dv_t = [], []
        for j in range(2):
            q_stack = _q_stack(pj, j)
            do_f = jnp.concatenate([dattn_scr[:, _pair_cols(j, i, 0)] for i in range(PAIRS_PER_KV)], axis=0)
            o_f = jnp.concatenate([attn_ref[:, _pair_cols(j, i, 0)] for i in range(PAIRS_PER_KV)], axis=0)
            prod = (do_f * o_f).astype(BF16)
            deltas = [jnp.dot(prod, half_ones[e], preferred_element_type=F32) for e in range(2)]
            do_b = do_f.astype(BF16)
            q_t, do_t = q_stack.T, do_b.T
            dq, dk_j, dv_j = None, None, None
            for e in range(2):
                p = p_ref[0, 2 * j + e]
                dp = lax.dot_general(do_b, vs[j][e], _NT, preferred_element_type=F32)
                ds = []
                for i in range(PAIRS_PER_KV):
                    rows = slice(BLOCK * i, BLOCK * (i + 1))
                    delta = deltas[e][rows, :]
                    ds.append((p[rows, :].astype(F32) * (dp[rows, :] - jnp.concatenate([delta, delta], axis=1))).astype(BF16))
                    gs_h = -jnp.sum(ps_ref[0, 2 * j + e, rows, 0:1] * delta[:, 0:1], axis=0, keepdims=True)
                    gsink = gsink + jnp.where(lane_s == _head(j, i, e), gs_h, 0.0)
                ds = jnp.concatenate(ds, axis=0)
                t = jnp.dot(ds, ks[j][e], preferred_element_type=F32)
                dq = t if dq is None else dq + t
                half = slice(HEAD_DIM * e, HEAD_DIM * (e + 1))
                a = jnp.dot(q_t[half, :], ds, preferred_element_type=F32)
                b = jnp.dot(do_t[half, :], p, preferred_element_type=F32)
                dk_j = a if dk_j is None else dk_j + a
                dv_j = b if dv_j is None else dv_j + b
            for i in range(PAIRS_PER_KV):
                dpj_ref[:, _pair_cols(j, i, OFF_Q)] = (dq[BLOCK * i:BLOCK * (i + 1), :] * SCALE).astype(BF16)
            dk_t.append(dk_j)
            dv_t.append(dv_j)
        dk = jnp.concatenate(dk_t, axis=0).T
        dv = jnp.concatenate(dv_t, axis=0).T
        dpj_ref[:, OFF_K:OFF_K + D_KV] = (dk[BLOCK:, :] + dkv_scr[:, 0:D_KV]).astype(BF16)
        dpj_ref[:, OFF_V:OFF_V + D_KV] = (dv[BLOCK:, :] + dkv_scr[:, D_KV:2 * D_KV]).astype(BF16)
        dkv_scr[:, 0:D_KV] = dk[:BLOCK, :]
        dkv_scr[:, D_KV:2 * D_KV] = dv[:BLOCK, :]
        gslab_ref[ROW_SINKS:ROW_SINKS + 1, :] += gsink

        @pl.when(step == N_BLOCKS - 1)
        def _():
            for k, slab_row in ((ACC_NORM_CONV, ROW_NORM_CONV), (ACC_NORM_ATTN, ROW_NORM_ATTN), (ACC_CONV0, ROW_CONV0),
                                (ACC_CONV0 + 1, ROW_CONV0 + 1), (ACC_CONV0 + 2, ROW_CONV0 + 2)):
                gslab_ref[slab_row:slab_row + 1, :] = jnp.sum(acc_scr[k], axis=0, keepdims=True)

    per_block = BLOCK // HALO
    last = N_BLOCKS - 1
    return pl.pallas_call(
        body, name="mix_bwd", grid=(N_BLOCKS,),
        in_specs=[
            pl.BlockSpec(memory_space=pl.ANY),
            pl.BlockSpec((BLOCK, 2 * D_KV), lambda s: (jnp.maximum(last - s - 1, 0), OFF_K // (2 * D_KV))),
            pl.BlockSpec((HALO, D_CONV), lambda s: (jnp.maximum((last - s) * per_block - 1, 0), OFF_CC // D_CONV)),
            pl.BlockSpec((HALO, D_CONV), lambda s: (jnp.maximum((last - s) * per_block - 1, 0), OFF_CU // D_CONV)),
            pl.BlockSpec((BLOCK, D_MIX), lambda s: (last - s, 0)),
            pl.BlockSpec((BLOCK, D_ATTN), lambda s: (last - s, 0)),
            pl.BlockSpec((1, 4, STACK, 2 * BLOCK), lambda s: (last - s, 0, 0, 0)),
            pl.BlockSpec((1, 4, STACK, 128), lambda s: (last - s, 0, 0, 0)),
            pl.BlockSpec((8, D_CONV), lambda s: (0, 0)),
            pl.BlockSpec((1, D_CONV), lambda s: (0, 0)),
            pl.BlockSpec((1, D_ATTN), lambda s: (0, 0)),
        ],
        out_specs=(pl.BlockSpec((BLOCK, D_PROJ), lambda s: (last - s, 0)),
                   pl.BlockSpec((8, D_MODEL), lambda s: (0, 0))),
        out_shape=(jax.ShapeDtypeStruct((SEQ, D_PROJ), BF16), jax.ShapeDtypeStruct((8, D_MODEL), F32)),
        scratch_shapes=[pltpu.VMEM((BLOCK, D_ATTN), F32), pltpu.VMEM((CHUNK, D_CONV), F32),
                        pltpu.VMEM((BLOCK, 2 * D_KV), F32), pltpu.VMEM((N_ACC, CHUNK, D_MODEL), F32),
                        pltpu.VMEM((ring, BLOCK, D_PROJ), F32), pltpu.SemaphoreType.DMA((ring,))],
        compiler_params=_params(dimension_semantics=("arbitrary",)),
    )(proj, proj, proj, proj, dmixed, attn, probs, shares, conv_full, norm_conv, norm_attn)


def _in_bwd_rs(dproj, w_full, x, dx2, norm_in, dw_in_chip, gslab, gnf, loss_part):
    tm = 256
    steps = SEQ // tm

    def body(dp_ref, w_hbm, x_ref, dx2_ref, g_ref, dwi_ref, gs_ref, gnf_ref, lp_ref, gx_ref, gwin_ref, gsum_ref,
             gni_scr, own, d2d, ici, myslab, slabs, w_ref, send_sems, recv_sems, local_sems):
        i = pl.program_id(0)
        rs_start, rs_finish = _ici_sum(dwi_ref, own, d2d, ici, send_sems, recv_sems, local_sems)
        slab_start, slab_finish = _slab_sum(myslab, slabs, send_sems, recv_sems, N_ICI_SUM_SEMS)

        @pl.when(i == 0)
        def _():
            gni_scr[...] = jnp.zeros_like(gni_scr)
            rs_start()
            w_load = pltpu.make_async_copy(w_hbm, w_ref, local_sems.at[1])
            w_load.start()
            w_load.wait()

        dh = jnp.dot(dp_ref[...], w_ref[...], preferred_element_type=F32)
        xv = x_ref[...]
        r = lax.rsqrt(jnp.mean(xv * xv, axis=-1, keepdims=True) + RMS_EPS)
        xn = xv * r
        u = dh * g_ref[...]
        gx_ref[...] = dx2_ref[...] + r * (u - xn * jnp.mean(u * xn, axis=-1, keepdims=True))
        gni_scr[...] += jnp.sum(dh * xn, axis=0, keepdims=True)

        @pl.when(i == steps - 1)
        def _():
            row = lax.broadcasted_iota(jnp.int32, (8, D_MODEL), 0)
            lane = lax.broadcasted_iota(jnp.int32, (8, D_MODEL), 1)
            slab = jnp.where(row == ROW_NORM_IN, gni_scr[...], jnp.where(row == ROW_NORM_FINAL, gnf_ref[...], gs_ref[...]))
            myslab[...] = jnp.where((row == ROW_SINKS) & (lane == LOSS_LANE), lp_ref[0:1, 0:1], slab)
            slab_start()
            gwin_ref[...] = rs_finish()
            gsum_ref[...] = slab_finish()

    const = lambda i: (0, 0)
    return pl.pallas_call(
        body, name="in_bwd", grid=(steps,),
        in_specs=[pl.BlockSpec((tm, D_PROJ), lambda i: (i, 0)), pl.BlockSpec(memory_space=pl.ANY),
                  pl.BlockSpec((tm, D_MODEL), lambda i: (i, 0)), pl.BlockSpec((tm, D_MODEL), lambda i: (i, 0)),
                  pl.BlockSpec((1, D_MODEL), const), pl.BlockSpec(memory_space=pl.ANY),
                  pl.BlockSpec((8, D_MODEL), const), pl.BlockSpec((1, D_MODEL), const), pl.BlockSpec((8, 128), const)],
        out_specs=(pl.BlockSpec((tm, D_MODEL), lambda i: (i, 0)), pl.BlockSpec((SHARD_IN, D_MODEL), const),
                   pl.BlockSpec((8, D_MODEL), const)),
        out_shape=(jax.ShapeDtypeStruct((SEQ, D_MODEL), F32), jax.ShapeDtypeStruct((SHARD_IN, D_MODEL), F32),
                   jax.ShapeDtypeStruct((8, D_MODEL), F32)),
        scratch_shapes=[pltpu.VMEM((1, D_MODEL), F32), pltpu.VMEM((SHARD_IN, D_MODEL), BF16),
                        pltpu.VMEM((SHARD_IN, D_MODEL), BF16), pltpu.VMEM((2, SHARD_IN, D_MODEL), BF16),
                        pltpu.VMEM((8, D_MODEL), F32), pltpu.VMEM((N_DEV, 8, D_MODEL), F32),
                        pltpu.VMEM((D_PROJ, D_MODEL), BF16),
                        pltpu.SemaphoreType.DMA((N_ICI_SUM_SEMS + 7,)), pltpu.SemaphoreType.DMA((N_ICI_SUM_SEMS + 7,)),
                        pltpu.SemaphoreType.DMA((2,))],
        compiler_params=_params(dimension_semantics=("arbitrary",)),
    )(dproj, w_full, x, dx2, norm_in, dw_in_chip, gslab, gnf, loss_part)


def _dw_rs(mixed, dx2b, dproj, h, table):
    tn_out, tn = 2 * SHARD_OUT, IN_PROJ_TILE
    out_steps, in_steps = D_MIX // tn_out, D_PROJ // tn
    steps = out_steps + in_steps
    out_order = (DG, NX, NY, OWN)

    def out_tile(i):
        chip = 2 * lax.axis_index("x") + lax.axis_index("y")
        return jnp.bitwise_xor(chip, (out_steps - 1) - jnp.minimum(i, out_steps - 1))

    def in_tile(table_ref, i):
        return _dw_entry(table_ref, jnp.maximum(i - out_steps, 0))

    def body(table_ref, mx_ref, dxb_ref, a_ref, h_hbm, chip_ref, gwo_ref, dwo, dwt, d2d_in, via, own, d2d, ici, b_ref,
             send_sems, recv_sems, local_sems):
        i = pl.program_id(0)
        h_load = pltpu.make_async_copy(h_hbm, b_ref, local_sems.at[8])

        @pl.when(i == 0)
        def _():
            h_load.start()

        @pl.when(i == out_steps)
        def _():
            h_load.wait()

        rs_start, rs_forward, rs_finish = _shard_sum(dwo, own, d2d, ici, send_sems, recv_sems, local_sems)
        before_tile, after_tiles, chip_finish = _chip_sum(
            dwt, d2d_in, via, chip_ref, lambda k: _dw_entry(table_ref, in_steps + k), send_sems, recv_sems, local_sems,
            N_SHARD_SUM_SEMS, 4)

        for j, k in enumerate(out_order):
            @pl.when(i == j + 1)
            def _():
                rs_start(k)

            if k != OWN:
                @pl.when(i == j + 2)
                def _():
                    rs_forward(k)

        @pl.when(i < out_steps)
        def _():
            tile = lax.dot_general(mx_ref[...], dxb_ref[...], _TN, preferred_element_type=F32).astype(BF16)
            for core in range(2):
                dwo[2 * out_tile(i) + core] = tile[SHARD_OUT * core:SHARD_OUT * (core + 1), :]

        @pl.when(i >= out_steps)
        def _():
            before_tile(i - out_steps)
            tile = lax.dot_general(a_ref[...], b_ref[...], _TN, preferred_element_type=F32).astype(BF16)
            dwt[pl.ds(pl.multiple_of(in_tile(table_ref, i) * tn, tn), tn), :] = tile

        @pl.when(i == steps - 2)
        def _():
            gwo_ref[...] = rs_finish()

        @pl.when(i == steps - 1)
        def _():
            after_tiles()
            chip_finish()

    vmem = pl.BlockSpec(memory_space=pltpu.VMEM)
    grid_spec = pltpu.PrefetchScalarGridSpec(
        num_scalar_prefetch=1, grid=(steps,),
        in_specs=[pl.BlockSpec((SEQ, tn_out), lambda i, table_ref: (0, out_tile(i))), vmem,
                  pl.BlockSpec((SEQ, tn), lambda i, table_ref: (0, in_tile(table_ref, i))),
                  pl.BlockSpec(memory_space=pl.ANY)],
        out_specs=(pl.BlockSpec(memory_space=pl.ANY), pl.BlockSpec((SHARD_OUT, D_MODEL), lambda i, table_ref: (0, 0))),
        scratch_shapes=[pltpu.VMEM((N_DEV, SHARD_OUT, D_MODEL), BF16),
                        pltpu.VMEM((D_PROJ, D_MODEL), BF16), pltpu.VMEM((3, SHARD_IN, D_MODEL), BF16),
                        pltpu.VMEM((2, HALF_IN, D_MODEL), BF16),
                        *_shard_sum_scratch(SHARD_OUT), pltpu.VMEM((SEQ, D_MODEL), BF16),
                        pltpu.SemaphoreType.DMA((N_SHARD_SUM_SEMS + N_CHIP_SUM_SEMS,)),
                        pltpu.SemaphoreType.DMA((N_SHARD_SUM_SEMS + N_CHIP_SUM_SEMS,)),
                        pltpu.SemaphoreType.DMA((9,))])
    return pl.pallas_call(
        body, name="dw", grid_spec=grid_spec,
        out_shape=(jax.ShapeDtypeStruct((4, SHARD_IN, D_MODEL), BF16), jax.ShapeDtypeStruct((SHARD_OUT, D_MODEL), F32)),
        compiler_params=_params(dimension_semantics=("arbitrary",)),
    )(table, mixed, dx2b, dproj, h)


def _adam_all(big_in, big_out, gsum, small, grad_x):
    n_chunks = 4
    n_big = 8

    def body(*refs):
        ins, outs = refs[:n_big + 1 + 18 + 1], refs[n_big + 1 + 18 + 1:n_big + 1 + 18 + 1 + 34]
        in_bufs, out_bufs, gx_buf = refs[-n_big - 6 - 4:-6 - 4], refs[-6 - 4:-4], refs[-4]
        in_sems, out_sems, gx_sems = refs[-3:]

        def gx_rows(j):
            return pl.ds(j * (SEQ // n_chunks), SEQ // n_chunks)

        def gx_load(j):
            return pltpu.make_async_copy(ins[27].at[gx_rows(j), :], gx_buf.at[gx_rows(j), :], gx_sems.at[j])

        def gx_store(j):
            return pltpu.make_async_copy(gx_buf.at[gx_rows(j), :], outs[33].at[gx_rows(j), :], gx_sems.at[n_chunks + j])

        def rows(a, j):
            tr = ins[a].shape[0] // n_chunks
            return pl.ds(j * tr, tr)

        def load(a, j):
            return pltpu.make_async_copy(ins[a].at[rows(a, j), :], in_bufs[a].at[rows(a, j), :], in_sems.at[a * n_chunks + j])

        def store(a, j):
            b, kind = divmod(a, 4)
            src = in_bufs[4 * b + 1] if kind == 0 else out_bufs[3 * b + kind - 1]
            return pltpu.make_async_copy(src.at[rows(a, j), :], outs[a].at[rows(a, j), :], out_sems.at[a * n_chunks + j])

        for j in range(n_chunks):
            for a in range(n_big):
                load(a, j).start()
            gx_load(j).start()

        def small_weights():
            gsum = ins[8][...]
            idx = _slot(lax.axis_index("x"), lax.axis_index("y"), lax.axis_index("c"))
            cg = jnp.zeros((3, SHARD_CONV), F32)
            for d in range(N_DEV):
                cg = jnp.where(idx == d, gsum[ROW_CONV0:ROW_CONV0 + 3, d * SHARD_CONV:(d + 1) * SHARD_CONV], cg)
            grads = (gsum[ROW_NORM_IN:ROW_NORM_IN + 1], gsum[ROW_SINKS:ROW_SINKS + 1, 0:N_Q_HEADS],
                     gsum[ROW_NORM_CONV:ROW_NORM_CONV + 1], gsum[ROW_NORM_ATTN:ROW_NORM_ATTN + 1],
                     gsum[ROW_NORM_FINAL:ROW_NORM_FINAL + 1], cg)
            for s, g in enumerate(grads):
                at = (slice(None), 0, slice(None)) if s == 5 else (slice(None), slice(None))
                w_ref, m_ref, v_ref = ins[9 + 3 * s:12 + 3 * s]
                delta, mn, vn = _adamw(w_ref[at], g, m_ref[at], v_ref[at])
                for ref, val in zip(outs[8 + 4 * s:12 + 4 * s], (g, delta, mn, vn)):
                    ref[at] = val
            outs[32][...] = gsum[ROW_SINKS:ROW_SINKS + 1, LOSS_LANE:LOSS_LANE + 1]

        small_weights()
        for j in range(n_chunks):
            for b in range(2):
                for a in range(4 * b, 4 * b + 4):
                    load(a, j).wait()
                w_buf, g_buf, m_buf, v_buf = in_bufs[4 * b:4 * b + 4]
                r = rows(4 * b, j)
                results = _adamw(w_buf[r, :], g_buf[r, :], m_buf[r, :], v_buf[r, :])
                for buf, val in zip(out_bufs[3 * b:3 * b + 3], results):
                    buf[r, :] = val
                for a in range(4 * b, 4 * b + 4):
                    store(a, j).start()
            gx_load(j).wait()
            gx_store(j).start()
        for j in range(n_chunks):
            for a in range(n_big):
                store(a, j).wait()
            gx_store(j).wait()

    vmem, hbm = pl.BlockSpec(memory_space=pltpu.VMEM), pl.BlockSpec(memory_space=pl.ANY)
    small_shapes = [a.shape for a in small[::3]]
    big_shapes = [(SHARD_IN, D_MODEL)] * 4 + [(SHARD_OUT, D_MODEL)] * 4
    out_shape = ([jax.ShapeDtypeStruct(s, F32) for s in big_shapes]
                 + [jax.ShapeDtypeStruct(s, F32) for s in small_shapes for _ in range(4)]
                 + [jax.ShapeDtypeStruct((1, 1), F32), jax.ShapeDtypeStruct((SEQ, D_MODEL), F32)])
    outs = pl.pallas_call(
        body, name="adam", in_specs=[hbm] * n_big + [vmem] * (1 + len(small)) + [hbm],
        out_specs=tuple([hbm] * n_big + [vmem] * (4 * len(small_shapes) + 1) + [hbm]), out_shape=tuple(out_shape),
        scratch_shapes=[pltpu.VMEM(s, F32) for s in big_shapes]
                       + [pltpu.VMEM(s, F32) for s in [(SHARD_IN, D_MODEL)] * 3 + [(SHARD_OUT, D_MODEL)] * 3]
                       + [pltpu.VMEM((SEQ, D_MODEL), F32),
                          pltpu.SemaphoreType.DMA((n_big * n_chunks,)), pltpu.SemaphoreType.DMA((n_big * n_chunks,)),
                          pltpu.SemaphoreType.DMA((2 * n_chunks,))],
        compiler_params=_params(),
    )(*big_in, *big_out, gsum, *small, grad_x)
    return outs[0:4], outs[4:8], [outs[8 + 4 * s:12 + 4 * s] for s in range(6)], outs[32], outs[33]


def _rows_first(a):
    return jnp.transpose(a, (1, 0, 2))


def kernel(x, norm_in, w_in, conv_w, attn_sinks, norm_conv_out, norm_attn_out, w_out, norm_final, loss_target, m_norm_in, m_w_in, m_conv_w, m_attn_sinks, m_norm_conv_out, m_norm_attn_out, m_w_out, m_norm_final, v_norm_in, v_w_in, v_conv_w, v_attn_sinks, v_norm_conv_out, v_norm_attn_out, v_w_out, v_norm_final):
    x2d = x.reshape(SEQ, D_MODEL)
    target = loss_target.reshape(SEQ, D_MODEL)
    nf = norm_final.reshape(1, D_MODEL)

    w_in_t, m_w_in_t, v_w_in_t = w_in[0].T, m_w_in[0].T, v_w_in[0].T
    tiles = jnp.asarray(TILE_ORDER, jnp.int32).reshape(-1)
    w_in_full, h, proj, g_out, conv_full = _gather_in_proj(x2d, norm_in, w_in_t, w_out[0], _rows_first(conv_w), tiles)
    sinks = attn_sinks.reshape(N_Q_HEADS)

    mixed, attn, probs, shares = _mix_fwd(proj, conv_full, sinks, norm_conv_out, norm_attn_out)
    dx2, dx2b, dmixed, gnf, loss_part = _out_proj_loss(mixed, x2d, target, g_out.reshape(D_MIX, D_MODEL), nf)
    dproj, gslab = _mix_bwd(proj, dmixed, attn, probs, shares, conv_full, norm_conv_out, norm_attn_out)
    dw_in_chip, g_w_out = _dw_rs(mixed, dx2b, dproj, h, jnp.asarray(DW_TABLE, jnp.int32).reshape(-1))
    grad_x, g_w_in, gsum = _in_bwd_rs(dproj, w_in_full, x2d, dx2, norm_in, dw_in_chip, gslab, gnf, loss_part)

    small = (norm_in, m_norm_in, v_norm_in, attn_sinks, m_attn_sinks, v_attn_sinks,
             norm_conv_out, m_norm_conv_out, v_norm_conv_out, norm_attn_out, m_norm_attn_out, v_norm_attn_out,
             nf, m_norm_final.reshape(1, D_MODEL), v_norm_final.reshape(1, D_MODEL),
             _rows_first(conv_w), _rows_first(m_conv_w), _rows_first(v_conv_w))
    big_in, big_out, (s_ni, s_sk, s_nc, s_na, s_nf, s_cv), loss, grad_x = _adam_all(
        (w_in_t, g_w_in, m_w_in_t, v_w_in_t), (w_out[0], g_w_out, m_w_out[0], v_w_out[0]), gsum, small, grad_x)

    def leaves(k):
        return (s_ni[k], big_in[k].T[None], jnp.transpose(s_cv[k], (1, 0, 2)), s_sk[k], s_nc[k], s_na[k], big_out[k][None],
                s_nf[k].reshape(D_MODEL))

    return (loss.reshape(()), grad_x.reshape(1, SEQ, D_MODEL), *leaves(0), *leaves(1), *leaves(2), *leaves(3))
```
